```python
import jax, jax.numpy as jnp
from jax import lax
import numpy as np

D_MODEL = 1024
BATCH = 8
SEQ = 4096
DEPTH = 2

D_CONV = D_MODEL
D_SGU = D_MODEL
D_CFM = D_MODEL
N_SGU_GROUPS = 8
SGU_GROUP_DIM = D_SGU // N_SGU_GROUPS
CHUNK = 128
SHORT_K = 3
CFM_K = 31
N_BRANCH = 3
D_FF = -(-8 * D_MODEL // (3 * 256)) * 256
N_MOD = 6
EPS = 1e-6
SPLITS = [D_CONV, D_CONV, D_CONV, D_SGU, D_SGU, D_CFM, D_CFM, D_MODEL, D_MODEL, D_MODEL]
N_IN = sum(SPLITS)
SPLIT_IDX = list(np.cumsum(SPLITS)[:-1])

kernel_name = "hybrid_gated_conv_sgu_conformer_block"


def rms_norm(x, g):
    x32 = x.astype(jnp.float32)
    y = x32 * lax.rsqrt(jnp.mean(jnp.square(x32), axis=-1, keepdims=True) + EPS)
    return y.astype(x.dtype) * g


def layer_norm(x, g, b):
    x32 = x.astype(jnp.float32)
    mu = jnp.mean(x32, axis=-1, keepdims=True)
    var = jnp.mean(jnp.square(x32 - mu), axis=-1, keepdims=True)
    return ((x32 - mu) * lax.rsqrt(var + EPS)).astype(x.dtype) * g + b


def modulate(h, shift, scale):
    return h * (1 + scale[:, None, :]) + shift[:, None, :]


def causal_depthwise_conv(x, w):
    k, ch = w.shape
    return lax.conv_general_dilated(
        x, w[:, None, :].astype(x.dtype), window_strides=(1,), padding=[(k - 1, 0)],
        dimension_numbers=('NWC', 'WIO', 'NWC'), feature_group_count=ch)


def spatial_gating(u, v, ln_g, ln_b, w_s, b_s):
    bn, s, _ = v.shape
    v = layer_norm(v, ln_g, ln_b)
    v = v.reshape(bn, s // CHUNK, CHUNK, N_SGU_GROUPS, SGU_GROUP_DIM)
    mask = jnp.tril(jnp.ones((CHUNK, CHUNK), dtype=bool))
    w = jnp.where(mask[None], w_s, 0).astype(v.dtype)
    mixed = jnp.einsum('gts,bnsgc->bntgc', w, v) + b_s.T[None, None, :, :, None]
    return u * mixed.reshape(bn, s, D_SGU)


def _fwd_setup_inputs(seed: int = 0) -> dict:
    key = jax.random.key(seed)
    ks = jax.random.split(key, 24)
    f32 = jnp.float32
    L, D = DEPTH, D_MODEL

    def nrm(k, shape, fan_in):
        return jax.random.normal(k, shape, f32) * (fan_in ** -0.5)

    def gain(k, shape):
        return 1.0 + 0.05 * jax.random.normal(k, shape, f32)

    def bias(k, shape):
        return 0.02 * jax.random.normal(k, shape, f32)

    return {
        "x": jax.random.normal(ks[0], (BATCH, SEQ, D), f32),
        "c": jax.random.normal(ks[1], (BATCH, D), f32),
        "w_ada": nrm(ks[2], (L, D, N_MOD * D), D),
        "b_ada": bias(ks[3], (L, N_MOD * D)),
        "norm1_g": gain(ks[4], (L, D)),
        "w_in": nrm(ks[5], (L, D, N_IN), D),
        "w_short": nrm(ks[6], (L, SHORT_K, D_CONV), SHORT_K),
        "w_a_out": nrm(ks[7], (L, D_CONV, D), D_CONV),
        "sgu_ln_g": gain(ks[8], (L, D_SGU)),
        "sgu_ln_b": bias(ks[9], (L, D_SGU)),
        "w_sgu": nrm(ks[10], (L, N_SGU_GROUPS, CHUNK, CHUNK), CHUNK),
        "b_sgu": 1.0 + 0.1 * jax.random.normal(ks[11], (L, N_SGU_GROUPS, CHUNK), f32),
        "w_b_out": nrm(ks[12], (L, D_SGU, D), D_SGU),
        "cfm_conv_w": nrm(ks[13], (L, CFM_K, D_CFM), CFM_K),
        "cfm_conv_b": bias(ks[14], (L, D_CFM)),
        "cfm_ln_g": gain(ks[15], (L, D_CFM)),
        "cfm_ln_b": bias(ks[16], (L, D_CFM)),
        "w_c_out": nrm(ks[17], (L, D_CFM, D), D_CFM),
        "w_o": nrm(ks[18], (L, D, D), D),
        "norm2_g": gain(ks[19], (L, D)),
        "w_ffn_in": nrm(ks[20], (L, D, 2 * D_FF), D),
        "w_ffn_out": nrm(ks[21], (L, D_FF, D), D_FF),
        "final_g": gain(ks[22], (D,)),
    }


def _fwd_reference(x, c, w_ada, b_ada, norm1_g, w_in, w_short, w_a_out, sgu_ln_g, sgu_ln_b,
              w_sgu, b_sgu, w_b_out, cfm_conv_w, cfm_conv_b, cfm_ln_g, cfm_ln_b, w_c_out,
              w_o, norm2_g, w_ffn_in, w_ffn_out, final_g):
    bn, s, d = x.shape
    c_act = jax.nn.silu(c)
    for l in range(DEPTH):
        mod = c_act @ w_ada[l] + b_ada[l]
        shift1, scale1, gate1, shift2, scale2, gate2 = jnp.split(mod, N_MOD, axis=-1)

        h = modulate(rms_norm(x, norm1_g[l]), shift1, scale1)
        z = h @ w_in[l]
        b_a, c_a, x_a, u, v, a_c, g_c, gate_a, gate_b, gate_c = jnp.split(z, SPLIT_IDX, axis=-1)

        y_a = (b_a * causal_depthwise_conv(c_a * x_a, w_short[l])) @ w_a_out[l]

        y_b = spatial_gating(jax.nn.gelu(u), jax.nn.gelu(v), sgu_ln_g[l], sgu_ln_b[l],
                             w_sgu[l], b_sgu[l]) @ w_b_out[l]

        glu = a_c * jax.nn.sigmoid(g_c)
        conv = causal_depthwise_conv(glu, cfm_conv_w[l]) + cfm_conv_b[l]
        y_c = jax.nn.silu(layer_norm(conv, cfm_ln_g[l], cfm_ln_b[l])) @ w_c_out[l]

        merged = (jax.nn.sigmoid(gate_a) * y_a + jax.nn.sigmoid(gate_b) * y_b
                  + jax.nn.sigmoid(gate_c) * y_c)
        x = x + gate1[:, None, :] * (merged @ w_o[l])

        h2 = modulate(rms_norm(x, norm2_g[l]), shift2, scale2)
        g_f, u_f = jnp.split(h2 @ w_ffn_in[l], 2, axis=-1)
        x = x + gate2[:, None, :] * ((jax.nn.silu(g_f) * u_f) @ w_ffn_out[l])

    return rms_norm(x, final_g)


import jax as _jax
import jax.numpy as _jnp

TWIN_FORMAT = 'train_step'
FWD_PARAMS = ['x', 'c', 'w_ada', 'b_ada', 'norm1_g', 'w_in', 'w_short', 'w_a_out', 'sgu_ln_g', 'sgu_ln_b', 'w_sgu', 'b_sgu', 'w_b_out', 'cfm_conv_w', 'cfm_conv_b', 'cfm_ln_g', 'cfm_ln_b', 'w_c_out', 'w_o', 'norm2_g', 'w_ffn_in', 'w_ffn_out', 'final_g']
TWIN_WEIGHTS = ['w_ada', 'b_ada', 'norm1_g', 'w_in', 'w_short', 'w_a_out', 'sgu_ln_g', 'sgu_ln_b', 'w_sgu', 'b_sgu', 'w_b_out', 'cfm_conv_w', 'cfm_conv_b', 'cfm_ln_g', 'cfm_ln_b', 'w_c_out', 'w_o', 'norm2_g', 'w_ffn_in', 'w_ffn_out', 'final_g']
TWIN_DIFF_INPUT = 'x'
TWIN_INPUTS = ['x', 'c', 'w_ada', 'b_ada', 'norm1_g', 'w_in', 'w_short', 'w_a_out', 'sgu_ln_g', 'sgu_ln_b', 'w_sgu', 'b_sgu', 'w_b_out', 'cfm_conv_w', 'cfm_conv_b', 'cfm_ln_g', 'cfm_ln_b', 'w_c_out', 'w_o', 'norm2_g', 'w_ffn_in', 'w_ffn_out', 'final_g', 'loss_target', 'm_w_ada', 'm_b_ada', 'm_norm1_g', 'm_w_in', 'm_w_short', 'm_w_a_out', 'm_sgu_ln_g', 'm_sgu_ln_b', 'm_w_sgu', 'm_b_sgu', 'm_w_b_out', 'm_cfm_conv_w', 'm_cfm_conv_b', 'm_cfm_ln_g', 'm_cfm_ln_b', 'm_w_c_out', 'm_w_o', 'm_norm2_g', 'm_w_ffn_in', 'm_w_ffn_out', 'm_final_g', 'v_w_ada', 'v_b_ada', 'v_norm1_g', 'v_w_in', 'v_w_short', 'v_w_a_out', 'v_sgu_ln_g', 'v_sgu_ln_b', 'v_w_sgu', 'v_b_sgu', 'v_w_b_out', 'v_cfm_conv_w', 'v_cfm_conv_b', 'v_cfm_ln_g', 'v_cfm_ln_b', 'v_w_c_out', 'v_w_o', 'v_norm2_g', 'v_w_ffn_in', 'v_w_ffn_out', 'v_final_g']
TWIN_OUTPUTS = ['loss', 'grad_x', 'grad_w_ada', 'grad_b_ada', 'grad_norm1_g', 'grad_w_in', 'grad_w_short', 'grad_w_a_out', 'grad_sgu_ln_g', 'grad_sgu_ln_b', 'grad_w_sgu', 'grad_b_sgu', 'grad_w_b_out', 'grad_cfm_conv_w', 'grad_cfm_conv_b', 'grad_cfm_ln_g', 'grad_cfm_ln_b', 'grad_w_c_out', 'grad_w_o', 'grad_norm2_g', 'grad_w_ffn_in', 'grad_w_ffn_out', 'grad_final_g', 'delta_w_ada', 'delta_b_ada', 'delta_norm1_g', 'delta_w_in', 'delta_w_short', 'delta_w_a_out', 'delta_sgu_ln_g', 'delta_sgu_ln_b', 'delta_w_sgu', 'delta_b_sgu', 'delta_w_b_out', 'delta_cfm_conv_w', 'delta_cfm_conv_b', 'delta_cfm_ln_g', 'delta_cfm_ln_b', 'delta_w_c_out', 'delta_w_o', 'delta_norm2_g', 'delta_w_ffn_in', 'delta_w_ffn_out', 'delta_final_g', 'new_m_w_ada', 'new_m_b_ada', 'new_m_norm1_g', 'new_m_w_in', 'new_m_w_short', 'new_m_w_a_out', 'new_m_sgu_ln_g', 'new_m_sgu_ln_b', 'new_m_w_sgu', 'new_m_b_sgu', 'new_m_w_b_out', 'new_m_cfm_conv_w', 'new_m_cfm_conv_b', 'new_m_cfm_ln_g', 'new_m_cfm_ln_b', 'new_m_w_c_out', 'new_m_w_o', 'new_m_norm2_g', 'new_m_w_ffn_in', 'new_m_w_ffn_out', 'new_m_final_g', 'new_v_w_ada', 'new_v_b_ada', 'new_v_norm1_g', 'new_v_w_in', 'new_v_w_short', 'new_v_w_a_out', 'new_v_sgu_ln_g', 'new_v_sgu_ln_b', 'new_v_w_sgu', 'new_v_b_sgu', 'new_v_w_b_out', 'new_v_cfm_conv_w', 'new_v_cfm_conv_b', 'new_v_cfm_ln_g', 'new_v_cfm_ln_b', 'new_v_w_c_out', 'new_v_w_o', 'new_v_norm2_g', 'new_v_w_ffn_in', 'new_v_w_ffn_out', 'new_v_final_g']
TWIN_LEAF_KINDS = {'loss': 'loss', 'grad_x': 'grad_x', 'grad_w_ada': 'grad_w', 'grad_b_ada': 'grad_w', 'grad_norm1_g': 'grad_w', 'grad_w_in': 'grad_w', 'grad_w_short': 'grad_w', 'grad_w_a_out': 'grad_w', 'grad_sgu_ln_g': 'grad_w', 'grad_sgu_ln_b': 'grad_w', 'grad_w_sgu': 'grad_w', 'grad_b_sgu': 'grad_w', 'grad_w_b_out': 'grad_w', 'grad_cfm_conv_w': 'grad_w', 'grad_cfm_conv_b': 'grad_w', 'grad_cfm_ln_g': 'grad_w', 'grad_cfm_ln_b': 'grad_w', 'grad_w_c_out': 'grad_w', 'grad_w_o': 'grad_w', 'grad_norm2_g': 'grad_w', 'grad_w_ffn_in': 'grad_w', 'grad_w_ffn_out': 'grad_w', 'grad_final_g': 'grad_w', 'delta_w_ada': 'delta_w', 'delta_b_ada': 'delta_w', 'delta_norm1_g': 'delta_w', 'delta_w_in': 'delta_w', 'delta_w_short': 'delta_w', 'delta_w_a_out': 'delta_w', 'delta_sgu_ln_g': 'delta_w', 'delta_sgu_ln_b': 'delta_w', 'delta_w_sgu': 'delta_w', 'delta_b_sgu': 'delta_w', 'delta_w_b_out': 'delta_w', 'delta_cfm_conv_w': 'delta_w', 'delta_cfm_conv_b': 'delta_w', 'delta_cfm_ln_g': 'delta_w', 'delta_cfm_ln_b': 'delta_w', 'delta_w_c_out': 'delta_w', 'delta_w_o': 'delta_w', 'delta_norm2_g': 'delta_w', 'delta_w_ffn_in': 'delta_w', 'delta_w_ffn_out': 'delta_w', 'delta_final_g': 'delta_w', 'new_m_w_ada': 'new_m', 'new_m_b_ada': 'new_m', 'new_m_norm1_g': 'new_m', 'new_m_w_in': 'new_m', 'new_m_w_short': 'new_m', 'new_m_w_a_out': 'new_m', 'new_m_sgu_ln_g': 'new_m', 'new_m_sgu_ln_b': 'new_m', 'new_m_w_sgu': 'new_m', 'new_m_b_sgu': 'new_m', 'new_m_w_b_out': 'new_m', 'new_m_cfm_conv_w': 'new_m', 'new_m_cfm_conv_b': 'new_m', 'new_m_cfm_ln_g': 'new_m', 'new_m_cfm_ln_b': 'new_m', 'new_m_w_c_out': 'new_m', 'new_m_w_o': 'new_m', 'new_m_norm2_g': 'new_m', 'new_m_w_ffn_in': 'new_m', 'new_m_w_ffn_out': 'new_m', 'new_m_final_g': 'new_m', 'new_v_w_ada': 'new_v', 'new_v_b_ada': 'new_v', 'new_v_norm1_g': 'new_v', 'new_v_w_in': 'new_v', 'new_v_w_short': 'new_v', 'new_v_w_a_out': 'new_v', 'new_v_sgu_ln_g': 'new_v', 'new_v_sgu_ln_b': 'new_v', 'new_v_w_sgu': 'new_v', 'new_v_b_sgu': 'new_v', 'new_v_w_b_out': 'new_v', 'new_v_cfm_conv_w': 'new_v', 'new_v_cfm_conv_b': 'new_v', 'new_v_cfm_ln_g': 'new_v', 'new_v_cfm_ln_b': 'new_v', 'new_v_w_c_out': 'new_v', 'new_v_w_o': 'new_v', 'new_v_norm2_g': 'new_v', 'new_v_w_ffn_in': 'new_v', 'new_v_w_ffn_out': 'new_v', 'new_v_final_g': 'new_v'}


def _forward(args):
    return _fwd_reference(*[args[k] for k in FWD_PARAMS])


def _output_shape():
    out = _jax.eval_shape(lambda: _forward(_fwd_setup_inputs(0)))
    return out.shape, out.dtype

N_MICROBATCH = 1
ADAM_LR = 0.001
ADAM_B1 = 0.9
ADAM_B2 = 0.999
ADAM_EPS = 1e-08
ADAM_WD = 0.01
ADAM_STEP = 10
PER_EXAMPLE_BATCH_AXIS = {'x': 0, 'c': 0, 'loss_target': 0}
SHARED_INPUTS = []
_WEIGHT_DTYPES = {'w_ada': _jnp.float32, 'b_ada': _jnp.float32, 'norm1_g': _jnp.float32, 'w_in': _jnp.float32, 'w_short': _jnp.float32, 'w_a_out': _jnp.float32, 'sgu_ln_g': _jnp.float32, 'sgu_ln_b': _jnp.float32, 'w_sgu': _jnp.float32, 'b_sgu': _jnp.float32, 'w_b_out': _jnp.float32, 'cfm_conv_w': _jnp.float32, 'cfm_conv_b': _jnp.float32, 'cfm_ln_g': _jnp.float32, 'cfm_ln_b': _jnp.float32, 'w_c_out': _jnp.float32, 'w_o': _jnp.float32, 'norm2_g': _jnp.float32, 'w_ffn_in': _jnp.float32, 'w_ffn_out': _jnp.float32, 'final_g': _jnp.float32}
MOMENT_SCALE = {'w_ada': 1.427603e-01, 'b_ada': 2.506782e-01, 'norm1_g': 1.937136e-01, 'w_in': 7.046083e-02, 'w_short': 1.167069e-01, 'w_a_out': 1.155627e-01, 'sgu_ln_g': 3.069959e-02, 'sgu_ln_b': 3.071186e-02, 'w_sgu': 3.038153e-02, 'b_sgu': 4.444686e-02, 'w_b_out': 5.700717e-02, 'cfm_conv_w': 3.109100e-02, 'cfm_conv_b': 4.885835e-02, 'cfm_ln_g': 3.728366e-02, 'cfm_ln_b': 3.711289e-02, 'w_c_out': 3.155795e-02, 'w_o': 1.352434e-01, 'norm2_g': 1.025515e-01, 'w_ffn_in': 5.120657e-02, 'w_ffn_out': 8.430411e-02, 'final_g': 3.321805e+01}


def _to_microbatches(a, axis):
    t = _jnp.moveaxis(a, axis, 0)
    t = t.reshape((N_MICROBATCH, t.shape[0] // N_MICROBATCH) + t.shape[1:])
    return _jnp.moveaxis(t, 1, axis + 1)


def setup_inputs(seed: int = 0) -> dict:
    inp = _fwd_setup_inputs(seed)
    key = _jax.random.fold_in(_jax.random.key(seed), 7919)
    shape, _ = _output_shape()
    out = dict(inp)
    out["loss_target"] = _jax.random.normal(_jax.random.fold_in(key, 0), shape, _jnp.float32)
    for i, name in enumerate(TWIN_WEIGHTS):
        w = inp[name].astype(_jnp.float32)
        if MOMENT_SCALE is None:
            s = _jnp.sqrt(_jnp.mean(_jnp.square(w)) + 1e-30)
        else:
            s = MOMENT_SCALE[name]
        km, kv = _jax.random.split(_jax.random.fold_in(key, i + 1))
        out[name] = w
        out["m_" + name] = s * _jax.random.normal(km, w.shape, _jnp.float32)
        out["v_" + name] = (s * s) * _jax.random.uniform(kv, w.shape, _jnp.float32, 0.5, 1.5)
    if N_MICROBATCH > 1:
        for name, axis in PER_EXAMPLE_BATCH_AXIS.items():
            out[name] = _to_microbatches(out[name], axis)
    return {'x': out['x'], 'c': out['c'], 'w_ada': out['w_ada'], 'b_ada': out['b_ada'], 'norm1_g': out['norm1_g'], 'w_in': out['w_in'], 'w_short': out['w_short'], 'w_a_out': out['w_a_out'], 'sgu_ln_g': out['sgu_ln_g'], 'sgu_ln_b': out['sgu_ln_b'], 'w_sgu': out['w_sgu'], 'b_sgu': out['b_sgu'], 'w_b_out': out['w_b_out'], 'cfm_conv_w': out['cfm_conv_w'], 'cfm_conv_b': out['cfm_conv_b'], 'cfm_ln_g': out['cfm_ln_g'], 'cfm_ln_b': out['cfm_ln_b'], 'w_c_out': out['w_c_out'], 'w_o': out['w_o'], 'norm2_g': out['norm2_g'], 'w_ffn_in': out['w_ffn_in'], 'w_ffn_out': out['w_ffn_out'], 'final_g': out['final_g'], 'loss_target': out['loss_target'], 'm_w_ada': out['m_w_ada'], 'm_b_ada': out['m_b_ada'], 'm_norm1_g': out['m_norm1_g'], 'm_w_in': out['m_w_in'], 'm_w_short': out['m_w_short'], 'm_w_a_out': out['m_w_a_out'], 'm_sgu_ln_g': out['m_sgu_ln_g'], 'm_sgu_ln_b': out['m_sgu_ln_b'], 'm_w_sgu': out['m_w_sgu'], 'm_b_sgu': out['m_b_sgu'], 'm_w_b_out': out['m_w_b_out'], 'm_cfm_conv_w': out['m_cfm_conv_w'], 'm_cfm_conv_b': out['m_cfm_conv_b'], 'm_cfm_ln_g': out['m_cfm_ln_g'], 'm_cfm_ln_b': out['m_cfm_ln_b'], 'm_w_c_out': out['m_w_c_out'], 'm_w_o': out['m_w_o'], 'm_norm2_g': out['m_norm2_g'], 'm_w_ffn_in': out['m_w_ffn_in'], 'm_w_ffn_out': out['m_w_ffn_out'], 'm_final_g': out['m_final_g'], 'v_w_ada': out['v_w_ada'], 'v_b_ada': out['v_b_ada'], 'v_norm1_g': out['v_norm1_g'], 'v_w_in': out['v_w_in'], 'v_w_short': out['v_w_short'], 'v_w_a_out': out['v_w_a_out'], 'v_sgu_ln_g': out['v_sgu_ln_g'], 'v_sgu_ln_b': out['v_sgu_ln_b'], 'v_w_sgu': out['v_w_sgu'], 'v_b_sgu': out['v_b_sgu'], 'v_w_b_out': out['v_w_b_out'], 'v_cfm_conv_w': out['v_cfm_conv_w'], 'v_cfm_conv_b': out['v_cfm_conv_b'], 'v_cfm_ln_g': out['v_cfm_ln_g'], 'v_cfm_ln_b': out['v_cfm_ln_b'], 'v_w_c_out': out['v_w_c_out'], 'v_w_o': out['v_w_o'], 'v_norm2_g': out['v_norm2_g'], 'v_w_ffn_in': out['v_w_ffn_in'], 'v_w_ffn_out': out['v_w_ffn_out'], 'v_final_g': out['v_final_g']}


def _loss(weights, diff, rest, loss_target):
    with _jax.named_scope("forward"):
        args = {**rest, TWIN_DIFF_INPUT: diff, **{k: w.astype(_WEIGHT_DTYPES[k]) for k, w in weights.items()}}
        y = _forward(args)
    with _jax.named_scope("loss_head"):
        err = _jnp.square(y.astype(_jnp.float32) - loss_target)
        return 0.5 * _jnp.sum(_jnp.mean(err, axis=-1)) if err.ndim else 0.5 * err


def _adamw(w, g, m, v):
    m = ADAM_B1 * m + (1.0 - ADAM_B1) * g
    v = ADAM_B2 * v + (1.0 - ADAM_B2) * _jnp.square(g)
    m_hat = m / (1.0 - ADAM_B1 ** ADAM_STEP)
    v_hat = v / (1.0 - ADAM_B2 ** ADAM_STEP)
    delta = -ADAM_LR * (m_hat / (_jnp.sqrt(v_hat) + ADAM_EPS) + ADAM_WD * w)
    return delta, m, v


def reference(x, c, w_ada, b_ada, norm1_g, w_in, w_short, w_a_out, sgu_ln_g, sgu_ln_b, w_sgu, b_sgu, w_b_out, cfm_conv_w, cfm_conv_b, cfm_ln_g, cfm_ln_b, w_c_out, w_o, norm2_g, w_ffn_in, w_ffn_out, final_g, loss_target, m_w_ada, m_b_ada, m_norm1_g, m_w_in, m_w_short, m_w_a_out, m_sgu_ln_g, m_sgu_ln_b, m_w_sgu, m_b_sgu, m_w_b_out, m_cfm_conv_w, m_cfm_conv_b, m_cfm_ln_g, m_cfm_ln_b, m_w_c_out, m_w_o, m_norm2_g, m_w_ffn_in, m_w_ffn_out, m_final_g, v_w_ada, v_b_ada, v_norm1_g, v_w_in, v_w_short, v_w_a_out, v_sgu_ln_g, v_sgu_ln_b, v_w_sgu, v_b_sgu, v_w_b_out, v_cfm_conv_w, v_cfm_conv_b, v_cfm_ln_g, v_cfm_ln_b, v_w_c_out, v_w_o, v_norm2_g, v_w_ffn_in, v_w_ffn_out, v_final_g):
    given = dict(x=x, c=c, w_ada=w_ada, b_ada=b_ada, norm1_g=norm1_g, w_in=w_in, w_short=w_short, w_a_out=w_a_out, sgu_ln_g=sgu_ln_g, sgu_ln_b=sgu_ln_b, w_sgu=w_sgu, b_sgu=b_sgu, w_b_out=w_b_out, cfm_conv_w=cfm_conv_w, cfm_conv_b=cfm_conv_b, cfm_ln_g=cfm_ln_g, cfm_ln_b=cfm_ln_b, w_c_out=w_c_out, w_o=w_o, norm2_g=norm2_g, w_ffn_in=w_ffn_in, w_ffn_out=w_ffn_out, final_g=final_g, loss_target=loss_target, m_w_ada=m_w_ada, m_b_ada=m_b_ada, m_norm1_g=m_norm1_g, m_w_in=m_w_in, m_w_short=m_w_short, m_w_a_out=m_w_a_out, m_sgu_ln_g=m_sgu_ln_g, m_sgu_ln_b=m_sgu_ln_b, m_w_sgu=m_w_sgu, m_b_sgu=m_b_sgu, m_w_b_out=m_w_b_out, m_cfm_conv_w=m_cfm_conv_w, m_cfm_conv_b=m_cfm_conv_b, m_cfm_ln_g=m_cfm_ln_g, m_cfm_ln_b=m_cfm_ln_b, m_w_c_out=m_w_c_out, m_w_o=m_w_o, m_norm2_g=m_norm2_g, m_w_ffn_in=m_w_ffn_in, m_w_ffn_out=m_w_ffn_out, m_final_g=m_final_g, v_w_ada=v_w_ada, v_b_ada=v_b_ada, v_norm1_g=v_norm1_g, v_w_in=v_w_in, v_w_short=v_w_short, v_w_a_out=v_w_a_out, v_sgu_ln_g=v_sgu_ln_g, v_sgu_ln_b=v_sgu_ln_b, v_w_sgu=v_w_sgu, v_b_sgu=v_b_sgu, v_w_b_out=v_w_b_out, v_cfm_conv_w=v_cfm_conv_w, v_cfm_conv_b=v_cfm_conv_b, v_cfm_ln_g=v_cfm_ln_g, v_cfm_ln_b=v_cfm_ln_b, v_w_c_out=v_w_c_out, v_w_o=v_w_o, v_norm2_g=v_norm2_g, v_w_ffn_in=v_w_ffn_in, v_w_ffn_out=v_w_ffn_out, v_final_g=v_final_g)
    weights = {n: given[n] for n in TWIN_WEIGHTS}
    shared = {n: given[n] for n in SHARED_INPUTS}
    per_example = {n: given[n] for n in ['x', 'c']}
    grad_fn = _jax.value_and_grad(_loss, argnums=(0, 1))

    def one_microbatch(ex, loss_target):
        ex = dict(ex)
        diff = ex.pop(TWIN_DIFF_INPUT)
        return grad_fn(weights, diff, {**shared, **ex}, loss_target)

    if N_MICROBATCH == 1:
        loss, (grad_w, grad_x) = one_microbatch(per_example, given["loss_target"])
    else:
        def body(carry, xs):
            loss_sum, grad_sum = carry
            l_k, (gw_k, gx_k) = one_microbatch(xs[0], xs[1])
            with _jax.named_scope("update"):
                return (loss_sum + l_k, _jax.tree.map(_jnp.add, grad_sum, gw_k)), gx_k

        init = (_jnp.zeros((), _jnp.float32), _jax.tree.map(_jnp.zeros_like, weights))
        (loss, grad_w), grad_x = _jax.lax.scan(body, init, (per_example, given["loss_target"]))
    with _jax.named_scope("update"):
        delta_w, new_m, new_v = {}, {}, {}
        for n in TWIN_WEIGHTS:
            delta_w[n], new_m[n], new_v[n] = _adamw(weights[n], grad_w[n], given["m_" + n], given["v_" + n])
    return (loss, grad_x, *[grad_w[n] for n in TWIN_WEIGHTS], *[delta_w[n] for n in TWIN_WEIGHTS],
            *[new_m[n] for n in TWIN_WEIGHTS], *[new_v[n] for n in TWIN_WEIGHTS])
```

```python
import functools
import math

import jax
import jax.numpy as jnp
from jax import lax
from jax.experimental import pallas as pl
from jax.experimental.pallas import tpu as pltpu

F32, BF16 = jnp.float32, jnp.bfloat16
NDEV = 8
NCHIP = NDEV // 2
DEPTH = 2
EPS = 1e-6
CHUNK = 128
NG = 8
SHORT_K = 3
CFM_K = 31
HALO = 32
N_MOD = 6
LANE = 128
VMEM_LIMIT = 56 * 1024 * 1024
ADAM_LR, ADAM_B1, ADAM_B2, ADAM_EPS, ADAM_WD, ADAM_STEP = 0.001, 0.9, 0.999, 1e-08, 0.01, 10
_G0 = math.sqrt(2.0 / math.pi)
_G1 = 0.044715
MESH = pl.DeviceIdType.MESH
ANY = pl.BlockSpec(memory_space=pl.ANY)


def _pcall(body, **kw):
    return pl.pallas_call(body, **kw)


def _params(sem=None):
    return pltpu.CompilerParams(dimension_semantics=sem, vmem_limit_bytes=VMEM_LIMIT)


def _sds(shape, dtype):
    return jax.ShapeDtypeStruct(tuple(shape), dtype)


def _mm_body(dims, nk, out_f32):
    def body(a_ref, b_ref, o_ref, *scr):
        k = pl.program_id(2)
        part = lax.dot_general(a_ref[...], b_ref[...], dims, preferred_element_type=F32)
        if nk == 1:
            o_ref[...] = part.reshape(o_ref.shape).astype(o_ref.dtype)
        elif out_f32:
            @pl.when(k == 0)
            def _():
                o_ref[...] = part.reshape(o_ref.shape)

            @pl.when(k > 0)
            def _():
                o_ref[...] += part.reshape(o_ref.shape)
        else:
            acc = scr[0]

            @pl.when(k == 0)
            def _():
                acc[...] = part

            @pl.when(k > 0)
            def _():
                acc[...] += part

            @pl.when(k == nk - 1)
            def _():
                o_ref[...] = acc[...].astype(o_ref.dtype)
    return body


def _mm_call(body, grid, in_specs, out_spec, out_shape, acc_shape, name, alias=None):
    scratch = [] if acc_shape is None else [pltpu.VMEM(acc_shape, F32)]
    kw = {}
    if alias is not None:
        in_specs = in_specs + [ANY]
        kw["input_output_aliases"] = {2: 0}
    return _pcall(body, grid=grid, in_specs=in_specs, out_specs=out_spec, out_shape=out_shape,
                  scratch_shapes=scratch, name=name,
                  compiler_params=_params(("parallel", "parallel", "arbitrary")), **kw)


def _mm_nn(a, b3, out_dtype, tm, tn, tk, name, w_outer=False):
    M, K = a.shape
    G, _, Nb = b3.shape
    npb, nk = Nb // tn, K // tk
    out_f32 = out_dtype == F32
    body = _mm_body((((1,), (0,)), ((), ())), nk, out_f32)
    if w_outer:
        grid = (G * npb, M // tm, nk)
        ij = lambda p, q: (q, p)
    else:
        grid = (M // tm, G * npb, nk)
        ij = lambda p, q: (p, q)

    def a_map(p, q, k):
        i, j = ij(p, q)
        return (i, k)

    def b_map(p, q, k):
        i, j = ij(p, q)
        return (j // npb, k, j % npb)

    def o_map(p, q, k):
        return ij(p, q)

    def wrapped(a_ref, b_ref, o_ref, *scr):
        body(a_ref, b_ref, o_ref, *scr)

    return _mm_call(wrapped, grid, [pl.BlockSpec((tm, tk), a_map), pl.BlockSpec((None, tk, tn), b_map)],
                    pl.BlockSpec((tm, tn), o_map), _sds((M, G * Nb), out_dtype),
                    None if (nk == 1 or out_f32) else (tm, tn), name)(a, b3)


def _mm_nt(a, b3, out_dtype, tm, tn, tk, name):
    M, _ = a.shape
    G, Ko, Nb = b3.shape
    kpb = Nb // tk
    nk = G * kpb
    out_f32 = out_dtype == F32
    body = _mm_body((((1,), (1,)), ((), ())), nk, out_f32)

    def wrapped(a_ref, b_ref, o_ref, *scr):
        body(a_ref, b_ref, o_ref, *scr)

    return _mm_call(wrapped, (M // tm, Ko // tn, nk),
                    [pl.BlockSpec((tm, tk), lambda i, j, k: (i, k)),
                     pl.BlockSpec((None, tn, tk), lambda i, j, k: (k // kpb, j, k % kpb))],
                    pl.BlockSpec((tm, tn), lambda i, j, k: (i, j)), _sds((M, Ko), out_dtype),
                    None if (nk == 1 or out_f32) else (tm, tn), name)(a, b3)


def _mm_tn(a, b, G, layer, prev, tm, tn, tk, name):
    T, M = a.shape
    Nb = b.shape[1] // G
    npb, nk = Nb // tn, T // tk
    body = _mm_body((((0,), (0,)), ((), ())), nk, True)

    def wrapped(a_ref, b_ref, *rest):
        o_ref = rest[-1]
        body(a_ref, b_ref, o_ref)

    in_specs = [pl.BlockSpec((tk, tm), lambda i, j, k: (k, i)), pl.BlockSpec((tk, tn), lambda i, j, k: (k, j))]
    out_spec = pl.BlockSpec((None, None, tm, tn), lambda i, j, k: (layer, j // npb, i, j % npb))
    call = _mm_call(wrapped, (M // tm, G * npb, nk), in_specs, out_spec, _sds((DEPTH, G, M, Nb), F32), None, name,
                    alias=prev)
    return call(a, b) if prev is None else call(a, b, prev)


def _rsum(v):
    return jnp.sum(v, axis=0, keepdims=True)


def _rmean(v):
    return jnp.mean(v, axis=-1, keepdims=True)


def _gelu(x):
    t = jnp.tanh(_G0 * (x + _G1 * (x * x * x)))
    return x * (0.5 * (1.0 + t)), t


def _dgelu(x, t):
    return 0.5 * (1.0 + t) + 0.5 * x * (1.0 - t * t) * (_G0 * (1.0 + 3.0 * _G1 * (x * x)))


def _sigmoid(x):
    return 1.0 / (1.0 + jnp.exp(-x))


def _rows(*vs):
    a = jnp.stack([v.astype(F32) for v in vs])
    return jnp.pad(a, ((0, 8 - len(vs)), (0, 0)))


def _row_spec(tm, D):
    return pl.BlockSpec((tm, D), lambda i: (i, 0))


def _const_spec(shape):
    nd = len(shape)
    return pl.BlockSpec(shape, lambda i: (0,) * nd)


def _norm_fwd(xp, f, vec, name):
    S, D = xp.shape
    tm = min(256, S)
    has_f = f is not None

    def body(*refs):
        if has_f:
            xp_ref, f_ref, vec_ref, xo_ref, h_ref = refs
            x = xp_ref[...] + vec_ref[0:1, :] * f_ref[...]
            xo_ref[...] = x
        else:
            xp_ref, vec_ref, h_ref = refs
            x = xp_ref[...]
        r = lax.rsqrt(_rmean(x * x) + EPS)
        h = (x * r) * vec_ref[1:2, :]
        h_ref[...] = (h * (1.0 + vec_ref[2:3, :]) + vec_ref[3:4, :]).astype(BF16)

    rs = _row_spec(tm, D)
    ins = [xp, f, vec] if has_f else [xp, vec]
    in_specs = ([rs, rs] if has_f else [rs]) + [_const_spec((8, D))]
    out_shape = ([_sds((S, D), F32)] if has_f else []) + [_sds((S, D), BF16)]
    out_specs = [rs] * len(out_shape)
    outs = _pcall(body, grid=(S // tm,), in_specs=in_specs, out_specs=out_specs, out_shape=out_shape, name=name,
                  compiler_params=_params(("parallel",)))(*ins)
    return (outs[0], outs[1]) if has_f else (xp, outs[0])


def _mixer_fwd(z, wsh, sgu_ln, wtril, bias_full, cw, cvec, name):
    S = z.shape[0]
    D = wsh.shape[1]
    tm = CHUNK

    def body(z_ref, wsh_ref, sln_ref, wt_ref, bias_ref, cw_ref, cv_ref, oa_ref, ob_ref, oc_ref, pe, ge):
        i = pl.program_id(0)

        @pl.when(i == 0)
        def _():
            pe[0:HALO, :] = jnp.zeros((HALO, D), F32)
            ge[0:HALO, :] = jnp.zeros((HALO, D), F32)

        pe[HALO:HALO + tm, :] = z_ref[:, D:2 * D] * z_ref[:, 2 * D:3 * D]
        q = wsh_ref[0:1, :] * pe[HALO - 2:HALO - 2 + tm, :]
        q = q + wsh_ref[1:2, :] * pe[HALO - 1:HALO - 1 + tm, :]
        q = q + wsh_ref[2:3, :] * pe[HALO:HALO + tm, :]
        oa_ref[...] = (z_ref[:, 0:D] * q).astype(BF16)
        gu, _ = _gelu(z_ref[:, 3 * D:4 * D])
        gv, _ = _gelu(z_ref[:, 4 * D:5 * D])
        d = gv - _rmean(gv)
        nrm = d * lax.rsqrt(_rmean(d * d) + EPS)
        vnb = (nrm * sln_ref[0:1, :] + sln_ref[1:2, :]).astype(BF16)
        for g in range(NG):
            cs = slice(g * LANE, (g + 1) * LANE)
            mixed = jnp.dot(wt_ref[g], vnb[:, cs], preferred_element_type=F32) + bias_ref[:, cs]
            ob_ref[:, cs] = (gu[:, cs] * mixed).astype(BF16)
        ge[HALO:HALO + tm, :] = z_ref[:, 5 * D:6 * D] * _sigmoid(z_ref[:, 6 * D:7 * D])
        conv = cv_ref[0:1, :] + cw_ref[0:1, :] * ge[HALO - (CFM_K - 1):HALO - (CFM_K - 1) + tm, :]
        for k in range(1, CFM_K):
            o = HALO - (CFM_K - 1) + k
            conv = conv + cw_ref[k:k + 1, :] * ge[o:o + tm, :]
        d = conv - _rmean(conv)
        ln = (d * lax.rsqrt(_rmean(d * d) + EPS)) * cv_ref[1:2, :] + cv_ref[2:3, :]
        oc_ref[...] = (ln * _sigmoid(ln)).astype(BF16)
        pe[0:HALO, :] = pe[tm:tm + HALO, :]
        ge[0:HALO, :] = ge[tm:tm + HALO, :]

    rs = _row_spec(tm, D)
    return _pcall(
        body, grid=(S // tm,),
        in_specs=[pl.BlockSpec((tm, 7 * D), lambda i: (i, 0)), _const_spec((8, D)), _const_spec((8, D)),
                  _const_spec((NG, CHUNK, CHUNK)), _const_spec((CHUNK, D)), _const_spec((HALO, D)), _const_spec((8, D))],
        out_specs=[rs, rs, rs], out_shape=[_sds((S, D), BF16)] * 3,
        scratch_shapes=[pltpu.VMEM((HALO + tm, D), F32), pltpu.VMEM((HALO + tm, D), F32)],
        name=name, compiler_params=_params(("arbitrary",)))(z, wsh, sgu_ln, wtril, bias_full, cw, cvec)


def _branch_out(acts, ws, z, name):
    S, D = acts[0].shape
    tm = min(256, S)

    def body(a0, a1, a2, w0, w1, w2, g0, g1, g2, m_ref, y_ref):
        m = None
        for n, (a, w, g) in enumerate(((a0, w0, g0), (a1, w1, g1), (a2, w2, g2))):
            y = jnp.dot(a[...], w[...], preferred_element_type=F32)
            y_ref[n] = y.astype(BF16)
            t = _sigmoid(g[...]) * y
            m = t if m is None else m + t
        m_ref[...] = m.astype(BF16)

    rs = _row_spec(tm, D)
    gate_specs = [pl.BlockSpec((tm, D), functools.partial(lambda i, n: (i, 7 + n), n=n)) for n in range(3)]
    return _pcall(body, grid=(S // tm,),
                  in_specs=[rs, rs, rs] + [_const_spec((D, D))] * 3 + gate_specs,
                  out_specs=[rs, pl.BlockSpec((3, tm, D), lambda i: (0, i, 0))],
                  out_shape=[_sds((S, D), BF16), _sds((3, S, D), BF16)], name=name,
                  compiler_params=_params(("parallel",)))(*acts, *ws, z, z, z)


def _swiglu_fwd(gu, name):
    S, F2 = gu.shape
    F = F2 // 2
    tm = min(256, S)

    def body(g_ref, u_ref, o_ref):
        g = g_ref[...]
        o_ref[...] = ((g * _sigmoid(g)) * u_ref[...]).astype(BF16)

    return _pcall(body, grid=(S // tm,),
                  in_specs=[pl.BlockSpec((tm, F), lambda i: (i, 0)), pl.BlockSpec((tm, F), lambda i: (i, 1))],
                  out_specs=pl.BlockSpec((tm, F), lambda i: (i, 0)), out_shape=_sds((S, F), BF16), name=name,
                  compiler_params=_params(("parallel",)))(gu, gu)


def _swiglu_bwd(dact, gu, name):
    S, F2 = gu.shape
    F = F2 // 2
    tm = min(128, S)

    def body(d_ref, g_ref, u_ref, o_ref):
        g = g_ref[...]
        sg = _sigmoid(g)
        d = d_ref[...]
        o_ref[:, 0:F] = (d * u_ref[...] * (sg * (1.0 + g * (1.0 - sg)))).astype(BF16)
        o_ref[:, F:2 * F] = (d * (g * sg)).astype(BF16)

    return _pcall(body, grid=(S // tm,),
                  in_specs=[pl.BlockSpec((tm, F), lambda i: (i, 0)), pl.BlockSpec((tm, F), lambda i: (i, 0)),
                            pl.BlockSpec((tm, F), lambda i: (i, 1))],
                  out_specs=pl.BlockSpec((tm, F2), lambda i: (i, 0)), out_shape=_sds((S, F2), BF16), name=name,
                  compiler_params=_params(("parallel",)))(dact, gu, gu)


def _final_bwd(x1, f, tgt, vec, name):
    S, D = x1.shape
    tm = min(256, S)

    def body(x_ref, f_ref, t_ref, vec_ref, dx_ref, df_ref, sums_ref, loss_ref):
        @pl.when(pl.program_id(0) == 0)
        def _():
            sums_ref[...] = jnp.zeros_like(sums_ref)
            loss_ref[...] = jnp.zeros_like(loss_ref)

        gate, fg = vec_ref[0:1, :], vec_ref[1:2, :]
        fv = f_ref[...]
        x = x_ref[...] + gate * fv
        r = lax.rsqrt(_rmean(x * x) + EPS)
        xn = x * r
        diff = xn * fg - t_ref[...]
        per_tok = _rmean(diff * diff)
        loss_ref[...] += 0.5 * jnp.sum(per_tok, axis=0, keepdims=True)
        dy = diff * (1.0 / D)
        sums_ref[0:1, :] += _rsum(dy * xn)
        dxn = dy * fg
        dx = r * (dxn - xn * _rmean(dxn * xn))
        sums_ref[1:2, :] += _rsum(dx * fv)
        dx_ref[...] = dx
        df_ref[...] = (dx * gate).astype(BF16)

    rs = _row_spec(tm, D)
    return _pcall(body, grid=(S // tm,), in_specs=[rs, rs, rs, _const_spec((8, D))],
                  out_specs=[rs, rs, _const_spec((8, D)), _const_spec((8, LANE))],
                  out_shape=[_sds((S, D), F32), _sds((S, D), BF16), _sds((8, D), F32), _sds((8, LANE), F32)],
                  name=name, compiler_params=_params(("arbitrary",)))(x1, f, tgt, vec)


def _norm_bwd(xin, dh, dxup, vec, fprev, name):
    S, D = xin.shape
    tm = min(256, S)
    has_prev = fprev is not None

    def body(*refs):
        if has_prev:
            x_ref, dh_ref, up_ref, vec_ref, fp_ref, dx_ref, dp_ref, sums_ref = refs
        else:
            x_ref, dh_ref, up_ref, vec_ref, dx_ref, sums_ref = refs

        @pl.when(pl.program_id(0) == 0)
        def _():
            sums_ref[...] = jnp.zeros_like(sums_ref)

        g, scale = vec_ref[0:1, :], vec_ref[1:2, :]
        x = x_ref[...]
        r = lax.rsqrt(_rmean(x * x) + EPS)
        xn = x * r
        dhv = dh_ref[...]
        sums_ref[0:1, :] += _rsum(dhv)
        sums_ref[1:2, :] += _rsum(dhv * (xn * g))
        dm = dhv * (1.0 + scale)
        sums_ref[2:3, :] += _rsum(dm * xn)
        dxn = dm * g
        dx = up_ref[...] + r * (dxn - xn * _rmean(dxn * xn))
        dx_ref[...] = dx
        if has_prev:
            sums_ref[3:4, :] += _rsum(dx * fp_ref[...])
            dp_ref[...] = (dx * vec_ref[2:3, :]).astype(BF16)

    rs = _row_spec(tm, D)
    ins = [xin, dh, dxup, vec] + ([fprev] if has_prev else [])
    in_specs = [rs, rs, rs, _const_spec((8, D))] + ([rs] if has_prev else [])
    out_shape = [_sds((S, D), F32)] + ([_sds((S, D), BF16)] if has_prev else []) + [_sds((8, D), F32)]
    out_specs = [rs] + ([rs] if has_prev else []) + [_const_spec((8, D))]
    outs = _pcall(body, grid=(S // tm,), in_specs=in_specs, out_specs=out_specs, out_shape=out_shape, name=name,
                  compiler_params=_params(("arbitrary",)))(*ins)
    return (outs[0], outs[1], outs[2]) if has_prev else (outs[0], None, outs[1])


def _gate_bwd(dmerged, z, ys, name):
    S, D = dmerged.shape
    tm = min(256, S)
    ncol = z.shape[1] // D

    def body(dm_ref, g_ref, y_ref, dy_ref, dz_ref):
        sg = _sigmoid(g_ref[...])
        dm = dm_ref[...]
        dy_ref[...] = (dm * sg).astype(BF16)
        dz_ref[...] = (dm * y_ref[...].astype(F32) * (sg * (1.0 - sg))).astype(BF16)

    return _pcall(body, grid=(S // tm, 3),
                  in_specs=[pl.BlockSpec((tm, D), lambda i, n: (i, 0)), pl.BlockSpec((tm, D), lambda i, n: (i, 7 + n)),
                            pl.BlockSpec((None, tm, D), lambda i, n: (n, i, 0))],
                  out_specs=[pl.BlockSpec((None, tm, D), lambda i, n: (n, i, 0)),
                             pl.BlockSpec((tm, D), lambda i, n: (i, 7 + n))],
                  out_shape=[_sds((3, S, D), BF16), _sds((S, ncol * D), BF16)], name=name,
                  compiler_params=_params(("parallel", "arbitrary")))(dmerged, z, ys)


def _mixer_bwd(z, dacts, dz, wsh, sgu_ln, wtril, wtril_t, bias_full, cw, cvec, name):
    S = z.shape[0]
    D = wsh.shape[1]
    tm = CHUNK
    nt = S // tm
    hb = tm // HALO

    def body(zc, zp, da_ref, db_ref, dc_ref, wsh_ref, sln_ref, wt_ref, wtt_ref, bias_ref, cw_ref, cv_ref, _dz_in,
             dz_ref, vec_ref, dcw_ref, dws_ref, dbs_ref, pe, ge, dqe, dce):
        i = pl.program_id(0)
        rb = nt - 1 - i

        @pl.when(i == 0)
        def _():
            vec_ref[...] = jnp.zeros_like(vec_ref)
            dcw_ref[...] = jnp.zeros_like(dcw_ref)
            dws_ref[...] = jnp.zeros_like(dws_ref)
            dbs_ref[...] = jnp.zeros_like(dbs_ref)
            dqe[tm:tm + HALO, :] = jnp.zeros((HALO, D), F32)
            dce[tm:tm + HALO, :] = jnp.zeros((HALO, D), F32)

        keep = (rb > 0).astype(F32)
        c_a, x_a = zc[:, D:2 * D], zc[:, 2 * D:3 * D]
        pe[0:HALO, :] = keep * (zp[:, D:2 * D] * zp[:, 2 * D:3 * D])
        pe[HALO:HALO + tm, :] = c_a * x_a
        q = wsh_ref[0:1, :] * pe[HALO - 2:HALO - 2 + tm, :]
        q = q + wsh_ref[1:2, :] * pe[HALO - 1:HALO - 1 + tm, :]
        q = q + wsh_ref[2:3, :] * pe[HALO:HALO + tm, :]
        dact = da_ref[...]
        dz_ref[:, 0:D] = (dact * q).astype(BF16)
        dq = dact * zc[:, 0:D]
        dqe[0:tm, :] = dq
        dp = wsh_ref[2:3, :] * dq + wsh_ref[1:2, :] * dqe[1:1 + tm, :] + wsh_ref[0:1, :] * dqe[2:2 + tm, :]
        dz_ref[:, D:2 * D] = (dp * x_a).astype(BF16)
        dz_ref[:, 2 * D:3 * D] = (dp * c_a).astype(BF16)
        for k in range(SHORT_K):
            o = HALO - (SHORT_K - 1) + k
            vec_ref[k:k + 1, :] += _rsum(dq * pe[o:o + tm, :])
        u, v = zc[:, 3 * D:4 * D], zc[:, 4 * D:5 * D]
        gu, tu = _gelu(u)
        gv, tv = _gelu(v)
        d = gv - _rmean(gv)
        rstd = lax.rsqrt(_rmean(d * d) + EPS)
        nrm = d * rstd
        vnb = (nrm * sln_ref[0:1, :] + sln_ref[1:2, :]).astype(BF16)
        dact = db_ref[...]
        dvn_parts, dgu_parts = [], []
        for g in range(NG):
            cs = slice(g * LANE, (g + 1) * LANE)
            vg = vnb[:, cs]
            mixed = jnp.dot(wt_ref[g], vg, preferred_element_type=F32) + bias_ref[:, cs]
            dgu_parts.append(dact[:, cs] * mixed)
            dmixed = dact[:, cs] * gu[:, cs]
            dmb = dmixed.astype(BF16)
            dws_ref[g] += lax.dot_general(dmb, vg, (((1,), (1,)), ((), ())), preferred_element_type=F32)
            dbs_ref[g] += jnp.broadcast_to(jnp.sum(dmixed, axis=1, keepdims=True), (CHUNK, LANE))
            dvn_parts.append(jnp.dot(wtt_ref[g], dmb, preferred_element_type=F32))
        dgu = jnp.concatenate(dgu_parts, axis=1)
        dvn = jnp.concatenate(dvn_parts, axis=1)
        dz_ref[:, 3 * D:4 * D] = (dgu * _dgelu(u, tu)).astype(BF16)
        vec_ref[3:4, :] += _rsum(dvn * nrm)
        vec_ref[4:5, :] += _rsum(dvn)
        dn = dvn * sln_ref[0:1, :]
        dgv = rstd * (dn - _rmean(dn) - nrm * _rmean(dn * nrm))
        dz_ref[:, 4 * D:5 * D] = (dgv * _dgelu(v, tv)).astype(BF16)
        a_c = zc[:, 5 * D:6 * D]
        sg = _sigmoid(zc[:, 6 * D:7 * D])
        ge[0:HALO, :] = keep * (zp[:, 5 * D:6 * D] * _sigmoid(zp[:, 6 * D:7 * D]))
        ge[HALO:HALO + tm, :] = a_c * sg
        o0 = HALO - (CFM_K - 1)
        conv = cv_ref[0:1, :] + cw_ref[0:1, :] * ge[o0:o0 + tm, :]
        for k in range(1, CFM_K):
            conv = conv + cw_ref[k:k + 1, :] * ge[o0 + k:o0 + k + tm, :]
        d = conv - _rmean(conv)
        rstd = lax.rsqrt(_rmean(d * d) + EPS)
        nrm = d * rstd
        ln = nrm * cv_ref[1:2, :] + cv_ref[2:3, :]
        sl = _sigmoid(ln)
        dln = dc_ref[...] * (sl * (1.0 + ln * (1.0 - sl)))
        vec_ref[6:7, :] += _rsum(dln * nrm)
        vec_ref[7:8, :] += _rsum(dln)
        dn = dln * cv_ref[1:2, :]
        dconv = rstd * (dn - _rmean(dn) - nrm * _rmean(dn * nrm))
        vec_ref[5:6, :] += _rsum(dconv)
        dce[0:tm, :] = dconv
        dglu = cw_ref[CFM_K - 1:CFM_K, :] * dconv
        for k in range(CFM_K - 1):
            o = CFM_K - 1 - k
            dglu = dglu + cw_ref[k:k + 1, :] * dce[o:o + tm, :]
        for k in range(CFM_K):
            dcw_ref[k:k + 1, :] += _rsum(dconv * ge[o0 + k:o0 + k + tm, :])
        dz_ref[:, 5 * D:6 * D] = (dglu * sg).astype(BF16)
        dz_ref[:, 6 * D:7 * D] = (dglu * a_c * (sg * (1.0 - sg))).astype(BF16)
        dqe[tm:tm + HALO, :] = dqe[0:HALO, :]
        dce[tm:tm + HALO, :] = dce[0:HALO, :]

    rev = lambda i: (nt - 1 - i, 0)
    rs = pl.BlockSpec((tm, D), rev)
    cur = pl.BlockSpec((tm, 7 * D), rev)
    prev = pl.BlockSpec((HALO, 7 * D), lambda i: (jnp.maximum((nt - 1 - i) * hb - 1, 0), 0))
    ext = pltpu.VMEM((HALO + tm, D), F32)
    outs = _pcall(
        body, grid=(nt,),
        in_specs=[cur, prev, rs, rs, rs, _const_spec((8, D)), _const_spec((8, D)), _const_spec((NG, CHUNK, CHUNK)),
                  _const_spec((NG, CHUNK, CHUNK)), _const_spec((CHUNK, D)), _const_spec((HALO, D)), _const_spec((8, D)),
                  ANY],
        out_specs=[cur, _const_spec((8, D)), _const_spec((HALO, D)), _const_spec((NG, CHUNK, CHUNK)),
                   _const_spec((NG, CHUNK, LANE))],
        out_shape=[_sds(dz.shape, BF16), _sds((8, D), F32), _sds((HALO, D), F32), _sds((NG, CHUNK, CHUNK), F32),
                   _sds((NG, CHUNK, LANE), F32)],
        scratch_shapes=[ext, ext, ext, ext], input_output_aliases={12: 0}, name=name,
        compiler_params=_params(("arbitrary",)))(z, z, *dacts, wsh, sgu_ln, wtril, wtril_t, bias_full, cw, cvec, dz)
    return outs


def _ada_fwd(c_all, w_ada_loc, name):
    nb, D = c_all.shape
    L, _, nc = w_ada_loc.shape

    def body(c_ref, w_ref, o_ref, ca_ref):
        cv = c_ref[...]
        ca = cv * _sigmoid(cv)
        ca_ref[...] = ca
        o_ref[...] = jnp.dot(ca.astype(BF16), w_ref[...].astype(BF16), preferred_element_type=F32)

    return _pcall(body, grid=(L,),
                  in_specs=[_const_spec((nb, D)), pl.BlockSpec((None, D, nc), lambda l: (l, 0, 0))],
                  out_specs=[pl.BlockSpec((None, nb, nc), lambda l: (l, 0, 0)), _const_spec((nb, D))],
                  out_shape=[_sds((L, nb, nc), F32), _sds((nb, D), F32)], name=name,
                  compiler_params=_params(("arbitrary",)))(c_all, w_ada_loc)


def _adamw(w, g, m, v):
    m = ADAM_B1 * m + (1.0 - ADAM_B1) * g
    v = ADAM_B2 * v + (1.0 - ADAM_B2) * (g * g)
    m_hat = m / (1.0 - ADAM_B1 ** ADAM_STEP)
    v_hat = v / (1.0 - ADAM_B2 ** ADAM_STEP)
    delta = -ADAM_LR * (m_hat / (jnp.sqrt(v_hat) + ADAM_EPS) + ADAM_WD * w)
    return delta, m, v


def _tile_rows(R, C):
    cap = max(8, (640 * 1024) // (4 * C))
    best = None
    for t in range(8, R + 1, 8):
        if R % t == 0 and t <= cap:
            best = t
    return R if best is None else best


def _adam_ada(ct, dm, w, m, v, name):
    L, D, nc = w.shape
    nb = ct.shape[1]
    tr = _tile_rows(D, nc)

    def body(ct_ref, dm_ref, w_ref, m_ref, v_ref, g_ref, d_ref, mo_ref, vo_ref):
        g = ct_ref[:, 0:1] * dm_ref[0:1, :]
        for b in range(1, nb):
            g = g + ct_ref[:, b:b + 1] * dm_ref[b:b + 1, :]
        g_ref[...] = g
        d_ref[...], mo_ref[...], vo_ref[...] = _adamw(w_ref[...], g, m_ref[...], v_ref[...])

    ws = pl.BlockSpec((None, tr, nc), lambda l, r: (l, r, 0))
    return _pcall(body, grid=(L, D // tr),
                  in_specs=[pl.BlockSpec((tr, nb), lambda l, r: (r, 0)), pl.BlockSpec((None, nb, nc), lambda l, r: (l, 0, 0)),
                            ws, ws, ws],
                  out_specs=[ws] * 4, out_shape=[_sds(w.shape, F32)] * 4, name=name,
                  compiler_params=_params(("parallel", "parallel")))(ct, dm, w, m, v)


def _adam_small(parts, w, m, v, name):
    n, R, C = parts.shape
    tr = _tile_rows(R, C * n // 2)

    def body(p_ref, w_ref, m_ref, v_ref, g_ref, d_ref, mo_ref, vo_ref):
        g = p_ref[0]
        for j in range(1, n):
            g = g + p_ref[j]
        g_ref[...] = g
        d_ref[...], mo_ref[...], vo_ref[...] = _adamw(w_ref[...], g, m_ref[...], v_ref[...])

    ws = pl.BlockSpec((tr, C), lambda r: (r, 0))
    return _pcall(body, grid=(R // tr,), in_specs=[pl.BlockSpec((n, tr, C), lambda r: (0, r, 0)), ws, ws, ws],
                  out_specs=[ws] * 4, out_shape=[_sds((R, C), F32)] * 4, name=name,
                  compiler_params=_params(("parallel",)))(parts, w, m, v)


def _adam_plain(g, w, m, v, name):
    R, C = w.shape

    def body(g_ref, w_ref, m_ref, v_ref, d_ref, mo_ref, vo_ref):
        d_ref[...], mo_ref[...], vo_ref[...] = _adamw(w_ref[...], g_ref[...], m_ref[...], v_ref[...])

    ws = _const_spec((R, C))
    return _pcall(body, grid=(1,), in_specs=[ws] * 4, out_specs=[ws] * 3, out_shape=[_sds((R, C), F32)] * 3, name=name,
                  compiler_params=_params(("arbitrary",)))(g, w, m, v)


def _pair_sum(G, R1, my_c, name):
    L, n, R, C = G.shape
    half = n // 2
    tr = _tile_rows(R, C)

    def body(c_ref, g_ref, r_ref, o_ref):
        o_ref[...] = g_ref[...] + r_ref[...]

    blk = (None, None, tr, C)
    gs = pltpu.PrefetchScalarGridSpec(
        num_scalar_prefetch=1, grid=(L, half, R // tr),
        in_specs=[pl.BlockSpec(blk, lambda l, p, r, c: (l, 2 * p + c[0], r, 0)),
                  pl.BlockSpec(blk, lambda l, p, r, c: (l, p, r, 0))],
        out_specs=pl.BlockSpec(blk, lambda l, p, r, c: (l, p, r, 0)))
    return _pcall(body, grid_spec=gs, out_shape=_sds((L, half, R, C), F32), name=name,
                  compiler_params=_params(("parallel", "parallel", "parallel")))(my_c, G, R1)


def _adam_big(P, R2, my_chip, w, m, v, name):
    L, _, R, C = P.shape
    nrecv = R2.shape[0]
    tr = _tile_rows(R, C)

    def body(p_sm, p_ref, r_ref, w_ref, m_ref, v_ref, g_ref, d_ref, mo_ref, vo_ref):
        g = p_ref[...]
        for k in range(nrecv):
            g = g + r_ref[k]
        g_ref[...] = g
        d_ref[...], mo_ref[...], vo_ref[...] = _adamw(w_ref[...], g, m_ref[...], v_ref[...])

    ws = pl.BlockSpec((None, tr, C), lambda l, r, p: (l, r, 0))
    gs = pltpu.PrefetchScalarGridSpec(
        num_scalar_prefetch=1, grid=(L, R // tr),
        in_specs=[pl.BlockSpec((None, None, tr, C), lambda l, r, p: (l, p[0], r, 0)),
                  pl.BlockSpec((nrecv, None, tr, C), lambda l, r, p: (0, l, r, 0)), ws, ws, ws],
        out_specs=[ws] * 4)
    return _pcall(body, grid_spec=gs, out_shape=[_sds((L, R, C), F32)] * 4, name=name,
                  compiler_params=_params(("parallel", "parallel")))(my_chip, P, R2, w, m, v)


def _place():
    return lax.axis_index("x"), lax.axis_index("y"), lax.axis_index("c")


def _sum_over_devices(scalar):
    return lax.psum(scalar, ("x", "y", "c"))


def _all_gather(shards, name):
    n = len(shards)

    def body(*refs):
        ins, outs = refs[:n], refs[n:2 * n]
        send_sems, recv_sems, local_sems = refs[2 * n:]
        x, y, c = _place()
        me, sibling = (x, y, c), (x, y, 1 - c)
        chips = [(1 - x, y), (x, 1 - y), (1 - x, 1 - y)]

        def slot(a, px, py, pc):
            return outs[a].at[4 * px + 2 * py + pc]

        def copy(a, k, block, to, src=None):
            return pltpu.make_async_remote_copy(
                src_ref=slot(a, *block) if src is None else src, dst_ref=slot(a, *block),
                send_sem=send_sems.at[7 * a + k], recv_sem=recv_sems.at[7 * a + k], device_id=to, device_id_type=MESH)

        mine = [pltpu.make_async_copy(ins[a], slot(a, *me), local_sems.at[a]) for a in range(n)]
        for cp in mine:
            cp.start()
        first = []
        for a in range(n):
            first.append(copy(a, 0, me, sibling, src=ins[a]))
            first += [copy(a, 1 + j, me, (*chip, c), src=ins[a]) for j, chip in enumerate(chips)]
        for cp in first:
            cp.start()
        passed = []
        for j, chip in enumerate(chips):
            for a in range(n):
                copy(a, 1 + j, (*chip, c), me).wait_recv()
                fwd = copy(a, 4 + j, (*chip, c), sibling)
                fwd.start()
                passed.append(fwd)
        for a in range(n):
            copy(a, 0, sibling, me).wait_recv()
        for j, chip in enumerate(chips):
            for a in range(n):
                copy(a, 4 + j, (*chip, 1 - c), me).wait_recv()
        for cp in first + passed:
            cp.wait_send()
        for cp in mine:
            cp.wait()

    outs = _pcall(body, in_specs=[ANY] * n, out_specs=[ANY] * n,
                  out_shape=[_sds((NDEV,) + s.shape, s.dtype) for s in shards],
                  scratch_shapes=[pltpu.SemaphoreType.DMA((7 * n,)), pltpu.SemaphoreType.DMA((7 * n,)),
                                  pltpu.SemaphoreType.DMA((n,))], name=name)(*shards)
    return list(outs)


def _rs_sibling(Gs, name):
    n = len(Gs)
    L = Gs[0].shape[0]
    per = L * NCHIP

    def body(*refs):
        ins, outs = refs[:n], refs[n:2 * n]
        send_sems, recv_sems = refs[2 * n:]
        x, y, c = _place()
        copies = []
        for a in range(n):
            for l in range(L):
                for p in range(NCHIP):
                    k = a * per + l * NCHIP + p
                    copies.append(pltpu.make_async_remote_copy(
                        src_ref=ins[a].at[l, 2 * p + 1 - c], dst_ref=outs[a].at[l, p], send_sem=send_sems.at[k],
                        recv_sem=recv_sems.at[k], device_id=(x, y, 1 - c), device_id_type=MESH))
        for cp in copies:
            cp.start()
        for cp in copies:
            cp.wait()

    outs = _pcall(body, in_specs=[ANY] * n, out_specs=[ANY] * n,
                  out_shape=[_sds((L, NCHIP) + g.shape[2:], g.dtype) for g in Gs],
                  scratch_shapes=[pltpu.SemaphoreType.DMA((n * per,)), pltpu.SemaphoreType.DMA((n * per,))],
                  name=name)(*Gs)
    return list(outs)


def _rs_chips(Ps, name):
    n = len(Ps)
    L = Ps[0].shape[0]
    per = 3 * L

    def body(*refs):
        ins, outs = refs[:n], refs[n:2 * n]
        send_sems, recv_sems = refs[2 * n:]
        x, y, c = _place()
        chips = [(1 - x, y), (x, 1 - y), (1 - x, 1 - y)]
        copies = []
        for a in range(n):
            for j, (px, py) in enumerate(chips):
                for l in range(L):
                    k = a * per + j * L + l
                    copies.append(pltpu.make_async_remote_copy(
                        src_ref=ins[a].at[l, 2 * px + py], dst_ref=outs[a].at[j, l], send_sem=send_sems.at[k],
                        recv_sem=recv_sems.at[k], device_id=(px, py, c), device_id_type=MESH))
        for cp in copies:
            cp.start()
        for cp in copies:
            cp.wait()

    outs = _pcall(body, in_specs=[ANY] * n, out_specs=[ANY] * n,
                  out_shape=[_sds((3, L) + p.shape[2:], p.dtype) for p in Ps],
                  scratch_shapes=[pltpu.SemaphoreType.DMA((n * per,)), pltpu.SemaphoreType.DMA((n * per,))],
                  name=name)(*Ps)
    return list(outs)


def _reduce_scatter_adam(Gs, shards, my_c, my_chip):
    R1s = _rs_sibling(Gs, "rs_sibling")
    Ps = [_pair_sum(g, r1, my_c, f"pair_sum{a}") for a, (g, r1) in enumerate(zip(Gs, R1s))]
    R2s = _rs_chips(Ps, "rs_chips")
    return [_adam_big(p, r2, my_chip, *wmv, name=f"adam_big{a}") for a, (p, r2, wmv) in enumerate(zip(Ps, R2s, shards))]


SMALL_ROWS = {"norm1_g": (0, 1), "norm2_g": (1, 1), "sgu_ln_g": (2, 1), "sgu_ln_b": (3, 1), "cfm_conv_b": (4, 1),
              "cfm_ln_g": (5, 1), "cfm_ln_b": (6, 1), "b_sgu": (7, 1), "w_sgu": (8, 128), "b_ada": (136, N_MOD),
              "w_short": (142, SHORT_K), "cfm_conv_w": (145, CFM_K)}
ROWS_PER_LAYER = 176
FINAL_ROW = DEPTH * ROWS_PER_LAYER
PACK_ROWS = 360


def _pack(get, D):
    parts = []
    for l in range(DEPTH):
        for name, (_, nrows) in SMALL_ROWS.items():
            a = get(name, l)
            parts.append(jnp.zeros((nrows, D), F32) if a is None else a.astype(F32).reshape(nrows, D))
    fin = get("final_g", None)
    parts.append(fin.astype(F32).reshape(1, D))
    parts.append(jnp.zeros((PACK_ROWS - FINAL_ROW - 1, D), F32))
    return jnp.concatenate(parts, axis=0)


def _unpack(pack, name, shape):
    D = pack.shape[1]
    r0, nrows = SMALL_ROWS[name]
    return jnp.stack([pack[l * ROWS_PER_LAYER + r0:l * ROWS_PER_LAYER + r0 + nrows] for l in range(DEPTH)]).reshape(shape)


def _mm_tiles(S):
    return min(512, S), min(1024, S)


def kernel(x, c, w_ada, b_ada, norm1_g, w_in, w_short, w_a_out, sgu_ln_g, sgu_ln_b, w_sgu, b_sgu, w_b_out, cfm_conv_w, cfm_conv_b, cfm_ln_g, cfm_ln_b, w_c_out, w_o, norm2_g, w_ffn_in, w_ffn_out, final_g, loss_target, m_w_ada, m_b_ada, m_norm1_g, m_w_in, m_w_short, m_w_a_out, m_sgu_ln_g, m_sgu_ln_b, m_w_sgu, m_b_sgu, m_w_b_out, m_cfm_conv_w, m_cfm_conv_b, m_cfm_ln_g, m_cfm_ln_b, m_w_c_out, m_w_o, m_norm2_g, m_w_ffn_in, m_w_ffn_out, m_final_g, v_w_ada, v_b_ada, v_norm1_g, v_w_in, v_w_short, v_w_a_out, v_sgu_ln_g, v_sgu_ln_b, v_w_sgu, v_b_sgu, v_w_b_out, v_cfm_conv_w, v_cfm_conv_b, v_cfm_ln_g, v_cfm_ln_b, v_w_c_out, v_w_o, v_norm2_g, v_w_ffn_in, v_w_ffn_out, v_final_g):
    W = dict(w_ada=w_ada, b_ada=b_ada, norm1_g=norm1_g, w_in=w_in, w_short=w_short, w_a_out=w_a_out, sgu_ln_g=sgu_ln_g,
             sgu_ln_b=sgu_ln_b, w_sgu=w_sgu, b_sgu=b_sgu, w_b_out=w_b_out, cfm_conv_w=cfm_conv_w, cfm_conv_b=cfm_conv_b,
             cfm_ln_g=cfm_ln_g, cfm_ln_b=cfm_ln_b, w_c_out=w_c_out, w_o=w_o, norm2_g=norm2_g, w_ffn_in=w_ffn_in,
             w_ffn_out=w_ffn_out, final_g=final_g)
    Mo = dict(w_ada=m_w_ada, b_ada=m_b_ada, norm1_g=m_norm1_g, w_in=m_w_in, w_short=m_w_short, w_a_out=m_w_a_out,
              sgu_ln_g=m_sgu_ln_g, sgu_ln_b=m_sgu_ln_b, w_sgu=m_w_sgu, b_sgu=m_b_sgu, w_b_out=m_w_b_out,
              cfm_conv_w=m_cfm_conv_w, cfm_conv_b=m_cfm_conv_b, cfm_ln_g=m_cfm_ln_g, cfm_ln_b=m_cfm_ln_b,
              w_c_out=m_w_c_out, w_o=m_w_o, norm2_g=m_norm2_g, w_ffn_in=m_w_ffn_in, w_ffn_out=m_w_ffn_out,
              final_g=m_final_g)
    Vo = dict(w_ada=v_w_ada, b_ada=v_b_ada, norm1_g=v_norm1_g, w_in=v_w_in, w_short=v_w_short, w_a_out=v_w_a_out,
              sgu_ln_g=v_sgu_ln_g, sgu_ln_b=v_sgu_ln_b, w_sgu=v_w_sgu, b_sgu=v_b_sgu, w_b_out=v_w_b_out,
              cfm_conv_w=v_cfm_conv_w, cfm_conv_b=v_cfm_conv_b, cfm_ln_g=v_cfm_ln_g, cfm_ln_b=v_cfm_ln_b,
              w_c_out=v_w_c_out, w_o=v_w_o, norm2_g=v_norm2_g, w_ffn_in=v_w_ffn_in, w_ffn_out=v_w_ffn_out,
              final_g=v_final_g)
    order = ["w_ada", "b_ada", "norm1_g", "w_in", "w_short", "w_a_out", "sgu_ln_g", "sgu_ln_b", "w_sgu", "b_sgu",
             "w_b_out", "cfm_conv_w", "cfm_conv_b", "cfm_ln_g", "cfm_ln_b", "w_c_out", "w_o", "norm2_g", "w_ffn_in",
             "w_ffn_out", "final_g"]

    S, D = x.shape[1], x.shape[2]
    F2 = w_ffn_in.shape[2] * NDEV
    FF = F2 // 2
    xi, yi, ci = _place()
    dev = 4 * xi + 2 * yi + ci
    my_c = jnp.reshape(ci, (1,)).astype(jnp.int32)
    my_chip = jnp.reshape(2 * xi + yi, (1,)).astype(jnp.int32)
    tm, tm_big = _mm_tiles(S)
    x0 = x.reshape(S, D)
    tgt = loss_target.reshape(S, D)

    c_all = _all_gather([jnp.pad(c, ((0, 7), (0, 0)))], "ag_c")[0][:, 0, :]
    modpart, c_act = _ada_fwd(c_all, w_ada, "ada_fwd")
    ncol = modpart.shape[2]
    mg = _all_gather([modpart.reshape(DEPTH * NDEV, ncol)], "ag_mod")[0].reshape(NDEV, DEPTH, NDEV, ncol)
    mine = lax.dynamic_index_in_dim(mg, dev, axis=2, keepdims=False)
    mod = (jnp.transpose(mine, (1, 0, 2)).reshape(DEPTH, N_MOD * D) + b_ada).reshape(DEPTH, N_MOD, D)

    tril = jnp.tril(jnp.ones((CHUNK, CHUNK), dtype=bool))

    def layer_consts(l):
        wt = jnp.where(tril[None], w_sgu[l], 0.0).astype(BF16)
        return dict(
            wsh=jnp.pad(w_short_full[l], ((0, 8 - SHORT_K), (0, 0))),
            sgu_ln=_rows(sgu_ln_g[l], sgu_ln_b[l]),
            wtril=wt, wtril_t=jnp.swapaxes(wt, 1, 2),
            bias_full=jnp.repeat(b_sgu[l].T, LANE, axis=1),
            cw=jnp.pad(cfm_w_full[l], ((0, HALO - CFM_K), (0, 0))),
            cvec=_rows(cfm_conv_b[l], cfm_ln_g[l], cfm_ln_b[l]))

    ncs = w_short.shape[2]
    sw = _all_gather([w_short.reshape(DEPTH * SHORT_K, ncs), cfm_conv_w.reshape(DEPTH * CFM_K, ncs)], "ag_convw")
    w_short_full = jnp.transpose(sw[0], (1, 0, 2)).reshape(DEPTH, SHORT_K, D)
    cfm_w_full = jnp.transpose(sw[1], (1, 0, 2)).reshape(DEPTH, CFM_K, D)

    def gather_layer(l):
        g = _all_gather([w_in[l].astype(BF16), w_a_out[l].astype(BF16), w_b_out[l].astype(BF16),
                         w_c_out[l].astype(BF16), w_o[l].astype(BF16), w_ffn_in[l].astype(BF16),
                         w_ffn_out[l].astype(BF16)], f"ag_w{l}")
        return dict(w_in=g[0], w_a=g[1].reshape(1, D, D), w_b=g[2].reshape(1, D, D), w_c=g[3].reshape(1, D, D),
                    w_o=g[4].reshape(1, D, D), w_fi=jnp.transpose(g[5], (1, 0, 2)).reshape(1, D, F2),
                    w_fo=g[6].reshape(1, FF, D))

    Wg = [gather_layer(l) for l in range(DEPTH)]
    nin = Wg[0]["w_in"].shape[2]
    tn_in = nin if nin % 256 == 0 and nin <= 1280 else 256
    tn_fi = 512 if F2 % 512 == 0 else 256

    saved = []
    xcur, fprev, gprev = x0, None, None
    for l in range(DEPTH):
        sh1, sc1, g1, sh2, sc2, g2 = [mod[l, k] for k in range(N_MOD)]
        wl, cl = Wg[l], layer_consts(l)
        vec1 = _rows(jnp.zeros((D,), F32) if gprev is None else gprev, norm1_g[l], sc1, sh1)
        xl, h = _norm_fwd(xcur, fprev, vec1, f"norm1_fwd{l}")
        z = _mm_nn(h, wl["w_in"], F32, tm, tn_in, D, f"mm_in{l}", w_outer=True)
        acts = _mixer_fwd(z, cl["wsh"], cl["sgu_ln"], cl["wtril"], cl["bias_full"], cl["cw"], cl["cvec"], f"mixer_fwd{l}")
        merged, ys = _branch_out(acts, [wl["w_a"][0], wl["w_b"][0], wl["w_c"][0]], z, f"branch_out{l}")
        o = _mm_nn(merged, wl["w_o"], F32, tm, D, D, f"mm_o{l}")
        x1, h2 = _norm_fwd(xl, o, _rows(g1, norm2_g[l], sc2, sh2), f"norm2_fwd{l}")
        gu = _mm_nn(h2, wl["w_fi"], F32, tm_big, tn_fi, D, f"mm_ffn_in{l}")
        act = _swiglu_fwd(gu, f"swiglu_fwd{l}")
        f = _mm_nn(act, wl["w_fo"], F32, tm, D, 256, f"mm_ffn_out{l}")
        saved.append(dict(xl=xl, h=h, z=z, acts=acts, ys=ys, merged=merged, o=o, x1=x1, h2=h2, gu=gu, act=act, f=f,
                          consts=cl, mod=(sh1, sc1, g1, sh2, sc2, g2)))
        xcur, fprev, gprev = x1, f, g2

    last = saved[-1]
    dxup, dfb, fsums, loss_blk = _final_bwd(last["x1"], last["f"], tgt, _rows(last["mod"][5], final_g), "final_bwd")
    loss = _sum_over_devices(loss_blk[0, 0])
    dgate2_next = fsums[1]
    G = dict(w_in=None, w_a=None, w_b=None, w_c=None, w_o=None, w_fi=None, w_fo=None)
    small = [dict() for _ in range(DEPTH)]
    dmods = [None] * DEPTH
    for l in reversed(range(DEPTH)):
        sv, wl, cl = saved[l], Wg[l], saved[l]["consts"]
        sh1, sc1, g1, sh2, sc2, g2 = sv["mod"]
        dact = _mm_nt(dfb, wl["w_fo"], F32, tm, 256, D, f"mm_dact{l}")
        G["w_fo"] = _mm_tn(sv["act"], dfb, 1, l, G["w_fo"], 256, D, tm, f"mm_dw_ffn_out{l}")
        dgu = _swiglu_bwd(dact, sv["gu"], f"swiglu_bwd{l}")
        dh2 = _mm_nt(dgu, wl["w_fi"], F32, tm_big, D, tn_fi, f"mm_dh2{l}")
        G["w_fi"] = _mm_tn(sv["h2"], dgu, 1, l, G["w_fi"], D, tn_fi, tm, f"mm_dw_ffn_in{l}")
        dx1, dob, s2 = _norm_bwd(sv["x1"], dh2, dxup, _rows(norm2_g[l], sc2, g1), sv["o"], f"norm2_bwd{l}")
        dmerged = _mm_nt(dob, wl["w_o"], F32, tm, D, D, f"mm_dmerged{l}")
        G["w_o"] = _mm_tn(sv["merged"], dob, 1, l, G["w_o"], D, D, tm, f"mm_dw_o{l}")
        dys, dz = _gate_bwd(dmerged, sv["z"], sv["ys"], f"gate_bwd{l}")
        dacts = []
        for n, key in enumerate(("w_a", "w_b", "w_c")):
            dacts.append(_mm_nt(dys[n], wl[key], F32, tm, D, D, f"mm_dact_{key}{l}"))
            G[key] = _mm_tn(sv["acts"][n], dys[n], 1, l, G[key], D, D, tm, f"mm_d{key}{l}")
        dz, mvec, dcw, dws, dbs = _mixer_bwd(sv["z"], dacts, dz, cl["wsh"], cl["sgu_ln"], cl["wtril"], cl["wtril_t"],
                                             cl["bias_full"], cl["cw"], cl["cvec"], f"mixer_bwd{l}")
        dh = _mm_nt(dz, wl["w_in"], F32, tm_big, D, tn_in, f"mm_dh{l}")
        G["w_in"] = _mm_tn(sv["h"], dz, NDEV, l, G["w_in"], D, tn_in, tm, f"mm_dw_in{l}")
        if l > 0:
            pv = saved[l - 1]
            dxup, dfb, s1 = _norm_bwd(sv["xl"], dh, dx1, _rows(norm1_g[l], sc1, pv["mod"][5]), pv["f"], f"norm1_bwd{l}")
        else:
            dxup, dfb, s1 = _norm_bwd(sv["xl"], dh, dx1, _rows(norm1_g[l], sc1), None, f"norm1_bwd{l}")
        dmods[l] = jnp.stack([s1[0], s1[1], s2[3], s2[0], s2[1], dgate2_next])
        dgate2_next = s1[3]
        small[l] = dict(norm1_g=s1[2], norm2_g=s2[2], sgu_ln_g=mvec[3], sgu_ln_b=mvec[4], cfm_conv_b=mvec[5],
                        cfm_ln_g=mvec[6], cfm_ln_b=mvec[7], b_sgu=dbs[:, :, 0],
                        w_sgu=jnp.where(tril[None], dws, 0.0), b_ada=dmods[l], w_short=mvec[0:SHORT_K],
                        cfm_conv_w=dcw[0:CFM_K])
    grad_x = dxup.reshape(x.shape)

    gpack = _pack(lambda name, l: fsums[0] if name == "final_g" else small[l][name], D)
    gathered = _all_gather([gpack], "ag_small")[0]
    sharded_small = ("w_short", "cfm_conv_w")
    packs = [_pack(lambda name, l, T=T: T["final_g"] if name == "final_g" else (None if name in sharded_small else T[name][l]), D)
             for T in (W, Mo, Vo)]
    sg, sd, sm, sv_ = _adam_small(gathered, *packs, name="adam_small")
    out = {}
    for name in order:
        if name in SMALL_ROWS and name not in sharded_small:
            out[name] = tuple(_unpack(p, name, W[name].shape) for p in (sg, sd, sm, sv_))
    out["final_g"] = tuple(p[FINAL_ROW] for p in (sg, sd, sm, sv_))

    def my_cols(name):
        full = _unpack(sg, name, (DEPTH, SMALL_ROWS[name][1], D))
        return lax.dynamic_slice_in_dim(full, dev * ncs, ncs, axis=2)

    gcs = jnp.concatenate([my_cols("w_short").reshape(-1, ncs), my_cols("cfm_conv_w").reshape(-1, ncs)])
    ncr = gcs.shape[0]
    padr = (-ncr) % 8
    cat = lambda T: jnp.pad(jnp.concatenate([T["w_short"].reshape(-1, ncs), T["cfm_conv_w"].reshape(-1, ncs)]), ((0, padr), (0, 0)))
    cd, cm, cv = _adam_plain(jnp.pad(gcs, ((0, padr), (0, 0))), cat(W), cat(Mo), cat(Vo), "adam_convw")
    nsh = DEPTH * SHORT_K
    out["w_short"] = tuple(a[0:nsh].reshape(w_short.shape) for a in (gcs, cd, cm, cv))
    out["cfm_conv_w"] = tuple(a[nsh:ncr].reshape(cfm_conv_w.shape) for a in (gcs, cd, cm, cv))

    dm_all = jnp.stack([gathered[:, l * ROWS_PER_LAYER + 136:l * ROWS_PER_LAYER + 136 + N_MOD, :].reshape(NDEV, N_MOD * D)
                        for l in range(DEPTH)])
    dm_mine = lax.dynamic_slice_in_dim(dm_all, dev * ncol, ncol, axis=2)
    out["w_ada"] = tuple(_adam_ada(jnp.transpose(c_act), dm_mine, w_ada, m_w_ada, v_w_ada, "adam_ada"))

    nfi = w_ffn_in.shape[2]
    Gs = [G["w_in"], G["w_a"].reshape(DEPTH, NDEV, D // NDEV, D), G["w_b"].reshape(DEPTH, NDEV, D // NDEV, D),
          G["w_c"].reshape(DEPTH, NDEV, D // NDEV, D), G["w_o"].reshape(DEPTH, NDEV, D // NDEV, D),
          jnp.transpose(G["w_fi"].reshape(DEPTH, D, NDEV, nfi), (0, 2, 1, 3)),
          G["w_fo"].reshape(DEPTH, NDEV, FF // NDEV, D)]
    big = ["w_in", "w_a_out", "w_b_out", "w_c_out", "w_o", "w_ffn_in", "w_ffn_out"]
    res = _reduce_scatter_adam(Gs, [(W[n], Mo[n], Vo[n]) for n in big], my_c, my_chip)
    for n, r in zip(big, res):
        out[n] = tuple(r)

    grads = [out[n][0] for n in order]
    deltas = [out[n][1] for n in order]
    new_m = [out[n][2] for n in order]
    new_v = [out[n][3] for n in order]
    return (loss, grad_x, *grads, *deltas, *new_m, *new_v)
```

```python
import functools
import math

import jax
import jax.numpy as jnp
from jax import lax
from jax.experimental import pallas as pl
from jax.experimental.pallas import tpu as pltpu

F32, BF16 = jnp.float32, jnp.bfloat16
NDEV = 8
NCHIP = NDEV // 2
DEPTH = 2
EPS = 1e-6
CHUNK = 128
NG = 8
SHORT_K = 3
CFM_K = 31
HALO = 32
N_MOD = 6
LANE = 128
VMEM_LIMIT = 56 * 1024 * 1024
ADAM_LR, ADAM_B1, ADAM_B2, ADAM_EPS, ADAM_WD, ADAM_STEP = 0.001, 0.9, 0.999, 1e-08, 0.01, 10
_G0 = math.sqrt(2.0 / math.pi)
_G1 = 0.044715
MESH = pl.DeviceIdType.MESH
ANY = pl.BlockSpec(memory_space=pl.ANY)


def _pcall(body, **kw):
    return pl.pallas_call(body, **kw)


def _params(sem=None):
    return pltpu.CompilerParams(dimension_semantics=sem, vmem_limit_bytes=VMEM_LIMIT)


def _sds(shape, dtype):
    return jax.ShapeDtypeStruct(tuple(shape), dtype)


def _mm_body(dims, nk, out_f32):
    def body(a_ref, b_ref, o_ref, *scr):
        k = pl.program_id(2)
        part = lax.dot_general(a_ref[...], b_ref[...], dims, preferred_element_type=F32)
        if nk == 1:
            o_ref[...] = part.reshape(o_ref.shape).astype(o_ref.dtype)
        elif out_f32:
            @pl.when(k == 0)
            def _():
                o_ref[...] = part.reshape(o_ref.shape)

            @pl.when(k > 0)
            def _():
                o_ref[...] += part.reshape(o_ref.shape)
        else:
            acc = scr[0]

            @pl.when(k == 0)
            def _():
                acc[...] = part

            @pl.when(k > 0)
            def _():
                acc[...] += part

            @pl.when(k == nk - 1)
            def _():
                o_ref[...] = acc[...].astype(o_ref.dtype)
    return body


def _mm_call(body, grid, in_specs, out_spec, out_shape, acc_shape, name, alias=None):
    scratch = [] if acc_shape is None else [pltpu.VMEM(acc_shape, F32)]
    kw = {}
    if alias is not None:
        in_specs = in_specs + [ANY]
        kw["input_output_aliases"] = {2: 0}
    return _pcall(body, grid=grid, in_specs=in_specs, out_specs=out_spec, out_shape=out_shape,
                  scratch_shapes=scratch, name=name,
                  compiler_params=_params(("parallel", "parallel", "arbitrary")), **kw)


def _mm_nn(a, b3, out_dtype, tm, tn, tk, name, w_outer=False):
    M, K = a.shape
    G, _, Nb = b3.shape
    npb, nk = Nb // tn, K // tk
    out_f32 = out_dtype == F32
    body = _mm_body((((1,), (0,)), ((), ())), nk, out_f32)
    if w_outer:
        grid = (G * npb, M // tm, nk)
        ij = lambda p, q: (q, p)
    else:
        grid = (M // tm, G * npb, nk)
        ij = lambda p, q: (p, q)

    def a_map(p, q, k):
        i, j = ij(p, q)
        return (i, k)

    def b_map(p, q, k):
        i, j = ij(p, q)
        return (j // npb, k, j % npb)

    def o_map(p, q, k):
        return ij(p, q)

    def wrapped(a_ref, b_ref, o_ref, *scr):
        body(a_ref, b_ref, o_ref, *scr)

    return _mm_call(wrapped, grid, [pl.BlockSpec((tm, tk), a_map), pl.BlockSpec((None, tk, tn), b_map)],
                    pl.BlockSpec((tm, tn), o_map), _sds((M, G * Nb), out_dtype),
                    None if (nk == 1 or out_f32) else (tm, tn), name)(a, b3)


def _mm_nt(a, b3, out_dtype, tm, tn, tk, name):
    M, _ = a.shape
    G, Ko, Nb = b3.shape
    kpb = Nb // tk
    nk = G * kpb
    out_f32 = out_dtype == F32
    body = _mm_body((((1,), (1,)), ((), ())), nk, out_f32)

    def wrapped(a_ref, b_ref, o_ref, *scr):
        body(a_ref, b_ref, o_ref, *scr)

    return _mm_call(wrapped, (M // tm, Ko // tn, nk),
                    [pl.BlockSpec((tm, tk), lambda i, j, k: (i, k)),
                     pl.BlockSpec((None, tn, tk), lambda i, j, k: (k // kpb, j, k % kpb))],
                    pl.BlockSpec((tm, tn), lambda i, j, k: (i, j)), _sds((M, Ko), out_dtype),
                    None if (nk == 1 or out_f32) else (tm, tn), name)(a, b3)


def _mm_tn(a, b, G, layer, prev, tm, tn, tk, name):
    T, M = a.shape
    Nb = b.shape[1] // G
    npb, nk = Nb // tn, T // tk
    body = _mm_body((((0,), (0,)), ((), ())), nk, False)

    def wrapped(a_ref, b_ref, *rest):
        if prev is None:
            o_ref, scr = rest[0], rest[1:]
        else:
            o_ref, scr = rest[1], rest[2:]
        body(a_ref, b_ref, o_ref, *scr)

    in_specs = [pl.BlockSpec((tk, tm), lambda i, j, k: (k, i)), pl.BlockSpec((tk, tn), lambda i, j, k: (k, j))]
    out_spec = pl.BlockSpec((None, None, tm, tn), lambda i, j, k: (layer, j // npb, i, j % npb))
    call = _mm_call(wrapped, (M // tm, G * npb, nk), in_specs, out_spec, _sds((DEPTH, G, M, Nb), BF16),
                    None if nk == 1 else (tm, tn), name, alias=prev)
    return call(a, b) if prev is None else call(a, b, prev)


def _rsum(v):
    return jnp.sum(v, axis=0, keepdims=True)


def _rmean(v):
    return jnp.mean(v, axis=-1, keepdims=True)


def _gelu(x):
    t = jnp.tanh(_G0 * (x + _G1 * (x * x * x)))
    return x * (0.5 * (1.0 + t)), t


def _dgelu(x, t):
    return 0.5 * (1.0 + t) + 0.5 * x * (1.0 - t * t) * (_G0 * (1.0 + 3.0 * _G1 * (x * x)))


def _sigmoid(x):
    return 1.0 / (1.0 + jnp.exp(-x))


def _fill_shifted(ext, rot):
    v = ext[...]
    n = v.shape[0]
    for b in range(1, 8):
        rot[b - 1] = pltpu.roll(v, n - b, 0)


def _rows_at(ext, rot, s, tm):
    a, b = divmod(s, 8)
    return ext[8 * a:8 * a + tm, :] if b == 0 else rot[b - 1, 8 * a:8 * a + tm, :]


def _rows(*vs):
    a = jnp.stack([v.astype(F32) for v in vs])
    return jnp.pad(a, ((0, 8 - len(vs)), (0, 0)))


def _row_spec(tm, D):
    return pl.BlockSpec((tm, D), lambda i: (i, 0))


def _const_spec(shape):
    nd = len(shape)
    return pl.BlockSpec(shape, lambda i: (0,) * nd)


def _norm_fwd(xp, f, vec, name):
    S, D = xp.shape
    tm = min(256, S)
    has_f = f is not None

    def body(*refs):
        if has_f:
            xp_ref, f_ref, vec_ref, xo_ref, h_ref = refs
            x = xp_ref[...] + vec_ref[0:1, :] * f_ref[...]
            xo_ref[...] = x
        else:
            xp_ref, vec_ref, h_ref = refs
            x = xp_ref[...]
        r = lax.rsqrt(_rmean(x * x) + EPS)
        h = (x * r) * vec_ref[1:2, :]
        h_ref[...] = (h * (1.0 + vec_ref[2:3, :]) + vec_ref[3:4, :]).astype(BF16)

    rs = _row_spec(tm, D)
    ins = [xp, f, vec] if has_f else [xp, vec]
    in_specs = ([rs, rs] if has_f else [rs]) + [_const_spec((8, D))]
    out_shape = ([_sds((S, D), F32)] if has_f else []) + [_sds((S, D), BF16)]
    out_specs = [rs] * len(out_shape)
    outs = _pcall(body, grid=(S // tm,), in_specs=in_specs, out_specs=out_specs, out_shape=out_shape, name=name,
                  compiler_params=_params(("parallel",)))(*ins)
    return (outs[0], outs[1]) if has_f else (xp, outs[0])


def _mixer_fwd(z, wsh, sgu_ln, wtril, bias_full, cw, cvec, name):
    S = z.shape[0]
    D = wsh.shape[1]
    tm = CHUNK

    def body(z_ref, wsh_ref, sln_ref, wt_ref, bias_ref, cw_ref, cv_ref, oa_ref, ob_ref, oc_ref, pe, ge, gr):
        i = pl.program_id(0)

        @pl.when(i == 0)
        def _():
            pe[0:HALO, :] = jnp.zeros((HALO, D), F32)
            ge[0:HALO, :] = jnp.zeros((HALO, D), F32)

        pe[HALO:HALO + tm, :] = z_ref[:, D:2 * D] * z_ref[:, 2 * D:3 * D]
        q = wsh_ref[0:1, :] * pe[HALO - 2:HALO - 2 + tm, :]
        q = q + wsh_ref[1:2, :] * pe[HALO - 1:HALO - 1 + tm, :]
        q = q + wsh_ref[2:3, :] * pe[HALO:HALO + tm, :]
        oa_ref[...] = (z_ref[:, 0:D] * q).astype(BF16)
        gu, _ = _gelu(z_ref[:, 3 * D:4 * D])
        gv, _ = _gelu(z_ref[:, 4 * D:5 * D])
        d = gv - _rmean(gv)
        nrm = d * lax.rsqrt(_rmean(d * d) + EPS)
        vnb = (nrm * sln_ref[0:1, :] + sln_ref[1:2, :]).astype(BF16)
        for g in range(NG):
            cs = slice(g * LANE, (g + 1) * LANE)
            mixed = jnp.dot(wt_ref[g], vnb[:, cs], preferred_element_type=F32) + bias_ref[:, cs]
            ob_ref[:, cs] = (gu[:, cs] * mixed).astype(BF16)
        ge[HALO:HALO + tm, :] = z_ref[:, 5 * D:6 * D] * _sigmoid(z_ref[:, 6 * D:7 * D])
        _fill_shifted(ge, gr)
        o0 = HALO - (CFM_K - 1)
        conv = cv_ref[0:1, :] + cw_ref[0:1, :] * _rows_at(ge, gr, o0, tm)
        for k in range(1, CFM_K):
            conv = conv + cw_ref[k:k + 1, :] * _rows_at(ge, gr, o0 + k, tm)
        d = conv - _rmean(conv)
        ln = (d * lax.rsqrt(_rmean(d * d) + EPS)) * cv_ref[1:2, :] + cv_ref[2:3, :]
        oc_ref[...] = (ln * _sigmoid(ln)).astype(BF16)
        pe[0:HALO, :] = pe[tm:tm + HALO, :]
        ge[0:HALO, :] = ge[tm:tm + HALO, :]

    rs = _row_spec(tm, D)
    return _pcall(
        body, grid=(S // tm,),
        in_specs=[pl.BlockSpec((tm, 7 * D), lambda i: (i, 0)), _const_spec((8, D)), _const_spec((8, D)),
                  _const_spec((NG, CHUNK, CHUNK)), _const_spec((CHUNK, D)), _const_spec((HALO, D)), _const_spec((8, D))],
        out_specs=[rs, rs, rs], out_shape=[_sds((S, D), BF16)] * 3,
        scratch_shapes=[pltpu.VMEM((HALO + tm, D), F32), pltpu.VMEM((HALO + tm, D), F32),
                        pltpu.VMEM((7, HALO + tm, D), F32)],
        name=name, compiler_params=_params(("arbitrary",)))(z, wsh, sgu_ln, wtril, bias_full, cw, cvec)


def _branch_out(acts, ws, z, name):
    S, D = acts[0].shape
    tm = min(256, S)

    def body(a0, a1, a2, w0, w1, w2, g0, g1, g2, m_ref, y_ref):
        m = None
        for n, (a, w, g) in enumerate(((a0, w0, g0), (a1, w1, g1), (a2, w2, g2))):
            y = jnp.dot(a[...], w[...], preferred_element_type=F32)
            y_ref[n] = y.astype(BF16)
            t = _sigmoid(g[...]) * y
            m = t if m is None else m + t
        m_ref[...] = m.astype(BF16)

    rs = _row_spec(tm, D)
    gate_specs = [pl.BlockSpec((tm, D), functools.partial(lambda i, n: (i, 7 + n), n=n)) for n in range(3)]
    return _pcall(body, grid=(S // tm,),
                  in_specs=[rs, rs, rs] + [_const_spec((D, D))] * 3 + gate_specs,
                  out_specs=[rs, pl.BlockSpec((3, tm, D), lambda i: (0, i, 0))],
                  out_shape=[_sds((S, D), BF16), _sds((3, S, D), BF16)], name=name,
                  compiler_params=_params(("parallel",)))(*acts, *ws, z, z, z)


def _swiglu_fwd(gu, name):
    S, F2 = gu.shape
    F = F2 // 2
    tm = min(256, S)

    def body(g_ref, u_ref, o_ref):
        g = g_ref[...]
        o_ref[...] = ((g * _sigmoid(g)) * u_ref[...]).astype(BF16)

    return _pcall(body, grid=(S // tm,),
                  in_specs=[pl.BlockSpec((tm, F), lambda i: (i, 0)), pl.BlockSpec((tm, F), lambda i: (i, 1))],
                  out_specs=pl.BlockSpec((tm, F), lambda i: (i, 0)), out_shape=_sds((S, F), BF16), name=name,
                  compiler_params=_params(("parallel",)))(gu, gu)


def _swiglu_bwd(dact, gu, name):
    S, F2 = gu.shape
    F = F2 // 2
    tm = min(128, S)

    def body(d_ref, g_ref, u_ref, o_ref):
        g = g_ref[...]
        sg = _sigmoid(g)
        d = d_ref[...]
        o_ref[:, 0:F] = (d * u_ref[...] * (sg * (1.0 + g * (1.0 - sg)))).astype(BF16)
        o_ref[:, F:2 * F] = (d * (g * sg)).astype(BF16)

    return _pcall(body, grid=(S // tm,),
                  in_specs=[pl.BlockSpec((tm, F), lambda i: (i, 0)), pl.BlockSpec((tm, F), lambda i: (i, 0)),
                            pl.BlockSpec((tm, F), lambda i: (i, 1))],
                  out_specs=pl.BlockSpec((tm, F2), lambda i: (i, 0)), out_shape=_sds((S, F2), BF16), name=name,
                  compiler_params=_params(("parallel",)))(dact, gu, gu)


def _final_bwd(x1, f, tgt, vec, name):
    S, D = x1.shape
    tm = min(256, S)

    def body(x_ref, f_ref, t_ref, vec_ref, dx_ref, df_ref, sums_ref, loss_ref):
        @pl.when(pl.program_id(0) == 0)
        def _():
            sums_ref[...] = jnp.zeros_like(sums_ref)
            loss_ref[...] = jnp.zeros_like(loss_ref)

        gate, fg = vec_ref[0:1, :], vec_ref[1:2, :]
        fv = f_ref[...]
        x = x_ref[...] + gate * fv
        r = lax.rsqrt(_rmean(x * x) + EPS)
        xn = x * r
        diff = xn * fg - t_ref[...]
        per_tok = _rmean(diff * diff)
        loss_ref[...] += 0.5 * jnp.sum(per_tok, axis=0, keepdims=True)
        dy = diff * (1.0 / D)
        sums_ref[0:1, :] += _rsum(dy * xn)
        dxn = dy * fg
        dx = r * (dxn - xn * _rmean(dxn * xn))
        sums_ref[1:2, :] += _rsum(dx * fv)
        dx_ref[...] = dx
        df_ref[...] = (dx * gate).astype(BF16)

    rs = _row_spec(tm, D)
    return _pcall(body, grid=(S // tm,), in_specs=[rs, rs, rs, _const_spec((8, D))],
                  out_specs=[rs, rs, _const_spec((8, D)), _const_spec((8, LANE))],
                  out_shape=[_sds((S, D), F32), _sds((S, D), BF16), _sds((8, D), F32), _sds((8, LANE), F32)],
                  name=name, compiler_params=_params(("arbitrary",)))(x1, f, tgt, vec)


def _norm_bwd(xin, dh, dxup, vec, fprev, name):
    S, D = xin.shape
    tm = min(256, S)
    has_prev = fprev is not None

    def body(*refs):
        if has_prev:
            x_ref, dh_ref, up_ref, vec_ref, fp_ref, dx_ref, dp_ref, sums_ref = refs
        else:
            x_ref, dh_ref, up_ref, vec_ref, dx_ref, sums_ref = refs

        @pl.when(pl.program_id(0) == 0)
        def _():
            sums_ref[...] = jnp.zeros_like(sums_ref)

        g, scale = vec_ref[0:1, :], vec_ref[1:2, :]
        x = x_ref[...]
        r = lax.rsqrt(_rmean(x * x) + EPS)
        xn = x * r
        dhv = dh_ref[...]
        sums_ref[0:1, :] += _rsum(dhv)
        sums_ref[1:2, :] += _rsum(dhv * (xn * g))
        dm = dhv * (1.0 + scale)
        sums_ref[2:3, :] += _rsum(dm * xn)
        dxn = dm * g
        dx = up_ref[...] + r * (dxn - xn * _rmean(dxn * xn))
        dx_ref[...] = dx
        if has_prev:
            sums_ref[3:4, :] += _rsum(dx * fp_ref[...])
            dp_ref[...] = (dx * vec_ref[2:3, :]).astype(BF16)

    rs = _row_spec(tm, D)
    ins = [xin, dh, dxup, vec] + ([fprev] if has_prev else [])
    in_specs = [rs, rs, rs, _const_spec((8, D))] + ([rs] if has_prev else [])
    out_shape = [_sds((S, D), F32)] + ([_sds((S, D), BF16)] if has_prev else []) + [_sds((8, D), F32)]
    out_specs = [rs] + ([rs] if has_prev else []) + [_const_spec((8, D))]
    outs = _pcall(body, grid=(S // tm,), in_specs=in_specs, out_specs=out_specs, out_shape=out_shape, name=name,
                  compiler_params=_params(("arbitrary",)))(*ins)
    return (outs[0], outs[1], outs[2]) if has_prev else (outs[0], None, outs[1])


def _gate_bwd(dmerged, z, ys, name):
    S, D = dmerged.shape
    tm = min(256, S)
    ncol = z.shape[1] // D

    def body(dm_ref, g_ref, y_ref, dy_ref, dz_ref):
        sg = _sigmoid(g_ref[...])
        dm = dm_ref[...]
        dy_ref[...] = (dm * sg).astype(BF16)
        dz_ref[...] = (dm * y_ref[...].astype(F32) * (sg * (1.0 - sg))).astype(BF16)

    return _pcall(body, grid=(S // tm, 3),
                  in_specs=[pl.BlockSpec((tm, D), lambda i, n: (i, 0)), pl.BlockSpec((tm, D), lambda i, n: (i, 7 + n)),
                            pl.BlockSpec((None, tm, D), lambda i, n: (n, i, 0))],
                  out_specs=[pl.BlockSpec((None, tm, D), lambda i, n: (n, i, 0)),
                             pl.BlockSpec((tm, D), lambda i, n: (i, 7 + n))],
                  out_shape=[_sds((3, S, D), BF16), _sds((S, ncol * D), BF16)], name=name,
                  compiler_params=_params(("parallel", "arbitrary")))(dmerged, z, ys)


def _mixer_bwd(z, dacts, dz, wsh, sgu_ln, wtril, wtril_t, bias_full, cw, cvec, name):
    S = z.shape[0]
    D = wsh.shape[1]
    tm = CHUNK
    nt = S // tm
    hb = tm // HALO

    def body(zc, zp, da_ref, db_ref, dc_ref, wsh_ref, sln_ref, wt_ref, wtt_ref, bias_ref, cw_ref, cv_ref, _dz_in,
             dz_ref, vec_ref, dcw_ref, dws_ref, dbs_ref, pe, ge, dqe, dce, gr, dcr):
        i = pl.program_id(0)
        rb = nt - 1 - i

        @pl.when(i == 0)
        def _():
            vec_ref[...] = jnp.zeros_like(vec_ref)
            dcw_ref[...] = jnp.zeros_like(dcw_ref)
            dws_ref[...] = jnp.zeros_like(dws_ref)
            dbs_ref[...] = jnp.zeros_like(dbs_ref)
            dqe[tm:tm + HALO, :] = jnp.zeros((HALO, D), F32)
            dce[tm:tm + HALO, :] = jnp.zeros((HALO, D), F32)

        keep = (rb > 0).astype(F32)
        c_a, x_a = zc[:, D:2 * D], zc[:, 2 * D:3 * D]
        pe[0:HALO, :] = keep * (zp[:, D:2 * D] * zp[:, 2 * D:3 * D])
        pe[HALO:HALO + tm, :] = c_a * x_a
        q = wsh_ref[0:1, :] * pe[HALO - 2:HALO - 2 + tm, :]
        q = q + wsh_ref[1:2, :] * pe[HALO - 1:HALO - 1 + tm, :]
        q = q + wsh_ref[2:3, :] * pe[HALO:HALO + tm, :]
        dact = da_ref[...]
        dz_ref[:, 0:D] = (dact * q).astype(BF16)
        dq = dact * zc[:, 0:D]
        dqe[0:tm, :] = dq
        dp = wsh_ref[2:3, :] * dq + wsh_ref[1:2, :] * dqe[1:1 + tm, :] + wsh_ref[0:1, :] * dqe[2:2 + tm, :]
        dz_ref[:, D:2 * D] = (dp * x_a).astype(BF16)
        dz_ref[:, 2 * D:3 * D] = (dp * c_a).astype(BF16)
        for k in range(SHORT_K):
            o = HALO - (SHORT_K - 1) + k
            vec_ref[k:k + 1, :] += _rsum(dq * pe[o:o + tm, :])
        u, v = zc[:, 3 * D:4 * D], zc[:, 4 * D:5 * D]
        gu, tu = _gelu(u)
        gv, tv = _gelu(v)
        d = gv - _rmean(gv)
        rstd = lax.rsqrt(_rmean(d * d) + EPS)
        nrm = d * rstd
        vnb = (nrm * sln_ref[0:1, :] + sln_ref[1:2, :]).astype(BF16)
        dact = db_ref[...]
        dvn_parts, dgu_parts = [], []
        for g in range(NG):
            cs = slice(g * LANE, (g + 1) * LANE)
            vg = vnb[:, cs]
            mixed = jnp.dot(wt_ref[g], vg, preferred_element_type=F32) + bias_ref[:, cs]
            dgu_parts.append(dact[:, cs] * mixed)
            dmixed = dact[:, cs] * gu[:, cs]
            dmb = dmixed.astype(BF16)
            dws_ref[g] += lax.dot_general(dmb, vg, (((1,), (1,)), ((), ())), preferred_element_type=F32)
            dbs_ref[g] += jnp.broadcast_to(jnp.sum(dmixed, axis=1, keepdims=True), (CHUNK, LANE))
            dvn_parts.append(jnp.dot(wtt_ref[g], dmb, preferred_element_type=F32))
        dgu = jnp.concatenate(dgu_parts, axis=1)
        dvn = jnp.concatenate(dvn_parts, axis=1)
        dz_ref[:, 3 * D:4 * D] = (dgu * _dgelu(u, tu)).astype(BF16)
        vec_ref[3:4, :] += _rsum(dvn * nrm)
        vec_ref[4:5, :] += _rsum(dvn)
        dn = dvn * sln_ref[0:1, :]
        dgv = rstd * (dn - _rmean(dn) - nrm * _rmean(dn * nrm))
        dz_ref[:, 4 * D:5 * D] = (dgv * _dgelu(v, tv)).astype(BF16)
        a_c = zc[:, 5 * D:6 * D]
        sg = _sigmoid(zc[:, 6 * D:7 * D])
        ge[0:HALO, :] = keep * (zp[:, 5 * D:6 * D] * _sigmoid(zp[:, 6 * D:7 * D]))
        ge[HALO:HALO + tm, :] = a_c * sg
        _fill_shifted(ge, gr)
        o0 = HALO - (CFM_K - 1)
        conv = cv_ref[0:1, :] + cw_ref[0:1, :] * _rows_at(ge, gr, o0, tm)
        for k in range(1, CFM_K):
            conv = conv + cw_ref[k:k + 1, :] * _rows_at(ge, gr, o0 + k, tm)
        d = conv - _rmean(conv)
        rstd = lax.rsqrt(_rmean(d * d) + EPS)
        nrm = d * rstd
        ln = nrm * cv_ref[1:2, :] + cv_ref[2:3, :]
        sl = _sigmoid(ln)
        dln = dc_ref[...] * (sl * (1.0 + ln * (1.0 - sl)))
        vec_ref[6:7, :] += _rsum(dln * nrm)
        vec_ref[7:8, :] += _rsum(dln)
        dn = dln * cv_ref[1:2, :]
        dconv = rstd * (dn - _rmean(dn) - nrm * _rmean(dn * nrm))
        vec_ref[5:6, :] += _rsum(dconv)
        dce[0:tm, :] = dconv
        _fill_shifted(dce, dcr)
        dglu = cw_ref[CFM_K - 1:CFM_K, :] * dconv
        for k in range(CFM_K - 1):
            dglu = dglu + cw_ref[k:k + 1, :] * _rows_at(dce, dcr, CFM_K - 1 - k, tm)
        for k in range(CFM_K):
            dcw_ref[k:k + 1, :] += _rsum(dconv * _rows_at(ge, gr, o0 + k, tm))
        dz_ref[:, 5 * D:6 * D] = (dglu * sg).astype(BF16)
        dz_ref[:, 6 * D:7 * D] = (dglu * a_c * (sg * (1.0 - sg))).astype(BF16)
        dqe[tm:tm + HALO, :] = dqe[0:HALO, :]
        dce[tm:tm + HALO, :] = dce[0:HALO, :]

    rev = lambda i: (nt - 1 - i, 0)
    rs = pl.BlockSpec((tm, D), rev)
    cur = pl.BlockSpec((tm, 7 * D), rev)
    prev = pl.BlockSpec((HALO, 7 * D), lambda i: (jnp.maximum((nt - 1 - i) * hb - 1, 0), 0))
    ext = pltpu.VMEM((HALO + tm, D), F32)
    outs = _pcall(
        body, grid=(nt,),
        in_specs=[cur, prev, rs, rs, rs, _const_spec((8, D)), _const_spec((8, D)), _const_spec((NG, CHUNK, CHUNK)),
                  _const_spec((NG, CHUNK, CHUNK)), _const_spec((CHUNK, D)), _const_spec((HALO, D)), _const_spec((8, D)),
                  ANY],
        out_specs=[cur, _const_spec((8, D)), _const_spec((HALO, D)), _const_spec((NG, CHUNK, CHUNK)),
                   _const_spec((NG, CHUNK, LANE))],
        out_shape=[_sds(dz.shape, BF16), _sds((8, D), F32), _sds((HALO, D), F32), _sds((NG, CHUNK, CHUNK), F32),
                   _sds((NG, CHUNK, LANE), F32)],
        scratch_shapes=[ext, ext, ext, ext, pltpu.VMEM((7, HALO + tm, D), F32), pltpu.VMEM((7, HALO + tm, D), F32)],
        input_output_aliases={12: 0}, name=name,
        compiler_params=_params(("arbitrary",)))(z, z, *dacts, wsh, sgu_ln, wtril, wtril_t, bias_full, cw, cvec, dz)
    return outs


def _ada_fwd(c_all, w_ada_loc, name):
    nb, D = c_all.shape
    L, _, nc = w_ada_loc.shape

    def body(c_ref, w_ref, o_ref, ca_ref):
        cv = c_ref[...]
        ca = cv * _sigmoid(cv)
        ca_ref[...] = ca
        o_ref[...] = jnp.dot(ca.astype(BF16), w_ref[...].astype(BF16), preferred_element_type=F32)

    return _pcall(body, grid=(L,),
                  in_specs=[_const_spec((nb, D)), pl.BlockSpec((None, D, nc), lambda l: (l, 0, 0))],
                  out_specs=[pl.BlockSpec((None, nb, nc), lambda l: (l, 0, 0)), _const_spec((nb, D))],
                  out_shape=[_sds((L, nb, nc), F32), _sds((nb, D), F32)], name=name,
                  compiler_params=_params(("arbitrary",)))(c_all, w_ada_loc)


def _adamw(w, g, m, v):
    m = ADAM_B1 * m + (1.0 - ADAM_B1) * g
    v = ADAM_B2 * v + (1.0 - ADAM_B2) * (g * g)
    m_hat = m / (1.0 - ADAM_B1 ** ADAM_STEP)
    v_hat = v / (1.0 - ADAM_B2 ** ADAM_STEP)
    delta = -ADAM_LR * (m_hat / (jnp.sqrt(v_hat) + ADAM_EPS) + ADAM_WD * w)
    return delta, m, v


def _tile_rows(R, C, align=8):
    cap = max(align, (640 * 1024) // (4 * C))
    best = None
    for t in range(align, R + 1, align):
        if R % t == 0 and t <= cap:
            best = t
    return R if best is None else best


def _adam_ada(ct, dm, w, m, v, name):
    L, D, nc = w.shape
    nb = ct.shape[1]
    tr = _tile_rows(D, nc)

    def body(ct_ref, dm_ref, w_ref, m_ref, v_ref, g_ref, d_ref, mo_ref, vo_ref):
        g = ct_ref[:, 0:1] * dm_ref[0:1, :]
        for b in range(1, nb):
            g = g + ct_ref[:, b:b + 1] * dm_ref[b:b + 1, :]
        g_ref[...] = g
        d_ref[...], mo_ref[...], vo_ref[...] = _adamw(w_ref[...], g, m_ref[...], v_ref[...])

    ws = pl.BlockSpec((None, tr, nc), lambda l, r: (l, r, 0))
    return _pcall(body, grid=(L, D // tr),
                  in_specs=[pl.BlockSpec((tr, nb), lambda l, r: (r, 0)), pl.BlockSpec((None, nb, nc), lambda l, r: (l, 0, 0)),
                            ws, ws, ws],
                  out_specs=[ws] * 4, out_shape=[_sds(w.shape, F32)] * 4, name=name,
                  compiler_params=_params(("parallel", "parallel")))(ct, dm, w, m, v)


def _adam_small(parts, w, m, v, name):
    n, R, C = parts.shape
    tr = _tile_rows(R, C * n // 2)

    def body(p_ref, w_ref, m_ref, v_ref, g_ref, d_ref, mo_ref, vo_ref):
        g = p_ref[0]
        for j in range(1, n):
            g = g + p_ref[j]
        g_ref[...] = g
        d_ref[...], mo_ref[...], vo_ref[...] = _adamw(w_ref[...], g, m_ref[...], v_ref[...])

    ws = pl.BlockSpec((tr, C), lambda r: (r, 0))
    return _pcall(body, grid=(R // tr,), in_specs=[pl.BlockSpec((n, tr, C), lambda r: (0, r, 0)), ws, ws, ws],
                  out_specs=[ws] * 4, out_shape=[_sds((R, C), F32)] * 4, name=name,
                  compiler_params=_params(("parallel",)))(parts, w, m, v)


def _adam_plain(g, w, m, v, name):
    R, C = w.shape

    def body(g_ref, w_ref, m_ref, v_ref, d_ref, mo_ref, vo_ref):
        d_ref[...], mo_ref[...], vo_ref[...] = _adamw(w_ref[...], g_ref[...], m_ref[...], v_ref[...])

    ws = _const_spec((R, C))
    return _pcall(body, grid=(1,), in_specs=[ws] * 4, out_specs=[ws] * 3, out_shape=[_sds((R, C), F32)] * 3, name=name,
                  compiler_params=_params(("arbitrary",)))(g, w, m, v)


def _pair_sum(G, R1, my_c, name):
    L, n, R, C = G.shape
    half = n // 2
    tr = _tile_rows(R, C, align=16)

    def body(c_ref, g_ref, r_ref, o_ref):
        o_ref[...] = (g_ref[...].astype(F32) + r_ref[...].astype(F32)).astype(o_ref.dtype)

    blk = (None, None, tr, C)
    gs = pltpu.PrefetchScalarGridSpec(
        num_scalar_prefetch=1, grid=(L, half, R // tr),
        in_specs=[pl.BlockSpec(blk, lambda l, p, r, c: (l, 2 * p + c[0], r, 0)),
                  pl.BlockSpec(blk, lambda l, p, r, c: (l, p, r, 0))],
        out_specs=pl.BlockSpec(blk, lambda l, p, r, c: (l, p, r, 0)))
    return _pcall(body, grid_spec=gs, out_shape=_sds((L, half, R, C), G.dtype), name=name,
                  compiler_params=_params(("parallel", "parallel", "parallel")))(my_c, G, R1)


def _adam_big(P, R2, my_chip, w, m, v, name):
    L, _, R, C = P.shape
    nrecv = R2.shape[0]
    tr = _tile_rows(R, C, align=16)

    def body(p_sm, p_ref, r_ref, w_ref, m_ref, v_ref, g_ref, d_ref, mo_ref, vo_ref):
        g = p_ref[...].astype(F32)
        for k in range(nrecv):
            g = g + r_ref[k].astype(F32)
        g_ref[...] = g
        d_ref[...], mo_ref[...], vo_ref[...] = _adamw(w_ref[...], g, m_ref[...], v_ref[...])

    ws = pl.BlockSpec((None, tr, C), lambda l, r, p: (l, r, 0))
    gs = pltpu.PrefetchScalarGridSpec(
        num_scalar_prefetch=1, grid=(L, R // tr),
        in_specs=[pl.BlockSpec((None, None, tr, C), lambda l, r, p: (l, p[0], r, 0)),
                  pl.BlockSpec((nrecv, None, tr, C), lambda l, r, p: (0, l, r, 0)), ws, ws, ws],
        out_specs=[ws] * 4)
    return _pcall(body, grid_spec=gs, out_shape=[_sds((L, R, C), F32)] * 4, name=name,
                  compiler_params=_params(("parallel", "parallel")))(my_chip, P, R2, w, m, v)


def _place():
    return lax.axis_index("x"), lax.axis_index("y"), lax.axis_index("c")


def _sum_over_devices(scalar):
    return lax.psum(scalar, ("x", "y", "c"))


def _all_gather(shards, name):
    n = len(shards)

    def body(*refs):
        ins, outs = refs[:n], refs[n:2 * n]
        send_sems, recv_sems, local_sems = refs[2 * n:]
        x, y, c = _place()
        me, sibling = (x, y, c), (x, y, 1 - c)
        chips = [(1 - x, y), (x, 1 - y), (1 - x, 1 - y)]

        def slot(a, px, py, pc):
            return outs[a].at[4 * px + 2 * py + pc]

        def copy(a, k, block, to, src=None):
            return pltpu.make_async_remote_copy(
                src_ref=slot(a, *block) if src is None else src, dst_ref=slot(a, *block),
                send_sem=send_sems.at[7 * a + k], recv_sem=recv_sems.at[7 * a + k], device_id=to, device_id_type=MESH)

        mine = [pltpu.make_async_copy(ins[a], slot(a, *me), local_sems.at[a]) for a in range(n)]
        for cp in mine:
            cp.start()
        first = []
        for a in range(n):
            first.append(copy(a, 0, me, sibling, src=ins[a]))
            first += [copy(a, 1 + j, me, (*chip, c), src=ins[a]) for j, chip in enumerate(chips)]
        for cp in first:
            cp.start()
        passed = []
        for j, chip in enumerate(chips):
            for a in range(n):
                copy(a, 1 + j, (*chip, c), me).wait_recv()
                fwd = copy(a, 4 + j, (*chip, c), sibling)
                fwd.start()
                passed.append(fwd)
        for a in range(n):
            copy(a, 0, sibling, me).wait_recv()
        for j, chip in enumerate(chips):
            for a in range(n):
                copy(a, 4 + j, (*chip, 1 - c), me).wait_recv()
        for cp in first + passed:
            cp.wait_send()
        for cp in mine:
            cp.wait()

    outs = _pcall(body, in_specs=[ANY] * n, out_specs=[ANY] * n,
                  out_shape=[_sds((NDEV,) + s.shape, s.dtype) for s in shards],
                  scratch_shapes=[pltpu.SemaphoreType.DMA((7 * n,)), pltpu.SemaphoreType.DMA((7 * n,)),
                                  pltpu.SemaphoreType.DMA((n,))], name=name)(*shards)
    return list(outs)


def _rs_sibling(Gs, name):
    n = len(Gs)
    L = Gs[0].shape[0]
    per = L * NCHIP

    def body(*refs):
        ins, outs = refs[:n], refs[n:2 * n]
        send_sems, recv_sems = refs[2 * n:]
        x, y, c = _place()
        copies = []
        for a in range(n):
            for l in range(L):
                for p in range(NCHIP):
                    k = a * per + l * NCHIP + p
                    copies.append(pltpu.make_async_remote_copy(
                        src_ref=ins[a].at[l, 2 * p + 1 - c], dst_ref=outs[a].at[l, p], send_sem=send_sems.at[k],
                        recv_sem=recv_sems.at[k], device_id=(x, y, 1 - c), device_id_type=MESH))
        for cp in copies:
            cp.start()
        for cp in copies:
            cp.wait()

    outs = _pcall(body, in_specs=[ANY] * n, out_specs=[ANY] * n,
                  out_shape=[_sds((L, NCHIP) + g.shape[2:], g.dtype) for g in Gs],
                  scratch_shapes=[pltpu.SemaphoreType.DMA((n * per,)), pltpu.SemaphoreType.DMA((n * per,))],
                  name=name)(*Gs)
    return list(outs)


def _rs_chips(Ps, name):
    n = len(Ps)
    L = Ps[0].shape[0]
    per = 3 * L

    def body(*refs):
        ins, outs = refs[:n], refs[n:2 * n]
        send_sems, recv_sems = refs[2 * n:]
        x, y, c = _place()
        chips = [(1 - x, y), (x, 1 - y), (1 - x, 1 - y)]
        copies = []
        for a in range(n):
            for j, (px, py) in enumerate(chips):
                for l in range(L):
                    k = a * per + j * L + l
                    copies.append(pltpu.make_async_remote_copy(
                        src_ref=ins[a].at[l, 2 * px + py], dst_ref=outs[a].at[j, l], send_sem=send_sems.at[k],
                        recv_sem=recv_sems.at[k], device_id=(px, py, c), device_id_type=MESH))
        for cp in copies:
            cp.start()
        for cp in copies:
            cp.wait()

    outs = _pcall(body, in_specs=[ANY] * n, out_specs=[ANY] * n,
                  out_shape=[_sds((3, L) + p.shape[2:], p.dtype) for p in Ps],
                  scratch_shapes=[pltpu.SemaphoreType.DMA((n * per,)), pltpu.SemaphoreType.DMA((n * per,))],
                  name=name)(*Ps)
    return list(outs)


def _reduce_scatter_adam(Gs, shards, my_c, my_chip):
    R1s = _rs_sibling(Gs, "rs_sibling")
    Ps = [_pair_sum(g, r1, my_c, f"pair_sum{a}") for a, (g, r1) in enumerate(zip(Gs, R1s))]
    R2s = _rs_chips(Ps, "rs_chips")
    return [_adam_big(p, r2, my_chip, *wmv, name=f"adam_big{a}") for a, (p, r2, wmv) in enumerate(zip(Ps, R2s, shards))]


SMALL_ROWS = {"norm1_g": (0, 1), "norm2_g": (1, 1), "sgu_ln_g": (2, 1), "sgu_ln_b": (3, 1), "cfm_conv_b": (4, 1),
              "cfm_ln_g": (5, 1), "cfm_ln_b": (6, 1), "b_sgu": (7, 1), "w_sgu": (8, 128), "b_ada": (136, N_MOD),
              "w_short": (142, SHORT_K), "cfm_conv_w": (145, CFM_K)}
ROWS_PER_LAYER = 176
FINAL_ROW = DEPTH * ROWS_PER_LAYER
PACK_ROWS = 360


def _pack(get, D):
    parts = []
    for l in range(DEPTH):
        for name, (_, nrows) in SMALL_ROWS.items():
            a = get(name, l)
            parts.append(jnp.zeros((nrows, D), F32) if a is None else a.astype(F32).reshape(nrows, D))
    fin = get("final_g", None)
    parts.append(fin.astype(F32).reshape(1, D))
    parts.append(jnp.zeros((PACK_ROWS - FINAL_ROW - 1, D), F32))
    return jnp.concatenate(parts, axis=0)


def _unpack(pack, name, shape):
    D = pack.shape[1]
    r0, nrows = SMALL_ROWS[name]
    return jnp.stack([pack[l * ROWS_PER_LAYER + r0:l * ROWS_PER_LAYER + r0 + nrows] for l in range(DEPTH)]).reshape(shape)


def _mm_tiles(S):
    return min(512, S), min(1024, S)


def kernel(x, c, w_ada, b_ada, norm1_g, w_in, w_short, w_a_out, sgu_ln_g, sgu_ln_b, w_sgu, b_sgu, w_b_out, cfm_conv_w, cfm_conv_b, cfm_ln_g, cfm_ln_b, w_c_out, w_o, norm2_g, w_ffn_in, w_ffn_out, final_g, loss_target, m_w_ada, m_b_ada, m_norm1_g, m_w_in, m_w_short, m_w_a_out, m_sgu_ln_g, m_sgu_ln_b, m_w_sgu, m_b_sgu, m_w_b_out, m_cfm_conv_w, m_cfm_conv_b, m_cfm_ln_g, m_cfm_ln_b, m_w_c_out, m_w_o, m_norm2_g, m_w_ffn_in, m_w_ffn_out, m_final_g, v_w_ada, v_b_ada, v_norm1_g, v_w_in, v_w_short, v_w_a_out, v_sgu_ln_g, v_sgu_ln_b, v_w_sgu, v_b_sgu, v_w_b_out, v_cfm_conv_w, v_cfm_conv_b, v_cfm_ln_g, v_cfm_ln_b, v_w_c_out, v_w_o, v_norm2_g, v_w_ffn_in, v_w_ffn_out, v_final_g):
    W = dict(w_ada=w_ada, b_ada=b_ada, norm1_g=norm1_g, w_in=w_in, w_short=w_short, w_a_out=w_a_out, sgu_ln_g=sgu_ln_g,
             sgu_ln_b=sgu_ln_b, w_sgu=w_sgu, b_sgu=b_sgu, w_b_out=w_b_out, cfm_conv_w=cfm_conv_w, cfm_conv_b=cfm_conv_b,
             cfm_ln_g=cfm_ln_g, cfm_ln_b=cfm_ln_b, w_c_out=w_c_out, w_o=w_o, norm2_g=norm2_g, w_ffn_in=w_ffn_in,
             w_ffn_out=w_ffn_out, final_g=final_g)
    Mo = dict(w_ada=m_w_ada, b_ada=m_b_ada, norm1_g=m_norm1_g, w_in=m_w_in, w_short=m_w_short, w_a_out=m_w_a_out,
              sgu_ln_g=m_sgu_ln_g, sgu_ln_b=m_sgu_ln_b, w_sgu=m_w_sgu, b_sgu=m_b_sgu, w_b_out=m_w_b_out,
              cfm_conv_w=m_cfm_conv_w, cfm_conv_b=m_cfm_conv_b, cfm_ln_g=m_cfm_ln_g, cfm_ln_b=m_cfm_ln_b,
              w_c_out=m_w_c_out, w_o=m_w_o, norm2_g=m_norm2_g, w_ffn_in=m_w_ffn_in, w_ffn_out=m_w_ffn_out,
              final_g=m_final_g)
    Vo = dict(w_ada=v_w_ada, b_ada=v_b_ada, norm1_g=v_norm1_g, w_in=v_w_in, w_short=v_w_short, w_a_out=v_w_a_out,
              sgu_ln_g=v_sgu_ln_g, sgu_ln_b=v_sgu_ln_b, w_sgu=v_w_sgu, b_sgu=v_b_sgu, w_b_out=v_w_b_out,
              cfm_conv_w=v_cfm_conv_w, cfm_conv_b=v_cfm_conv_b, cfm_ln_g=v_cfm_ln_g, cfm_ln_b=v_cfm_ln_b,
              w_c_out=v_w_c_out, w_o=v_w_o, norm2_g=v_norm2_g, w_ffn_in=v_w_ffn_in, w_ffn_out=v_w_ffn_out,
              final_g=v_final_g)
    order = ["w_ada", "b_ada", "norm1_g", "w_in", "w_short", "w_a_out", "sgu_ln_g", "sgu_ln_b", "w_sgu", "b_sgu",
             "w_b_out", "cfm_conv_w", "cfm_conv_b", "cfm_ln_g", "cfm_ln_b", "w_c_out", "w_o", "norm2_g", "w_ffn_in",
             "w_ffn_out", "final_g"]

    S, D = x.shape[1], x.shape[2]
    F2 = w_ffn_in.shape[2] * NDEV
    FF = F2 // 2
    xi, yi, ci = _place()
    dev = 4 * xi + 2 * yi + ci
    my_c = jnp.reshape(ci, (1,)).astype(jnp.int32)
    my_chip = jnp.reshape(2 * xi + yi, (1,)).astype(jnp.int32)
    tm, tm_big = _mm_tiles(S)
    x0 = x.reshape(S, D)
    tgt = loss_target.reshape(S, D)

    c_all = _all_gather([jnp.pad(c, ((0, 7), (0, 0)))], "ag_c")[0][:, 0, :]
    modpart, c_act = _ada_fwd(c_all, w_ada, "ada_fwd")
    ncol = modpart.shape[2]
    mg = _all_gather([modpart.reshape(DEPTH * NDEV, ncol)], "ag_mod")[0].reshape(NDEV, DEPTH, NDEV, ncol)
    mine = lax.dynamic_index_in_dim(mg, dev, axis=2, keepdims=False)
    mod = (jnp.transpose(mine, (1, 0, 2)).reshape(DEPTH, N_MOD * D) + b_ada).reshape(DEPTH, N_MOD, D)

    tril = jnp.tril(jnp.ones((CHUNK, CHUNK), dtype=bool))

    def layer_consts(l):
        wt = jnp.where(tril[None], w_sgu[l], 0.0).astype(BF16)
        return dict(
            wsh=jnp.pad(w_short_full[l], ((0, 8 - SHORT_K), (0, 0))),
            sgu_ln=_rows(sgu_ln_g[l], sgu_ln_b[l]),
            wtril=wt, wtril_t=jnp.swapaxes(wt, 1, 2),
            bias_full=jnp.repeat(b_sgu[l].T, LANE, axis=1),
            cw=jnp.pad(cfm_w_full[l], ((0, HALO - CFM_K), (0, 0))),
            cvec=_rows(cfm_conv_b[l], cfm_ln_g[l], cfm_ln_b[l]))

    ncs = w_short.shape[2]
    sw = _all_gather([w_short.reshape(DEPTH * SHORT_K, ncs), cfm_conv_w.reshape(DEPTH * CFM_K, ncs)], "ag_convw")
    w_short_full = jnp.transpose(sw[0], (1, 0, 2)).reshape(DEPTH, SHORT_K, D)
    cfm_w_full = jnp.transpose(sw[1], (1, 0, 2)).reshape(DEPTH, CFM_K, D)

    def gather_layer(l):
        g = _all_gather([w_in[l].astype(BF16), w_a_out[l].astype(BF16), w_b_out[l].astype(BF16),
                         w_c_out[l].astype(BF16), w_o[l].astype(BF16), w_ffn_in[l].astype(BF16),
                         w_ffn_out[l].astype(BF16)], f"ag_w{l}")
        return dict(w_in=g[0], w_a=g[1].reshape(1, D, D), w_b=g[2].reshape(1, D, D), w_c=g[3].reshape(1, D, D),
                    w_o=g[4].reshape(1, D, D), w_fi=jnp.transpose(g[5], (1, 0, 2)).reshape(1, D, F2),
                    w_fo=g[6].reshape(1, FF, D))

    Wg = [gather_layer(l) for l in range(DEPTH)]
    nin = Wg[0]["w_in"].shape[2]
    tn_in = nin if nin % 256 == 0 and nin <= 1280 else 256
    tn_fi = 512 if F2 % 512 == 0 else 256

    saved = []
    xcur, fprev, gprev = x0, None, None
    for l in range(DEPTH):
        sh1, sc1, g1, sh2, sc2, g2 = [mod[l, k] for k in range(N_MOD)]
        wl, cl = Wg[l], layer_consts(l)
        vec1 = _rows(jnp.zeros((D,), F32) if gprev is None else gprev, norm1_g[l], sc1, sh1)
        xl, h = _norm_fwd(xcur, fprev, vec1, f"norm1_fwd{l}")
        z = _mm_nn(h, wl["w_in"], F32, tm, tn_in, D, f"mm_in{l}", w_outer=True)
        acts = _mixer_fwd(z, cl["wsh"], cl["sgu_ln"], cl["wtril"], cl["bias_full"], cl["cw"], cl["cvec"], f"mixer_fwd{l}")
        merged, ys = _branch_out(acts, [wl["w_a"][0], wl["w_b"][0], wl["w_c"][0]], z, f"branch_out{l}")
        o = _mm_nn(merged, wl["w_o"], F32, tm, D, D, f"mm_o{l}")
        x1, h2 = _norm_fwd(xl, o, _rows(g1, norm2_g[l], sc2, sh2), f"norm2_fwd{l}")
        gu = _mm_nn(h2, wl["w_fi"], F32, tm_big, tn_fi, D, f"mm_ffn_in{l}")
        act = _swiglu_fwd(gu, f"swiglu_fwd{l}")
        f = _mm_nn(act, wl["w_fo"], F32, tm, D, FF, f"mm_ffn_out{l}")
        saved.append(dict(xl=xl, h=h, z=z, acts=acts, ys=ys, merged=merged, o=o, x1=x1, h2=h2, gu=gu, act=act, f=f,
                          consts=cl, mod=(sh1, sc1, g1, sh2, sc2, g2)))
        xcur, fprev, gprev = x1, f, g2

    last = saved[-1]
    dxup, dfb, fsums, loss_blk = _final_bwd(last["x1"], last["f"], tgt, _rows(last["mod"][5], final_g), "final_bwd")
    loss = _sum_over_devices(loss_blk[0, 0])
    dgate2_next = fsums[1]
    G = dict(w_in=None, w_a=None, w_b=None, w_c=None, w_o=None, w_fi=None, w_fo=None)
    small = [dict() for _ in range(DEPTH)]
    dmods = [None] * DEPTH
    for l in reversed(range(DEPTH)):
        sv, wl, cl = saved[l], Wg[l], saved[l]["consts"]
        sh1, sc1, g1, sh2, sc2, g2 = sv["mod"]
        dact = _mm_nt(dfb, wl["w_fo"], F32, tm, FF, D, f"mm_dact{l}")
        G["w_fo"] = _mm_tn(sv["act"], dfb, 1, l, G["w_fo"], FF // 2, D, tm_big, f"mm_dw_ffn_out{l}")
        dgu = _swiglu_bwd(dact, sv["gu"], f"swiglu_bwd{l}")
        dh2 = _mm_nt(dgu, wl["w_fi"], F32, tm_big, D, tn_fi, f"mm_dh2{l}")
        G["w_fi"] = _mm_tn(sv["h2"], dgu, 1, l, G["w_fi"], D, tn_fi, tm_big, f"mm_dw_ffn_in{l}")
        dx1, dob, s2 = _norm_bwd(sv["x1"], dh2, dxup, _rows(norm2_g[l], sc2, g1), sv["o"], f"norm2_bwd{l}")
        dmerged = _mm_nt(dob, wl["w_o"], F32, tm, D, D, f"mm_dmerged{l}")
        G["w_o"] = _mm_tn(sv["merged"], dob, 1, l, G["w_o"], D, D, tm_big, f"mm_dw_o{l}")
        dys, dz = _gate_bwd(dmerged, sv["z"], sv["ys"], f"gate_bwd{l}")
        dacts = []
        for n, key in enumerate(("w_a", "w_b", "w_c")):
            dacts.append(_mm_nt(dys[n], wl[key], F32, tm, D, D, f"mm_dact_{key}{l}"))
            G[key] = _mm_tn(sv["acts"][n], dys[n], 1, l, G[key], D, D, tm_big, f"mm_d{key}{l}")
        dz, mvec, dcw, dws, dbs = _mixer_bwd(sv["z"], dacts, dz, cl["wsh"], cl["sgu_ln"], cl["wtril"], cl["wtril_t"],
                                             cl["bias_full"], cl["cw"], cl["cvec"], f"mixer_bwd{l}")
        dh = _mm_nt(dz, wl["w_in"], F32, tm_big, D, tn_in, f"mm_dh{l}")
        G["w_in"] = _mm_tn(sv["h"], dz, NDEV, l, G["w_in"], D, tn_in, tm_big, f"mm_dw_in{l}")
        if l > 0:
            pv = saved[l - 1]
            dxup, dfb, s1 = _norm_bwd(sv["xl"], dh, dx1, _rows(norm1_g[l], sc1, pv["mod"][5]), pv["f"], f"norm1_bwd{l}")
        else:
            dxup, dfb, s1 = _norm_bwd(sv["xl"], dh, dx1, _rows(norm1_g[l], sc1), None, f"norm1_bwd{l}")
        dmods[l] = jnp.stack([s1[0], s1[1], s2[3], s2[0], s2[1], dgate2_next])
        dgate2_next = s1[3]
        small[l] = dict(norm1_g=s1[2], norm2_g=s2[2], sgu_ln_g=mvec[3], sgu_ln_b=mvec[4], cfm_conv_b=mvec[5],
                        cfm_ln_g=mvec[6], cfm_ln_b=mvec[7], b_sgu=dbs[:, :, 0],
                        w_sgu=jnp.where(tril[None], dws, 0.0), b_ada=dmods[l], w_short=mvec[0:SHORT_K],
                        cfm_conv_w=dcw[0:CFM_K])
    grad_x = dxup.reshape(x.shape)

    gpack = _pack(lambda name, l: fsums[0] if name == "final_g" else small[l][name], D)
    gathered = _all_gather([gpack], "ag_small")[0]
    sharded_small = ("w_short", "cfm_conv_w")
    packs = [_pack(lambda name, l, T=T: T["final_g"] if name == "final_g" else (None if name in sharded_small else T[name][l]), D)
             for T in (W, Mo, Vo)]
    sg, sd, sm, sv_ = _adam_small(gathered, *packs, name="adam_small")
    out = {}
    for name in order:
        if name in SMALL_ROWS and name not in sharded_small:
            out[name] = tuple(_unpack(p, name, W[name].shape) for p in (sg, sd, sm, sv_))
    out["final_g"] = tuple(p[FINAL_ROW] for p in (sg, sd, sm, sv_))

    def my_cols(name):
        full = _unpack(sg, name, (DEPTH, SMALL_ROWS[name][1], D))
        return lax.dynamic_slice_in_dim(full, dev * ncs, ncs, axis=2)

    gcs = jnp.concatenate([my_cols("w_short").reshape(-1, ncs), my_cols("cfm_conv_w").reshape(-1, ncs)])
    ncr = gcs.shape[0]
    padr = (-ncr) % 8
    cat = lambda T: jnp.pad(jnp.concatenate([T["w_short"].reshape(-1, ncs), T["cfm_conv_w"].reshape(-1, ncs)]), ((0, padr), (0, 0)))
    cd, cm, cv = _adam_plain(jnp.pad(gcs, ((0, padr), (0, 0))), cat(W), cat(Mo), cat(Vo), "adam_convw")
    nsh = DEPTH * SHORT_K
    out["w_short"] = tuple(a[0:nsh].reshape(w_short.shape) for a in (gcs, cd, cm, cv))
    out["cfm_conv_w"] = tuple(a[nsh:ncr].reshape(cfm_conv_w.shape) for a in (gcs, cd, cm, cv))

    dm_all = jnp.stack([gathered[:, l * ROWS_PER_LAYER + 136:l * ROWS_PER_LAYER + 136 + N_MOD, :].reshape(NDEV, N_MOD * D)
                        for l in range(DEPTH)])
    dm_mine = lax.dynamic_slice_in_dim(dm_all, dev * ncol, ncol, axis=2)
    out["w_ada"] = tuple(_adam_ada(jnp.transpose(c_act), dm_mine, w_ada, m_w_ada, v_w_ada, "adam_ada"))

    nfi = w_ffn_in.shape[2]
    Gs = [G["w_in"], G["w_a"].reshape(DEPTH, NDEV, D // NDEV, D), G["w_b"].reshape(DEPTH, NDEV, D // NDEV, D),
          G["w_c"].reshape(DEPTH, NDEV, D // NDEV, D), G["w_o"].reshape(DEPTH, NDEV, D // NDEV, D),
          jnp.transpose(G["w_fi"].reshape(DEPTH, D, NDEV, nfi), (0, 2, 1, 3)),
          G["w_fo"].reshape(DEPTH, NDEV, FF // NDEV, D)]
    big = ["w_in", "w_a_out", "w_b_out", "w_c_out", "w_o", "w_ffn_in", "w_ffn_out"]
    res = _reduce_scatter_adam(Gs, [(W[n], Mo[n], Vo[n]) for n in big], my_c, my_chip)
    for n, r in zip(big, res):
        out[n] = tuple(r)

    grads = [out[n][0] for n in order]
    deltas = [out[n][1] for n in order]
    new_m = [out[n][2] for n in order]
    new_v = [out[n][3] for n in order]
    return (loss, grad_x, *grads, *deltas, *new_m, *new_v)
```

```python
import functools
import math

import jax
import jax.numpy as jnp
from jax import lax
from jax.experimental import pallas as pl
from jax.experimental.pallas import tpu as pltpu

F32, BF16 = jnp.float32, jnp.bfloat16
NDEV = 8
NCHIP = NDEV // 2
DEPTH = 2
EPS = 1e-6
CHUNK = 128
NG = 8
SHORT_K = 3
CFM_K = 31
HALO = 32
N_MOD = 6
LANE = 128
VMEM_LIMIT = 56 * 1024 * 1024
ADAM_LR, ADAM_B1, ADAM_B2, ADAM_EPS, ADAM_WD, ADAM_STEP = 0.001, 0.9, 0.999, 1e-08, 0.01, 10
_G0 = math.sqrt(2.0 / math.pi)
_G1 = 0.044715
MESH = pl.DeviceIdType.MESH
ANY = pl.BlockSpec(memory_space=pl.ANY)


def _pcall(body, **kw):
    return pl.pallas_call(body, **kw)


def _params(sem=None):
    return pltpu.CompilerParams(dimension_semantics=sem, vmem_limit_bytes=VMEM_LIMIT)


def _sds(shape, dtype):
    return jax.ShapeDtypeStruct(tuple(shape), dtype)


def _mm_body(dims, nk, out_f32):
    def body(a_ref, b_ref, o_ref, *scr):
        k = pl.program_id(2)
        part = lax.dot_general(a_ref[...], b_ref[...], dims, preferred_element_type=F32)
        if nk == 1:
            o_ref[...] = part.reshape(o_ref.shape).astype(o_ref.dtype)
        elif out_f32:
            @pl.when(k == 0)
            def _():
                o_ref[...] = part.reshape(o_ref.shape)

            @pl.when(k > 0)
            def _():
                o_ref[...] += part.reshape(o_ref.shape)
        else:
            acc = scr[0]

            @pl.when(k == 0)
            def _():
                acc[...] = part

            @pl.when(k > 0)
            def _():
                acc[...] += part

            @pl.when(k == nk - 1)
            def _():
                o_ref[...] = acc[...].astype(o_ref.dtype)
    return body


def _after(body, n_in, deps):
    nd = len(deps)
    if nd == 0:
        return body

    def ordered(*refs):
        return body(*refs[:n_in], *refs[n_in + nd:])
    return ordered


def _mm_call(body, grid, in_specs, out_spec, out_shape, acc_shape, name, deps=()):
    scratch = [] if acc_shape is None else [pltpu.VMEM(acc_shape, F32)]
    return _pcall(_after(body, 2, deps), grid=grid, in_specs=in_specs + [ANY] * len(deps), out_specs=out_spec,
                  out_shape=out_shape, scratch_shapes=scratch, name=name,
                  compiler_params=_params(("parallel", "parallel", "arbitrary")))


def _mm_nn(a, b3, out_dtype, tm, tn, tk, name, w_outer=False, deps=()):
    M, K = a.shape
    G, _, Nb = b3.shape
    npb, nk = Nb // tn, K // tk
    out_f32 = out_dtype == F32
    body = _mm_body((((1,), (0,)), ((), ())), nk, out_f32)
    if w_outer:
        grid = (G * npb, M // tm, nk)
        ij = lambda p, q: (q, p)
    else:
        grid = (M // tm, G * npb, nk)
        ij = lambda p, q: (p, q)

    def a_map(p, q, k):
        i, j = ij(p, q)
        return (i, k)

    def b_map(p, q, k):
        i, j = ij(p, q)
        return (j // npb, k, j % npb)

    def o_map(p, q, k):
        return ij(p, q)

    def wrapped(a_ref, b_ref, o_ref, *scr):
        body(a_ref, b_ref, o_ref, *scr)

    return _mm_call(wrapped, grid, [pl.BlockSpec((tm, tk), a_map), pl.BlockSpec((None, tk, tn), b_map)],
                    pl.BlockSpec((tm, tn), o_map), _sds((M, G * Nb), out_dtype),
                    None if (nk == 1 or out_f32) else (tm, tn), name, deps)(a, b3, *deps)


def _mm_nt(a, b3, out_dtype, tm, tn, tk, name, deps=()):
    M, _ = a.shape
    G, Ko, Nb = b3.shape
    kpb = Nb // tk
    nk = G * kpb
    out_f32 = out_dtype == F32
    body = _mm_body((((1,), (1,)), ((), ())), nk, out_f32)

    def wrapped(a_ref, b_ref, o_ref, *scr):
        body(a_ref, b_ref, o_ref, *scr)

    return _mm_call(wrapped, (M // tm, Ko // tn, nk),
                    [pl.BlockSpec((tm, tk), lambda i, j, k: (i, k)),
                     pl.BlockSpec((None, tn, tk), lambda i, j, k: (k // kpb, j, k % kpb))],
                    pl.BlockSpec((tm, tn), lambda i, j, k: (i, j)), _sds((M, Ko), out_dtype),
                    None if (nk == 1 or out_f32) else (tm, tn), name, deps)(a, b3, *deps)


def _mm_tn(a, b, G, tm, tn, tk, name, deps=()):
    T, M = a.shape
    Nb = b.shape[1] // G
    npb, nk = Nb // tn, T // tk
    body = _mm_body((((0,), (0,)), ((), ())), nk, False)

    def wrapped(a_ref, b_ref, o_ref, *scr):
        body(a_ref, b_ref, o_ref, *scr)

    in_specs = [pl.BlockSpec((tk, tm), lambda i, j, k: (k, i)), pl.BlockSpec((tk, tn), lambda i, j, k: (k, j))]
    out_spec = pl.BlockSpec((None, tm, tn), lambda i, j, k: (j // npb, i, j % npb))
    return _mm_call(wrapped, (M // tm, G * npb, nk), in_specs, out_spec, _sds((G, M, Nb), BF16),
                    None if nk == 1 else (tm, tn), name, deps)(a, b, *deps)


def _rsum(v):
    return jnp.sum(v, axis=0, keepdims=True)


def _rmean(v):
    return jnp.mean(v, axis=-1, keepdims=True)


def _gelu(x):
    t = jnp.tanh(_G0 * (x + _G1 * (x * x * x)))
    return x * (0.5 * (1.0 + t)), t


def _dgelu(x, t):
    return 0.5 * (1.0 + t) + 0.5 * x * (1.0 - t * t) * (_G0 * (1.0 + 3.0 * _G1 * (x * x)))


def _sigmoid(x):
    return 1.0 / (1.0 + jnp.exp(-x))


def _fill_shifted(ext, rot):
    v = ext[...]
    n = v.shape[0]
    for b in range(1, 8):
        rot[b - 1] = pltpu.roll(v, n - b, 0)


def _rows_at(ext, rot, s, tm):
    a, b = divmod(s, 8)
    return ext[8 * a:8 * a + tm, :] if b == 0 else rot[b - 1, 8 * a:8 * a + tm, :]


def _rows(*vs):
    a = jnp.stack([v.astype(F32) for v in vs])
    return jnp.pad(a, ((0, 8 - len(vs)), (0, 0)))


def _row_spec(tm, D):
    return pl.BlockSpec((tm, D), lambda i: (i, 0))


def _const_spec(shape):
    nd = len(shape)
    return pl.BlockSpec(shape, lambda i: (0,) * nd)


def _norm_fwd(xp, f, vec, name, deps=()):
    S, D = xp.shape
    tm = min(256, S)
    has_f = f is not None

    def body(*refs):
        if has_f:
            xp_ref, f_ref, vec_ref, xo_ref, h_ref = refs
            x = xp_ref[...] + vec_ref[0:1, :] * f_ref[...]
            xo_ref[...] = x
        else:
            xp_ref, vec_ref, h_ref = refs
            x = xp_ref[...]
        r = lax.rsqrt(_rmean(x * x) + EPS)
        h = (x * r) * vec_ref[1:2, :]
        h_ref[...] = (h * (1.0 + vec_ref[2:3, :]) + vec_ref[3:4, :]).astype(BF16)

    rs = _row_spec(tm, D)
    ins = [xp, f, vec] if has_f else [xp, vec]
    in_specs = ([rs, rs] if has_f else [rs]) + [_const_spec((8, D))]
    out_shape = ([_sds((S, D), F32)] if has_f else []) + [_sds((S, D), BF16)]
    out_specs = [rs] * len(out_shape)
    outs = _pcall(_after(body, len(ins), deps), grid=(S // tm,), in_specs=in_specs + [ANY] * len(deps),
                  out_specs=out_specs, out_shape=out_shape, name=name,
                  compiler_params=_params(("parallel",)))(*ins, *deps)
    return (outs[0], outs[1]) if has_f else (xp, outs[0])


def _mixer_fwd(z, wsh, sgu_ln, wtril, bias_full, cw, cvec, name, deps=()):
    S = z.shape[0]
    D = wsh.shape[1]
    tm = CHUNK

    def body(z_ref, wsh_ref, sln_ref, wt_ref, bias_ref, cw_ref, cv_ref, oa_ref, ob_ref, oc_ref, pe, ge, gr):
        i = pl.program_id(0)

        @pl.when(i == 0)
        def _():
            pe[0:HALO, :] = jnp.zeros((HALO, D), F32)
            ge[0:HALO, :] = jnp.zeros((HALO, D), F32)

        pe[HALO:HALO + tm, :] = z_ref[:, D:2 * D] * z_ref[:, 2 * D:3 * D]
        q = wsh_ref[0:1, :] * pe[HALO - 2:HALO - 2 + tm, :]
        q = q + wsh_ref[1:2, :] * pe[HALO - 1:HALO - 1 + tm, :]
        q = q + wsh_ref[2:3, :] * pe[HALO:HALO + tm, :]
        oa_ref[...] = (z_ref[:, 0:D] * q).astype(BF16)
        gu, _ = _gelu(z_ref[:, 3 * D:4 * D])
        gv, _ = _gelu(z_ref[:, 4 * D:5 * D])
        d = gv - _rmean(gv)
        nrm = d * lax.rsqrt(_rmean(d * d) + EPS)
        vnb = (nrm * sln_ref[0:1, :] + sln_ref[1:2, :]).astype(BF16)
        for g in range(NG):
            cs = slice(g * LANE, (g + 1) * LANE)
            mixed = jnp.dot(wt_ref[g], vnb[:, cs], preferred_element_type=F32) + bias_ref[:, cs]
            ob_ref[:, cs] = (gu[:, cs] * mixed).astype(BF16)
        ge[HALO:HALO + tm, :] = z_ref[:, 5 * D:6 * D] * _sigmoid(z_ref[:, 6 * D:7 * D])
        _fill_shifted(ge, gr)
        o0 = HALO - (CFM_K - 1)
        conv = cv_ref[0:1, :] + cw_ref[0:1, :] * _rows_at(ge, gr, o0, tm)
        for k in range(1, CFM_K):
            conv = conv + cw_ref[k:k + 1, :] * _rows_at(ge, gr, o0 + k, tm)
        d = conv - _rmean(conv)
        ln = (d * lax.rsqrt(_rmean(d * d) + EPS)) * cv_ref[1:2, :] + cv_ref[2:3, :]
        oc_ref[...] = (ln * _sigmoid(ln)).astype(BF16)
        pe[0:HALO, :] = pe[tm:tm + HALO, :]
        ge[0:HALO, :] = ge[tm:tm + HALO, :]

    rs = _row_spec(tm, D)
    return _pcall(
        _after(body, 7, deps), grid=(S // tm,),
        in_specs=[pl.BlockSpec((tm, 7 * D), lambda i: (i, 0)), _const_spec((8, D)), _const_spec((8, D)),
                  _const_spec((NG, CHUNK, CHUNK)), _const_spec((CHUNK, D)), _const_spec((HALO, D)), _const_spec((8, D))]
        + [ANY] * len(deps),
        out_specs=[rs, rs, rs], out_shape=[_sds((S, D), BF16)] * 3,
        scratch_shapes=[pltpu.VMEM((HALO + tm, D), F32), pltpu.VMEM((HALO + tm, D), F32),
                        pltpu.VMEM((7, HALO + tm, D), F32)],
        name=name, compiler_params=_params(("arbitrary",)))(z, wsh, sgu_ln, wtril, bias_full, cw, cvec, *deps)


def _branch_out(acts, ws, z, name):
    S, D = acts[0].shape
    tm = min(256, S)

    def body(a0, a1, a2, w0, w1, w2, g0, g1, g2, m_ref, y_ref):
        m = None
        for n, (a, w, g) in enumerate(((a0, w0, g0), (a1, w1, g1), (a2, w2, g2))):
            y = jnp.dot(a[...], w[...], preferred_element_type=F32)
            y_ref[n] = y.astype(BF16)
            t = _sigmoid(g[...]) * y
            m = t if m is None else m + t
        m_ref[...] = m.astype(BF16)

    rs = _row_spec(tm, D)
    gate_specs = [pl.BlockSpec((tm, D), functools.partial(lambda i, n: (i, 7 + n), n=n)) for n in range(3)]
    return _pcall(body, grid=(S // tm,),
                  in_specs=[rs, rs, rs] + [_const_spec((D, D))] * 3 + gate_specs,
                  out_specs=[rs, pl.BlockSpec((3, tm, D), lambda i: (0, i, 0))],
                  out_shape=[_sds((S, D), BF16), _sds((3, S, D), BF16)], name=name,
                  compiler_params=_params(("parallel",)))(*acts, *ws, z, z, z)


def _swiglu_fwd(gu, name):
    S, F2 = gu.shape
    F = F2 // 2
    tm = min(256, S)

    def body(g_ref, u_ref, o_ref):
        g = g_ref[...]
        o_ref[...] = ((g * _sigmoid(g)) * u_ref[...]).astype(BF16)

    return _pcall(body, grid=(S // tm,),
                  in_specs=[pl.BlockSpec((tm, F), lambda i: (i, 0)), pl.BlockSpec((tm, F), lambda i: (i, 1))],
                  out_specs=pl.BlockSpec((tm, F), lambda i: (i, 0)), out_shape=_sds((S, F), BF16), name=name,
                  compiler_params=_params(("parallel",)))(gu, gu)


def _swiglu_bwd(dact, gu, name):
    S, F2 = gu.shape
    F = F2 // 2
    tm = min(128, S)

    def body(d_ref, g_ref, u_ref, o_ref):
        g = g_ref[...]
        sg = _sigmoid(g)
        d = d_ref[...]
        o_ref[:, 0:F] = (d * u_ref[...] * (sg * (1.0 + g * (1.0 - sg)))).astype(BF16)
        o_ref[:, F:2 * F] = (d * (g * sg)).astype(BF16)

    return _pcall(body, grid=(S // tm,),
                  in_specs=[pl.BlockSpec((tm, F), lambda i: (i, 0)), pl.BlockSpec((tm, F), lambda i: (i, 0)),
                            pl.BlockSpec((tm, F), lambda i: (i, 1))],
                  out_specs=pl.BlockSpec((tm, F2), lambda i: (i, 0)), out_shape=_sds((S, F2), BF16), name=name,
                  compiler_params=_params(("parallel",)))(dact, gu, gu)


def _final_bwd(x1, f, tgt, vec, name):
    S, D = x1.shape
    tm = min(256, S)

    def body(x_ref, f_ref, t_ref, vec_ref, dx_ref, df_ref, sums_ref, loss_ref):
        @pl.when(pl.program_id(0) == 0)
        def _():
            sums_ref[...] = jnp.zeros_like(sums_ref)
            loss_ref[...] = jnp.zeros_like(loss_ref)

        gate, fg = vec_ref[0:1, :], vec_ref[1:2, :]
        fv = f_ref[...]
        x = x_ref[...] + gate * fv
        r = lax.rsqrt(_rmean(x * x) + EPS)
        xn = x * r
        diff = xn * fg - t_ref[...]
        per_tok = _rmean(diff * diff)
        loss_ref[...] += 0.5 * jnp.sum(per_tok, axis=0, keepdims=True)
        dy = diff * (1.0 / D)
        sums_ref[0:1, :] += _rsum(dy * xn)
        dxn = dy * fg
        dx = r * (dxn - xn * _rmean(dxn * xn))
        sums_ref[1:2, :] += _rsum(dx * fv)
        dx_ref[...] = dx
        df_ref[...] = (dx * gate).astype(BF16)

    rs = _row_spec(tm, D)
    return _pcall(body, grid=(S // tm,), in_specs=[rs, rs, rs, _const_spec((8, D))],
                  out_specs=[rs, rs, _const_spec((8, D)), _const_spec((8, LANE))],
                  out_shape=[_sds((S, D), F32), _sds((S, D), BF16), _sds((8, D), F32), _sds((8, LANE), F32)],
                  name=name, compiler_params=_params(("arbitrary",)))(x1, f, tgt, vec)


def _norm_bwd(xin, dh, dxup, vec, fprev, name, deps=()):
    S, D = xin.shape
    tm = min(256, S)
    has_prev = fprev is not None

    def body(*refs):
        if has_prev:
            x_ref, dh_ref, up_ref, vec_ref, fp_ref, dx_ref, dp_ref, sums_ref = refs
        else:
            x_ref, dh_ref, up_ref, vec_ref, dx_ref, sums_ref = refs

        @pl.when(pl.program_id(0) == 0)
        def _():
            sums_ref[...] = jnp.zeros_like(sums_ref)

        g, scale = vec_ref[0:1, :], vec_ref[1:2, :]
        x = x_ref[...]
        r = lax.rsqrt(_rmean(x * x) + EPS)
        xn = x * r
        dhv = dh_ref[...]
        sums_ref[0:1, :] += _rsum(dhv)
        sums_ref[1:2, :] += _rsum(dhv * (xn * g))
        dm = dhv * (1.0 + scale)
        sums_ref[2:3, :] += _rsum(dm * xn)
        dxn = dm * g
        dx = up_ref[...] + r * (dxn - xn * _rmean(dxn * xn))
        dx_ref[...] = dx
        if has_prev:
            sums_ref[3:4, :] += _rsum(dx * fp_ref[...])
            dp_ref[...] = (dx * vec_ref[2:3, :]).astype(BF16)

    rs = _row_spec(tm, D)
    ins = [xin, dh, dxup, vec] + ([fprev] if has_prev else [])
    in_specs = [rs, rs, rs, _const_spec((8, D))] + ([rs] if has_prev else [])
    out_shape = [_sds((S, D), F32)] + ([_sds((S, D), BF16)] if has_prev else []) + [_sds((8, D), F32)]
    out_specs = [rs] + ([rs] if has_prev else []) + [_const_spec((8, D))]
    outs = _pcall(_after(body, len(ins), deps), grid=(S // tm,), in_specs=in_specs + [ANY] * len(deps),
                  out_specs=out_specs, out_shape=out_shape, name=name,
                  compiler_params=_params(("arbitrary",)))(*ins, *deps)
    return (outs[0], outs[1], outs[2]) if has_prev else (outs[0], None, outs[1])


def _gate_bwd(dmerged, z, ys, name, deps=()):
    S, D = dmerged.shape
    tm = min(256, S)
    ncol = z.shape[1] // D

    def body(dm_ref, g_ref, y_ref, dy_ref, dz_ref):
        sg = _sigmoid(g_ref[...])
        dm = dm_ref[...]
        dy_ref[...] = (dm * sg).astype(BF16)
        dz_ref[...] = (dm * y_ref[...].astype(F32) * (sg * (1.0 - sg))).astype(BF16)

    return _pcall(_after(body, 3, deps), grid=(S // tm, 3),
                  in_specs=[pl.BlockSpec((tm, D), lambda i, n: (i, 0)), pl.BlockSpec((tm, D), lambda i, n: (i, 7 + n)),
                            pl.BlockSpec((None, tm, D), lambda i, n: (n, i, 0))] + [ANY] * len(deps),
                  out_specs=[pl.BlockSpec((None, tm, D), lambda i, n: (n, i, 0)),
                             pl.BlockSpec((tm, D), lambda i, n: (i, 7 + n))],
                  out_shape=[_sds((3, S, D), BF16), _sds((S, ncol * D), BF16)], name=name,
                  compiler_params=_params(("parallel", "arbitrary")))(dmerged, z, ys, *deps)


def _mixer_bwd(z, dacts, dz, wsh, sgu_ln, wtril, wtril_t, bias_full, cw, cvec, name):
    S = z.shape[0]
    D = wsh.shape[1]
    tm = CHUNK
    nt = S // tm
    hb = tm // HALO

    def body(zc, zp, da_ref, db_ref, dc_ref, wsh_ref, sln_ref, wt_ref, wtt_ref, bias_ref, cw_ref, cv_ref, _dz_in,
             dz_ref, vec_ref, dcw_ref, dws_ref, dbs_ref, pe, ge, dqe, dce, gr, dcr):
        i = pl.program_id(0)
        rb = nt - 1 - i

        @pl.when(i == 0)
        def _():
            vec_ref[...] = jnp.zeros_like(vec_ref)
            dcw_ref[...] = jnp.zeros_like(dcw_ref)
            dws_ref[...] = jnp.zeros_like(dws_ref)
            dbs_ref[...] = jnp.zeros_like(dbs_ref)
            dqe[tm:tm + HALO, :] = jnp.zeros((HALO, D), F32)
            dce[tm:tm + HALO, :] = jnp.zeros((HALO, D), F32)

        keep = (rb > 0).astype(F32)
        c_a, x_a = zc[:, D:2 * D], zc[:, 2 * D:3 * D]
        pe[0:HALO, :] = keep * (zp[:, D:2 * D] * zp[:, 2 * D:3 * D])
        pe[HALO:HALO + tm, :] = c_a * x_a
        q = wsh_ref[0:1, :] * pe[HALO - 2:HALO - 2 + tm, :]
        q = q + wsh_ref[1:2, :] * pe[HALO - 1:HALO - 1 + tm, :]
        q = q + wsh_ref[2:3, :] * pe[HALO:HALO + tm, :]
        dact = da_ref[...]
        dz_ref[:, 0:D] = (dact * q).astype(BF16)
        dq = dact * zc[:, 0:D]
        dqe[0:tm, :] = dq
        dp = wsh_ref[2:3, :] * dq + wsh_ref[1:2, :] * dqe[1:1 + tm, :] + wsh_ref[0:1, :] * dqe[2:2 + tm, :]
        dz_ref[:, D:2 * D] = (dp * x_a).astype(BF16)
        dz_ref[:, 2 * D:3 * D] = (dp * c_a).astype(BF16)
        for k in range(SHORT_K):
            o = HALO - (SHORT_K - 1) + k
            vec_ref[k:k + 1, :] += _rsum(dq * pe[o:o + tm, :])
        u, v = zc[:, 3 * D:4 * D], zc[:, 4 * D:5 * D]
        gu, tu = _gelu(u)
        gv, tv = _gelu(v)
        d = gv - _rmean(gv)
        rstd = lax.rsqrt(_rmean(d * d) + EPS)
        nrm = d * rstd
        vnb = (nrm * sln_ref[0:1, :] + sln_ref[1:2, :]).astype(BF16)
        dact = db_ref[...]
        dvn_parts, dgu_parts = [], []
        for g in range(NG):
            cs = slice(g * LANE, (g + 1) * LANE)
            vg = vnb[:, cs]
            mixed = jnp.dot(wt_ref[g], vg, preferred_element_type=F32) + bias_ref[:, cs]
            dgu_parts.append(dact[:, cs] * mixed)
            dmixed = dact[:, cs] * gu[:, cs]
            dmb = dmixed.astype(BF16)
            dws_ref[g] += lax.dot_general(dmb, vg, (((1,), (1,)), ((), ())), preferred_element_type=F32)
            dbs_ref[g] += jnp.broadcast_to(jnp.sum(dmixed, axis=1, keepdims=True), (CHUNK, LANE))
            dvn_parts.append(jnp.dot(wtt_ref[g], dmb, preferred_element_type=F32))
        dgu = jnp.concatenate(dgu_parts, axis=1)
        dvn = jnp.concatenate(dvn_parts, axis=1)
        dz_ref[:, 3 * D:4 * D] = (dgu * _dgelu(u, tu)).astype(BF16)
        vec_ref[3:4, :] += _rsum(dvn * nrm)
        vec_ref[4:5, :] += _rsum(dvn)
        dn = dvn * sln_ref[0:1, :]
        dgv = rstd * (dn - _rmean(dn) - nrm * _rmean(dn * nrm))
        dz_ref[:, 4 * D:5 * D] = (dgv * _dgelu(v, tv)).astype(BF16)
        a_c = zc[:, 5 * D:6 * D]
        sg = _sigmoid(zc[:, 6 * D:7 * D])
        ge[0:HALO, :] = keep * (zp[:, 5 * D:6 * D] * _sigmoid(zp[:, 6 * D:7 * D]))
        ge[HALO:HALO + tm, :] = a_c * sg
        _fill_shifted(ge, gr)
        o0 = HALO - (CFM_K - 1)
        conv = cv_ref[0:1, :] + cw_ref[0:1, :] * _rows_at(ge, gr, o0, tm)
        for k in range(1, CFM_K):
            conv = conv + cw_ref[k:k + 1, :] * _rows_at(ge, gr, o0 + k, tm)
        d = conv - _rmean(conv)
        rstd = lax.rsqrt(_rmean(d * d) + EPS)
        nrm = d * rstd
        ln = nrm * cv_ref[1:2, :] + cv_ref[2:3, :]
        sl = _sigmoid(ln)
        dln = dc_ref[...] * (sl * (1.0 + ln * (1.0 - sl)))
        vec_ref[6:7, :] += _rsum(dln * nrm)
        vec_ref[7:8, :] += _rsum(dln)
        dn = dln * cv_ref[1:2, :]
        dconv = rstd * (dn - _rmean(dn) - nrm * _rmean(dn * nrm))
        vec_ref[5:6, :] += _rsum(dconv)
        dce[0:tm, :] = dconv
        _fill_shifted(dce, dcr)
        dglu = cw_ref[CFM_K - 1:CFM_K, :] * dconv
        for k in range(CFM_K - 1):
            dglu = dglu + cw_ref[k:k + 1, :] * _rows_at(dce, dcr, CFM_K - 1 - k, tm)
        for k in range(CFM_K):
            dcw_ref[k:k + 1, :] += _rsum(dconv * _rows_at(ge, gr, o0 + k, tm))
        dz_ref[:, 5 * D:6 * D] = (dglu * sg).astype(BF16)
        dz_ref[:, 6 * D:7 * D] = (dglu * a_c * (sg * (1.0 - sg))).astype(BF16)
        dqe[tm:tm + HALO, :] = dqe[0:HALO, :]
        dce[tm:tm + HALO, :] = dce[0:HALO, :]

    rev = lambda i: (nt - 1 - i, 0)
    rs = pl.BlockSpec((tm, D), rev)
    cur = pl.BlockSpec((tm, 7 * D), rev)
    prev = pl.BlockSpec((HALO, 7 * D), lambda i: (jnp.maximum((nt - 1 - i) * hb - 1, 0), 0))
    ext = pltpu.VMEM((HALO + tm, D), F32)
    outs = _pcall(
        body, grid=(nt,),
        in_specs=[cur, prev, rs, rs, rs, _const_spec((8, D)), _const_spec((8, D)), _const_spec((NG, CHUNK, CHUNK)),
                  _const_spec((NG, CHUNK, CHUNK)), _const_spec((CHUNK, D)), _const_spec((HALO, D)), _const_spec((8, D)),
                  ANY],
        out_specs=[cur, _const_spec((8, D)), _const_spec((HALO, D)), _const_spec((NG, CHUNK, CHUNK)),
                   _const_spec((NG, CHUNK, LANE))],
        out_shape=[_sds(dz.shape, BF16), _sds((8, D), F32), _sds((HALO, D), F32), _sds((NG, CHUNK, CHUNK), F32),
                   _sds((NG, CHUNK, LANE), F32)],
        scratch_shapes=[ext, ext, ext, ext, pltpu.VMEM((7, HALO + tm, D), F32), pltpu.VMEM((7, HALO + tm, D), F32)],
        input_output_aliases={12: 0}, name=name,
        compiler_params=_params(("arbitrary",)))(z, z, *dacts, wsh, sgu_ln, wtril, wtril_t, bias_full, cw, cvec, dz)
    return outs


def _ada_fwd(c_all, w_ada_loc, name):
    nb, D = c_all.shape
    L, _, nc = w_ada_loc.shape

    def body(c_ref, w_ref, o_ref, ca_ref):
        cv = c_ref[...]
        ca = cv * _sigmoid(cv)
        ca_ref[...] = ca
        o_ref[...] = jnp.dot(ca.astype(BF16), w_ref[...].astype(BF16), preferred_element_type=F32)

    return _pcall(body, grid=(L,),
                  in_specs=[_const_spec((nb, D)), pl.BlockSpec((None, D, nc), lambda l: (l, 0, 0))],
                  out_specs=[pl.BlockSpec((None, nb, nc), lambda l: (l, 0, 0)), _const_spec((nb, D))],
                  out_shape=[_sds((L, nb, nc), F32), _sds((nb, D), F32)], name=name,
                  compiler_params=_params(("arbitrary",)))(c_all, w_ada_loc)


def _adamw(w, g, m, v):
    m = ADAM_B1 * m + (1.0 - ADAM_B1) * g
    v = ADAM_B2 * v + (1.0 - ADAM_B2) * (g * g)
    m_hat = m / (1.0 - ADAM_B1 ** ADAM_STEP)
    v_hat = v / (1.0 - ADAM_B2 ** ADAM_STEP)
    delta = -ADAM_LR * (m_hat / (jnp.sqrt(v_hat) + ADAM_EPS) + ADAM_WD * w)
    return delta, m, v


def _tile_rows(R, C, align=8):
    cap = max(align, (640 * 1024) // (4 * C))
    best = None
    for t in range(align, R + 1, align):
        if R % t == 0 and t <= cap:
            best = t
    return R if best is None else best


def _adam_ada(ct, dm, w, m, v, name):
    L, D, nc = w.shape
    nb = ct.shape[1]
    tr = _tile_rows(D, nc)

    def body(ct_ref, dm_ref, w_ref, m_ref, v_ref, g_ref, d_ref, mo_ref, vo_ref):
        g = ct_ref[:, 0:1] * dm_ref[0:1, :]
        for b in range(1, nb):
            g = g + ct_ref[:, b:b + 1] * dm_ref[b:b + 1, :]
        g_ref[...] = g
        d_ref[...], mo_ref[...], vo_ref[...] = _adamw(w_ref[...], g, m_ref[...], v_ref[...])

    ws = pl.BlockSpec((None, tr, nc), lambda l, r: (l, r, 0))
    return _pcall(body, grid=(L, D // tr),
                  in_specs=[pl.BlockSpec((tr, nb), lambda l, r: (r, 0)), pl.BlockSpec((None, nb, nc), lambda l, r: (l, 0, 0)),
                            ws, ws, ws],
                  out_specs=[ws] * 4, out_shape=[_sds(w.shape, F32)] * 4, name=name,
                  compiler_params=_params(("parallel", "parallel")))(ct, dm, w, m, v)


def _adam_small(parts, w, m, v, name):
    n, R, C = parts.shape
    tr = _tile_rows(R, C * n // 2)

    def body(p_ref, w_ref, m_ref, v_ref, g_ref, d_ref, mo_ref, vo_ref):
        g = p_ref[0]
        for j in range(1, n):
            g = g + p_ref[j]
        g_ref[...] = g
        d_ref[...], mo_ref[...], vo_ref[...] = _adamw(w_ref[...], g, m_ref[...], v_ref[...])

    ws = pl.BlockSpec((tr, C), lambda r: (r, 0))
    return _pcall(body, grid=(R // tr,), in_specs=[pl.BlockSpec((n, tr, C), lambda r: (0, r, 0)), ws, ws, ws],
                  out_specs=[ws] * 4, out_shape=[_sds((R, C), F32)] * 4, name=name,
                  compiler_params=_params(("parallel",)))(parts, w, m, v)


def _adam_plain(g, w, m, v, name):
    R, C = w.shape

    def body(g_ref, w_ref, m_ref, v_ref, d_ref, mo_ref, vo_ref):
        d_ref[...], mo_ref[...], vo_ref[...] = _adamw(w_ref[...], g_ref[...], m_ref[...], v_ref[...])

    ws = _const_spec((R, C))
    return _pcall(body, grid=(1,), in_specs=[ws] * 4, out_specs=[ws] * 3, out_shape=[_sds((R, C), F32)] * 3, name=name,
                  compiler_params=_params(("arbitrary",)))(g, w, m, v)


def _pair_sum(G, R1, my_c, name):
    n, R, C = G.shape
    half = n // 2
    tr = _tile_rows(R, C, align=16)

    def body(c_ref, g_ref, r_ref, o_ref):
        o_ref[...] = (g_ref[...].astype(F32) + r_ref[...].astype(F32)).astype(o_ref.dtype)

    blk = (None, tr, C)
    gs = pltpu.PrefetchScalarGridSpec(
        num_scalar_prefetch=1, grid=(half, R // tr),
        in_specs=[pl.BlockSpec(blk, lambda p, r, c: (2 * p + c[0], r, 0)), pl.BlockSpec(blk, lambda p, r, c: (p, r, 0))],
        out_specs=pl.BlockSpec(blk, lambda p, r, c: (p, r, 0)))
    return _pcall(body, grid_spec=gs, out_shape=_sds((half, R, C), G.dtype), name=name,
                  compiler_params=_params(("parallel", "parallel")))(my_c, G, R1)


def _adam_big(P, R2, my_chip, w, m, v, layer, prev, name):
    _, R, C = P.shape
    nrecv = R2.shape[0]
    tr = _tile_rows(R, C, align=16)

    def body(p_sm, p_ref, r_ref, w_ref, m_ref, v_ref, *rest):
        g_ref, d_ref, mo_ref, vo_ref = rest[-4:]
        g = p_ref[...].astype(F32)
        for k in range(nrecv):
            g = g + r_ref[k].astype(F32)
        g_ref[...] = g
        d_ref[...], mo_ref[...], vo_ref[...] = _adamw(w_ref[...], g, m_ref[...], v_ref[...])

    ws = pl.BlockSpec((None, tr, C), lambda r, p: (layer, r, 0))
    held = [] if prev is None else list(prev)
    gs = pltpu.PrefetchScalarGridSpec(
        num_scalar_prefetch=1, grid=(R // tr,),
        in_specs=[pl.BlockSpec((None, tr, C), lambda r, p: (p[0], r, 0)),
                  pl.BlockSpec((nrecv, tr, C), lambda r, p: (0, r, 0)), ws, ws, ws] + [ANY] * len(held),
        out_specs=[ws] * 4)
    alias = {6 + i: i for i in range(len(held))}
    return _pcall(body, grid_spec=gs, out_shape=[_sds(w.shape, F32)] * 4, name=name, input_output_aliases=alias,
                  compiler_params=_params(("parallel",)))(my_chip, P, R2, w, m, v, *held)


def _place():
    return lax.axis_index("x"), lax.axis_index("y"), lax.axis_index("c")


def _sum_over_devices(scalar):
    return lax.psum(scalar, ("x", "y", "c"))


def _all_gather(shards, name, deps=()):
    n = len(shards)

    def body(*refs):
        ins, outs = refs[:n], refs[n:2 * n]
        send_sems, recv_sems, local_sems = refs[2 * n:]
        x, y, c = _place()
        me, sibling = (x, y, c), (x, y, 1 - c)
        chips = [(1 - x, y), (x, 1 - y), (1 - x, 1 - y)]

        def slot(a, px, py, pc):
            return outs[a].at[4 * px + 2 * py + pc]

        def copy(a, k, block, to, src=None):
            return pltpu.make_async_remote_copy(
                src_ref=slot(a, *block) if src is None else src, dst_ref=slot(a, *block),
                send_sem=send_sems.at[7 * a + k], recv_sem=recv_sems.at[7 * a + k], device_id=to, device_id_type=MESH)

        mine = [pltpu.make_async_copy(ins[a], slot(a, *me), local_sems.at[a]) for a in range(n)]
        for cp in mine:
            cp.start()
        first = []
        for a in range(n):
            first.append(copy(a, 0, me, sibling, src=ins[a]))
            first += [copy(a, 1 + j, me, (*chip, c), src=ins[a]) for j, chip in enumerate(chips)]
        for cp in first:
            cp.start()
        passed = []
        for j, chip in enumerate(chips):
            for a in range(n):
                copy(a, 1 + j, (*chip, c), me).wait_recv()
                fwd = copy(a, 4 + j, (*chip, c), sibling)
                fwd.start()
                passed.append(fwd)
        for a in range(n):
            copy(a, 0, sibling, me).wait_recv()
        for j, chip in enumerate(chips):
            for a in range(n):
                copy(a, 4 + j, (*chip, 1 - c), me).wait_recv()
        for cp in first + passed:
            cp.wait_send()
        for cp in mine:
            cp.wait()

    outs = _pcall(_after(body, n, deps), in_specs=[ANY] * (n + len(deps)), out_specs=[ANY] * n,
                  out_shape=[_sds((NDEV,) + s.shape, s.dtype) for s in shards],
                  scratch_shapes=[pltpu.SemaphoreType.DMA((7 * n,)), pltpu.SemaphoreType.DMA((7 * n,)),
                                  pltpu.SemaphoreType.DMA((n,))], name=name)(*shards, *deps)
    return list(outs)


HBM = pl.BlockSpec(memory_space=pltpu.HBM)
SEM = pl.BlockSpec(memory_space=pltpu.SEMAPHORE)


def _copies(plan, refs, send_sems, recv_sems):
    return [pltpu.make_async_remote_copy(src_ref=s, dst_ref=d, send_sem=send_sems.at[k], recv_sem=recv_sems.at[k],
                                         device_id=dev, device_id_type=MESH)
            for k, (s, d, dev) in enumerate(plan(refs, *_place()))]


def _xfer_start(bufs, ncopies, plan, name):
    n = len(bufs)

    def body(*refs):
        for cp in _copies(plan, refs[:n], refs[n], refs[n + 1]):
            cp.start()
        token = refs[2 * n + 2]
        token[...] = jnp.zeros_like(token)

    outs = _pcall(
        body, name=name,
        out_shape=(pltpu.SemaphoreType.DMA((ncopies,)), pltpu.SemaphoreType.DMA((ncopies,)),
                   *[pltpu.HBM(b.shape, b.dtype) for b in bufs], _sds((8, LANE), F32)),
        in_specs=[HBM] * n, out_specs=(SEM, SEM, *[HBM] * n, pl.BlockSpec(memory_space=pltpu.VMEM)),
        input_output_aliases={i: 2 + i for i in range(n)},
        compiler_params=pltpu.CompilerParams(has_side_effects=pltpu.SideEffectType.DATAFLOW_SIDE_EFFECTING),
    )(*[pltpu.with_memory_space_constraint(b, pltpu.HBM) for b in bufs])
    return (outs[0], outs[1]), list(outs[2:2 + n]), outs[2 + n]


def _xfer_wait(sems, bufs, plan, after, name):
    n = len(bufs)

    def body(*refs):
        for cp in _copies(plan, refs[:n], refs[n], refs[n + 1]):
            cp.wait_send()
            cp.wait_recv()

    outs = _pcall(
        body, name=name, out_shape=tuple(pltpu.HBM(b.shape, b.dtype) for b in bufs),
        in_specs=[HBM] * n + [SEM, SEM, ANY], out_specs=tuple([HBM] * n), input_output_aliases={i: i for i in range(n)},
        compiler_params=pltpu.CompilerParams(has_side_effects=pltpu.SideEffectType.DATAFLOW_SIDE_EFFECTING),
    )(*bufs, *sems, after)
    return list(outs)


def _chips_of(x, y):
    return [(1 - x, y), (x, 1 - y), (1 - x, 1 - y)]


def _gather_plan1(n):
    def plan(refs, x, y, c):
        out = []
        for a in range(n):
            blk = refs[a].at[4 * x + 2 * y + c]
            out.append((blk, blk, (x, y, 1 - c)))
            out += [(blk, blk, (px, py, c)) for px, py in _chips_of(x, y)]
        return out
    return plan


def _gather_plan2(n):
    def plan(refs, x, y, c):
        out = []
        for a in range(n):
            for px, py in _chips_of(x, y):
                blk = refs[a].at[4 * px + 2 * py + c]
                out.append((blk, blk, (x, y, 1 - c)))
        return out
    return plan


def _gather_start(shards, dev, name):
    lands = [lax.dynamic_update_slice(lax.empty((NDEV,) + s.shape, s.dtype), s[None], (dev,) + (0,) * s.ndim)
             for s in shards]
    n = len(shards)
    sems, lands, tok = _xfer_start(lands, 4 * n, _gather_plan1(n), name + "_p1_start")
    return dict(sems=sems, lands=lands, tok=tok, n=n)


def _gather_mid(st, after, name):
    n = st["n"]
    lands = _xfer_wait(st["sems"], st["lands"], _gather_plan1(n), after, name + "_p1_wait")
    sems, lands, tok = _xfer_start(lands, 3 * n, _gather_plan2(n), name + "_p2_start")
    return dict(sems=sems, lands=lands, tok=tok, n=n)


def _gather_finish(st, after, name):
    return _xfer_wait(st["sems"], st["lands"], _gather_plan2(st["n"]), after, name + "_p2_wait")


def _scatter_plan1(n):
    def plan(refs, x, y, c):
        return [(refs[a].at[2 * p + 1 - c], refs[n + a].at[p], (x, y, 1 - c)) for a in range(n) for p in range(NCHIP)]
    return plan


def _scatter_plan2(n):
    def plan(refs, x, y, c):
        return [(refs[a].at[2 * px + py], refs[n + a].at[j], (px, py, c))
                for a in range(n) for j, (px, py) in enumerate(_chips_of(x, y))]
    return plan


def _scatter_start(Gs, name):
    n = len(Gs)
    R1s = [lax.empty((NCHIP,) + g.shape[1:], g.dtype) for g in Gs]
    sems, bufs, tok = _xfer_start(list(Gs) + R1s, NCHIP * n, _scatter_plan1(n), name + "_s1_start")
    return dict(sems=sems, bufs=bufs, tok=tok, n=n)


def _scatter_mid(st, after, my_c, name):
    n = st["n"]
    bufs = _xfer_wait(st["sems"], st["bufs"], _scatter_plan1(n), after, name + "_s1_wait")
    Ps = [_pair_sum(bufs[a], bufs[n + a], my_c, f"{name}_pair_sum{a}") for a in range(n)]
    R2s = [lax.empty((3,) + p.shape[1:], p.dtype) for p in Ps]
    sems, bufs, tok = _xfer_start(Ps + R2s, 3 * n, _scatter_plan2(n), name + "_s2_start")
    return dict(sems=sems, bufs=bufs, tok=tok, n=n)


def _scatter_finish(st, after, name):
    n = st["n"]
    bufs = _xfer_wait(st["sems"], st["bufs"], _scatter_plan2(n), after, name + "_s2_wait")
    return bufs[:n], bufs[n:]


SMALL_ROWS = {"norm1_g": (0, 1), "norm2_g": (1, 1), "sgu_ln_g": (2, 1), "sgu_ln_b": (3, 1), "cfm_conv_b": (4, 1),
              "cfm_ln_g": (5, 1), "cfm_ln_b": (6, 1), "b_sgu": (7, 1), "w_sgu": (8, 128), "b_ada": (136, N_MOD),
              "w_short": (142, SHORT_K), "cfm_conv_w": (145, CFM_K)}
ROWS_PER_LAYER = 176
FINAL_ROW = DEPTH * ROWS_PER_LAYER
PACK_ROWS = 360


def _pack(get, D):
    parts = []
    for l in range(DEPTH):
        for name, (_, nrows) in SMALL_ROWS.items():
            a = get(name, l)
            parts.append(jnp.zeros((nrows * D,), F32) if a is None else a.astype(F32).reshape(nrows * D))
    fin = get("final_g", None)
    parts.append(fin.astype(F32).reshape(D))
    parts.append(jnp.zeros(((PACK_ROWS - FINAL_ROW - 1) * D,), F32))
    return jnp.concatenate(parts).reshape(PACK_ROWS, D)


def _unpack(pack, name, shape):
    D = pack.shape[1]
    r0, nrows = SMALL_ROWS[name]
    return jnp.stack([pack[l * ROWS_PER_LAYER + r0:l * ROWS_PER_LAYER + r0 + nrows] for l in range(DEPTH)]).reshape(shape)


def _mm_tiles(S):
    return min(512, S), min(1024, S)


def kernel(x, c, w_ada, b_ada, norm1_g, w_in, w_short, w_a_out, sgu_ln_g, sgu_ln_b, w_sgu, b_sgu, w_b_out, cfm_conv_w, cfm_conv_b, cfm_ln_g, cfm_ln_b, w_c_out, w_o, norm2_g, w_ffn_in, w_ffn_out, final_g, loss_target, m_w_ada, m_b_ada, m_norm1_g, m_w_in, m_w_short, m_w_a_out, m_sgu_ln_g, m_sgu_ln_b, m_w_sgu, m_b_sgu, m_w_b_out, m_cfm_conv_w, m_cfm_conv_b, m_cfm_ln_g, m_cfm_ln_b, m_w_c_out, m_w_o, m_norm2_g, m_w_ffn_in, m_w_ffn_out, m_final_g, v_w_ada, v_b_ada, v_norm1_g, v_w_in, v_w_short, v_w_a_out, v_sgu_ln_g, v_sgu_ln_b, v_w_sgu, v_b_sgu, v_w_b_out, v_cfm_conv_w, v_cfm_conv_b, v_cfm_ln_g, v_cfm_ln_b, v_w_c_out, v_w_o, v_norm2_g, v_w_ffn_in, v_w_ffn_out, v_final_g):
    W = dict(w_ada=w_ada, b_ada=b_ada, norm1_g=norm1_g, w_in=w_in, w_short=w_short, w_a_out=w_a_out, sgu_ln_g=sgu_ln_g,
             sgu_ln_b=sgu_ln_b, w_sgu=w_sgu, b_sgu=b_sgu, w_b_out=w_b_out, cfm_conv_w=cfm_conv_w, cfm_conv_b=cfm_conv_b,
             cfm_ln_g=cfm_ln_g, cfm_ln_b=cfm_ln_b, w_c_out=w_c_out, w_o=w_o, norm2_g=norm2_g, w_ffn_in=w_ffn_in,
             w_ffn_out=w_ffn_out, final_g=final_g)
    Mo = dict(w_ada=m_w_ada, b_ada=m_b_ada, norm1_g=m_norm1_g, w_in=m_w_in, w_short=m_w_short, w_a_out=m_w_a_out,
              sgu_ln_g=m_sgu_ln_g, sgu_ln_b=m_sgu_ln_b, w_sgu=m_w_sgu, b_sgu=m_b_sgu, w_b_out=m_w_b_out,
              cfm_conv_w=m_cfm_conv_w, cfm_conv_b=m_cfm_conv_b, cfm_ln_g=m_cfm_ln_g, cfm_ln_b=m_cfm_ln_b,
              w_c_out=m_w_c_out, w_o=m_w_o, norm2_g=m_norm2_g, w_ffn_in=m_w_ffn_in, w_ffn_out=m_w_ffn_out,
              final_g=m_final_g)
    Vo = dict(w_ada=v_w_ada, b_ada=v_b_ada, norm1_g=v_norm1_g, w_in=v_w_in, w_short=v_w_short, w_a_out=v_w_a_out,
              sgu_ln_g=v_sgu_ln_g, sgu_ln_b=v_sgu_ln_b, w_sgu=v_w_sgu, b_sgu=v_b_sgu, w_b_out=v_w_b_out,
              cfm_conv_w=v_cfm_conv_w, cfm_conv_b=v_cfm_conv_b, cfm_ln_g=v_cfm_ln_g, cfm_ln_b=v_cfm_ln_b,
              w_c_out=v_w_c_out, w_o=v_w_o, norm2_g=v_norm2_g, w_ffn_in=v_w_ffn_in, w_ffn_out=v_w_ffn_out,
              final_g=v_final_g)
    order = ["w_ada", "b_ada", "norm1_g", "w_in", "w_short", "w_a_out", "sgu_ln_g", "sgu_ln_b", "w_sgu", "b_sgu",
             "w_b_out", "cfm_conv_w", "cfm_conv_b", "cfm_ln_g", "cfm_ln_b", "w_c_out", "w_o", "norm2_g", "w_ffn_in",
             "w_ffn_out", "final_g"]

    assert DEPTH == 2, "the weight-gather schedule below is written for two layers"
    S, D = x.shape[1], x.shape[2]
    F2 = w_ffn_in.shape[2] * NDEV
    FF = F2 // 2
    xi, yi, ci = _place()
    dev = 4 * xi + 2 * yi + ci
    my_c = jnp.reshape(ci, (1,)).astype(jnp.int32)
    my_chip = jnp.reshape(2 * xi + yi, (1,)).astype(jnp.int32)
    tm, tm_big = _mm_tiles(S)
    x0 = x.reshape(S, D)
    tgt = loss_target.reshape(S, D)

    c_all = _all_gather([jnp.pad(c, ((0, 7), (0, 0)))], "ag_c")[0][:, 0, :]
    modpart, c_act = _ada_fwd(c_all, w_ada, "ada_fwd")
    ncol = modpart.shape[2]
    mg = _all_gather([modpart.reshape(DEPTH * NDEV, ncol)], "ag_mod")[0].reshape(NDEV, DEPTH, NDEV, ncol)
    mine = lax.dynamic_index_in_dim(mg, dev, axis=2, keepdims=False)
    mod = (jnp.transpose(mine, (1, 0, 2)).reshape(DEPTH, N_MOD * D) + b_ada).reshape(DEPTH, N_MOD, D)

    tril = jnp.tril(jnp.ones((CHUNK, CHUNK), dtype=bool))

    def layer_consts(l):
        wt = jnp.where(tril[None], w_sgu[l], 0.0).astype(BF16)
        return dict(
            wsh=jnp.pad(w_short_full[l], ((0, 8 - SHORT_K), (0, 0))),
            sgu_ln=_rows(sgu_ln_g[l], sgu_ln_b[l]),
            wtril=wt, wtril_t=jnp.swapaxes(wt, 1, 2),
            bias_full=jnp.repeat(b_sgu[l].T, LANE, axis=1),
            cw=jnp.pad(cfm_w_full[l], ((0, HALO - CFM_K), (0, 0))),
            cvec=_rows(cfm_conv_b[l], cfm_ln_g[l], cfm_ln_b[l]))

    ncs = w_short.shape[2]
    sw = _all_gather([w_short.reshape(DEPTH * SHORT_K, ncs), cfm_conv_w.reshape(DEPTH * CFM_K, ncs)], "ag_convw")
    w_short_full = jnp.transpose(sw[0], (1, 0, 2)).reshape(DEPTH, SHORT_K, D)
    cfm_w_full = jnp.transpose(sw[1], (1, 0, 2)).reshape(DEPTH, CFM_K, D)

    def shards_of(l):
        return [w_in[l].astype(BF16), w_a_out[l].astype(BF16), w_b_out[l].astype(BF16), w_c_out[l].astype(BF16),
                w_o[l].astype(BF16), w_ffn_in[l].astype(BF16), w_ffn_out[l].astype(BF16)]

    def rest_of(g):
        return dict(w_a=g[0].reshape(1, D, D), w_b=g[1].reshape(1, D, D), w_c=g[2].reshape(1, D, D),
                    w_o=g[3].reshape(1, D, D), w_fi=jnp.transpose(g[4], (1, 0, 2)).reshape(1, D, F2),
                    w_fo=g[5].reshape(1, FF, D))

    Wg = [dict(w_in=_all_gather(shards_of(0)[:1], "ag_w_in0")[0]), None]
    ag_rest0 = _gather_start(shards_of(0)[1:], dev, "ag_rest0")
    ag_l1 = None
    nin = Wg[0]["w_in"].shape[2]
    tn_in = nin if nin % 256 == 0 and nin <= 1280 else 256
    tn_fi = 512 if F2 % 512 == 0 else 256

    saved = []
    xcur, fprev, gprev = x0, None, None
    for l in range(DEPTH):
        sh1, sc1, g1, sh2, sc2, g2 = [mod[l, k] for k in range(N_MOD)]
        cl = layer_consts(l)
        vec1 = _rows(jnp.zeros((D,), F32) if gprev is None else gprev, norm1_g[l], sc1, sh1)
        if l == 0:
            xl, h = _norm_fwd(xcur, fprev, vec1, f"norm1_fwd{l}", deps=(ag_rest0["tok"],))
        else:
            ag_l1 = _gather_mid(ag_l1, fprev, f"ag_w{l}")
            xl, h = _norm_fwd(xcur, fprev, vec1, f"norm1_fwd{l}", deps=(ag_l1["tok"],))
            g = _gather_finish(ag_l1, h, f"ag_w{l}")
            Wg[l] = dict(w_in=g[0], **rest_of(g[1:]))
        wl = Wg[l]
        z = _mm_nn(h, wl["w_in"], F32, tm, tn_in, D, f"mm_in{l}", w_outer=True)
        mix_deps = ()
        if l == 0:
            ag_rest0 = _gather_mid(ag_rest0, z, "ag_rest0")
            mix_deps = (ag_rest0["tok"],)
            if DEPTH > 1:
                ag_l1 = _gather_start(shards_of(1), dev, "ag_w1")
                mix_deps += (ag_l1["tok"],)
        acts = _mixer_fwd(z, cl["wsh"], cl["sgu_ln"], cl["wtril"], cl["bias_full"], cl["cw"], cl["cvec"], f"mixer_fwd{l}",
                          deps=mix_deps)
        if l == 0:
            wl.update(rest_of(_gather_finish(ag_rest0, acts[0], "ag_rest0")))
        merged, ys = _branch_out(acts, [wl["w_a"][0], wl["w_b"][0], wl["w_c"][0]], z, f"branch_out{l}")
        o = _mm_nn(merged, wl["w_o"], F32, tm, D, D, f"mm_o{l}")
        x1, h2 = _norm_fwd(xl, o, _rows(g1, norm2_g[l], sc2, sh2), f"norm2_fwd{l}")
        gu = _mm_nn(h2, wl["w_fi"], F32, tm_big, tn_fi, D, f"mm_ffn_in{l}")
        act = _swiglu_fwd(gu, f"swiglu_fwd{l}")
        f = _mm_nn(act, wl["w_fo"], F32, tm, D, FF, f"mm_ffn_out{l}")
        saved.append(dict(xl=xl, h=h, z=z, acts=acts, ys=ys, merged=merged, o=o, x1=x1, h2=h2, gu=gu, act=act, f=f,
                          consts=cl, mod=(sh1, sc1, g1, sh2, sc2, g2)))
        xcur, fprev, gprev = x1, f, g2

    last = saved[-1]
    dxup, dfb, fsums, loss_blk = _final_bwd(last["x1"], last["f"], tgt, _rows(last["mod"][5], final_g), "final_bwd")
    loss = _sum_over_devices(loss_blk[0, 0])
    dgate2_next = fsums[1]
    small = [dict() for _ in range(DEPTH)]
    dmods = [None] * DEPTH
    nfi = w_ffn_in.shape[2]
    early_names, late_names = ["w_ffn_out", "w_ffn_in", "w_o"], ["w_a_out", "w_b_out", "w_c_out", "w_in"]
    results = {n: None for n in early_names + late_names}

    def adam_group(names, Ps, R2s, l):
        for n, p, r2 in zip(names, Ps, R2s):
            results[n] = _adam_big(p, r2, my_chip, W[n], Mo[n], Vo[n], l, results[n], f"adam_{n}{l}")

    late_prev = None
    for l in reversed(range(DEPTH)):
        sv, wl, cl = saved[l], Wg[l], saved[l]["consts"]
        sh1, sc1, g1, sh2, sc2, g2 = sv["mod"]
        dact = _mm_nt(dfb, wl["w_fo"], F32, tm, FF, D, f"mm_dact{l}", deps=() if late_prev is None else (late_prev["tok"],))
        g_fo = _mm_tn(sv["act"], dfb, 1, FF // 2, D, tm_big, f"mm_dw_ffn_out{l}")
        dgu = _swiglu_bwd(dact, sv["gu"], f"swiglu_bwd{l}")
        dh2 = _mm_nt(dgu, wl["w_fi"], F32, tm_big, D, tn_fi, f"mm_dh2{l}")
        if late_prev is not None:
            adam_group(late_names, *_scatter_finish(late_prev, dh2, f"rs_late{l + 1}"), l + 1)
            late_prev = None
        g_fi = _mm_tn(sv["h2"], dgu, 1, D, tn_fi, tm_big, f"mm_dw_ffn_in{l}")
        dx1, dob, s2 = _norm_bwd(sv["x1"], dh2, dxup, _rows(norm2_g[l], sc2, g1), sv["o"], f"norm2_bwd{l}")
        dmerged = _mm_nt(dob, wl["w_o"], F32, tm, D, D, f"mm_dmerged{l}")
        g_o = _mm_tn(sv["merged"], dob, 1, D, D, tm_big, f"mm_dw_o{l}")
        early = _scatter_start([g_fo.reshape(NDEV, FF // NDEV, D),
                                jnp.transpose(g_fi.reshape(D, NDEV, nfi), (1, 0, 2)),
                                g_o.reshape(NDEV, D // NDEV, D)], f"rs_early{l}")
        dys, dz = _gate_bwd(dmerged, sv["z"], sv["ys"], f"gate_bwd{l}", deps=(early["tok"],))
        early = _scatter_mid(early, dys, my_c, f"rs_early{l}")
        dacts, g_abc = [], []
        for n, key in enumerate(("w_a", "w_b", "w_c")):
            dacts.append(_mm_nt(dys[n], wl[key], F32, tm, D, D, f"mm_dact_{key}{l}", deps=(early["tok"],) if n == 0 else ()))
            g_abc.append(_mm_tn(sv["acts"][n], dys[n], 1, D, D, tm_big, f"mm_d{key}{l}"))
        dz, mvec, dcw, dws, dbs = _mixer_bwd(sv["z"], dacts, dz, cl["wsh"], cl["sgu_ln"], cl["wtril"], cl["wtril_t"],
                                             cl["bias_full"], cl["cw"], cl["cvec"], f"mixer_bwd{l}")
        dh = _mm_nt(dz, wl["w_in"], F32, tm_big, D, tn_in, f"mm_dh{l}")
        g_in = _mm_tn(sv["h"], dz, NDEV, D, tn_in, tm_big, f"mm_dw_in{l}")
        late = _scatter_start([g.reshape(NDEV, D // NDEV, D) for g in g_abc] + [g_in], f"rs_late{l}")
        if l > 0:
            pv = saved[l - 1]
            dxup, dfb, s1 = _norm_bwd(sv["xl"], dh, dx1, _rows(norm1_g[l], sc1, pv["mod"][5]), pv["f"], f"norm1_bwd{l}",
                                      deps=(late["tok"],))
        else:
            dxup, dfb, s1 = _norm_bwd(sv["xl"], dh, dx1, _rows(norm1_g[l], sc1), None, f"norm1_bwd{l}", deps=(late["tok"],))
        adam_group(early_names, *_scatter_finish(early, dxup, f"rs_early{l}"), l)
        late_prev = _scatter_mid(late, results["w_o"][0], my_c, f"rs_late{l}")
        dmods[l] = jnp.stack([s1[0], s1[1], s2[3], s2[0], s2[1], dgate2_next])
        dgate2_next = s1[3]
        small[l] = dict(norm1_g=s1[2], norm2_g=s2[2], sgu_ln_g=mvec[3], sgu_ln_b=mvec[4], cfm_conv_b=mvec[5],
                        cfm_ln_g=mvec[6], cfm_ln_b=mvec[7], b_sgu=dbs[:, :, 0],
                        w_sgu=jnp.where(tril[None], dws, 0.0), b_ada=dmods[l], w_short=mvec[0:SHORT_K],
                        cfm_conv_w=dcw[0:CFM_K])
    grad_x = dxup.reshape(x.shape)

    gpack = _pack(lambda name, l: fsums[0] if name == "final_g" else small[l][name], D)
    gathered = _all_gather([gpack], "ag_small", deps=(late_prev["tok"],))[0]
    sharded_small = ("w_short", "cfm_conv_w")
    packs = [_pack(lambda name, l, T=T: T["final_g"] if name == "final_g" else (None if name in sharded_small else T[name][l]), D)
             for T in (W, Mo, Vo)]
    sg, sd, sm, sv_ = _adam_small(gathered, *packs, name="adam_small")
    out = {}
    for name in order:
        if name in SMALL_ROWS and name not in sharded_small:
            out[name] = tuple(_unpack(p, name, W[name].shape) for p in (sg, sd, sm, sv_))
    out["final_g"] = tuple(p[FINAL_ROW] for p in (sg, sd, sm, sv_))

    def my_cols(name):
        full = _unpack(sg, name, (DEPTH, SMALL_ROWS[name][1], D))
        return lax.dynamic_slice_in_dim(full, dev * ncs, ncs, axis=2)

    gcs = jnp.concatenate([my_cols("w_short").reshape(-1, ncs), my_cols("cfm_conv_w").reshape(-1, ncs)])
    ncr = gcs.shape[0]
    padr = (-ncr) % 8
    cat = lambda T: jnp.pad(jnp.concatenate([T["w_short"].reshape(-1, ncs), T["cfm_conv_w"].reshape(-1, ncs)]), ((0, padr), (0, 0)))
    cd, cm, cv = _adam_plain(jnp.pad(gcs, ((0, padr), (0, 0))), cat(W), cat(Mo), cat(Vo), "adam_convw")
    nsh = DEPTH * SHORT_K
    out["w_short"] = tuple(a[0:nsh].reshape(w_short.shape) for a in (gcs, cd, cm, cv))
    out["cfm_conv_w"] = tuple(a[nsh:ncr].reshape(cfm_conv_w.shape) for a in (gcs, cd, cm, cv))

    dm_all = jnp.stack([gathered[:, l * ROWS_PER_LAYER + 136:l * ROWS_PER_LAYER + 136 + N_MOD, :].reshape(NDEV, N_MOD * D)
                        for l in range(DEPTH)])
    dm_mine = lax.dynamic_slice_in_dim(dm_all, dev * ncol, ncol, axis=2)
    out["w_ada"] = tuple(_adam_ada(jnp.transpose(c_act), dm_mine, w_ada, m_w_ada, v_w_ada, "adam_ada"))

    adam_group(late_names, *_scatter_finish(late_prev, out["w_ada"][0], "rs_late0"), 0)
    for n in early_names + late_names:
        out[n] = tuple(results[n])

    grads = [out[n][0] for n in order]
    deltas = [out[n][1] for n in order]
    new_m = [out[n][2] for n in order]
    new_v = [out[n][3] for n in order]
    return (loss, grad_x, *grads, *deltas, *new_m, *new_v)
```

```python
import functools
import math

import jax
import jax.numpy as jnp
from jax import lax
from jax.experimental import pallas as pl
from jax.experimental.pallas import tpu as pltpu

F32, BF16 = jnp.float32, jnp.bfloat16
NDEV = 8
NCHIP = NDEV // 2
DEPTH = 2
EPS = 1e-6
CHUNK = 128
NG = 8
SHORT_K = 3
CFM_K = 31
HALO = 32
N_MOD = 6
LANE = 128
VMEM_LIMIT = 56 * 1024 * 1024
ADAM_LR, ADAM_B1, ADAM_B2, ADAM_EPS, ADAM_WD, ADAM_STEP = 0.001, 0.9, 0.999, 1e-08, 0.01, 10
_G0 = math.sqrt(2.0 / math.pi)
_G1 = 0.044715
MESH = pl.DeviceIdType.MESH
ANY = pl.BlockSpec(memory_space=pl.ANY)


def _pcall(body, **kw):
    return pl.pallas_call(body, **kw)


def _params(sem=None):
    return pltpu.CompilerParams(dimension_semantics=sem, vmem_limit_bytes=VMEM_LIMIT)


def _sds(shape, dtype):
    return jax.ShapeDtypeStruct(tuple(shape), dtype)


def _mm_body(dims, nk, out_f32):
    def body(a_ref, b_ref, o_ref, *scr):
        k = pl.program_id(2)
        part = lax.dot_general(a_ref[...], b_ref[...], dims, preferred_element_type=F32)
        if nk == 1:
            o_ref[...] = part.reshape(o_ref.shape).astype(o_ref.dtype)
        elif out_f32:
            @pl.when(k == 0)
            def _():
                o_ref[...] = part.reshape(o_ref.shape)

            @pl.when(k > 0)
            def _():
                o_ref[...] += part.reshape(o_ref.shape)
        else:
            acc = scr[0]

            @pl.when(k == 0)
            def _():
                acc[...] = part

            @pl.when(k > 0)
            def _():
                acc[...] += part

            @pl.when(k == nk - 1)
            def _():
                o_ref[...] = acc[...].astype(o_ref.dtype)
    return body


def _after(body, n_in, deps):
    nd = len(deps)
    if nd == 0:
        return body

    def ordered(*refs):
        return body(*refs[:n_in], *refs[n_in + nd:])
    return ordered


def _mm_call(body, grid, in_specs, out_spec, out_shape, acc_shape, name, deps=()):
    scratch = [] if acc_shape is None else [pltpu.VMEM(acc_shape, F32)]
    return _pcall(_after(body, 2, deps), grid=grid, in_specs=in_specs + [ANY] * len(deps), out_specs=out_spec,
                  out_shape=out_shape, scratch_shapes=scratch, name=name,
                  compiler_params=_params(("parallel", "parallel", "arbitrary")))


def _mm_nn(a, b3, out_dtype, tm, tn, tk, name, w_outer=False, deps=()):
    M, K = a.shape
    G, _, Nb = b3.shape
    npb, nk = Nb // tn, K // tk
    out_f32 = out_dtype == F32
    body = _mm_body((((1,), (0,)), ((), ())), nk, out_f32)
    if w_outer:
        grid = (G * npb, M // tm, nk)
        ij = lambda p, q: (q, p)
    else:
        grid = (M // tm, G * npb, nk)
        ij = lambda p, q: (p, q)

    def a_map(p, q, k):
        i, j = ij(p, q)
        return (i, k)

    def b_map(p, q, k):
        i, j = ij(p, q)
        return (j // npb, k, j % npb)

    def o_map(p, q, k):
        return ij(p, q)

    def wrapped(a_ref, b_ref, o_ref, *scr):
        body(a_ref, b_ref, o_ref, *scr)

    return _mm_call(wrapped, grid, [pl.BlockSpec((tm, tk), a_map), pl.BlockSpec((None, tk, tn), b_map)],
                    pl.BlockSpec((tm, tn), o_map), _sds((M, G * Nb), out_dtype),
                    None if (nk == 1 or out_f32) else (tm, tn), name, deps)(a, b3, *deps)


def _mm_nt(a, b3, out_dtype, tm, tn, tk, name, deps=()):
    M, _ = a.shape
    G, Ko, Nb = b3.shape
    kpb = Nb // tk
    nk = G * kpb
    out_f32 = out_dtype == F32
    body = _mm_body((((1,), (1,)), ((), ())), nk, out_f32)

    def wrapped(a_ref, b_ref, o_ref, *scr):
        body(a_ref, b_ref, o_ref, *scr)

    return _mm_call(wrapped, (M // tm, Ko // tn, nk),
                    [pl.BlockSpec((tm, tk), lambda i, j, k: (i, k)),
                     pl.BlockSpec((None, tn, tk), lambda i, j, k: (k // kpb, j, k % kpb))],
                    pl.BlockSpec((tm, tn), lambda i, j, k: (i, j)), _sds((M, Ko), out_dtype),
                    None if (nk == 1 or out_f32) else (tm, tn), name, deps)(a, b3, *deps)


def _mm_tn(a, b, G, tm, tn, tk, name, deps=()):
    T, M = a.shape
    Nb = b.shape[1] // G
    npb, nk = Nb // tn, T // tk
    body = _mm_body((((0,), (0,)), ((), ())), nk, False)

    def wrapped(a_ref, b_ref, o_ref, *scr):
        body(a_ref, b_ref, o_ref, *scr)

    in_specs = [pl.BlockSpec((tk, tm), lambda i, j, k: (k, i)), pl.BlockSpec((tk, tn), lambda i, j, k: (k, j))]
    out_spec = pl.BlockSpec((None, tm, tn), lambda i, j, k: (j // npb, i, j % npb))
    return _mm_call(wrapped, (M // tm, G * npb, nk), in_specs, out_spec, _sds((G, M, Nb), BF16),
                    None if nk == 1 else (tm, tn), name, deps)(a, b, *deps)


def _rsum(v):
    return jnp.sum(v, axis=0, keepdims=True)


def _rmean(v):
    return jnp.mean(v, axis=-1, keepdims=True)


def _gelu(x):
    t = jnp.tanh(_G0 * (x + _G1 * (x * x * x)))
    return x * (0.5 * (1.0 + t)), t


def _dgelu(x, t):
    return 0.5 * (1.0 + t) + 0.5 * x * (1.0 - t * t) * (_G0 * (1.0 + 3.0 * _G1 * (x * x)))


def _sigmoid(x):
    return 1.0 / (1.0 + jnp.exp(-x))


def _fill_shifted(ext, rot):
    v = ext[...]
    n = v.shape[0]
    for b in range(1, 8):
        rot[b - 1] = pltpu.roll(v, n - b, 0)


def _rows_at(ext, rot, s, tm, cs=slice(None)):
    a, b = divmod(s, 8)
    return ext[8 * a:8 * a + tm, cs] if b == 0 else rot[b - 1, 8 * a:8 * a + tm, cs]


def _causal_conv(w_ref, taps, bias, ext, rot, offset, tm, out):
    D = out.shape[1]
    for cb in range(D // LANE):
        cs = slice(cb * LANE, (cb + 1) * LANE)
        acc = None
        for k, o in zip(taps, offset):
            term = w_ref[k:k + 1, cs] * _rows_at(ext, rot, o, tm, cs)
            acc = term if acc is None else acc + term
        out[:, cs] = acc if bias is None else acc + bias[:, cs]


def _rows(*vs):
    a = jnp.stack([v.astype(F32) for v in vs])
    return jnp.pad(a, ((0, 8 - len(vs)), (0, 0)))


def _row_spec(tm, D):
    return pl.BlockSpec((tm, D), lambda i: (i, 0))


def _const_spec(shape):
    nd = len(shape)
    return pl.BlockSpec(shape, lambda i: (0,) * nd)


def _norm_fwd(xp, f, vec, name, deps=()):
    S, D = xp.shape
    tm = min(256, S)
    has_f = f is not None

    def body(*refs):
        if has_f:
            xp_ref, f_ref, vec_ref, xo_ref, h_ref = refs
            x = xp_ref[...] + vec_ref[0:1, :] * f_ref[...]
            xo_ref[...] = x
        else:
            xp_ref, vec_ref, h_ref = refs
            x = xp_ref[...]
        r = lax.rsqrt(_rmean(x * x) + EPS)
        h = (x * r) * vec_ref[1:2, :]
        h_ref[...] = (h * (1.0 + vec_ref[2:3, :]) + vec_ref[3:4, :]).astype(BF16)

    rs = _row_spec(tm, D)
    ins = [xp, f, vec] if has_f else [xp, vec]
    in_specs = ([rs, rs] if has_f else [rs]) + [_const_spec((8, D))]
    out_shape = ([_sds((S, D), F32)] if has_f else []) + [_sds((S, D), BF16)]
    out_specs = [rs] * len(out_shape)
    outs = _pcall(_after(body, len(ins), deps), grid=(S // tm,), in_specs=in_specs + [ANY] * len(deps),
                  out_specs=out_specs, out_shape=out_shape, name=name,
                  compiler_params=_params(("parallel",)))(*ins, *deps)
    return (outs[0], outs[1]) if has_f else (xp, outs[0])


def _mixer_fwd(z, wsh, sgu_ln, wtril, bias_full, cw, cvec, name, deps=()):
    S = z.shape[0]
    D = wsh.shape[1]
    tm = CHUNK

    def body(z_ref, wsh_ref, sln_ref, wt_ref, bias_ref, cw_ref, cv_ref, oa_ref, ob_ref, oc_ref, pe, ge, gr, cbuf):
        i = pl.program_id(0)

        @pl.when(i == 0)
        def _():
            pe[0:HALO, :] = jnp.zeros((HALO, D), F32)
            ge[0:HALO, :] = jnp.zeros((HALO, D), F32)

        def col(n):
            return z_ref[:, n * D:(n + 1) * D].astype(F32)

        pe[HALO:HALO + tm, :] = col(1) * col(2)
        q = wsh_ref[0:1, :] * pe[HALO - 2:HALO - 2 + tm, :]
        q = q + wsh_ref[1:2, :] * pe[HALO - 1:HALO - 1 + tm, :]
        q = q + wsh_ref[2:3, :] * pe[HALO:HALO + tm, :]
        oa_ref[...] = (col(0) * q).astype(BF16)
        gu, _ = _gelu(col(3))
        gv, _ = _gelu(col(4))
        d = gv - _rmean(gv)
        nrm = d * lax.rsqrt(_rmean(d * d) + EPS)
        vnb = (nrm * sln_ref[0:1, :] + sln_ref[1:2, :]).astype(BF16)
        for g in range(NG):
            cs = slice(g * LANE, (g + 1) * LANE)
            mixed = jnp.dot(wt_ref[g], vnb[:, cs], preferred_element_type=F32) + bias_ref[:, cs]
            ob_ref[:, cs] = (gu[:, cs] * mixed).astype(BF16)
        ge[HALO:HALO + tm, :] = col(5) * _sigmoid(col(6))
        _fill_shifted(ge, gr)
        o0 = HALO - (CFM_K - 1)
        _causal_conv(cw_ref, range(CFM_K), cv_ref[0:1, :], ge, gr, range(o0, o0 + CFM_K), tm, cbuf)
        conv = cbuf[...]
        d = conv - _rmean(conv)
        ln = (d * lax.rsqrt(_rmean(d * d) + EPS)) * cv_ref[1:2, :] + cv_ref[2:3, :]
        oc_ref[...] = (ln * _sigmoid(ln)).astype(BF16)
        pe[0:HALO, :] = pe[tm:tm + HALO, :]
        ge[0:HALO, :] = ge[tm:tm + HALO, :]

    rs = _row_spec(tm, D)
    return _pcall(
        _after(body, 7, deps), grid=(S // tm,),
        in_specs=[pl.BlockSpec((tm, 7 * D), lambda i: (i, 0)), _const_spec((8, D)), _const_spec((8, D)),
                  _const_spec((NG, CHUNK, CHUNK)), _const_spec((CHUNK, D)), _const_spec((HALO, D)), _const_spec((8, D))]
        + [ANY] * len(deps),
        out_specs=[rs, rs, rs], out_shape=[_sds((S, D), BF16)] * 3,
        scratch_shapes=[pltpu.VMEM((HALO + tm, D), F32), pltpu.VMEM((HALO + tm, D), F32),
                        pltpu.VMEM((7, HALO + tm, D), F32), pltpu.VMEM((tm, D), F32)],
        name=name, compiler_params=_params(("arbitrary",)))(z, wsh, sgu_ln, wtril, bias_full, cw, cvec, *deps)


def _branch_out(acts, ws, z, name):
    S, D = acts[0].shape
    tm = min(256, S)

    def body(a0, a1, a2, w0, w1, w2, g0, g1, g2, m_ref, y_ref):
        m = None
        for n, (a, w, g) in enumerate(((a0, w0, g0), (a1, w1, g1), (a2, w2, g2))):
            y = jnp.dot(a[...], w[...], preferred_element_type=F32)
            y_ref[n] = y.astype(BF16)
            t = _sigmoid(g[...].astype(F32)) * y
            m = t if m is None else m + t
        m_ref[...] = m.astype(BF16)

    rs = _row_spec(tm, D)
    gate_specs = [pl.BlockSpec((tm, D), functools.partial(lambda i, n: (i, 7 + n), n=n)) for n in range(3)]
    return _pcall(body, grid=(S // tm,),
                  in_specs=[rs, rs, rs] + [_const_spec((D, D))] * 3 + gate_specs,
                  out_specs=[rs, pl.BlockSpec((3, tm, D), lambda i: (0, i, 0))],
                  out_shape=[_sds((S, D), BF16), _sds((3, S, D), BF16)], name=name,
                  compiler_params=_params(("parallel",)))(*acts, *ws, z, z, z)


def _swiglu_fwd(gu, name):
    S, F2 = gu.shape
    F = F2 // 2
    tm = min(256, S)

    def body(g_ref, u_ref, o_ref):
        g = g_ref[...].astype(F32)
        o_ref[...] = ((g * _sigmoid(g)) * u_ref[...].astype(F32)).astype(BF16)

    return _pcall(body, grid=(S // tm,),
                  in_specs=[pl.BlockSpec((tm, F), lambda i: (i, 0)), pl.BlockSpec((tm, F), lambda i: (i, 1))],
                  out_specs=pl.BlockSpec((tm, F), lambda i: (i, 0)), out_shape=_sds((S, F), BF16), name=name,
                  compiler_params=_params(("parallel",)))(gu, gu)


def _swiglu_bwd(dact, gu, name):
    S, F2 = gu.shape
    F = F2 // 2
    tm = min(128, S)

    def body(d_ref, g_ref, u_ref, o_ref):
        g = g_ref[...].astype(F32)
        sg = _sigmoid(g)
        d = d_ref[...]
        o_ref[:, 0:F] = (d * u_ref[...].astype(F32) * (sg * (1.0 + g * (1.0 - sg)))).astype(BF16)
        o_ref[:, F:2 * F] = (d * (g * sg)).astype(BF16)

    return _pcall(body, grid=(S // tm,),
                  in_specs=[pl.BlockSpec((tm, F), lambda i: (i, 0)), pl.BlockSpec((tm, F), lambda i: (i, 0)),
                            pl.BlockSpec((tm, F), lambda i: (i, 1))],
                  out_specs=pl.BlockSpec((tm, F2), lambda i: (i, 0)), out_shape=_sds((S, F2), BF16), name=name,
                  compiler_params=_params(("parallel",)))(dact, gu, gu)


def _final_bwd(x1, f, tgt, vec, name):
    S, D = x1.shape
    tm = min(256, S)

    def body(x_ref, f_ref, t_ref, vec_ref, dx_ref, df_ref, sums_ref, loss_ref):
        @pl.when(pl.program_id(0) == 0)
        def _():
            sums_ref[...] = jnp.zeros_like(sums_ref)
            loss_ref[...] = jnp.zeros_like(loss_ref)

        gate, fg = vec_ref[0:1, :], vec_ref[1:2, :]
        fv = f_ref[...]
        x = x_ref[...] + gate * fv
        r = lax.rsqrt(_rmean(x * x) + EPS)
        xn = x * r
        diff = xn * fg - t_ref[...]
        per_tok = _rmean(diff * diff)
        loss_ref[...] += 0.5 * jnp.sum(per_tok, axis=0, keepdims=True)
        dy = diff * (1.0 / D)
        sums_ref[0:1, :] += _rsum(dy * xn)
        dxn = dy * fg
        dx = r * (dxn - xn * _rmean(dxn * xn))
        sums_ref[1:2, :] += _rsum(dx * fv)
        dx_ref[...] = dx
        df_ref[...] = (dx * gate).astype(BF16)

    rs = _row_spec(tm, D)
    return _pcall(body, grid=(S // tm,), in_specs=[rs, rs, rs, _const_spec((8, D))],
                  out_specs=[rs, rs, _const_spec((8, D)), _const_spec((8, LANE))],
                  out_shape=[_sds((S, D), F32), _sds((S, D), BF16), _sds((8, D), F32), _sds((8, LANE), F32)],
                  name=name, compiler_params=_params(("arbitrary",)))(x1, f, tgt, vec)


def _norm_bwd(xin, dh, dxup, vec, fprev, name, deps=()):
    S, D = xin.shape
    tm = min(256, S)
    has_prev = fprev is not None

    def body(*refs):
        if has_prev:
            x_ref, dh_ref, up_ref, vec_ref, fp_ref, dx_ref, dp_ref, sums_ref = refs
        else:
            x_ref, dh_ref, up_ref, vec_ref, dx_ref, sums_ref = refs

        @pl.when(pl.program_id(0) == 0)
        def _():
            sums_ref[...] = jnp.zeros_like(sums_ref)

        g, scale = vec_ref[0:1, :], vec_ref[1:2, :]
        x = x_ref[...]
        r = lax.rsqrt(_rmean(x * x) + EPS)
        xn = x * r
        dhv = dh_ref[...]
        sums_ref[0:1, :] += _rsum(dhv)
        sums_ref[1:2, :] += _rsum(dhv * (xn * g))
        dm = dhv * (1.0 + scale)
        sums_ref[2:3, :] += _rsum(dm * xn)
        dxn = dm * g
        dx = up_ref[...] + r * (dxn - xn * _rmean(dxn * xn))
        dx_ref[...] = dx
        if has_prev:
            sums_ref[3:4, :] += _rsum(dx * fp_ref[...])
            dp_ref[...] = (dx * vec_ref[2:3, :]).astype(BF16)

    rs = _row_spec(tm, D)
    ins = [xin, dh, dxup, vec] + ([fprev] if has_prev else [])
    in_specs = [rs, rs, rs, _const_spec((8, D))] + ([rs] if has_prev else [])
    out_shape = [_sds((S, D), F32)] + ([_sds((S, D), BF16)] if has_prev else []) + [_sds((8, D), F32)]
    out_specs = [rs] + ([rs] if has_prev else []) + [_const_spec((8, D))]
    outs = _pcall(_after(body, len(ins), deps), grid=(S // tm,), in_specs=in_specs + [ANY] * len(deps),
                  out_specs=out_specs, out_shape=out_shape, name=name,
                  compiler_params=_params(("arbitrary",)))(*ins, *deps)
    return (outs[0], outs[1], outs[2]) if has_prev else (outs[0], None, outs[1])


def _gate_bwd(dmerged, z, ys, name, deps=()):
    S, D = dmerged.shape
    tm = min(256, S)
    ncol = z.shape[1] // D

    def body(dm_ref, g_ref, y_ref, dy_ref, dz_ref):
        sg = _sigmoid(g_ref[...].astype(F32))
        dm = dm_ref[...]
        dy_ref[...] = (dm * sg).astype(BF16)
        dz_ref[...] = (dm * y_ref[...].astype(F32) * (sg * (1.0 - sg))).astype(BF16)

    return _pcall(_after(body, 3, deps), grid=(S // tm, 3),
                  in_specs=[pl.BlockSpec((tm, D), lambda i, n: (i, 0)), pl.BlockSpec((tm, D), lambda i, n: (i, 7 + n)),
                            pl.BlockSpec((None, tm, D), lambda i, n: (n, i, 0))] + [ANY] * len(deps),
                  out_specs=[pl.BlockSpec((None, tm, D), lambda i, n: (n, i, 0)),
                             pl.BlockSpec((tm, D), lambda i, n: (i, 7 + n))],
                  out_shape=[_sds((3, S, D), BF16), _sds((S, ncol * D), BF16)], name=name,
                  compiler_params=_params(("parallel", "arbitrary")))(dmerged, z, ys, *deps)


def _mixer_bwd(z, dacts, dz, wsh, sgu_ln, wtril, wtril_t, bias_full, cw, cvec, name):
    S = z.shape[0]
    D = wsh.shape[1]
    tm = CHUNK
    nt = S // tm
    hb = tm // HALO

    def body(zc, zp, da_ref, db_ref, dc_ref, wsh_ref, sln_ref, wt_ref, wtt_ref, bias_ref, cw_ref, cv_ref, _dz_in,
             dz_ref, vec_ref, dcw_ref, dws_ref, dbs_ref, pe, ge, dqe, dce, gr, dcr, cbuf):
        i = pl.program_id(0)
        rb = nt - 1 - i

        @pl.when(i == 0)
        def _():
            vec_ref[...] = jnp.zeros_like(vec_ref)
            dcw_ref[...] = jnp.zeros_like(dcw_ref)
            dws_ref[...] = jnp.zeros_like(dws_ref)
            dbs_ref[...] = jnp.zeros_like(dbs_ref)
            dqe[tm:tm + HALO, :] = jnp.zeros((HALO, D), F32)
            dce[tm:tm + HALO, :] = jnp.zeros((HALO, D), F32)

        keep = (rb > 0).astype(F32)

        def col(n):
            return zc[:, n * D:(n + 1) * D].astype(F32)

        def pcol(n):
            return zp[:, n * D:(n + 1) * D].astype(F32)

        c_a, x_a = col(1), col(2)
        pe[0:HALO, :] = keep * (pcol(1) * pcol(2))
        pe[HALO:HALO + tm, :] = c_a * x_a
        q = wsh_ref[0:1, :] * pe[HALO - 2:HALO - 2 + tm, :]
        q = q + wsh_ref[1:2, :] * pe[HALO - 1:HALO - 1 + tm, :]
        q = q + wsh_ref[2:3, :] * pe[HALO:HALO + tm, :]
        dact = da_ref[...]
        dz_ref[:, 0:D] = (dact * q).astype(BF16)
        dq = dact * col(0)
        dqe[0:tm, :] = dq
        dp = wsh_ref[2:3, :] * dq + wsh_ref[1:2, :] * dqe[1:1 + tm, :] + wsh_ref[0:1, :] * dqe[2:2 + tm, :]
        dz_ref[:, D:2 * D] = (dp * x_a).astype(BF16)
        dz_ref[:, 2 * D:3 * D] = (dp * c_a).astype(BF16)
        for k in range(SHORT_K):
            o = HALO - (SHORT_K - 1) + k
            vec_ref[k:k + 1, :] += _rsum(dq * pe[o:o + tm, :])
        u, v = col(3), col(4)
        gu, tu = _gelu(u)
        gv, tv = _gelu(v)
        d = gv - _rmean(gv)
        rstd = lax.rsqrt(_rmean(d * d) + EPS)
        nrm = d * rstd
        vnb = (nrm * sln_ref[0:1, :] + sln_ref[1:2, :]).astype(BF16)
        dact = db_ref[...]
        dvn_parts, dgu_parts = [], []
        for g in range(NG):
            cs = slice(g * LANE, (g + 1) * LANE)
            vg = vnb[:, cs]
            mixed = jnp.dot(wt_ref[g], vg, preferred_element_type=F32) + bias_ref[:, cs]
            dgu_parts.append(dact[:, cs] * mixed)
            dmixed = dact[:, cs] * gu[:, cs]
            dmb = dmixed.astype(BF16)
            dws_ref[g] += lax.dot_general(dmb, vg, (((1,), (1,)), ((), ())), preferred_element_type=F32)
            dbs_ref[g] += jnp.broadcast_to(jnp.sum(dmixed, axis=1, keepdims=True), (CHUNK, LANE))
            dvn_parts.append(jnp.dot(wtt_ref[g], dmb, preferred_element_type=F32))
        dgu = jnp.concatenate(dgu_parts, axis=1)
        dvn = jnp.concatenate(dvn_parts, axis=1)
        dz_ref[:, 3 * D:4 * D] = (dgu * _dgelu(u, tu)).astype(BF16)
        vec_ref[3:4, :] += _rsum(dvn * nrm)
        vec_ref[4:5, :] += _rsum(dvn)
        dn = dvn * sln_ref[0:1, :]
        dgv = rstd * (dn - _rmean(dn) - nrm * _rmean(dn * nrm))
        dz_ref[:, 4 * D:5 * D] = (dgv * _dgelu(v, tv)).astype(BF16)
        a_c = col(5)
        sg = _sigmoid(col(6))
        ge[0:HALO, :] = keep * (pcol(5) * _sigmoid(pcol(6)))
        ge[HALO:HALO + tm, :] = a_c * sg
        _fill_shifted(ge, gr)
        o0 = HALO - (CFM_K - 1)
        _causal_conv(cw_ref, range(CFM_K), cv_ref[0:1, :], ge, gr, range(o0, o0 + CFM_K), tm, cbuf)
        conv = cbuf[...]
        d = conv - _rmean(conv)
        rstd = lax.rsqrt(_rmean(d * d) + EPS)
        nrm = d * rstd
        ln = nrm * cv_ref[1:2, :] + cv_ref[2:3, :]
        sl = _sigmoid(ln)
        dln = dc_ref[...] * (sl * (1.0 + ln * (1.0 - sl)))
        vec_ref[6:7, :] += _rsum(dln * nrm)
        vec_ref[7:8, :] += _rsum(dln)
        dn = dln * cv_ref[1:2, :]
        dconv = rstd * (dn - _rmean(dn) - nrm * _rmean(dn * nrm))
        vec_ref[5:6, :] += _rsum(dconv)
        dce[0:tm, :] = dconv
        _fill_shifted(dce, dcr)
        _causal_conv(cw_ref, range(CFM_K), None, dce, dcr, [CFM_K - 1 - k for k in range(CFM_K)], tm, cbuf)
        dglu = cbuf[...]
        for cb in range(D // LANE):
            cs = slice(cb * LANE, (cb + 1) * LANE)
            dcv = dce[0:tm, cs]
            for k in range(CFM_K):
                dcw_ref[k:k + 1, cs] += _rsum(dcv * _rows_at(ge, gr, o0 + k, tm, cs))
        dz_ref[:, 5 * D:6 * D] = (dglu * sg).astype(BF16)
        dz_ref[:, 6 * D:7 * D] = (dglu * a_c * (sg * (1.0 - sg))).astype(BF16)
        dqe[tm:tm + HALO, :] = dqe[0:HALO, :]
        dce[tm:tm + HALO, :] = dce[0:HALO, :]

    rev = lambda i: (nt - 1 - i, 0)
    rs = pl.BlockSpec((tm, D), rev)
    cur = pl.BlockSpec((tm, 7 * D), rev)
    prev = pl.BlockSpec((HALO, 7 * D), lambda i: (jnp.maximum((nt - 1 - i) * hb - 1, 0), 0))
    ext = pltpu.VMEM((HALO + tm, D), F32)
    outs = _pcall(
        body, grid=(nt,),
        in_specs=[cur, prev, rs, rs, rs, _const_spec((8, D)), _const_spec((8, D)), _const_spec((NG, CHUNK, CHUNK)),
                  _const_spec((NG, CHUNK, CHUNK)), _const_spec((CHUNK, D)), _const_spec((HALO, D)), _const_spec((8, D)),
                  ANY],
        out_specs=[cur, _const_spec((8, D)), _const_spec((HALO, D)), _const_spec((NG, CHUNK, CHUNK)),
                   _const_spec((NG, CHUNK, LANE))],
        out_shape=[_sds(dz.shape, BF16), _sds((8, D), F32), _sds((HALO, D), F32), _sds((NG, CHUNK, CHUNK), F32),
                   _sds((NG, CHUNK, LANE), F32)],
        scratch_shapes=[ext, ext, ext, ext, pltpu.VMEM((7, HALO + tm, D), F32), pltpu.VMEM((7, HALO + tm, D), F32),
                        pltpu.VMEM((tm, D), F32)],
        input_output_aliases={12: 0}, name=name,
        compiler_params=_params(("arbitrary",)))(z, z, *dacts, wsh, sgu_ln, wtril, wtril_t, bias_full, cw, cvec, dz)
    return outs


def _ada_fwd(c_all, w_ada_loc, name):
    nb, D = c_all.shape
    L, _, nc = w_ada_loc.shape

    def body(c_ref, w_ref, o_ref, ca_ref):
        cv = c_ref[...]
        ca = cv * _sigmoid(cv)
        ca_ref[...] = ca
        o_ref[...] = jnp.dot(ca.astype(BF16), w_ref[...].astype(BF16), preferred_element_type=F32)

    return _pcall(body, grid=(L,),
                  in_specs=[_const_spec((nb, D)), pl.BlockSpec((None, D, nc), lambda l: (l, 0, 0))],
                  out_specs=[pl.BlockSpec((None, nb, nc), lambda l: (l, 0, 0)), _const_spec((nb, D))],
                  out_shape=[_sds((L, nb, nc), F32), _sds((nb, D), F32)], name=name,
                  compiler_params=_params(("arbitrary",)))(c_all, w_ada_loc)


def _adamw(w, g, m, v):
    m = ADAM_B1 * m + (1.0 - ADAM_B1) * g
    v = ADAM_B2 * v + (1.0 - ADAM_B2) * (g * g)
    m_hat = m / (1.0 - ADAM_B1 ** ADAM_STEP)
    v_hat = v / (1.0 - ADAM_B2 ** ADAM_STEP)
    delta = -ADAM_LR * (m_hat / (jnp.sqrt(v_hat) + ADAM_EPS) + ADAM_WD * w)
    return delta, m, v


def _tile_rows(R, C, align=8):
    cap = max(align, (1536 * 1024) // (4 * C))
    best = None
    for t in range(align, R + 1, align):
        if R % t == 0 and t <= cap:
            best = t
    return R if best is None else best


def _adam_ada(ct, dm, w, m, v, name):
    L, D, nc = w.shape
    nb = ct.shape[1]
    tr = _tile_rows(D, nc)

    def body(ct_ref, dm_ref, w_ref, m_ref, v_ref, g_ref, d_ref, mo_ref, vo_ref):
        g = ct_ref[:, 0:1] * dm_ref[0:1, :]
        for b in range(1, nb):
            g = g + ct_ref[:, b:b + 1] * dm_ref[b:b + 1, :]
        g_ref[...] = g
        d_ref[...], mo_ref[...], vo_ref[...] = _adamw(w_ref[...], g, m_ref[...], v_ref[...])

    ws = pl.BlockSpec((None, tr, nc), lambda l, r: (l, r, 0))
    return _pcall(body, grid=(L, D // tr),
                  in_specs=[pl.BlockSpec((tr, nb), lambda l, r: (r, 0)), pl.BlockSpec((None, nb, nc), lambda l, r: (l, 0, 0)),
                            ws, ws, ws],
                  out_specs=[ws] * 4, out_shape=[_sds(w.shape, F32)] * 4, name=name,
                  compiler_params=_params(("parallel", "parallel")))(ct, dm, w, m, v)


def _adam_small(parts, w, m, v, name):
    n, R, C = parts.shape
    tr = _tile_rows(R, C * n // 2)

    def body(p_ref, w_ref, m_ref, v_ref, g_ref, d_ref, mo_ref, vo_ref):
        g = p_ref[0]
        for j in range(1, n):
            g = g + p_ref[j]
        g_ref[...] = g
        d_ref[...], mo_ref[...], vo_ref[...] = _adamw(w_ref[...], g, m_ref[...], v_ref[...])

    ws = pl.BlockSpec((tr, C), lambda r: (r, 0))
    return _pcall(body, grid=(R // tr,), in_specs=[pl.BlockSpec((n, tr, C), lambda r: (0, r, 0)), ws, ws, ws],
                  out_specs=[ws] * 4, out_shape=[_sds((R, C), F32)] * 4, name=name,
                  compiler_params=_params(("parallel",)))(parts, w, m, v)


def _adam_plain(g, w, m, v, name):
    R, C = w.shape

    def body(g_ref, w_ref, m_ref, v_ref, d_ref, mo_ref, vo_ref):
        d_ref[...], mo_ref[...], vo_ref[...] = _adamw(w_ref[...], g_ref[...], m_ref[...], v_ref[...])

    ws = _const_spec((R, C))
    return _pcall(body, grid=(1,), in_specs=[ws] * 4, out_specs=[ws] * 3, out_shape=[_sds((R, C), F32)] * 3, name=name,
                  compiler_params=_params(("arbitrary",)))(g, w, m, v)


def _pair_sum(G, R1, my_c, name):
    n, R, C = G.shape
    half = n // 2
    tr = _tile_rows(R, C, align=16)

    def body(c_ref, g_ref, r_ref, o_ref):
        o_ref[...] = (g_ref[...].astype(F32) + r_ref[...].astype(F32)).astype(o_ref.dtype)

    blk = (None, tr, C)
    gs = pltpu.PrefetchScalarGridSpec(
        num_scalar_prefetch=1, grid=(half, R // tr),
        in_specs=[pl.BlockSpec(blk, lambda p, r, c: (2 * p + c[0], r, 0)), pl.BlockSpec(blk, lambda p, r, c: (p, r, 0))],
        out_specs=pl.BlockSpec(blk, lambda p, r, c: (p, r, 0)))
    return _pcall(body, grid_spec=gs, out_shape=_sds((half, R, C), G.dtype), name=name,
                  compiler_params=_params(("parallel", "parallel")))(my_c, G, R1)


def _adam_big(P, R2, my_chip, w, m, v, layer, prev, name):
    _, R, C = P.shape
    nrecv = R2.shape[0]
    tr = _tile_rows(R, C, align=16)

    def body(p_sm, p_ref, r_ref, w_ref, m_ref, v_ref, *rest):
        g_ref, d_ref, mo_ref, vo_ref = rest[-4:]
        g = p_ref[...].astype(F32)
        for k in range(nrecv):
            g = g + r_ref[k].astype(F32)
        g_ref[...] = g
        d_ref[...], mo_ref[...], vo_ref[...] = _adamw(w_ref[...], g, m_ref[...], v_ref[...])

    ws = pl.BlockSpec((None, tr, C), lambda r, p: (layer, r, 0))
    held = [] if prev is None else list(prev)
    gs = pltpu.PrefetchScalarGridSpec(
        num_scalar_prefetch=1, grid=(R // tr,),
        in_specs=[pl.BlockSpec((None, tr, C), lambda r, p: (p[0], r, 0)),
                  pl.BlockSpec((nrecv, tr, C), lambda r, p: (0, r, 0)), ws, ws, ws] + [ANY] * len(held),
        out_specs=[ws] * 4)
    alias = {6 + i: i for i in range(len(held))}
    return _pcall(body, grid_spec=gs, out_shape=[_sds(w.shape, F32)] * 4, name=name, input_output_aliases=alias,
                  compiler_params=_params(("parallel",)))(my_chip, P, R2, w, m, v, *held)


def _place():
    return lax.axis_index("x"), lax.axis_index("y"), lax.axis_index("c")


def _sum_over_devices(scalar):
    return lax.psum(scalar, ("x", "y", "c"))


def _all_gather(shards, name, deps=()):
    n = len(shards)

    def body(*refs):
        ins, outs = refs[:n], refs[n:2 * n]
        send_sems, recv_sems, local_sems = refs[2 * n:]
        x, y, c = _place()
        me, sibling = (x, y, c), (x, y, 1 - c)
        chips = [(1 - x, y), (x, 1 - y), (1 - x, 1 - y)]

        def slot(a, px, py, pc):
            return outs[a].at[4 * px + 2 * py + pc]

        def copy(a, k, block, to, src=None):
            return pltpu.make_async_remote_copy(
                src_ref=slot(a, *block) if src is None else src, dst_ref=slot(a, *block),
                send_sem=send_sems.at[7 * a + k], recv_sem=recv_sems.at[7 * a + k], device_id=to, device_id_type=MESH)

        mine = [pltpu.make_async_copy(ins[a], slot(a, *me), local_sems.at[a]) for a in range(n)]
        for cp in mine:
            cp.start()
        first = []
        for a in range(n):
            first.append(copy(a, 0, me, sibling, src=ins[a]))
            first += [copy(a, 1 + j, me, (*chip, c), src=ins[a]) for j, chip in enumerate(chips)]
        for cp in first:
            cp.start()
        passed = []
        for j, chip in enumerate(chips):
            for a in range(n):
                copy(a, 1 + j, (*chip, c), me).wait_recv()
                fwd = copy(a, 4 + j, (*chip, c), sibling)
                fwd.start()
                passed.append(fwd)
        for a in range(n):
            copy(a, 0, sibling, me).wait_recv()
        for j, chip in enumerate(chips):
            for a in range(n):
                copy(a, 4 + j, (*chip, 1 - c), me).wait_recv()
        for cp in first + passed:
            cp.wait_send()
        for cp in mine:
            cp.wait()

    outs = _pcall(_after(body, n, deps), in_specs=[ANY] * (n + len(deps)), out_specs=[ANY] * n,
                  out_shape=[_sds((NDEV,) + s.shape, s.dtype) for s in shards],
                  scratch_shapes=[pltpu.SemaphoreType.DMA((7 * n,)), pltpu.SemaphoreType.DMA((7 * n,)),
                                  pltpu.SemaphoreType.DMA((n,))], name=name)(*shards, *deps)
    return list(outs)


HBM = pl.BlockSpec(memory_space=pltpu.HBM)
SEM = pl.BlockSpec(memory_space=pltpu.SEMAPHORE)


def _copies(plan, refs, send_sems, recv_sems):
    return [pltpu.make_async_remote_copy(src_ref=s, dst_ref=d, send_sem=send_sems.at[k], recv_sem=recv_sems.at[k],
                                         device_id=dev, device_id_type=MESH)
            for k, (s, d, dev) in enumerate(plan(refs, *_place()))]


def _xfer_start(bufs, ncopies, plan, name):
    n = len(bufs)

    def body(*refs):
        for cp in _copies(plan, refs[:n], refs[n], refs[n + 1]):
            cp.start()
        token = refs[2 * n + 2]
        token[...] = jnp.zeros_like(token)

    outs = _pcall(
        body, name=name,
        out_shape=(pltpu.SemaphoreType.DMA((ncopies,)), pltpu.SemaphoreType.DMA((ncopies,)),
                   *[pltpu.HBM(b.shape, b.dtype) for b in bufs], _sds((8, LANE), F32)),
        in_specs=[HBM] * n, out_specs=(SEM, SEM, *[HBM] * n, pl.BlockSpec(memory_space=pltpu.VMEM)),
        input_output_aliases={i: 2 + i for i in range(n)},
        compiler_params=pltpu.CompilerParams(has_side_effects=pltpu.SideEffectType.DATAFLOW_SIDE_EFFECTING),
    )(*[pltpu.with_memory_space_constraint(b, pltpu.HBM) for b in bufs])
    return (outs[0], outs[1]), list(outs[2:2 + n]), outs[2 + n]


def _xfer_wait(sems, bufs, plan, after, name):
    n = len(bufs)

    def body(*refs):
        for cp in _copies(plan, refs[:n], refs[n], refs[n + 1]):
            cp.wait_send()
            cp.wait_recv()

    outs = _pcall(
        body, name=name, out_shape=tuple(pltpu.HBM(b.shape, b.dtype) for b in bufs),
        in_specs=[HBM] * n + [SEM, SEM, ANY], out_specs=tuple([HBM] * n), input_output_aliases={i: i for i in range(n)},
        compiler_params=pltpu.CompilerParams(has_side_effects=pltpu.SideEffectType.DATAFLOW_SIDE_EFFECTING),
    )(*bufs, *sems, after)
    return list(outs)


def _chips_of(x, y):
    return [(1 - x, y), (x, 1 - y), (1 - x, 1 - y)]


def _gather_plan1(n):
    def plan(refs, x, y, c):
        out = []
        for a in range(n):
            blk = refs[a].at[4 * x + 2 * y + c]
            out.append((blk, blk, (x, y, 1 - c)))
            out += [(blk, blk, (px, py, c)) for px, py in _chips_of(x, y)]
        return out
    return plan


def _gather_plan2(n):
    def plan(refs, x, y, c):
        out = []
        for a in range(n):
            for px, py in _chips_of(x, y):
                blk = refs[a].at[4 * px + 2 * py + c]
                out.append((blk, blk, (x, y, 1 - c)))
        return out
    return plan


def _gather_start(shards, dev, name):
    lands = [lax.dynamic_update_slice(lax.empty((NDEV,) + s.shape, s.dtype), s[None], (dev,) + (0,) * s.ndim)
             for s in shards]
    n = len(shards)
    sems, lands, tok = _xfer_start(lands, 4 * n, _gather_plan1(n), name + "_p1_start")
    return dict(sems=sems, lands=lands, tok=tok, n=n)


def _gather_mid(st, after, name):
    n = st["n"]
    lands = _xfer_wait(st["sems"], st["lands"], _gather_plan1(n), after, name + "_p1_wait")
    sems, lands, tok = _xfer_start(lands, 3 * n, _gather_plan2(n), name + "_p2_start")
    return dict(sems=sems, lands=lands, tok=tok, n=n)


def _gather_finish(st, after, name):
    return _xfer_wait(st["sems"], st["lands"], _gather_plan2(st["n"]), after, name + "_p2_wait")


def _scatter_plan1(n):
    def plan(refs, x, y, c):
        return [(refs[a].at[2 * p + 1 - c], refs[n + a].at[p], (x, y, 1 - c)) for a in range(n) for p in range(NCHIP)]
    return plan


def _scatter_plan2(n):
    def plan(refs, x, y, c):
        return [(refs[a].at[2 * px + py], refs[n + a].at[j], (px, py, c))
                for a in range(n) for j, (px, py) in enumerate(_chips_of(x, y))]
    return plan


def _scatter_start(Gs, name):
    n = len(Gs)
    R1s = [lax.empty((NCHIP,) + g.shape[1:], g.dtype) for g in Gs]
    sems, bufs, tok = _xfer_start(list(Gs) + R1s, NCHIP * n, _scatter_plan1(n), name + "_s1_start")
    return dict(sems=sems, bufs=bufs, tok=tok, n=n)


def _scatter_mid(st, after, my_c, name):
    n = st["n"]
    bufs = _xfer_wait(st["sems"], st["bufs"], _scatter_plan1(n), after, name + "_s1_wait")
    Ps = [_pair_sum(bufs[a], bufs[n + a], my_c, f"{name}_pair_sum{a}") for a in range(n)]
    R2s = [lax.empty((3,) + p.shape[1:], p.dtype) for p in Ps]
    sems, bufs, tok = _xfer_start(Ps + R2s, 3 * n, _scatter_plan2(n), name + "_s2_start")
    return dict(sems=sems, bufs=bufs, tok=tok, n=n)


def _scatter_finish(st, after, name):
    n = st["n"]
    bufs = _xfer_wait(st["sems"], st["bufs"], _scatter_plan2(n), after, name + "_s2_wait")
    return bufs[:n], bufs[n:]


SMALL_ROWS = {"norm1_g": (0, 1), "norm2_g": (1, 1), "sgu_ln_g": (2, 1), "sgu_ln_b": (3, 1), "cfm_conv_b": (4, 1),
              "cfm_ln_g": (5, 1), "cfm_ln_b": (6, 1), "b_sgu": (7, 1), "w_sgu": (8, 128), "b_ada": (136, N_MOD),
              "w_short": (142, SHORT_K), "cfm_conv_w": (145, CFM_K)}
ROWS_PER_LAYER = 176
FINAL_ROW = DEPTH * ROWS_PER_LAYER
PACK_ROWS = 360


def _pack(get, D):
    parts = []
    for l in range(DEPTH):
        for name, (_, nrows) in SMALL_ROWS.items():
            a = get(name, l)
            parts.append(jnp.zeros((nrows * D,), F32) if a is None else a.astype(F32).reshape(nrows * D))
    fin = get("final_g", None)
    parts.append(fin.astype(F32).reshape(D))
    parts.append(jnp.zeros(((PACK_ROWS - FINAL_ROW - 1) * D,), F32))
    return jnp.concatenate(parts).reshape(PACK_ROWS, D)


def _unpack(pack, name, shape):
    D = pack.shape[1]
    r0, nrows = SMALL_ROWS[name]
    return jnp.stack([pack[l * ROWS_PER_LAYER + r0:l * ROWS_PER_LAYER + r0 + nrows] for l in range(DEPTH)]).reshape(shape)


def _mm_tiles(S):
    return min(512, S), min(1024, S)


def kernel(x, c, w_ada, b_ada, norm1_g, w_in, w_short, w_a_out, sgu_ln_g, sgu_ln_b, w_sgu, b_sgu, w_b_out, cfm_conv_w, cfm_conv_b, cfm_ln_g, cfm_ln_b, w_c_out, w_o, norm2_g, w_ffn_in, w_ffn_out, final_g, loss_target, m_w_ada, m_b_ada, m_norm1_g, m_w_in, m_w_short, m_w_a_out, m_sgu_ln_g, m_sgu_ln_b, m_w_sgu, m_b_sgu, m_w_b_out, m_cfm_conv_w, m_cfm_conv_b, m_cfm_ln_g, m_cfm_ln_b, m_w_c_out, m_w_o, m_norm2_g, m_w_ffn_in, m_w_ffn_out, m_final_g, v_w_ada, v_b_ada, v_norm1_g, v_w_in, v_w_short, v_w_a_out, v_sgu_ln_g, v_sgu_ln_b, v_w_sgu, v_b_sgu, v_w_b_out, v_cfm_conv_w, v_cfm_conv_b, v_cfm_ln_g, v_cfm_ln_b, v_w_c_out, v_w_o, v_norm2_g, v_w_ffn_in, v_w_ffn_out, v_final_g):
    W = dict(w_ada=w_ada, b_ada=b_ada, norm1_g=norm1_g, w_in=w_in, w_short=w_short, w_a_out=w_a_out, sgu_ln_g=sgu_ln_g,
             sgu_ln_b=sgu_ln_b, w_sgu=w_sgu, b_sgu=b_sgu, w_b_out=w_b_out, cfm_conv_w=cfm_conv_w, cfm_conv_b=cfm_conv_b,
             cfm_ln_g=cfm_ln_g, cfm_ln_b=cfm_ln_b, w_c_out=w_c_out, w_o=w_o, norm2_g=norm2_g, w_ffn_in=w_ffn_in,
             w_ffn_out=w_ffn_out, final_g=final_g)
    Mo = dict(w_ada=m_w_ada, b_ada=m_b_ada, norm1_g=m_norm1_g, w_in=m_w_in, w_short=m_w_short, w_a_out=m_w_a_out,
              sgu_ln_g=m_sgu_ln_g, sgu_ln_b=m_sgu_ln_b, w_sgu=m_w_sgu, b_sgu=m_b_sgu, w_b_out=m_w_b_out,
              cfm_conv_w=m_cfm_conv_w, cfm_conv_b=m_cfm_conv_b, cfm_ln_g=m_cfm_ln_g, cfm_ln_b=m_cfm_ln_b,
              w_c_out=m_w_c_out, w_o=m_w_o, norm2_g=m_norm2_g, w_ffn_in=m_w_ffn_in, w_ffn_out=m_w_ffn_out,
              final_g=m_final_g)
    Vo = dict(w_ada=v_w_ada, b_ada=v_b_ada, norm1_g=v_norm1_g, w_in=v_w_in, w_short=v_w_short, w_a_out=v_w_a_out,
              sgu_ln_g=v_sgu_ln_g, sgu_ln_b=v_sgu_ln_b, w_sgu=v_w_sgu, b_sgu=v_b_sgu, w_b_out=v_w_b_out,
              cfm_conv_w=v_cfm_conv_w, cfm_conv_b=v_cfm_conv_b, cfm_ln_g=v_cfm_ln_g, cfm_ln_b=v_cfm_ln_b,
              w_c_out=v_w_c_out, w_o=v_w_o, norm2_g=v_norm2_g, w_ffn_in=v_w_ffn_in, w_ffn_out=v_w_ffn_out,
              final_g=v_final_g)
    order = ["w_ada", "b_ada", "norm1_g", "w_in", "w_short", "w_a_out", "sgu_ln_g", "sgu_ln_b", "w_sgu", "b_sgu",
             "w_b_out", "cfm_conv_w", "cfm_conv_b", "cfm_ln_g", "cfm_ln_b", "w_c_out", "w_o", "norm2_g", "w_ffn_in",
             "w_ffn_out", "final_g"]

    assert DEPTH == 2, "the weight-gather schedule below is written for two layers"
    S, D = x.shape[1], x.shape[2]
    F2 = w_ffn_in.shape[2] * NDEV
    FF = F2 // 2
    xi, yi, ci = _place()
    dev = 4 * xi + 2 * yi + ci
    my_c = jnp.reshape(ci, (1,)).astype(jnp.int32)
    my_chip = jnp.reshape(2 * xi + yi, (1,)).astype(jnp.int32)
    tm, tm_big = _mm_tiles(S)
    x0 = x.reshape(S, D)
    tgt = loss_target.reshape(S, D)

    def shards_of(l):
        return [w_in[l].astype(BF16), w_a_out[l].astype(BF16), w_b_out[l].astype(BF16), w_c_out[l].astype(BF16),
                w_o[l].astype(BF16), w_ffn_in[l].astype(BF16), w_ffn_out[l].astype(BF16)]

    ag_in0 = _gather_start(shards_of(0)[:1], dev, "ag_w_in0")
    ag_rest0 = _gather_start(shards_of(0)[1:], dev, "ag_rest0")

    c_all = _all_gather([jnp.pad(c, ((0, 7), (0, 0)))], "ag_c", deps=(ag_in0["tok"], ag_rest0["tok"]))[0][:, 0, :]
    modpart, c_act = _ada_fwd(c_all, w_ada, "ada_fwd")
    ncol = modpart.shape[2]
    mg = _all_gather([modpart.reshape(DEPTH * NDEV, ncol)], "ag_mod")[0].reshape(NDEV, DEPTH, NDEV, ncol)
    mine = lax.dynamic_index_in_dim(mg, dev, axis=2, keepdims=False)
    mod = (jnp.transpose(mine, (1, 0, 2)).reshape(DEPTH, N_MOD * D) + b_ada).reshape(DEPTH, N_MOD, D)

    tril = jnp.tril(jnp.ones((CHUNK, CHUNK), dtype=bool))

    def layer_consts(l):
        wt = jnp.where(tril[None], w_sgu[l], 0.0).astype(BF16)
        return dict(
            wsh=jnp.pad(w_short_full[l], ((0, 8 - SHORT_K), (0, 0))),
            sgu_ln=_rows(sgu_ln_g[l], sgu_ln_b[l]),
            wtril=wt, wtril_t=jnp.swapaxes(wt, 1, 2),
            bias_full=jnp.repeat(b_sgu[l].T, LANE, axis=1),
            cw=jnp.pad(cfm_w_full[l], ((0, HALO - CFM_K), (0, 0))),
            cvec=_rows(cfm_conv_b[l], cfm_ln_g[l], cfm_ln_b[l]))

    ncs = w_short.shape[2]
    sw = _all_gather([w_short.reshape(DEPTH * SHORT_K, ncs), cfm_conv_w.reshape(DEPTH * CFM_K, ncs)], "ag_convw",
                     deps=(mod,))
    w_short_full = jnp.transpose(sw[0], (1, 0, 2)).reshape(DEPTH, SHORT_K, D)
    cfm_w_full = jnp.transpose(sw[1], (1, 0, 2)).reshape(DEPTH, CFM_K, D)

    def rest_of(g):
        return dict(w_a=g[0].reshape(1, D, D), w_b=g[1].reshape(1, D, D), w_c=g[2].reshape(1, D, D),
                    w_o=g[3].reshape(1, D, D), w_fi=jnp.transpose(g[4], (1, 0, 2)).reshape(1, D, F2),
                    w_fo=g[5].reshape(1, FF, D))

    ag_in0 = _gather_mid(ag_in0, cfm_w_full, "ag_w_in0")
    Wg = [None, None]
    ag_l1 = None
    nin = w_in.shape[2]
    tn_in = nin if nin % 256 == 0 and nin <= 1280 else 256
    tn_fi = 512 if F2 % 512 == 0 else 256

    saved = []
    xcur, fprev, gprev = x0, None, None
    for l in range(DEPTH):
        sh1, sc1, g1, sh2, sc2, g2 = [mod[l, k] for k in range(N_MOD)]
        cl = layer_consts(l)
        vec1 = _rows(jnp.zeros((D,), F32) if gprev is None else gprev, norm1_g[l], sc1, sh1)
        if l == 0:
            xl, h = _norm_fwd(xcur, fprev, vec1, f"norm1_fwd{l}", deps=(ag_in0["tok"],))
            Wg[0] = dict(w_in=_gather_finish(ag_in0, h, "ag_w_in0")[0])
        else:
            ag_l1 = _gather_mid(ag_l1, fprev, f"ag_w{l}")
            xl, h = _norm_fwd(xcur, fprev, vec1, f"norm1_fwd{l}", deps=(ag_l1["tok"],))
            g = _gather_finish(ag_l1, h, f"ag_w{l}")
            Wg[l] = dict(w_in=g[0], **rest_of(g[1:]))
        wl = Wg[l]
        z = _mm_nn(h, wl["w_in"], BF16, tm_big, tn_in, D, f"mm_in{l}", w_outer=True)
        mix_deps = ()
        if l == 0:
            ag_rest0 = _gather_mid(ag_rest0, z, "ag_rest0")
            mix_deps = (ag_rest0["tok"],)
            if DEPTH > 1:
                ag_l1 = _gather_start(shards_of(1), dev, "ag_w1")
                mix_deps += (ag_l1["tok"],)
        acts = _mixer_fwd(z, cl["wsh"], cl["sgu_ln"], cl["wtril"], cl["bias_full"], cl["cw"], cl["cvec"], f"mixer_fwd{l}",
                          deps=mix_deps)
        if l == 0:
            wl.update(rest_of(_gather_finish(ag_rest0, acts[0], "ag_rest0")))
        merged, ys = _branch_out(acts, [wl["w_a"][0], wl["w_b"][0], wl["w_c"][0]], z, f"branch_out{l}")
        o = _mm_nn(merged, wl["w_o"], F32, tm, D, D, f"mm_o{l}")
        x1, h2 = _norm_fwd(xl, o, _rows(g1, norm2_g[l], sc2, sh2), f"norm2_fwd{l}")
        gu = _mm_nn(h2, wl["w_fi"], BF16, tm_big, tn_fi, D, f"mm_ffn_in{l}")
        act = _swiglu_fwd(gu, f"swiglu_fwd{l}")
        f = _mm_nn(act, wl["w_fo"], F32, tm, D, FF, f"mm_ffn_out{l}")
        saved.append(dict(xl=xl, h=h, z=z, acts=acts, ys=ys, merged=merged, o=o, x1=x1, h2=h2, gu=gu, act=act, f=f,
                          consts=cl, mod=(sh1, sc1, g1, sh2, sc2, g2)))
        xcur, fprev, gprev = x1, f, g2

    last = saved[-1]
    dxup, dfb, fsums, loss_blk = _final_bwd(last["x1"], last["f"], tgt, _rows(last["mod"][5], final_g), "final_bwd")
    loss = _sum_over_devices(loss_blk[0, 0])
    dgate2_next = fsums[1]
    small = [dict() for _ in range(DEPTH)]
    dmods = [None] * DEPTH
    nfi = w_ffn_in.shape[2]
    early_names, late_names = ["w_ffn_out", "w_ffn_in", "w_o"], ["w_a_out", "w_b_out", "w_c_out", "w_in"]
    results = {n: None for n in early_names + late_names}

    def adam_group(names, Ps, R2s, l):
        for n, p, r2 in zip(names, Ps, R2s):
            results[n] = _adam_big(p, r2, my_chip, W[n], Mo[n], Vo[n], l, results[n], f"adam_{n}{l}")

    late_prev = None
    for l in reversed(range(DEPTH)):
        sv, wl, cl = saved[l], Wg[l], saved[l]["consts"]
        sh1, sc1, g1, sh2, sc2, g2 = sv["mod"]
        dact = _mm_nt(dfb, wl["w_fo"], F32, tm, FF, D, f"mm_dact{l}", deps=() if late_prev is None else (late_prev["tok"],))
        g_fo = _mm_tn(sv["act"], dfb, 1, FF // 2, D, tm_big, f"mm_dw_ffn_out{l}")
        dgu = _swiglu_bwd(dact, sv["gu"], f"swiglu_bwd{l}")
        dh2 = _mm_nt(dgu, wl["w_fi"], F32, tm_big, D, tn_fi, f"mm_dh2{l}")
        if late_prev is not None:
            adam_group(late_names, *_scatter_finish(late_prev, dh2, f"rs_late{l + 1}"), l + 1)
            late_prev = None
        g_fi = _mm_tn(sv["h2"], dgu, 1, D, tn_fi, tm_big, f"mm_dw_ffn_in{l}")
        dx1, dob, s2 = _norm_bwd(sv["x1"], dh2, dxup, _rows(norm2_g[l], sc2, g1), sv["o"], f"norm2_bwd{l}")
        dmerged = _mm_nt(dob, wl["w_o"], F32, tm, D, D, f"mm_dmerged{l}")
        g_o = _mm_tn(sv["merged"], dob, 1, D, D, tm_big, f"mm_dw_o{l}")
        early = _scatter_start([g_fo.reshape(NDEV, FF // NDEV, D),
                                jnp.transpose(g_fi.reshape(D, NDEV, nfi), (1, 0, 2)),
                                g_o.reshape(NDEV, D // NDEV, D)], f"rs_early{l}")
        dys, dz = _gate_bwd(dmerged, sv["z"], sv["ys"], f"gate_bwd{l}", deps=(early["tok"],))
        early = _scatter_mid(early, dys, my_c, f"rs_early{l}")
        dacts, g_abc = [], []
        for n, key in enumerate(("w_a", "w_b", "w_c")):
            dacts.append(_mm_nt(dys[n], wl[key], F32, tm, D, D, f"mm_dact_{key}{l}", deps=(early["tok"],) if n == 0 else ()))
            g_abc.append(_mm_tn(sv["acts"][n], dys[n], 1, D, D, tm_big, f"mm_d{key}{l}"))
        dz, mvec, dcw, dws, dbs = _mixer_bwd(sv["z"], dacts, dz, cl["wsh"], cl["sgu_ln"], cl["wtril"], cl["wtril_t"],
                                             cl["bias_full"], cl["cw"], cl["cvec"], f"mixer_bwd{l}")
        dh = _mm_nt(dz, wl["w_in"], F32, tm_big, D, tn_in, f"mm_dh{l}")
        g_in = _mm_tn(sv["h"], dz, NDEV, D, tn_in, tm_big, f"mm_dw_in{l}")
        late = _scatter_start([g.reshape(NDEV, D // NDEV, D) for g in g_abc] + [g_in], f"rs_late{l}")
        if l > 0:
            pv = saved[l - 1]
            dxup, dfb, s1 = _norm_bwd(sv["xl"], dh, dx1, _rows(norm1_g[l], sc1, pv["mod"][5]), pv["f"], f"norm1_bwd{l}",
                                      deps=(late["tok"],))
        else:
            dxup, dfb, s1 = _norm_bwd(sv["xl"], dh, dx1, _rows(norm1_g[l], sc1), None, f"norm1_bwd{l}", deps=(late["tok"],))
        adam_group(early_names, *_scatter_finish(early, dxup, f"rs_early{l}"), l)
        late_prev = _scatter_mid(late, results["w_o"][0], my_c, f"rs_late{l}")
        dmods[l] = jnp.stack([s1[0], s1[1], s2[3], s2[0], s2[1], dgate2_next])
        dgate2_next = s1[3]
        small[l] = dict(norm1_g=s1[2], norm2_g=s2[2], sgu_ln_g=mvec[3], sgu_ln_b=mvec[4], cfm_conv_b=mvec[5],
                        cfm_ln_g=mvec[6], cfm_ln_b=mvec[7], b_sgu=dbs[:, :, 0],
                        w_sgu=jnp.where(tril[None], dws, 0.0), b_ada=dmods[l], w_short=mvec[0:SHORT_K],
                        cfm_conv_w=dcw[0:CFM_K])
    grad_x = dxup.reshape(x.shape)

    gpack = _pack(lambda name, l: fsums[0] if name == "final_g" else small[l][name], D)
    gathered = _all_gather([gpack], "ag_small", deps=(late_prev["tok"],))[0]
    sharded_small = ("w_short", "cfm_conv_w")
    packs = [_pack(lambda name, l, T=T: T["final_g"] if name == "final_g" else (None if name in sharded_small else T[name][l]), D)
             for T in (W, Mo, Vo)]
    sg, sd, sm, sv_ = _adam_small(gathered, *packs, name="adam_small")
    out = {}
    for name in order:
        if name in SMALL_ROWS and name not in sharded_small:
            out[name] = tuple(_unpack(p, name, W[name].shape) for p in (sg, sd, sm, sv_))
    out["final_g"] = tuple(p[FINAL_ROW] for p in (sg, sd, sm, sv_))

    def my_cols(name):
        full = _unpack(sg, name, (DEPTH, SMALL_ROWS[name][1], D))
        return lax.dynamic_slice_in_dim(full, dev * ncs, ncs, axis=2)

    gcs = jnp.concatenate([my_cols("w_short").reshape(-1, ncs), my_cols("cfm_conv_w").reshape(-1, ncs)])
    ncr = gcs.shape[0]
    padr = (-ncr) % 8
    cat = lambda T: jnp.pad(jnp.concatenate([T["w_short"].reshape(-1, ncs), T["cfm_conv_w"].reshape(-1, ncs)]), ((0, padr), (0, 0)))
    cd, cm, cv = _adam_plain(jnp.pad(gcs, ((0, padr), (0, 0))), cat(W), cat(Mo), cat(Vo), "adam_convw")
    nsh = DEPTH * SHORT_K
    out["w_short"] = tuple(a[0:nsh].reshape(w_short.shape) for a in (gcs, cd, cm, cv))
    out["cfm_conv_w"] = tuple(a[nsh:ncr].reshape(cfm_conv_w.shape) for a in (gcs, cd, cm, cv))

    dm_all = jnp.stack([gathered[:, l * ROWS_PER_LAYER + 136:l * ROWS_PER_LAYER + 136 + N_MOD, :].reshape(NDEV, N_MOD * D)
                        for l in range(DEPTH)])
    dm_mine = lax.dynamic_slice_in_dim(dm_all, dev * ncol, ncol, axis=2)
    out["w_ada"] = tuple(_adam_ada(jnp.transpose(c_act), dm_mine, w_ada, m_w_ada, v_w_ada, "adam_ada"))

    adam_group(late_names, *_scatter_finish(late_prev, out["w_ada"][0], "rs_late0"), 0)
    for n in early_names + late_names:
        out[n] = tuple(results[n])

    grads = [out[n][0] for n in order]
    deltas = [out[n][1] for n in order]
    new_m = [out[n][2] for n in order]
    new_v = [out[n][3] for n in order]
    return (loss, grad_x, *grads, *deltas, *new_m, *new_v)
```

```python
import functools
import math

import jax
import jax.numpy as jnp
from jax import lax
from jax.experimental import pallas as pl
from jax.experimental.pallas import tpu as pltpu

F32, BF16 = jnp.float32, jnp.bfloat16
NDEV = 8
NCHIP = NDEV // 2
DEPTH = 2
EPS = 1e-6
CHUNK = 128
NG = 8
SHORT_K = 3
CFM_K = 31
HALO = 32
N_MOD = 6
LANE = 128
VMEM_LIMIT = 56 * 1024 * 1024
ADAM_LR, ADAM_B1, ADAM_B2, ADAM_EPS, ADAM_WD, ADAM_STEP = 0.001, 0.9, 0.999, 1e-08, 0.01, 10
_G0 = math.sqrt(2.0 / math.pi)
_G1 = 0.044715
MESH = pl.DeviceIdType.MESH
ANY = pl.BlockSpec(memory_space=pl.ANY)


def _pcall(body, **kw):
    return pl.pallas_call(body, **kw)


def _params(sem=None):
    return pltpu.CompilerParams(dimension_semantics=sem, vmem_limit_bytes=VMEM_LIMIT)


def _sds(shape, dtype):
    return jax.ShapeDtypeStruct(tuple(shape), dtype)


def _mm_body(dims, nk, out_f32):
    def body(a_ref, b_ref, o_ref, *scr):
        k = pl.program_id(2)
        part = lax.dot_general(a_ref[...], b_ref[...], dims, preferred_element_type=F32)
        if nk == 1:
            o_ref[...] = part.reshape(o_ref.shape).astype(o_ref.dtype)
        elif out_f32:
            @pl.when(k == 0)
            def _():
                o_ref[...] = part.reshape(o_ref.shape)

            @pl.when(k > 0)
            def _():
                o_ref[...] += part.reshape(o_ref.shape)
        else:
            acc = scr[0]

            @pl.when(k == 0)
            def _():
                acc[...] = part

            @pl.when(k > 0)
            def _():
                acc[...] += part

            @pl.when(k == nk - 1)
            def _():
                o_ref[...] = acc[...].astype(o_ref.dtype)
    return body


def _after(body, n_in, deps):
    nd = len(deps)
    if nd == 0:
        return body

    def ordered(*refs):
        return body(*refs[:n_in], *refs[n_in + nd:])
    return ordered


def _mm_call(body, grid, in_specs, out_spec, out_shape, acc_shape, name, deps=()):
    scratch = [] if acc_shape is None else [pltpu.VMEM(acc_shape, F32)]
    return _pcall(_after(body, 2, deps), grid=grid, in_specs=in_specs + [ANY] * len(deps), out_specs=out_spec,
                  out_shape=out_shape, scratch_shapes=scratch, name=name,
                  compiler_params=_params(("parallel", "parallel", "arbitrary")))


def _mm_nn(a, b3, out_dtype, tm, tn, tk, name, w_outer=False, deps=()):
    M, K = a.shape
    G, _, Nb = b3.shape
    npb, nk = Nb // tn, K // tk
    out_f32 = out_dtype == F32
    body = _mm_body((((1,), (0,)), ((), ())), nk, out_f32)
    if w_outer:
        grid = (G * npb, M // tm, nk)
        ij = lambda p, q: (q, p)
    else:
        grid = (M // tm, G * npb, nk)
        ij = lambda p, q: (p, q)

    def a_map(p, q, k):
        i, j = ij(p, q)
        return (i, k)

    def b_map(p, q, k):
        i, j = ij(p, q)
        return (j // npb, k, j % npb)

    def o_map(p, q, k):
        return ij(p, q)

    def wrapped(a_ref, b_ref, o_ref, *scr):
        body(a_ref, b_ref, o_ref, *scr)

    return _mm_call(wrapped, grid, [pl.BlockSpec((tm, tk), a_map), pl.BlockSpec((None, tk, tn), b_map)],
                    pl.BlockSpec((tm, tn), o_map), _sds((M, G * Nb), out_dtype),
                    None if (nk == 1 or out_f32) else (tm, tn), name, deps)(a, b3, *deps)


def _mm_nt(a, b3, out_dtype, tm, tn, tk, name, deps=()):
    M, _ = a.shape
    G, Ko, Nb = b3.shape
    kpb = Nb // tk
    nk = G * kpb
    out_f32 = out_dtype == F32
    body = _mm_body((((1,), (1,)), ((), ())), nk, out_f32)

    def wrapped(a_ref, b_ref, o_ref, *scr):
        body(a_ref, b_ref, o_ref, *scr)

    return _mm_call(wrapped, (M // tm, Ko // tn, nk),
                    [pl.BlockSpec((tm, tk), lambda i, j, k: (i, k)),
                     pl.BlockSpec((None, tn, tk), lambda i, j, k: (k // kpb, j, k % kpb))],
                    pl.BlockSpec((tm, tn), lambda i, j, k: (i, j)), _sds((M, Ko), out_dtype),
                    None if (nk == 1 or out_f32) else (tm, tn), name, deps)(a, b3, *deps)


def _mm_tn(a, b, G, tm, tn, tk, name, deps=()):
    T, M = a.shape
    Nb = b.shape[1] // G
    npb, nk = Nb // tn, T // tk
    body = _mm_body((((0,), (0,)), ((), ())), nk, False)

    def wrapped(a_ref, b_ref, o_ref, *scr):
        body(a_ref, b_ref, o_ref, *scr)

    in_specs = [pl.BlockSpec((tk, tm), lambda i, j, k: (k, i)), pl.BlockSpec((tk, tn), lambda i, j, k: (k, j))]
    out_spec = pl.BlockSpec((None, tm, tn), lambda i, j, k: (j // npb, i, j % npb))
    return _mm_call(wrapped, (M // tm, G * npb, nk), in_specs, out_spec, _sds((G, M, Nb), BF16),
                    None if nk == 1 else (tm, tn), name, deps)(a, b, *deps)


def _rsum(v):
    return jnp.sum(v, axis=0, keepdims=True)


def _rmean(v):
    return jnp.mean(v, axis=-1, keepdims=True)


def _gelu(x):
    t = jnp.tanh(_G0 * (x + _G1 * (x * x * x)))
    return x * (0.5 * (1.0 + t)), t


def _dgelu(x, t):
    return 0.5 * (1.0 + t) + 0.5 * x * (1.0 - t * t) * (_G0 * (1.0 + 3.0 * _G1 * (x * x)))


def _sigmoid(x):
    return 1.0 / (1.0 + jnp.exp(-x))


def _fill_shifted(ext, rot):
    v = ext[...]
    n = v.shape[0]
    for b in range(1, 8):
        rot[b - 1] = pltpu.roll(v, n - b, 0)


def _rows_at(ext, rot, s, tm, cs=slice(None)):
    a, b = divmod(s, 8)
    return ext[8 * a:8 * a + tm, cs] if b == 0 else rot[b - 1, 8 * a:8 * a + tm, cs]


def _causal_conv(w_ref, taps, bias, ext, rot, offset, tm, out):
    D = out.shape[1]
    for cb in range(D // LANE):
        cs = slice(cb * LANE, (cb + 1) * LANE)
        acc = None
        for k, o in zip(taps, offset):
            term = w_ref[k:k + 1, cs] * _rows_at(ext, rot, o, tm, cs)
            acc = term if acc is None else acc + term
        out[:, cs] = acc if bias is None else acc + bias[:, cs]


def _rows(*vs):
    a = jnp.stack([v.astype(F32) for v in vs])
    return jnp.pad(a, ((0, 8 - len(vs)), (0, 0)))


def _row_spec(tm, D):
    return pl.BlockSpec((tm, D), lambda i: (i, 0))


def _const_spec(shape):
    nd = len(shape)
    return pl.BlockSpec(shape, lambda i: (0,) * nd)


def _norm_fwd(xp, f, vec, name, deps=()):
    S, D = xp.shape
    tm = min(256, S)
    has_f = f is not None

    def body(*refs):
        if has_f:
            xp_ref, f_ref, vec_ref, xo_ref, h_ref = refs
            x = xp_ref[...] + vec_ref[0:1, :] * f_ref[...]
            xo_ref[...] = x
        else:
            xp_ref, vec_ref, h_ref = refs
            x = xp_ref[...]
        r = lax.rsqrt(_rmean(x * x) + EPS)
        h = (x * r) * vec_ref[1:2, :]
        h_ref[...] = (h * (1.0 + vec_ref[2:3, :]) + vec_ref[3:4, :]).astype(BF16)

    rs = _row_spec(tm, D)
    ins = [xp, f, vec] if has_f else [xp, vec]
    in_specs = ([rs, rs] if has_f else [rs]) + [_const_spec((8, D))]
    out_shape = ([_sds((S, D), F32)] if has_f else []) + [_sds((S, D), BF16)]
    out_specs = [rs] * len(out_shape)
    outs = _pcall(_after(body, len(ins), deps), grid=(S // tm,), in_specs=in_specs + [ANY] * len(deps),
                  out_specs=out_specs, out_shape=out_shape, name=name,
                  compiler_params=_params(("parallel",)))(*ins, *deps)
    return (outs[0], outs[1]) if has_f else (xp, outs[0])


def _mixer_fwd(z, wsh, sgu_ln, wtril, bias_full, cw, cvec, name, deps=()):
    S = z.shape[0]
    D = wsh.shape[1]
    tm = CHUNK

    def body(z_ref, wsh_ref, sln_ref, wt_ref, bias_ref, cw_ref, cv_ref, oa_ref, ob_ref, oc_ref, pe, ge, gr, cbuf):
        i = pl.program_id(0)

        @pl.when(i == 0)
        def _():
            pe[0:HALO, :] = jnp.zeros((HALO, D), F32)
            ge[0:HALO, :] = jnp.zeros((HALO, D), F32)

        def col(n):
            return z_ref[:, n * D:(n + 1) * D].astype(F32)

        pe[HALO:HALO + tm, :] = col(1) * col(2)
        q = wsh_ref[0:1, :] * pe[HALO - 2:HALO - 2 + tm, :]
        q = q + wsh_ref[1:2, :] * pe[HALO - 1:HALO - 1 + tm, :]
        q = q + wsh_ref[2:3, :] * pe[HALO:HALO + tm, :]
        oa_ref[...] = (col(0) * q).astype(BF16)
        gu, _ = _gelu(col(3))
        gv, _ = _gelu(col(4))
        d = gv - _rmean(gv)
        nrm = d * lax.rsqrt(_rmean(d * d) + EPS)
        vnb = (nrm * sln_ref[0:1, :] + sln_ref[1:2, :]).astype(BF16)
        for g in range(NG):
            cs = slice(g * LANE, (g + 1) * LANE)
            mixed = jnp.dot(wt_ref[g], vnb[:, cs], preferred_element_type=F32) + bias_ref[:, cs]
            ob_ref[:, cs] = (gu[:, cs] * mixed).astype(BF16)
        ge[HALO:HALO + tm, :] = col(5) * _sigmoid(col(6))
        _fill_shifted(ge, gr)
        o0 = HALO - (CFM_K - 1)
        _causal_conv(cw_ref, range(CFM_K), cv_ref[0:1, :], ge, gr, range(o0, o0 + CFM_K), tm, cbuf)
        conv = cbuf[...]
        d = conv - _rmean(conv)
        ln = (d * lax.rsqrt(_rmean(d * d) + EPS)) * cv_ref[1:2, :] + cv_ref[2:3, :]
        oc_ref[...] = (ln * _sigmoid(ln)).astype(BF16)
        pe[0:HALO, :] = pe[tm:tm + HALO, :]
        ge[0:HALO, :] = ge[tm:tm + HALO, :]

    rs = _row_spec(tm, D)
    return _pcall(
        _after(body, 7, deps), grid=(S // tm,),
        in_specs=[pl.BlockSpec((tm, 7 * D), lambda i: (i, 0)), _const_spec((8, D)), _const_spec((8, D)),
                  _const_spec((NG, CHUNK, CHUNK)), _const_spec((CHUNK, D)), _const_spec((HALO, D)), _const_spec((8, D))]
        + [ANY] * len(deps),
        out_specs=[rs, rs, rs], out_shape=[_sds((S, D), BF16)] * 3,
        scratch_shapes=[pltpu.VMEM((HALO + tm, D), F32), pltpu.VMEM((HALO + tm, D), F32),
                        pltpu.VMEM((7, HALO + tm, D), F32), pltpu.VMEM((tm, D), F32)],
        name=name, compiler_params=_params(("arbitrary",)))(z, wsh, sgu_ln, wtril, bias_full, cw, cvec, *deps)


def _branch_out(acts, ws, z, name):
    S, D = acts[0].shape
    tm = min(256, S)

    def body(a0, a1, a2, w0, w1, w2, g0, g1, g2, m_ref, y_ref):
        m = None
        for n, (a, w, g) in enumerate(((a0, w0, g0), (a1, w1, g1), (a2, w2, g2))):
            y = jnp.dot(a[...], w[...], preferred_element_type=F32)
            y_ref[n] = y.astype(BF16)
            t = _sigmoid(g[...].astype(F32)) * y
            m = t if m is None else m + t
        m_ref[...] = m.astype(BF16)

    rs = _row_spec(tm, D)
    gate_specs = [pl.BlockSpec((tm, D), functools.partial(lambda i, n: (i, 7 + n), n=n)) for n in range(3)]
    return _pcall(body, grid=(S // tm,),
                  in_specs=[rs, rs, rs] + [_const_spec((D, D))] * 3 + gate_specs,
                  out_specs=[rs, pl.BlockSpec((3, tm, D), lambda i: (0, i, 0))],
                  out_shape=[_sds((S, D), BF16), _sds((3, S, D), BF16)], name=name,
                  compiler_params=_params(("parallel",)))(*acts, *ws, z, z, z)


def _swiglu_fwd(gu, name):
    S, F2 = gu.shape
    F = F2 // 2
    tm = min(256, S)

    def body(g_ref, u_ref, o_ref):
        g = g_ref[...].astype(F32)
        o_ref[...] = ((g * _sigmoid(g)) * u_ref[...].astype(F32)).astype(BF16)

    return _pcall(body, grid=(S // tm,),
                  in_specs=[pl.BlockSpec((tm, F), lambda i: (i, 0)), pl.BlockSpec((tm, F), lambda i: (i, 1))],
                  out_specs=pl.BlockSpec((tm, F), lambda i: (i, 0)), out_shape=_sds((S, F), BF16), name=name,
                  compiler_params=_params(("parallel",)))(gu, gu)


def _swiglu_bwd(dact, gu, name):
    S, F2 = gu.shape
    F = F2 // 2
    tm = min(128, S)

    def body(d_ref, g_ref, u_ref, o_ref):
        g = g_ref[...].astype(F32)
        sg = _sigmoid(g)
        d = d_ref[...]
        o_ref[:, 0:F] = (d * u_ref[...].astype(F32) * (sg * (1.0 + g * (1.0 - sg)))).astype(BF16)
        o_ref[:, F:2 * F] = (d * (g * sg)).astype(BF16)

    return _pcall(body, grid=(S // tm,),
                  in_specs=[pl.BlockSpec((tm, F), lambda i: (i, 0)), pl.BlockSpec((tm, F), lambda i: (i, 0)),
                            pl.BlockSpec((tm, F), lambda i: (i, 1))],
                  out_specs=pl.BlockSpec((tm, F2), lambda i: (i, 0)), out_shape=_sds((S, F2), BF16), name=name,
                  compiler_params=_params(("parallel",)))(dact, gu, gu)


def _final_bwd(x1, f, tgt, vec, name):
    S, D = x1.shape
    tm = min(256, S)

    def body(x_ref, f_ref, t_ref, vec_ref, dx_ref, df_ref, sums_ref, loss_ref):
        @pl.when(pl.program_id(0) == 0)
        def _():
            sums_ref[...] = jnp.zeros_like(sums_ref)
            loss_ref[...] = jnp.zeros_like(loss_ref)

        gate, fg = vec_ref[0:1, :], vec_ref[1:2, :]
        fv = f_ref[...]
        x = x_ref[...] + gate * fv
        r = lax.rsqrt(_rmean(x * x) + EPS)
        xn = x * r
        diff = xn * fg - t_ref[...]
        per_tok = _rmean(diff * diff)
        loss_ref[...] += 0.5 * jnp.sum(per_tok, axis=0, keepdims=True)
        dy = diff * (1.0 / D)
        sums_ref[0:1, :] += _rsum(dy * xn)
        dxn = dy * fg
        dx = r * (dxn - xn * _rmean(dxn * xn))
        sums_ref[1:2, :] += _rsum(dx * fv)
        dx_ref[...] = dx
        df_ref[...] = (dx * gate).astype(BF16)

    rs = _row_spec(tm, D)
    return _pcall(body, grid=(S // tm,), in_specs=[rs, rs, rs, _const_spec((8, D))],
                  out_specs=[rs, rs, _const_spec((8, D)), _const_spec((8, LANE))],
                  out_shape=[_sds((S, D), F32), _sds((S, D), BF16), _sds((8, D), F32), _sds((8, LANE), F32)],
                  name=name, compiler_params=_params(("arbitrary",)))(x1, f, tgt, vec)


def _norm_bwd(xin, dh, dxup, vec, fprev, name, deps=()):
    S, D = xin.shape
    tm = min(256, S)
    has_prev = fprev is not None

    def body(*refs):
        if has_prev:
            x_ref, dh_ref, up_ref, vec_ref, fp_ref, dx_ref, dp_ref, sums_ref = refs
        else:
            x_ref, dh_ref, up_ref, vec_ref, dx_ref, sums_ref = refs

        @pl.when(pl.program_id(0) == 0)
        def _():
            sums_ref[...] = jnp.zeros_like(sums_ref)

        g, scale = vec_ref[0:1, :], vec_ref[1:2, :]
        x = x_ref[...]
        r = lax.rsqrt(_rmean(x * x) + EPS)
        xn = x * r
        dhv = dh_ref[...]
        sums_ref[0:1, :] += _rsum(dhv)
        sums_ref[1:2, :] += _rsum(dhv * (xn * g))
        dm = dhv * (1.0 + scale)
        sums_ref[2:3, :] += _rsum(dm * xn)
        dxn = dm * g
        dx = up_ref[...] + r * (dxn - xn * _rmean(dxn * xn))
        dx_ref[...] = dx
        if has_prev:
            sums_ref[3:4, :] += _rsum(dx * fp_ref[...])
            dp_ref[...] = (dx * vec_ref[2:3, :]).astype(BF16)

    rs = _row_spec(tm, D)
    ins = [xin, dh, dxup, vec] + ([fprev] if has_prev else [])
    in_specs = [rs, rs, rs, _const_spec((8, D))] + ([rs] if has_prev else [])
    out_shape = [_sds((S, D), F32)] + ([_sds((S, D), BF16)] if has_prev else []) + [_sds((8, D), F32)]
    out_specs = [rs] + ([rs] if has_prev else []) + [_const_spec((8, D))]
    outs = _pcall(_after(body, len(ins), deps), grid=(S // tm,), in_specs=in_specs + [ANY] * len(deps),
                  out_specs=out_specs, out_shape=out_shape, name=name,
                  compiler_params=_params(("arbitrary",)))(*ins, *deps)
    return (outs[0], outs[1], outs[2]) if has_prev else (outs[0], None, outs[1])


def _gate_bwd(dmerged, z, ys, name, deps=()):
    S, D = dmerged.shape
    tm = min(256, S)
    ncol = z.shape[1] // D

    def body(dm_ref, g_ref, y_ref, dy_ref, dz_ref):
        sg = _sigmoid(g_ref[...].astype(F32))
        dm = dm_ref[...]
        dy_ref[...] = (dm * sg).astype(BF16)
        dz_ref[...] = (dm * y_ref[...].astype(F32) * (sg * (1.0 - sg))).astype(BF16)

    return _pcall(_after(body, 3, deps), grid=(S // tm, 3),
                  in_specs=[pl.BlockSpec((tm, D), lambda i, n: (i, 0)), pl.BlockSpec((tm, D), lambda i, n: (i, 7 + n)),
                            pl.BlockSpec((None, tm, D), lambda i, n: (n, i, 0))] + [ANY] * len(deps),
                  out_specs=[pl.BlockSpec((None, tm, D), lambda i, n: (n, i, 0)),
                             pl.BlockSpec((tm, D), lambda i, n: (i, 7 + n))],
                  out_shape=[_sds((3, S, D), BF16), _sds((S, ncol * D), BF16)], name=name,
                  compiler_params=_params(("parallel", "arbitrary")))(dmerged, z, ys, *deps)


def _mixer_bwd(z, dacts, dz, wsh, sgu_ln, wtril, wtril_t, bias_full, cw, cvec, name):
    S = z.shape[0]
    D = wsh.shape[1]
    tm = CHUNK
    nt = S // tm
    hb = tm // HALO

    def body(zc, zp, da_ref, db_ref, dc_ref, wsh_ref, sln_ref, wt_ref, wtt_ref, bias_ref, cw_ref, cv_ref, _dz_in,
             dz_ref, vec_ref, dcw_ref, dws_ref, dbs_ref, pe, ge, dqe, dce, gr, dcr, cbuf):
        i = pl.program_id(0)
        rb = nt - 1 - i

        @pl.when(i == 0)
        def _():
            vec_ref[...] = jnp.zeros_like(vec_ref)
            dcw_ref[...] = jnp.zeros_like(dcw_ref)
            dws_ref[...] = jnp.zeros_like(dws_ref)
            dbs_ref[...] = jnp.zeros_like(dbs_ref)
            dqe[tm:tm + HALO, :] = jnp.zeros((HALO, D), F32)
            dce[tm:tm + HALO, :] = jnp.zeros((HALO, D), F32)

        keep = (rb > 0).astype(F32)

        def col(n):
            return zc[:, n * D:(n + 1) * D].astype(F32)

        def pcol(n):
            return zp[:, n * D:(n + 1) * D].astype(F32)

        c_a, x_a = col(1), col(2)
        pe[0:HALO, :] = keep * (pcol(1) * pcol(2))
        pe[HALO:HALO + tm, :] = c_a * x_a
        q = wsh_ref[0:1, :] * pe[HALO - 2:HALO - 2 + tm, :]
        q = q + wsh_ref[1:2, :] * pe[HALO - 1:HALO - 1 + tm, :]
        q = q + wsh_ref[2:3, :] * pe[HALO:HALO + tm, :]
        dact = da_ref[...]
        dz_ref[:, 0:D] = (dact * q).astype(BF16)
        dq = dact * col(0)
        dqe[0:tm, :] = dq
        dp = wsh_ref[2:3, :] * dq + wsh_ref[1:2, :] * dqe[1:1 + tm, :] + wsh_ref[0:1, :] * dqe[2:2 + tm, :]
        dz_ref[:, D:2 * D] = (dp * x_a).astype(BF16)
        dz_ref[:, 2 * D:3 * D] = (dp * c_a).astype(BF16)
        for k in range(SHORT_K):
            o = HALO - (SHORT_K - 1) + k
            vec_ref[k:k + 1, :] += _rsum(dq * pe[o:o + tm, :])
        u, v = col(3), col(4)
        gu, tu = _gelu(u)
        gv, tv = _gelu(v)
        d = gv - _rmean(gv)
        rstd = lax.rsqrt(_rmean(d * d) + EPS)
        nrm = d * rstd
        vnb = (nrm * sln_ref[0:1, :] + sln_ref[1:2, :]).astype(BF16)
        dact = db_ref[...]
        dvn_parts, dgu_parts = [], []
        for g in range(NG):
            cs = slice(g * LANE, (g + 1) * LANE)
            vg = vnb[:, cs]
            mixed = jnp.dot(wt_ref[g], vg, preferred_element_type=F32) + bias_ref[:, cs]
            dgu_parts.append(dact[:, cs] * mixed)
            dmixed = dact[:, cs] * gu[:, cs]
            dmb = dmixed.astype(BF16)
            dws_ref[g] += lax.dot_general(dmb, vg, (((1,), (1,)), ((), ())), preferred_element_type=F32)
            dbs_ref[g] += jnp.broadcast_to(jnp.sum(dmixed, axis=1, keepdims=True), (CHUNK, LANE))
            dvn_parts.append(jnp.dot(wtt_ref[g], dmb, preferred_element_type=F32))
        dgu = jnp.concatenate(dgu_parts, axis=1)
        dvn = jnp.concatenate(dvn_parts, axis=1)
        dz_ref[:, 3 * D:4 * D] = (dgu * _dgelu(u, tu)).astype(BF16)
        vec_ref[3:4, :] += _rsum(dvn * nrm)
        vec_ref[4:5, :] += _rsum(dvn)
        dn = dvn * sln_ref[0:1, :]
        dgv = rstd * (dn - _rmean(dn) - nrm * _rmean(dn * nrm))
        dz_ref[:, 4 * D:5 * D] = (dgv * _dgelu(v, tv)).astype(BF16)
        a_c = col(5)
        sg = _sigmoid(col(6))
        ge[0:HALO, :] = keep * (pcol(5) * _sigmoid(pcol(6)))
        ge[HALO:HALO + tm, :] = a_c * sg
        _fill_shifted(ge, gr)
        o0 = HALO - (CFM_K - 1)
        _causal_conv(cw_ref, range(CFM_K), cv_ref[0:1, :], ge, gr, range(o0, o0 + CFM_K), tm, cbuf)
        conv = cbuf[...]
        d = conv - _rmean(conv)
        rstd = lax.rsqrt(_rmean(d * d) + EPS)
        nrm = d * rstd
        ln = nrm * cv_ref[1:2, :] + cv_ref[2:3, :]
        sl = _sigmoid(ln)
        dln = dc_ref[...] * (sl * (1.0 + ln * (1.0 - sl)))
        vec_ref[6:7, :] += _rsum(dln * nrm)
        vec_ref[7:8, :] += _rsum(dln)
        dn = dln * cv_ref[1:2, :]
        dconv = rstd * (dn - _rmean(dn) - nrm * _rmean(dn * nrm))
        vec_ref[5:6, :] += _rsum(dconv)
        dce[0:tm, :] = dconv
        _fill_shifted(dce, dcr)
        _causal_conv(cw_ref, range(CFM_K), None, dce, dcr, [CFM_K - 1 - k for k in range(CFM_K)], tm, cbuf)
        dglu = cbuf[...]
        for cb in range(D // LANE):
            cs = slice(cb * LANE, (cb + 1) * LANE)
            dcv = dce[0:tm, cs]
            for k in range(CFM_K):
                dcw_ref[k:k + 1, cs] += _rsum(dcv * _rows_at(ge, gr, o0 + k, tm, cs))
        dz_ref[:, 5 * D:6 * D] = (dglu * sg).astype(BF16)
        dz_ref[:, 6 * D:7 * D] = (dglu * a_c * (sg * (1.0 - sg))).astype(BF16)
        dqe[tm:tm + HALO, :] = dqe[0:HALO, :]
        dce[tm:tm + HALO, :] = dce[0:HALO, :]

    rev = lambda i: (nt - 1 - i, 0)
    rs = pl.BlockSpec((tm, D), rev)
    cur = pl.BlockSpec((tm, 7 * D), rev)
    prev = pl.BlockSpec((HALO, 7 * D), lambda i: (jnp.maximum((nt - 1 - i) * hb - 1, 0), 0))
    ext = pltpu.VMEM((HALO + tm, D), F32)
    outs = _pcall(
        body, grid=(nt,),
        in_specs=[cur, prev, rs, rs, rs, _const_spec((8, D)), _const_spec((8, D)), _const_spec((NG, CHUNK, CHUNK)),
                  _const_spec((NG, CHUNK, CHUNK)), _const_spec((CHUNK, D)), _const_spec((HALO, D)), _const_spec((8, D)),
                  ANY],
        out_specs=[cur, _const_spec((8, D)), _const_spec((HALO, D)), _const_spec((NG, CHUNK, CHUNK)),
                   _const_spec((NG, CHUNK, LANE))],
        out_shape=[_sds(dz.shape, BF16), _sds((8, D), F32), _sds((HALO, D), F32), _sds((NG, CHUNK, CHUNK), F32),
                   _sds((NG, CHUNK, LANE), F32)],
        scratch_shapes=[ext, ext, ext, ext, pltpu.VMEM((7, HALO + tm, D), F32), pltpu.VMEM((7, HALO + tm, D), F32),
                        pltpu.VMEM((tm, D), F32)],
        input_output_aliases={12: 0}, name=name,
        compiler_params=_params(("arbitrary",)))(z, z, *dacts, wsh, sgu_ln, wtril, wtril_t, bias_full, cw, cvec, dz)
    return outs


def _ada_fwd(c_all, w_ada_loc, name):
    nb, D = c_all.shape
    L, _, nc = w_ada_loc.shape

    def body(c_ref, w_ref, o_ref, ca_ref):
        cv = c_ref[...]
        ca = cv * _sigmoid(cv)
        ca_ref[...] = ca
        o_ref[...] = jnp.dot(ca.astype(BF16), w_ref[...].astype(BF16), preferred_element_type=F32)

    return _pcall(body, grid=(L,),
                  in_specs=[_const_spec((nb, D)), pl.BlockSpec((None, D, nc), lambda l: (l, 0, 0))],
                  out_specs=[pl.BlockSpec((None, nb, nc), lambda l: (l, 0, 0)), _const_spec((nb, D))],
                  out_shape=[_sds((L, nb, nc), F32), _sds((nb, D), F32)], name=name,
                  compiler_params=_params(("arbitrary",)))(c_all, w_ada_loc)


def _adamw(w, g, m, v):
    m = ADAM_B1 * m + (1.0 - ADAM_B1) * g
    v = ADAM_B2 * v + (1.0 - ADAM_B2) * (g * g)
    m_hat = m / (1.0 - ADAM_B1 ** ADAM_STEP)
    v_hat = v / (1.0 - ADAM_B2 ** ADAM_STEP)
    delta = -ADAM_LR * (m_hat / (jnp.sqrt(v_hat) + ADAM_EPS) + ADAM_WD * w)
    return delta, m, v


def _tile_rows(R, C, align=8):
    cap = max(align, (1536 * 1024) // (4 * C))
    best = None
    for t in range(align, R + 1, align):
        if R % t == 0 and t <= cap:
            best = t
    return R if best is None else best


def _adam_ada(ct, dm, w, m, v, name):
    L, D, nc = w.shape
    nb = ct.shape[1]
    tr = _tile_rows(D, nc)

    def body(ct_ref, dm_ref, w_ref, m_ref, v_ref, g_ref, d_ref, mo_ref, vo_ref):
        g = ct_ref[:, 0:1] * dm_ref[0:1, :]
        for b in range(1, nb):
            g = g + ct_ref[:, b:b + 1] * dm_ref[b:b + 1, :]
        g_ref[...] = g
        d_ref[...], mo_ref[...], vo_ref[...] = _adamw(w_ref[...], g, m_ref[...], v_ref[...])

    ws = pl.BlockSpec((None, tr, nc), lambda l, r: (l, r, 0))
    return _pcall(body, grid=(L, D // tr),
                  in_specs=[pl.BlockSpec((tr, nb), lambda l, r: (r, 0)), pl.BlockSpec((None, nb, nc), lambda l, r: (l, 0, 0)),
                            ws, ws, ws],
                  out_specs=[ws] * 4, out_shape=[_sds(w.shape, F32)] * 4, name=name,
                  compiler_params=_params(("parallel", "parallel")))(ct, dm, w, m, v)


def _adam_small(parts, w, m, v, name, deps=()):
    n, R, C = parts.shape
    tr = _tile_rows(R, C * n // 2)

    def body(p_ref, w_ref, m_ref, v_ref, g_ref, d_ref, mo_ref, vo_ref):
        g = p_ref[0]
        for j in range(1, n):
            g = g + p_ref[j]
        g_ref[...] = g
        d_ref[...], mo_ref[...], vo_ref[...] = _adamw(w_ref[...], g, m_ref[...], v_ref[...])

    ws = pl.BlockSpec((tr, C), lambda r: (r, 0))
    return _pcall(_after(body, 4, deps), grid=(R // tr,),
                  in_specs=[pl.BlockSpec((n, tr, C), lambda r: (0, r, 0)), ws, ws, ws] + [ANY] * len(deps),
                  out_specs=[ws] * 4, out_shape=[_sds((R, C), F32)] * 4, name=name,
                  compiler_params=_params(("parallel",)))(parts, w, m, v, *deps)


def _adam_plain(g, w, m, v, name):
    R, C = w.shape

    def body(g_ref, w_ref, m_ref, v_ref, d_ref, mo_ref, vo_ref):
        d_ref[...], mo_ref[...], vo_ref[...] = _adamw(w_ref[...], g_ref[...], m_ref[...], v_ref[...])

    ws = _const_spec((R, C))
    return _pcall(body, grid=(1,), in_specs=[ws] * 4, out_specs=[ws] * 3, out_shape=[_sds((R, C), F32)] * 3, name=name,
                  compiler_params=_params(("arbitrary",)))(g, w, m, v)


def _pair_sum(G, R1, my_c, name):
    n, R, C = G.shape
    half = n // 2
    tr = _tile_rows(R, C, align=16)

    def body(c_ref, g_ref, r_ref, o_ref):
        o_ref[...] = (g_ref[...].astype(F32) + r_ref[...].astype(F32)).astype(o_ref.dtype)

    blk = (None, tr, C)
    gs = pltpu.PrefetchScalarGridSpec(
        num_scalar_prefetch=1, grid=(half, R // tr),
        in_specs=[pl.BlockSpec(blk, lambda p, r, c: (2 * p + c[0], r, 0)), pl.BlockSpec(blk, lambda p, r, c: (p, r, 0))],
        out_specs=pl.BlockSpec(blk, lambda p, r, c: (p, r, 0)))
    return _pcall(body, grid_spec=gs, out_shape=_sds((half, R, C), G.dtype), name=name,
                  compiler_params=_params(("parallel", "parallel")))(my_c, G, R1)


def _adam_big(P, R2, my_chip, w, m, v, layer, prev, name, deps=()):
    _, R, C = P.shape
    nrecv = R2.shape[0]
    tr = _tile_rows(R, C, align=16)

    def body(p_sm, p_ref, r_ref, w_ref, m_ref, v_ref, *rest):
        g_ref, d_ref, mo_ref, vo_ref = rest[-4:]
        g = p_ref[...].astype(F32)
        for k in range(nrecv):
            g = g + r_ref[k].astype(F32)
        g_ref[...] = g
        d_ref[...], mo_ref[...], vo_ref[...] = _adamw(w_ref[...], g, m_ref[...], v_ref[...])

    ws = pl.BlockSpec((None, tr, C), lambda r, p: (layer, r, 0))
    held = [] if prev is None else list(prev)
    gs = pltpu.PrefetchScalarGridSpec(
        num_scalar_prefetch=1, grid=(R // tr,),
        in_specs=[pl.BlockSpec((None, tr, C), lambda r, p: (p[0], r, 0)),
                  pl.BlockSpec((nrecv, tr, C), lambda r, p: (0, r, 0)), ws, ws, ws] + [ANY] * (len(held) + len(deps)),
        out_specs=[ws] * 4)
    alias = {6 + i: i for i in range(len(held))}
    return _pcall(body, grid_spec=gs, out_shape=[_sds(w.shape, F32)] * 4, name=name, input_output_aliases=alias,
                  compiler_params=_params(("parallel",)))(my_chip, P, R2, w, m, v, *held, *deps)


def _place():
    return lax.axis_index("x"), lax.axis_index("y"), lax.axis_index("c")


def _sum_over_devices(scalar):
    return lax.psum(scalar, ("x", "y", "c"))


def _all_gather(shards, name, deps=()):
    n = len(shards)

    def body(*refs):
        ins, outs = refs[:n], refs[n:2 * n]
        send_sems, recv_sems, local_sems = refs[2 * n:]
        x, y, c = _place()
        me, sibling = (x, y, c), (x, y, 1 - c)
        chips = [(1 - x, y), (x, 1 - y), (1 - x, 1 - y)]

        def slot(a, px, py, pc):
            return outs[a].at[4 * px + 2 * py + pc]

        def copy(a, k, block, to, src=None):
            return pltpu.make_async_remote_copy(
                src_ref=slot(a, *block) if src is None else src, dst_ref=slot(a, *block),
                send_sem=send_sems.at[7 * a + k], recv_sem=recv_sems.at[7 * a + k], device_id=to, device_id_type=MESH)

        mine = [pltpu.make_async_copy(ins[a], slot(a, *me), local_sems.at[a]) for a in range(n)]
        for cp in mine:
            cp.start()
        first = []
        for a in range(n):
            first.append(copy(a, 0, me, sibling, src=ins[a]))
            first += [copy(a, 1 + j, me, (*chip, c), src=ins[a]) for j, chip in enumerate(chips)]
        for cp in first:
            cp.start()
        passed = []
        for j, chip in enumerate(chips):
            for a in range(n):
                copy(a, 1 + j, (*chip, c), me).wait_recv()
                fwd = copy(a, 4 + j, (*chip, c), sibling)
                fwd.start()
                passed.append(fwd)
        for a in range(n):
            copy(a, 0, sibling, me).wait_recv()
        for j, chip in enumerate(chips):
            for a in range(n):
                copy(a, 4 + j, (*chip, 1 - c), me).wait_recv()
        for cp in first + passed:
            cp.wait_send()
        for cp in mine:
            cp.wait()

    outs = _pcall(_after(body, n, deps), in_specs=[ANY] * (n + len(deps)), out_specs=[ANY] * n,
                  out_shape=[_sds((NDEV,) + s.shape, s.dtype) for s in shards],
                  scratch_shapes=[pltpu.SemaphoreType.DMA((7 * n,)), pltpu.SemaphoreType.DMA((7 * n,)),
                                  pltpu.SemaphoreType.DMA((n,))], name=name)(*shards, *deps)
    return list(outs)


HBM = pl.BlockSpec(memory_space=pltpu.HBM)
SEM = pl.BlockSpec(memory_space=pltpu.SEMAPHORE)


def _copies(plan, refs, send_sems, recv_sems):
    return [pltpu.make_async_remote_copy(src_ref=s, dst_ref=d, send_sem=send_sems.at[k], recv_sem=recv_sems.at[k],
                                         device_id=dev, device_id_type=MESH)
            for k, (s, d, dev) in enumerate(plan(refs, *_place()))]


def _xfer_start(bufs, ncopies, plan, name, deps=()):
    n = len(bufs)

    def body(*refs):
        for cp in _copies(plan, refs[:n], refs[n], refs[n + 1]):
            cp.start()
        token = refs[2 * n + 2]
        token[...] = jnp.zeros_like(token)

    outs = _pcall(
        _after(body, n, deps), name=name,
        out_shape=(pltpu.SemaphoreType.DMA((ncopies,)), pltpu.SemaphoreType.DMA((ncopies,)),
                   *[pltpu.HBM(b.shape, b.dtype) for b in bufs], _sds((8, LANE), F32)),
        in_specs=[HBM] * n + [ANY] * len(deps),
        out_specs=(SEM, SEM, *[HBM] * n, pl.BlockSpec(memory_space=pltpu.VMEM)),
        input_output_aliases={i: 2 + i for i in range(n)},
        compiler_params=pltpu.CompilerParams(has_side_effects=pltpu.SideEffectType.DATAFLOW_SIDE_EFFECTING),
    )(*[pltpu.with_memory_space_constraint(b, pltpu.HBM) for b in bufs], *deps)
    return (outs[0], outs[1]), list(outs[2:2 + n]), outs[2 + n]


def _xfer_wait(sems, bufs, plan, after, name):
    n = len(bufs)

    def body(*refs):
        for cp in _copies(plan, refs[:n], refs[n], refs[n + 1]):
            cp.wait_send()
            cp.wait_recv()

    outs = _pcall(
        body, name=name, out_shape=tuple(pltpu.HBM(b.shape, b.dtype) for b in bufs),
        in_specs=[HBM] * n + [SEM, SEM, ANY], out_specs=tuple([HBM] * n), input_output_aliases={i: i for i in range(n)},
        compiler_params=pltpu.CompilerParams(has_side_effects=pltpu.SideEffectType.DATAFLOW_SIDE_EFFECTING),
    )(*bufs, *sems, after)
    return list(outs)


def _chips_of(x, y):
    return [(1 - x, y), (x, 1 - y), (1 - x, 1 - y)]


def _gather_plan1(n):
    def plan(refs, x, y, c):
        out = []
        for a in range(n):
            blk = refs[a].at[4 * x + 2 * y + c]
            out.append((blk, blk, (x, y, 1 - c)))
            out += [(blk, blk, (px, py, c)) for px, py in _chips_of(x, y)]
        return out
    return plan


def _gather_plan2(n):
    def plan(refs, x, y, c):
        out = []
        for a in range(n):
            for px, py in _chips_of(x, y):
                blk = refs[a].at[4 * px + 2 * py + c]
                out.append((blk, blk, (x, y, 1 - c)))
        return out
    return plan


def _gather_start(shards, dev, name, deps=()):
    lands = [lax.dynamic_update_slice(lax.empty((NDEV,) + s.shape, s.dtype), s[None], (dev,) + (0,) * s.ndim)
             for s in shards]
    n = len(shards)
    sems, lands, tok = _xfer_start(lands, 4 * n, _gather_plan1(n), name + "_p1_start", deps)
    return dict(sems=sems, lands=lands, tok=tok, n=n)


def _gather_mid(st, after, name):
    n = st["n"]
    lands = _xfer_wait(st["sems"], st["lands"], _gather_plan1(n), after, name + "_p1_wait")
    sems, lands, tok = _xfer_start(lands, 3 * n, _gather_plan2(n), name + "_p2_start")
    return dict(sems=sems, lands=lands, tok=tok, n=n)


def _gather_finish(st, after, name):
    return _xfer_wait(st["sems"], st["lands"], _gather_plan2(st["n"]), after, name + "_p2_wait")


def _scatter_plan1(n):
    def plan(refs, x, y, c):
        return [(refs[a].at[2 * p + 1 - c], refs[n + a].at[p], (x, y, 1 - c)) for a in range(n) for p in range(NCHIP)]
    return plan


def _scatter_plan2(n):
    def plan(refs, x, y, c):
        return [(refs[a].at[2 * px + py], refs[n + a].at[j], (px, py, c))
                for a in range(n) for j, (px, py) in enumerate(_chips_of(x, y))]
    return plan


def _scatter_start(Gs, name):
    n = len(Gs)
    R1s = [lax.empty((NCHIP,) + g.shape[1:], g.dtype) for g in Gs]
    sems, bufs, tok = _xfer_start(list(Gs) + R1s, NCHIP * n, _scatter_plan1(n), name + "_s1_start")
    return dict(sems=sems, bufs=bufs, tok=tok, n=n)


def _scatter_mid(st, after, my_c, name):
    n = st["n"]
    bufs = _xfer_wait(st["sems"], st["bufs"], _scatter_plan1(n), after, name + "_s1_wait")
    Ps = [_pair_sum(bufs[a], bufs[n + a], my_c, f"{name}_pair_sum{a}") for a in range(n)]
    R2s = [lax.empty((3,) + p.shape[1:], p.dtype) for p in Ps]
    sems, bufs, tok = _xfer_start(Ps + R2s, 3 * n, _scatter_plan2(n), name + "_s2_start")
    return dict(sems=sems, bufs=bufs, tok=tok, n=n)


def _scatter_finish(st, after, name):
    n = st["n"]
    bufs = _xfer_wait(st["sems"], st["bufs"], _scatter_plan2(n), after, name + "_s2_wait")
    return bufs[:n], bufs[n:]


SMALL_ROWS = {"norm1_g": (0, 1), "norm2_g": (1, 1), "sgu_ln_g": (2, 1), "sgu_ln_b": (3, 1), "cfm_conv_b": (4, 1),
              "cfm_ln_g": (5, 1), "cfm_ln_b": (6, 1), "b_sgu": (7, 1), "w_sgu": (8, 128), "b_ada": (136, N_MOD),
              "w_short": (142, SHORT_K), "cfm_conv_w": (145, CFM_K)}
ROWS_PER_LAYER = 176
FINAL_ROW = DEPTH * ROWS_PER_LAYER
PACK_ROWS = 360


def _pack(get, D, layers=tuple(range(DEPTH)), tail=True):
    parts = []
    for l in layers:
        for name, (_, nrows) in SMALL_ROWS.items():
            a = get(name, l)
            parts.append(jnp.zeros((nrows * D,), F32) if a is None else a.astype(F32).reshape(nrows * D))
    if tail:
        fin = get("final_g", None)
        parts.append(fin.astype(F32).reshape(D))
        parts.append(jnp.zeros(((PACK_ROWS - FINAL_ROW - 1) * D,), F32))
    return jnp.concatenate(parts).reshape(-1, D)


def _unpack(pack, name, shape):
    D = pack.shape[1]
    r0, nrows = SMALL_ROWS[name]
    return jnp.stack([pack[l * ROWS_PER_LAYER + r0:l * ROWS_PER_LAYER + r0 + nrows] for l in range(DEPTH)]).reshape(shape)


def _mm_tiles(S):
    return min(512, S), min(1024, S)


def kernel(x, c, w_ada, b_ada, norm1_g, w_in, w_short, w_a_out, sgu_ln_g, sgu_ln_b, w_sgu, b_sgu, w_b_out, cfm_conv_w, cfm_conv_b, cfm_ln_g, cfm_ln_b, w_c_out, w_o, norm2_g, w_ffn_in, w_ffn_out, final_g, loss_target, m_w_ada, m_b_ada, m_norm1_g, m_w_in, m_w_short, m_w_a_out, m_sgu_ln_g, m_sgu_ln_b, m_w_sgu, m_b_sgu, m_w_b_out, m_cfm_conv_w, m_cfm_conv_b, m_cfm_ln_g, m_cfm_ln_b, m_w_c_out, m_w_o, m_norm2_g, m_w_ffn_in, m_w_ffn_out, m_final_g, v_w_ada, v_b_ada, v_norm1_g, v_w_in, v_w_short, v_w_a_out, v_sgu_ln_g, v_sgu_ln_b, v_w_sgu, v_b_sgu, v_w_b_out, v_cfm_conv_w, v_cfm_conv_b, v_cfm_ln_g, v_cfm_ln_b, v_w_c_out, v_w_o, v_norm2_g, v_w_ffn_in, v_w_ffn_out, v_final_g):
    W = dict(w_ada=w_ada, b_ada=b_ada, norm1_g=norm1_g, w_in=w_in, w_short=w_short, w_a_out=w_a_out, sgu_ln_g=sgu_ln_g,
             sgu_ln_b=sgu_ln_b, w_sgu=w_sgu, b_sgu=b_sgu, w_b_out=w_b_out, cfm_conv_w=cfm_conv_w, cfm_conv_b=cfm_conv_b,
             cfm_ln_g=cfm_ln_g, cfm_ln_b=cfm_ln_b, w_c_out=w_c_out, w_o=w_o, norm2_g=norm2_g, w_ffn_in=w_ffn_in,
             w_ffn_out=w_ffn_out, final_g=final_g)
    Mo = dict(w_ada=m_w_ada, b_ada=m_b_ada, norm1_g=m_norm1_g, w_in=m_w_in, w_short=m_w_short, w_a_out=m_w_a_out,
              sgu_ln_g=m_sgu_ln_g, sgu_ln_b=m_sgu_ln_b, w_sgu=m_w_sgu, b_sgu=m_b_sgu, w_b_out=m_w_b_out,
              cfm_conv_w=m_cfm_conv_w, cfm_conv_b=m_cfm_conv_b, cfm_ln_g=m_cfm_ln_g, cfm_ln_b=m_cfm_ln_b,
              w_c_out=m_w_c_out, w_o=m_w_o, norm2_g=m_norm2_g, w_ffn_in=m_w_ffn_in, w_ffn_out=m_w_ffn_out,
              final_g=m_final_g)
    Vo = dict(w_ada=v_w_ada, b_ada=v_b_ada, norm1_g=v_norm1_g, w_in=v_w_in, w_short=v_w_short, w_a_out=v_w_a_out,
              sgu_ln_g=v_sgu_ln_g, sgu_ln_b=v_sgu_ln_b, w_sgu=v_w_sgu, b_sgu=v_b_sgu, w_b_out=v_w_b_out,
              cfm_conv_w=v_cfm_conv_w, cfm_conv_b=v_cfm_conv_b, cfm_ln_g=v_cfm_ln_g, cfm_ln_b=v_cfm_ln_b,
              w_c_out=v_w_c_out, w_o=v_w_o, norm2_g=v_norm2_g, w_ffn_in=v_w_ffn_in, w_ffn_out=v_w_ffn_out,
              final_g=v_final_g)
    order = ["w_ada", "b_ada", "norm1_g", "w_in", "w_short", "w_a_out", "sgu_ln_g", "sgu_ln_b", "w_sgu", "b_sgu",
             "w_b_out", "cfm_conv_w", "cfm_conv_b", "cfm_ln_g", "cfm_ln_b", "w_c_out", "w_o", "norm2_g", "w_ffn_in",
             "w_ffn_out", "final_g"]

    assert DEPTH == 2, "the weight-gather schedule below is written for two layers"
    S, D = x.shape[1], x.shape[2]
    F2 = w_ffn_in.shape[2] * NDEV
    FF = F2 // 2
    xi, yi, ci = _place()
    dev = 4 * xi + 2 * yi + ci
    my_c = jnp.reshape(ci, (1,)).astype(jnp.int32)
    my_chip = jnp.reshape(2 * xi + yi, (1,)).astype(jnp.int32)
    tm, tm_big = _mm_tiles(S)
    x0 = x.reshape(S, D)
    tgt = loss_target.reshape(S, D)

    def shards_of(l):
        return [w_in[l].astype(BF16), w_a_out[l].astype(BF16), w_b_out[l].astype(BF16), w_c_out[l].astype(BF16),
                w_o[l].astype(BF16), w_ffn_in[l].astype(BF16), w_ffn_out[l].astype(BF16)]

    c_all = _all_gather([jnp.pad(c, ((0, 7), (0, 0)))], "ag_c")[0][:, 0, :]
    modpart, c_act = _ada_fwd(c_all, w_ada, "ada_fwd")
    ncol = modpart.shape[2]
    mg = _all_gather([modpart.reshape(DEPTH * NDEV, ncol)], "ag_mod")[0].reshape(NDEV, DEPTH, NDEV, ncol)
    mine = lax.dynamic_index_in_dim(mg, dev, axis=2, keepdims=False)
    mod = (jnp.transpose(mine, (1, 0, 2)).reshape(DEPTH, N_MOD * D) + b_ada).reshape(DEPTH, N_MOD, D)

    tril = jnp.tril(jnp.ones((CHUNK, CHUNK), dtype=bool))

    def layer_consts(l):
        wt = jnp.where(tril[None], w_sgu[l], 0.0).astype(BF16)
        return dict(
            wsh=jnp.pad(w_short_full[l], ((0, 8 - SHORT_K), (0, 0))),
            sgu_ln=_rows(sgu_ln_g[l], sgu_ln_b[l]),
            wtril=wt, wtril_t=jnp.swapaxes(wt, 1, 2),
            bias_full=jnp.repeat(b_sgu[l].T, LANE, axis=1),
            cw=jnp.pad(cfm_w_full[l], ((0, HALO - CFM_K), (0, 0))),
            cvec=_rows(cfm_conv_b[l], cfm_ln_g[l], cfm_ln_b[l]))

    ncs = w_short.shape[2]
    sw = _all_gather([w_short.reshape(DEPTH * SHORT_K, ncs), cfm_conv_w.reshape(DEPTH * CFM_K, ncs)], "ag_convw",
                     deps=(mod,))
    w_short_full = jnp.transpose(sw[0], (1, 0, 2)).reshape(DEPTH, SHORT_K, D)
    cfm_w_full = jnp.transpose(sw[1], (1, 0, 2)).reshape(DEPTH, CFM_K, D)

    def rest_of(g):
        return dict(w_a=g[0].reshape(1, D, D), w_b=g[1].reshape(1, D, D), w_c=g[2].reshape(1, D, D),
                    w_o=g[3].reshape(1, D, D), w_fi=jnp.transpose(g[4], (1, 0, 2)).reshape(1, D, F2),
                    w_fo=g[5].reshape(1, FF, D))

    ag_in0 = _gather_start(shards_of(0)[:1], dev, "ag_w_in0", deps=(cfm_w_full,))
    ag_rest0 = _gather_start(shards_of(0)[1:], dev, "ag_rest0", deps=(ag_in0["tok"],))
    ag_in0 = _gather_mid(ag_in0, ag_rest0["tok"], "ag_w_in0")
    Wg = [None, None]
    ag_l1 = None
    nin = w_in.shape[2]
    tn_in = nin if nin % 256 == 0 and nin <= 1280 else 256
    tn_fi = 512 if F2 % 512 == 0 else 256

    saved = []
    xcur, fprev, gprev = x0, None, None
    for l in range(DEPTH):
        sh1, sc1, g1, sh2, sc2, g2 = [mod[l, k] for k in range(N_MOD)]
        cl = layer_consts(l)
        vec1 = _rows(jnp.zeros((D,), F32) if gprev is None else gprev, norm1_g[l], sc1, sh1)
        if l == 0:
            xl, h = _norm_fwd(xcur, fprev, vec1, f"norm1_fwd{l}", deps=(ag_in0["tok"],))
            Wg[0] = dict(w_in=_gather_finish(ag_in0, h, "ag_w_in0")[0])
        else:
            ag_l1 = _gather_mid(ag_l1, fprev, f"ag_w{l}")
            xl, h = _norm_fwd(xcur, fprev, vec1, f"norm1_fwd{l}", deps=(ag_l1["tok"],))
            g = _gather_finish(ag_l1, h, f"ag_w{l}")
            Wg[l] = dict(w_in=g[0], **rest_of(g[1:]))
        wl = Wg[l]
        z = _mm_nn(h, wl["w_in"], BF16, tm_big, tn_in, D, f"mm_in{l}", w_outer=True)
        mix_deps = ()
        if l == 0:
            ag_rest0 = _gather_mid(ag_rest0, z, "ag_rest0")
            mix_deps = (ag_rest0["tok"],)
            if DEPTH > 1:
                ag_l1 = _gather_start(shards_of(1), dev, "ag_w1")
                mix_deps += (ag_l1["tok"],)
        acts = _mixer_fwd(z, cl["wsh"], cl["sgu_ln"], cl["wtril"], cl["bias_full"], cl["cw"], cl["cvec"], f"mixer_fwd{l}",
                          deps=mix_deps)
        if l == 0:
            wl.update(rest_of(_gather_finish(ag_rest0, acts[0], "ag_rest0")))
        merged, ys = _branch_out(acts, [wl["w_a"][0], wl["w_b"][0], wl["w_c"][0]], z, f"branch_out{l}")
        o = _mm_nn(merged, wl["w_o"], F32, tm, D, D, f"mm_o{l}")
        x1, h2 = _norm_fwd(xl, o, _rows(g1, norm2_g[l], sc2, sh2), f"norm2_fwd{l}")
        gu = _mm_nn(h2, wl["w_fi"], BF16, tm_big, tn_fi, D, f"mm_ffn_in{l}")
        act = _swiglu_fwd(gu, f"swiglu_fwd{l}")
        f = _mm_nn(act, wl["w_fo"], F32, tm, D, FF, f"mm_ffn_out{l}")
        saved.append(dict(xl=xl, h=h, z=z, acts=acts, ys=ys, merged=merged, o=o, x1=x1, h2=h2, gu=gu, act=act, f=f,
                          consts=cl, mod=(sh1, sc1, g1, sh2, sc2, g2)))
        xcur, fprev, gprev = x1, f, g2

    last = saved[-1]
    dxup, dfb, fsums, loss_blk = _final_bwd(last["x1"], last["f"], tgt, _rows(last["mod"][5], final_g), "final_bwd")
    loss = _sum_over_devices(loss_blk[0, 0])
    dgate2_next = fsums[1]
    small = [dict() for _ in range(DEPTH)]
    dmods = [None] * DEPTH
    nfi = w_ffn_in.shape[2]
    early_names, late_names = ["w_ffn_out", "w_ffn_in", "w_o"], ["w_a_out", "w_b_out", "w_c_out", "w_in"]
    results = {n: None for n in early_names + late_names}

    def adam_group(names, Ps, R2s, l, deps=()):
        for n, p, r2 in zip(names, Ps, R2s):
            results[n] = _adam_big(p, r2, my_chip, W[n], Mo[n], Vo[n], l, results[n], f"adam_{n}{l}", deps)

    deferred = []
    late_prev = None
    ag_s1, gathered1 = None, None
    tk_w = min(2048, S)
    tn_dw_in = tn_in // 2 if tn_in == 1280 else tn_in
    for l in reversed(range(DEPTH)):
        sv, wl, cl = saved[l], Wg[l], saved[l]["consts"]
        sh1, sc1, g1, sh2, sc2, g2 = sv["mod"]
        dact = _mm_nt(dfb, wl["w_fo"], F32, tm, FF, D, f"mm_dact{l}",
                      deps=() if late_prev is None else (late_prev["tok"], ag_s1["tok"]))
        g_fo = _mm_tn(sv["act"], dfb, 1, FF // 2, D, tk_w, f"mm_dw_ffn_out{l}")
        dgu = _swiglu_bwd(dact, sv["gu"], f"swiglu_bwd{l}")
        dh2 = _mm_nt(dgu, wl["w_fi"], F32, tm, D, F2, f"mm_dh2{l}")
        if late_prev is not None:
            deferred.append((late_names, *_scatter_finish(late_prev, dh2, f"rs_late{l + 1}"), l + 1))
            late_prev = None
        g_fi = _mm_tn(sv["h2"], dgu, 1, D, tn_fi, S, f"mm_dw_ffn_in{l}")
        if ag_s1 is not None:
            ag_s1 = _gather_mid(ag_s1, g_fi, "ag_small1")
        dx1, dob, s2 = _norm_bwd(sv["x1"], dh2, dxup, _rows(norm2_g[l], sc2, g1), sv["o"], f"norm2_bwd{l}",
                                 deps=() if ag_s1 is None else (ag_s1["tok"],))
        dmerged = _mm_nt(dob, wl["w_o"], F32, tm, D, D, f"mm_dmerged{l}")
        g_o = _mm_tn(sv["merged"], dob, 1, D, D, S, f"mm_dw_o{l}")
        early = _scatter_start([g_fo.reshape(NDEV, FF // NDEV, D),
                                jnp.transpose(g_fi.reshape(D, NDEV, nfi), (1, 0, 2)),
                                g_o.reshape(NDEV, D // NDEV, D)], f"rs_early{l}")
        dys, dz = _gate_bwd(dmerged, sv["z"], sv["ys"], f"gate_bwd{l}", deps=(early["tok"],))
        if ag_s1 is not None:
            gathered1 = _gather_finish(ag_s1, dys, "ag_small1")[0]
            ag_s1 = None
        early = _scatter_mid(early, dys, my_c, f"rs_early{l}")
        dacts, g_abc = [], []
        for n, key in enumerate(("w_a", "w_b", "w_c")):
            dacts.append(_mm_nt(dys[n], wl[key], F32, tm, D, D, f"mm_dact_{key}{l}", deps=(early["tok"],) if n == 0 else ()))
            g_abc.append(_mm_tn(sv["acts"][n], dys[n], 1, D, D, S, f"mm_d{key}{l}"))
        dz, mvec, dcw, dws, dbs = _mixer_bwd(sv["z"], dacts, dz, cl["wsh"], cl["sgu_ln"], cl["wtril"], cl["wtril_t"],
                                             cl["bias_full"], cl["cw"], cl["cvec"], f"mixer_bwd{l}")
        dh = _mm_nt(dz, wl["w_in"], F32, tm_big, D, tn_in, f"mm_dh{l}")
        g_in = _mm_tn(sv["h"], dz, NDEV, D, tn_dw_in, S, f"mm_dw_in{l}")
        late = _scatter_start([g.reshape(NDEV, D // NDEV, D) for g in g_abc] + [g_in], f"rs_late{l}")
        if l > 0:
            pv = saved[l - 1]
            dxup, dfb, s1 = _norm_bwd(sv["xl"], dh, dx1, _rows(norm1_g[l], sc1, pv["mod"][5]), pv["f"], f"norm1_bwd{l}",
                                      deps=(late["tok"],))
        else:
            dxup, dfb, s1 = _norm_bwd(sv["xl"], dh, dx1, _rows(norm1_g[l], sc1), None, f"norm1_bwd{l}", deps=(late["tok"],))
        deferred.append((early_names, *_scatter_finish(early, dxup, f"rs_early{l}"), l))
        dmods[l] = jnp.stack([s1[0], s1[1], s2[3], s2[0], s2[1], dgate2_next])
        dgate2_next = s1[3]
        small[l] = dict(norm1_g=s1[2], norm2_g=s2[2], sgu_ln_g=mvec[3], sgu_ln_b=mvec[4], cfm_conv_b=mvec[5],
                        cfm_ln_g=mvec[6], cfm_ln_b=mvec[7], b_sgu=dbs[:, :, 0],
                        w_sgu=jnp.where(tril[None], dws, 0.0), b_ada=dmods[l], w_short=mvec[0:SHORT_K],
                        cfm_conv_w=dcw[0:CFM_K])
        small_get = lambda name, k: fsums[0] if name == "final_g" else small[k][name]
        if l > 0:
            late_prev = _scatter_mid(late, dxup, my_c, f"rs_late{l}")
            ag_s1 = _gather_start([_pack(small_get, D, layers=(l,), tail=True)], dev, "ag_small1", deps=(late_prev["tok"],))
    grad_x = dxup.reshape(x.shape)

    gathered0 = _all_gather([_pack(small_get, D, layers=(0,), tail=False)], "ag_small0", deps=(dxup,))[0]
    late_prev = _scatter_mid(late, gathered0, my_c, "rs_late0")
    gathered = jnp.concatenate([gathered0, gathered1], axis=1)
    sharded_small = ("w_short", "cfm_conv_w")
    packs = [_pack(lambda name, l, T=T: T["final_g"] if name == "final_g" else (None if name in sharded_small else T[name][l]), D)
             for T in (W, Mo, Vo)]
    sg, sd, sm, sv_ = _adam_small(gathered, *packs, name="adam_small", deps=(late_prev["tok"],))
    out = {}
    for name in order:
        if name in SMALL_ROWS and name not in sharded_small:
            out[name] = tuple(_unpack(p, name, W[name].shape) for p in (sg, sd, sm, sv_))
    out["final_g"] = tuple(p[FINAL_ROW] for p in (sg, sd, sm, sv_))

    def my_cols(name):
        full = _unpack(sg, name, (DEPTH, SMALL_ROWS[name][1], D))
        return lax.dynamic_slice_in_dim(full, dev * ncs, ncs, axis=2)

    gcs = jnp.concatenate([my_cols("w_short").reshape(-1, ncs), my_cols("cfm_conv_w").reshape(-1, ncs)])
    ncr = gcs.shape[0]
    padr = (-ncr) % 8
    cat = lambda T: jnp.pad(jnp.concatenate([T["w_short"].reshape(-1, ncs), T["cfm_conv_w"].reshape(-1, ncs)]), ((0, padr), (0, 0)))
    cd, cm, cv = _adam_plain(jnp.pad(gcs, ((0, padr), (0, 0))), cat(W), cat(Mo), cat(Vo), "adam_convw")
    nsh = DEPTH * SHORT_K
    out["w_short"] = tuple(a[0:nsh].reshape(w_short.shape) for a in (gcs, cd, cm, cv))
    out["cfm_conv_w"] = tuple(a[nsh:ncr].reshape(cfm_conv_w.shape) for a in (gcs, cd, cm, cv))

    dm_all = jnp.stack([gathered[:, l * ROWS_PER_LAYER + 136:l * ROWS_PER_LAYER + 136 + N_MOD, :].reshape(NDEV, N_MOD * D)
                        for l in range(DEPTH)])
    dm_mine = lax.dynamic_slice_in_dim(dm_all, dev * ncol, ncol, axis=2)
    out["w_ada"] = tuple(_adam_ada(jnp.transpose(c_act), dm_mine, w_ada, m_w_ada, v_w_ada, "adam_ada"))

    for names, Ps, R2s, l in deferred:
        adam_group(names, Ps, R2s, l, deps=(late_prev["tok"],))
    adam_group(late_names, *_scatter_finish(late_prev, results["w_o"][0], "rs_late0"), 0)
    for n in early_names + late_names:
        out[n] = tuple(results[n])

    grads = [out[n][0] for n in order]
    deltas = [out[n][1] for n in order]
    new_m = [out[n][2] for n in order]
    new_v = [out[n][3] for n in order]
    return (loss, grad_x, *grads, *deltas, *new_m, *new_v)
```

```python
import functools
import math

import jax
import jax.numpy as jnp
from jax import lax
from jax.experimental import pallas as pl
from jax.experimental.pallas import tpu as pltpu

F32, BF16 = jnp.float32, jnp.bfloat16
NDEV = 8
NCHIP = NDEV // 2
DEPTH = 2
EPS = 1e-6
CHUNK = 128
NG = 8
SHORT_K = 3
CFM_K = 31
HALO = 32
N_MOD = 6
LANE = 128
VMEM_LIMIT = 56 * 1024 * 1024
ADAM_LR, ADAM_B1, ADAM_B2, ADAM_EPS, ADAM_WD, ADAM_STEP = 0.001, 0.9, 0.999, 1e-08, 0.01, 10
_G0 = math.sqrt(2.0 / math.pi)
_G1 = 0.044715
MESH = pl.DeviceIdType.MESH
ANY = pl.BlockSpec(memory_space=pl.ANY)


def _pcall(body, **kw):
    return pl.pallas_call(body, **kw)


def _params(sem=None):
    return pltpu.CompilerParams(dimension_semantics=sem, vmem_limit_bytes=VMEM_LIMIT)


def _sds(shape, dtype):
    return jax.ShapeDtypeStruct(tuple(shape), dtype)


def _mm_body(dims, nk, out_f32):
    def body(a_ref, b_ref, o_ref, *scr):
        k = pl.program_id(2)
        part = lax.dot_general(a_ref[...], b_ref[...], dims, preferred_element_type=F32)
        if nk == 1:
            o_ref[...] = part.reshape(o_ref.shape).astype(o_ref.dtype)
        elif out_f32:
            @pl.when(k == 0)
            def _():
                o_ref[...] = part.reshape(o_ref.shape)

            @pl.when(k > 0)
            def _():
                o_ref[...] += part.reshape(o_ref.shape)
        else:
            acc = scr[0]

            @pl.when(k == 0)
            def _():
                acc[...] = part

            @pl.when(k > 0)
            def _():
                acc[...] += part

            @pl.when(k == nk - 1)
            def _():
                o_ref[...] = acc[...].astype(o_ref.dtype)
    return body


def _after(body, n_in, deps):
    nd = len(deps)
    if nd == 0:
        return body

    def ordered(*refs):
        return body(*refs[:n_in], *refs[n_in + nd:])
    return ordered


def _mm_call(body, grid, in_specs, out_spec, out_shape, acc_shape, name, deps=()):
    scratch = [] if acc_shape is None else [pltpu.VMEM(acc_shape, F32)]
    return _pcall(_after(body, 2, deps), grid=grid, in_specs=in_specs + [ANY] * len(deps), out_specs=out_spec,
                  out_shape=out_shape, scratch_shapes=scratch, name=name,
                  compiler_params=_params(("parallel", "parallel", "arbitrary")))


def _mm_nn(a, b3, out_dtype, tm, tn, tk, name, w_outer=False, deps=()):
    M, K = a.shape
    G, _, Nb = b3.shape
    npb, nk = Nb // tn, K // tk
    out_f32 = out_dtype == F32
    body = _mm_body((((1,), (0,)), ((), ())), nk, out_f32)
    if w_outer:
        grid = (G * npb, M // tm, nk)
        ij = lambda p, q: (q, p)
    else:
        grid = (M // tm, G * npb, nk)
        ij = lambda p, q: (p, q)

    def a_map(p, q, k):
        i, j = ij(p, q)
        return (i, k)

    def b_map(p, q, k):
        i, j = ij(p, q)
        return (j // npb, k, j % npb)

    def o_map(p, q, k):
        return ij(p, q)

    def wrapped(a_ref, b_ref, o_ref, *scr):
        body(a_ref, b_ref, o_ref, *scr)

    return _mm_call(wrapped, grid, [pl.BlockSpec((tm, tk), a_map), pl.BlockSpec((None, tk, tn), b_map)],
                    pl.BlockSpec((tm, tn), o_map), _sds((M, G * Nb), out_dtype),
                    None if (nk == 1 or out_f32) else (tm, tn), name, deps)(a, b3, *deps)


def _mm_nt(a, b3, out_dtype, tm, tn, tk, name, deps=()):
    M, _ = a.shape
    G, Ko, Nb = b3.shape
    kpb = Nb // tk
    nk = G * kpb
    out_f32 = out_dtype == F32
    body = _mm_body((((1,), (1,)), ((), ())), nk, out_f32)

    def wrapped(a_ref, b_ref, o_ref, *scr):
        body(a_ref, b_ref, o_ref, *scr)

    return _mm_call(wrapped, (M // tm, Ko // tn, nk),
                    [pl.BlockSpec((tm, tk), lambda i, j, k: (i, k)),
                     pl.BlockSpec((None, tn, tk), lambda i, j, k: (k // kpb, j, k % kpb))],
                    pl.BlockSpec((tm, tn), lambda i, j, k: (i, j)), _sds((M, Ko), out_dtype),
                    None if (nk == 1 or out_f32) else (tm, tn), name, deps)(a, b3, *deps)


def _mm_tn(a, b, G, tm, tn, tk, name, deps=()):
    T, M = a.shape
    Nb = b.shape[1] // G
    npb, nk = Nb // tn, T // tk
    body = _mm_body((((0,), (0,)), ((), ())), nk, False)

    def wrapped(a_ref, b_ref, o_ref, *scr):
        body(a_ref, b_ref, o_ref, *scr)

    in_specs = [pl.BlockSpec((tk, tm), lambda i, j, k: (k, i)), pl.BlockSpec((tk, tn), lambda i, j, k: (k, j))]
    out_spec = pl.BlockSpec((None, tm, tn), lambda i, j, k: (j // npb, i, j % npb))
    return _mm_call(wrapped, (M // tm, G * npb, nk), in_specs, out_spec, _sds((G, M, Nb), BF16),
                    None if nk == 1 else (tm, tn), name, deps)(a, b, *deps)


def _rsum(v):
    return jnp.sum(v, axis=0, keepdims=True)


def _rmean(v):
    return jnp.mean(v, axis=-1, keepdims=True)


def _gelu(x):
    t = jnp.tanh(_G0 * (x + _G1 * (x * x * x)))
    return x * (0.5 * (1.0 + t)), t


def _dgelu(x, t):
    return 0.5 * (1.0 + t) + 0.5 * x * (1.0 - t * t) * (_G0 * (1.0 + 3.0 * _G1 * (x * x)))


def _sigmoid(x):
    return 1.0 / (1.0 + jnp.exp(-x))


def _fill_shifted(ext, rot):
    v = ext[...]
    n = v.shape[0]
    for b in range(1, 8):
        rot[b - 1] = pltpu.roll(v, n - b, 0)


def _rows_at(ext, rot, s, tm, cs=slice(None)):
    a, b = divmod(s, 8)
    return ext[8 * a:8 * a + tm, cs] if b == 0 else rot[b - 1, 8 * a:8 * a + tm, cs]


def _causal_conv(w_ref, taps, bias, ext, rot, offset, tm, out):
    D = out.shape[1]
    for cb in range(D // LANE):
        cs = slice(cb * LANE, (cb + 1) * LANE)
        acc = None
        for k, o in zip(taps, offset):
            term = w_ref[k:k + 1, cs] * _rows_at(ext, rot, o, tm, cs)
            acc = term if acc is None else acc + term
        out[:, cs] = acc if bias is None else acc + bias[:, cs]


def _rows(*vs):
    a = jnp.stack([v.astype(F32) for v in vs])
    return jnp.pad(a, ((0, 8 - len(vs)), (0, 0)))


def _row_spec(tm, D):
    return pl.BlockSpec((tm, D), lambda i: (i, 0))


def _const_spec(shape):
    nd = len(shape)
    return pl.BlockSpec(shape, lambda i: (0,) * nd)


def _norm_fwd(xp, f, vec, name, deps=()):
    S, D = xp.shape
    tm = min(256, S)
    has_f = f is not None

    def body(*refs):
        if has_f:
            xp_ref, f_ref, vec_ref, xo_ref, h_ref = refs
            x = xp_ref[...] + vec_ref[0:1, :] * f_ref[...]
            xo_ref[...] = x
        else:
            xp_ref, vec_ref, h_ref = refs
            x = xp_ref[...]
        r = lax.rsqrt(_rmean(x * x) + EPS)
        h = (x * r) * vec_ref[1:2, :]
        h_ref[...] = (h * (1.0 + vec_ref[2:3, :]) + vec_ref[3:4, :]).astype(BF16)

    rs = _row_spec(tm, D)
    ins = [xp, f, vec] if has_f else [xp, vec]
    in_specs = ([rs, rs] if has_f else [rs]) + [_const_spec((8, D))]
    out_shape = ([_sds((S, D), F32)] if has_f else []) + [_sds((S, D), BF16)]
    out_specs = [rs] * len(out_shape)
    outs = _pcall(_after(body, len(ins), deps), grid=(S // tm,), in_specs=in_specs + [ANY] * len(deps),
                  out_specs=out_specs, out_shape=out_shape, name=name,
                  compiler_params=_params(("parallel",)))(*ins, *deps)
    return (outs[0], outs[1]) if has_f else (xp, outs[0])


def _mixer_fwd(z, wsh, sgu_ln, wtril, bias_full, cw, cvec, name, deps=()):
    S = z.shape[0]
    D = wsh.shape[1]
    tm = CHUNK

    def body(z_ref, wsh_ref, sln_ref, wt_ref, bias_ref, cw_ref, cv_ref, oa_ref, ob_ref, oc_ref, pe, ge, gr, cbuf):
        i = pl.program_id(0)

        @pl.when(i == 0)
        def _():
            pe[0:HALO, :] = jnp.zeros((HALO, D), F32)
            ge[0:HALO, :] = jnp.zeros((HALO, D), F32)

        def col(n):
            return z_ref[:, n * D:(n + 1) * D].astype(F32)

        pe[HALO:HALO + tm, :] = col(1) * col(2)
        q = wsh_ref[0:1, :] * pe[HALO - 2:HALO - 2 + tm, :]
        q = q + wsh_ref[1:2, :] * pe[HALO - 1:HALO - 1 + tm, :]
        q = q + wsh_ref[2:3, :] * pe[HALO:HALO + tm, :]
        oa_ref[...] = (col(0) * q).astype(BF16)
        gu, _ = _gelu(col(3))
        gv, _ = _gelu(col(4))
        d = gv - _rmean(gv)
        nrm = d * lax.rsqrt(_rmean(d * d) + EPS)
        vnb = (nrm * sln_ref[0:1, :] + sln_ref[1:2, :]).astype(BF16)
        for g in range(NG):
            cs = slice(g * LANE, (g + 1) * LANE)
            mixed = jnp.dot(wt_ref[g], vnb[:, cs], preferred_element_type=F32) + bias_ref[:, cs]
            ob_ref[:, cs] = (gu[:, cs] * mixed).astype(BF16)
        ge[HALO:HALO + tm, :] = col(5) * _sigmoid(col(6))
        _fill_shifted(ge, gr)
        o0 = HALO - (CFM_K - 1)
        _causal_conv(cw_ref, range(CFM_K), cv_ref[0:1, :], ge, gr, range(o0, o0 + CFM_K), tm, cbuf)
        conv = cbuf[...]
        d = conv - _rmean(conv)
        ln = (d * lax.rsqrt(_rmean(d * d) + EPS)) * cv_ref[1:2, :] + cv_ref[2:3, :]
        oc_ref[...] = (ln * _sigmoid(ln)).astype(BF16)
        pe[0:HALO, :] = pe[tm:tm + HALO, :]
        ge[0:HALO, :] = ge[tm:tm + HALO, :]

    rs = _row_spec(tm, D)
    return _pcall(
        _after(body, 7, deps), grid=(S // tm,),
        in_specs=[pl.BlockSpec((tm, 7 * D), lambda i: (i, 0)), _const_spec((8, D)), _const_spec((8, D)),
                  _const_spec((NG, CHUNK, CHUNK)), _const_spec((CHUNK, D)), _const_spec((HALO, D)), _const_spec((8, D))]
        + [ANY] * len(deps),
        out_specs=[rs, rs, rs], out_shape=[_sds((S, D), BF16)] * 3,
        scratch_shapes=[pltpu.VMEM((HALO + tm, D), F32), pltpu.VMEM((HALO + tm, D), F32),
                        pltpu.VMEM((7, HALO + tm, D), F32), pltpu.VMEM((tm, D), F32)],
        name=name, compiler_params=_params(("arbitrary",)))(z, wsh, sgu_ln, wtril, bias_full, cw, cvec, *deps)


def _branch_out(acts, ws, z, name):
    S, D = acts[0].shape
    tm = min(256, S)

    def body(a0, a1, a2, w0, w1, w2, g0, g1, g2, m_ref, y_ref):
        m = None
        for n, (a, w, g) in enumerate(((a0, w0, g0), (a1, w1, g1), (a2, w2, g2))):
            y = jnp.dot(a[...], w[...], preferred_element_type=F32)
            y_ref[n] = y.astype(BF16)
            t = _sigmoid(g[...].astype(F32)) * y
            m = t if m is None else m + t
        m_ref[...] = m.astype(BF16)

    rs = _row_spec(tm, D)
    gate_specs = [pl.BlockSpec((tm, D), functools.partial(lambda i, n: (i, 7 + n), n=n)) for n in range(3)]
    return _pcall(body, grid=(S // tm,),
                  in_specs=[rs, rs, rs] + [_const_spec((D, D))] * 3 + gate_specs,
                  out_specs=[rs, pl.BlockSpec((3, tm, D), lambda i: (0, i, 0))],
                  out_shape=[_sds((S, D), BF16), _sds((3, S, D), BF16)], name=name,
                  compiler_params=_params(("parallel",)))(*acts, *ws, z, z, z)


def _swiglu_fwd(gu, name):
    S, F2 = gu.shape
    F = F2 // 2
    tm = min(256, S)

    def body(g_ref, u_ref, o_ref):
        g = g_ref[...].astype(F32)
        o_ref[...] = ((g * _sigmoid(g)) * u_ref[...].astype(F32)).astype(BF16)

    return _pcall(body, grid=(S // tm,),
                  in_specs=[pl.BlockSpec((tm, F), lambda i: (i, 0)), pl.BlockSpec((tm, F), lambda i: (i, 1))],
                  out_specs=pl.BlockSpec((tm, F), lambda i: (i, 0)), out_shape=_sds((S, F), BF16), name=name,
                  compiler_params=_params(("parallel",)))(gu, gu)


def _swiglu_bwd(dact, gu, name):
    S, F2 = gu.shape
    F = F2 // 2
    tm = min(128, S)

    def body(d_ref, g_ref, u_ref, o_ref):
        g = g_ref[...].astype(F32)
        sg = _sigmoid(g)
        d = d_ref[...].astype(F32)
        o_ref[:, 0:F] = (d * u_ref[...].astype(F32) * (sg * (1.0 + g * (1.0 - sg)))).astype(BF16)
        o_ref[:, F:2 * F] = (d * (g * sg)).astype(BF16)

    return _pcall(body, grid=(S // tm,),
                  in_specs=[pl.BlockSpec((tm, F), lambda i: (i, 0)), pl.BlockSpec((tm, F), lambda i: (i, 0)),
                            pl.BlockSpec((tm, F), lambda i: (i, 1))],
                  out_specs=pl.BlockSpec((tm, F2), lambda i: (i, 0)), out_shape=_sds((S, F2), BF16), name=name,
                  compiler_params=_params(("parallel",)))(dact, gu, gu)


def _final_bwd(x1, f, tgt, vec, name):
    S, D = x1.shape
    tm = min(256, S)

    def body(x_ref, f_ref, t_ref, vec_ref, dx_ref, df_ref, sums_ref, loss_ref):
        @pl.when(pl.program_id(0) == 0)
        def _():
            sums_ref[...] = jnp.zeros_like(sums_ref)
            loss_ref[...] = jnp.zeros_like(loss_ref)

        gate, fg = vec_ref[0:1, :], vec_ref[1:2, :]
        fv = f_ref[...]
        x = x_ref[...] + gate * fv
        r = lax.rsqrt(_rmean(x * x) + EPS)
        xn = x * r
        diff = xn * fg - t_ref[...]
        per_tok = _rmean(diff * diff)
        loss_ref[...] += 0.5 * jnp.sum(per_tok, axis=0, keepdims=True)
        dy = diff * (1.0 / D)
        sums_ref[0:1, :] += _rsum(dy * xn)
        dxn = dy * fg
        dx = r * (dxn - xn * _rmean(dxn * xn))
        sums_ref[1:2, :] += _rsum(dx * fv)
        dx_ref[...] = dx
        df_ref[...] = (dx * gate).astype(BF16)

    rs = _row_spec(tm, D)
    return _pcall(body, grid=(S // tm,), in_specs=[rs, rs, rs, _const_spec((8, D))],
                  out_specs=[rs, rs, _const_spec((8, D)), _const_spec((8, LANE))],
                  out_shape=[_sds((S, D), F32), _sds((S, D), BF16), _sds((8, D), F32), _sds((8, LANE), F32)],
                  name=name, compiler_params=_params(("arbitrary",)))(x1, f, tgt, vec)


def _norm_bwd(xin, dh, dxup, vec, fprev, name, deps=()):
    S, D = xin.shape
    tm = min(256, S)
    has_prev = fprev is not None

    def body(*refs):
        if has_prev:
            x_ref, dh_ref, up_ref, vec_ref, fp_ref, dx_ref, dp_ref, sums_ref = refs
        else:
            x_ref, dh_ref, up_ref, vec_ref, dx_ref, sums_ref = refs

        @pl.when(pl.program_id(0) == 0)
        def _():
            sums_ref[...] = jnp.zeros_like(sums_ref)

        g, scale = vec_ref[0:1, :], vec_ref[1:2, :]
        x = x_ref[...]
        r = lax.rsqrt(_rmean(x * x) + EPS)
        xn = x * r
        dhv = dh_ref[...]
        sums_ref[0:1, :] += _rsum(dhv)
        sums_ref[1:2, :] += _rsum(dhv * (xn * g))
        dm = dhv * (1.0 + scale)
        sums_ref[2:3, :] += _rsum(dm * xn)
        dxn = dm * g
        dx = up_ref[...] + r * (dxn - xn * _rmean(dxn * xn))
        dx_ref[...] = dx
        if has_prev:
            sums_ref[3:4, :] += _rsum(dx * fp_ref[...])
            dp_ref[...] = (dx * vec_ref[2:3, :]).astype(BF16)

    rs = _row_spec(tm, D)
    ins = [xin, dh, dxup, vec] + ([fprev] if has_prev else [])
    in_specs = [rs, rs, rs, _const_spec((8, D))] + ([rs] if has_prev else [])
    out_shape = [_sds((S, D), F32)] + ([_sds((S, D), BF16)] if has_prev else []) + [_sds((8, D), F32)]
    out_specs = [rs] + ([rs] if has_prev else []) + [_const_spec((8, D))]
    outs = _pcall(_after(body, len(ins), deps), grid=(S // tm,), in_specs=in_specs + [ANY] * len(deps),
                  out_specs=out_specs, out_shape=out_shape, name=name,
                  compiler_params=_params(("arbitrary",)))(*ins, *deps)
    return (outs[0], outs[1], outs[2]) if has_prev else (outs[0], None, outs[1])


def _gate_bwd(dmerged, z, ys, name, deps=()):
    S, D = dmerged.shape
    tm = min(256, S)
    ncol = z.shape[1] // D

    def body(dm_ref, g_ref, y_ref, dya_ref, dyb_ref, dyc_ref, dz_ref):
        n = pl.program_id(1)
        sg = _sigmoid(g_ref[...].astype(F32))
        dm = dm_ref[...].astype(F32)
        dy = (dm * sg).astype(BF16)
        for k, ref in enumerate((dya_ref, dyb_ref, dyc_ref)):
            @pl.when(n == k)
            def _(ref=ref):
                ref[...] = dy
        dz_ref[...] = (dm * y_ref[...].astype(F32) * (sg * (1.0 - sg))).astype(BF16)

    row = pl.BlockSpec((tm, D), lambda i, n: (i, 0))
    outs = _pcall(_after(body, 3, deps), grid=(S // tm, 3),
                  in_specs=[row, pl.BlockSpec((tm, D), lambda i, n: (i, 7 + n)),
                            pl.BlockSpec((None, tm, D), lambda i, n: (n, i, 0))] + [ANY] * len(deps),
                  out_specs=[row, row, row, pl.BlockSpec((tm, D), lambda i, n: (i, 7 + n))],
                  out_shape=[_sds((S, D), BF16)] * 3 + [_sds((S, ncol * D), BF16)], name=name,
                  compiler_params=_params(("parallel", "arbitrary")))(dmerged, z, ys, *deps)
    return outs[:3], outs[3]


def _mixer_bwd(z, dacts, dz, wsh, sgu_ln, wtril, wtril_t, bias_full, cw, cvec, name):
    S = z.shape[0]
    D = wsh.shape[1]
    tm = CHUNK
    nt = S // tm
    hb = tm // HALO

    def body(zc, zp, da_ref, db_ref, dc_ref, wsh_ref, sln_ref, wt_ref, wtt_ref, bias_ref, cw_ref, cv_ref, _dz_in,
             dz_ref, vec_ref, dcw_ref, dws_ref, dbs_ref, pe, ge, dqe, dce, gr, dcr, cbuf, dcw8):
        i = pl.program_id(0)
        rb = nt - 1 - i

        @pl.when(i == 0)
        def _():
            vec_ref[...] = jnp.zeros_like(vec_ref)
            dcw8[...] = jnp.zeros_like(dcw8)
            dws_ref[...] = jnp.zeros_like(dws_ref)
            dbs_ref[...] = jnp.zeros_like(dbs_ref)
            dqe[tm:tm + HALO, :] = jnp.zeros((HALO, D), F32)
            dce[tm:tm + HALO, :] = jnp.zeros((HALO, D), F32)

        keep = (rb > 0).astype(F32)

        def col(n):
            return zc[:, n * D:(n + 1) * D].astype(F32)

        def pcol(n):
            return zp[:, n * D:(n + 1) * D].astype(F32)

        c_a, x_a = col(1), col(2)
        pe[0:HALO, :] = keep * (pcol(1) * pcol(2))
        pe[HALO:HALO + tm, :] = c_a * x_a
        q = wsh_ref[0:1, :] * pe[HALO - 2:HALO - 2 + tm, :]
        q = q + wsh_ref[1:2, :] * pe[HALO - 1:HALO - 1 + tm, :]
        q = q + wsh_ref[2:3, :] * pe[HALO:HALO + tm, :]
        dact = da_ref[...].astype(F32)
        dz_ref[:, 0:D] = (dact * q).astype(BF16)
        dq = dact * col(0)
        dqe[0:tm, :] = dq
        dp = wsh_ref[2:3, :] * dq + wsh_ref[1:2, :] * dqe[1:1 + tm, :] + wsh_ref[0:1, :] * dqe[2:2 + tm, :]
        dz_ref[:, D:2 * D] = (dp * x_a).astype(BF16)
        dz_ref[:, 2 * D:3 * D] = (dp * c_a).astype(BF16)
        for k in range(SHORT_K):
            o = HALO - (SHORT_K - 1) + k
            vec_ref[k:k + 1, :] += _rsum(dq * pe[o:o + tm, :])
        u, v = col(3), col(4)
        gu, tu = _gelu(u)
        gv, tv = _gelu(v)
        d = gv - _rmean(gv)
        rstd = lax.rsqrt(_rmean(d * d) + EPS)
        nrm = d * rstd
        vnb = (nrm * sln_ref[0:1, :] + sln_ref[1:2, :]).astype(BF16)
        dact = db_ref[...].astype(F32)
        dvn_parts, dgu_parts = [], []
        for g in range(NG):
            cs = slice(g * LANE, (g + 1) * LANE)
            vg = vnb[:, cs]
            mixed = jnp.dot(wt_ref[g], vg, preferred_element_type=F32) + bias_ref[:, cs]
            dgu_parts.append(dact[:, cs] * mixed)
            dmixed = dact[:, cs] * gu[:, cs]
            dmb = dmixed.astype(BF16)
            dws_ref[g] += lax.dot_general(dmb, vg, (((1,), (1,)), ((), ())), preferred_element_type=F32)
            dbs_ref[g] += jnp.broadcast_to(jnp.sum(dmixed, axis=1, keepdims=True), (CHUNK, LANE))
            dvn_parts.append(jnp.dot(wtt_ref[g], dmb, preferred_element_type=F32))
        dgu = jnp.concatenate(dgu_parts, axis=1)
        dvn = jnp.concatenate(dvn_parts, axis=1)
        dz_ref[:, 3 * D:4 * D] = (dgu * _dgelu(u, tu)).astype(BF16)
        vec_ref[3:4, :] += _rsum(dvn * nrm)
        vec_ref[4:5, :] += _rsum(dvn)
        dn = dvn * sln_ref[0:1, :]
        dgv = rstd * (dn - _rmean(dn) - nrm * _rmean(dn * nrm))
        dz_ref[:, 4 * D:5 * D] = (dgv * _dgelu(v, tv)).astype(BF16)
        a_c = col(5)
        sg = _sigmoid(col(6))
        ge[0:HALO, :] = keep * (pcol(5) * _sigmoid(pcol(6)))
        ge[HALO:HALO + tm, :] = a_c * sg
        _fill_shifted(ge, gr)
        o0 = HALO - (CFM_K - 1)
        _causal_conv(cw_ref, range(CFM_K), cv_ref[0:1, :], ge, gr, range(o0, o0 + CFM_K), tm, cbuf)
        conv = cbuf[...]
        d = conv - _rmean(conv)
        rstd = lax.rsqrt(_rmean(d * d) + EPS)
        nrm = d * rstd
        ln = nrm * cv_ref[1:2, :] + cv_ref[2:3, :]
        sl = _sigmoid(ln)
        dln = dc_ref[...].astype(F32) * (sl * (1.0 + ln * (1.0 - sl)))
        vec_ref[6:7, :] += _rsum(dln * nrm)
        vec_ref[7:8, :] += _rsum(dln)
        dn = dln * cv_ref[1:2, :]
        dconv = rstd * (dn - _rmean(dn) - nrm * _rmean(dn * nrm))
        vec_ref[5:6, :] += _rsum(dconv)
        dce[0:tm, :] = dconv
        _fill_shifted(dce, dcr)
        _causal_conv(cw_ref, range(CFM_K), None, dce, dcr, [CFM_K - 1 - k for k in range(CFM_K)], tm, cbuf)
        dglu = cbuf[...]
        for cb in range(D // LANE):
            cs = slice(cb * LANE, (cb + 1) * LANE)
            dcv = dce[0:tm, cs]
            for k in range(CFM_K):
                prod = dcv * _rows_at(ge, gr, o0 + k, tm, cs)
                dcw8[k, :, cs] += jnp.sum(prod.reshape(tm // 8, 8, LANE), axis=0)

        @pl.when(i == nt - 1)
        def _():
            dcw_ref[...] = jnp.sum(dcw8[...], axis=1)
        dz_ref[:, 5 * D:6 * D] = (dglu * sg).astype(BF16)
        dz_ref[:, 6 * D:7 * D] = (dglu * a_c * (sg * (1.0 - sg))).astype(BF16)
        dqe[tm:tm + HALO, :] = dqe[0:HALO, :]
        dce[tm:tm + HALO, :] = dce[0:HALO, :]

    rev = lambda i: (nt - 1 - i, 0)
    rs = pl.BlockSpec((tm, D), rev)
    cur = pl.BlockSpec((tm, 7 * D), rev)
    prev = pl.BlockSpec((HALO, 7 * D), lambda i: (jnp.maximum((nt - 1 - i) * hb - 1, 0), 0))
    ext = pltpu.VMEM((HALO + tm, D), F32)
    outs = _pcall(
        body, grid=(nt,),
        in_specs=[cur, prev, rs, rs, rs, _const_spec((8, D)), _const_spec((8, D)), _const_spec((NG, CHUNK, CHUNK)),
                  _const_spec((NG, CHUNK, CHUNK)), _const_spec((CHUNK, D)), _const_spec((HALO, D)), _const_spec((8, D)),
                  ANY],
        out_specs=[cur, _const_spec((8, D)), _const_spec((HALO, D)), _const_spec((NG, CHUNK, CHUNK)),
                   _const_spec((NG, CHUNK, LANE))],
        out_shape=[_sds(dz.shape, BF16), _sds((8, D), F32), _sds((HALO, D), F32), _sds((NG, CHUNK, CHUNK), F32),
                   _sds((NG, CHUNK, LANE), F32)],
        scratch_shapes=[ext, ext, ext, ext, pltpu.VMEM((7, HALO + tm, D), F32), pltpu.VMEM((7, HALO + tm, D), F32),
                        pltpu.VMEM((tm, D), F32), pltpu.VMEM((HALO, 8, D), F32)],
        input_output_aliases={12: 0}, name=name,
        compiler_params=_params(("arbitrary",)))(z, z, *dacts, wsh, sgu_ln, wtril, wtril_t, bias_full, cw, cvec, dz)
    return outs


def _ada_fwd(c_all, w_ada_loc, name):
    nb, D = c_all.shape
    L, _, nc = w_ada_loc.shape

    def body(c_ref, w_ref, o_ref, ca_ref):
        cv = c_ref[...]
        ca = cv * _sigmoid(cv)
        ca_ref[...] = ca
        o_ref[...] = jnp.dot(ca.astype(BF16), w_ref[...].astype(BF16), preferred_element_type=F32)

    return _pcall(body, grid=(L,),
                  in_specs=[_const_spec((nb, D)), pl.BlockSpec((None, D, nc), lambda l: (l, 0, 0))],
                  out_specs=[pl.BlockSpec((None, nb, nc), lambda l: (l, 0, 0)), _const_spec((nb, D))],
                  out_shape=[_sds((L, nb, nc), F32), _sds((nb, D), F32)], name=name,
                  compiler_params=_params(("arbitrary",)))(c_all, w_ada_loc)


def _adamw(w, g, m, v):
    m = ADAM_B1 * m + (1.0 - ADAM_B1) * g
    v = ADAM_B2 * v + (1.0 - ADAM_B2) * (g * g)
    m_hat = m / (1.0 - ADAM_B1 ** ADAM_STEP)
    v_hat = v / (1.0 - ADAM_B2 ** ADAM_STEP)
    delta = -ADAM_LR * (m_hat / (jnp.sqrt(v_hat) + ADAM_EPS) + ADAM_WD * w)
    return delta, m, v


def _tile_rows(R, C, align=8):
    cap = max(align, (1536 * 1024) // (4 * C))
    best = None
    for t in range(align, R + 1, align):
        if R % t == 0 and t <= cap:
            best = t
    return R if best is None else best


def _adam_ada(ct, dm, w, m, v, name):
    L, D, nc = w.shape
    nb = ct.shape[1]
    tr = _tile_rows(D, nc)

    def body(ct_ref, dm_ref, w_ref, m_ref, v_ref, g_ref, d_ref, mo_ref, vo_ref):
        g = ct_ref[:, 0:1] * dm_ref[0:1, :]
        for b in range(1, nb):
            g = g + ct_ref[:, b:b + 1] * dm_ref[b:b + 1, :]
        g_ref[...] = g
        d_ref[...], mo_ref[...], vo_ref[...] = _adamw(w_ref[...], g, m_ref[...], v_ref[...])

    ws = pl.BlockSpec((None, tr, nc), lambda l, r: (l, r, 0))
    return _pcall(body, grid=(L, D // tr),
                  in_specs=[pl.BlockSpec((tr, nb), lambda l, r: (r, 0)), pl.BlockSpec((None, nb, nc), lambda l, r: (l, 0, 0)),
                            ws, ws, ws],
                  out_specs=[ws] * 4, out_shape=[_sds(w.shape, F32)] * 4, name=name,
                  compiler_params=_params(("parallel", "parallel")))(ct, dm, w, m, v)


def _adam_small(parts, w, m, v, name, deps=()):
    n, R, C = parts.shape
    tr = _tile_rows(R, C * n // 2)

    def body(p_ref, w_ref, m_ref, v_ref, g_ref, d_ref, mo_ref, vo_ref):
        g = p_ref[0]
        for j in range(1, n):
            g = g + p_ref[j]
        g_ref[...] = g
        d_ref[...], mo_ref[...], vo_ref[...] = _adamw(w_ref[...], g, m_ref[...], v_ref[...])

    ws = pl.BlockSpec((tr, C), lambda r: (r, 0))
    return _pcall(_after(body, 4, deps), grid=(R // tr,),
                  in_specs=[pl.BlockSpec((n, tr, C), lambda r: (0, r, 0)), ws, ws, ws] + [ANY] * len(deps),
                  out_specs=[ws] * 4, out_shape=[_sds((R, C), F32)] * 4, name=name,
                  compiler_params=_params(("parallel",)))(parts, w, m, v, *deps)


def _adam_plain(g, w, m, v, name):
    R, C = w.shape

    def body(g_ref, w_ref, m_ref, v_ref, d_ref, mo_ref, vo_ref):
        d_ref[...], mo_ref[...], vo_ref[...] = _adamw(w_ref[...], g_ref[...], m_ref[...], v_ref[...])

    ws = _const_spec((R, C))
    return _pcall(body, grid=(1,), in_specs=[ws] * 4, out_specs=[ws] * 3, out_shape=[_sds((R, C), F32)] * 3, name=name,
                  compiler_params=_params(("arbitrary",)))(g, w, m, v)


def _pair_sum(G, R1, my_c, name):
    n, R, C = G.shape
    half = n // 2
    tr = _tile_rows(R, C, align=16)

    def body(c_ref, g_ref, r_ref, o_ref):
        o_ref[...] = (g_ref[...].astype(F32) + r_ref[...].astype(F32)).astype(o_ref.dtype)

    blk = (None, tr, C)
    gs = pltpu.PrefetchScalarGridSpec(
        num_scalar_prefetch=1, grid=(half, R // tr),
        in_specs=[pl.BlockSpec(blk, lambda p, r, c: (2 * p + c[0], r, 0)), pl.BlockSpec(blk, lambda p, r, c: (p, r, 0))],
        out_specs=pl.BlockSpec(blk, lambda p, r, c: (p, r, 0)))
    return _pcall(body, grid_spec=gs, out_shape=_sds((half, R, C), G.dtype), name=name,
                  compiler_params=_params(("parallel", "parallel")))(my_c, G, R1)


def _adam_big(P, R2, my_chip, w, m, v, layer, prev, name, deps=()):
    _, R, C = P.shape
    nrecv = R2.shape[0]
    tr = _tile_rows(R, C, align=16)

    def body(p_sm, p_ref, r_ref, w_ref, m_ref, v_ref, *rest):
        g_ref, d_ref, mo_ref, vo_ref = rest[-4:]
        g = p_ref[...].astype(F32)
        for k in range(nrecv):
            g = g + r_ref[k].astype(F32)
        g_ref[...] = g
        d_ref[...], mo_ref[...], vo_ref[...] = _adamw(w_ref[...], g, m_ref[...], v_ref[...])

    ws = pl.BlockSpec((None, tr, C), lambda r, p: (layer, r, 0))
    held = [] if prev is None else list(prev)
    gs = pltpu.PrefetchScalarGridSpec(
        num_scalar_prefetch=1, grid=(R // tr,),
        in_specs=[pl.BlockSpec((None, tr, C), lambda r, p: (p[0], r, 0)),
                  pl.BlockSpec((nrecv, tr, C), lambda r, p: (0, r, 0)), ws, ws, ws] + [ANY] * (len(held) + len(deps)),
        out_specs=[ws] * 4)
    alias = {6 + i: i for i in range(len(held))}
    return _pcall(body, grid_spec=gs, out_shape=[_sds(w.shape, F32)] * 4, name=name, input_output_aliases=alias,
                  compiler_params=_params(("parallel",)))(my_chip, P, R2, w, m, v, *held, *deps)


def _place():
    return lax.axis_index("x"), lax.axis_index("y"), lax.axis_index("c")


def _sum_over_devices(scalar):
    return lax.psum(scalar, ("x", "y", "c"))


def _all_gather(shards, name, deps=()):
    n = len(shards)

    def body(*refs):
        ins, outs = refs[:n], refs[n:2 * n]
        send_sems, recv_sems, local_sems = refs[2 * n:]
        x, y, c = _place()
        me, sibling = (x, y, c), (x, y, 1 - c)
        chips = [(1 - x, y), (x, 1 - y), (1 - x, 1 - y)]

        def slot(a, px, py, pc):
            return outs[a].at[4 * px + 2 * py + pc]

        def copy(a, k, block, to, src=None):
            return pltpu.make_async_remote_copy(
                src_ref=slot(a, *block) if src is None else src, dst_ref=slot(a, *block),
                send_sem=send_sems.at[7 * a + k], recv_sem=recv_sems.at[7 * a + k], device_id=to, device_id_type=MESH)

        mine = [pltpu.make_async_copy(ins[a], slot(a, *me), local_sems.at[a]) for a in range(n)]
        for cp in mine:
            cp.start()
        first = []
        for a in range(n):
            first.append(copy(a, 0, me, sibling, src=ins[a]))
            first += [copy(a, 1 + j, me, (*chip, c), src=ins[a]) for j, chip in enumerate(chips)]
        for cp in first:
            cp.start()
        passed = []
        for j, chip in enumerate(chips):
            for a in range(n):
                copy(a, 1 + j, (*chip, c), me).wait_recv()
                fwd = copy(a, 4 + j, (*chip, c), sibling)
                fwd.start()
                passed.append(fwd)
        for a in range(n):
            copy(a, 0, sibling, me).wait_recv()
        for j, chip in enumerate(chips):
            for a in range(n):
                copy(a, 4 + j, (*chip, 1 - c), me).wait_recv()
        for cp in first + passed:
            cp.wait_send()
        for cp in mine:
            cp.wait()

    outs = _pcall(_after(body, n, deps), in_specs=[ANY] * (n + len(deps)), out_specs=[ANY] * n,
                  out_shape=[_sds((NDEV,) + s.shape, s.dtype) for s in shards],
                  scratch_shapes=[pltpu.SemaphoreType.DMA((7 * n,)), pltpu.SemaphoreType.DMA((7 * n,)),
                                  pltpu.SemaphoreType.DMA((n,))], name=name)(*shards, *deps)
    return list(outs)


HBM = pl.BlockSpec(memory_space=pltpu.HBM)
SEM = pl.BlockSpec(memory_space=pltpu.SEMAPHORE)


def _copies(plan, refs, send_sems, recv_sems):
    return [pltpu.make_async_remote_copy(src_ref=s, dst_ref=d, send_sem=send_sems.at[k], recv_sem=recv_sems.at[k],
                                         device_id=dev, device_id_type=MESH)
            for k, (s, d, dev) in enumerate(plan(refs, *_place()))]


def _xfer_start(bufs, ncopies, plan, name, deps=()):
    n = len(bufs)

    def body(*refs):
        for cp in _copies(plan, refs[:n], refs[n], refs[n + 1]):
            cp.start()
        token = refs[2 * n + 2]
        token[...] = jnp.zeros_like(token)

    outs = _pcall(
        _after(body, n, deps), name=name,
        out_shape=(pltpu.SemaphoreType.DMA((ncopies,)), pltpu.SemaphoreType.DMA((ncopies,)),
                   *[pltpu.HBM(b.shape, b.dtype) for b in bufs], _sds((8, LANE), F32)),
        in_specs=[HBM] * n + [ANY] * len(deps),
        out_specs=(SEM, SEM, *[HBM] * n, pl.BlockSpec(memory_space=pltpu.VMEM)),
        input_output_aliases={i: 2 + i for i in range(n)},
        compiler_params=pltpu.CompilerParams(has_side_effects=pltpu.SideEffectType.DATAFLOW_SIDE_EFFECTING),
    )(*[pltpu.with_memory_space_constraint(b, pltpu.HBM) for b in bufs], *deps)
    return (outs[0], outs[1]), list(outs[2:2 + n]), outs[2 + n]


def _xfer_wait(sems, bufs, plan, after, name):
    n = len(bufs)

    def body(*refs):
        for cp in _copies(plan, refs[:n], refs[n], refs[n + 1]):
            cp.wait_send()
            cp.wait_recv()

    outs = _pcall(
        body, name=name, out_shape=tuple(pltpu.HBM(b.shape, b.dtype) for b in bufs),
        in_specs=[HBM] * n + [SEM, SEM, ANY], out_specs=tuple([HBM] * n), input_output_aliases={i: i for i in range(n)},
        compiler_params=pltpu.CompilerParams(has_side_effects=pltpu.SideEffectType.DATAFLOW_SIDE_EFFECTING),
    )(*bufs, *sems, after)
    return list(outs)


def _chips_of(x, y):
    return [(1 - x, y), (x, 1 - y), (1 - x, 1 - y)]


def _gather_plan1(n):
    def plan(refs, x, y, c):
        out = []
        for a in range(n):
            blk = refs[a].at[4 * x + 2 * y + c]
            out.append((blk, blk, (x, y, 1 - c)))
            out += [(blk, blk, (px, py, c)) for px, py in _chips_of(x, y)]
        return out
    return plan


def _gather_plan2(n):
    def plan(refs, x, y, c):
        out = []
        for a in range(n):
            for px, py in _chips_of(x, y):
                blk = refs[a].at[4 * px + 2 * py + c]
                out.append((blk, blk, (x, y, 1 - c)))
        return out
    return plan


def _gather_start(shards, dev, name, deps=()):
    lands = [lax.dynamic_update_slice(lax.empty((NDEV,) + s.shape, s.dtype), s[None], (dev,) + (0,) * s.ndim)
             for s in shards]
    n = len(shards)
    sems, lands, tok = _xfer_start(lands, 4 * n, _gather_plan1(n), name + "_p1_start", deps)
    return dict(sems=sems, lands=lands, tok=tok, n=n)


def _gather_mid(st, after, name):
    n = st["n"]
    lands = _xfer_wait(st["sems"], st["lands"], _gather_plan1(n), after, name + "_p1_wait")
    sems, lands, tok = _xfer_start(lands, 3 * n, _gather_plan2(n), name + "_p2_start")
    return dict(sems=sems, lands=lands, tok=tok, n=n)


def _gather_finish(st, after, name):
    return _xfer_wait(st["sems"], st["lands"], _gather_plan2(st["n"]), after, name + "_p2_wait")


def _scatter_plan1(n):
    def plan(refs, x, y, c):
        return [(refs[a].at[2 * p + 1 - c], refs[n + a].at[p], (x, y, 1 - c)) for a in range(n) for p in range(NCHIP)]
    return plan


def _scatter_plan2(n):
    def plan(refs, x, y, c):
        return [(refs[a].at[2 * px + py], refs[n + a].at[j], (px, py, c))
                for a in range(n) for j, (px, py) in enumerate(_chips_of(x, y))]
    return plan


def _scatter_start(Gs, name):
    n = len(Gs)
    R1s = [lax.empty((NCHIP,) + g.shape[1:], g.dtype) for g in Gs]
    sems, bufs, tok = _xfer_start(list(Gs) + R1s, NCHIP * n, _scatter_plan1(n), name + "_s1_start")
    return dict(sems=sems, bufs=bufs, tok=tok, n=n)


def _scatter_mid(st, after, my_c, name):
    n = st["n"]
    bufs = _xfer_wait(st["sems"], st["bufs"], _scatter_plan1(n), after, name + "_s1_wait")
    Ps = [_pair_sum(bufs[a], bufs[n + a], my_c, f"{name}_pair_sum{a}") for a in range(n)]
    R2s = [lax.empty((3,) + p.shape[1:], p.dtype) for p in Ps]
    sems, bufs, tok = _xfer_start(Ps + R2s, 3 * n, _scatter_plan2(n), name + "_s2_start")
    return dict(sems=sems, bufs=bufs, tok=tok, n=n)


def _scatter_finish(st, after, name):
    n = st["n"]
    bufs = _xfer_wait(st["sems"], st["bufs"], _scatter_plan2(n), after, name + "_s2_wait")
    return bufs[:n], bufs[n:]


SMALL_ROWS = {"norm1_g": (0, 1), "norm2_g": (1, 1), "sgu_ln_g": (2, 1), "sgu_ln_b": (3, 1), "cfm_conv_b": (4, 1),
              "cfm_ln_g": (5, 1), "cfm_ln_b": (6, 1), "b_sgu": (7, 1), "w_sgu": (8, 128), "b_ada": (136, N_MOD),
              "w_short": (142, SHORT_K), "cfm_conv_w": (145, CFM_K)}
ROWS_PER_LAYER = 176
FINAL_ROW = DEPTH * ROWS_PER_LAYER
PACK_ROWS = 360


def _pack(get, D, layers=tuple(range(DEPTH)), tail=True):
    parts = []
    for l in layers:
        for name, (_, nrows) in SMALL_ROWS.items():
            a = get(name, l)
            parts.append(jnp.zeros((nrows * D,), F32) if a is None else a.astype(F32).reshape(nrows * D))
    if tail:
        fin = get("final_g", None)
        parts.append(fin.astype(F32).reshape(D))
        parts.append(jnp.zeros(((PACK_ROWS - FINAL_ROW - 1) * D,), F32))
    return jnp.concatenate(parts).reshape(-1, D)


def _unpack(pack, name, shape):
    D = pack.shape[1]
    r0, nrows = SMALL_ROWS[name]
    return jnp.stack([pack[l * ROWS_PER_LAYER + r0:l * ROWS_PER_LAYER + r0 + nrows] for l in range(DEPTH)]).reshape(shape)


def _mm_tiles(S):
    return min(512, S), min(1024, S)


def kernel(x, c, w_ada, b_ada, norm1_g, w_in, w_short, w_a_out, sgu_ln_g, sgu_ln_b, w_sgu, b_sgu, w_b_out, cfm_conv_w, cfm_conv_b, cfm_ln_g, cfm_ln_b, w_c_out, w_o, norm2_g, w_ffn_in, w_ffn_out, final_g, loss_target, m_w_ada, m_b_ada, m_norm1_g, m_w_in, m_w_short, m_w_a_out, m_sgu_ln_g, m_sgu_ln_b, m_w_sgu, m_b_sgu, m_w_b_out, m_cfm_conv_w, m_cfm_conv_b, m_cfm_ln_g, m_cfm_ln_b, m_w_c_out, m_w_o, m_norm2_g, m_w_ffn_in, m_w_ffn_out, m_final_g, v_w_ada, v_b_ada, v_norm1_g, v_w_in, v_w_short, v_w_a_out, v_sgu_ln_g, v_sgu_ln_b, v_w_sgu, v_b_sgu, v_w_b_out, v_cfm_conv_w, v_cfm_conv_b, v_cfm_ln_g, v_cfm_ln_b, v_w_c_out, v_w_o, v_norm2_g, v_w_ffn_in, v_w_ffn_out, v_final_g):
    W = dict(w_ada=w_ada, b_ada=b_ada, norm1_g=norm1_g, w_in=w_in, w_short=w_short, w_a_out=w_a_out, sgu_ln_g=sgu_ln_g,
             sgu_ln_b=sgu_ln_b, w_sgu=w_sgu, b_sgu=b_sgu, w_b_out=w_b_out, cfm_conv_w=cfm_conv_w, cfm_conv_b=cfm_conv_b,
             cfm_ln_g=cfm_ln_g, cfm_ln_b=cfm_ln_b, w_c_out=w_c_out, w_o=w_o, norm2_g=norm2_g, w_ffn_in=w_ffn_in,
             w_ffn_out=w_ffn_out, final_g=final_g)
    Mo = dict(w_ada=m_w_ada, b_ada=m_b_ada, norm1_g=m_norm1_g, w_in=m_w_in, w_short=m_w_short, w_a_out=m_w_a_out,
              sgu_ln_g=m_sgu_ln_g, sgu_ln_b=m_sgu_ln_b, w_sgu=m_w_sgu, b_sgu=m_b_sgu, w_b_out=m_w_b_out,
              cfm_conv_w=m_cfm_conv_w, cfm_conv_b=m_cfm_conv_b, cfm_ln_g=m_cfm_ln_g, cfm_ln_b=m_cfm_ln_b,
              w_c_out=m_w_c_out, w_o=m_w_o, norm2_g=m_norm2_g, w_ffn_in=m_w_ffn_in, w_ffn_out=m_w_ffn_out,
              final_g=m_final_g)
    Vo = dict(w_ada=v_w_ada, b_ada=v_b_ada, norm1_g=v_norm1_g, w_in=v_w_in, w_short=v_w_short, w_a_out=v_w_a_out,
              sgu_ln_g=v_sgu_ln_g, sgu_ln_b=v_sgu_ln_b, w_sgu=v_w_sgu, b_sgu=v_b_sgu, w_b_out=v_w_b_out,
              cfm_conv_w=v_cfm_conv_w, cfm_conv_b=v_cfm_conv_b, cfm_ln_g=v_cfm_ln_g, cfm_ln_b=v_cfm_ln_b,
              w_c_out=v_w_c_out, w_o=v_w_o, norm2_g=v_norm2_g, w_ffn_in=v_w_ffn_in, w_ffn_out=v_w_ffn_out,
              final_g=v_final_g)
    order = ["w_ada", "b_ada", "norm1_g", "w_in", "w_short", "w_a_out", "sgu_ln_g", "sgu_ln_b", "w_sgu", "b_sgu",
             "w_b_out", "cfm_conv_w", "cfm_conv_b", "cfm_ln_g", "cfm_ln_b", "w_c_out", "w_o", "norm2_g", "w_ffn_in",
             "w_ffn_out", "final_g"]

    assert DEPTH == 2, "the weight-gather schedule below is written for two layers"
    S, D = x.shape[1], x.shape[2]
    F2 = w_ffn_in.shape[2] * NDEV
    FF = F2 // 2
    xi, yi, ci = _place()
    dev = 4 * xi + 2 * yi + ci
    my_c = jnp.reshape(ci, (1,)).astype(jnp.int32)
    my_chip = jnp.reshape(2 * xi + yi, (1,)).astype(jnp.int32)
    tm, tm_big = _mm_tiles(S)
    x0 = x.reshape(S, D)
    tgt = loss_target.reshape(S, D)

    def shards_of(l):
        return [w_in[l].astype(BF16), w_a_out[l].astype(BF16), w_b_out[l].astype(BF16), w_c_out[l].astype(BF16),
                w_o[l].astype(BF16), w_ffn_in[l].astype(BF16), w_ffn_out[l].astype(BF16)]

    c_all = _all_gather([jnp.pad(c, ((0, 7), (0, 0)))], "ag_c")[0][:, 0, :]
    modpart, c_act = _ada_fwd(c_all, w_ada, "ada_fwd")
    ncol = modpart.shape[2]
    mg = _all_gather([modpart.reshape(DEPTH * NDEV, ncol)], "ag_mod")[0].reshape(NDEV, DEPTH, NDEV, ncol)
    mine = lax.dynamic_index_in_dim(mg, dev, axis=2, keepdims=False)
    mod = (jnp.transpose(mine, (1, 0, 2)).reshape(DEPTH, N_MOD * D) + b_ada).reshape(DEPTH, N_MOD, D)

    tril = jnp.tril(jnp.ones((CHUNK, CHUNK), dtype=bool))

    def layer_consts(l):
        wt = jnp.where(tril[None], w_sgu[l], 0.0).astype(BF16)
        return dict(
            wsh=jnp.pad(w_short_full[l], ((0, 8 - SHORT_K), (0, 0))),
            sgu_ln=_rows(sgu_ln_g[l], sgu_ln_b[l]),
            wtril=wt, wtril_t=jnp.swapaxes(wt, 1, 2),
            bias_full=jnp.repeat(b_sgu[l].T, LANE, axis=1),
            cw=jnp.pad(cfm_w_full[l], ((0, HALO - CFM_K), (0, 0))),
            cvec=_rows(cfm_conv_b[l], cfm_ln_g[l], cfm_ln_b[l]))

    ncs = w_short.shape[2]
    sw = _all_gather([w_short.reshape(DEPTH * SHORT_K, ncs), cfm_conv_w.reshape(DEPTH * CFM_K, ncs)], "ag_convw",
                     deps=(mod,))
    w_short_full = jnp.transpose(sw[0], (1, 0, 2)).reshape(DEPTH, SHORT_K, D)
    cfm_w_full = jnp.transpose(sw[1], (1, 0, 2)).reshape(DEPTH, CFM_K, D)

    def rest_of(g):
        return dict(w_a=g[0].reshape(1, D, D), w_b=g[1].reshape(1, D, D), w_c=g[2].reshape(1, D, D),
                    w_o=g[3].reshape(1, D, D), w_fi=jnp.transpose(g[4], (1, 0, 2)).reshape(1, D, F2),
                    w_fo=g[5].reshape(1, FF, D))

    ag_in0 = _gather_start(shards_of(0)[:1], dev, "ag_w_in0", deps=(cfm_w_full,))
    ag_rest0 = _gather_start(shards_of(0)[1:], dev, "ag_rest0", deps=(ag_in0["tok"],))
    ag_in0 = _gather_mid(ag_in0, ag_rest0["tok"], "ag_w_in0")
    Wg = [None, None]
    ag_l1 = None
    nin = w_in.shape[2]
    tn_in = nin if nin % 256 == 0 and nin <= 1280 else 256
    tn_fi = 512 if F2 % 512 == 0 else 256
    tn_ffn = 1408 if F2 % 1408 == 0 else tn_fi
    tn_dw = min(512, D)

    saved = []
    xcur, fprev, gprev = x0, None, None
    for l in range(DEPTH):
        sh1, sc1, g1, sh2, sc2, g2 = [mod[l, k] for k in range(N_MOD)]
        cl = layer_consts(l)
        vec1 = _rows(jnp.zeros((D,), F32) if gprev is None else gprev, norm1_g[l], sc1, sh1)
        if l == 0:
            xl, h = _norm_fwd(xcur, fprev, vec1, f"norm1_fwd{l}", deps=(ag_in0["tok"],))
            Wg[0] = dict(w_in=_gather_finish(ag_in0, h, "ag_w_in0")[0])
        else:
            ag_l1 = _gather_mid(ag_l1, fprev, f"ag_w{l}")
            xl, h = _norm_fwd(xcur, fprev, vec1, f"norm1_fwd{l}", deps=(ag_l1["tok"],))
            g = _gather_finish(ag_l1, h, f"ag_w{l}")
            Wg[l] = dict(w_in=g[0], **rest_of(g[1:]))
        wl = Wg[l]
        z = _mm_nn(h, wl["w_in"], BF16, tm_big, tn_in, D, f"mm_in{l}", w_outer=True)
        mix_deps = ()
        if l == 0:
            ag_rest0 = _gather_mid(ag_rest0, z, "ag_rest0")
            mix_deps = (ag_rest0["tok"],)
            if DEPTH > 1:
                ag_l1 = _gather_start(shards_of(1), dev, "ag_w1")
                mix_deps += (ag_l1["tok"],)
        acts = _mixer_fwd(z, cl["wsh"], cl["sgu_ln"], cl["wtril"], cl["bias_full"], cl["cw"], cl["cvec"], f"mixer_fwd{l}",
                          deps=mix_deps)
        if l == 0:
            wl.update(rest_of(_gather_finish(ag_rest0, acts[0], "ag_rest0")))
        merged, ys = _branch_out(acts, [wl["w_a"][0], wl["w_b"][0], wl["w_c"][0]], z, f"branch_out{l}")
        o = _mm_nn(merged, wl["w_o"], F32, tm_big, D, D, f"mm_o{l}")
        x1, h2 = _norm_fwd(xl, o, _rows(g1, norm2_g[l], sc2, sh2), f"norm2_fwd{l}")
        gu = _mm_nn(h2, wl["w_fi"], BF16, tm_big, tn_ffn, D, f"mm_ffn_in{l}")
        act = _swiglu_fwd(gu, f"swiglu_fwd{l}")
        f = _mm_nn(act, wl["w_fo"], F32, tm, D, FF, f"mm_ffn_out{l}")
        saved.append(dict(xl=xl, h=h, z=z, acts=acts, ys=ys, merged=merged, o=o, x1=x1, h2=h2, gu=gu, act=act, f=f,
                          consts=cl, mod=(sh1, sc1, g1, sh2, sc2, g2)))
        xcur, fprev, gprev = x1, f, g2

    last = saved[-1]
    dxup, dfb, fsums, loss_blk = _final_bwd(last["x1"], last["f"], tgt, _rows(last["mod"][5], final_g), "final_bwd")
    loss = _sum_over_devices(loss_blk[0, 0])
    dgate2_next = fsums[1]
    small = [dict() for _ in range(DEPTH)]
    dmods = [None] * DEPTH
    nfi = w_ffn_in.shape[2]
    early_names, late_names = ["w_ffn_out", "w_ffn_in", "w_o"], ["w_a_out", "w_b_out", "w_c_out", "w_in"]
    results = {n: None for n in early_names + late_names}

    def adam_group(names, Ps, R2s, l, deps=()):
        for n, p, r2 in zip(names, Ps, R2s):
            results[n] = _adam_big(p, r2, my_chip, W[n], Mo[n], Vo[n], l, results[n], f"adam_{n}{l}", deps)

    deferred = []
    late_prev = None
    ag_s1, gathered1 = None, None
    tk_w = min(2048, S)
    tn_dw_in = tn_in // 2 if tn_in == 1280 else tn_in
    for l in reversed(range(DEPTH)):
        sv, wl, cl = saved[l], Wg[l], saved[l]["consts"]
        sh1, sc1, g1, sh2, sc2, g2 = sv["mod"]
        dact = _mm_nt(dfb, wl["w_fo"], BF16, tm, FF, D, f"mm_dact{l}",
                      deps=() if late_prev is None else (late_prev["tok"], ag_s1["tok"]))
        g_fo = _mm_tn(sv["act"], dfb, 1, FF // 2, D, tk_w, f"mm_dw_ffn_out{l}")
        dgu = _swiglu_bwd(dact, sv["gu"], f"swiglu_bwd{l}")
        dh2 = _mm_nt(dgu, wl["w_fi"], F32, tm, D, F2, f"mm_dh2{l}")
        if late_prev is not None:
            deferred.append((late_names, *_scatter_finish(late_prev, dh2, f"rs_late{l + 1}"), l + 1))
            late_prev = None
        g_fi = _mm_tn(sv["h2"], dgu, 1, D, tn_fi, S, f"mm_dw_ffn_in{l}")
        if ag_s1 is not None:
            ag_s1 = _gather_mid(ag_s1, g_fi, "ag_small1")
        dx1, dob, s2 = _norm_bwd(sv["x1"], dh2, dxup, _rows(norm2_g[l], sc2, g1), sv["o"], f"norm2_bwd{l}",
                                 deps=() if ag_s1 is None else (ag_s1["tok"],))
        dmerged = _mm_nt(dob, wl["w_o"], BF16, tm_big, D, D, f"mm_dmerged{l}")
        g_o = _mm_tn(sv["merged"], dob, 1, D, tn_dw, S, f"mm_dw_o{l}")
        early = _scatter_start([g_fo.reshape(NDEV, FF // NDEV, D),
                                jnp.transpose(g_fi.reshape(D, NDEV, nfi), (1, 0, 2)),
                                g_o.reshape(NDEV, D // NDEV, D)], f"rs_early{l}")
        dys, dz = _gate_bwd(dmerged, sv["z"], sv["ys"], f"gate_bwd{l}", deps=(early["tok"],))
        if ag_s1 is not None:
            gathered1 = _gather_finish(ag_s1, dys[0], "ag_small1")[0]
            ag_s1 = None
        early = _scatter_mid(early, dys[0], my_c, f"rs_early{l}")
        dacts, g_abc = [], []
        for n, key in enumerate(("w_a", "w_b", "w_c")):
            dacts.append(_mm_nt(dys[n], wl[key], BF16, tm_big, D, D, f"mm_dact_{key}{l}",
                                deps=(early["tok"],) if n == 0 else ()))
            g_abc.append(_mm_tn(sv["acts"][n], dys[n], 1, D, tn_dw, S, f"mm_d{key}{l}"))
        dz, mvec, dcw, dws, dbs = _mixer_bwd(sv["z"], dacts, dz, cl["wsh"], cl["sgu_ln"], cl["wtril"], cl["wtril_t"],
                                             cl["bias_full"], cl["cw"], cl["cvec"], f"mixer_bwd{l}")
        dh = _mm_nt(dz, wl["w_in"], F32, tm_big, D, tn_in, f"mm_dh{l}")
        g_in = _mm_tn(sv["h"], dz, NDEV, D, tn_dw_in, S, f"mm_dw_in{l}")
        late = _scatter_start([g.reshape(NDEV, D // NDEV, D) for g in g_abc] + [g_in], f"rs_late{l}")
        if l > 0:
            pv = saved[l - 1]
            dxup, dfb, s1 = _norm_bwd(sv["xl"], dh, dx1, _rows(norm1_g[l], sc1, pv["mod"][5]), pv["f"], f"norm1_bwd{l}",
                                      deps=(late["tok"],))
        else:
            dxup, dfb, s1 = _norm_bwd(sv["xl"], dh, dx1, _rows(norm1_g[l], sc1), None, f"norm1_bwd{l}", deps=(late["tok"],))
        deferred.append((early_names, *_scatter_finish(early, dxup, f"rs_early{l}"), l))
        dmods[l] = jnp.stack([s1[0], s1[1], s2[3], s2[0], s2[1], dgate2_next])
        dgate2_next = s1[3]
        small[l] = dict(norm1_g=s1[2], norm2_g=s2[2], sgu_ln_g=mvec[3], sgu_ln_b=mvec[4], cfm_conv_b=mvec[5],
                        cfm_ln_g=mvec[6], cfm_ln_b=mvec[7], b_sgu=dbs[:, :, 0],
                        w_sgu=jnp.where(tril[None], dws, 0.0), b_ada=dmods[l], w_short=mvec[0:SHORT_K],
                        cfm_conv_w=dcw[0:CFM_K])
        small_get = lambda name, k: fsums[0] if name == "final_g" else small[k][name]
        if l > 0:
            late_prev = _scatter_mid(late, dxup, my_c, f"rs_late{l}")
            ag_s1 = _gather_start([_pack(small_get, D, layers=(l,), tail=True)], dev, "ag_small1", deps=(late_prev["tok"],))
    grad_x = dxup.reshape(x.shape)

    gathered0 = _all_gather([_pack(small_get, D, layers=(0,), tail=False)], "ag_small0", deps=(dxup,))[0]
    late_prev = _scatter_mid(late, gathered0, my_c, "rs_late0")
    gathered = jnp.concatenate([gathered0, gathered1], axis=1)
    sharded_small = ("w_short", "cfm_conv_w")
    packs = [_pack(lambda name, l, T=T: T["final_g"] if name == "final_g" else (None if name in sharded_small else T[name][l]), D)
             for T in (W, Mo, Vo)]
    sg, sd, sm, sv_ = _adam_small(gathered, *packs, name="adam_small", deps=(late_prev["tok"],))
    out = {}
    for name in order:
        if name in SMALL_ROWS and name not in sharded_small:
            out[name] = tuple(_unpack(p, name, W[name].shape) for p in (sg, sd, sm, sv_))
    out["final_g"] = tuple(p[FINAL_ROW] for p in (sg, sd, sm, sv_))

    def my_cols(name):
        full = _unpack(sg, name, (DEPTH, SMALL_ROWS[name][1], D))
        return lax.dynamic_slice_in_dim(full, dev * ncs, ncs, axis=2)

    gcs = jnp.concatenate([my_cols("w_short").reshape(-1, ncs), my_cols("cfm_conv_w").reshape(-1, ncs)])
    ncr = gcs.shape[0]
    padr = (-ncr) % 8
    cat = lambda T: jnp.pad(jnp.concatenate([T["w_short"].reshape(-1, ncs), T["cfm_conv_w"].reshape(-1, ncs)]), ((0, padr), (0, 0)))
    cd, cm, cv = _adam_plain(jnp.pad(gcs, ((0, padr), (0, 0))), cat(W), cat(Mo), cat(Vo), "adam_convw")
    nsh = DEPTH * SHORT_K
    out["w_short"] = tuple(a[0:nsh].reshape(w_short.shape) for a in (gcs, cd, cm, cv))
    out["cfm_conv_w"] = tuple(a[nsh:ncr].reshape(cfm_conv_w.shape) for a in (gcs, cd, cm, cv))

    dm_all = jnp.stack([gathered[:, l * ROWS_PER_LAYER + 136:l * ROWS_PER_LAYER + 136 + N_MOD, :].reshape(NDEV, N_MOD * D)
                        for l in range(DEPTH)])
    dm_mine = lax.dynamic_slice_in_dim(dm_all, dev * ncol, ncol, axis=2)
    out["w_ada"] = tuple(_adam_ada(jnp.transpose(c_act), dm_mine, w_ada, m_w_ada, v_w_ada, "adam_ada"))

    for names, Ps, R2s, l in deferred:
        adam_group(names, Ps, R2s, l, deps=(late_prev["tok"],))
    adam_group(late_names, *_scatter_finish(late_prev, results["w_o"][0], "rs_late0"), 0)
    for n in early_names + late_names:
        out[n] = tuple(results[n])

    grads = [out[n][0] for n in order]
    deltas = [out[n][1] for n in order]
    new_m = [out[n][2] for n in order]
    new_v = [out[n][3] for n in order]
    return (loss, grad_x, *grads, *deltas, *new_m, *new_v)
```

```python
import functools
import math

import jax
import jax.numpy as jnp
from jax import lax
from jax.experimental import pallas as pl
from jax.experimental.pallas import tpu as pltpu

F32, BF16 = jnp.float32, jnp.bfloat16
NDEV = 8
NCHIP = NDEV // 2
DEPTH = 2
EPS = 1e-6
CHUNK = 128
NG = 8
SHORT_K = 3
CFM_K = 31
HALO = 32
N_MOD = 6
LANE = 128
VMEM_LIMIT = 56 * 1024 * 1024
ADAM_LR, ADAM_B1, ADAM_B2, ADAM_EPS, ADAM_WD, ADAM_STEP = 0.001, 0.9, 0.999, 1e-08, 0.01, 10
_G0 = math.sqrt(2.0 / math.pi)
_G1 = 0.044715
MESH = pl.DeviceIdType.MESH
ANY = pl.BlockSpec(memory_space=pl.ANY)


def _pcall(body, **kw):
    return pl.pallas_call(body, **kw)


def _params(sem=None):
    return pltpu.CompilerParams(dimension_semantics=sem, vmem_limit_bytes=VMEM_LIMIT)


def _sds(shape, dtype):
    return jax.ShapeDtypeStruct(tuple(shape), dtype)


def _mm_body(dims, nk, out_f32):
    def body(a_ref, b_ref, o_ref, *scr):
        k = pl.program_id(2)
        part = lax.dot_general(a_ref[...], b_ref[...], dims, preferred_element_type=F32)
        if nk == 1:
            o_ref[...] = part.reshape(o_ref.shape).astype(o_ref.dtype)
        elif out_f32:
            @pl.when(k == 0)
            def _():
                o_ref[...] = part.reshape(o_ref.shape)

            @pl.when(k > 0)
            def _():
                o_ref[...] += part.reshape(o_ref.shape)
        else:
            acc = scr[0]

            @pl.when(k == 0)
            def _():
                acc[...] = part

            @pl.when(k > 0)
            def _():
                acc[...] += part

            @pl.when(k == nk - 1)
            def _():
                o_ref[...] = acc[...].astype(o_ref.dtype)
    return body


def _after(body, n_in, deps):
    nd = len(deps)
    if nd == 0:
        return body

    def ordered(*refs):
        return body(*refs[:n_in], *refs[n_in + nd:])
    return ordered


def _mm_call(body, grid, in_specs, out_spec, out_shape, acc_shape, name, deps=()):
    scratch = [] if acc_shape is None else [pltpu.VMEM(acc_shape, F32)]
    return _pcall(_after(body, 2, deps), grid=grid, in_specs=in_specs + [ANY] * len(deps), out_specs=out_spec,
                  out_shape=out_shape, scratch_shapes=scratch, name=name,
                  compiler_params=_params(("parallel", "parallel", "arbitrary")))


def _mm_nn(a, b3, out_dtype, tm, tn, tk, name, w_outer=False, deps=()):
    M, K = a.shape
    G, _, Nb = b3.shape
    npb, nk = Nb // tn, K // tk
    out_f32 = out_dtype == F32
    body = _mm_body((((1,), (0,)), ((), ())), nk, out_f32)
    if w_outer:
        grid = (G * npb, M // tm, nk)
        ij = lambda p, q: (q, p)
    else:
        grid = (M // tm, G * npb, nk)
        ij = lambda p, q: (p, q)

    def a_map(p, q, k):
        i, j = ij(p, q)
        return (i, k)

    def b_map(p, q, k):
        i, j = ij(p, q)
        return (j // npb, k, j % npb)

    def o_map(p, q, k):
        return ij(p, q)

    def wrapped(a_ref, b_ref, o_ref, *scr):
        body(a_ref, b_ref, o_ref, *scr)

    return _mm_call(wrapped, grid, [pl.BlockSpec((tm, tk), a_map), pl.BlockSpec((None, tk, tn), b_map)],
                    pl.BlockSpec((tm, tn), o_map), _sds((M, G * Nb), out_dtype),
                    None if (nk == 1 or out_f32) else (tm, tn), name, deps)(a, b3, *deps)


def _mm_nt(a, b3, out_dtype, tm, tn, tk, name, deps=()):
    M, _ = a.shape
    G, Ko, Nb = b3.shape
    kpb = Nb // tk
    nk = G * kpb
    out_f32 = out_dtype == F32
    body = _mm_body((((1,), (1,)), ((), ())), nk, out_f32)

    def wrapped(a_ref, b_ref, o_ref, *scr):
        body(a_ref, b_ref, o_ref, *scr)

    return _mm_call(wrapped, (M // tm, Ko // tn, nk),
                    [pl.BlockSpec((tm, tk), lambda i, j, k: (i, k)),
                     pl.BlockSpec((None, tn, tk), lambda i, j, k: (k // kpb, j, k % kpb))],
                    pl.BlockSpec((tm, tn), lambda i, j, k: (i, j)), _sds((M, Ko), out_dtype),
                    None if (nk == 1 or out_f32) else (tm, tn), name, deps)(a, b3, *deps)


def _mm_wgrad(at, b, G, tm, tn, tk, name, deps=()):
    M, T = at.shape
    Nb = b.shape[1] // G
    npb, nk = Nb // tn, T // tk
    body = _mm_body((((1,), (0,)), ((), ())), nk, False)

    def wrapped(a_ref, b_ref, o_ref, *scr):
        body(a_ref, b_ref, o_ref, *scr)

    a = at
    in_specs = [pl.BlockSpec((tm, tk), lambda i, j, k: (i, k)), pl.BlockSpec((tk, tn), lambda i, j, k: (k, j))]
    out_spec = pl.BlockSpec((None, tm, tn), lambda i, j, k: (j // npb, i, j % npb))
    return _mm_call(wrapped, (M // tm, G * npb, nk), in_specs, out_spec, _sds((G, M, Nb), BF16),
                    None if nk == 1 else (tm, tn), name, deps)(a, b, *deps)


def _rsum(v):
    return jnp.sum(v, axis=0, keepdims=True)


def _rmean(v):
    return jnp.mean(v, axis=-1, keepdims=True)


def _gelu(x):
    t = jnp.tanh(_G0 * (x + _G1 * (x * x * x)))
    return x * (0.5 * (1.0 + t)), t


def _dgelu(x, t):
    return 0.5 * (1.0 + t) + 0.5 * x * (1.0 - t * t) * (_G0 * (1.0 + 3.0 * _G1 * (x * x)))


def _sigmoid(x):
    return 1.0 / (1.0 + jnp.exp(-x))


def _fill_shifted(ext, rot):
    v = ext[...]
    n = v.shape[0]
    for b in range(1, 8):
        rot[b - 1] = pltpu.roll(v, n - b, 0)


def _rows_at(ext, rot, s, tm, cs=slice(None)):
    a, b = divmod(s, 8)
    return ext[8 * a:8 * a + tm, cs] if b == 0 else rot[b - 1, 8 * a:8 * a + tm, cs]


def _causal_conv(w_ref, taps, bias, ext, rot, offset, tm, out):
    D = out.shape[1]
    for cb in range(D // LANE):
        cs = slice(cb * LANE, (cb + 1) * LANE)
        acc = None
        for k, o in zip(taps, offset):
            term = w_ref[k:k + 1, cs] * _rows_at(ext, rot, o, tm, cs)
            acc = term if acc is None else acc + term
        out[:, cs] = acc if bias is None else acc + bias[:, cs]


def _rows(*vs):
    a = jnp.stack([v.astype(F32) for v in vs])
    return jnp.pad(a, ((0, 8 - len(vs)), (0, 0)))


def _row_spec(tm, D):
    return pl.BlockSpec((tm, D), lambda i: (i, 0))


def _const_spec(shape):
    nd = len(shape)
    return pl.BlockSpec(shape, lambda i: (0,) * nd)


def _norm_fwd(xp, f, vec, name, deps=()):
    S, D = xp.shape
    tm = min(256, S)
    has_f = f is not None

    def body(*refs):
        if has_f:
            xp_ref, f_ref, vec_ref, xo_ref, h_ref, ht_ref = refs
            x = xp_ref[...] + vec_ref[0:1, :] * f_ref[...]
            xo_ref[...] = x
        else:
            xp_ref, vec_ref, h_ref, ht_ref = refs
            x = xp_ref[...]
        r = lax.rsqrt(_rmean(x * x) + EPS)
        h = (x * r) * vec_ref[1:2, :]
        h = h * (1.0 + vec_ref[2:3, :]) + vec_ref[3:4, :]
        h_ref[...] = h.astype(BF16)
        ht_ref[...] = h.T.astype(BF16)

    rs = _row_spec(tm, D)
    ins = [xp, f, vec] if has_f else [xp, vec]
    in_specs = ([rs, rs] if has_f else [rs]) + [_const_spec((8, D))]
    out_shape = ([_sds((S, D), F32)] if has_f else []) + [_sds((S, D), BF16), _sds((D, S), BF16)]
    out_specs = [rs] * (len(out_shape) - 1) + [pl.BlockSpec((D, tm), lambda i: (0, i))]
    outs = _pcall(_after(body, len(ins), deps), grid=(S // tm,), in_specs=in_specs + [ANY] * len(deps),
                  out_specs=out_specs, out_shape=out_shape, name=name,
                  compiler_params=_params(("parallel",)))(*ins, *deps)
    return (outs[0], outs[1], outs[2]) if has_f else (xp, outs[0], outs[1])


def _mixer_fwd(z, wsh, sgu_ln, wtril, bias_full, cw, cvec, name, deps=()):
    S = z.shape[0]
    D = wsh.shape[1]
    tm = CHUNK

    def body(z_ref, wsh_ref, sln_ref, wt_ref, bias_ref, cw_ref, cv_ref, oa_ref, ob_ref, oc_ref, ta_ref, tb_ref, tc_ref,
             pe, ge, gr, cbuf):
        i = pl.program_id(0)

        @pl.when(i == 0)
        def _():
            pe[0:HALO, :] = jnp.zeros((HALO, D), F32)
            ge[0:HALO, :] = jnp.zeros((HALO, D), F32)

        def col(n):
            return z_ref[:, n * D:(n + 1) * D].astype(F32)

        pe[HALO:HALO + tm, :] = col(1) * col(2)
        q = wsh_ref[0:1, :] * pe[HALO - 2:HALO - 2 + tm, :]
        q = q + wsh_ref[1:2, :] * pe[HALO - 1:HALO - 1 + tm, :]
        q = q + wsh_ref[2:3, :] * pe[HALO:HALO + tm, :]
        act_a = col(0) * q
        oa_ref[...] = act_a.astype(BF16)
        ta_ref[...] = act_a.T.astype(BF16)
        gu, _ = _gelu(col(3))
        gv, _ = _gelu(col(4))
        d = gv - _rmean(gv)
        nrm = d * lax.rsqrt(_rmean(d * d) + EPS)
        vnb = (nrm * sln_ref[0:1, :] + sln_ref[1:2, :]).astype(BF16)
        for g in range(NG):
            cs = slice(g * LANE, (g + 1) * LANE)
            mixed = jnp.dot(wt_ref[g], vnb[:, cs], preferred_element_type=F32) + bias_ref[:, cs]
            act_b = gu[:, cs] * mixed
            ob_ref[:, cs] = act_b.astype(BF16)
            tb_ref[cs, :] = act_b.T.astype(BF16)
        ge[HALO:HALO + tm, :] = col(5) * _sigmoid(col(6))
        _fill_shifted(ge, gr)
        o0 = HALO - (CFM_K - 1)
        _causal_conv(cw_ref, range(CFM_K), cv_ref[0:1, :], ge, gr, range(o0, o0 + CFM_K), tm, cbuf)
        conv = cbuf[...]
        d = conv - _rmean(conv)
        ln = (d * lax.rsqrt(_rmean(d * d) + EPS)) * cv_ref[1:2, :] + cv_ref[2:3, :]
        act_c = ln * _sigmoid(ln)
        oc_ref[...] = act_c.astype(BF16)
        tc_ref[...] = act_c.T.astype(BF16)
        pe[0:HALO, :] = pe[tm:tm + HALO, :]
        ge[0:HALO, :] = ge[tm:tm + HALO, :]

    rs = _row_spec(tm, D)
    outs = _pcall(
        _after(body, 7, deps), grid=(S // tm,),
        in_specs=[pl.BlockSpec((tm, 7 * D), lambda i: (i, 0)), _const_spec((8, D)), _const_spec((8, D)),
                  _const_spec((NG, CHUNK, CHUNK)), _const_spec((CHUNK, D)), _const_spec((HALO, D)), _const_spec((8, D))]
        + [ANY] * len(deps),
        out_specs=[rs, rs, rs] + [pl.BlockSpec((D, tm), lambda i: (0, i))] * 3,
        out_shape=[_sds((S, D), BF16)] * 3 + [_sds((D, S), BF16)] * 3,
        scratch_shapes=[pltpu.VMEM((HALO + tm, D), F32), pltpu.VMEM((HALO + tm, D), F32),
                        pltpu.VMEM((7, HALO + tm, D), F32), pltpu.VMEM((tm, D), F32)],
        name=name, compiler_params=_params(("arbitrary",)))(z, wsh, sgu_ln, wtril, bias_full, cw, cvec, *deps)
    return outs[:3], outs[3:]


def _branch_out(acts, ws, z, name):
    S, D = acts[0].shape
    tm = min(256, S)

    def body(a0, a1, a2, w0, w1, w2, g0, g1, g2, m_ref, mt_ref, y_ref):
        m = None
        for n, (a, w, g) in enumerate(((a0, w0, g0), (a1, w1, g1), (a2, w2, g2))):
            y = jnp.dot(a[...], w[...], preferred_element_type=F32)
            y_ref[n] = y.astype(BF16)
            t = _sigmoid(g[...].astype(F32)) * y
            m = t if m is None else m + t
        m_ref[...] = m.astype(BF16)
        mt_ref[...] = m.T.astype(BF16)

    rs = _row_spec(tm, D)
    gate_specs = [pl.BlockSpec((tm, D), functools.partial(lambda i, n: (i, 7 + n), n=n)) for n in range(3)]
    return _pcall(body, grid=(S // tm,),
                  in_specs=[rs, rs, rs] + [_const_spec((D, D))] * 3 + gate_specs,
                  out_specs=[rs, pl.BlockSpec((D, tm), lambda i: (0, i)), pl.BlockSpec((3, tm, D), lambda i: (0, i, 0))],
                  out_shape=[_sds((S, D), BF16), _sds((D, S), BF16), _sds((3, S, D), BF16)], name=name,
                  compiler_params=_params(("parallel",)))(*acts, *ws, z, z, z)


def _ffn_in_swiglu(h2, w3, tm, tn, name):
    S, D = h2.shape
    F = w3.shape[2] // 2
    nj = F // tn

    def body(a_ref, wg_ref, wu_ref, gu_ref, act_ref, actt_ref):
        a = a_ref[...]
        g = jnp.dot(a, wg_ref[...], preferred_element_type=F32)
        u = jnp.dot(a, wu_ref[...], preferred_element_type=F32)
        gu_ref[0] = g.astype(BF16)
        gu_ref[1] = u.astype(BF16)
        act = (g * _sigmoid(g)) * u
        act_ref[...] = act.astype(BF16)
        actt_ref[...] = act.T.astype(BF16)

    return _pcall(body, grid=(S // tm, nj),
                  in_specs=[pl.BlockSpec((tm, D), lambda i, j: (i, 0)), pl.BlockSpec((None, D, tn), lambda i, j: (0, 0, j)),
                            pl.BlockSpec((None, D, tn), lambda i, j: (0, 0, j + nj))],
                  out_specs=[pl.BlockSpec((2, tm, tn), lambda i, j: (0, i, j)), pl.BlockSpec((tm, tn), lambda i, j: (i, j)),
                             pl.BlockSpec((tn, tm), lambda i, j: (j, i))],
                  out_shape=[_sds((2, S, F), BF16), _sds((S, F), BF16), _sds((F, S), BF16)], name=name,
                  compiler_params=_params(("parallel", "parallel")))(h2, w3, w3)


def _swiglu_bwd(dact, gu, name):
    _, S, F = gu.shape
    F2 = 2 * F
    tm = min(128, S)

    def body(d_ref, g_ref, u_ref, o_ref):
        g = g_ref[...].astype(F32)
        sg = _sigmoid(g)
        d = d_ref[...].astype(F32)
        o_ref[:, 0:F] = (d * u_ref[...].astype(F32) * (sg * (1.0 + g * (1.0 - sg)))).astype(BF16)
        o_ref[:, F:2 * F] = (d * (g * sg)).astype(BF16)

    return _pcall(body, grid=(S // tm,),
                  in_specs=[pl.BlockSpec((tm, F), lambda i: (i, 0)), pl.BlockSpec((None, tm, F), lambda i: (0, i, 0)),
                            pl.BlockSpec((None, tm, F), lambda i: (1, i, 0))],
                  out_specs=pl.BlockSpec((tm, F2), lambda i: (i, 0)), out_shape=_sds((S, F2), BF16), name=name,
                  compiler_params=_params(("parallel",)))(dact, gu, gu)


def _final_bwd(x1, f, tgt, vec, name):
    S, D = x1.shape
    tm = min(256, S)

    def body(x_ref, f_ref, t_ref, vec_ref, dx_ref, df_ref, sums_ref, loss_ref):
        @pl.when(pl.program_id(0) == 0)
        def _():
            sums_ref[...] = jnp.zeros_like(sums_ref)
            loss_ref[...] = jnp.zeros_like(loss_ref)

        gate, fg = vec_ref[0:1, :], vec_ref[1:2, :]
        fv = f_ref[...]
        x = x_ref[...] + gate * fv
        r = lax.rsqrt(_rmean(x * x) + EPS)
        xn = x * r
        diff = xn * fg - t_ref[...]
        per_tok = _rmean(diff * diff)
        loss_ref[...] += 0.5 * jnp.sum(per_tok, axis=0, keepdims=True)
        dy = diff * (1.0 / D)
        sums_ref[0:1, :] += _rsum(dy * xn)
        dxn = dy * fg
        dx = r * (dxn - xn * _rmean(dxn * xn))
        sums_ref[1:2, :] += _rsum(dx * fv)
        dx_ref[...] = dx
        df_ref[...] = (dx * gate).astype(BF16)

    rs = _row_spec(tm, D)
    return _pcall(body, grid=(S // tm,), in_specs=[rs, rs, rs, _const_spec((8, D))],
                  out_specs=[rs, rs, _const_spec((8, D)), _const_spec((8, LANE))],
                  out_shape=[_sds((S, D), F32), _sds((S, D), BF16), _sds((8, D), F32), _sds((8, LANE), F32)],
                  name=name, compiler_params=_params(("arbitrary",)))(x1, f, tgt, vec)


def _norm_bwd(xin, dh, dxup, vec, fprev, name, deps=()):
    S, D = xin.shape
    tm = min(256, S)
    has_prev = fprev is not None

    def body(*refs):
        if has_prev:
            x_ref, dh_ref, up_ref, vec_ref, fp_ref, dx_ref, dp_ref, sums_ref = refs
        else:
            x_ref, dh_ref, up_ref, vec_ref, dx_ref, sums_ref = refs

        @pl.when(pl.program_id(0) == 0)
        def _():
            sums_ref[...] = jnp.zeros_like(sums_ref)

        g, scale = vec_ref[0:1, :], vec_ref[1:2, :]
        x = x_ref[...]
        r = lax.rsqrt(_rmean(x * x) + EPS)
        xn = x * r
        dhv = dh_ref[...]
        sums_ref[0:1, :] += _rsum(dhv)
        sums_ref[1:2, :] += _rsum(dhv * (xn * g))
        dm = dhv * (1.0 + scale)
        sums_ref[2:3, :] += _rsum(dm * xn)
        dxn = dm * g
        dx = up_ref[...] + r * (dxn - xn * _rmean(dxn * xn))
        dx_ref[...] = dx
        if has_prev:
            sums_ref[3:4, :] += _rsum(dx * fp_ref[...])
            dp_ref[...] = (dx * vec_ref[2:3, :]).astype(BF16)

    rs = _row_spec(tm, D)
    ins = [xin, dh, dxup, vec] + ([fprev] if has_prev else [])
    in_specs = [rs, rs, rs, _const_spec((8, D))] + ([rs] if has_prev else [])
    out_shape = [_sds((S, D), F32)] + ([_sds((S, D), BF16)] if has_prev else []) + [_sds((8, D), F32)]
    out_specs = [rs] + ([rs] if has_prev else []) + [_const_spec((8, D))]
    outs = _pcall(_after(body, len(ins), deps), grid=(S // tm,), in_specs=in_specs + [ANY] * len(deps),
                  out_specs=out_specs, out_shape=out_shape, name=name,
                  compiler_params=_params(("arbitrary",)))(*ins, *deps)
    return (outs[0], outs[1], outs[2]) if has_prev else (outs[0], None, outs[1])


def _gate_bwd(dmerged, z, ys, name, deps=()):
    S, D = dmerged.shape
    tm = min(256, S)
    ncol = z.shape[1] // D

    def body(dm_ref, g_ref, y_ref, dya_ref, dyb_ref, dyc_ref, dz_ref):
        n = pl.program_id(1)
        sg = _sigmoid(g_ref[...].astype(F32))
        dm = dm_ref[...].astype(F32)
        dy = (dm * sg).astype(BF16)
        for k, ref in enumerate((dya_ref, dyb_ref, dyc_ref)):
            @pl.when(n == k)
            def _(ref=ref):
                ref[...] = dy
        dz_ref[...] = (dm * y_ref[...].astype(F32) * (sg * (1.0 - sg))).astype(BF16)

    row = pl.BlockSpec((tm, D), lambda i, n: (i, 0))
    outs = _pcall(_after(body, 3, deps), grid=(S // tm, 3),
                  in_specs=[row, pl.BlockSpec((tm, D), lambda i, n: (i, 7 + n)),
                            pl.BlockSpec((None, tm, D), lambda i, n: (n, i, 0))] + [ANY] * len(deps),
                  out_specs=[row, row, row, pl.BlockSpec((tm, D), lambda i, n: (i, 7 + n))],
                  out_shape=[_sds((S, D), BF16)] * 3 + [_sds((S, ncol * D), BF16)], name=name,
                  compiler_params=_params(("parallel", "arbitrary")))(dmerged, z, ys, *deps)
    return outs[:3], outs[3]


def _mixer_bwd(z, dacts, dz, wsh, sgu_ln, wtril, wtril_t, bias_full, cw, cvec, name):
    S = z.shape[0]
    D = wsh.shape[1]
    tm = CHUNK
    nt = S // tm
    hb = tm // HALO

    def body(zc, zp, da_ref, db_ref, dc_ref, wsh_ref, sln_ref, wt_ref, wtt_ref, bias_ref, cw_ref, cv_ref, _dz_in,
             dz_ref, vec_ref, dcw_ref, dws_ref, dbs_ref, pe, ge, dqe, dce, gr, dcr, cbuf, dcw8):
        i = pl.program_id(0)
        rb = nt - 1 - i

        @pl.when(i == 0)
        def _():
            vec_ref[...] = jnp.zeros_like(vec_ref)
            dcw8[...] = jnp.zeros_like(dcw8)
            dws_ref[...] = jnp.zeros_like(dws_ref)
            dbs_ref[...] = jnp.zeros_like(dbs_ref)
            dqe[tm:tm + HALO, :] = jnp.zeros((HALO, D), F32)
            dce[tm:tm + HALO, :] = jnp.zeros((HALO, D), F32)

        keep = (rb > 0).astype(F32)

        def col(n):
            return zc[:, n * D:(n + 1) * D].astype(F32)

        def pcol(n):
            return zp[:, n * D:(n + 1) * D].astype(F32)

        c_a, x_a = col(1), col(2)
        pe[0:HALO, :] = keep * (pcol(1) * pcol(2))
        pe[HALO:HALO + tm, :] = c_a * x_a
        q = wsh_ref[0:1, :] * pe[HALO - 2:HALO - 2 + tm, :]
        q = q + wsh_ref[1:2, :] * pe[HALO - 1:HALO - 1 + tm, :]
        q = q + wsh_ref[2:3, :] * pe[HALO:HALO + tm, :]
        dact = da_ref[...].astype(F32)
        dz_ref[:, 0:D] = (dact * q).astype(BF16)
        dq = dact * col(0)
        dqe[0:tm, :] = dq
        dp = wsh_ref[2:3, :] * dq + wsh_ref[1:2, :] * dqe[1:1 + tm, :] + wsh_ref[0:1, :] * dqe[2:2 + tm, :]
        dz_ref[:, D:2 * D] = (dp * x_a).astype(BF16)
        dz_ref[:, 2 * D:3 * D] = (dp * c_a).astype(BF16)
        for k in range(SHORT_K):
            o = HALO - (SHORT_K - 1) + k
            vec_ref[k:k + 1, :] += _rsum(dq * pe[o:o + tm, :])
        u, v = col(3), col(4)
        gu, tu = _gelu(u)
        gv, tv = _gelu(v)
        d = gv - _rmean(gv)
        rstd = lax.rsqrt(_rmean(d * d) + EPS)
        nrm = d * rstd
        vnb = (nrm * sln_ref[0:1, :] + sln_ref[1:2, :]).astype(BF16)
        dact = db_ref[...].astype(F32)
        dvn_parts, dgu_parts = [], []
        for g in range(NG):
            cs = slice(g * LANE, (g + 1) * LANE)
            vg = vnb[:, cs]
            mixed = jnp.dot(wt_ref[g], vg, preferred_element_type=F32) + bias_ref[:, cs]
            dgu_parts.append(dact[:, cs] * mixed)
            dmixed = dact[:, cs] * gu[:, cs]
            dmb = dmixed.astype(BF16)
            dws_ref[g] += lax.dot_general(dmb, vg, (((1,), (1,)), ((), ())), preferred_element_type=F32)
            dbs_ref[g] += jnp.broadcast_to(jnp.sum(dmixed, axis=1, keepdims=True), (CHUNK, LANE))
            dvn_parts.append(jnp.dot(wtt_ref[g], dmb, preferred_element_type=F32))
        dgu = jnp.concatenate(dgu_parts, axis=1)
        dvn = jnp.concatenate(dvn_parts, axis=1)
        dz_ref[:, 3 * D:4 * D] = (dgu * _dgelu(u, tu)).astype(BF16)
        vec_ref[3:4, :] += _rsum(dvn * nrm)
        vec_ref[4:5, :] += _rsum(dvn)
        dn = dvn * sln_ref[0:1, :]
        dgv = rstd * (dn - _rmean(dn) - nrm * _rmean(dn * nrm))
        dz_ref[:, 4 * D:5 * D] = (dgv * _dgelu(v, tv)).astype(BF16)
        a_c = col(5)
        sg = _sigmoid(col(6))
        ge[0:HALO, :] = keep * (pcol(5) * _sigmoid(pcol(6)))
        ge[HALO:HALO + tm, :] = a_c * sg
        _fill_shifted(ge, gr)
        o0 = HALO - (CFM_K - 1)
        _causal_conv(cw_ref, range(CFM_K), cv_ref[0:1, :], ge, gr, range(o0, o0 + CFM_K), tm, cbuf)
        conv = cbuf[...]
        d = conv - _rmean(conv)
        rstd = lax.rsqrt(_rmean(d * d) + EPS)
        nrm = d * rstd
        ln = nrm * cv_ref[1:2, :] + cv_ref[2:3, :]
        sl = _sigmoid(ln)
        dln = dc_ref[...].astype(F32) * (sl * (1.0 + ln * (1.0 - sl)))
        vec_ref[6:7, :] += _rsum(dln * nrm)
        vec_ref[7:8, :] += _rsum(dln)
        dn = dln * cv_ref[1:2, :]
        dconv = rstd * (dn - _rmean(dn) - nrm * _rmean(dn * nrm))
        vec_ref[5:6, :] += _rsum(dconv)
        dce[0:tm, :] = dconv
        _fill_shifted(dce, dcr)
        _causal_conv(cw_ref, range(CFM_K), None, dce, dcr, [CFM_K - 1 - k for k in range(CFM_K)], tm, cbuf)
        dglu = cbuf[...]
        for cb in range(D // LANE):
            cs = slice(cb * LANE, (cb + 1) * LANE)
            dcv = dce[0:tm, cs]
            for k in range(CFM_K):
                prod = dcv * _rows_at(ge, gr, o0 + k, tm, cs)
                dcw8[k, :, cs] += jnp.sum(prod.reshape(tm // 8, 8, LANE), axis=0)

        @pl.when(i == nt - 1)
        def _():
            dcw_ref[...] = jnp.sum(dcw8[...], axis=1)
        dz_ref[:, 5 * D:6 * D] = (dglu * sg).astype(BF16)
        dz_ref[:, 6 * D:7 * D] = (dglu * a_c * (sg * (1.0 - sg))).astype(BF16)
        dqe[tm:tm + HALO, :] = dqe[0:HALO, :]
        dce[tm:tm + HALO, :] = dce[0:HALO, :]

    rev = lambda i: (nt - 1 - i, 0)
    rs = pl.BlockSpec((tm, D), rev)
    cur = pl.BlockSpec((tm, 7 * D), rev)
    prev = pl.BlockSpec((HALO, 7 * D), lambda i: (jnp.maximum((nt - 1 - i) * hb - 1, 0), 0))
    ext = pltpu.VMEM((HALO + tm, D), F32)
    outs = _pcall(
        body, grid=(nt,),
        in_specs=[cur, prev, rs, rs, rs, _const_spec((8, D)), _const_spec((8, D)), _const_spec((NG, CHUNK, CHUNK)),
                  _const_spec((NG, CHUNK, CHUNK)), _const_spec((CHUNK, D)), _const_spec((HALO, D)), _const_spec((8, D)),
                  ANY],
        out_specs=[cur, _const_spec((8, D)), _const_spec((HALO, D)), _const_spec((NG, CHUNK, CHUNK)),
                   _const_spec((NG, CHUNK, LANE))],
        out_shape=[_sds(dz.shape, BF16), _sds((8, D), F32), _sds((HALO, D), F32), _sds((NG, CHUNK, CHUNK), F32),
                   _sds((NG, CHUNK, LANE), F32)],
        scratch_shapes=[ext, ext, ext, ext, pltpu.VMEM((7, HALO + tm, D), F32), pltpu.VMEM((7, HALO + tm, D), F32),
                        pltpu.VMEM((tm, D), F32), pltpu.VMEM((HALO, 8, D), F32)],
        input_output_aliases={12: 0}, name=name,
        compiler_params=_params(("arbitrary",)))(z, z, *dacts, wsh, sgu_ln, wtril, wtril_t, bias_full, cw, cvec, dz)
    return outs


def _ada_fwd(c_all, w_ada_loc, name):
    nb, D = c_all.shape
    L, _, nc = w_ada_loc.shape

    def body(c_ref, w_ref, o_ref, ca_ref):
        cv = c_ref[...]
        ca = cv * _sigmoid(cv)
        ca_ref[...] = ca
        o_ref[...] = jnp.dot(ca.astype(BF16), w_ref[...].astype(BF16), preferred_element_type=F32)

    return _pcall(body, grid=(L,),
                  in_specs=[_const_spec((nb, D)), pl.BlockSpec((None, D, nc), lambda l: (l, 0, 0))],
                  out_specs=[pl.BlockSpec((None, nb, nc), lambda l: (l, 0, 0)), _const_spec((nb, D))],
                  out_shape=[_sds((L, nb, nc), F32), _sds((nb, D), F32)], name=name,
                  compiler_params=_params(("arbitrary",)))(c_all, w_ada_loc)


def _adamw(w, g, m, v):
    m = ADAM_B1 * m + (1.0 - ADAM_B1) * g
    v = ADAM_B2 * v + (1.0 - ADAM_B2) * (g * g)
    m_hat = m / (1.0 - ADAM_B1 ** ADAM_STEP)
    v_hat = v / (1.0 - ADAM_B2 ** ADAM_STEP)
    delta = -ADAM_LR * (m_hat / (jnp.sqrt(v_hat) + ADAM_EPS) + ADAM_WD * w)
    return delta, m, v


def _tile_rows(R, C, align=8):
    cap = max(align, (1536 * 1024) // (4 * C))
    best = None
    for t in range(align, R + 1, align):
        if R % t == 0 and t <= cap:
            best = t
    return R if best is None else best


def _adam_ada(ct, dm, w, m, v, name):
    L, D, nc = w.shape
    nb = ct.shape[1]
    tr = _tile_rows(D, nc)

    def body(ct_ref, dm_ref, w_ref, m_ref, v_ref, g_ref, d_ref, mo_ref, vo_ref):
        g = ct_ref[:, 0:1] * dm_ref[0:1, :]
        for b in range(1, nb):
            g = g + ct_ref[:, b:b + 1] * dm_ref[b:b + 1, :]
        g_ref[...] = g
        d_ref[...], mo_ref[...], vo_ref[...] = _adamw(w_ref[...], g, m_ref[...], v_ref[...])

    ws = pl.BlockSpec((None, tr, nc), lambda l, r: (l, r, 0))
    return _pcall(body, grid=(L, D // tr),
                  in_specs=[pl.BlockSpec((tr, nb), lambda l, r: (r, 0)), pl.BlockSpec((None, nb, nc), lambda l, r: (l, 0, 0)),
                            ws, ws, ws],
                  out_specs=[ws] * 4, out_shape=[_sds(w.shape, F32)] * 4, name=name,
                  compiler_params=_params(("parallel", "parallel")))(ct, dm, w, m, v)


def _adam_small(parts, w, m, v, name, deps=()):
    n, R, C = parts.shape
    tr = _tile_rows(R, C * n // 2)

    def body(p_ref, w_ref, m_ref, v_ref, g_ref, d_ref, mo_ref, vo_ref):
        g = p_ref[0]
        for j in range(1, n):
            g = g + p_ref[j]
        g_ref[...] = g
        d_ref[...], mo_ref[...], vo_ref[...] = _adamw(w_ref[...], g, m_ref[...], v_ref[...])

    ws = pl.BlockSpec((tr, C), lambda r: (r, 0))
    return _pcall(_after(body, 4, deps), grid=(R // tr,),
                  in_specs=[pl.BlockSpec((n, tr, C), lambda r: (0, r, 0)), ws, ws, ws] + [ANY] * len(deps),
                  out_specs=[ws] * 4, out_shape=[_sds((R, C), F32)] * 4, name=name,
                  compiler_params=_params(("parallel",)))(parts, w, m, v, *deps)


def _adam_plain(g, w, m, v, name):
    R, C = w.shape

    def body(g_ref, w_ref, m_ref, v_ref, d_ref, mo_ref, vo_ref):
        d_ref[...], mo_ref[...], vo_ref[...] = _adamw(w_ref[...], g_ref[...], m_ref[...], v_ref[...])

    ws = _const_spec((R, C))
    return _pcall(body, grid=(1,), in_specs=[ws] * 4, out_specs=[ws] * 3, out_shape=[_sds((R, C), F32)] * 3, name=name,
                  compiler_params=_params(("arbitrary",)))(g, w, m, v)


def _pair_sum(G, R1, my_c, name):
    n, R, C = G.shape
    half = n // 2
    tr = _tile_rows(R, C, align=16)

    def body(c_ref, g_ref, r_ref, o_ref):
        o_ref[...] = (g_ref[...].astype(F32) + r_ref[...].astype(F32)).astype(o_ref.dtype)

    blk = (None, tr, C)
    gs = pltpu.PrefetchScalarGridSpec(
        num_scalar_prefetch=1, grid=(half, R // tr),
        in_specs=[pl.BlockSpec(blk, lambda p, r, c: (2 * p + c[0], r, 0)), pl.BlockSpec(blk, lambda p, r, c: (p, r, 0))],
        out_specs=pl.BlockSpec(blk, lambda p, r, c: (p, r, 0)))
    return _pcall(body, grid_spec=gs, out_shape=_sds((half, R, C), G.dtype), name=name,
                  compiler_params=_params(("parallel", "parallel")))(my_c, G, R1)


def _adam_big(P, R2, my_chip, w, m, v, layer, prev, name, deps=()):
    _, R, C = P.shape
    nrecv = R2.shape[0]
    tr = _tile_rows(R, C, align=16)

    def body(p_sm, p_ref, r_ref, w_ref, m_ref, v_ref, *rest):
        g_ref, d_ref, mo_ref, vo_ref = rest[-4:]
        g = p_ref[...].astype(F32)
        for k in range(nrecv):
            g = g + r_ref[k].astype(F32)
        g_ref[...] = g
        d_ref[...], mo_ref[...], vo_ref[...] = _adamw(w_ref[...], g, m_ref[...], v_ref[...])

    ws = pl.BlockSpec((None, tr, C), lambda r, p: (layer, r, 0))
    held = [] if prev is None else list(prev)
    gs = pltpu.PrefetchScalarGridSpec(
        num_scalar_prefetch=1, grid=(R // tr,),
        in_specs=[pl.BlockSpec((None, tr, C), lambda r, p: (p[0], r, 0)),
                  pl.BlockSpec((nrecv, tr, C), lambda r, p: (0, r, 0)), ws, ws, ws] + [ANY] * (len(held) + len(deps)),
        out_specs=[ws] * 4)
    alias = {6 + i: i for i in range(len(held))}
    return _pcall(body, grid_spec=gs, out_shape=[_sds(w.shape, F32)] * 4, name=name, input_output_aliases=alias,
                  compiler_params=_params(("parallel",)))(my_chip, P, R2, w, m, v, *held, *deps)


def _place():
    return lax.axis_index("x"), lax.axis_index("y"), lax.axis_index("c")


def _all_gather(shards, name, deps=()):
    n = len(shards)

    def body(*refs):
        ins, outs = refs[:n], refs[n:2 * n]
        send_sems, recv_sems, local_sems = refs[2 * n:]
        x, y, c = _place()
        me, sibling = (x, y, c), (x, y, 1 - c)
        chips = [(1 - x, y), (x, 1 - y), (1 - x, 1 - y)]

        def slot(a, px, py, pc):
            return outs[a].at[4 * px + 2 * py + pc]

        def copy(a, k, block, to, src=None):
            return pltpu.make_async_remote_copy(
                src_ref=slot(a, *block) if src is None else src, dst_ref=slot(a, *block),
                send_sem=send_sems.at[7 * a + k], recv_sem=recv_sems.at[7 * a + k], device_id=to, device_id_type=MESH)

        mine = [pltpu.make_async_copy(ins[a], slot(a, *me), local_sems.at[a]) for a in range(n)]
        for cp in mine:
            cp.start()
        first = []
        for a in range(n):
            first.append(copy(a, 0, me, sibling, src=ins[a]))
            first += [copy(a, 1 + j, me, (*chip, c), src=ins[a]) for j, chip in enumerate(chips)]
        for cp in first:
            cp.start()
        passed = []
        for j, chip in enumerate(chips):
            for a in range(n):
                copy(a, 1 + j, (*chip, c), me).wait_recv()
                fwd = copy(a, 4 + j, (*chip, c), sibling)
                fwd.start()
                passed.append(fwd)
        for a in range(n):
            copy(a, 0, sibling, me).wait_recv()
        for j, chip in enumerate(chips):
            for a in range(n):
                copy(a, 4 + j, (*chip, 1 - c), me).wait_recv()
        for cp in first + passed:
            cp.wait_send()
        for cp in mine:
            cp.wait()

    outs = _pcall(_after(body, n, deps), in_specs=[ANY] * (n + len(deps)), out_specs=[ANY] * n,
                  out_shape=[_sds((NDEV,) + s.shape, s.dtype) for s in shards],
                  scratch_shapes=[pltpu.SemaphoreType.DMA((7 * n,)), pltpu.SemaphoreType.DMA((7 * n,)),
                                  pltpu.SemaphoreType.DMA((n,))], name=name)(*shards, *deps)
    return list(outs)


HBM = pl.BlockSpec(memory_space=pltpu.HBM)
SEM = pl.BlockSpec(memory_space=pltpu.SEMAPHORE)


def _copies(plan, refs, send_sems, recv_sems):
    return [pltpu.make_async_remote_copy(src_ref=s, dst_ref=d, send_sem=send_sems.at[k], recv_sem=recv_sems.at[k],
                                         device_id=dev, device_id_type=MESH)
            for k, (s, d, dev) in enumerate(plan(refs, *_place()))]


def _xfer_start(bufs, ncopies, plan, name, deps=()):
    n = len(bufs)

    def body(*refs):
        for cp in _copies(plan, refs[:n], refs[n], refs[n + 1]):
            cp.start()
        token = refs[2 * n + 2]
        token[...] = jnp.zeros_like(token)

    outs = _pcall(
        _after(body, n, deps), name=name,
        out_shape=(pltpu.SemaphoreType.DMA((ncopies,)), pltpu.SemaphoreType.DMA((ncopies,)),
                   *[pltpu.HBM(b.shape, b.dtype) for b in bufs], _sds((8, LANE), F32)),
        in_specs=[HBM] * n + [ANY] * len(deps),
        out_specs=(SEM, SEM, *[HBM] * n, pl.BlockSpec(memory_space=pltpu.VMEM)),
        input_output_aliases={i: 2 + i for i in range(n)},
        compiler_params=pltpu.CompilerParams(has_side_effects=pltpu.SideEffectType.DATAFLOW_SIDE_EFFECTING),
    )(*[pltpu.with_memory_space_constraint(b, pltpu.HBM) for b in bufs], *deps)
    return (outs[0], outs[1]), list(outs[2:2 + n]), outs[2 + n]


def _xfer_wait(sems, bufs, plan, after, name):
    n = len(bufs)

    def body(*refs):
        for cp in _copies(plan, refs[:n], refs[n], refs[n + 1]):
            cp.wait_send()
            cp.wait_recv()

    outs = _pcall(
        body, name=name, out_shape=tuple(pltpu.HBM(b.shape, b.dtype) for b in bufs),
        in_specs=[HBM] * n + [SEM, SEM, ANY], out_specs=tuple([HBM] * n), input_output_aliases={i: i for i in range(n)},
        compiler_params=pltpu.CompilerParams(has_side_effects=pltpu.SideEffectType.DATAFLOW_SIDE_EFFECTING),
    )(*bufs, *sems, after)
    return list(outs)


def _chips_of(x, y):
    return [(1 - x, y), (x, 1 - y), (1 - x, 1 - y)]


def _gather_plan1(n):
    def plan(refs, x, y, c):
        out = []
        for a in range(n):
            blk = refs[a].at[4 * x + 2 * y + c]
            out.append((blk, blk, (x, y, 1 - c)))
            out += [(blk, blk, (px, py, c)) for px, py in _chips_of(x, y)]
        return out
    return plan


def _gather_plan2(n):
    def plan(refs, x, y, c):
        out = []
        for a in range(n):
            for px, py in _chips_of(x, y):
                blk = refs[a].at[4 * px + 2 * py + c]
                out.append((blk, blk, (x, y, 1 - c)))
        return out
    return plan


def _gather_start(shards, dev, name, deps=()):
    lands = [lax.dynamic_update_slice(lax.empty((NDEV,) + s.shape, s.dtype), s[None], (dev,) + (0,) * s.ndim)
             for s in shards]
    n = len(shards)
    sems, lands, tok = _xfer_start(lands, 4 * n, _gather_plan1(n), name + "_p1_start", deps)
    return dict(sems=sems, lands=lands, tok=tok, n=n)


def _gather_mid(st, after, name):
    n = st["n"]
    lands = _xfer_wait(st["sems"], st["lands"], _gather_plan1(n), after, name + "_p1_wait")
    sems, lands, tok = _xfer_start(lands, 3 * n, _gather_plan2(n), name + "_p2_start")
    return dict(sems=sems, lands=lands, tok=tok, n=n)


def _gather_finish(st, after, name):
    return _xfer_wait(st["sems"], st["lands"], _gather_plan2(st["n"]), after, name + "_p2_wait")


def _scatter_plan1(n):
    def plan(refs, x, y, c):
        return [(refs[a].at[2 * p + 1 - c], refs[n + a].at[p], (x, y, 1 - c)) for a in range(n) for p in range(NCHIP)]
    return plan


def _scatter_plan2(n):
    def plan(refs, x, y, c):
        return [(refs[a].at[2 * px + py], refs[n + a].at[j], (px, py, c))
                for a in range(n) for j, (px, py) in enumerate(_chips_of(x, y))]
    return plan


def _scatter_start(Gs, name):
    n = len(Gs)
    R1s = [lax.empty((NCHIP,) + g.shape[1:], g.dtype) for g in Gs]
    sems, bufs, tok = _xfer_start(list(Gs) + R1s, NCHIP * n, _scatter_plan1(n), name + "_s1_start")
    return dict(sems=sems, bufs=bufs, tok=tok, n=n)


def _scatter_mid(st, after, my_c, name):
    n = st["n"]
    bufs = _xfer_wait(st["sems"], st["bufs"], _scatter_plan1(n), after, name + "_s1_wait")
    Ps = [_pair_sum(bufs[a], bufs[n + a], my_c, f"{name}_pair_sum{a}") for a in range(n)]
    R2s = [lax.empty((3,) + p.shape[1:], p.dtype) for p in Ps]
    sems, bufs, tok = _xfer_start(Ps + R2s, 3 * n, _scatter_plan2(n), name + "_s2_start")
    return dict(sems=sems, bufs=bufs, tok=tok, n=n)


def _scatter_finish(st, after, name):
    n = st["n"]
    bufs = _xfer_wait(st["sems"], st["bufs"], _scatter_plan2(n), after, name + "_s2_wait")
    return bufs[:n], bufs[n:]


SMALL_ROWS = {"norm1_g": (0, 1), "norm2_g": (1, 1), "sgu_ln_g": (2, 1), "sgu_ln_b": (3, 1), "cfm_conv_b": (4, 1),
              "cfm_ln_g": (5, 1), "cfm_ln_b": (6, 1), "b_sgu": (7, 1), "w_sgu": (8, 128), "b_ada": (136, N_MOD),
              "w_short": (142, SHORT_K), "cfm_conv_w": (145, CFM_K)}
ROWS_PER_LAYER = 176
FINAL_ROW = DEPTH * ROWS_PER_LAYER
PACK_ROWS = 360


def _pack(get, D, layers=tuple(range(DEPTH)), tail=True):
    parts = []
    for l in layers:
        for name, (_, nrows) in SMALL_ROWS.items():
            a = get(name, l)
            parts.append(jnp.zeros((nrows * D,), F32) if a is None else a.astype(F32).reshape(nrows * D))
    if tail:
        for name in ("final_g", "loss"):
            a = get(name, None)
            parts.append(jnp.zeros((D,), F32) if a is None else a.astype(F32).reshape(D))
        parts.append(jnp.zeros(((PACK_ROWS - FINAL_ROW - 2) * D,), F32))
    return jnp.concatenate(parts).reshape(-1, D)


def _unpack(pack, name, shape):
    D = pack.shape[1]
    r0, nrows = SMALL_ROWS[name]
    return jnp.stack([pack[l * ROWS_PER_LAYER + r0:l * ROWS_PER_LAYER + r0 + nrows] for l in range(DEPTH)]).reshape(shape)


def _mm_tiles(S):
    return min(512, S), min(1024, S)


def kernel(x, c, w_ada, b_ada, norm1_g, w_in, w_short, w_a_out, sgu_ln_g, sgu_ln_b, w_sgu, b_sgu, w_b_out, cfm_conv_w, cfm_conv_b, cfm_ln_g, cfm_ln_b, w_c_out, w_o, norm2_g, w_ffn_in, w_ffn_out, final_g, loss_target, m_w_ada, m_b_ada, m_norm1_g, m_w_in, m_w_short, m_w_a_out, m_sgu_ln_g, m_sgu_ln_b, m_w_sgu, m_b_sgu, m_w_b_out, m_cfm_conv_w, m_cfm_conv_b, m_cfm_ln_g, m_cfm_ln_b, m_w_c_out, m_w_o, m_norm2_g, m_w_ffn_in, m_w_ffn_out, m_final_g, v_w_ada, v_b_ada, v_norm1_g, v_w_in, v_w_short, v_w_a_out, v_sgu_ln_g, v_sgu_ln_b, v_w_sgu, v_b_sgu, v_w_b_out, v_cfm_conv_w, v_cfm_conv_b, v_cfm_ln_g, v_cfm_ln_b, v_w_c_out, v_w_o, v_norm2_g, v_w_ffn_in, v_w_ffn_out, v_final_g):
    W = dict(w_ada=w_ada, b_ada=b_ada, norm1_g=norm1_g, w_in=w_in, w_short=w_short, w_a_out=w_a_out, sgu_ln_g=sgu_ln_g,
             sgu_ln_b=sgu_ln_b, w_sgu=w_sgu, b_sgu=b_sgu, w_b_out=w_b_out, cfm_conv_w=cfm_conv_w, cfm_conv_b=cfm_conv_b,
             cfm_ln_g=cfm_ln_g, cfm_ln_b=cfm_ln_b, w_c_out=w_c_out, w_o=w_o, norm2_g=norm2_g, w_ffn_in=w_ffn_in,
             w_ffn_out=w_ffn_out, final_g=final_g)
    Mo = dict(w_ada=m_w_ada, b_ada=m_b_ada, norm1_g=m_norm1_g, w_in=m_w_in, w_short=m_w_short, w_a_out=m_w_a_out,
              sgu_ln_g=m_sgu_ln_g, sgu_ln_b=m_sgu_ln_b, w_sgu=m_w_sgu, b_sgu=m_b_sgu, w_b_out=m_w_b_out,
              cfm_conv_w=m_cfm_conv_w, cfm_conv_b=m_cfm_conv_b, cfm_ln_g=m_cfm_ln_g, cfm_ln_b=m_cfm_ln_b,
              w_c_out=m_w_c_out, w_o=m_w_o, norm2_g=m_norm2_g, w_ffn_in=m_w_ffn_in, w_ffn_out=m_w_ffn_out,
              final_g=m_final_g)
    Vo = dict(w_ada=v_w_ada, b_ada=v_b_ada, norm1_g=v_norm1_g, w_in=v_w_in, w_short=v_w_short, w_a_out=v_w_a_out,
              sgu_ln_g=v_sgu_ln_g, sgu_ln_b=v_sgu_ln_b, w_sgu=v_w_sgu, b_sgu=v_b_sgu, w_b_out=v_w_b_out,
              cfm_conv_w=v_cfm_conv_w, cfm_conv_b=v_cfm_conv_b, cfm_ln_g=v_cfm_ln_g, cfm_ln_b=v_cfm_ln_b,
              w_c_out=v_w_c_out, w_o=v_w_o, norm2_g=v_norm2_g, w_ffn_in=v_w_ffn_in, w_ffn_out=v_w_ffn_out,
              final_g=v_final_g)
    order = ["w_ada", "b_ada", "norm1_g", "w_in", "w_short", "w_a_out", "sgu_ln_g", "sgu_ln_b", "w_sgu", "b_sgu",
             "w_b_out", "cfm_conv_w", "cfm_conv_b", "cfm_ln_g", "cfm_ln_b", "w_c_out", "w_o", "norm2_g", "w_ffn_in",
             "w_ffn_out", "final_g"]

    assert DEPTH == 2, "the weight-gather schedule below is written for two layers"
    S, D = x.shape[1], x.shape[2]
    F2 = w_ffn_in.shape[2] * NDEV
    FF = F2 // 2
    xi, yi, ci = _place()
    dev = 4 * xi + 2 * yi + ci
    my_c = jnp.reshape(ci, (1,)).astype(jnp.int32)
    my_chip = jnp.reshape(2 * xi + yi, (1,)).astype(jnp.int32)
    tm, tm_big = _mm_tiles(S)
    x0 = x.reshape(S, D)
    tgt = loss_target.reshape(S, D)

    def shards_of(l):
        return [w_in[l].astype(BF16), w_a_out[l].astype(BF16), w_b_out[l].astype(BF16), w_c_out[l].astype(BF16),
                w_o[l].astype(BF16), w_ffn_in[l].astype(BF16), w_ffn_out[l].astype(BF16)]

    c_all = _all_gather([jnp.pad(c, ((0, 7), (0, 0)))], "ag_c")[0][:, 0, :]
    modpart, c_act = _ada_fwd(c_all, w_ada, "ada_fwd")
    ncol = modpart.shape[2]
    mg = _all_gather([modpart.reshape(DEPTH * NDEV, ncol)], "ag_mod")[0].reshape(NDEV, DEPTH, NDEV, ncol)
    mine = lax.dynamic_index_in_dim(mg, dev, axis=2, keepdims=False)
    mod = (jnp.transpose(mine, (1, 0, 2)).reshape(DEPTH, N_MOD * D) + b_ada).reshape(DEPTH, N_MOD, D)

    tril = jnp.tril(jnp.ones((CHUNK, CHUNK), dtype=bool))

    def layer_consts(l):
        wt = jnp.where(tril[None], w_sgu[l], 0.0).astype(BF16)
        return dict(
            wsh=jnp.pad(w_short_full[l], ((0, 8 - SHORT_K), (0, 0))),
            sgu_ln=_rows(sgu_ln_g[l], sgu_ln_b[l]),
            wtril=wt, wtril_t=jnp.swapaxes(wt, 1, 2),
            bias_full=jnp.repeat(b_sgu[l].T, LANE, axis=1),
            cw=jnp.pad(cfm_w_full[l], ((0, HALO - CFM_K), (0, 0))),
            cvec=_rows(cfm_conv_b[l], cfm_ln_g[l], cfm_ln_b[l]))

    ncs = w_short.shape[2]
    sw = _all_gather([w_short.reshape(DEPTH * SHORT_K, ncs), cfm_conv_w.reshape(DEPTH * CFM_K, ncs)], "ag_convw",
                     deps=(mod,))
    w_short_full = jnp.transpose(sw[0], (1, 0, 2)).reshape(DEPTH, SHORT_K, D)
    cfm_w_full = jnp.transpose(sw[1], (1, 0, 2)).reshape(DEPTH, CFM_K, D)

    def rest_of(g):
        return dict(w_a=g[0].reshape(1, D, D), w_b=g[1].reshape(1, D, D), w_c=g[2].reshape(1, D, D),
                    w_o=g[3].reshape(1, D, D), w_fi=jnp.transpose(g[4], (1, 0, 2)).reshape(1, D, F2),
                    w_fo=g[5].reshape(1, FF, D))

    ag_in0 = _gather_start(shards_of(0)[:1], dev, "ag_w_in0", deps=(cfm_w_full,))
    ag_rest0 = _gather_start(shards_of(0)[1:], dev, "ag_rest0", deps=(ag_in0["tok"],))
    ag_in0 = _gather_mid(ag_in0, ag_rest0["tok"], "ag_w_in0")
    Wg = [None, None]
    ag_l1 = None
    nin = w_in.shape[2]
    tn_in = nin if nin % 256 == 0 and nin <= 1280 else 256
    tn_fi = 512 if F2 % 512 == 0 else 256
    tn_ffn = 1408 if F2 % 1408 == 0 else tn_fi
    tn_dw = min(256, D)

    saved = []
    xcur, fprev, gprev = x0, None, None
    for l in range(DEPTH):
        sh1, sc1, g1, sh2, sc2, g2 = [mod[l, k] for k in range(N_MOD)]
        cl = layer_consts(l)
        vec1 = _rows(jnp.zeros((D,), F32) if gprev is None else gprev, norm1_g[l], sc1, sh1)
        if l == 0:
            xl, h, ht = _norm_fwd(xcur, fprev, vec1, f"norm1_fwd{l}", deps=(ag_in0["tok"],))
            Wg[0] = dict(w_in=_gather_finish(ag_in0, h, "ag_w_in0")[0])
        else:
            ag_l1 = _gather_mid(ag_l1, fprev, f"ag_w{l}")
            xl, h, ht = _norm_fwd(xcur, fprev, vec1, f"norm1_fwd{l}", deps=(ag_l1["tok"],))
            g = _gather_finish(ag_l1, h, f"ag_w{l}")
            Wg[l] = dict(w_in=g[0], **rest_of(g[1:]))
        wl = Wg[l]
        z = _mm_nn(h, wl["w_in"], BF16, tm_big, tn_in, D, f"mm_in{l}", w_outer=True)
        mix_deps = ()
        if l == 0:
            ag_rest0 = _gather_mid(ag_rest0, z, "ag_rest0")
            mix_deps = (ag_rest0["tok"],)
            if DEPTH > 1:
                ag_l1 = _gather_start(shards_of(1), dev, "ag_w1")
                mix_deps += (ag_l1["tok"],)
        acts, acts_t = _mixer_fwd(z, cl["wsh"], cl["sgu_ln"], cl["wtril"], cl["bias_full"], cl["cw"], cl["cvec"],
                                  f"mixer_fwd{l}", deps=mix_deps)
        if l == 0:
            wl.update(rest_of(_gather_finish(ag_rest0, acts[0], "ag_rest0")))
        merged, merged_t, ys = _branch_out(acts, [wl["w_a"][0], wl["w_b"][0], wl["w_c"][0]], z, f"branch_out{l}")
        o = _mm_nn(merged, wl["w_o"], F32, tm_big, D, D, f"mm_o{l}")
        x1, h2, h2t = _norm_fwd(xl, o, _rows(g1, norm2_g[l], sc2, sh2), f"norm2_fwd{l}")
        gu, act, act_t = _ffn_in_swiglu(h2, wl["w_fi"], tm, tn_ffn, f"mm_ffn_in{l}")
        f = _mm_nn(act, wl["w_fo"], F32, tm, D, FF, f"mm_ffn_out{l}")
        saved.append(dict(xl=xl, ht=ht, z=z, acts_t=acts_t, ys=ys, merged_t=merged_t, o=o, x1=x1, h2t=h2t, gu=gu,
                          act_t=act_t, f=f, consts=cl, mod=(sh1, sc1, g1, sh2, sc2, g2)))
        xcur, fprev, gprev = x1, f, g2

    last = saved[-1]
    dxup, dfb, fsums, loss_blk = _final_bwd(last["x1"], last["f"], tgt, _rows(last["mod"][5], final_g), "final_bwd")
    loss_row = jnp.pad(loss_blk[0, 0:1], (0, D - 1))
    dgate2_next = fsums[1]
    small = [dict() for _ in range(DEPTH)]
    dmods = [None] * DEPTH
    nfi = w_ffn_in.shape[2]
    early_names, late_names = ["w_ffn_out", "w_ffn_in", "w_o"], ["w_a_out", "w_b_out", "w_c_out", "w_in"]
    results = {n: None for n in early_names + late_names}

    def adam_group(names, Ps, R2s, l, deps=()):
        for n, p, r2 in zip(names, Ps, R2s):
            results[n] = _adam_big(p, r2, my_chip, W[n], Mo[n], Vo[n], l, results[n], f"adam_{n}{l}", deps)

    deferred = []
    late_prev = None
    ag_s1, gathered1 = None, None
    tk_w = min(2048, S)
    tn_dw_in = tn_in // 2 if tn_in == 1280 else tn_in
    for l in reversed(range(DEPTH)):
        sv, wl, cl = saved[l], Wg[l], saved[l]["consts"]
        sh1, sc1, g1, sh2, sc2, g2 = sv["mod"]
        dact = _mm_nt(dfb, wl["w_fo"], BF16, tm, FF, D, f"mm_dact{l}",
                      deps=() if late_prev is None else (late_prev["tok"], ag_s1["tok"]))
        g_fo = _mm_wgrad(sv["act_t"], dfb, 1, FF // 2, D, tk_w, f"mm_dw_ffn_out{l}")
        dgu = _swiglu_bwd(dact, sv["gu"], f"swiglu_bwd{l}")
        dh2 = _mm_nt(dgu, wl["w_fi"], F32, tm, D, F2, f"mm_dh2{l}")
        if late_prev is not None:
            deferred.append((late_names, *_scatter_finish(late_prev, dh2, f"rs_late{l + 1}"), l + 1))
            late_prev = None
        g_fi = _mm_wgrad(sv["h2t"], dgu, 1, D, tn_fi, S, f"mm_dw_ffn_in{l}")
        if ag_s1 is not None:
            ag_s1 = _gather_mid(ag_s1, g_fi, "ag_small1")
        dx1, dob, s2 = _norm_bwd(sv["x1"], dh2, dxup, _rows(norm2_g[l], sc2, g1), sv["o"], f"norm2_bwd{l}",
                                 deps=() if ag_s1 is None else (ag_s1["tok"],))
        dmerged = _mm_nt(dob, wl["w_o"], BF16, tm_big, D, D, f"mm_dmerged{l}")
        g_o = _mm_wgrad(sv["merged_t"], dob, 1, D, tn_dw, S, f"mm_dw_o{l}")
        early = _scatter_start([g_fo.reshape(NDEV, FF // NDEV, D),
                                jnp.transpose(g_fi.reshape(D, NDEV, nfi), (1, 0, 2)),
                                g_o.reshape(NDEV, D // NDEV, D)], f"rs_early{l}")
        dys, dz = _gate_bwd(dmerged, sv["z"], sv["ys"], f"gate_bwd{l}", deps=(early["tok"],))
        if ag_s1 is not None:
            gathered1 = _gather_finish(ag_s1, dys[0], "ag_small1")[0]
            ag_s1 = None
        early = _scatter_mid(early, dys[0], my_c, f"rs_early{l}")
        dacts, g_abc = [], []
        for n, key in enumerate(("w_a", "w_b", "w_c")):
            dacts.append(_mm_nt(dys[n], wl[key], BF16, tm_big, D, D, f"mm_dact_{key}{l}",
                                deps=(early["tok"],) if n == 0 else ()))
            g_abc.append(_mm_wgrad(sv["acts_t"][n], dys[n], 1, D, tn_dw, S, f"mm_d{key}{l}"))
        dz, mvec, dcw, dws, dbs = _mixer_bwd(sv["z"], dacts, dz, cl["wsh"], cl["sgu_ln"], cl["wtril"], cl["wtril_t"],
                                             cl["bias_full"], cl["cw"], cl["cvec"], f"mixer_bwd{l}")
        dh = _mm_nt(dz, wl["w_in"], F32, tm_big, D, tn_in, f"mm_dh{l}")
        g_in = _mm_wgrad(sv["ht"], dz, NDEV, D, tn_dw_in, S, f"mm_dw_in{l}")
        late = _scatter_start([g.reshape(NDEV, D // NDEV, D) for g in g_abc] + [g_in], f"rs_late{l}")
        if l > 0:
            pv = saved[l - 1]
            dxup, dfb, s1 = _norm_bwd(sv["xl"], dh, dx1, _rows(norm1_g[l], sc1, pv["mod"][5]), pv["f"], f"norm1_bwd{l}",
                                      deps=(late["tok"],))
        else:
            dxup, dfb, s1 = _norm_bwd(sv["xl"], dh, dx1, _rows(norm1_g[l], sc1), None, f"norm1_bwd{l}", deps=(late["tok"],))
        deferred.append((early_names, *_scatter_finish(early, dxup, f"rs_early{l}"), l))
        dmods[l] = jnp.stack([s1[0], s1[1], s2[3], s2[0], s2[1], dgate2_next])
        dgate2_next = s1[3]
        small[l] = dict(norm1_g=s1[2], norm2_g=s2[2], sgu_ln_g=mvec[3], sgu_ln_b=mvec[4], cfm_conv_b=mvec[5],
                        cfm_ln_g=mvec[6], cfm_ln_b=mvec[7], b_sgu=dbs[:, :, 0],
                        w_sgu=jnp.where(tril[None], dws, 0.0), b_ada=dmods[l], w_short=mvec[0:SHORT_K],
                        cfm_conv_w=dcw[0:CFM_K])
        small_get = lambda name, k: {"final_g": fsums[0], "loss": loss_row}.get(name) if k is None else small[k][name]
        if l > 0:
            late_prev = _scatter_mid(late, dxup, my_c, f"rs_late{l}")
            ag_s1 = _gather_start([_pack(small_get, D, layers=(l,), tail=True)], dev, "ag_small1", deps=(late_prev["tok"],))
    grad_x = dxup.reshape(x.shape)

    gathered0 = _all_gather([_pack(small_get, D, layers=(0,), tail=False)], "ag_small0", deps=(dxup,))[0]
    late_prev = _scatter_mid(late, gathered0, my_c, "rs_late0")
    gathered = jnp.concatenate([gathered0, gathered1], axis=1)
    sharded_small = ("w_short", "cfm_conv_w")
    def param_get(T):
        def get(name, l):
            if name == "final_g":
                return T[name]
            return None if name in sharded_small or name == "loss" else T[name][l]
        return get

    packs = [_pack(param_get(T), D) for T in (W, Mo, Vo)]
    sg, sd, sm, sv_ = _adam_small(gathered, *packs, name="adam_small", deps=(late_prev["tok"],))
    loss = sg[FINAL_ROW + 1, 0]
    out = {}
    for name in order:
        if name in SMALL_ROWS and name not in sharded_small:
            out[name] = tuple(_unpack(p, name, W[name].shape) for p in (sg, sd, sm, sv_))
    out["final_g"] = tuple(p[FINAL_ROW] for p in (sg, sd, sm, sv_))

    def my_cols(name):
        full = _unpack(sg, name, (DEPTH, SMALL_ROWS[name][1], D))
        return lax.dynamic_slice_in_dim(full, dev * ncs, ncs, axis=2)

    gcs = jnp.concatenate([my_cols("w_short").reshape(-1, ncs), my_cols("cfm_conv_w").reshape(-1, ncs)])
    ncr = gcs.shape[0]
    padr = (-ncr) % 8
    cat = lambda T: jnp.pad(jnp.concatenate([T["w_short"].reshape(-1, ncs), T["cfm_conv_w"].reshape(-1, ncs)]), ((0, padr), (0, 0)))
    cd, cm, cv = _adam_plain(jnp.pad(gcs, ((0, padr), (0, 0))), cat(W), cat(Mo), cat(Vo), "adam_convw")
    nsh = DEPTH * SHORT_K
    out["w_short"] = tuple(a[0:nsh].reshape(w_short.shape) for a in (gcs, cd, cm, cv))
    out["cfm_conv_w"] = tuple(a[nsh:ncr].reshape(cfm_conv_w.shape) for a in (gcs, cd, cm, cv))

    dm_all = jnp.stack([gathered[:, l * ROWS_PER_LAYER + 136:l * ROWS_PER_LAYER + 136 + N_MOD, :].reshape(NDEV, N_MOD * D)
                        for l in range(DEPTH)])
    dm_mine = lax.dynamic_slice_in_dim(dm_all, dev * ncol, ncol, axis=2)
    out["w_ada"] = tuple(_adam_ada(jnp.transpose(c_act), dm_mine, w_ada, m_w_ada, v_w_ada, "adam_ada"))

    for names, Ps, R2s, l in deferred:
        adam_group(names, Ps, R2s, l, deps=(late_prev["tok"],))
    adam_group(late_names, *_scatter_finish(late_prev, results["w_o"][0], "rs_late0"), 0)
    for n in early_names + late_names:
        out[n] = tuple(results[n])

    grads = [out[n][0] for n in order]
    deltas = [out[n][1] for n in order]
    new_m = [out[n][2] for n in order]
    new_v = [out[n][3] for n in order]
    return (loss, grad_x, *grads, *deltas, *new_m, *new_v)
```

```python
import functools
import math

import jax
import jax.numpy as jnp
from jax import lax
from jax.experimental import pallas as pl
from jax.experimental.pallas import tpu as pltpu

F32, BF16 = jnp.float32, jnp.bfloat16
NDEV = 8
NCHIP = NDEV // 2
DEPTH = 2
EPS = 1e-6
CHUNK = 128
NG = 8
SHORT_K = 3
CFM_K = 31
HALO = 32
N_MOD = 6
LANE = 128
VMEM_LIMIT = 56 * 1024 * 1024
ADAM_LR, ADAM_B1, ADAM_B2, ADAM_EPS, ADAM_WD, ADAM_STEP = 0.001, 0.9, 0.999, 1e-08, 0.01, 10
_G0 = math.sqrt(2.0 / math.pi)
_G1 = 0.044715
MESH = pl.DeviceIdType.MESH
ANY = pl.BlockSpec(memory_space=pl.ANY)


def _pcall(body, **kw):
    return pl.pallas_call(body, **kw)


def _params(sem=None):
    return pltpu.CompilerParams(dimension_semantics=sem, vmem_limit_bytes=VMEM_LIMIT)


def _sds(shape, dtype):
    return jax.ShapeDtypeStruct(tuple(shape), dtype)


def _mm_body(dims, nk, out_f32):
    def body(a_ref, b_ref, o_ref, *scr):
        k = pl.program_id(2)
        part = lax.dot_general(a_ref[...], b_ref[...], dims, preferred_element_type=F32)
        if nk == 1:
            o_ref[...] = part.reshape(o_ref.shape).astype(o_ref.dtype)
        elif out_f32:
            @pl.when(k == 0)
            def _():
                o_ref[...] = part.reshape(o_ref.shape)

            @pl.when(k > 0)
            def _():
                o_ref[...] += part.reshape(o_ref.shape)
        else:
            acc = scr[0]

            @pl.when(k == 0)
            def _():
                acc[...] = part

            @pl.when(k > 0)
            def _():
                acc[...] += part

            @pl.when(k == nk - 1)
            def _():
                o_ref[...] = acc[...].astype(o_ref.dtype)
    return body


def _after(body, n_in, deps):
    nd = len(deps)
    if nd == 0:
        return body

    def ordered(*refs):
        return body(*refs[:n_in], *refs[n_in + nd:])
    return ordered


def _mm_call(body, grid, in_specs, out_spec, out_shape, acc_shape, name, deps=()):
    scratch = [] if acc_shape is None else [pltpu.VMEM(acc_shape, F32)]
    return _pcall(_after(body, 2, deps), grid=grid, in_specs=in_specs + [ANY] * len(deps), out_specs=out_spec,
                  out_shape=out_shape, scratch_shapes=scratch, name=name,
                  compiler_params=_params(("parallel", "parallel", "arbitrary")))


def _mm_nn(a, b3, out_dtype, tm, tn, tk, name, w_outer=False, deps=()):
    M, K = a.shape
    G, _, Nb = b3.shape
    npb, nk = Nb // tn, K // tk
    out_f32 = out_dtype == F32
    body = _mm_body((((1,), (0,)), ((), ())), nk, out_f32)
    if w_outer:
        grid = (G * npb, M // tm, nk)
        ij = lambda p, q: (q, p)
    else:
        grid = (M // tm, G * npb, nk)
        ij = lambda p, q: (p, q)

    def a_map(p, q, k):
        i, j = ij(p, q)
        return (i, k)

    def b_map(p, q, k):
        i, j = ij(p, q)
        return (j // npb, k, j % npb)

    def o_map(p, q, k):
        return ij(p, q)

    def wrapped(a_ref, b_ref, o_ref, *scr):
        body(a_ref, b_ref, o_ref, *scr)

    return _mm_call(wrapped, grid, [pl.BlockSpec((tm, tk), a_map), pl.BlockSpec((None, tk, tn), b_map)],
                    pl.BlockSpec((tm, tn), o_map), _sds((M, G * Nb), out_dtype),
                    None if (nk == 1 or out_f32) else (tm, tn), name, deps)(a, b3, *deps)


def _mm_nt(a, b3, out_dtype, tm, tn, tk, name, deps=()):
    M, _ = a.shape
    G, Ko, Nb = b3.shape
    kpb = Nb // tk
    nk = G * kpb
    out_f32 = out_dtype == F32
    body = _mm_body((((1,), (1,)), ((), ())), nk, out_f32)

    def wrapped(a_ref, b_ref, o_ref, *scr):
        body(a_ref, b_ref, o_ref, *scr)

    return _mm_call(wrapped, (M // tm, Ko // tn, nk),
                    [pl.BlockSpec((tm, tk), lambda i, j, k: (i, k)),
                     pl.BlockSpec((None, tn, tk), lambda i, j, k: (k // kpb, j, k % kpb))],
                    pl.BlockSpec((tm, tn), lambda i, j, k: (i, j)), _sds((M, Ko), out_dtype),
                    None if (nk == 1 or out_f32) else (tm, tn), name, deps)(a, b3, *deps)


def _mm_wgrad(at, b, G, tm, tn, tk, name, deps=()):
    M, T = at.shape
    Nb = b.shape[1] // G
    npb, nk = Nb // tn, T // tk
    body = _mm_body((((1,), (0,)), ((), ())), nk, False)

    def wrapped(a_ref, b_ref, o_ref, *scr):
        body(a_ref, b_ref, o_ref, *scr)

    a = at
    in_specs = [pl.BlockSpec((tm, tk), lambda i, j, k: (i, k)), pl.BlockSpec((tk, tn), lambda i, j, k: (k, j))]
    out_spec = pl.BlockSpec((None, tm, tn), lambda i, j, k: (j // npb, i, j % npb))
    return _mm_call(wrapped, (M // tm, G * npb, nk), in_specs, out_spec, _sds((G, M, Nb), BF16),
                    None if nk == 1 else (tm, tn), name, deps)(a, b, *deps)


def _rsum(v):
    return jnp.sum(v, axis=0, keepdims=True)


def _rmean(v):
    return jnp.mean(v, axis=-1, keepdims=True)


def _gelu(x):
    t = jnp.tanh(_G0 * (x + _G1 * (x * x * x)))
    return x * (0.5 * (1.0 + t)), t


def _dgelu(x, t):
    return 0.5 * (1.0 + t) + 0.5 * x * (1.0 - t * t) * (_G0 * (1.0 + 3.0 * _G1 * (x * x)))


def _sigmoid(x):
    return 1.0 / (1.0 + jnp.exp(-x))


def _fill_shifted(ext, rot):
    v = ext[...]
    n = v.shape[0]
    for b in range(1, 8):
        rot[b - 1] = pltpu.roll(v, n - b, 0)


def _rows_at(ext, rot, s, tm, cs=slice(None)):
    a, b = divmod(s, 8)
    return ext[8 * a:8 * a + tm, cs] if b == 0 else rot[b - 1, 8 * a:8 * a + tm, cs]


def _causal_conv(w_ref, taps, bias, ext, rot, offset, tm, out):
    D = out.shape[1]
    for cb in range(D // LANE):
        cs = slice(cb * LANE, (cb + 1) * LANE)
        acc = None
        for k, o in zip(taps, offset):
            term = w_ref[k:k + 1, cs] * _rows_at(ext, rot, o, tm, cs)
            acc = term if acc is None else acc + term
        out[:, cs] = acc if bias is None else acc + bias[:, cs]


def _rows(*vs):
    a = jnp.stack([v.astype(F32) for v in vs])
    return jnp.pad(a, ((0, 8 - len(vs)), (0, 0)))


def _row_spec(tm, D):
    return pl.BlockSpec((tm, D), lambda i: (i, 0))


def _const_spec(shape):
    nd = len(shape)
    return pl.BlockSpec(shape, lambda i: (0,) * nd)


def _norm_fwd(xp, f, vec, name, deps=()):
    S, D = xp.shape
    tm = min(256, S)
    has_f = f is not None

    def body(*refs):
        if has_f:
            xp_ref, f_ref, vec_ref, xo_ref, h_ref, ht_ref = refs
            x = xp_ref[...] + vec_ref[0:1, :] * f_ref[...]
            xo_ref[...] = x
        else:
            xp_ref, vec_ref, h_ref, ht_ref = refs
            x = xp_ref[...]
        r = lax.rsqrt(_rmean(x * x) + EPS)
        h = (x * r) * vec_ref[1:2, :]
        h = h * (1.0 + vec_ref[2:3, :]) + vec_ref[3:4, :]
        h_ref[...] = h.astype(BF16)
        ht_ref[...] = h.T.astype(BF16)

    rs = _row_spec(tm, D)
    ins = [xp, f, vec] if has_f else [xp, vec]
    in_specs = ([rs, rs] if has_f else [rs]) + [_const_spec((8, D))]
    out_shape = ([_sds((S, D), F32)] if has_f else []) + [_sds((S, D), BF16), _sds((D, S), BF16)]
    out_specs = [rs] * (len(out_shape) - 1) + [pl.BlockSpec((D, tm), lambda i: (0, i))]
    outs = _pcall(_after(body, len(ins), deps), grid=(S // tm,), in_specs=in_specs + [ANY] * len(deps),
                  out_specs=out_specs, out_shape=out_shape, name=name,
                  compiler_params=_params(("parallel",)))(*ins, *deps)
    return (outs[0], outs[1], outs[2]) if has_f else (xp, outs[0], outs[1])


def _mixer_fwd(z, wsh, sgu_ln, wtril, bias_full, cw, cvec, name, deps=()):
    S = z.shape[0]
    D = wsh.shape[1]
    tm = CHUNK

    def body(z_ref, wsh_ref, sln_ref, wt_ref, bias_ref, cw_ref, cv_ref, oa_ref, ob_ref, oc_ref, ta_ref, tb_ref, tc_ref,
             conv_ref, pe, ge, gr, cbuf):
        i = pl.program_id(0)

        @pl.when(i == 0)
        def _():
            pe[0:HALO, :] = jnp.zeros((HALO, D), F32)
            ge[0:HALO, :] = jnp.zeros((HALO, D), F32)

        def col(n):
            return z_ref[:, n * D:(n + 1) * D].astype(F32)

        pe[HALO:HALO + tm, :] = col(1) * col(2)
        q = wsh_ref[0:1, :] * pe[HALO - 2:HALO - 2 + tm, :]
        q = q + wsh_ref[1:2, :] * pe[HALO - 1:HALO - 1 + tm, :]
        q = q + wsh_ref[2:3, :] * pe[HALO:HALO + tm, :]
        act_a = col(0) * q
        oa_ref[...] = act_a.astype(BF16)
        ta_ref[...] = act_a.T.astype(BF16)
        gu, _ = _gelu(col(3))
        gv, _ = _gelu(col(4))
        d = gv - _rmean(gv)
        nrm = d * lax.rsqrt(_rmean(d * d) + EPS)
        vnb = (nrm * sln_ref[0:1, :] + sln_ref[1:2, :]).astype(BF16)
        for g in range(NG):
            cs = slice(g * LANE, (g + 1) * LANE)
            mixed = jnp.dot(wt_ref[g], vnb[:, cs], preferred_element_type=F32) + bias_ref[:, cs]
            act_b = gu[:, cs] * mixed
            ob_ref[:, cs] = act_b.astype(BF16)
            tb_ref[cs, :] = act_b.T.astype(BF16)
        ge[HALO:HALO + tm, :] = col(5) * _sigmoid(col(6))
        _fill_shifted(ge, gr)
        o0 = HALO - (CFM_K - 1)
        _causal_conv(cw_ref, range(CFM_K), cv_ref[0:1, :], ge, gr, range(o0, o0 + CFM_K), tm, cbuf)
        conv = cbuf[...]
        conv_ref[...] = conv.astype(BF16)
        d = conv - _rmean(conv)
        ln = (d * lax.rsqrt(_rmean(d * d) + EPS)) * cv_ref[1:2, :] + cv_ref[2:3, :]
        act_c = ln * _sigmoid(ln)
        oc_ref[...] = act_c.astype(BF16)
        tc_ref[...] = act_c.T.astype(BF16)
        pe[0:HALO, :] = pe[tm:tm + HALO, :]
        ge[0:HALO, :] = ge[tm:tm + HALO, :]

    rs = _row_spec(tm, D)
    outs = _pcall(
        _after(body, 7, deps), grid=(S // tm,),
        in_specs=[pl.BlockSpec((tm, 7 * D), lambda i: (i, 0)), _const_spec((8, D)), _const_spec((8, D)),
                  _const_spec((NG, CHUNK, CHUNK)), _const_spec((CHUNK, D)), _const_spec((HALO, D)), _const_spec((8, D))]
        + [ANY] * len(deps),
        out_specs=[rs, rs, rs] + [pl.BlockSpec((D, tm), lambda i: (0, i))] * 3 + [rs],
        out_shape=[_sds((S, D), BF16)] * 3 + [_sds((D, S), BF16)] * 3 + [_sds((S, D), BF16)],
        scratch_shapes=[pltpu.VMEM((HALO + tm, D), F32), pltpu.VMEM((HALO + tm, D), F32),
                        pltpu.VMEM((7, HALO + tm, D), F32), pltpu.VMEM((tm, D), F32)],
        name=name, compiler_params=_params(("arbitrary",)))(z, wsh, sgu_ln, wtril, bias_full, cw, cvec, *deps)
    return outs[:3], outs[3:6], outs[6]


def _branch_out(acts, ws, z, name):
    S, D = acts[0].shape
    tm = min(256, S)

    def body(a0, a1, a2, w0, w1, w2, g0, g1, g2, m_ref, mt_ref, y_ref):
        m = None
        for n, (a, w, g) in enumerate(((a0, w0, g0), (a1, w1, g1), (a2, w2, g2))):
            y = jnp.dot(a[...], w[...], preferred_element_type=F32)
            y_ref[n] = y.astype(BF16)
            t = _sigmoid(g[...].astype(F32)) * y
            m = t if m is None else m + t
        m_ref[...] = m.astype(BF16)
        mt_ref[...] = m.T.astype(BF16)

    rs = _row_spec(tm, D)
    gate_specs = [pl.BlockSpec((tm, D), functools.partial(lambda i, n: (i, 7 + n), n=n)) for n in range(3)]
    return _pcall(body, grid=(S // tm,),
                  in_specs=[rs, rs, rs] + [_const_spec((D, D))] * 3 + gate_specs,
                  out_specs=[rs, pl.BlockSpec((D, tm), lambda i: (0, i)), pl.BlockSpec((3, tm, D), lambda i: (0, i, 0))],
                  out_shape=[_sds((S, D), BF16), _sds((D, S), BF16), _sds((3, S, D), BF16)], name=name,
                  compiler_params=_params(("parallel",)))(*acts, *ws, z, z, z)


def _ffn_in_swiglu(h2, w3, tm, tn, name):
    S, D = h2.shape
    F = w3.shape[2] // 2
    nj = F // tn

    def body(a_ref, wg_ref, wu_ref, gu_ref, act_ref, actt_ref):
        a = a_ref[...]
        g = jnp.dot(a, wg_ref[...], preferred_element_type=F32)
        u = jnp.dot(a, wu_ref[...], preferred_element_type=F32)
        gu_ref[0] = g.astype(BF16)
        gu_ref[1] = u.astype(BF16)
        act = (g * _sigmoid(g)) * u
        act_ref[...] = act.astype(BF16)
        actt_ref[...] = act.T.astype(BF16)

    return _pcall(body, grid=(S // tm, nj),
                  in_specs=[pl.BlockSpec((tm, D), lambda i, j: (i, 0)), pl.BlockSpec((None, D, tn), lambda i, j: (0, 0, j)),
                            pl.BlockSpec((None, D, tn), lambda i, j: (0, 0, j + nj))],
                  out_specs=[pl.BlockSpec((2, tm, tn), lambda i, j: (0, i, j)), pl.BlockSpec((tm, tn), lambda i, j: (i, j)),
                             pl.BlockSpec((tn, tm), lambda i, j: (j, i))],
                  out_shape=[_sds((2, S, F), BF16), _sds((S, F), BF16), _sds((F, S), BF16)], name=name,
                  compiler_params=_params(("parallel", "parallel")))(h2, w3, w3)


def _swiglu_bwd(dact, gu, name):
    _, S, F = gu.shape
    F2 = 2 * F
    tm = min(256, S)

    def body(d_ref, g_ref, u_ref, o_ref):
        g = g_ref[...].astype(F32)
        sg = _sigmoid(g)
        d = d_ref[...].astype(F32)
        o_ref[:, 0:F] = (d * u_ref[...].astype(F32) * (sg * (1.0 + g * (1.0 - sg)))).astype(BF16)
        o_ref[:, F:2 * F] = (d * (g * sg)).astype(BF16)

    return _pcall(body, grid=(S // tm,),
                  in_specs=[pl.BlockSpec((tm, F), lambda i: (i, 0)), pl.BlockSpec((None, tm, F), lambda i: (0, i, 0)),
                            pl.BlockSpec((None, tm, F), lambda i: (1, i, 0))],
                  out_specs=pl.BlockSpec((tm, F2), lambda i: (i, 0)), out_shape=_sds((S, F2), BF16), name=name,
                  compiler_params=_params(("parallel",)))(dact, gu, gu)


def _final_bwd(x1, f, tgt, vec, name):
    S, D = x1.shape
    tm = min(256, S)

    def body(x_ref, f_ref, t_ref, vec_ref, dx_ref, df_ref, sums_ref, loss_ref):
        @pl.when(pl.program_id(0) == 0)
        def _():
            sums_ref[...] = jnp.zeros_like(sums_ref)
            loss_ref[...] = jnp.zeros_like(loss_ref)

        gate, fg = vec_ref[0:1, :], vec_ref[1:2, :]
        fv = f_ref[...]
        x = x_ref[...] + gate * fv
        r = lax.rsqrt(_rmean(x * x) + EPS)
        xn = x * r
        diff = xn * fg - t_ref[...]
        per_tok = _rmean(diff * diff)
        loss_ref[...] += 0.5 * jnp.sum(per_tok, axis=0, keepdims=True)
        dy = diff * (1.0 / D)
        sums_ref[0:1, :] += _rsum(dy * xn)
        dxn = dy * fg
        dx = r * (dxn - xn * _rmean(dxn * xn))
        sums_ref[1:2, :] += _rsum(dx * fv)
        dx_ref[...] = dx
        df_ref[...] = (dx * gate).astype(BF16)

    rs = _row_spec(tm, D)
    return _pcall(body, grid=(S // tm,), in_specs=[rs, rs, rs, _const_spec((8, D))],
                  out_specs=[rs, rs, _const_spec((8, D)), _const_spec((8, LANE))],
                  out_shape=[_sds((S, D), F32), _sds((S, D), BF16), _sds((8, D), F32), _sds((8, LANE), F32)],
                  name=name, compiler_params=_params(("arbitrary",)))(x1, f, tgt, vec)


def _norm_bwd(xin, dh, dxup, vec, fprev, name, deps=()):
    S, D = xin.shape
    tm = min(256, S)
    has_prev = fprev is not None

    def body(*refs):
        if has_prev:
            x_ref, dh_ref, up_ref, vec_ref, fp_ref, dx_ref, dp_ref, sums_ref = refs
        else:
            x_ref, dh_ref, up_ref, vec_ref, dx_ref, sums_ref = refs

        @pl.when(pl.program_id(0) == 0)
        def _():
            sums_ref[...] = jnp.zeros_like(sums_ref)

        g, scale = vec_ref[0:1, :], vec_ref[1:2, :]
        x = x_ref[...]
        r = lax.rsqrt(_rmean(x * x) + EPS)
        xn = x * r
        dhv = dh_ref[...]
        sums_ref[0:1, :] += _rsum(dhv)
        sums_ref[1:2, :] += _rsum(dhv * (xn * g))
        dm = dhv * (1.0 + scale)
        sums_ref[2:3, :] += _rsum(dm * xn)
        dxn = dm * g
        dx = up_ref[...] + r * (dxn - xn * _rmean(dxn * xn))
        dx_ref[...] = dx
        if has_prev:
            sums_ref[3:4, :] += _rsum(dx * fp_ref[...])
            dp_ref[...] = (dx * vec_ref[2:3, :]).astype(BF16)

    rs = _row_spec(tm, D)
    ins = [xin, dh, dxup, vec] + ([fprev] if has_prev else [])
    in_specs = [rs, rs, rs, _const_spec((8, D))] + ([rs] if has_prev else [])
    out_shape = [_sds((S, D), F32)] + ([_sds((S, D), BF16)] if has_prev else []) + [_sds((8, D), F32)]
    out_specs = [rs] + ([rs] if has_prev else []) + [_const_spec((8, D))]
    outs = _pcall(_after(body, len(ins), deps), grid=(S // tm,), in_specs=in_specs + [ANY] * len(deps),
                  out_specs=out_specs, out_shape=out_shape, name=name,
                  compiler_params=_params(("arbitrary",)))(*ins, *deps)
    return (outs[0], outs[1], outs[2]) if has_prev else (outs[0], None, outs[1])


def _gate_bwd(dmerged, z, ys, name, deps=()):
    S, D = dmerged.shape
    tm = min(512, S)
    ncol = z.shape[1] // D

    def body(dm_ref, g_ref, y_ref, dya_ref, dyb_ref, dyc_ref, dz_ref):
        n = pl.program_id(1)
        sg = _sigmoid(g_ref[...].astype(F32))
        dm = dm_ref[...].astype(F32)
        dy = (dm * sg).astype(BF16)
        for k, ref in enumerate((dya_ref, dyb_ref, dyc_ref)):
            @pl.when(n == k)
            def _(ref=ref):
                ref[...] = dy
        dz_ref[...] = (dm * y_ref[...].astype(F32) * (sg * (1.0 - sg))).astype(BF16)

    row = pl.BlockSpec((tm, D), lambda i, n: (i, 0))
    outs = _pcall(_after(body, 3, deps), grid=(S // tm, 3),
                  in_specs=[row, pl.BlockSpec((tm, D), lambda i, n: (i, 7 + n)),
                            pl.BlockSpec((None, tm, D), lambda i, n: (n, i, 0))] + [ANY] * len(deps),
                  out_specs=[row, row, row, pl.BlockSpec((tm, D), lambda i, n: (i, 7 + n))],
                  out_shape=[_sds((S, D), BF16)] * 3 + [_sds((S, ncol * D), BF16)], name=name,
                  compiler_params=_params(("parallel", "arbitrary")))(dmerged, z, ys, *deps)
    return outs[:3], outs[3]


def _mixer_bwd(z, dacts, conv, dz, wsh, sgu_ln, wtril, wtril_t, bias_full, cw, cvec, name):
    S = z.shape[0]
    D = wsh.shape[1]
    tm = CHUNK
    nt = S // tm
    hb = tm // HALO

    def body(zc, zp, da_ref, db_ref, dc_ref, conv_ref, wsh_ref, sln_ref, wt_ref, wtt_ref, bias_ref, cw_ref, cv_ref, _dz_in,
             dz_ref, vec_ref, dcw_ref, dws_ref, dbs_ref, pe, ge, dqe, dce, gr, dcr, cbuf, dcw8):
        i = pl.program_id(0)
        rb = nt - 1 - i

        @pl.when(i == 0)
        def _():
            vec_ref[...] = jnp.zeros_like(vec_ref)
            dcw8[...] = jnp.zeros_like(dcw8)
            dws_ref[...] = jnp.zeros_like(dws_ref)
            dbs_ref[...] = jnp.zeros_like(dbs_ref)
            dqe[tm:tm + HALO, :] = jnp.zeros((HALO, D), F32)
            dce[tm:tm + HALO, :] = jnp.zeros((HALO, D), F32)

        keep = (rb > 0).astype(F32)

        def col(n):
            return zc[:, n * D:(n + 1) * D].astype(F32)

        def pcol(n):
            return zp[:, n * D:(n + 1) * D].astype(F32)

        c_a, x_a = col(1), col(2)
        pe[0:HALO, :] = keep * (pcol(1) * pcol(2))
        pe[HALO:HALO + tm, :] = c_a * x_a
        q = wsh_ref[0:1, :] * pe[HALO - 2:HALO - 2 + tm, :]
        q = q + wsh_ref[1:2, :] * pe[HALO - 1:HALO - 1 + tm, :]
        q = q + wsh_ref[2:3, :] * pe[HALO:HALO + tm, :]
        dact = da_ref[...].astype(F32)
        dz_ref[:, 0:D] = (dact * q).astype(BF16)
        dq = dact * col(0)
        dqe[0:tm, :] = dq
        dp = wsh_ref[2:3, :] * dq + wsh_ref[1:2, :] * dqe[1:1 + tm, :] + wsh_ref[0:1, :] * dqe[2:2 + tm, :]
        dz_ref[:, D:2 * D] = (dp * x_a).astype(BF16)
        dz_ref[:, 2 * D:3 * D] = (dp * c_a).astype(BF16)
        for k in range(SHORT_K):
            o = HALO - (SHORT_K - 1) + k
            vec_ref[k:k + 1, :] += _rsum(dq * pe[o:o + tm, :])
        u, v = col(3), col(4)
        gu, tu = _gelu(u)
        gv, tv = _gelu(v)
        d = gv - _rmean(gv)
        rstd = lax.rsqrt(_rmean(d * d) + EPS)
        nrm = d * rstd
        vnb = (nrm * sln_ref[0:1, :] + sln_ref[1:2, :]).astype(BF16)
        dact = db_ref[...].astype(F32)
        dvn_parts, dgu_parts = [], []
        for g in range(NG):
            cs = slice(g * LANE, (g + 1) * LANE)
            vg = vnb[:, cs]
            mixed = jnp.dot(wt_ref[g], vg, preferred_element_type=F32) + bias_ref[:, cs]
            dgu_parts.append(dact[:, cs] * mixed)
            dmixed = dact[:, cs] * gu[:, cs]
            dmb = dmixed.astype(BF16)
            dws_ref[g] += lax.dot_general(dmb, vg, (((1,), (1,)), ((), ())), preferred_element_type=F32)
            dbs_ref[g] += jnp.broadcast_to(jnp.sum(dmixed, axis=1, keepdims=True), (CHUNK, LANE))
            dvn_parts.append(jnp.dot(wtt_ref[g], dmb, preferred_element_type=F32))
        dgu = jnp.concatenate(dgu_parts, axis=1)
        dvn = jnp.concatenate(dvn_parts, axis=1)
        dz_ref[:, 3 * D:4 * D] = (dgu * _dgelu(u, tu)).astype(BF16)
        vec_ref[3:4, :] += _rsum(dvn * nrm)
        vec_ref[4:5, :] += _rsum(dvn)
        dn = dvn * sln_ref[0:1, :]
        dgv = rstd * (dn - _rmean(dn) - nrm * _rmean(dn * nrm))
        dz_ref[:, 4 * D:5 * D] = (dgv * _dgelu(v, tv)).astype(BF16)
        a_c = col(5)
        sg = _sigmoid(col(6))
        ge[0:HALO, :] = keep * (pcol(5) * _sigmoid(pcol(6)))
        ge[HALO:HALO + tm, :] = a_c * sg
        _fill_shifted(ge, gr)
        o0 = HALO - (CFM_K - 1)
        conv = conv_ref[...].astype(F32)
        d = conv - _rmean(conv)
        rstd = lax.rsqrt(_rmean(d * d) + EPS)
        nrm = d * rstd
        ln = nrm * cv_ref[1:2, :] + cv_ref[2:3, :]
        sl = _sigmoid(ln)
        dln = dc_ref[...].astype(F32) * (sl * (1.0 + ln * (1.0 - sl)))
        vec_ref[6:7, :] += _rsum(dln * nrm)
        vec_ref[7:8, :] += _rsum(dln)
        dn = dln * cv_ref[1:2, :]
        dconv = rstd * (dn - _rmean(dn) - nrm * _rmean(dn * nrm))
        vec_ref[5:6, :] += _rsum(dconv)
        dce[0:tm, :] = dconv
        _fill_shifted(dce, dcr)
        _causal_conv(cw_ref, range(CFM_K), None, dce, dcr, [CFM_K - 1 - k for k in range(CFM_K)], tm, cbuf)
        dglu = cbuf[...]
        for cb in range(D // LANE):
            cs = slice(cb * LANE, (cb + 1) * LANE)
            dcv = dce[0:tm, cs]
            for k in range(CFM_K):
                prod = dcv * _rows_at(ge, gr, o0 + k, tm, cs)
                dcw8[k, :, cs] += jnp.sum(prod.reshape(tm // 8, 8, LANE), axis=0)

        @pl.when(i == nt - 1)
        def _():
            dcw_ref[...] = jnp.sum(dcw8[...], axis=1)
        dz_ref[:, 5 * D:6 * D] = (dglu * sg).astype(BF16)
        dz_ref[:, 6 * D:7 * D] = (dglu * a_c * (sg * (1.0 - sg))).astype(BF16)
        dqe[tm:tm + HALO, :] = dqe[0:HALO, :]
        dce[tm:tm + HALO, :] = dce[0:HALO, :]

    rev = lambda i: (nt - 1 - i, 0)
    rs = pl.BlockSpec((tm, D), rev)
    cur = pl.BlockSpec((tm, 7 * D), rev)
    prev = pl.BlockSpec((HALO, 7 * D), lambda i: (jnp.maximum((nt - 1 - i) * hb - 1, 0), 0))
    ext = pltpu.VMEM((HALO + tm, D), F32)
    outs = _pcall(
        body, grid=(nt,),
        in_specs=[cur, prev, rs, rs, rs, rs, _const_spec((8, D)), _const_spec((8, D)), _const_spec((NG, CHUNK, CHUNK)),
                  _const_spec((NG, CHUNK, CHUNK)), _const_spec((CHUNK, D)), _const_spec((HALO, D)), _const_spec((8, D)),
                  ANY],
        out_specs=[cur, _const_spec((8, D)), _const_spec((HALO, D)), _const_spec((NG, CHUNK, CHUNK)),
                   _const_spec((NG, CHUNK, LANE))],
        out_shape=[_sds(dz.shape, BF16), _sds((8, D), F32), _sds((HALO, D), F32), _sds((NG, CHUNK, CHUNK), F32),
                   _sds((NG, CHUNK, LANE), F32)],
        scratch_shapes=[ext, ext, ext, ext, pltpu.VMEM((7, HALO + tm, D), F32), pltpu.VMEM((7, HALO + tm, D), F32),
                        pltpu.VMEM((tm, D), F32), pltpu.VMEM((HALO, 8, D), F32)],
        input_output_aliases={13: 0}, name=name,
        compiler_params=_params(("arbitrary",)))(z, z, *dacts, conv, wsh, sgu_ln, wtril, wtril_t, bias_full, cw, cvec, dz)
    return outs


def _ada_fwd(c_all, w_ada_loc, name):
    nb, D = c_all.shape
    L, _, nc = w_ada_loc.shape

    def body(c_ref, w_ref, o_ref, ca_ref):
        cv = c_ref[...]
        ca = cv * _sigmoid(cv)
        ca_ref[...] = ca
        o_ref[...] = jnp.dot(ca.astype(BF16), w_ref[...].astype(BF16), preferred_element_type=F32)

    return _pcall(body, grid=(L,),
                  in_specs=[_const_spec((nb, D)), pl.BlockSpec((None, D, nc), lambda l: (l, 0, 0))],
                  out_specs=[pl.BlockSpec((None, nb, nc), lambda l: (l, 0, 0)), _const_spec((nb, D))],
                  out_shape=[_sds((L, nb, nc), F32), _sds((nb, D), F32)], name=name,
                  compiler_params=_params(("arbitrary",)))(c_all, w_ada_loc)


def _adamw(w, g, m, v):
    m = ADAM_B1 * m + (1.0 - ADAM_B1) * g
    v = ADAM_B2 * v + (1.0 - ADAM_B2) * (g * g)
    m_hat = m / (1.0 - ADAM_B1 ** ADAM_STEP)
    v_hat = v / (1.0 - ADAM_B2 ** ADAM_STEP)
    delta = -ADAM_LR * (m_hat / (jnp.sqrt(v_hat) + ADAM_EPS) + ADAM_WD * w)
    return delta, m, v


def _tile_rows(R, C, align=8):
    cap = max(align, (1536 * 1024) // (4 * C))
    best = None
    for t in range(align, R + 1, align):
        if R % t == 0 and t <= cap:
            best = t
    return R if best is None else best


def _adam_ada(ct, dm, w, m, v, name):
    L, D, nc = w.shape
    nb = ct.shape[1]
    tr = _tile_rows(D, nc)

    def body(ct_ref, dm_ref, w_ref, m_ref, v_ref, g_ref, d_ref, mo_ref, vo_ref):
        g = ct_ref[:, 0:1] * dm_ref[0:1, :]
        for b in range(1, nb):
            g = g + ct_ref[:, b:b + 1] * dm_ref[b:b + 1, :]
        g_ref[...] = g
        d_ref[...], mo_ref[...], vo_ref[...] = _adamw(w_ref[...], g, m_ref[...], v_ref[...])

    ws = pl.BlockSpec((None, tr, nc), lambda l, r: (l, r, 0))
    return _pcall(body, grid=(L, D // tr),
                  in_specs=[pl.BlockSpec((tr, nb), lambda l, r: (r, 0)), pl.BlockSpec((None, nb, nc), lambda l, r: (l, 0, 0)),
                            ws, ws, ws],
                  out_specs=[ws] * 4, out_shape=[_sds(w.shape, F32)] * 4, name=name,
                  compiler_params=_params(("parallel", "parallel")))(ct, dm, w, m, v)


def _adam_small(parts, w, m, v, name, deps=()):
    n, R, C = parts.shape
    tr = _tile_rows(R, C * n // 2)

    def body(p_ref, w_ref, m_ref, v_ref, g_ref, d_ref, mo_ref, vo_ref):
        g = p_ref[0]
        for j in range(1, n):
            g = g + p_ref[j]
        g_ref[...] = g
        d_ref[...], mo_ref[...], vo_ref[...] = _adamw(w_ref[...], g, m_ref[...], v_ref[...])

    ws = pl.BlockSpec((tr, C), lambda r: (r, 0))
    return _pcall(_after(body, 4, deps), grid=(R // tr,),
                  in_specs=[pl.BlockSpec((n, tr, C), lambda r: (0, r, 0)), ws, ws, ws] + [ANY] * len(deps),
                  out_specs=[ws] * 4, out_shape=[_sds((R, C), F32)] * 4, name=name,
                  compiler_params=_params(("parallel",)))(parts, w, m, v, *deps)


def _adam_plain(g, w, m, v, name):
    R, C = w.shape

    def body(g_ref, w_ref, m_ref, v_ref, d_ref, mo_ref, vo_ref):
        d_ref[...], mo_ref[...], vo_ref[...] = _adamw(w_ref[...], g_ref[...], m_ref[...], v_ref[...])

    ws = _const_spec((R, C))
    return _pcall(body, grid=(1,), in_specs=[ws] * 4, out_specs=[ws] * 3, out_shape=[_sds((R, C), F32)] * 3, name=name,
                  compiler_params=_params(("arbitrary",)))(g, w, m, v)


def _pair_sum(G, R1, my_c, name):
    n, R, C = G.shape
    half = n // 2
    tr = _tile_rows(R, C, align=16)

    def body(c_ref, g_ref, r_ref, o_ref):
        o_ref[...] = (g_ref[...].astype(F32) + r_ref[...].astype(F32)).astype(o_ref.dtype)

    blk = (None, tr, C)
    gs = pltpu.PrefetchScalarGridSpec(
        num_scalar_prefetch=1, grid=(half, R // tr),
        in_specs=[pl.BlockSpec(blk, lambda p, r, c: (2 * p + c[0], r, 0)), pl.BlockSpec(blk, lambda p, r, c: (p, r, 0))],
        out_specs=pl.BlockSpec(blk, lambda p, r, c: (p, r, 0)))
    return _pcall(body, grid_spec=gs, out_shape=_sds((half, R, C), G.dtype), name=name,
                  compiler_params=_params(("parallel", "parallel")))(my_c, G, R1)


def _adam_big(P, R2, my_chip, w, m, v, layer, prev, name, deps=()):
    _, R, C = P.shape
    nrecv = R2.shape[0]
    tr = _tile_rows(R, C, align=16)

    def body(p_sm, p_ref, r_ref, w_ref, m_ref, v_ref, *rest):
        g_ref, d_ref, mo_ref, vo_ref = rest[-4:]
        g = p_ref[...].astype(F32)
        for k in range(nrecv):
            g = g + r_ref[k].astype(F32)
        g_ref[...] = g
        d_ref[...], mo_ref[...], vo_ref[...] = _adamw(w_ref[...], g, m_ref[...], v_ref[...])

    ws = pl.BlockSpec((None, tr, C), lambda r, p: (layer, r, 0))
    held = [] if prev is None else list(prev)
    gs = pltpu.PrefetchScalarGridSpec(
        num_scalar_prefetch=1, grid=(R // tr,),
        in_specs=[pl.BlockSpec((None, tr, C), lambda r, p: (p[0], r, 0)),
                  pl.BlockSpec((nrecv, tr, C), lambda r, p: (0, r, 0)), ws, ws, ws] + [ANY] * (len(held) + len(deps)),
        out_specs=[ws] * 4)
    alias = {6 + i: i for i in range(len(held))}
    return _pcall(body, grid_spec=gs, out_shape=[_sds(w.shape, F32)] * 4, name=name, input_output_aliases=alias,
                  compiler_params=_params(("parallel",)))(my_chip, P, R2, w, m, v, *held, *deps)


def _place():
    return lax.axis_index("x"), lax.axis_index("y"), lax.axis_index("c")


def _all_gather(shards, name, deps=()):
    n = len(shards)

    def body(*refs):
        ins, outs = refs[:n], refs[n:2 * n]
        send_sems, recv_sems, local_sems = refs[2 * n:]
        x, y, c = _place()
        me, sibling = (x, y, c), (x, y, 1 - c)
        chips = [(1 - x, y), (x, 1 - y), (1 - x, 1 - y)]

        def slot(a, px, py, pc):
            return outs[a].at[4 * px + 2 * py + pc]

        def copy(a, k, block, to, src=None):
            return pltpu.make_async_remote_copy(
                src_ref=slot(a, *block) if src is None else src, dst_ref=slot(a, *block),
                send_sem=send_sems.at[7 * a + k], recv_sem=recv_sems.at[7 * a + k], device_id=to, device_id_type=MESH)

        mine = [pltpu.make_async_copy(ins[a], slot(a, *me), local_sems.at[a]) for a in range(n)]
        for cp in mine:
            cp.start()
        first = []
        for a in range(n):
            first.append(copy(a, 0, me, sibling, src=ins[a]))
            first += [copy(a, 1 + j, me, (*chip, c), src=ins[a]) for j, chip in enumerate(chips)]
        for cp in first:
            cp.start()
        passed = []
        for j, chip in enumerate(chips):
            for a in range(n):
                copy(a, 1 + j, (*chip, c), me).wait_recv()
                fwd = copy(a, 4 + j, (*chip, c), sibling)
                fwd.start()
                passed.append(fwd)
        for a in range(n):
            copy(a, 0, sibling, me).wait_recv()
        for j, chip in enumerate(chips):
            for a in range(n):
                copy(a, 4 + j, (*chip, 1 - c), me).wait_recv()
        for cp in first + passed:
            cp.wait_send()
        for cp in mine:
            cp.wait()

    outs = _pcall(_after(body, n, deps), in_specs=[ANY] * (n + len(deps)), out_specs=[ANY] * n,
                  out_shape=[_sds((NDEV,) + s.shape, s.dtype) for s in shards],
                  scratch_shapes=[pltpu.SemaphoreType.DMA((7 * n,)), pltpu.SemaphoreType.DMA((7 * n,)),
                                  pltpu.SemaphoreType.DMA((n,))], name=name)(*shards, *deps)
    return list(outs)


HBM = pl.BlockSpec(memory_space=pltpu.HBM)
SEM = pl.BlockSpec(memory_space=pltpu.SEMAPHORE)


def _copies(plan, refs, send_sems, recv_sems):
    return [pltpu.make_async_remote_copy(src_ref=s, dst_ref=d, send_sem=send_sems.at[k], recv_sem=recv_sems.at[k],
                                         device_id=dev, device_id_type=MESH)
            for k, (s, d, dev) in enumerate(plan(refs, *_place()))]


def _xfer_start(bufs, ncopies, plan, name, deps=()):
    n = len(bufs)

    def body(*refs):
        for cp in _copies(plan, refs[:n], refs[n], refs[n + 1]):
            cp.start()
        token = refs[2 * n + 2]
        token[...] = jnp.zeros_like(token)

    outs = _pcall(
        _after(body, n, deps), name=name,
        out_shape=(pltpu.SemaphoreType.DMA((ncopies,)), pltpu.SemaphoreType.DMA((ncopies,)),
                   *[pltpu.HBM(b.shape, b.dtype) for b in bufs], _sds((8, LANE), F32)),
        in_specs=[HBM] * n + [ANY] * len(deps),
        out_specs=(SEM, SEM, *[HBM] * n, pl.BlockSpec(memory_space=pltpu.VMEM)),
        input_output_aliases={i: 2 + i for i in range(n)},
        compiler_params=pltpu.CompilerParams(has_side_effects=pltpu.SideEffectType.DATAFLOW_SIDE_EFFECTING),
    )(*[pltpu.with_memory_space_constraint(b, pltpu.HBM) for b in bufs], *deps)
    return (outs[0], outs[1]), list(outs[2:2 + n]), outs[2 + n]


def _xfer_wait(sems, bufs, plan, after, name):
    n = len(bufs)
    after = list(after) if isinstance(after, (list, tuple)) else [after]

    def body(*refs):
        for cp in _copies(plan, refs[:n], refs[n], refs[n + 1]):
            cp.wait_send()
            cp.wait_recv()

    outs = _pcall(
        body, name=name, out_shape=tuple(pltpu.HBM(b.shape, b.dtype) for b in bufs),
        in_specs=[HBM] * n + [SEM, SEM] + [ANY] * len(after), out_specs=tuple([HBM] * n),
        input_output_aliases={i: i for i in range(n)},
        compiler_params=pltpu.CompilerParams(has_side_effects=pltpu.SideEffectType.DATAFLOW_SIDE_EFFECTING),
    )(*bufs, *sems, *after)
    return list(outs)


def _chips_of(x, y):
    return [(1 - x, y), (x, 1 - y), (1 - x, 1 - y)]


def _gather_plan1(n):
    def plan(refs, x, y, c):
        out = []
        for a in range(n):
            blk = refs[a].at[4 * x + 2 * y + c]
            out.append((blk, blk, (x, y, 1 - c)))
            out += [(blk, blk, (px, py, c)) for px, py in _chips_of(x, y)]
        return out
    return plan


def _gather_plan2(n):
    def plan(refs, x, y, c):
        out = []
        for a in range(n):
            for px, py in _chips_of(x, y):
                blk = refs[a].at[4 * px + 2 * py + c]
                out.append((blk, blk, (x, y, 1 - c)))
        return out
    return plan


def _gather_start(shards, dev, name, deps=()):
    lands = [lax.dynamic_update_slice(lax.empty((NDEV,) + s.shape, s.dtype), s[None], (dev,) + (0,) * s.ndim)
             for s in shards]
    n = len(shards)
    sems, lands, tok = _xfer_start(lands, 4 * n, _gather_plan1(n), name + "_p1_start", deps)
    return dict(sems=sems, lands=lands, tok=tok, n=n)


def _gather_mid(st, after, name):
    n = st["n"]
    lands = _xfer_wait(st["sems"], st["lands"], _gather_plan1(n), after, name + "_p1_wait")
    sems, lands, tok = _xfer_start(lands, 3 * n, _gather_plan2(n), name + "_p2_start")
    return dict(sems=sems, lands=lands, tok=tok, n=n)


def _gather_finish(st, after, name):
    return _xfer_wait(st["sems"], st["lands"], _gather_plan2(st["n"]), after, name + "_p2_wait")


def _scatter_plan1(n):
    def plan(refs, x, y, c):
        return [(refs[a].at[2 * p + 1 - c], refs[n + a].at[p], (x, y, 1 - c)) for a in range(n) for p in range(NCHIP)]
    return plan


def _scatter_plan2(n):
    def plan(refs, x, y, c):
        return [(refs[a].at[2 * px + py], refs[n + a].at[j], (px, py, c))
                for a in range(n) for j, (px, py) in enumerate(_chips_of(x, y))]
    return plan


def _scatter_start(Gs, name):
    n = len(Gs)
    R1s = [lax.empty((NCHIP,) + g.shape[1:], g.dtype) for g in Gs]
    sems, bufs, tok = _xfer_start(list(Gs) + R1s, NCHIP * n, _scatter_plan1(n), name + "_s1_start")
    return dict(sems=sems, bufs=bufs, tok=tok, n=n)


def _scatter_mid(st, after, my_c, name):
    n = st["n"]
    bufs = _xfer_wait(st["sems"], st["bufs"], _scatter_plan1(n), after, name + "_s1_wait")
    Ps = [_pair_sum(bufs[a], bufs[n + a], my_c, f"{name}_pair_sum{a}") for a in range(n)]
    R2s = [lax.empty((3,) + p.shape[1:], p.dtype) for p in Ps]
    sems, bufs, tok = _xfer_start(Ps + R2s, 3 * n, _scatter_plan2(n), name + "_s2_start")
    return dict(sems=sems, bufs=bufs, tok=tok, n=n)


def _scatter_finish(st, after, name):
    n = st["n"]
    bufs = _xfer_wait(st["sems"], st["bufs"], _scatter_plan2(n), after, name + "_s2_wait")
    return bufs[:n], bufs[n:]


SMALL_ROWS = {"norm1_g": (0, 1), "norm2_g": (1, 1), "sgu_ln_g": (2, 1), "sgu_ln_b": (3, 1), "cfm_conv_b": (4, 1),
              "cfm_ln_g": (5, 1), "cfm_ln_b": (6, 1), "b_sgu": (7, 1), "w_sgu": (8, 128), "b_ada": (136, N_MOD),
              "w_short": (142, SHORT_K), "cfm_conv_w": (145, CFM_K)}
ROWS_PER_LAYER = 176
FINAL_ROW = DEPTH * ROWS_PER_LAYER
PACK_ROWS = 360


def _pack(get, D, layers=tuple(range(DEPTH)), tail=True):
    parts = []
    for l in layers:
        for name, (_, nrows) in SMALL_ROWS.items():
            a = get(name, l)
            parts.append(jnp.zeros((nrows * D,), F32) if a is None else a.astype(F32).reshape(nrows * D))
    if tail:
        for name in ("final_g", "loss"):
            a = get(name, None)
            parts.append(jnp.zeros((D,), F32) if a is None else a.astype(F32).reshape(D))
        parts.append(jnp.zeros(((PACK_ROWS - FINAL_ROW - 2) * D,), F32))
    return jnp.concatenate(parts).reshape(-1, D)


def _unpack(pack, name, shape):
    D = pack.shape[1]
    r0, nrows = SMALL_ROWS[name]
    return jnp.stack([pack[l * ROWS_PER_LAYER + r0:l * ROWS_PER_LAYER + r0 + nrows] for l in range(DEPTH)]).reshape(shape)


def _mm_tiles(S):
    return min(512, S), min(1024, S)


def kernel(x, c, w_ada, b_ada, norm1_g, w_in, w_short, w_a_out, sgu_ln_g, sgu_ln_b, w_sgu, b_sgu, w_b_out, cfm_conv_w, cfm_conv_b, cfm_ln_g, cfm_ln_b, w_c_out, w_o, norm2_g, w_ffn_in, w_ffn_out, final_g, loss_target, m_w_ada, m_b_ada, m_norm1_g, m_w_in, m_w_short, m_w_a_out, m_sgu_ln_g, m_sgu_ln_b, m_w_sgu, m_b_sgu, m_w_b_out, m_cfm_conv_w, m_cfm_conv_b, m_cfm_ln_g, m_cfm_ln_b, m_w_c_out, m_w_o, m_norm2_g, m_w_ffn_in, m_w_ffn_out, m_final_g, v_w_ada, v_b_ada, v_norm1_g, v_w_in, v_w_short, v_w_a_out, v_sgu_ln_g, v_sgu_ln_b, v_w_sgu, v_b_sgu, v_w_b_out, v_cfm_conv_w, v_cfm_conv_b, v_cfm_ln_g, v_cfm_ln_b, v_w_c_out, v_w_o, v_norm2_g, v_w_ffn_in, v_w_ffn_out, v_final_g):
    W = dict(w_ada=w_ada, b_ada=b_ada, norm1_g=norm1_g, w_in=w_in, w_short=w_short, w_a_out=w_a_out, sgu_ln_g=sgu_ln_g,
             sgu_ln_b=sgu_ln_b, w_sgu=w_sgu, b_sgu=b_sgu, w_b_out=w_b_out, cfm_conv_w=cfm_conv_w, cfm_conv_b=cfm_conv_b,
             cfm_ln_g=cfm_ln_g, cfm_ln_b=cfm_ln_b, w_c_out=w_c_out, w_o=w_o, norm2_g=norm2_g, w_ffn_in=w_ffn_in,
             w_ffn_out=w_ffn_out, final_g=final_g)
    Mo = dict(w_ada=m_w_ada, b_ada=m_b_ada, norm1_g=m_norm1_g, w_in=m_w_in, w_short=m_w_short, w_a_out=m_w_a_out,
              sgu_ln_g=m_sgu_ln_g, sgu_ln_b=m_sgu_ln_b, w_sgu=m_w_sgu, b_sgu=m_b_sgu, w_b_out=m_w_b_out,
              cfm_conv_w=m_cfm_conv_w, cfm_conv_b=m_cfm_conv_b, cfm_ln_g=m_cfm_ln_g, cfm_ln_b=m_cfm_ln_b,
              w_c_out=m_w_c_out, w_o=m_w_o, norm2_g=m_norm2_g, w_ffn_in=m_w_ffn_in, w_ffn_out=m_w_ffn_out,
              final_g=m_final_g)
    Vo = dict(w_ada=v_w_ada, b_ada=v_b_ada, norm1_g=v_norm1_g, w_in=v_w_in, w_short=v_w_short, w_a_out=v_w_a_out,
              sgu_ln_g=v_sgu_ln_g, sgu_ln_b=v_sgu_ln_b, w_sgu=v_w_sgu, b_sgu=v_b_sgu, w_b_out=v_w_b_out,
              cfm_conv_w=v_cfm_conv_w, cfm_conv_b=v_cfm_conv_b, cfm_ln_g=v_cfm_ln_g, cfm_ln_b=v_cfm_ln_b,
              w_c_out=v_w_c_out, w_o=v_w_o, norm2_g=v_norm2_g, w_ffn_in=v_w_ffn_in, w_ffn_out=v_w_ffn_out,
              final_g=v_final_g)
    order = ["w_ada", "b_ada", "norm1_g", "w_in", "w_short", "w_a_out", "sgu_ln_g", "sgu_ln_b", "w_sgu", "b_sgu",
             "w_b_out", "cfm_conv_w", "cfm_conv_b", "cfm_ln_g", "cfm_ln_b", "w_c_out", "w_o", "norm2_g", "w_ffn_in",
             "w_ffn_out", "final_g"]

    assert DEPTH == 2, "the weight-gather schedule below is written for two layers"
    S, D = x.shape[1], x.shape[2]
    F2 = w_ffn_in.shape[2] * NDEV
    FF = F2 // 2
    xi, yi, ci = _place()
    dev = 4 * xi + 2 * yi + ci
    my_c = jnp.reshape(ci, (1,)).astype(jnp.int32)
    my_chip = jnp.reshape(2 * xi + yi, (1,)).astype(jnp.int32)
    tm, tm_big = _mm_tiles(S)
    x0 = x.reshape(S, D)
    tgt = loss_target.reshape(S, D)

    def shards_of(l):
        return [w_in[l].astype(BF16), w_a_out[l].astype(BF16), w_b_out[l].astype(BF16), w_c_out[l].astype(BF16),
                w_o[l].astype(BF16), w_ffn_in[l].astype(BF16), w_ffn_out[l].astype(BF16)]

    c_all = _all_gather([jnp.pad(c, ((0, 7), (0, 0)))], "ag_c")[0][:, 0, :]
    modpart, c_act = _ada_fwd(c_all, w_ada, "ada_fwd")
    ncol = modpart.shape[2]
    mg = _all_gather([modpart.reshape(DEPTH * NDEV, ncol)], "ag_mod")[0].reshape(NDEV, DEPTH, NDEV, ncol)
    mine = lax.dynamic_index_in_dim(mg, dev, axis=2, keepdims=False)
    mod = (jnp.transpose(mine, (1, 0, 2)).reshape(DEPTH, N_MOD * D) + b_ada).reshape(DEPTH, N_MOD, D)

    tril = jnp.tril(jnp.ones((CHUNK, CHUNK), dtype=bool))

    def layer_consts(l):
        wt = jnp.where(tril[None], w_sgu[l], 0.0).astype(BF16)
        return dict(
            wsh=jnp.pad(w_short_full[l], ((0, 8 - SHORT_K), (0, 0))),
            sgu_ln=_rows(sgu_ln_g[l], sgu_ln_b[l]),
            wtril=wt, wtril_t=jnp.swapaxes(wt, 1, 2),
            bias_full=jnp.repeat(b_sgu[l].T, LANE, axis=1),
            cw=jnp.pad(cfm_w_full[l], ((0, HALO - CFM_K), (0, 0))),
            cvec=_rows(cfm_conv_b[l], cfm_ln_g[l], cfm_ln_b[l]))

    ncs = w_short.shape[2]
    sw = _all_gather([w_short.reshape(DEPTH * SHORT_K, ncs), cfm_conv_w.reshape(DEPTH * CFM_K, ncs)], "ag_convw",
                     deps=(mod,))
    w_short_full = jnp.transpose(sw[0], (1, 0, 2)).reshape(DEPTH, SHORT_K, D)
    cfm_w_full = jnp.transpose(sw[1], (1, 0, 2)).reshape(DEPTH, CFM_K, D)

    def rest_of(g):
        return dict(w_a=g[0].reshape(1, D, D), w_b=g[1].reshape(1, D, D), w_c=g[2].reshape(1, D, D),
                    w_o=g[3].reshape(1, D, D), w_fi=jnp.transpose(g[4], (1, 0, 2)).reshape(1, D, F2),
                    w_fo=g[5].reshape(1, FF, D))

    ag_in0 = _gather_start(shards_of(0)[:1], dev, "ag_w_in0", deps=(cfm_w_full,))
    ag_rest0 = _gather_start(shards_of(0)[1:], dev, "ag_rest0", deps=(ag_in0["tok"],))
    consts = [layer_consts(l) for l in range(DEPTH)]
    sharded_small = ("w_short", "cfm_conv_w")

    def param_get(T):
        def get(name, l):
            if name == "final_g":
                return T[name]
            return None if name in sharded_small or name == "loss" else T[name][l]
        return get

    packs = [_pack(param_get(T), D) for T in (W, Mo, Vo)]
    ncr = DEPTH * (SHORT_K + CFM_K)
    padr = (-ncr) % 8
    convw_wmv = [jnp.pad(jnp.concatenate([T["w_short"].reshape(-1, ncs), T["cfm_conv_w"].reshape(-1, ncs)]),
                         ((0, padr), (0, 0))) for T in (W, Mo, Vo)]
    early_work = [ag_rest0["tok"], *packs, *convw_wmv] + [a for cl in consts for a in cl.values()]
    ag_in0 = _gather_mid(ag_in0, early_work, "ag_w_in0")
    Wg = [None, None]
    ag_l1 = None
    nin = w_in.shape[2]
    tn_in = nin if nin % 256 == 0 and nin <= 1280 else 256
    tn_fi = 512 if F2 % 512 == 0 else 256
    tn_ffn = 1408 if F2 % 1408 == 0 else tn_fi
    tn_dw = min(256, D)

    saved = []
    xcur, fprev, gprev = x0, None, None
    for l in range(DEPTH):
        sh1, sc1, g1, sh2, sc2, g2 = [mod[l, k] for k in range(N_MOD)]
        cl = consts[l]
        vec1 = _rows(jnp.zeros((D,), F32) if gprev is None else gprev, norm1_g[l], sc1, sh1)
        if l == 0:
            xl, h, ht = _norm_fwd(xcur, fprev, vec1, f"norm1_fwd{l}", deps=(ag_in0["tok"],))
            Wg[0] = dict(w_in=_gather_finish(ag_in0, h, "ag_w_in0")[0])
        else:
            ag_l1 = _gather_mid(ag_l1, fprev, f"ag_w{l}")
            xl, h, ht = _norm_fwd(xcur, fprev, vec1, f"norm1_fwd{l}", deps=(ag_l1["tok"],))
            g = _gather_finish(ag_l1, h, f"ag_w{l}")
            Wg[l] = dict(w_in=g[0], **rest_of(g[1:]))
        wl = Wg[l]
        z = _mm_nn(h, wl["w_in"], BF16, tm_big, tn_in, D, f"mm_in{l}", w_outer=True)
        mix_deps = ()
        if l == 0:
            ag_rest0 = _gather_mid(ag_rest0, z, "ag_rest0")
            mix_deps = (ag_rest0["tok"],)
            if DEPTH > 1:
                ag_l1 = _gather_start(shards_of(1), dev, "ag_w1")
                mix_deps += (ag_l1["tok"],)
        acts, acts_t, conv = _mixer_fwd(z, cl["wsh"], cl["sgu_ln"], cl["wtril"], cl["bias_full"], cl["cw"], cl["cvec"],
                                        f"mixer_fwd{l}", deps=mix_deps)
        if l == 0:
            wl.update(rest_of(_gather_finish(ag_rest0, acts[0], "ag_rest0")))
        merged, merged_t, ys = _branch_out(acts, [wl["w_a"][0], wl["w_b"][0], wl["w_c"][0]], z, f"branch_out{l}")
        o = _mm_nn(merged, wl["w_o"], F32, tm_big, D, D, f"mm_o{l}")
        x1, h2, h2t = _norm_fwd(xl, o, _rows(g1, norm2_g[l], sc2, sh2), f"norm2_fwd{l}")
        gu, act, act_t = _ffn_in_swiglu(h2, wl["w_fi"], tm, tn_ffn, f"mm_ffn_in{l}")
        f = _mm_nn(act, wl["w_fo"], F32, tm, D, FF, f"mm_ffn_out{l}")
        saved.append(dict(xl=xl, ht=ht, z=z, acts_t=acts_t, conv=conv, ys=ys, merged_t=merged_t, o=o, x1=x1, h2t=h2t, gu=gu,
                          act_t=act_t, f=f, consts=cl, mod=(sh1, sc1, g1, sh2, sc2, g2)))
        xcur, fprev, gprev = x1, f, g2

    last = saved[-1]
    dxup, dfb, fsums, loss_blk = _final_bwd(last["x1"], last["f"], tgt, _rows(last["mod"][5], final_g), "final_bwd")
    loss_row = jnp.pad(loss_blk[0, 0:1], (0, D - 1))
    dgate2_next = fsums[1]
    small = [dict() for _ in range(DEPTH)]
    dmods = [None] * DEPTH
    nfi = w_ffn_in.shape[2]
    early_names, late_names = ["w_ffn_out", "w_ffn_in", "w_o"], ["w_a_out", "w_b_out", "w_c_out", "w_in"]
    results = {n: None for n in early_names + late_names}

    def adam_group(names, Ps, R2s, l, deps=()):
        for n, p, r2 in zip(names, Ps, R2s):
            results[n] = _adam_big(p, r2, my_chip, W[n], Mo[n], Vo[n], l, results[n], f"adam_{n}{l}", deps)

    deferred = []
    late_prev = None
    ag_s1, gathered1 = None, None
    tk_w = min(2048, S)
    tn_dw_in = tn_in // 2 if tn_in == 1280 else tn_in
    for l in reversed(range(DEPTH)):
        sv, wl, cl = saved[l], Wg[l], saved[l]["consts"]
        sh1, sc1, g1, sh2, sc2, g2 = sv["mod"]
        dact = _mm_nt(dfb, wl["w_fo"], BF16, tm, FF, D, f"mm_dact{l}",
                      deps=() if late_prev is None else (late_prev["tok"], ag_s1["tok"]))
        g_fo = _mm_wgrad(sv["act_t"], dfb, 1, FF // 2, D, tk_w, f"mm_dw_ffn_out{l}")
        dgu = _swiglu_bwd(dact, sv["gu"], f"swiglu_bwd{l}")
        dh2 = _mm_nt(dgu, wl["w_fi"], F32, tm, D, F2, f"mm_dh2{l}")
        if late_prev is not None:
            deferred.append((late_names, *_scatter_finish(late_prev, dh2, f"rs_late{l + 1}"), l + 1))
            late_prev = None
        g_fi = _mm_wgrad(sv["h2t"], dgu, 1, D, tn_fi, S, f"mm_dw_ffn_in{l}")
        if ag_s1 is not None:
            ag_s1 = _gather_mid(ag_s1, g_fi, "ag_small1")
        dx1, dob, s2 = _norm_bwd(sv["x1"], dh2, dxup, _rows(norm2_g[l], sc2, g1), sv["o"], f"norm2_bwd{l}",
                                 deps=() if ag_s1 is None else (ag_s1["tok"],))
        dmerged = _mm_nt(dob, wl["w_o"], BF16, tm_big, D, D, f"mm_dmerged{l}")
        g_o = _mm_wgrad(sv["merged_t"], dob, 1, D, tn_dw, S, f"mm_dw_o{l}")
        early = _scatter_start([g_fo.reshape(NDEV, FF // NDEV, D),
                                jnp.transpose(g_fi.reshape(D, NDEV, nfi), (1, 0, 2)),
                                g_o.reshape(NDEV, D // NDEV, D)], f"rs_early{l}")
        dys, dz = _gate_bwd(dmerged, sv["z"], sv["ys"], f"gate_bwd{l}", deps=(early["tok"],))
        if ag_s1 is not None:
            gathered1 = _gather_finish(ag_s1, dys[0], "ag_small1")[0]
            ag_s1 = None
        early = _scatter_mid(early, dys[0], my_c, f"rs_early{l}")
        dacts, g_abc = [], []
        for n, key in enumerate(("w_a", "w_b", "w_c")):
            dacts.append(_mm_nt(dys[n], wl[key], BF16, tm_big, D, D, f"mm_dact_{key}{l}",
                                deps=(early["tok"],) if n == 0 else ()))
            g_abc.append(_mm_wgrad(sv["acts_t"][n], dys[n], 1, D, tn_dw, S, f"mm_d{key}{l}"))
        dz, mvec, dcw, dws, dbs = _mixer_bwd(sv["z"], dacts, sv["conv"], dz, cl["wsh"], cl["sgu_ln"], cl["wtril"],
                                             cl["wtril_t"], cl["bias_full"], cl["cw"], cl["cvec"], f"mixer_bwd{l}")
        dh = _mm_nt(dz, wl["w_in"], F32, tm_big, D, tn_in, f"mm_dh{l}")
        g_in = _mm_wgrad(sv["ht"], dz, NDEV, D, tn_dw_in, S, f"mm_dw_in{l}")
        late = _scatter_start([g.reshape(NDEV, D // NDEV, D) for g in g_abc] + [g_in], f"rs_late{l}")
        if l > 0:
            pv = saved[l - 1]
            dxup, dfb, s1 = _norm_bwd(sv["xl"], dh, dx1, _rows(norm1_g[l], sc1, pv["mod"][5]), pv["f"], f"norm1_bwd{l}",
                                      deps=(late["tok"],))
        else:
            dxup, dfb, s1 = _norm_bwd(sv["xl"], dh, dx1, _rows(norm1_g[l], sc1), None, f"norm1_bwd{l}", deps=(late["tok"],))
        deferred.append((early_names, *_scatter_finish(early, dxup, f"rs_early{l}"), l))
        dmods[l] = jnp.stack([s1[0], s1[1], s2[3], s2[0], s2[1], dgate2_next])
        dgate2_next = s1[3]
        small[l] = dict(norm1_g=s1[2], norm2_g=s2[2], sgu_ln_g=mvec[3], sgu_ln_b=mvec[4], cfm_conv_b=mvec[5],
                        cfm_ln_g=mvec[6], cfm_ln_b=mvec[7], b_sgu=dbs[:, :, 0],
                        w_sgu=jnp.where(tril[None], dws, 0.0), b_ada=dmods[l], w_short=mvec[0:SHORT_K],
                        cfm_conv_w=dcw[0:CFM_K])
        small_get = lambda name, k: {"final_g": fsums[0], "loss": loss_row}.get(name) if k is None else small[k][name]
        if l > 0:
            late_prev = _scatter_mid(late, dxup, my_c, f"rs_late{l}")
            ag_s1 = _gather_start([_pack(small_get, D, layers=(l,), tail=True)], dev, "ag_small1", deps=(late_prev["tok"],))
    grad_x = dxup.reshape(x.shape)

    gathered0 = _all_gather([_pack(small_get, D, layers=(0,), tail=False)], "ag_small0", deps=(dxup,))[0]
    late_prev = _scatter_mid(late, gathered0, my_c, "rs_late0")
    gathered = jnp.concatenate([gathered0, gathered1], axis=1)
    sg, sd, sm, sv_ = _adam_small(gathered, *packs, name="adam_small", deps=(late_prev["tok"],))
    loss = sg[FINAL_ROW + 1, 0]
    out = {}
    for name in order:
        if name in SMALL_ROWS and name not in sharded_small:
            out[name] = tuple(_unpack(p, name, W[name].shape) for p in (sg, sd, sm, sv_))
    out["final_g"] = tuple(p[FINAL_ROW] for p in (sg, sd, sm, sv_))

    def my_cols(name):
        full = _unpack(sg, name, (DEPTH, SMALL_ROWS[name][1], D))
        return lax.dynamic_slice_in_dim(full, dev * ncs, ncs, axis=2)

    gcs = jnp.concatenate([my_cols("w_short").reshape(-1, ncs), my_cols("cfm_conv_w").reshape(-1, ncs)])
    cd, cm, cv = _adam_plain(jnp.pad(gcs, ((0, padr), (0, 0))), *convw_wmv, "adam_convw")
    nsh = DEPTH * SHORT_K
    out["w_short"] = tuple(a[0:nsh].reshape(w_short.shape) for a in (gcs, cd, cm, cv))
    out["cfm_conv_w"] = tuple(a[nsh:ncr].reshape(cfm_conv_w.shape) for a in (gcs, cd, cm, cv))

    dm_all = jnp.stack([gathered[:, l * ROWS_PER_LAYER + 136:l * ROWS_PER_LAYER + 136 + N_MOD, :].reshape(NDEV, N_MOD * D)
                        for l in range(DEPTH)])
    dm_mine = lax.dynamic_slice_in_dim(dm_all, dev * ncol, ncol, axis=2)
    out["w_ada"] = tuple(_adam_ada(jnp.transpose(c_act), dm_mine, w_ada, m_w_ada, v_w_ada, "adam_ada"))

    for names, Ps, R2s, l in deferred:
        adam_group(names, Ps, R2s, l, deps=(late_prev["tok"],))
    adam_group(late_names, *_scatter_finish(late_prev, results["w_o"][0], "rs_late0"), 0)
    for n in early_names + late_names:
        out[n] = tuple(results[n])

    grads = [out[n][0] for n in order]
    deltas = [out[n][1] for n in order]
    new_m = [out[n][2] for n in order]
    new_v = [out[n][3] for n in order]
    return (loss, grad_x, *grads, *deltas, *new_m, *new_v)
```

```python
import functools
import math

import jax
import jax.numpy as jnp
from jax import lax
from jax.experimental import pallas as pl
from jax.experimental.pallas import tpu as pltpu

F32, BF16 = jnp.float32, jnp.bfloat16
NDEV = 8
NCHIP = NDEV // 2
DEPTH = 2
EPS = 1e-6
CHUNK = 128
NG = 8
SHORT_K = 3
CFM_K = 31
HALO = 32
N_MOD = 6
LANE = 128
VMEM_LIMIT = 56 * 1024 * 1024
ADAM_LR, ADAM_B1, ADAM_B2, ADAM_EPS, ADAM_WD, ADAM_STEP = 0.001, 0.9, 0.999, 1e-08, 0.01, 10
_G0 = math.sqrt(2.0 / math.pi)
_G1 = 0.044715
MESH = pl.DeviceIdType.MESH
ANY = pl.BlockSpec(memory_space=pl.ANY)


def _pcall(body, **kw):
    return pl.pallas_call(body, **kw)


def _params(sem=None):
    return pltpu.CompilerParams(dimension_semantics=sem, vmem_limit_bytes=VMEM_LIMIT)


def _sds(shape, dtype):
    return jax.ShapeDtypeStruct(tuple(shape), dtype)


def _mm_body(dims, nk, out_f32):
    def body(a_ref, b_ref, o_ref, *scr):
        k = pl.program_id(2)
        part = lax.dot_general(a_ref[...], b_ref[...], dims, preferred_element_type=F32)
        if nk == 1:
            o_ref[...] = part.reshape(o_ref.shape).astype(o_ref.dtype)
        elif out_f32:
            @pl.when(k == 0)
            def _():
                o_ref[...] = part.reshape(o_ref.shape)

            @pl.when(k > 0)
            def _():
                o_ref[...] += part.reshape(o_ref.shape)
        else:
            acc = scr[0]

            @pl.when(k == 0)
            def _():
                acc[...] = part

            @pl.when(k > 0)
            def _():
                acc[...] += part

            @pl.when(k == nk - 1)
            def _():
                o_ref[...] = acc[...].astype(o_ref.dtype)
    return body


def _after(body, n_in, deps):
    nd = len(deps)
    if nd == 0:
        return body

    def ordered(*refs):
        return body(*refs[:n_in], *refs[n_in + nd:])
    return ordered


def _mm_call(body, grid, in_specs, out_spec, out_shape, acc_shape, name, deps=()):
    scratch = [] if acc_shape is None else [pltpu.VMEM(acc_shape, F32)]
    return _pcall(_after(body, 2, deps), grid=grid, in_specs=in_specs + [ANY] * len(deps), out_specs=out_spec,
                  out_shape=out_shape, scratch_shapes=scratch, name=name,
                  compiler_params=_params(("parallel", "parallel", "arbitrary")))


def _mm_nn(a, b3, out_dtype, tm, tn, tk, name, w_outer=False, deps=()):
    M, K = a.shape
    G, _, Nb = b3.shape
    npb, nk = Nb // tn, K // tk
    out_f32 = out_dtype == F32
    body = _mm_body((((1,), (0,)), ((), ())), nk, out_f32)
    if w_outer:
        grid = (G * npb, M // tm, nk)
        ij = lambda p, q: (q, p)
    else:
        grid = (M // tm, G * npb, nk)
        ij = lambda p, q: (p, q)

    def a_map(p, q, k):
        i, j = ij(p, q)
        return (i, k)

    def b_map(p, q, k):
        i, j = ij(p, q)
        return (j // npb, k, j % npb)

    def o_map(p, q, k):
        return ij(p, q)

    def wrapped(a_ref, b_ref, o_ref, *scr):
        body(a_ref, b_ref, o_ref, *scr)

    return _mm_call(wrapped, grid, [pl.BlockSpec((tm, tk), a_map), pl.BlockSpec((None, tk, tn), b_map)],
                    pl.BlockSpec((tm, tn), o_map), _sds((M, G * Nb), out_dtype),
                    None if (nk == 1 or out_f32) else (tm, tn), name, deps)(a, b3, *deps)


def _mm_nt(a, b3, out_dtype, tm, tn, tk, name, deps=()):
    M, _ = a.shape
    G, Ko, Nb = b3.shape
    kpb = Nb // tk
    nk = G * kpb
    out_f32 = out_dtype == F32
    body = _mm_body((((1,), (1,)), ((), ())), nk, out_f32)

    def wrapped(a_ref, b_ref, o_ref, *scr):
        body(a_ref, b_ref, o_ref, *scr)

    return _mm_call(wrapped, (M // tm, Ko // tn, nk),
                    [pl.BlockSpec((tm, tk), lambda i, j, k: (i, k)),
                     pl.BlockSpec((None, tn, tk), lambda i, j, k: (k // kpb, j, k % kpb))],
                    pl.BlockSpec((tm, tn), lambda i, j, k: (i, j)), _sds((M, Ko), out_dtype),
                    None if (nk == 1 or out_f32) else (tm, tn), name, deps)(a, b3, *deps)


def _mm_wgrad(at, b, G, tm, tn, tk, name, deps=()):
    M, T = at.shape
    Nb = b.shape[1] // G
    npb, nk = Nb // tn, T // tk
    body = _mm_body((((1,), (0,)), ((), ())), nk, False)

    def wrapped(a_ref, b_ref, o_ref, *scr):
        body(a_ref, b_ref, o_ref, *scr)

    a = at
    in_specs = [pl.BlockSpec((tm, tk), lambda i, j, k: (i, k)), pl.BlockSpec((tk, tn), lambda i, j, k: (k, j))]
    out_spec = pl.BlockSpec((None, tm, tn), lambda i, j, k: (j // npb, i, j % npb))
    return _mm_call(wrapped, (M // tm, G * npb, nk), in_specs, out_spec, _sds((G, M, Nb), BF16),
                    None if nk == 1 else (tm, tn), name, deps)(a, b, *deps)


def _rsum(v):
    return jnp.sum(v, axis=0, keepdims=True)


def _rmean(v):
    return jnp.mean(v, axis=-1, keepdims=True)


def _gelu(x):
    t = jnp.tanh(_G0 * (x + _G1 * (x * x * x)))
    return x * (0.5 * (1.0 + t)), t


def _dgelu(x, t):
    return 0.5 * (1.0 + t) + 0.5 * x * (1.0 - t * t) * (_G0 * (1.0 + 3.0 * _G1 * (x * x)))


def _sigmoid(x):
    return 1.0 / (1.0 + jnp.exp(-x))


def _fill_shifted(ext, rot):
    v = ext[...]
    n = v.shape[0]
    for b in range(1, 8):
        rot[b - 1] = pltpu.roll(v, n - b, 0)


def _rows_at(ext, rot, s, tm, cs=slice(None)):
    a, b = divmod(s, 8)
    return ext[8 * a:8 * a + tm, cs] if b == 0 else rot[b - 1, 8 * a:8 * a + tm, cs]


def _causal_conv(w_ref, taps, bias, ext, rot, offset, tm, out):
    D = out.shape[1]
    for cb in range(D // LANE):
        cs = slice(cb * LANE, (cb + 1) * LANE)
        acc = None
        for k, o in zip(taps, offset):
            term = w_ref[k:k + 1, cs] * _rows_at(ext, rot, o, tm, cs)
            acc = term if acc is None else acc + term
        out[:, cs] = acc if bias is None else acc + bias[:, cs]


def _rows(*vs):
    a = jnp.stack([v.astype(F32) for v in vs])
    return jnp.pad(a, ((0, 8 - len(vs)), (0, 0)))


def _row_spec(tm, D):
    return pl.BlockSpec((tm, D), lambda i: (i, 0))


def _const_spec(shape):
    nd = len(shape)
    return pl.BlockSpec(shape, lambda i: (0,) * nd)


def _norm_fwd(xp, f, vec, name, deps=()):
    S, D = xp.shape
    tm = min(256, S)
    has_f = f is not None

    def body(*refs):
        if has_f:
            xp_ref, f_ref, vec_ref, xo_ref, h_ref, ht_ref = refs
            x = xp_ref[...] + vec_ref[0:1, :] * f_ref[...]
            xo_ref[...] = x
        else:
            xp_ref, vec_ref, h_ref, ht_ref = refs
            x = xp_ref[...]
        r = lax.rsqrt(_rmean(x * x) + EPS)
        h = (x * r) * vec_ref[1:2, :]
        h = h * (1.0 + vec_ref[2:3, :]) + vec_ref[3:4, :]
        h_ref[...] = h.astype(BF16)
        ht_ref[...] = h.T.astype(BF16)

    rs = _row_spec(tm, D)
    ins = [xp, f, vec] if has_f else [xp, vec]
    in_specs = ([rs, rs] if has_f else [rs]) + [_const_spec((8, D))]
    out_shape = ([_sds((S, D), F32)] if has_f else []) + [_sds((S, D), BF16), _sds((D, S), BF16)]
    out_specs = [rs] * (len(out_shape) - 1) + [pl.BlockSpec((D, tm), lambda i: (0, i))]
    outs = _pcall(_after(body, len(ins), deps), grid=(S // tm,), in_specs=in_specs + [ANY] * len(deps),
                  out_specs=out_specs, out_shape=out_shape, name=name,
                  compiler_params=_params(("parallel",)))(*ins, *deps)
    return (outs[0], outs[1], outs[2]) if has_f else (xp, outs[0], outs[1])


def _mixer_fwd(z, wsh, sgu_ln, wtril, bias_full, cw, cvec, name, deps=()):
    S = z.shape[0]
    D = wsh.shape[1]
    tm = CHUNK

    def body(z_ref, wsh_ref, sln_ref, wt_ref, bias_ref, cw_ref, cv_ref, oa_ref, ob_ref, oc_ref, ta_ref, tb_ref, tc_ref,
             conv_ref, pe, ge, gr, cbuf):
        i = pl.program_id(0)

        @pl.when(i == 0)
        def _():
            pe[0:HALO, :] = jnp.zeros((HALO, D), F32)
            ge[0:HALO, :] = jnp.zeros((HALO, D), F32)

        def col(n):
            return z_ref[:, n * D:(n + 1) * D].astype(F32)

        pe[HALO:HALO + tm, :] = col(1) * col(2)
        q = wsh_ref[0:1, :] * pe[HALO - 2:HALO - 2 + tm, :]
        q = q + wsh_ref[1:2, :] * pe[HALO - 1:HALO - 1 + tm, :]
        q = q + wsh_ref[2:3, :] * pe[HALO:HALO + tm, :]
        act_a = col(0) * q
        oa_ref[...] = act_a.astype(BF16)
        ta_ref[...] = act_a.T.astype(BF16)
        gu, _ = _gelu(col(3))
        gv, _ = _gelu(col(4))
        d = gv - _rmean(gv)
        nrm = d * lax.rsqrt(_rmean(d * d) + EPS)
        vnb = (nrm * sln_ref[0:1, :] + sln_ref[1:2, :]).astype(BF16)
        for g in range(NG):
            cs = slice(g * LANE, (g + 1) * LANE)
            mixed = jnp.dot(wt_ref[g], vnb[:, cs], preferred_element_type=F32) + bias_ref[:, cs]
            act_b = gu[:, cs] * mixed
            ob_ref[:, cs] = act_b.astype(BF16)
            tb_ref[cs, :] = act_b.T.astype(BF16)
        ge[HALO:HALO + tm, :] = col(5) * _sigmoid(col(6))
        _fill_shifted(ge, gr)
        o0 = HALO - (CFM_K - 1)
        _causal_conv(cw_ref, range(CFM_K), cv_ref[0:1, :], ge, gr, range(o0, o0 + CFM_K), tm, cbuf)
        conv = cbuf[...]
        conv_ref[...] = conv.astype(BF16)
        d = conv - _rmean(conv)
        ln = (d * lax.rsqrt(_rmean(d * d) + EPS)) * cv_ref[1:2, :] + cv_ref[2:3, :]
        act_c = ln * _sigmoid(ln)
        oc_ref[...] = act_c.astype(BF16)
        tc_ref[...] = act_c.T.astype(BF16)
        pe[0:HALO, :] = pe[tm:tm + HALO, :]
        ge[0:HALO, :] = ge[tm:tm + HALO, :]

    rs = _row_spec(tm, D)
    outs = _pcall(
        _after(body, 7, deps), grid=(S // tm,),
        in_specs=[pl.BlockSpec((tm, 7 * D), lambda i: (i, 0)), _const_spec((8, D)), _const_spec((8, D)),
                  _const_spec((NG, CHUNK, CHUNK)), _const_spec((CHUNK, D)), _const_spec((HALO, D)), _const_spec((8, D))]
        + [ANY] * len(deps),
        out_specs=[rs, rs, rs] + [pl.BlockSpec((D, tm), lambda i: (0, i))] * 3 + [rs],
        out_shape=[_sds((S, D), BF16)] * 3 + [_sds((D, S), BF16)] * 3 + [_sds((S, D), BF16)],
        scratch_shapes=[pltpu.VMEM((HALO + tm, D), F32), pltpu.VMEM((HALO + tm, D), F32),
                        pltpu.VMEM((7, HALO + tm, D), F32), pltpu.VMEM((tm, D), F32)],
        name=name, compiler_params=_params(("arbitrary",)))(z, wsh, sgu_ln, wtril, bias_full, cw, cvec, *deps)
    return outs[:3], outs[3:6], outs[6]


def _branch_out(acts, ws, z, name):
    S, D = acts[0].shape
    tm = min(256, S)

    def body(a0, a1, a2, w0, w1, w2, g0, g1, g2, m_ref, mt_ref, y_ref):
        m = None
        for n, (a, w, g) in enumerate(((a0, w0, g0), (a1, w1, g1), (a2, w2, g2))):
            y = jnp.dot(a[...], w[...], preferred_element_type=F32)
            y_ref[n] = y.astype(BF16)
            t = _sigmoid(g[...].astype(F32)) * y
            m = t if m is None else m + t
        m_ref[...] = m.astype(BF16)
        mt_ref[...] = m.T.astype(BF16)

    rs = _row_spec(tm, D)
    gate_specs = [pl.BlockSpec((tm, D), functools.partial(lambda i, n: (i, 7 + n), n=n)) for n in range(3)]
    return _pcall(body, grid=(S // tm,),
                  in_specs=[rs, rs, rs] + [_const_spec((D, D))] * 3 + gate_specs,
                  out_specs=[rs, pl.BlockSpec((D, tm), lambda i: (0, i)), pl.BlockSpec((3, tm, D), lambda i: (0, i, 0))],
                  out_shape=[_sds((S, D), BF16), _sds((D, S), BF16), _sds((3, S, D), BF16)], name=name,
                  compiler_params=_params(("parallel",)))(*acts, *ws, z, z, z)


def _ffn_in_swiglu(h2, w3, tm, tn, name):
    S, D = h2.shape
    F = w3.shape[2] // 2
    nj = F // tn

    def body(a_ref, wg_ref, wu_ref, gu_ref, act_ref, actt_ref):
        a = a_ref[...]
        g = jnp.dot(a, wg_ref[...], preferred_element_type=F32)
        u = jnp.dot(a, wu_ref[...], preferred_element_type=F32)
        gu_ref[0] = g.astype(BF16)
        gu_ref[1] = u.astype(BF16)
        act = (g * _sigmoid(g)) * u
        act_ref[...] = act.astype(BF16)
        actt_ref[...] = act.T.astype(BF16)

    return _pcall(body, grid=(S // tm, nj),
                  in_specs=[pl.BlockSpec((tm, D), lambda i, j: (i, 0)), pl.BlockSpec((None, D, tn), lambda i, j: (0, 0, j)),
                            pl.BlockSpec((None, D, tn), lambda i, j: (0, 0, j + nj))],
                  out_specs=[pl.BlockSpec((2, tm, tn), lambda i, j: (0, i, j)), pl.BlockSpec((tm, tn), lambda i, j: (i, j)),
                             pl.BlockSpec((tn, tm), lambda i, j: (j, i))],
                  out_shape=[_sds((2, S, F), BF16), _sds((S, F), BF16), _sds((F, S), BF16)], name=name,
                  compiler_params=_params(("parallel", "parallel")))(h2, w3, w3)


def _swiglu_bwd(dact, gu, name):
    _, S, F = gu.shape
    F2 = 2 * F
    tm = min(256, S)

    def body(d_ref, g_ref, u_ref, o_ref):
        g = g_ref[...].astype(F32)
        sg = _sigmoid(g)
        d = d_ref[...].astype(F32)
        o_ref[:, 0:F] = (d * u_ref[...].astype(F32) * (sg * (1.0 + g * (1.0 - sg)))).astype(BF16)
        o_ref[:, F:2 * F] = (d * (g * sg)).astype(BF16)

    return _pcall(body, grid=(S // tm,),
                  in_specs=[pl.BlockSpec((tm, F), lambda i: (i, 0)), pl.BlockSpec((None, tm, F), lambda i: (0, i, 0)),
                            pl.BlockSpec((None, tm, F), lambda i: (1, i, 0))],
                  out_specs=pl.BlockSpec((tm, F2), lambda i: (i, 0)), out_shape=_sds((S, F2), BF16), name=name,
                  compiler_params=_params(("parallel",)))(dact, gu, gu)


def _final_bwd(x1, f, tgt, vec, name):
    S, D = x1.shape
    tm = min(256, S)

    def body(x_ref, f_ref, t_ref, vec_ref, dx_ref, df_ref, sums_ref, loss_ref):
        @pl.when(pl.program_id(0) == 0)
        def _():
            sums_ref[...] = jnp.zeros_like(sums_ref)
            loss_ref[...] = jnp.zeros_like(loss_ref)

        gate, fg = vec_ref[0:1, :], vec_ref[1:2, :]
        fv = f_ref[...]
        x = x_ref[...] + gate * fv
        r = lax.rsqrt(_rmean(x * x) + EPS)
        xn = x * r
        diff = xn * fg - t_ref[...]
        per_tok = _rmean(diff * diff)
        loss_ref[...] += 0.5 * jnp.sum(per_tok, axis=0, keepdims=True)
        dy = diff * (1.0 / D)
        sums_ref[0:1, :] += _rsum(dy * xn)
        dxn = dy * fg
        dx = r * (dxn - xn * _rmean(dxn * xn))
        sums_ref[1:2, :] += _rsum(dx * fv)
        dx_ref[...] = dx
        df_ref[...] = (dx * gate).astype(BF16)

    rs = _row_spec(tm, D)
    return _pcall(body, grid=(S // tm,), in_specs=[rs, rs, rs, _const_spec((8, D))],
                  out_specs=[rs, rs, _const_spec((8, D)), _const_spec((8, LANE))],
                  out_shape=[_sds((S, D), F32), _sds((S, D), BF16), _sds((8, D), F32), _sds((8, LANE), F32)],
                  name=name, compiler_params=_params(("arbitrary",)))(x1, f, tgt, vec)


def _norm_bwd(xin, dh, dxup, vec, fprev, name, deps=()):
    S, D = xin.shape
    tm = min(256, S)
    has_prev = fprev is not None

    def body(*refs):
        if has_prev:
            x_ref, dh_ref, up_ref, vec_ref, fp_ref, dx_ref, dp_ref, sums_ref = refs
        else:
            x_ref, dh_ref, up_ref, vec_ref, dx_ref, sums_ref = refs

        @pl.when(pl.program_id(0) == 0)
        def _():
            sums_ref[...] = jnp.zeros_like(sums_ref)

        g, scale = vec_ref[0:1, :], vec_ref[1:2, :]
        x = x_ref[...]
        r = lax.rsqrt(_rmean(x * x) + EPS)
        xn = x * r
        dhv = dh_ref[...]
        sums_ref[0:1, :] += _rsum(dhv)
        sums_ref[1:2, :] += _rsum(dhv * (xn * g))
        dm = dhv * (1.0 + scale)
        sums_ref[2:3, :] += _rsum(dm * xn)
        dxn = dm * g
        dx = up_ref[...] + r * (dxn - xn * _rmean(dxn * xn))
        dx_ref[...] = dx
        if has_prev:
            sums_ref[3:4, :] += _rsum(dx * fp_ref[...])
            dp_ref[...] = (dx * vec_ref[2:3, :]).astype(BF16)

    rs = _row_spec(tm, D)
    ins = [xin, dh, dxup, vec] + ([fprev] if has_prev else [])
    in_specs = [rs, rs, rs, _const_spec((8, D))] + ([rs] if has_prev else [])
    out_shape = [_sds((S, D), F32)] + ([_sds((S, D), BF16)] if has_prev else []) + [_sds((8, D), F32)]
    out_specs = [rs] + ([rs] if has_prev else []) + [_const_spec((8, D))]
    outs = _pcall(_after(body, len(ins), deps), grid=(S // tm,), in_specs=in_specs + [ANY] * len(deps),
                  out_specs=out_specs, out_shape=out_shape, name=name,
                  compiler_params=_params(("arbitrary",)))(*ins, *deps)
    return (outs[0], outs[1], outs[2]) if has_prev else (outs[0], None, outs[1])


def _gate_bwd(dmerged, z, ys, name, deps=()):
    S, D = dmerged.shape
    tm = min(512, S)
    ncol = z.shape[1] // D

    def body(dm_ref, g_ref, y_ref, dya_ref, dyb_ref, dyc_ref, dz_ref):
        n = pl.program_id(1)
        sg = _sigmoid(g_ref[...].astype(F32))
        dm = dm_ref[...].astype(F32)
        dy = (dm * sg).astype(BF16)
        for k, ref in enumerate((dya_ref, dyb_ref, dyc_ref)):
            @pl.when(n == k)
            def _(ref=ref):
                ref[...] = dy
        dz_ref[...] = (dm * y_ref[...].astype(F32) * (sg * (1.0 - sg))).astype(BF16)

    row = pl.BlockSpec((tm, D), lambda i, n: (i, 0))
    outs = _pcall(_after(body, 3, deps), grid=(S // tm, 3),
                  in_specs=[row, pl.BlockSpec((tm, D), lambda i, n: (i, 7 + n)),
                            pl.BlockSpec((None, tm, D), lambda i, n: (n, i, 0))] + [ANY] * len(deps),
                  out_specs=[row, row, row, pl.BlockSpec((tm, D), lambda i, n: (i, 7 + n))],
                  out_shape=[_sds((S, D), BF16)] * 3 + [_sds((S, ncol * D), BF16)], name=name,
                  compiler_params=_params(("parallel", "arbitrary")))(dmerged, z, ys, *deps)
    return outs[:3], outs[3]


def _mixer_bwd(z, dacts, conv, dz, wsh, sgu_ln, wtril, wtril_t, bias_full, cw, cvec, name):
    S = z.shape[0]
    D = wsh.shape[1]
    tm = CHUNK
    nt = S // tm
    hb = tm // HALO

    def body(zc, zp, da_ref, db_ref, dc_ref, conv_ref, wsh_ref, sln_ref, wt_ref, wtt_ref, bias_ref, cw_ref, cv_ref, _dz_in,
             dz_ref, vec_ref, dcw_ref, dws_ref, dbs_ref, pe, ge, dqe, dce, gr, dcr, cbuf, dcw8):
        i = pl.program_id(0)
        rb = nt - 1 - i

        @pl.when(i == 0)
        def _():
            vec_ref[...] = jnp.zeros_like(vec_ref)
            dcw8[...] = jnp.zeros_like(dcw8)
            dws_ref[...] = jnp.zeros_like(dws_ref)
            dbs_ref[...] = jnp.zeros_like(dbs_ref)
            dqe[tm:tm + HALO, :] = jnp.zeros((HALO, D), F32)
            dce[tm:tm + HALO, :] = jnp.zeros((HALO, D), F32)

        keep = (rb > 0).astype(F32)

        def col(n):
            return zc[:, n * D:(n + 1) * D].astype(F32)

        def pcol(n):
            return zp[:, n * D:(n + 1) * D].astype(F32)

        c_a, x_a = col(1), col(2)
        pe[0:HALO, :] = keep * (pcol(1) * pcol(2))
        pe[HALO:HALO + tm, :] = c_a * x_a
        q = wsh_ref[0:1, :] * pe[HALO - 2:HALO - 2 + tm, :]
        q = q + wsh_ref[1:2, :] * pe[HALO - 1:HALO - 1 + tm, :]
        q = q + wsh_ref[2:3, :] * pe[HALO:HALO + tm, :]
        dact = da_ref[...].astype(F32)
        dz_ref[:, 0:D] = (dact * q).astype(BF16)
        dq = dact * col(0)
        dqe[0:tm, :] = dq
        dp = wsh_ref[2:3, :] * dq + wsh_ref[1:2, :] * dqe[1:1 + tm, :] + wsh_ref[0:1, :] * dqe[2:2 + tm, :]
        dz_ref[:, D:2 * D] = (dp * x_a).astype(BF16)
        dz_ref[:, 2 * D:3 * D] = (dp * c_a).astype(BF16)
        for k in range(SHORT_K):
            o = HALO - (SHORT_K - 1) + k
            vec_ref[k:k + 1, :] += _rsum(dq * pe[o:o + tm, :])
        u, v = col(3), col(4)
        gu, tu = _gelu(u)
        gv, tv = _gelu(v)
        d = gv - _rmean(gv)
        rstd = lax.rsqrt(_rmean(d * d) + EPS)
        nrm = d * rstd
        vnb = (nrm * sln_ref[0:1, :] + sln_ref[1:2, :]).astype(BF16)
        dact = db_ref[...].astype(F32)
        dvn_parts, dgu_parts = [], []
        for g in range(NG):
            cs = slice(g * LANE, (g + 1) * LANE)
            vg = vnb[:, cs]
            mixed = jnp.dot(wt_ref[g], vg, preferred_element_type=F32) + bias_ref[:, cs]
            dgu_parts.append(dact[:, cs] * mixed)
            dmixed = dact[:, cs] * gu[:, cs]
            dmb = dmixed.astype(BF16)
            dws_ref[g] += lax.dot_general(dmb, vg, (((1,), (1,)), ((), ())), preferred_element_type=F32)
            dbs_ref[g] += jnp.broadcast_to(jnp.sum(dmixed, axis=1, keepdims=True), (CHUNK, LANE))
            dvn_parts.append(jnp.dot(wtt_ref[g], dmb, preferred_element_type=F32))
        dgu = jnp.concatenate(dgu_parts, axis=1)
        dvn = jnp.concatenate(dvn_parts, axis=1)
        dz_ref[:, 3 * D:4 * D] = (dgu * _dgelu(u, tu)).astype(BF16)
        vec_ref[3:4, :] += _rsum(dvn * nrm)
        vec_ref[4:5, :] += _rsum(dvn)
        dn = dvn * sln_ref[0:1, :]
        dgv = rstd * (dn - _rmean(dn) - nrm * _rmean(dn * nrm))
        dz_ref[:, 4 * D:5 * D] = (dgv * _dgelu(v, tv)).astype(BF16)
        a_c = col(5)
        sg = _sigmoid(col(6))
        ge[0:HALO, :] = keep * (pcol(5) * _sigmoid(pcol(6)))
        ge[HALO:HALO + tm, :] = a_c * sg
        _fill_shifted(ge, gr)
        o0 = HALO - (CFM_K - 1)
        conv = conv_ref[...].astype(F32)
        d = conv - _rmean(conv)
        rstd = lax.rsqrt(_rmean(d * d) + EPS)
        nrm = d * rstd
        ln = nrm * cv_ref[1:2, :] + cv_ref[2:3, :]
        sl = _sigmoid(ln)
        dln = dc_ref[...].astype(F32) * (sl * (1.0 + ln * (1.0 - sl)))
        vec_ref[6:7, :] += _rsum(dln * nrm)
        vec_ref[7:8, :] += _rsum(dln)
        dn = dln * cv_ref[1:2, :]
        dconv = rstd * (dn - _rmean(dn) - nrm * _rmean(dn * nrm))
        vec_ref[5:6, :] += _rsum(dconv)
        dce[0:tm, :] = dconv
        _fill_shifted(dce, dcr)
        _causal_conv(cw_ref, range(CFM_K), None, dce, dcr, [CFM_K - 1 - k for k in range(CFM_K)], tm, cbuf)
        dglu = cbuf[...]
        for cb in range(D // LANE):
            cs = slice(cb * LANE, (cb + 1) * LANE)
            dcv = dce[0:tm, cs]
            for k in range(CFM_K):
                prod = dcv * _rows_at(ge, gr, o0 + k, tm, cs)
                dcw8[k, :, cs] += jnp.sum(prod.reshape(tm // 8, 8, LANE), axis=0)

        @pl.when(i == nt - 1)
        def _():
            dcw_ref[...] = jnp.sum(dcw8[...], axis=1)
        dz_ref[:, 5 * D:6 * D] = (dglu * sg).astype(BF16)
        dz_ref[:, 6 * D:7 * D] = (dglu * a_c * (sg * (1.0 - sg))).astype(BF16)
        dqe[tm:tm + HALO, :] = dqe[0:HALO, :]
        dce[tm:tm + HALO, :] = dce[0:HALO, :]

    rev = lambda i: (nt - 1 - i, 0)
    rs = pl.BlockSpec((tm, D), rev)
    cur = pl.BlockSpec((tm, 7 * D), rev)
    prev = pl.BlockSpec((HALO, 7 * D), lambda i: (jnp.maximum((nt - 1 - i) * hb - 1, 0), 0))
    ext = pltpu.VMEM((HALO + tm, D), F32)
    outs = _pcall(
        body, grid=(nt,),
        in_specs=[cur, prev, rs, rs, rs, rs, _const_spec((8, D)), _const_spec((8, D)), _const_spec((NG, CHUNK, CHUNK)),
                  _const_spec((NG, CHUNK, CHUNK)), _const_spec((CHUNK, D)), _const_spec((HALO, D)), _const_spec((8, D)),
                  ANY],
        out_specs=[cur, _const_spec((8, D)), _const_spec((HALO, D)), _const_spec((NG, CHUNK, CHUNK)),
                   _const_spec((NG, CHUNK, LANE))],
        out_shape=[_sds(dz.shape, BF16), _sds((8, D), F32), _sds((HALO, D), F32), _sds((NG, CHUNK, CHUNK), F32),
                   _sds((NG, CHUNK, LANE), F32)],
        scratch_shapes=[ext, ext, ext, ext, pltpu.VMEM((7, HALO + tm, D), F32), pltpu.VMEM((7, HALO + tm, D), F32),
                        pltpu.VMEM((tm, D), F32), pltpu.VMEM((HALO, 8, D), F32)],
        input_output_aliases={13: 0}, name=name,
        compiler_params=_params(("arbitrary",)))(z, z, *dacts, conv, wsh, sgu_ln, wtril, wtril_t, bias_full, cw, cvec, dz)
    return outs


def _ada_fwd(c_all, w_ada_loc, name):
    nb, D = c_all.shape
    L, _, nc = w_ada_loc.shape

    def body(c_ref, w_ref, o_ref, ca_ref):
        cv = c_ref[...]
        ca = cv * _sigmoid(cv)
        ca_ref[...] = ca
        o_ref[...] = jnp.dot(ca.astype(BF16), w_ref[...].astype(BF16), preferred_element_type=F32)

    return _pcall(body, grid=(L,),
                  in_specs=[_const_spec((nb, D)), pl.BlockSpec((None, D, nc), lambda l: (l, 0, 0))],
                  out_specs=[pl.BlockSpec((None, nb, nc), lambda l: (l, 0, 0)), _const_spec((nb, D))],
                  out_shape=[_sds((L, nb, nc), F32), _sds((nb, D), F32)], name=name,
                  compiler_params=_params(("arbitrary",)))(c_all, w_ada_loc)


def _adamw(w, g, m, v):
    m = ADAM_B1 * m + (1.0 - ADAM_B1) * g
    v = ADAM_B2 * v + (1.0 - ADAM_B2) * (g * g)
    m_hat = m / (1.0 - ADAM_B1 ** ADAM_STEP)
    v_hat = v / (1.0 - ADAM_B2 ** ADAM_STEP)
    delta = -ADAM_LR * (m_hat / (jnp.sqrt(v_hat) + ADAM_EPS) + ADAM_WD * w)
    return delta, m, v


def _tile_rows(R, C, align=8):
    cap = max(align, (1536 * 1024) // (4 * C))
    best = None
    for t in range(align, R + 1, align):
        if R % t == 0 and t <= cap:
            best = t
    return R if best is None else best


def _adam_ada(ct, dm, w, m, v, name):
    L, D, nc = w.shape
    nb = ct.shape[1]
    tr = _tile_rows(D, nc)

    def body(ct_ref, dm_ref, w_ref, m_ref, v_ref, g_ref, d_ref, mo_ref, vo_ref):
        g = ct_ref[:, 0:1] * dm_ref[0:1, :]
        for b in range(1, nb):
            g = g + ct_ref[:, b:b + 1] * dm_ref[b:b + 1, :]
        g_ref[...] = g
        d_ref[...], mo_ref[...], vo_ref[...] = _adamw(w_ref[...], g, m_ref[...], v_ref[...])

    ws = pl.BlockSpec((None, tr, nc), lambda l, r: (l, r, 0))
    return _pcall(body, grid=(L, D // tr),
                  in_specs=[pl.BlockSpec((tr, nb), lambda l, r: (r, 0)), pl.BlockSpec((None, nb, nc), lambda l, r: (l, 0, 0)),
                            ws, ws, ws],
                  out_specs=[ws] * 4, out_shape=[_sds(w.shape, F32)] * 4, name=name,
                  compiler_params=_params(("parallel", "parallel")))(ct, dm, w, m, v)


def _adam_small(parts, w, m, v, name, deps=()):
    n, R, C = parts.shape
    tr = _tile_rows(R, C * n // 2)

    def body(p_ref, w_ref, m_ref, v_ref, g_ref, d_ref, mo_ref, vo_ref):
        g = p_ref[0]
        for j in range(1, n):
            g = g + p_ref[j]
        g_ref[...] = g
        d_ref[...], mo_ref[...], vo_ref[...] = _adamw(w_ref[...], g, m_ref[...], v_ref[...])

    ws = pl.BlockSpec((tr, C), lambda r: (r, 0))
    return _pcall(_after(body, 4, deps), grid=(R // tr,),
                  in_specs=[pl.BlockSpec((n, tr, C), lambda r: (0, r, 0)), ws, ws, ws] + [ANY] * len(deps),
                  out_specs=[ws] * 4, out_shape=[_sds((R, C), F32)] * 4, name=name,
                  compiler_params=_params(("parallel",)))(parts, w, m, v, *deps)


def _adam_plain(g, w, m, v, name):
    R, C = w.shape

    def body(g_ref, w_ref, m_ref, v_ref, d_ref, mo_ref, vo_ref):
        d_ref[...], mo_ref[...], vo_ref[...] = _adamw(w_ref[...], g_ref[...], m_ref[...], v_ref[...])

    ws = _const_spec((R, C))
    return _pcall(body, grid=(1,), in_specs=[ws] * 4, out_specs=[ws] * 3, out_shape=[_sds((R, C), F32)] * 3, name=name,
                  compiler_params=_params(("arbitrary",)))(g, w, m, v)


def _pair_sum(G, R1, my_c, name):
    n, R, C = G.shape
    half = n // 2
    tr = _tile_rows(R, C, align=16)

    def body(c_ref, g_ref, r_ref, o_ref):
        o_ref[...] = (g_ref[...].astype(F32) + r_ref[...].astype(F32)).astype(o_ref.dtype)

    blk = (None, tr, C)
    gs = pltpu.PrefetchScalarGridSpec(
        num_scalar_prefetch=1, grid=(half, R // tr),
        in_specs=[pl.BlockSpec(blk, lambda p, r, c: (2 * p + c[0], r, 0)), pl.BlockSpec(blk, lambda p, r, c: (p, r, 0))],
        out_specs=pl.BlockSpec(blk, lambda p, r, c: (p, r, 0)))
    return _pcall(body, grid_spec=gs, out_shape=_sds((half, R, C), G.dtype), name=name,
                  compiler_params=_params(("parallel", "parallel")))(my_c, G, R1)


def _adam_big(P, R2, my_chip, w, m, v, layer, prev, name, deps=()):
    _, R, C = P.shape
    nrecv = R2.shape[0]
    tr = _tile_rows(R, C, align=16)

    def body(p_sm, p_ref, r_ref, w_ref, m_ref, v_ref, *rest):
        g_ref, d_ref, mo_ref, vo_ref = rest[-4:]
        g = p_ref[...].astype(F32)
        for k in range(nrecv):
            g = g + r_ref[k].astype(F32)
        g_ref[...] = g
        d_ref[...], mo_ref[...], vo_ref[...] = _adamw(w_ref[...], g, m_ref[...], v_ref[...])

    ws = pl.BlockSpec((None, tr, C), lambda r, p: (layer, r, 0))
    held = [] if prev is None else list(prev)
    gs = pltpu.PrefetchScalarGridSpec(
        num_scalar_prefetch=1, grid=(R // tr,),
        in_specs=[pl.BlockSpec((None, tr, C), lambda r, p: (p[0], r, 0)),
                  pl.BlockSpec((nrecv, tr, C), lambda r, p: (0, r, 0)), ws, ws, ws] + [ANY] * (len(held) + len(deps)),
        out_specs=[ws] * 4)
    alias = {6 + i: i for i in range(len(held))}
    return _pcall(body, grid_spec=gs, out_shape=[_sds(w.shape, F32)] * 4, name=name, input_output_aliases=alias,
                  compiler_params=_params(("parallel",)))(my_chip, P, R2, w, m, v, *held, *deps)


def _place():
    return lax.axis_index("x"), lax.axis_index("y"), lax.axis_index("c")


def _all_gather(shards, name, deps=()):
    n = len(shards)

    def body(*refs):
        ins, outs = refs[:n], refs[n:2 * n]
        send_sems, recv_sems, local_sems = refs[2 * n:]
        x, y, c = _place()
        me, sibling = (x, y, c), (x, y, 1 - c)
        chips = [(1 - x, y), (x, 1 - y), (1 - x, 1 - y)]

        def slot(a, px, py, pc):
            return outs[a].at[4 * px + 2 * py + pc]

        def copy(a, k, block, to, src=None):
            return pltpu.make_async_remote_copy(
                src_ref=slot(a, *block) if src is None else src, dst_ref=slot(a, *block),
                send_sem=send_sems.at[7 * a + k], recv_sem=recv_sems.at[7 * a + k], device_id=to, device_id_type=MESH)

        mine = [pltpu.make_async_copy(ins[a], slot(a, *me), local_sems.at[a]) for a in range(n)]
        for cp in mine:
            cp.start()
        first = []
        for a in range(n):
            first.append(copy(a, 0, me, sibling, src=ins[a]))
            first += [copy(a, 1 + j, me, (*chip, c), src=ins[a]) for j, chip in enumerate(chips)]
        for cp in first:
            cp.start()
        passed = []
        for j, chip in enumerate(chips):
            for a in range(n):
                copy(a, 1 + j, (*chip, c), me).wait_recv()
                fwd = copy(a, 4 + j, (*chip, c), sibling)
                fwd.start()
                passed.append(fwd)
        for a in range(n):
            copy(a, 0, sibling, me).wait_recv()
        for j, chip in enumerate(chips):
            for a in range(n):
                copy(a, 4 + j, (*chip, 1 - c), me).wait_recv()
        for cp in first + passed:
            cp.wait_send()
        for cp in mine:
            cp.wait()

    outs = _pcall(_after(body, n, deps), in_specs=[ANY] * (n + len(deps)), out_specs=[ANY] * n,
                  out_shape=[_sds((NDEV,) + s.shape, s.dtype) for s in shards],
                  scratch_shapes=[pltpu.SemaphoreType.DMA((7 * n,)), pltpu.SemaphoreType.DMA((7 * n,)),
                                  pltpu.SemaphoreType.DMA((n,))], name=name)(*shards, *deps)
    return list(outs)


HBM = pl.BlockSpec(memory_space=pltpu.HBM)
SEM = pl.BlockSpec(memory_space=pltpu.SEMAPHORE)


def _copies(plan, refs, send_sems, recv_sems):
    return [pltpu.make_async_remote_copy(src_ref=s, dst_ref=d, send_sem=send_sems.at[k], recv_sem=recv_sems.at[k],
                                         device_id=dev, device_id_type=MESH)
            for k, (s, d, dev) in enumerate(plan(refs, *_place()))]


def _xfer_start(bufs, ncopies, plan, name, deps=()):
    n = len(bufs)

    def body(*refs):
        for cp in _copies(plan, refs[:n], refs[n], refs[n + 1]):
            cp.start()
        token = refs[2 * n + 2]
        token[...] = jnp.zeros_like(token)

    outs = _pcall(
        _after(body, n, deps), name=name,
        out_shape=(pltpu.SemaphoreType.DMA((ncopies,)), pltpu.SemaphoreType.DMA((ncopies,)),
                   *[pltpu.HBM(b.shape, b.dtype) for b in bufs], _sds((8, LANE), F32)),
        in_specs=[HBM] * n + [ANY] * len(deps),
        out_specs=(SEM, SEM, *[HBM] * n, pl.BlockSpec(memory_space=pltpu.VMEM)),
        input_output_aliases={i: 2 + i for i in range(n)},
        compiler_params=pltpu.CompilerParams(has_side_effects=pltpu.SideEffectType.DATAFLOW_SIDE_EFFECTING),
    )(*[pltpu.with_memory_space_constraint(b, pltpu.HBM) for b in bufs], *deps)
    return (outs[0], outs[1]), list(outs[2:2 + n]), outs[2 + n]


def _xfer_wait(sems, bufs, plan, after, name):
    n = len(bufs)
    after = list(after) if isinstance(after, (list, tuple)) else [after]

    def body(*refs):
        for cp in _copies(plan, refs[:n], refs[n], refs[n + 1]):
            cp.wait_send()
            cp.wait_recv()

    outs = _pcall(
        body, name=name, out_shape=tuple(pltpu.HBM(b.shape, b.dtype) for b in bufs),
        in_specs=[HBM] * n + [SEM, SEM] + [ANY] * len(after), out_specs=tuple([HBM] * n),
        input_output_aliases={i: i for i in range(n)},
        compiler_params=pltpu.CompilerParams(has_side_effects=pltpu.SideEffectType.DATAFLOW_SIDE_EFFECTING),
    )(*bufs, *sems, *after)
    return list(outs)


def _chips_of(x, y):
    return [(1 - x, y), (x, 1 - y), (1 - x, 1 - y)]


def _gather_plan1(n):
    def plan(refs, x, y, c):
        out = []
        for a in range(n):
            blk = refs[a].at[4 * x + 2 * y + c]
            out.append((blk, blk, (x, y, 1 - c)))
            out += [(blk, blk, (px, py, c)) for px, py in _chips_of(x, y)]
        return out
    return plan


def _gather_plan2(n):
    def plan(refs, x, y, c):
        out = []
        for a in range(n):
            for px, py in _chips_of(x, y):
                blk = refs[a].at[4 * px + 2 * py + c]
                out.append((blk, blk, (x, y, 1 - c)))
        return out
    return plan


def _gather_start(shards, dev, name, deps=()):
    lands = [lax.dynamic_update_slice(lax.empty((NDEV,) + s.shape, s.dtype), s[None], (dev,) + (0,) * s.ndim)
             for s in shards]
    n = len(shards)
    sems, lands, tok = _xfer_start(lands, 4 * n, _gather_plan1(n), name + "_p1_start", deps)
    return dict(sems=sems, lands=lands, tok=tok, n=n)


def _gather_mid(st, after, name):
    n = st["n"]
    lands = _xfer_wait(st["sems"], st["lands"], _gather_plan1(n), after, name + "_p1_wait")
    sems, lands, tok = _xfer_start(lands, 3 * n, _gather_plan2(n), name + "_p2_start")
    return dict(sems=sems, lands=lands, tok=tok, n=n)


def _gather_finish(st, after, name):
    return _xfer_wait(st["sems"], st["lands"], _gather_plan2(st["n"]), after, name + "_p2_wait")


def _scatter_plan1(n):
    def plan(refs, x, y, c):
        return [(refs[a].at[2 * p + 1 - c], refs[n + a].at[p], (x, y, 1 - c)) for a in range(n) for p in range(NCHIP)]
    return plan


def _scatter_plan2(n):
    def plan(refs, x, y, c):
        return [(refs[a].at[2 * px + py], refs[n + a].at[j], (px, py, c))
                for a in range(n) for j, (px, py) in enumerate(_chips_of(x, y))]
    return plan


def _scatter_start(Gs, name):
    n = len(Gs)
    R1s = [lax.empty((NCHIP,) + g.shape[1:], g.dtype) for g in Gs]
    sems, bufs, tok = _xfer_start(list(Gs) + R1s, NCHIP * n, _scatter_plan1(n), name + "_s1_start")
    return dict(sems=sems, bufs=bufs, tok=tok, n=n)


def _scatter_mid(st, after, my_c, name):
    n = st["n"]
    bufs = _xfer_wait(st["sems"], st["bufs"], _scatter_plan1(n), after, name + "_s1_wait")
    Ps = [_pair_sum(bufs[a], bufs[n + a], my_c, f"{name}_pair_sum{a}") for a in range(n)]
    R2s = [lax.empty((3,) + p.shape[1:], p.dtype) for p in Ps]
    sems, bufs, tok = _xfer_start(Ps + R2s, 3 * n, _scatter_plan2(n), name + "_s2_start")
    return dict(sems=sems, bufs=bufs, tok=tok, n=n)


def _scatter_finish(st, after, name):
    n = st["n"]
    bufs = _xfer_wait(st["sems"], st["bufs"], _scatter_plan2(n), after, name + "_s2_wait")
    return bufs[:n], bufs[n:]


SMALL_ROWS = {"norm1_g": (0, 1), "norm2_g": (1, 1), "sgu_ln_g": (2, 1), "sgu_ln_b": (3, 1), "cfm_conv_b": (4, 1),
              "cfm_ln_g": (5, 1), "cfm_ln_b": (6, 1), "b_sgu": (7, 1), "w_sgu": (8, 128), "b_ada": (136, N_MOD),
              "w_short": (142, SHORT_K), "cfm_conv_w": (145, CFM_K)}
ROWS_PER_LAYER = 176
FINAL_ROW = DEPTH * ROWS_PER_LAYER
PACK_ROWS = 360


def _pack(get, D, layers=tuple(range(DEPTH)), tail=True):
    parts = []
    for l in layers:
        for name, (_, nrows) in SMALL_ROWS.items():
            a = get(name, l)
            parts.append(jnp.zeros((nrows * D,), F32) if a is None else a.astype(F32).reshape(nrows * D))
    if tail:
        for name in ("final_g", "loss"):
            a = get(name, None)
            parts.append(jnp.zeros((D,), F32) if a is None else a.astype(F32).reshape(D))
        parts.append(jnp.zeros(((PACK_ROWS - FINAL_ROW - 2) * D,), F32))
    return jnp.concatenate(parts).reshape(-1, D)


def _unpack(pack, name, shape):
    D = pack.shape[1]
    r0, nrows = SMALL_ROWS[name]
    return jnp.stack([pack[l * ROWS_PER_LAYER + r0:l * ROWS_PER_LAYER + r0 + nrows] for l in range(DEPTH)]).reshape(shape)


def _mm_tiles(S):
    return min(512, S), min(1024, S)


def kernel(x, c, w_ada, b_ada, norm1_g, w_in, w_short, w_a_out, sgu_ln_g, sgu_ln_b, w_sgu, b_sgu, w_b_out, cfm_conv_w, cfm_conv_b, cfm_ln_g, cfm_ln_b, w_c_out, w_o, norm2_g, w_ffn_in, w_ffn_out, final_g, loss_target, m_w_ada, m_b_ada, m_norm1_g, m_w_in, m_w_short, m_w_a_out, m_sgu_ln_g, m_sgu_ln_b, m_w_sgu, m_b_sgu, m_w_b_out, m_cfm_conv_w, m_cfm_conv_b, m_cfm_ln_g, m_cfm_ln_b, m_w_c_out, m_w_o, m_norm2_g, m_w_ffn_in, m_w_ffn_out, m_final_g, v_w_ada, v_b_ada, v_norm1_g, v_w_in, v_w_short, v_w_a_out, v_sgu_ln_g, v_sgu_ln_b, v_w_sgu, v_b_sgu, v_w_b_out, v_cfm_conv_w, v_cfm_conv_b, v_cfm_ln_g, v_cfm_ln_b, v_w_c_out, v_w_o, v_norm2_g, v_w_ffn_in, v_w_ffn_out, v_final_g):
    W = dict(w_ada=w_ada, b_ada=b_ada, norm1_g=norm1_g, w_in=w_in, w_short=w_short, w_a_out=w_a_out, sgu_ln_g=sgu_ln_g,
             sgu_ln_b=sgu_ln_b, w_sgu=w_sgu, b_sgu=b_sgu, w_b_out=w_b_out, cfm_conv_w=cfm_conv_w, cfm_conv_b=cfm_conv_b,
             cfm_ln_g=cfm_ln_g, cfm_ln_b=cfm_ln_b, w_c_out=w_c_out, w_o=w_o, norm2_g=norm2_g, w_ffn_in=w_ffn_in,
             w_ffn_out=w_ffn_out, final_g=final_g)
    Mo = dict(w_ada=m_w_ada, b_ada=m_b_ada, norm1_g=m_norm1_g, w_in=m_w_in, w_short=m_w_short, w_a_out=m_w_a_out,
              sgu_ln_g=m_sgu_ln_g, sgu_ln_b=m_sgu_ln_b, w_sgu=m_w_sgu, b_sgu=m_b_sgu, w_b_out=m_w_b_out,
              cfm_conv_w=m_cfm_conv_w, cfm_conv_b=m_cfm_conv_b, cfm_ln_g=m_cfm_ln_g, cfm_ln_b=m_cfm_ln_b,
              w_c_out=m_w_c_out, w_o=m_w_o, norm2_g=m_norm2_g, w_ffn_in=m_w_ffn_in, w_ffn_out=m_w_ffn_out,
              final_g=m_final_g)
    Vo = dict(w_ada=v_w_ada, b_ada=v_b_ada, norm1_g=v_norm1_g, w_in=v_w_in, w_short=v_w_short, w_a_out=v_w_a_out,
              sgu_ln_g=v_sgu_ln_g, sgu_ln_b=v_sgu_ln_b, w_sgu=v_w_sgu, b_sgu=v_b_sgu, w_b_out=v_w_b_out,
              cfm_conv_w=v_cfm_conv_w, cfm_conv_b=v_cfm_conv_b, cfm_ln_g=v_cfm_ln_g, cfm_ln_b=v_cfm_ln_b,
              w_c_out=v_w_c_out, w_o=v_w_o, norm2_g=v_norm2_g, w_ffn_in=v_w_ffn_in, w_ffn_out=v_w_ffn_out,
              final_g=v_final_g)
    order = ["w_ada", "b_ada", "norm1_g", "w_in", "w_short", "w_a_out", "sgu_ln_g", "sgu_ln_b", "w_sgu", "b_sgu",
             "w_b_out", "cfm_conv_w", "cfm_conv_b", "cfm_ln_g", "cfm_ln_b", "w_c_out", "w_o", "norm2_g", "w_ffn_in",
             "w_ffn_out", "final_g"]

    assert DEPTH == 2, "the weight-gather schedule below is written for two layers"
    S, D = x.shape[1], x.shape[2]
    F2 = w_ffn_in.shape[2] * NDEV
    FF = F2 // 2
    xi, yi, ci = _place()
    dev = 4 * xi + 2 * yi + ci
    my_c = jnp.reshape(ci, (1,)).astype(jnp.int32)
    my_chip = jnp.reshape(2 * xi + yi, (1,)).astype(jnp.int32)
    tm, tm_big = _mm_tiles(S)
    x0 = x.reshape(S, D)
    tgt = loss_target.reshape(S, D)

    def shards_of(l):
        return [w_in[l].astype(BF16), w_a_out[l].astype(BF16), w_b_out[l].astype(BF16), w_c_out[l].astype(BF16),
                w_o[l].astype(BF16), w_ffn_in[l].astype(BF16), w_ffn_out[l].astype(BF16)]

    c_all = _all_gather([jnp.pad(c, ((0, 7), (0, 0)))], "ag_c")[0][:, 0, :]
    modpart, c_act = _ada_fwd(c_all, w_ada, "ada_fwd")
    ncol = modpart.shape[2]
    mg = _all_gather([modpart.reshape(DEPTH * NDEV, ncol)], "ag_mod")[0].reshape(NDEV, DEPTH, NDEV, ncol)
    mine = lax.dynamic_index_in_dim(mg, dev, axis=2, keepdims=False)
    mod = (jnp.transpose(mine, (1, 0, 2)).reshape(DEPTH, N_MOD * D) + b_ada).reshape(DEPTH, N_MOD, D)

    ncs = w_short.shape[2]
    ag_in0 = _gather_start([w_in[0].astype(BF16), w_short.reshape(DEPTH * SHORT_K, ncs),
                            cfm_conv_w.reshape(DEPTH * CFM_K, ncs)], dev, "ag_w_in0", deps=(mod,))
    W, Mo, Vo = lax.optimization_barrier((ag_in0["tok"], (W, Mo, Vo)))[1]
    (norm1_g, norm2_g, w_in, w_a_out, w_b_out, w_c_out, w_o, w_ffn_in, w_ffn_out, sgu_ln_g, sgu_ln_b, w_sgu, b_sgu,
     cfm_conv_b, cfm_ln_g, cfm_ln_b, final_g) = [W[k] for k in (
         "norm1_g", "norm2_g", "w_in", "w_a_out", "w_b_out", "w_c_out", "w_o", "w_ffn_in", "w_ffn_out", "sgu_ln_g",
         "sgu_ln_b", "w_sgu", "b_sgu", "cfm_conv_b", "cfm_ln_g", "cfm_ln_b", "final_g")]
    m_w_ada, v_w_ada = Mo["w_ada"], Vo["w_ada"]
    xl0, h0, ht0 = _norm_fwd(x0, None, _rows(jnp.zeros((D,), F32), norm1_g[0], mod[0, 1], mod[0, 0]), "norm1_fwd0",
                             deps=(ag_in0["tok"],))
    ag_rest0 = _gather_start(shards_of(0)[1:], dev, "ag_rest0", deps=(h0,))

    tril = jnp.tril(jnp.ones((CHUNK, CHUNK), dtype=bool))

    def layer_consts(l):
        wt = jnp.where(tril[None], w_sgu[l], 0.0).astype(BF16)
        return dict(sgu_ln=_rows(sgu_ln_g[l], sgu_ln_b[l]), wtril=wt, wtril_t=jnp.swapaxes(wt, 1, 2),
                    bias_full=jnp.repeat(b_sgu[l].T, LANE, axis=1), cvec=_rows(cfm_conv_b[l], cfm_ln_g[l], cfm_ln_b[l]))

    def rest_of(g):
        return dict(w_a=g[0].reshape(1, D, D), w_b=g[1].reshape(1, D, D), w_c=g[2].reshape(1, D, D),
                    w_o=g[3].reshape(1, D, D), w_fi=jnp.transpose(g[4], (1, 0, 2)).reshape(1, D, F2),
                    w_fo=g[5].reshape(1, FF, D))

    consts = [layer_consts(l) for l in range(DEPTH)]
    sharded_small = ("w_short", "cfm_conv_w")

    def param_get(T):
        def get(name, l):
            if name == "final_g":
                return T[name]
            return None if name in sharded_small or name == "loss" else T[name][l]
        return get

    packs = [_pack(param_get(T), D) for T in (W, Mo, Vo)]
    ncr = DEPTH * (SHORT_K + CFM_K)
    padr = (-ncr) % 8
    convw_wmv = [jnp.pad(jnp.concatenate([T["w_short"].reshape(-1, ncs), T["cfm_conv_w"].reshape(-1, ncs)]),
                         ((0, padr), (0, 0))) for T in (W, Mo, Vo)]
    early_work = [ag_rest0["tok"], *packs, *convw_wmv] + [a for cl in consts for a in cl.values()]
    ag_in0 = _gather_mid(ag_in0, early_work, "ag_w_in0")
    g_in0 = _gather_finish(ag_in0, ag_in0["tok"], "ag_w_in0")
    w_short_full = jnp.transpose(g_in0[1], (1, 0, 2)).reshape(DEPTH, SHORT_K, D)
    cfm_w_full = jnp.transpose(g_in0[2], (1, 0, 2)).reshape(DEPTH, CFM_K, D)
    for l in range(DEPTH):
        consts[l]["wsh"] = jnp.pad(w_short_full[l], ((0, 8 - SHORT_K), (0, 0)))
        consts[l]["cw"] = jnp.pad(cfm_w_full[l], ((0, HALO - CFM_K), (0, 0)))
    Wg = [dict(w_in=g_in0[0]), None]
    ag_l1 = None
    nin = w_in.shape[2]
    tn_in = nin if nin % 256 == 0 and nin <= 1280 else 256
    tn_fi = 512 if F2 % 512 == 0 else 256
    tn_ffn = 1408 if F2 % 1408 == 0 else tn_fi
    tn_dw = min(256, D)

    saved = []
    xcur, fprev, gprev = x0, None, None
    for l in range(DEPTH):
        sh1, sc1, g1, sh2, sc2, g2 = [mod[l, k] for k in range(N_MOD)]
        cl = consts[l]
        if l == 0:
            xl, h, ht = xl0, h0, ht0
        else:
            vec1 = _rows(gprev, norm1_g[l], sc1, sh1)
            ag_l1 = _gather_mid(ag_l1, fprev, f"ag_w{l}")
            xl, h, ht = _norm_fwd(xcur, fprev, vec1, f"norm1_fwd{l}", deps=(ag_l1["tok"],))
            g = _gather_finish(ag_l1, h, f"ag_w{l}")
            Wg[l] = dict(w_in=g[0], **rest_of(g[1:]))
        wl = Wg[l]
        z = _mm_nn(h, wl["w_in"], BF16, tm_big, tn_in, D, f"mm_in{l}", w_outer=True)
        mix_deps = ()
        if l == 0:
            ag_rest0 = _gather_mid(ag_rest0, z, "ag_rest0")
            mix_deps = (ag_rest0["tok"],)
            if DEPTH > 1:
                ag_l1 = _gather_start(shards_of(1), dev, "ag_w1")
                mix_deps += (ag_l1["tok"],)
        acts, acts_t, conv = _mixer_fwd(z, cl["wsh"], cl["sgu_ln"], cl["wtril"], cl["bias_full"], cl["cw"], cl["cvec"],
                                        f"mixer_fwd{l}", deps=mix_deps)
        if l == 0:
            wl.update(rest_of(_gather_finish(ag_rest0, acts[0], "ag_rest0")))
        merged, merged_t, ys = _branch_out(acts, [wl["w_a"][0], wl["w_b"][0], wl["w_c"][0]], z, f"branch_out{l}")
        o = _mm_nn(merged, wl["w_o"], F32, tm_big, D, D, f"mm_o{l}")
        x1, h2, h2t = _norm_fwd(xl, o, _rows(g1, norm2_g[l], sc2, sh2), f"norm2_fwd{l}")
        gu, act, act_t = _ffn_in_swiglu(h2, wl["w_fi"], tm, tn_ffn, f"mm_ffn_in{l}")
        f = _mm_nn(act, wl["w_fo"], F32, tm, D, FF, f"mm_ffn_out{l}")
        saved.append(dict(xl=xl, ht=ht, z=z, acts_t=acts_t, conv=conv, ys=ys, merged_t=merged_t, o=o, x1=x1, h2t=h2t, gu=gu,
                          act_t=act_t, f=f, consts=cl, mod=(sh1, sc1, g1, sh2, sc2, g2)))
        xcur, fprev, gprev = x1, f, g2

    last = saved[-1]
    dxup, dfb, fsums, loss_blk = _final_bwd(last["x1"], last["f"], tgt, _rows(last["mod"][5], final_g), "final_bwd")
    loss_row = jnp.pad(loss_blk[0, 0:1], (0, D - 1))
    dgate2_next = fsums[1]
    small = [dict() for _ in range(DEPTH)]
    dmods = [None] * DEPTH
    nfi = w_ffn_in.shape[2]
    early_names, late_names = ["w_ffn_out", "w_ffn_in", "w_o"], ["w_a_out", "w_b_out", "w_c_out", "w_in"]
    results = {n: None for n in early_names + late_names}

    def adam_group(names, Ps, R2s, l, deps=()):
        for n, p, r2 in zip(names, Ps, R2s):
            results[n] = _adam_big(p, r2, my_chip, W[n], Mo[n], Vo[n], l, results[n], f"adam_{n}{l}", deps)

    deferred = []
    late_prev = None
    ag_s1, gathered1 = None, None
    tk_w = min(2048, S)
    tn_dw_in = tn_in // 2 if tn_in == 1280 else tn_in
    for l in reversed(range(DEPTH)):
        sv, wl, cl = saved[l], Wg[l], saved[l]["consts"]
        sh1, sc1, g1, sh2, sc2, g2 = sv["mod"]
        dact = _mm_nt(dfb, wl["w_fo"], BF16, tm, FF, D, f"mm_dact{l}",
                      deps=() if late_prev is None else (late_prev["tok"], ag_s1["tok"]))
        g_fo = _mm_wgrad(sv["act_t"], dfb, 1, FF // 2, D, tk_w, f"mm_dw_ffn_out{l}")
        dgu = _swiglu_bwd(dact, sv["gu"], f"swiglu_bwd{l}")
        dh2 = _mm_nt(dgu, wl["w_fi"], F32, tm, D, F2, f"mm_dh2{l}")
        if late_prev is not None:
            deferred.append((late_names, *_scatter_finish(late_prev, dh2, f"rs_late{l + 1}"), l + 1))
            late_prev = None
        g_fi = _mm_wgrad(sv["h2t"], dgu, 1, D, tn_fi, S, f"mm_dw_ffn_in{l}")
        if ag_s1 is not None:
            ag_s1 = _gather_mid(ag_s1, g_fi, "ag_small1")
        dx1, dob, s2 = _norm_bwd(sv["x1"], dh2, dxup, _rows(norm2_g[l], sc2, g1), sv["o"], f"norm2_bwd{l}",
                                 deps=() if ag_s1 is None else (ag_s1["tok"],))
        dmerged = _mm_nt(dob, wl["w_o"], BF16, tm_big, D, D, f"mm_dmerged{l}")
        g_o = _mm_wgrad(sv["merged_t"], dob, 1, D, tn_dw, S, f"mm_dw_o{l}")
        early = _scatter_start([g_fo.reshape(NDEV, FF // NDEV, D),
                                jnp.transpose(g_fi.reshape(D, NDEV, nfi), (1, 0, 2)),
                                g_o.reshape(NDEV, D // NDEV, D)], f"rs_early{l}")
        dys, dz = _gate_bwd(dmerged, sv["z"], sv["ys"], f"gate_bwd{l}", deps=(early["tok"],))
        if ag_s1 is not None:
            gathered1 = _gather_finish(ag_s1, dys[0], "ag_small1")[0]
            ag_s1 = None
        early = _scatter_mid(early, dys[0], my_c, f"rs_early{l}")
        dacts, g_abc = [], []
        for n, key in enumerate(("w_a", "w_b", "w_c")):
            dacts.append(_mm_nt(dys[n], wl[key], BF16, tm_big, D, D, f"mm_dact_{key}{l}",
                                deps=(early["tok"],) if n == 0 else ()))
            g_abc.append(_mm_wgrad(sv["acts_t"][n], dys[n], 1, D, tn_dw, S, f"mm_d{key}{l}"))
        dz, mvec, dcw, dws, dbs = _mixer_bwd(sv["z"], dacts, sv["conv"], dz, cl["wsh"], cl["sgu_ln"], cl["wtril"],
                                             cl["wtril_t"], cl["bias_full"], cl["cw"], cl["cvec"], f"mixer_bwd{l}")
        dh = _mm_nt(dz, wl["w_in"], F32, tm_big, D, tn_in, f"mm_dh{l}")
        g_in = _mm_wgrad(sv["ht"], dz, NDEV, D, tn_dw_in, S, f"mm_dw_in{l}")
        late = _scatter_start([g.reshape(NDEV, D // NDEV, D) for g in g_abc] + [g_in], f"rs_late{l}")
        if l > 0:
            pv = saved[l - 1]
            dxup, dfb, s1 = _norm_bwd(sv["xl"], dh, dx1, _rows(norm1_g[l], sc1, pv["mod"][5]), pv["f"], f"norm1_bwd{l}",
                                      deps=(late["tok"],))
        else:
            dxup, dfb, s1 = _norm_bwd(sv["xl"], dh, dx1, _rows(norm1_g[l], sc1), None, f"norm1_bwd{l}", deps=(late["tok"],))
        deferred.append((early_names, *_scatter_finish(early, dxup, f"rs_early{l}"), l))
        dmods[l] = jnp.stack([s1[0], s1[1], s2[3], s2[0], s2[1], dgate2_next])
        dgate2_next = s1[3]
        small[l] = dict(norm1_g=s1[2], norm2_g=s2[2], sgu_ln_g=mvec[3], sgu_ln_b=mvec[4], cfm_conv_b=mvec[5],
                        cfm_ln_g=mvec[6], cfm_ln_b=mvec[7], b_sgu=dbs[:, :, 0],
                        w_sgu=jnp.where(tril[None], dws, 0.0), b_ada=dmods[l], w_short=mvec[0:SHORT_K],
                        cfm_conv_w=dcw[0:CFM_K])
        small_get = lambda name, k: {"final_g": fsums[0], "loss": loss_row}.get(name) if k is None else small[k][name]
        if l > 0:
            late_prev = _scatter_mid(late, dxup, my_c, f"rs_late{l}")
            ag_s1 = _gather_start([_pack(small_get, D, layers=(l,), tail=True)], dev, "ag_small1", deps=(late_prev["tok"],))
    grad_x = dxup.reshape(x.shape)

    gathered0 = _all_gather([_pack(small_get, D, layers=(0,), tail=False)], "ag_small0", deps=(dxup,))[0]
    late_prev = _scatter_mid(late, gathered0, my_c, "rs_late0")
    gathered = jnp.concatenate([gathered0, gathered1], axis=1)
    sg, sd, sm, sv_ = _adam_small(gathered, *packs, name="adam_small", deps=(late_prev["tok"],))
    loss = sg[FINAL_ROW + 1, 0]
    out = {}
    for name in order:
        if name in SMALL_ROWS and name not in sharded_small:
            out[name] = tuple(_unpack(p, name, W[name].shape) for p in (sg, sd, sm, sv_))
    out["final_g"] = tuple(p[FINAL_ROW] for p in (sg, sd, sm, sv_))

    def my_cols(name):
        full = _unpack(sg, name, (DEPTH, SMALL_ROWS[name][1], D))
        return lax.dynamic_slice_in_dim(full, dev * ncs, ncs, axis=2)

    gcs = jnp.concatenate([my_cols("w_short").reshape(-1, ncs), my_cols("cfm_conv_w").reshape(-1, ncs)])
    cd, cm, cv = _adam_plain(jnp.pad(gcs, ((0, padr), (0, 0))), *convw_wmv, "adam_convw")
    nsh = DEPTH * SHORT_K
    out["w_short"] = tuple(a[0:nsh].reshape(w_short.shape) for a in (gcs, cd, cm, cv))
    out["cfm_conv_w"] = tuple(a[nsh:ncr].reshape(cfm_conv_w.shape) for a in (gcs, cd, cm, cv))

    dm_all = jnp.stack([gathered[:, l * ROWS_PER_LAYER + 136:l * ROWS_PER_LAYER + 136 + N_MOD, :].reshape(NDEV, N_MOD * D)
                        for l in range(DEPTH)])
    dm_mine = lax.dynamic_slice_in_dim(dm_all, dev * ncol, ncol, axis=2)
    out["w_ada"] = tuple(_adam_ada(jnp.transpose(c_act), dm_mine, w_ada, m_w_ada, v_w_ada, "adam_ada"))

    for names, Ps, R2s, l in deferred:
        adam_group(names, Ps, R2s, l, deps=(late_prev["tok"],))
    adam_group(late_names, *_scatter_finish(late_prev, results["w_o"][0], "rs_late0"), 0)
    for n in early_names + late_names:
        out[n] = tuple(results[n])

    grads = [out[n][0] for n in order]
    deltas = [out[n][1] for n in order]
    new_m = [out[n][2] for n in order]
    new_v = [out[n][3] for n in order]
    return (loss, grad_x, *grads, *deltas, *new_m, *new_v)
```

```python
import functools
import math

import jax
import jax.numpy as jnp
from jax import lax
from jax.experimental import pallas as pl
from jax.experimental.pallas import tpu as pltpu

F32, BF16 = jnp.float32, jnp.bfloat16
NDEV = 8
NCHIP = NDEV // 2
DEPTH = 2
EPS = 1e-6
CHUNK = 128
NG = 8
SHORT_K = 3
CFM_K = 31
HALO = 32
N_MOD = 6
LANE = 128
VMEM_LIMIT = 56 * 1024 * 1024
ADAM_LR, ADAM_B1, ADAM_B2, ADAM_EPS, ADAM_WD, ADAM_STEP = 0.001, 0.9, 0.999, 1e-08, 0.01, 10
_G0 = math.sqrt(2.0 / math.pi)
_G1 = 0.044715
MESH = pl.DeviceIdType.MESH
ANY = pl.BlockSpec(memory_space=pl.ANY)


def _pcall(body, **kw):
    return pl.pallas_call(body, **kw)


def _params(sem=None):
    return pltpu.CompilerParams(dimension_semantics=sem, vmem_limit_bytes=VMEM_LIMIT)


def _sds(shape, dtype):
    return jax.ShapeDtypeStruct(tuple(shape), dtype)


def _mm_body(dims, nk, out_f32):
    def body(a_ref, b_ref, o_ref, *scr):
        k = pl.program_id(2)
        part = lax.dot_general(a_ref[...], b_ref[...], dims, preferred_element_type=F32)
        if nk == 1:
            o_ref[...] = part.reshape(o_ref.shape).astype(o_ref.dtype)
        elif out_f32:
            @pl.when(k == 0)
            def _():
                o_ref[...] = part.reshape(o_ref.shape)

            @pl.when(k > 0)
            def _():
                o_ref[...] += part.reshape(o_ref.shape)
        else:
            acc = scr[0]

            @pl.when(k == 0)
            def _():
                acc[...] = part

            @pl.when(k > 0)
            def _():
                acc[...] += part

            @pl.when(k == nk - 1)
            def _():
                o_ref[...] = acc[...].astype(o_ref.dtype)
    return body


def _after(body, n_in, deps):
    nd = len(deps)
    if nd == 0:
        return body

    def ordered(*refs):
        return body(*refs[:n_in], *refs[n_in + nd:])
    return ordered


def _mm_call(body, grid, in_specs, out_spec, out_shape, acc_shape, name, deps=()):
    scratch = [] if acc_shape is None else [pltpu.VMEM(acc_shape, F32)]
    return _pcall(_after(body, 2, deps), grid=grid, in_specs=in_specs + [ANY] * len(deps), out_specs=out_spec,
                  out_shape=out_shape, scratch_shapes=scratch, name=name,
                  compiler_params=_params(("parallel", "parallel", "arbitrary")))


def _mm_nn(a, b3, out_dtype, tm, tn, tk, name, w_outer=False, deps=()):
    M, K = a.shape
    G, _, Nb = b3.shape
    npb, nk = Nb // tn, K // tk
    out_f32 = out_dtype == F32
    body = _mm_body((((1,), (0,)), ((), ())), nk, out_f32)
    if w_outer:
        grid = (G * npb, M // tm, nk)
        ij = lambda p, q: (q, p)
    else:
        grid = (M // tm, G * npb, nk)
        ij = lambda p, q: (p, q)

    def a_map(p, q, k):
        i, j = ij(p, q)
        return (i, k)

    def b_map(p, q, k):
        i, j = ij(p, q)
        return (j // npb, k, j % npb)

    def o_map(p, q, k):
        return ij(p, q)

    def wrapped(a_ref, b_ref, o_ref, *scr):
        body(a_ref, b_ref, o_ref, *scr)

    return _mm_call(wrapped, grid, [pl.BlockSpec((tm, tk), a_map), pl.BlockSpec((None, tk, tn), b_map)],
                    pl.BlockSpec((tm, tn), o_map), _sds((M, G * Nb), out_dtype),
                    None if (nk == 1 or out_f32) else (tm, tn), name, deps)(a, b3, *deps)


def _mm_nt(a, b3, out_dtype, tm, tn, tk, name, deps=()):
    M, _ = a.shape
    G, Ko, Nb = b3.shape
    kpb = Nb // tk
    nk = G * kpb
    out_f32 = out_dtype == F32
    body = _mm_body((((1,), (1,)), ((), ())), nk, out_f32)

    def wrapped(a_ref, b_ref, o_ref, *scr):
        body(a_ref, b_ref, o_ref, *scr)

    return _mm_call(wrapped, (M // tm, Ko // tn, nk),
                    [pl.BlockSpec((tm, tk), lambda i, j, k: (i, k)),
                     pl.BlockSpec((None, tn, tk), lambda i, j, k: (k // kpb, j, k % kpb))],
                    pl.BlockSpec((tm, tn), lambda i, j, k: (i, j)), _sds((M, Ko), out_dtype),
                    None if (nk == 1 or out_f32) else (tm, tn), name, deps)(a, b3, *deps)


def _mm_wgrad(at, b, G, tm, tn, tk, name, deps=()):
    M, T = at.shape
    Nb = b.shape[1] // G
    npb, nk = Nb // tn, T // tk
    body = _mm_body((((1,), (0,)), ((), ())), nk, False)

    def wrapped(a_ref, b_ref, o_ref, *scr):
        body(a_ref, b_ref, o_ref, *scr)

    a = at
    in_specs = [pl.BlockSpec((tm, tk), lambda i, j, k: (i, k)), pl.BlockSpec((tk, tn), lambda i, j, k: (k, j))]
    out_spec = pl.BlockSpec((None, tm, tn), lambda i, j, k: (j // npb, i, j % npb))
    return _mm_call(wrapped, (M // tm, G * npb, nk), in_specs, out_spec, _sds((G, M, Nb), BF16),
                    None if nk == 1 else (tm, tn), name, deps)(a, b, *deps)


def _rsum(v):
    return jnp.sum(v, axis=0, keepdims=True)


def _rmean(v):
    return jnp.mean(v, axis=-1, keepdims=True)


def _gelu(x):
    t = jnp.tanh(_G0 * (x + _G1 * (x * x * x)))
    return x * (0.5 * (1.0 + t)), t


def _dgelu(x, t):
    return 0.5 * (1.0 + t) + 0.5 * x * (1.0 - t * t) * (_G0 * (1.0 + 3.0 * _G1 * (x * x)))


def _sigmoid(x):
    return 1.0 / (1.0 + jnp.exp(-x))


def _fill_shifted(ext, rot):
    v = ext[...]
    n = v.shape[0]
    for b in range(1, 8):
        rot[b - 1] = pltpu.roll(v, n - b, 0)


def _rows_at(ext, rot, s, tm, cs=slice(None)):
    a, b = divmod(s, 8)
    return ext[8 * a:8 * a + tm, cs] if b == 0 else rot[b - 1, 8 * a:8 * a + tm, cs]


def _causal_conv(w_ref, taps, bias, ext, rot, offset, tm, out):
    D = out.shape[1]
    for cb in range(D // LANE):
        cs = slice(cb * LANE, (cb + 1) * LANE)
        acc = None
        for k, o in zip(taps, offset):
            term = w_ref[k:k + 1, cs] * _rows_at(ext, rot, o, tm, cs)
            acc = term if acc is None else acc + term
        out[:, cs] = acc if bias is None else acc + bias[:, cs]


def _rows(*vs):
    a = jnp.stack([v.astype(F32) for v in vs])
    return jnp.pad(a, ((0, 8 - len(vs)), (0, 0)))


def _row_spec(tm, D):
    return pl.BlockSpec((tm, D), lambda i: (i, 0))


def _const_spec(shape):
    nd = len(shape)
    return pl.BlockSpec(shape, lambda i: (0,) * nd)


def _norm_fwd(xp, f, vec, name, deps=()):
    S, D = xp.shape
    tm = min(256, S)
    has_f = f is not None

    def body(*refs):
        if has_f:
            xp_ref, f_ref, vec_ref, xo_ref, h_ref, ht_ref = refs
            x = xp_ref[...] + vec_ref[0:1, :] * f_ref[...]
            xo_ref[...] = x
        else:
            xp_ref, vec_ref, h_ref, ht_ref = refs
            x = xp_ref[...]
        r = lax.rsqrt(_rmean(x * x) + EPS)
        h = (x * r) * vec_ref[1:2, :]
        h = h * (1.0 + vec_ref[2:3, :]) + vec_ref[3:4, :]
        h_ref[...] = h.astype(BF16)
        ht_ref[...] = h.T.astype(BF16)

    rs = _row_spec(tm, D)
    ins = [xp, f, vec] if has_f else [xp, vec]
    in_specs = ([rs, rs] if has_f else [rs]) + [_const_spec((8, D))]
    out_shape = ([_sds((S, D), F32)] if has_f else []) + [_sds((S, D), BF16), _sds((D, S), BF16)]
    out_specs = [rs] * (len(out_shape) - 1) + [pl.BlockSpec((D, tm), lambda i: (0, i))]
    outs = _pcall(_after(body, len(ins), deps), grid=(S // tm,), in_specs=in_specs + [ANY] * len(deps),
                  out_specs=out_specs, out_shape=out_shape, name=name,
                  compiler_params=_params(("parallel",)))(*ins, *deps)
    return (outs[0], outs[1], outs[2]) if has_f else (xp, outs[0], outs[1])


def _mixer_fwd(z, wsh, sgu_ln, wtril, bias_full, cw, cvec, name, deps=()):
    S = z.shape[0]
    D = wsh.shape[1]
    tm = CHUNK

    def body(z_ref, wsh_ref, sln_ref, wt_ref, bias_ref, cw_ref, cv_ref, oa_ref, ob_ref, oc_ref, ta_ref, tb_ref, tc_ref,
             conv_ref, pe, ge, gr, cbuf):
        i = pl.program_id(0)

        @pl.when(i == 0)
        def _():
            pe[0:HALO, :] = jnp.zeros((HALO, D), F32)
            ge[0:HALO, :] = jnp.zeros((HALO, D), F32)

        def col(n):
            return z_ref[:, n * D:(n + 1) * D].astype(F32)

        pe[HALO:HALO + tm, :] = col(1) * col(2)
        q = wsh_ref[0:1, :] * pe[HALO - 2:HALO - 2 + tm, :]
        q = q + wsh_ref[1:2, :] * pe[HALO - 1:HALO - 1 + tm, :]
        q = q + wsh_ref[2:3, :] * pe[HALO:HALO + tm, :]
        act_a = col(0) * q
        oa_ref[...] = act_a.astype(BF16)
        ta_ref[...] = act_a.T.astype(BF16)
        gu, _ = _gelu(col(3))
        gv, _ = _gelu(col(4))
        d = gv - _rmean(gv)
        nrm = d * lax.rsqrt(_rmean(d * d) + EPS)
        vnb = (nrm * sln_ref[0:1, :] + sln_ref[1:2, :]).astype(BF16)
        for g in range(NG):
            cs = slice(g * LANE, (g + 1) * LANE)
            mixed = jnp.dot(wt_ref[g], vnb[:, cs], preferred_element_type=F32) + bias_ref[:, cs]
            act_b = gu[:, cs] * mixed
            ob_ref[:, cs] = act_b.astype(BF16)
            tb_ref[cs, :] = act_b.T.astype(BF16)
        ge[HALO:HALO + tm, :] = col(5) * _sigmoid(col(6))
        _fill_shifted(ge, gr)
        o0 = HALO - (CFM_K - 1)
        _causal_conv(cw_ref, range(CFM_K), cv_ref[0:1, :], ge, gr, range(o0, o0 + CFM_K), tm, cbuf)
        conv = cbuf[...]
        conv_ref[...] = conv.astype(BF16)
        d = conv - _rmean(conv)
        ln = (d * lax.rsqrt(_rmean(d * d) + EPS)) * cv_ref[1:2, :] + cv_ref[2:3, :]
        act_c = ln * _sigmoid(ln)
        oc_ref[...] = act_c.astype(BF16)
        tc_ref[...] = act_c.T.astype(BF16)
        pe[0:HALO, :] = pe[tm:tm + HALO, :]
        ge[0:HALO, :] = ge[tm:tm + HALO, :]

    rs = _row_spec(tm, D)
    outs = _pcall(
        _after(body, 7, deps), grid=(S // tm,),
        in_specs=[pl.BlockSpec((tm, 7 * D), lambda i: (i, 0)), _const_spec((8, D)), _const_spec((8, D)),
                  _const_spec((NG, CHUNK, CHUNK)), _const_spec((CHUNK, D)), _const_spec((HALO, D)), _const_spec((8, D))]
        + [ANY] * len(deps),
        out_specs=[rs, rs, rs] + [pl.BlockSpec((D, tm), lambda i: (0, i))] * 3 + [rs],
        out_shape=[_sds((S, D), BF16)] * 3 + [_sds((D, S), BF16)] * 3 + [_sds((S, D), BF16)],
        scratch_shapes=[pltpu.VMEM((HALO + tm, D), F32), pltpu.VMEM((HALO + tm, D), F32),
                        pltpu.VMEM((7, HALO + tm, D), F32), pltpu.VMEM((tm, D), F32)],
        name=name, compiler_params=_params(("arbitrary",)))(z, wsh, sgu_ln, wtril, bias_full, cw, cvec, *deps)
    return outs[:3], outs[3:6], outs[6]


def _branch_out(acts, ws, z, name):
    S, D = acts[0].shape
    tm = min(256, S)

    def body(a0, a1, a2, w0, w1, w2, g0, g1, g2, m_ref, mt_ref, y_ref):
        m = None
        for n, (a, w, g) in enumerate(((a0, w0, g0), (a1, w1, g1), (a2, w2, g2))):
            y = jnp.dot(a[...], w[...], preferred_element_type=F32)
            y_ref[n] = y.astype(BF16)
            t = _sigmoid(g[...].astype(F32)) * y
            m = t if m is None else m + t
        m_ref[...] = m.astype(BF16)
        mt_ref[...] = m.T.astype(BF16)

    rs = _row_spec(tm, D)
    gate_specs = [pl.BlockSpec((tm, D), functools.partial(lambda i, n: (i, 7 + n), n=n)) for n in range(3)]
    return _pcall(body, grid=(S // tm,),
                  in_specs=[rs, rs, rs] + [_const_spec((D, D))] * 3 + gate_specs,
                  out_specs=[rs, pl.BlockSpec((D, tm), lambda i: (0, i)), pl.BlockSpec((3, tm, D), lambda i: (0, i, 0))],
                  out_shape=[_sds((S, D), BF16), _sds((D, S), BF16), _sds((3, S, D), BF16)], name=name,
                  compiler_params=_params(("parallel",)))(*acts, *ws, z, z, z)


def _ffn_in_swiglu(h2, w3, tm, tn, name):
    S, D = h2.shape
    F = w3.shape[2] // 2
    nj = F // tn

    def body(a_ref, wg_ref, wu_ref, gu_ref, act_ref, actt_ref):
        a = a_ref[...]
        g = jnp.dot(a, wg_ref[...], preferred_element_type=F32)
        u = jnp.dot(a, wu_ref[...], preferred_element_type=F32)
        gu_ref[0] = g.astype(BF16)
        gu_ref[1] = u.astype(BF16)
        act = (g * _sigmoid(g)) * u
        act_ref[...] = act.astype(BF16)
        actt_ref[...] = act.T.astype(BF16)

    return _pcall(body, grid=(S // tm, nj),
                  in_specs=[pl.BlockSpec((tm, D), lambda i, j: (i, 0)), pl.BlockSpec((None, D, tn), lambda i, j: (0, 0, j)),
                            pl.BlockSpec((None, D, tn), lambda i, j: (0, 0, j + nj))],
                  out_specs=[pl.BlockSpec((2, tm, tn), lambda i, j: (0, i, j)), pl.BlockSpec((tm, tn), lambda i, j: (i, j)),
                             pl.BlockSpec((tn, tm), lambda i, j: (j, i))],
                  out_shape=[_sds((2, S, F), BF16), _sds((S, F), BF16), _sds((F, S), BF16)], name=name,
                  compiler_params=_params(("parallel", "parallel")))(h2, w3, w3)


def _swiglu_bwd(dact, gu, name):
    _, S, F = gu.shape
    F2 = 2 * F
    tm = min(256, S)

    def body(d_ref, g_ref, u_ref, o_ref):
        g = g_ref[...].astype(F32)
        sg = _sigmoid(g)
        d = d_ref[...].astype(F32)
        o_ref[:, 0:F] = (d * u_ref[...].astype(F32) * (sg * (1.0 + g * (1.0 - sg)))).astype(BF16)
        o_ref[:, F:2 * F] = (d * (g * sg)).astype(BF16)

    return _pcall(body, grid=(S // tm,),
                  in_specs=[pl.BlockSpec((tm, F), lambda i: (i, 0)), pl.BlockSpec((None, tm, F), lambda i: (0, i, 0)),
                            pl.BlockSpec((None, tm, F), lambda i: (1, i, 0))],
                  out_specs=pl.BlockSpec((tm, F2), lambda i: (i, 0)), out_shape=_sds((S, F2), BF16), name=name,
                  compiler_params=_params(("parallel",)))(dact, gu, gu)


def _final_bwd(x1, f, tgt, vec, name):
    S, D = x1.shape
    tm = min(256, S)

    def body(x_ref, f_ref, t_ref, vec_ref, dx_ref, df_ref, sums_ref, loss_ref):
        @pl.when(pl.program_id(0) == 0)
        def _():
            sums_ref[...] = jnp.zeros_like(sums_ref)
            loss_ref[...] = jnp.zeros_like(loss_ref)

        gate, fg = vec_ref[0:1, :], vec_ref[1:2, :]
        fv = f_ref[...]
        x = x_ref[...] + gate * fv
        r = lax.rsqrt(_rmean(x * x) + EPS)
        xn = x * r
        diff = xn * fg - t_ref[...]
        per_tok = _rmean(diff * diff)
        loss_ref[...] += 0.5 * jnp.sum(per_tok, axis=0, keepdims=True)
        dy = diff * (1.0 / D)
        sums_ref[0:1, :] += _rsum(dy * xn)
        dxn = dy * fg
        dx = r * (dxn - xn * _rmean(dxn * xn))
        sums_ref[1:2, :] += _rsum(dx * fv)
        dx_ref[...] = dx
        df_ref[...] = (dx * gate).astype(BF16)

    rs = _row_spec(tm, D)
    return _pcall(body, grid=(S // tm,), in_specs=[rs, rs, rs, _const_spec((8, D))],
                  out_specs=[rs, rs, _const_spec((8, D)), _const_spec((8, LANE))],
                  out_shape=[_sds((S, D), F32), _sds((S, D), BF16), _sds((8, D), F32), _sds((8, LANE), F32)],
                  name=name, compiler_params=_params(("arbitrary",)))(x1, f, tgt, vec)


def _norm_bwd(xin, dh, dxup, vec, fprev, name, deps=()):
    S, D = xin.shape
    tm = min(256, S)
    has_prev = fprev is not None

    def body(*refs):
        if has_prev:
            x_ref, dh_ref, up_ref, vec_ref, fp_ref, dx_ref, dp_ref, sums_ref = refs
        else:
            x_ref, dh_ref, up_ref, vec_ref, dx_ref, sums_ref = refs

        @pl.when(pl.program_id(0) == 0)
        def _():
            sums_ref[...] = jnp.zeros_like(sums_ref)

        g, scale = vec_ref[0:1, :], vec_ref[1:2, :]
        x = x_ref[...]
        r = lax.rsqrt(_rmean(x * x) + EPS)
        xn = x * r
        dhv = dh_ref[...]
        sums_ref[0:1, :] += _rsum(dhv)
        sums_ref[1:2, :] += _rsum(dhv * (xn * g))
        dm = dhv * (1.0 + scale)
        sums_ref[2:3, :] += _rsum(dm * xn)
        dxn = dm * g
        dx = up_ref[...] + r * (dxn - xn * _rmean(dxn * xn))
        dx_ref[...] = dx
        if has_prev:
            sums_ref[3:4, :] += _rsum(dx * fp_ref[...])
            dp_ref[...] = (dx * vec_ref[2:3, :]).astype(BF16)

    rs = _row_spec(tm, D)
    ins = [xin, dh, dxup, vec] + ([fprev] if has_prev else [])
    in_specs = [rs, rs, rs, _const_spec((8, D))] + ([rs] if has_prev else [])
    out_shape = [_sds((S, D), F32)] + ([_sds((S, D), BF16)] if has_prev else []) + [_sds((8, D), F32)]
    out_specs = [rs] + ([rs] if has_prev else []) + [_const_spec((8, D))]
    outs = _pcall(_after(body, len(ins), deps), grid=(S // tm,), in_specs=in_specs + [ANY] * len(deps),
                  out_specs=out_specs, out_shape=out_shape, name=name,
                  compiler_params=_params(("arbitrary",)))(*ins, *deps)
    return (outs[0], outs[1], outs[2]) if has_prev else (outs[0], None, outs[1])


def _gate_bwd(dmerged, z, ys, name, deps=()):
    S, D = dmerged.shape
    tm = min(512, S)
    ncol = z.shape[1] // D

    def body(dm_ref, g_ref, y_ref, dya_ref, dyb_ref, dyc_ref, dz_ref):
        n = pl.program_id(1)
        sg = _sigmoid(g_ref[...].astype(F32))
        dm = dm_ref[...].astype(F32)
        dy = (dm * sg).astype(BF16)
        for k, ref in enumerate((dya_ref, dyb_ref, dyc_ref)):
            @pl.when(n == k)
            def _(ref=ref):
                ref[...] = dy
        dz_ref[...] = (dm * y_ref[...].astype(F32) * (sg * (1.0 - sg))).astype(BF16)

    row = pl.BlockSpec((tm, D), lambda i, n: (i, 0))
    outs = _pcall(_after(body, 3, deps), grid=(S // tm, 3),
                  in_specs=[row, pl.BlockSpec((tm, D), lambda i, n: (i, 7 + n)),
                            pl.BlockSpec((None, tm, D), lambda i, n: (n, i, 0))] + [ANY] * len(deps),
                  out_specs=[row, row, row, pl.BlockSpec((tm, D), lambda i, n: (i, 7 + n))],
                  out_shape=[_sds((S, D), BF16)] * 3 + [_sds((S, ncol * D), BF16)], name=name,
                  compiler_params=_params(("parallel", "arbitrary")))(dmerged, z, ys, *deps)
    return outs[:3], outs[3]


def _mixer_bwd(z, dacts, conv, dz, wsh, sgu_ln, wtril, wtril_t, bias_full, cw, cvec, name):
    S = z.shape[0]
    D = wsh.shape[1]
    tm = CHUNK
    nt = S // tm
    hb = tm // HALO

    def body(zc, zp, da_ref, db_ref, dc_ref, conv_ref, wsh_ref, sln_ref, wt_ref, wtt_ref, bias_ref, cw_ref, cv_ref, _dz_in,
             dz_ref, vec_ref, dcw_ref, dws_ref, dbs_ref, pe, ge, dqe, dce, gr, dcr, cbuf, dcw8):
        i = pl.program_id(0)
        rb = nt - 1 - i

        @pl.when(i == 0)
        def _():
            vec_ref[...] = jnp.zeros_like(vec_ref)
            dcw8[...] = jnp.zeros_like(dcw8)
            dws_ref[...] = jnp.zeros_like(dws_ref)
            dbs_ref[...] = jnp.zeros_like(dbs_ref)
            dqe[tm:tm + HALO, :] = jnp.zeros((HALO, D), F32)
            dce[tm:tm + HALO, :] = jnp.zeros((HALO, D), F32)

        keep = (rb > 0).astype(F32)

        def col(n):
            return zc[:, n * D:(n + 1) * D].astype(F32)

        def pcol(n):
            return zp[:, n * D:(n + 1) * D].astype(F32)

        c_a, x_a = col(1), col(2)
        pe[0:HALO, :] = keep * (pcol(1) * pcol(2))
        pe[HALO:HALO + tm, :] = c_a * x_a
        q = wsh_ref[0:1, :] * pe[HALO - 2:HALO - 2 + tm, :]
        q = q + wsh_ref[1:2, :] * pe[HALO - 1:HALO - 1 + tm, :]
        q = q + wsh_ref[2:3, :] * pe[HALO:HALO + tm, :]
        dact = da_ref[...].astype(F32)
        dz_ref[:, 0:D] = (dact * q).astype(BF16)
        dq = dact * col(0)
        dqe[0:tm, :] = dq
        dp = wsh_ref[2:3, :] * dq + wsh_ref[1:2, :] * dqe[1:1 + tm, :] + wsh_ref[0:1, :] * dqe[2:2 + tm, :]
        dz_ref[:, D:2 * D] = (dp * x_a).astype(BF16)
        dz_ref[:, 2 * D:3 * D] = (dp * c_a).astype(BF16)
        for k in range(SHORT_K):
            o = HALO - (SHORT_K - 1) + k
            vec_ref[k:k + 1, :] += _rsum(dq * pe[o:o + tm, :])
        u, v = col(3), col(4)
        gu, tu = _gelu(u)
        gv, tv = _gelu(v)
        d = gv - _rmean(gv)
        rstd = lax.rsqrt(_rmean(d * d) + EPS)
        nrm = d * rstd
        vnb = (nrm * sln_ref[0:1, :] + sln_ref[1:2, :]).astype(BF16)
        dact = db_ref[...].astype(F32)
        dvn_parts, dgu_parts = [], []
        for g in range(NG):
            cs = slice(g * LANE, (g + 1) * LANE)
            vg = vnb[:, cs]
            mixed = jnp.dot(wt_ref[g], vg, preferred_element_type=F32) + bias_ref[:, cs]
            dgu_parts.append(dact[:, cs] * mixed)
            dmixed = dact[:, cs] * gu[:, cs]
            dmb = dmixed.astype(BF16)
            dws_ref[g] += lax.dot_general(dmb, vg, (((1,), (1,)), ((), ())), preferred_element_type=F32)
            dbs_ref[g] += jnp.broadcast_to(jnp.sum(dmixed, axis=1, keepdims=True), (CHUNK, LANE))
            dvn_parts.append(jnp.dot(wtt_ref[g], dmb, preferred_element_type=F32))
        dgu = jnp.concatenate(dgu_parts, axis=1)
        dvn = jnp.concatenate(dvn_parts, axis=1)
        dz_ref[:, 3 * D:4 * D] = (dgu * _dgelu(u, tu)).astype(BF16)
        vec_ref[3:4, :] += _rsum(dvn * nrm)
        vec_ref[4:5, :] += _rsum(dvn)
        dn = dvn * sln_ref[0:1, :]
        dgv = rstd * (dn - _rmean(dn) - nrm * _rmean(dn * nrm))
        dz_ref[:, 4 * D:5 * D] = (dgv * _dgelu(v, tv)).astype(BF16)
        a_c = col(5)
        sg = _sigmoid(col(6))
        ge[0:HALO, :] = keep * (pcol(5) * _sigmoid(pcol(6)))
        ge[HALO:HALO + tm, :] = a_c * sg
        _fill_shifted(ge, gr)
        o0 = HALO - (CFM_K - 1)
        conv = conv_ref[...].astype(F32)
        d = conv - _rmean(conv)
        rstd = lax.rsqrt(_rmean(d * d) + EPS)
        nrm = d * rstd
        ln = nrm * cv_ref[1:2, :] + cv_ref[2:3, :]
        sl = _sigmoid(ln)
        dln = dc_ref[...].astype(F32) * (sl * (1.0 + ln * (1.0 - sl)))
        vec_ref[6:7, :] += _rsum(dln * nrm)
        vec_ref[7:8, :] += _rsum(dln)
        dn = dln * cv_ref[1:2, :]
        dconv = rstd * (dn - _rmean(dn) - nrm * _rmean(dn * nrm))
        vec_ref[5:6, :] += _rsum(dconv)
        dce[0:tm, :] = dconv
        _fill_shifted(dce, dcr)
        _causal_conv(cw_ref, range(CFM_K), None, dce, dcr, [CFM_K - 1 - k for k in range(CFM_K)], tm, cbuf)
        dglu = cbuf[...]
        for cb in range(D // LANE):
            cs = slice(cb * LANE, (cb + 1) * LANE)
            dcv = dce[0:tm, cs]
            for k in range(CFM_K):
                prod = dcv * _rows_at(ge, gr, o0 + k, tm, cs)
                dcw8[k, :, cs] += jnp.sum(prod.reshape(tm // 8, 8, LANE), axis=0)

        @pl.when(i == nt - 1)
        def _():
            dcw_ref[...] = jnp.sum(dcw8[...], axis=1)
        dz_ref[:, 5 * D:6 * D] = (dglu * sg).astype(BF16)
        dz_ref[:, 6 * D:7 * D] = (dglu * a_c * (sg * (1.0 - sg))).astype(BF16)
        dqe[tm:tm + HALO, :] = dqe[0:HALO, :]
        dce[tm:tm + HALO, :] = dce[0:HALO, :]

    rev = lambda i: (nt - 1 - i, 0)
    rs = pl.BlockSpec((tm, D), rev)
    cur = pl.BlockSpec((tm, 7 * D), rev)
    prev = pl.BlockSpec((HALO, 7 * D), lambda i: (jnp.maximum((nt - 1 - i) * hb - 1, 0), 0))
    ext = pltpu.VMEM((HALO + tm, D), F32)
    outs = _pcall(
        body, grid=(nt,),
        in_specs=[cur, prev, rs, rs, rs, rs, _const_spec((8, D)), _const_spec((8, D)), _const_spec((NG, CHUNK, CHUNK)),
                  _const_spec((NG, CHUNK, CHUNK)), _const_spec((CHUNK, D)), _const_spec((HALO, D)), _const_spec((8, D)),
                  ANY],
        out_specs=[cur, _const_spec((8, D)), _const_spec((HALO, D)), _const_spec((NG, CHUNK, CHUNK)),
                   _const_spec((NG, CHUNK, LANE))],
        out_shape=[_sds(dz.shape, BF16), _sds((8, D), F32), _sds((HALO, D), F32), _sds((NG, CHUNK, CHUNK), F32),
                   _sds((NG, CHUNK, LANE), F32)],
        scratch_shapes=[ext, ext, ext, ext, pltpu.VMEM((7, HALO + tm, D), F32), pltpu.VMEM((7, HALO + tm, D), F32),
                        pltpu.VMEM((tm, D), F32), pltpu.VMEM((HALO, 8, D), F32)],
        input_output_aliases={13: 0}, name=name,
        compiler_params=_params(("arbitrary",)))(z, z, *dacts, conv, wsh, sgu_ln, wtril, wtril_t, bias_full, cw, cvec, dz)
    return outs


def _ada_fwd(c_all, w_ada_loc, name):
    nb, D = c_all.shape
    L, _, nc = w_ada_loc.shape

    def body(c_ref, w_ref, o_ref, ca_ref):
        cv = c_ref[...]
        ca = cv * _sigmoid(cv)
        ca_ref[...] = ca
        o_ref[...] = jnp.dot(ca.astype(BF16), w_ref[...].astype(BF16), preferred_element_type=F32)

    return _pcall(body, grid=(L,),
                  in_specs=[_const_spec((nb, D)), pl.BlockSpec((None, D, nc), lambda l: (l, 0, 0))],
                  out_specs=[pl.BlockSpec((None, nb, nc), lambda l: (l, 0, 0)), _const_spec((nb, D))],
                  out_shape=[_sds((L, nb, nc), F32), _sds((nb, D), F32)], name=name,
                  compiler_params=_params(("arbitrary",)))(c_all, w_ada_loc)


def _adamw(w, g, m, v):
    m = ADAM_B1 * m + (1.0 - ADAM_B1) * g
    v = ADAM_B2 * v + (1.0 - ADAM_B2) * (g * g)
    m_hat = m / (1.0 - ADAM_B1 ** ADAM_STEP)
    v_hat = v / (1.0 - ADAM_B2 ** ADAM_STEP)
    delta = -ADAM_LR * (m_hat / (jnp.sqrt(v_hat) + ADAM_EPS) + ADAM_WD * w)
    return delta, m, v


def _tile_rows(R, C, align=8):
    cap = max(align, (1536 * 1024) // (4 * C))
    best = None
    for t in range(align, R + 1, align):
        if R % t == 0 and t <= cap:
            best = t
    return R if best is None else best


def _adam_ada(ct, dm, w, m, v, name):
    L, D, nc = w.shape
    nb = ct.shape[1]
    tr = _tile_rows(D, nc)

    def body(ct_ref, dm_ref, w_ref, m_ref, v_ref, g_ref, d_ref, mo_ref, vo_ref):
        g = ct_ref[:, 0:1] * dm_ref[0:1, :]
        for b in range(1, nb):
            g = g + ct_ref[:, b:b + 1] * dm_ref[b:b + 1, :]
        g_ref[...] = g
        d_ref[...], mo_ref[...], vo_ref[...] = _adamw(w_ref[...], g, m_ref[...], v_ref[...])

    ws = pl.BlockSpec((None, tr, nc), lambda l, r: (l, r, 0))
    return _pcall(body, grid=(L, D // tr),
                  in_specs=[pl.BlockSpec((tr, nb), lambda l, r: (r, 0)), pl.BlockSpec((None, nb, nc), lambda l, r: (l, 0, 0)),
                            ws, ws, ws],
                  out_specs=[ws] * 4, out_shape=[_sds(w.shape, F32)] * 4, name=name,
                  compiler_params=_params(("parallel", "parallel")))(ct, dm, w, m, v)


def _adam_small(parts, w, m, v, name, deps=()):
    n, R, C = parts.shape
    tr = _tile_rows(R, C * n // 2)

    def body(p_ref, w_ref, m_ref, v_ref, g_ref, d_ref, mo_ref, vo_ref):
        g = p_ref[0]
        for j in range(1, n):
            g = g + p_ref[j]
        g_ref[...] = g
        d_ref[...], mo_ref[...], vo_ref[...] = _adamw(w_ref[...], g, m_ref[...], v_ref[...])

    ws = pl.BlockSpec((tr, C), lambda r: (r, 0))
    return _pcall(_after(body, 4, deps), grid=(R // tr,),
                  in_specs=[pl.BlockSpec((n, tr, C), lambda r: (0, r, 0)), ws, ws, ws] + [ANY] * len(deps),
                  out_specs=[ws] * 4, out_shape=[_sds((R, C), F32)] * 4, name=name,
                  compiler_params=_params(("parallel",)))(parts, w, m, v, *deps)


def _adam_plain(g, w, m, v, name):
    R, C = w.shape

    def body(g_ref, w_ref, m_ref, v_ref, d_ref, mo_ref, vo_ref):
        d_ref[...], mo_ref[...], vo_ref[...] = _adamw(w_ref[...], g_ref[...], m_ref[...], v_ref[...])

    ws = _const_spec((R, C))
    return _pcall(body, grid=(1,), in_specs=[ws] * 4, out_specs=[ws] * 3, out_shape=[_sds((R, C), F32)] * 3, name=name,
                  compiler_params=_params(("arbitrary",)))(g, w, m, v)


def _pair_sum(G, R1, my_c, name):
    n, R, C = G.shape
    half = n // 2
    tr = _tile_rows(R, C, align=16)

    def body(c_ref, g_ref, r_ref, o_ref):
        o_ref[...] = (g_ref[...].astype(F32) + r_ref[...].astype(F32)).astype(o_ref.dtype)

    blk = (None, tr, C)
    gs = pltpu.PrefetchScalarGridSpec(
        num_scalar_prefetch=1, grid=(half, R // tr),
        in_specs=[pl.BlockSpec(blk, lambda p, r, c: (2 * p + c[0], r, 0)), pl.BlockSpec(blk, lambda p, r, c: (p, r, 0))],
        out_specs=pl.BlockSpec(blk, lambda p, r, c: (p, r, 0)))
    return _pcall(body, grid_spec=gs, out_shape=_sds((half, R, C), G.dtype), name=name,
                  compiler_params=_params(("parallel", "parallel")))(my_c, G, R1)


def _adam_big(P, R2, my_chip, w, m, v, layer, prev, name, deps=()):
    _, R, C = P.shape
    nrecv = R2.shape[0]
    tr = _tile_rows(R, C, align=16)

    def body(p_sm, p_ref, r_ref, w_ref, m_ref, v_ref, *rest):
        g_ref, d_ref, mo_ref, vo_ref = rest[-4:]
        g = p_ref[...].astype(F32)
        for k in range(nrecv):
            g = g + r_ref[k].astype(F32)
        g_ref[...] = g
        d_ref[...], mo_ref[...], vo_ref[...] = _adamw(w_ref[...], g, m_ref[...], v_ref[...])

    ws = pl.BlockSpec((None, tr, C), lambda r, p: (layer, r, 0))
    held = [] if prev is None else list(prev)
    gs = pltpu.PrefetchScalarGridSpec(
        num_scalar_prefetch=1, grid=(R // tr,),
        in_specs=[pl.BlockSpec((None, tr, C), lambda r, p: (p[0], r, 0)),
                  pl.BlockSpec((nrecv, tr, C), lambda r, p: (0, r, 0)), ws, ws, ws] + [ANY] * (len(held) + len(deps)),
        out_specs=[ws] * 4)
    alias = {6 + i: i for i in range(len(held))}
    return _pcall(body, grid_spec=gs, out_shape=[_sds(w.shape, F32)] * 4, name=name, input_output_aliases=alias,
                  compiler_params=_params(("parallel",)))(my_chip, P, R2, w, m, v, *held, *deps)


def _place():
    return lax.axis_index("x"), lax.axis_index("y"), lax.axis_index("c")


def _all_gather(shards, name, deps=()):
    n = len(shards)

    def body(*refs):
        ins, outs = refs[:n], refs[n:2 * n]
        send_sems, recv_sems, local_sems = refs[2 * n:]
        x, y, c = _place()
        me, sibling = (x, y, c), (x, y, 1 - c)
        chips = [(1 - x, y), (x, 1 - y), (1 - x, 1 - y)]

        def slot(a, px, py, pc):
            return outs[a].at[4 * px + 2 * py + pc]

        def copy(a, k, block, to, src=None):
            return pltpu.make_async_remote_copy(
                src_ref=slot(a, *block) if src is None else src, dst_ref=slot(a, *block),
                send_sem=send_sems.at[7 * a + k], recv_sem=recv_sems.at[7 * a + k], device_id=to, device_id_type=MESH)

        mine = [pltpu.make_async_copy(ins[a], slot(a, *me), local_sems.at[a]) for a in range(n)]
        for cp in mine:
            cp.start()
        first = []
        for a in range(n):
            first.append(copy(a, 0, me, sibling, src=ins[a]))
            first += [copy(a, 1 + j, me, (*chip, c), src=ins[a]) for j, chip in enumerate(chips)]
        for cp in first:
            cp.start()
        passed = []
        for j, chip in enumerate(chips):
            for a in range(n):
                copy(a, 1 + j, (*chip, c), me).wait_recv()
                fwd = copy(a, 4 + j, (*chip, c), sibling)
                fwd.start()
                passed.append(fwd)
        for a in range(n):
            copy(a, 0, sibling, me).wait_recv()
        for j, chip in enumerate(chips):
            for a in range(n):
                copy(a, 4 + j, (*chip, 1 - c), me).wait_recv()
        for cp in first + passed:
            cp.wait_send()
        for cp in mine:
            cp.wait()

    outs = _pcall(_after(body, n, deps), in_specs=[ANY] * (n + len(deps)), out_specs=[ANY] * n,
                  out_shape=[_sds((NDEV,) + s.shape, s.dtype) for s in shards],
                  scratch_shapes=[pltpu.SemaphoreType.DMA((7 * n,)), pltpu.SemaphoreType.DMA((7 * n,)),
                                  pltpu.SemaphoreType.DMA((n,))], name=name)(*shards, *deps)
    return list(outs)


HBM = pl.BlockSpec(memory_space=pltpu.HBM)
SEM = pl.BlockSpec(memory_space=pltpu.SEMAPHORE)


def _copies(plan, refs, send_sems, recv_sems):
    return [pltpu.make_async_remote_copy(src_ref=s, dst_ref=d, send_sem=send_sems.at[k], recv_sem=recv_sems.at[k],
                                         device_id=dev, device_id_type=MESH)
            for k, (s, d, dev) in enumerate(plan(refs, *_place()))]


def _xfer_start(bufs, ncopies, plan, name, deps=()):
    n = len(bufs)

    def body(*refs):
        for cp in _copies(plan, refs[:n], refs[n], refs[n + 1]):
            cp.start()
        token = refs[2 * n + 2]
        token[...] = jnp.zeros_like(token)

    outs = _pcall(
        _after(body, n, deps), name=name,
        out_shape=(pltpu.SemaphoreType.DMA((ncopies,)), pltpu.SemaphoreType.DMA((ncopies,)),
                   *[pltpu.HBM(b.shape, b.dtype) for b in bufs], _sds((8, LANE), F32)),
        in_specs=[HBM] * n + [ANY] * len(deps),
        out_specs=(SEM, SEM, *[HBM] * n, pl.BlockSpec(memory_space=pltpu.VMEM)),
        input_output_aliases={i: 2 + i for i in range(n)},
        compiler_params=pltpu.CompilerParams(has_side_effects=pltpu.SideEffectType.DATAFLOW_SIDE_EFFECTING),
    )(*[pltpu.with_memory_space_constraint(b, pltpu.HBM) for b in bufs], *deps)
    return (outs[0], outs[1]), list(outs[2:2 + n]), outs[2 + n]


def _xfer_wait(sems, bufs, plan, after, name):
    n = len(bufs)
    after = list(after) if isinstance(after, (list, tuple)) else [after]

    def body(*refs):
        for cp in _copies(plan, refs[:n], refs[n], refs[n + 1]):
            cp.wait_send()
            cp.wait_recv()

    outs = _pcall(
        body, name=name, out_shape=tuple(pltpu.HBM(b.shape, b.dtype) for b in bufs),
        in_specs=[HBM] * n + [SEM, SEM] + [ANY] * len(after), out_specs=tuple([HBM] * n),
        input_output_aliases={i: i for i in range(n)},
        compiler_params=pltpu.CompilerParams(has_side_effects=pltpu.SideEffectType.DATAFLOW_SIDE_EFFECTING),
    )(*bufs, *sems, *after)
    return list(outs)


def _chips_of(x, y):
    return [(1 - x, y), (x, 1 - y), (1 - x, 1 - y)]


def _slot(ref, paired, chip, c):
    if not paired:
        return ref.at[2 * chip + c]
    w = ref.shape[2] // 2
    return ref.at[chip, :, pl.ds(pl.multiple_of(c * w, LANE), w)]


def _gather_plan1(n, paired):
    def plan(refs, x, y, c):
        out = []
        for a in range(n):
            blk = _slot(refs[a], a in paired, 2 * x + y, c)
            out.append((blk, blk, (x, y, 1 - c)))
            out += [(blk, blk, (px, py, c)) for px, py in _chips_of(x, y)]
        return out
    return plan


def _gather_plan2(n, paired):
    def plan(refs, x, y, c):
        out = []
        for a in range(n):
            for px, py in _chips_of(x, y):
                blk = _slot(refs[a], a in paired, 2 * px + py, c)
                out.append((blk, blk, (x, y, 1 - c)))
        return out
    return plan


def _gather_start(shards, dev, name, deps=(), paired=()):
    lands = []
    for a, s in enumerate(shards):
        if a in paired:
            R, C = s.shape
            lands.append(lax.dynamic_update_slice(lax.empty((NCHIP, R, 2 * C), s.dtype), s[None],
                                                  (dev // 2, 0, (dev % 2) * C)))
        else:
            lands.append(lax.dynamic_update_slice(lax.empty((NDEV,) + s.shape, s.dtype), s[None],
                                                  (dev,) + (0,) * s.ndim))
    n = len(shards)
    sems, lands, tok = _xfer_start(lands, 4 * n, _gather_plan1(n, paired), name + "_p1_start", deps)
    return dict(sems=sems, lands=lands, tok=tok, n=n, paired=paired)


def _gather_mid(st, after, name):
    n, paired = st["n"], st["paired"]
    lands = _xfer_wait(st["sems"], st["lands"], _gather_plan1(n, paired), after, name + "_p1_wait")
    sems, lands, tok = _xfer_start(lands, 3 * n, _gather_plan2(n, paired), name + "_p2_start")
    return dict(sems=sems, lands=lands, tok=tok, n=n, paired=paired)


def _gather_finish(st, after, name):
    return _xfer_wait(st["sems"], st["lands"], _gather_plan2(st["n"], st["paired"]), after, name + "_p2_wait")


def _scatter_plan1(n):
    def plan(refs, x, y, c):
        return [(refs[a].at[2 * p + 1 - c], refs[n + a].at[p], (x, y, 1 - c)) for a in range(n) for p in range(NCHIP)]
    return plan


def _scatter_plan2(n):
    def plan(refs, x, y, c):
        return [(refs[a].at[2 * px + py], refs[n + a].at[j], (px, py, c))
                for a in range(n) for j, (px, py) in enumerate(_chips_of(x, y))]
    return plan


def _scatter_start(Gs, name):
    n = len(Gs)
    R1s = [lax.empty((NCHIP,) + g.shape[1:], g.dtype) for g in Gs]
    sems, bufs, tok = _xfer_start(list(Gs) + R1s, NCHIP * n, _scatter_plan1(n), name + "_s1_start")
    return dict(sems=sems, bufs=bufs, tok=tok, n=n)


def _scatter_mid(st, after, my_c, name):
    n = st["n"]
    bufs = _xfer_wait(st["sems"], st["bufs"], _scatter_plan1(n), after, name + "_s1_wait")
    Ps = [_pair_sum(bufs[a], bufs[n + a], my_c, f"{name}_pair_sum{a}") for a in range(n)]
    R2s = [lax.empty((3,) + p.shape[1:], p.dtype) for p in Ps]
    sems, bufs, tok = _xfer_start(Ps + R2s, 3 * n, _scatter_plan2(n), name + "_s2_start")
    return dict(sems=sems, bufs=bufs, tok=tok, n=n)


def _scatter_finish(st, after, name):
    n = st["n"]
    bufs = _xfer_wait(st["sems"], st["bufs"], _scatter_plan2(n), after, name + "_s2_wait")
    return bufs[:n], bufs[n:]


SMALL_ROWS = {"norm1_g": (0, 1), "norm2_g": (1, 1), "sgu_ln_g": (2, 1), "sgu_ln_b": (3, 1), "cfm_conv_b": (4, 1),
              "cfm_ln_g": (5, 1), "cfm_ln_b": (6, 1), "b_sgu": (7, 1), "w_sgu": (8, 128), "b_ada": (136, N_MOD),
              "w_short": (142, SHORT_K), "cfm_conv_w": (145, CFM_K)}
ROWS_PER_LAYER = 176
FINAL_ROW = DEPTH * ROWS_PER_LAYER
PACK_ROWS = 360


def _pack(get, D, layers=tuple(range(DEPTH)), tail=True):
    parts = []
    for l in layers:
        for name, (_, nrows) in SMALL_ROWS.items():
            a = get(name, l)
            parts.append(jnp.zeros((nrows * D,), F32) if a is None else a.astype(F32).reshape(nrows * D))
    if tail:
        for name in ("final_g", "loss"):
            a = get(name, None)
            parts.append(jnp.zeros((D,), F32) if a is None else a.astype(F32).reshape(D))
        parts.append(jnp.zeros(((PACK_ROWS - FINAL_ROW - 2) * D,), F32))
    return jnp.concatenate(parts).reshape(-1, D)


def _unpack(pack, name, shape):
    D = pack.shape[1]
    r0, nrows = SMALL_ROWS[name]
    return jnp.stack([pack[l * ROWS_PER_LAYER + r0:l * ROWS_PER_LAYER + r0 + nrows] for l in range(DEPTH)]).reshape(shape)


def _mm_tiles(S):
    return min(512, S), min(1024, S)


def kernel(x, c, w_ada, b_ada, norm1_g, w_in, w_short, w_a_out, sgu_ln_g, sgu_ln_b, w_sgu, b_sgu, w_b_out, cfm_conv_w, cfm_conv_b, cfm_ln_g, cfm_ln_b, w_c_out, w_o, norm2_g, w_ffn_in, w_ffn_out, final_g, loss_target, m_w_ada, m_b_ada, m_norm1_g, m_w_in, m_w_short, m_w_a_out, m_sgu_ln_g, m_sgu_ln_b, m_w_sgu, m_b_sgu, m_w_b_out, m_cfm_conv_w, m_cfm_conv_b, m_cfm_ln_g, m_cfm_ln_b, m_w_c_out, m_w_o, m_norm2_g, m_w_ffn_in, m_w_ffn_out, m_final_g, v_w_ada, v_b_ada, v_norm1_g, v_w_in, v_w_short, v_w_a_out, v_sgu_ln_g, v_sgu_ln_b, v_w_sgu, v_b_sgu, v_w_b_out, v_cfm_conv_w, v_cfm_conv_b, v_cfm_ln_g, v_cfm_ln_b, v_w_c_out, v_w_o, v_norm2_g, v_w_ffn_in, v_w_ffn_out, v_final_g):
    W = dict(w_ada=w_ada, b_ada=b_ada, norm1_g=norm1_g, w_in=w_in, w_short=w_short, w_a_out=w_a_out, sgu_ln_g=sgu_ln_g,
             sgu_ln_b=sgu_ln_b, w_sgu=w_sgu, b_sgu=b_sgu, w_b_out=w_b_out, cfm_conv_w=cfm_conv_w, cfm_conv_b=cfm_conv_b,
             cfm_ln_g=cfm_ln_g, cfm_ln_b=cfm_ln_b, w_c_out=w_c_out, w_o=w_o, norm2_g=norm2_g, w_ffn_in=w_ffn_in,
             w_ffn_out=w_ffn_out, final_g=final_g)
    Mo = dict(w_ada=m_w_ada, b_ada=m_b_ada, norm1_g=m_norm1_g, w_in=m_w_in, w_short=m_w_short, w_a_out=m_w_a_out,
              sgu_ln_g=m_sgu_ln_g, sgu_ln_b=m_sgu_ln_b, w_sgu=m_w_sgu, b_sgu=m_b_sgu, w_b_out=m_w_b_out,
              cfm_conv_w=m_cfm_conv_w, cfm_conv_b=m_cfm_conv_b, cfm_ln_g=m_cfm_ln_g, cfm_ln_b=m_cfm_ln_b,
              w_c_out=m_w_c_out, w_o=m_w_o, norm2_g=m_norm2_g, w_ffn_in=m_w_ffn_in, w_ffn_out=m_w_ffn_out,
              final_g=m_final_g)
    Vo = dict(w_ada=v_w_ada, b_ada=v_b_ada, norm1_g=v_norm1_g, w_in=v_w_in, w_short=v_w_short, w_a_out=v_w_a_out,
              sgu_ln_g=v_sgu_ln_g, sgu_ln_b=v_sgu_ln_b, w_sgu=v_w_sgu, b_sgu=v_b_sgu, w_b_out=v_w_b_out,
              cfm_conv_w=v_cfm_conv_w, cfm_conv_b=v_cfm_conv_b, cfm_ln_g=v_cfm_ln_g, cfm_ln_b=v_cfm_ln_b,
              w_c_out=v_w_c_out, w_o=v_w_o, norm2_g=v_norm2_g, w_ffn_in=v_w_ffn_in, w_ffn_out=v_w_ffn_out,
              final_g=v_final_g)
    order = ["w_ada", "b_ada", "norm1_g", "w_in", "w_short", "w_a_out", "sgu_ln_g", "sgu_ln_b", "w_sgu", "b_sgu",
             "w_b_out", "cfm_conv_w", "cfm_conv_b", "cfm_ln_g", "cfm_ln_b", "w_c_out", "w_o", "norm2_g", "w_ffn_in",
             "w_ffn_out", "final_g"]

    assert DEPTH == 2, "the weight-gather schedule below is written for two layers"
    S, D = x.shape[1], x.shape[2]
    F2 = w_ffn_in.shape[2] * NDEV
    FF = F2 // 2
    xi, yi, ci = _place()
    dev = 4 * xi + 2 * yi + ci
    my_c = jnp.reshape(ci, (1,)).astype(jnp.int32)
    my_chip = jnp.reshape(2 * xi + yi, (1,)).astype(jnp.int32)
    tm, tm_big = _mm_tiles(S)
    tm_huge = min(2048, S)
    x0 = x.reshape(S, D)
    tgt = loss_target.reshape(S, D)

    def shards_of(l):
        return [w_in[l].astype(BF16), w_a_out[l].astype(BF16), w_b_out[l].astype(BF16), w_c_out[l].astype(BF16),
                w_o[l].astype(BF16), w_ffn_in[l].astype(BF16), w_ffn_out[l].astype(BF16)]

    c_all = _all_gather([jnp.pad(c, ((0, 7), (0, 0)))], "ag_c")[0][:, 0, :]
    modpart, c_act = _ada_fwd(c_all, w_ada, "ada_fwd")
    ncol = modpart.shape[2]
    mg = _all_gather([modpart.reshape(DEPTH * NDEV, ncol)], "ag_mod")[0].reshape(NDEV, DEPTH, NDEV, ncol)
    mine = lax.dynamic_index_in_dim(mg, dev, axis=2, keepdims=False)
    mod = (jnp.transpose(mine, (1, 0, 2)).reshape(DEPTH, N_MOD * D) + b_ada).reshape(DEPTH, N_MOD, D)

    ncs = w_short.shape[2]
    ag_in0 = _gather_start([w_in[0].astype(BF16), w_short.reshape(DEPTH * SHORT_K, ncs),
                            cfm_conv_w.reshape(DEPTH * CFM_K, ncs)], dev, "ag_w_in0", deps=(mod,), paired=(0,))
    W, Mo, Vo = lax.optimization_barrier((ag_in0["tok"], (W, Mo, Vo)))[1]
    (norm1_g, norm2_g, w_in, w_a_out, w_b_out, w_c_out, w_o, w_ffn_in, w_ffn_out, sgu_ln_g, sgu_ln_b, w_sgu, b_sgu,
     cfm_conv_b, cfm_ln_g, cfm_ln_b, final_g) = [W[k] for k in (
         "norm1_g", "norm2_g", "w_in", "w_a_out", "w_b_out", "w_c_out", "w_o", "w_ffn_in", "w_ffn_out", "sgu_ln_g",
         "sgu_ln_b", "w_sgu", "b_sgu", "cfm_conv_b", "cfm_ln_g", "cfm_ln_b", "final_g")]
    m_w_ada, v_w_ada = Mo["w_ada"], Vo["w_ada"]
    xl0, h0, ht0 = _norm_fwd(x0, None, _rows(jnp.zeros((D,), F32), norm1_g[0], mod[0, 1], mod[0, 0]), "norm1_fwd0",
                             deps=(ag_in0["tok"],))
    ag_rest0 = _gather_start(shards_of(0)[1:], dev, "ag_rest0", deps=(h0,))

    tril = jnp.tril(jnp.ones((CHUNK, CHUNK), dtype=bool))

    def layer_consts(l):
        wt = jnp.where(tril[None], w_sgu[l], 0.0).astype(BF16)
        return dict(sgu_ln=_rows(sgu_ln_g[l], sgu_ln_b[l]), wtril=wt, wtril_t=jnp.swapaxes(wt, 1, 2),
                    bias_full=jnp.repeat(b_sgu[l].T, LANE, axis=1), cvec=_rows(cfm_conv_b[l], cfm_ln_g[l], cfm_ln_b[l]))

    def rest_of(g):
        return dict(w_a=g[0].reshape(1, D, D), w_b=g[1].reshape(1, D, D), w_c=g[2].reshape(1, D, D),
                    w_o=g[3].reshape(1, D, D), w_fi=jnp.transpose(g[4], (1, 0, 2)).reshape(1, D, F2),
                    w_fo=g[5].reshape(1, FF, D))

    consts = [layer_consts(l) for l in range(DEPTH)]
    sharded_small = ("w_short", "cfm_conv_w")

    def param_get(T):
        def get(name, l):
            if name == "final_g":
                return T[name]
            return None if name in sharded_small or name == "loss" else T[name][l]
        return get

    packs = [_pack(param_get(T), D) for T in (W, Mo, Vo)]
    ncr = DEPTH * (SHORT_K + CFM_K)
    padr = (-ncr) % 8
    convw_wmv = [jnp.pad(jnp.concatenate([T["w_short"].reshape(-1, ncs), T["cfm_conv_w"].reshape(-1, ncs)]),
                         ((0, padr), (0, 0))) for T in (W, Mo, Vo)]
    early_work = [ag_rest0["tok"], *packs, *convw_wmv] + [a for cl in consts for a in cl.values()]
    ag_in0 = _gather_mid(ag_in0, early_work, "ag_w_in0")
    g_in0 = _gather_finish(ag_in0, ag_in0["tok"], "ag_w_in0")
    w_short_full = jnp.transpose(g_in0[1], (1, 0, 2)).reshape(DEPTH, SHORT_K, D)
    cfm_w_full = jnp.transpose(g_in0[2], (1, 0, 2)).reshape(DEPTH, CFM_K, D)
    for l in range(DEPTH):
        consts[l]["wsh"] = jnp.pad(w_short_full[l], ((0, 8 - SHORT_K), (0, 0)))
        consts[l]["cw"] = jnp.pad(cfm_w_full[l], ((0, HALO - CFM_K), (0, 0)))
    Wg = [dict(w_in=g_in0[0]), None]
    ag_l1 = None
    nin = w_in.shape[2]
    tn_in = nin if nin % 256 == 0 and nin <= 1280 else 256
    tn_fi = 512 if F2 % 512 == 0 else 256
    tn_ffn = 1408 if F2 % 1408 == 0 else tn_fi
    tn_dw = min(256, D)

    saved = []
    xcur, fprev, gprev = x0, None, None
    for l in range(DEPTH):
        sh1, sc1, g1, sh2, sc2, g2 = [mod[l, k] for k in range(N_MOD)]
        cl = consts[l]
        if l == 0:
            xl, h, ht = xl0, h0, ht0
        else:
            vec1 = _rows(gprev, norm1_g[l], sc1, sh1)
            ag_l1 = _gather_mid(ag_l1, fprev, f"ag_w{l}")
            xl, h, ht = _norm_fwd(xcur, fprev, vec1, f"norm1_fwd{l}", deps=(ag_l1["tok"],))
            g = _gather_finish(ag_l1, h, f"ag_w{l}")
            Wg[l] = dict(w_in=g[0], **rest_of(g[1:]))
        wl = Wg[l]
        z = _mm_nn(h, wl["w_in"], BF16, tm_huge, tn_in, D, f"mm_in{l}", w_outer=True)
        mix_deps = ()
        if l == 0:
            ag_rest0 = _gather_mid(ag_rest0, z, "ag_rest0")
            mix_deps = (ag_rest0["tok"],)
            if DEPTH > 1:
                ag_l1 = _gather_start(shards_of(1), dev, "ag_w1", paired=(0,))
                mix_deps += (ag_l1["tok"],)
        acts, acts_t, conv = _mixer_fwd(z, cl["wsh"], cl["sgu_ln"], cl["wtril"], cl["bias_full"], cl["cw"], cl["cvec"],
                                        f"mixer_fwd{l}", deps=mix_deps)
        if l == 0:
            wl.update(rest_of(_gather_finish(ag_rest0, acts[0], "ag_rest0")))
        merged, merged_t, ys = _branch_out(acts, [wl["w_a"][0], wl["w_b"][0], wl["w_c"][0]], z, f"branch_out{l}")
        o = _mm_nn(merged, wl["w_o"], F32, tm_big, D, D, f"mm_o{l}")
        x1, h2, h2t = _norm_fwd(xl, o, _rows(g1, norm2_g[l], sc2, sh2), f"norm2_fwd{l}")
        gu, act, act_t = _ffn_in_swiglu(h2, wl["w_fi"], tm_huge, 256, f"mm_ffn_in{l}")
        f = _mm_nn(act, wl["w_fo"], F32, tm_big, D, FF, f"mm_ffn_out{l}")
        saved.append(dict(xl=xl, ht=ht, z=z, acts_t=acts_t, conv=conv, ys=ys, merged_t=merged_t, o=o, x1=x1, h2t=h2t, gu=gu,
                          act_t=act_t, f=f, consts=cl, mod=(sh1, sc1, g1, sh2, sc2, g2)))
        xcur, fprev, gprev = x1, f, g2

    last = saved[-1]
    dxup, dfb, fsums, loss_blk = _final_bwd(last["x1"], last["f"], tgt, _rows(last["mod"][5], final_g), "final_bwd")
    loss_row = jnp.pad(loss_blk[0, 0:1], (0, D - 1))
    dgate2_next = fsums[1]
    small = [dict() for _ in range(DEPTH)]
    dmods = [None] * DEPTH
    nfi = w_ffn_in.shape[2]
    early_names, late_names = ["w_ffn_out", "w_ffn_in", "w_o"], ["w_a_out", "w_b_out", "w_c_out", "w_in"]
    results = {n: None for n in early_names + late_names}

    def adam_group(names, Ps, R2s, l, deps=()):
        for n, p, r2 in zip(names, Ps, R2s):
            results[n] = _adam_big(p, r2, my_chip, W[n], Mo[n], Vo[n], l, results[n], f"adam_{n}{l}", deps)

    deferred = []
    late_prev = None
    ag_s1, gathered1 = None, None
    tk_w = min(2048, S)
    tn_dw_in = tn_in // 2 if tn_in == 1280 else tn_in
    for l in reversed(range(DEPTH)):
        sv, wl, cl = saved[l], Wg[l], saved[l]["consts"]
        sh1, sc1, g1, sh2, sc2, g2 = sv["mod"]
        dact = _mm_nt(dfb, wl["w_fo"], BF16, tm_big, FF, D, f"mm_dact{l}",
                      deps=() if late_prev is None else (late_prev["tok"], ag_s1["tok"]))
        g_fo = _mm_wgrad(sv["act_t"], dfb, 1, FF // 2, D, tk_w, f"mm_dw_ffn_out{l}")
        dgu = _swiglu_bwd(dact, sv["gu"], f"swiglu_bwd{l}")
        dh2 = _mm_nt(dgu, wl["w_fi"], F32, tm, D, F2, f"mm_dh2{l}")
        if late_prev is not None:
            deferred.append((late_names, *_scatter_finish(late_prev, dh2, f"rs_late{l + 1}"), l + 1))
            late_prev = None
        g_fi = _mm_wgrad(sv["h2t"], dgu, 1, D, tn_fi, S, f"mm_dw_ffn_in{l}")
        if ag_s1 is not None:
            ag_s1 = _gather_mid(ag_s1, g_fi, "ag_small1")
        dx1, dob, s2 = _norm_bwd(sv["x1"], dh2, dxup, _rows(norm2_g[l], sc2, g1), sv["o"], f"norm2_bwd{l}",
                                 deps=() if ag_s1 is None else (ag_s1["tok"],))
        dmerged = _mm_nt(dob, wl["w_o"], BF16, tm_big, D, D, f"mm_dmerged{l}")
        g_o = _mm_wgrad(sv["merged_t"], dob, 1, D, tn_dw, S, f"mm_dw_o{l}")
        early = _scatter_start([g_fo.reshape(NDEV, FF // NDEV, D),
                                jnp.transpose(g_fi.reshape(D, NDEV, nfi), (1, 0, 2)),
                                g_o.reshape(NDEV, D // NDEV, D)], f"rs_early{l}")
        dys, dz = _gate_bwd(dmerged, sv["z"], sv["ys"], f"gate_bwd{l}", deps=(early["tok"],))
        if ag_s1 is not None:
            gathered1 = _gather_finish(ag_s1, dys[0], "ag_small1")[0]
            ag_s1 = None
        early = _scatter_mid(early, dys[0], my_c, f"rs_early{l}")
        dacts, g_abc = [], []
        for n, key in enumerate(("w_a", "w_b", "w_c")):
            dacts.append(_mm_nt(dys[n], wl[key], BF16, tm_big, D, D, f"mm_dact_{key}{l}",
                                deps=(early["tok"],) if n == 0 else ()))
            g_abc.append(_mm_wgrad(sv["acts_t"][n], dys[n], 1, D, tn_dw, S, f"mm_d{key}{l}"))
        dz, mvec, dcw, dws, dbs = _mixer_bwd(sv["z"], dacts, sv["conv"], dz, cl["wsh"], cl["sgu_ln"], cl["wtril"],
                                             cl["wtril_t"], cl["bias_full"], cl["cw"], cl["cvec"], f"mixer_bwd{l}")
        dh = _mm_nt(dz, wl["w_in"], F32, tm_big, D, wl["w_in"].shape[2], f"mm_dh{l}")
        g_in = _mm_wgrad(sv["ht"], dz, NDEV, D, tn_dw_in, S, f"mm_dw_in{l}")
        late = _scatter_start([g.reshape(NDEV, D // NDEV, D) for g in g_abc] + [g_in], f"rs_late{l}")
        if l > 0:
            pv = saved[l - 1]
            dxup, dfb, s1 = _norm_bwd(sv["xl"], dh, dx1, _rows(norm1_g[l], sc1, pv["mod"][5]), pv["f"], f"norm1_bwd{l}",
                                      deps=(late["tok"],))
        else:
            dxup, dfb, s1 = _norm_bwd(sv["xl"], dh, dx1, _rows(norm1_g[l], sc1), None, f"norm1_bwd{l}", deps=(late["tok"],))
        deferred.append((early_names, *_scatter_finish(early, dxup, f"rs_early{l}"), l))
        dmods[l] = jnp.stack([s1[0], s1[1], s2[3], s2[0], s2[1], dgate2_next])
        dgate2_next = s1[3]
        small[l] = dict(norm1_g=s1[2], norm2_g=s2[2], sgu_ln_g=mvec[3], sgu_ln_b=mvec[4], cfm_conv_b=mvec[5],
                        cfm_ln_g=mvec[6], cfm_ln_b=mvec[7], b_sgu=dbs[:, :, 0],
                        w_sgu=jnp.where(tril[None], dws, 0.0), b_ada=dmods[l], w_short=mvec[0:SHORT_K],
                        cfm_conv_w=dcw[0:CFM_K])
        small_get = lambda name, k: {"final_g": fsums[0], "loss": loss_row}.get(name) if k is None else small[k][name]
        if l > 0:
            late_prev = _scatter_mid(late, dxup, my_c, f"rs_late{l}")
            ag_s1 = _gather_start([_pack(small_get, D, layers=(l,), tail=True)], dev, "ag_small1", deps=(late_prev["tok"],))
    grad_x = dxup.reshape(x.shape)

    gathered0 = _all_gather([_pack(small_get, D, layers=(0,), tail=False)], "ag_small0", deps=(dxup,))[0]
    late_prev = _scatter_mid(late, gathered0, my_c, "rs_late0")
    gathered = jnp.concatenate([gathered0, gathered1], axis=1)
    sg, sd, sm, sv_ = _adam_small(gathered, *packs, name="adam_small", deps=(late_prev["tok"],))
    loss = sg[FINAL_ROW + 1, 0]
    out = {}
    for name in order:
        if name in SMALL_ROWS and name not in sharded_small:
            out[name] = tuple(_unpack(p, name, W[name].shape) for p in (sg, sd, sm, sv_))
    out["final_g"] = tuple(p[FINAL_ROW] for p in (sg, sd, sm, sv_))

    def my_cols(name):
        full = _unpack(sg, name, (DEPTH, SMALL_ROWS[name][1], D))
        return lax.dynamic_slice_in_dim(full, dev * ncs, ncs, axis=2)

    gcs = jnp.concatenate([my_cols("w_short").reshape(-1, ncs), my_cols("cfm_conv_w").reshape(-1, ncs)])
    cd, cm, cv = _adam_plain(jnp.pad(gcs, ((0, padr), (0, 0))), *convw_wmv, "adam_convw")
    nsh = DEPTH * SHORT_K
    out["w_short"] = tuple(a[0:nsh].reshape(w_short.shape) for a in (gcs, cd, cm, cv))
    out["cfm_conv_w"] = tuple(a[nsh:ncr].reshape(cfm_conv_w.shape) for a in (gcs, cd, cm, cv))

    dm_all = jnp.stack([gathered[:, l * ROWS_PER_LAYER + 136:l * ROWS_PER_LAYER + 136 + N_MOD, :].reshape(NDEV, N_MOD * D)
                        for l in range(DEPTH)])
    dm_mine = lax.dynamic_slice_in_dim(dm_all, dev * ncol, ncol, axis=2)
    out["w_ada"] = tuple(_adam_ada(jnp.transpose(c_act), dm_mine, w_ada, m_w_ada, v_w_ada, "adam_ada"))

    for names, Ps, R2s, l in deferred:
        adam_group(names, Ps, R2s, l, deps=(late_prev["tok"],))
    adam_group(late_names, *_scatter_finish(late_prev, results["w_o"][0], "rs_late0"), 0)
    for n in early_names + late_names:
        out[n] = tuple(results[n])

    grads = [out[n][0] for n in order]
    deltas = [out[n][1] for n in order]
    new_m = [out[n][2] for n in order]
    new_v = [out[n][3] for n in order]
    return (loss, grad_x, *grads, *deltas, *new_m, *new_v)
```

```python
import functools
import math

import jax
import jax.numpy as jnp
from jax import lax
from jax.experimental import pallas as pl
from jax.experimental.pallas import tpu as pltpu

F32, BF16 = jnp.float32, jnp.bfloat16
NDEV = 8
NCHIP = NDEV // 2
DEPTH = 2
EPS = 1e-6
CHUNK = 128
NG = 8
SHORT_K = 3
CFM_K = 31
HALO = 32
N_MOD = 6
LANE = 128
VMEM_LIMIT = 56 * 1024 * 1024
ADAM_LR, ADAM_B1, ADAM_B2, ADAM_EPS, ADAM_WD, ADAM_STEP = 0.001, 0.9, 0.999, 1e-08, 0.01, 10
_G0 = math.sqrt(2.0 / math.pi)
_G1 = 0.044715
MESH = pl.DeviceIdType.MESH
ANY = pl.BlockSpec(memory_space=pl.ANY)


def _pcall(body, **kw):
    return pl.pallas_call(body, **kw)


def _params(sem=None):
    return pltpu.CompilerParams(dimension_semantics=sem, vmem_limit_bytes=VMEM_LIMIT)


def _sds(shape, dtype):
    return jax.ShapeDtypeStruct(tuple(shape), dtype)


def _mm_body(dims, nk, out_f32):
    def body(a_ref, b_ref, o_ref, *scr):
        k = pl.program_id(2)
        part = lax.dot_general(a_ref[...], b_ref[...], dims, preferred_element_type=F32)
        if nk == 1:
            o_ref[...] = part.reshape(o_ref.shape).astype(o_ref.dtype)
        elif out_f32:
            @pl.when(k == 0)
            def _():
                o_ref[...] = part.reshape(o_ref.shape)

            @pl.when(k > 0)
            def _():
                o_ref[...] += part.reshape(o_ref.shape)
        else:
            acc = scr[0]

            @pl.when(k == 0)
            def _():
                acc[...] = part

            @pl.when(k > 0)
            def _():
                acc[...] += part

            @pl.when(k == nk - 1)
            def _():
                o_ref[...] = acc[...].astype(o_ref.dtype)
    return body


def _after(body, n_in, deps):
    nd = len(deps)
    if nd == 0:
        return body

    def ordered(*refs):
        return body(*refs[:n_in], *refs[n_in + nd:])
    return ordered


def _mm_call(body, grid, in_specs, out_spec, out_shape, acc_shape, name, deps=()):
    scratch = [] if acc_shape is None else [pltpu.VMEM(acc_shape, F32)]
    return _pcall(_after(body, 2, deps), grid=grid, in_specs=in_specs + [ANY] * len(deps), out_specs=out_spec,
                  out_shape=out_shape, scratch_shapes=scratch, name=name,
                  compiler_params=_params(("parallel", "parallel", "arbitrary")))


def _mm_nn(a, b3, out_dtype, tm, tn, tk, name, w_outer=False, deps=()):
    M, K = a.shape
    G, _, Nb = b3.shape
    npb, nk = Nb // tn, K // tk
    out_f32 = out_dtype == F32
    body = _mm_body((((1,), (0,)), ((), ())), nk, out_f32)
    if w_outer:
        grid = (G * npb, M // tm, nk)
        ij = lambda p, q: (q, p)
    else:
        grid = (M // tm, G * npb, nk)
        ij = lambda p, q: (p, q)

    def a_map(p, q, k):
        i, j = ij(p, q)
        return (i, k)

    def b_map(p, q, k):
        i, j = ij(p, q)
        return (j // npb, k, j % npb)

    def o_map(p, q, k):
        return ij(p, q)

    def wrapped(a_ref, b_ref, o_ref, *scr):
        body(a_ref, b_ref, o_ref, *scr)

    return _mm_call(wrapped, grid, [pl.BlockSpec((tm, tk), a_map), pl.BlockSpec((None, tk, tn), b_map)],
                    pl.BlockSpec((tm, tn), o_map), _sds((M, G * Nb), out_dtype),
                    None if (nk == 1 or out_f32) else (tm, tn), name, deps)(a, b3, *deps)


def _mm_nt(a, b3, out_dtype, tm, tn, tk, name, deps=()):
    M, _ = a.shape
    G, Ko, Nb = b3.shape
    kpb = Nb // tk
    nk = G * kpb
    out_f32 = out_dtype == F32
    body = _mm_body((((1,), (1,)), ((), ())), nk, out_f32)

    def wrapped(a_ref, b_ref, o_ref, *scr):
        body(a_ref, b_ref, o_ref, *scr)

    return _mm_call(wrapped, (M // tm, Ko // tn, nk),
                    [pl.BlockSpec((tm, tk), lambda i, j, k: (i, k)),
                     pl.BlockSpec((None, tn, tk), lambda i, j, k: (k // kpb, j, k % kpb))],
                    pl.BlockSpec((tm, tn), lambda i, j, k: (i, j)), _sds((M, Ko), out_dtype),
                    None if (nk == 1 or out_f32) else (tm, tn), name, deps)(a, b3, *deps)


def _mm_wgrad(at, b, G, tm, tn, tk, name, deps=()):
    M, T = at.shape
    Nb = b.shape[1] // G
    npb, nk = Nb // tn, T // tk
    body = _mm_body((((1,), (0,)), ((), ())), nk, False)

    def wrapped(a_ref, b_ref, o_ref, *scr):
        body(a_ref, b_ref, o_ref, *scr)

    a = at
    in_specs = [pl.BlockSpec((tm, tk), lambda i, j, k: (i, k)), pl.BlockSpec((tk, tn), lambda i, j, k: (k, j))]
    out_spec = pl.BlockSpec((None, tm, tn), lambda i, j, k: (j // npb, i, j % npb))
    return _mm_call(wrapped, (M // tm, G * npb, nk), in_specs, out_spec, _sds((G, M, Nb), BF16),
                    None if nk == 1 else (tm, tn), name, deps)(a, b, *deps)


def _rsum(v):
    return jnp.sum(v, axis=0, keepdims=True)


def _rmean(v):
    return jnp.mean(v, axis=-1, keepdims=True)


def _gelu(x):
    t = jnp.tanh(_G0 * (x + _G1 * (x * x * x)))
    return x * (0.5 * (1.0 + t)), t


def _dgelu(x, t):
    return 0.5 * (1.0 + t) + 0.5 * x * (1.0 - t * t) * (_G0 * (1.0 + 3.0 * _G1 * (x * x)))


def _sigmoid(x):
    return 1.0 / (1.0 + jnp.exp(-x))


def _fill_shifted(ext, rot):
    v = ext[...]
    n = v.shape[0]
    for b in range(1, 8):
        rot[b - 1] = pltpu.roll(v, n - b, 0)


def _rows_at(ext, rot, s, tm, cs=slice(None)):
    a, b = divmod(s, 8)
    return ext[8 * a:8 * a + tm, cs] if b == 0 else rot[b - 1, 8 * a:8 * a + tm, cs]


def _causal_conv(w_ref, taps, bias, ext, rot, offset, tm, out):
    D = out.shape[1]
    for cb in range(D // LANE):
        cs = slice(cb * LANE, (cb + 1) * LANE)
        acc = None
        for k, o in zip(taps, offset):
            term = w_ref[k:k + 1, cs] * _rows_at(ext, rot, o, tm, cs)
            acc = term if acc is None else acc + term
        out[:, cs] = acc if bias is None else acc + bias[:, cs]


def _rows(*vs):
    a = jnp.stack([v.astype(F32) for v in vs])
    return jnp.pad(a, ((0, 8 - len(vs)), (0, 0)))


def _row_spec(tm, D):
    return pl.BlockSpec((tm, D), lambda i: (i, 0))


def _const_spec(shape):
    nd = len(shape)
    return pl.BlockSpec(shape, lambda i: (0,) * nd)


def _norm_fwd(xp, f, vec, name, deps=()):
    S, D = xp.shape
    tm = min(256, S)
    has_f = f is not None

    def body(*refs):
        if has_f:
            xp_ref, f_ref, vec_ref, xo_ref, h_ref, ht_ref = refs
            x = xp_ref[...] + vec_ref[0:1, :] * f_ref[...]
            xo_ref[...] = x
        else:
            xp_ref, vec_ref, h_ref, ht_ref = refs
            x = xp_ref[...]
        r = lax.rsqrt(_rmean(x * x) + EPS)
        h = (x * r) * vec_ref[1:2, :]
        h = h * (1.0 + vec_ref[2:3, :]) + vec_ref[3:4, :]
        h_ref[...] = h.astype(BF16)
        ht_ref[...] = h.T.astype(BF16)

    rs = _row_spec(tm, D)
    ins = [xp, f, vec] if has_f else [xp, vec]
    in_specs = ([rs, rs] if has_f else [rs]) + [_const_spec((8, D))]
    out_shape = ([_sds((S, D), F32)] if has_f else []) + [_sds((S, D), BF16), _sds((D, S), BF16)]
    out_specs = [rs] * (len(out_shape) - 1) + [pl.BlockSpec((D, tm), lambda i: (0, i))]
    outs = _pcall(_after(body, len(ins), deps), grid=(S // tm,), in_specs=in_specs + [ANY] * len(deps),
                  out_specs=out_specs, out_shape=out_shape, name=name,
                  compiler_params=_params(("parallel",)))(*ins, *deps)
    return (outs[0], outs[1], outs[2]) if has_f else (xp, outs[0], outs[1])


def _mixer_fwd(z, wsh, sgu_ln, wtril, bias_full, cw, cvec, name, deps=()):
    S = z.shape[0]
    D = wsh.shape[1]
    tm = CHUNK

    def body(z_ref, wsh_ref, sln_ref, wt_ref, bias_ref, cw_ref, cv_ref, oa_ref, ob_ref, oc_ref, ta_ref, tb_ref, tc_ref,
             conv_ref, pe, ge, gr, cbuf):
        i = pl.program_id(0)

        @pl.when(i == 0)
        def _():
            pe[0:HALO, :] = jnp.zeros((HALO, D), F32)
            ge[0:HALO, :] = jnp.zeros((HALO, D), F32)

        def col(n):
            return z_ref[:, n * D:(n + 1) * D].astype(F32)

        pe[HALO:HALO + tm, :] = col(1) * col(2)
        q = wsh_ref[0:1, :] * pe[HALO - 2:HALO - 2 + tm, :]
        q = q + wsh_ref[1:2, :] * pe[HALO - 1:HALO - 1 + tm, :]
        q = q + wsh_ref[2:3, :] * pe[HALO:HALO + tm, :]
        act_a = col(0) * q
        oa_ref[...] = act_a.astype(BF16)
        ta_ref[...] = act_a.T.astype(BF16)
        gu, _ = _gelu(col(3))
        gv, _ = _gelu(col(4))
        d = gv - _rmean(gv)
        nrm = d * lax.rsqrt(_rmean(d * d) + EPS)
        vnb = (nrm * sln_ref[0:1, :] + sln_ref[1:2, :]).astype(BF16)
        for g in range(NG):
            cs = slice(g * LANE, (g + 1) * LANE)
            mixed = jnp.dot(wt_ref[g], vnb[:, cs], preferred_element_type=F32) + bias_ref[:, cs]
            act_b = gu[:, cs] * mixed
            ob_ref[:, cs] = act_b.astype(BF16)
            tb_ref[cs, :] = act_b.T.astype(BF16)
        ge[HALO:HALO + tm, :] = col(5) * _sigmoid(col(6))
        _fill_shifted(ge, gr)
        o0 = HALO - (CFM_K - 1)
        _causal_conv(cw_ref, range(CFM_K), cv_ref[0:1, :], ge, gr, range(o0, o0 + CFM_K), tm, cbuf)
        conv = cbuf[...]
        conv_ref[...] = conv.astype(BF16)
        d = conv - _rmean(conv)
        ln = (d * lax.rsqrt(_rmean(d * d) + EPS)) * cv_ref[1:2, :] + cv_ref[2:3, :]
        act_c = ln * _sigmoid(ln)
        oc_ref[...] = act_c.astype(BF16)
        tc_ref[...] = act_c.T.astype(BF16)
        pe[0:HALO, :] = pe[tm:tm + HALO, :]
        ge[0:HALO, :] = ge[tm:tm + HALO, :]

    rs = _row_spec(tm, D)
    outs = _pcall(
        _after(body, 7, deps), grid=(S // tm,),
        in_specs=[pl.BlockSpec((tm, 7 * D), lambda i: (i, 0)), _const_spec((8, D)), _const_spec((8, D)),
                  _const_spec((NG, CHUNK, CHUNK)), _const_spec((CHUNK, D)), _const_spec((HALO, D)), _const_spec((8, D))]
        + [ANY] * len(deps),
        out_specs=[rs, rs, rs] + [pl.BlockSpec((D, tm), lambda i: (0, i))] * 3 + [rs],
        out_shape=[_sds((S, D), BF16)] * 3 + [_sds((D, S), BF16)] * 3 + [_sds((S, D), BF16)],
        scratch_shapes=[pltpu.VMEM((HALO + tm, D), F32), pltpu.VMEM((HALO + tm, D), F32),
                        pltpu.VMEM((7, HALO + tm, D), F32), pltpu.VMEM((tm, D), F32)],
        name=name, compiler_params=_params(("arbitrary",)))(z, wsh, sgu_ln, wtril, bias_full, cw, cvec, *deps)
    return outs[:3], outs[3:6], outs[6]


def _branch_out(acts, ws, z, name):
    S, D = acts[0].shape
    tm = min(256, S)

    def body(a0, a1, a2, w0, w1, w2, g0, g1, g2, m_ref, mt_ref, y_ref):
        m = None
        for n, (a, w, g) in enumerate(((a0, w0, g0), (a1, w1, g1), (a2, w2, g2))):
            y = jnp.dot(a[...], w[...], preferred_element_type=F32)
            y_ref[n] = y.astype(BF16)
            t = _sigmoid(g[...].astype(F32)) * y
            m = t if m is None else m + t
        m_ref[...] = m.astype(BF16)
        mt_ref[...] = m.T.astype(BF16)

    rs = _row_spec(tm, D)
    gate_specs = [pl.BlockSpec((tm, D), functools.partial(lambda i, n: (i, 7 + n), n=n)) for n in range(3)]
    return _pcall(body, grid=(S // tm,),
                  in_specs=[rs, rs, rs] + [_const_spec((D, D))] * 3 + gate_specs,
                  out_specs=[rs, pl.BlockSpec((D, tm), lambda i: (0, i)), pl.BlockSpec((3, tm, D), lambda i: (0, i, 0))],
                  out_shape=[_sds((S, D), BF16), _sds((D, S), BF16), _sds((3, S, D), BF16)], name=name,
                  compiler_params=_params(("parallel",)))(*acts, *ws, z, z, z)


def _ffn_in_swiglu(h2, w3, tm, tn, name):
    S, D = h2.shape
    F = w3.shape[2] // 2
    nj = F // tn

    def body(a_ref, wg_ref, wu_ref, gu_ref, act_ref, actt_ref):
        a = a_ref[...]
        g = jnp.dot(a, wg_ref[...], preferred_element_type=F32)
        u = jnp.dot(a, wu_ref[...], preferred_element_type=F32)
        gu_ref[0] = g.astype(BF16)
        gu_ref[1] = u.astype(BF16)
        act = (g * _sigmoid(g)) * u
        act_ref[...] = act.astype(BF16)
        actt_ref[...] = act.T.astype(BF16)

    return _pcall(body, grid=(S // tm, nj),
                  in_specs=[pl.BlockSpec((tm, D), lambda i, j: (i, 0)), pl.BlockSpec((None, D, tn), lambda i, j: (0, 0, j)),
                            pl.BlockSpec((None, D, tn), lambda i, j: (0, 0, j + nj))],
                  out_specs=[pl.BlockSpec((2, tm, tn), lambda i, j: (0, i, j)), pl.BlockSpec((tm, tn), lambda i, j: (i, j)),
                             pl.BlockSpec((tn, tm), lambda i, j: (j, i))],
                  out_shape=[_sds((2, S, F), BF16), _sds((S, F), BF16), _sds((F, S), BF16)], name=name,
                  compiler_params=_params(("parallel", "parallel")))(h2, w3, w3)


def _swiglu_bwd(dact, gu, name):
    _, S, F = gu.shape
    F2 = 2 * F
    tm = min(256, S)

    def body(d_ref, g_ref, u_ref, o_ref):
        g = g_ref[...].astype(F32)
        sg = _sigmoid(g)
        d = d_ref[...].astype(F32)
        o_ref[:, 0:F] = (d * u_ref[...].astype(F32) * (sg * (1.0 + g * (1.0 - sg)))).astype(BF16)
        o_ref[:, F:2 * F] = (d * (g * sg)).astype(BF16)

    return _pcall(body, grid=(S // tm,),
                  in_specs=[pl.BlockSpec((tm, F), lambda i: (i, 0)), pl.BlockSpec((None, tm, F), lambda i: (0, i, 0)),
                            pl.BlockSpec((None, tm, F), lambda i: (1, i, 0))],
                  out_specs=pl.BlockSpec((tm, F2), lambda i: (i, 0)), out_shape=_sds((S, F2), BF16), name=name,
                  compiler_params=_params(("parallel",)))(dact, gu, gu)


def _final_bwd(x1, f, tgt, vec, name):
    S, D = x1.shape
    tm = min(256, S)

    def body(x_ref, f_ref, t_ref, vec_ref, dx_ref, df_ref, sums_ref, loss_ref):
        @pl.when(pl.program_id(0) == 0)
        def _():
            sums_ref[...] = jnp.zeros_like(sums_ref)
            loss_ref[...] = jnp.zeros_like(loss_ref)

        gate, fg = vec_ref[0:1, :], vec_ref[1:2, :]
        fv = f_ref[...]
        x = x_ref[...] + gate * fv
        r = lax.rsqrt(_rmean(x * x) + EPS)
        xn = x * r
        diff = xn * fg - t_ref[...]
        per_tok = _rmean(diff * diff)
        loss_ref[...] += 0.5 * jnp.sum(per_tok, axis=0, keepdims=True)
        dy = diff * (1.0 / D)
        sums_ref[0:1, :] += _rsum(dy * xn)
        dxn = dy * fg
        dx = r * (dxn - xn * _rmean(dxn * xn))
        sums_ref[1:2, :] += _rsum(dx * fv)
        dx_ref[...] = dx
        df_ref[...] = (dx * gate).astype(BF16)

    rs = _row_spec(tm, D)
    return _pcall(body, grid=(S // tm,), in_specs=[rs, rs, rs, _const_spec((8, D))],
                  out_specs=[rs, rs, _const_spec((8, D)), _const_spec((8, LANE))],
                  out_shape=[_sds((S, D), F32), _sds((S, D), BF16), _sds((8, D), F32), _sds((8, LANE), F32)],
                  name=name, compiler_params=_params(("arbitrary",)))(x1, f, tgt, vec)


def _norm_bwd(xin, dh, dxup, vec, fprev, name, deps=()):
    S, D = xin.shape
    tm = min(256, S)
    has_prev = fprev is not None

    def body(*refs):
        if has_prev:
            x_ref, dh_ref, up_ref, vec_ref, fp_ref, dx_ref, dp_ref, sums_ref = refs
        else:
            x_ref, dh_ref, up_ref, vec_ref, dx_ref, sums_ref = refs

        @pl.when(pl.program_id(0) == 0)
        def _():
            sums_ref[...] = jnp.zeros_like(sums_ref)

        g, scale = vec_ref[0:1, :], vec_ref[1:2, :]
        x = x_ref[...]
        r = lax.rsqrt(_rmean(x * x) + EPS)
        xn = x * r
        dhv = dh_ref[...]
        sums_ref[0:1, :] += _rsum(dhv)
        sums_ref[1:2, :] += _rsum(dhv * (xn * g))
        dm = dhv * (1.0 + scale)
        sums_ref[2:3, :] += _rsum(dm * xn)
        dxn = dm * g
        dx = up_ref[...] + r * (dxn - xn * _rmean(dxn * xn))
        dx_ref[...] = dx
        if has_prev:
            sums_ref[3:4, :] += _rsum(dx * fp_ref[...])
            dp_ref[...] = (dx * vec_ref[2:3, :]).astype(BF16)

    rs = _row_spec(tm, D)
    ins = [xin, dh, dxup, vec] + ([fprev] if has_prev else [])
    in_specs = [rs, rs, rs, _const_spec((8, D))] + ([rs] if has_prev else [])
    out_shape = [_sds((S, D), F32)] + ([_sds((S, D), BF16)] if has_prev else []) + [_sds((8, D), F32)]
    out_specs = [rs] + ([rs] if has_prev else []) + [_const_spec((8, D))]
    outs = _pcall(_after(body, len(ins), deps), grid=(S // tm,), in_specs=in_specs + [ANY] * len(deps),
                  out_specs=out_specs, out_shape=out_shape, name=name,
                  compiler_params=_params(("arbitrary",)))(*ins, *deps)
    return (outs[0], outs[1], outs[2]) if has_prev else (outs[0], None, outs[1])


def _gate_bwd(dmerged, z, ys, name, deps=()):
    S, D = dmerged.shape
    tm = min(512, S)
    ncol = z.shape[1] // D

    def body(dm_ref, g_ref, y_ref, dya_ref, dyb_ref, dyc_ref, dz_ref):
        n = pl.program_id(1)
        sg = _sigmoid(g_ref[...].astype(F32))
        dm = dm_ref[...].astype(F32)
        dy = (dm * sg).astype(BF16)
        for k, ref in enumerate((dya_ref, dyb_ref, dyc_ref)):
            @pl.when(n == k)
            def _(ref=ref):
                ref[...] = dy
        dz_ref[...] = (dm * y_ref[...].astype(F32) * (sg * (1.0 - sg))).astype(BF16)

    row = pl.BlockSpec((tm, D), lambda i, n: (i, 0))
    outs = _pcall(_after(body, 3, deps), grid=(S // tm, 3),
                  in_specs=[row, pl.BlockSpec((tm, D), lambda i, n: (i, 7 + n)),
                            pl.BlockSpec((None, tm, D), lambda i, n: (n, i, 0))] + [ANY] * len(deps),
                  out_specs=[row, row, row, pl.BlockSpec((tm, D), lambda i, n: (i, 7 + n))],
                  out_shape=[_sds((S, D), BF16)] * 3 + [_sds((S, ncol * D), BF16)], name=name,
                  compiler_params=_params(("parallel", "arbitrary")))(dmerged, z, ys, *deps)
    return outs[:3], outs[3]


def _mixer_bwd(z, dacts, conv, dz, wsh, sgu_ln, wtril, wtril_t, bias_full, cw, cvec, name):
    S = z.shape[0]
    D = wsh.shape[1]
    tm = CHUNK
    nt = S // tm
    hb = tm // HALO

    def body(zc, zp, da_ref, db_ref, dc_ref, conv_ref, wsh_ref, sln_ref, wt_ref, wtt_ref, bias_ref, cw_ref, cv_ref, _dz_in,
             dz_ref, vec_ref, dcw_ref, dws_ref, dbs_ref, pe, ge, dqe, dce, gr, dcr, cbuf, dcw8):
        i = pl.program_id(0)
        rb = nt - 1 - i

        @pl.when(i == 0)
        def _():
            vec_ref[...] = jnp.zeros_like(vec_ref)
            dcw8[...] = jnp.zeros_like(dcw8)
            dws_ref[...] = jnp.zeros_like(dws_ref)
            dbs_ref[...] = jnp.zeros_like(dbs_ref)
            dqe[tm:tm + HALO, :] = jnp.zeros((HALO, D), F32)
            dce[tm:tm + HALO, :] = jnp.zeros((HALO, D), F32)

        keep = (rb > 0).astype(F32)

        def col(n):
            return zc[:, n * D:(n + 1) * D].astype(F32)

        def pcol(n):
            return zp[:, n * D:(n + 1) * D].astype(F32)

        c_a, x_a = col(1), col(2)
        pe[0:HALO, :] = keep * (pcol(1) * pcol(2))
        pe[HALO:HALO + tm, :] = c_a * x_a
        q = wsh_ref[0:1, :] * pe[HALO - 2:HALO - 2 + tm, :]
        q = q + wsh_ref[1:2, :] * pe[HALO - 1:HALO - 1 + tm, :]
        q = q + wsh_ref[2:3, :] * pe[HALO:HALO + tm, :]
        dact = da_ref[...].astype(F32)
        dz_ref[:, 0:D] = (dact * q).astype(BF16)
        dq = dact * col(0)
        dqe[0:tm, :] = dq
        dp = wsh_ref[2:3, :] * dq + wsh_ref[1:2, :] * dqe[1:1 + tm, :] + wsh_ref[0:1, :] * dqe[2:2 + tm, :]
        dz_ref[:, D:2 * D] = (dp * x_a).astype(BF16)
        dz_ref[:, 2 * D:3 * D] = (dp * c_a).astype(BF16)
        for k in range(SHORT_K):
            o = HALO - (SHORT_K - 1) + k
            vec_ref[k:k + 1, :] += _rsum(dq * pe[o:o + tm, :])
        u, v = col(3), col(4)
        gu, tu = _gelu(u)
        gv, tv = _gelu(v)
        d = gv - _rmean(gv)
        rstd = lax.rsqrt(_rmean(d * d) + EPS)
        nrm = d * rstd
        vnb = (nrm * sln_ref[0:1, :] + sln_ref[1:2, :]).astype(BF16)
        dact = db_ref[...].astype(F32)
        dvn_parts, dgu_parts = [], []
        for g in range(NG):
            cs = slice(g * LANE, (g + 1) * LANE)
            vg = vnb[:, cs]
            mixed = jnp.dot(wt_ref[g], vg, preferred_element_type=F32) + bias_ref[:, cs]
            dgu_parts.append(dact[:, cs] * mixed)
            dmixed = dact[:, cs] * gu[:, cs]
            dmb = dmixed.astype(BF16)
            dws_ref[g] += lax.dot_general(dmb, vg, (((1,), (1,)), ((), ())), preferred_element_type=F32)
            dbs_ref[g] += jnp.broadcast_to(jnp.sum(dmixed, axis=1, keepdims=True), (CHUNK, LANE))
            dvn_parts.append(jnp.dot(wtt_ref[g], dmb, preferred_element_type=F32))
        dgu = jnp.concatenate(dgu_parts, axis=1)
        dvn = jnp.concatenate(dvn_parts, axis=1)
        dz_ref[:, 3 * D:4 * D] = (dgu * _dgelu(u, tu)).astype(BF16)
        vec_ref[3:4, :] += _rsum(dvn * nrm)
        vec_ref[4:5, :] += _rsum(dvn)
        dn = dvn * sln_ref[0:1, :]
        dgv = rstd * (dn - _rmean(dn) - nrm * _rmean(dn * nrm))
        dz_ref[:, 4 * D:5 * D] = (dgv * _dgelu(v, tv)).astype(BF16)
        a_c = col(5)
        sg = _sigmoid(col(6))
        ge[0:HALO, :] = keep * (pcol(5) * _sigmoid(pcol(6)))
        ge[HALO:HALO + tm, :] = a_c * sg
        _fill_shifted(ge, gr)
        o0 = HALO - (CFM_K - 1)
        conv = conv_ref[...].astype(F32)
        d = conv - _rmean(conv)
        rstd = lax.rsqrt(_rmean(d * d) + EPS)
        nrm = d * rstd
        ln = nrm * cv_ref[1:2, :] + cv_ref[2:3, :]
        sl = _sigmoid(ln)
        dln = dc_ref[...].astype(F32) * (sl * (1.0 + ln * (1.0 - sl)))
        vec_ref[6:7, :] += _rsum(dln * nrm)
        vec_ref[7:8, :] += _rsum(dln)
        dn = dln * cv_ref[1:2, :]
        dconv = rstd * (dn - _rmean(dn) - nrm * _rmean(dn * nrm))
        vec_ref[5:6, :] += _rsum(dconv)
        dce[0:tm, :] = dconv
        _fill_shifted(dce, dcr)
        _causal_conv(cw_ref, range(CFM_K), None, dce, dcr, [CFM_K - 1 - k for k in range(CFM_K)], tm, cbuf)
        dglu = cbuf[...]
        for cb in range(D // LANE):
            cs = slice(cb * LANE, (cb + 1) * LANE)
            dcv = dce[0:tm, cs]
            for k in range(CFM_K):
                prod = dcv * _rows_at(ge, gr, o0 + k, tm, cs)
                dcw8[k, :, cs] += jnp.sum(prod.reshape(tm // 8, 8, LANE), axis=0)

        @pl.when(i == nt - 1)
        def _():
            dcw_ref[...] = jnp.sum(dcw8[...], axis=1)
        dz_ref[:, 5 * D:6 * D] = (dglu * sg).astype(BF16)
        dz_ref[:, 6 * D:7 * D] = (dglu * a_c * (sg * (1.0 - sg))).astype(BF16)
        dqe[tm:tm + HALO, :] = dqe[0:HALO, :]
        dce[tm:tm + HALO, :] = dce[0:HALO, :]

    rev = lambda i: (nt - 1 - i, 0)
    rs = pl.BlockSpec((tm, D), rev)
    cur = pl.BlockSpec((tm, 7 * D), rev)
    prev = pl.BlockSpec((HALO, 7 * D), lambda i: (jnp.maximum((nt - 1 - i) * hb - 1, 0), 0))
    ext = pltpu.VMEM((HALO + tm, D), F32)
    outs = _pcall(
        body, grid=(nt,),
        in_specs=[cur, prev, rs, rs, rs, rs, _const_spec((8, D)), _const_spec((8, D)), _const_spec((NG, CHUNK, CHUNK)),
                  _const_spec((NG, CHUNK, CHUNK)), _const_spec((CHUNK, D)), _const_spec((HALO, D)), _const_spec((8, D)),
                  ANY],
        out_specs=[cur, _const_spec((8, D)), _const_spec((HALO, D)), _const_spec((NG, CHUNK, CHUNK)),
                   _const_spec((NG, CHUNK, LANE))],
        out_shape=[_sds(dz.shape, BF16), _sds((8, D), F32), _sds((HALO, D), F32), _sds((NG, CHUNK, CHUNK), F32),
                   _sds((NG, CHUNK, LANE), F32)],
        scratch_shapes=[ext, ext, ext, ext, pltpu.VMEM((7, HALO + tm, D), F32), pltpu.VMEM((7, HALO + tm, D), F32),
                        pltpu.VMEM((tm, D), F32), pltpu.VMEM((HALO, 8, D), F32)],
        input_output_aliases={13: 0}, name=name,
        compiler_params=_params(("arbitrary",)))(z, z, *dacts, conv, wsh, sgu_ln, wtril, wtril_t, bias_full, cw, cvec, dz)
    return outs


def _ada_fwd(c_all, w_ada_loc, name):
    nb, D = c_all.shape
    L, _, nc = w_ada_loc.shape

    def body(c_ref, w_ref, o_ref, ca_ref):
        cv = c_ref[...]
        ca = cv * _sigmoid(cv)
        ca_ref[...] = ca
        o_ref[...] = jnp.dot(ca.astype(BF16), w_ref[...].astype(BF16), preferred_element_type=F32)

    return _pcall(body, grid=(L,),
                  in_specs=[_const_spec((nb, D)), pl.BlockSpec((None, D, nc), lambda l: (l, 0, 0))],
                  out_specs=[pl.BlockSpec((None, nb, nc), lambda l: (l, 0, 0)), _const_spec((nb, D))],
                  out_shape=[_sds((L, nb, nc), F32), _sds((nb, D), F32)], name=name,
                  compiler_params=_params(("arbitrary",)))(c_all, w_ada_loc)


def _adamw(w, g, m, v):
    m = ADAM_B1 * m + (1.0 - ADAM_B1) * g
    v = ADAM_B2 * v + (1.0 - ADAM_B2) * (g * g)
    m_hat = m / (1.0 - ADAM_B1 ** ADAM_STEP)
    v_hat = v / (1.0 - ADAM_B2 ** ADAM_STEP)
    delta = -ADAM_LR * (m_hat / (jnp.sqrt(v_hat) + ADAM_EPS) + ADAM_WD * w)
    return delta, m, v


def _tile_rows(R, C, align=8):
    cap = max(align, (1536 * 1024) // (4 * C))
    best = None
    for t in range(align, R + 1, align):
        if R % t == 0 and t <= cap:
            best = t
    return R if best is None else best


def _adam_ada(ct, dm, w, m, v, name):
    L, D, nc = w.shape
    nb = ct.shape[1]
    tr = _tile_rows(D, nc)

    def body(ct_ref, dm_ref, w_ref, m_ref, v_ref, g_ref, d_ref, mo_ref, vo_ref):
        g = ct_ref[:, 0:1] * dm_ref[0:1, :]
        for b in range(1, nb):
            g = g + ct_ref[:, b:b + 1] * dm_ref[b:b + 1, :]
        g_ref[...] = g
        d_ref[...], mo_ref[...], vo_ref[...] = _adamw(w_ref[...], g, m_ref[...], v_ref[...])

    ws = pl.BlockSpec((None, tr, nc), lambda l, r: (l, r, 0))
    return _pcall(body, grid=(L, D // tr),
                  in_specs=[pl.BlockSpec((tr, nb), lambda l, r: (r, 0)), pl.BlockSpec((None, nb, nc), lambda l, r: (l, 0, 0)),
                            ws, ws, ws],
                  out_specs=[ws] * 4, out_shape=[_sds(w.shape, F32)] * 4, name=name,
                  compiler_params=_params(("parallel", "parallel")))(ct, dm, w, m, v)


def _adam_small(parts, w, m, v, name, deps=()):
    n, R, C = parts.shape
    tr = _tile_rows(R, C * n // 2)

    def body(p_ref, w_ref, m_ref, v_ref, g_ref, d_ref, mo_ref, vo_ref):
        g = p_ref[0]
        for j in range(1, n):
            g = g + p_ref[j]
        g_ref[...] = g
        d_ref[...], mo_ref[...], vo_ref[...] = _adamw(w_ref[...], g, m_ref[...], v_ref[...])

    ws = pl.BlockSpec((tr, C), lambda r: (r, 0))
    return _pcall(_after(body, 4, deps), grid=(R // tr,),
                  in_specs=[pl.BlockSpec((n, tr, C), lambda r: (0, r, 0)), ws, ws, ws] + [ANY] * len(deps),
                  out_specs=[ws] * 4, out_shape=[_sds((R, C), F32)] * 4, name=name,
                  compiler_params=_params(("parallel",)))(parts, w, m, v, *deps)


def _adam_plain(g, w, m, v, name):
    R, C = w.shape

    def body(g_ref, w_ref, m_ref, v_ref, d_ref, mo_ref, vo_ref):
        d_ref[...], mo_ref[...], vo_ref[...] = _adamw(w_ref[...], g_ref[...], m_ref[...], v_ref[...])

    ws = _const_spec((R, C))
    return _pcall(body, grid=(1,), in_specs=[ws] * 4, out_specs=[ws] * 3, out_shape=[_sds((R, C), F32)] * 3, name=name,
                  compiler_params=_params(("arbitrary",)))(g, w, m, v)


def _pair_sum(G, R1, my_c, name):
    n, R, C = G.shape
    half = n // 2
    tr = _tile_rows(R, C, align=16)

    def body(c_ref, g_ref, r_ref, o_ref):
        o_ref[...] = (g_ref[...].astype(F32) + r_ref[...].astype(F32)).astype(o_ref.dtype)

    blk = (None, tr, C)
    gs = pltpu.PrefetchScalarGridSpec(
        num_scalar_prefetch=1, grid=(half, R // tr),
        in_specs=[pl.BlockSpec(blk, lambda p, r, c: (2 * p + c[0], r, 0)), pl.BlockSpec(blk, lambda p, r, c: (p, r, 0))],
        out_specs=pl.BlockSpec(blk, lambda p, r, c: (p, r, 0)))
    return _pcall(body, grid_spec=gs, out_shape=_sds((half, R, C), G.dtype), name=name,
                  compiler_params=_params(("parallel", "parallel")))(my_c, G, R1)


def _adam_big(P, R2, my_chip, w, m, v, layer, prev, name, deps=()):
    _, R, C = P.shape
    nrecv = R2.shape[0]
    tr = _tile_rows(R, C, align=16)

    def body(p_sm, p_ref, r_ref, w_ref, m_ref, v_ref, *rest):
        g_ref, d_ref, mo_ref, vo_ref = rest[-4:]
        g = p_ref[...].astype(F32)
        for k in range(nrecv):
            g = g + r_ref[k].astype(F32)
        g_ref[...] = g
        d_ref[...], mo_ref[...], vo_ref[...] = _adamw(w_ref[...], g, m_ref[...], v_ref[...])

    ws = pl.BlockSpec((None, tr, C), lambda r, p: (layer, r, 0))
    held = [] if prev is None else list(prev)
    gs = pltpu.PrefetchScalarGridSpec(
        num_scalar_prefetch=1, grid=(R // tr,),
        in_specs=[pl.BlockSpec((None, tr, C), lambda r, p: (p[0], r, 0)),
                  pl.BlockSpec((nrecv, tr, C), lambda r, p: (0, r, 0)), ws, ws, ws] + [ANY] * (len(held) + len(deps)),
        out_specs=[ws] * 4)
    alias = {6 + i: i for i in range(len(held))}
    return _pcall(body, grid_spec=gs, out_shape=[_sds(w.shape, F32)] * 4, name=name, input_output_aliases=alias,
                  compiler_params=_params(("parallel",)))(my_chip, P, R2, w, m, v, *held, *deps)


def _place():
    return lax.axis_index("x"), lax.axis_index("y"), lax.axis_index("c")


def _all_gather(shards, name, deps=()):
    n = len(shards)

    def body(*refs):
        ins, outs = refs[:n], refs[n:2 * n]
        send_sems, recv_sems, local_sems = refs[2 * n:]
        x, y, c = _place()
        me, sibling = (x, y, c), (x, y, 1 - c)
        chips = [(1 - x, y), (x, 1 - y), (1 - x, 1 - y)]

        def slot(a, px, py, pc):
            return outs[a].at[4 * px + 2 * py + pc]

        def copy(a, k, block, to, src=None):
            return pltpu.make_async_remote_copy(
                src_ref=slot(a, *block) if src is None else src, dst_ref=slot(a, *block),
                send_sem=send_sems.at[7 * a + k], recv_sem=recv_sems.at[7 * a + k], device_id=to, device_id_type=MESH)

        mine = [pltpu.make_async_copy(ins[a], slot(a, *me), local_sems.at[a]) for a in range(n)]
        for cp in mine:
            cp.start()
        first = []
        for a in range(n):
            first.append(copy(a, 0, me, sibling, src=ins[a]))
            first += [copy(a, 1 + j, me, (*chip, c), src=ins[a]) for j, chip in enumerate(chips)]
        for cp in first:
            cp.start()
        passed = []
        for j, chip in enumerate(chips):
            for a in range(n):
                copy(a, 1 + j, (*chip, c), me).wait_recv()
                fwd = copy(a, 4 + j, (*chip, c), sibling)
                fwd.start()
                passed.append(fwd)
        for a in range(n):
            copy(a, 0, sibling, me).wait_recv()
        for j, chip in enumerate(chips):
            for a in range(n):
                copy(a, 4 + j, (*chip, 1 - c), me).wait_recv()
        for cp in first + passed:
            cp.wait_send()
        for cp in mine:
            cp.wait()

    outs = _pcall(_after(body, n, deps), in_specs=[ANY] * (n + len(deps)), out_specs=[ANY] * n,
                  out_shape=[_sds((NDEV,) + s.shape, s.dtype) for s in shards],
                  scratch_shapes=[pltpu.SemaphoreType.DMA((7 * n,)), pltpu.SemaphoreType.DMA((7 * n,)),
                                  pltpu.SemaphoreType.DMA((n,))], name=name)(*shards, *deps)
    return list(outs)


HBM = pl.BlockSpec(memory_space=pltpu.HBM)
SEM = pl.BlockSpec(memory_space=pltpu.SEMAPHORE)


def _copies(plan, refs, send_sems, recv_sems):
    return [pltpu.make_async_remote_copy(src_ref=s, dst_ref=d, send_sem=send_sems.at[k], recv_sem=recv_sems.at[k],
                                         device_id=dev, device_id_type=MESH)
            for k, (s, d, dev) in enumerate(plan(refs, *_place()))]


def _xfer_start(bufs, ncopies, plan, name, deps=()):
    n = len(bufs)

    def body(*refs):
        for cp in _copies(plan, refs[:n], refs[n], refs[n + 1]):
            cp.start()
        token = refs[2 * n + 2]
        token[...] = jnp.zeros_like(token)

    outs = _pcall(
        _after(body, n, deps), name=name,
        out_shape=(pltpu.SemaphoreType.DMA((ncopies,)), pltpu.SemaphoreType.DMA((ncopies,)),
                   *[pltpu.HBM(b.shape, b.dtype) for b in bufs], _sds((8, LANE), F32)),
        in_specs=[HBM] * n + [ANY] * len(deps),
        out_specs=(SEM, SEM, *[HBM] * n, pl.BlockSpec(memory_space=pltpu.VMEM)),
        input_output_aliases={i: 2 + i for i in range(n)},
        compiler_params=pltpu.CompilerParams(has_side_effects=pltpu.SideEffectType.DATAFLOW_SIDE_EFFECTING),
    )(*[pltpu.with_memory_space_constraint(b, pltpu.HBM) for b in bufs], *deps)
    return (outs[0], outs[1]), list(outs[2:2 + n]), outs[2 + n]


def _xfer_wait(sems, bufs, plan, after, name):
    n = len(bufs)
    after = list(after) if isinstance(after, (list, tuple)) else [after]

    def body(*refs):
        for cp in _copies(plan, refs[:n], refs[n], refs[n + 1]):
            cp.wait_send()
            cp.wait_recv()

    outs = _pcall(
        body, name=name, out_shape=tuple(pltpu.HBM(b.shape, b.dtype) for b in bufs),
        in_specs=[HBM] * n + [SEM, SEM] + [ANY] * len(after), out_specs=tuple([HBM] * n),
        input_output_aliases={i: i for i in range(n)},
        compiler_params=pltpu.CompilerParams(has_side_effects=pltpu.SideEffectType.DATAFLOW_SIDE_EFFECTING),
    )(*bufs, *sems, *after)
    return list(outs)


def _chips_of(x, y):
    return [(1 - x, y), (x, 1 - y), (1 - x, 1 - y)]


def _slot(ref, paired, chip, c):
    if not paired:
        return ref.at[2 * chip + c]
    w = ref.shape[2] // 2
    return ref.at[chip, :, pl.ds(pl.multiple_of(c * w, LANE), w)]


def _gather_plan1(n, paired):
    def plan(refs, x, y, c):
        out = []
        for a in range(n):
            blk = _slot(refs[a], a in paired, 2 * x + y, c)
            out.append((blk, blk, (x, y, 1 - c)))
            out += [(blk, blk, (px, py, c)) for px, py in _chips_of(x, y)]
        return out
    return plan


def _gather_plan2(n, paired):
    def plan(refs, x, y, c):
        out = []
        for a in range(n):
            for px, py in _chips_of(x, y):
                blk = _slot(refs[a], a in paired, 2 * px + py, c)
                out.append((blk, blk, (x, y, 1 - c)))
        return out
    return plan


def _place_own(shards, paired, name, deps=()):
    n = len(shards)
    shapes = [(NCHIP, s.shape[0], 2 * s.shape[1]) if a in paired else (NDEV,) + s.shape for a, s in enumerate(shards)]

    def body(*refs):
        ins, outs, sems = refs[:n], refs[2 * n:3 * n], refs[3 * n]
        x, y, c = _place()
        copies = [pltpu.make_async_copy(ins[a], _slot(outs[a], a in paired, 2 * x + y, c), sems.at[a]) for a in range(n)]
        for cp in copies:
            cp.start()
        for cp in copies:
            cp.wait()

    empties = [lax.empty(shape, s.dtype) for shape, s in zip(shapes, shards)]
    outs = _pcall(_after(body, 2 * n, deps), in_specs=[ANY] * (2 * n + len(deps)), out_specs=[ANY] * n,
                  out_shape=[_sds(shape, s.dtype) for shape, s in zip(shapes, shards)],
                  input_output_aliases={n + a: a for a in range(n)},
                  scratch_shapes=[pltpu.SemaphoreType.DMA((n,))], name=name)(*shards, *empties, *deps)
    return list(outs)


def _gather_start(shards, dev, name, deps=(), paired=()):
    n = len(shards)
    lands = _place_own(shards, paired, name + "_place", deps)
    sems, lands, tok = _xfer_start(lands, 4 * n, _gather_plan1(n, paired), name + "_p1_start")
    return dict(sems=sems, lands=lands, tok=tok, n=n, paired=paired)


def _gather_mid(st, after, name):
    n, paired = st["n"], st["paired"]
    lands = _xfer_wait(st["sems"], st["lands"], _gather_plan1(n, paired), after, name + "_p1_wait")
    sems, lands, tok = _xfer_start(lands, 3 * n, _gather_plan2(n, paired), name + "_p2_start")
    return dict(sems=sems, lands=lands, tok=tok, n=n, paired=paired)


def _gather_finish(st, after, name):
    return _xfer_wait(st["sems"], st["lands"], _gather_plan2(st["n"], st["paired"]), after, name + "_p2_wait")


def _scatter_plan1(n):
    def plan(refs, x, y, c):
        return [(refs[a].at[2 * p + 1 - c], refs[n + a].at[p], (x, y, 1 - c)) for a in range(n) for p in range(NCHIP)]
    return plan


def _scatter_plan2(n):
    def plan(refs, x, y, c):
        return [(refs[a].at[2 * px + py], refs[n + a].at[j], (px, py, c))
                for a in range(n) for j, (px, py) in enumerate(_chips_of(x, y))]
    return plan


def _scatter_start(Gs, name):
    n = len(Gs)
    R1s = [lax.empty((NCHIP,) + g.shape[1:], g.dtype) for g in Gs]
    sems, bufs, tok = _xfer_start(list(Gs) + R1s, NCHIP * n, _scatter_plan1(n), name + "_s1_start")
    return dict(sems=sems, bufs=bufs, tok=tok, n=n)


def _scatter_mid(st, after, my_c, name):
    n = st["n"]
    bufs = _xfer_wait(st["sems"], st["bufs"], _scatter_plan1(n), after, name + "_s1_wait")
    Ps = [_pair_sum(bufs[a], bufs[n + a], my_c, f"{name}_pair_sum{a}") for a in range(n)]
    R2s = [lax.empty((3,) + p.shape[1:], p.dtype) for p in Ps]
    sems, bufs, tok = _xfer_start(Ps + R2s, 3 * n, _scatter_plan2(n), name + "_s2_start")
    return dict(sems=sems, bufs=bufs, tok=tok, n=n)


def _scatter_finish(st, after, name):
    n = st["n"]
    bufs = _xfer_wait(st["sems"], st["bufs"], _scatter_plan2(n), after, name + "_s2_wait")
    return bufs[:n], bufs[n:]


SMALL_ROWS = {"norm1_g": (0, 1), "norm2_g": (1, 1), "sgu_ln_g": (2, 1), "sgu_ln_b": (3, 1), "cfm_conv_b": (4, 1),
              "cfm_ln_g": (5, 1), "cfm_ln_b": (6, 1), "b_sgu": (7, 1), "w_sgu": (8, 128), "b_ada": (136, N_MOD),
              "w_short": (142, SHORT_K), "cfm_conv_w": (145, CFM_K)}
ROWS_PER_LAYER = 176
FINAL_ROW = DEPTH * ROWS_PER_LAYER
PACK_ROWS = 360


def _pack(get, D, layers=tuple(range(DEPTH)), tail=True):
    parts = []
    for l in layers:
        for name, (_, nrows) in SMALL_ROWS.items():
            a = get(name, l)
            parts.append(jnp.zeros((nrows * D,), F32) if a is None else a.astype(F32).reshape(nrows * D))
    if tail:
        for name in ("final_g", "loss"):
            a = get(name, None)
            parts.append(jnp.zeros((D,), F32) if a is None else a.astype(F32).reshape(D))
        parts.append(jnp.zeros(((PACK_ROWS - FINAL_ROW - 2) * D,), F32))
    return jnp.concatenate(parts).reshape(-1, D)


def _unpack(pack, name, shape):
    D = pack.shape[1]
    r0, nrows = SMALL_ROWS[name]
    return jnp.stack([pack[l * ROWS_PER_LAYER + r0:l * ROWS_PER_LAYER + r0 + nrows] for l in range(DEPTH)]).reshape(shape)


def _mm_tiles(S):
    return min(512, S), min(1024, S)


def kernel(x, c, w_ada, b_ada, norm1_g, w_in, w_short, w_a_out, sgu_ln_g, sgu_ln_b, w_sgu, b_sgu, w_b_out, cfm_conv_w, cfm_conv_b, cfm_ln_g, cfm_ln_b, w_c_out, w_o, norm2_g, w_ffn_in, w_ffn_out, final_g, loss_target, m_w_ada, m_b_ada, m_norm1_g, m_w_in, m_w_short, m_w_a_out, m_sgu_ln_g, m_sgu_ln_b, m_w_sgu, m_b_sgu, m_w_b_out, m_cfm_conv_w, m_cfm_conv_b, m_cfm_ln_g, m_cfm_ln_b, m_w_c_out, m_w_o, m_norm2_g, m_w_ffn_in, m_w_ffn_out, m_final_g, v_w_ada, v_b_ada, v_norm1_g, v_w_in, v_w_short, v_w_a_out, v_sgu_ln_g, v_sgu_ln_b, v_w_sgu, v_b_sgu, v_w_b_out, v_cfm_conv_w, v_cfm_conv_b, v_cfm_ln_g, v_cfm_ln_b, v_w_c_out, v_w_o, v_norm2_g, v_w_ffn_in, v_w_ffn_out, v_final_g):
    W = dict(w_ada=w_ada, b_ada=b_ada, norm1_g=norm1_g, w_in=w_in, w_short=w_short, w_a_out=w_a_out, sgu_ln_g=sgu_ln_g,
             sgu_ln_b=sgu_ln_b, w_sgu=w_sgu, b_sgu=b_sgu, w_b_out=w_b_out, cfm_conv_w=cfm_conv_w, cfm_conv_b=cfm_conv_b,
             cfm_ln_g=cfm_ln_g, cfm_ln_b=cfm_ln_b, w_c_out=w_c_out, w_o=w_o, norm2_g=norm2_g, w_ffn_in=w_ffn_in,
             w_ffn_out=w_ffn_out, final_g=final_g)
    Mo = dict(w_ada=m_w_ada, b_ada=m_b_ada, norm1_g=m_norm1_g, w_in=m_w_in, w_short=m_w_short, w_a_out=m_w_a_out,
              sgu_ln_g=m_sgu_ln_g, sgu_ln_b=m_sgu_ln_b, w_sgu=m_w_sgu, b_sgu=m_b_sgu, w_b_out=m_w_b_out,
              cfm_conv_w=m_cfm_conv_w, cfm_conv_b=m_cfm_conv_b, cfm_ln_g=m_cfm_ln_g, cfm_ln_b=m_cfm_ln_b,
              w_c_out=m_w_c_out, w_o=m_w_o, norm2_g=m_norm2_g, w_ffn_in=m_w_ffn_in, w_ffn_out=m_w_ffn_out,
              final_g=m_final_g)
    Vo = dict(w_ada=v_w_ada, b_ada=v_b_ada, norm1_g=v_norm1_g, w_in=v_w_in, w_short=v_w_short, w_a_out=v_w_a_out,
              sgu_ln_g=v_sgu_ln_g, sgu_ln_b=v_sgu_ln_b, w_sgu=v_w_sgu, b_sgu=v_b_sgu, w_b_out=v_w_b_out,
              cfm_conv_w=v_cfm_conv_w, cfm_conv_b=v_cfm_conv_b, cfm_ln_g=v_cfm_ln_g, cfm_ln_b=v_cfm_ln_b,
              w_c_out=v_w_c_out, w_o=v_w_o, norm2_g=v_norm2_g, w_ffn_in=v_w_ffn_in, w_ffn_out=v_w_ffn_out,
              final_g=v_final_g)
    order = ["w_ada", "b_ada", "norm1_g", "w_in", "w_short", "w_a_out", "sgu_ln_g", "sgu_ln_b", "w_sgu", "b_sgu",
             "w_b_out", "cfm_conv_w", "cfm_conv_b", "cfm_ln_g", "cfm_ln_b", "w_c_out", "w_o", "norm2_g", "w_ffn_in",
             "w_ffn_out", "final_g"]

    assert DEPTH == 2, "the weight-gather schedule below is written for two layers"
    S, D = x.shape[1], x.shape[2]
    F2 = w_ffn_in.shape[2] * NDEV
    FF = F2 // 2
    xi, yi, ci = _place()
    dev = 4 * xi + 2 * yi + ci
    my_c = jnp.reshape(ci, (1,)).astype(jnp.int32)
    my_chip = jnp.reshape(2 * xi + yi, (1,)).astype(jnp.int32)
    tm, tm_big = _mm_tiles(S)
    tm_huge = min(2048, S)
    x0 = x.reshape(S, D)
    tgt = loss_target.reshape(S, D)

    def shards_of(l):
        return [w_in[l].astype(BF16), w_a_out[l].astype(BF16), w_b_out[l].astype(BF16), w_c_out[l].astype(BF16),
                w_o[l].astype(BF16), w_ffn_in[l].astype(BF16), w_ffn_out[l].astype(BF16)]

    c_all = _all_gather([jnp.pad(c, ((0, 7), (0, 0)))], "ag_c")[0][:, 0, :]
    modpart, c_act = _ada_fwd(c_all, w_ada, "ada_fwd")
    ncol = modpart.shape[2]
    mg = _all_gather([modpart.reshape(DEPTH * NDEV, ncol)], "ag_mod")[0].reshape(NDEV, DEPTH, NDEV, ncol)
    mine = lax.dynamic_index_in_dim(mg, dev, axis=2, keepdims=False)
    mod = (jnp.transpose(mine, (1, 0, 2)).reshape(DEPTH, N_MOD * D) + b_ada).reshape(DEPTH, N_MOD, D)

    ncs = w_short.shape[2]
    ag_in0 = _gather_start([w_in[0].astype(BF16), w_short.reshape(DEPTH * SHORT_K, ncs),
                            cfm_conv_w.reshape(DEPTH * CFM_K, ncs)], dev, "ag_w_in0", deps=(mod,), paired=(0,))
    W, Mo, Vo = lax.optimization_barrier((ag_in0["tok"], (W, Mo, Vo)))[1]
    (norm1_g, norm2_g, w_in, w_a_out, w_b_out, w_c_out, w_o, w_ffn_in, w_ffn_out, sgu_ln_g, sgu_ln_b, w_sgu, b_sgu,
     cfm_conv_b, cfm_ln_g, cfm_ln_b, final_g) = [W[k] for k in (
         "norm1_g", "norm2_g", "w_in", "w_a_out", "w_b_out", "w_c_out", "w_o", "w_ffn_in", "w_ffn_out", "sgu_ln_g",
         "sgu_ln_b", "w_sgu", "b_sgu", "cfm_conv_b", "cfm_ln_g", "cfm_ln_b", "final_g")]
    m_w_ada, v_w_ada = Mo["w_ada"], Vo["w_ada"]
    xl0, h0, ht0 = _norm_fwd(x0, None, _rows(jnp.zeros((D,), F32), norm1_g[0], mod[0, 1], mod[0, 0]), "norm1_fwd0",
                             deps=(ag_in0["tok"],))
    ag_rest0 = _gather_start(shards_of(0)[1:], dev, "ag_rest0", deps=(h0,))

    tril = jnp.tril(jnp.ones((CHUNK, CHUNK), dtype=bool))

    def layer_consts(l):
        wt = jnp.where(tril[None], w_sgu[l], 0.0).astype(BF16)
        return dict(sgu_ln=_rows(sgu_ln_g[l], sgu_ln_b[l]), wtril=wt, wtril_t=jnp.swapaxes(wt, 1, 2),
                    bias_full=jnp.repeat(b_sgu[l].T, LANE, axis=1), cvec=_rows(cfm_conv_b[l], cfm_ln_g[l], cfm_ln_b[l]))

    def rest_of(g):
        return dict(w_a=g[0].reshape(1, D, D), w_b=g[1].reshape(1, D, D), w_c=g[2].reshape(1, D, D),
                    w_o=g[3].reshape(1, D, D), w_fi=jnp.transpose(g[4], (1, 0, 2)).reshape(1, D, F2),
                    w_fo=g[5].reshape(1, FF, D))

    consts = [layer_consts(l) for l in range(DEPTH)]
    sharded_small = ("w_short", "cfm_conv_w")

    def param_get(T):
        def get(name, l):
            if name == "final_g":
                return T[name]
            return None if name in sharded_small or name == "loss" else T[name][l]
        return get

    packs = [_pack(param_get(T), D) for T in (W, Mo, Vo)]
    ncr = DEPTH * (SHORT_K + CFM_K)
    padr = (-ncr) % 8
    convw_wmv = [jnp.pad(jnp.concatenate([T["w_short"].reshape(-1, ncs), T["cfm_conv_w"].reshape(-1, ncs)]),
                         ((0, padr), (0, 0))) for T in (W, Mo, Vo)]
    early_work = [ag_rest0["tok"], *packs, *convw_wmv] + [a for cl in consts for a in cl.values()]
    ag_in0 = _gather_mid(ag_in0, early_work, "ag_w_in0")
    g_in0 = _gather_finish(ag_in0, ag_in0["tok"], "ag_w_in0")
    w_short_full = jnp.transpose(g_in0[1], (1, 0, 2)).reshape(DEPTH, SHORT_K, D)
    cfm_w_full = jnp.transpose(g_in0[2], (1, 0, 2)).reshape(DEPTH, CFM_K, D)
    for l in range(DEPTH):
        consts[l]["wsh"] = jnp.pad(w_short_full[l], ((0, 8 - SHORT_K), (0, 0)))
        consts[l]["cw"] = jnp.pad(cfm_w_full[l], ((0, HALO - CFM_K), (0, 0)))
    Wg = [dict(w_in=g_in0[0]), None]
    ag_l1 = None
    nin = w_in.shape[2]
    tn_in = nin if nin % 256 == 0 and nin <= 1280 else 256
    tn_fi = 512 if F2 % 512 == 0 else 256
    tn_ffn = 1408 if F2 % 1408 == 0 else tn_fi
    tn_dw = min(256, D)

    saved = []
    xcur, fprev, gprev = x0, None, None
    for l in range(DEPTH):
        sh1, sc1, g1, sh2, sc2, g2 = [mod[l, k] for k in range(N_MOD)]
        cl = consts[l]
        if l == 0:
            xl, h, ht = xl0, h0, ht0
        else:
            vec1 = _rows(gprev, norm1_g[l], sc1, sh1)
            ag_l1 = _gather_mid(ag_l1, fprev, f"ag_w{l}")
            xl, h, ht = _norm_fwd(xcur, fprev, vec1, f"norm1_fwd{l}", deps=(ag_l1["tok"],))
            g = _gather_finish(ag_l1, h, f"ag_w{l}")
            Wg[l] = dict(w_in=g[0], **rest_of(g[1:]))
        wl = Wg[l]
        z = _mm_nn(h, wl["w_in"], BF16, tm_huge, tn_in, D, f"mm_in{l}", w_outer=True)
        mix_deps = ()
        if l == 0:
            ag_rest0 = _gather_mid(ag_rest0, z, "ag_rest0")
            mix_deps = (ag_rest0["tok"],)
            if DEPTH > 1:
                ag_l1 = _gather_start(shards_of(1), dev, "ag_w1", paired=(0,))
                mix_deps += (ag_l1["tok"],)
        acts, acts_t, conv = _mixer_fwd(z, cl["wsh"], cl["sgu_ln"], cl["wtril"], cl["bias_full"], cl["cw"], cl["cvec"],
                                        f"mixer_fwd{l}", deps=mix_deps)
        if l == 0:
            wl.update(rest_of(_gather_finish(ag_rest0, acts[0], "ag_rest0")))
        merged, merged_t, ys = _branch_out(acts, [wl["w_a"][0], wl["w_b"][0], wl["w_c"][0]], z, f"branch_out{l}")
        o = _mm_nn(merged, wl["w_o"], F32, tm_big, D, D, f"mm_o{l}")
        x1, h2, h2t = _norm_fwd(xl, o, _rows(g1, norm2_g[l], sc2, sh2), f"norm2_fwd{l}")
        gu, act, act_t = _ffn_in_swiglu(h2, wl["w_fi"], tm_huge, 256, f"mm_ffn_in{l}")
        f = _mm_nn(act, wl["w_fo"], F32, tm_big, D, FF, f"mm_ffn_out{l}")
        saved.append(dict(xl=xl, ht=ht, z=z, acts_t=acts_t, conv=conv, ys=ys, merged_t=merged_t, o=o, x1=x1, h2t=h2t, gu=gu,
                          act_t=act_t, f=f, consts=cl, mod=(sh1, sc1, g1, sh2, sc2, g2)))
        xcur, fprev, gprev = x1, f, g2

    last = saved[-1]
    dxup, dfb, fsums, loss_blk = _final_bwd(last["x1"], last["f"], tgt, _rows(last["mod"][5], final_g), "final_bwd")
    loss_row = jnp.pad(loss_blk[0, 0:1], (0, D - 1))
    dgate2_next = fsums[1]
    small = [dict() for _ in range(DEPTH)]
    dmods = [None] * DEPTH
    nfi = w_ffn_in.shape[2]
    early_names, late_names = ["w_ffn_out", "w_ffn_in", "w_o"], ["w_a_out", "w_b_out", "w_c_out", "w_in"]
    results = {n: None for n in early_names + late_names}

    def adam_group(names, Ps, R2s, l, deps=()):
        for n, p, r2 in zip(names, Ps, R2s):
            results[n] = _adam_big(p, r2, my_chip, W[n], Mo[n], Vo[n], l, results[n], f"adam_{n}{l}", deps)

    deferred = []
    late_prev = None
    ag_s1, gathered1 = None, None
    tk_w = min(2048, S)
    tn_dw_in = tn_in // 2 if tn_in == 1280 else tn_in
    for l in reversed(range(DEPTH)):
        sv, wl, cl = saved[l], Wg[l], saved[l]["consts"]
        sh1, sc1, g1, sh2, sc2, g2 = sv["mod"]
        dact = _mm_nt(dfb, wl["w_fo"], BF16, tm_big, FF, D, f"mm_dact{l}",
                      deps=() if late_prev is None else (late_prev["tok"], ag_s1["tok"]))
        g_fo = _mm_wgrad(sv["act_t"], dfb, 1, FF // 2, D, tk_w, f"mm_dw_ffn_out{l}")
        dgu = _swiglu_bwd(dact, sv["gu"], f"swiglu_bwd{l}")
        dh2 = _mm_nt(dgu, wl["w_fi"], F32, tm, D, F2, f"mm_dh2{l}")
        if late_prev is not None:
            deferred.append((late_names, *_scatter_finish(late_prev, dh2, f"rs_late{l + 1}"), l + 1))
            late_prev = None
        g_fi = _mm_wgrad(sv["h2t"], dgu, 1, D, tn_fi, S, f"mm_dw_ffn_in{l}")
        if ag_s1 is not None:
            ag_s1 = _gather_mid(ag_s1, g_fi, "ag_small1")
        dx1, dob, s2 = _norm_bwd(sv["x1"], dh2, dxup, _rows(norm2_g[l], sc2, g1), sv["o"], f"norm2_bwd{l}",
                                 deps=() if ag_s1 is None else (ag_s1["tok"],))
        dmerged = _mm_nt(dob, wl["w_o"], BF16, tm_big, D, D, f"mm_dmerged{l}")
        g_o = _mm_wgrad(sv["merged_t"], dob, 1, D, tn_dw, S, f"mm_dw_o{l}")
        early = _scatter_start([g_fo.reshape(NDEV, FF // NDEV, D),
                                jnp.transpose(g_fi.reshape(D, NDEV, nfi), (1, 0, 2)),
                                g_o.reshape(NDEV, D // NDEV, D)], f"rs_early{l}")
        dys, dz = _gate_bwd(dmerged, sv["z"], sv["ys"], f"gate_bwd{l}", deps=(early["tok"],))
        if ag_s1 is not None:
            gathered1 = _gather_finish(ag_s1, dys[0], "ag_small1")[0]
            ag_s1 = None
        early = _scatter_mid(early, dys[0], my_c, f"rs_early{l}")
        dacts, g_abc = [], []
        for n, key in enumerate(("w_a", "w_b", "w_c")):
            dacts.append(_mm_nt(dys[n], wl[key], BF16, tm_big, D, D, f"mm_dact_{key}{l}",
                                deps=(early["tok"],) if n == 0 else ()))
            g_abc.append(_mm_wgrad(sv["acts_t"][n], dys[n], 1, D, tn_dw, S, f"mm_d{key}{l}"))
        dz, mvec, dcw, dws, dbs = _mixer_bwd(sv["z"], dacts, sv["conv"], dz, cl["wsh"], cl["sgu_ln"], cl["wtril"],
                                             cl["wtril_t"], cl["bias_full"], cl["cw"], cl["cvec"], f"mixer_bwd{l}")
        dh = _mm_nt(dz, wl["w_in"], F32, tm_big, D, wl["w_in"].shape[2], f"mm_dh{l}")
        g_in = _mm_wgrad(sv["ht"], dz, NDEV, D, tn_dw_in, S, f"mm_dw_in{l}")
        late = _scatter_start([g.reshape(NDEV, D // NDEV, D) for g in g_abc] + [g_in], f"rs_late{l}")
        if l > 0:
            pv = saved[l - 1]
            dxup, dfb, s1 = _norm_bwd(sv["xl"], dh, dx1, _rows(norm1_g[l], sc1, pv["mod"][5]), pv["f"], f"norm1_bwd{l}",
                                      deps=(late["tok"],))
        else:
            dxup, dfb, s1 = _norm_bwd(sv["xl"], dh, dx1, _rows(norm1_g[l], sc1), None, f"norm1_bwd{l}", deps=(late["tok"],))
        deferred.append((early_names, *_scatter_finish(early, dxup, f"rs_early{l}"), l))
        dmods[l] = jnp.stack([s1[0], s1[1], s2[3], s2[0], s2[1], dgate2_next])
        dgate2_next = s1[3]
        small[l] = dict(norm1_g=s1[2], norm2_g=s2[2], sgu_ln_g=mvec[3], sgu_ln_b=mvec[4], cfm_conv_b=mvec[5],
                        cfm_ln_g=mvec[6], cfm_ln_b=mvec[7], b_sgu=dbs[:, :, 0],
                        w_sgu=jnp.where(tril[None], dws, 0.0), b_ada=dmods[l], w_short=mvec[0:SHORT_K],
                        cfm_conv_w=dcw[0:CFM_K])
        small_get = lambda name, k: {"final_g": fsums[0], "loss": loss_row}.get(name) if k is None else small[k][name]
        if l > 0:
            late_prev = _scatter_mid(late, dxup, my_c, f"rs_late{l}")
            ag_s1 = _gather_start([_pack(small_get, D, layers=(l,), tail=True)], dev, "ag_small1", deps=(late_prev["tok"],))
    grad_x = dxup.reshape(x.shape)

    gathered0 = _all_gather([_pack(small_get, D, layers=(0,), tail=False)], "ag_small0", deps=(dxup,))[0]
    late_prev = _scatter_mid(late, gathered0, my_c, "rs_late0")
    gathered = jnp.concatenate([gathered0, gathered1], axis=1)
    sg, sd, sm, sv_ = _adam_small(gathered, *packs, name="adam_small", deps=(late_prev["tok"],))
    loss = sg[FINAL_ROW + 1, 0]
    out = {}
    for name in order:
        if name in SMALL_ROWS and name not in sharded_small:
            out[name] = tuple(_unpack(p, name, W[name].shape) for p in (sg, sd, sm, sv_))
    out["final_g"] = tuple(p[FINAL_ROW] for p in (sg, sd, sm, sv_))

    def my_cols(name):
        full = _unpack(sg, name, (DEPTH, SMALL_ROWS[name][1], D))
        return lax.dynamic_slice_in_dim(full, dev * ncs, ncs, axis=2)

    gcs = jnp.concatenate([my_cols("w_short").reshape(-1, ncs), my_cols("cfm_conv_w").reshape(-1, ncs)])
    cd, cm, cv = _adam_plain(jnp.pad(gcs, ((0, padr), (0, 0))), *convw_wmv, "adam_convw")
    nsh = DEPTH * SHORT_K
    out["w_short"] = tuple(a[0:nsh].reshape(w_short.shape) for a in (gcs, cd, cm, cv))
    out["cfm_conv_w"] = tuple(a[nsh:ncr].reshape(cfm_conv_w.shape) for a in (gcs, cd, cm, cv))

    dm_all = jnp.stack([gathered[:, l * ROWS_PER_LAYER + 136:l * ROWS_PER_LAYER + 136 + N_MOD, :].reshape(NDEV, N_MOD * D)
                        for l in range(DEPTH)])
    dm_mine = lax.dynamic_slice_in_dim(dm_all, dev * ncol, ncol, axis=2)
    out["w_ada"] = tuple(_adam_ada(jnp.transpose(c_act), dm_mine, w_ada, m_w_ada, v_w_ada, "adam_ada"))

    for names, Ps, R2s, l in deferred:
        adam_group(names, Ps, R2s, l, deps=(late_prev["tok"],))
    adam_group(late_names, *_scatter_finish(late_prev, results["w_o"][0], "rs_late0"), 0)
    for n in early_names + late_names:
        out[n] = tuple(results[n])

    grads = [out[n][0] for n in order]
    deltas = [out[n][1] for n in order]
    new_m = [out[n][2] for n in order]
    new_v = [out[n][3] for n in order]
    return (loss, grad_x, *grads, *deltas, *new_m, *new_v)
```

```python
import functools
import math

import jax
import jax.numpy as jnp
from jax import lax
from jax.experimental import pallas as pl
from jax.experimental.pallas import tpu as pltpu

F32, BF16 = jnp.float32, jnp.bfloat16
NDEV = 8
NCHIP = NDEV // 2
DEPTH = 2
EPS = 1e-6
CHUNK = 128
NG = 8
SHORT_K = 3
CFM_K = 31
HALO = 32
N_MOD = 6
LANE = 128
VMEM_LIMIT = 56 * 1024 * 1024
ADAM_LR, ADAM_B1, ADAM_B2, ADAM_EPS, ADAM_WD, ADAM_STEP = 0.001, 0.9, 0.999, 1e-08, 0.01, 10
_G0 = math.sqrt(2.0 / math.pi)
_G1 = 0.044715
MESH = pl.DeviceIdType.MESH
ANY = pl.BlockSpec(memory_space=pl.ANY)


def _pcall(body, **kw):
    return pl.pallas_call(body, **kw)


def _params(sem=None):
    return pltpu.CompilerParams(dimension_semantics=sem, vmem_limit_bytes=VMEM_LIMIT)


def _sds(shape, dtype):
    return jax.ShapeDtypeStruct(tuple(shape), dtype)


def _mm_body(dims, nk, out_f32, blocks=1):
    def body(a_ref, b_ref, o_ref, *scr):
        k = pl.program_id(2)
        if blocks == 1:
            part = lax.dot_general(a_ref[...], b_ref[...], dims, preferred_element_type=F32)
        else:
            w = a_ref.shape[1] // blocks
            part = None
            for g in range(blocks):
                t = lax.dot_general(a_ref[:, g * w:(g + 1) * w], b_ref[g], dims, preferred_element_type=F32)
                part = t if part is None else part + t
        if nk == 1:
            o_ref[...] = part.reshape(o_ref.shape).astype(o_ref.dtype)
        elif out_f32:
            @pl.when(k == 0)
            def _():
                o_ref[...] = part.reshape(o_ref.shape)

            @pl.when(k > 0)
            def _():
                o_ref[...] += part.reshape(o_ref.shape)
        else:
            acc = scr[0]

            @pl.when(k == 0)
            def _():
                acc[...] = part

            @pl.when(k > 0)
            def _():
                acc[...] += part

            @pl.when(k == nk - 1)
            def _():
                o_ref[...] = acc[...].astype(o_ref.dtype)
    return body


def _after(body, n_in, deps):
    nd = len(deps)
    if nd == 0:
        return body

    def ordered(*refs):
        return body(*refs[:n_in], *refs[n_in + nd:])
    return ordered


def _mm_call(body, grid, in_specs, out_spec, out_shape, acc_shape, name, deps=()):
    scratch = [] if acc_shape is None else [pltpu.VMEM(acc_shape, F32)]
    return _pcall(_after(body, 2, deps), grid=grid, in_specs=in_specs + [ANY] * len(deps), out_specs=out_spec,
                  out_shape=out_shape, scratch_shapes=scratch, name=name,
                  compiler_params=_params(("parallel", "parallel", "arbitrary")))


def _mm_nn(a, b3, out_dtype, tm, tn, tk, name, w_outer=False, deps=()):
    M, K = a.shape
    G, _, Nb = b3.shape
    npb, nk = Nb // tn, K // tk
    out_f32 = out_dtype == F32
    body = _mm_body((((1,), (0,)), ((), ())), nk, out_f32)
    if w_outer:
        grid = (G * npb, M // tm, nk)
        ij = lambda p, q: (q, p)
    else:
        grid = (M // tm, G * npb, nk)
        ij = lambda p, q: (p, q)

    def a_map(p, q, k):
        i, j = ij(p, q)
        return (i, k)

    def b_map(p, q, k):
        i, j = ij(p, q)
        return (j // npb, k, j % npb)

    def o_map(p, q, k):
        return ij(p, q)

    def wrapped(a_ref, b_ref, o_ref, *scr):
        body(a_ref, b_ref, o_ref, *scr)

    return _mm_call(wrapped, grid, [pl.BlockSpec((tm, tk), a_map), pl.BlockSpec((None, tk, tn), b_map)],
                    pl.BlockSpec((tm, tn), o_map), _sds((M, G * Nb), out_dtype),
                    None if (nk == 1 or out_f32) else (tm, tn), name, deps)(a, b3, *deps)


def _mm_nt(a, b3, out_dtype, tm, tn, tk, name, deps=(), blocks_per_step=1):
    M, _ = a.shape
    G, Ko, Nb = b3.shape
    kpb = Nb // tk
    nk = G * kpb // blocks_per_step
    out_f32 = out_dtype == F32
    body = _mm_body((((1,), (1,)), ((), ())), nk, out_f32, blocks_per_step)

    def wrapped(a_ref, b_ref, o_ref, *scr):
        body(a_ref, b_ref, o_ref, *scr)

    if blocks_per_step > 1:
        assert tk == Nb and G % blocks_per_step == 0
        b_spec = pl.BlockSpec((blocks_per_step, tn, tk), lambda i, j, k: (k, j, 0))
    else:
        b_spec = pl.BlockSpec((None, tn, tk), lambda i, j, k: (k // kpb, j, k % kpb))
    return _mm_call(wrapped, (M // tm, Ko // tn, nk),
                    [pl.BlockSpec((tm, tk * blocks_per_step), lambda i, j, k: (i, k)), b_spec],
                    pl.BlockSpec((tm, tn), lambda i, j, k: (i, j)), _sds((M, Ko), out_dtype),
                    None if (nk == 1 or out_f32) else (tm, tn), name, deps)(a, b3, *deps)


def _mm_wgrad(at, b, G, tm, tn, tk, name, deps=()):
    M, T = at.shape
    Nb = b.shape[1] // G
    npb, nk = Nb // tn, T // tk
    body = _mm_body((((1,), (0,)), ((), ())), nk, False)

    def wrapped(a_ref, b_ref, o_ref, *scr):
        body(a_ref, b_ref, o_ref, *scr)

    a = at
    in_specs = [pl.BlockSpec((tm, tk), lambda i, j, k: (i, k)), pl.BlockSpec((tk, tn), lambda i, j, k: (k, j))]
    out_spec = pl.BlockSpec((None, tm, tn), lambda i, j, k: (j // npb, i, j % npb))
    return _mm_call(wrapped, (M // tm, G * npb, nk), in_specs, out_spec, _sds((G, M, Nb), BF16),
                    None if nk == 1 else (tm, tn), name, deps)(a, b, *deps)


def _rsum(v):
    return jnp.sum(v, axis=0, keepdims=True)


def _rmean(v):
    return jnp.mean(v, axis=-1, keepdims=True)


def _gelu(x):
    t = jnp.tanh(_G0 * (x + _G1 * (x * x * x)))
    return x * (0.5 * (1.0 + t)), t


def _dgelu(x, t):
    return 0.5 * (1.0 + t) + 0.5 * x * (1.0 - t * t) * (_G0 * (1.0 + 3.0 * _G1 * (x * x)))


def _sigmoid(x):
    return 1.0 / (1.0 + jnp.exp(-x))


def _fill_shifted(ext, rot):
    v = ext[...]
    n = v.shape[0]
    for b in range(1, 8):
        rot[b - 1] = pltpu.roll(v, n - b, 0)


def _rows_at(ext, rot, s, tm, cs=slice(None)):
    a, b = divmod(s, 8)
    return ext[8 * a:8 * a + tm, cs] if b == 0 else rot[b - 1, 8 * a:8 * a + tm, cs]


def _causal_conv(w_ref, taps, bias, ext, rot, offset, tm, out):
    D = out.shape[1]
    for cb in range(D // LANE):
        cs = slice(cb * LANE, (cb + 1) * LANE)
        acc = None
        for k, o in zip(taps, offset):
            term = w_ref[k:k + 1, cs] * _rows_at(ext, rot, o, tm, cs)
            acc = term if acc is None else acc + term
        out[:, cs] = acc if bias is None else acc + bias[:, cs]


def _rows(*vs):
    a = jnp.stack([v.astype(F32) for v in vs])
    return jnp.pad(a, ((0, 8 - len(vs)), (0, 0)))


def _row_spec(tm, D):
    return pl.BlockSpec((tm, D), lambda i: (i, 0))


def _const_spec(shape):
    nd = len(shape)
    return pl.BlockSpec(shape, lambda i: (0,) * nd)


def _norm_fwd(xp, f, vec, name, deps=()):
    S, D = xp.shape
    tm = min(256, S)
    has_f = f is not None

    def body(*refs):
        if has_f:
            xp_ref, f_ref, vec_ref, xo_ref, h_ref, ht_ref = refs
            x = xp_ref[...] + vec_ref[0:1, :] * f_ref[...]
            xo_ref[...] = x
        else:
            xp_ref, vec_ref, h_ref, ht_ref = refs
            x = xp_ref[...]
        r = lax.rsqrt(_rmean(x * x) + EPS)
        h = (x * r) * vec_ref[1:2, :]
        h = h * (1.0 + vec_ref[2:3, :]) + vec_ref[3:4, :]
        h_ref[...] = h.astype(BF16)
        ht_ref[...] = h.T.astype(BF16)

    rs = _row_spec(tm, D)
    ins = [xp, f, vec] if has_f else [xp, vec]
    in_specs = ([rs, rs] if has_f else [rs]) + [_const_spec((8, D))]
    out_shape = ([_sds((S, D), F32)] if has_f else []) + [_sds((S, D), BF16), _sds((D, S), BF16)]
    out_specs = [rs] * (len(out_shape) - 1) + [pl.BlockSpec((D, tm), lambda i: (0, i))]
    outs = _pcall(_after(body, len(ins), deps), grid=(S // tm,), in_specs=in_specs + [ANY] * len(deps),
                  out_specs=out_specs, out_shape=out_shape, name=name,
                  compiler_params=_params(("parallel",)))(*ins, *deps)
    return (outs[0], outs[1], outs[2]) if has_f else (xp, outs[0], outs[1])


def _mixer_fwd(z, wsh, sgu_ln, wtril, bias_full, cw, cvec, name, deps=()):
    S = z.shape[0]
    D = wsh.shape[1]
    tm = CHUNK

    def body(z_ref, wsh_ref, sln_ref, wt_ref, bias_ref, cw_ref, cv_ref, oa_ref, ob_ref, oc_ref, ta_ref, tb_ref, tc_ref,
             conv_ref, pe, ge, gr, cbuf):
        i = pl.program_id(0)

        @pl.when(i == 0)
        def _():
            pe[0:HALO, :] = jnp.zeros((HALO, D), F32)
            ge[0:HALO, :] = jnp.zeros((HALO, D), F32)

        def col(n):
            return z_ref[:, n * D:(n + 1) * D].astype(F32)

        pe[HALO:HALO + tm, :] = col(1) * col(2)
        q = wsh_ref[0:1, :] * pe[HALO - 2:HALO - 2 + tm, :]
        q = q + wsh_ref[1:2, :] * pe[HALO - 1:HALO - 1 + tm, :]
        q = q + wsh_ref[2:3, :] * pe[HALO:HALO + tm, :]
        act_a = col(0) * q
        oa_ref[...] = act_a.astype(BF16)
        ta_ref[...] = act_a.T.astype(BF16)
        gu, _ = _gelu(col(3))
        gv, _ = _gelu(col(4))
        d = gv - _rmean(gv)
        nrm = d * lax.rsqrt(_rmean(d * d) + EPS)
        vnb = (nrm * sln_ref[0:1, :] + sln_ref[1:2, :]).astype(BF16)
        for g in range(NG):
            cs = slice(g * LANE, (g + 1) * LANE)
            mixed = jnp.dot(wt_ref[g], vnb[:, cs], preferred_element_type=F32) + bias_ref[:, cs]
            act_b = gu[:, cs] * mixed
            ob_ref[:, cs] = act_b.astype(BF16)
            tb_ref[cs, :] = act_b.T.astype(BF16)
        ge[HALO:HALO + tm, :] = col(5) * _sigmoid(col(6))
        _fill_shifted(ge, gr)
        o0 = HALO - (CFM_K - 1)
        _causal_conv(cw_ref, range(CFM_K), cv_ref[0:1, :], ge, gr, range(o0, o0 + CFM_K), tm, cbuf)
        conv = cbuf[...]
        conv_ref[...] = conv.astype(BF16)
        d = conv - _rmean(conv)
        ln = (d * lax.rsqrt(_rmean(d * d) + EPS)) * cv_ref[1:2, :] + cv_ref[2:3, :]
        act_c = ln * _sigmoid(ln)
        oc_ref[...] = act_c.astype(BF16)
        tc_ref[...] = act_c.T.astype(BF16)
        pe[0:HALO, :] = pe[tm:tm + HALO, :]
        ge[0:HALO, :] = ge[tm:tm + HALO, :]

    rs = _row_spec(tm, D)
    outs = _pcall(
        _after(body, 7, deps), grid=(S // tm,),
        in_specs=[pl.BlockSpec((tm, 7 * D), lambda i: (i, 0)), _const_spec((8, D)), _const_spec((8, D)),
                  _const_spec((NG, CHUNK, CHUNK)), _const_spec((CHUNK, D)), _const_spec((HALO, D)), _const_spec((8, D))]
        + [ANY] * len(deps),
        out_specs=[rs, rs, rs] + [pl.BlockSpec((D, tm), lambda i: (0, i))] * 3 + [rs],
        out_shape=[_sds((S, D), BF16)] * 3 + [_sds((D, S), BF16)] * 3 + [_sds((S, D), BF16)],
        scratch_shapes=[pltpu.VMEM((HALO + tm, D), F32), pltpu.VMEM((HALO + tm, D), F32),
                        pltpu.VMEM((7, HALO + tm, D), F32), pltpu.VMEM((tm, D), F32)],
        name=name, compiler_params=_params(("arbitrary",)))(z, wsh, sgu_ln, wtril, bias_full, cw, cvec, *deps)
    return outs[:3], outs[3:6], outs[6]


def _branch_out(acts, ws, z, name):
    S, D = acts[0].shape
    tm = min(256, S)

    def body(a0, a1, a2, w0, w1, w2, g0, g1, g2, m_ref, mt_ref, y_ref):
        m = None
        for n, (a, w, g) in enumerate(((a0, w0, g0), (a1, w1, g1), (a2, w2, g2))):
            y = jnp.dot(a[...], w[...], preferred_element_type=F32)
            y_ref[n] = y.astype(BF16)
            t = _sigmoid(g[...].astype(F32)) * y
            m = t if m is None else m + t
        m_ref[...] = m.astype(BF16)
        mt_ref[...] = m.T.astype(BF16)

    rs = _row_spec(tm, D)
    gate_specs = [pl.BlockSpec((tm, D), functools.partial(lambda i, n: (i, 7 + n), n=n)) for n in range(3)]
    return _pcall(body, grid=(S // tm,),
                  in_specs=[rs, rs, rs] + [_const_spec((D, D))] * 3 + gate_specs,
                  out_specs=[rs, pl.BlockSpec((D, tm), lambda i: (0, i)), pl.BlockSpec((3, tm, D), lambda i: (0, i, 0))],
                  out_shape=[_sds((S, D), BF16), _sds((D, S), BF16), _sds((3, S, D), BF16)], name=name,
                  compiler_params=_params(("parallel",)))(*acts, *ws, z, z, z)


def _ffn_in_swiglu(h2, w3, tm, tn, name):
    S, D = h2.shape
    F = w3.shape[2] // 2
    nj = F // tn

    def body(a_ref, wg_ref, wu_ref, gu_ref, act_ref, actt_ref):
        a = a_ref[...]
        g = jnp.dot(a, wg_ref[...], preferred_element_type=F32)
        u = jnp.dot(a, wu_ref[...], preferred_element_type=F32)
        gu_ref[0] = g.astype(BF16)
        gu_ref[1] = u.astype(BF16)
        act = (g * _sigmoid(g)) * u
        act_ref[...] = act.astype(BF16)
        actt_ref[...] = act.T.astype(BF16)

    return _pcall(body, grid=(S // tm, nj),
                  in_specs=[pl.BlockSpec((tm, D), lambda i, j: (i, 0)), pl.BlockSpec((None, D, tn), lambda i, j: (0, 0, j)),
                            pl.BlockSpec((None, D, tn), lambda i, j: (0, 0, j + nj))],
                  out_specs=[pl.BlockSpec((2, tm, tn), lambda i, j: (0, i, j)), pl.BlockSpec((tm, tn), lambda i, j: (i, j)),
                             pl.BlockSpec((tn, tm), lambda i, j: (j, i))],
                  out_shape=[_sds((2, S, F), BF16), _sds((S, F), BF16), _sds((F, S), BF16)], name=name,
                  compiler_params=_params(("parallel", "parallel")))(h2, w3, w3)


def _swiglu_bwd(dact, gu, name):
    _, S, F = gu.shape
    F2 = 2 * F
    tm = min(256, S)

    def body(d_ref, g_ref, u_ref, o_ref):
        g = g_ref[...].astype(F32)
        sg = _sigmoid(g)
        d = d_ref[...].astype(F32)
        o_ref[:, 0:F] = (d * u_ref[...].astype(F32) * (sg * (1.0 + g * (1.0 - sg)))).astype(BF16)
        o_ref[:, F:2 * F] = (d * (g * sg)).astype(BF16)

    return _pcall(body, grid=(S // tm,),
                  in_specs=[pl.BlockSpec((tm, F), lambda i: (i, 0)), pl.BlockSpec((None, tm, F), lambda i: (0, i, 0)),
                            pl.BlockSpec((None, tm, F), lambda i: (1, i, 0))],
                  out_specs=pl.BlockSpec((tm, F2), lambda i: (i, 0)), out_shape=_sds((S, F2), BF16), name=name,
                  compiler_params=_params(("parallel",)))(dact, gu, gu)


def _final_bwd(x1, f, tgt, vec, name):
    S, D = x1.shape
    tm = min(256, S)

    def body(x_ref, f_ref, t_ref, vec_ref, dx_ref, df_ref, sums_ref, loss_ref):
        @pl.when(pl.program_id(0) == 0)
        def _():
            sums_ref[...] = jnp.zeros_like(sums_ref)
            loss_ref[...] = jnp.zeros_like(loss_ref)

        gate, fg = vec_ref[0:1, :], vec_ref[1:2, :]
        fv = f_ref[...]
        x = x_ref[...] + gate * fv
        r = lax.rsqrt(_rmean(x * x) + EPS)
        xn = x * r
        diff = xn * fg - t_ref[...]
        per_tok = _rmean(diff * diff)
        loss_ref[...] += 0.5 * jnp.sum(per_tok, axis=0, keepdims=True)
        dy = diff * (1.0 / D)
        sums_ref[0:1, :] += _rsum(dy * xn)
        dxn = dy * fg
        dx = r * (dxn - xn * _rmean(dxn * xn))
        sums_ref[1:2, :] += _rsum(dx * fv)
        dx_ref[...] = dx
        df_ref[...] = (dx * gate).astype(BF16)

    rs = _row_spec(tm, D)
    return _pcall(body, grid=(S // tm,), in_specs=[rs, rs, rs, _const_spec((8, D))],
                  out_specs=[rs, rs, _const_spec((8, D)), _const_spec((8, LANE))],
                  out_shape=[_sds((S, D), F32), _sds((S, D), BF16), _sds((8, D), F32), _sds((8, LANE), F32)],
                  name=name, compiler_params=_params(("arbitrary",)))(x1, f, tgt, vec)


def _norm_bwd(xin, dh, dxup, vec, fprev, name, deps=()):
    S, D = xin.shape
    tm = min(256, S)
    has_prev = fprev is not None

    def body(*refs):
        if has_prev:
            x_ref, dh_ref, up_ref, vec_ref, fp_ref, dx_ref, dp_ref, sums_ref = refs
        else:
            x_ref, dh_ref, up_ref, vec_ref, dx_ref, sums_ref = refs

        @pl.when(pl.program_id(0) == 0)
        def _():
            sums_ref[...] = jnp.zeros_like(sums_ref)

        g, scale = vec_ref[0:1, :], vec_ref[1:2, :]
        x = x_ref[...]
        r = lax.rsqrt(_rmean(x * x) + EPS)
        xn = x * r
        dhv = dh_ref[...]
        sums_ref[0:1, :] += _rsum(dhv)
        sums_ref[1:2, :] += _rsum(dhv * (xn * g))
        dm = dhv * (1.0 + scale)
        sums_ref[2:3, :] += _rsum(dm * xn)
        dxn = dm * g
        dx = up_ref[...] + r * (dxn - xn * _rmean(dxn * xn))
        dx_ref[...] = dx
        if has_prev:
            sums_ref[3:4, :] += _rsum(dx * fp_ref[...])
            dp_ref[...] = (dx * vec_ref[2:3, :]).astype(BF16)

    rs = _row_spec(tm, D)
    ins = [xin, dh, dxup, vec] + ([fprev] if has_prev else [])
    in_specs = [rs, rs, rs, _const_spec((8, D))] + ([rs] if has_prev else [])
    out_shape = [_sds((S, D), F32)] + ([_sds((S, D), BF16)] if has_prev else []) + [_sds((8, D), F32)]
    out_specs = [rs] + ([rs] if has_prev else []) + [_const_spec((8, D))]
    outs = _pcall(_after(body, len(ins), deps), grid=(S // tm,), in_specs=in_specs + [ANY] * len(deps),
                  out_specs=out_specs, out_shape=out_shape, name=name,
                  compiler_params=_params(("arbitrary",)))(*ins, *deps)
    return (outs[0], outs[1], outs[2]) if has_prev else (outs[0], None, outs[1])


def _gate_bwd(dmerged, z, ys, name, deps=()):
    S, D = dmerged.shape
    tm = min(512, S)
    ncol = z.shape[1] // D

    def body(dm_ref, g_ref, y_ref, dya_ref, dyb_ref, dyc_ref, dz_ref):
        n = pl.program_id(1)
        sg = _sigmoid(g_ref[...].astype(F32))
        dm = dm_ref[...].astype(F32)
        dy = (dm * sg).astype(BF16)
        for k, ref in enumerate((dya_ref, dyb_ref, dyc_ref)):
            @pl.when(n == k)
            def _(ref=ref):
                ref[...] = dy
        dz_ref[...] = (dm * y_ref[...].astype(F32) * (sg * (1.0 - sg))).astype(BF16)

    row = pl.BlockSpec((tm, D), lambda i, n: (i, 0))
    outs = _pcall(_after(body, 3, deps), grid=(S // tm, 3),
                  in_specs=[row, pl.BlockSpec((tm, D), lambda i, n: (i, 7 + n)),
                            pl.BlockSpec((None, tm, D), lambda i, n: (n, i, 0))] + [ANY] * len(deps),
                  out_specs=[row, row, row, pl.BlockSpec((tm, D), lambda i, n: (i, 7 + n))],
                  out_shape=[_sds((S, D), BF16)] * 3 + [_sds((S, ncol * D), BF16)], name=name,
                  compiler_params=_params(("parallel", "arbitrary")))(dmerged, z, ys, *deps)
    return outs[:3], outs[3]


def _mixer_bwd(z, dacts, conv, dz, wsh, sgu_ln, wtril, wtril_t, bias_full, cw, cvec, name):
    S = z.shape[0]
    D = wsh.shape[1]
    tm = CHUNK
    nt = S // tm
    hb = tm // HALO

    def body(zc, zp, da_ref, db_ref, dc_ref, conv_ref, wsh_ref, sln_ref, wt_ref, wtt_ref, bias_ref, cw_ref, cv_ref, _dz_in,
             dz_ref, vec_ref, dcw_ref, dws_ref, dbs_ref, pe, ge, dqe, dce, gr, dcr, cbuf, dcw8):
        i = pl.program_id(0)
        rb = nt - 1 - i

        @pl.when(i == 0)
        def _():
            vec_ref[...] = jnp.zeros_like(vec_ref)
            dcw8[...] = jnp.zeros_like(dcw8)
            dws_ref[...] = jnp.zeros_like(dws_ref)
            dbs_ref[...] = jnp.zeros_like(dbs_ref)
            dqe[tm:tm + HALO, :] = jnp.zeros((HALO, D), F32)
            dce[tm:tm + HALO, :] = jnp.zeros((HALO, D), F32)

        keep = (rb > 0).astype(F32)

        def col(n):
            return zc[:, n * D:(n + 1) * D].astype(F32)

        def pcol(n):
            return zp[:, n * D:(n + 1) * D].astype(F32)

        c_a, x_a = col(1), col(2)
        pe[0:HALO, :] = keep * (pcol(1) * pcol(2))
        pe[HALO:HALO + tm, :] = c_a * x_a
        q = wsh_ref[0:1, :] * pe[HALO - 2:HALO - 2 + tm, :]
        q = q + wsh_ref[1:2, :] * pe[HALO - 1:HALO - 1 + tm, :]
        q = q + wsh_ref[2:3, :] * pe[HALO:HALO + tm, :]
        dact = da_ref[...].astype(F32)
        dz_ref[:, 0:D] = (dact * q).astype(BF16)
        dq = dact * col(0)
        dqe[0:tm, :] = dq
        dp = wsh_ref[2:3, :] * dq + wsh_ref[1:2, :] * dqe[1:1 + tm, :] + wsh_ref[0:1, :] * dqe[2:2 + tm, :]
        dz_ref[:, D:2 * D] = (dp * x_a).astype(BF16)
        dz_ref[:, 2 * D:3 * D] = (dp * c_a).astype(BF16)
        for k in range(SHORT_K):
            o = HALO - (SHORT_K - 1) + k
            vec_ref[k:k + 1, :] += _rsum(dq * pe[o:o + tm, :])
        u, v = col(3), col(4)
        gu, tu = _gelu(u)
        gv, tv = _gelu(v)
        d = gv - _rmean(gv)
        rstd = lax.rsqrt(_rmean(d * d) + EPS)
        nrm = d * rstd
        vnb = (nrm * sln_ref[0:1, :] + sln_ref[1:2, :]).astype(BF16)
        dact = db_ref[...].astype(F32)
        dvn_parts, dgu_parts = [], []
        for g in range(NG):
            cs = slice(g * LANE, (g + 1) * LANE)
            vg = vnb[:, cs]
            mixed = jnp.dot(wt_ref[g], vg, preferred_element_type=F32) + bias_ref[:, cs]
            dgu_parts.append(dact[:, cs] * mixed)
            dmixed = dact[:, cs] * gu[:, cs]
            dmb = dmixed.astype(BF16)
            dws_ref[g] += lax.dot_general(dmb, vg, (((1,), (1,)), ((), ())), preferred_element_type=F32)
            dbs_ref[g] += jnp.broadcast_to(jnp.sum(dmixed, axis=1, keepdims=True), (CHUNK, LANE))
            dvn_parts.append(jnp.dot(wtt_ref[g], dmb, preferred_element_type=F32))
        dgu = jnp.concatenate(dgu_parts, axis=1)
        dvn = jnp.concatenate(dvn_parts, axis=1)
        dz_ref[:, 3 * D:4 * D] = (dgu * _dgelu(u, tu)).astype(BF16)
        vec_ref[3:4, :] += _rsum(dvn * nrm)
        vec_ref[4:5, :] += _rsum(dvn)
        dn = dvn * sln_ref[0:1, :]
        dgv = rstd * (dn - _rmean(dn) - nrm * _rmean(dn * nrm))
        dz_ref[:, 4 * D:5 * D] = (dgv * _dgelu(v, tv)).astype(BF16)
        a_c = col(5)
        sg = _sigmoid(col(6))
        ge[0:HALO, :] = keep * (pcol(5) * _sigmoid(pcol(6)))
        ge[HALO:HALO + tm, :] = a_c * sg
        _fill_shifted(ge, gr)
        o0 = HALO - (CFM_K - 1)
        conv = conv_ref[...].astype(F32)
        d = conv - _rmean(conv)
        rstd = lax.rsqrt(_rmean(d * d) + EPS)
        nrm = d * rstd
        ln = nrm * cv_ref[1:2, :] + cv_ref[2:3, :]
        sl = _sigmoid(ln)
        dln = dc_ref[...].astype(F32) * (sl * (1.0 + ln * (1.0 - sl)))
        vec_ref[6:7, :] += _rsum(dln * nrm)
        vec_ref[7:8, :] += _rsum(dln)
        dn = dln * cv_ref[1:2, :]
        dconv = rstd * (dn - _rmean(dn) - nrm * _rmean(dn * nrm))
        vec_ref[5:6, :] += _rsum(dconv)
        dce[0:tm, :] = dconv
        _fill_shifted(dce, dcr)
        _causal_conv(cw_ref, range(CFM_K), None, dce, dcr, [CFM_K - 1 - k for k in range(CFM_K)], tm, cbuf)
        dglu = cbuf[...]
        for cb in range(D // LANE):
            cs = slice(cb * LANE, (cb + 1) * LANE)
            dcv = dce[0:tm, cs]
            for k in range(CFM_K):
                prod = dcv * _rows_at(ge, gr, o0 + k, tm, cs)
                dcw8[k, :, cs] += jnp.sum(prod.reshape(tm // 8, 8, LANE), axis=0)

        @pl.when(i == nt - 1)
        def _():
            dcw_ref[...] = jnp.sum(dcw8[...], axis=1)
        dz_ref[:, 5 * D:6 * D] = (dglu * sg).astype(BF16)
        dz_ref[:, 6 * D:7 * D] = (dglu * a_c * (sg * (1.0 - sg))).astype(BF16)
        dqe[tm:tm + HALO, :] = dqe[0:HALO, :]
        dce[tm:tm + HALO, :] = dce[0:HALO, :]

    rev = lambda i: (nt - 1 - i, 0)
    rs = pl.BlockSpec((tm, D), rev)
    cur = pl.BlockSpec((tm, 7 * D), rev)
    prev = pl.BlockSpec((HALO, 7 * D), lambda i: (jnp.maximum((nt - 1 - i) * hb - 1, 0), 0))
    ext = pltpu.VMEM((HALO + tm, D), F32)
    outs = _pcall(
        body, grid=(nt,),
        in_specs=[cur, prev, rs, rs, rs, rs, _const_spec((8, D)), _const_spec((8, D)), _const_spec((NG, CHUNK, CHUNK)),
                  _const_spec((NG, CHUNK, CHUNK)), _const_spec((CHUNK, D)), _const_spec((HALO, D)), _const_spec((8, D)),
                  ANY],
        out_specs=[cur, _const_spec((8, D)), _const_spec((HALO, D)), _const_spec((NG, CHUNK, CHUNK)),
                   _const_spec((NG, CHUNK, LANE))],
        out_shape=[_sds(dz.shape, BF16), _sds((8, D), F32), _sds((HALO, D), F32), _sds((NG, CHUNK, CHUNK), F32),
                   _sds((NG, CHUNK, LANE), F32)],
        scratch_shapes=[ext, ext, ext, ext, pltpu.VMEM((7, HALO + tm, D), F32), pltpu.VMEM((7, HALO + tm, D), F32),
                        pltpu.VMEM((tm, D), F32), pltpu.VMEM((HALO, 8, D), F32)],
        input_output_aliases={13: 0}, name=name,
        compiler_params=_params(("arbitrary",)))(z, z, *dacts, conv, wsh, sgu_ln, wtril, wtril_t, bias_full, cw, cvec, dz)
    return outs


def _ada_fwd(c_all, w_ada_loc, name):
    nb, D = c_all.shape
    L, _, nc = w_ada_loc.shape

    def body(c_ref, w_ref, o_ref, ca_ref):
        cv = c_ref[...]
        ca = cv * _sigmoid(cv)
        ca_ref[...] = ca
        o_ref[...] = jnp.dot(ca.astype(BF16), w_ref[...].astype(BF16), preferred_element_type=F32)

    return _pcall(body, grid=(L,),
                  in_specs=[_const_spec((nb, D)), pl.BlockSpec((None, D, nc), lambda l: (l, 0, 0))],
                  out_specs=[pl.BlockSpec((None, nb, nc), lambda l: (l, 0, 0)), _const_spec((nb, D))],
                  out_shape=[_sds((L, nb, nc), F32), _sds((nb, D), F32)], name=name,
                  compiler_params=_params(("arbitrary",)))(c_all, w_ada_loc)


def _adamw(w, g, m, v):
    m = ADAM_B1 * m + (1.0 - ADAM_B1) * g
    v = ADAM_B2 * v + (1.0 - ADAM_B2) * (g * g)
    m_hat = m / (1.0 - ADAM_B1 ** ADAM_STEP)
    v_hat = v / (1.0 - ADAM_B2 ** ADAM_STEP)
    delta = -ADAM_LR * (m_hat / (jnp.sqrt(v_hat) + ADAM_EPS) + ADAM_WD * w)
    return delta, m, v


def _tile_rows(R, C, align=8):
    cap = max(align, (1536 * 1024) // (4 * C))
    best = None
    for t in range(align, R + 1, align):
        if R % t == 0 and t <= cap:
            best = t
    return R if best is None else best


def _adam_ada(ct, dm, w, m, v, name):
    L, D, nc = w.shape
    nb = ct.shape[1]
    tr = _tile_rows(D, nc)

    def body(ct_ref, dm_ref, w_ref, m_ref, v_ref, g_ref, d_ref, mo_ref, vo_ref):
        g = ct_ref[:, 0:1] * dm_ref[0:1, :]
        for b in range(1, nb):
            g = g + ct_ref[:, b:b + 1] * dm_ref[b:b + 1, :]
        g_ref[...] = g
        d_ref[...], mo_ref[...], vo_ref[...] = _adamw(w_ref[...], g, m_ref[...], v_ref[...])

    ws = pl.BlockSpec((None, tr, nc), lambda l, r: (l, r, 0))
    return _pcall(body, grid=(L, D // tr),
                  in_specs=[pl.BlockSpec((tr, nb), lambda l, r: (r, 0)), pl.BlockSpec((None, nb, nc), lambda l, r: (l, 0, 0)),
                            ws, ws, ws],
                  out_specs=[ws] * 4, out_shape=[_sds(w.shape, F32)] * 4, name=name,
                  compiler_params=_params(("parallel", "parallel")))(ct, dm, w, m, v)


def _adam_small(parts, w, m, v, name, deps=()):
    n, R, C = parts.shape
    tr = _tile_rows(R, C * n // 2)

    def body(p_ref, w_ref, m_ref, v_ref, g_ref, d_ref, mo_ref, vo_ref):
        g = p_ref[0]
        for j in range(1, n):
            g = g + p_ref[j]
        g_ref[...] = g
        d_ref[...], mo_ref[...], vo_ref[...] = _adamw(w_ref[...], g, m_ref[...], v_ref[...])

    ws = pl.BlockSpec((tr, C), lambda r: (r, 0))
    return _pcall(_after(body, 4, deps), grid=(R // tr,),
                  in_specs=[pl.BlockSpec((n, tr, C), lambda r: (0, r, 0)), ws, ws, ws] + [ANY] * len(deps),
                  out_specs=[ws] * 4, out_shape=[_sds((R, C), F32)] * 4, name=name,
                  compiler_params=_params(("parallel",)))(parts, w, m, v, *deps)


def _adam_plain(g, w, m, v, name):
    R, C = w.shape

    def body(g_ref, w_ref, m_ref, v_ref, d_ref, mo_ref, vo_ref):
        d_ref[...], mo_ref[...], vo_ref[...] = _adamw(w_ref[...], g_ref[...], m_ref[...], v_ref[...])

    ws = _const_spec((R, C))
    return _pcall(body, grid=(1,), in_specs=[ws] * 4, out_specs=[ws] * 3, out_shape=[_sds((R, C), F32)] * 3, name=name,
                  compiler_params=_params(("arbitrary",)))(g, w, m, v)


def _pair_sum(G, R1, my_c, name):
    n, R, C = G.shape
    half = n // 2
    tr = _tile_rows(R, C, align=16)

    def body(c_ref, g_ref, r_ref, o_ref):
        o_ref[...] = (g_ref[...].astype(F32) + r_ref[...].astype(F32)).astype(o_ref.dtype)

    blk = (None, tr, C)
    gs = pltpu.PrefetchScalarGridSpec(
        num_scalar_prefetch=1, grid=(half, R // tr),
        in_specs=[pl.BlockSpec(blk, lambda p, r, c: (2 * p + c[0], r, 0)), pl.BlockSpec(blk, lambda p, r, c: (p, r, 0))],
        out_specs=pl.BlockSpec(blk, lambda p, r, c: (p, r, 0)))
    return _pcall(body, grid_spec=gs, out_shape=_sds((half, R, C), G.dtype), name=name,
                  compiler_params=_params(("parallel", "parallel")))(my_c, G, R1)


def _adam_big(P, R2, my_chip, w, m, v, layer, prev, name, deps=()):
    _, R, C = P.shape
    nrecv = R2.shape[0]
    tr = _tile_rows(R, C, align=16)

    def body(p_sm, p_ref, r_ref, w_ref, m_ref, v_ref, *rest):
        g_ref, d_ref, mo_ref, vo_ref = rest[-4:]
        g = p_ref[...].astype(F32)
        for k in range(nrecv):
            g = g + r_ref[k].astype(F32)
        g_ref[...] = g
        d_ref[...], mo_ref[...], vo_ref[...] = _adamw(w_ref[...], g, m_ref[...], v_ref[...])

    ws = pl.BlockSpec((None, tr, C), lambda r, p: (layer, r, 0))
    held = [] if prev is None else list(prev)
    gs = pltpu.PrefetchScalarGridSpec(
        num_scalar_prefetch=1, grid=(R // tr,),
        in_specs=[pl.BlockSpec((None, tr, C), lambda r, p: (p[0], r, 0)),
                  pl.BlockSpec((nrecv, tr, C), lambda r, p: (0, r, 0)), ws, ws, ws] + [ANY] * (len(held) + len(deps)),
        out_specs=[ws] * 4)
    alias = {6 + i: i for i in range(len(held))}
    return _pcall(body, grid_spec=gs, out_shape=[_sds(w.shape, F32)] * 4, name=name, input_output_aliases=alias,
                  compiler_params=_params(("parallel",)))(my_chip, P, R2, w, m, v, *held, *deps)


def _place():
    return lax.axis_index("x"), lax.axis_index("y"), lax.axis_index("c")


def _all_gather(shards, name, deps=()):
    n = len(shards)

    def body(*refs):
        ins, outs = refs[:n], refs[n:2 * n]
        send_sems, recv_sems, local_sems = refs[2 * n:]
        x, y, c = _place()
        me, sibling = (x, y, c), (x, y, 1 - c)
        chips = [(1 - x, y), (x, 1 - y), (1 - x, 1 - y)]

        def slot(a, px, py, pc):
            return outs[a].at[4 * px + 2 * py + pc]

        def copy(a, k, block, to, src=None):
            return pltpu.make_async_remote_copy(
                src_ref=slot(a, *block) if src is None else src, dst_ref=slot(a, *block),
                send_sem=send_sems.at[7 * a + k], recv_sem=recv_sems.at[7 * a + k], device_id=to, device_id_type=MESH)

        mine = [pltpu.make_async_copy(ins[a], slot(a, *me), local_sems.at[a]) for a in range(n)]
        for cp in mine:
            cp.start()
        first = []
        for a in range(n):
            first.append(copy(a, 0, me, sibling, src=ins[a]))
            first += [copy(a, 1 + j, me, (*chip, c), src=ins[a]) for j, chip in enumerate(chips)]
        for cp in first:
            cp.start()
        passed = []
        for j, chip in enumerate(chips):
            for a in range(n):
                copy(a, 1 + j, (*chip, c), me).wait_recv()
                fwd = copy(a, 4 + j, (*chip, c), sibling)
                fwd.start()
                passed.append(fwd)
        for a in range(n):
            copy(a, 0, sibling, me).wait_recv()
        for j, chip in enumerate(chips):
            for a in range(n):
                copy(a, 4 + j, (*chip, 1 - c), me).wait_recv()
        for cp in first + passed:
            cp.wait_send()
        for cp in mine:
            cp.wait()

    outs = _pcall(_after(body, n, deps), in_specs=[ANY] * (n + len(deps)), out_specs=[ANY] * n,
                  out_shape=[_sds((NDEV,) + s.shape, s.dtype) for s in shards],
                  scratch_shapes=[pltpu.SemaphoreType.DMA((7 * n,)), pltpu.SemaphoreType.DMA((7 * n,)),
                                  pltpu.SemaphoreType.DMA((n,))], name=name)(*shards, *deps)
    return list(outs)


HBM = pl.BlockSpec(memory_space=pltpu.HBM)
SEM = pl.BlockSpec(memory_space=pltpu.SEMAPHORE)


def _copies(plan, refs, send_sems, recv_sems):
    return [pltpu.make_async_remote_copy(src_ref=s, dst_ref=d, send_sem=send_sems.at[k], recv_sem=recv_sems.at[k],
                                         device_id=dev, device_id_type=MESH)
            for k, (s, d, dev) in enumerate(plan(refs, *_place()))]


def _xfer_start(bufs, ncopies, plan, name, deps=()):
    n = len(bufs)

    def body(*refs):
        for cp in _copies(plan, refs[:n], refs[n], refs[n + 1]):
            cp.start()
        token = refs[2 * n + 2]
        token[...] = jnp.zeros_like(token)

    outs = _pcall(
        _after(body, n, deps), name=name,
        out_shape=(pltpu.SemaphoreType.DMA((ncopies,)), pltpu.SemaphoreType.DMA((ncopies,)),
                   *[pltpu.HBM(b.shape, b.dtype) for b in bufs], _sds((8, LANE), F32)),
        in_specs=[HBM] * n + [ANY] * len(deps),
        out_specs=(SEM, SEM, *[HBM] * n, pl.BlockSpec(memory_space=pltpu.VMEM)),
        input_output_aliases={i: 2 + i for i in range(n)},
        compiler_params=pltpu.CompilerParams(has_side_effects=pltpu.SideEffectType.DATAFLOW_SIDE_EFFECTING),
    )(*[pltpu.with_memory_space_constraint(b, pltpu.HBM) for b in bufs], *deps)
    return (outs[0], outs[1]), list(outs[2:2 + n]), outs[2 + n]


def _xfer_wait(sems, bufs, plan, after, name):
    n = len(bufs)
    after = list(after) if isinstance(after, (list, tuple)) else [after]

    def body(*refs):
        for cp in _copies(plan, refs[:n], refs[n], refs[n + 1]):
            cp.wait_send()
            cp.wait_recv()

    outs = _pcall(
        body, name=name, out_shape=tuple(pltpu.HBM(b.shape, b.dtype) for b in bufs),
        in_specs=[HBM] * n + [SEM, SEM] + [ANY] * len(after), out_specs=tuple([HBM] * n),
        input_output_aliases={i: i for i in range(n)},
        compiler_params=pltpu.CompilerParams(has_side_effects=pltpu.SideEffectType.DATAFLOW_SIDE_EFFECTING),
    )(*bufs, *sems, *after)
    return list(outs)


def _chips_of(x, y):
    return [(1 - x, y), (x, 1 - y), (1 - x, 1 - y)]


def _gather_plan1(n):
    def plan(refs, x, y, c):
        out = []
        for a in range(n):
            blk = refs[a].at[4 * x + 2 * y + c]
            out.append((blk, blk, (x, y, 1 - c)))
            out += [(blk, blk, (px, py, c)) for px, py in _chips_of(x, y)]
        return out
    return plan


def _gather_plan2(n):
    def plan(refs, x, y, c):
        out = []
        for a in range(n):
            for px, py in _chips_of(x, y):
                blk = refs[a].at[4 * px + 2 * py + c]
                out.append((blk, blk, (x, y, 1 - c)))
        return out
    return plan


def _gather_start(shards, dev, name, deps=()):
    lands = [lax.dynamic_update_slice(lax.empty((NDEV,) + s.shape, s.dtype), s[None], (dev,) + (0,) * s.ndim)
             for s in shards]
    n = len(shards)
    sems, lands, tok = _xfer_start(lands, 4 * n, _gather_plan1(n), name + "_p1_start", deps)
    return dict(sems=sems, lands=lands, tok=tok, n=n)


def _gather_mid(st, after, name):
    n = st["n"]
    lands = _xfer_wait(st["sems"], st["lands"], _gather_plan1(n), after, name + "_p1_wait")
    sems, lands, tok = _xfer_start(lands, 3 * n, _gather_plan2(n), name + "_p2_start")
    return dict(sems=sems, lands=lands, tok=tok, n=n)


def _gather_finish(st, after, name):
    return _xfer_wait(st["sems"], st["lands"], _gather_plan2(st["n"]), after, name + "_p2_wait")


def _scatter_plan1(n):
    def plan(refs, x, y, c):
        return [(refs[a].at[2 * p + 1 - c], refs[n + a].at[p], (x, y, 1 - c)) for a in range(n) for p in range(NCHIP)]
    return plan


def _scatter_plan2(n):
    def plan(refs, x, y, c):
        return [(refs[a].at[2 * px + py], refs[n + a].at[j], (px, py, c))
                for a in range(n) for j, (px, py) in enumerate(_chips_of(x, y))]
    return plan


def _scatter_start(Gs, name):
    n = len(Gs)
    R1s = [lax.empty((NCHIP,) + g.shape[1:], g.dtype) for g in Gs]
    sems, bufs, tok = _xfer_start(list(Gs) + R1s, NCHIP * n, _scatter_plan1(n), name + "_s1_start")
    return dict(sems=sems, bufs=bufs, tok=tok, n=n)


def _scatter_mid(st, after, my_c, name):
    n = st["n"]
    bufs = _xfer_wait(st["sems"], st["bufs"], _scatter_plan1(n), after, name + "_s1_wait")
    Ps = [_pair_sum(bufs[a], bufs[n + a], my_c, f"{name}_pair_sum{a}") for a in range(n)]
    R2s = [lax.empty((3,) + p.shape[1:], p.dtype) for p in Ps]
    sems, bufs, tok = _xfer_start(Ps + R2s, 3 * n, _scatter_plan2(n), name + "_s2_start")
    return dict(sems=sems, bufs=bufs, tok=tok, n=n)


def _scatter_finish(st, after, name):
    n = st["n"]
    bufs = _xfer_wait(st["sems"], st["bufs"], _scatter_plan2(n), after, name + "_s2_wait")
    return bufs[:n], bufs[n:]


SMALL_ROWS = {"norm1_g": (0, 1), "norm2_g": (1, 1), "sgu_ln_g": (2, 1), "sgu_ln_b": (3, 1), "cfm_conv_b": (4, 1),
              "cfm_ln_g": (5, 1), "cfm_ln_b": (6, 1), "b_sgu": (7, 1), "w_sgu": (8, 128), "b_ada": (136, N_MOD),
              "w_short": (142, SHORT_K), "cfm_conv_w": (145, CFM_K)}
ROWS_PER_LAYER = 176
FINAL_ROW = DEPTH * ROWS_PER_LAYER
PACK_ROWS = 360


def _pack(get, D, layers=tuple(range(DEPTH)), tail=True):
    parts = []
    for l in layers:
        for name, (_, nrows) in SMALL_ROWS.items():
            a = get(name, l)
            parts.append(jnp.zeros((nrows * D,), F32) if a is None else a.astype(F32).reshape(nrows * D))
    if tail:
        for name in ("final_g", "loss"):
            a = get(name, None)
            parts.append(jnp.zeros((D,), F32) if a is None else a.astype(F32).reshape(D))
        parts.append(jnp.zeros(((PACK_ROWS - FINAL_ROW - 2) * D,), F32))
    return jnp.concatenate(parts).reshape(-1, D)


def _unpack(pack, name, shape):
    D = pack.shape[1]
    r0, nrows = SMALL_ROWS[name]
    return jnp.stack([pack[l * ROWS_PER_LAYER + r0:l * ROWS_PER_LAYER + r0 + nrows] for l in range(DEPTH)]).reshape(shape)


def _mm_tiles(S):
    return min(512, S), min(1024, S)


def kernel(x, c, w_ada, b_ada, norm1_g, w_in, w_short, w_a_out, sgu_ln_g, sgu_ln_b, w_sgu, b_sgu, w_b_out, cfm_conv_w, cfm_conv_b, cfm_ln_g, cfm_ln_b, w_c_out, w_o, norm2_g, w_ffn_in, w_ffn_out, final_g, loss_target, m_w_ada, m_b_ada, m_norm1_g, m_w_in, m_w_short, m_w_a_out, m_sgu_ln_g, m_sgu_ln_b, m_w_sgu, m_b_sgu, m_w_b_out, m_cfm_conv_w, m_cfm_conv_b, m_cfm_ln_g, m_cfm_ln_b, m_w_c_out, m_w_o, m_norm2_g, m_w_ffn_in, m_w_ffn_out, m_final_g, v_w_ada, v_b_ada, v_norm1_g, v_w_in, v_w_short, v_w_a_out, v_sgu_ln_g, v_sgu_ln_b, v_w_sgu, v_b_sgu, v_w_b_out, v_cfm_conv_w, v_cfm_conv_b, v_cfm_ln_g, v_cfm_ln_b, v_w_c_out, v_w_o, v_norm2_g, v_w_ffn_in, v_w_ffn_out, v_final_g):
    W = dict(w_ada=w_ada, b_ada=b_ada, norm1_g=norm1_g, w_in=w_in, w_short=w_short, w_a_out=w_a_out, sgu_ln_g=sgu_ln_g,
             sgu_ln_b=sgu_ln_b, w_sgu=w_sgu, b_sgu=b_sgu, w_b_out=w_b_out, cfm_conv_w=cfm_conv_w, cfm_conv_b=cfm_conv_b,
             cfm_ln_g=cfm_ln_g, cfm_ln_b=cfm_ln_b, w_c_out=w_c_out, w_o=w_o, norm2_g=norm2_g, w_ffn_in=w_ffn_in,
             w_ffn_out=w_ffn_out, final_g=final_g)
    Mo = dict(w_ada=m_w_ada, b_ada=m_b_ada, norm1_g=m_norm1_g, w_in=m_w_in, w_short=m_w_short, w_a_out=m_w_a_out,
              sgu_ln_g=m_sgu_ln_g, sgu_ln_b=m_sgu_ln_b, w_sgu=m_w_sgu, b_sgu=m_b_sgu, w_b_out=m_w_b_out,
              cfm_conv_w=m_cfm_conv_w, cfm_conv_b=m_cfm_conv_b, cfm_ln_g=m_cfm_ln_g, cfm_ln_b=m_cfm_ln_b,
              w_c_out=m_w_c_out, w_o=m_w_o, norm2_g=m_norm2_g, w_ffn_in=m_w_ffn_in, w_ffn_out=m_w_ffn_out,
              final_g=m_final_g)
    Vo = dict(w_ada=v_w_ada, b_ada=v_b_ada, norm1_g=v_norm1_g, w_in=v_w_in, w_short=v_w_short, w_a_out=v_w_a_out,
              sgu_ln_g=v_sgu_ln_g, sgu_ln_b=v_sgu_ln_b, w_sgu=v_w_sgu, b_sgu=v_b_sgu, w_b_out=v_w_b_out,
              cfm_conv_w=v_cfm_conv_w, cfm_conv_b=v_cfm_conv_b, cfm_ln_g=v_cfm_ln_g, cfm_ln_b=v_cfm_ln_b,
              w_c_out=v_w_c_out, w_o=v_w_o, norm2_g=v_norm2_g, w_ffn_in=v_w_ffn_in, w_ffn_out=v_w_ffn_out,
              final_g=v_final_g)
    order = ["w_ada", "b_ada", "norm1_g", "w_in", "w_short", "w_a_out", "sgu_ln_g", "sgu_ln_b", "w_sgu", "b_sgu",
             "w_b_out", "cfm_conv_w", "cfm_conv_b", "cfm_ln_g", "cfm_ln_b", "w_c_out", "w_o", "norm2_g", "w_ffn_in",
             "w_ffn_out", "final_g"]

    assert DEPTH == 2, "the weight-gather schedule below is written for two layers"
    S, D = x.shape[1], x.shape[2]
    F2 = w_ffn_in.shape[2] * NDEV
    FF = F2 // 2
    xi, yi, ci = _place()
    dev = 4 * xi + 2 * yi + ci
    my_c = jnp.reshape(ci, (1,)).astype(jnp.int32)
    my_chip = jnp.reshape(2 * xi + yi, (1,)).astype(jnp.int32)
    tm, tm_big = _mm_tiles(S)
    tm_huge = min(2048, S)
    x0 = x.reshape(S, D)
    tgt = loss_target.reshape(S, D)

    def shards_of(l):
        return [w_in[l].astype(BF16), w_a_out[l].astype(BF16), w_b_out[l].astype(BF16), w_c_out[l].astype(BF16),
                w_o[l].astype(BF16), w_ffn_in[l].astype(BF16), w_ffn_out[l].astype(BF16)]

    c_all = _all_gather([jnp.pad(c, ((0, 7), (0, 0)))], "ag_c")[0][:, 0, :]
    modpart, c_act = _ada_fwd(c_all, w_ada, "ada_fwd")
    ncol = modpart.shape[2]
    mg = _all_gather([modpart.reshape(DEPTH * NDEV, ncol)], "ag_mod")[0].reshape(NDEV, DEPTH, NDEV, ncol)
    mine = lax.dynamic_index_in_dim(mg, dev, axis=2, keepdims=False)
    mod = (jnp.transpose(mine, (1, 0, 2)).reshape(DEPTH, N_MOD * D) + b_ada).reshape(DEPTH, N_MOD, D)

    ncs = w_short.shape[2]
    ag_in0 = _gather_start([w_in[0].astype(BF16), w_short.reshape(DEPTH * SHORT_K, ncs),
                            cfm_conv_w.reshape(DEPTH * CFM_K, ncs)], dev, "ag_w_in0", deps=(mod,))
    W, Mo, Vo = lax.optimization_barrier((ag_in0["tok"], (W, Mo, Vo)))[1]
    (norm1_g, norm2_g, w_in, w_a_out, w_b_out, w_c_out, w_o, w_ffn_in, w_ffn_out, sgu_ln_g, sgu_ln_b, w_sgu, b_sgu,
     cfm_conv_b, cfm_ln_g, cfm_ln_b, final_g) = [W[k] for k in (
         "norm1_g", "norm2_g", "w_in", "w_a_out", "w_b_out", "w_c_out", "w_o", "w_ffn_in", "w_ffn_out", "sgu_ln_g",
         "sgu_ln_b", "w_sgu", "b_sgu", "cfm_conv_b", "cfm_ln_g", "cfm_ln_b", "final_g")]
    m_w_ada, v_w_ada = Mo["w_ada"], Vo["w_ada"]
    xl0, h0, ht0 = _norm_fwd(x0, None, _rows(jnp.zeros((D,), F32), norm1_g[0], mod[0, 1], mod[0, 0]), "norm1_fwd0",
                             deps=(ag_in0["tok"],))
    ag_rest0 = _gather_start(shards_of(0)[1:], dev, "ag_rest0", deps=(h0,))

    tril = jnp.tril(jnp.ones((CHUNK, CHUNK), dtype=bool))

    def layer_consts(l):
        wt = jnp.where(tril[None], w_sgu[l], 0.0).astype(BF16)
        return dict(sgu_ln=_rows(sgu_ln_g[l], sgu_ln_b[l]), wtril=wt, wtril_t=jnp.swapaxes(wt, 1, 2),
                    bias_full=jnp.repeat(b_sgu[l].T, LANE, axis=1), cvec=_rows(cfm_conv_b[l], cfm_ln_g[l], cfm_ln_b[l]))

    def rest_of(g):
        return dict(w_a=g[0].reshape(1, D, D), w_b=g[1].reshape(1, D, D), w_c=g[2].reshape(1, D, D),
                    w_o=g[3].reshape(1, D, D), w_fi=jnp.transpose(g[4], (1, 0, 2)).reshape(1, D, F2),
                    w_fo=g[5].reshape(1, FF, D))

    consts = [layer_consts(l) for l in range(DEPTH)]
    sharded_small = ("w_short", "cfm_conv_w")

    def param_get(T):
        def get(name, l):
            if name == "final_g":
                return T[name]
            return None if name in sharded_small or name == "loss" else T[name][l]
        return get

    packs = [_pack(param_get(T), D) for T in (W, Mo, Vo)]
    ncr = DEPTH * (SHORT_K + CFM_K)
    padr = (-ncr) % 8
    convw_wmv = [jnp.pad(jnp.concatenate([T["w_short"].reshape(-1, ncs), T["cfm_conv_w"].reshape(-1, ncs)]),
                         ((0, padr), (0, 0))) for T in (W, Mo, Vo)]
    early_work = [ag_rest0["tok"], *packs, *convw_wmv] + [a for cl in consts for a in cl.values()]
    ag_in0 = _gather_mid(ag_in0, early_work, "ag_w_in0")
    g_in0 = _gather_finish(ag_in0, ag_in0["tok"], "ag_w_in0")
    w_short_full = jnp.transpose(g_in0[1], (1, 0, 2)).reshape(DEPTH, SHORT_K, D)
    cfm_w_full = jnp.transpose(g_in0[2], (1, 0, 2)).reshape(DEPTH, CFM_K, D)
    for l in range(DEPTH):
        consts[l]["wsh"] = jnp.pad(w_short_full[l], ((0, 8 - SHORT_K), (0, 0)))
        consts[l]["cw"] = jnp.pad(cfm_w_full[l], ((0, HALO - CFM_K), (0, 0)))
    Wg = [dict(w_in=g_in0[0]), None]
    ag_l1 = None
    nin = w_in.shape[2]
    tn_in = nin if nin % 256 == 0 and nin <= 1280 else 256
    tn_fi = 512 if F2 % 512 == 0 else 256
    tn_ffn = 1408 if F2 % 1408 == 0 else tn_fi
    tn_dw = min(256, D)

    saved = []
    xcur, fprev, gprev = x0, None, None
    for l in range(DEPTH):
        sh1, sc1, g1, sh2, sc2, g2 = [mod[l, k] for k in range(N_MOD)]
        cl = consts[l]
        if l == 0:
            xl, h, ht = xl0, h0, ht0
        else:
            vec1 = _rows(gprev, norm1_g[l], sc1, sh1)
            ag_l1 = _gather_mid(ag_l1, fprev, f"ag_w{l}")
            xl, h, ht = _norm_fwd(xcur, fprev, vec1, f"norm1_fwd{l}", deps=(ag_l1["tok"],))
            g = _gather_finish(ag_l1, h, f"ag_w{l}")
            Wg[l] = dict(w_in=g[0], **rest_of(g[1:]))
        wl = Wg[l]
        z = _mm_nn(h, wl["w_in"], BF16, tm_huge, tn_in, D, f"mm_in{l}", w_outer=True)
        mix_deps = ()
        if l == 0:
            ag_rest0 = _gather_mid(ag_rest0, z, "ag_rest0")
            mix_deps = (ag_rest0["tok"],)
            if DEPTH > 1:
                ag_l1 = _gather_start(shards_of(1), dev, "ag_w1")
                mix_deps += (ag_l1["tok"],)
        acts, acts_t, conv = _mixer_fwd(z, cl["wsh"], cl["sgu_ln"], cl["wtril"], cl["bias_full"], cl["cw"], cl["cvec"],
                                        f"mixer_fwd{l}", deps=mix_deps)
        if l == 0:
            wl.update(rest_of(_gather_finish(ag_rest0, acts[0], "ag_rest0")))
        merged, merged_t, ys = _branch_out(acts, [wl["w_a"][0], wl["w_b"][0], wl["w_c"][0]], z, f"branch_out{l}")
        o = _mm_nn(merged, wl["w_o"], F32, tm_big, D, D, f"mm_o{l}")
        x1, h2, h2t = _norm_fwd(xl, o, _rows(g1, norm2_g[l], sc2, sh2), f"norm2_fwd{l}")
        gu, act, act_t = _ffn_in_swiglu(h2, wl["w_fi"], tm_huge, 256, f"mm_ffn_in{l}")
        f = _mm_nn(act, wl["w_fo"], F32, tm_big, D, FF, f"mm_ffn_out{l}")
        saved.append(dict(xl=xl, ht=ht, z=z, acts_t=acts_t, conv=conv, ys=ys, merged_t=merged_t, o=o, x1=x1, h2t=h2t, gu=gu,
                          act_t=act_t, f=f, consts=cl, mod=(sh1, sc1, g1, sh2, sc2, g2)))
        xcur, fprev, gprev = x1, f, g2

    last = saved[-1]
    dxup, dfb, fsums, loss_blk = _final_bwd(last["x1"], last["f"], tgt, _rows(last["mod"][5], final_g), "final_bwd")
    loss_row = jnp.pad(loss_blk[0, 0:1], (0, D - 1))
    dgate2_next = fsums[1]
    small = [dict() for _ in range(DEPTH)]
    dmods = [None] * DEPTH
    nfi = w_ffn_in.shape[2]
    early_names, late_names = ["w_ffn_out", "w_ffn_in", "w_o"], ["w_a_out", "w_b_out", "w_c_out", "w_in"]
    results = {n: None for n in early_names + late_names}

    def adam_group(names, Ps, R2s, l, deps=()):
        for n, p, r2 in zip(names, Ps, R2s):
            results[n] = _adam_big(p, r2, my_chip, W[n], Mo[n], Vo[n], l, results[n], f"adam_{n}{l}", deps)

    deferred = []
    late_prev = None
    ag_s1, gathered1 = None, None
    tk_w = min(2048, S)
    tn_dw_in = tn_in // 2 if tn_in == 1280 else tn_in
    for l in reversed(range(DEPTH)):
        sv, wl, cl = saved[l], Wg[l], saved[l]["consts"]
        sh1, sc1, g1, sh2, sc2, g2 = sv["mod"]
        dact = _mm_nt(dfb, wl["w_fo"], BF16, tm_big, FF, D, f"mm_dact{l}",
                      deps=() if late_prev is None else (late_prev["tok"], ag_s1["tok"]))
        g_fo = _mm_wgrad(sv["act_t"], dfb, 1, FF // 2, D, tk_w, f"mm_dw_ffn_out{l}")
        dgu = _swiglu_bwd(dact, sv["gu"], f"swiglu_bwd{l}")
        dh2 = _mm_nt(dgu, wl["w_fi"], F32, tm, D, F2, f"mm_dh2{l}")
        if late_prev is not None:
            deferred.append((late_names, *_scatter_finish(late_prev, dh2, f"rs_late{l + 1}"), l + 1))
            late_prev = None
        g_fi = _mm_wgrad(sv["h2t"], dgu, 1, D, tn_fi, S, f"mm_dw_ffn_in{l}")
        if ag_s1 is not None:
            ag_s1 = _gather_mid(ag_s1, g_fi, "ag_small1")
        dx1, dob, s2 = _norm_bwd(sv["x1"], dh2, dxup, _rows(norm2_g[l], sc2, g1), sv["o"], f"norm2_bwd{l}",
                                 deps=() if ag_s1 is None else (ag_s1["tok"],))
        dmerged = _mm_nt(dob, wl["w_o"], BF16, tm_big, D, D, f"mm_dmerged{l}")
        g_o = _mm_wgrad(sv["merged_t"], dob, 1, D, tn_dw, S, f"mm_dw_o{l}")
        early = _scatter_start([g_fo.reshape(NDEV, FF // NDEV, D),
                                jnp.transpose(g_fi.reshape(D, NDEV, nfi), (1, 0, 2)),
                                g_o.reshape(NDEV, D // NDEV, D)], f"rs_early{l}")
        dys, dz = _gate_bwd(dmerged, sv["z"], sv["ys"], f"gate_bwd{l}", deps=(early["tok"],))
        if ag_s1 is not None:
            gathered1 = _gather_finish(ag_s1, dys[0], "ag_small1")[0]
            ag_s1 = None
        early = _scatter_mid(early, dys[0], my_c, f"rs_early{l}")
        dacts, g_abc = [], []
        for n, key in enumerate(("w_a", "w_b", "w_c")):
            dacts.append(_mm_nt(dys[n], wl[key], BF16, tm_big, D, D, f"mm_dact_{key}{l}",
                                deps=(early["tok"],) if n == 0 else ()))
            g_abc.append(_mm_wgrad(sv["acts_t"][n], dys[n], 1, D, tn_dw, S, f"mm_d{key}{l}"))
        dz, mvec, dcw, dws, dbs = _mixer_bwd(sv["z"], dacts, sv["conv"], dz, cl["wsh"], cl["sgu_ln"], cl["wtril"],
                                             cl["wtril_t"], cl["bias_full"], cl["cw"], cl["cvec"], f"mixer_bwd{l}")
        dh = _mm_nt(dz, wl["w_in"], F32, tm_big, D, tn_in, f"mm_dh{l}",
                    blocks_per_step=2 if (tn_in == nin and wl["w_in"].shape[0] % 2 == 0) else 1)
        g_in = _mm_wgrad(sv["ht"], dz, NDEV, D, tn_dw_in, S, f"mm_dw_in{l}")
        late = _scatter_start([g.reshape(NDEV, D // NDEV, D) for g in g_abc] + [g_in], f"rs_late{l}")
        if l > 0:
            pv = saved[l - 1]
            dxup, dfb, s1 = _norm_bwd(sv["xl"], dh, dx1, _rows(norm1_g[l], sc1, pv["mod"][5]), pv["f"], f"norm1_bwd{l}",
                                      deps=(late["tok"],))
        else:
            dxup, dfb, s1 = _norm_bwd(sv["xl"], dh, dx1, _rows(norm1_g[l], sc1), None, f"norm1_bwd{l}", deps=(late["tok"],))
        deferred.append((early_names, *_scatter_finish(early, dxup, f"rs_early{l}"), l))
        dmods[l] = jnp.stack([s1[0], s1[1], s2[3], s2[0], s2[1], dgate2_next])
        dgate2_next = s1[3]
        small[l] = dict(norm1_g=s1[2], norm2_g=s2[2], sgu_ln_g=mvec[3], sgu_ln_b=mvec[4], cfm_conv_b=mvec[5],
                        cfm_ln_g=mvec[6], cfm_ln_b=mvec[7], b_sgu=dbs[:, :, 0],
                        w_sgu=jnp.where(tril[None], dws, 0.0), b_ada=dmods[l], w_short=mvec[0:SHORT_K],
                        cfm_conv_w=dcw[0:CFM_K])
        small_get = lambda name, k: {"final_g": fsums[0], "loss": loss_row}.get(name) if k is None else small[k][name]
        if l > 0:
            late_prev = _scatter_mid(late, dxup, my_c, f"rs_late{l}")
            ag_s1 = _gather_start([_pack(small_get, D, layers=(l,), tail=True)], dev, "ag_small1", deps=(late_prev["tok"],))
    grad_x = dxup.reshape(x.shape)

    gathered0 = _all_gather([_pack(small_get, D, layers=(0,), tail=False)], "ag_small0", deps=(dxup,))[0]
    late_prev = _scatter_mid(late, gathered0, my_c, "rs_late0")
    gathered = jnp.concatenate([gathered0, gathered1], axis=1)
    sg, sd, sm, sv_ = _adam_small(gathered, *packs, name="adam_small", deps=(late_prev["tok"],))
    loss = sg[FINAL_ROW + 1, 0]
    out = {}
    for name in order:
        if name in SMALL_ROWS and name not in sharded_small:
            out[name] = tuple(_unpack(p, name, W[name].shape) for p in (sg, sd, sm, sv_))
    out["final_g"] = tuple(p[FINAL_ROW] for p in (sg, sd, sm, sv_))

    def my_cols(name):
        full = _unpack(sg, name, (DEPTH, SMALL_ROWS[name][1], D))
        return lax.dynamic_slice_in_dim(full, dev * ncs, ncs, axis=2)

    gcs = jnp.concatenate([my_cols("w_short").reshape(-1, ncs), my_cols("cfm_conv_w").reshape(-1, ncs)])
    cd, cm, cv = _adam_plain(jnp.pad(gcs, ((0, padr), (0, 0))), *convw_wmv, "adam_convw")
    nsh = DEPTH * SHORT_K
    out["w_short"] = tuple(a[0:nsh].reshape(w_short.shape) for a in (gcs, cd, cm, cv))
    out["cfm_conv_w"] = tuple(a[nsh:ncr].reshape(cfm_conv_w.shape) for a in (gcs, cd, cm, cv))

    dm_all = jnp.stack([gathered[:, l * ROWS_PER_LAYER + 136:l * ROWS_PER_LAYER + 136 + N_MOD, :].reshape(NDEV, N_MOD * D)
                        for l in range(DEPTH)])
    dm_mine = lax.dynamic_slice_in_dim(dm_all, dev * ncol, ncol, axis=2)
    out["w_ada"] = tuple(_adam_ada(jnp.transpose(c_act), dm_mine, w_ada, m_w_ada, v_w_ada, "adam_ada"))

    for names, Ps, R2s, l in deferred:
        adam_group(names, Ps, R2s, l, deps=(late_prev["tok"],))
    adam_group(late_names, *_scatter_finish(late_prev, results["w_o"][0], "rs_late0"), 0)
    for n in early_names + late_names:
        out[n] = tuple(results[n])

    grads = [out[n][0] for n in order]
    deltas = [out[n][1] for n in order]
    new_m = [out[n][2] for n in order]
    new_v = [out[n][3] for n in order]
    return (loss, grad_x, *grads, *deltas, *new_m, *new_v)
```

```python
import functools
import math

import jax
import jax.numpy as jnp
from jax import lax
from jax.experimental import pallas as pl
from jax.experimental.pallas import tpu as pltpu

F32, BF16 = jnp.float32, jnp.bfloat16
NDEV = 8
NCHIP = NDEV // 2
DEPTH = 2
EPS = 1e-6
CHUNK = 128
NG = 8
SHORT_K = 3
CFM_K = 31
HALO = 32
N_MOD = 6
LANE = 128
VMEM_LIMIT = 56 * 1024 * 1024
ADAM_LR, ADAM_B1, ADAM_B2, ADAM_EPS, ADAM_WD, ADAM_STEP = 0.001, 0.9, 0.999, 1e-08, 0.01, 10
_G0 = math.sqrt(2.0 / math.pi)
_G1 = 0.044715
MESH = pl.DeviceIdType.MESH
ANY = pl.BlockSpec(memory_space=pl.ANY)


def _pcall(body, **kw):
    return pl.pallas_call(body, **kw)


def _params(sem=None):
    return pltpu.CompilerParams(dimension_semantics=sem, vmem_limit_bytes=VMEM_LIMIT)


def _sds(shape, dtype):
    return jax.ShapeDtypeStruct(tuple(shape), dtype)


def _mm_body(dims, nk, out_f32, blocks=1):
    def body(a_ref, b_ref, o_ref, *scr):
        k = pl.program_id(2)
        if blocks == 1:
            part = lax.dot_general(a_ref[...], b_ref[...], dims, preferred_element_type=F32)
        else:
            w = a_ref.shape[1] // blocks
            part = None
            for g in range(blocks):
                t = lax.dot_general(a_ref[:, g * w:(g + 1) * w], b_ref[g], dims, preferred_element_type=F32)
                part = t if part is None else part + t
        if nk == 1:
            o_ref[...] = part.reshape(o_ref.shape).astype(o_ref.dtype)
        elif out_f32:
            @pl.when(k == 0)
            def _():
                o_ref[...] = part.reshape(o_ref.shape)

            @pl.when(k > 0)
            def _():
                o_ref[...] += part.reshape(o_ref.shape)
        else:
            acc = scr[0]

            @pl.when(k == 0)
            def _():
                acc[...] = part

            @pl.when(k > 0)
            def _():
                acc[...] += part

            @pl.when(k == nk - 1)
            def _():
                o_ref[...] = acc[...].astype(o_ref.dtype)
    return body


def _after(body, n_in, deps):
    nd = len(deps)
    if nd == 0:
        return body

    def ordered(*refs):
        return body(*refs[:n_in], *refs[n_in + nd:])
    return ordered


def _mm_call(body, grid, in_specs, out_spec, out_shape, acc_shape, name, deps=()):
    scratch = [] if acc_shape is None else [pltpu.VMEM(acc_shape, F32)]
    return _pcall(_after(body, 2, deps), grid=grid, in_specs=in_specs + [ANY] * len(deps), out_specs=out_spec,
                  out_shape=out_shape, scratch_shapes=scratch, name=name,
                  compiler_params=_params(("parallel", "parallel", "arbitrary")))


def _mm_nn(a, b3, out_dtype, tm, tn, tk, name, w_outer=False, deps=()):
    M, K = a.shape
    G, _, Nb = b3.shape
    npb, nk = Nb // tn, K // tk
    out_f32 = out_dtype == F32
    body = _mm_body((((1,), (0,)), ((), ())), nk, out_f32)
    if w_outer:
        grid = (G * npb, M // tm, nk)
        ij = lambda p, q: (q, p)
    else:
        grid = (M // tm, G * npb, nk)
        ij = lambda p, q: (p, q)

    def a_map(p, q, k):
        i, j = ij(p, q)
        return (i, k)

    def b_map(p, q, k):
        i, j = ij(p, q)
        return (j // npb, k, j % npb)

    def o_map(p, q, k):
        return ij(p, q)

    def wrapped(a_ref, b_ref, o_ref, *scr):
        body(a_ref, b_ref, o_ref, *scr)

    return _mm_call(wrapped, grid, [pl.BlockSpec((tm, tk), a_map), pl.BlockSpec((None, tk, tn), b_map)],
                    pl.BlockSpec((tm, tn), o_map), _sds((M, G * Nb), out_dtype),
                    None if (nk == 1 or out_f32) else (tm, tn), name, deps)(a, b3, *deps)


def _mm_nt(a, b3, out_dtype, tm, tn, tk, name, deps=(), blocks_per_step=1):
    M, _ = a.shape
    G, Ko, Nb = b3.shape
    kpb = Nb // tk
    nk = G * kpb // blocks_per_step
    out_f32 = out_dtype == F32
    body = _mm_body((((1,), (1,)), ((), ())), nk, out_f32, blocks_per_step)

    def wrapped(a_ref, b_ref, o_ref, *scr):
        body(a_ref, b_ref, o_ref, *scr)

    if blocks_per_step > 1:
        assert tk == Nb and G % blocks_per_step == 0
        b_spec = pl.BlockSpec((blocks_per_step, tn, tk), lambda i, j, k: (k, j, 0))
    else:
        b_spec = pl.BlockSpec((None, tn, tk), lambda i, j, k: (k // kpb, j, k % kpb))
    return _mm_call(wrapped, (M // tm, Ko // tn, nk),
                    [pl.BlockSpec((tm, tk * blocks_per_step), lambda i, j, k: (i, k)), b_spec],
                    pl.BlockSpec((tm, tn), lambda i, j, k: (i, j)), _sds((M, Ko), out_dtype),
                    None if (nk == 1 or out_f32) else (tm, tn), name, deps)(a, b3, *deps)


def _mm_wgrad(at, b, G, tm, tn, tk, name, deps=()):
    M, T = at.shape
    Nb = b.shape[1] // G
    npb, nk = Nb // tn, T // tk
    body = _mm_body((((1,), (0,)), ((), ())), nk, False)

    def wrapped(a_ref, b_ref, o_ref, *scr):
        body(a_ref, b_ref, o_ref, *scr)

    a = at
    in_specs = [pl.BlockSpec((tm, tk), lambda i, j, k: (i, k)), pl.BlockSpec((tk, tn), lambda i, j, k: (k, j))]
    out_spec = pl.BlockSpec((None, tm, tn), lambda i, j, k: (j // npb, i, j % npb))
    return _mm_call(wrapped, (M // tm, G * npb, nk), in_specs, out_spec, _sds((G, M, Nb), BF16),
                    None if nk == 1 else (tm, tn), name, deps)(a, b, *deps)


def _rsum(v):
    return jnp.sum(v, axis=0, keepdims=True)


def _rmean(v):
    return jnp.mean(v, axis=-1, keepdims=True)


def _gelu(x):
    t = jnp.tanh(_G0 * (x + _G1 * (x * x * x)))
    return x * (0.5 * (1.0 + t)), t


def _dgelu(x, t):
    return 0.5 * (1.0 + t) + 0.5 * x * (1.0 - t * t) * (_G0 * (1.0 + 3.0 * _G1 * (x * x)))


def _sigmoid(x):
    return 1.0 / (1.0 + jnp.exp(-x))


def _fill_shifted(ext, rot):
    v = ext[...]
    n = v.shape[0]
    for b in range(1, 8):
        rot[b - 1] = pltpu.roll(v, n - b, 0)


def _rows_at(ext, rot, s, tm, cs=slice(None)):
    a, b = divmod(s, 8)
    return ext[8 * a:8 * a + tm, cs] if b == 0 else rot[b - 1, 8 * a:8 * a + tm, cs]


def _causal_conv(w_ref, taps, bias, ext, rot, offset, tm, out):
    D = out.shape[1]
    for cb in range(D // LANE):
        cs = slice(cb * LANE, (cb + 1) * LANE)
        acc = None
        for k, o in zip(taps, offset):
            term = w_ref[k:k + 1, cs] * _rows_at(ext, rot, o, tm, cs)
            acc = term if acc is None else acc + term
        out[:, cs] = acc if bias is None else acc + bias[:, cs]


def _rows(*vs):
    a = jnp.stack([v.astype(F32) for v in vs])
    return jnp.pad(a, ((0, 8 - len(vs)), (0, 0)))


def _row_spec(tm, D):
    return pl.BlockSpec((tm, D), lambda i: (i, 0))


def _const_spec(shape):
    nd = len(shape)
    return pl.BlockSpec(shape, lambda i: (0,) * nd)


def _norm_fwd(xp, f, vec, name, deps=()):
    S, D = xp.shape
    tm = min(512, S)
    has_f = f is not None

    def body(*refs):
        if has_f:
            xp_ref, f_ref, vec_ref, xo_ref, h_ref, ht_ref = refs
            x = xp_ref[...] + vec_ref[0:1, :] * f_ref[...]
            xo_ref[...] = x
        else:
            xp_ref, vec_ref, h_ref, ht_ref = refs
            x = xp_ref[...]
        r = lax.rsqrt(_rmean(x * x) + EPS)
        h = (x * r) * vec_ref[1:2, :]
        h = h * (1.0 + vec_ref[2:3, :]) + vec_ref[3:4, :]
        h_ref[...] = h.astype(BF16)
        ht_ref[...] = h.T.astype(BF16)

    rs = _row_spec(tm, D)
    ins = [xp, f, vec] if has_f else [xp, vec]
    in_specs = ([rs, rs] if has_f else [rs]) + [_const_spec((8, D))]
    out_shape = ([_sds((S, D), F32)] if has_f else []) + [_sds((S, D), BF16), _sds((D, S), BF16)]
    out_specs = [rs] * (len(out_shape) - 1) + [pl.BlockSpec((D, tm), lambda i: (0, i))]
    outs = _pcall(_after(body, len(ins), deps), grid=(S // tm,), in_specs=in_specs + [ANY] * len(deps),
                  out_specs=out_specs, out_shape=out_shape, name=name,
                  compiler_params=_params(("parallel",)))(*ins, *deps)
    return (outs[0], outs[1], outs[2]) if has_f else (xp, outs[0], outs[1])


def _mixer_fwd(z, wsh, sgu_ln, wtril, bias_full, cw, cvec, name, deps=()):
    S = z.shape[0]
    D = wsh.shape[1]
    tm = CHUNK

    def body(z_ref, wsh_ref, sln_ref, wt_ref, bias_ref, cw_ref, cv_ref, oa_ref, ob_ref, oc_ref, ta_ref, tb_ref, tc_ref,
             conv_ref, pe, ge, gr, cbuf):
        i = pl.program_id(0)

        @pl.when(i == 0)
        def _():
            pe[0:HALO, :] = jnp.zeros((HALO, D), F32)
            ge[0:HALO, :] = jnp.zeros((HALO, D), F32)

        def col(n):
            return z_ref[:, n * D:(n + 1) * D].astype(F32)

        pe[HALO:HALO + tm, :] = col(1) * col(2)
        q = wsh_ref[0:1, :] * pe[HALO - 2:HALO - 2 + tm, :]
        q = q + wsh_ref[1:2, :] * pe[HALO - 1:HALO - 1 + tm, :]
        q = q + wsh_ref[2:3, :] * pe[HALO:HALO + tm, :]
        act_a = col(0) * q
        oa_ref[...] = act_a.astype(BF16)
        ta_ref[...] = act_a.T.astype(BF16)
        gu, _ = _gelu(col(3))
        gv, _ = _gelu(col(4))
        d = gv - _rmean(gv)
        nrm = d * lax.rsqrt(_rmean(d * d) + EPS)
        vnb = (nrm * sln_ref[0:1, :] + sln_ref[1:2, :]).astype(BF16)
        for g in range(NG):
            cs = slice(g * LANE, (g + 1) * LANE)
            mixed = jnp.dot(wt_ref[g], vnb[:, cs], preferred_element_type=F32) + bias_ref[:, cs]
            act_b = gu[:, cs] * mixed
            ob_ref[:, cs] = act_b.astype(BF16)
            tb_ref[cs, :] = act_b.T.astype(BF16)
        ge[HALO:HALO + tm, :] = col(5) * _sigmoid(col(6))
        _fill_shifted(ge, gr)
        o0 = HALO - (CFM_K - 1)
        _causal_conv(cw_ref, range(CFM_K), cv_ref[0:1, :], ge, gr, range(o0, o0 + CFM_K), tm, cbuf)
        conv = cbuf[...]
        conv_ref[...] = conv.astype(BF16)
        d = conv - _rmean(conv)
        ln = (d * lax.rsqrt(_rmean(d * d) + EPS)) * cv_ref[1:2, :] + cv_ref[2:3, :]
        act_c = ln * _sigmoid(ln)
        oc_ref[...] = act_c.astype(BF16)
        tc_ref[...] = act_c.T.astype(BF16)
        pe[0:HALO, :] = pe[tm:tm + HALO, :]
        ge[0:HALO, :] = ge[tm:tm + HALO, :]

    rs = _row_spec(tm, D)
    outs = _pcall(
        _after(body, 7, deps), grid=(S // tm,),
        in_specs=[pl.BlockSpec((tm, 7 * D), lambda i: (i, 0)), _const_spec((8, D)), _const_spec((8, D)),
                  _const_spec((NG, CHUNK, CHUNK)), _const_spec((CHUNK, D)), _const_spec((HALO, D)), _const_spec((8, D))]
        + [ANY] * len(deps),
        out_specs=[rs, rs, rs] + [pl.BlockSpec((D, tm), lambda i: (0, i))] * 3 + [rs],
        out_shape=[_sds((S, D), BF16)] * 3 + [_sds((D, S), BF16)] * 3 + [_sds((S, D), BF16)],
        scratch_shapes=[pltpu.VMEM((HALO + tm, D), F32), pltpu.VMEM((HALO + tm, D), F32),
                        pltpu.VMEM((7, HALO + tm, D), F32), pltpu.VMEM((tm, D), F32)],
        name=name, compiler_params=_params(("arbitrary",)))(z, wsh, sgu_ln, wtril, bias_full, cw, cvec, *deps)
    return outs[:3], outs[3:6], outs[6]


def _branch_out(acts, ws, z, name):
    S, D = acts[0].shape
    tm = min(256, S)

    def body(a0, a1, a2, w0, w1, w2, g0, g1, g2, m_ref, mt_ref, y_ref):
        m = None
        for n, (a, w, g) in enumerate(((a0, w0, g0), (a1, w1, g1), (a2, w2, g2))):
            y = jnp.dot(a[...], w[...], preferred_element_type=F32)
            y_ref[n] = y.astype(BF16)
            t = _sigmoid(g[...].astype(F32)) * y
            m = t if m is None else m + t
        m_ref[...] = m.astype(BF16)
        mt_ref[...] = m.T.astype(BF16)

    rs = _row_spec(tm, D)
    gate_specs = [pl.BlockSpec((tm, D), functools.partial(lambda i, n: (i, 7 + n), n=n)) for n in range(3)]
    return _pcall(body, grid=(S // tm,),
                  in_specs=[rs, rs, rs] + [_const_spec((D, D))] * 3 + gate_specs,
                  out_specs=[rs, pl.BlockSpec((D, tm), lambda i: (0, i)), pl.BlockSpec((3, tm, D), lambda i: (0, i, 0))],
                  out_shape=[_sds((S, D), BF16), _sds((D, S), BF16), _sds((3, S, D), BF16)], name=name,
                  compiler_params=_params(("parallel",)))(*acts, *ws, z, z, z)


def _ffn_in_swiglu(h2, w3, tm, tn, name):
    S, D = h2.shape
    F = w3.shape[2] // 2
    nj = F // tn

    def body(a_ref, wg_ref, wu_ref, gu_ref, act_ref, actt_ref):
        a = a_ref[...]
        g = jnp.dot(a, wg_ref[...], preferred_element_type=F32)
        u = jnp.dot(a, wu_ref[...], preferred_element_type=F32)
        gu_ref[0] = g.astype(BF16)
        gu_ref[1] = u.astype(BF16)
        act = (g * _sigmoid(g)) * u
        act_ref[...] = act.astype(BF16)
        actt_ref[...] = act.T.astype(BF16)

    return _pcall(body, grid=(S // tm, nj),
                  in_specs=[pl.BlockSpec((tm, D), lambda i, j: (i, 0)), pl.BlockSpec((None, D, tn), lambda i, j: (0, 0, j)),
                            pl.BlockSpec((None, D, tn), lambda i, j: (0, 0, j + nj))],
                  out_specs=[pl.BlockSpec((2, tm, tn), lambda i, j: (0, i, j)), pl.BlockSpec((tm, tn), lambda i, j: (i, j)),
                             pl.BlockSpec((tn, tm), lambda i, j: (j, i))],
                  out_shape=[_sds((2, S, F), BF16), _sds((S, F), BF16), _sds((F, S), BF16)], name=name,
                  compiler_params=_params(("parallel", "parallel")))(h2, w3, w3)


def _swiglu_bwd(dact, gu, name):
    _, S, F = gu.shape
    F2 = 2 * F
    tm = min(256, S)

    def body(d_ref, g_ref, u_ref, o_ref):
        g = g_ref[...].astype(F32)
        sg = _sigmoid(g)
        d = d_ref[...].astype(F32)
        o_ref[:, 0:F] = (d * u_ref[...].astype(F32) * (sg * (1.0 + g * (1.0 - sg)))).astype(BF16)
        o_ref[:, F:2 * F] = (d * (g * sg)).astype(BF16)

    return _pcall(body, grid=(S // tm,),
                  in_specs=[pl.BlockSpec((tm, F), lambda i: (i, 0)), pl.BlockSpec((None, tm, F), lambda i: (0, i, 0)),
                            pl.BlockSpec((None, tm, F), lambda i: (1, i, 0))],
                  out_specs=pl.BlockSpec((tm, F2), lambda i: (i, 0)), out_shape=_sds((S, F2), BF16), name=name,
                  compiler_params=_params(("parallel",)))(dact, gu, gu)


def _final_bwd(x1, f, tgt, vec, name):
    S, D = x1.shape
    tm = min(512, S)

    def body(x_ref, f_ref, t_ref, vec_ref, dx_ref, df_ref, sums_ref, loss_ref):
        @pl.when(pl.program_id(0) == 0)
        def _():
            sums_ref[...] = jnp.zeros_like(sums_ref)
            loss_ref[...] = jnp.zeros_like(loss_ref)

        gate, fg = vec_ref[0:1, :], vec_ref[1:2, :]
        fv = f_ref[...]
        x = x_ref[...] + gate * fv
        r = lax.rsqrt(_rmean(x * x) + EPS)
        xn = x * r
        diff = xn * fg - t_ref[...]
        per_tok = _rmean(diff * diff)
        loss_ref[...] += 0.5 * jnp.sum(per_tok, axis=0, keepdims=True)
        dy = diff * (1.0 / D)
        sums_ref[0:1, :] += _rsum(dy * xn)
        dxn = dy * fg
        dx = r * (dxn - xn * _rmean(dxn * xn))
        sums_ref[1:2, :] += _rsum(dx * fv)
        dx_ref[...] = dx
        df_ref[...] = (dx * gate).astype(BF16)

    rs = _row_spec(tm, D)
    return _pcall(body, grid=(S // tm,), in_specs=[rs, rs, rs, _const_spec((8, D))],
                  out_specs=[rs, rs, _const_spec((8, D)), _const_spec((8, LANE))],
                  out_shape=[_sds((S, D), F32), _sds((S, D), BF16), _sds((8, D), F32), _sds((8, LANE), F32)],
                  name=name, compiler_params=_params(("arbitrary",)))(x1, f, tgt, vec)


def _norm_bwd(xin, dh, dxup, vec, fprev, name, deps=()):
    S, D = xin.shape
    tm = min(512, S)
    has_prev = fprev is not None

    def body(*refs):
        if has_prev:
            x_ref, dh_ref, up_ref, vec_ref, fp_ref, dx_ref, dp_ref, sums_ref = refs
        else:
            x_ref, dh_ref, up_ref, vec_ref, dx_ref, sums_ref = refs

        @pl.when(pl.program_id(0) == 0)
        def _():
            sums_ref[...] = jnp.zeros_like(sums_ref)

        g, scale = vec_ref[0:1, :], vec_ref[1:2, :]
        x = x_ref[...]
        r = lax.rsqrt(_rmean(x * x) + EPS)
        xn = x * r
        dhv = dh_ref[...]
        sums_ref[0:1, :] += _rsum(dhv)
        sums_ref[1:2, :] += _rsum(dhv * (xn * g))
        dm = dhv * (1.0 + scale)
        sums_ref[2:3, :] += _rsum(dm * xn)
        dxn = dm * g
        dx = up_ref[...] + r * (dxn - xn * _rmean(dxn * xn))
        dx_ref[...] = dx
        if has_prev:
            sums_ref[3:4, :] += _rsum(dx * fp_ref[...])
            dp_ref[...] = (dx * vec_ref[2:3, :]).astype(BF16)

    rs = _row_spec(tm, D)
    ins = [xin, dh, dxup, vec] + ([fprev] if has_prev else [])
    in_specs = [rs, rs, rs, _const_spec((8, D))] + ([rs] if has_prev else [])
    out_shape = [_sds((S, D), F32)] + ([_sds((S, D), BF16)] if has_prev else []) + [_sds((8, D), F32)]
    out_specs = [rs] + ([rs] if has_prev else []) + [_const_spec((8, D))]
    outs = _pcall(_after(body, len(ins), deps), grid=(S // tm,), in_specs=in_specs + [ANY] * len(deps),
                  out_specs=out_specs, out_shape=out_shape, name=name,
                  compiler_params=_params(("arbitrary",)))(*ins, *deps)
    return (outs[0], outs[1], outs[2]) if has_prev else (outs[0], None, outs[1])


def _gate_bwd(dmerged, z, ys, name, deps=()):
    S, D = dmerged.shape
    tm = min(512, S)
    ncol = z.shape[1] // D

    def body(dm_ref, g_ref, y_ref, dya_ref, dyb_ref, dyc_ref, dz_ref):
        n = pl.program_id(1)
        sg = _sigmoid(g_ref[...].astype(F32))
        dm = dm_ref[...].astype(F32)
        dy = (dm * sg).astype(BF16)
        for k, ref in enumerate((dya_ref, dyb_ref, dyc_ref)):
            @pl.when(n == k)
            def _(ref=ref):
                ref[...] = dy
        dz_ref[...] = (dm * y_ref[...].astype(F32) * (sg * (1.0 - sg))).astype(BF16)

    row = pl.BlockSpec((tm, D), lambda i, n: (i, 0))
    outs = _pcall(_after(body, 3, deps), grid=(S // tm, 3),
                  in_specs=[row, pl.BlockSpec((tm, D), lambda i, n: (i, 7 + n)),
                            pl.BlockSpec((None, tm, D), lambda i, n: (n, i, 0))] + [ANY] * len(deps),
                  out_specs=[row, row, row, pl.BlockSpec((tm, D), lambda i, n: (i, 7 + n))],
                  out_shape=[_sds((S, D), BF16)] * 3 + [_sds((S, ncol * D), BF16)], name=name,
                  compiler_params=_params(("parallel", "arbitrary")))(dmerged, z, ys, *deps)
    return outs[:3], outs[3]


def _mixer_bwd(z, dacts, conv, dz, wsh, sgu_ln, wtril, wtril_t, bias_full, cw, cvec, name):
    S = z.shape[0]
    D = wsh.shape[1]
    tm = CHUNK
    nt = S // tm
    hb = tm // HALO

    def body(zc, zp, da_ref, db_ref, dc_ref, conv_ref, wsh_ref, sln_ref, wt_ref, wtt_ref, bias_ref, cw_ref, cv_ref, _dz_in,
             dz_ref, vec_ref, dcw_ref, dws_ref, dbs_ref, pe, ge, dqe, dce, gr, dcr, cbuf, dcw8):
        i = pl.program_id(0)
        rb = nt - 1 - i

        @pl.when(i == 0)
        def _():
            vec_ref[...] = jnp.zeros_like(vec_ref)
            dcw8[...] = jnp.zeros_like(dcw8)
            dws_ref[...] = jnp.zeros_like(dws_ref)
            dbs_ref[...] = jnp.zeros_like(dbs_ref)
            dqe[tm:tm + HALO, :] = jnp.zeros((HALO, D), F32)
            dce[tm:tm + HALO, :] = jnp.zeros((HALO, D), F32)

        keep = (rb > 0).astype(F32)

        def col(n):
            return zc[:, n * D:(n + 1) * D].astype(F32)

        def pcol(n):
            return zp[:, n * D:(n + 1) * D].astype(F32)

        c_a, x_a = col(1), col(2)
        pe[0:HALO, :] = keep * (pcol(1) * pcol(2))
        pe[HALO:HALO + tm, :] = c_a * x_a
        q = wsh_ref[0:1, :] * pe[HALO - 2:HALO - 2 + tm, :]
        q = q + wsh_ref[1:2, :] * pe[HALO - 1:HALO - 1 + tm, :]
        q = q + wsh_ref[2:3, :] * pe[HALO:HALO + tm, :]
        dact = da_ref[...].astype(F32)
        dz_ref[:, 0:D] = (dact * q).astype(BF16)
        dq = dact * col(0)
        dqe[0:tm, :] = dq
        dp = wsh_ref[2:3, :] * dq + wsh_ref[1:2, :] * dqe[1:1 + tm, :] + wsh_ref[0:1, :] * dqe[2:2 + tm, :]
        dz_ref[:, D:2 * D] = (dp * x_a).astype(BF16)
        dz_ref[:, 2 * D:3 * D] = (dp * c_a).astype(BF16)
        for k in range(SHORT_K):
            o = HALO - (SHORT_K - 1) + k
            vec_ref[k:k + 1, :] += _rsum(dq * pe[o:o + tm, :])
        u, v = col(3), col(4)
        gu, tu = _gelu(u)
        gv, tv = _gelu(v)
        d = gv - _rmean(gv)
        rstd = lax.rsqrt(_rmean(d * d) + EPS)
        nrm = d * rstd
        vnb = (nrm * sln_ref[0:1, :] + sln_ref[1:2, :]).astype(BF16)
        dact = db_ref[...].astype(F32)
        dvn_parts, dgu_parts = [], []
        for g in range(NG):
            cs = slice(g * LANE, (g + 1) * LANE)
            vg = vnb[:, cs]
            mixed = jnp.dot(wt_ref[g], vg, preferred_element_type=F32) + bias_ref[:, cs]
            dgu_parts.append(dact[:, cs] * mixed)
            dmixed = dact[:, cs] * gu[:, cs]
            dmb = dmixed.astype(BF16)
            dws_ref[g] += lax.dot_general(dmb, vg, (((1,), (1,)), ((), ())), preferred_element_type=F32)
            dbs_ref[g] += jnp.broadcast_to(jnp.sum(dmixed, axis=1, keepdims=True), (CHUNK, LANE))
            dvn_parts.append(jnp.dot(wtt_ref[g], dmb, preferred_element_type=F32))
        dgu = jnp.concatenate(dgu_parts, axis=1)
        dvn = jnp.concatenate(dvn_parts, axis=1)
        dz_ref[:, 3 * D:4 * D] = (dgu * _dgelu(u, tu)).astype(BF16)
        vec_ref[3:4, :] += _rsum(dvn * nrm)
        vec_ref[4:5, :] += _rsum(dvn)
        dn = dvn * sln_ref[0:1, :]
        dgv = rstd * (dn - _rmean(dn) - nrm * _rmean(dn * nrm))
        dz_ref[:, 4 * D:5 * D] = (dgv * _dgelu(v, tv)).astype(BF16)
        a_c = col(5)
        sg = _sigmoid(col(6))
        ge[0:HALO, :] = keep * (pcol(5) * _sigmoid(pcol(6)))
        ge[HALO:HALO + tm, :] = a_c * sg
        _fill_shifted(ge, gr)
        o0 = HALO - (CFM_K - 1)
        conv = conv_ref[...].astype(F32)
        d = conv - _rmean(conv)
        rstd = lax.rsqrt(_rmean(d * d) + EPS)
        nrm = d * rstd
        ln = nrm * cv_ref[1:2, :] + cv_ref[2:3, :]
        sl = _sigmoid(ln)
        dln = dc_ref[...].astype(F32) * (sl * (1.0 + ln * (1.0 - sl)))
        vec_ref[6:7, :] += _rsum(dln * nrm)
        vec_ref[7:8, :] += _rsum(dln)
        dn = dln * cv_ref[1:2, :]
        dconv = rstd * (dn - _rmean(dn) - nrm * _rmean(dn * nrm))
        vec_ref[5:6, :] += _rsum(dconv)
        dce[0:tm, :] = dconv
        _fill_shifted(dce, dcr)
        _causal_conv(cw_ref, range(CFM_K), None, dce, dcr, [CFM_K - 1 - k for k in range(CFM_K)], tm, cbuf)
        dglu = cbuf[...]
        for cb in range(D // LANE):
            cs = slice(cb * LANE, (cb + 1) * LANE)
            dcv = dce[0:tm, cs]
            for k in range(CFM_K):
                prod = dcv * _rows_at(ge, gr, o0 + k, tm, cs)
                dcw8[k, :, cs] += jnp.sum(prod.reshape(tm // 8, 8, LANE), axis=0)

        @pl.when(i == nt - 1)
        def _():
            dcw_ref[...] = jnp.sum(dcw8[...], axis=1)
        dz_ref[:, 5 * D:6 * D] = (dglu * sg).astype(BF16)
        dz_ref[:, 6 * D:7 * D] = (dglu * a_c * (sg * (1.0 - sg))).astype(BF16)
        dqe[tm:tm + HALO, :] = dqe[0:HALO, :]
        dce[tm:tm + HALO, :] = dce[0:HALO, :]

    rev = lambda i: (nt - 1 - i, 0)
    rs = pl.BlockSpec((tm, D), rev)
    cur = pl.BlockSpec((tm, 7 * D), rev)
    prev = pl.BlockSpec((HALO, 7 * D), lambda i: (jnp.maximum((nt - 1 - i) * hb - 1, 0), 0))
    ext = pltpu.VMEM((HALO + tm, D), F32)
    outs = _pcall(
        body, grid=(nt,),
        in_specs=[cur, prev, rs, rs, rs, rs, _const_spec((8, D)), _const_spec((8, D)), _const_spec((NG, CHUNK, CHUNK)),
                  _const_spec((NG, CHUNK, CHUNK)), _const_spec((CHUNK, D)), _const_spec((HALO, D)), _const_spec((8, D)),
                  ANY],
        out_specs=[cur, _const_spec((8, D)), _const_spec((HALO, D)), _const_spec((NG, CHUNK, CHUNK)),
                   _const_spec((NG, CHUNK, LANE))],
        out_shape=[_sds(dz.shape, BF16), _sds((8, D), F32), _sds((HALO, D), F32), _sds((NG, CHUNK, CHUNK), F32),
                   _sds((NG, CHUNK, LANE), F32)],
        scratch_shapes=[ext, ext, ext, ext, pltpu.VMEM((7, HALO + tm, D), F32), pltpu.VMEM((7, HALO + tm, D), F32),
                        pltpu.VMEM((tm, D), F32), pltpu.VMEM((HALO, 8, D), F32)],
        input_output_aliases={13: 0}, name=name,
        compiler_params=_params(("arbitrary",)))(z, z, *dacts, conv, wsh, sgu_ln, wtril, wtril_t, bias_full, cw, cvec, dz)
    return outs


def _ada_fwd(c_all, w_ada_loc, name):
    nb, D = c_all.shape
    L, _, nc = w_ada_loc.shape

    def body(c_ref, w_ref, o_ref, ca_ref):
        cv = c_ref[...]
        ca = cv * _sigmoid(cv)
        ca_ref[...] = ca
        o_ref[...] = jnp.dot(ca.astype(BF16), w_ref[...].astype(BF16), preferred_element_type=F32)

    return _pcall(body, grid=(L,),
                  in_specs=[_const_spec((nb, D)), pl.BlockSpec((None, D, nc), lambda l: (l, 0, 0))],
                  out_specs=[pl.BlockSpec((None, nb, nc), lambda l: (l, 0, 0)), _const_spec((nb, D))],
                  out_shape=[_sds((L, nb, nc), F32), _sds((nb, D), F32)], name=name,
                  compiler_params=_params(("arbitrary",)))(c_all, w_ada_loc)


def _adamw(w, g, m, v):
    m = ADAM_B1 * m + (1.0 - ADAM_B1) * g
    v = ADAM_B2 * v + (1.0 - ADAM_B2) * (g * g)
    m_hat = m / (1.0 - ADAM_B1 ** ADAM_STEP)
    v_hat = v / (1.0 - ADAM_B2 ** ADAM_STEP)
    delta = -ADAM_LR * (m_hat / (jnp.sqrt(v_hat) + ADAM_EPS) + ADAM_WD * w)
    return delta, m, v


def _tile_rows(R, C, align=8):
    cap = max(align, (1536 * 1024) // (4 * C))
    best = None
    for t in range(align, R + 1, align):
        if R % t == 0 and t <= cap:
            best = t
    return R if best is None else best


def _adam_ada(ct, dm, w, m, v, name):
    L, D, nc = w.shape
    nb = ct.shape[1]
    tr = _tile_rows(D, nc)

    def body(ct_ref, dm_ref, w_ref, m_ref, v_ref, g_ref, d_ref, mo_ref, vo_ref):
        g = ct_ref[:, 0:1] * dm_ref[0:1, :]
        for b in range(1, nb):
            g = g + ct_ref[:, b:b + 1] * dm_ref[b:b + 1, :]
        g_ref[...] = g
        d_ref[...], mo_ref[...], vo_ref[...] = _adamw(w_ref[...], g, m_ref[...], v_ref[...])

    ws = pl.BlockSpec((None, tr, nc), lambda l, r: (l, r, 0))
    return _pcall(body, grid=(L, D // tr),
                  in_specs=[pl.BlockSpec((tr, nb), lambda l, r: (r, 0)), pl.BlockSpec((None, nb, nc), lambda l, r: (l, 0, 0)),
                            ws, ws, ws],
                  out_specs=[ws] * 4, out_shape=[_sds(w.shape, F32)] * 4, name=name,
                  compiler_params=_params(("parallel", "parallel")))(ct, dm, w, m, v)


def _adam_small(parts, w, m, v, name, deps=()):
    n, R, C = parts.shape
    tr = _tile_rows(R, C * n // 2)

    def body(p_ref, w_ref, m_ref, v_ref, g_ref, d_ref, mo_ref, vo_ref):
        g = p_ref[0]
        for j in range(1, n):
            g = g + p_ref[j]
        g_ref[...] = g
        d_ref[...], mo_ref[...], vo_ref[...] = _adamw(w_ref[...], g, m_ref[...], v_ref[...])

    ws = pl.BlockSpec((tr, C), lambda r: (r, 0))
    return _pcall(_after(body, 4, deps), grid=(R // tr,),
                  in_specs=[pl.BlockSpec((n, tr, C), lambda r: (0, r, 0)), ws, ws, ws] + [ANY] * len(deps),
                  out_specs=[ws] * 4, out_shape=[_sds((R, C), F32)] * 4, name=name,
                  compiler_params=_params(("parallel",)))(parts, w, m, v, *deps)


def _adam_plain(g, w, m, v, name):
    R, C = w.shape

    def body(g_ref, w_ref, m_ref, v_ref, d_ref, mo_ref, vo_ref):
        d_ref[...], mo_ref[...], vo_ref[...] = _adamw(w_ref[...], g_ref[...], m_ref[...], v_ref[...])

    ws = _const_spec((R, C))
    return _pcall(body, grid=(1,), in_specs=[ws] * 4, out_specs=[ws] * 3, out_shape=[_sds((R, C), F32)] * 3, name=name,
                  compiler_params=_params(("arbitrary",)))(g, w, m, v)


def _pair_sum(G, R1, my_c, name):
    n, R, C = G.shape
    half = n // 2
    tr = _tile_rows(R, C, align=16)

    def body(c_ref, g_ref, r_ref, o_ref):
        o_ref[...] = (g_ref[...].astype(F32) + r_ref[...].astype(F32)).astype(o_ref.dtype)

    blk = (None, tr, C)
    gs = pltpu.PrefetchScalarGridSpec(
        num_scalar_prefetch=1, grid=(half, R // tr),
        in_specs=[pl.BlockSpec(blk, lambda p, r, c: (2 * p + c[0], r, 0)), pl.BlockSpec(blk, lambda p, r, c: (p, r, 0))],
        out_specs=pl.BlockSpec(blk, lambda p, r, c: (p, r, 0)))
    return _pcall(body, grid_spec=gs, out_shape=_sds((half, R, C), G.dtype), name=name,
                  compiler_params=_params(("parallel", "parallel")))(my_c, G, R1)


def _adam_big(P, R2, my_chip, w, m, v, layer, prev, name, deps=()):
    _, R, C = P.shape
    nrecv = R2.shape[0]
    tr = _tile_rows(R, C, align=16)

    def body(p_sm, p_ref, r_ref, w_ref, m_ref, v_ref, *rest):
        g_ref, d_ref, mo_ref, vo_ref = rest[-4:]
        g = p_ref[...].astype(F32)
        for k in range(nrecv):
            g = g + r_ref[k].astype(F32)
        g_ref[...] = g
        d_ref[...], mo_ref[...], vo_ref[...] = _adamw(w_ref[...], g, m_ref[...], v_ref[...])

    ws = pl.BlockSpec((None, tr, C), lambda r, p: (layer, r, 0))
    held = [] if prev is None else list(prev)
    gs = pltpu.PrefetchScalarGridSpec(
        num_scalar_prefetch=1, grid=(R // tr,),
        in_specs=[pl.BlockSpec((None, tr, C), lambda r, p: (p[0], r, 0)),
                  pl.BlockSpec((nrecv, tr, C), lambda r, p: (0, r, 0)), ws, ws, ws] + [ANY] * (len(held) + len(deps)),
        out_specs=[ws] * 4)
    alias = {6 + i: i for i in range(len(held))}
    return _pcall(body, grid_spec=gs, out_shape=[_sds(w.shape, F32)] * 4, name=name, input_output_aliases=alias,
                  compiler_params=_params(("parallel",)))(my_chip, P, R2, w, m, v, *held, *deps)


def _place():
    return lax.axis_index("x"), lax.axis_index("y"), lax.axis_index("c")


def _all_gather(shards, name, deps=()):
    n = len(shards)

    def body(*refs):
        ins, outs = refs[:n], refs[n:2 * n]
        send_sems, recv_sems, local_sems = refs[2 * n:]
        x, y, c = _place()
        me, sibling = (x, y, c), (x, y, 1 - c)
        chips = [(1 - x, y), (x, 1 - y), (1 - x, 1 - y)]

        def slot(a, px, py, pc):
            return outs[a].at[4 * px + 2 * py + pc]

        def copy(a, k, block, to, src=None):
            return pltpu.make_async_remote_copy(
                src_ref=slot(a, *block) if src is None else src, dst_ref=slot(a, *block),
                send_sem=send_sems.at[7 * a + k], recv_sem=recv_sems.at[7 * a + k], device_id=to, device_id_type=MESH)

        mine = [pltpu.make_async_copy(ins[a], slot(a, *me), local_sems.at[a]) for a in range(n)]
        for cp in mine:
            cp.start()
        first = []
        for a in range(n):
            first.append(copy(a, 0, me, sibling, src=ins[a]))
            first += [copy(a, 1 + j, me, (*chip, c), src=ins[a]) for j, chip in enumerate(chips)]
        for cp in first:
            cp.start()
        passed = []
        for j, chip in enumerate(chips):
            for a in range(n):
                copy(a, 1 + j, (*chip, c), me).wait_recv()
                fwd = copy(a, 4 + j, (*chip, c), sibling)
                fwd.start()
                passed.append(fwd)
        for a in range(n):
            copy(a, 0, sibling, me).wait_recv()
        for j, chip in enumerate(chips):
            for a in range(n):
                copy(a, 4 + j, (*chip, 1 - c), me).wait_recv()
        for cp in first + passed:
            cp.wait_send()
        for cp in mine:
            cp.wait()

    outs = _pcall(_after(body, n, deps), in_specs=[ANY] * (n + len(deps)), out_specs=[ANY] * n,
                  out_shape=[_sds((NDEV,) + s.shape, s.dtype) for s in shards],
                  scratch_shapes=[pltpu.SemaphoreType.DMA((7 * n,)), pltpu.SemaphoreType.DMA((7 * n,)),
                                  pltpu.SemaphoreType.DMA((n,))], name=name)(*shards, *deps)
    return list(outs)


HBM = pl.BlockSpec(memory_space=pltpu.HBM)
SEM = pl.BlockSpec(memory_space=pltpu.SEMAPHORE)


def _copies(plan, refs, send_sems, recv_sems):
    return [pltpu.make_async_remote_copy(src_ref=s, dst_ref=d, send_sem=send_sems.at[k], recv_sem=recv_sems.at[k],
                                         device_id=dev, device_id_type=MESH)
            for k, (s, d, dev) in enumerate(plan(refs, *_place()))]


def _xfer_start(bufs, ncopies, plan, name, deps=()):
    n = len(bufs)

    def body(*refs):
        for cp in _copies(plan, refs[:n], refs[n], refs[n + 1]):
            cp.start()
        token = refs[2 * n + 2]
        token[...] = jnp.zeros_like(token)

    outs = _pcall(
        _after(body, n, deps), name=name,
        out_shape=(pltpu.SemaphoreType.DMA((ncopies,)), pltpu.SemaphoreType.DMA((ncopies,)),
                   *[pltpu.HBM(b.shape, b.dtype) for b in bufs], _sds((8, LANE), F32)),
        in_specs=[HBM] * n + [ANY] * len(deps),
        out_specs=(SEM, SEM, *[HBM] * n, pl.BlockSpec(memory_space=pltpu.VMEM)),
        input_output_aliases={i: 2 + i for i in range(n)},
        compiler_params=pltpu.CompilerParams(has_side_effects=pltpu.SideEffectType.DATAFLOW_SIDE_EFFECTING),
    )(*[pltpu.with_memory_space_constraint(b, pltpu.HBM) for b in bufs], *deps)
    return (outs[0], outs[1]), list(outs[2:2 + n]), outs[2 + n]


def _xfer_wait(sems, bufs, plan, after, name):
    n = len(bufs)
    after = list(after) if isinstance(after, (list, tuple)) else [after]

    def body(*refs):
        for cp in _copies(plan, refs[:n], refs[n], refs[n + 1]):
            cp.wait_send()
            cp.wait_recv()

    outs = _pcall(
        body, name=name, out_shape=tuple(pltpu.HBM(b.shape, b.dtype) for b in bufs),
        in_specs=[HBM] * n + [SEM, SEM] + [ANY] * len(after), out_specs=tuple([HBM] * n),
        input_output_aliases={i: i for i in range(n)},
        compiler_params=pltpu.CompilerParams(has_side_effects=pltpu.SideEffectType.DATAFLOW_SIDE_EFFECTING),
    )(*bufs, *sems, *after)
    return list(outs)


def _chips_of(x, y):
    return [(1 - x, y), (x, 1 - y), (1 - x, 1 - y)]


def _gather_plan1(n):
    def plan(refs, x, y, c):
        out = []
        for a in range(n):
            blk = refs[a].at[4 * x + 2 * y + c]
            out.append((blk, blk, (x, y, 1 - c)))
            out += [(blk, blk, (px, py, c)) for px, py in _chips_of(x, y)]
        return out
    return plan


def _gather_plan2(n):
    def plan(refs, x, y, c):
        out = []
        for a in range(n):
            for px, py in _chips_of(x, y):
                blk = refs[a].at[4 * px + 2 * py + c]
                out.append((blk, blk, (x, y, 1 - c)))
        return out
    return plan


def _gather_start(shards, dev, name, deps=()):
    lands = [lax.dynamic_update_slice(lax.empty((NDEV,) + s.shape, s.dtype), s[None], (dev,) + (0,) * s.ndim)
             for s in shards]
    n = len(shards)
    sems, lands, tok = _xfer_start(lands, 4 * n, _gather_plan1(n), name + "_p1_start", deps)
    return dict(sems=sems, lands=lands, tok=tok, n=n)


def _gather_mid(st, after, name):
    n = st["n"]
    lands = _xfer_wait(st["sems"], st["lands"], _gather_plan1(n), after, name + "_p1_wait")
    sems, lands, tok = _xfer_start(lands, 3 * n, _gather_plan2(n), name + "_p2_start")
    return dict(sems=sems, lands=lands, tok=tok, n=n)


def _gather_finish(st, after, name):
    return _xfer_wait(st["sems"], st["lands"], _gather_plan2(st["n"]), after, name + "_p2_wait")


def _scatter_plan1(n):
    def plan(refs, x, y, c):
        return [(refs[a].at[2 * p + 1 - c], refs[n + a].at[p], (x, y, 1 - c)) for a in range(n) for p in range(NCHIP)]
    return plan


def _scatter_plan2(n):
    def plan(refs, x, y, c):
        return [(refs[a].at[2 * px + py], refs[n + a].at[j], (px, py, c))
                for a in range(n) for j, (px, py) in enumerate(_chips_of(x, y))]
    return plan


def _scatter_start(Gs, name):
    n = len(Gs)
    R1s = [lax.empty((NCHIP,) + g.shape[1:], g.dtype) for g in Gs]
    sems, bufs, tok = _xfer_start(list(Gs) + R1s, NCHIP * n, _scatter_plan1(n), name + "_s1_start")
    return dict(sems=sems, bufs=bufs, tok=tok, n=n)


def _scatter_mid(st, after, my_c, name):
    n = st["n"]
    bufs = _xfer_wait(st["sems"], st["bufs"], _scatter_plan1(n), after, name + "_s1_wait")
    Ps = [_pair_sum(bufs[a], bufs[n + a], my_c, f"{name}_pair_sum{a}") for a in range(n)]
    R2s = [lax.empty((3,) + p.shape[1:], p.dtype) for p in Ps]
    sems, bufs, tok = _xfer_start(Ps + R2s, 3 * n, _scatter_plan2(n), name + "_s2_start")
    return dict(sems=sems, bufs=bufs, tok=tok, n=n)


def _scatter_finish(st, after, name):
    n = st["n"]
    bufs = _xfer_wait(st["sems"], st["bufs"], _scatter_plan2(n), after, name + "_s2_wait")
    return bufs[:n], bufs[n:]


SMALL_ROWS = {"norm1_g": (0, 1), "norm2_g": (1, 1), "sgu_ln_g": (2, 1), "sgu_ln_b": (3, 1), "cfm_conv_b": (4, 1),
              "cfm_ln_g": (5, 1), "cfm_ln_b": (6, 1), "b_sgu": (7, 1), "w_sgu": (8, 128), "b_ada": (136, N_MOD),
              "w_short": (142, SHORT_K), "cfm_conv_w": (145, CFM_K)}
ROWS_PER_LAYER = 176
FINAL_ROW = DEPTH * ROWS_PER_LAYER
PACK_ROWS = 360


def _pack(get, D, layers=tuple(range(DEPTH)), tail=True):
    parts = []
    for l in layers:
        for name, (_, nrows) in SMALL_ROWS.items():
            a = get(name, l)
            parts.append(jnp.zeros((nrows * D,), F32) if a is None else a.astype(F32).reshape(nrows * D))
    if tail:
        for name in ("final_g", "loss"):
            a = get(name, None)
            parts.append(jnp.zeros((D,), F32) if a is None else a.astype(F32).reshape(D))
        parts.append(jnp.zeros(((PACK_ROWS - FINAL_ROW - 2) * D,), F32))
    return jnp.concatenate(parts).reshape(-1, D)


def _unpack(pack, name, shape):
    D = pack.shape[1]
    r0, nrows = SMALL_ROWS[name]
    return jnp.stack([pack[l * ROWS_PER_LAYER + r0:l * ROWS_PER_LAYER + r0 + nrows] for l in range(DEPTH)]).reshape(shape)


def _mm_tiles(S):
    return min(512, S), min(1024, S), min(2048, S)


def kernel(x, c, w_ada, b_ada, norm1_g, w_in, w_short, w_a_out, sgu_ln_g, sgu_ln_b, w_sgu, b_sgu, w_b_out, cfm_conv_w, cfm_conv_b, cfm_ln_g, cfm_ln_b, w_c_out, w_o, norm2_g, w_ffn_in, w_ffn_out, final_g, loss_target, m_w_ada, m_b_ada, m_norm1_g, m_w_in, m_w_short, m_w_a_out, m_sgu_ln_g, m_sgu_ln_b, m_w_sgu, m_b_sgu, m_w_b_out, m_cfm_conv_w, m_cfm_conv_b, m_cfm_ln_g, m_cfm_ln_b, m_w_c_out, m_w_o, m_norm2_g, m_w_ffn_in, m_w_ffn_out, m_final_g, v_w_ada, v_b_ada, v_norm1_g, v_w_in, v_w_short, v_w_a_out, v_sgu_ln_g, v_sgu_ln_b, v_w_sgu, v_b_sgu, v_w_b_out, v_cfm_conv_w, v_cfm_conv_b, v_cfm_ln_g, v_cfm_ln_b, v_w_c_out, v_w_o, v_norm2_g, v_w_ffn_in, v_w_ffn_out, v_final_g):
    W = dict(w_ada=w_ada, b_ada=b_ada, norm1_g=norm1_g, w_in=w_in, w_short=w_short, w_a_out=w_a_out, sgu_ln_g=sgu_ln_g,
             sgu_ln_b=sgu_ln_b, w_sgu=w_sgu, b_sgu=b_sgu, w_b_out=w_b_out, cfm_conv_w=cfm_conv_w, cfm_conv_b=cfm_conv_b,
             cfm_ln_g=cfm_ln_g, cfm_ln_b=cfm_ln_b, w_c_out=w_c_out, w_o=w_o, norm2_g=norm2_g, w_ffn_in=w_ffn_in,
             w_ffn_out=w_ffn_out, final_g=final_g)
    Mo = dict(w_ada=m_w_ada, b_ada=m_b_ada, norm1_g=m_norm1_g, w_in=m_w_in, w_short=m_w_short, w_a_out=m_w_a_out,
              sgu_ln_g=m_sgu_ln_g, sgu_ln_b=m_sgu_ln_b, w_sgu=m_w_sgu, b_sgu=m_b_sgu, w_b_out=m_w_b_out,
              cfm_conv_w=m_cfm_conv_w, cfm_conv_b=m_cfm_conv_b, cfm_ln_g=m_cfm_ln_g, cfm_ln_b=m_cfm_ln_b,
              w_c_out=m_w_c_out, w_o=m_w_o, norm2_g=m_norm2_g, w_ffn_in=m_w_ffn_in, w_ffn_out=m_w_ffn_out,
              final_g=m_final_g)
    Vo = dict(w_ada=v_w_ada, b_ada=v_b_ada, norm1_g=v_norm1_g, w_in=v_w_in, w_short=v_w_short, w_a_out=v_w_a_out,
              sgu_ln_g=v_sgu_ln_g, sgu_ln_b=v_sgu_ln_b, w_sgu=v_w_sgu, b_sgu=v_b_sgu, w_b_out=v_w_b_out,
              cfm_conv_w=v_cfm_conv_w, cfm_conv_b=v_cfm_conv_b, cfm_ln_g=v_cfm_ln_g, cfm_ln_b=v_cfm_ln_b,
              w_c_out=v_w_c_out, w_o=v_w_o, norm2_g=v_norm2_g, w_ffn_in=v_w_ffn_in, w_ffn_out=v_w_ffn_out,
              final_g=v_final_g)
    order = ["w_ada", "b_ada", "norm1_g", "w_in", "w_short", "w_a_out", "sgu_ln_g", "sgu_ln_b", "w_sgu", "b_sgu",
             "w_b_out", "cfm_conv_w", "cfm_conv_b", "cfm_ln_g", "cfm_ln_b", "w_c_out", "w_o", "norm2_g", "w_ffn_in",
             "w_ffn_out", "final_g"]

    assert DEPTH == 2, "the weight-gather schedule below is written for two layers"
    S, D = x.shape[1], x.shape[2]
    F2 = w_ffn_in.shape[2] * NDEV
    FF = F2 // 2
    xi, yi, ci = _place()
    dev = 4 * xi + 2 * yi + ci
    my_c = jnp.reshape(ci, (1,)).astype(jnp.int32)
    my_chip = jnp.reshape(2 * xi + yi, (1,)).astype(jnp.int32)
    tm, tm_big, tm_huge = _mm_tiles(S)
    x0 = x.reshape(S, D)
    tgt = loss_target.reshape(S, D)

    def shards_of(l):
        return [w_in[l].astype(BF16), w_a_out[l].astype(BF16), w_b_out[l].astype(BF16), w_c_out[l].astype(BF16),
                w_o[l].astype(BF16), w_ffn_in[l].astype(BF16), w_ffn_out[l].astype(BF16)]

    c_all = _all_gather([jnp.pad(c, ((0, 7), (0, 0)))], "ag_c")[0][:, 0, :]
    modpart, c_act = _ada_fwd(c_all, w_ada, "ada_fwd")
    ncol = modpart.shape[2]
    mg = _all_gather([modpart.reshape(DEPTH * NDEV, ncol)], "ag_mod")[0].reshape(NDEV, DEPTH, NDEV, ncol)
    mine = lax.dynamic_index_in_dim(mg, dev, axis=2, keepdims=False)
    mod = (jnp.transpose(mine, (1, 0, 2)).reshape(DEPTH, N_MOD * D) + b_ada).reshape(DEPTH, N_MOD, D)

    ncs = w_short.shape[2]
    ag_in0 = _gather_start([w_in[0].astype(BF16), w_short.reshape(DEPTH * SHORT_K, ncs),
                            cfm_conv_w.reshape(DEPTH * CFM_K, ncs)], dev, "ag_w_in0", deps=(mod,))
    W, Mo, Vo = lax.optimization_barrier((ag_in0["tok"], (W, Mo, Vo)))[1]
    (norm1_g, norm2_g, w_in, w_a_out, w_b_out, w_c_out, w_o, w_ffn_in, w_ffn_out, sgu_ln_g, sgu_ln_b, w_sgu, b_sgu,
     cfm_conv_b, cfm_ln_g, cfm_ln_b, final_g) = [W[k] for k in (
         "norm1_g", "norm2_g", "w_in", "w_a_out", "w_b_out", "w_c_out", "w_o", "w_ffn_in", "w_ffn_out", "sgu_ln_g",
         "sgu_ln_b", "w_sgu", "b_sgu", "cfm_conv_b", "cfm_ln_g", "cfm_ln_b", "final_g")]
    m_w_ada, v_w_ada = Mo["w_ada"], Vo["w_ada"]
    xl0, h0, ht0 = _norm_fwd(x0, None, _rows(jnp.zeros((D,), F32), norm1_g[0], mod[0, 1], mod[0, 0]), "norm1_fwd0",
                             deps=(ag_in0["tok"],))
    ag_rest0 = _gather_start(shards_of(0)[1:], dev, "ag_rest0", deps=(h0,))

    tril = jnp.tril(jnp.ones((CHUNK, CHUNK), dtype=bool))

    def layer_consts(l):
        wt = jnp.where(tril[None], w_sgu[l], 0.0).astype(BF16)
        return dict(sgu_ln=_rows(sgu_ln_g[l], sgu_ln_b[l]), wtril=wt, wtril_t=jnp.swapaxes(wt, 1, 2),
                    bias_full=jnp.repeat(b_sgu[l].T, LANE, axis=1), cvec=_rows(cfm_conv_b[l], cfm_ln_g[l], cfm_ln_b[l]))

    def rest_of(g):
        return dict(w_a=g[0].reshape(1, D, D), w_b=g[1].reshape(1, D, D), w_c=g[2].reshape(1, D, D),
                    w_o=g[3].reshape(1, D, D), w_fi=jnp.transpose(g[4], (1, 0, 2)).reshape(1, D, F2),
                    w_fo=g[5].reshape(1, FF, D))

    sharded_small = ("w_short", "cfm_conv_w")

    def param_get(T):
        def get(name, l):
            if name == "final_g":
                return T[name]
            return None if name in sharded_small or name == "loss" else T[name][l]
        return get

    packs = [_pack(param_get(T), D) for T in (W, Mo, Vo)]
    ag_in0 = _gather_mid(ag_in0, [ag_rest0["tok"], *packs], "ag_w_in0")
    (w_sgu, b_sgu, sgu_ln_g, sgu_ln_b, cfm_conv_b, cfm_ln_g, cfm_ln_b), conv_wmv_in = lax.optimization_barrier(
        (ag_in0["tok"], ((w_sgu, b_sgu, sgu_ln_g, sgu_ln_b, cfm_conv_b, cfm_ln_g, cfm_ln_b),
                         [(T["w_short"], T["cfm_conv_w"]) for T in (W, Mo, Vo)])))[1]
    consts = [layer_consts(l) for l in range(DEPTH)]
    ncr = DEPTH * (SHORT_K + CFM_K)
    padr = (-ncr) % 8
    convw_wmv = [jnp.pad(jnp.concatenate([a.reshape(-1, ncs), b.reshape(-1, ncs)]), ((0, padr), (0, 0)))
                 for a, b in conv_wmv_in]
    g_in0 = _gather_finish(ag_in0, [*convw_wmv] + [a for cl in consts for a in cl.values()], "ag_w_in0")
    w_short_full = jnp.transpose(g_in0[1], (1, 0, 2)).reshape(DEPTH, SHORT_K, D)
    cfm_w_full = jnp.transpose(g_in0[2], (1, 0, 2)).reshape(DEPTH, CFM_K, D)
    for l in range(DEPTH):
        consts[l]["wsh"] = jnp.pad(w_short_full[l], ((0, 8 - SHORT_K), (0, 0)))
        consts[l]["cw"] = jnp.pad(cfm_w_full[l], ((0, HALO - CFM_K), (0, 0)))
    Wg = [dict(w_in=g_in0[0]), None]
    ag_l1 = None
    nin = w_in.shape[2]
    tn_in = nin if nin % 256 == 0 and nin <= 1280 else 256
    tn_fi = 512 if F2 % 512 == 0 else 256
    tn_dw = min(256, D)

    saved = []
    xcur, fprev, gprev = x0, None, None
    for l in range(DEPTH):
        sh1, sc1, g1, sh2, sc2, g2 = [mod[l, k] for k in range(N_MOD)]
        cl = consts[l]
        if l == 0:
            xl, h, ht = xl0, h0, ht0
        else:
            vec1 = _rows(gprev, norm1_g[l], sc1, sh1)
            ag_l1 = _gather_mid(ag_l1, fprev, f"ag_w{l}")
            xl, h, ht = _norm_fwd(xcur, fprev, vec1, f"norm1_fwd{l}", deps=(ag_l1["tok"],))
            g = _gather_finish(ag_l1, h, f"ag_w{l}")
            Wg[l] = dict(w_in=g[0], **rest_of(g[1:]))
        wl = Wg[l]
        z = _mm_nn(h, wl["w_in"], BF16, tm_huge, tn_in, D, f"mm_in{l}", w_outer=True)
        mix_deps = ()
        if l == 0:
            ag_rest0 = _gather_mid(ag_rest0, z, "ag_rest0")
            mix_deps = (ag_rest0["tok"],)
            if DEPTH > 1:
                ag_l1 = _gather_start(shards_of(1), dev, "ag_w1")
                mix_deps += (ag_l1["tok"],)
        acts, acts_t, conv = _mixer_fwd(z, cl["wsh"], cl["sgu_ln"], cl["wtril"], cl["bias_full"], cl["cw"], cl["cvec"],
                                        f"mixer_fwd{l}", deps=mix_deps)
        if l == 0:
            wl.update(rest_of(_gather_finish(ag_rest0, acts[0], "ag_rest0")))
        merged, merged_t, ys = _branch_out(acts, [wl["w_a"][0], wl["w_b"][0], wl["w_c"][0]], z, f"branch_out{l}")
        o = _mm_nn(merged, wl["w_o"], F32, tm_big, D, D, f"mm_o{l}")
        x1, h2, h2t = _norm_fwd(xl, o, _rows(g1, norm2_g[l], sc2, sh2), f"norm2_fwd{l}")
        gu, act, act_t = _ffn_in_swiglu(h2, wl["w_fi"], tm_huge, 256, f"mm_ffn_in{l}")
        f = _mm_nn(act, wl["w_fo"], F32, tm_big, D, FF, f"mm_ffn_out{l}")
        saved.append(dict(xl=xl, ht=ht, z=z, acts_t=acts_t, conv=conv, ys=ys, merged_t=merged_t, o=o, x1=x1, h2t=h2t, gu=gu,
                          act_t=act_t, f=f, consts=cl, mod=(sh1, sc1, g1, sh2, sc2, g2)))
        xcur, fprev, gprev = x1, f, g2

    last = saved[-1]
    dxup, dfb, fsums, loss_blk = _final_bwd(last["x1"], last["f"], tgt, _rows(last["mod"][5], final_g), "final_bwd")
    loss_row = jnp.pad(loss_blk[0, 0:1], (0, D - 1))
    dgate2_next = fsums[1]
    small = [dict() for _ in range(DEPTH)]
    dmods = [None] * DEPTH
    nfi = w_ffn_in.shape[2]
    early_names, late_names = ["w_ffn_out", "w_ffn_in", "w_o"], ["w_a_out", "w_b_out", "w_c_out", "w_in"]
    results = {n: None for n in early_names + late_names}

    def adam_group(names, Ps, R2s, l, deps=()):
        for n, p, r2 in zip(names, Ps, R2s):
            results[n] = _adam_big(p, r2, my_chip, W[n], Mo[n], Vo[n], l, results[n], f"adam_{n}{l}", deps)

    deferred = []
    late_prev = None
    ag_s1, gathered1 = None, None
    tk_w = min(2048, S)
    tn_dw_in = tn_in // 2 if tn_in == 1280 else tn_in
    for l in reversed(range(DEPTH)):
        sv, wl, cl = saved[l], Wg[l], saved[l]["consts"]
        sh1, sc1, g1, sh2, sc2, g2 = sv["mod"]
        dact = _mm_nt(dfb, wl["w_fo"], BF16, tm_big, FF, D, f"mm_dact{l}",
                      deps=() if late_prev is None else (late_prev["tok"], ag_s1["tok"]))
        g_fo = _mm_wgrad(sv["act_t"], dfb, 1, FF // 2, D, tk_w, f"mm_dw_ffn_out{l}")
        dgu = _swiglu_bwd(dact, sv["gu"], f"swiglu_bwd{l}")
        dh2 = _mm_nt(dgu, wl["w_fi"], F32, tm, D, F2, f"mm_dh2{l}")
        if late_prev is not None:
            deferred.append((late_names, *_scatter_finish(late_prev, dh2, f"rs_late{l + 1}"), l + 1))
            late_prev = None
        g_fi = _mm_wgrad(sv["h2t"], dgu, 1, D, tn_fi, S, f"mm_dw_ffn_in{l}")
        if ag_s1 is not None:
            ag_s1 = _gather_mid(ag_s1, g_fi, "ag_small1")
        dx1, dob, s2 = _norm_bwd(sv["x1"], dh2, dxup, _rows(norm2_g[l], sc2, g1), sv["o"], f"norm2_bwd{l}",
                                 deps=() if ag_s1 is None else (ag_s1["tok"],))
        dmerged = _mm_nt(dob, wl["w_o"], BF16, tm_big, D, D, f"mm_dmerged{l}")
        g_o = _mm_wgrad(sv["merged_t"], dob, 1, D, tn_dw, S, f"mm_dw_o{l}")
        early = _scatter_start([g_fo.reshape(NDEV, FF // NDEV, D),
                                jnp.transpose(g_fi.reshape(D, NDEV, nfi), (1, 0, 2)),
                                g_o.reshape(NDEV, D // NDEV, D)], f"rs_early{l}")
        dys, dz = _gate_bwd(dmerged, sv["z"], sv["ys"], f"gate_bwd{l}", deps=(early["tok"],))
        if ag_s1 is not None:
            gathered1 = _gather_finish(ag_s1, dys[0], "ag_small1")[0]
            ag_s1 = None
        early = _scatter_mid(early, dys[0], my_c, f"rs_early{l}")
        dacts, g_abc = [], []
        for n, key in enumerate(("w_a", "w_b", "w_c")):
            dacts.append(_mm_nt(dys[n], wl[key], BF16, tm_big, D, D, f"mm_dact_{key}{l}",
                                deps=(early["tok"],) if n == 0 else ()))
            g_abc.append(_mm_wgrad(sv["acts_t"][n], dys[n], 1, D, tn_dw, S, f"mm_d{key}{l}"))
        dz, mvec, dcw, dws, dbs = _mixer_bwd(sv["z"], dacts, sv["conv"], dz, cl["wsh"], cl["sgu_ln"], cl["wtril"],
                                             cl["wtril_t"], cl["bias_full"], cl["cw"], cl["cvec"], f"mixer_bwd{l}")
        dh = _mm_nt(dz, wl["w_in"], F32, tm_big, D, tn_in, f"mm_dh{l}",
                    blocks_per_step=2 if (tn_in == nin and wl["w_in"].shape[0] % 2 == 0) else 1)
        g_in = _mm_wgrad(sv["ht"], dz, NDEV, D, tn_dw_in, S, f"mm_dw_in{l}")
        late = _scatter_start([g.reshape(NDEV, D // NDEV, D) for g in g_abc] + [g_in], f"rs_late{l}")
        if l > 0:
            pv = saved[l - 1]
            dxup, dfb, s1 = _norm_bwd(sv["xl"], dh, dx1, _rows(norm1_g[l], sc1, pv["mod"][5]), pv["f"], f"norm1_bwd{l}",
                                      deps=(late["tok"],))
        else:
            dxup, dfb, s1 = _norm_bwd(sv["xl"], dh, dx1, _rows(norm1_g[l], sc1), None, f"norm1_bwd{l}", deps=(late["tok"],))
        deferred.append((early_names, *_scatter_finish(early, dxup, f"rs_early{l}"), l))
        dmods[l] = jnp.stack([s1[0], s1[1], s2[3], s2[0], s2[1], dgate2_next])
        dgate2_next = s1[3]
        small[l] = dict(norm1_g=s1[2], norm2_g=s2[2], sgu_ln_g=mvec[3], sgu_ln_b=mvec[4], cfm_conv_b=mvec[5],
                        cfm_ln_g=mvec[6], cfm_ln_b=mvec[7], b_sgu=dbs[:, :, 0],
                        w_sgu=jnp.where(tril[None], dws, 0.0), b_ada=dmods[l], w_short=mvec[0:SHORT_K],
                        cfm_conv_w=dcw[0:CFM_K])
        small_get = lambda name, k: {"final_g": fsums[0], "loss": loss_row}.get(name) if k is None else small[k][name]
        if l > 0:
            late_prev = _scatter_mid(late, dxup, my_c, f"rs_late{l}")
            ag_s1 = _gather_start([_pack(small_get, D, layers=(l,), tail=True)], dev, "ag_small1", deps=(late_prev["tok"],))
    grad_x = dxup.reshape(x.shape)

    gathered0 = _all_gather([_pack(small_get, D, layers=(0,), tail=False)], "ag_small0", deps=(dxup,))[0]
    late_prev = _scatter_mid(late, gathered0, my_c, "rs_late0")
    gathered = jnp.concatenate([gathered0, gathered1], axis=1)
    sg, sd, sm, sv_ = _adam_small(gathered, *packs, name="adam_small", deps=(late_prev["tok"],))
    loss = sg[FINAL_ROW + 1, 0]
    out = {}
    for name in order:
        if name in SMALL_ROWS and name not in sharded_small:
            out[name] = tuple(_unpack(p, name, W[name].shape) for p in (sg, sd, sm, sv_))
    out["final_g"] = tuple(p[FINAL_ROW] for p in (sg, sd, sm, sv_))

    def my_cols(name):
        full = _unpack(sg, name, (DEPTH, SMALL_ROWS[name][1], D))
        return lax.dynamic_slice_in_dim(full, dev * ncs, ncs, axis=2)

    gcs = jnp.concatenate([my_cols("w_short").reshape(-1, ncs), my_cols("cfm_conv_w").reshape(-1, ncs)])
    cd, cm, cv = _adam_plain(jnp.pad(gcs, ((0, padr), (0, 0))), *convw_wmv, "adam_convw")
    nsh = DEPTH * SHORT_K
    out["w_short"] = tuple(a[0:nsh].reshape(w_short.shape) for a in (gcs, cd, cm, cv))
    out["cfm_conv_w"] = tuple(a[nsh:ncr].reshape(cfm_conv_w.shape) for a in (gcs, cd, cm, cv))

    dm_all = jnp.stack([gathered[:, l * ROWS_PER_LAYER + 136:l * ROWS_PER_LAYER + 136 + N_MOD, :].reshape(NDEV, N_MOD * D)
                        for l in range(DEPTH)])
    dm_mine = lax.dynamic_slice_in_dim(dm_all, dev * ncol, ncol, axis=2)
    out["w_ada"] = tuple(_adam_ada(jnp.transpose(c_act), dm_mine, w_ada, m_w_ada, v_w_ada, "adam_ada"))

    for names, Ps, R2s, l in deferred:
        adam_group(names, Ps, R2s, l, deps=(late_prev["tok"],))
    adam_group(late_names, *_scatter_finish(late_prev, results["w_o"][0], "rs_late0"), 0)
    for n in early_names + late_names:
        out[n] = tuple(results[n])

    grads = [out[n][0] for n in order]
    deltas = [out[n][1] for n in order]
    new_m = [out[n][2] for n in order]
    new_v = [out[n][3] for n in order]
    return (loss, grad_x, *grads, *deltas, *new_m, *new_v)
```

```python
import functools
import math

import jax
import jax.numpy as jnp
from jax import lax
from jax.experimental import pallas as pl
from jax.experimental.pallas import tpu as pltpu

F32, BF16 = jnp.float32, jnp.bfloat16
NDEV = 8
NCHIP = NDEV // 2
DEPTH = 2
EPS = 1e-6
CHUNK = 128
NG = 8
SHORT_K = 3
CFM_K = 31
HALO = 32
N_MOD = 6
LANE = 128
VMEM_LIMIT = 56 * 1024 * 1024
ADAM_LR, ADAM_B1, ADAM_B2, ADAM_EPS, ADAM_WD, ADAM_STEP = 0.001, 0.9, 0.999, 1e-08, 0.01, 10
_G0 = math.sqrt(2.0 / math.pi)
_G1 = 0.044715
MESH = pl.DeviceIdType.MESH
ANY = pl.BlockSpec(memory_space=pl.ANY)


def _pcall(body, **kw):
    return pl.pallas_call(body, **kw)


def _params(sem=None):
    return pltpu.CompilerParams(dimension_semantics=sem, vmem_limit_bytes=VMEM_LIMIT)


def _sds(shape, dtype):
    return jax.ShapeDtypeStruct(tuple(shape), dtype)


def _mm_body(dims, nk, out_f32, blocks=1):
    def body(a_ref, b_ref, o_ref, *scr):
        k = pl.program_id(2)
        if blocks == 1:
            part = lax.dot_general(a_ref[...], b_ref[...], dims, preferred_element_type=F32)
        else:
            w = a_ref.shape[1] // blocks
            part = None
            for g in range(blocks):
                t = lax.dot_general(a_ref[:, g * w:(g + 1) * w], b_ref[g], dims, preferred_element_type=F32)
                part = t if part is None else part + t
        if nk == 1:
            o_ref[...] = part.reshape(o_ref.shape).astype(o_ref.dtype)
        elif out_f32:
            @pl.when(k == 0)
            def _():
                o_ref[...] = part.reshape(o_ref.shape)

            @pl.when(k > 0)
            def _():
                o_ref[...] += part.reshape(o_ref.shape)
        else:
            acc = scr[0]

            @pl.when(k == 0)
            def _():
                acc[...] = part

            @pl.when(k > 0)
            def _():
                acc[...] += part

            @pl.when(k == nk - 1)
            def _():
                o_ref[...] = acc[...].astype(o_ref.dtype)
    return body


def _after(body, n_in, deps):
    nd = len(deps)
    if nd == 0:
        return body

    def ordered(*refs):
        return body(*refs[:n_in], *refs[n_in + nd:])
    return ordered


def _mm_call(body, grid, in_specs, out_spec, out_shape, acc_shape, name, deps=()):
    scratch = [] if acc_shape is None else [pltpu.VMEM(acc_shape, F32)]
    return _pcall(_after(body, 2, deps), grid=grid, in_specs=in_specs + [ANY] * len(deps), out_specs=out_spec,
                  out_shape=out_shape, scratch_shapes=scratch, name=name,
                  compiler_params=_params(("parallel", "parallel", "arbitrary")))


def _mm_nn(a, b3, out_dtype, tm, tn, tk, name, w_outer=False, deps=()):
    M, K = a.shape
    G, _, Nb = b3.shape
    npb, nk = Nb // tn, K // tk
    out_f32 = out_dtype == F32
    body = _mm_body((((1,), (0,)), ((), ())), nk, out_f32)
    if w_outer:
        grid = (G * npb, M // tm, nk)
        ij = lambda p, q: (q, p)
    else:
        grid = (M // tm, G * npb, nk)
        ij = lambda p, q: (p, q)

    def a_map(p, q, k):
        i, j = ij(p, q)
        return (i, k)

    def b_map(p, q, k):
        i, j = ij(p, q)
        return (j // npb, k, j % npb)

    def o_map(p, q, k):
        return ij(p, q)

    def wrapped(a_ref, b_ref, o_ref, *scr):
        body(a_ref, b_ref, o_ref, *scr)

    return _mm_call(wrapped, grid, [pl.BlockSpec((tm, tk), a_map), pl.BlockSpec((None, tk, tn), b_map)],
                    pl.BlockSpec((tm, tn), o_map), _sds((M, G * Nb), out_dtype),
                    None if (nk == 1 or out_f32) else (tm, tn), name, deps)(a, b3, *deps)


def _mm_nt(a, b3, out_dtype, tm, tn, tk, name, deps=(), blocks_per_step=1):
    M, _ = a.shape
    G, Ko, Nb = b3.shape
    kpb = Nb // tk
    nk = G * kpb // blocks_per_step
    out_f32 = out_dtype == F32
    body = _mm_body((((1,), (1,)), ((), ())), nk, out_f32, blocks_per_step)

    def wrapped(a_ref, b_ref, o_ref, *scr):
        body(a_ref, b_ref, o_ref, *scr)

    if blocks_per_step > 1:
        assert tk == Nb and G % blocks_per_step == 0
        b_spec = pl.BlockSpec((blocks_per_step, tn, tk), lambda i, j, k: (k, j, 0))
    else:
        b_spec = pl.BlockSpec((None, tn, tk), lambda i, j, k: (k // kpb, j, k % kpb))
    return _mm_call(wrapped, (M // tm, Ko // tn, nk),
                    [pl.BlockSpec((tm, tk * blocks_per_step), lambda i, j, k: (i, k)), b_spec],
                    pl.BlockSpec((tm, tn), lambda i, j, k: (i, j)), _sds((M, Ko), out_dtype),
                    None if (nk == 1 or out_f32) else (tm, tn), name, deps)(a, b3, *deps)


def _mm_wgrad(at, b, G, tm, tn, tk, name, deps=()):
    M, T = at.shape
    Nb = b.shape[1] // G
    npb, nk = Nb // tn, T // tk
    body = _mm_body((((1,), (0,)), ((), ())), nk, False)

    def wrapped(a_ref, b_ref, o_ref, *scr):
        body(a_ref, b_ref, o_ref, *scr)

    a = at
    in_specs = [pl.BlockSpec((tm, tk), lambda i, j, k: (i, k)), pl.BlockSpec((tk, tn), lambda i, j, k: (k, j))]
    out_spec = pl.BlockSpec((None, tm, tn), lambda i, j, k: (j // npb, i, j % npb))
    return _mm_call(wrapped, (M // tm, G * npb, nk), in_specs, out_spec, _sds((G, M, Nb), BF16),
                    None if nk == 1 else (tm, tn), name, deps)(a, b, *deps)


def _mm3_nt(x3, ws, tm, name, deps=()):
    nb, S, K = x3.shape
    Ko = ws[0].shape[1]

    def body(x_ref, w0, w1, w2, o_ref):
        n = pl.program_id(0)
        for k, w in enumerate((w0, w1, w2)):
            @pl.when(n == k)
            def _(w=w):
                o_ref[...] = lax.dot_general(x_ref[...], w[...], (((1,), (1,)), ((), ())),
                                             preferred_element_type=F32).astype(BF16)

    wspec = pl.BlockSpec((None, Ko, K), lambda n, i: (0, 0, 0))
    return _pcall(_after(body, 4, deps), grid=(nb, S // tm),
                  in_specs=[pl.BlockSpec((None, tm, K), lambda n, i: (n, i, 0)), wspec, wspec, wspec] + [ANY] * len(deps),
                  out_specs=pl.BlockSpec((None, tm, Ko), lambda n, i: (n, i, 0)), out_shape=_sds((nb, S, Ko), BF16),
                  name=name, compiler_params=_params(("arbitrary", "parallel")))(x3, *ws, *deps)


def _mm3_wgrad(at3, b3, tn, name):
    nb, M, T = at3.shape
    N = b3.shape[2]

    def body(a_ref, b_ref, o_ref):
        o_ref[...] = jnp.dot(a_ref[...], b_ref[...], preferred_element_type=F32).astype(BF16)

    return _pcall(body, grid=(nb, N // tn),
                  in_specs=[pl.BlockSpec((None, M, T), lambda n, j: (n, 0, 0)), pl.BlockSpec((None, T, tn), lambda n, j: (n, 0, j))],
                  out_specs=pl.BlockSpec((None, M, tn), lambda n, j: (n, 0, j)), out_shape=_sds((nb, M, N), BF16),
                  name=name, compiler_params=_params(("arbitrary", "parallel")))(at3, b3)


def _rsum(v):
    return jnp.sum(v, axis=0, keepdims=True)


def _rmean(v):
    return jnp.mean(v, axis=-1, keepdims=True)


def _gelu(x):
    t = jnp.tanh(_G0 * (x + _G1 * (x * x * x)))
    return x * (0.5 * (1.0 + t)), t


def _dgelu(x, t):
    return 0.5 * (1.0 + t) + 0.5 * x * (1.0 - t * t) * (_G0 * (1.0 + 3.0 * _G1 * (x * x)))


def _sigmoid(x):
    return 1.0 / (1.0 + jnp.exp(-x))


def _fill_shifted(ext, rot):
    v = ext[...]
    n = v.shape[0]
    for b in range(1, 8):
        rot[b - 1] = pltpu.roll(v, n - b, 0)


def _rows_at(ext, rot, s, tm, cs=slice(None)):
    a, b = divmod(s, 8)
    return ext[8 * a:8 * a + tm, cs] if b == 0 else rot[b - 1, 8 * a:8 * a + tm, cs]


def _causal_conv(w_ref, taps, bias, ext, rot, offset, tm, out):
    D = out.shape[1]
    for cb in range(D // LANE):
        cs = slice(cb * LANE, (cb + 1) * LANE)
        acc = None
        for k, o in zip(taps, offset):
            term = w_ref[k:k + 1, cs] * _rows_at(ext, rot, o, tm, cs)
            acc = term if acc is None else acc + term
        out[:, cs] = acc if bias is None else acc + bias[:, cs]


def _rows(*vs):
    a = jnp.stack([v.astype(F32) for v in vs])
    return jnp.pad(a, ((0, 8 - len(vs)), (0, 0)))


def _row_spec(tm, D):
    return pl.BlockSpec((tm, D), lambda i: (i, 0))


def _const_spec(shape):
    nd = len(shape)
    return pl.BlockSpec(shape, lambda i: (0,) * nd)


def _norm_fwd(xp, f, vec, name, deps=()):
    S, D = xp.shape
    tm = min(512, S)
    has_f = f is not None

    def body(*refs):
        if has_f:
            xp_ref, f_ref, vec_ref, xo_ref, h_ref, ht_ref = refs
            x = xp_ref[...] + vec_ref[0:1, :] * f_ref[...]
            xo_ref[...] = x
        else:
            xp_ref, vec_ref, h_ref, ht_ref = refs
            x = xp_ref[...]
        r = lax.rsqrt(_rmean(x * x) + EPS)
        h = (x * r) * vec_ref[1:2, :]
        h = h * (1.0 + vec_ref[2:3, :]) + vec_ref[3:4, :]
        h_ref[...] = h.astype(BF16)
        ht_ref[...] = h.T.astype(BF16)

    rs = _row_spec(tm, D)
    ins = [xp, f, vec] if has_f else [xp, vec]
    in_specs = ([rs, rs] if has_f else [rs]) + [_const_spec((8, D))]
    out_shape = ([_sds((S, D), F32)] if has_f else []) + [_sds((S, D), BF16), _sds((D, S), BF16)]
    out_specs = [rs] * (len(out_shape) - 1) + [pl.BlockSpec((D, tm), lambda i: (0, i))]
    outs = _pcall(_after(body, len(ins), deps), grid=(S // tm,), in_specs=in_specs + [ANY] * len(deps),
                  out_specs=out_specs, out_shape=out_shape, name=name,
                  compiler_params=_params(("parallel",)))(*ins, *deps)
    return (outs[0], outs[1], outs[2]) if has_f else (xp, outs[0], outs[1])


def _mixer_fwd(z, wsh, sgu_ln, wtril, bias_full, cw, cvec, name, deps=()):
    S = z.shape[0]
    D = wsh.shape[1]
    tm = CHUNK

    def body(z_ref, wsh_ref, sln_ref, wt_ref, bias_ref, cw_ref, cv_ref, oa_ref, ob_ref, oc_ref, t_ref,
             conv_ref, pe, ge, gr, cbuf):
        i = pl.program_id(0)

        @pl.when(i == 0)
        def _():
            pe[0:HALO, :] = jnp.zeros((HALO, D), F32)
            ge[0:HALO, :] = jnp.zeros((HALO, D), F32)

        def col(n):
            return z_ref[:, n * D:(n + 1) * D].astype(F32)

        pe[HALO:HALO + tm, :] = col(1) * col(2)
        q = wsh_ref[0:1, :] * pe[HALO - 2:HALO - 2 + tm, :]
        q = q + wsh_ref[1:2, :] * pe[HALO - 1:HALO - 1 + tm, :]
        q = q + wsh_ref[2:3, :] * pe[HALO:HALO + tm, :]
        act_a = col(0) * q
        oa_ref[...] = act_a.astype(BF16)
        t_ref[0] = act_a.T.astype(BF16)
        gu, _ = _gelu(col(3))
        gv, _ = _gelu(col(4))
        d = gv - _rmean(gv)
        nrm = d * lax.rsqrt(_rmean(d * d) + EPS)
        vnb = (nrm * sln_ref[0:1, :] + sln_ref[1:2, :]).astype(BF16)
        for g in range(NG):
            cs = slice(g * LANE, (g + 1) * LANE)
            mixed = jnp.dot(wt_ref[g], vnb[:, cs], preferred_element_type=F32) + bias_ref[:, cs]
            act_b = gu[:, cs] * mixed
            ob_ref[:, cs] = act_b.astype(BF16)
            t_ref[1, cs, :] = act_b.T.astype(BF16)
        ge[HALO:HALO + tm, :] = col(5) * _sigmoid(col(6))
        _fill_shifted(ge, gr)
        o0 = HALO - (CFM_K - 1)
        _causal_conv(cw_ref, range(CFM_K), cv_ref[0:1, :], ge, gr, range(o0, o0 + CFM_K), tm, cbuf)
        conv = cbuf[...]
        conv_ref[...] = conv.astype(BF16)
        d = conv - _rmean(conv)
        ln = (d * lax.rsqrt(_rmean(d * d) + EPS)) * cv_ref[1:2, :] + cv_ref[2:3, :]
        act_c = ln * _sigmoid(ln)
        oc_ref[...] = act_c.astype(BF16)
        t_ref[2] = act_c.T.astype(BF16)
        pe[0:HALO, :] = pe[tm:tm + HALO, :]
        ge[0:HALO, :] = ge[tm:tm + HALO, :]

    rs = _row_spec(tm, D)
    outs = _pcall(
        _after(body, 7, deps), grid=(S // tm,),
        in_specs=[pl.BlockSpec((tm, 7 * D), lambda i: (i, 0)), _const_spec((8, D)), _const_spec((8, D)),
                  _const_spec((NG, CHUNK, CHUNK)), _const_spec((CHUNK, D)), _const_spec((HALO, D)), _const_spec((8, D))]
        + [ANY] * len(deps),
        out_specs=[rs, rs, rs, pl.BlockSpec((3, D, tm), lambda i: (0, 0, i)), rs],
        out_shape=[_sds((S, D), BF16)] * 3 + [_sds((3, D, S), BF16), _sds((S, D), BF16)],
        scratch_shapes=[pltpu.VMEM((HALO + tm, D), F32), pltpu.VMEM((HALO + tm, D), F32),
                        pltpu.VMEM((7, HALO + tm, D), F32), pltpu.VMEM((tm, D), F32)],
        name=name, compiler_params=_params(("arbitrary",)))(z, wsh, sgu_ln, wtril, bias_full, cw, cvec, *deps)
    return outs[:3], outs[3], outs[4]


def _branch_out(acts, ws, z, name):
    S, D = acts[0].shape
    tm = min(512, S)

    def body(a0, a1, a2, w0, w1, w2, g0, g1, g2, m_ref, mt_ref, y_ref):
        m = None
        for n, (a, w, g) in enumerate(((a0, w0, g0), (a1, w1, g1), (a2, w2, g2))):
            y = jnp.dot(a[...], w[...], preferred_element_type=F32)
            y_ref[n] = y.astype(BF16)
            t = _sigmoid(g[...].astype(F32)) * y
            m = t if m is None else m + t
        m_ref[...] = m.astype(BF16)
        mt_ref[...] = m.T.astype(BF16)

    rs = _row_spec(tm, D)
    gate_specs = [pl.BlockSpec((tm, D), functools.partial(lambda i, n: (i, 7 + n), n=n)) for n in range(3)]
    return _pcall(body, grid=(S // tm,),
                  in_specs=[rs, rs, rs] + [_const_spec((D, D))] * 3 + gate_specs,
                  out_specs=[rs, pl.BlockSpec((D, tm), lambda i: (0, i)), pl.BlockSpec((3, tm, D), lambda i: (0, i, 0))],
                  out_shape=[_sds((S, D), BF16), _sds((D, S), BF16), _sds((3, S, D), BF16)], name=name,
                  compiler_params=_params(("parallel",)))(*acts, *ws, z, z, z)


def _ffn_in_swiglu(h2, w3, tm, tn, name):
    S, D = h2.shape
    F = w3.shape[2] // 2
    nj = F // tn

    def body(a_ref, wg_ref, wu_ref, gu_ref, act_ref, actt_ref):
        a = a_ref[...]
        g = jnp.dot(a, wg_ref[...], preferred_element_type=F32)
        u = jnp.dot(a, wu_ref[...], preferred_element_type=F32)
        gu_ref[0] = g.astype(BF16)
        gu_ref[1] = u.astype(BF16)
        act = (g * _sigmoid(g)) * u
        act_ref[...] = act.astype(BF16)
        actt_ref[...] = act.T.astype(BF16)

    return _pcall(body, grid=(S // tm, nj),
                  in_specs=[pl.BlockSpec((tm, D), lambda i, j: (i, 0)), pl.BlockSpec((None, D, tn), lambda i, j: (0, 0, j)),
                            pl.BlockSpec((None, D, tn), lambda i, j: (0, 0, j + nj))],
                  out_specs=[pl.BlockSpec((2, tm, tn), lambda i, j: (0, i, j)), pl.BlockSpec((tm, tn), lambda i, j: (i, j)),
                             pl.BlockSpec((tn, tm), lambda i, j: (j, i))],
                  out_shape=[_sds((2, S, F), BF16), _sds((S, F), BF16), _sds((F, S), BF16)], name=name,
                  compiler_params=_params(("parallel", "parallel")))(h2, w3, w3)


def _swiglu_bwd(dact, gu, name):
    _, S, F = gu.shape
    F2 = 2 * F
    tm = min(256, S)

    def body(d_ref, g_ref, u_ref, o_ref):
        g = g_ref[...].astype(F32)
        sg = _sigmoid(g)
        d = d_ref[...].astype(F32)
        o_ref[:, 0:F] = (d * u_ref[...].astype(F32) * (sg * (1.0 + g * (1.0 - sg)))).astype(BF16)
        o_ref[:, F:2 * F] = (d * (g * sg)).astype(BF16)

    return _pcall(body, grid=(S // tm,),
                  in_specs=[pl.BlockSpec((tm, F), lambda i: (i, 0)), pl.BlockSpec((None, tm, F), lambda i: (0, i, 0)),
                            pl.BlockSpec((None, tm, F), lambda i: (1, i, 0))],
                  out_specs=pl.BlockSpec((tm, F2), lambda i: (i, 0)), out_shape=_sds((S, F2), BF16), name=name,
                  compiler_params=_params(("parallel",)))(dact, gu, gu)


def _final_bwd(x1, f, tgt, vec, name):
    S, D = x1.shape
    tm = min(512, S)

    def body(x_ref, f_ref, t_ref, vec_ref, dx_ref, df_ref, sums_ref, loss_ref):
        @pl.when(pl.program_id(0) == 0)
        def _():
            sums_ref[...] = jnp.zeros_like(sums_ref)
            loss_ref[...] = jnp.zeros_like(loss_ref)

        gate, fg = vec_ref[0:1, :], vec_ref[1:2, :]
        fv = f_ref[...]
        x = x_ref[...] + gate * fv
        r = lax.rsqrt(_rmean(x * x) + EPS)
        xn = x * r
        diff = xn * fg - t_ref[...]
        per_tok = _rmean(diff * diff)
        loss_ref[...] += 0.5 * jnp.sum(per_tok, axis=0, keepdims=True)
        dy = diff * (1.0 / D)
        sums_ref[0:1, :] += _rsum(dy * xn)
        dxn = dy * fg
        dx = r * (dxn - xn * _rmean(dxn * xn))
        sums_ref[1:2, :] += _rsum(dx * fv)
        dx_ref[...] = dx
        df_ref[...] = (dx * gate).astype(BF16)

    rs = _row_spec(tm, D)
    return _pcall(body, grid=(S // tm,), in_specs=[rs, rs, rs, _const_spec((8, D))],
                  out_specs=[rs, rs, _const_spec((8, D)), _const_spec((8, LANE))],
                  out_shape=[_sds((S, D), F32), _sds((S, D), BF16), _sds((8, D), F32), _sds((8, LANE), F32)],
                  name=name, compiler_params=_params(("arbitrary",)))(x1, f, tgt, vec)


def _norm_bwd(xin, dh, dxup, vec, fprev, name, deps=()):
    S, D = xin.shape
    tm = min(512, S)
    has_prev = fprev is not None

    def body(*refs):
        if has_prev:
            x_ref, dh_ref, up_ref, vec_ref, fp_ref, dx_ref, dp_ref, sums_ref = refs
        else:
            x_ref, dh_ref, up_ref, vec_ref, dx_ref, sums_ref = refs

        @pl.when(pl.program_id(0) == 0)
        def _():
            sums_ref[...] = jnp.zeros_like(sums_ref)

        g, scale = vec_ref[0:1, :], vec_ref[1:2, :]
        x = x_ref[...]
        r = lax.rsqrt(_rmean(x * x) + EPS)
        xn = x * r
        dhv = dh_ref[...]
        sums_ref[0:1, :] += _rsum(dhv)
        sums_ref[1:2, :] += _rsum(dhv * (xn * g))
        dm = dhv * (1.0 + scale)
        sums_ref[2:3, :] += _rsum(dm * xn)
        dxn = dm * g
        dx = up_ref[...] + r * (dxn - xn * _rmean(dxn * xn))
        dx_ref[...] = dx
        if has_prev:
            sums_ref[3:4, :] += _rsum(dx * fp_ref[...])
            dp_ref[...] = (dx * vec_ref[2:3, :]).astype(BF16)

    rs = _row_spec(tm, D)
    ins = [xin, dh, dxup, vec] + ([fprev] if has_prev else [])
    in_specs = [rs, rs, rs, _const_spec((8, D))] + ([rs] if has_prev else [])
    out_shape = [_sds((S, D), F32)] + ([_sds((S, D), BF16)] if has_prev else []) + [_sds((8, D), F32)]
    out_specs = [rs] + ([rs] if has_prev else []) + [_const_spec((8, D))]
    outs = _pcall(_after(body, len(ins), deps), grid=(S // tm,), in_specs=in_specs + [ANY] * len(deps),
                  out_specs=out_specs, out_shape=out_shape, name=name,
                  compiler_params=_params(("arbitrary",)))(*ins, *deps)
    return (outs[0], outs[1], outs[2]) if has_prev else (outs[0], None, outs[1])


def _gate_bwd(dmerged, z, ys, name, deps=()):
    S, D = dmerged.shape
    tm = min(512, S)
    ncol = z.shape[1] // D

    def body(dm_ref, g_ref, y_ref, dy_ref, dz_ref):
        sg = _sigmoid(g_ref[...].astype(F32))
        dm = dm_ref[...].astype(F32)
        dy_ref[...] = (dm * sg).astype(BF16)
        dz_ref[...] = (dm * y_ref[...].astype(F32) * (sg * (1.0 - sg))).astype(BF16)

    branch = pl.BlockSpec((None, tm, D), lambda i, n: (n, i, 0))
    return _pcall(_after(body, 3, deps), grid=(S // tm, 3),
                  in_specs=[pl.BlockSpec((tm, D), lambda i, n: (i, 0)), pl.BlockSpec((tm, D), lambda i, n: (i, 7 + n)),
                            branch] + [ANY] * len(deps),
                  out_specs=[branch, pl.BlockSpec((tm, D), lambda i, n: (i, 7 + n))],
                  out_shape=[_sds((3, S, D), BF16), _sds((S, ncol * D), BF16)], name=name,
                  compiler_params=_params(("parallel", "arbitrary")))(dmerged, z, ys, *deps)


def _mixer_bwd(z, dacts, conv, dz, wsh, sgu_ln, wtril, wtril_t, bias_full, cw, cvec, name):
    S = z.shape[0]
    D = wsh.shape[1]
    tm = CHUNK
    nt = S // tm
    hb = tm // HALO

    def body(zc, zp, da_ref, db_ref, dc_ref, conv_ref, wsh_ref, sln_ref, wt_ref, wtt_ref, bias_ref, cw_ref, cv_ref, _dz_in,
             dz_ref, vec_ref, dcw_ref, dws_ref, dbs_ref, pe, ge, dqe, dce, gr, dcr, cbuf, dcw8):
        i = pl.program_id(0)
        rb = nt - 1 - i

        @pl.when(i == 0)
        def _():
            vec_ref[...] = jnp.zeros_like(vec_ref)
            dcw8[...] = jnp.zeros_like(dcw8)
            dws_ref[...] = jnp.zeros_like(dws_ref)
            dbs_ref[...] = jnp.zeros_like(dbs_ref)
            dqe[tm:tm + HALO, :] = jnp.zeros((HALO, D), F32)
            dce[tm:tm + HALO, :] = jnp.zeros((HALO, D), F32)

        keep = (rb > 0).astype(F32)

        def col(n):
            return zc[:, n * D:(n + 1) * D].astype(F32)

        def pcol(n):
            return zp[:, n * D:(n + 1) * D].astype(F32)

        c_a, x_a = col(1), col(2)
        pe[0:HALO, :] = keep * (pcol(1) * pcol(2))
        pe[HALO:HALO + tm, :] = c_a * x_a
        q = wsh_ref[0:1, :] * pe[HALO - 2:HALO - 2 + tm, :]
        q = q + wsh_ref[1:2, :] * pe[HALO - 1:HALO - 1 + tm, :]
        q = q + wsh_ref[2:3, :] * pe[HALO:HALO + tm, :]
        dact = da_ref[...].astype(F32)
        dz_ref[:, 0:D] = (dact * q).astype(BF16)
        dq = dact * col(0)
        dqe[0:tm, :] = dq
        dp = wsh_ref[2:3, :] * dq + wsh_ref[1:2, :] * dqe[1:1 + tm, :] + wsh_ref[0:1, :] * dqe[2:2 + tm, :]
        dz_ref[:, D:2 * D] = (dp * x_a).astype(BF16)
        dz_ref[:, 2 * D:3 * D] = (dp * c_a).astype(BF16)
        for k in range(SHORT_K):
            o = HALO - (SHORT_K - 1) + k
            vec_ref[k:k + 1, :] += _rsum(dq * pe[o:o + tm, :])
        u, v = col(3), col(4)
        gu, tu = _gelu(u)
        gv, tv = _gelu(v)
        d = gv - _rmean(gv)
        rstd = lax.rsqrt(_rmean(d * d) + EPS)
        nrm = d * rstd
        vnb = (nrm * sln_ref[0:1, :] + sln_ref[1:2, :]).astype(BF16)
        dact = db_ref[...].astype(F32)
        dvn_parts, dgu_parts = [], []
        for g in range(NG):
            cs = slice(g * LANE, (g + 1) * LANE)
            vg = vnb[:, cs]
            mixed = jnp.dot(wt_ref[g], vg, preferred_element_type=F32) + bias_ref[:, cs]
            dgu_parts.append(dact[:, cs] * mixed)
            dmixed = dact[:, cs] * gu[:, cs]
            dmb = dmixed.astype(BF16)
            dws_ref[g] += lax.dot_general(dmb, vg, (((1,), (1,)), ((), ())), preferred_element_type=F32)
            dbs_ref[g] += jnp.broadcast_to(jnp.sum(dmixed, axis=1, keepdims=True), (CHUNK, LANE))
            dvn_parts.append(jnp.dot(wtt_ref[g], dmb, preferred_element_type=F32))
        dgu = jnp.concatenate(dgu_parts, axis=1)
        dvn = jnp.concatenate(dvn_parts, axis=1)
        dz_ref[:, 3 * D:4 * D] = (dgu * _dgelu(u, tu)).astype(BF16)
        vec_ref[3:4, :] += _rsum(dvn * nrm)
        vec_ref[4:5, :] += _rsum(dvn)
        dn = dvn * sln_ref[0:1, :]
        dgv = rstd * (dn - _rmean(dn) - nrm * _rmean(dn * nrm))
        dz_ref[:, 4 * D:5 * D] = (dgv * _dgelu(v, tv)).astype(BF16)
        a_c = col(5)
        sg = _sigmoid(col(6))
        ge[0:HALO, :] = keep * (pcol(5) * _sigmoid(pcol(6)))
        ge[HALO:HALO + tm, :] = a_c * sg
        _fill_shifted(ge, gr)
        o0 = HALO - (CFM_K - 1)
        conv = conv_ref[...].astype(F32)
        d = conv - _rmean(conv)
        rstd = lax.rsqrt(_rmean(d * d) + EPS)
        nrm = d * rstd
        ln = nrm * cv_ref[1:2, :] + cv_ref[2:3, :]
        sl = _sigmoid(ln)
        dln = dc_ref[...].astype(F32) * (sl * (1.0 + ln * (1.0 - sl)))
        vec_ref[6:7, :] += _rsum(dln * nrm)
        vec_ref[7:8, :] += _rsum(dln)
        dn = dln * cv_ref[1:2, :]
        dconv = rstd * (dn - _rmean(dn) - nrm * _rmean(dn * nrm))
        vec_ref[5:6, :] += _rsum(dconv)
        dce[0:tm, :] = dconv
        _fill_shifted(dce, dcr)
        _causal_conv(cw_ref, range(CFM_K), None, dce, dcr, [CFM_K - 1 - k for k in range(CFM_K)], tm, cbuf)
        dglu = cbuf[...]
        for cb in range(D // LANE):
            cs = slice(cb * LANE, (cb + 1) * LANE)
            dcv = dce[0:tm, cs]
            for k in range(CFM_K):
                prod = dcv * _rows_at(ge, gr, o0 + k, tm, cs)
                dcw8[k, :, cs] += jnp.sum(prod.reshape(tm // 8, 8, LANE), axis=0)

        @pl.when(i == nt - 1)
        def _():
            dcw_ref[...] = jnp.sum(dcw8[...], axis=1)
        dz_ref[:, 5 * D:6 * D] = (dglu * sg).astype(BF16)
        dz_ref[:, 6 * D:7 * D] = (dglu * a_c * (sg * (1.0 - sg))).astype(BF16)
        dqe[tm:tm + HALO, :] = dqe[0:HALO, :]
        dce[tm:tm + HALO, :] = dce[0:HALO, :]

    rev = lambda i: (nt - 1 - i, 0)
    rs = pl.BlockSpec((tm, D), rev)
    cur = pl.BlockSpec((tm, 7 * D), rev)
    prev = pl.BlockSpec((HALO, 7 * D), lambda i: (jnp.maximum((nt - 1 - i) * hb - 1, 0), 0))
    ext = pltpu.VMEM((HALO + tm, D), F32)
    outs = _pcall(
        body, grid=(nt,),
        in_specs=[cur, prev] + [pl.BlockSpec((None, tm, D), functools.partial(lambda i, n: (n, nt - 1 - i, 0), n=n))
                                for n in range(3)]
        + [rs, _const_spec((8, D)), _const_spec((8, D)), _const_spec((NG, CHUNK, CHUNK)),
                  _const_spec((NG, CHUNK, CHUNK)), _const_spec((CHUNK, D)), _const_spec((HALO, D)), _const_spec((8, D)),
                  ANY],
        out_specs=[cur, _const_spec((8, D)), _const_spec((HALO, D)), _const_spec((NG, CHUNK, CHUNK)),
                   _const_spec((NG, CHUNK, LANE))],
        out_shape=[_sds(dz.shape, BF16), _sds((8, D), F32), _sds((HALO, D), F32), _sds((NG, CHUNK, CHUNK), F32),
                   _sds((NG, CHUNK, LANE), F32)],
        scratch_shapes=[ext, ext, ext, ext, pltpu.VMEM((7, HALO + tm, D), F32), pltpu.VMEM((7, HALO + tm, D), F32),
                        pltpu.VMEM((tm, D), F32), pltpu.VMEM((HALO, 8, D), F32)],
        input_output_aliases={13: 0}, name=name,
        compiler_params=_params(("arbitrary",)))(z, z, dacts, dacts, dacts, conv, wsh, sgu_ln, wtril, wtril_t, bias_full, cw,
                                                 cvec, dz)
    return outs


def _ada_fwd(c_all, w_ada_loc, name):
    nb, D = c_all.shape
    L, _, nc = w_ada_loc.shape

    def body(c_ref, w_ref, o_ref, ca_ref):
        cv = c_ref[...]
        ca = cv * _sigmoid(cv)
        ca_ref[...] = ca
        o_ref[...] = jnp.dot(ca.astype(BF16), w_ref[...].astype(BF16), preferred_element_type=F32)

    return _pcall(body, grid=(L,),
                  in_specs=[_const_spec((nb, D)), pl.BlockSpec((None, D, nc), lambda l: (l, 0, 0))],
                  out_specs=[pl.BlockSpec((None, nb, nc), lambda l: (l, 0, 0)), _const_spec((nb, D))],
                  out_shape=[_sds((L, nb, nc), F32), _sds((nb, D), F32)], name=name,
                  compiler_params=_params(("arbitrary",)))(c_all, w_ada_loc)


def _adamw(w, g, m, v):
    m = ADAM_B1 * m + (1.0 - ADAM_B1) * g
    v = ADAM_B2 * v + (1.0 - ADAM_B2) * (g * g)
    m_hat = m / (1.0 - ADAM_B1 ** ADAM_STEP)
    v_hat = v / (1.0 - ADAM_B2 ** ADAM_STEP)
    delta = -ADAM_LR * (m_hat / (jnp.sqrt(v_hat) + ADAM_EPS) + ADAM_WD * w)
    return delta, m, v


def _tile_rows(R, C, align=8):
    cap = max(align, (1536 * 1024) // (4 * C))
    best = None
    for t in range(align, R + 1, align):
        if R % t == 0 and t <= cap:
            best = t
    return R if best is None else best


def _adam_ada(ct, dm, w, m, v, name):
    L, D, nc = w.shape
    nb = ct.shape[1]
    tr = _tile_rows(D, nc)

    def body(ct_ref, dm_ref, w_ref, m_ref, v_ref, g_ref, d_ref, mo_ref, vo_ref):
        g = ct_ref[:, 0:1] * dm_ref[0:1, :]
        for b in range(1, nb):
            g = g + ct_ref[:, b:b + 1] * dm_ref[b:b + 1, :]
        g_ref[...] = g
        d_ref[...], mo_ref[...], vo_ref[...] = _adamw(w_ref[...], g, m_ref[...], v_ref[...])

    ws = pl.BlockSpec((None, tr, nc), lambda l, r: (l, r, 0))
    return _pcall(body, grid=(L, D // tr),
                  in_specs=[pl.BlockSpec((tr, nb), lambda l, r: (r, 0)), pl.BlockSpec((None, nb, nc), lambda l, r: (l, 0, 0)),
                            ws, ws, ws],
                  out_specs=[ws] * 4, out_shape=[_sds(w.shape, F32)] * 4, name=name,
                  compiler_params=_params(("parallel", "parallel")))(ct, dm, w, m, v)


def _adam_small(parts, w, m, v, name, deps=()):
    n, R, C = parts.shape
    tr = _tile_rows(R, C * n // 2)

    def body(p_ref, w_ref, m_ref, v_ref, g_ref, d_ref, mo_ref, vo_ref):
        g = p_ref[0]
        for j in range(1, n):
            g = g + p_ref[j]
        g_ref[...] = g
        d_ref[...], mo_ref[...], vo_ref[...] = _adamw(w_ref[...], g, m_ref[...], v_ref[...])

    ws = pl.BlockSpec((tr, C), lambda r: (r, 0))
    return _pcall(_after(body, 4, deps), grid=(R // tr,),
                  in_specs=[pl.BlockSpec((n, tr, C), lambda r: (0, r, 0)), ws, ws, ws] + [ANY] * len(deps),
                  out_specs=[ws] * 4, out_shape=[_sds((R, C), F32)] * 4, name=name,
                  compiler_params=_params(("parallel",)))(parts, w, m, v, *deps)


def _adam_plain(g, w, m, v, name):
    R, C = w.shape

    def body(g_ref, w_ref, m_ref, v_ref, d_ref, mo_ref, vo_ref):
        d_ref[...], mo_ref[...], vo_ref[...] = _adamw(w_ref[...], g_ref[...], m_ref[...], v_ref[...])

    ws = _const_spec((R, C))
    return _pcall(body, grid=(1,), in_specs=[ws] * 4, out_specs=[ws] * 3, out_shape=[_sds((R, C), F32)] * 3, name=name,
                  compiler_params=_params(("arbitrary",)))(g, w, m, v)


def _pair_sum(G, R1, my_c, name):
    n, R, C = G.shape
    half = n // 2
    tr = _tile_rows(R, C, align=16)

    def body(c_ref, g_ref, r_ref, o_ref):
        o_ref[...] = (g_ref[...].astype(F32) + r_ref[...].astype(F32)).astype(o_ref.dtype)

    blk = (None, tr, C)
    gs = pltpu.PrefetchScalarGridSpec(
        num_scalar_prefetch=1, grid=(half, R // tr),
        in_specs=[pl.BlockSpec(blk, lambda p, r, c: (2 * p + c[0], r, 0)), pl.BlockSpec(blk, lambda p, r, c: (p, r, 0))],
        out_specs=pl.BlockSpec(blk, lambda p, r, c: (p, r, 0)))
    return _pcall(body, grid_spec=gs, out_shape=_sds((half, R, C), G.dtype), name=name,
                  compiler_params=_params(("parallel", "parallel")))(my_c, G, R1)


def _adam_big(P, R2, my_chip, w, m, v, layer, prev, name, deps=()):
    _, R, C = P.shape
    nrecv = R2.shape[0]
    tr = _tile_rows(R, C, align=16)

    def body(p_sm, p_ref, r_ref, w_ref, m_ref, v_ref, *rest):
        g_ref, d_ref, mo_ref, vo_ref = rest[-4:]
        g = p_ref[...].astype(F32)
        for k in range(nrecv):
            g = g + r_ref[k].astype(F32)
        g_ref[...] = g
        d_ref[...], mo_ref[...], vo_ref[...] = _adamw(w_ref[...], g, m_ref[...], v_ref[...])

    ws = pl.BlockSpec((None, tr, C), lambda r, p: (layer, r, 0))
    held = [] if prev is None else list(prev)
    gs = pltpu.PrefetchScalarGridSpec(
        num_scalar_prefetch=1, grid=(R // tr,),
        in_specs=[pl.BlockSpec((None, tr, C), lambda r, p: (p[0], r, 0)),
                  pl.BlockSpec((nrecv, tr, C), lambda r, p: (0, r, 0)), ws, ws, ws] + [ANY] * (len(held) + len(deps)),
        out_specs=[ws] * 4)
    alias = {6 + i: i for i in range(len(held))}
    return _pcall(body, grid_spec=gs, out_shape=[_sds(w.shape, F32)] * 4, name=name, input_output_aliases=alias,
                  compiler_params=_params(("parallel",)))(my_chip, P, R2, w, m, v, *held, *deps)


def _place():
    return lax.axis_index("x"), lax.axis_index("y"), lax.axis_index("c")


def _all_gather(shards, name, deps=()):
    n = len(shards)

    def body(*refs):
        ins, outs = refs[:n], refs[n:2 * n]
        send_sems, recv_sems, local_sems = refs[2 * n:]
        x, y, c = _place()
        me, sibling = (x, y, c), (x, y, 1 - c)
        chips = [(1 - x, y), (x, 1 - y), (1 - x, 1 - y)]

        def slot(a, px, py, pc):
            return outs[a].at[4 * px + 2 * py + pc]

        def copy(a, k, block, to, src=None):
            return pltpu.make_async_remote_copy(
                src_ref=slot(a, *block) if src is None else src, dst_ref=slot(a, *block),
                send_sem=send_sems.at[7 * a + k], recv_sem=recv_sems.at[7 * a + k], device_id=to, device_id_type=MESH)

        mine = [pltpu.make_async_copy(ins[a], slot(a, *me), local_sems.at[a]) for a in range(n)]
        for cp in mine:
            cp.start()
        first = []
        for a in range(n):
            first.append(copy(a, 0, me, sibling, src=ins[a]))
            first += [copy(a, 1 + j, me, (*chip, c), src=ins[a]) for j, chip in enumerate(chips)]
        for cp in first:
            cp.start()
        passed = []
        for j, chip in enumerate(chips):
            for a in range(n):
                copy(a, 1 + j, (*chip, c), me).wait_recv()
                fwd = copy(a, 4 + j, (*chip, c), sibling)
                fwd.start()
                passed.append(fwd)
        for a in range(n):
            copy(a, 0, sibling, me).wait_recv()
        for j, chip in enumerate(chips):
            for a in range(n):
                copy(a, 4 + j, (*chip, 1 - c), me).wait_recv()
        for cp in first + passed:
            cp.wait_send()
        for cp in mine:
            cp.wait()

    outs = _pcall(_after(body, n, deps), in_specs=[ANY] * (n + len(deps)), out_specs=[ANY] * n,
                  out_shape=[_sds((NDEV,) + s.shape, s.dtype) for s in shards],
                  scratch_shapes=[pltpu.SemaphoreType.DMA((7 * n,)), pltpu.SemaphoreType.DMA((7 * n,)),
                                  pltpu.SemaphoreType.DMA((n,))], name=name)(*shards, *deps)
    return list(outs)


HBM = pl.BlockSpec(memory_space=pltpu.HBM)
SEM = pl.BlockSpec(memory_space=pltpu.SEMAPHORE)


def _copies(plan, refs, send_sems, recv_sems):
    return [pltpu.make_async_remote_copy(src_ref=s, dst_ref=d, send_sem=send_sems.at[k], recv_sem=recv_sems.at[k],
                                         device_id=dev, device_id_type=MESH)
            for k, (s, d, dev) in enumerate(plan(refs, *_place()))]


def _xfer_start(bufs, ncopies, plan, name, deps=()):
    n = len(bufs)

    def body(*refs):
        for cp in _copies(plan, refs[:n], refs[n], refs[n + 1]):
            cp.start()
        token = refs[2 * n + 2]
        token[...] = jnp.zeros_like(token)

    outs = _pcall(
        _after(body, n, deps), name=name,
        out_shape=(pltpu.SemaphoreType.DMA((ncopies,)), pltpu.SemaphoreType.DMA((ncopies,)),
                   *[pltpu.HBM(b.shape, b.dtype) for b in bufs], _sds((8, LANE), F32)),
        in_specs=[HBM] * n + [ANY] * len(deps),
        out_specs=(SEM, SEM, *[HBM] * n, pl.BlockSpec(memory_space=pltpu.VMEM)),
        input_output_aliases={i: 2 + i for i in range(n)},
        compiler_params=pltpu.CompilerParams(has_side_effects=pltpu.SideEffectType.DATAFLOW_SIDE_EFFECTING),
    )(*[pltpu.with_memory_space_constraint(b, pltpu.HBM) for b in bufs], *deps)
    return (outs[0], outs[1]), list(outs[2:2 + n]), outs[2 + n]


def _xfer_wait(sems, bufs, plan, after, name):
    n = len(bufs)
    after = list(after) if isinstance(after, (list, tuple)) else [after]

    def body(*refs):
        for cp in _copies(plan, refs[:n], refs[n], refs[n + 1]):
            cp.wait_send()
            cp.wait_recv()

    outs = _pcall(
        body, name=name, out_shape=tuple(pltpu.HBM(b.shape, b.dtype) for b in bufs),
        in_specs=[HBM] * n + [SEM, SEM] + [ANY] * len(after), out_specs=tuple([HBM] * n),
        input_output_aliases={i: i for i in range(n)},
        compiler_params=pltpu.CompilerParams(has_side_effects=pltpu.SideEffectType.DATAFLOW_SIDE_EFFECTING),
    )(*bufs, *sems, *after)
    return list(outs)


def _chips_of(x, y):
    return [(1 - x, y), (x, 1 - y), (1 - x, 1 - y)]


def _gather_plan1(n):
    def plan(refs, x, y, c):
        out = []
        for a in range(n):
            blk = refs[a].at[4 * x + 2 * y + c]
            out.append((blk, blk, (x, y, 1 - c)))
            out += [(blk, blk, (px, py, c)) for px, py in _chips_of(x, y)]
        return out
    return plan


def _gather_plan2(n):
    def plan(refs, x, y, c):
        out = []
        for a in range(n):
            for px, py in _chips_of(x, y):
                blk = refs[a].at[4 * px + 2 * py + c]
                out.append((blk, blk, (x, y, 1 - c)))
        return out
    return plan


def _gather_start(shards, dev, name, deps=()):
    lands = [lax.dynamic_update_slice(lax.empty((NDEV,) + s.shape, s.dtype), s[None], (dev,) + (0,) * s.ndim)
             for s in shards]
    n = len(shards)
    sems, lands, tok = _xfer_start(lands, 4 * n, _gather_plan1(n), name + "_p1_start", deps)
    return dict(sems=sems, lands=lands, tok=tok, n=n)


def _gather_mid(st, after, name):
    n = st["n"]
    lands = _xfer_wait(st["sems"], st["lands"], _gather_plan1(n), after, name + "_p1_wait")
    sems, lands, tok = _xfer_start(lands, 3 * n, _gather_plan2(n), name + "_p2_start")
    return dict(sems=sems, lands=lands, tok=tok, n=n)


def _gather_finish(st, after, name):
    return _xfer_wait(st["sems"], st["lands"], _gather_plan2(st["n"]), after, name + "_p2_wait")


def _scatter_plan1(n):
    def plan(refs, x, y, c):
        return [(refs[a].at[2 * p + 1 - c], refs[n + a].at[p], (x, y, 1 - c)) for a in range(n) for p in range(NCHIP)]
    return plan


def _scatter_plan2(n):
    def plan(refs, x, y, c):
        return [(refs[a].at[2 * px + py], refs[n + a].at[j], (px, py, c))
                for a in range(n) for j, (px, py) in enumerate(_chips_of(x, y))]
    return plan


def _scatter_start(Gs, name):
    n = len(Gs)
    R1s = [lax.empty((NCHIP,) + g.shape[1:], g.dtype) for g in Gs]
    sems, bufs, tok = _xfer_start(list(Gs) + R1s, NCHIP * n, _scatter_plan1(n), name + "_s1_start")
    return dict(sems=sems, bufs=bufs, tok=tok, n=n)


def _scatter_mid(st, after, my_c, name):
    n = st["n"]
    bufs = _xfer_wait(st["sems"], st["bufs"], _scatter_plan1(n), after, name + "_s1_wait")
    Ps = [_pair_sum(bufs[a], bufs[n + a], my_c, f"{name}_pair_sum{a}") for a in range(n)]
    R2s = [lax.empty((3,) + p.shape[1:], p.dtype) for p in Ps]
    sems, bufs, tok = _xfer_start(Ps + R2s, 3 * n, _scatter_plan2(n), name + "_s2_start")
    return dict(sems=sems, bufs=bufs, tok=tok, n=n)


def _scatter_finish(st, after, name):
    n = st["n"]
    bufs = _xfer_wait(st["sems"], st["bufs"], _scatter_plan2(n), after, name + "_s2_wait")
    return bufs[:n], bufs[n:]


SMALL_ROWS = {"norm1_g": (0, 1), "norm2_g": (1, 1), "sgu_ln_g": (2, 1), "sgu_ln_b": (3, 1), "cfm_conv_b": (4, 1),
              "cfm_ln_g": (5, 1), "cfm_ln_b": (6, 1), "b_sgu": (7, 1), "w_sgu": (8, 128), "b_ada": (136, N_MOD),
              "w_short": (142, SHORT_K), "cfm_conv_w": (145, CFM_K)}
ROWS_PER_LAYER = 176
FINAL_ROW = DEPTH * ROWS_PER_LAYER
PACK_ROWS = 360


def _pack(get, D, layers=tuple(range(DEPTH)), tail=True):
    parts = []
    for l in layers:
        for name, (_, nrows) in SMALL_ROWS.items():
            a = get(name, l)
            parts.append(jnp.zeros((nrows * D,), F32) if a is None else a.astype(F32).reshape(nrows * D))
    if tail:
        for name in ("final_g", "loss"):
            a = get(name, None)
            parts.append(jnp.zeros((D,), F32) if a is None else a.astype(F32).reshape(D))
        parts.append(jnp.zeros(((PACK_ROWS - FINAL_ROW - 2) * D,), F32))
    return jnp.concatenate(parts).reshape(-1, D)


def _unpack(pack, name, shape):
    D = pack.shape[1]
    r0, nrows = SMALL_ROWS[name]
    return jnp.stack([pack[l * ROWS_PER_LAYER + r0:l * ROWS_PER_LAYER + r0 + nrows] for l in range(DEPTH)]).reshape(shape)


def _mm_tiles(S):
    return min(512, S), min(1024, S), min(2048, S)


def kernel(x, c, w_ada, b_ada, norm1_g, w_in, w_short, w_a_out, sgu_ln_g, sgu_ln_b, w_sgu, b_sgu, w_b_out, cfm_conv_w, cfm_conv_b, cfm_ln_g, cfm_ln_b, w_c_out, w_o, norm2_g, w_ffn_in, w_ffn_out, final_g, loss_target, m_w_ada, m_b_ada, m_norm1_g, m_w_in, m_w_short, m_w_a_out, m_sgu_ln_g, m_sgu_ln_b, m_w_sgu, m_b_sgu, m_w_b_out, m_cfm_conv_w, m_cfm_conv_b, m_cfm_ln_g, m_cfm_ln_b, m_w_c_out, m_w_o, m_norm2_g, m_w_ffn_in, m_w_ffn_out, m_final_g, v_w_ada, v_b_ada, v_norm1_g, v_w_in, v_w_short, v_w_a_out, v_sgu_ln_g, v_sgu_ln_b, v_w_sgu, v_b_sgu, v_w_b_out, v_cfm_conv_w, v_cfm_conv_b, v_cfm_ln_g, v_cfm_ln_b, v_w_c_out, v_w_o, v_norm2_g, v_w_ffn_in, v_w_ffn_out, v_final_g):
    W = dict(w_ada=w_ada, b_ada=b_ada, norm1_g=norm1_g, w_in=w_in, w_short=w_short, w_a_out=w_a_out, sgu_ln_g=sgu_ln_g,
             sgu_ln_b=sgu_ln_b, w_sgu=w_sgu, b_sgu=b_sgu, w_b_out=w_b_out, cfm_conv_w=cfm_conv_w, cfm_conv_b=cfm_conv_b,
             cfm_ln_g=cfm_ln_g, cfm_ln_b=cfm_ln_b, w_c_out=w_c_out, w_o=w_o, norm2_g=norm2_g, w_ffn_in=w_ffn_in,
             w_ffn_out=w_ffn_out, final_g=final_g)
    Mo = dict(w_ada=m_w_ada, b_ada=m_b_ada, norm1_g=m_norm1_g, w_in=m_w_in, w_short=m_w_short, w_a_out=m_w_a_out,
              sgu_ln_g=m_sgu_ln_g, sgu_ln_b=m_sgu_ln_b, w_sgu=m_w_sgu, b_sgu=m_b_sgu, w_b_out=m_w_b_out,
              cfm_conv_w=m_cfm_conv_w, cfm_conv_b=m_cfm_conv_b, cfm_ln_g=m_cfm_ln_g, cfm_ln_b=m_cfm_ln_b,
              w_c_out=m_w_c_out, w_o=m_w_o, norm2_g=m_norm2_g, w_ffn_in=m_w_ffn_in, w_ffn_out=m_w_ffn_out,
              final_g=m_final_g)
    Vo = dict(w_ada=v_w_ada, b_ada=v_b_ada, norm1_g=v_norm1_g, w_in=v_w_in, w_short=v_w_short, w_a_out=v_w_a_out,
              sgu_ln_g=v_sgu_ln_g, sgu_ln_b=v_sgu_ln_b, w_sgu=v_w_sgu, b_sgu=v_b_sgu, w_b_out=v_w_b_out,
              cfm_conv_w=v_cfm_conv_w, cfm_conv_b=v_cfm_conv_b, cfm_ln_g=v_cfm_ln_g, cfm_ln_b=v_cfm_ln_b,
              w_c_out=v_w_c_out, w_o=v_w_o, norm2_g=v_norm2_g, w_ffn_in=v_w_ffn_in, w_ffn_out=v_w_ffn_out,
              final_g=v_final_g)
    order = ["w_ada", "b_ada", "norm1_g", "w_in", "w_short", "w_a_out", "sgu_ln_g", "sgu_ln_b", "w_sgu", "b_sgu",
             "w_b_out", "cfm_conv_w", "cfm_conv_b", "cfm_ln_g", "cfm_ln_b", "w_c_out", "w_o", "norm2_g", "w_ffn_in",
             "w_ffn_out", "final_g"]

    assert DEPTH == 2, "the weight-gather schedule below is written for two layers"
    S, D = x.shape[1], x.shape[2]
    F2 = w_ffn_in.shape[2] * NDEV
    FF = F2 // 2
    xi, yi, ci = _place()
    dev = 4 * xi + 2 * yi + ci
    my_c = jnp.reshape(ci, (1,)).astype(jnp.int32)
    my_chip = jnp.reshape(2 * xi + yi, (1,)).astype(jnp.int32)
    tm, tm_big, tm_huge = _mm_tiles(S)
    x0 = x.reshape(S, D)
    tgt = loss_target.reshape(S, D)

    def shards_of(l):
        return [w_in[l].astype(BF16), w_a_out[l].astype(BF16), w_b_out[l].astype(BF16), w_c_out[l].astype(BF16),
                w_o[l].astype(BF16), w_ffn_in[l].astype(BF16), w_ffn_out[l].astype(BF16)]

    c_all = _all_gather([jnp.pad(c, ((0, 7), (0, 0)))], "ag_c")[0][:, 0, :]
    modpart, c_act = _ada_fwd(c_all, w_ada, "ada_fwd")
    ncol = modpart.shape[2]
    mg = _all_gather([modpart.reshape(DEPTH * NDEV, ncol)], "ag_mod")[0].reshape(NDEV, DEPTH, NDEV, ncol)
    mine = lax.dynamic_index_in_dim(mg, dev, axis=2, keepdims=False)
    mod = (jnp.transpose(mine, (1, 0, 2)).reshape(DEPTH, N_MOD * D) + b_ada).reshape(DEPTH, N_MOD, D)

    ncs = w_short.shape[2]
    ag_in0 = _gather_start([w_in[0].astype(BF16), w_short.reshape(DEPTH * SHORT_K, ncs),
                            cfm_conv_w.reshape(DEPTH * CFM_K, ncs)], dev, "ag_w_in0", deps=(mod,))
    W, Mo, Vo = lax.optimization_barrier((ag_in0["tok"], (W, Mo, Vo)))[1]
    (norm1_g, norm2_g, w_in, w_a_out, w_b_out, w_c_out, w_o, w_ffn_in, w_ffn_out, sgu_ln_g, sgu_ln_b, w_sgu, b_sgu,
     cfm_conv_b, cfm_ln_g, cfm_ln_b, final_g) = [W[k] for k in (
         "norm1_g", "norm2_g", "w_in", "w_a_out", "w_b_out", "w_c_out", "w_o", "w_ffn_in", "w_ffn_out", "sgu_ln_g",
         "sgu_ln_b", "w_sgu", "b_sgu", "cfm_conv_b", "cfm_ln_g", "cfm_ln_b", "final_g")]
    m_w_ada, v_w_ada = Mo["w_ada"], Vo["w_ada"]
    xl0, h0, ht0 = _norm_fwd(x0, None, _rows(jnp.zeros((D,), F32), norm1_g[0], mod[0, 1], mod[0, 0]), "norm1_fwd0",
                             deps=(ag_in0["tok"],))
    ag_rest0 = _gather_start(shards_of(0)[1:], dev, "ag_rest0", deps=(h0,))

    tril = jnp.tril(jnp.ones((CHUNK, CHUNK), dtype=bool))

    def layer_consts(l):
        wt = jnp.where(tril[None], w_sgu[l], 0.0).astype(BF16)
        return dict(sgu_ln=_rows(sgu_ln_g[l], sgu_ln_b[l]), wtril=wt, wtril_t=jnp.swapaxes(wt, 1, 2),
                    bias_full=jnp.repeat(b_sgu[l].T, LANE, axis=1), cvec=_rows(cfm_conv_b[l], cfm_ln_g[l], cfm_ln_b[l]))

    def rest_of(g):
        return dict(w_a=g[0].reshape(1, D, D), w_b=g[1].reshape(1, D, D), w_c=g[2].reshape(1, D, D),
                    w_o=g[3].reshape(1, D, D), w_fi=jnp.transpose(g[4], (1, 0, 2)).reshape(1, D, F2),
                    w_fo=g[5].reshape(1, FF, D))

    sharded_small = ("w_short", "cfm_conv_w")

    def param_get(T):
        def get(name, l):
            if name == "final_g":
                return T[name]
            return None if name in sharded_small or name == "loss" else T[name][l]
        return get

    packs = [_pack(param_get(T), D) for T in (W, Mo, Vo)]
    ag_in0 = _gather_mid(ag_in0, [ag_rest0["tok"], *packs], "ag_w_in0")
    (w_sgu, b_sgu, sgu_ln_g, sgu_ln_b, cfm_conv_b, cfm_ln_g, cfm_ln_b), conv_wmv_in = lax.optimization_barrier(
        (ag_in0["tok"], ((w_sgu, b_sgu, sgu_ln_g, sgu_ln_b, cfm_conv_b, cfm_ln_g, cfm_ln_b),
                         [(T["w_short"], T["cfm_conv_w"]) for T in (W, Mo, Vo)])))[1]
    consts = [layer_consts(l) for l in range(DEPTH)]
    ncr = DEPTH * (SHORT_K + CFM_K)
    padr = (-ncr) % 8
    convw_wmv = [jnp.pad(jnp.concatenate([a.reshape(-1, ncs), b.reshape(-1, ncs)]), ((0, padr), (0, 0)))
                 for a, b in conv_wmv_in]
    g_in0 = _gather_finish(ag_in0, [*convw_wmv] + [a for cl in consts for a in cl.values()], "ag_w_in0")
    w_short_full = jnp.transpose(g_in0[1], (1, 0, 2)).reshape(DEPTH, SHORT_K, D)
    cfm_w_full = jnp.transpose(g_in0[2], (1, 0, 2)).reshape(DEPTH, CFM_K, D)
    for l in range(DEPTH):
        consts[l]["wsh"] = jnp.pad(w_short_full[l], ((0, 8 - SHORT_K), (0, 0)))
        consts[l]["cw"] = jnp.pad(cfm_w_full[l], ((0, HALO - CFM_K), (0, 0)))
    Wg = [dict(w_in=g_in0[0]), None]
    ag_l1 = None
    nin = w_in.shape[2]
    tn_in = nin if nin % 256 == 0 and nin <= 1280 else 256
    tn_fi = 512 if F2 % 512 == 0 else 256
    tn_dw = min(256, D)

    saved = []
    xcur, fprev, gprev = x0, None, None
    for l in range(DEPTH):
        sh1, sc1, g1, sh2, sc2, g2 = [mod[l, k] for k in range(N_MOD)]
        cl = consts[l]
        if l == 0:
            xl, h, ht = xl0, h0, ht0
        else:
            vec1 = _rows(gprev, norm1_g[l], sc1, sh1)
            ag_l1 = _gather_mid(ag_l1, fprev, f"ag_w{l}")
            xl, h, ht = _norm_fwd(xcur, fprev, vec1, f"norm1_fwd{l}", deps=(ag_l1["tok"],))
            g = _gather_finish(ag_l1, h, f"ag_w{l}")
            Wg[l] = dict(w_in=g[0], **rest_of(g[1:]))
        wl = Wg[l]
        z = _mm_nn(h, wl["w_in"], BF16, tm_huge, tn_in, D, f"mm_in{l}", w_outer=True)
        mix_deps = ()
        if l == 0:
            ag_rest0 = _gather_mid(ag_rest0, z, "ag_rest0")
            mix_deps = (ag_rest0["tok"],)
            if DEPTH > 1:
                ag_l1 = _gather_start(shards_of(1), dev, "ag_w1")
                mix_deps += (ag_l1["tok"],)
        acts, acts_t, conv = _mixer_fwd(z, cl["wsh"], cl["sgu_ln"], cl["wtril"], cl["bias_full"], cl["cw"], cl["cvec"],
                                        f"mixer_fwd{l}", deps=mix_deps)
        if l == 0:
            wl.update(rest_of(_gather_finish(ag_rest0, acts[0], "ag_rest0")))
        merged, merged_t, ys = _branch_out(acts, [wl["w_a"][0], wl["w_b"][0], wl["w_c"][0]], z, f"branch_out{l}")
        o = _mm_nn(merged, wl["w_o"], F32, tm_big, D, D, f"mm_o{l}")
        x1, h2, h2t = _norm_fwd(xl, o, _rows(g1, norm2_g[l], sc2, sh2), f"norm2_fwd{l}")
        gu, act, act_t = _ffn_in_swiglu(h2, wl["w_fi"], tm_huge, 256, f"mm_ffn_in{l}")
        f = _mm_nn(act, wl["w_fo"], F32, tm_big, D, FF, f"mm_ffn_out{l}")
        saved.append(dict(xl=xl, ht=ht, z=z, acts_t=acts_t, conv=conv, ys=ys, merged_t=merged_t, o=o, x1=x1, h2t=h2t, gu=gu,
                          act_t=act_t, f=f, consts=cl, mod=(sh1, sc1, g1, sh2, sc2, g2)))
        xcur, fprev, gprev = x1, f, g2

    last = saved[-1]
    dxup, dfb, fsums, loss_blk = _final_bwd(last["x1"], last["f"], tgt, _rows(last["mod"][5], final_g), "final_bwd")
    loss_row = jnp.pad(loss_blk[0, 0:1], (0, D - 1))
    dgate2_next = fsums[1]
    small = [dict() for _ in range(DEPTH)]
    dmods = [None] * DEPTH
    nfi = w_ffn_in.shape[2]
    early_names, late_names = ["w_ffn_out", "w_ffn_in", "w_o"], ["w_a_out", "w_b_out", "w_c_out", "w_in"]
    results = {n: None for n in early_names + late_names}

    def adam_group(names, Ps, R2s, l, deps=()):
        for n, p, r2 in zip(names, Ps, R2s):
            results[n] = _adam_big(p, r2, my_chip, W[n], Mo[n], Vo[n], l, results[n], f"adam_{n}{l}", deps)

    deferred = []
    late_prev = None
    ag_s1, gathered1 = None, None
    tk_w = min(2048, S)
    tn_dw_in = tn_in // 2 if tn_in == 1280 else tn_in
    for l in reversed(range(DEPTH)):
        sv, wl, cl = saved[l], Wg[l], saved[l]["consts"]
        sh1, sc1, g1, sh2, sc2, g2 = sv["mod"]
        dact = _mm_nt(dfb, wl["w_fo"], BF16, tm_big, FF, D, f"mm_dact{l}",
                      deps=() if late_prev is None else (late_prev["tok"], ag_s1["tok"]))
        g_fo = _mm_wgrad(sv["act_t"], dfb, 1, FF // 2, D, tk_w, f"mm_dw_ffn_out{l}")
        dgu = _swiglu_bwd(dact, sv["gu"], f"swiglu_bwd{l}")
        dh2 = _mm_nt(dgu, wl["w_fi"], F32, tm, D, F2, f"mm_dh2{l}")
        if late_prev is not None:
            deferred.append((late_names, *_scatter_finish(late_prev, dh2, f"rs_late{l + 1}"), l + 1))
            late_prev = None
        g_fi = _mm_wgrad(sv["h2t"], dgu, 1, D, tn_fi, S, f"mm_dw_ffn_in{l}")
        if ag_s1 is not None:
            ag_s1 = _gather_mid(ag_s1, g_fi, "ag_small1")
        dx1, dob, s2 = _norm_bwd(sv["x1"], dh2, dxup, _rows(norm2_g[l], sc2, g1), sv["o"], f"norm2_bwd{l}",
                                 deps=() if ag_s1 is None else (ag_s1["tok"],))
        dmerged = _mm_nt(dob, wl["w_o"], BF16, tm_big, D, D, f"mm_dmerged{l}")
        g_o = _mm_wgrad(sv["merged_t"], dob, 1, D, tn_dw, S, f"mm_dw_o{l}")
        early = _scatter_start([g_fo.reshape(NDEV, FF // NDEV, D),
                                jnp.transpose(g_fi.reshape(D, NDEV, nfi), (1, 0, 2)),
                                g_o.reshape(NDEV, D // NDEV, D)], f"rs_early{l}")
        dys, dz = _gate_bwd(dmerged, sv["z"], sv["ys"], f"gate_bwd{l}", deps=(early["tok"],))
        if ag_s1 is not None:
            gathered1 = _gather_finish(ag_s1, dys, "ag_small1")[0]
            ag_s1 = None
        early = _scatter_mid(early, dys, my_c, f"rs_early{l}")
        dacts = _mm3_nt(dys, [wl["w_a"], wl["w_b"], wl["w_c"]], tm_big, f"mm_dact_abc{l}", deps=(early["tok"],))
        g3 = _mm3_wgrad(sv["acts_t"], dys, tn_dw, f"mm_dw_abc{l}")
        g_abc = [g3[n] for n in range(3)]
        dz, mvec, dcw, dws, dbs = _mixer_bwd(sv["z"], dacts, sv["conv"], dz, cl["wsh"], cl["sgu_ln"], cl["wtril"],
                                             cl["wtril_t"], cl["bias_full"], cl["cw"], cl["cvec"], f"mixer_bwd{l}")
        dh = _mm_nt(dz, wl["w_in"], F32, tm_big, D, tn_in, f"mm_dh{l}",
                    blocks_per_step=2 if (tn_in == nin and wl["w_in"].shape[0] % 2 == 0) else 1)
        g_in = _mm_wgrad(sv["ht"], dz, NDEV, D, tn_dw_in, S, f"mm_dw_in{l}")
        late = _scatter_start([g.reshape(NDEV, D // NDEV, D) for g in g_abc] + [g_in], f"rs_late{l}")
        if l > 0:
            pv = saved[l - 1]
            dxup, dfb, s1 = _norm_bwd(sv["xl"], dh, dx1, _rows(norm1_g[l], sc1, pv["mod"][5]), pv["f"], f"norm1_bwd{l}",
                                      deps=(late["tok"],))
        else:
            dxup, dfb, s1 = _norm_bwd(sv["xl"], dh, dx1, _rows(norm1_g[l], sc1), None, f"norm1_bwd{l}", deps=(late["tok"],))
        deferred.append((early_names, *_scatter_finish(early, dxup, f"rs_early{l}"), l))
        dmods[l] = jnp.stack([s1[0], s1[1], s2[3], s2[0], s2[1], dgate2_next])
        dgate2_next = s1[3]
        small[l] = dict(norm1_g=s1[2], norm2_g=s2[2], sgu_ln_g=mvec[3], sgu_ln_b=mvec[4], cfm_conv_b=mvec[5],
                        cfm_ln_g=mvec[6], cfm_ln_b=mvec[7], b_sgu=dbs[:, :, 0],
                        w_sgu=jnp.where(tril[None], dws, 0.0), b_ada=dmods[l], w_short=mvec[0:SHORT_K],
                        cfm_conv_w=dcw[0:CFM_K])
        small_get = lambda name, k: {"final_g": fsums[0], "loss": loss_row}.get(name) if k is None else small[k][name]
        if l > 0:
            late_prev = _scatter_mid(late, dxup, my_c, f"rs_late{l}")
            ag_s1 = _gather_start([_pack(small_get, D, layers=(l,), tail=True)], dev, "ag_small1", deps=(late_prev["tok"],))
    grad_x = dxup.reshape(x.shape)

    gathered0 = _all_gather([_pack(small_get, D, layers=(0,), tail=False)], "ag_small0", deps=(dxup,))[0]
    late_prev = _scatter_mid(late, gathered0, my_c, "rs_late0")
    gathered = jnp.concatenate([gathered0, gathered1], axis=1)
    sg, sd, sm, sv_ = _adam_small(gathered, *packs, name="adam_small", deps=(late_prev["tok"],))
    loss = sg[FINAL_ROW + 1, 0]
    out = {}
    for name in order:
        if name in SMALL_ROWS and name not in sharded_small:
            out[name] = tuple(_unpack(p, name, W[name].shape) for p in (sg, sd, sm, sv_))
    out["final_g"] = tuple(p[FINAL_ROW] for p in (sg, sd, sm, sv_))

    def my_cols(name):
        full = _unpack(sg, name, (DEPTH, SMALL_ROWS[name][1], D))
        return lax.dynamic_slice_in_dim(full, dev * ncs, ncs, axis=2)

    gcs = jnp.concatenate([my_cols("w_short").reshape(-1, ncs), my_cols("cfm_conv_w").reshape(-1, ncs)])
    cd, cm, cv = _adam_plain(jnp.pad(gcs, ((0, padr), (0, 0))), *convw_wmv, "adam_convw")
    nsh = DEPTH * SHORT_K
    out["w_short"] = tuple(a[0:nsh].reshape(w_short.shape) for a in (gcs, cd, cm, cv))
    out["cfm_conv_w"] = tuple(a[nsh:ncr].reshape(cfm_conv_w.shape) for a in (gcs, cd, cm, cv))

    dm_all = jnp.stack([gathered[:, l * ROWS_PER_LAYER + 136:l * ROWS_PER_LAYER + 136 + N_MOD, :].reshape(NDEV, N_MOD * D)
                        for l in range(DEPTH)])
    dm_mine = lax.dynamic_slice_in_dim(dm_all, dev * ncol, ncol, axis=2)
    out["w_ada"] = tuple(_adam_ada(jnp.transpose(c_act), dm_mine, w_ada, m_w_ada, v_w_ada, "adam_ada"))

    for names, Ps, R2s, l in deferred:
        adam_group(names, Ps, R2s, l, deps=(late_prev["tok"],))
    adam_group(late_names, *_scatter_finish(late_prev, results["w_o"][0], "rs_late0"), 0)
    for n in early_names + late_names:
        out[n] = tuple(results[n])

    grads = [out[n][0] for n in order]
    deltas = [out[n][1] for n in order]
    new_m = [out[n][2] for n in order]
    new_v = [out[n][3] for n in order]
    return (loss, grad_x, *grads, *deltas, *new_m, *new_v)
```

```python
import functools
import math

import jax
import jax.numpy as jnp
from jax import lax
from jax.experimental import pallas as pl
from jax.experimental.pallas import tpu as pltpu

F32, BF16 = jnp.float32, jnp.bfloat16
NDEV = 8
NCHIP = NDEV // 2
DEPTH = 2
EPS = 1e-6
CHUNK = 128
NG = 8
SHORT_K = 3
CFM_K = 31
HALO = 32
N_MOD = 6
LANE = 128
VMEM_LIMIT = 56 * 1024 * 1024
ADAM_LR, ADAM_B1, ADAM_B2, ADAM_EPS, ADAM_WD, ADAM_STEP = 0.001, 0.9, 0.999, 1e-08, 0.01, 10
_G0 = math.sqrt(2.0 / math.pi)
_G1 = 0.044715
MESH = pl.DeviceIdType.MESH
ANY = pl.BlockSpec(memory_space=pl.ANY)


def _pcall(body, **kw):
    return pl.pallas_call(body, **kw)


def _params(sem=None):
    return pltpu.CompilerParams(dimension_semantics=sem, vmem_limit_bytes=VMEM_LIMIT)


def _sds(shape, dtype):
    return jax.ShapeDtypeStruct(tuple(shape), dtype)


def _mm_body(dims, nk, out_f32, blocks=1):
    def body(a_ref, b_ref, o_ref, *scr):
        k = pl.program_id(2)
        if blocks == 1:
            part = lax.dot_general(a_ref[...], b_ref[...], dims, preferred_element_type=F32)
        else:
            w = a_ref.shape[1] // blocks
            part = None
            for g in range(blocks):
                t = lax.dot_general(a_ref[:, g * w:(g + 1) * w], b_ref[g], dims, preferred_element_type=F32)
                part = t if part is None else part + t
        if nk == 1:
            o_ref[...] = part.reshape(o_ref.shape).astype(o_ref.dtype)
        elif out_f32:
            @pl.when(k == 0)
            def _():
                o_ref[...] = part.reshape(o_ref.shape)

            @pl.when(k > 0)
            def _():
                o_ref[...] += part.reshape(o_ref.shape)
        else:
            acc = scr[0]

            @pl.when(k == 0)
            def _():
                acc[...] = part

            @pl.when(k > 0)
            def _():
                acc[...] += part

            @pl.when(k == nk - 1)
            def _():
                o_ref[...] = acc[...].astype(o_ref.dtype)
    return body


def _after(body, n_in, deps):
    nd = len(deps)
    if nd == 0:
        return body

    def ordered(*refs):
        return body(*refs[:n_in], *refs[n_in + nd:])
    return ordered


def _mm_call(body, grid, in_specs, out_spec, out_shape, acc_shape, name, deps=()):
    scratch = [] if acc_shape is None else [pltpu.VMEM(acc_shape, F32)]
    return _pcall(_after(body, 2, deps), grid=grid, in_specs=in_specs + [ANY] * len(deps), out_specs=out_spec,
                  out_shape=out_shape, scratch_shapes=scratch, name=name,
                  compiler_params=_params(("parallel", "parallel", "arbitrary")))


def _mm_nn(a, b3, out_dtype, tm, tn, tk, name, w_outer=False, deps=()):
    M, K = a.shape
    G, _, Nb = b3.shape
    npb, nk = Nb // tn, K // tk
    out_f32 = out_dtype == F32
    body = _mm_body((((1,), (0,)), ((), ())), nk, out_f32)
    if w_outer:
        grid = (G * npb, M // tm, nk)
        ij = lambda p, q: (q, p)
    else:
        grid = (M // tm, G * npb, nk)
        ij = lambda p, q: (p, q)

    def a_map(p, q, k):
        i, j = ij(p, q)
        return (i, k)

    def b_map(p, q, k):
        i, j = ij(p, q)
        return (j // npb, k, j % npb)

    def o_map(p, q, k):
        return ij(p, q)

    def wrapped(a_ref, b_ref, o_ref, *scr):
        body(a_ref, b_ref, o_ref, *scr)

    return _mm_call(wrapped, grid, [pl.BlockSpec((tm, tk), a_map), pl.BlockSpec((None, tk, tn), b_map)],
                    pl.BlockSpec((tm, tn), o_map), _sds((M, G * Nb), out_dtype),
                    None if (nk == 1 or out_f32) else (tm, tn), name, deps)(a, b3, *deps)


def _mm_nt(a, b3, out_dtype, tm, tn, tk, name, deps=(), blocks_per_step=1):
    M, _ = a.shape
    G, Ko, Nb = b3.shape
    kpb = Nb // tk
    nk = G * kpb // blocks_per_step
    out_f32 = out_dtype == F32
    body = _mm_body((((1,), (1,)), ((), ())), nk, out_f32, blocks_per_step)

    def wrapped(a_ref, b_ref, o_ref, *scr):
        body(a_ref, b_ref, o_ref, *scr)

    if blocks_per_step > 1:
        assert tk == Nb and G % blocks_per_step == 0
        b_spec = pl.BlockSpec((blocks_per_step, tn, tk), lambda i, j, k: (k, j, 0))
    elif nk == 1 and tn == Ko:
        b_spec = pl.BlockSpec((None, tn, tk), lambda i, j, k: (0, 0, 0), pipeline_mode=pl.Buffered(1))
    else:
        b_spec = pl.BlockSpec((None, tn, tk), lambda i, j, k: (k // kpb, j, k % kpb))
    return _mm_call(wrapped, (M // tm, Ko // tn, nk),
                    [pl.BlockSpec((tm, tk * blocks_per_step), lambda i, j, k: (i, k)), b_spec],
                    pl.BlockSpec((tm, tn), lambda i, j, k: (i, j)), _sds((M, Ko), out_dtype),
                    None if (nk == 1 or out_f32) else (tm, tn), name, deps)(a, b3, *deps)


def _mm_wgrad(at, b, G, tm, tn, tk, name, deps=()):
    M, T = at.shape
    Nb = b.shape[1] // G
    npb, nk = Nb // tn, T // tk
    body = _mm_body((((1,), (0,)), ((), ())), nk, False)

    def wrapped(a_ref, b_ref, o_ref, *scr):
        body(a_ref, b_ref, o_ref, *scr)

    a = at
    in_specs = [pl.BlockSpec((tm, tk), lambda i, j, k: (i, k)), pl.BlockSpec((tk, tn), lambda i, j, k: (k, j))]
    out_spec = pl.BlockSpec((None, tm, tn), lambda i, j, k: (j // npb, i, j % npb))
    return _mm_call(wrapped, (M // tm, G * npb, nk), in_specs, out_spec, _sds((G, M, Nb), BF16),
                    None if nk == 1 else (tm, tn), name, deps)(a, b, *deps)


def _mm3_nt(x3, ws, tm, name, deps=()):
    nb, S, K = x3.shape
    Ko = ws[0].shape[1]

    def body(x_ref, w0, w1, w2, o_ref):
        n = pl.program_id(0)
        for k, w in enumerate((w0, w1, w2)):
            @pl.when(n == k)
            def _(w=w):
                o_ref[...] = lax.dot_general(x_ref[...], w[...], (((1,), (1,)), ((), ())),
                                             preferred_element_type=F32).astype(BF16)

    wspec = pl.BlockSpec((None, Ko, K), lambda n, i: (0, 0, 0))
    return _pcall(_after(body, 4, deps), grid=(nb, S // tm),
                  in_specs=[pl.BlockSpec((None, tm, K), lambda n, i: (n, i, 0)), wspec, wspec, wspec] + [ANY] * len(deps),
                  out_specs=pl.BlockSpec((None, tm, Ko), lambda n, i: (n, i, 0)), out_shape=_sds((nb, S, Ko), BF16),
                  name=name, compiler_params=_params(("arbitrary", "parallel")))(x3, *ws, *deps)


def _mm3_wgrad(at3, b3, tn, name):
    nb, M, T = at3.shape
    N = b3.shape[2]

    def body(a_ref, b_ref, o_ref):
        o_ref[...] = jnp.dot(a_ref[...], b_ref[...], preferred_element_type=F32).astype(BF16)

    return _pcall(body, grid=(nb, N // tn),
                  in_specs=[pl.BlockSpec((None, M, T), lambda n, j: (n, 0, 0)), pl.BlockSpec((None, T, tn), lambda n, j: (n, 0, j))],
                  out_specs=pl.BlockSpec((None, M, tn), lambda n, j: (n, 0, j)), out_shape=_sds((nb, M, N), BF16),
                  name=name, compiler_params=_params(("arbitrary", "parallel")))(at3, b3)


def _rsum(v):
    return jnp.sum(v, axis=0, keepdims=True)


def _rmean(v):
    return jnp.mean(v, axis=-1, keepdims=True)


def _gelu(x):
    t = jnp.tanh(_G0 * (x + _G1 * (x * x * x)))
    return x * (0.5 * (1.0 + t)), t


def _dgelu(x, t):
    return 0.5 * (1.0 + t) + 0.5 * x * (1.0 - t * t) * (_G0 * (1.0 + 3.0 * _G1 * (x * x)))


def _sigmoid(x):
    return 1.0 / (1.0 + jnp.exp(-x))


def _fill_shifted(ext, rot):
    v = ext[...]
    n = v.shape[0]
    for b in range(1, 8):
        rot[b - 1] = pltpu.roll(v, n - b, 0)


def _rows_at(ext, rot, s, tm, cs=slice(None)):
    a, b = divmod(s, 8)
    return ext[8 * a:8 * a + tm, cs] if b == 0 else rot[b - 1, 8 * a:8 * a + tm, cs]


def _causal_conv(w_ref, taps, bias, ext, rot, offset, tm, out):
    D = out.shape[1]
    for cb in range(D // LANE):
        cs = slice(cb * LANE, (cb + 1) * LANE)
        acc = None
        for k, o in zip(taps, offset):
            term = w_ref[k:k + 1, cs] * _rows_at(ext, rot, o, tm, cs)
            acc = term if acc is None else acc + term
        out[:, cs] = acc if bias is None else acc + bias[:, cs]


def _rows(*vs):
    a = jnp.stack([v.astype(F32) for v in vs])
    return jnp.pad(a, ((0, 8 - len(vs)), (0, 0)))


def _row_spec(tm, D):
    return pl.BlockSpec((tm, D), lambda i: (i, 0))


def _const_spec(shape):
    nd = len(shape)
    return pl.BlockSpec(shape, lambda i: (0,) * nd)


def _norm_fwd(xp, f, vec, name, deps=()):
    S, D = xp.shape
    tm = min(512, S)
    has_f = f is not None

    def body(*refs):
        if has_f:
            xp_ref, f_ref, vec_ref, xo_ref, h_ref, ht_ref = refs
            x = xp_ref[...] + vec_ref[0:1, :] * f_ref[...]
            xo_ref[...] = x
        else:
            xp_ref, vec_ref, h_ref, ht_ref = refs
            x = xp_ref[...]
        r = lax.rsqrt(_rmean(x * x) + EPS)
        h = (x * r) * vec_ref[1:2, :]
        h = h * (1.0 + vec_ref[2:3, :]) + vec_ref[3:4, :]
        h_ref[...] = h.astype(BF16)
        ht_ref[...] = h.T.astype(BF16)

    rs = _row_spec(tm, D)
    ins = [xp, f, vec] if has_f else [xp, vec]
    in_specs = ([rs, rs] if has_f else [rs]) + [_const_spec((8, D))]
    out_shape = ([_sds((S, D), F32)] if has_f else []) + [_sds((S, D), BF16), _sds((D, S), BF16)]
    out_specs = [rs] * (len(out_shape) - 1) + [pl.BlockSpec((D, tm), lambda i: (0, i))]
    outs = _pcall(_after(body, len(ins), deps), grid=(S // tm,), in_specs=in_specs + [ANY] * len(deps),
                  out_specs=out_specs, out_shape=out_shape, name=name,
                  compiler_params=_params(("parallel",)))(*ins, *deps)
    return (outs[0], outs[1], outs[2]) if has_f else (xp, outs[0], outs[1])


def _mixer_fwd(z, wsh, sgu_ln, wtril, bias_full, cw, cvec, name, deps=()):
    S = z.shape[0]
    D = wsh.shape[1]
    tm = CHUNK

    def body(z_ref, wsh_ref, sln_ref, wt_ref, bias_ref, cw_ref, cv_ref, oa_ref, ob_ref, oc_ref, t_ref,
             conv_ref, pe, ge, gr, cbuf):
        i = pl.program_id(0)

        @pl.when(i == 0)
        def _():
            pe[0:HALO, :] = jnp.zeros((HALO, D), F32)
            ge[0:HALO, :] = jnp.zeros((HALO, D), F32)

        def col(n):
            return z_ref[:, n * D:(n + 1) * D].astype(F32)

        pe[HALO:HALO + tm, :] = col(1) * col(2)
        q = wsh_ref[0:1, :] * pe[HALO - 2:HALO - 2 + tm, :]
        q = q + wsh_ref[1:2, :] * pe[HALO - 1:HALO - 1 + tm, :]
        q = q + wsh_ref[2:3, :] * pe[HALO:HALO + tm, :]
        act_a = col(0) * q
        oa_ref[...] = act_a.astype(BF16)
        t_ref[0] = act_a.T.astype(BF16)
        gu, _ = _gelu(col(3))
        gv, _ = _gelu(col(4))
        d = gv - _rmean(gv)
        nrm = d * lax.rsqrt(_rmean(d * d) + EPS)
        vnb = (nrm * sln_ref[0:1, :] + sln_ref[1:2, :]).astype(BF16)
        for g in range(NG):
            cs = slice(g * LANE, (g + 1) * LANE)
            mixed = jnp.dot(wt_ref[g], vnb[:, cs], preferred_element_type=F32) + bias_ref[:, cs]
            act_b = gu[:, cs] * mixed
            ob_ref[:, cs] = act_b.astype(BF16)
            t_ref[1, cs, :] = act_b.T.astype(BF16)
        ge[HALO:HALO + tm, :] = col(5) * _sigmoid(col(6))
        _fill_shifted(ge, gr)
        o0 = HALO - (CFM_K - 1)
        _causal_conv(cw_ref, range(CFM_K), cv_ref[0:1, :], ge, gr, range(o0, o0 + CFM_K), tm, cbuf)
        conv = cbuf[...]
        conv_ref[...] = conv.astype(BF16)
        d = conv - _rmean(conv)
        ln = (d * lax.rsqrt(_rmean(d * d) + EPS)) * cv_ref[1:2, :] + cv_ref[2:3, :]
        act_c = ln * _sigmoid(ln)
        oc_ref[...] = act_c.astype(BF16)
        t_ref[2] = act_c.T.astype(BF16)
        pe[0:HALO, :] = pe[tm:tm + HALO, :]
        ge[0:HALO, :] = ge[tm:tm + HALO, :]

    rs = _row_spec(tm, D)
    outs = _pcall(
        _after(body, 7, deps), grid=(S // tm,),
        in_specs=[pl.BlockSpec((tm, 7 * D), lambda i: (i, 0)), _const_spec((8, D)), _const_spec((8, D)),
                  _const_spec((NG, CHUNK, CHUNK)), _const_spec((CHUNK, D)), _const_spec((HALO, D)), _const_spec((8, D))]
        + [ANY] * len(deps),
        out_specs=[rs, rs, rs, pl.BlockSpec((3, D, tm), lambda i: (0, 0, i)), rs],
        out_shape=[_sds((S, D), BF16)] * 3 + [_sds((3, D, S), BF16), _sds((S, D), BF16)],
        scratch_shapes=[pltpu.VMEM((HALO + tm, D), F32), pltpu.VMEM((HALO + tm, D), F32),
                        pltpu.VMEM((7, HALO + tm, D), F32), pltpu.VMEM((tm, D), F32)],
        name=name, compiler_params=_params(("arbitrary",)))(z, wsh, sgu_ln, wtril, bias_full, cw, cvec, *deps)
    return outs[:3], outs[3], outs[4]


def _branch_out(acts, ws, z, name):
    S, D = acts[0].shape
    tm = min(512, S)

    def body(a0, a1, a2, w0, w1, w2, g0, g1, g2, m_ref, mt_ref, y_ref):
        m = None
        for n, (a, w, g) in enumerate(((a0, w0, g0), (a1, w1, g1), (a2, w2, g2))):
            y = jnp.dot(a[...], w[...], preferred_element_type=F32)
            y_ref[n] = y.astype(BF16)
            t = _sigmoid(g[...].astype(F32)) * y
            m = t if m is None else m + t
        m_ref[...] = m.astype(BF16)
        mt_ref[...] = m.T.astype(BF16)

    rs = _row_spec(tm, D)
    gate_specs = [pl.BlockSpec((tm, D), functools.partial(lambda i, n: (i, 7 + n), n=n)) for n in range(3)]
    return _pcall(body, grid=(S // tm,),
                  in_specs=[rs, rs, rs] + [_const_spec((D, D))] * 3 + gate_specs,
                  out_specs=[rs, pl.BlockSpec((D, tm), lambda i: (0, i)), pl.BlockSpec((3, tm, D), lambda i: (0, i, 0))],
                  out_shape=[_sds((S, D), BF16), _sds((D, S), BF16), _sds((3, S, D), BF16)], name=name,
                  compiler_params=_params(("parallel",)))(*acts, *ws, z, z, z)


def _ffn_in_swiglu(h2, w3, tm, tn, name):
    S, D = h2.shape
    F = w3.shape[2] // 2
    nj = F // tn

    def body(a_ref, wg_ref, wu_ref, gu_ref, act_ref, actt_ref):
        a = a_ref[...]
        g = jnp.dot(a, wg_ref[...], preferred_element_type=F32)
        u = jnp.dot(a, wu_ref[...], preferred_element_type=F32)
        gu_ref[0] = g.astype(BF16)
        gu_ref[1] = u.astype(BF16)
        act = (g * _sigmoid(g)) * u
        act_ref[...] = act.astype(BF16)
        actt_ref[...] = act.T.astype(BF16)

    return _pcall(body, grid=(S // tm, nj),
                  in_specs=[pl.BlockSpec((tm, D), lambda i, j: (i, 0)), pl.BlockSpec((None, D, tn), lambda i, j: (0, 0, j)),
                            pl.BlockSpec((None, D, tn), lambda i, j: (0, 0, j + nj))],
                  out_specs=[pl.BlockSpec((2, tm, tn), lambda i, j: (0, i, j)), pl.BlockSpec((tm, tn), lambda i, j: (i, j)),
                             pl.BlockSpec((tn, tm), lambda i, j: (j, i))],
                  out_shape=[_sds((2, S, F), BF16), _sds((S, F), BF16), _sds((F, S), BF16)], name=name,
                  compiler_params=_params(("parallel", "parallel")))(h2, w3, w3)


def _swiglu_bwd(dact, gu, name):
    _, S, F = gu.shape
    F2 = 2 * F
    tm = min(256, S)

    def body(d_ref, g_ref, u_ref, o_ref):
        g = g_ref[...].astype(F32)
        sg = _sigmoid(g)
        d = d_ref[...].astype(F32)
        o_ref[:, 0:F] = (d * u_ref[...].astype(F32) * (sg * (1.0 + g * (1.0 - sg)))).astype(BF16)
        o_ref[:, F:2 * F] = (d * (g * sg)).astype(BF16)

    return _pcall(body, grid=(S // tm,),
                  in_specs=[pl.BlockSpec((tm, F), lambda i: (i, 0)), pl.BlockSpec((None, tm, F), lambda i: (0, i, 0)),
                            pl.BlockSpec((None, tm, F), lambda i: (1, i, 0))],
                  out_specs=pl.BlockSpec((tm, F2), lambda i: (i, 0)), out_shape=_sds((S, F2), BF16), name=name,
                  compiler_params=_params(("parallel",)))(dact, gu, gu)


def _final_bwd(x1, f, tgt, vec, name):
    S, D = x1.shape
    tm = min(512, S)

    def body(x_ref, f_ref, t_ref, vec_ref, dx_ref, df_ref, sums_ref, loss_ref):
        @pl.when(pl.program_id(0) == 0)
        def _():
            sums_ref[...] = jnp.zeros_like(sums_ref)
            loss_ref[...] = jnp.zeros_like(loss_ref)

        gate, fg = vec_ref[0:1, :], vec_ref[1:2, :]
        fv = f_ref[...]
        x = x_ref[...] + gate * fv
        r = lax.rsqrt(_rmean(x * x) + EPS)
        xn = x * r
        diff = xn * fg - t_ref[...]
        per_tok = _rmean(diff * diff)
        loss_ref[...] += 0.5 * jnp.sum(per_tok, axis=0, keepdims=True)
        dy = diff * (1.0 / D)
        sums_ref[0:1, :] += _rsum(dy * xn)
        dxn = dy * fg
        dx = r * (dxn - xn * _rmean(dxn * xn))
        sums_ref[1:2, :] += _rsum(dx * fv)
        dx_ref[...] = dx
        df_ref[...] = (dx * gate).astype(BF16)

    rs = _row_spec(tm, D)
    return _pcall(body, grid=(S // tm,), in_specs=[rs, rs, rs, _const_spec((8, D))],
                  out_specs=[rs, rs, _const_spec((8, D)), _const_spec((8, LANE))],
                  out_shape=[_sds((S, D), F32), _sds((S, D), BF16), _sds((8, D), F32), _sds((8, LANE), F32)],
                  name=name, compiler_params=_params(("arbitrary",)))(x1, f, tgt, vec)


def _norm_bwd(xin, dh, dxup, vec, fprev, name, deps=()):
    S, D = xin.shape
    tm = min(512, S)
    has_prev = fprev is not None

    def body(*refs):
        if has_prev:
            x_ref, dh_ref, up_ref, vec_ref, fp_ref, dx_ref, dp_ref, sums_ref = refs
        else:
            x_ref, dh_ref, up_ref, vec_ref, dx_ref, sums_ref = refs

        @pl.when(pl.program_id(0) == 0)
        def _():
            sums_ref[...] = jnp.zeros_like(sums_ref)

        g, scale = vec_ref[0:1, :], vec_ref[1:2, :]
        x = x_ref[...]
        r = lax.rsqrt(_rmean(x * x) + EPS)
        xn = x * r
        dhv = dh_ref[...]
        sums_ref[0:1, :] += _rsum(dhv)
        sums_ref[1:2, :] += _rsum(dhv * (xn * g))
        dm = dhv * (1.0 + scale)
        sums_ref[2:3, :] += _rsum(dm * xn)
        dxn = dm * g
        dx = up_ref[...] + r * (dxn - xn * _rmean(dxn * xn))
        dx_ref[...] = dx
        if has_prev:
            sums_ref[3:4, :] += _rsum(dx * fp_ref[...])
            dp_ref[...] = (dx * vec_ref[2:3, :]).astype(BF16)

    rs = _row_spec(tm, D)
    ins = [xin, dh, dxup, vec] + ([fprev] if has_prev else [])
    in_specs = [rs, rs, rs, _const_spec((8, D))] + ([rs] if has_prev else [])
    out_shape = [_sds((S, D), F32)] + ([_sds((S, D), BF16)] if has_prev else []) + [_sds((8, D), F32)]
    out_specs = [rs] + ([rs] if has_prev else []) + [_const_spec((8, D))]
    outs = _pcall(_after(body, len(ins), deps), grid=(S // tm,), in_specs=in_specs + [ANY] * len(deps),
                  out_specs=out_specs, out_shape=out_shape, name=name,
                  compiler_params=_params(("arbitrary",)))(*ins, *deps)
    return (outs[0], outs[1], outs[2]) if has_prev else (outs[0], None, outs[1])


def _gate_bwd(dmerged, z, ys, name, deps=()):
    S, D = dmerged.shape
    tm = min(512, S)
    ncol = z.shape[1] // D

    def body(dm_ref, g_ref, y_ref, dy_ref, dz_ref):
        sg = _sigmoid(g_ref[...].astype(F32))
        dm = dm_ref[...].astype(F32)
        dy_ref[...] = (dm * sg).astype(BF16)
        dz_ref[...] = (dm * y_ref[...].astype(F32) * (sg * (1.0 - sg))).astype(BF16)

    branch = pl.BlockSpec((None, tm, D), lambda i, n: (n, i, 0))
    return _pcall(_after(body, 3, deps), grid=(S // tm, 3),
                  in_specs=[pl.BlockSpec((tm, D), lambda i, n: (i, 0)), pl.BlockSpec((tm, D), lambda i, n: (i, 7 + n)),
                            branch] + [ANY] * len(deps),
                  out_specs=[branch, pl.BlockSpec((tm, D), lambda i, n: (i, 7 + n))],
                  out_shape=[_sds((3, S, D), BF16), _sds((S, ncol * D), BF16)], name=name,
                  compiler_params=_params(("parallel", "arbitrary")))(dmerged, z, ys, *deps)


def _mixer_bwd(z, dacts, conv, dz, wsh, sgu_ln, wtril, wtril_t, bias_full, cw, cvec, name):
    S = z.shape[0]
    D = wsh.shape[1]
    tm = CHUNK
    nt = S // tm
    hb = tm // HALO

    def body(zc, zp, da_ref, db_ref, dc_ref, conv_ref, wsh_ref, sln_ref, wt_ref, wtt_ref, bias_ref, cw_ref, cv_ref, _dz_in,
             dz_ref, vec_ref, dcw_ref, dws_ref, dbs_ref, pe, ge, dqe, dce, gr, dcr, cbuf, dcw8):
        i = pl.program_id(0)
        rb = nt - 1 - i

        @pl.when(i == 0)
        def _():
            vec_ref[...] = jnp.zeros_like(vec_ref)
            dcw8[...] = jnp.zeros_like(dcw8)
            dws_ref[...] = jnp.zeros_like(dws_ref)
            dbs_ref[...] = jnp.zeros_like(dbs_ref)
            dqe[tm:tm + HALO, :] = jnp.zeros((HALO, D), F32)
            dce[tm:tm + HALO, :] = jnp.zeros((HALO, D), F32)

        keep = (rb > 0).astype(F32)

        def col(n):
            return zc[:, n * D:(n + 1) * D].astype(F32)

        def pcol(n):
            return zp[:, n * D:(n + 1) * D].astype(F32)

        c_a, x_a = col(1), col(2)
        pe[0:HALO, :] = keep * (pcol(1) * pcol(2))
        pe[HALO:HALO + tm, :] = c_a * x_a
        q = wsh_ref[0:1, :] * pe[HALO - 2:HALO - 2 + tm, :]
        q = q + wsh_ref[1:2, :] * pe[HALO - 1:HALO - 1 + tm, :]
        q = q + wsh_ref[2:3, :] * pe[HALO:HALO + tm, :]
        dact = da_ref[...].astype(F32)
        dz_ref[:, 0:D] = (dact * q).astype(BF16)
        dq = dact * col(0)
        dqe[0:tm, :] = dq
        dp = wsh_ref[2:3, :] * dq + wsh_ref[1:2, :] * dqe[1:1 + tm, :] + wsh_ref[0:1, :] * dqe[2:2 + tm, :]
        dz_ref[:, D:2 * D] = (dp * x_a).astype(BF16)
        dz_ref[:, 2 * D:3 * D] = (dp * c_a).astype(BF16)
        for k in range(SHORT_K):
            o = HALO - (SHORT_K - 1) + k
            vec_ref[k:k + 1, :] += _rsum(dq * pe[o:o + tm, :])
        u, v = col(3), col(4)
        gu, tu = _gelu(u)
        gv, tv = _gelu(v)
        d = gv - _rmean(gv)
        rstd = lax.rsqrt(_rmean(d * d) + EPS)
        nrm = d * rstd
        vnb = (nrm * sln_ref[0:1, :] + sln_ref[1:2, :]).astype(BF16)
        dact = db_ref[...].astype(F32)
        dvn_parts, dgu_parts = [], []
        for g in range(NG):
            cs = slice(g * LANE, (g + 1) * LANE)
            vg = vnb[:, cs]
            mixed = jnp.dot(wt_ref[g], vg, preferred_element_type=F32) + bias_ref[:, cs]
            dgu_parts.append(dact[:, cs] * mixed)
            dmixed = dact[:, cs] * gu[:, cs]
            dmb = dmixed.astype(BF16)
            dws_ref[g] += lax.dot_general(dmb, vg, (((1,), (1,)), ((), ())), preferred_element_type=F32)
            dbs_ref[g] += jnp.broadcast_to(jnp.sum(dmixed, axis=1, keepdims=True), (CHUNK, LANE))
            dvn_parts.append(jnp.dot(wtt_ref[g], dmb, preferred_element_type=F32))
        dgu = jnp.concatenate(dgu_parts, axis=1)
        dvn = jnp.concatenate(dvn_parts, axis=1)
        dz_ref[:, 3 * D:4 * D] = (dgu * _dgelu(u, tu)).astype(BF16)
        vec_ref[3:4, :] += _rsum(dvn * nrm)
        vec_ref[4:5, :] += _rsum(dvn)
        dn = dvn * sln_ref[0:1, :]
        dgv = rstd * (dn - _rmean(dn) - nrm * _rmean(dn * nrm))
        dz_ref[:, 4 * D:5 * D] = (dgv * _dgelu(v, tv)).astype(BF16)
        a_c = col(5)
        sg = _sigmoid(col(6))
        ge[0:HALO, :] = keep * (pcol(5) * _sigmoid(pcol(6)))
        ge[HALO:HALO + tm, :] = a_c * sg
        _fill_shifted(ge, gr)
        o0 = HALO - (CFM_K - 1)
        conv = conv_ref[...].astype(F32)
        d = conv - _rmean(conv)
        rstd = lax.rsqrt(_rmean(d * d) + EPS)
        nrm = d * rstd
        ln = nrm * cv_ref[1:2, :] + cv_ref[2:3, :]
        sl = _sigmoid(ln)
        dln = dc_ref[...].astype(F32) * (sl * (1.0 + ln * (1.0 - sl)))
        vec_ref[6:7, :] += _rsum(dln * nrm)
        vec_ref[7:8, :] += _rsum(dln)
        dn = dln * cv_ref[1:2, :]
        dconv = rstd * (dn - _rmean(dn) - nrm * _rmean(dn * nrm))
        vec_ref[5:6, :] += _rsum(dconv)
        dce[0:tm, :] = dconv
        _fill_shifted(dce, dcr)
        _causal_conv(cw_ref, range(CFM_K), None, dce, dcr, [CFM_K - 1 - k for k in range(CFM_K)], tm, cbuf)
        dglu = cbuf[...]
        for cb in range(D // LANE):
            cs = slice(cb * LANE, (cb + 1) * LANE)
            dcv = dce[0:tm, cs]
            for k in range(CFM_K):
                prod = dcv * _rows_at(ge, gr, o0 + k, tm, cs)
                dcw8[k, :, cs] += jnp.sum(prod.reshape(tm // 8, 8, LANE), axis=0)

        @pl.when(i == nt - 1)
        def _():
            dcw_ref[...] = jnp.sum(dcw8[...], axis=1)
        dz_ref[:, 5 * D:6 * D] = (dglu * sg).astype(BF16)
        dz_ref[:, 6 * D:7 * D] = (dglu * a_c * (sg * (1.0 - sg))).astype(BF16)
        dqe[tm:tm + HALO, :] = dqe[0:HALO, :]
        dce[tm:tm + HALO, :] = dce[0:HALO, :]

    rev = lambda i: (nt - 1 - i, 0)
    rs = pl.BlockSpec((tm, D), rev)
    cur = pl.BlockSpec((tm, 7 * D), rev)
    prev = pl.BlockSpec((HALO, 7 * D), lambda i: (jnp.maximum((nt - 1 - i) * hb - 1, 0), 0))
    ext = pltpu.VMEM((HALO + tm, D), F32)
    outs = _pcall(
        body, grid=(nt,),
        in_specs=[cur, prev] + [pl.BlockSpec((None, tm, D), functools.partial(lambda i, n: (n, nt - 1 - i, 0), n=n))
                                for n in range(3)]
        + [rs, _const_spec((8, D)), _const_spec((8, D)), _const_spec((NG, CHUNK, CHUNK)),
                  _const_spec((NG, CHUNK, CHUNK)), _const_spec((CHUNK, D)), _const_spec((HALO, D)), _const_spec((8, D)),
                  ANY],
        out_specs=[cur, _const_spec((8, D)), _const_spec((HALO, D)), _const_spec((NG, CHUNK, CHUNK)),
                   _const_spec((NG, CHUNK, LANE))],
        out_shape=[_sds(dz.shape, BF16), _sds((8, D), F32), _sds((HALO, D), F32), _sds((NG, CHUNK, CHUNK), F32),
                   _sds((NG, CHUNK, LANE), F32)],
        scratch_shapes=[ext, ext, ext, ext, pltpu.VMEM((7, HALO + tm, D), F32), pltpu.VMEM((7, HALO + tm, D), F32),
                        pltpu.VMEM((tm, D), F32), pltpu.VMEM((HALO, 8, D), F32)],
        input_output_aliases={13: 0}, name=name,
        compiler_params=_params(("arbitrary",)))(z, z, dacts, dacts, dacts, conv, wsh, sgu_ln, wtril, wtril_t, bias_full, cw,
                                                 cvec, dz)
    return outs


def _ada_fwd(c_all, w_ada_loc, name):
    nb, D = c_all.shape
    L, _, nc = w_ada_loc.shape

    def body(c_ref, w_ref, o_ref, ca_ref):
        cv = c_ref[...]
        ca = cv * _sigmoid(cv)
        ca_ref[...] = ca
        o_ref[...] = jnp.dot(ca.astype(BF16), w_ref[...].astype(BF16), preferred_element_type=F32)

    return _pcall(body, grid=(L,),
                  in_specs=[_const_spec((nb, D)), pl.BlockSpec((None, D, nc), lambda l: (l, 0, 0))],
                  out_specs=[pl.BlockSpec((None, nb, nc), lambda l: (l, 0, 0)), _const_spec((nb, D))],
                  out_shape=[_sds((L, nb, nc), F32), _sds((nb, D), F32)], name=name,
                  compiler_params=_params(("arbitrary",)))(c_all, w_ada_loc)


def _adamw(w, g, m, v):
    m = ADAM_B1 * m + (1.0 - ADAM_B1) * g
    v = ADAM_B2 * v + (1.0 - ADAM_B2) * (g * g)
    m_hat = m / (1.0 - ADAM_B1 ** ADAM_STEP)
    v_hat = v / (1.0 - ADAM_B2 ** ADAM_STEP)
    delta = -ADAM_LR * (m_hat / (jnp.sqrt(v_hat) + ADAM_EPS) + ADAM_WD * w)
    return delta, m, v


def _tile_rows(R, C, align=8):
    cap = max(align, (1536 * 1024) // (4 * C))
    best = None
    for t in range(align, R + 1, align):
        if R % t == 0 and t <= cap:
            best = t
    return R if best is None else best


def _adam_ada(ct, dm, w, m, v, name):
    L, D, nc = w.shape
    nb = ct.shape[1]
    tr = _tile_rows(D, nc)

    def body(ct_ref, dm_ref, w_ref, m_ref, v_ref, g_ref, d_ref, mo_ref, vo_ref):
        g = ct_ref[:, 0:1] * dm_ref[0:1, :]
        for b in range(1, nb):
            g = g + ct_ref[:, b:b + 1] * dm_ref[b:b + 1, :]
        g_ref[...] = g
        d_ref[...], mo_ref[...], vo_ref[...] = _adamw(w_ref[...], g, m_ref[...], v_ref[...])

    ws = pl.BlockSpec((None, tr, nc), lambda l, r: (l, r, 0))
    return _pcall(body, grid=(L, D // tr),
                  in_specs=[pl.BlockSpec((tr, nb), lambda l, r: (r, 0)), pl.BlockSpec((None, nb, nc), lambda l, r: (l, 0, 0)),
                            ws, ws, ws],
                  out_specs=[ws] * 4, out_shape=[_sds(w.shape, F32)] * 4, name=name,
                  compiler_params=_params(("parallel", "parallel")))(ct, dm, w, m, v)


def _adam_small(parts, w, m, v, name, deps=()):
    n, R, C = parts.shape
    tr = _tile_rows(R, C * n // 2)

    def body(p_ref, w_ref, m_ref, v_ref, g_ref, d_ref, mo_ref, vo_ref):
        g = p_ref[0]
        for j in range(1, n):
            g = g + p_ref[j]
        g_ref[...] = g
        d_ref[...], mo_ref[...], vo_ref[...] = _adamw(w_ref[...], g, m_ref[...], v_ref[...])

    ws = pl.BlockSpec((tr, C), lambda r: (r, 0))
    return _pcall(_after(body, 4, deps), grid=(R // tr,),
                  in_specs=[pl.BlockSpec((n, tr, C), lambda r: (0, r, 0)), ws, ws, ws] + [ANY] * len(deps),
                  out_specs=[ws] * 4, out_shape=[_sds((R, C), F32)] * 4, name=name,
                  compiler_params=_params(("parallel",)))(parts, w, m, v, *deps)


def _adam_plain(g, w, m, v, name):
    R, C = w.shape

    def body(g_ref, w_ref, m_ref, v_ref, d_ref, mo_ref, vo_ref):
        d_ref[...], mo_ref[...], vo_ref[...] = _adamw(w_ref[...], g_ref[...], m_ref[...], v_ref[...])

    ws = _const_spec((R, C))
    return _pcall(body, grid=(1,), in_specs=[ws] * 4, out_specs=[ws] * 3, out_shape=[_sds((R, C), F32)] * 3, name=name,
                  compiler_params=_params(("arbitrary",)))(g, w, m, v)


def _pair_sum(G, R1, my_c, name):
    n, R, C = G.shape
    half = n // 2
    tr = _tile_rows(R, C, align=16)

    def body(c_ref, g_ref, r_ref, o_ref):
        o_ref[...] = (g_ref[...].astype(F32) + r_ref[...].astype(F32)).astype(o_ref.dtype)

    blk = (None, tr, C)
    gs = pltpu.PrefetchScalarGridSpec(
        num_scalar_prefetch=1, grid=(half, R // tr),
        in_specs=[pl.BlockSpec(blk, lambda p, r, c: (2 * p + c[0], r, 0)), pl.BlockSpec(blk, lambda p, r, c: (p, r, 0))],
        out_specs=pl.BlockSpec(blk, lambda p, r, c: (p, r, 0)))
    return _pcall(body, grid_spec=gs, out_shape=_sds((half, R, C), G.dtype), name=name,
                  compiler_params=_params(("parallel", "parallel")))(my_c, G, R1)


def _adam_big(P, R2, my_chip, w, m, v, layer, prev, name, deps=()):
    _, R, C = P.shape
    nrecv = R2.shape[0]
    tr = _tile_rows(R, C, align=16)

    def body(p_sm, p_ref, r_ref, w_ref, m_ref, v_ref, *rest):
        g_ref, d_ref, mo_ref, vo_ref = rest[-4:]
        g = p_ref[...].astype(F32)
        for k in range(nrecv):
            g = g + r_ref[k].astype(F32)
        g_ref[...] = g
        d_ref[...], mo_ref[...], vo_ref[...] = _adamw(w_ref[...], g, m_ref[...], v_ref[...])

    ws = pl.BlockSpec((None, tr, C), lambda r, p: (layer, r, 0))
    held = [] if prev is None else list(prev)
    gs = pltpu.PrefetchScalarGridSpec(
        num_scalar_prefetch=1, grid=(R // tr,),
        in_specs=[pl.BlockSpec((None, tr, C), lambda r, p: (p[0], r, 0)),
                  pl.BlockSpec((nrecv, tr, C), lambda r, p: (0, r, 0)), ws, ws, ws] + [ANY] * (len(held) + len(deps)),
        out_specs=[ws] * 4)
    alias = {6 + i: i for i in range(len(held))}
    return _pcall(body, grid_spec=gs, out_shape=[_sds(w.shape, F32)] * 4, name=name, input_output_aliases=alias,
                  compiler_params=_params(("parallel",)))(my_chip, P, R2, w, m, v, *held, *deps)


def _place():
    return lax.axis_index("x"), lax.axis_index("y"), lax.axis_index("c")


def _all_gather(shards, name, deps=()):
    n = len(shards)

    def body(*refs):
        ins, outs = refs[:n], refs[n:2 * n]
        send_sems, recv_sems, local_sems = refs[2 * n:]
        x, y, c = _place()
        me, sibling = (x, y, c), (x, y, 1 - c)
        chips = [(1 - x, y), (x, 1 - y), (1 - x, 1 - y)]

        def slot(a, px, py, pc):
            return outs[a].at[4 * px + 2 * py + pc]

        def copy(a, k, block, to, src=None):
            return pltpu.make_async_remote_copy(
                src_ref=slot(a, *block) if src is None else src, dst_ref=slot(a, *block),
                send_sem=send_sems.at[7 * a + k], recv_sem=recv_sems.at[7 * a + k], device_id=to, device_id_type=MESH)

        mine = [pltpu.make_async_copy(ins[a], slot(a, *me), local_sems.at[a]) for a in range(n)]
        for cp in mine:
            cp.start()
        first = []
        for a in range(n):
            first.append(copy(a, 0, me, sibling, src=ins[a]))
            first += [copy(a, 1 + j, me, (*chip, c), src=ins[a]) for j, chip in enumerate(chips)]
        for cp in first:
            cp.start()
        passed = []
        for j, chip in enumerate(chips):
            for a in range(n):
                copy(a, 1 + j, (*chip, c), me).wait_recv()
                fwd = copy(a, 4 + j, (*chip, c), sibling)
                fwd.start()
                passed.append(fwd)
        for a in range(n):
            copy(a, 0, sibling, me).wait_recv()
        for j, chip in enumerate(chips):
            for a in range(n):
                copy(a, 4 + j, (*chip, 1 - c), me).wait_recv()
        for cp in first + passed:
            cp.wait_send()
        for cp in mine:
            cp.wait()

    outs = _pcall(_after(body, n, deps), in_specs=[ANY] * (n + len(deps)), out_specs=[ANY] * n,
                  out_shape=[_sds((NDEV,) + s.shape, s.dtype) for s in shards],
                  scratch_shapes=[pltpu.SemaphoreType.DMA((7 * n,)), pltpu.SemaphoreType.DMA((7 * n,)),
                                  pltpu.SemaphoreType.DMA((n,))], name=name)(*shards, *deps)
    return list(outs)


HBM = pl.BlockSpec(memory_space=pltpu.HBM)
SEM = pl.BlockSpec(memory_space=pltpu.SEMAPHORE)


def _copies(plan, refs, send_sems, recv_sems):
    return [pltpu.make_async_remote_copy(src_ref=s, dst_ref=d, send_sem=send_sems.at[k], recv_sem=recv_sems.at[k],
                                         device_id=dev, device_id_type=MESH)
            for k, (s, d, dev) in enumerate(plan(refs, *_place()))]


def _xfer_start(bufs, ncopies, plan, name, deps=()):
    n = len(bufs)

    def body(*refs):
        for cp in _copies(plan, refs[:n], refs[n], refs[n + 1]):
            cp.start()
        token = refs[2 * n + 2]
        token[...] = jnp.zeros_like(token)

    outs = _pcall(
        _after(body, n, deps), name=name,
        out_shape=(pltpu.SemaphoreType.DMA((ncopies,)), pltpu.SemaphoreType.DMA((ncopies,)),
                   *[pltpu.HBM(b.shape, b.dtype) for b in bufs], _sds((8, LANE), F32)),
        in_specs=[HBM] * n + [ANY] * len(deps),
        out_specs=(SEM, SEM, *[HBM] * n, pl.BlockSpec(memory_space=pltpu.VMEM)),
        input_output_aliases={i: 2 + i for i in range(n)},
        compiler_params=pltpu.CompilerParams(has_side_effects=pltpu.SideEffectType.DATAFLOW_SIDE_EFFECTING),
    )(*[pltpu.with_memory_space_constraint(b, pltpu.HBM) for b in bufs], *deps)
    return (outs[0], outs[1]), list(outs[2:2 + n]), outs[2 + n]


def _xfer_wait(sems, bufs, plan, after, name):
    n = len(bufs)
    after = list(after) if isinstance(after, (list, tuple)) else [after]

    def body(*refs):
        for cp in _copies(plan, refs[:n], refs[n], refs[n + 1]):
            cp.wait_send()
            cp.wait_recv()

    outs = _pcall(
        body, name=name, out_shape=tuple(pltpu.HBM(b.shape, b.dtype) for b in bufs),
        in_specs=[HBM] * n + [SEM, SEM] + [ANY] * len(after), out_specs=tuple([HBM] * n),
        input_output_aliases={i: i for i in range(n)},
        compiler_params=pltpu.CompilerParams(has_side_effects=pltpu.SideEffectType.DATAFLOW_SIDE_EFFECTING),
    )(*bufs, *sems, *after)
    return list(outs)


def _chips_of(x, y):
    return [(1 - x, y), (x, 1 - y), (1 - x, 1 - y)]


def _gather_plan1(n):
    def plan(refs, x, y, c):
        out = []
        for a in range(n):
            blk = refs[a].at[4 * x + 2 * y + c]
            out.append((blk, blk, (x, y, 1 - c)))
            out += [(blk, blk, (px, py, c)) for px, py in _chips_of(x, y)]
        return out
    return plan


def _gather_plan2(n):
    def plan(refs, x, y, c):
        out = []
        for a in range(n):
            for px, py in _chips_of(x, y):
                blk = refs[a].at[4 * px + 2 * py + c]
                out.append((blk, blk, (x, y, 1 - c)))
        return out
    return plan


def _gather_start(shards, dev, name, deps=()):
    lands = [lax.dynamic_update_slice(lax.empty((NDEV,) + s.shape, s.dtype), s[None], (dev,) + (0,) * s.ndim)
             for s in shards]
    n = len(shards)
    sems, lands, tok = _xfer_start(lands, 4 * n, _gather_plan1(n), name + "_p1_start", deps)
    return dict(sems=sems, lands=lands, tok=tok, n=n)


def _gather_mid(st, after, name):
    n = st["n"]
    lands = _xfer_wait(st["sems"], st["lands"], _gather_plan1(n), after, name + "_p1_wait")
    sems, lands, tok = _xfer_start(lands, 3 * n, _gather_plan2(n), name + "_p2_start")
    return dict(sems=sems, lands=lands, tok=tok, n=n)


def _gather_finish(st, after, name):
    return _xfer_wait(st["sems"], st["lands"], _gather_plan2(st["n"]), after, name + "_p2_wait")


def _scatter_plan1(n):
    def plan(refs, x, y, c):
        return [(refs[a].at[2 * p + 1 - c], refs[n + a].at[p], (x, y, 1 - c)) for a in range(n) for p in range(NCHIP)]
    return plan


def _scatter_plan2(n):
    def plan(refs, x, y, c):
        return [(refs[a].at[2 * px + py], refs[n + a].at[j], (px, py, c))
                for a in range(n) for j, (px, py) in enumerate(_chips_of(x, y))]
    return plan


def _scatter_start(Gs, name):
    n = len(Gs)
    R1s = [lax.empty((NCHIP,) + g.shape[1:], g.dtype) for g in Gs]
    sems, bufs, tok = _xfer_start(list(Gs) + R1s, NCHIP * n, _scatter_plan1(n), name + "_s1_start")
    return dict(sems=sems, bufs=bufs, tok=tok, n=n)


def _scatter_mid(st, after, my_c, name):
    n = st["n"]
    bufs = _xfer_wait(st["sems"], st["bufs"], _scatter_plan1(n), after, name + "_s1_wait")
    Ps = [_pair_sum(bufs[a], bufs[n + a], my_c, f"{name}_pair_sum{a}") for a in range(n)]
    R2s = [lax.empty((3,) + p.shape[1:], p.dtype) for p in Ps]
    sems, bufs, tok = _xfer_start(Ps + R2s, 3 * n, _scatter_plan2(n), name + "_s2_start")
    return dict(sems=sems, bufs=bufs, tok=tok, n=n)


def _scatter_finish(st, after, name):
    n = st["n"]
    bufs = _xfer_wait(st["sems"], st["bufs"], _scatter_plan2(n), after, name + "_s2_wait")
    return bufs[:n], bufs[n:]


SMALL_ROWS = {"norm1_g": (0, 1), "norm2_g": (1, 1), "sgu_ln_g": (2, 1), "sgu_ln_b": (3, 1), "cfm_conv_b": (4, 1),
              "cfm_ln_g": (5, 1), "cfm_ln_b": (6, 1), "b_sgu": (7, 1), "w_sgu": (8, 128), "b_ada": (136, N_MOD),
              "w_short": (142, SHORT_K), "cfm_conv_w": (145, CFM_K)}
ROWS_PER_LAYER = 176
FINAL_ROW = DEPTH * ROWS_PER_LAYER
PACK_ROWS = FINAL_ROW + 16


def _pack(get, D, layers=tuple(range(DEPTH)), tail=True):
    parts = []
    for l in layers:
        for name, (_, nrows) in SMALL_ROWS.items():
            a = get(name, l)
            parts.append(jnp.zeros((nrows * D,), F32) if a is None else a.astype(F32).reshape(nrows * D))
    if tail:
        for name in ("final_g", "loss"):
            a = get(name, None)
            parts.append(jnp.zeros((D,), F32) if a is None else a.astype(F32).reshape(D))
        parts.append(jnp.zeros(((PACK_ROWS - FINAL_ROW - 2) * D,), F32))
    return jnp.concatenate(parts).reshape(-1, D)


def _unpack(parts, name, shape):
    r0, nrows = SMALL_ROWS[name]
    return jnp.stack([parts[l][r0:r0 + nrows] for l in range(DEPTH)]).reshape(shape)


def _mm_tiles(S):
    return min(512, S), min(1024, S), min(2048, S)


def kernel(x, c, w_ada, b_ada, norm1_g, w_in, w_short, w_a_out, sgu_ln_g, sgu_ln_b, w_sgu, b_sgu, w_b_out, cfm_conv_w, cfm_conv_b, cfm_ln_g, cfm_ln_b, w_c_out, w_o, norm2_g, w_ffn_in, w_ffn_out, final_g, loss_target, m_w_ada, m_b_ada, m_norm1_g, m_w_in, m_w_short, m_w_a_out, m_sgu_ln_g, m_sgu_ln_b, m_w_sgu, m_b_sgu, m_w_b_out, m_cfm_conv_w, m_cfm_conv_b, m_cfm_ln_g, m_cfm_ln_b, m_w_c_out, m_w_o, m_norm2_g, m_w_ffn_in, m_w_ffn_out, m_final_g, v_w_ada, v_b_ada, v_norm1_g, v_w_in, v_w_short, v_w_a_out, v_sgu_ln_g, v_sgu_ln_b, v_w_sgu, v_b_sgu, v_w_b_out, v_cfm_conv_w, v_cfm_conv_b, v_cfm_ln_g, v_cfm_ln_b, v_w_c_out, v_w_o, v_norm2_g, v_w_ffn_in, v_w_ffn_out, v_final_g):
    W = dict(w_ada=w_ada, b_ada=b_ada, norm1_g=norm1_g, w_in=w_in, w_short=w_short, w_a_out=w_a_out, sgu_ln_g=sgu_ln_g,
             sgu_ln_b=sgu_ln_b, w_sgu=w_sgu, b_sgu=b_sgu, w_b_out=w_b_out, cfm_conv_w=cfm_conv_w, cfm_conv_b=cfm_conv_b,
             cfm_ln_g=cfm_ln_g, cfm_ln_b=cfm_ln_b, w_c_out=w_c_out, w_o=w_o, norm2_g=norm2_g, w_ffn_in=w_ffn_in,
             w_ffn_out=w_ffn_out, final_g=final_g)
    Mo = dict(w_ada=m_w_ada, b_ada=m_b_ada, norm1_g=m_norm1_g, w_in=m_w_in, w_short=m_w_short, w_a_out=m_w_a_out,
              sgu_ln_g=m_sgu_ln_g, sgu_ln_b=m_sgu_ln_b, w_sgu=m_w_sgu, b_sgu=m_b_sgu, w_b_out=m_w_b_out,
              cfm_conv_w=m_cfm_conv_w, cfm_conv_b=m_cfm_conv_b, cfm_ln_g=m_cfm_ln_g, cfm_ln_b=m_cfm_ln_b,
              w_c_out=m_w_c_out, w_o=m_w_o, norm2_g=m_norm2_g, w_ffn_in=m_w_ffn_in, w_ffn_out=m_w_ffn_out,
              final_g=m_final_g)
    Vo = dict(w_ada=v_w_ada, b_ada=v_b_ada, norm1_g=v_norm1_g, w_in=v_w_in, w_short=v_w_short, w_a_out=v_w_a_out,
              sgu_ln_g=v_sgu_ln_g, sgu_ln_b=v_sgu_ln_b, w_sgu=v_w_sgu, b_sgu=v_b_sgu, w_b_out=v_w_b_out,
              cfm_conv_w=v_cfm_conv_w, cfm_conv_b=v_cfm_conv_b, cfm_ln_g=v_cfm_ln_g, cfm_ln_b=v_cfm_ln_b,
              w_c_out=v_w_c_out, w_o=v_w_o, norm2_g=v_norm2_g, w_ffn_in=v_w_ffn_in, w_ffn_out=v_w_ffn_out,
              final_g=v_final_g)
    order = ["w_ada", "b_ada", "norm1_g", "w_in", "w_short", "w_a_out", "sgu_ln_g", "sgu_ln_b", "w_sgu", "b_sgu",
             "w_b_out", "cfm_conv_w", "cfm_conv_b", "cfm_ln_g", "cfm_ln_b", "w_c_out", "w_o", "norm2_g", "w_ffn_in",
             "w_ffn_out", "final_g"]

    assert DEPTH == 2, "the weight-gather schedule below is written for two layers"
    S, D = x.shape[1], x.shape[2]
    F2 = w_ffn_in.shape[2] * NDEV
    FF = F2 // 2
    xi, yi, ci = _place()
    dev = 4 * xi + 2 * yi + ci
    my_c = jnp.reshape(ci, (1,)).astype(jnp.int32)
    my_chip = jnp.reshape(2 * xi + yi, (1,)).astype(jnp.int32)
    tm, tm_big, tm_huge = _mm_tiles(S)
    x0 = x.reshape(S, D)
    tgt = loss_target.reshape(S, D)

    def shards_of(l):
        return [w_in[l].astype(BF16), w_a_out[l].astype(BF16), w_b_out[l].astype(BF16), w_c_out[l].astype(BF16),
                w_o[l].astype(BF16), w_ffn_in[l].astype(BF16), w_ffn_out[l].astype(BF16)]

    c_all = _all_gather([jnp.pad(c, ((0, 7), (0, 0)))], "ag_c")[0][:, 0, :]
    modpart, c_act = _ada_fwd(c_all, w_ada, "ada_fwd")
    ncol = modpart.shape[2]
    mg = _all_gather([modpart.reshape(DEPTH * NDEV, ncol)], "ag_mod")[0].reshape(NDEV, DEPTH, NDEV, ncol)
    mine = lax.dynamic_index_in_dim(mg, dev, axis=2, keepdims=False)
    mod = (jnp.transpose(mine, (1, 0, 2)).reshape(DEPTH, N_MOD * D) + b_ada).reshape(DEPTH, N_MOD, D)

    ncs = w_short.shape[2]
    ag_in0 = _gather_start([w_in[0].astype(BF16), w_short.reshape(DEPTH * SHORT_K, ncs),
                            cfm_conv_w.reshape(DEPTH * CFM_K, ncs)], dev, "ag_w_in0", deps=(mod,))
    W, Mo, Vo = lax.optimization_barrier((ag_in0["tok"], (W, Mo, Vo)))[1]
    (norm1_g, norm2_g, w_in, w_a_out, w_b_out, w_c_out, w_o, w_ffn_in, w_ffn_out, sgu_ln_g, sgu_ln_b, w_sgu, b_sgu,
     cfm_conv_b, cfm_ln_g, cfm_ln_b, final_g) = [W[k] for k in (
         "norm1_g", "norm2_g", "w_in", "w_a_out", "w_b_out", "w_c_out", "w_o", "w_ffn_in", "w_ffn_out", "sgu_ln_g",
         "sgu_ln_b", "w_sgu", "b_sgu", "cfm_conv_b", "cfm_ln_g", "cfm_ln_b", "final_g")]
    m_w_ada, v_w_ada = Mo["w_ada"], Vo["w_ada"]
    xl0, h0, ht0 = _norm_fwd(x0, None, _rows(jnp.zeros((D,), F32), norm1_g[0], mod[0, 1], mod[0, 0]), "norm1_fwd0",
                             deps=(ag_in0["tok"],))
    ag_rest0 = _gather_start(shards_of(0)[1:], dev, "ag_rest0", deps=(h0,))

    tril = jnp.tril(jnp.ones((CHUNK, CHUNK), dtype=bool))

    def layer_consts(l):
        wt = jnp.where(tril[None], w_sgu[l], 0.0).astype(BF16)
        return dict(sgu_ln=_rows(sgu_ln_g[l], sgu_ln_b[l]), wtril=wt, wtril_t=jnp.swapaxes(wt, 1, 2),
                    bias_full=jnp.repeat(b_sgu[l].T, LANE, axis=1), cvec=_rows(cfm_conv_b[l], cfm_ln_g[l], cfm_ln_b[l]))

    def rest_of(g):
        return dict(w_a=g[0].reshape(1, D, D), w_b=g[1].reshape(1, D, D), w_c=g[2].reshape(1, D, D),
                    w_o=g[3].reshape(1, D, D), w_fi=jnp.transpose(g[4], (1, 0, 2)).reshape(1, D, F2),
                    w_fo=g[5].reshape(1, FF, D))

    sharded_small = ("w_short", "cfm_conv_w")

    def param_get(T):
        def get(name, l):
            if name == "final_g":
                return T[name]
            return None if name in sharded_small or name == "loss" else T[name][l]
        return get

    packs = [[_pack(param_get(T), D, layers=(l,), tail=(l == DEPTH - 1)) for T in (W, Mo, Vo)] for l in range(DEPTH)]
    ag_in0 = _gather_mid(ag_in0, [ag_rest0["tok"]] + [p for part in packs for p in part], "ag_w_in0")
    (w_sgu, b_sgu, sgu_ln_g, sgu_ln_b, cfm_conv_b, cfm_ln_g, cfm_ln_b), conv_wmv_in = lax.optimization_barrier(
        (ag_in0["tok"], ((w_sgu, b_sgu, sgu_ln_g, sgu_ln_b, cfm_conv_b, cfm_ln_g, cfm_ln_b),
                         [(T["w_short"], T["cfm_conv_w"]) for T in (W, Mo, Vo)])))[1]
    consts = [layer_consts(l) for l in range(DEPTH)]
    ncr = DEPTH * (SHORT_K + CFM_K)
    padr = (-ncr) % 8
    convw_wmv = [jnp.pad(jnp.concatenate([a.reshape(-1, ncs), b.reshape(-1, ncs)]), ((0, padr), (0, 0)))
                 for a, b in conv_wmv_in]
    g_in0 = _gather_finish(ag_in0, [*convw_wmv] + [a for cl in consts for a in cl.values()], "ag_w_in0")
    w_short_full = jnp.transpose(g_in0[1], (1, 0, 2)).reshape(DEPTH, SHORT_K, D)
    cfm_w_full = jnp.transpose(g_in0[2], (1, 0, 2)).reshape(DEPTH, CFM_K, D)
    for l in range(DEPTH):
        consts[l]["wsh"] = jnp.pad(w_short_full[l], ((0, 8 - SHORT_K), (0, 0)))
        consts[l]["cw"] = jnp.pad(cfm_w_full[l], ((0, HALO - CFM_K), (0, 0)))
    Wg = [dict(w_in=g_in0[0]), None]
    ag_l1 = None
    nin = w_in.shape[2]
    tn_in = nin if nin % 256 == 0 and nin <= 1280 else 256
    tn_fi = 512 if F2 % 512 == 0 else 256
    tn_dw = min(256, D)

    saved = []
    xcur, fprev, gprev = x0, None, None
    for l in range(DEPTH):
        sh1, sc1, g1, sh2, sc2, g2 = [mod[l, k] for k in range(N_MOD)]
        cl = consts[l]
        if l == 0:
            xl, h, ht = xl0, h0, ht0
        else:
            vec1 = _rows(gprev, norm1_g[l], sc1, sh1)
            ag_l1 = _gather_mid(ag_l1, fprev, f"ag_w{l}")
            xl, h, ht = _norm_fwd(xcur, fprev, vec1, f"norm1_fwd{l}", deps=(ag_l1["tok"],))
            g = _gather_finish(ag_l1, h, f"ag_w{l}")
            Wg[l] = dict(w_in=g[0], **rest_of(g[1:]))
        wl = Wg[l]
        z = _mm_nn(h, wl["w_in"], BF16, tm_huge, tn_in, D, f"mm_in{l}", w_outer=True)
        mix_deps = ()
        if l == 0:
            ag_rest0 = _gather_mid(ag_rest0, z, "ag_rest0")
            mix_deps = (ag_rest0["tok"],)
            if DEPTH > 1:
                ag_l1 = _gather_start(shards_of(1), dev, "ag_w1")
                mix_deps += (ag_l1["tok"],)
        acts, acts_t, conv = _mixer_fwd(z, cl["wsh"], cl["sgu_ln"], cl["wtril"], cl["bias_full"], cl["cw"], cl["cvec"],
                                        f"mixer_fwd{l}", deps=mix_deps)
        if l == 0:
            wl.update(rest_of(_gather_finish(ag_rest0, acts[0], "ag_rest0")))
        merged, merged_t, ys = _branch_out(acts, [wl["w_a"][0], wl["w_b"][0], wl["w_c"][0]], z, f"branch_out{l}")
        o = _mm_nn(merged, wl["w_o"], F32, tm_big, D, D, f"mm_o{l}")
        x1, h2, h2t = _norm_fwd(xl, o, _rows(g1, norm2_g[l], sc2, sh2), f"norm2_fwd{l}")
        gu, act, act_t = _ffn_in_swiglu(h2, wl["w_fi"], tm_huge, 256, f"mm_ffn_in{l}")
        f = _mm_nn(act, wl["w_fo"], F32, tm_big, D, FF, f"mm_ffn_out{l}")
        saved.append(dict(xl=xl, ht=ht, z=z, acts_t=acts_t, conv=conv, ys=ys, merged_t=merged_t, o=o, x1=x1, h2t=h2t, gu=gu,
                          act_t=act_t, f=f, consts=cl, mod=(sh1, sc1, g1, sh2, sc2, g2)))
        xcur, fprev, gprev = x1, f, g2

    last = saved[-1]
    dxup, dfb, fsums, loss_blk = _final_bwd(last["x1"], last["f"], tgt, _rows(last["mod"][5], final_g), "final_bwd")
    loss_row = jnp.pad(loss_blk[0, 0:1], (0, D - 1))
    dgate2_next = fsums[1]
    small = [dict() for _ in range(DEPTH)]
    dmods = [None] * DEPTH
    nfi = w_ffn_in.shape[2]
    early_names, late_names = ["w_ffn_out", "w_ffn_in", "w_o"], ["w_a_out", "w_b_out", "w_c_out", "w_in"]
    results = {n: None for n in early_names + late_names}

    def adam_group(names, Ps, R2s, l, deps=()):
        for n, p, r2 in zip(names, Ps, R2s):
            results[n] = _adam_big(p, r2, my_chip, W[n], Mo[n], Vo[n], l, results[n], f"adam_{n}{l}", deps)

    deferred = []
    late_prev = None
    ag_s1, gathered1 = None, None
    tk_w = min(2048, S)
    tn_dw_in = tn_in // 2 if tn_in == 1280 else tn_in
    for l in reversed(range(DEPTH)):
        sv, wl, cl = saved[l], Wg[l], saved[l]["consts"]
        sh1, sc1, g1, sh2, sc2, g2 = sv["mod"]
        dact = _mm_nt(dfb, wl["w_fo"], BF16, tm_big, FF, D, f"mm_dact{l}",
                      deps=() if late_prev is None else (late_prev["tok"], ag_s1["tok"]))
        g_fo = _mm_wgrad(sv["act_t"], dfb, 1, FF // 2, D, tk_w, f"mm_dw_ffn_out{l}")
        dgu = _swiglu_bwd(dact, sv["gu"], f"swiglu_bwd{l}")
        dh2 = _mm_nt(dgu, wl["w_fi"], F32, tm_big, D, F2, f"mm_dh2{l}")
        if late_prev is not None:
            deferred.append((late_names, *_scatter_finish(late_prev, dh2, f"rs_late{l + 1}"), l + 1))
            late_prev = None
        g_fi = _mm_wgrad(sv["h2t"], dgu, 1, D, tn_fi, S, f"mm_dw_ffn_in{l}")
        if ag_s1 is not None:
            ag_s1 = _gather_mid(ag_s1, g_fi, "ag_small1")
        dx1, dob, s2 = _norm_bwd(sv["x1"], dh2, dxup, _rows(norm2_g[l], sc2, g1), sv["o"], f"norm2_bwd{l}",
                                 deps=() if ag_s1 is None else (ag_s1["tok"],))
        dmerged = _mm_nt(dob, wl["w_o"], BF16, tm_big, D, D, f"mm_dmerged{l}")
        g_o = _mm_wgrad(sv["merged_t"], dob, 1, D, tn_dw, S, f"mm_dw_o{l}")
        early = _scatter_start([g_fo.reshape(NDEV, FF // NDEV, D),
                                jnp.transpose(g_fi.reshape(D, NDEV, nfi), (1, 0, 2)),
                                g_o.reshape(NDEV, D // NDEV, D)], f"rs_early{l}")
        dys, dz = _gate_bwd(dmerged, sv["z"], sv["ys"], f"gate_bwd{l}", deps=(early["tok"],))
        if ag_s1 is not None:
            gathered1 = _gather_finish(ag_s1, dys, "ag_small1")[0]
            ag_s1 = None
        early = _scatter_mid(early, dys, my_c, f"rs_early{l}")
        dacts = _mm3_nt(dys, [wl["w_a"], wl["w_b"], wl["w_c"]], tm_big, f"mm_dact_abc{l}", deps=(early["tok"],))
        g3 = _mm3_wgrad(sv["acts_t"], dys, tn_dw, f"mm_dw_abc{l}")
        g_abc = [g3[n] for n in range(3)]
        dz, mvec, dcw, dws, dbs = _mixer_bwd(sv["z"], dacts, sv["conv"], dz, cl["wsh"], cl["sgu_ln"], cl["wtril"],
                                             cl["wtril_t"], cl["bias_full"], cl["cw"], cl["cvec"], f"mixer_bwd{l}")
        dh = _mm_nt(dz, wl["w_in"], F32, tm_big, D, tn_in, f"mm_dh{l}",
                    blocks_per_step=2 if (tn_in == nin and wl["w_in"].shape[0] % 2 == 0) else 1)
        g_in = _mm_wgrad(sv["ht"], dz, NDEV, D, tn_dw_in, S, f"mm_dw_in{l}")
        late = _scatter_start([g.reshape(NDEV, D // NDEV, D) for g in g_abc] + [g_in], f"rs_late{l}")
        if l > 0:
            pv = saved[l - 1]
            dxup, dfb, s1 = _norm_bwd(sv["xl"], dh, dx1, _rows(norm1_g[l], sc1, pv["mod"][5]), pv["f"], f"norm1_bwd{l}",
                                      deps=(late["tok"],))
        else:
            dxup, dfb, s1 = _norm_bwd(sv["xl"], dh, dx1, _rows(norm1_g[l], sc1), None, f"norm1_bwd{l}", deps=(late["tok"],))
        deferred.append((early_names, *_scatter_finish(early, dxup, f"rs_early{l}"), l))
        dmods[l] = jnp.stack([s1[0], s1[1], s2[3], s2[0], s2[1], dgate2_next])
        dgate2_next = s1[3]
        small[l] = dict(norm1_g=s1[2], norm2_g=s2[2], sgu_ln_g=mvec[3], sgu_ln_b=mvec[4], cfm_conv_b=mvec[5],
                        cfm_ln_g=mvec[6], cfm_ln_b=mvec[7], b_sgu=dbs[:, :, 0],
                        w_sgu=jnp.where(tril[None], dws, 0.0), b_ada=dmods[l], w_short=mvec[0:SHORT_K],
                        cfm_conv_w=dcw[0:CFM_K])
        small_get = lambda name, k: {"final_g": fsums[0], "loss": loss_row}.get(name) if k is None else small[k][name]
        if l > 0:
            late_prev = _scatter_mid(late, dxup, my_c, f"rs_late{l}")
            ag_s1 = _gather_start([_pack(small_get, D, layers=(l,), tail=True)], dev, "ag_small1", deps=(late_prev["tok"],))
    grad_x = dxup.reshape(x.shape)

    gathered0 = _all_gather([_pack(small_get, D, layers=(0,), tail=False)], "ag_small0", deps=(dxup,))[0]
    late_prev = _scatter_mid(late, gathered0, my_c, "rs_late0")
    gath = [gathered0, gathered1]
    small_res = [_adam_small(gath[l], *packs[l], name=f"adam_small{l}", deps=(late_prev["tok"],)) for l in range(DEPTH)]
    kinds = [[small_res[l][k] for l in range(DEPTH)] for k in range(4)]
    loss = kinds[0][DEPTH - 1][ROWS_PER_LAYER + 1, 0]
    out = {}
    for name in order:
        if name in SMALL_ROWS and name not in sharded_small:
            out[name] = tuple(_unpack(p, name, W[name].shape) for p in kinds)
    out["final_g"] = tuple(p[DEPTH - 1][ROWS_PER_LAYER] for p in kinds)

    def my_cols(name):
        full = _unpack(kinds[0], name, (DEPTH, SMALL_ROWS[name][1], D))
        return lax.dynamic_slice_in_dim(full, dev * ncs, ncs, axis=2)

    gcs = jnp.concatenate([my_cols("w_short").reshape(-1, ncs), my_cols("cfm_conv_w").reshape(-1, ncs)])
    cd, cm, cv = _adam_plain(jnp.pad(gcs, ((0, padr), (0, 0))), *convw_wmv, "adam_convw")
    nsh = DEPTH * SHORT_K
    out["w_short"] = tuple(a[0:nsh].reshape(w_short.shape) for a in (gcs, cd, cm, cv))
    out["cfm_conv_w"] = tuple(a[nsh:ncr].reshape(cfm_conv_w.shape) for a in (gcs, cd, cm, cv))

    r_ada = SMALL_ROWS["b_ada"][0]
    dm_all = jnp.stack([gath[l][:, r_ada:r_ada + N_MOD, :].reshape(NDEV, N_MOD * D) for l in range(DEPTH)])
    dm_mine = lax.dynamic_slice_in_dim(dm_all, dev * ncol, ncol, axis=2)
    out["w_ada"] = tuple(_adam_ada(jnp.transpose(c_act), dm_mine, w_ada, m_w_ada, v_w_ada, "adam_ada"))

    for names, Ps, R2s, l in deferred:
        adam_group(names, Ps, R2s, l, deps=(late_prev["tok"],))
    adam_group(late_names, *_scatter_finish(late_prev, results["w_o"][0], "rs_late0"), 0)
    for n in early_names + late_names:
        out[n] = tuple(results[n])

    grads = [out[n][0] for n in order]
    deltas = [out[n][1] for n in order]
    new_m = [out[n][2] for n in order]
    new_v = [out[n][3] for n in order]
    return (loss, grad_x, *grads, *deltas, *new_m, *new_v)
```

```python
import functools
import math

import jax
import jax.numpy as jnp
from jax import lax
from jax.experimental import pallas as pl
from jax.experimental.pallas import tpu as pltpu

F32, BF16 = jnp.float32, jnp.bfloat16
NDEV = 8
NCHIP = NDEV // 2
DEPTH = 2
EPS = 1e-6
CHUNK = 128
NG = 8
SHORT_K = 3
CFM_K = 31
HALO = 32
N_MOD = 6
LANE = 128
VMEM_LIMIT = 56 * 1024 * 1024
ADAM_LR, ADAM_B1, ADAM_B2, ADAM_EPS, ADAM_WD, ADAM_STEP = 0.001, 0.9, 0.999, 1e-08, 0.01, 10
_G0 = math.sqrt(2.0 / math.pi)
_G1 = 0.044715
MESH = pl.DeviceIdType.MESH
ANY = pl.BlockSpec(memory_space=pl.ANY)


def _pcall(body, **kw):
    return pl.pallas_call(body, **kw)


def _params(sem=None):
    return pltpu.CompilerParams(dimension_semantics=sem, vmem_limit_bytes=VMEM_LIMIT)


def _sds(shape, dtype):
    return jax.ShapeDtypeStruct(tuple(shape), dtype)


def _mm_body(dims, nk, out_f32, blocks=1):
    def body(a_ref, b_ref, o_ref, *scr):
        k = pl.program_id(2)
        if blocks == 1:
            part = lax.dot_general(a_ref[...], b_ref[...], dims, preferred_element_type=F32)
        else:
            w = a_ref.shape[1] // blocks
            part = None
            for g in range(blocks):
                t = lax.dot_general(a_ref[:, g * w:(g + 1) * w], b_ref[g], dims, preferred_element_type=F32)
                part = t if part is None else part + t
        if nk == 1:
            o_ref[...] = part.reshape(o_ref.shape).astype(o_ref.dtype)
        elif out_f32:
            @pl.when(k == 0)
            def _():
                o_ref[...] = part.reshape(o_ref.shape)

            @pl.when(k > 0)
            def _():
                o_ref[...] += part.reshape(o_ref.shape)
        else:
            acc = scr[0]

            @pl.when(k == 0)
            def _():
                acc[...] = part

            @pl.when(k > 0)
            def _():
                acc[...] += part

            @pl.when(k == nk - 1)
            def _():
                o_ref[...] = acc[...].astype(o_ref.dtype)
    return body


def _after(body, n_in, deps):
    nd = len(deps)
    if nd == 0:
        return body

    def ordered(*refs):
        return body(*refs[:n_in], *refs[n_in + nd:])
    return ordered


def _mm_call(body, grid, in_specs, out_spec, out_shape, acc_shape, name, deps=()):
    scratch = [] if acc_shape is None else [pltpu.VMEM(acc_shape, F32)]
    return _pcall(_after(body, 2, deps), grid=grid, in_specs=in_specs + [ANY] * len(deps), out_specs=out_spec,
                  out_shape=out_shape, scratch_shapes=scratch, name=name,
                  compiler_params=_params(("parallel", "parallel", "arbitrary")))


def _mm_nn(a, b3, out_dtype, tm, tn, tk, name, w_outer=False, deps=()):
    M, K = a.shape
    G, _, Nb = b3.shape
    npb, nk = Nb // tn, K // tk
    out_f32 = out_dtype == F32
    body = _mm_body((((1,), (0,)), ((), ())), nk, out_f32)
    if w_outer:
        grid = (G * npb, M // tm, nk)
        ij = lambda p, q: (q, p)
    else:
        grid = (M // tm, G * npb, nk)
        ij = lambda p, q: (p, q)

    def a_map(p, q, k):
        i, j = ij(p, q)
        return (i, k)

    def b_map(p, q, k):
        i, j = ij(p, q)
        return (j // npb, k, j % npb)

    def o_map(p, q, k):
        return ij(p, q)

    def wrapped(a_ref, b_ref, o_ref, *scr):
        body(a_ref, b_ref, o_ref, *scr)

    return _mm_call(wrapped, grid, [pl.BlockSpec((tm, tk), a_map), pl.BlockSpec((None, tk, tn), b_map)],
                    pl.BlockSpec((tm, tn), o_map), _sds((M, G * Nb), out_dtype),
                    None if (nk == 1 or out_f32) else (tm, tn), name, deps)(a, b3, *deps)


def _mm_nt(a, b3, out_dtype, tm, tn, tk, name, deps=(), blocks_per_step=1):
    M, _ = a.shape
    G, Ko, Nb = b3.shape
    kpb = Nb // tk
    nk = G * kpb // blocks_per_step
    out_f32 = out_dtype == F32
    body = _mm_body((((1,), (1,)), ((), ())), nk, out_f32, blocks_per_step)

    def wrapped(a_ref, b_ref, o_ref, *scr):
        body(a_ref, b_ref, o_ref, *scr)

    if blocks_per_step > 1:
        assert tk == Nb and G % blocks_per_step == 0
        b_spec = pl.BlockSpec((blocks_per_step, tn, tk), lambda i, j, k: (k, j, 0))
    else:
        b_spec = pl.BlockSpec((None, tn, tk), lambda i, j, k: (k // kpb, j, k % kpb))
    return _mm_call(wrapped, (M // tm, Ko // tn, nk),
                    [pl.BlockSpec((tm, tk * blocks_per_step), lambda i, j, k: (i, k)), b_spec],
                    pl.BlockSpec((tm, tn), lambda i, j, k: (i, j)), _sds((M, Ko), out_dtype),
                    None if (nk == 1 or out_f32) else (tm, tn), name, deps)(a, b3, *deps)


def _mm_wgrad(at, b, G, tm, tn, tk, name, deps=()):
    M, T = at.shape
    Nb = b.shape[1] // G
    npb, nk = Nb // tn, T // tk
    body = _mm_body((((1,), (0,)), ((), ())), nk, False)

    def wrapped(a_ref, b_ref, o_ref, *scr):
        body(a_ref, b_ref, o_ref, *scr)

    a = at
    in_specs = [pl.BlockSpec((tm, tk), lambda i, j, k: (i, k)), pl.BlockSpec((tk, tn), lambda i, j, k: (k, j))]
    out_spec = pl.BlockSpec((None, tm, tn), lambda i, j, k: (j // npb, i, j % npb))
    return _mm_call(wrapped, (M // tm, G * npb, nk), in_specs, out_spec, _sds((G, M, Nb), BF16),
                    None if nk == 1 else (tm, tn), name, deps)(a, b, *deps)


def _mm3_nt(x3, ws, tm, name, deps=()):
    nb, S, K = x3.shape
    Ko = ws[0].shape[1]

    def body(x_ref, w0, w1, w2, o_ref):
        n = pl.program_id(0)
        for k, w in enumerate((w0, w1, w2)):
            @pl.when(n == k)
            def _(w=w):
                o_ref[...] = lax.dot_general(x_ref[...], w[...], (((1,), (1,)), ((), ())),
                                             preferred_element_type=F32).astype(BF16)

    wspec = pl.BlockSpec((None, Ko, K), lambda n, i: (0, 0, 0))
    return _pcall(_after(body, 4, deps), grid=(nb, S // tm),
                  in_specs=[pl.BlockSpec((None, tm, K), lambda n, i: (n, i, 0)), wspec, wspec, wspec] + [ANY] * len(deps),
                  out_specs=pl.BlockSpec((None, tm, Ko), lambda n, i: (n, i, 0)), out_shape=_sds((nb, S, Ko), BF16),
                  name=name, compiler_params=_params(("arbitrary", "parallel")))(x3, *ws, *deps)


def _mm3_wgrad(at3, b3, tn, name):
    nb, M, T = at3.shape
    N = b3.shape[2]

    def body(a_ref, b_ref, o_ref):
        o_ref[...] = jnp.dot(a_ref[...], b_ref[...], preferred_element_type=F32).astype(BF16)

    return _pcall(body, grid=(nb, N // tn),
                  in_specs=[pl.BlockSpec((None, M, T), lambda n, j: (n, 0, 0)), pl.BlockSpec((None, T, tn), lambda n, j: (n, 0, j))],
                  out_specs=pl.BlockSpec((None, M, tn), lambda n, j: (n, 0, j)), out_shape=_sds((nb, M, N), BF16),
                  name=name, compiler_params=_params(("arbitrary", "parallel")))(at3, b3)


def _rsum(v):
    return jnp.sum(v, axis=0, keepdims=True)


def _rmean(v):
    return jnp.mean(v, axis=-1, keepdims=True)


def _gelu(x):
    t = jnp.tanh(_G0 * (x + _G1 * (x * x * x)))
    return x * (0.5 * (1.0 + t)), t


def _dgelu(x, t):
    return 0.5 * (1.0 + t) + 0.5 * x * (1.0 - t * t) * (_G0 * (1.0 + 3.0 * _G1 * (x * x)))


def _sigmoid(x):
    return 1.0 / (1.0 + jnp.exp(-x))


def _fill_shifted(ext, rot):
    v = ext[...]
    n = v.shape[0]
    for b in range(1, 8):
        rot[b - 1] = pltpu.roll(v, n - b, 0)


def _rows_at(ext, rot, s, tm, cs=slice(None)):
    a, b = divmod(s, 8)
    return ext[8 * a:8 * a + tm, cs] if b == 0 else rot[b - 1, 8 * a:8 * a + tm, cs]


def _causal_conv(w_ref, taps, bias, ext, rot, offset, tm, out):
    D = out.shape[1]
    for cb in range(D // LANE):
        cs = slice(cb * LANE, (cb + 1) * LANE)
        acc = None
        for k, o in zip(taps, offset):
            term = w_ref[k:k + 1, cs] * _rows_at(ext, rot, o, tm, cs)
            acc = term if acc is None else acc + term
        out[:, cs] = acc if bias is None else acc + bias[:, cs]


def _rows(*vs):
    a = jnp.stack([v.astype(F32) for v in vs])
    return jnp.pad(a, ((0, 8 - len(vs)), (0, 0)))


def _row_spec(tm, D):
    return pl.BlockSpec((tm, D), lambda i: (i, 0))


def _const_spec(shape):
    nd = len(shape)
    return pl.BlockSpec(shape, lambda i: (0,) * nd)


def _norm_fwd(xp, f, vec, name, deps=()):
    S, D = xp.shape
    tm = min(512, S)
    has_f = f is not None

    def body(*refs):
        if has_f:
            xp_ref, f_ref, vec_ref, xo_ref, h_ref, ht_ref = refs
            x = xp_ref[...] + vec_ref[0:1, :] * f_ref[...]
            xo_ref[...] = x
        else:
            xp_ref, vec_ref, h_ref, ht_ref = refs
            x = xp_ref[...]
        r = lax.rsqrt(_rmean(x * x) + EPS)
        h = (x * r) * vec_ref[1:2, :]
        h = h * (1.0 + vec_ref[2:3, :]) + vec_ref[3:4, :]
        h_ref[...] = h.astype(BF16)
        ht_ref[...] = h.T.astype(BF16)

    rs = _row_spec(tm, D)
    ins = [xp, f, vec] if has_f else [xp, vec]
    in_specs = ([rs, rs] if has_f else [rs]) + [_const_spec((8, D))]
    out_shape = ([_sds((S, D), F32)] if has_f else []) + [_sds((S, D), BF16), _sds((D, S), BF16)]
    out_specs = [rs] * (len(out_shape) - 1) + [pl.BlockSpec((D, tm), lambda i: (0, i))]
    outs = _pcall(_after(body, len(ins), deps), grid=(S // tm,), in_specs=in_specs + [ANY] * len(deps),
                  out_specs=out_specs, out_shape=out_shape, name=name,
                  compiler_params=_params(("parallel",)))(*ins, *deps)
    return (outs[0], outs[1], outs[2]) if has_f else (xp, outs[0], outs[1])


def _mixer_fwd(z, wsh, sgu_ln, wtril, bias_full, cw, cvec, name, deps=()):
    S = z.shape[0]
    D = wsh.shape[1]
    tm = CHUNK

    def body(z_ref, wsh_ref, sln_ref, wt_ref, bias_ref, cw_ref, cv_ref, oa_ref, ob_ref, oc_ref, t_ref,
             conv_ref, pe, ge, gr, cbuf):
        i = pl.program_id(0)

        @pl.when(i == 0)
        def _():
            pe[0:HALO, :] = jnp.zeros((HALO, D), F32)
            ge[0:HALO, :] = jnp.zeros((HALO, D), F32)

        def col(n):
            return z_ref[:, n * D:(n + 1) * D].astype(F32)

        pe[HALO:HALO + tm, :] = col(1) * col(2)
        q = wsh_ref[0:1, :] * pe[HALO - 2:HALO - 2 + tm, :]
        q = q + wsh_ref[1:2, :] * pe[HALO - 1:HALO - 1 + tm, :]
        q = q + wsh_ref[2:3, :] * pe[HALO:HALO + tm, :]
        act_a = col(0) * q
        oa_ref[...] = act_a.astype(BF16)
        t_ref[0] = act_a.T.astype(BF16)
        gu, _ = _gelu(col(3))
        gv, _ = _gelu(col(4))
        d = gv - _rmean(gv)
        nrm = d * lax.rsqrt(_rmean(d * d) + EPS)
        vnb = (nrm * sln_ref[0:1, :] + sln_ref[1:2, :]).astype(BF16)
        for g in range(NG):
            cs = slice(g * LANE, (g + 1) * LANE)
            mixed = jnp.dot(wt_ref[g], vnb[:, cs], preferred_element_type=F32) + bias_ref[:, cs]
            act_b = gu[:, cs] * mixed
            ob_ref[:, cs] = act_b.astype(BF16)
            t_ref[1, cs, :] = act_b.T.astype(BF16)
        ge[HALO:HALO + tm, :] = col(5) * _sigmoid(col(6))
        _fill_shifted(ge, gr)
        o0 = HALO - (CFM_K - 1)
        _causal_conv(cw_ref, range(CFM_K), cv_ref[0:1, :], ge, gr, range(o0, o0 + CFM_K), tm, cbuf)
        conv = cbuf[...]
        conv_ref[...] = conv.astype(BF16)
        d = conv - _rmean(conv)
        ln = (d * lax.rsqrt(_rmean(d * d) + EPS)) * cv_ref[1:2, :] + cv_ref[2:3, :]
        act_c = ln * _sigmoid(ln)
        oc_ref[...] = act_c.astype(BF16)
        t_ref[2] = act_c.T.astype(BF16)
        pe[0:HALO, :] = pe[tm:tm + HALO, :]
        ge[0:HALO, :] = ge[tm:tm + HALO, :]

    rs = _row_spec(tm, D)
    outs = _pcall(
        _after(body, 7, deps), grid=(S // tm,),
        in_specs=[pl.BlockSpec((tm, 7 * D), lambda i: (i, 0)), _const_spec((8, D)), _const_spec((8, D)),
                  _const_spec((NG, CHUNK, CHUNK)), _const_spec((CHUNK, D)), _const_spec((HALO, D)), _const_spec((8, D))]
        + [ANY] * len(deps),
        out_specs=[rs, rs, rs, pl.BlockSpec((3, D, tm), lambda i: (0, 0, i)), rs],
        out_shape=[_sds((S, D), BF16)] * 3 + [_sds((3, D, S), BF16), _sds((S, D), BF16)],
        scratch_shapes=[pltpu.VMEM((HALO + tm, D), F32), pltpu.VMEM((HALO + tm, D), F32),
                        pltpu.VMEM((7, HALO + tm, D), F32), pltpu.VMEM((tm, D), F32)],
        name=name, compiler_params=_params(("arbitrary",)))(z, wsh, sgu_ln, wtril, bias_full, cw, cvec, *deps)
    return outs[:3], outs[3], outs[4]


def _branch_out(acts, ws, z, name):
    S, D = acts[0].shape
    tm = min(512, S)

    def body(a0, a1, a2, w0, w1, w2, g0, g1, g2, m_ref, mt_ref, y_ref):
        m = None
        for n, (a, w, g) in enumerate(((a0, w0, g0), (a1, w1, g1), (a2, w2, g2))):
            y = jnp.dot(a[...], w[...], preferred_element_type=F32)
            y_ref[n] = y.astype(BF16)
            t = _sigmoid(g[...].astype(F32)) * y
            m = t if m is None else m + t
        m_ref[...] = m.astype(BF16)
        mt_ref[...] = m.T.astype(BF16)

    rs = _row_spec(tm, D)
    gate_specs = [pl.BlockSpec((tm, D), functools.partial(lambda i, n: (i, 7 + n), n=n)) for n in range(3)]
    return _pcall(body, grid=(S // tm,),
                  in_specs=[rs, rs, rs] + [_const_spec((D, D))] * 3 + gate_specs,
                  out_specs=[rs, pl.BlockSpec((D, tm), lambda i: (0, i)), pl.BlockSpec((3, tm, D), lambda i: (0, i, 0))],
                  out_shape=[_sds((S, D), BF16), _sds((D, S), BF16), _sds((3, S, D), BF16)], name=name,
                  compiler_params=_params(("parallel",)))(*acts, *ws, z, z, z)


def _ffn_in_swiglu(h2, w3, tm, tn, name):
    S, D = h2.shape
    F = w3.shape[2] // 2
    nj = F // tn

    def body(a_ref, wg_ref, wu_ref, gu_ref, act_ref, actt_ref):
        a = a_ref[...]
        g = jnp.dot(a, wg_ref[...], preferred_element_type=F32)
        u = jnp.dot(a, wu_ref[...], preferred_element_type=F32)
        gu_ref[0] = g.astype(BF16)
        gu_ref[1] = u.astype(BF16)
        act = (g * _sigmoid(g)) * u
        act_ref[...] = act.astype(BF16)
        actt_ref[...] = act.T.astype(BF16)

    return _pcall(body, grid=(S // tm, nj),
                  in_specs=[pl.BlockSpec((tm, D), lambda i, j: (i, 0)), pl.BlockSpec((None, D, tn), lambda i, j: (0, 0, j)),
                            pl.BlockSpec((None, D, tn), lambda i, j: (0, 0, j + nj))],
                  out_specs=[pl.BlockSpec((2, tm, tn), lambda i, j: (0, i, j)), pl.BlockSpec((tm, tn), lambda i, j: (i, j)),
                             pl.BlockSpec((tn, tm), lambda i, j: (j, i))],
                  out_shape=[_sds((2, S, F), BF16), _sds((S, F), BF16), _sds((F, S), BF16)], name=name,
                  compiler_params=_params(("parallel", "parallel")))(h2, w3, w3)


def _swiglu_bwd(dact, gu, name):
    _, S, F = gu.shape
    F2 = 2 * F
    tm = min(256, S)

    def body(d_ref, g_ref, u_ref, o_ref):
        g = g_ref[...].astype(F32)
        sg = _sigmoid(g)
        d = d_ref[...].astype(F32)
        o_ref[:, 0:F] = (d * u_ref[...].astype(F32) * (sg * (1.0 + g * (1.0 - sg)))).astype(BF16)
        o_ref[:, F:2 * F] = (d * (g * sg)).astype(BF16)

    return _pcall(body, grid=(S // tm,),
                  in_specs=[pl.BlockSpec((tm, F), lambda i: (i, 0)), pl.BlockSpec((None, tm, F), lambda i: (0, i, 0)),
                            pl.BlockSpec((None, tm, F), lambda i: (1, i, 0))],
                  out_specs=pl.BlockSpec((tm, F2), lambda i: (i, 0)), out_shape=_sds((S, F2), BF16), name=name,
                  compiler_params=_params(("parallel",)))(dact, gu, gu)


def _final_bwd(x1, f, tgt, vec, name):
    S, D = x1.shape
    tm = min(512, S)

    def body(x_ref, f_ref, t_ref, vec_ref, dx_ref, df_ref, sums_ref, loss_ref):
        @pl.when(pl.program_id(0) == 0)
        def _():
            sums_ref[...] = jnp.zeros_like(sums_ref)
            loss_ref[...] = jnp.zeros_like(loss_ref)

        gate, fg = vec_ref[0:1, :], vec_ref[1:2, :]
        fv = f_ref[...]
        x = x_ref[...] + gate * fv
        r = lax.rsqrt(_rmean(x * x) + EPS)
        xn = x * r
        diff = xn * fg - t_ref[...]
        per_tok = _rmean(diff * diff)
        loss_ref[...] += 0.5 * jnp.sum(per_tok, axis=0, keepdims=True)
        dy = diff * (1.0 / D)
        sums_ref[0:1, :] += _rsum(dy * xn)
        dxn = dy * fg
        dx = r * (dxn - xn * _rmean(dxn * xn))
        sums_ref[1:2, :] += _rsum(dx * fv)
        dx_ref[...] = dx
        df_ref[...] = (dx * gate).astype(BF16)

    rs = _row_spec(tm, D)
    return _pcall(body, grid=(S // tm,), in_specs=[rs, rs, rs, _const_spec((8, D))],
                  out_specs=[rs, rs, _const_spec((8, D)), _const_spec((8, LANE))],
                  out_shape=[_sds((S, D), F32), _sds((S, D), BF16), _sds((8, D), F32), _sds((8, LANE), F32)],
                  name=name, compiler_params=_params(("arbitrary",)))(x1, f, tgt, vec)


def _norm_bwd(xin, dh, dxup, vec, fprev, name, deps=()):
    S, D = xin.shape
    tm = min(512, S)
    has_prev = fprev is not None

    def body(*refs):
        if has_prev:
            x_ref, dh_ref, up_ref, vec_ref, fp_ref, dx_ref, dp_ref, sums_ref = refs
        else:
            x_ref, dh_ref, up_ref, vec_ref, dx_ref, sums_ref = refs

        @pl.when(pl.program_id(0) == 0)
        def _():
            sums_ref[...] = jnp.zeros_like(sums_ref)

        g, scale = vec_ref[0:1, :], vec_ref[1:2, :]
        x = x_ref[...]
        r = lax.rsqrt(_rmean(x * x) + EPS)
        xn = x * r
        dhv = dh_ref[...]
        sums_ref[0:1, :] += _rsum(dhv)
        sums_ref[1:2, :] += _rsum(dhv * (xn * g))
        dm = dhv * (1.0 + scale)
        sums_ref[2:3, :] += _rsum(dm * xn)
        dxn = dm * g
        dx = up_ref[...] + r * (dxn - xn * _rmean(dxn * xn))
        dx_ref[...] = dx
        if has_prev:
            sums_ref[3:4, :] += _rsum(dx * fp_ref[...])
            dp_ref[...] = (dx * vec_ref[2:3, :]).astype(BF16)

    rs = _row_spec(tm, D)
    ins = [xin, dh, dxup, vec] + ([fprev] if has_prev else [])
    in_specs = [rs, rs, rs, _const_spec((8, D))] + ([rs] if has_prev else [])
    out_shape = [_sds((S, D), F32)] + ([_sds((S, D), BF16)] if has_prev else []) + [_sds((8, D), F32)]
    out_specs = [rs] + ([rs] if has_prev else []) + [_const_spec((8, D))]
    outs = _pcall(_after(body, len(ins), deps), grid=(S // tm,), in_specs=in_specs + [ANY] * len(deps),
                  out_specs=out_specs, out_shape=out_shape, name=name,
                  compiler_params=_params(("arbitrary",)))(*ins, *deps)
    return (outs[0], outs[1], outs[2]) if has_prev else (outs[0], None, outs[1])


def _gate_bwd(dmerged, z, ys, name, deps=()):
    S, D = dmerged.shape
    tm = min(512, S)
    ncol = z.shape[1] // D

    def body(dm_ref, g_ref, y_ref, dy_ref, dz_ref):
        sg = _sigmoid(g_ref[...].astype(F32))
        dm = dm_ref[...].astype(F32)
        dy_ref[...] = (dm * sg).astype(BF16)
        dz_ref[...] = (dm * y_ref[...].astype(F32) * (sg * (1.0 - sg))).astype(BF16)

    branch = pl.BlockSpec((None, tm, D), lambda i, n: (n, i, 0))
    return _pcall(_after(body, 3, deps), grid=(S // tm, 3),
                  in_specs=[pl.BlockSpec((tm, D), lambda i, n: (i, 0)), pl.BlockSpec((tm, D), lambda i, n: (i, 7 + n)),
                            branch] + [ANY] * len(deps),
                  out_specs=[branch, pl.BlockSpec((tm, D), lambda i, n: (i, 7 + n))],
                  out_shape=[_sds((3, S, D), BF16), _sds((S, ncol * D), BF16)], name=name,
                  compiler_params=_params(("parallel", "arbitrary")))(dmerged, z, ys, *deps)


def _mixer_bwd(z, dacts, conv, dz, wsh, sgu_ln, wtril, wtril_t, bias_full, cw, cvec, name):
    S = z.shape[0]
    D = wsh.shape[1]
    tm = CHUNK
    nt = S // tm
    hb = tm // HALO

    def body(zc, zp, da_ref, db_ref, dc_ref, conv_ref, wsh_ref, sln_ref, wt_ref, wtt_ref, bias_ref, cw_ref, cv_ref, _dz_in,
             dz_ref, vec_ref, dcw_ref, dws_ref, dbs_ref, pe, ge, dqe, dce, gr, dcr, cbuf, dcw8):
        i = pl.program_id(0)
        rb = nt - 1 - i

        @pl.when(i == 0)
        def _():
            vec_ref[...] = jnp.zeros_like(vec_ref)
            dcw8[...] = jnp.zeros_like(dcw8)
            dws_ref[...] = jnp.zeros_like(dws_ref)
            dbs_ref[...] = jnp.zeros_like(dbs_ref)
            dqe[tm:tm + HALO, :] = jnp.zeros((HALO, D), F32)
            dce[tm:tm + HALO, :] = jnp.zeros((HALO, D), F32)

        keep = (rb > 0).astype(F32)

        def col(n):
            return zc[:, n * D:(n + 1) * D].astype(F32)

        def pcol(n):
            return zp[:, n * D:(n + 1) * D].astype(F32)

        c_a, x_a = col(1), col(2)
        pe[0:HALO, :] = keep * (pcol(1) * pcol(2))
        pe[HALO:HALO + tm, :] = c_a * x_a
        q = wsh_ref[0:1, :] * pe[HALO - 2:HALO - 2 + tm, :]
        q = q + wsh_ref[1:2, :] * pe[HALO - 1:HALO - 1 + tm, :]
        q = q + wsh_ref[2:3, :] * pe[HALO:HALO + tm, :]
        dact = da_ref[...].astype(F32)
        dz_ref[:, 0:D] = (dact * q).astype(BF16)
        dq = dact * col(0)
        dqe[0:tm, :] = dq
        dp = wsh_ref[2:3, :] * dq + wsh_ref[1:2, :] * dqe[1:1 + tm, :] + wsh_ref[0:1, :] * dqe[2:2 + tm, :]
        dz_ref[:, D:2 * D] = (dp * x_a).astype(BF16)
        dz_ref[:, 2 * D:3 * D] = (dp * c_a).astype(BF16)
        for k in range(SHORT_K):
            o = HALO - (SHORT_K - 1) + k
            vec_ref[k:k + 1, :] += _rsum(dq * pe[o:o + tm, :])
        u, v = col(3), col(4)
        gu, tu = _gelu(u)
        gv, tv = _gelu(v)
        d = gv - _rmean(gv)
        rstd = lax.rsqrt(_rmean(d * d) + EPS)
        nrm = d * rstd
        vnb = (nrm * sln_ref[0:1, :] + sln_ref[1:2, :]).astype(BF16)
        dact = db_ref[...].astype(F32)
        dvn_parts, dgu_parts = [], []
        for g in range(NG):
            cs = slice(g * LANE, (g + 1) * LANE)
            vg = vnb[:, cs]
            mixed = jnp.dot(wt_ref[g], vg, preferred_element_type=F32) + bias_ref[:, cs]
            dgu_parts.append(dact[:, cs] * mixed)
            dmixed = dact[:, cs] * gu[:, cs]
            dmb = dmixed.astype(BF16)
            dws_ref[g] += lax.dot_general(dmb, vg, (((1,), (1,)), ((), ())), preferred_element_type=F32)
            dbs_ref[g] += jnp.broadcast_to(jnp.sum(dmixed, axis=1, keepdims=True), (CHUNK, LANE))
            dvn_parts.append(jnp.dot(wtt_ref[g], dmb, preferred_element_type=F32))
        dgu = jnp.concatenate(dgu_parts, axis=1)
        dvn = jnp.concatenate(dvn_parts, axis=1)
        dz_ref[:, 3 * D:4 * D] = (dgu * _dgelu(u, tu)).astype(BF16)
        vec_ref[3:4, :] += _rsum(dvn * nrm)
        vec_ref[4:5, :] += _rsum(dvn)
        dn = dvn * sln_ref[0:1, :]
        dgv = rstd * (dn - _rmean(dn) - nrm * _rmean(dn * nrm))
        dz_ref[:, 4 * D:5 * D] = (dgv * _dgelu(v, tv)).astype(BF16)
        a_c = col(5)
        sg = _sigmoid(col(6))
        ge[0:HALO, :] = keep * (pcol(5) * _sigmoid(pcol(6)))
        ge[HALO:HALO + tm, :] = a_c * sg
        _fill_shifted(ge, gr)
        o0 = HALO - (CFM_K - 1)
        conv = conv_ref[...].astype(F32)
        d = conv - _rmean(conv)
        rstd = lax.rsqrt(_rmean(d * d) + EPS)
        nrm = d * rstd
        ln = nrm * cv_ref[1:2, :] + cv_ref[2:3, :]
        sl = _sigmoid(ln)
        dln = dc_ref[...].astype(F32) * (sl * (1.0 + ln * (1.0 - sl)))
        vec_ref[6:7, :] += _rsum(dln * nrm)
        vec_ref[7:8, :] += _rsum(dln)
        dn = dln * cv_ref[1:2, :]
        dconv = rstd * (dn - _rmean(dn) - nrm * _rmean(dn * nrm))
        vec_ref[5:6, :] += _rsum(dconv)
        dce[0:tm, :] = dconv
        _fill_shifted(dce, dcr)
        _causal_conv(cw_ref, range(CFM_K), None, dce, dcr, [CFM_K - 1 - k for k in range(CFM_K)], tm, cbuf)
        dglu = cbuf[...]
        for cb in range(D // LANE):
            cs = slice(cb * LANE, (cb + 1) * LANE)
            dcv = dce[0:tm, cs]
            for k in range(CFM_K):
                prod = dcv * _rows_at(ge, gr, o0 + k, tm, cs)
                dcw8[k, :, cs] += jnp.sum(prod.reshape(tm // 8, 8, LANE), axis=0)

        @pl.when(i == nt - 1)
        def _():
            dcw_ref[...] = jnp.sum(dcw8[...], axis=1)
        dz_ref[:, 5 * D:6 * D] = (dglu * sg).astype(BF16)
        dz_ref[:, 6 * D:7 * D] = (dglu * a_c * (sg * (1.0 - sg))).astype(BF16)
        dqe[tm:tm + HALO, :] = dqe[0:HALO, :]
        dce[tm:tm + HALO, :] = dce[0:HALO, :]

    rev = lambda i: (nt - 1 - i, 0)
    rs = pl.BlockSpec((tm, D), rev)
    cur = pl.BlockSpec((tm, 7 * D), rev)
    prev = pl.BlockSpec((HALO, 7 * D), lambda i: (jnp.maximum((nt - 1 - i) * hb - 1, 0), 0))
    ext = pltpu.VMEM((HALO + tm, D), F32)
    outs = _pcall(
        body, grid=(nt,),
        in_specs=[cur, prev] + [pl.BlockSpec((None, tm, D), functools.partial(lambda i, n: (n, nt - 1 - i, 0), n=n))
                                for n in range(3)]
        + [rs, _const_spec((8, D)), _const_spec((8, D)), _const_spec((NG, CHUNK, CHUNK)),
                  _const_spec((NG, CHUNK, CHUNK)), _const_spec((CHUNK, D)), _const_spec((HALO, D)), _const_spec((8, D)),
                  ANY],
        out_specs=[cur, _const_spec((8, D)), _const_spec((HALO, D)), _const_spec((NG, CHUNK, CHUNK)),
                   _const_spec((NG, CHUNK, LANE))],
        out_shape=[_sds(dz.shape, BF16), _sds((8, D), F32), _sds((HALO, D), F32), _sds((NG, CHUNK, CHUNK), F32),
                   _sds((NG, CHUNK, LANE), F32)],
        scratch_shapes=[ext, ext, ext, ext, pltpu.VMEM((7, HALO + tm, D), F32), pltpu.VMEM((7, HALO + tm, D), F32),
                        pltpu.VMEM((tm, D), F32), pltpu.VMEM((HALO, 8, D), F32)],
        input_output_aliases={13: 0}, name=name,
        compiler_params=_params(("arbitrary",)))(z, z, dacts, dacts, dacts, conv, wsh, sgu_ln, wtril, wtril_t, bias_full, cw,
                                                 cvec, dz)
    return outs


def _ada_fwd(c_all, w_ada_loc, name):
    nb, D = c_all.shape
    L, _, nc = w_ada_loc.shape

    def body(c_ref, w_ref, o_ref, ca_ref):
        cv = c_ref[...]
        ca = cv * _sigmoid(cv)
        ca_ref[...] = ca
        o_ref[...] = jnp.dot(ca.astype(BF16), w_ref[...].astype(BF16), preferred_element_type=F32)

    return _pcall(body, grid=(L,),
                  in_specs=[_const_spec((nb, D)), pl.BlockSpec((None, D, nc), lambda l: (l, 0, 0))],
                  out_specs=[pl.BlockSpec((None, nb, nc), lambda l: (l, 0, 0)), _const_spec((nb, D))],
                  out_shape=[_sds((L, nb, nc), F32), _sds((nb, D), F32)], name=name,
                  compiler_params=_params(("arbitrary",)))(c_all, w_ada_loc)


def _adamw(w, g, m, v):
    m = ADAM_B1 * m + (1.0 - ADAM_B1) * g
    v = ADAM_B2 * v + (1.0 - ADAM_B2) * (g * g)
    m_hat = m / (1.0 - ADAM_B1 ** ADAM_STEP)
    v_hat = v / (1.0 - ADAM_B2 ** ADAM_STEP)
    delta = -ADAM_LR * (m_hat / (jnp.sqrt(v_hat) + ADAM_EPS) + ADAM_WD * w)
    return delta, m, v


def _tile_rows(R, C, align=8):
    cap = max(align, (1536 * 1024) // (4 * C))
    best = None
    for t in range(align, R + 1, align):
        if R % t == 0 and t <= cap:
            best = t
    return R if best is None else best


def _adam_ada(ct, dm, w, m, v, name):
    L, D, nc = w.shape
    nb = ct.shape[1]
    tr = _tile_rows(D, nc)

    def body(ct_ref, dm_ref, w_ref, m_ref, v_ref, g_ref, d_ref, mo_ref, vo_ref):
        g = ct_ref[:, 0:1] * dm_ref[0:1, :]
        for b in range(1, nb):
            g = g + ct_ref[:, b:b + 1] * dm_ref[b:b + 1, :]
        g_ref[...] = g
        d_ref[...], mo_ref[...], vo_ref[...] = _adamw(w_ref[...], g, m_ref[...], v_ref[...])

    ws = pl.BlockSpec((None, tr, nc), lambda l, r: (l, r, 0))
    return _pcall(body, grid=(L, D // tr),
                  in_specs=[pl.BlockSpec((tr, nb), lambda l, r: (r, 0)), pl.BlockSpec((None, nb, nc), lambda l, r: (l, 0, 0)),
                            ws, ws, ws],
                  out_specs=[ws] * 4, out_shape=[_sds(w.shape, F32)] * 4, name=name,
                  compiler_params=_params(("parallel", "parallel")))(ct, dm, w, m, v)


def _adam_small(parts, w, m, v, name, deps=(), single_rows=()):
    n, R, C = parts.shape
    tr = _tile_rows(R, C * n // 2)
    nl = len(single_rows[0]) if single_rows else 0

    def body(p_ref, w_ref, m_ref, v_ref, g_ref, d_ref, mo_ref, vo_ref, *single):
        g = p_ref[0]
        for j in range(1, n):
            g = g + p_ref[j]
        d, mo, vo = _adamw(w_ref[...], g, m_ref[...], v_ref[...])
        g_ref[...], d_ref[...], mo_ref[...], vo_ref[...] = g, d, mo, vo
        step = pl.program_id(0)
        for pi, rows in enumerate(single_rows):
            for l, row in enumerate(rows):
                @pl.when(step == row // tr)
                def _(pi=pi, l=l, off=row % tr):
                    for k, val in enumerate((g, d, mo, vo)):
                        single[4 * pi + k][l:l + 1, :] = val[off:off + 1, :]

    ws = pl.BlockSpec((tr, C), lambda r: (r, 0))
    one = pl.BlockSpec((nl, C), lambda r: (0, 0))
    outs = _pcall(_after(body, 4, deps), grid=(R // tr,),
                  in_specs=[pl.BlockSpec((n, tr, C), lambda r: (0, r, 0)), ws, ws, ws] + [ANY] * len(deps),
                  out_specs=[ws] * 4 + [one] * (4 * len(single_rows)),
                  out_shape=[_sds((R, C), F32)] * 4 + [_sds((nl, C), F32)] * (4 * len(single_rows)), name=name,
                  compiler_params=_params(("arbitrary",)))(parts, w, m, v, *deps)
    return outs[:4], outs[4:]


def _adam_plain(g, w, m, v, name):
    R, C = w.shape

    def body(g_ref, w_ref, m_ref, v_ref, d_ref, mo_ref, vo_ref):
        d_ref[...], mo_ref[...], vo_ref[...] = _adamw(w_ref[...], g_ref[...], m_ref[...], v_ref[...])

    ws = _const_spec((R, C))
    return _pcall(body, grid=(1,), in_specs=[ws] * 4, out_specs=[ws] * 3, out_shape=[_sds((R, C), F32)] * 3, name=name,
                  compiler_params=_params(("arbitrary",)))(g, w, m, v)


def _pair_sum(G, R1, my_c, name):
    n, R, C = G.shape
    half = n // 2
    tr = _tile_rows(R, C, align=16)

    def body(c_ref, g_ref, r_ref, o_ref):
        o_ref[...] = (g_ref[...].astype(F32) + r_ref[...].astype(F32)).astype(o_ref.dtype)

    blk = (None, tr, C)
    gs = pltpu.PrefetchScalarGridSpec(
        num_scalar_prefetch=1, grid=(half, R // tr),
        in_specs=[pl.BlockSpec(blk, lambda p, r, c: (2 * p + c[0], r, 0)), pl.BlockSpec(blk, lambda p, r, c: (p, r, 0))],
        out_specs=pl.BlockSpec(blk, lambda p, r, c: (p, r, 0)))
    return _pcall(body, grid_spec=gs, out_shape=_sds((half, R, C), G.dtype), name=name,
                  compiler_params=_params(("parallel", "parallel")))(my_c, G, R1)


def _adam_big(P, R2, my_chip, w, m, v, layer, prev, name, deps=()):
    _, R, C = P.shape
    nrecv = R2.shape[0]
    tr = _tile_rows(R, C, align=16)

    def body(p_sm, p_ref, r_ref, w_ref, m_ref, v_ref, *rest):
        g_ref, d_ref, mo_ref, vo_ref = rest[-4:]
        g = p_ref[...].astype(F32)
        for k in range(nrecv):
            g = g + r_ref[k].astype(F32)
        g_ref[...] = g
        d_ref[...], mo_ref[...], vo_ref[...] = _adamw(w_ref[...], g, m_ref[...], v_ref[...])

    ws = pl.BlockSpec((None, tr, C), lambda r, p: (layer, r, 0))
    held = [] if prev is None else list(prev)
    gs = pltpu.PrefetchScalarGridSpec(
        num_scalar_prefetch=1, grid=(R // tr,),
        in_specs=[pl.BlockSpec((None, tr, C), lambda r, p: (p[0], r, 0)),
                  pl.BlockSpec((nrecv, tr, C), lambda r, p: (0, r, 0)), ws, ws, ws] + [ANY] * (len(held) + len(deps)),
        out_specs=[ws] * 4)
    alias = {6 + i: i for i in range(len(held))}
    return _pcall(body, grid_spec=gs, out_shape=[_sds(w.shape, F32)] * 4, name=name, input_output_aliases=alias,
                  compiler_params=_params(("parallel",)))(my_chip, P, R2, w, m, v, *held, *deps)


def _place():
    return lax.axis_index("x"), lax.axis_index("y"), lax.axis_index("c")


def _all_gather(shards, name, deps=()):
    n = len(shards)

    def body(*refs):
        ins, outs = refs[:n], refs[n:2 * n]
        send_sems, recv_sems, local_sems = refs[2 * n:]
        x, y, c = _place()
        me, sibling = (x, y, c), (x, y, 1 - c)
        chips = [(1 - x, y), (x, 1 - y), (1 - x, 1 - y)]

        def slot(a, px, py, pc):
            return outs[a].at[4 * px + 2 * py + pc]

        def copy(a, k, block, to, src=None):
            return pltpu.make_async_remote_copy(
                src_ref=slot(a, *block) if src is None else src, dst_ref=slot(a, *block),
                send_sem=send_sems.at[7 * a + k], recv_sem=recv_sems.at[7 * a + k], device_id=to, device_id_type=MESH)

        mine = [pltpu.make_async_copy(ins[a], slot(a, *me), local_sems.at[a]) for a in range(n)]
        for cp in mine:
            cp.start()
        first = []
        for a in range(n):
            first.append(copy(a, 0, me, sibling, src=ins[a]))
            first += [copy(a, 1 + j, me, (*chip, c), src=ins[a]) for j, chip in enumerate(chips)]
        for cp in first:
            cp.start()
        passed = []
        for j, chip in enumerate(chips):
            for a in range(n):
                copy(a, 1 + j, (*chip, c), me).wait_recv()
                fwd = copy(a, 4 + j, (*chip, c), sibling)
                fwd.start()
                passed.append(fwd)
        for a in range(n):
            copy(a, 0, sibling, me).wait_recv()
        for j, chip in enumerate(chips):
            for a in range(n):
                copy(a, 4 + j, (*chip, 1 - c), me).wait_recv()
        for cp in first + passed:
            cp.wait_send()
        for cp in mine:
            cp.wait()

    outs = _pcall(_after(body, n, deps), in_specs=[ANY] * (n + len(deps)), out_specs=[ANY] * n,
                  out_shape=[_sds((NDEV,) + s.shape, s.dtype) for s in shards],
                  scratch_shapes=[pltpu.SemaphoreType.DMA((7 * n,)), pltpu.SemaphoreType.DMA((7 * n,)),
                                  pltpu.SemaphoreType.DMA((n,))], name=name)(*shards, *deps)
    return list(outs)


HBM = pl.BlockSpec(memory_space=pltpu.HBM)
SEM = pl.BlockSpec(memory_space=pltpu.SEMAPHORE)


def _copies(plan, refs, send_sems, recv_sems):
    return [pltpu.make_async_remote_copy(src_ref=s, dst_ref=d, send_sem=send_sems.at[k], recv_sem=recv_sems.at[k],
                                         device_id=dev, device_id_type=MESH)
            for k, (s, d, dev) in enumerate(plan(refs, *_place()))]


def _xfer_start(bufs, ncopies, plan, name, deps=()):
    n = len(bufs)

    def body(*refs):
        for cp in _copies(plan, refs[:n], refs[n], refs[n + 1]):
            cp.start()
        token = refs[2 * n + 2]
        token[...] = jnp.zeros_like(token)

    outs = _pcall(
        _after(body, n, deps), name=name,
        out_shape=(pltpu.SemaphoreType.DMA((ncopies,)), pltpu.SemaphoreType.DMA((ncopies,)),
                   *[pltpu.HBM(b.shape, b.dtype) for b in bufs], _sds((8, LANE), F32)),
        in_specs=[HBM] * n + [ANY] * len(deps),
        out_specs=(SEM, SEM, *[HBM] * n, pl.BlockSpec(memory_space=pltpu.VMEM)),
        input_output_aliases={i: 2 + i for i in range(n)},
        compiler_params=pltpu.CompilerParams(has_side_effects=pltpu.SideEffectType.DATAFLOW_SIDE_EFFECTING),
    )(*[pltpu.with_memory_space_constraint(b, pltpu.HBM) for b in bufs], *deps)
    return (outs[0], outs[1]), list(outs[2:2 + n]), outs[2 + n]


def _xfer_wait(sems, bufs, plan, after, name):
    n = len(bufs)
    after = list(after) if isinstance(after, (list, tuple)) else [after]

    def body(*refs):
        for cp in _copies(plan, refs[:n], refs[n], refs[n + 1]):
            cp.wait_send()
            cp.wait_recv()

    outs = _pcall(
        body, name=name, out_shape=tuple(pltpu.HBM(b.shape, b.dtype) for b in bufs),
        in_specs=[HBM] * n + [SEM, SEM] + [ANY] * len(after), out_specs=tuple([HBM] * n),
        input_output_aliases={i: i for i in range(n)},
        compiler_params=pltpu.CompilerParams(has_side_effects=pltpu.SideEffectType.DATAFLOW_SIDE_EFFECTING),
    )(*bufs, *sems, *after)
    return list(outs)


def _chips_of(x, y):
    return [(1 - x, y), (x, 1 - y), (1 - x, 1 - y)]


def _gather_plan1(n):
    def plan(refs, x, y, c):
        out = []
        for a in range(n):
            blk = refs[a].at[4 * x + 2 * y + c]
            out.append((blk, blk, (x, y, 1 - c)))
            out += [(blk, blk, (px, py, c)) for px, py in _chips_of(x, y)]
        return out
    return plan


def _gather_plan2(n):
    def plan(refs, x, y, c):
        out = []
        for a in range(n):
            for px, py in _chips_of(x, y):
                blk = refs[a].at[4 * px + 2 * py + c]
                out.append((blk, blk, (x, y, 1 - c)))
        return out
    return plan


def _gather_start(shards, dev, name, deps=()):
    lands = [lax.dynamic_update_slice(lax.empty((NDEV,) + s.shape, s.dtype), s[None], (dev,) + (0,) * s.ndim)
             for s in shards]
    n = len(shards)
    sems, lands, tok = _xfer_start(lands, 4 * n, _gather_plan1(n), name + "_p1_start", deps)
    return dict(sems=sems, lands=lands, tok=tok, n=n)


def _gather_mid(st, after, name):
    n = st["n"]
    lands = _xfer_wait(st["sems"], st["lands"], _gather_plan1(n), after, name + "_p1_wait")
    sems, lands, tok = _xfer_start(lands, 3 * n, _gather_plan2(n), name + "_p2_start")
    return dict(sems=sems, lands=lands, tok=tok, n=n)


def _gather_finish(st, after, name):
    return _xfer_wait(st["sems"], st["lands"], _gather_plan2(st["n"]), after, name + "_p2_wait")


def _scatter_plan1(n):
    def plan(refs, x, y, c):
        return [(refs[a].at[2 * p + 1 - c], refs[n + a].at[p], (x, y, 1 - c)) for a in range(n) for p in range(NCHIP)]
    return plan


def _scatter_plan2(n):
    def plan(refs, x, y, c):
        return [(refs[a].at[2 * px + py], refs[n + a].at[j], (px, py, c))
                for a in range(n) for j, (px, py) in enumerate(_chips_of(x, y))]
    return plan


def _scatter_start(Gs, name):
    n = len(Gs)
    R1s = [lax.empty((NCHIP,) + g.shape[1:], g.dtype) for g in Gs]
    sems, bufs, tok = _xfer_start(list(Gs) + R1s, NCHIP * n, _scatter_plan1(n), name + "_s1_start")
    return dict(sems=sems, bufs=bufs, tok=tok, n=n)


def _scatter_mid(st, after, my_c, name):
    n = st["n"]
    bufs = _xfer_wait(st["sems"], st["bufs"], _scatter_plan1(n), after, name + "_s1_wait")
    Ps = [_pair_sum(bufs[a], bufs[n + a], my_c, f"{name}_pair_sum{a}") for a in range(n)]
    R2s = [lax.empty((3,) + p.shape[1:], p.dtype) for p in Ps]
    sems, bufs, tok = _xfer_start(Ps + R2s, 3 * n, _scatter_plan2(n), name + "_s2_start")
    return dict(sems=sems, bufs=bufs, tok=tok, n=n)


def _scatter_finish(st, after, name):
    n = st["n"]
    bufs = _xfer_wait(st["sems"], st["bufs"], _scatter_plan2(n), after, name + "_s2_wait")
    return bufs[:n], bufs[n:]


SMALL_ROWS = {"norm1_g": (0, 1), "norm2_g": (1, 1), "sgu_ln_g": (2, 1), "sgu_ln_b": (3, 1), "cfm_conv_b": (4, 1),
              "cfm_ln_g": (5, 1), "cfm_ln_b": (6, 1), "b_sgu": (7, 1), "w_sgu": (8, 128), "b_ada": (136, N_MOD),
              "w_short": (142, SHORT_K), "cfm_conv_w": (145, CFM_K)}
ROWS_PER_LAYER = 176
FINAL_ROW = DEPTH * ROWS_PER_LAYER
PACK_ROWS = 360


def _pack(get, D, layers=tuple(range(DEPTH)), tail=True):
    parts = []
    for l in layers:
        for name, (_, nrows) in SMALL_ROWS.items():
            a = get(name, l)
            parts.append(jnp.zeros((nrows * D,), F32) if a is None else a.astype(F32).reshape(nrows * D))
    if tail:
        for name in ("final_g", "loss"):
            a = get(name, None)
            parts.append(jnp.zeros((D,), F32) if a is None else a.astype(F32).reshape(D))
        parts.append(jnp.zeros(((PACK_ROWS - FINAL_ROW - 2) * D,), F32))
    return jnp.concatenate(parts).reshape(-1, D)


def _unpack(pack, name, shape):
    D = pack.shape[1]
    r0, nrows = SMALL_ROWS[name]
    return jnp.stack([pack[l * ROWS_PER_LAYER + r0:l * ROWS_PER_LAYER + r0 + nrows] for l in range(DEPTH)]).reshape(shape)


def _mm_tiles(S):
    return min(512, S), min(1024, S), min(2048, S)


def kernel(x, c, w_ada, b_ada, norm1_g, w_in, w_short, w_a_out, sgu_ln_g, sgu_ln_b, w_sgu, b_sgu, w_b_out, cfm_conv_w, cfm_conv_b, cfm_ln_g, cfm_ln_b, w_c_out, w_o, norm2_g, w_ffn_in, w_ffn_out, final_g, loss_target, m_w_ada, m_b_ada, m_norm1_g, m_w_in, m_w_short, m_w_a_out, m_sgu_ln_g, m_sgu_ln_b, m_w_sgu, m_b_sgu, m_w_b_out, m_cfm_conv_w, m_cfm_conv_b, m_cfm_ln_g, m_cfm_ln_b, m_w_c_out, m_w_o, m_norm2_g, m_w_ffn_in, m_w_ffn_out, m_final_g, v_w_ada, v_b_ada, v_norm1_g, v_w_in, v_w_short, v_w_a_out, v_sgu_ln_g, v_sgu_ln_b, v_w_sgu, v_b_sgu, v_w_b_out, v_cfm_conv_w, v_cfm_conv_b, v_cfm_ln_g, v_cfm_ln_b, v_w_c_out, v_w_o, v_norm2_g, v_w_ffn_in, v_w_ffn_out, v_final_g):
    W = dict(w_ada=w_ada, b_ada=b_ada, norm1_g=norm1_g, w_in=w_in, w_short=w_short, w_a_out=w_a_out, sgu_ln_g=sgu_ln_g,
             sgu_ln_b=sgu_ln_b, w_sgu=w_sgu, b_sgu=b_sgu, w_b_out=w_b_out, cfm_conv_w=cfm_conv_w, cfm_conv_b=cfm_conv_b,
             cfm_ln_g=cfm_ln_g, cfm_ln_b=cfm_ln_b, w_c_out=w_c_out, w_o=w_o, norm2_g=norm2_g, w_ffn_in=w_ffn_in,
             w_ffn_out=w_ffn_out, final_g=final_g)
    Mo = dict(w_ada=m_w_ada, b_ada=m_b_ada, norm1_g=m_norm1_g, w_in=m_w_in, w_short=m_w_short, w_a_out=m_w_a_out,
              sgu_ln_g=m_sgu_ln_g, sgu_ln_b=m_sgu_ln_b, w_sgu=m_w_sgu, b_sgu=m_b_sgu, w_b_out=m_w_b_out,
              cfm_conv_w=m_cfm_conv_w, cfm_conv_b=m_cfm_conv_b, cfm_ln_g=m_cfm_ln_g, cfm_ln_b=m_cfm_ln_b,
              w_c_out=m_w_c_out, w_o=m_w_o, norm2_g=m_norm2_g, w_ffn_in=m_w_ffn_in, w_ffn_out=m_w_ffn_out,
              final_g=m_final_g)
    Vo = dict(w_ada=v_w_ada, b_ada=v_b_ada, norm1_g=v_norm1_g, w_in=v_w_in, w_short=v_w_short, w_a_out=v_w_a_out,
              sgu_ln_g=v_sgu_ln_g, sgu_ln_b=v_sgu_ln_b, w_sgu=v_w_sgu, b_sgu=v_b_sgu, w_b_out=v_w_b_out,
              cfm_conv_w=v_cfm_conv_w, cfm_conv_b=v_cfm_conv_b, cfm_ln_g=v_cfm_ln_g, cfm_ln_b=v_cfm_ln_b,
              w_c_out=v_w_c_out, w_o=v_w_o, norm2_g=v_norm2_g, w_ffn_in=v_w_ffn_in, w_ffn_out=v_w_ffn_out,
              final_g=v_final_g)
    order = ["w_ada", "b_ada", "norm1_g", "w_in", "w_short", "w_a_out", "sgu_ln_g", "sgu_ln_b", "w_sgu", "b_sgu",
             "w_b_out", "cfm_conv_w", "cfm_conv_b", "cfm_ln_g", "cfm_ln_b", "w_c_out", "w_o", "norm2_g", "w_ffn_in",
             "w_ffn_out", "final_g"]

    assert DEPTH == 2, "the weight-gather schedule below is written for two layers"
    S, D = x.shape[1], x.shape[2]
    F2 = w_ffn_in.shape[2] * NDEV
    FF = F2 // 2
    xi, yi, ci = _place()
    dev = 4 * xi + 2 * yi + ci
    my_c = jnp.reshape(ci, (1,)).astype(jnp.int32)
    my_chip = jnp.reshape(2 * xi + yi, (1,)).astype(jnp.int32)
    tm, tm_big, tm_huge = _mm_tiles(S)
    x0 = x.reshape(S, D)
    tgt = loss_target.reshape(S, D)

    def shards_of(l):
        return [w_in[l].astype(BF16), w_a_out[l].astype(BF16), w_b_out[l].astype(BF16), w_c_out[l].astype(BF16),
                w_o[l].astype(BF16), w_ffn_in[l].astype(BF16), w_ffn_out[l].astype(BF16)]

    c_all = _all_gather([jnp.pad(c, ((0, 7), (0, 0)))], "ag_c")[0][:, 0, :]
    modpart, c_act = _ada_fwd(c_all, w_ada, "ada_fwd")
    ncol = modpart.shape[2]
    mg = _all_gather([modpart.reshape(DEPTH * NDEV, ncol)], "ag_mod")[0].reshape(NDEV, DEPTH, NDEV, ncol)
    mine = lax.dynamic_index_in_dim(mg, dev, axis=2, keepdims=False)
    mod = (jnp.transpose(mine, (1, 0, 2)).reshape(DEPTH, N_MOD * D) + b_ada).reshape(DEPTH, N_MOD, D)

    ncs = w_short.shape[2]
    ag_in0 = _gather_start([w_in[0].astype(BF16), w_short.reshape(DEPTH * SHORT_K, ncs),
                            cfm_conv_w.reshape(DEPTH * CFM_K, ncs)], dev, "ag_w_in0", deps=(mod,))
    W, Mo, Vo = lax.optimization_barrier((ag_in0["tok"], (W, Mo, Vo)))[1]
    (norm1_g, norm2_g, w_in, w_a_out, w_b_out, w_c_out, w_o, w_ffn_in, w_ffn_out, sgu_ln_g, sgu_ln_b, w_sgu, b_sgu,
     cfm_conv_b, cfm_ln_g, cfm_ln_b, final_g) = [W[k] for k in (
         "norm1_g", "norm2_g", "w_in", "w_a_out", "w_b_out", "w_c_out", "w_o", "w_ffn_in", "w_ffn_out", "sgu_ln_g",
         "sgu_ln_b", "w_sgu", "b_sgu", "cfm_conv_b", "cfm_ln_g", "cfm_ln_b", "final_g")]
    m_w_ada, v_w_ada = Mo["w_ada"], Vo["w_ada"]
    xl0, h0, ht0 = _norm_fwd(x0, None, _rows(jnp.zeros((D,), F32), norm1_g[0], mod[0, 1], mod[0, 0]), "norm1_fwd0",
                             deps=(ag_in0["tok"],))
    ag_rest0 = _gather_start(shards_of(0)[1:], dev, "ag_rest0", deps=(h0,))

    tril = jnp.tril(jnp.ones((CHUNK, CHUNK), dtype=bool))

    def layer_consts(l):
        wt = jnp.where(tril[None], w_sgu[l], 0.0).astype(BF16)
        return dict(sgu_ln=_rows(sgu_ln_g[l], sgu_ln_b[l]), wtril=wt, wtril_t=jnp.swapaxes(wt, 1, 2),
                    bias_full=jnp.repeat(b_sgu[l].T, LANE, axis=1), cvec=_rows(cfm_conv_b[l], cfm_ln_g[l], cfm_ln_b[l]))

    def rest_of(g):
        return dict(w_a=g[0].reshape(1, D, D), w_b=g[1].reshape(1, D, D), w_c=g[2].reshape(1, D, D),
                    w_o=g[3].reshape(1, D, D), w_fi=jnp.transpose(g[4], (1, 0, 2)).reshape(1, D, F2),
                    w_fo=g[5].reshape(1, FF, D))

    sharded_small = ("w_short", "cfm_conv_w")

    def param_get(T):
        def get(name, l):
            if name == "final_g":
                return T[name]
            return None if name in sharded_small or name == "loss" else T[name][l]
        return get

    packs = [_pack(param_get(T), D) for T in (W, Mo, Vo)]
    ag_in0 = _gather_mid(ag_in0, [ag_rest0["tok"], *packs], "ag_w_in0")
    (w_sgu, b_sgu, sgu_ln_g, sgu_ln_b, cfm_conv_b, cfm_ln_g, cfm_ln_b), conv_wmv_in = lax.optimization_barrier(
        (ag_in0["tok"], ((w_sgu, b_sgu, sgu_ln_g, sgu_ln_b, cfm_conv_b, cfm_ln_g, cfm_ln_b),
                         [(T["w_short"], T["cfm_conv_w"]) for T in (W, Mo, Vo)])))[1]
    consts = [layer_consts(l) for l in range(DEPTH)]
    ncr = DEPTH * (SHORT_K + CFM_K)
    padr = (-ncr) % 8
    convw_wmv = [jnp.pad(jnp.concatenate([a.reshape(-1, ncs), b.reshape(-1, ncs)]), ((0, padr), (0, 0)))
                 for a, b in conv_wmv_in]
    g_in0 = _gather_finish(ag_in0, [*convw_wmv] + [a for cl in consts for a in cl.values()], "ag_w_in0")
    w_short_full = jnp.transpose(g_in0[1], (1, 0, 2)).reshape(DEPTH, SHORT_K, D)
    cfm_w_full = jnp.transpose(g_in0[2], (1, 0, 2)).reshape(DEPTH, CFM_K, D)
    for l in range(DEPTH):
        consts[l]["wsh"] = jnp.pad(w_short_full[l], ((0, 8 - SHORT_K), (0, 0)))
        consts[l]["cw"] = jnp.pad(cfm_w_full[l], ((0, HALO - CFM_K), (0, 0)))
    Wg = [dict(w_in=g_in0[0]), None]
    ag_l1 = None
    nin = w_in.shape[2]
    tn_in = nin if nin % 256 == 0 and nin <= 1280 else 256
    tn_fi = 512 if F2 % 512 == 0 else 256
    tn_dw = min(256, D)

    saved = []
    xcur, fprev, gprev = x0, None, None
    for l in range(DEPTH):
        sh1, sc1, g1, sh2, sc2, g2 = [mod[l, k] for k in range(N_MOD)]
        cl = consts[l]
        if l == 0:
            xl, h, ht = xl0, h0, ht0
        else:
            vec1 = _rows(gprev, norm1_g[l], sc1, sh1)
            ag_l1 = _gather_mid(ag_l1, fprev, f"ag_w{l}")
            xl, h, ht = _norm_fwd(xcur, fprev, vec1, f"norm1_fwd{l}", deps=(ag_l1["tok"],))
            g = _gather_finish(ag_l1, h, f"ag_w{l}")
            Wg[l] = dict(w_in=g[0], **rest_of(g[1:]))
        wl = Wg[l]
        z = _mm_nn(h, wl["w_in"], BF16, tm_huge, tn_in, D, f"mm_in{l}", w_outer=True)
        mix_deps = ()
        if l == 0:
            ag_rest0 = _gather_mid(ag_rest0, z, "ag_rest0")
            mix_deps = (ag_rest0["tok"],)
            if DEPTH > 1:
                ag_l1 = _gather_start(shards_of(1), dev, "ag_w1")
                mix_deps += (ag_l1["tok"],)
        acts, acts_t, conv = _mixer_fwd(z, cl["wsh"], cl["sgu_ln"], cl["wtril"], cl["bias_full"], cl["cw"], cl["cvec"],
                                        f"mixer_fwd{l}", deps=mix_deps)
        if l == 0:
            wl.update(rest_of(_gather_finish(ag_rest0, acts[0], "ag_rest0")))
        merged, merged_t, ys = _branch_out(acts, [wl["w_a"][0], wl["w_b"][0], wl["w_c"][0]], z, f"branch_out{l}")
        o = _mm_nn(merged, wl["w_o"], F32, tm_big, D, D, f"mm_o{l}")
        x1, h2, h2t = _norm_fwd(xl, o, _rows(g1, norm2_g[l], sc2, sh2), f"norm2_fwd{l}")
        gu, act, act_t = _ffn_in_swiglu(h2, wl["w_fi"], tm_huge, 256, f"mm_ffn_in{l}")
        f = _mm_nn(act, wl["w_fo"], F32, tm_big, D, FF, f"mm_ffn_out{l}")
        saved.append(dict(xl=xl, ht=ht, z=z, acts_t=acts_t, conv=conv, ys=ys, merged_t=merged_t, o=o, x1=x1, h2t=h2t, gu=gu,
                          act_t=act_t, f=f, consts=cl, mod=(sh1, sc1, g1, sh2, sc2, g2)))
        xcur, fprev, gprev = x1, f, g2

    last = saved[-1]
    dxup, dfb, fsums, loss_blk = _final_bwd(last["x1"], last["f"], tgt, _rows(last["mod"][5], final_g), "final_bwd")
    loss_row = jnp.pad(loss_blk[0, 0:1], (0, D - 1))
    dgate2_next = fsums[1]
    small = [dict() for _ in range(DEPTH)]
    dmods = [None] * DEPTH
    nfi = w_ffn_in.shape[2]
    early_names, late_names = ["w_ffn_out", "w_ffn_in", "w_o"], ["w_a_out", "w_b_out", "w_c_out", "w_in"]
    results = {n: None for n in early_names + late_names}

    def adam_group(names, Ps, R2s, l, deps=()):
        for n, p, r2 in zip(names, Ps, R2s):
            results[n] = _adam_big(p, r2, my_chip, W[n], Mo[n], Vo[n], l, results[n], f"adam_{n}{l}", deps)

    deferred = []
    late_prev = None
    ag_s1, gathered1 = None, None
    tk_w = min(2048, S)
    tn_dw_in = tn_in // 2 if tn_in == 1280 else tn_in
    for l in reversed(range(DEPTH)):
        sv, wl, cl = saved[l], Wg[l], saved[l]["consts"]
        sh1, sc1, g1, sh2, sc2, g2 = sv["mod"]
        dact = _mm_nt(dfb, wl["w_fo"], BF16, tm_big, FF, D, f"mm_dact{l}",
                      deps=() if late_prev is None else (late_prev["tok"], ag_s1["tok"]))
        g_fo = _mm_wgrad(sv["act_t"], dfb, 1, FF // 2, D, tk_w, f"mm_dw_ffn_out{l}")
        dgu = _swiglu_bwd(dact, sv["gu"], f"swiglu_bwd{l}")
        dh2 = _mm_nt(dgu, wl["w_fi"], F32, tm, D, F2, f"mm_dh2{l}")
        if late_prev is not None:
            deferred.append((late_names, *_scatter_finish(late_prev, dh2, f"rs_late{l + 1}"), l + 1))
            late_prev = None
        g_fi = _mm_wgrad(sv["h2t"], dgu, 1, D, tn_fi, S, f"mm_dw_ffn_in{l}")
        if ag_s1 is not None:
            ag_s1 = _gather_mid(ag_s1, g_fi, "ag_small1")
        dx1, dob, s2 = _norm_bwd(sv["x1"], dh2, dxup, _rows(norm2_g[l], sc2, g1), sv["o"], f"norm2_bwd{l}",
                                 deps=() if ag_s1 is None else (ag_s1["tok"],))
        dmerged = _mm_nt(dob, wl["w_o"], BF16, tm_big, D, D, f"mm_dmerged{l}")
        g_o = _mm_wgrad(sv["merged_t"], dob, 1, D, tn_dw, S, f"mm_dw_o{l}")
        early = _scatter_start([g_fo.reshape(NDEV, FF // NDEV, D),
                                jnp.transpose(g_fi.reshape(D, NDEV, nfi), (1, 0, 2)),
                                g_o.reshape(NDEV, D // NDEV, D)], f"rs_early{l}")
        dys, dz = _gate_bwd(dmerged, sv["z"], sv["ys"], f"gate_bwd{l}", deps=(early["tok"],))
        if ag_s1 is not None:
            gathered1 = _gather_finish(ag_s1, dys, "ag_small1")[0]
            ag_s1 = None
        early = _scatter_mid(early, dys, my_c, f"rs_early{l}")
        dacts = _mm3_nt(dys, [wl["w_a"], wl["w_b"], wl["w_c"]], tm_big, f"mm_dact_abc{l}", deps=(early["tok"],))
        g3 = _mm3_wgrad(sv["acts_t"], dys, tn_dw, f"mm_dw_abc{l}")
        g_abc = [g3[n] for n in range(3)]
        dz, mvec, dcw, dws, dbs = _mixer_bwd(sv["z"], dacts, sv["conv"], dz, cl["wsh"], cl["sgu_ln"], cl["wtril"],
                                             cl["wtril_t"], cl["bias_full"], cl["cw"], cl["cvec"], f"mixer_bwd{l}")
        dh = _mm_nt(dz, wl["w_in"], F32, tm_big, D, tn_in, f"mm_dh{l}",
                    blocks_per_step=2 if (tn_in == nin and wl["w_in"].shape[0] % 2 == 0) else 1)
        g_in = _mm_wgrad(sv["ht"], dz, NDEV, D, tn_dw_in, S, f"mm_dw_in{l}")
        late = _scatter_start([g.reshape(NDEV, D // NDEV, D) for g in g_abc] + [g_in], f"rs_late{l}")
        if l > 0:
            pv = saved[l - 1]
            dxup, dfb, s1 = _norm_bwd(sv["xl"], dh, dx1, _rows(norm1_g[l], sc1, pv["mod"][5]), pv["f"], f"norm1_bwd{l}",
                                      deps=(late["tok"],))
        else:
            dxup, dfb, s1 = _norm_bwd(sv["xl"], dh, dx1, _rows(norm1_g[l], sc1), None, f"norm1_bwd{l}", deps=(late["tok"],))
        deferred.append((early_names, *_scatter_finish(early, dxup, f"rs_early{l}"), l))
        dmods[l] = jnp.stack([s1[0], s1[1], s2[3], s2[0], s2[1], dgate2_next])
        dgate2_next = s1[3]
        small[l] = dict(norm1_g=s1[2], norm2_g=s2[2], sgu_ln_g=mvec[3], sgu_ln_b=mvec[4], cfm_conv_b=mvec[5],
                        cfm_ln_g=mvec[6], cfm_ln_b=mvec[7], b_sgu=dbs[:, :, 0],
                        w_sgu=jnp.where(tril[None], dws, 0.0), b_ada=dmods[l], w_short=mvec[0:SHORT_K],
                        cfm_conv_w=dcw[0:CFM_K])
        small_get = lambda name, k: {"final_g": fsums[0], "loss": loss_row}.get(name) if k is None else small[k][name]
        if l > 0:
            late_prev = _scatter_mid(late, dxup, my_c, f"rs_late{l}")
            ag_s1 = _gather_start([_pack(small_get, D, layers=(l,), tail=True)], dev, "ag_small1", deps=(late_prev["tok"],))
    grad_x = dxup.reshape(x.shape)

    gathered0 = _all_gather([_pack(small_get, D, layers=(0,), tail=False)], "ag_small0", deps=(dxup,))[0]
    late_prev = _scatter_mid(late, gathered0, my_c, "rs_late0")
    gathered = jnp.concatenate([gathered0, gathered1], axis=1)
    one_row = [n for n in order if n in SMALL_ROWS and SMALL_ROWS[n][1] == 1 and W[n].ndim == 2]
    (sg, sd, sm, sv_), singles = _adam_small(
        gathered, *packs, name="adam_small", deps=(late_prev["tok"],),
        single_rows=[tuple(l * ROWS_PER_LAYER + SMALL_ROWS[n][0] for l in range(DEPTH)) for n in one_row])
    loss = sg[FINAL_ROW + 1, 0]
    out = {n: tuple(singles[4 * i:4 * i + 4]) for i, n in enumerate(one_row)}
    for name in order:
        if name in SMALL_ROWS and name not in sharded_small and name not in out:
            out[name] = tuple(_unpack(p, name, W[name].shape) for p in (sg, sd, sm, sv_))
    out["final_g"] = tuple(p[FINAL_ROW] for p in (sg, sd, sm, sv_))

    def my_cols(name):
        full = _unpack(sg, name, (DEPTH, SMALL_ROWS[name][1], D))
        return lax.dynamic_slice_in_dim(full, dev * ncs, ncs, axis=2)

    gcs = jnp.concatenate([my_cols("w_short").reshape(-1, ncs), my_cols("cfm_conv_w").reshape(-1, ncs)])
    cd, cm, cv = _adam_plain(jnp.pad(gcs, ((0, padr), (0, 0))), *convw_wmv, "adam_convw")
    nsh = DEPTH * SHORT_K
    out["w_short"] = tuple(a[0:nsh].reshape(w_short.shape) for a in (gcs, cd, cm, cv))
    out["cfm_conv_w"] = tuple(a[nsh:ncr].reshape(cfm_conv_w.shape) for a in (gcs, cd, cm, cv))

    dm_all = jnp.stack([gathered[:, l * ROWS_PER_LAYER + 136:l * ROWS_PER_LAYER + 136 + N_MOD, :].reshape(NDEV, N_MOD * D)
                        for l in range(DEPTH)])
    dm_mine = lax.dynamic_slice_in_dim(dm_all, dev * ncol, ncol, axis=2)
    out["w_ada"] = tuple(_adam_ada(jnp.transpose(c_act), dm_mine, w_ada, m_w_ada, v_w_ada, "adam_ada"))

    for names, Ps, R2s, l in deferred:
        adam_group(names, Ps, R2s, l, deps=(late_prev["tok"],))
    adam_group(late_names, *_scatter_finish(late_prev, results["w_o"][0], "rs_late0"), 0)
    for n in early_names + late_names:
        out[n] = tuple(results[n])

    grads = [out[n][0] for n in order]
    deltas = [out[n][1] for n in order]
    new_m = [out[n][2] for n in order]
    new_v = [out[n][3] for n in order]
    return (loss, grad_x, *grads, *deltas, *new_m, *new_v)
```

```python
import functools
import math

import jax
import jax.numpy as jnp
from jax import lax
from jax.experimental import pallas as pl
from jax.experimental.pallas import tpu as pltpu

F32, BF16 = jnp.float32, jnp.bfloat16
NDEV = 8
NCHIP = NDEV // 2
DEPTH = 2
EPS = 1e-6
CHUNK = 128
NG = 8
SHORT_K = 3
CFM_K = 31
HALO = 32
N_MOD = 6
LANE = 128
VMEM_LIMIT = 56 * 1024 * 1024
ADAM_LR, ADAM_B1, ADAM_B2, ADAM_EPS, ADAM_WD, ADAM_STEP = 0.001, 0.9, 0.999, 1e-08, 0.01, 10
_G0 = math.sqrt(2.0 / math.pi)
_G1 = 0.044715
MESH = pl.DeviceIdType.MESH
ANY = pl.BlockSpec(memory_space=pl.ANY)


def _pcall(body, **kw):
    return pl.pallas_call(body, **kw)


def _params(sem=None):
    return pltpu.CompilerParams(dimension_semantics=sem, vmem_limit_bytes=VMEM_LIMIT)


def _sds(shape, dtype):
    return jax.ShapeDtypeStruct(tuple(shape), dtype)


def _mm_body(dims, nk, out_f32, blocks=1):
    def body(a_ref, b_ref, o_ref, *scr):
        k = pl.program_id(2)
        if blocks == 1:
            part = lax.dot_general(a_ref[...], b_ref[...], dims, preferred_element_type=F32)
        else:
            w = a_ref.shape[1] // blocks
            part = None
            for g in range(blocks):
                t = lax.dot_general(a_ref[:, g * w:(g + 1) * w], b_ref[g], dims, preferred_element_type=F32)
                part = t if part is None else part + t
        if nk == 1:
            o_ref[...] = part.reshape(o_ref.shape).astype(o_ref.dtype)
        elif out_f32:
            @pl.when(k == 0)
            def _():
                o_ref[...] = part.reshape(o_ref.shape)

            @pl.when(k > 0)
            def _():
                o_ref[...] += part.reshape(o_ref.shape)
        else:
            acc = scr[0]

            @pl.when(k == 0)
            def _():
                acc[...] = part

            @pl.when(k > 0)
            def _():
                acc[...] += part

            @pl.when(k == nk - 1)
            def _():
                o_ref[...] = acc[...].astype(o_ref.dtype)
    return body


def _after(body, n_in, deps):
    nd = len(deps)
    if nd == 0:
        return body

    def ordered(*refs):
        return body(*refs[:n_in], *refs[n_in + nd:])
    return ordered


def _mm_call(body, grid, in_specs, out_spec, out_shape, acc_shape, name, deps=()):
    scratch = [] if acc_shape is None else [pltpu.VMEM(acc_shape, F32)]
    return _pcall(_after(body, 2, deps), grid=grid, in_specs=in_specs + [ANY] * len(deps), out_specs=out_spec,
                  out_shape=out_shape, scratch_shapes=scratch, name=name,
                  compiler_params=_params(("parallel", "parallel", "arbitrary")))


def _mm_nn(a, b3, out_dtype, tm, tn, tk, name, w_outer=False, deps=()):
    M, K = a.shape
    G, _, Nb = b3.shape
    npb, nk = Nb // tn, K // tk
    out_f32 = out_dtype == F32
    body = _mm_body((((1,), (0,)), ((), ())), nk, out_f32)
    if w_outer:
        grid = (G * npb, M // tm, nk)
        ij = lambda p, q: (q, p)
    else:
        grid = (M // tm, G * npb, nk)
        ij = lambda p, q: (p, q)

    def a_map(p, q, k):
        i, j = ij(p, q)
        return (i, k)

    def b_map(p, q, k):
        i, j = ij(p, q)
        return (j // npb, k, j % npb)

    def o_map(p, q, k):
        return ij(p, q)

    def wrapped(a_ref, b_ref, o_ref, *scr):
        body(a_ref, b_ref, o_ref, *scr)

    return _mm_call(wrapped, grid, [pl.BlockSpec((tm, tk), a_map), pl.BlockSpec((None, tk, tn), b_map)],
                    pl.BlockSpec((tm, tn), o_map), _sds((M, G * Nb), out_dtype),
                    None if (nk == 1 or out_f32) else (tm, tn), name, deps)(a, b3, *deps)


def _mm_nt(a, b3, out_dtype, tm, tn, tk, name, deps=(), blocks_per_step=1):
    M, _ = a.shape
    G, Ko, Nb = b3.shape
    kpb = Nb // tk
    nk = G * kpb // blocks_per_step
    out_f32 = out_dtype == F32
    body = _mm_body((((1,), (1,)), ((), ())), nk, out_f32, blocks_per_step)

    def wrapped(a_ref, b_ref, o_ref, *scr):
        body(a_ref, b_ref, o_ref, *scr)

    if blocks_per_step > 1:
        assert tk == Nb and G % blocks_per_step == 0
        b_spec = pl.BlockSpec((blocks_per_step, tn, tk), lambda i, j, k: (k, j, 0))
    else:
        b_spec = pl.BlockSpec((None, tn, tk), lambda i, j, k: (k // kpb, j, k % kpb))
    return _mm_call(wrapped, (M // tm, Ko // tn, nk),
                    [pl.BlockSpec((tm, tk * blocks_per_step), lambda i, j, k: (i, k)), b_spec],
                    pl.BlockSpec((tm, tn), lambda i, j, k: (i, j)), _sds((M, Ko), out_dtype),
                    None if (nk == 1 or out_f32) else (tm, tn), name, deps)(a, b3, *deps)


def _mm_wgrad(at, b, G, tm, tn, tk, name, deps=()):
    M, T = at.shape
    Nb = b.shape[1] // G
    npb, nk = Nb // tn, T // tk
    body = _mm_body((((1,), (0,)), ((), ())), nk, False)

    def wrapped(a_ref, b_ref, o_ref, *scr):
        body(a_ref, b_ref, o_ref, *scr)

    a = at
    in_specs = [pl.BlockSpec((tm, tk), lambda i, j, k: (i, k)), pl.BlockSpec((tk, tn), lambda i, j, k: (k, j))]
    out_spec = pl.BlockSpec((None, tm, tn), lambda i, j, k: (j // npb, i, j % npb))
    return _mm_call(wrapped, (M // tm, G * npb, nk), in_specs, out_spec, _sds((G, M, Nb), BF16),
                    None if nk == 1 else (tm, tn), name, deps)(a, b, *deps)


def _mm3_nt(x3, ws, tm, name, deps=()):
    nb, S, K = x3.shape
    Ko = ws[0].shape[1]

    def body(x_ref, w0, w1, w2, o_ref):
        n = pl.program_id(0)
        for k, w in enumerate((w0, w1, w2)):
            @pl.when(n == k)
            def _(w=w):
                o_ref[...] = lax.dot_general(x_ref[...], w[...], (((1,), (1,)), ((), ())),
                                             preferred_element_type=F32).astype(BF16)

    wspec = pl.BlockSpec((None, Ko, K), lambda n, i: (0, 0, 0))
    return _pcall(_after(body, 4, deps), grid=(nb, S // tm),
                  in_specs=[pl.BlockSpec((None, tm, K), lambda n, i: (n, i, 0)), wspec, wspec, wspec] + [ANY] * len(deps),
                  out_specs=pl.BlockSpec((None, tm, Ko), lambda n, i: (n, i, 0)), out_shape=_sds((nb, S, Ko), BF16),
                  name=name, compiler_params=_params(("arbitrary", "parallel")))(x3, *ws, *deps)


def _mm3_wgrad(at3, b3, tn, name):
    nb, M, T = at3.shape
    N = b3.shape[2]

    def body(a_ref, b_ref, o_ref):
        o_ref[...] = jnp.dot(a_ref[...], b_ref[...], preferred_element_type=F32).astype(BF16)

    return _pcall(body, grid=(nb, N // tn),
                  in_specs=[pl.BlockSpec((None, M, T), lambda n, j: (n, 0, 0)), pl.BlockSpec((None, T, tn), lambda n, j: (n, 0, j))],
                  out_specs=pl.BlockSpec((None, M, tn), lambda n, j: (n, 0, j)), out_shape=_sds((nb, M, N), BF16),
                  name=name, compiler_params=_params(("arbitrary", "parallel")))(at3, b3)


def _rsum(v):
    return jnp.sum(v, axis=0, keepdims=True)


def _rmean(v):
    return jnp.mean(v, axis=-1, keepdims=True)


def _gelu(x):
    t = jnp.tanh(_G0 * (x + _G1 * (x * x * x)))
    return x * (0.5 * (1.0 + t)), t


def _dgelu(x, t):
    return 0.5 * (1.0 + t) + 0.5 * x * (1.0 - t * t) * (_G0 * (1.0 + 3.0 * _G1 * (x * x)))


def _sigmoid(x):
    return 0.5 * jnp.tanh(0.5 * x) + 0.5


def _fill_shifted(ext, rot):
    v = ext[...]
    n = v.shape[0]
    for b in range(1, 8):
        rot[b - 1] = pltpu.roll(v, n - b, 0)


def _rows_at(ext, rot, s, tm, cs=slice(None)):
    a, b = divmod(s, 8)
    return ext[8 * a:8 * a + tm, cs] if b == 0 else rot[b - 1, 8 * a:8 * a + tm, cs]


def _causal_conv(w_ref, taps, bias, ext, rot, offset, tm, out):
    D = out.shape[1]
    for cb in range(D // LANE):
        cs = slice(cb * LANE, (cb + 1) * LANE)
        acc = None
        for k, o in zip(taps, offset):
            term = w_ref[k:k + 1, cs] * _rows_at(ext, rot, o, tm, cs)
            acc = term if acc is None else acc + term
        out[:, cs] = acc if bias is None else acc + bias[:, cs]


def _rows(*vs):
    a = jnp.stack([v.astype(F32) for v in vs])
    return jnp.pad(a, ((0, 8 - len(vs)), (0, 0)))


def _row_spec(tm, D):
    return pl.BlockSpec((tm, D), lambda i: (i, 0))


def _const_spec(shape):
    nd = len(shape)
    return pl.BlockSpec(shape, lambda i: (0,) * nd)


def _norm_fwd(xp, f, vec, name, deps=()):
    S, D = xp.shape
    tm = min(512, S)
    has_f = f is not None

    def body(*refs):
        if has_f:
            xp_ref, f_ref, vec_ref, xo_ref, h_ref, ht_ref = refs
            x = xp_ref[...] + vec_ref[0:1, :] * f_ref[...]
            xo_ref[...] = x
        else:
            xp_ref, vec_ref, h_ref, ht_ref = refs
            x = xp_ref[...]
        r = lax.rsqrt(_rmean(x * x) + EPS)
        h = (x * r) * vec_ref[1:2, :]
        h = h * (1.0 + vec_ref[2:3, :]) + vec_ref[3:4, :]
        h_ref[...] = h.astype(BF16)
        ht_ref[...] = h.T.astype(BF16)

    rs = _row_spec(tm, D)
    ins = [xp, f, vec] if has_f else [xp, vec]
    in_specs = ([rs, rs] if has_f else [rs]) + [_const_spec((8, D))]
    out_shape = ([_sds((S, D), F32)] if has_f else []) + [_sds((S, D), BF16), _sds((D, S), BF16)]
    out_specs = [rs] * (len(out_shape) - 1) + [pl.BlockSpec((D, tm), lambda i: (0, i))]
    outs = _pcall(_after(body, len(ins), deps), grid=(S // tm,), in_specs=in_specs + [ANY] * len(deps),
                  out_specs=out_specs, out_shape=out_shape, name=name,
                  compiler_params=_params(("parallel",)))(*ins, *deps)
    return (outs[0], outs[1], outs[2]) if has_f else (xp, outs[0], outs[1])


def _mixer_fwd(z, wsh, sgu_ln, wtril, bias_full, cw, cvec, name, deps=()):
    S = z.shape[0]
    D = wsh.shape[1]
    tm = CHUNK

    def body(z_ref, wsh_ref, sln_ref, wt_ref, bias_ref, cw_ref, cv_ref, oa_ref, ob_ref, oc_ref, t_ref,
             conv_ref, pe, ge, gr, cbuf):
        i = pl.program_id(0)

        @pl.when(i == 0)
        def _():
            pe[0:HALO, :] = jnp.zeros((HALO, D), F32)
            ge[0:HALO, :] = jnp.zeros((HALO, D), F32)

        def col(n):
            return z_ref[:, n * D:(n + 1) * D].astype(F32)

        pe[HALO:HALO + tm, :] = col(1) * col(2)
        q = wsh_ref[0:1, :] * pe[HALO - 2:HALO - 2 + tm, :]
        q = q + wsh_ref[1:2, :] * pe[HALO - 1:HALO - 1 + tm, :]
        q = q + wsh_ref[2:3, :] * pe[HALO:HALO + tm, :]
        act_a = col(0) * q
        oa_ref[...] = act_a.astype(BF16)
        t_ref[0] = act_a.T.astype(BF16)
        gu, _ = _gelu(col(3))
        gv, _ = _gelu(col(4))
        d = gv - _rmean(gv)
        nrm = d * lax.rsqrt(_rmean(d * d) + EPS)
        vnb = (nrm * sln_ref[0:1, :] + sln_ref[1:2, :]).astype(BF16)
        for g in range(NG):
            cs = slice(g * LANE, (g + 1) * LANE)
            mixed = jnp.dot(wt_ref[g], vnb[:, cs], preferred_element_type=F32) + bias_ref[:, cs]
            act_b = gu[:, cs] * mixed
            ob_ref[:, cs] = act_b.astype(BF16)
            t_ref[1, cs, :] = act_b.T.astype(BF16)
        ge[HALO:HALO + tm, :] = col(5) * _sigmoid(col(6))
        _fill_shifted(ge, gr)
        o0 = HALO - (CFM_K - 1)
        _causal_conv(cw_ref, range(CFM_K), cv_ref[0:1, :], ge, gr, range(o0, o0 + CFM_K), tm, cbuf)
        conv = cbuf[...]
        conv_ref[...] = conv.astype(BF16)
        d = conv - _rmean(conv)
        ln = (d * lax.rsqrt(_rmean(d * d) + EPS)) * cv_ref[1:2, :] + cv_ref[2:3, :]
        act_c = ln * _sigmoid(ln)
        oc_ref[...] = act_c.astype(BF16)
        t_ref[2] = act_c.T.astype(BF16)
        pe[0:HALO, :] = pe[tm:tm + HALO, :]
        ge[0:HALO, :] = ge[tm:tm + HALO, :]

    rs = _row_spec(tm, D)
    outs = _pcall(
        _after(body, 7, deps), grid=(S // tm,),
        in_specs=[pl.BlockSpec((tm, 7 * D), lambda i: (i, 0)), _const_spec((8, D)), _const_spec((8, D)),
                  _const_spec((NG, CHUNK, CHUNK)), _const_spec((CHUNK, D)), _const_spec((HALO, D)), _const_spec((8, D))]
        + [ANY] * len(deps),
        out_specs=[rs, rs, rs, pl.BlockSpec((3, D, tm), lambda i: (0, 0, i)), rs],
        out_shape=[_sds((S, D), BF16)] * 3 + [_sds((3, D, S), BF16), _sds((S, D), BF16)],
        scratch_shapes=[pltpu.VMEM((HALO + tm, D), F32), pltpu.VMEM((HALO + tm, D), F32),
                        pltpu.VMEM((7, HALO + tm, D), F32), pltpu.VMEM((tm, D), F32)],
        name=name, compiler_params=_params(("arbitrary",)))(z, wsh, sgu_ln, wtril, bias_full, cw, cvec, *deps)
    return outs[:3], outs[3], outs[4]


def _branch_out(acts, ws, z, name):
    S, D = acts[0].shape
    tm = min(512, S)

    def body(a0, a1, a2, w0, w1, w2, g0, g1, g2, m_ref, mt_ref, y_ref):
        m = None
        for n, (a, w, g) in enumerate(((a0, w0, g0), (a1, w1, g1), (a2, w2, g2))):
            y = jnp.dot(a[...], w[...], preferred_element_type=F32)
            y_ref[n] = y.astype(BF16)
            t = _sigmoid(g[...].astype(F32)) * y
            m = t if m is None else m + t
        m_ref[...] = m.astype(BF16)
        mt_ref[...] = m.T.astype(BF16)

    rs = _row_spec(tm, D)
    gate_specs = [pl.BlockSpec((tm, D), functools.partial(lambda i, n: (i, 7 + n), n=n)) for n in range(3)]
    return _pcall(body, grid=(S // tm,),
                  in_specs=[rs, rs, rs] + [_const_spec((D, D))] * 3 + gate_specs,
                  out_specs=[rs, pl.BlockSpec((D, tm), lambda i: (0, i)), pl.BlockSpec((3, tm, D), lambda i: (0, i, 0))],
                  out_shape=[_sds((S, D), BF16), _sds((D, S), BF16), _sds((3, S, D), BF16)], name=name,
                  compiler_params=_params(("parallel",)))(*acts, *ws, z, z, z)


def _ffn_in_swiglu(h2, w3, tm, tn, name):
    S, D = h2.shape
    F = w3.shape[2] // 2
    nj = F // tn

    def body(a_ref, wg_ref, wu_ref, gu_ref, act_ref, actt_ref):
        a = a_ref[...]
        g = jnp.dot(a, wg_ref[...], preferred_element_type=F32)
        u = jnp.dot(a, wu_ref[...], preferred_element_type=F32)
        gu_ref[0] = g.astype(BF16)
        gu_ref[1] = u.astype(BF16)
        act = (g * _sigmoid(g)) * u
        act_ref[...] = act.astype(BF16)
        actt_ref[...] = act.T.astype(BF16)

    return _pcall(body, grid=(S // tm, nj),
                  in_specs=[pl.BlockSpec((tm, D), lambda i, j: (i, 0)), pl.BlockSpec((None, D, tn), lambda i, j: (0, 0, j)),
                            pl.BlockSpec((None, D, tn), lambda i, j: (0, 0, j + nj))],
                  out_specs=[pl.BlockSpec((2, tm, tn), lambda i, j: (0, i, j)), pl.BlockSpec((tm, tn), lambda i, j: (i, j)),
                             pl.BlockSpec((tn, tm), lambda i, j: (j, i))],
                  out_shape=[_sds((2, S, F), BF16), _sds((S, F), BF16), _sds((F, S), BF16)], name=name,
                  compiler_params=_params(("parallel", "parallel")))(h2, w3, w3)


def _swiglu_bwd(dact, gu, name):
    _, S, F = gu.shape
    F2 = 2 * F
    tm = min(256, S)

    def body(d_ref, g_ref, u_ref, o_ref):
        g = g_ref[...].astype(F32)
        sg = _sigmoid(g)
        d = d_ref[...].astype(F32)
        o_ref[:, 0:F] = (d * u_ref[...].astype(F32) * (sg * (1.0 + g * (1.0 - sg)))).astype(BF16)
        o_ref[:, F:2 * F] = (d * (g * sg)).astype(BF16)

    return _pcall(body, grid=(S // tm,),
                  in_specs=[pl.BlockSpec((tm, F), lambda i: (i, 0)), pl.BlockSpec((None, tm, F), lambda i: (0, i, 0)),
                            pl.BlockSpec((None, tm, F), lambda i: (1, i, 0))],
                  out_specs=pl.BlockSpec((tm, F2), lambda i: (i, 0)), out_shape=_sds((S, F2), BF16), name=name,
                  compiler_params=_params(("parallel",)))(dact, gu, gu)


def _final_bwd(x1, f, tgt, vec, name):
    S, D = x1.shape
    tm = min(512, S)

    def body(x_ref, f_ref, t_ref, vec_ref, dx_ref, df_ref, sums_ref, loss_ref):
        @pl.when(pl.program_id(0) == 0)
        def _():
            sums_ref[...] = jnp.zeros_like(sums_ref)
            loss_ref[...] = jnp.zeros_like(loss_ref)

        gate, fg = vec_ref[0:1, :], vec_ref[1:2, :]
        fv = f_ref[...]
        x = x_ref[...] + gate * fv
        r = lax.rsqrt(_rmean(x * x) + EPS)
        xn = x * r
        diff = xn * fg - t_ref[...]
        per_tok = _rmean(diff * diff)
        loss_ref[...] += 0.5 * jnp.sum(per_tok, axis=0, keepdims=True)
        dy = diff * (1.0 / D)
        sums_ref[0:1, :] += _rsum(dy * xn)
        dxn = dy * fg
        dx = r * (dxn - xn * _rmean(dxn * xn))
        sums_ref[1:2, :] += _rsum(dx * fv)
        dx_ref[...] = dx
        df_ref[...] = (dx * gate).astype(BF16)

    rs = _row_spec(tm, D)
    return _pcall(body, grid=(S // tm,), in_specs=[rs, rs, rs, _const_spec((8, D))],
                  out_specs=[rs, rs, _const_spec((8, D)), _const_spec((8, LANE))],
                  out_shape=[_sds((S, D), F32), _sds((S, D), BF16), _sds((8, D), F32), _sds((8, LANE), F32)],
                  name=name, compiler_params=_params(("arbitrary",)))(x1, f, tgt, vec)


def _norm_bwd(xin, dh, dxup, vec, fprev, name, deps=()):
    S, D = xin.shape
    tm = min(512, S)
    has_prev = fprev is not None

    def body(*refs):
        if has_prev:
            x_ref, dh_ref, up_ref, vec_ref, fp_ref, dx_ref, dp_ref, sums_ref = refs
        else:
            x_ref, dh_ref, up_ref, vec_ref, dx_ref, sums_ref = refs

        @pl.when(pl.program_id(0) == 0)
        def _():
            sums_ref[...] = jnp.zeros_like(sums_ref)

        g, scale = vec_ref[0:1, :], vec_ref[1:2, :]
        x = x_ref[...]
        r = lax.rsqrt(_rmean(x * x) + EPS)
        xn = x * r
        dhv = dh_ref[...]
        sums_ref[0:1, :] += _rsum(dhv)
        sums_ref[1:2, :] += _rsum(dhv * (xn * g))
        dm = dhv * (1.0 + scale)
        sums_ref[2:3, :] += _rsum(dm * xn)
        dxn = dm * g
        dx = up_ref[...] + r * (dxn - xn * _rmean(dxn * xn))
        dx_ref[...] = dx
        if has_prev:
            sums_ref[3:4, :] += _rsum(dx * fp_ref[...])
            dp_ref[...] = (dx * vec_ref[2:3, :]).astype(BF16)

    rs = _row_spec(tm, D)
    ins = [xin, dh, dxup, vec] + ([fprev] if has_prev else [])
    in_specs = [rs, rs, rs, _const_spec((8, D))] + ([rs] if has_prev else [])
    out_shape = [_sds((S, D), F32)] + ([_sds((S, D), BF16)] if has_prev else []) + [_sds((8, D), F32)]
    out_specs = [rs] + ([rs] if has_prev else []) + [_const_spec((8, D))]
    outs = _pcall(_after(body, len(ins), deps), grid=(S // tm,), in_specs=in_specs + [ANY] * len(deps),
                  out_specs=out_specs, out_shape=out_shape, name=name,
                  compiler_params=_params(("arbitrary",)))(*ins, *deps)
    return (outs[0], outs[1], outs[2]) if has_prev else (outs[0], None, outs[1])


def _gate_bwd(dmerged, z, ys, name, deps=()):
    S, D = dmerged.shape
    tm = min(512, S)
    ncol = z.shape[1] // D

    def body(dm_ref, g_ref, y_ref, dy_ref, dz_ref):
        sg = _sigmoid(g_ref[...].astype(F32))
        dm = dm_ref[...].astype(F32)
        dy_ref[...] = (dm * sg).astype(BF16)
        dz_ref[...] = (dm * y_ref[...].astype(F32) * (sg * (1.0 - sg))).astype(BF16)

    branch = pl.BlockSpec((None, tm, D), lambda i, n: (n, i, 0))
    return _pcall(_after(body, 3, deps), grid=(S // tm, 3),
                  in_specs=[pl.BlockSpec((tm, D), lambda i, n: (i, 0)), pl.BlockSpec((tm, D), lambda i, n: (i, 7 + n)),
                            branch] + [ANY] * len(deps),
                  out_specs=[branch, pl.BlockSpec((tm, D), lambda i, n: (i, 7 + n))],
                  out_shape=[_sds((3, S, D), BF16), _sds((S, ncol * D), BF16)], name=name,
                  compiler_params=_params(("parallel", "arbitrary")))(dmerged, z, ys, *deps)


def _mixer_bwd(z, dacts, conv, dz, wsh, sgu_ln, wtril, wtril_t, bias_full, cw, cvec, name):
    S = z.shape[0]
    D = wsh.shape[1]
    tm = CHUNK
    nt = S // tm
    hb = tm // HALO

    def body(zc, zp, da_ref, db_ref, dc_ref, conv_ref, wsh_ref, sln_ref, wt_ref, wtt_ref, bias_ref, cw_ref, cv_ref, _dz_in,
             dz_ref, vec_ref, dcw_ref, dws_ref, dbs_ref, pe, ge, dqe, dce, gr, dcr, cbuf, dcw8):
        i = pl.program_id(0)
        rb = nt - 1 - i

        @pl.when(i == 0)
        def _():
            vec_ref[...] = jnp.zeros_like(vec_ref)
            dcw8[...] = jnp.zeros_like(dcw8)
            dws_ref[...] = jnp.zeros_like(dws_ref)
            dbs_ref[...] = jnp.zeros_like(dbs_ref)
            dqe[tm:tm + HALO, :] = jnp.zeros((HALO, D), F32)
            dce[tm:tm + HALO, :] = jnp.zeros((HALO, D), F32)

        keep = (rb > 0).astype(F32)

        def col(n):
            return zc[:, n * D:(n + 1) * D].astype(F32)

        def pcol(n):
            return zp[:, n * D:(n + 1) * D].astype(F32)

        c_a, x_a = col(1), col(2)
        pe[0:HALO, :] = keep * (pcol(1) * pcol(2))
        pe[HALO:HALO + tm, :] = c_a * x_a
        q = wsh_ref[0:1, :] * pe[HALO - 2:HALO - 2 + tm, :]
        q = q + wsh_ref[1:2, :] * pe[HALO - 1:HALO - 1 + tm, :]
        q = q + wsh_ref[2:3, :] * pe[HALO:HALO + tm, :]
        dact = da_ref[...].astype(F32)
        dz_ref[:, 0:D] = (dact * q).astype(BF16)
        dq = dact * col(0)
        dqe[0:tm, :] = dq
        dp = wsh_ref[2:3, :] * dq + wsh_ref[1:2, :] * dqe[1:1 + tm, :] + wsh_ref[0:1, :] * dqe[2:2 + tm, :]
        dz_ref[:, D:2 * D] = (dp * x_a).astype(BF16)
        dz_ref[:, 2 * D:3 * D] = (dp * c_a).astype(BF16)
        for k in range(SHORT_K):
            o = HALO - (SHORT_K - 1) + k
            vec_ref[k:k + 1, :] += _rsum(dq * pe[o:o + tm, :])
        u, v = col(3), col(4)
        gu, tu = _gelu(u)
        gv, tv = _gelu(v)
        d = gv - _rmean(gv)
        rstd = lax.rsqrt(_rmean(d * d) + EPS)
        nrm = d * rstd
        vnb = (nrm * sln_ref[0:1, :] + sln_ref[1:2, :]).astype(BF16)
        dact = db_ref[...].astype(F32)
        dvn_parts, dgu_parts = [], []
        for g in range(NG):
            cs = slice(g * LANE, (g + 1) * LANE)
            vg = vnb[:, cs]
            mixed = jnp.dot(wt_ref[g], vg, preferred_element_type=F32) + bias_ref[:, cs]
            dgu_parts.append(dact[:, cs] * mixed)
            dmixed = dact[:, cs] * gu[:, cs]
            dmb = dmixed.astype(BF16)
            dws_ref[g] += lax.dot_general(dmb, vg, (((1,), (1,)), ((), ())), preferred_element_type=F32)
            dbs_ref[g] += jnp.broadcast_to(jnp.sum(dmixed, axis=1, keepdims=True), (CHUNK, LANE))
            dvn_parts.append(jnp.dot(wtt_ref[g], dmb, preferred_element_type=F32))
        dgu = jnp.concatenate(dgu_parts, axis=1)
        dvn = jnp.concatenate(dvn_parts, axis=1)
        dz_ref[:, 3 * D:4 * D] = (dgu * _dgelu(u, tu)).astype(BF16)
        vec_ref[3:4, :] += _rsum(dvn * nrm)
        vec_ref[4:5, :] += _rsum(dvn)
        dn = dvn * sln_ref[0:1, :]
        dgv = rstd * (dn - _rmean(dn) - nrm * _rmean(dn * nrm))
        dz_ref[:, 4 * D:5 * D] = (dgv * _dgelu(v, tv)).astype(BF16)
        a_c = col(5)
        sg = _sigmoid(col(6))
        ge[0:HALO, :] = keep * (pcol(5) * _sigmoid(pcol(6)))
        ge[HALO:HALO + tm, :] = a_c * sg
        _fill_shifted(ge, gr)
        o0 = HALO - (CFM_K - 1)
        conv = conv_ref[...].astype(F32)
        d = conv - _rmean(conv)
        rstd = lax.rsqrt(_rmean(d * d) + EPS)
        nrm = d * rstd
        ln = nrm * cv_ref[1:2, :] + cv_ref[2:3, :]
        sl = _sigmoid(ln)
        dln = dc_ref[...].astype(F32) * (sl * (1.0 + ln * (1.0 - sl)))
        vec_ref[6:7, :] += _rsum(dln * nrm)
        vec_ref[7:8, :] += _rsum(dln)
        dn = dln * cv_ref[1:2, :]
        dconv = rstd * (dn - _rmean(dn) - nrm * _rmean(dn * nrm))
        vec_ref[5:6, :] += _rsum(dconv)
        dce[0:tm, :] = dconv
        _fill_shifted(dce, dcr)
        _causal_conv(cw_ref, range(CFM_K), None, dce, dcr, [CFM_K - 1 - k for k in range(CFM_K)], tm, cbuf)
        dglu = cbuf[...]
        for cb in range(D // LANE):
            cs = slice(cb * LANE, (cb + 1) * LANE)
            dcv = dce[0:tm, cs]
            for k in range(CFM_K):
                prod = dcv * _rows_at(ge, gr, o0 + k, tm, cs)
                dcw8[k, :, cs] += jnp.sum(prod.reshape(tm // 8, 8, LANE), axis=0)

        @pl.when(i == nt - 1)
        def _():
            dcw_ref[...] = jnp.sum(dcw8[...], axis=1)
        dz_ref[:, 5 * D:6 * D] = (dglu * sg).astype(BF16)
        dz_ref[:, 6 * D:7 * D] = (dglu * a_c * (sg * (1.0 - sg))).astype(BF16)
        dqe[tm:tm + HALO, :] = dqe[0:HALO, :]
        dce[tm:tm + HALO, :] = dce[0:HALO, :]

    rev = lambda i: (nt - 1 - i, 0)
    rs = pl.BlockSpec((tm, D), rev)
    cur = pl.BlockSpec((tm, 7 * D), rev)
    prev = pl.BlockSpec((HALO, 7 * D), lambda i: (jnp.maximum((nt - 1 - i) * hb - 1, 0), 0))
    ext = pltpu.VMEM((HALO + tm, D), F32)
    outs = _pcall(
        body, grid=(nt,),
        in_specs=[cur, prev] + [pl.BlockSpec((None, tm, D), functools.partial(lambda i, n: (n, nt - 1 - i, 0), n=n))
                                for n in range(3)]
        + [rs, _const_spec((8, D)), _const_spec((8, D)), _const_spec((NG, CHUNK, CHUNK)),
                  _const_spec((NG, CHUNK, CHUNK)), _const_spec((CHUNK, D)), _const_spec((HALO, D)), _const_spec((8, D)),
                  ANY],
        out_specs=[cur, _const_spec((8, D)), _const_spec((HALO, D)), _const_spec((NG, CHUNK, CHUNK)),
                   _const_spec((NG, CHUNK, LANE))],
        out_shape=[_sds(dz.shape, BF16), _sds((8, D), F32), _sds((HALO, D), F32), _sds((NG, CHUNK, CHUNK), F32),
                   _sds((NG, CHUNK, LANE), F32)],
        scratch_shapes=[ext, ext, ext, ext, pltpu.VMEM((7, HALO + tm, D), F32), pltpu.VMEM((7, HALO + tm, D), F32),
                        pltpu.VMEM((tm, D), F32), pltpu.VMEM((HALO, 8, D), F32)],
        input_output_aliases={13: 0}, name=name,
        compiler_params=_params(("arbitrary",)))(z, z, dacts, dacts, dacts, conv, wsh, sgu_ln, wtril, wtril_t, bias_full, cw,
                                                 cvec, dz)
    return outs


def _ada_fwd(c_all, w_ada_loc, name):
    nb, D = c_all.shape
    L, _, nc = w_ada_loc.shape

    def body(c_ref, w_ref, o_ref, ca_ref):
        cv = c_ref[...]
        ca = cv * _sigmoid(cv)
        ca_ref[...] = ca
        o_ref[...] = jnp.dot(ca.astype(BF16), w_ref[...].astype(BF16), preferred_element_type=F32)

    return _pcall(body, grid=(L,),
                  in_specs=[_const_spec((nb, D)), pl.BlockSpec((None, D, nc), lambda l: (l, 0, 0))],
                  out_specs=[pl.BlockSpec((None, nb, nc), lambda l: (l, 0, 0)), _const_spec((nb, D))],
                  out_shape=[_sds((L, nb, nc), F32), _sds((nb, D), F32)], name=name,
                  compiler_params=_params(("arbitrary",)))(c_all, w_ada_loc)


def _adamw(w, g, m, v):
    m = ADAM_B1 * m + (1.0 - ADAM_B1) * g
    v = ADAM_B2 * v + (1.0 - ADAM_B2) * (g * g)
    m_hat = m / (1.0 - ADAM_B1 ** ADAM_STEP)
    v_hat = v / (1.0 - ADAM_B2 ** ADAM_STEP)
    delta = -ADAM_LR * (m_hat / (jnp.sqrt(v_hat) + ADAM_EPS) + ADAM_WD * w)
    return delta, m, v


def _tile_rows(R, C, align=8):
    cap = max(align, (1536 * 1024) // (4 * C))
    best = None
    for t in range(align, R + 1, align):
        if R % t == 0 and t <= cap:
            best = t
    return R if best is None else best


def _adam_ada(ct, dm, w, m, v, name):
    L, D, nc = w.shape
    nb = ct.shape[1]
    tr = _tile_rows(D, nc)

    def body(ct_ref, dm_ref, w_ref, m_ref, v_ref, g_ref, d_ref, mo_ref, vo_ref):
        g = ct_ref[:, 0:1] * dm_ref[0:1, :]
        for b in range(1, nb):
            g = g + ct_ref[:, b:b + 1] * dm_ref[b:b + 1, :]
        g_ref[...] = g
        d_ref[...], mo_ref[...], vo_ref[...] = _adamw(w_ref[...], g, m_ref[...], v_ref[...])

    ws = pl.BlockSpec((None, tr, nc), lambda l, r: (l, r, 0))
    return _pcall(body, grid=(L, D // tr),
                  in_specs=[pl.BlockSpec((tr, nb), lambda l, r: (r, 0)), pl.BlockSpec((None, nb, nc), lambda l, r: (l, 0, 0)),
                            ws, ws, ws],
                  out_specs=[ws] * 4, out_shape=[_sds(w.shape, F32)] * 4, name=name,
                  compiler_params=_params(("parallel", "parallel")))(ct, dm, w, m, v)


def _adam_small(parts, w, m, v, name, deps=(), single_rows=()):
    n, R, C = parts.shape
    tr = _tile_rows(R, C * n // 2)
    nl = len(single_rows[0]) if single_rows else 0

    def body(p_ref, w_ref, m_ref, v_ref, g_ref, d_ref, mo_ref, vo_ref, *single):
        g = p_ref[0]
        for j in range(1, n):
            g = g + p_ref[j]
        d, mo, vo = _adamw(w_ref[...], g, m_ref[...], v_ref[...])
        g_ref[...], d_ref[...], mo_ref[...], vo_ref[...] = g, d, mo, vo
        step = pl.program_id(0)
        for pi, rows in enumerate(single_rows):
            for l, row in enumerate(rows):
                @pl.when(step == row // tr)
                def _(pi=pi, l=l, off=row % tr):
                    for k, val in enumerate((g, d, mo, vo)):
                        single[4 * pi + k][l:l + 1, :] = val[off:off + 1, :]

    ws = pl.BlockSpec((tr, C), lambda r: (r, 0))
    one = pl.BlockSpec((nl, C), lambda r: (0, 0))
    outs = _pcall(_after(body, 4, deps), grid=(R // tr,),
                  in_specs=[pl.BlockSpec((n, tr, C), lambda r: (0, r, 0)), ws, ws, ws] + [ANY] * len(deps),
                  out_specs=[ws] * 4 + [one] * (4 * len(single_rows)),
                  out_shape=[_sds((R, C), F32)] * 4 + [_sds((nl, C), F32)] * (4 * len(single_rows)), name=name,
                  compiler_params=_params(("arbitrary",)))(parts, w, m, v, *deps)
    return outs[:4], outs[4:]


def _adam_plain(g, w, m, v, name):
    R, C = w.shape

    def body(g_ref, w_ref, m_ref, v_ref, d_ref, mo_ref, vo_ref):
        d_ref[...], mo_ref[...], vo_ref[...] = _adamw(w_ref[...], g_ref[...], m_ref[...], v_ref[...])

    ws = _const_spec((R, C))
    return _pcall(body, grid=(1,), in_specs=[ws] * 4, out_specs=[ws] * 3, out_shape=[_sds((R, C), F32)] * 3, name=name,
                  compiler_params=_params(("arbitrary",)))(g, w, m, v)


def _pair_sum(G, R1, my_c, name):
    n, R, C = G.shape
    half = n // 2
    tr = _tile_rows(R, C, align=16)

    def body(c_ref, g_ref, r_ref, o_ref):
        o_ref[...] = (g_ref[...].astype(F32) + r_ref[...].astype(F32)).astype(o_ref.dtype)

    blk = (None, tr, C)
    gs = pltpu.PrefetchScalarGridSpec(
        num_scalar_prefetch=1, grid=(half, R // tr),
        in_specs=[pl.BlockSpec(blk, lambda p, r, c: (2 * p + c[0], r, 0)), pl.BlockSpec(blk, lambda p, r, c: (p, r, 0))],
        out_specs=pl.BlockSpec(blk, lambda p, r, c: (p, r, 0)))
    return _pcall(body, grid_spec=gs, out_shape=_sds((half, R, C), G.dtype), name=name,
                  compiler_params=_params(("parallel", "parallel")))(my_c, G, R1)


def _adam_big(P, R2, my_chip, w, m, v, layer, prev, name, deps=()):
    _, R, C = P.shape
    nrecv = R2.shape[0]
    tr = _tile_rows(R, C, align=16)

    def body(p_sm, p_ref, r_ref, w_ref, m_ref, v_ref, *rest):
        g_ref, d_ref, mo_ref, vo_ref = rest[-4:]
        g = p_ref[...].astype(F32)
        for k in range(nrecv):
            g = g + r_ref[k].astype(F32)
        g_ref[...] = g
        d_ref[...], mo_ref[...], vo_ref[...] = _adamw(w_ref[...], g, m_ref[...], v_ref[...])

    ws = pl.BlockSpec((None, tr, C), lambda r, p: (layer, r, 0))
    held = [] if prev is None else list(prev)
    gs = pltpu.PrefetchScalarGridSpec(
        num_scalar_prefetch=1, grid=(R // tr,),
        in_specs=[pl.BlockSpec((None, tr, C), lambda r, p: (p[0], r, 0)),
                  pl.BlockSpec((nrecv, tr, C), lambda r, p: (0, r, 0)), ws, ws, ws] + [ANY] * (len(held) + len(deps)),
        out_specs=[ws] * 4)
    alias = {6 + i: i for i in range(len(held))}
    return _pcall(body, grid_spec=gs, out_shape=[_sds(w.shape, F32)] * 4, name=name, input_output_aliases=alias,
                  compiler_params=_params(("parallel",)))(my_chip, P, R2, w, m, v, *held, *deps)


def _place():
    return lax.axis_index("x"), lax.axis_index("y"), lax.axis_index("c")


def _all_gather(shards, name, deps=(), into=None):
    n = len(shards)
    bufs = [] if into is None else [b for b, _ in into]
    nb = len(bufs)

    def body(*refs):
        ins, outs = refs[:n], refs[n + nb:2 * n + nb]
        send_sems, recv_sems, local_sems = refs[2 * n + nb:]
        x, y, c = _place()
        me, sibling = (x, y, c), (x, y, 1 - c)
        chips = [(1 - x, y), (x, 1 - y), (1 - x, 1 - y)]

        def slot(a, px, py, pc):
            block = outs[a].at[4 * px + 2 * py + pc]
            return block if into is None else block.at[pl.ds(into[a][1], ins[a].shape[0])]

        def copy(a, k, block, to, src=None):
            return pltpu.make_async_remote_copy(
                src_ref=slot(a, *block) if src is None else src, dst_ref=slot(a, *block),
                send_sem=send_sems.at[7 * a + k], recv_sem=recv_sems.at[7 * a + k], device_id=to, device_id_type=MESH)

        mine = [pltpu.make_async_copy(ins[a], slot(a, *me), local_sems.at[a]) for a in range(n)]
        for cp in mine:
            cp.start()
        first = []
        for a in range(n):
            first.append(copy(a, 0, me, sibling, src=ins[a]))
            first += [copy(a, 1 + j, me, (*chip, c), src=ins[a]) for j, chip in enumerate(chips)]
        for cp in first:
            cp.start()
        passed = []
        for j, chip in enumerate(chips):
            for a in range(n):
                copy(a, 1 + j, (*chip, c), me).wait_recv()
                fwd = copy(a, 4 + j, (*chip, c), sibling)
                fwd.start()
                passed.append(fwd)
        for a in range(n):
            copy(a, 0, sibling, me).wait_recv()
        for j, chip in enumerate(chips):
            for a in range(n):
                copy(a, 4 + j, (*chip, 1 - c), me).wait_recv()
        for cp in first + passed:
            cp.wait_send()
        for cp in mine:
            cp.wait()

    out_shape = [_sds((NDEV,) + s.shape, s.dtype) for s in shards] if into is None else [_sds(b.shape, b.dtype) for b in bufs]
    outs = _pcall(_after(body, n + nb, deps), in_specs=[ANY] * (n + nb + len(deps)), out_specs=[ANY] * n,
                  out_shape=out_shape, input_output_aliases={n + a: a for a in range(nb)},
                  scratch_shapes=[pltpu.SemaphoreType.DMA((7 * n,)), pltpu.SemaphoreType.DMA((7 * n,)),
                                  pltpu.SemaphoreType.DMA((n,))], name=name)(*shards, *bufs, *deps)
    return list(outs)


HBM = pl.BlockSpec(memory_space=pltpu.HBM)
SEM = pl.BlockSpec(memory_space=pltpu.SEMAPHORE)


def _copies(plan, refs, send_sems, recv_sems):
    return [pltpu.make_async_remote_copy(src_ref=s, dst_ref=d, send_sem=send_sems.at[k], recv_sem=recv_sems.at[k],
                                         device_id=dev, device_id_type=MESH)
            for k, (s, d, dev) in enumerate(plan(refs, *_place()))]


def _xfer_start(bufs, ncopies, plan, name, deps=()):
    n = len(bufs)

    def body(*refs):
        for cp in _copies(plan, refs[:n], refs[n], refs[n + 1]):
            cp.start()
        token = refs[2 * n + 2]
        token[...] = jnp.zeros_like(token)

    outs = _pcall(
        _after(body, n, deps), name=name,
        out_shape=(pltpu.SemaphoreType.DMA((ncopies,)), pltpu.SemaphoreType.DMA((ncopies,)),
                   *[pltpu.HBM(b.shape, b.dtype) for b in bufs], _sds((8, LANE), F32)),
        in_specs=[HBM] * n + [ANY] * len(deps),
        out_specs=(SEM, SEM, *[HBM] * n, pl.BlockSpec(memory_space=pltpu.VMEM)),
        input_output_aliases={i: 2 + i for i in range(n)},
        compiler_params=pltpu.CompilerParams(has_side_effects=pltpu.SideEffectType.DATAFLOW_SIDE_EFFECTING),
    )(*[pltpu.with_memory_space_constraint(b, pltpu.HBM) for b in bufs], *deps)
    return (outs[0], outs[1]), list(outs[2:2 + n]), outs[2 + n]


def _xfer_wait(sems, bufs, plan, after, name):
    n = len(bufs)
    after = list(after) if isinstance(after, (list, tuple)) else [after]

    def body(*refs):
        for cp in _copies(plan, refs[:n], refs[n], refs[n + 1]):
            cp.wait_send()
            cp.wait_recv()

    outs = _pcall(
        body, name=name, out_shape=tuple(pltpu.HBM(b.shape, b.dtype) for b in bufs),
        in_specs=[HBM] * n + [SEM, SEM] + [ANY] * len(after), out_specs=tuple([HBM] * n),
        input_output_aliases={i: i for i in range(n)},
        compiler_params=pltpu.CompilerParams(has_side_effects=pltpu.SideEffectType.DATAFLOW_SIDE_EFFECTING),
    )(*bufs, *sems, *after)
    return list(outs)


def _chips_of(x, y):
    return [(1 - x, y), (x, 1 - y), (1 - x, 1 - y)]


def _landing(ref, dev_index, rows):
    block = ref.at[dev_index]
    return block if rows is None else block.at[pl.ds(rows[0], rows[1])]


def _gather_plan1(n, rows=None):
    def plan(refs, x, y, c):
        out = []
        for a in range(n):
            blk = _landing(refs[a], 4 * x + 2 * y + c, rows)
            out.append((blk, blk, (x, y, 1 - c)))
            out += [(blk, blk, (px, py, c)) for px, py in _chips_of(x, y)]
        return out
    return plan


def _gather_plan2(n, rows=None):
    def plan(refs, x, y, c):
        out = []
        for a in range(n):
            for px, py in _chips_of(x, y):
                blk = _landing(refs[a], 4 * px + 2 * py + c, rows)
                out.append((blk, blk, (x, y, 1 - c)))
        return out
    return plan


def _gather_start(shards, dev, name, deps=(), within=None):
    rows = None if within is None else (within[0], shards[0].shape[0])
    lands = []
    for s in shards:
        shape = (NDEV,) + s.shape if within is None else (NDEV, within[1]) + s.shape[1:]
        start = (dev,) + (0,) * s.ndim if within is None else (dev, within[0]) + (0,) * (s.ndim - 1)
        lands.append(lax.dynamic_update_slice(lax.empty(shape, s.dtype), s[None], start))
    n = len(shards)
    sems, lands, tok = _xfer_start(lands, 4 * n, _gather_plan1(n, rows), name + "_p1_start", deps)
    return dict(sems=sems, lands=lands, tok=tok, n=n, rows=rows)


def _gather_mid(st, after, name):
    n, rows = st["n"], st["rows"]
    lands = _xfer_wait(st["sems"], st["lands"], _gather_plan1(n, rows), after, name + "_p1_wait")
    sems, lands, tok = _xfer_start(lands, 3 * n, _gather_plan2(n, rows), name + "_p2_start")
    return dict(sems=sems, lands=lands, tok=tok, n=n, rows=rows)


def _gather_finish(st, after, name):
    return _xfer_wait(st["sems"], st["lands"], _gather_plan2(st["n"], st["rows"]), after, name + "_p2_wait")


def _scatter_plan1(n):
    def plan(refs, x, y, c):
        return [(refs[a].at[2 * p + 1 - c], refs[n + a].at[p], (x, y, 1 - c)) for a in range(n) for p in range(NCHIP)]
    return plan


def _scatter_plan2(n):
    def plan(refs, x, y, c):
        return [(refs[a].at[2 * px + py], refs[n + a].at[j], (px, py, c))
                for a in range(n) for j, (px, py) in enumerate(_chips_of(x, y))]
    return plan


def _scatter_start(Gs, name):
    n = len(Gs)
    R1s = [lax.empty((NCHIP,) + g.shape[1:], g.dtype) for g in Gs]
    sems, bufs, tok = _xfer_start(list(Gs) + R1s, NCHIP * n, _scatter_plan1(n), name + "_s1_start")
    return dict(sems=sems, bufs=bufs, tok=tok, n=n)


def _scatter_mid(st, after, my_c, name):
    n = st["n"]
    bufs = _xfer_wait(st["sems"], st["bufs"], _scatter_plan1(n), after, name + "_s1_wait")
    Ps = [_pair_sum(bufs[a], bufs[n + a], my_c, f"{name}_pair_sum{a}") for a in range(n)]
    R2s = [lax.empty((3,) + p.shape[1:], p.dtype) for p in Ps]
    sems, bufs, tok = _xfer_start(Ps + R2s, 3 * n, _scatter_plan2(n), name + "_s2_start")
    return dict(sems=sems, bufs=bufs, tok=tok, n=n)


def _scatter_finish(st, after, name):
    n = st["n"]
    bufs = _xfer_wait(st["sems"], st["bufs"], _scatter_plan2(n), after, name + "_s2_wait")
    return bufs[:n], bufs[n:]


SMALL_ROWS = {"norm1_g": (0, 1), "norm2_g": (1, 1), "sgu_ln_g": (2, 1), "sgu_ln_b": (3, 1), "cfm_conv_b": (4, 1),
              "cfm_ln_g": (5, 1), "cfm_ln_b": (6, 1), "b_sgu": (7, 1), "w_sgu": (8, 128), "b_ada": (136, N_MOD),
              "w_short": (142, SHORT_K), "cfm_conv_w": (145, CFM_K)}
ROWS_PER_LAYER = 176
FINAL_ROW = DEPTH * ROWS_PER_LAYER
PACK_ROWS = 360


def _pack(get, D, layers=tuple(range(DEPTH)), tail=True):
    parts = []
    for l in layers:
        for name, (_, nrows) in SMALL_ROWS.items():
            a = get(name, l)
            parts.append(jnp.zeros((nrows * D,), F32) if a is None else a.astype(F32).reshape(nrows * D))
    if tail:
        for name in ("final_g", "loss"):
            a = get(name, None)
            parts.append(jnp.zeros((D,), F32) if a is None else a.astype(F32).reshape(D))
        parts.append(jnp.zeros(((PACK_ROWS - FINAL_ROW - 2) * D,), F32))
    return jnp.concatenate(parts).reshape(-1, D)


def _unpack(pack, name, shape):
    D = pack.shape[1]
    r0, nrows = SMALL_ROWS[name]
    return jnp.stack([pack[l * ROWS_PER_LAYER + r0:l * ROWS_PER_LAYER + r0 + nrows] for l in range(DEPTH)]).reshape(shape)


def _mm_tiles(S):
    return min(512, S), min(1024, S), min(2048, S)


def kernel(x, c, w_ada, b_ada, norm1_g, w_in, w_short, w_a_out, sgu_ln_g, sgu_ln_b, w_sgu, b_sgu, w_b_out, cfm_conv_w, cfm_conv_b, cfm_ln_g, cfm_ln_b, w_c_out, w_o, norm2_g, w_ffn_in, w_ffn_out, final_g, loss_target, m_w_ada, m_b_ada, m_norm1_g, m_w_in, m_w_short, m_w_a_out, m_sgu_ln_g, m_sgu_ln_b, m_w_sgu, m_b_sgu, m_w_b_out, m_cfm_conv_w, m_cfm_conv_b, m_cfm_ln_g, m_cfm_ln_b, m_w_c_out, m_w_o, m_norm2_g, m_w_ffn_in, m_w_ffn_out, m_final_g, v_w_ada, v_b_ada, v_norm1_g, v_w_in, v_w_short, v_w_a_out, v_sgu_ln_g, v_sgu_ln_b, v_w_sgu, v_b_sgu, v_w_b_out, v_cfm_conv_w, v_cfm_conv_b, v_cfm_ln_g, v_cfm_ln_b, v_w_c_out, v_w_o, v_norm2_g, v_w_ffn_in, v_w_ffn_out, v_final_g):
    W = dict(w_ada=w_ada, b_ada=b_ada, norm1_g=norm1_g, w_in=w_in, w_short=w_short, w_a_out=w_a_out, sgu_ln_g=sgu_ln_g,
             sgu_ln_b=sgu_ln_b, w_sgu=w_sgu, b_sgu=b_sgu, w_b_out=w_b_out, cfm_conv_w=cfm_conv_w, cfm_conv_b=cfm_conv_b,
             cfm_ln_g=cfm_ln_g, cfm_ln_b=cfm_ln_b, w_c_out=w_c_out, w_o=w_o, norm2_g=norm2_g, w_ffn_in=w_ffn_in,
             w_ffn_out=w_ffn_out, final_g=final_g)
    Mo = dict(w_ada=m_w_ada, b_ada=m_b_ada, norm1_g=m_norm1_g, w_in=m_w_in, w_short=m_w_short, w_a_out=m_w_a_out,
              sgu_ln_g=m_sgu_ln_g, sgu_ln_b=m_sgu_ln_b, w_sgu=m_w_sgu, b_sgu=m_b_sgu, w_b_out=m_w_b_out,
              cfm_conv_w=m_cfm_conv_w, cfm_conv_b=m_cfm_conv_b, cfm_ln_g=m_cfm_ln_g, cfm_ln_b=m_cfm_ln_b,
              w_c_out=m_w_c_out, w_o=m_w_o, norm2_g=m_norm2_g, w_ffn_in=m_w_ffn_in, w_ffn_out=m_w_ffn_out,
              final_g=m_final_g)
    Vo = dict(w_ada=v_w_ada, b_ada=v_b_ada, norm1_g=v_norm1_g, w_in=v_w_in, w_short=v_w_short, w_a_out=v_w_a_out,
              sgu_ln_g=v_sgu_ln_g, sgu_ln_b=v_sgu_ln_b, w_sgu=v_w_sgu, b_sgu=v_b_sgu, w_b_out=v_w_b_out,
              cfm_conv_w=v_cfm_conv_w, cfm_conv_b=v_cfm_conv_b, cfm_ln_g=v_cfm_ln_g, cfm_ln_b=v_cfm_ln_b,
              w_c_out=v_w_c_out, w_o=v_w_o, norm2_g=v_norm2_g, w_ffn_in=v_w_ffn_in, w_ffn_out=v_w_ffn_out,
              final_g=v_final_g)
    order = ["w_ada", "b_ada", "norm1_g", "w_in", "w_short", "w_a_out", "sgu_ln_g", "sgu_ln_b", "w_sgu", "b_sgu",
             "w_b_out", "cfm_conv_w", "cfm_conv_b", "cfm_ln_g", "cfm_ln_b", "w_c_out", "w_o", "norm2_g", "w_ffn_in",
             "w_ffn_out", "final_g"]

    assert DEPTH == 2, "the weight-gather schedule below is written for two layers"
    S, D = x.shape[1], x.shape[2]
    F2 = w_ffn_in.shape[2] * NDEV
    FF = F2 // 2
    xi, yi, ci = _place()
    dev = 4 * xi + 2 * yi + ci
    my_c = jnp.reshape(ci, (1,)).astype(jnp.int32)
    my_chip = jnp.reshape(2 * xi + yi, (1,)).astype(jnp.int32)
    tm, tm_big, tm_huge = _mm_tiles(S)
    x0 = x.reshape(S, D)
    tgt = loss_target.reshape(S, D)

    def shards_of(l):
        return [w_in[l].astype(BF16), w_a_out[l].astype(BF16), w_b_out[l].astype(BF16), w_c_out[l].astype(BF16),
                w_o[l].astype(BF16), w_ffn_in[l].astype(BF16), w_ffn_out[l].astype(BF16)]

    c_all = _all_gather([jnp.pad(c, ((0, 7), (0, 0)))], "ag_c")[0][:, 0, :]
    modpart, c_act = _ada_fwd(c_all, w_ada, "ada_fwd")
    ncol = modpart.shape[2]
    mg = _all_gather([modpart.reshape(DEPTH * NDEV, ncol)], "ag_mod")[0].reshape(NDEV, DEPTH, NDEV, ncol)
    mine = lax.dynamic_index_in_dim(mg, dev, axis=2, keepdims=False)
    mod = (jnp.transpose(mine, (1, 0, 2)).reshape(DEPTH, N_MOD * D) + b_ada).reshape(DEPTH, N_MOD, D)

    ncs = w_short.shape[2]
    ag_in0 = _gather_start([w_in[0].astype(BF16), w_short.reshape(DEPTH * SHORT_K, ncs),
                            cfm_conv_w.reshape(DEPTH * CFM_K, ncs)], dev, "ag_w_in0", deps=(mod,))
    W, Mo, Vo = lax.optimization_barrier((ag_in0["tok"], (W, Mo, Vo)))[1]
    (norm1_g, norm2_g, w_in, w_a_out, w_b_out, w_c_out, w_o, w_ffn_in, w_ffn_out, sgu_ln_g, sgu_ln_b, w_sgu, b_sgu,
     cfm_conv_b, cfm_ln_g, cfm_ln_b, final_g) = [W[k] for k in (
         "norm1_g", "norm2_g", "w_in", "w_a_out", "w_b_out", "w_c_out", "w_o", "w_ffn_in", "w_ffn_out", "sgu_ln_g",
         "sgu_ln_b", "w_sgu", "b_sgu", "cfm_conv_b", "cfm_ln_g", "cfm_ln_b", "final_g")]
    m_w_ada, v_w_ada = Mo["w_ada"], Vo["w_ada"]
    xl0, h0, ht0 = _norm_fwd(x0, None, _rows(jnp.zeros((D,), F32), norm1_g[0], mod[0, 1], mod[0, 0]), "norm1_fwd0",
                             deps=(ag_in0["tok"],))
    ag_rest0 = _gather_start(shards_of(0)[1:], dev, "ag_rest0", deps=(h0,))

    tril = jnp.tril(jnp.ones((CHUNK, CHUNK), dtype=bool))

    def layer_consts(l):
        wt = jnp.where(tril[None], w_sgu[l], 0.0).astype(BF16)
        return dict(sgu_ln=_rows(sgu_ln_g[l], sgu_ln_b[l]), wtril=wt, wtril_t=jnp.swapaxes(wt, 1, 2),
                    bias_full=jnp.repeat(b_sgu[l].T, LANE, axis=1), cvec=_rows(cfm_conv_b[l], cfm_ln_g[l], cfm_ln_b[l]))

    def rest_of(g):
        return dict(w_a=g[0].reshape(1, D, D), w_b=g[1].reshape(1, D, D), w_c=g[2].reshape(1, D, D),
                    w_o=g[3].reshape(1, D, D), w_fi=jnp.transpose(g[4], (1, 0, 2)).reshape(1, D, F2),
                    w_fo=g[5].reshape(1, FF, D))

    sharded_small = ("w_short", "cfm_conv_w")

    def param_get(T):
        def get(name, l):
            if name == "final_g":
                return T[name]
            return None if name in sharded_small or name == "loss" else T[name][l]
        return get

    packs = [_pack(param_get(T), D) for T in (W, Mo, Vo)]
    ag_in0 = _gather_mid(ag_in0, [ag_rest0["tok"], *packs], "ag_w_in0")
    (w_sgu, b_sgu, sgu_ln_g, sgu_ln_b, cfm_conv_b, cfm_ln_g, cfm_ln_b), conv_wmv_in = lax.optimization_barrier(
        (ag_in0["tok"], ((w_sgu, b_sgu, sgu_ln_g, sgu_ln_b, cfm_conv_b, cfm_ln_g, cfm_ln_b),
                         [(T["w_short"], T["cfm_conv_w"]) for T in (W, Mo, Vo)])))[1]
    consts = [layer_consts(l) for l in range(DEPTH)]
    ncr = DEPTH * (SHORT_K + CFM_K)
    padr = (-ncr) % 8
    convw_wmv = [jnp.pad(jnp.concatenate([a.reshape(-1, ncs), b.reshape(-1, ncs)]), ((0, padr), (0, 0)))
                 for a, b in conv_wmv_in]
    g_in0 = _gather_finish(ag_in0, [*convw_wmv] + [a for cl in consts for a in cl.values()], "ag_w_in0")
    w_short_full = jnp.transpose(g_in0[1], (1, 0, 2)).reshape(DEPTH, SHORT_K, D)
    cfm_w_full = jnp.transpose(g_in0[2], (1, 0, 2)).reshape(DEPTH, CFM_K, D)
    for l in range(DEPTH):
        consts[l]["wsh"] = jnp.pad(w_short_full[l], ((0, 8 - SHORT_K), (0, 0)))
        consts[l]["cw"] = jnp.pad(cfm_w_full[l], ((0, HALO - CFM_K), (0, 0)))
    Wg = [dict(w_in=g_in0[0]), None]
    ag_l1 = None
    nin = w_in.shape[2]
    tn_in = nin if nin % 256 == 0 and nin <= 1280 else 256
    tn_fi = 512 if F2 % 512 == 0 else 256
    tn_dw = min(256, D)

    saved = []
    xcur, fprev, gprev = x0, None, None
    for l in range(DEPTH):
        sh1, sc1, g1, sh2, sc2, g2 = [mod[l, k] for k in range(N_MOD)]
        cl = consts[l]
        if l == 0:
            xl, h, ht = xl0, h0, ht0
        else:
            vec1 = _rows(gprev, norm1_g[l], sc1, sh1)
            ag_l1 = _gather_mid(ag_l1, fprev, f"ag_w{l}")
            xl, h, ht = _norm_fwd(xcur, fprev, vec1, f"norm1_fwd{l}", deps=(ag_l1["tok"],))
            g = _gather_finish(ag_l1, h, f"ag_w{l}")
            Wg[l] = dict(w_in=g[0], **rest_of(g[1:]))
        wl = Wg[l]
        z = _mm_nn(h, wl["w_in"], BF16, tm_huge, tn_in, D, f"mm_in{l}", w_outer=True)
        mix_deps = ()
        if l == 0:
            ag_rest0 = _gather_mid(ag_rest0, z, "ag_rest0")
            mix_deps = (ag_rest0["tok"],)
            if DEPTH > 1:
                ag_l1 = _gather_start(shards_of(1), dev, "ag_w1")
                mix_deps += (ag_l1["tok"],)
        acts, acts_t, conv = _mixer_fwd(z, cl["wsh"], cl["sgu_ln"], cl["wtril"], cl["bias_full"], cl["cw"], cl["cvec"],
                                        f"mixer_fwd{l}", deps=mix_deps)
        if l == 0:
            wl.update(rest_of(_gather_finish(ag_rest0, acts[0], "ag_rest0")))
        merged, merged_t, ys = _branch_out(acts, [wl["w_a"][0], wl["w_b"][0], wl["w_c"][0]], z, f"branch_out{l}")
        o = _mm_nn(merged, wl["w_o"], F32, tm_big, D, D, f"mm_o{l}")
        x1, h2, h2t = _norm_fwd(xl, o, _rows(g1, norm2_g[l], sc2, sh2), f"norm2_fwd{l}")
        gu, act, act_t = _ffn_in_swiglu(h2, wl["w_fi"], tm_huge, 256, f"mm_ffn_in{l}")
        f = _mm_nn(act, wl["w_fo"], F32, tm_big, D, FF, f"mm_ffn_out{l}")
        saved.append(dict(xl=xl, ht=ht, z=z, acts_t=acts_t, conv=conv, ys=ys, merged_t=merged_t, o=o, x1=x1, h2t=h2t, gu=gu,
                          act_t=act_t, f=f, consts=cl, mod=(sh1, sc1, g1, sh2, sc2, g2)))
        xcur, fprev, gprev = x1, f, g2

    last = saved[-1]
    dxup, dfb, fsums, loss_blk = _final_bwd(last["x1"], last["f"], tgt, _rows(last["mod"][5], final_g), "final_bwd")
    loss_row = jnp.pad(loss_blk[0, 0:1], (0, D - 1))
    dgate2_next = fsums[1]
    small = [dict() for _ in range(DEPTH)]
    dmods = [None] * DEPTH
    nfi = w_ffn_in.shape[2]
    early_names, late_names = ["w_ffn_out", "w_ffn_in", "w_o"], ["w_a_out", "w_b_out", "w_c_out", "w_in"]
    results = {n: None for n in early_names + late_names}

    def adam_group(names, Ps, R2s, l, deps=()):
        for n, p, r2 in zip(names, Ps, R2s):
            results[n] = _adam_big(p, r2, my_chip, W[n], Mo[n], Vo[n], l, results[n], f"adam_{n}{l}", deps)

    deferred = []
    late_prev = None
    ag_s1, gathered1 = None, None
    tk_w = min(2048, S)
    tn_dw_in = tn_in // 2 if tn_in == 1280 else tn_in
    for l in reversed(range(DEPTH)):
        sv, wl, cl = saved[l], Wg[l], saved[l]["consts"]
        sh1, sc1, g1, sh2, sc2, g2 = sv["mod"]
        dact = _mm_nt(dfb, wl["w_fo"], BF16, tm_big, FF, D, f"mm_dact{l}",
                      deps=() if late_prev is None else (late_prev["tok"], ag_s1["tok"]))
        g_fo = _mm_wgrad(sv["act_t"], dfb, 1, FF // 2, D, tk_w, f"mm_dw_ffn_out{l}")
        dgu = _swiglu_bwd(dact, sv["gu"], f"swiglu_bwd{l}")
        dh2 = _mm_nt(dgu, wl["w_fi"], F32, tm, D, F2, f"mm_dh2{l}")
        if late_prev is not None:
            deferred.append((late_names, *_scatter_finish(late_prev, dh2, f"rs_late{l + 1}"), l + 1))
            late_prev = None
        g_fi = _mm_wgrad(sv["h2t"], dgu, 1, D, tn_fi, S, f"mm_dw_ffn_in{l}")
        if ag_s1 is not None:
            ag_s1 = _gather_mid(ag_s1, g_fi, "ag_small1")
        dx1, dob, s2 = _norm_bwd(sv["x1"], dh2, dxup, _rows(norm2_g[l], sc2, g1), sv["o"], f"norm2_bwd{l}",
                                 deps=() if ag_s1 is None else (ag_s1["tok"],))
        dmerged = _mm_nt(dob, wl["w_o"], BF16, tm_big, D, D, f"mm_dmerged{l}")
        g_o = _mm_wgrad(sv["merged_t"], dob, 1, D, tn_dw, S, f"mm_dw_o{l}")
        early = _scatter_start([g_fo.reshape(NDEV, FF // NDEV, D),
                                jnp.transpose(g_fi.reshape(D, NDEV, nfi), (1, 0, 2)),
                                g_o.reshape(NDEV, D // NDEV, D)], f"rs_early{l}")
        dys, dz = _gate_bwd(dmerged, sv["z"], sv["ys"], f"gate_bwd{l}", deps=(early["tok"],))
        if ag_s1 is not None:
            gathered1 = _gather_finish(ag_s1, dys, "ag_small1")[0]
            ag_s1 = None
        early = _scatter_mid(early, dys, my_c, f"rs_early{l}")
        dacts = _mm3_nt(dys, [wl["w_a"], wl["w_b"], wl["w_c"]], tm_big, f"mm_dact_abc{l}", deps=(early["tok"],))
        g3 = _mm3_wgrad(sv["acts_t"], dys, tn_dw, f"mm_dw_abc{l}")
        g_abc = [g3[n] for n in range(3)]
        dz, mvec, dcw, dws, dbs = _mixer_bwd(sv["z"], dacts, sv["conv"], dz, cl["wsh"], cl["sgu_ln"], cl["wtril"],
                                             cl["wtril_t"], cl["bias_full"], cl["cw"], cl["cvec"], f"mixer_bwd{l}")
        dh = _mm_nt(dz, wl["w_in"], F32, tm_big, D, tn_in, f"mm_dh{l}",
                    blocks_per_step=2 if (tn_in == nin and wl["w_in"].shape[0] % 2 == 0) else 1)
        g_in = _mm_wgrad(sv["ht"], dz, NDEV, D, tn_dw_in, S, f"mm_dw_in{l}")
        late = _scatter_start([g.reshape(NDEV, D // NDEV, D) for g in g_abc] + [g_in], f"rs_late{l}")
        if l > 0:
            pv = saved[l - 1]
            dxup, dfb, s1 = _norm_bwd(sv["xl"], dh, dx1, _rows(norm1_g[l], sc1, pv["mod"][5]), pv["f"], f"norm1_bwd{l}",
                                      deps=(late["tok"],))
        else:
            dxup, dfb, s1 = _norm_bwd(sv["xl"], dh, dx1, _rows(norm1_g[l], sc1), None, f"norm1_bwd{l}", deps=(late["tok"],))
        deferred.append((early_names, *_scatter_finish(early, dxup, f"rs_early{l}"), l))
        dmods[l] = jnp.stack([s1[0], s1[1], s2[3], s2[0], s2[1], dgate2_next])
        dgate2_next = s1[3]
        small[l] = dict(norm1_g=s1[2], norm2_g=s2[2], sgu_ln_g=mvec[3], sgu_ln_b=mvec[4], cfm_conv_b=mvec[5],
                        cfm_ln_g=mvec[6], cfm_ln_b=mvec[7], b_sgu=dbs[:, :, 0],
                        w_sgu=jnp.where(tril[None], dws, 0.0), b_ada=dmods[l], w_short=mvec[0:SHORT_K],
                        cfm_conv_w=dcw[0:CFM_K])
        small_get = lambda name, k: {"final_g": fsums[0], "loss": loss_row}.get(name) if k is None else small[k][name]
        if l > 0:
            late_prev = _scatter_mid(late, dxup, my_c, f"rs_late{l}")
            ag_s1 = _gather_start([_pack(small_get, D, layers=(l,), tail=True)], dev, "ag_small1", deps=(late_prev["tok"],),
                                  within=(l * ROWS_PER_LAYER, PACK_ROWS))
    grad_x = dxup.reshape(x.shape)

    gathered = _all_gather([_pack(small_get, D, layers=(0,), tail=False)], "ag_small0", deps=(dxup,),
                           into=[(gathered1, 0)])[0]
    late_prev = _scatter_mid(late, gathered, my_c, "rs_late0")
    one_row = [n for n in order if n in SMALL_ROWS and SMALL_ROWS[n][1] == 1 and W[n].ndim == 2]
    (sg, sd, sm, sv_), singles = _adam_small(
        gathered, *packs, name="adam_small", deps=(late_prev["tok"],),
        single_rows=[tuple(l * ROWS_PER_LAYER + SMALL_ROWS[n][0] for l in range(DEPTH)) for n in one_row])
    loss = sg[FINAL_ROW + 1, 0]
    out = {n: tuple(singles[4 * i:4 * i + 4]) for i, n in enumerate(one_row)}
    for name in order:
        if name in SMALL_ROWS and name not in sharded_small and name not in out:
            out[name] = tuple(_unpack(p, name, W[name].shape) for p in (sg, sd, sm, sv_))
    out["final_g"] = tuple(p[FINAL_ROW] for p in (sg, sd, sm, sv_))

    def my_cols(name):
        full = _unpack(sg, name, (DEPTH, SMALL_ROWS[name][1], D))
        return lax.dynamic_slice_in_dim(full, dev * ncs, ncs, axis=2)

    gcs = jnp.concatenate([my_cols("w_short").reshape(-1, ncs), my_cols("cfm_conv_w").reshape(-1, ncs)])
    cd, cm, cv = _adam_plain(jnp.pad(gcs, ((0, padr), (0, 0))), *convw_wmv, "adam_convw")
    nsh = DEPTH * SHORT_K
    out["w_short"] = tuple(a[0:nsh].reshape(w_short.shape) for a in (gcs, cd, cm, cv))
    out["cfm_conv_w"] = tuple(a[nsh:ncr].reshape(cfm_conv_w.shape) for a in (gcs, cd, cm, cv))

    dm_all = jnp.stack([gathered[:, l * ROWS_PER_LAYER + 136:l * ROWS_PER_LAYER + 136 + N_MOD, :].reshape(NDEV, N_MOD * D)
                        for l in range(DEPTH)])
    dm_mine = lax.dynamic_slice_in_dim(dm_all, dev * ncol, ncol, axis=2)
    out["w_ada"] = tuple(_adam_ada(jnp.transpose(c_act), dm_mine, w_ada, m_w_ada, v_w_ada, "adam_ada"))

    for names, Ps, R2s, l in deferred:
        adam_group(names, Ps, R2s, l, deps=(late_prev["tok"],))
    adam_group(late_names, *_scatter_finish(late_prev, results["w_o"][0], "rs_late0"), 0)
    for n in early_names + late_names:
        out[n] = tuple(results[n])

    grads = [out[n][0] for n in order]
    deltas = [out[n][1] for n in order]
    new_m = [out[n][2] for n in order]
    new_v = [out[n][3] for n in order]
    return (loss, grad_x, *grads, *deltas, *new_m, *new_v)
```

```python
import functools
import math

import jax
import jax.numpy as jnp
from jax import lax
from jax.experimental import pallas as pl
from jax.experimental.pallas import tpu as pltpu

F32, BF16 = jnp.float32, jnp.bfloat16
NDEV = 8
NCHIP = NDEV // 2
DEPTH = 2
EPS = 1e-6
CHUNK = 128
NG = 8
SHORT_K = 3
CFM_K = 31
HALO = 32
N_MOD = 6
LANE = 128
VMEM_LIMIT = 56 * 1024 * 1024
ADAM_LR, ADAM_B1, ADAM_B2, ADAM_EPS, ADAM_WD, ADAM_STEP = 0.001, 0.9, 0.999, 1e-08, 0.01, 10
_G0 = math.sqrt(2.0 / math.pi)
_G1 = 0.044715
MESH = pl.DeviceIdType.MESH
ANY = pl.BlockSpec(memory_space=pl.ANY)


def _pcall(body, **kw):
    return pl.pallas_call(body, **kw)


def _params(sem=None):
    return pltpu.CompilerParams(dimension_semantics=sem, vmem_limit_bytes=VMEM_LIMIT)


def _sds(shape, dtype):
    return jax.ShapeDtypeStruct(tuple(shape), dtype)


def _mm_body(dims, nk, out_f32, blocks=1):
    def body(a_ref, b_ref, o_ref, *scr):
        k = pl.program_id(2)
        if blocks == 1:
            part = lax.dot_general(a_ref[...], b_ref[...], dims, preferred_element_type=F32)
        else:
            w = a_ref.shape[1] // blocks
            part = None
            for g in range(blocks):
                t = lax.dot_general(a_ref[:, g * w:(g + 1) * w], b_ref[g], dims, preferred_element_type=F32)
                part = t if part is None else part + t
        if nk == 1:
            o_ref[...] = part.reshape(o_ref.shape).astype(o_ref.dtype)
        elif out_f32:
            @pl.when(k == 0)
            def _():
                o_ref[...] = part.reshape(o_ref.shape)

            @pl.when(k > 0)
            def _():
                o_ref[...] += part.reshape(o_ref.shape)
        else:
            acc = scr[0]

            @pl.when(k == 0)
            def _():
                acc[...] = part

            @pl.when(k > 0)
            def _():
                acc[...] += part

            @pl.when(k == nk - 1)
            def _():
                o_ref[...] = acc[...].astype(o_ref.dtype)
    return body


def _after(body, n_in, deps):
    nd = len(deps)
    if nd == 0:
        return body

    def ordered(*refs):
        return body(*refs[:n_in], *refs[n_in + nd:])
    return ordered


def _mm_call(body, grid, in_specs, out_spec, out_shape, acc_shape, name, deps=()):
    scratch = [] if acc_shape is None else [pltpu.VMEM(acc_shape, F32)]
    return _pcall(_after(body, 2, deps), grid=grid, in_specs=in_specs + [ANY] * len(deps), out_specs=out_spec,
                  out_shape=out_shape, scratch_shapes=scratch, name=name,
                  compiler_params=_params(("parallel", "parallel", "arbitrary")))


def _mm_nn(a, b3, out_dtype, tm, tn, tk, name, w_outer=False, deps=()):
    M, K = a.shape
    G, _, Nb = b3.shape
    npb, nk = Nb // tn, K // tk
    out_f32 = out_dtype == F32
    body = _mm_body((((1,), (0,)), ((), ())), nk, out_f32)
    if w_outer:
        grid = (G * npb, M // tm, nk)
        ij = lambda p, q: (q, p)
    else:
        grid = (M // tm, G * npb, nk)
        ij = lambda p, q: (p, q)

    def a_map(p, q, k):
        i, j = ij(p, q)
        return (i, k)

    def b_map(p, q, k):
        i, j = ij(p, q)
        return (j // npb, k, j % npb)

    def o_map(p, q, k):
        return ij(p, q)

    def wrapped(a_ref, b_ref, o_ref, *scr):
        body(a_ref, b_ref, o_ref, *scr)

    return _mm_call(wrapped, grid, [pl.BlockSpec((tm, tk), a_map), pl.BlockSpec((None, tk, tn), b_map)],
                    pl.BlockSpec((tm, tn), o_map), _sds((M, G * Nb), out_dtype),
                    None if (nk == 1 or out_f32) else (tm, tn), name, deps)(a, b3, *deps)


def _mm_nt(a, b3, out_dtype, tm, tn, tk, name, deps=(), blocks_per_step=1):
    M, _ = a.shape
    G, Ko, Nb = b3.shape
    kpb = Nb // tk
    nk = G * kpb // blocks_per_step
    out_f32 = out_dtype == F32
    body = _mm_body((((1,), (1,)), ((), ())), nk, out_f32, blocks_per_step)

    def wrapped(a_ref, b_ref, o_ref, *scr):
        body(a_ref, b_ref, o_ref, *scr)

    if blocks_per_step > 1:
        assert tk == Nb and G % blocks_per_step == 0
        b_spec = pl.BlockSpec((blocks_per_step, tn, tk), lambda i, j, k: (k, j, 0))
    else:
        b_spec = pl.BlockSpec((None, tn, tk), lambda i, j, k: (k // kpb, j, k % kpb))
    return _mm_call(wrapped, (M // tm, Ko // tn, nk),
                    [pl.BlockSpec((tm, tk * blocks_per_step), lambda i, j, k: (i, k)), b_spec],
                    pl.BlockSpec((tm, tn), lambda i, j, k: (i, j)), _sds((M, Ko), out_dtype),
                    None if (nk == 1 or out_f32) else (tm, tn), name, deps)(a, b3, *deps)


def _mm_wgrad(at, b, G, tm, tn, tk, name, deps=()):
    M, T = at.shape
    Nb = b.shape[1] // G
    npb, nk = Nb // tn, T // tk
    body = _mm_body((((1,), (0,)), ((), ())), nk, False)

    def wrapped(a_ref, b_ref, o_ref, *scr):
        body(a_ref, b_ref, o_ref, *scr)

    a = at
    in_specs = [pl.BlockSpec((tm, tk), lambda i, j, k: (i, k)), pl.BlockSpec((tk, tn), lambda i, j, k: (k, j))]
    out_spec = pl.BlockSpec((None, tm, tn), lambda i, j, k: (j // npb, i, j % npb))
    return _mm_call(wrapped, (M // tm, G * npb, nk), in_specs, out_spec, _sds((G, M, Nb), BF16),
                    None if nk == 1 else (tm, tn), name, deps)(a, b, *deps)


def _mm3_nt(x3, ws, tm, name, deps=()):
    nb, S, K = x3.shape
    Ko = ws[0].shape[1]

    def body(x_ref, w0, w1, w2, o_ref):
        n = pl.program_id(0)
        for k, w in enumerate((w0, w1, w2)):
            @pl.when(n == k)
            def _(w=w):
                o_ref[...] = lax.dot_general(x_ref[...], w[...], (((1,), (1,)), ((), ())),
                                             preferred_element_type=F32).astype(BF16)

    wspec = pl.BlockSpec((None, Ko, K), lambda n, i: (0, 0, 0))
    return _pcall(_after(body, 4, deps), grid=(nb, S // tm),
                  in_specs=[pl.BlockSpec((None, tm, K), lambda n, i: (n, i, 0)), wspec, wspec, wspec] + [ANY] * len(deps),
                  out_specs=pl.BlockSpec((None, tm, Ko), lambda n, i: (n, i, 0)), out_shape=_sds((nb, S, Ko), BF16),
                  name=name, compiler_params=_params(("arbitrary", "parallel")))(x3, *ws, *deps)


def _mm3_wgrad(at3, b3, tn, name):
    nb, M, T = at3.shape
    N = b3.shape[2]

    def body(a_ref, b_ref, o_ref):
        o_ref[...] = jnp.dot(a_ref[...], b_ref[...], preferred_element_type=F32).astype(BF16)

    return _pcall(body, grid=(nb, N // tn),
                  in_specs=[pl.BlockSpec((None, M, T), lambda n, j: (n, 0, 0)), pl.BlockSpec((None, T, tn), lambda n, j: (n, 0, j))],
                  out_specs=pl.BlockSpec((None, M, tn), lambda n, j: (n, 0, j)), out_shape=_sds((nb, M, N), BF16),
                  name=name, compiler_params=_params(("arbitrary", "parallel")))(at3, b3)


def _rsum(v):
    return jnp.sum(v, axis=0, keepdims=True)


def _rmean(v):
    return jnp.mean(v, axis=-1, keepdims=True)


def _gelu(x):
    t = jnp.tanh(_G0 * (x + _G1 * (x * x * x)))
    return x * (0.5 * (1.0 + t)), t


def _dgelu(x, t):
    return 0.5 * (1.0 + t) + 0.5 * x * (1.0 - t * t) * (_G0 * (1.0 + 3.0 * _G1 * (x * x)))


def _sigmoid(x):
    return 0.5 * jnp.tanh(0.5 * x) + 0.5


def _fill_shifted(ext, rot):
    v = ext[...]
    n = v.shape[0]
    for b in range(1, 8):
        rot[b - 1] = pltpu.roll(v, n - b, 0)


def _rows_at(ext, rot, s, tm, cs=slice(None)):
    a, b = divmod(s, 8)
    return ext[8 * a:8 * a + tm, cs] if b == 0 else rot[b - 1, 8 * a:8 * a + tm, cs]


def _causal_conv(w_ref, taps, bias, ext, rot, offset, tm, out):
    D = out.shape[1]
    for cb in range(D // LANE):
        cs = slice(cb * LANE, (cb + 1) * LANE)
        acc = None
        for k, o in zip(taps, offset):
            term = w_ref[k:k + 1, cs] * _rows_at(ext, rot, o, tm, cs)
            acc = term if acc is None else acc + term
        out[:, cs] = acc if bias is None else acc + bias[:, cs]


def _rows(*vs):
    a = jnp.stack([v.astype(F32) for v in vs])
    return jnp.pad(a, ((0, 8 - len(vs)), (0, 0)))


def _row_spec(tm, D):
    return pl.BlockSpec((tm, D), lambda i: (i, 0))


def _const_spec(shape):
    nd = len(shape)
    return pl.BlockSpec(shape, lambda i: (0,) * nd)


def _norm_fwd(xp, f, vec, name, deps=()):
    S, D = xp.shape
    tm = min(512, S)
    has_f = f is not None

    def body(*refs):
        if has_f:
            xp_ref, f_ref, vec_ref, xo_ref, h_ref, ht_ref = refs
            x = xp_ref[...] + vec_ref[0:1, :] * f_ref[...]
            xo_ref[...] = x
        else:
            xp_ref, vec_ref, h_ref, ht_ref = refs
            x = xp_ref[...]
        r = lax.rsqrt(_rmean(x * x) + EPS)
        h = (x * r) * vec_ref[1:2, :]
        h = h * (1.0 + vec_ref[2:3, :]) + vec_ref[3:4, :]
        h_ref[...] = h.astype(BF16)
        ht_ref[...] = h.T.astype(BF16)

    rs = _row_spec(tm, D)
    ins = [xp, f, vec] if has_f else [xp, vec]
    in_specs = ([rs, rs] if has_f else [rs]) + [_const_spec((8, D))]
    out_shape = ([_sds((S, D), F32)] if has_f else []) + [_sds((S, D), BF16), _sds((D, S), BF16)]
    out_specs = [rs] * (len(out_shape) - 1) + [pl.BlockSpec((D, tm), lambda i: (0, i))]
    outs = _pcall(_after(body, len(ins), deps), grid=(S // tm,), in_specs=in_specs + [ANY] * len(deps),
                  out_specs=out_specs, out_shape=out_shape, name=name,
                  compiler_params=_params(("parallel",)))(*ins, *deps)
    return (outs[0], outs[1], outs[2]) if has_f else (xp, outs[0], outs[1])


def _mm_resid_norm(a, w3, xprev, vec, tm, name, deps=()):
    S, K = a.shape
    D = w3.shape[2]

    def body(a_ref, w_ref, xp_ref, vec_ref, p_ref, xo_ref, h_ref, ht_ref):
        p = jnp.dot(a_ref[...], w_ref[...], preferred_element_type=F32)
        p_ref[...] = p
        x = xp_ref[...] + vec_ref[0:1, :] * p
        xo_ref[...] = x
        r = lax.rsqrt(_rmean(x * x) + EPS)
        h = (x * r) * vec_ref[1:2, :]
        h = h * (1.0 + vec_ref[2:3, :]) + vec_ref[3:4, :]
        h_ref[...] = h.astype(BF16)
        ht_ref[...] = h.T.astype(BF16)

    rs = _row_spec(tm, D)
    return _pcall(_after(body, 4, deps), grid=(S // tm,),
                  in_specs=[_row_spec(tm, K), pl.BlockSpec((None, K, D), lambda i: (0, 0, 0)), rs, _const_spec((8, D))]
                  + [ANY] * len(deps),
                  out_specs=[rs, rs, rs, pl.BlockSpec((D, tm), lambda i: (0, i))],
                  out_shape=[_sds((S, D), F32), _sds((S, D), F32), _sds((S, D), BF16), _sds((D, S), BF16)], name=name,
                  compiler_params=_params(("parallel",)))(a, w3, xprev, vec, *deps)


def _mixer_fwd(z, wsh, sgu_ln, wtril, bias_full, cw, cvec, name, deps=()):
    S = z.shape[0]
    D = wsh.shape[1]
    tm = CHUNK

    def body(z_ref, wsh_ref, sln_ref, wt_ref, bias_ref, cw_ref, cv_ref, oa_ref, ob_ref, oc_ref, t_ref,
             conv_ref, pe, ge, gr, cbuf):
        i = pl.program_id(0)

        @pl.when(i == 0)
        def _():
            pe[0:HALO, :] = jnp.zeros((HALO, D), F32)
            ge[0:HALO, :] = jnp.zeros((HALO, D), F32)

        def col(n):
            return z_ref[:, n * D:(n + 1) * D].astype(F32)

        pe[HALO:HALO + tm, :] = col(1) * col(2)
        q = wsh_ref[0:1, :] * pe[HALO - 2:HALO - 2 + tm, :]
        q = q + wsh_ref[1:2, :] * pe[HALO - 1:HALO - 1 + tm, :]
        q = q + wsh_ref[2:3, :] * pe[HALO:HALO + tm, :]
        act_a = col(0) * q
        oa_ref[...] = act_a.astype(BF16)
        t_ref[0] = act_a.T.astype(BF16)
        gu, _ = _gelu(col(3))
        gv, _ = _gelu(col(4))
        d = gv - _rmean(gv)
        nrm = d * lax.rsqrt(_rmean(d * d) + EPS)
        vnb = (nrm * sln_ref[0:1, :] + sln_ref[1:2, :]).astype(BF16)
        for g in range(NG):
            cs = slice(g * LANE, (g + 1) * LANE)
            mixed = jnp.dot(wt_ref[g], vnb[:, cs], preferred_element_type=F32) + bias_ref[:, cs]
            act_b = gu[:, cs] * mixed
            ob_ref[:, cs] = act_b.astype(BF16)
            t_ref[1, cs, :] = act_b.T.astype(BF16)
        ge[HALO:HALO + tm, :] = col(5) * _sigmoid(col(6))
        _fill_shifted(ge, gr)
        o0 = HALO - (CFM_K - 1)
        _causal_conv(cw_ref, range(CFM_K), cv_ref[0:1, :], ge, gr, range(o0, o0 + CFM_K), tm, cbuf)
        conv = cbuf[...]
        conv_ref[...] = conv.astype(BF16)
        d = conv - _rmean(conv)
        ln = (d * lax.rsqrt(_rmean(d * d) + EPS)) * cv_ref[1:2, :] + cv_ref[2:3, :]
        act_c = ln * _sigmoid(ln)
        oc_ref[...] = act_c.astype(BF16)
        t_ref[2] = act_c.T.astype(BF16)
        pe[0:HALO, :] = pe[tm:tm + HALO, :]
        ge[0:HALO, :] = ge[tm:tm + HALO, :]

    rs = _row_spec(tm, D)
    outs = _pcall(
        _after(body, 7, deps), grid=(S // tm,),
        in_specs=[pl.BlockSpec((tm, 7 * D), lambda i: (i, 0)), _const_spec((8, D)), _const_spec((8, D)),
                  _const_spec((NG, CHUNK, CHUNK)), _const_spec((CHUNK, D)), _const_spec((HALO, D)), _const_spec((8, D))]
        + [ANY] * len(deps),
        out_specs=[rs, rs, rs, pl.BlockSpec((3, D, tm), lambda i: (0, 0, i)), rs],
        out_shape=[_sds((S, D), BF16)] * 3 + [_sds((3, D, S), BF16), _sds((S, D), BF16)],
        scratch_shapes=[pltpu.VMEM((HALO + tm, D), F32), pltpu.VMEM((HALO + tm, D), F32),
                        pltpu.VMEM((7, HALO + tm, D), F32), pltpu.VMEM((tm, D), F32)],
        name=name, compiler_params=_params(("arbitrary",)))(z, wsh, sgu_ln, wtril, bias_full, cw, cvec, *deps)
    return outs[:3], outs[3], outs[4]


def _branch_out(acts, ws, z, name):
    S, D = acts[0].shape
    tm = min(512, S)

    def body(a0, a1, a2, w0, w1, w2, g0, g1, g2, m_ref, mt_ref, y_ref):
        m = None
        for n, (a, w, g) in enumerate(((a0, w0, g0), (a1, w1, g1), (a2, w2, g2))):
            y = jnp.dot(a[...], w[...], preferred_element_type=F32)
            y_ref[n] = y.astype(BF16)
            t = _sigmoid(g[...].astype(F32)) * y
            m = t if m is None else m + t
        m_ref[...] = m.astype(BF16)
        mt_ref[...] = m.T.astype(BF16)

    rs = _row_spec(tm, D)
    gate_specs = [pl.BlockSpec((tm, D), functools.partial(lambda i, n: (i, 7 + n), n=n)) for n in range(3)]
    return _pcall(body, grid=(S // tm,),
                  in_specs=[rs, rs, rs] + [_const_spec((D, D))] * 3 + gate_specs,
                  out_specs=[rs, pl.BlockSpec((D, tm), lambda i: (0, i)), pl.BlockSpec((3, tm, D), lambda i: (0, i, 0))],
                  out_shape=[_sds((S, D), BF16), _sds((D, S), BF16), _sds((3, S, D), BF16)], name=name,
                  compiler_params=_params(("parallel",)))(*acts, *ws, z, z, z)


def _ffn_in_swiglu(h2, w3, tm, tn, name):
    S, D = h2.shape
    F = w3.shape[2] // 2
    nj = F // tn

    def body(a_ref, wg_ref, wu_ref, gu_ref, act_ref, actt_ref):
        a = a_ref[...]
        g = jnp.dot(a, wg_ref[...], preferred_element_type=F32)
        u = jnp.dot(a, wu_ref[...], preferred_element_type=F32)
        gu_ref[0] = g.astype(BF16)
        gu_ref[1] = u.astype(BF16)
        act = (g * _sigmoid(g)) * u
        act_ref[...] = act.astype(BF16)
        actt_ref[...] = act.T.astype(BF16)

    return _pcall(body, grid=(S // tm, nj),
                  in_specs=[pl.BlockSpec((tm, D), lambda i, j: (i, 0)), pl.BlockSpec((None, D, tn), lambda i, j: (0, 0, j)),
                            pl.BlockSpec((None, D, tn), lambda i, j: (0, 0, j + nj))],
                  out_specs=[pl.BlockSpec((2, tm, tn), lambda i, j: (0, i, j)), pl.BlockSpec((tm, tn), lambda i, j: (i, j)),
                             pl.BlockSpec((tn, tm), lambda i, j: (j, i))],
                  out_shape=[_sds((2, S, F), BF16), _sds((S, F), BF16), _sds((F, S), BF16)], name=name,
                  compiler_params=_params(("parallel", "parallel")))(h2, w3, w3)


def _swiglu_bwd(dact, gu, name):
    _, S, F = gu.shape
    F2 = 2 * F
    tm = min(256, S)

    def body(d_ref, g_ref, u_ref, o_ref):
        g = g_ref[...].astype(F32)
        sg = _sigmoid(g)
        d = d_ref[...].astype(F32)
        o_ref[:, 0:F] = (d * u_ref[...].astype(F32) * (sg * (1.0 + g * (1.0 - sg)))).astype(BF16)
        o_ref[:, F:2 * F] = (d * (g * sg)).astype(BF16)

    return _pcall(body, grid=(S // tm,),
                  in_specs=[pl.BlockSpec((tm, F), lambda i: (i, 0)), pl.BlockSpec((None, tm, F), lambda i: (0, i, 0)),
                            pl.BlockSpec((None, tm, F), lambda i: (1, i, 0))],
                  out_specs=pl.BlockSpec((tm, F2), lambda i: (i, 0)), out_shape=_sds((S, F2), BF16), name=name,
                  compiler_params=_params(("parallel",)))(dact, gu, gu)


def _final_bwd(x1, f, tgt, vec, name):
    S, D = x1.shape
    tm = min(512, S)

    def body(x_ref, f_ref, t_ref, vec_ref, dx_ref, df_ref, sums_ref, loss_ref):
        @pl.when(pl.program_id(0) == 0)
        def _():
            sums_ref[...] = jnp.zeros_like(sums_ref)
            loss_ref[...] = jnp.zeros_like(loss_ref)

        gate, fg = vec_ref[0:1, :], vec_ref[1:2, :]
        fv = f_ref[...]
        x = x_ref[...] + gate * fv
        r = lax.rsqrt(_rmean(x * x) + EPS)
        xn = x * r
        diff = xn * fg - t_ref[...]
        per_tok = _rmean(diff * diff)
        loss_ref[...] += 0.5 * jnp.sum(per_tok, axis=0, keepdims=True)
        dy = diff * (1.0 / D)
        sums_ref[0:1, :] += _rsum(dy * xn)
        dxn = dy * fg
        dx = r * (dxn - xn * _rmean(dxn * xn))
        sums_ref[1:2, :] += _rsum(dx * fv)
        dx_ref[...] = dx
        df_ref[...] = (dx * gate).astype(BF16)

    rs = _row_spec(tm, D)
    return _pcall(body, grid=(S // tm,), in_specs=[rs, rs, rs, _const_spec((8, D))],
                  out_specs=[rs, rs, _const_spec((8, D)), _const_spec((8, LANE))],
                  out_shape=[_sds((S, D), F32), _sds((S, D), BF16), _sds((8, D), F32), _sds((8, LANE), F32)],
                  name=name, compiler_params=_params(("arbitrary",)))(x1, f, tgt, vec)


def _norm_bwd(xin, dh, dxup, vec, fprev, name, deps=()):
    S, D = xin.shape
    tm = min(512, S)
    has_prev = fprev is not None

    def body(*refs):
        if has_prev:
            x_ref, dh_ref, up_ref, vec_ref, fp_ref, dx_ref, dp_ref, sums_ref = refs
        else:
            x_ref, dh_ref, up_ref, vec_ref, dx_ref, sums_ref = refs

        @pl.when(pl.program_id(0) == 0)
        def _():
            sums_ref[...] = jnp.zeros_like(sums_ref)

        g, scale = vec_ref[0:1, :], vec_ref[1:2, :]
        x = x_ref[...]
        r = lax.rsqrt(_rmean(x * x) + EPS)
        xn = x * r
        dhv = dh_ref[...]
        sums_ref[0:1, :] += _rsum(dhv)
        sums_ref[1:2, :] += _rsum(dhv * (xn * g))
        dm = dhv * (1.0 + scale)
        sums_ref[2:3, :] += _rsum(dm * xn)
        dxn = dm * g
        dx = up_ref[...] + r * (dxn - xn * _rmean(dxn * xn))
        dx_ref[...] = dx
        if has_prev:
            sums_ref[3:4, :] += _rsum(dx * fp_ref[...])
            dp_ref[...] = (dx * vec_ref[2:3, :]).astype(BF16)

    rs = _row_spec(tm, D)
    ins = [xin, dh, dxup, vec] + ([fprev] if has_prev else [])
    in_specs = [rs, rs, rs, _const_spec((8, D))] + ([rs] if has_prev else [])
    out_shape = [_sds((S, D), F32)] + ([_sds((S, D), BF16)] if has_prev else []) + [_sds((8, D), F32)]
    out_specs = [rs] + ([rs] if has_prev else []) + [_const_spec((8, D))]
    outs = _pcall(_after(body, len(ins), deps), grid=(S // tm,), in_specs=in_specs + [ANY] * len(deps),
                  out_specs=out_specs, out_shape=out_shape, name=name,
                  compiler_params=_params(("arbitrary",)))(*ins, *deps)
    return (outs[0], outs[1], outs[2]) if has_prev else (outs[0], None, outs[1])


def _gate_bwd(dmerged, z, ys, name, deps=()):
    S, D = dmerged.shape
    tm = min(512, S)
    ncol = z.shape[1] // D

    def body(dm_ref, g_ref, y_ref, dy_ref, dz_ref):
        sg = _sigmoid(g_ref[...].astype(F32))
        dm = dm_ref[...].astype(F32)
        dy_ref[...] = (dm * sg).astype(BF16)
        dz_ref[...] = (dm * y_ref[...].astype(F32) * (sg * (1.0 - sg))).astype(BF16)

    branch = pl.BlockSpec((None, tm, D), lambda i, n: (n, i, 0))
    return _pcall(_after(body, 3, deps), grid=(S // tm, 3),
                  in_specs=[pl.BlockSpec((tm, D), lambda i, n: (i, 0)), pl.BlockSpec((tm, D), lambda i, n: (i, 7 + n)),
                            branch] + [ANY] * len(deps),
                  out_specs=[branch, pl.BlockSpec((tm, D), lambda i, n: (i, 7 + n))],
                  out_shape=[_sds((3, S, D), BF16), _sds((S, ncol * D), BF16)], name=name,
                  compiler_params=_params(("parallel", "arbitrary")))(dmerged, z, ys, *deps)


def _mixer_bwd(z, dacts, conv, dz, wsh, sgu_ln, wtril, wtril_t, bias_full, cw, cvec, name):
    S = z.shape[0]
    D = wsh.shape[1]
    tm = CHUNK
    nt = S // tm
    hb = tm // HALO

    def body(zc, zp, da_ref, db_ref, dc_ref, conv_ref, wsh_ref, sln_ref, wt_ref, wtt_ref, bias_ref, cw_ref, cv_ref, _dz_in,
             dz_ref, vec_ref, dcw_ref, dws_ref, dbs_ref, pe, ge, dqe, dce, gr, dcr, cbuf, dcw8):
        i = pl.program_id(0)
        rb = nt - 1 - i

        @pl.when(i == 0)
        def _():
            vec_ref[...] = jnp.zeros_like(vec_ref)
            dcw8[...] = jnp.zeros_like(dcw8)
            dws_ref[...] = jnp.zeros_like(dws_ref)
            dbs_ref[...] = jnp.zeros_like(dbs_ref)
            dqe[tm:tm + HALO, :] = jnp.zeros((HALO, D), F32)
            dce[tm:tm + HALO, :] = jnp.zeros((HALO, D), F32)

        keep = (rb > 0).astype(F32)

        def col(n):
            return zc[:, n * D:(n + 1) * D].astype(F32)

        def pcol(n):
            return zp[:, n * D:(n + 1) * D].astype(F32)

        c_a, x_a = col(1), col(2)
        pe[0:HALO, :] = keep * (pcol(1) * pcol(2))
        pe[HALO:HALO + tm, :] = c_a * x_a
        q = wsh_ref[0:1, :] * pe[HALO - 2:HALO - 2 + tm, :]
        q = q + wsh_ref[1:2, :] * pe[HALO - 1:HALO - 1 + tm, :]
        q = q + wsh_ref[2:3, :] * pe[HALO:HALO + tm, :]
        dact = da_ref[...].astype(F32)
        dz_ref[:, 0:D] = (dact * q).astype(BF16)
        dq = dact * col(0)
        dqe[0:tm, :] = dq
        dp = wsh_ref[2:3, :] * dq + wsh_ref[1:2, :] * dqe[1:1 + tm, :] + wsh_ref[0:1, :] * dqe[2:2 + tm, :]
        dz_ref[:, D:2 * D] = (dp * x_a).astype(BF16)
        dz_ref[:, 2 * D:3 * D] = (dp * c_a).astype(BF16)
        for k in range(SHORT_K):
            o = HALO - (SHORT_K - 1) + k
            vec_ref[k:k + 1, :] += _rsum(dq * pe[o:o + tm, :])
        u, v = col(3), col(4)
        gu, tu = _gelu(u)
        gv, tv = _gelu(v)
        d = gv - _rmean(gv)
        rstd = lax.rsqrt(_rmean(d * d) + EPS)
        nrm = d * rstd
        vnb = (nrm * sln_ref[0:1, :] + sln_ref[1:2, :]).astype(BF16)
        dact = db_ref[...].astype(F32)
        dvn_parts, dgu_parts = [], []
        for g in range(NG):
            cs = slice(g * LANE, (g + 1) * LANE)
            vg = vnb[:, cs]
            mixed = jnp.dot(wt_ref[g], vg, preferred_element_type=F32) + bias_ref[:, cs]
            dgu_parts.append(dact[:, cs] * mixed)
            dmixed = dact[:, cs] * gu[:, cs]
            dmb = dmixed.astype(BF16)
            dws_ref[g] += lax.dot_general(dmb, vg, (((1,), (1,)), ((), ())), preferred_element_type=F32)
            dbs_ref[g] += jnp.broadcast_to(jnp.sum(dmixed, axis=1, keepdims=True), (CHUNK, LANE))
            dvn_parts.append(jnp.dot(wtt_ref[g], dmb, preferred_element_type=F32))
        dgu = jnp.concatenate(dgu_parts, axis=1)
        dvn = jnp.concatenate(dvn_parts, axis=1)
        dz_ref[:, 3 * D:4 * D] = (dgu * _dgelu(u, tu)).astype(BF16)
        vec_ref[3:4, :] += _rsum(dvn * nrm)
        vec_ref[4:5, :] += _rsum(dvn)
        dn = dvn * sln_ref[0:1, :]
        dgv = rstd * (dn - _rmean(dn) - nrm * _rmean(dn * nrm))
        dz_ref[:, 4 * D:5 * D] = (dgv * _dgelu(v, tv)).astype(BF16)
        a_c = col(5)
        sg = _sigmoid(col(6))
        ge[0:HALO, :] = keep * (pcol(5) * _sigmoid(pcol(6)))
        ge[HALO:HALO + tm, :] = a_c * sg
        _fill_shifted(ge, gr)
        o0 = HALO - (CFM_K - 1)
        conv = conv_ref[...].astype(F32)
        d = conv - _rmean(conv)
        rstd = lax.rsqrt(_rmean(d * d) + EPS)
        nrm = d * rstd
        ln = nrm * cv_ref[1:2, :] + cv_ref[2:3, :]
        sl = _sigmoid(ln)
        dln = dc_ref[...].astype(F32) * (sl * (1.0 + ln * (1.0 - sl)))
        vec_ref[6:7, :] += _rsum(dln * nrm)
        vec_ref[7:8, :] += _rsum(dln)
        dn = dln * cv_ref[1:2, :]
        dconv = rstd * (dn - _rmean(dn) - nrm * _rmean(dn * nrm))
        vec_ref[5:6, :] += _rsum(dconv)
        dce[0:tm, :] = dconv
        _fill_shifted(dce, dcr)
        _causal_conv(cw_ref, range(CFM_K), None, dce, dcr, [CFM_K - 1 - k for k in range(CFM_K)], tm, cbuf)
        dglu = cbuf[...]
        for cb in range(D // LANE):
            cs = slice(cb * LANE, (cb + 1) * LANE)
            dcv = dce[0:tm, cs]
            for k in range(CFM_K):
                prod = dcv * _rows_at(ge, gr, o0 + k, tm, cs)
                dcw8[k, :, cs] += jnp.sum(prod.reshape(tm // 8, 8, LANE), axis=0)

        @pl.when(i == nt - 1)
        def _():
            dcw_ref[...] = jnp.sum(dcw8[...], axis=1)
        dz_ref[:, 5 * D:6 * D] = (dglu * sg).astype(BF16)
        dz_ref[:, 6 * D:7 * D] = (dglu * a_c * (sg * (1.0 - sg))).astype(BF16)
        dqe[tm:tm + HALO, :] = dqe[0:HALO, :]
        dce[tm:tm + HALO, :] = dce[0:HALO, :]

    rev = lambda i: (nt - 1 - i, 0)
    rs = pl.BlockSpec((tm, D), rev)
    cur = pl.BlockSpec((tm, 7 * D), rev)
    prev = pl.BlockSpec((HALO, 7 * D), lambda i: (jnp.maximum((nt - 1 - i) * hb - 1, 0), 0))
    ext = pltpu.VMEM((HALO + tm, D), F32)
    outs = _pcall(
        body, grid=(nt,),
        in_specs=[cur, prev] + [pl.BlockSpec((None, tm, D), functools.partial(lambda i, n: (n, nt - 1 - i, 0), n=n))
                                for n in range(3)]
        + [rs, _const_spec((8, D)), _const_spec((8, D)), _const_spec((NG, CHUNK, CHUNK)),
                  _const_spec((NG, CHUNK, CHUNK)), _const_spec((CHUNK, D)), _const_spec((HALO, D)), _const_spec((8, D)),
                  ANY],
        out_specs=[cur, _const_spec((8, D)), _const_spec((HALO, D)), _const_spec((NG, CHUNK, CHUNK)),
                   _const_spec((NG, CHUNK, LANE))],
        out_shape=[_sds(dz.shape, BF16), _sds((8, D), F32), _sds((HALO, D), F32), _sds((NG, CHUNK, CHUNK), F32),
                   _sds((NG, CHUNK, LANE), F32)],
        scratch_shapes=[ext, ext, ext, ext, pltpu.VMEM((7, HALO + tm, D), F32), pltpu.VMEM((7, HALO + tm, D), F32),
                        pltpu.VMEM((tm, D), F32), pltpu.VMEM((HALO, 8, D), F32)],
        input_output_aliases={13: 0}, name=name,
        compiler_params=_params(("arbitrary",)))(z, z, dacts, dacts, dacts, conv, wsh, sgu_ln, wtril, wtril_t, bias_full, cw,
                                                 cvec, dz)
    return outs


def _ada_fwd(c_all, w_ada_loc, name):
    nb, D = c_all.shape
    L, _, nc = w_ada_loc.shape

    def body(c_ref, w_ref, o_ref, ca_ref):
        cv = c_ref[...]
        ca = cv * _sigmoid(cv)
        ca_ref[...] = ca
        o_ref[...] = jnp.dot(ca.astype(BF16), w_ref[...].astype(BF16), preferred_element_type=F32)

    return _pcall(body, grid=(L,),
                  in_specs=[_const_spec((nb, D)), pl.BlockSpec((None, D, nc), lambda l: (l, 0, 0))],
                  out_specs=[pl.BlockSpec((None, nb, nc), lambda l: (l, 0, 0)), _const_spec((nb, D))],
                  out_shape=[_sds((L, nb, nc), F32), _sds((nb, D), F32)], name=name,
                  compiler_params=_params(("arbitrary",)))(c_all, w_ada_loc)


def _adamw(w, g, m, v):
    m = ADAM_B1 * m + (1.0 - ADAM_B1) * g
    v = ADAM_B2 * v + (1.0 - ADAM_B2) * (g * g)
    m_hat = m / (1.0 - ADAM_B1 ** ADAM_STEP)
    v_hat = v / (1.0 - ADAM_B2 ** ADAM_STEP)
    delta = -ADAM_LR * (m_hat / (jnp.sqrt(v_hat) + ADAM_EPS) + ADAM_WD * w)
    return delta, m, v


def _tile_rows(R, C, align=8):
    cap = max(align, (1536 * 1024) // (4 * C))
    best = None
    for t in range(align, R + 1, align):
        if R % t == 0 and t <= cap:
            best = t
    return R if best is None else best


def _adam_ada(ct, dm, w, m, v, name):
    L, D, nc = w.shape
    nb = ct.shape[1]
    tr = _tile_rows(D, nc)

    def body(ct_ref, dm_ref, w_ref, m_ref, v_ref, g_ref, d_ref, mo_ref, vo_ref):
        g = ct_ref[:, 0:1] * dm_ref[0:1, :]
        for b in range(1, nb):
            g = g + ct_ref[:, b:b + 1] * dm_ref[b:b + 1, :]
        g_ref[...] = g
        d_ref[...], mo_ref[...], vo_ref[...] = _adamw(w_ref[...], g, m_ref[...], v_ref[...])

    ws = pl.BlockSpec((None, tr, nc), lambda l, r: (l, r, 0))
    return _pcall(body, grid=(L, D // tr),
                  in_specs=[pl.BlockSpec((tr, nb), lambda l, r: (r, 0)), pl.BlockSpec((None, nb, nc), lambda l, r: (l, 0, 0)),
                            ws, ws, ws],
                  out_specs=[ws] * 4, out_shape=[_sds(w.shape, F32)] * 4, name=name,
                  compiler_params=_params(("parallel", "parallel")))(ct, dm, w, m, v)


def _adam_small(parts, w, m, v, name, deps=(), single_rows=()):
    n, R, C = parts.shape
    tr = _tile_rows(R, C * n // 2)
    nl = len(single_rows[0]) if single_rows else 0

    def body(p_ref, w_ref, m_ref, v_ref, g_ref, d_ref, mo_ref, vo_ref, *single):
        g = p_ref[0]
        for j in range(1, n):
            g = g + p_ref[j]
        d, mo, vo = _adamw(w_ref[...], g, m_ref[...], v_ref[...])
        g_ref[...], d_ref[...], mo_ref[...], vo_ref[...] = g, d, mo, vo
        step = pl.program_id(0)
        for pi, rows in enumerate(single_rows):
            for l, row in enumerate(rows):
                @pl.when(step == row // tr)
                def _(pi=pi, l=l, off=row % tr):
                    for k, val in enumerate((g, d, mo, vo)):
                        single[4 * pi + k][l:l + 1, :] = val[off:off + 1, :]

    ws = pl.BlockSpec((tr, C), lambda r: (r, 0))
    one = pl.BlockSpec((nl, C), lambda r: (0, 0))
    outs = _pcall(_after(body, 4, deps), grid=(R // tr,),
                  in_specs=[pl.BlockSpec((n, tr, C), lambda r: (0, r, 0)), ws, ws, ws] + [ANY] * len(deps),
                  out_specs=[ws] * 4 + [one] * (4 * len(single_rows)),
                  out_shape=[_sds((R, C), F32)] * 4 + [_sds((nl, C), F32)] * (4 * len(single_rows)), name=name,
                  compiler_params=_params(("arbitrary",)))(parts, w, m, v, *deps)
    return outs[:4], outs[4:]


def _adam_plain(g, w, m, v, name):
    R, C = w.shape

    def body(g_ref, w_ref, m_ref, v_ref, d_ref, mo_ref, vo_ref):
        d_ref[...], mo_ref[...], vo_ref[...] = _adamw(w_ref[...], g_ref[...], m_ref[...], v_ref[...])

    ws = _const_spec((R, C))
    return _pcall(body, grid=(1,), in_specs=[ws] * 4, out_specs=[ws] * 3, out_shape=[_sds((R, C), F32)] * 3, name=name,
                  compiler_params=_params(("arbitrary",)))(g, w, m, v)


def _pair_sum(G, R1, my_c, name):
    n, R, C = G.shape
    half = n // 2
    tr = _tile_rows(R, C, align=16)

    def body(c_ref, g_ref, r_ref, o_ref):
        o_ref[...] = (g_ref[...].astype(F32) + r_ref[...].astype(F32)).astype(o_ref.dtype)

    blk = (None, tr, C)
    gs = pltpu.PrefetchScalarGridSpec(
        num_scalar_prefetch=1, grid=(half, R // tr),
        in_specs=[pl.BlockSpec(blk, lambda p, r, c: (2 * p + c[0], r, 0)), pl.BlockSpec(blk, lambda p, r, c: (p, r, 0))],
        out_specs=pl.BlockSpec(blk, lambda p, r, c: (p, r, 0)))
    return _pcall(body, grid_spec=gs, out_shape=_sds((half, R, C), G.dtype), name=name,
                  compiler_params=_params(("parallel", "parallel")))(my_c, G, R1)


def _adam_big(P, R2, my_chip, w, m, v, layer, prev, name, deps=()):
    _, R, C = P.shape
    nrecv = R2.shape[0]
    tr = _tile_rows(R, C, align=16)

    def body(p_sm, p_ref, r_ref, w_ref, m_ref, v_ref, *rest):
        g_ref, d_ref, mo_ref, vo_ref = rest[-4:]
        g = p_ref[...].astype(F32)
        for k in range(nrecv):
            g = g + r_ref[k].astype(F32)
        g_ref[...] = g
        d_ref[...], mo_ref[...], vo_ref[...] = _adamw(w_ref[...], g, m_ref[...], v_ref[...])

    ws = pl.BlockSpec((None, tr, C), lambda r, p: (layer, r, 0))
    held = [] if prev is None else list(prev)
    gs = pltpu.PrefetchScalarGridSpec(
        num_scalar_prefetch=1, grid=(R // tr,),
        in_specs=[pl.BlockSpec((None, tr, C), lambda r, p: (p[0], r, 0)),
                  pl.BlockSpec((nrecv, tr, C), lambda r, p: (0, r, 0)), ws, ws, ws] + [ANY] * (len(held) + len(deps)),
        out_specs=[ws] * 4)
    alias = {6 + i: i for i in range(len(held))}
    return _pcall(body, grid_spec=gs, out_shape=[_sds(w.shape, F32)] * 4, name=name, input_output_aliases=alias,
                  compiler_params=_params(("parallel",)))(my_chip, P, R2, w, m, v, *held, *deps)


def _place():
    return lax.axis_index("x"), lax.axis_index("y"), lax.axis_index("c")


def _all_gather(shards, name, deps=(), into=None):
    n = len(shards)
    bufs = [] if into is None else [b for b, _ in into]
    nb = len(bufs)

    def body(*refs):
        ins, outs = refs[:n], refs[n + nb:2 * n + nb]
        send_sems, recv_sems, local_sems = refs[2 * n + nb:]
        x, y, c = _place()
        me, sibling = (x, y, c), (x, y, 1 - c)
        chips = [(1 - x, y), (x, 1 - y), (1 - x, 1 - y)]

        def slot(a, px, py, pc):
            block = outs[a].at[4 * px + 2 * py + pc]
            return block if into is None else block.at[pl.ds(into[a][1], ins[a].shape[0])]

        def copy(a, k, block, to, src=None):
            return pltpu.make_async_remote_copy(
                src_ref=slot(a, *block) if src is None else src, dst_ref=slot(a, *block),
                send_sem=send_sems.at[7 * a + k], recv_sem=recv_sems.at[7 * a + k], device_id=to, device_id_type=MESH)

        mine = [pltpu.make_async_copy(ins[a], slot(a, *me), local_sems.at[a]) for a in range(n)]
        for cp in mine:
            cp.start()
        first = []
        for a in range(n):
            first.append(copy(a, 0, me, sibling, src=ins[a]))
            first += [copy(a, 1 + j, me, (*chip, c), src=ins[a]) for j, chip in enumerate(chips)]
        for cp in first:
            cp.start()
        passed = []
        for j, chip in enumerate(chips):
            for a in range(n):
                copy(a, 1 + j, (*chip, c), me).wait_recv()
                fwd = copy(a, 4 + j, (*chip, c), sibling)
                fwd.start()
                passed.append(fwd)
        for a in range(n):
            copy(a, 0, sibling, me).wait_recv()
        for j, chip in enumerate(chips):
            for a in range(n):
                copy(a, 4 + j, (*chip, 1 - c), me).wait_recv()
        for cp in first + passed:
            cp.wait_send()
        for cp in mine:
            cp.wait()

    out_shape = [_sds((NDEV,) + s.shape, s.dtype) for s in shards] if into is None else [_sds(b.shape, b.dtype) for b in bufs]
    outs = _pcall(_after(body, n + nb, deps), in_specs=[ANY] * (n + nb + len(deps)), out_specs=[ANY] * n,
                  out_shape=out_shape, input_output_aliases={n + a: a for a in range(nb)},
                  scratch_shapes=[pltpu.SemaphoreType.DMA((7 * n,)), pltpu.SemaphoreType.DMA((7 * n,)),
                                  pltpu.SemaphoreType.DMA((n,))], name=name)(*shards, *bufs, *deps)
    return list(outs)


HBM = pl.BlockSpec(memory_space=pltpu.HBM)
SEM = pl.BlockSpec(memory_space=pltpu.SEMAPHORE)


def _copies(plan, refs, send_sems, recv_sems):
    return [pltpu.make_async_remote_copy(src_ref=s, dst_ref=d, send_sem=send_sems.at[k], recv_sem=recv_sems.at[k],
                                         device_id=dev, device_id_type=MESH)
            for k, (s, d, dev) in enumerate(plan(refs, *_place()))]


def _xfer_start(bufs, ncopies, plan, name, deps=()):
    n = len(bufs)

    def body(*refs):
        for cp in _copies(plan, refs[:n], refs[n], refs[n + 1]):
            cp.start()
        token = refs[2 * n + 2]
        token[...] = jnp.zeros_like(token)

    outs = _pcall(
        _after(body, n, deps), name=name,
        out_shape=(pltpu.SemaphoreType.DMA((ncopies,)), pltpu.SemaphoreType.DMA((ncopies,)),
                   *[pltpu.HBM(b.shape, b.dtype) for b in bufs], _sds((8, LANE), F32)),
        in_specs=[HBM] * n + [ANY] * len(deps),
        out_specs=(SEM, SEM, *[HBM] * n, pl.BlockSpec(memory_space=pltpu.VMEM)),
        input_output_aliases={i: 2 + i for i in range(n)},
        compiler_params=pltpu.CompilerParams(has_side_effects=pltpu.SideEffectType.DATAFLOW_SIDE_EFFECTING),
    )(*[pltpu.with_memory_space_constraint(b, pltpu.HBM) for b in bufs], *deps)
    return (outs[0], outs[1]), list(outs[2:2 + n]), outs[2 + n]


def _xfer_wait(sems, bufs, plan, after, name):
    n = len(bufs)
    after = list(after) if isinstance(after, (list, tuple)) else [after]

    def body(*refs):
        for cp in _copies(plan, refs[:n], refs[n], refs[n + 1]):
            cp.wait_send()
            cp.wait_recv()

    outs = _pcall(
        body, name=name, out_shape=tuple(pltpu.HBM(b.shape, b.dtype) for b in bufs),
        in_specs=[HBM] * n + [SEM, SEM] + [ANY] * len(after), out_specs=tuple([HBM] * n),
        input_output_aliases={i: i for i in range(n)},
        compiler_params=pltpu.CompilerParams(has_side_effects=pltpu.SideEffectType.DATAFLOW_SIDE_EFFECTING),
    )(*bufs, *sems, *after)
    return list(outs)


def _chips_of(x, y):
    return [(1 - x, y), (x, 1 - y), (1 - x, 1 - y)]


def _landing(ref, dev_index, rows):
    block = ref.at[dev_index]
    return block if rows is None else block.at[pl.ds(rows[0], rows[1])]


def _gather_plan1(n, rows=None):
    def plan(refs, x, y, c):
        out = []
        for a in range(n):
            blk = _landing(refs[a], 4 * x + 2 * y + c, rows)
            out.append((blk, blk, (x, y, 1 - c)))
            out += [(blk, blk, (px, py, c)) for px, py in _chips_of(x, y)]
        return out
    return plan


def _gather_plan2(n, rows=None):
    def plan(refs, x, y, c):
        out = []
        for a in range(n):
            for px, py in _chips_of(x, y):
                blk = _landing(refs[a], 4 * px + 2 * py + c, rows)
                out.append((blk, blk, (x, y, 1 - c)))
        return out
    return plan


def _gather_start(shards, dev, name, deps=(), within=None):
    rows = None if within is None else (within[0], shards[0].shape[0])
    lands = []
    for s in shards:
        shape = (NDEV,) + s.shape if within is None else (NDEV, within[1]) + s.shape[1:]
        start = (dev,) + (0,) * s.ndim if within is None else (dev, within[0]) + (0,) * (s.ndim - 1)
        lands.append(lax.dynamic_update_slice(lax.empty(shape, s.dtype), s[None], start))
    n = len(shards)
    sems, lands, tok = _xfer_start(lands, 4 * n, _gather_plan1(n, rows), name + "_p1_start", deps)
    return dict(sems=sems, lands=lands, tok=tok, n=n, rows=rows)


def _gather_mid(st, after, name):
    n, rows = st["n"], st["rows"]
    lands = _xfer_wait(st["sems"], st["lands"], _gather_plan1(n, rows), after, name + "_p1_wait")
    sems, lands, tok = _xfer_start(lands, 3 * n, _gather_plan2(n, rows), name + "_p2_start")
    return dict(sems=sems, lands=lands, tok=tok, n=n, rows=rows)


def _gather_finish(st, after, name):
    return _xfer_wait(st["sems"], st["lands"], _gather_plan2(st["n"], st["rows"]), after, name + "_p2_wait")


def _scatter_plan1(n):
    def plan(refs, x, y, c):
        return [(refs[a].at[2 * p + 1 - c], refs[n + a].at[p], (x, y, 1 - c)) for a in range(n) for p in range(NCHIP)]
    return plan


def _scatter_plan2(n):
    def plan(refs, x, y, c):
        return [(refs[a].at[2 * px + py], refs[n + a].at[j], (px, py, c))
                for a in range(n) for j, (px, py) in enumerate(_chips_of(x, y))]
    return plan


def _scatter_start(Gs, name):
    n = len(Gs)
    R1s = [lax.empty((NCHIP,) + g.shape[1:], g.dtype) for g in Gs]
    sems, bufs, tok = _xfer_start(list(Gs) + R1s, NCHIP * n, _scatter_plan1(n), name + "_s1_start")
    return dict(sems=sems, bufs=bufs, tok=tok, n=n)


def _scatter_mid(st, after, my_c, name):
    n = st["n"]
    bufs = _xfer_wait(st["sems"], st["bufs"], _scatter_plan1(n), after, name + "_s1_wait")
    Ps = [_pair_sum(bufs[a], bufs[n + a], my_c, f"{name}_pair_sum{a}") for a in range(n)]
    R2s = [lax.empty((3,) + p.shape[1:], p.dtype) for p in Ps]
    sems, bufs, tok = _xfer_start(Ps + R2s, 3 * n, _scatter_plan2(n), name + "_s2_start")
    return dict(sems=sems, bufs=bufs, tok=tok, n=n)


def _scatter_finish(st, after, name):
    n = st["n"]
    bufs = _xfer_wait(st["sems"], st["bufs"], _scatter_plan2(n), after, name + "_s2_wait")
    return bufs[:n], bufs[n:]


SMALL_ROWS = {"norm1_g": (0, 1), "norm2_g": (1, 1), "sgu_ln_g": (2, 1), "sgu_ln_b": (3, 1), "cfm_conv_b": (4, 1),
              "cfm_ln_g": (5, 1), "cfm_ln_b": (6, 1), "b_sgu": (7, 1), "w_sgu": (8, 128), "b_ada": (136, N_MOD),
              "w_short": (142, SHORT_K), "cfm_conv_w": (145, CFM_K)}
ROWS_PER_LAYER = 176
FINAL_ROW = DEPTH * ROWS_PER_LAYER
PACK_ROWS = 360


def _pack(get, D, layers=tuple(range(DEPTH)), tail=True):
    parts = []
    for l in layers:
        for name, (_, nrows) in SMALL_ROWS.items():
            a = get(name, l)
            parts.append(jnp.zeros((nrows * D,), F32) if a is None else a.astype(F32).reshape(nrows * D))
    if tail:
        for name in ("final_g", "loss"):
            a = get(name, None)
            parts.append(jnp.zeros((D,), F32) if a is None else a.astype(F32).reshape(D))
        parts.append(jnp.zeros(((PACK_ROWS - FINAL_ROW - 2) * D,), F32))
    return jnp.concatenate(parts).reshape(-1, D)


def _unpack(pack, name, shape):
    D = pack.shape[1]
    r0, nrows = SMALL_ROWS[name]
    return jnp.stack([pack[l * ROWS_PER_LAYER + r0:l * ROWS_PER_LAYER + r0 + nrows] for l in range(DEPTH)]).reshape(shape)


def _mm_tiles(S):
    return min(512, S), min(1024, S), min(2048, S)


def kernel(x, c, w_ada, b_ada, norm1_g, w_in, w_short, w_a_out, sgu_ln_g, sgu_ln_b, w_sgu, b_sgu, w_b_out, cfm_conv_w, cfm_conv_b, cfm_ln_g, cfm_ln_b, w_c_out, w_o, norm2_g, w_ffn_in, w_ffn_out, final_g, loss_target, m_w_ada, m_b_ada, m_norm1_g, m_w_in, m_w_short, m_w_a_out, m_sgu_ln_g, m_sgu_ln_b, m_w_sgu, m_b_sgu, m_w_b_out, m_cfm_conv_w, m_cfm_conv_b, m_cfm_ln_g, m_cfm_ln_b, m_w_c_out, m_w_o, m_norm2_g, m_w_ffn_in, m_w_ffn_out, m_final_g, v_w_ada, v_b_ada, v_norm1_g, v_w_in, v_w_short, v_w_a_out, v_sgu_ln_g, v_sgu_ln_b, v_w_sgu, v_b_sgu, v_w_b_out, v_cfm_conv_w, v_cfm_conv_b, v_cfm_ln_g, v_cfm_ln_b, v_w_c_out, v_w_o, v_norm2_g, v_w_ffn_in, v_w_ffn_out, v_final_g):
    W = dict(w_ada=w_ada, b_ada=b_ada, norm1_g=norm1_g, w_in=w_in, w_short=w_short, w_a_out=w_a_out, sgu_ln_g=sgu_ln_g,
             sgu_ln_b=sgu_ln_b, w_sgu=w_sgu, b_sgu=b_sgu, w_b_out=w_b_out, cfm_conv_w=cfm_conv_w, cfm_conv_b=cfm_conv_b,
             cfm_ln_g=cfm_ln_g, cfm_ln_b=cfm_ln_b, w_c_out=w_c_out, w_o=w_o, norm2_g=norm2_g, w_ffn_in=w_ffn_in,
             w_ffn_out=w_ffn_out, final_g=final_g)
    Mo = dict(w_ada=m_w_ada, b_ada=m_b_ada, norm1_g=m_norm1_g, w_in=m_w_in, w_short=m_w_short, w_a_out=m_w_a_out,
              sgu_ln_g=m_sgu_ln_g, sgu_ln_b=m_sgu_ln_b, w_sgu=m_w_sgu, b_sgu=m_b_sgu, w_b_out=m_w_b_out,
              cfm_conv_w=m_cfm_conv_w, cfm_conv_b=m_cfm_conv_b, cfm_ln_g=m_cfm_ln_g, cfm_ln_b=m_cfm_ln_b,
              w_c_out=m_w_c_out, w_o=m_w_o, norm2_g=m_norm2_g, w_ffn_in=m_w_ffn_in, w_ffn_out=m_w_ffn_out,
              final_g=m_final_g)
    Vo = dict(w_ada=v_w_ada, b_ada=v_b_ada, norm1_g=v_norm1_g, w_in=v_w_in, w_short=v_w_short, w_a_out=v_w_a_out,
              sgu_ln_g=v_sgu_ln_g, sgu_ln_b=v_sgu_ln_b, w_sgu=v_w_sgu, b_sgu=v_b_sgu, w_b_out=v_w_b_out,
              cfm_conv_w=v_cfm_conv_w, cfm_conv_b=v_cfm_conv_b, cfm_ln_g=v_cfm_ln_g, cfm_ln_b=v_cfm_ln_b,
              w_c_out=v_w_c_out, w_o=v_w_o, norm2_g=v_norm2_g, w_ffn_in=v_w_ffn_in, w_ffn_out=v_w_ffn_out,
              final_g=v_final_g)
    order = ["w_ada", "b_ada", "norm1_g", "w_in", "w_short", "w_a_out", "sgu_ln_g", "sgu_ln_b", "w_sgu", "b_sgu",
             "w_b_out", "cfm_conv_w", "cfm_conv_b", "cfm_ln_g", "cfm_ln_b", "w_c_out", "w_o", "norm2_g", "w_ffn_in",
             "w_ffn_out", "final_g"]

    assert DEPTH == 2, "the weight-gather schedule below is written for two layers"
    S, D = x.shape[1], x.shape[2]
    F2 = w_ffn_in.shape[2] * NDEV
    FF = F2 // 2
    xi, yi, ci = _place()
    dev = 4 * xi + 2 * yi + ci
    my_c = jnp.reshape(ci, (1,)).astype(jnp.int32)
    my_chip = jnp.reshape(2 * xi + yi, (1,)).astype(jnp.int32)
    tm, tm_big, tm_huge = _mm_tiles(S)
    x0 = x.reshape(S, D)
    tgt = loss_target.reshape(S, D)

    def shards_of(l):
        return [w_in[l].astype(BF16), w_a_out[l].astype(BF16), w_b_out[l].astype(BF16), w_c_out[l].astype(BF16),
                w_o[l].astype(BF16), w_ffn_in[l].astype(BF16), w_ffn_out[l].astype(BF16)]

    c_all = _all_gather([jnp.pad(c, ((0, 7), (0, 0)))], "ag_c")[0][:, 0, :]
    modpart, c_act = _ada_fwd(c_all, w_ada, "ada_fwd")
    ncol = modpart.shape[2]
    mg = _all_gather([modpart.reshape(DEPTH * NDEV, ncol)], "ag_mod")[0].reshape(NDEV, DEPTH, NDEV, ncol)
    mine = lax.dynamic_index_in_dim(mg, dev, axis=2, keepdims=False)
    mod = (jnp.transpose(mine, (1, 0, 2)).reshape(DEPTH, N_MOD * D) + b_ada).reshape(DEPTH, N_MOD, D)

    ncs = w_short.shape[2]
    ag_in0 = _gather_start([w_in[0].astype(BF16), w_short.reshape(DEPTH * SHORT_K, ncs),
                            cfm_conv_w.reshape(DEPTH * CFM_K, ncs)], dev, "ag_w_in0", deps=(mod,))
    W, Mo, Vo = lax.optimization_barrier((ag_in0["tok"], (W, Mo, Vo)))[1]
    (norm1_g, norm2_g, w_in, w_a_out, w_b_out, w_c_out, w_o, w_ffn_in, w_ffn_out, sgu_ln_g, sgu_ln_b, w_sgu, b_sgu,
     cfm_conv_b, cfm_ln_g, cfm_ln_b, final_g) = [W[k] for k in (
         "norm1_g", "norm2_g", "w_in", "w_a_out", "w_b_out", "w_c_out", "w_o", "w_ffn_in", "w_ffn_out", "sgu_ln_g",
         "sgu_ln_b", "w_sgu", "b_sgu", "cfm_conv_b", "cfm_ln_g", "cfm_ln_b", "final_g")]
    m_w_ada, v_w_ada = Mo["w_ada"], Vo["w_ada"]
    xl0, h0, ht0 = _norm_fwd(x0, None, _rows(jnp.zeros((D,), F32), norm1_g[0], mod[0, 1], mod[0, 0]), "norm1_fwd0",
                             deps=(ag_in0["tok"],))
    ag_rest0 = _gather_start(shards_of(0)[1:], dev, "ag_rest0", deps=(h0,))

    tril = jnp.tril(jnp.ones((CHUNK, CHUNK), dtype=bool))

    def layer_consts(l):
        wt = jnp.where(tril[None], w_sgu[l], 0.0).astype(BF16)
        return dict(sgu_ln=_rows(sgu_ln_g[l], sgu_ln_b[l]), wtril=wt, wtril_t=jnp.swapaxes(wt, 1, 2),
                    bias_full=jnp.repeat(b_sgu[l].T, LANE, axis=1), cvec=_rows(cfm_conv_b[l], cfm_ln_g[l], cfm_ln_b[l]))

    def rest_of(g):
        return dict(w_a=g[0].reshape(1, D, D), w_b=g[1].reshape(1, D, D), w_c=g[2].reshape(1, D, D),
                    w_o=g[3].reshape(1, D, D), w_fi=jnp.transpose(g[4], (1, 0, 2)).reshape(1, D, F2),
                    w_fo=g[5].reshape(1, FF, D))

    sharded_small = ("w_short", "cfm_conv_w")

    def param_get(T):
        def get(name, l):
            if name == "final_g":
                return T[name]
            return None if name in sharded_small or name == "loss" else T[name][l]
        return get

    packs = [_pack(param_get(T), D) for T in (W, Mo, Vo)]
    ag_in0 = _gather_mid(ag_in0, [ag_rest0["tok"], *packs], "ag_w_in0")
    (w_sgu, b_sgu, sgu_ln_g, sgu_ln_b, cfm_conv_b, cfm_ln_g, cfm_ln_b), conv_wmv_in = lax.optimization_barrier(
        (ag_in0["tok"], ((w_sgu, b_sgu, sgu_ln_g, sgu_ln_b, cfm_conv_b, cfm_ln_g, cfm_ln_b),
                         [(T["w_short"], T["cfm_conv_w"]) for T in (W, Mo, Vo)])))[1]
    consts = [layer_consts(l) for l in range(DEPTH)]
    ncr = DEPTH * (SHORT_K + CFM_K)
    padr = (-ncr) % 8
    convw_wmv = [jnp.pad(jnp.concatenate([a.reshape(-1, ncs), b.reshape(-1, ncs)]), ((0, padr), (0, 0)))
                 for a, b in conv_wmv_in]
    g_in0 = _gather_finish(ag_in0, [*convw_wmv] + [a for cl in consts for a in cl.values()], "ag_w_in0")
    w_short_full = jnp.transpose(g_in0[1], (1, 0, 2)).reshape(DEPTH, SHORT_K, D)
    cfm_w_full = jnp.transpose(g_in0[2], (1, 0, 2)).reshape(DEPTH, CFM_K, D)
    for l in range(DEPTH):
        consts[l]["wsh"] = jnp.pad(w_short_full[l], ((0, 8 - SHORT_K), (0, 0)))
        consts[l]["cw"] = jnp.pad(cfm_w_full[l], ((0, HALO - CFM_K), (0, 0)))
    Wg = [dict(w_in=g_in0[0]), None]
    ag_l1 = None
    nin = w_in.shape[2]
    tn_in = nin if nin % 256 == 0 and nin <= 1280 else 256
    tn_fi = 512 if F2 % 512 == 0 else 256
    tn_dw = min(256, D)

    saved = []
    xcur, gprev, ffn_tail = x0, None, None
    for l in range(DEPTH):
        sh1, sc1, g1, sh2, sc2, g2 = [mod[l, k] for k in range(N_MOD)]
        cl = consts[l]
        if l == 0:
            xl, h, ht = xl0, h0, ht0
        else:
            vec1 = _rows(gprev, norm1_g[l], sc1, sh1)
            act_prev, w_fo_prev = ffn_tail
            ag_l1 = _gather_mid(ag_l1, act_prev, f"ag_w{l}")
            f_prev, xl, h, ht = _mm_resid_norm(act_prev, w_fo_prev, xcur, vec1, tm, f"mm_ffn_out_norm1_{l}",
                                               deps=(ag_l1["tok"],))
            saved[l - 1]["f"] = f_prev
            g = _gather_finish(ag_l1, h, f"ag_w{l}")
            Wg[l] = dict(w_in=g[0], **rest_of(g[1:]))
        wl = Wg[l]
        z = _mm_nn(h, wl["w_in"], BF16, tm_huge, tn_in, D, f"mm_in{l}", w_outer=True)
        mix_deps = ()
        if l == 0:
            ag_rest0 = _gather_mid(ag_rest0, z, "ag_rest0")
            mix_deps = (ag_rest0["tok"],)
            if DEPTH > 1:
                ag_l1 = _gather_start(shards_of(1), dev, "ag_w1")
                mix_deps += (ag_l1["tok"],)
        acts, acts_t, conv = _mixer_fwd(z, cl["wsh"], cl["sgu_ln"], cl["wtril"], cl["bias_full"], cl["cw"], cl["cvec"],
                                        f"mixer_fwd{l}", deps=mix_deps)
        if l == 0:
            wl.update(rest_of(_gather_finish(ag_rest0, acts[0], "ag_rest0")))
        merged, merged_t, ys = _branch_out(acts, [wl["w_a"][0], wl["w_b"][0], wl["w_c"][0]], z, f"branch_out{l}")
        o, x1, h2, h2t = _mm_resid_norm(merged, wl["w_o"], xl, _rows(g1, norm2_g[l], sc2, sh2), tm, f"mm_o_norm2_{l}")
        gu, act, act_t = _ffn_in_swiglu(h2, wl["w_fi"], tm_huge, 256, f"mm_ffn_in{l}")
        f = _mm_nn(act, wl["w_fo"], F32, tm_big, D, FF, f"mm_ffn_out{l}") if l == DEPTH - 1 else None
        saved.append(dict(xl=xl, ht=ht, z=z, acts_t=acts_t, conv=conv, ys=ys, merged_t=merged_t, o=o, x1=x1, h2t=h2t, gu=gu,
                          act_t=act_t, f=f, consts=cl, mod=(sh1, sc1, g1, sh2, sc2, g2)))
        xcur, gprev, ffn_tail = x1, g2, (act, wl["w_fo"])

    last = saved[-1]
    dxup, dfb, fsums, loss_blk = _final_bwd(last["x1"], last["f"], tgt, _rows(last["mod"][5], final_g), "final_bwd")
    loss_row = jnp.pad(loss_blk[0, 0:1], (0, D - 1))
    dgate2_next = fsums[1]
    small = [dict() for _ in range(DEPTH)]
    dmods = [None] * DEPTH
    nfi = w_ffn_in.shape[2]
    early_names, late_names = ["w_ffn_out", "w_ffn_in", "w_o"], ["w_a_out", "w_b_out", "w_c_out", "w_in"]
    results = {n: None for n in early_names + late_names}

    def adam_group(names, Ps, R2s, l, deps=()):
        for n, p, r2 in zip(names, Ps, R2s):
            results[n] = _adam_big(p, r2, my_chip, W[n], Mo[n], Vo[n], l, results[n], f"adam_{n}{l}", deps)

    deferred = []
    late_prev = None
    ag_s1, gathered1 = None, None
    tk_w = min(2048, S)
    tn_dw_in = tn_in // 2 if tn_in == 1280 else tn_in
    for l in reversed(range(DEPTH)):
        sv, wl, cl = saved[l], Wg[l], saved[l]["consts"]
        sh1, sc1, g1, sh2, sc2, g2 = sv["mod"]
        dact = _mm_nt(dfb, wl["w_fo"], BF16, tm_big, FF, D, f"mm_dact{l}",
                      deps=() if late_prev is None else (late_prev["tok"], ag_s1["tok"]))
        g_fo = _mm_wgrad(sv["act_t"], dfb, 1, FF // 2, D, tk_w, f"mm_dw_ffn_out{l}")
        dgu = _swiglu_bwd(dact, sv["gu"], f"swiglu_bwd{l}")
        dh2 = _mm_nt(dgu, wl["w_fi"], F32, tm, D, F2, f"mm_dh2{l}")
        if late_prev is not None:
            deferred.append((late_names, *_scatter_finish(late_prev, dh2, f"rs_late{l + 1}"), l + 1))
            late_prev = None
        g_fi = _mm_wgrad(sv["h2t"], dgu, 1, D, tn_fi, S, f"mm_dw_ffn_in{l}")
        if ag_s1 is not None:
            ag_s1 = _gather_mid(ag_s1, g_fi, "ag_small1")
        dx1, dob, s2 = _norm_bwd(sv["x1"], dh2, dxup, _rows(norm2_g[l], sc2, g1), sv["o"], f"norm2_bwd{l}",
                                 deps=() if ag_s1 is None else (ag_s1["tok"],))
        dmerged = _mm_nt(dob, wl["w_o"], BF16, tm_big, D, D, f"mm_dmerged{l}")
        g_o = _mm_wgrad(sv["merged_t"], dob, 1, D, tn_dw, S, f"mm_dw_o{l}")
        early = _scatter_start([g_fo.reshape(NDEV, FF // NDEV, D),
                                jnp.transpose(g_fi.reshape(D, NDEV, nfi), (1, 0, 2)),
                                g_o.reshape(NDEV, D // NDEV, D)], f"rs_early{l}")
        dys, dz = _gate_bwd(dmerged, sv["z"], sv["ys"], f"gate_bwd{l}", deps=(early["tok"],))
        if ag_s1 is not None:
            gathered1 = _gather_finish(ag_s1, dys, "ag_small1")[0]
            ag_s1 = None
        early = _scatter_mid(early, dys, my_c, f"rs_early{l}")
        dacts = _mm3_nt(dys, [wl["w_a"], wl["w_b"], wl["w_c"]], tm_big, f"mm_dact_abc{l}", deps=(early["tok"],))
        g3 = _mm3_wgrad(sv["acts_t"], dys, tn_dw, f"mm_dw_abc{l}")
        g_abc = [g3[n] for n in range(3)]
        dz, mvec, dcw, dws, dbs = _mixer_bwd(sv["z"], dacts, sv["conv"], dz, cl["wsh"], cl["sgu_ln"], cl["wtril"],
                                             cl["wtril_t"], cl["bias_full"], cl["cw"], cl["cvec"], f"mixer_bwd{l}")
        dh = _mm_nt(dz, wl["w_in"], F32, tm_big, D, tn_in, f"mm_dh{l}",
                    blocks_per_step=2 if (tn_in == nin and wl["w_in"].shape[0] % 2 == 0) else 1)
        g_in = _mm_wgrad(sv["ht"], dz, NDEV, D, tn_dw_in, S, f"mm_dw_in{l}")
        late = _scatter_start([g.reshape(NDEV, D // NDEV, D) for g in g_abc] + [g_in], f"rs_late{l}")
        if l > 0:
            pv = saved[l - 1]
            dxup, dfb, s1 = _norm_bwd(sv["xl"], dh, dx1, _rows(norm1_g[l], sc1, pv["mod"][5]), pv["f"], f"norm1_bwd{l}",
                                      deps=(late["tok"],))
        else:
            dxup, dfb, s1 = _norm_bwd(sv["xl"], dh, dx1, _rows(norm1_g[l], sc1), None, f"norm1_bwd{l}", deps=(late["tok"],))
        deferred.append((early_names, *_scatter_finish(early, dxup, f"rs_early{l}"), l))
        dmods[l] = jnp.stack([s1[0], s1[1], s2[3], s2[0], s2[1], dgate2_next])
        dgate2_next = s1[3]
        small[l] = dict(norm1_g=s1[2], norm2_g=s2[2], sgu_ln_g=mvec[3], sgu_ln_b=mvec[4], cfm_conv_b=mvec[5],
                        cfm_ln_g=mvec[6], cfm_ln_b=mvec[7], b_sgu=dbs[:, :, 0],
                        w_sgu=jnp.where(tril[None], dws, 0.0), b_ada=dmods[l], w_short=mvec[0:SHORT_K],
                        cfm_conv_w=dcw[0:CFM_K])
        small_get = lambda name, k: {"final_g": fsums[0], "loss": loss_row}.get(name) if k is None else small[k][name]
        if l > 0:
            late_prev = _scatter_mid(late, dxup, my_c, f"rs_late{l}")
            ag_s1 = _gather_start([_pack(small_get, D, layers=(l,), tail=True)], dev, "ag_small1", deps=(late_prev["tok"],),
                                  within=(l * ROWS_PER_LAYER, PACK_ROWS))
    grad_x = dxup.reshape(x.shape)

    gathered = _all_gather([_pack(small_get, D, layers=(0,), tail=False)], "ag_small0", deps=(dxup,),
                           into=[(gathered1, 0)])[0]
    late_prev = _scatter_mid(late, gathered, my_c, "rs_late0")
    one_row = [n for n in order if n in SMALL_ROWS and SMALL_ROWS[n][1] == 1 and W[n].ndim == 2]
    (sg, sd, sm, sv_), singles = _adam_small(
        gathered, *packs, name="adam_small", deps=(late_prev["tok"],),
        single_rows=[tuple(l * ROWS_PER_LAYER + SMALL_ROWS[n][0] for l in range(DEPTH)) for n in one_row])
    loss = sg[FINAL_ROW + 1, 0]
    out = {n: tuple(singles[4 * i:4 * i + 4]) for i, n in enumerate(one_row)}
    for name in order:
        if name in SMALL_ROWS and name not in sharded_small and name not in out:
            out[name] = tuple(_unpack(p, name, W[name].shape) for p in (sg, sd, sm, sv_))
    out["final_g"] = tuple(p[FINAL_ROW] for p in (sg, sd, sm, sv_))

    def my_cols(name):
        full = _unpack(sg, name, (DEPTH, SMALL_ROWS[name][1], D))
        return lax.dynamic_slice_in_dim(full, dev * ncs, ncs, axis=2)

    gcs = jnp.concatenate([my_cols("w_short").reshape(-1, ncs), my_cols("cfm_conv_w").reshape(-1, ncs)])
    cd, cm, cv = _adam_plain(jnp.pad(gcs, ((0, padr), (0, 0))), *convw_wmv, "adam_convw")
    nsh = DEPTH * SHORT_K
    out["w_short"] = tuple(a[0:nsh].reshape(w_short.shape) for a in (gcs, cd, cm, cv))
    out["cfm_conv_w"] = tuple(a[nsh:ncr].reshape(cfm_conv_w.shape) for a in (gcs, cd, cm, cv))

    dm_all = jnp.stack([gathered[:, l * ROWS_PER_LAYER + 136:l * ROWS_PER_LAYER + 136 + N_MOD, :].reshape(NDEV, N_MOD * D)
                        for l in range(DEPTH)])
    dm_mine = lax.dynamic_slice_in_dim(dm_all, dev * ncol, ncol, axis=2)
    out["w_ada"] = tuple(_adam_ada(jnp.transpose(c_act), dm_mine, w_ada, m_w_ada, v_w_ada, "adam_ada"))

    for names, Ps, R2s, l in deferred:
        adam_group(names, Ps, R2s, l, deps=(late_prev["tok"],))
    adam_group(late_names, *_scatter_finish(late_prev, results["w_o"][0], "rs_late0"), 0)
    for n in early_names + late_names:
        out[n] = tuple(results[n])

    grads = [out[n][0] for n in order]
    deltas = [out[n][1] for n in order]
    new_m = [out[n][2] for n in order]
    new_v = [out[n][3] for n in order]
    return (loss, grad_x, *grads, *deltas, *new_m, *new_v)
```

```python
import functools
import math

import jax
import jax.numpy as jnp
from jax import lax
from jax.experimental import pallas as pl
from jax.experimental.pallas import tpu as pltpu

F32, BF16 = jnp.float32, jnp.bfloat16
NDEV = 8
NCHIP = NDEV // 2
DEPTH = 2
EPS = 1e-6
CHUNK = 128
NG = 8
SHORT_K = 3
CFM_K = 31
HALO = 32
N_MOD = 6
LANE = 128
VMEM_LIMIT = 56 * 1024 * 1024
ADAM_LR, ADAM_B1, ADAM_B2, ADAM_EPS, ADAM_WD, ADAM_STEP = 0.001, 0.9, 0.999, 1e-08, 0.01, 10
_G0 = math.sqrt(2.0 / math.pi)
_G1 = 0.044715
MESH = pl.DeviceIdType.MESH
ANY = pl.BlockSpec(memory_space=pl.ANY)


def _pcall(body, **kw):
    return pl.pallas_call(body, **kw)


def _params(sem=None):
    return pltpu.CompilerParams(dimension_semantics=sem, vmem_limit_bytes=VMEM_LIMIT)


def _sds(shape, dtype):
    return jax.ShapeDtypeStruct(tuple(shape), dtype)


def _mm_body(dims, nk, out_f32, blocks=1):
    def body(a_ref, b_ref, o_ref, *scr):
        k = pl.program_id(2)
        if blocks == 1:
            part = lax.dot_general(a_ref[...], b_ref[...], dims, preferred_element_type=F32)
        else:
            w = a_ref.shape[1] // blocks
            part = None
            for g in range(blocks):
                t = lax.dot_general(a_ref[:, g * w:(g + 1) * w], b_ref[g], dims, preferred_element_type=F32)
                part = t if part is None else part + t
        if nk == 1:
            o_ref[...] = part.reshape(o_ref.shape).astype(o_ref.dtype)
        elif out_f32:
            @pl.when(k == 0)
            def _():
                o_ref[...] = part.reshape(o_ref.shape)

            @pl.when(k > 0)
            def _():
                o_ref[...] += part.reshape(o_ref.shape)
        else:
            acc = scr[0]

            @pl.when(k == 0)
            def _():
                acc[...] = part

            @pl.when(k > 0)
            def _():
                acc[...] += part

            @pl.when(k == nk - 1)
            def _():
                o_ref[...] = acc[...].astype(o_ref.dtype)
    return body


def _after(body, n_in, deps):
    nd = len(deps)
    if nd == 0:
        return body

    def ordered(*refs):
        return body(*refs[:n_in], *refs[n_in + nd:])
    return ordered


def _mm_call(body, grid, in_specs, out_spec, out_shape, acc_shape, name, deps=()):
    scratch = [] if acc_shape is None else [pltpu.VMEM(acc_shape, F32)]
    return _pcall(_after(body, 2, deps), grid=grid, in_specs=in_specs + [ANY] * len(deps), out_specs=out_spec,
                  out_shape=out_shape, scratch_shapes=scratch, name=name,
                  compiler_params=_params(("parallel", "parallel", "arbitrary")))


def _mm_nn(a, b3, out_dtype, tm, tn, tk, name, w_outer=False, deps=()):
    M, K = a.shape
    G, _, Nb = b3.shape
    npb, nk = Nb // tn, K // tk
    out_f32 = out_dtype == F32
    body = _mm_body((((1,), (0,)), ((), ())), nk, out_f32)
    if w_outer:
        grid = (G * npb, M // tm, nk)
        ij = lambda p, q: (q, p)
    else:
        grid = (M // tm, G * npb, nk)
        ij = lambda p, q: (p, q)

    def a_map(p, q, k):
        i, j = ij(p, q)
        return (i, k)

    def b_map(p, q, k):
        i, j = ij(p, q)
        return (j // npb, k, j % npb)

    def o_map(p, q, k):
        return ij(p, q)

    def wrapped(a_ref, b_ref, o_ref, *scr):
        body(a_ref, b_ref, o_ref, *scr)

    return _mm_call(wrapped, grid, [pl.BlockSpec((tm, tk), a_map), pl.BlockSpec((None, tk, tn), b_map)],
                    pl.BlockSpec((tm, tn), o_map), _sds((M, G * Nb), out_dtype),
                    None if (nk == 1 or out_f32) else (tm, tn), name, deps)(a, b3, *deps)


def _mm_nt(a, b3, out_dtype, tm, tn, tk, name, deps=(), blocks_per_step=1):
    M, _ = a.shape
    G, Ko, Nb = b3.shape
    kpb = Nb // tk
    nk = G * kpb // blocks_per_step
    out_f32 = out_dtype == F32
    body = _mm_body((((1,), (1,)), ((), ())), nk, out_f32, blocks_per_step)

    def wrapped(a_ref, b_ref, o_ref, *scr):
        body(a_ref, b_ref, o_ref, *scr)

    if blocks_per_step > 1:
        assert tk == Nb and G % blocks_per_step == 0
        b_spec = pl.BlockSpec((blocks_per_step, tn, tk), lambda i, j, k: (k, j, 0))
    else:
        b_spec = pl.BlockSpec((None, tn, tk), lambda i, j, k: (k // kpb, j, k % kpb))
    return _mm_call(wrapped, (M // tm, Ko // tn, nk),
                    [pl.BlockSpec((tm, tk * blocks_per_step), lambda i, j, k: (i, k)), b_spec],
                    pl.BlockSpec((tm, tn), lambda i, j, k: (i, j)), _sds((M, Ko), out_dtype),
                    None if (nk == 1 or out_f32) else (tm, tn), name, deps)(a, b3, *deps)


def _mm_wgrad(at, b, G, tm, tn, tk, name, deps=()):
    M, T = at.shape
    Nb = b.shape[1] // G
    npb, nk = Nb // tn, T // tk
    body = _mm_body((((1,), (0,)), ((), ())), nk, False)

    def wrapped(a_ref, b_ref, o_ref, *scr):
        body(a_ref, b_ref, o_ref, *scr)

    a = at
    in_specs = [pl.BlockSpec((tm, tk), lambda i, j, k: (i, k)), pl.BlockSpec((tk, tn), lambda i, j, k: (k, j))]
    out_spec = pl.BlockSpec((None, tm, tn), lambda i, j, k: (j // npb, i, j % npb))
    return _mm_call(wrapped, (M // tm, G * npb, nk), in_specs, out_spec, _sds((G, M, Nb), BF16),
                    None if nk == 1 else (tm, tn), name, deps)(a, b, *deps)


def _mm3_nt(x3, ws, tm, name, deps=()):
    nb, S, K = x3.shape
    Ko = ws[0].shape[1]

    def body(x_ref, w0, w1, w2, o_ref):
        n = pl.program_id(0)
        for k, w in enumerate((w0, w1, w2)):
            @pl.when(n == k)
            def _(w=w):
                o_ref[...] = lax.dot_general(x_ref[...], w[...], (((1,), (1,)), ((), ())),
                                             preferred_element_type=F32).astype(BF16)

    wspec = pl.BlockSpec((None, Ko, K), lambda n, i: (0, 0, 0))
    return _pcall(_after(body, 4, deps), grid=(nb, S // tm),
                  in_specs=[pl.BlockSpec((None, tm, K), lambda n, i: (n, i, 0)), wspec, wspec, wspec] + [ANY] * len(deps),
                  out_specs=pl.BlockSpec((None, tm, Ko), lambda n, i: (n, i, 0)), out_shape=_sds((nb, S, Ko), BF16),
                  name=name, compiler_params=_params(("arbitrary", "parallel")))(x3, *ws, *deps)


def _mm3_wgrad(at3, b3, tn, name):
    nb, M, T = at3.shape
    N = b3.shape[2]

    def body(a_ref, b_ref, o_ref):
        o_ref[...] = jnp.dot(a_ref[...], b_ref[...], preferred_element_type=F32).astype(BF16)

    return _pcall(body, grid=(nb, N // tn),
                  in_specs=[pl.BlockSpec((None, M, T), lambda n, j: (n, 0, 0)), pl.BlockSpec((None, T, tn), lambda n, j: (n, 0, j))],
                  out_specs=pl.BlockSpec((None, M, tn), lambda n, j: (n, 0, j)), out_shape=_sds((nb, M, N), BF16),
                  name=name, compiler_params=_params(("arbitrary", "parallel")))(at3, b3)


def _rsum(v):
    return jnp.sum(v, axis=0, keepdims=True)


def _rmean(v):
    return jnp.mean(v, axis=-1, keepdims=True)


def _gelu(x):
    t = jnp.tanh(_G0 * (x + _G1 * (x * x * x)))
    return x * (0.5 * (1.0 + t)), t


def _dgelu(x, t):
    return 0.5 * (1.0 + t) + 0.5 * x * (1.0 - t * t) * (_G0 * (1.0 + 3.0 * _G1 * (x * x)))


def _sigmoid(x):
    return 0.5 * jnp.tanh(0.5 * x) + 0.5


def _fill_shifted(ext, rot):
    v = ext[...]
    n = v.shape[0]
    for b in range(1, 8):
        rot[b - 1] = pltpu.roll(v, n - b, 0)


def _rows_at(ext, rot, s, tm, cs=slice(None)):
    a, b = divmod(s, 8)
    return ext[8 * a:8 * a + tm, cs] if b == 0 else rot[b - 1, 8 * a:8 * a + tm, cs]


def _causal_conv(w_ref, taps, bias, ext, rot, offset, tm, out):
    D = out.shape[1]
    for cb in range(D // LANE):
        cs = slice(cb * LANE, (cb + 1) * LANE)
        acc = None
        for k, o in zip(taps, offset):
            term = w_ref[k:k + 1, cs] * _rows_at(ext, rot, o, tm, cs)
            acc = term if acc is None else acc + term
        out[:, cs] = acc if bias is None else acc + bias[:, cs]


def _rows(*vs):
    a = jnp.stack([v.astype(F32) for v in vs])
    return jnp.pad(a, ((0, 8 - len(vs)), (0, 0)))


def _row_spec(tm, D):
    return pl.BlockSpec((tm, D), lambda i: (i, 0))


def _const_spec(shape):
    nd = len(shape)
    return pl.BlockSpec(shape, lambda i: (0,) * nd)


def _norm_fwd(xp, f, vec, name, deps=()):
    S, D = xp.shape
    tm = min(512, S)
    has_f = f is not None

    def body(*refs):
        if has_f:
            xp_ref, f_ref, vec_ref, xo_ref, h_ref, ht_ref = refs
            x = xp_ref[...] + vec_ref[0:1, :] * f_ref[...]
            xo_ref[...] = x
        else:
            xp_ref, vec_ref, h_ref, ht_ref = refs
            x = xp_ref[...]
        r = lax.rsqrt(_rmean(x * x) + EPS)
        h = (x * r) * vec_ref[1:2, :]
        h = h * (1.0 + vec_ref[2:3, :]) + vec_ref[3:4, :]
        h_ref[...] = h.astype(BF16)
        ht_ref[...] = h.T.astype(BF16)

    rs = _row_spec(tm, D)
    ins = [xp, f, vec] if has_f else [xp, vec]
    in_specs = ([rs, rs] if has_f else [rs]) + [_const_spec((8, D))]
    out_shape = ([_sds((S, D), F32)] if has_f else []) + [_sds((S, D), BF16), _sds((D, S), BF16)]
    out_specs = [rs] * (len(out_shape) - 1) + [pl.BlockSpec((D, tm), lambda i: (0, i))]
    outs = _pcall(_after(body, len(ins), deps), grid=(S // tm,), in_specs=in_specs + [ANY] * len(deps),
                  out_specs=out_specs, out_shape=out_shape, name=name,
                  compiler_params=_params(("parallel",)))(*ins, *deps)
    return (outs[0], outs[1], outs[2]) if has_f else (xp, outs[0], outs[1])


def _mm_resid_norm(a, w3, xprev, vec, tm, name, deps=()):
    S, K = a.shape
    D = w3.shape[2]

    def body(a_ref, w_ref, xp_ref, vec_ref, p_ref, xo_ref, h_ref, ht_ref):
        p = jnp.dot(a_ref[...], w_ref[...], preferred_element_type=F32)
        p_ref[...] = p
        x = xp_ref[...] + vec_ref[0:1, :] * p
        xo_ref[...] = x
        r = lax.rsqrt(_rmean(x * x) + EPS)
        h = (x * r) * vec_ref[1:2, :]
        h = h * (1.0 + vec_ref[2:3, :]) + vec_ref[3:4, :]
        h_ref[...] = h.astype(BF16)
        ht_ref[...] = h.T.astype(BF16)

    rs = _row_spec(tm, D)
    return _pcall(_after(body, 4, deps), grid=(S // tm,),
                  in_specs=[_row_spec(tm, K), pl.BlockSpec((None, K, D), lambda i: (0, 0, 0)), rs, _const_spec((8, D))]
                  + [ANY] * len(deps),
                  out_specs=[rs, rs, rs, pl.BlockSpec((D, tm), lambda i: (0, i))],
                  out_shape=[_sds((S, D), F32), _sds((S, D), F32), _sds((S, D), BF16), _sds((D, S), BF16)], name=name,
                  compiler_params=_params(("parallel",)))(a, w3, xprev, vec, *deps)


def _mixer_fwd(z, wsh, sgu_ln, wtril, bias_full, cw, cvec, name, deps=()):
    S = z.shape[0]
    D = wsh.shape[1]
    tm = CHUNK

    def body(z_ref, wsh_ref, sln_ref, wt_ref, bias_ref, cw_ref, cv_ref, oa_ref, ob_ref, oc_ref, t_ref,
             conv_ref, pe, ge, gr, cbuf):
        i = pl.program_id(0)

        @pl.when(i == 0)
        def _():
            pe[0:HALO, :] = jnp.zeros((HALO, D), F32)
            ge[0:HALO, :] = jnp.zeros((HALO, D), F32)

        def col(n):
            return z_ref[:, n * D:(n + 1) * D].astype(F32)

        pe[HALO:HALO + tm, :] = col(1) * col(2)
        q = wsh_ref[0:1, :] * pe[HALO - 2:HALO - 2 + tm, :]
        q = q + wsh_ref[1:2, :] * pe[HALO - 1:HALO - 1 + tm, :]
        q = q + wsh_ref[2:3, :] * pe[HALO:HALO + tm, :]
        act_a = col(0) * q
        oa_ref[...] = act_a.astype(BF16)
        t_ref[0] = act_a.T.astype(BF16)
        gu, _ = _gelu(col(3))
        gv, _ = _gelu(col(4))
        d = gv - _rmean(gv)
        nrm = d * lax.rsqrt(_rmean(d * d) + EPS)
        vnb = (nrm * sln_ref[0:1, :] + sln_ref[1:2, :]).astype(BF16)
        for g in range(NG):
            cs = slice(g * LANE, (g + 1) * LANE)
            mixed = jnp.dot(wt_ref[g], vnb[:, cs], preferred_element_type=F32) + bias_ref[:, cs]
            act_b = gu[:, cs] * mixed
            ob_ref[:, cs] = act_b.astype(BF16)
            t_ref[1, cs, :] = act_b.T.astype(BF16)
        ge[HALO:HALO + tm, :] = col(5) * _sigmoid(col(6))
        _fill_shifted(ge, gr)
        o0 = HALO - (CFM_K - 1)
        _causal_conv(cw_ref, range(CFM_K), cv_ref[0:1, :], ge, gr, range(o0, o0 + CFM_K), tm, cbuf)
        conv = cbuf[...]
        conv_ref[...] = conv.astype(BF16)
        d = conv - _rmean(conv)
        ln = (d * lax.rsqrt(_rmean(d * d) + EPS)) * cv_ref[1:2, :] + cv_ref[2:3, :]
        act_c = ln * _sigmoid(ln)
        oc_ref[...] = act_c.astype(BF16)
        t_ref[2] = act_c.T.astype(BF16)
        pe[0:HALO, :] = pe[tm:tm + HALO, :]
        ge[0:HALO, :] = ge[tm:tm + HALO, :]

    rs = _row_spec(tm, D)
    outs = _pcall(
        _after(body, 7, deps), grid=(S // tm,),
        in_specs=[pl.BlockSpec((tm, 7 * D), lambda i: (i, 0)), _const_spec((8, D)), _const_spec((8, D)),
                  _const_spec((NG, CHUNK, CHUNK)), _const_spec((CHUNK, D)), _const_spec((HALO, D)), _const_spec((8, D))]
        + [ANY] * len(deps),
        out_specs=[rs, rs, rs, pl.BlockSpec((3, D, tm), lambda i: (0, 0, i)), rs],
        out_shape=[_sds((S, D), BF16)] * 3 + [_sds((3, D, S), BF16), _sds((S, D), BF16)],
        scratch_shapes=[pltpu.VMEM((HALO + tm, D), F32), pltpu.VMEM((HALO + tm, D), F32),
                        pltpu.VMEM((7, HALO + tm, D), F32), pltpu.VMEM((tm, D), F32)],
        name=name, compiler_params=_params(("arbitrary",)))(z, wsh, sgu_ln, wtril, bias_full, cw, cvec, *deps)
    return outs[:3], outs[3], outs[4]


def _branch_out(acts, ws, z, name):
    S, D = acts[0].shape
    tm = min(512, S)

    def body(a0, a1, a2, w0, w1, w2, g0, g1, g2, m_ref, mt_ref, y_ref):
        m = None
        for n, (a, w, g) in enumerate(((a0, w0, g0), (a1, w1, g1), (a2, w2, g2))):
            y = jnp.dot(a[...], w[...], preferred_element_type=F32)
            y_ref[n] = y.astype(BF16)
            t = _sigmoid(g[...].astype(F32)) * y
            m = t if m is None else m + t
        m_ref[...] = m.astype(BF16)
        mt_ref[...] = m.T.astype(BF16)

    rs = _row_spec(tm, D)
    gate_specs = [pl.BlockSpec((tm, D), functools.partial(lambda i, n: (i, 7 + n), n=n)) for n in range(3)]
    return _pcall(body, grid=(S // tm,),
                  in_specs=[rs, rs, rs] + [_const_spec((D, D))] * 3 + gate_specs,
                  out_specs=[rs, pl.BlockSpec((D, tm), lambda i: (0, i)), pl.BlockSpec((3, tm, D), lambda i: (0, i, 0))],
                  out_shape=[_sds((S, D), BF16), _sds((D, S), BF16), _sds((3, S, D), BF16)], name=name,
                  compiler_params=_params(("parallel",)))(*acts, *ws, z, z, z)


def _ffn_in_swiglu(h2, w3, tm, tn, name):
    S, D = h2.shape
    F = w3.shape[2] // 2
    nj = F // tn

    def body(a_ref, wg_ref, wu_ref, gu_ref, act_ref, actt_ref):
        a = a_ref[...]
        g = jnp.dot(a, wg_ref[...], preferred_element_type=F32)
        u = jnp.dot(a, wu_ref[...], preferred_element_type=F32)
        gu_ref[0] = g.astype(BF16)
        gu_ref[1] = u.astype(BF16)
        act = (g * _sigmoid(g)) * u
        act_ref[...] = act.astype(BF16)
        actt_ref[...] = act.T.astype(BF16)

    return _pcall(body, grid=(S // tm, nj),
                  in_specs=[pl.BlockSpec((tm, D), lambda i, j: (i, 0)), pl.BlockSpec((None, D, tn), lambda i, j: (0, 0, j)),
                            pl.BlockSpec((None, D, tn), lambda i, j: (0, 0, j + nj))],
                  out_specs=[pl.BlockSpec((2, tm, tn), lambda i, j: (0, i, j)), pl.BlockSpec((tm, tn), lambda i, j: (i, j)),
                             pl.BlockSpec((tn, tm), lambda i, j: (j, i))],
                  out_shape=[_sds((2, S, F), BF16), _sds((S, F), BF16), _sds((F, S), BF16)], name=name,
                  compiler_params=_params(("parallel", "parallel")))(h2, w3, w3)


def _swiglu_bwd(dact, gu, name):
    _, S, F = gu.shape
    F2 = 2 * F
    tm = min(256, S)

    def body(d_ref, g_ref, u_ref, o_ref):
        g = g_ref[...].astype(F32)
        sg = _sigmoid(g)
        d = d_ref[...].astype(F32)
        o_ref[:, 0:F] = (d * u_ref[...].astype(F32) * (sg * (1.0 + g * (1.0 - sg)))).astype(BF16)
        o_ref[:, F:2 * F] = (d * (g * sg)).astype(BF16)

    return _pcall(body, grid=(S // tm,),
                  in_specs=[pl.BlockSpec((tm, F), lambda i: (i, 0)), pl.BlockSpec((None, tm, F), lambda i: (0, i, 0)),
                            pl.BlockSpec((None, tm, F), lambda i: (1, i, 0))],
                  out_specs=pl.BlockSpec((tm, F2), lambda i: (i, 0)), out_shape=_sds((S, F2), BF16), name=name,
                  compiler_params=_params(("parallel",)))(dact, gu, gu)


def _final_bwd(x1, act, w3, tgt, vec, name):
    S, D = x1.shape
    K = act.shape[1]
    tm = min(512, S)

    def body(x_ref, a_ref, w_ref, t_ref, vec_ref, dx_ref, df_ref, sums_ref, loss_ref):
        @pl.when(pl.program_id(0) == 0)
        def _():
            sums_ref[...] = jnp.zeros_like(sums_ref)
            loss_ref[...] = jnp.zeros_like(loss_ref)

        gate, fg = vec_ref[0:1, :], vec_ref[1:2, :]
        fv = jnp.dot(a_ref[...], w_ref[...], preferred_element_type=F32)
        x = x_ref[...] + gate * fv
        r = lax.rsqrt(_rmean(x * x) + EPS)
        xn = x * r
        diff = xn * fg - t_ref[...]
        per_tok = _rmean(diff * diff)
        loss_ref[...] += 0.5 * jnp.sum(per_tok, axis=0, keepdims=True)
        dy = diff * (1.0 / D)
        sums_ref[0:1, :] += _rsum(dy * xn)
        dxn = dy * fg
        dx = r * (dxn - xn * _rmean(dxn * xn))
        sums_ref[1:2, :] += _rsum(dx * fv)
        dx_ref[...] = dx
        df_ref[...] = (dx * gate).astype(BF16)

    rs = _row_spec(tm, D)
    return _pcall(body, grid=(S // tm,),
                  in_specs=[rs, _row_spec(tm, K), pl.BlockSpec((None, K, D), lambda i: (0, 0, 0)), rs, _const_spec((8, D))],
                  out_specs=[rs, rs, _const_spec((8, D)), _const_spec((8, LANE))],
                  out_shape=[_sds((S, D), F32), _sds((S, D), BF16), _sds((8, D), F32), _sds((8, LANE), F32)],
                  name=name, compiler_params=_params(("arbitrary",)))(x1, act, w3, tgt, vec)


def _norm_bwd(xin, dh, dxup, vec, fprev, name, deps=()):
    S, D = xin.shape
    has_prev = fprev is not None
    fused = isinstance(dh, tuple)
    tm = min(512, S)
    n_dh = 2 if fused else 1

    def body(*refs):
        x_ref, dh_refs, (up_ref, vec_ref) = refs[0], refs[1:1 + n_dh], refs[1 + n_dh:3 + n_dh]
        rest = refs[3 + n_dh:]
        if has_prev:
            fp_ref, dx_ref, dp_ref, sums_ref = rest
        else:
            dx_ref, sums_ref = rest

        @pl.when(pl.program_id(0) == 0)
        def _():
            sums_ref[...] = jnp.zeros_like(sums_ref)

        g, scale = vec_ref[0:1, :], vec_ref[1:2, :]
        x = x_ref[...]
        r = lax.rsqrt(_rmean(x * x) + EPS)
        xn = x * r
        if fused:
            dhv = lax.dot_general(dh_refs[0][...], dh_refs[1][...], (((1,), (1,)), ((), ())), preferred_element_type=F32)
        else:
            dhv = dh_refs[0][...]
        sums_ref[0:1, :] += _rsum(dhv)
        sums_ref[1:2, :] += _rsum(dhv * (xn * g))
        dm = dhv * (1.0 + scale)
        sums_ref[2:3, :] += _rsum(dm * xn)
        dxn = dm * g
        dx = up_ref[...] + r * (dxn - xn * _rmean(dxn * xn))
        dx_ref[...] = dx
        if has_prev:
            sums_ref[3:4, :] += _rsum(dx * fp_ref[...])
            dp_ref[...] = (dx * vec_ref[2:3, :]).astype(BF16)

    rs = _row_spec(tm, D)
    if fused:
        K = dh[0].shape[1]
        dh_ins, dh_specs = list(dh), [_row_spec(tm, K), pl.BlockSpec((None, D, K), lambda i: (0, 0, 0),
                                                                       pipeline_mode=pl.Buffered(1))]
    else:
        dh_ins, dh_specs = [dh], [rs]
    ins = [xin, *dh_ins, dxup, vec] + ([fprev] if has_prev else [])
    in_specs = [rs, *dh_specs, rs, _const_spec((8, D))] + ([rs] if has_prev else [])
    out_shape = [_sds((S, D), F32)] + ([_sds((S, D), BF16)] if has_prev else []) + [_sds((8, D), F32)]
    out_specs = [rs] + ([rs] if has_prev else []) + [_const_spec((8, D))]
    outs = _pcall(_after(body, len(ins), deps), grid=(S // tm,), in_specs=in_specs + [ANY] * len(deps),
                  out_specs=out_specs, out_shape=out_shape, name=name,
                  compiler_params=_params(("arbitrary",)))(*ins, *deps)
    return (outs[0], outs[1], outs[2]) if has_prev else (outs[0], None, outs[1])


def _gate_bwd(dmerged, z, ys, name, deps=()):
    S, D = dmerged.shape
    tm = min(512, S)
    ncol = z.shape[1] // D

    def body(dm_ref, g_ref, y_ref, dy_ref, dz_ref):
        sg = _sigmoid(g_ref[...].astype(F32))
        dm = dm_ref[...].astype(F32)
        dy_ref[...] = (dm * sg).astype(BF16)
        dz_ref[...] = (dm * y_ref[...].astype(F32) * (sg * (1.0 - sg))).astype(BF16)

    branch = pl.BlockSpec((None, tm, D), lambda i, n: (n, i, 0))
    return _pcall(_after(body, 3, deps), grid=(S // tm, 3),
                  in_specs=[pl.BlockSpec((tm, D), lambda i, n: (i, 0)), pl.BlockSpec((tm, D), lambda i, n: (i, 7 + n)),
                            branch] + [ANY] * len(deps),
                  out_specs=[branch, pl.BlockSpec((tm, D), lambda i, n: (i, 7 + n))],
                  out_shape=[_sds((3, S, D), BF16), _sds((S, ncol * D), BF16)], name=name,
                  compiler_params=_params(("parallel", "arbitrary")))(dmerged, z, ys, *deps)


def _mixer_bwd(z, dacts, conv, dz, wsh, sgu_ln, wtril, wtril_t, bias_full, cw, cvec, name):
    S = z.shape[0]
    D = wsh.shape[1]
    tm = CHUNK
    nt = S // tm
    hb = tm // HALO

    def body(zc, zp, da_ref, db_ref, dc_ref, conv_ref, wsh_ref, sln_ref, wt_ref, wtt_ref, bias_ref, cw_ref, cv_ref, _dz_in,
             dz_ref, vec_ref, dcw_ref, dws_ref, dbs_ref, pe, ge, dqe, dce, gr, dcr, cbuf, dcw8):
        i = pl.program_id(0)
        rb = nt - 1 - i

        @pl.when(i == 0)
        def _():
            vec_ref[...] = jnp.zeros_like(vec_ref)
            dcw8[...] = jnp.zeros_like(dcw8)
            dws_ref[...] = jnp.zeros_like(dws_ref)
            dbs_ref[...] = jnp.zeros_like(dbs_ref)
            dqe[tm:tm + HALO, :] = jnp.zeros((HALO, D), F32)
            dce[tm:tm + HALO, :] = jnp.zeros((HALO, D), F32)

        keep = (rb > 0).astype(F32)

        def col(n):
            return zc[:, n * D:(n + 1) * D].astype(F32)

        def pcol(n):
            return zp[:, n * D:(n + 1) * D].astype(F32)

        c_a, x_a = col(1), col(2)
        pe[0:HALO, :] = keep * (pcol(1) * pcol(2))
        pe[HALO:HALO + tm, :] = c_a * x_a
        q = wsh_ref[0:1, :] * pe[HALO - 2:HALO - 2 + tm, :]
        q = q + wsh_ref[1:2, :] * pe[HALO - 1:HALO - 1 + tm, :]
        q = q + wsh_ref[2:3, :] * pe[HALO:HALO + tm, :]
        dact = da_ref[...].astype(F32)
        dz_ref[:, 0:D] = (dact * q).astype(BF16)
        dq = dact * col(0)
        dqe[0:tm, :] = dq
        dp = wsh_ref[2:3, :] * dq + wsh_ref[1:2, :] * dqe[1:1 + tm, :] + wsh_ref[0:1, :] * dqe[2:2 + tm, :]
        dz_ref[:, D:2 * D] = (dp * x_a).astype(BF16)
        dz_ref[:, 2 * D:3 * D] = (dp * c_a).astype(BF16)
        for k in range(SHORT_K):
            o = HALO - (SHORT_K - 1) + k
            vec_ref[k:k + 1, :] += _rsum(dq * pe[o:o + tm, :])
        u, v = col(3), col(4)
        gu, tu = _gelu(u)
        gv, tv = _gelu(v)
        d = gv - _rmean(gv)
        rstd = lax.rsqrt(_rmean(d * d) + EPS)
        nrm = d * rstd
        vnb = (nrm * sln_ref[0:1, :] + sln_ref[1:2, :]).astype(BF16)
        dact = db_ref[...].astype(F32)
        dvn_parts, dgu_parts = [], []
        for g in range(NG):
            cs = slice(g * LANE, (g + 1) * LANE)
            vg = vnb[:, cs]
            mixed = jnp.dot(wt_ref[g], vg, preferred_element_type=F32) + bias_ref[:, cs]
            dgu_parts.append(dact[:, cs] * mixed)
            dmixed = dact[:, cs] * gu[:, cs]
            dmb = dmixed.astype(BF16)
            dws_ref[g] += lax.dot_general(dmb, vg, (((1,), (1,)), ((), ())), preferred_element_type=F32)
            dbs_ref[g] += jnp.broadcast_to(jnp.sum(dmixed, axis=1, keepdims=True), (CHUNK, LANE))
            dvn_parts.append(jnp.dot(wtt_ref[g], dmb, preferred_element_type=F32))
        dgu = jnp.concatenate(dgu_parts, axis=1)
        dvn = jnp.concatenate(dvn_parts, axis=1)
        dz_ref[:, 3 * D:4 * D] = (dgu * _dgelu(u, tu)).astype(BF16)
        vec_ref[3:4, :] += _rsum(dvn * nrm)
        vec_ref[4:5, :] += _rsum(dvn)
        dn = dvn * sln_ref[0:1, :]
        dgv = rstd * (dn - _rmean(dn) - nrm * _rmean(dn * nrm))
        dz_ref[:, 4 * D:5 * D] = (dgv * _dgelu(v, tv)).astype(BF16)
        a_c = col(5)
        sg = _sigmoid(col(6))
        ge[0:HALO, :] = keep * (pcol(5) * _sigmoid(pcol(6)))
        ge[HALO:HALO + tm, :] = a_c * sg
        _fill_shifted(ge, gr)
        o0 = HALO - (CFM_K - 1)
        conv = conv_ref[...].astype(F32)
        d = conv - _rmean(conv)
        rstd = lax.rsqrt(_rmean(d * d) + EPS)
        nrm = d * rstd
        ln = nrm * cv_ref[1:2, :] + cv_ref[2:3, :]
        sl = _sigmoid(ln)
        dln = dc_ref[...].astype(F32) * (sl * (1.0 + ln * (1.0 - sl)))
        vec_ref[6:7, :] += _rsum(dln * nrm)
        vec_ref[7:8, :] += _rsum(dln)
        dn = dln * cv_ref[1:2, :]
        dconv = rstd * (dn - _rmean(dn) - nrm * _rmean(dn * nrm))
        vec_ref[5:6, :] += _rsum(dconv)
        dce[0:tm, :] = dconv
        _fill_shifted(dce, dcr)
        _causal_conv(cw_ref, range(CFM_K), None, dce, dcr, [CFM_K - 1 - k for k in range(CFM_K)], tm, cbuf)
        dglu = cbuf[...]
        for cb in range(D // LANE):
            cs = slice(cb * LANE, (cb + 1) * LANE)
            dcv = dce[0:tm, cs]
            for k in range(CFM_K):
                prod = dcv * _rows_at(ge, gr, o0 + k, tm, cs)
                dcw8[k, :, cs] += jnp.sum(prod.reshape(tm // 8, 8, LANE), axis=0)

        @pl.when(i == nt - 1)
        def _():
            dcw_ref[...] = jnp.sum(dcw8[...], axis=1)
        dz_ref[:, 5 * D:6 * D] = (dglu * sg).astype(BF16)
        dz_ref[:, 6 * D:7 * D] = (dglu * a_c * (sg * (1.0 - sg))).astype(BF16)
        dqe[tm:tm + HALO, :] = dqe[0:HALO, :]
        dce[tm:tm + HALO, :] = dce[0:HALO, :]

    rev = lambda i: (nt - 1 - i, 0)
    rs = pl.BlockSpec((tm, D), rev)
    cur = pl.BlockSpec((tm, 7 * D), rev)
    prev = pl.BlockSpec((HALO, 7 * D), lambda i: (jnp.maximum((nt - 1 - i) * hb - 1, 0), 0))
    ext = pltpu.VMEM((HALO + tm, D), F32)
    outs = _pcall(
        body, grid=(nt,),
        in_specs=[cur, prev] + [pl.BlockSpec((None, tm, D), functools.partial(lambda i, n: (n, nt - 1 - i, 0), n=n))
                                for n in range(3)]
        + [rs, _const_spec((8, D)), _const_spec((8, D)), _const_spec((NG, CHUNK, CHUNK)),
                  _const_spec((NG, CHUNK, CHUNK)), _const_spec((CHUNK, D)), _const_spec((HALO, D)), _const_spec((8, D)),
                  ANY],
        out_specs=[cur, _const_spec((8, D)), _const_spec((HALO, D)), _const_spec((NG, CHUNK, CHUNK)),
                   _const_spec((NG, CHUNK, LANE))],
        out_shape=[_sds(dz.shape, BF16), _sds((8, D), F32), _sds((HALO, D), F32), _sds((NG, CHUNK, CHUNK), F32),
                   _sds((NG, CHUNK, LANE), F32)],
        scratch_shapes=[ext, ext, ext, ext, pltpu.VMEM((7, HALO + tm, D), F32), pltpu.VMEM((7, HALO + tm, D), F32),
                        pltpu.VMEM((tm, D), F32), pltpu.VMEM((HALO, 8, D), F32)],
        input_output_aliases={13: 0}, name=name,
        compiler_params=_params(("arbitrary",)))(z, z, dacts, dacts, dacts, conv, wsh, sgu_ln, wtril, wtril_t, bias_full, cw,
                                                 cvec, dz)
    return outs


def _ada_fwd(c_all, w_ada_loc, name):
    nb, D = c_all.shape
    L, _, nc = w_ada_loc.shape

    def body(c_ref, w_ref, o_ref, ca_ref):
        cv = c_ref[...]
        ca = cv * _sigmoid(cv)
        ca_ref[...] = ca
        o_ref[...] = jnp.dot(ca.astype(BF16), w_ref[...].astype(BF16), preferred_element_type=F32)

    return _pcall(body, grid=(L,),
                  in_specs=[_const_spec((nb, D)), pl.BlockSpec((None, D, nc), lambda l: (l, 0, 0))],
                  out_specs=[pl.BlockSpec((None, nb, nc), lambda l: (l, 0, 0)), _const_spec((nb, D))],
                  out_shape=[_sds((L, nb, nc), F32), _sds((nb, D), F32)], name=name,
                  compiler_params=_params(("arbitrary",)))(c_all, w_ada_loc)


def _adamw(w, g, m, v):
    m = ADAM_B1 * m + (1.0 - ADAM_B1) * g
    v = ADAM_B2 * v + (1.0 - ADAM_B2) * (g * g)
    m_hat = m / (1.0 - ADAM_B1 ** ADAM_STEP)
    v_hat = v / (1.0 - ADAM_B2 ** ADAM_STEP)
    delta = -ADAM_LR * (m_hat / (jnp.sqrt(v_hat) + ADAM_EPS) + ADAM_WD * w)
    return delta, m, v


def _tile_rows(R, C, align=8):
    cap = max(align, (1536 * 1024) // (4 * C))
    best = None
    for t in range(align, R + 1, align):
        if R % t == 0 and t <= cap:
            best = t
    return R if best is None else best


def _adam_ada(ct, dm, w, m, v, name):
    L, D, nc = w.shape
    nb = ct.shape[1]
    tr = _tile_rows(D, nc)

    def body(ct_ref, dm_ref, w_ref, m_ref, v_ref, g_ref, d_ref, mo_ref, vo_ref):
        g = ct_ref[:, 0:1] * dm_ref[0:1, :]
        for b in range(1, nb):
            g = g + ct_ref[:, b:b + 1] * dm_ref[b:b + 1, :]
        g_ref[...] = g
        d_ref[...], mo_ref[...], vo_ref[...] = _adamw(w_ref[...], g, m_ref[...], v_ref[...])

    ws = pl.BlockSpec((None, tr, nc), lambda l, r: (l, r, 0))
    return _pcall(body, grid=(L, D // tr),
                  in_specs=[pl.BlockSpec((tr, nb), lambda l, r: (r, 0)), pl.BlockSpec((None, nb, nc), lambda l, r: (l, 0, 0)),
                            ws, ws, ws],
                  out_specs=[ws] * 4, out_shape=[_sds(w.shape, F32)] * 4, name=name,
                  compiler_params=_params(("parallel", "parallel")))(ct, dm, w, m, v)


def _adam_small(parts, w, m, v, name, deps=(), single_rows=()):
    n, R, C = parts.shape
    tr = _tile_rows(R, C * n // 2)
    nl = len(single_rows[0]) if single_rows else 0

    def body(p_ref, w_ref, m_ref, v_ref, g_ref, d_ref, mo_ref, vo_ref, *single):
        g = p_ref[0]
        for j in range(1, n):
            g = g + p_ref[j]
        d, mo, vo = _adamw(w_ref[...], g, m_ref[...], v_ref[...])
        g_ref[...], d_ref[...], mo_ref[...], vo_ref[...] = g, d, mo, vo
        step = pl.program_id(0)
        for pi, rows in enumerate(single_rows):
            for l, row in enumerate(rows):
                @pl.when(step == row // tr)
                def _(pi=pi, l=l, off=row % tr):
                    for k, val in enumerate((g, d, mo, vo)):
                        single[4 * pi + k][l:l + 1, :] = val[off:off + 1, :]

    ws = pl.BlockSpec((tr, C), lambda r: (r, 0))
    one = pl.BlockSpec((nl, C), lambda r: (0, 0))
    outs = _pcall(_after(body, 4, deps), grid=(R // tr,),
                  in_specs=[pl.BlockSpec((n, tr, C), lambda r: (0, r, 0)), ws, ws, ws] + [ANY] * len(deps),
                  out_specs=[ws] * 4 + [one] * (4 * len(single_rows)),
                  out_shape=[_sds((R, C), F32)] * 4 + [_sds((nl, C), F32)] * (4 * len(single_rows)), name=name,
                  compiler_params=_params(("arbitrary",)))(parts, w, m, v, *deps)
    return outs[:4], outs[4:]


def _adam_plain(g, w, m, v, name):
    R, C = w.shape

    def body(g_ref, w_ref, m_ref, v_ref, d_ref, mo_ref, vo_ref):
        d_ref[...], mo_ref[...], vo_ref[...] = _adamw(w_ref[...], g_ref[...], m_ref[...], v_ref[...])

    ws = _const_spec((R, C))
    return _pcall(body, grid=(1,), in_specs=[ws] * 4, out_specs=[ws] * 3, out_shape=[_sds((R, C), F32)] * 3, name=name,
                  compiler_params=_params(("arbitrary",)))(g, w, m, v)


def _pair_sum(G, R1, my_c, name):
    n, R, C = G.shape
    half = n // 2
    tr = _tile_rows(R, C, align=16)

    def body(c_ref, g_ref, r_ref, o_ref):
        o_ref[...] = (g_ref[...].astype(F32) + r_ref[...].astype(F32)).astype(o_ref.dtype)

    blk = (None, tr, C)
    gs = pltpu.PrefetchScalarGridSpec(
        num_scalar_prefetch=1, grid=(half, R // tr),
        in_specs=[pl.BlockSpec(blk, lambda p, r, c: (2 * p + c[0], r, 0)), pl.BlockSpec(blk, lambda p, r, c: (p, r, 0))],
        out_specs=pl.BlockSpec(blk, lambda p, r, c: (p, r, 0)))
    return _pcall(body, grid_spec=gs, out_shape=_sds((half, R, C), G.dtype), name=name,
                  compiler_params=_params(("parallel", "parallel")))(my_c, G, R1)


def _adam_big(P, R2, my_chip, w, m, v, layer, prev, name, deps=()):
    _, R, C = P.shape
    nrecv = R2.shape[0]
    tr = _tile_rows(R, C, align=16)

    def body(p_sm, p_ref, r_ref, w_ref, m_ref, v_ref, *rest):
        g_ref, d_ref, mo_ref, vo_ref = rest[-4:]
        g = p_ref[...].astype(F32)
        for k in range(nrecv):
            g = g + r_ref[k].astype(F32)
        g_ref[...] = g
        d_ref[...], mo_ref[...], vo_ref[...] = _adamw(w_ref[...], g, m_ref[...], v_ref[...])

    ws = pl.BlockSpec((None, tr, C), lambda r, p: (layer, r, 0))
    held = [] if prev is None else list(prev)
    gs = pltpu.PrefetchScalarGridSpec(
        num_scalar_prefetch=1, grid=(R // tr,),
        in_specs=[pl.BlockSpec((None, tr, C), lambda r, p: (p[0], r, 0)),
                  pl.BlockSpec((nrecv, tr, C), lambda r, p: (0, r, 0)), ws, ws, ws] + [ANY] * (len(held) + len(deps)),
        out_specs=[ws] * 4)
    alias = {6 + i: i for i in range(len(held))}
    return _pcall(body, grid_spec=gs, out_shape=[_sds(w.shape, F32)] * 4, name=name, input_output_aliases=alias,
                  compiler_params=_params(("parallel",)))(my_chip, P, R2, w, m, v, *held, *deps)


def _place():
    return lax.axis_index("x"), lax.axis_index("y"), lax.axis_index("c")


def _all_gather(shards, name, deps=(), into=None):
    n = len(shards)
    bufs = [] if into is None else [b for b, _ in into]
    nb = len(bufs)

    def body(*refs):
        ins, outs = refs[:n], refs[n + nb:2 * n + nb]
        send_sems, recv_sems, local_sems = refs[2 * n + nb:]
        x, y, c = _place()
        me, sibling = (x, y, c), (x, y, 1 - c)
        chips = [(1 - x, y), (x, 1 - y), (1 - x, 1 - y)]

        def slot(a, px, py, pc):
            block = outs[a].at[4 * px + 2 * py + pc]
            return block if into is None else block.at[pl.ds(into[a][1], ins[a].shape[0])]

        def copy(a, k, block, to, src=None):
            return pltpu.make_async_remote_copy(
                src_ref=slot(a, *block) if src is None else src, dst_ref=slot(a, *block),
                send_sem=send_sems.at[7 * a + k], recv_sem=recv_sems.at[7 * a + k], device_id=to, device_id_type=MESH)

        mine = [pltpu.make_async_copy(ins[a], slot(a, *me), local_sems.at[a]) for a in range(n)]
        for cp in mine:
            cp.start()
        first = []
        for a in range(n):
            first.append(copy(a, 0, me, sibling, src=ins[a]))
            first += [copy(a, 1 + j, me, (*chip, c), src=ins[a]) for j, chip in enumerate(chips)]
        for cp in first:
            cp.start()
        passed = []
        for j, chip in enumerate(chips):
            for a in range(n):
                copy(a, 1 + j, (*chip, c), me).wait_recv()
                fwd = copy(a, 4 + j, (*chip, c), sibling)
                fwd.start()
                passed.append(fwd)
        for a in range(n):
            copy(a, 0, sibling, me).wait_recv()
        for j, chip in enumerate(chips):
            for a in range(n):
                copy(a, 4 + j, (*chip, 1 - c), me).wait_recv()
        for cp in first + passed:
            cp.wait_send()
        for cp in mine:
            cp.wait()

    out_shape = [_sds((NDEV,) + s.shape, s.dtype) for s in shards] if into is None else [_sds(b.shape, b.dtype) for b in bufs]
    outs = _pcall(_after(body, n + nb, deps), in_specs=[ANY] * (n + nb + len(deps)), out_specs=[ANY] * n,
                  out_shape=out_shape, input_output_aliases={n + a: a for a in range(nb)},
                  scratch_shapes=[pltpu.SemaphoreType.DMA((7 * n,)), pltpu.SemaphoreType.DMA((7 * n,)),
                                  pltpu.SemaphoreType.DMA((n,))], name=name)(*shards, *bufs, *deps)
    return list(outs)


HBM = pl.BlockSpec(memory_space=pltpu.HBM)
SEM = pl.BlockSpec(memory_space=pltpu.SEMAPHORE)


def _copies(plan, refs, send_sems, recv_sems):
    return [pltpu.make_async_remote_copy(src_ref=s, dst_ref=d, send_sem=send_sems.at[k], recv_sem=recv_sems.at[k],
                                         device_id=dev, device_id_type=MESH)
            for k, (s, d, dev) in enumerate(plan(refs, *_place()))]


def _xfer_start(bufs, ncopies, plan, name, deps=()):
    n = len(bufs)

    def body(*refs):
        for cp in _copies(plan, refs[:n], refs[n], refs[n + 1]):
            cp.start()
        token = refs[2 * n + 2]
        token[...] = jnp.zeros_like(token)

    outs = _pcall(
        _after(body, n, deps), name=name,
        out_shape=(pltpu.SemaphoreType.DMA((ncopies,)), pltpu.SemaphoreType.DMA((ncopies,)),
                   *[pltpu.HBM(b.shape, b.dtype) for b in bufs], _sds((8, LANE), F32)),
        in_specs=[HBM] * n + [ANY] * len(deps),
        out_specs=(SEM, SEM, *[HBM] * n, pl.BlockSpec(memory_space=pltpu.VMEM)),
        input_output_aliases={i: 2 + i for i in range(n)},
        compiler_params=pltpu.CompilerParams(has_side_effects=pltpu.SideEffectType.DATAFLOW_SIDE_EFFECTING),
    )(*[pltpu.with_memory_space_constraint(b, pltpu.HBM) for b in bufs], *deps)
    return (outs[0], outs[1]), list(outs[2:2 + n]), outs[2 + n]


def _xfer_wait(sems, bufs, plan, after, name):
    n = len(bufs)
    after = list(after) if isinstance(after, (list, tuple)) else [after]

    def body(*refs):
        for cp in _copies(plan, refs[:n], refs[n], refs[n + 1]):
            cp.wait_send()
            cp.wait_recv()

    outs = _pcall(
        body, name=name, out_shape=tuple(pltpu.HBM(b.shape, b.dtype) for b in bufs),
        in_specs=[HBM] * n + [SEM, SEM] + [ANY] * len(after), out_specs=tuple([HBM] * n),
        input_output_aliases={i: i for i in range(n)},
        compiler_params=pltpu.CompilerParams(has_side_effects=pltpu.SideEffectType.DATAFLOW_SIDE_EFFECTING),
    )(*bufs, *sems, *after)
    return list(outs)


def _chips_of(x, y):
    return [(1 - x, y), (x, 1 - y), (1 - x, 1 - y)]


def _landing(ref, dev_index, rows):
    block = ref.at[dev_index]
    return block if rows is None else block.at[pl.ds(rows[0], rows[1])]


def _gather_plan1(n, rows=None):
    def plan(refs, x, y, c):
        out = []
        for a in range(n):
            blk = _landing(refs[a], 4 * x + 2 * y + c, rows)
            out.append((blk, blk, (x, y, 1 - c)))
            out += [(blk, blk, (px, py, c)) for px, py in _chips_of(x, y)]
        return out
    return plan


def _gather_plan2(n, rows=None):
    def plan(refs, x, y, c):
        out = []
        for a in range(n):
            for px, py in _chips_of(x, y):
                blk = _landing(refs[a], 4 * px + 2 * py + c, rows)
                out.append((blk, blk, (x, y, 1 - c)))
        return out
    return plan


def _gather_start(shards, dev, name, deps=(), within=None):
    rows = None if within is None else (within[0], shards[0].shape[0])
    lands = []
    for s in shards:
        shape = (NDEV,) + s.shape if within is None else (NDEV, within[1]) + s.shape[1:]
        start = (dev,) + (0,) * s.ndim if within is None else (dev, within[0]) + (0,) * (s.ndim - 1)
        lands.append(lax.dynamic_update_slice(lax.empty(shape, s.dtype), s[None], start))
    n = len(shards)
    sems, lands, tok = _xfer_start(lands, 4 * n, _gather_plan1(n, rows), name + "_p1_start", deps)
    return dict(sems=sems, lands=lands, tok=tok, n=n, rows=rows)


def _gather_mid(st, after, name):
    n, rows = st["n"], st["rows"]
    lands = _xfer_wait(st["sems"], st["lands"], _gather_plan1(n, rows), after, name + "_p1_wait")
    sems, lands, tok = _xfer_start(lands, 3 * n, _gather_plan2(n, rows), name + "_p2_start")
    return dict(sems=sems, lands=lands, tok=tok, n=n, rows=rows)


def _gather_finish(st, after, name):
    return _xfer_wait(st["sems"], st["lands"], _gather_plan2(st["n"], st["rows"]), after, name + "_p2_wait")


def _scatter_plan1(n):
    def plan(refs, x, y, c):
        return [(refs[a].at[2 * p + 1 - c], refs[n + a].at[p], (x, y, 1 - c)) for a in range(n) for p in range(NCHIP)]
    return plan


def _scatter_plan2(n):
    def plan(refs, x, y, c):
        return [(refs[a].at[2 * px + py], refs[n + a].at[j], (px, py, c))
                for a in range(n) for j, (px, py) in enumerate(_chips_of(x, y))]
    return plan


def _scatter_start(Gs, name):
    n = len(Gs)
    R1s = [lax.empty((NCHIP,) + g.shape[1:], g.dtype) for g in Gs]
    sems, bufs, tok = _xfer_start(list(Gs) + R1s, NCHIP * n, _scatter_plan1(n), name + "_s1_start")
    return dict(sems=sems, bufs=bufs, tok=tok, n=n)


def _scatter_mid(st, after, my_c, name):
    n = st["n"]
    bufs = _xfer_wait(st["sems"], st["bufs"], _scatter_plan1(n), after, name + "_s1_wait")
    Ps = [_pair_sum(bufs[a], bufs[n + a], my_c, f"{name}_pair_sum{a}") for a in range(n)]
    R2s = [lax.empty((3,) + p.shape[1:], p.dtype) for p in Ps]
    sems, bufs, tok = _xfer_start(Ps + R2s, 3 * n, _scatter_plan2(n), name + "_s2_start")
    return dict(sems=sems, bufs=bufs, tok=tok, n=n)


def _scatter_finish(st, after, name):
    n = st["n"]
    bufs = _xfer_wait(st["sems"], st["bufs"], _scatter_plan2(n), after, name + "_s2_wait")
    return bufs[:n], bufs[n:]


SMALL_ROWS = {"norm1_g": (0, 1), "norm2_g": (1, 1), "sgu_ln_g": (2, 1), "sgu_ln_b": (3, 1), "cfm_conv_b": (4, 1),
              "cfm_ln_g": (5, 1), "cfm_ln_b": (6, 1), "b_sgu": (7, 1), "w_sgu": (8, 128), "b_ada": (136, N_MOD),
              "w_short": (142, SHORT_K), "cfm_conv_w": (145, CFM_K)}
ROWS_PER_LAYER = 176
FINAL_ROW = DEPTH * ROWS_PER_LAYER
PACK_ROWS = 360


def _pack(get, D, layers=tuple(range(DEPTH)), tail=True):
    parts = []
    for l in layers:
        for name, (_, nrows) in SMALL_ROWS.items():
            a = get(name, l)
            parts.append(jnp.zeros((nrows * D,), F32) if a is None else a.astype(F32).reshape(nrows * D))
    if tail:
        for name in ("final_g", "loss"):
            a = get(name, None)
            parts.append(jnp.zeros((D,), F32) if a is None else a.astype(F32).reshape(D))
        parts.append(jnp.zeros(((PACK_ROWS - FINAL_ROW - 2) * D,), F32))
    return jnp.concatenate(parts).reshape(-1, D)


def _unpack(pack, name, shape):
    D = pack.shape[1]
    r0, nrows = SMALL_ROWS[name]
    return jnp.stack([pack[l * ROWS_PER_LAYER + r0:l * ROWS_PER_LAYER + r0 + nrows] for l in range(DEPTH)]).reshape(shape)


def _mm_tiles(S):
    return min(512, S), min(1024, S), min(2048, S)


def kernel(x, c, w_ada, b_ada, norm1_g, w_in, w_short, w_a_out, sgu_ln_g, sgu_ln_b, w_sgu, b_sgu, w_b_out, cfm_conv_w, cfm_conv_b, cfm_ln_g, cfm_ln_b, w_c_out, w_o, norm2_g, w_ffn_in, w_ffn_out, final_g, loss_target, m_w_ada, m_b_ada, m_norm1_g, m_w_in, m_w_short, m_w_a_out, m_sgu_ln_g, m_sgu_ln_b, m_w_sgu, m_b_sgu, m_w_b_out, m_cfm_conv_w, m_cfm_conv_b, m_cfm_ln_g, m_cfm_ln_b, m_w_c_out, m_w_o, m_norm2_g, m_w_ffn_in, m_w_ffn_out, m_final_g, v_w_ada, v_b_ada, v_norm1_g, v_w_in, v_w_short, v_w_a_out, v_sgu_ln_g, v_sgu_ln_b, v_w_sgu, v_b_sgu, v_w_b_out, v_cfm_conv_w, v_cfm_conv_b, v_cfm_ln_g, v_cfm_ln_b, v_w_c_out, v_w_o, v_norm2_g, v_w_ffn_in, v_w_ffn_out, v_final_g):
    W = dict(w_ada=w_ada, b_ada=b_ada, norm1_g=norm1_g, w_in=w_in, w_short=w_short, w_a_out=w_a_out, sgu_ln_g=sgu_ln_g,
             sgu_ln_b=sgu_ln_b, w_sgu=w_sgu, b_sgu=b_sgu, w_b_out=w_b_out, cfm_conv_w=cfm_conv_w, cfm_conv_b=cfm_conv_b,
             cfm_ln_g=cfm_ln_g, cfm_ln_b=cfm_ln_b, w_c_out=w_c_out, w_o=w_o, norm2_g=norm2_g, w_ffn_in=w_ffn_in,
             w_ffn_out=w_ffn_out, final_g=final_g)
    Mo = dict(w_ada=m_w_ada, b_ada=m_b_ada, norm1_g=m_norm1_g, w_in=m_w_in, w_short=m_w_short, w_a_out=m_w_a_out,
              sgu_ln_g=m_sgu_ln_g, sgu_ln_b=m_sgu_ln_b, w_sgu=m_w_sgu, b_sgu=m_b_sgu, w_b_out=m_w_b_out,
              cfm_conv_w=m_cfm_conv_w, cfm_conv_b=m_cfm_conv_b, cfm_ln_g=m_cfm_ln_g, cfm_ln_b=m_cfm_ln_b,
              w_c_out=m_w_c_out, w_o=m_w_o, norm2_g=m_norm2_g, w_ffn_in=m_w_ffn_in, w_ffn_out=m_w_ffn_out,
              final_g=m_final_g)
    Vo = dict(w_ada=v_w_ada, b_ada=v_b_ada, norm1_g=v_norm1_g, w_in=v_w_in, w_short=v_w_short, w_a_out=v_w_a_out,
              sgu_ln_g=v_sgu_ln_g, sgu_ln_b=v_sgu_ln_b, w_sgu=v_w_sgu, b_sgu=v_b_sgu, w_b_out=v_w_b_out,
              cfm_conv_w=v_cfm_conv_w, cfm_conv_b=v_cfm_conv_b, cfm_ln_g=v_cfm_ln_g, cfm_ln_b=v_cfm_ln_b,
              w_c_out=v_w_c_out, w_o=v_w_o, norm2_g=v_norm2_g, w_ffn_in=v_w_ffn_in, w_ffn_out=v_w_ffn_out,
              final_g=v_final_g)
    order = ["w_ada", "b_ada", "norm1_g", "w_in", "w_short", "w_a_out", "sgu_ln_g", "sgu_ln_b", "w_sgu", "b_sgu",
             "w_b_out", "cfm_conv_w", "cfm_conv_b", "cfm_ln_g", "cfm_ln_b", "w_c_out", "w_o", "norm2_g", "w_ffn_in",
             "w_ffn_out", "final_g"]

    assert DEPTH == 2, "the weight-gather schedule below is written for two layers"
    S, D = x.shape[1], x.shape[2]
    F2 = w_ffn_in.shape[2] * NDEV
    FF = F2 // 2
    xi, yi, ci = _place()
    dev = 4 * xi + 2 * yi + ci
    my_c = jnp.reshape(ci, (1,)).astype(jnp.int32)
    my_chip = jnp.reshape(2 * xi + yi, (1,)).astype(jnp.int32)
    tm, tm_big, tm_huge = _mm_tiles(S)
    x0 = x.reshape(S, D)
    tgt = loss_target.reshape(S, D)

    def shards_of(l):
        return [w_in[l].astype(BF16), w_a_out[l].astype(BF16), w_b_out[l].astype(BF16), w_c_out[l].astype(BF16),
                w_o[l].astype(BF16), w_ffn_in[l].astype(BF16), w_ffn_out[l].astype(BF16)]

    c_all = _all_gather([jnp.pad(c, ((0, 7), (0, 0)))], "ag_c")[0][:, 0, :]
    modpart, c_act = _ada_fwd(c_all, w_ada, "ada_fwd")
    ncol = modpart.shape[2]
    mg = _all_gather([modpart.reshape(DEPTH * NDEV, ncol)], "ag_mod")[0].reshape(NDEV, DEPTH, NDEV, ncol)
    mine = lax.dynamic_index_in_dim(mg, dev, axis=2, keepdims=False)
    mod = (jnp.transpose(mine, (1, 0, 2)).reshape(DEPTH, N_MOD * D) + b_ada).reshape(DEPTH, N_MOD, D)

    ncs = w_short.shape[2]
    ag_in0 = _gather_start([w_in[0].astype(BF16), w_short.reshape(DEPTH * SHORT_K, ncs),
                            cfm_conv_w.reshape(DEPTH * CFM_K, ncs)], dev, "ag_w_in0", deps=(mod,))
    W, Mo, Vo = lax.optimization_barrier((ag_in0["tok"], (W, Mo, Vo)))[1]
    (norm1_g, norm2_g, w_in, w_a_out, w_b_out, w_c_out, w_o, w_ffn_in, w_ffn_out, sgu_ln_g, sgu_ln_b, w_sgu, b_sgu,
     cfm_conv_b, cfm_ln_g, cfm_ln_b, final_g) = [W[k] for k in (
         "norm1_g", "norm2_g", "w_in", "w_a_out", "w_b_out", "w_c_out", "w_o", "w_ffn_in", "w_ffn_out", "sgu_ln_g",
         "sgu_ln_b", "w_sgu", "b_sgu", "cfm_conv_b", "cfm_ln_g", "cfm_ln_b", "final_g")]
    m_w_ada, v_w_ada = Mo["w_ada"], Vo["w_ada"]
    xl0, h0, ht0 = _norm_fwd(x0, None, _rows(jnp.zeros((D,), F32), norm1_g[0], mod[0, 1], mod[0, 0]), "norm1_fwd0",
                             deps=(ag_in0["tok"],))
    ag_rest0 = _gather_start(shards_of(0)[1:], dev, "ag_rest0", deps=(h0,))

    tril = jnp.tril(jnp.ones((CHUNK, CHUNK), dtype=bool))

    def layer_consts(l):
        wt = jnp.where(tril[None], w_sgu[l], 0.0).astype(BF16)
        return dict(sgu_ln=_rows(sgu_ln_g[l], sgu_ln_b[l]), wtril=wt, wtril_t=jnp.swapaxes(wt, 1, 2),
                    bias_full=jnp.repeat(b_sgu[l].T, LANE, axis=1), cvec=_rows(cfm_conv_b[l], cfm_ln_g[l], cfm_ln_b[l]))

    def rest_of(g):
        return dict(w_a=g[0].reshape(1, D, D), w_b=g[1].reshape(1, D, D), w_c=g[2].reshape(1, D, D),
                    w_o=g[3].reshape(1, D, D), w_fi=jnp.transpose(g[4], (1, 0, 2)).reshape(1, D, F2),
                    w_fo=g[5].reshape(1, FF, D))

    sharded_small = ("w_short", "cfm_conv_w")

    def param_get(T):
        def get(name, l):
            if name == "final_g":
                return T[name]
            return None if name in sharded_small or name == "loss" else T[name][l]
        return get

    packs = [_pack(param_get(T), D) for T in (W, Mo, Vo)]
    ag_in0 = _gather_mid(ag_in0, [ag_rest0["tok"], *packs], "ag_w_in0")
    (w_sgu, b_sgu, sgu_ln_g, sgu_ln_b, cfm_conv_b, cfm_ln_g, cfm_ln_b), conv_wmv_in = lax.optimization_barrier(
        (ag_in0["tok"], ((w_sgu, b_sgu, sgu_ln_g, sgu_ln_b, cfm_conv_b, cfm_ln_g, cfm_ln_b),
                         [(T["w_short"], T["cfm_conv_w"]) for T in (W, Mo, Vo)])))[1]
    consts = [layer_consts(l) for l in range(DEPTH)]
    ncr = DEPTH * (SHORT_K + CFM_K)
    padr = (-ncr) % 8
    convw_wmv = [jnp.pad(jnp.concatenate([a.reshape(-1, ncs), b.reshape(-1, ncs)]), ((0, padr), (0, 0)))
                 for a, b in conv_wmv_in]
    g_in0 = _gather_finish(ag_in0, [*convw_wmv] + [a for cl in consts for a in cl.values()], "ag_w_in0")
    w_short_full = jnp.transpose(g_in0[1], (1, 0, 2)).reshape(DEPTH, SHORT_K, D)
    cfm_w_full = jnp.transpose(g_in0[2], (1, 0, 2)).reshape(DEPTH, CFM_K, D)
    for l in range(DEPTH):
        consts[l]["wsh"] = jnp.pad(w_short_full[l], ((0, 8 - SHORT_K), (0, 0)))
        consts[l]["cw"] = jnp.pad(cfm_w_full[l], ((0, HALO - CFM_K), (0, 0)))
    Wg = [dict(w_in=g_in0[0]), None]
    ag_l1 = None
    nin = w_in.shape[2]
    tn_in = nin if nin % 256 == 0 and nin <= 1280 else 256
    tn_fi = 512 if F2 % 512 == 0 else 256
    tn_dw = min(256, D)

    saved = []
    xcur, gprev, ffn_tail = x0, None, None
    for l in range(DEPTH):
        sh1, sc1, g1, sh2, sc2, g2 = [mod[l, k] for k in range(N_MOD)]
        cl = consts[l]
        if l == 0:
            xl, h, ht = xl0, h0, ht0
        else:
            vec1 = _rows(gprev, norm1_g[l], sc1, sh1)
            act_prev, w_fo_prev = ffn_tail
            ag_l1 = _gather_mid(ag_l1, act_prev, f"ag_w{l}")
            f_prev, xl, h, ht = _mm_resid_norm(act_prev, w_fo_prev, xcur, vec1, tm, f"mm_ffn_out_norm1_{l}",
                                               deps=(ag_l1["tok"],))
            saved[l - 1]["f"] = f_prev
            g = _gather_finish(ag_l1, h, f"ag_w{l}")
            Wg[l] = dict(w_in=g[0], **rest_of(g[1:]))
        wl = Wg[l]
        z = _mm_nn(h, wl["w_in"], BF16, tm_huge, tn_in, D, f"mm_in{l}", w_outer=True)
        mix_deps = ()
        if l == 0:
            ag_rest0 = _gather_mid(ag_rest0, z, "ag_rest0")
            mix_deps = (ag_rest0["tok"],)
            if DEPTH > 1:
                ag_l1 = _gather_start(shards_of(1), dev, "ag_w1")
                mix_deps += (ag_l1["tok"],)
        acts, acts_t, conv = _mixer_fwd(z, cl["wsh"], cl["sgu_ln"], cl["wtril"], cl["bias_full"], cl["cw"], cl["cvec"],
                                        f"mixer_fwd{l}", deps=mix_deps)
        if l == 0:
            wl.update(rest_of(_gather_finish(ag_rest0, acts[0], "ag_rest0")))
        merged, merged_t, ys = _branch_out(acts, [wl["w_a"][0], wl["w_b"][0], wl["w_c"][0]], z, f"branch_out{l}")
        o, x1, h2, h2t = _mm_resid_norm(merged, wl["w_o"], xl, _rows(g1, norm2_g[l], sc2, sh2), tm, f"mm_o_norm2_{l}")
        gu, act, act_t = _ffn_in_swiglu(h2, wl["w_fi"], tm_huge, 256, f"mm_ffn_in{l}")
        saved.append(dict(xl=xl, ht=ht, z=z, acts_t=acts_t, conv=conv, ys=ys, merged_t=merged_t, o=o, x1=x1, h2t=h2t, gu=gu,
                          act_t=act_t, f=None, consts=cl, mod=(sh1, sc1, g1, sh2, sc2, g2)))
        xcur, gprev, ffn_tail = x1, g2, (act, wl["w_fo"])

    last = saved[-1]
    dxup, dfb, fsums, loss_blk = _final_bwd(last["x1"], *ffn_tail, tgt, _rows(last["mod"][5], final_g), "final_bwd")
    loss_row = jnp.pad(loss_blk[0, 0:1], (0, D - 1))
    dgate2_next = fsums[1]
    small = [dict() for _ in range(DEPTH)]
    dmods = [None] * DEPTH
    nfi = w_ffn_in.shape[2]
    early_names, late_names = ["w_ffn_out", "w_ffn_in", "w_o"], ["w_a_out", "w_b_out", "w_c_out", "w_in"]
    results = {n: None for n in early_names + late_names}

    def adam_group(names, Ps, R2s, l, deps=()):
        for n, p, r2 in zip(names, Ps, R2s):
            results[n] = _adam_big(p, r2, my_chip, W[n], Mo[n], Vo[n], l, results[n], f"adam_{n}{l}", deps)

    deferred = []
    late_prev = None
    ag_s1, gathered1 = None, None
    tk_w = min(2048, S)
    tn_dw_in = tn_in // 2 if tn_in == 1280 else tn_in
    for l in reversed(range(DEPTH)):
        sv, wl, cl = saved[l], Wg[l], saved[l]["consts"]
        sh1, sc1, g1, sh2, sc2, g2 = sv["mod"]
        dact = _mm_nt(dfb, wl["w_fo"], BF16, tm_big, FF, D, f"mm_dact{l}",
                      deps=() if late_prev is None else (late_prev["tok"], ag_s1["tok"]))
        g_fo = _mm_wgrad(sv["act_t"], dfb, 1, FF // 2, D, tk_w, f"mm_dw_ffn_out{l}")
        dgu = _swiglu_bwd(dact, sv["gu"], f"swiglu_bwd{l}")
        g_fi = _mm_wgrad(sv["h2t"], dgu, 1, D, tn_fi, S, f"mm_dw_ffn_in{l}")
        if late_prev is not None:
            deferred.append((late_names, *_scatter_finish(late_prev, g_fi, f"rs_late{l + 1}"), l + 1))
            late_prev = None
        if ag_s1 is not None:
            ag_s1 = _gather_mid(ag_s1, g_fi, "ag_small1")
        dx1, dob, s2 = _norm_bwd(sv["x1"], (dgu, wl["w_fi"]), dxup, _rows(norm2_g[l], sc2, g1), sv["o"],
                                 f"mm_dh2_norm2_bwd{l}", deps=() if ag_s1 is None else (ag_s1["tok"],))
        dmerged = _mm_nt(dob, wl["w_o"], BF16, tm_big, D, D, f"mm_dmerged{l}")
        g_o = _mm_wgrad(sv["merged_t"], dob, 1, D, tn_dw, S, f"mm_dw_o{l}")
        early = _scatter_start([g_fo.reshape(NDEV, FF // NDEV, D),
                                jnp.transpose(g_fi.reshape(D, NDEV, nfi), (1, 0, 2)),
                                g_o.reshape(NDEV, D // NDEV, D)], f"rs_early{l}")
        dys, dz = _gate_bwd(dmerged, sv["z"], sv["ys"], f"gate_bwd{l}", deps=(early["tok"],))
        if ag_s1 is not None:
            gathered1 = _gather_finish(ag_s1, dys, "ag_small1")[0]
            ag_s1 = None
        early = _scatter_mid(early, dys, my_c, f"rs_early{l}")
        dacts = _mm3_nt(dys, [wl["w_a"], wl["w_b"], wl["w_c"]], tm_big, f"mm_dact_abc{l}", deps=(early["tok"],))
        g3 = _mm3_wgrad(sv["acts_t"], dys, tn_dw, f"mm_dw_abc{l}")
        g_abc = [g3[n] for n in range(3)]
        dz, mvec, dcw, dws, dbs = _mixer_bwd(sv["z"], dacts, sv["conv"], dz, cl["wsh"], cl["sgu_ln"], cl["wtril"],
                                             cl["wtril_t"], cl["bias_full"], cl["cw"], cl["cvec"], f"mixer_bwd{l}")
        dh = _mm_nt(dz, wl["w_in"], F32, tm_big, D, tn_in, f"mm_dh{l}",
                    blocks_per_step=2 if (tn_in == nin and wl["w_in"].shape[0] % 2 == 0) else 1)
        g_in = _mm_wgrad(sv["ht"], dz, NDEV, D, tn_dw_in, S, f"mm_dw_in{l}")
        late = _scatter_start([g.reshape(NDEV, D // NDEV, D) for g in g_abc] + [g_in], f"rs_late{l}")
        if l > 0:
            pv = saved[l - 1]
            dxup, dfb, s1 = _norm_bwd(sv["xl"], dh, dx1, _rows(norm1_g[l], sc1, pv["mod"][5]), pv["f"], f"norm1_bwd{l}",
                                      deps=(late["tok"],))
        else:
            dxup, dfb, s1 = _norm_bwd(sv["xl"], dh, dx1, _rows(norm1_g[l], sc1), None, f"norm1_bwd{l}", deps=(late["tok"],))
        deferred.append((early_names, *_scatter_finish(early, dxup, f"rs_early{l}"), l))
        dmods[l] = jnp.stack([s1[0], s1[1], s2[3], s2[0], s2[1], dgate2_next])
        dgate2_next = s1[3]
        small[l] = dict(norm1_g=s1[2], norm2_g=s2[2], sgu_ln_g=mvec[3], sgu_ln_b=mvec[4], cfm_conv_b=mvec[5],
                        cfm_ln_g=mvec[6], cfm_ln_b=mvec[7], b_sgu=dbs[:, :, 0],
                        w_sgu=jnp.where(tril[None], dws, 0.0), b_ada=dmods[l], w_short=mvec[0:SHORT_K],
                        cfm_conv_w=dcw[0:CFM_K])
        small_get = lambda name, k: {"final_g": fsums[0], "loss": loss_row}.get(name) if k is None else small[k][name]
        if l > 0:
            late_prev = _scatter_mid(late, dxup, my_c, f"rs_late{l}")
            ag_s1 = _gather_start([_pack(small_get, D, layers=(l,), tail=True)], dev, "ag_small1", deps=(late_prev["tok"],),
                                  within=(l * ROWS_PER_LAYER, PACK_ROWS))
    grad_x = dxup.reshape(x.shape)

    gathered = _all_gather([_pack(small_get, D, layers=(0,), tail=False)], "ag_small0", deps=(dxup,),
                           into=[(gathered1, 0)])[0]
    late_prev = _scatter_mid(late, gathered, my_c, "rs_late0")
    one_row = [n for n in order if n in SMALL_ROWS and SMALL_ROWS[n][1] == 1 and W[n].ndim == 2]
    (sg, sd, sm, sv_), singles = _adam_small(
        gathered, *packs, name="adam_small", deps=(late_prev["tok"],),
        single_rows=[tuple(l * ROWS_PER_LAYER + SMALL_ROWS[n][0] for l in range(DEPTH)) for n in one_row])
    loss = sg[FINAL_ROW + 1, 0]
    out = {n: tuple(singles[4 * i:4 * i + 4]) for i, n in enumerate(one_row)}
    for name in order:
        if name in SMALL_ROWS and name not in sharded_small and name not in out:
            out[name] = tuple(_unpack(p, name, W[name].shape) for p in (sg, sd, sm, sv_))
    out["final_g"] = tuple(p[FINAL_ROW] for p in (sg, sd, sm, sv_))

    def my_cols(name):
        full = _unpack(sg, name, (DEPTH, SMALL_ROWS[name][1], D))
        return lax.dynamic_slice_in_dim(full, dev * ncs, ncs, axis=2)

    gcs = jnp.concatenate([my_cols("w_short").reshape(-1, ncs), my_cols("cfm_conv_w").reshape(-1, ncs)])
    cd, cm, cv = _adam_plain(jnp.pad(gcs, ((0, padr), (0, 0))), *convw_wmv, "adam_convw")
    nsh = DEPTH * SHORT_K
    out["w_short"] = tuple(a[0:nsh].reshape(w_short.shape) for a in (gcs, cd, cm, cv))
    out["cfm_conv_w"] = tuple(a[nsh:ncr].reshape(cfm_conv_w.shape) for a in (gcs, cd, cm, cv))

    dm_all = jnp.stack([gathered[:, l * ROWS_PER_LAYER + 136:l * ROWS_PER_LAYER + 136 + N_MOD, :].reshape(NDEV, N_MOD * D)
                        for l in range(DEPTH)])
    dm_mine = lax.dynamic_slice_in_dim(dm_all, dev * ncol, ncol, axis=2)
    out["w_ada"] = tuple(_adam_ada(jnp.transpose(c_act), dm_mine, w_ada, m_w_ada, v_w_ada, "adam_ada"))

    for names, Ps, R2s, l in deferred:
        adam_group(names, Ps, R2s, l, deps=(late_prev["tok"],))
    adam_group(late_names, *_scatter_finish(late_prev, results["w_o"][0], "rs_late0"), 0)
    for n in early_names + late_names:
        out[n] = tuple(results[n])

    grads = [out[n][0] for n in order]
    deltas = [out[n][1] for n in order]
    new_m = [out[n][2] for n in order]
    new_v = [out[n][3] for n in order]
    return (loss, grad_x, *grads, *deltas, *new_m, *new_v)
```

```python
import functools
import math

import jax
import jax.numpy as jnp
from jax import lax
from jax.experimental import pallas as pl
from jax.experimental.pallas import tpu as pltpu

F32, BF16 = jnp.float32, jnp.bfloat16
NDEV = 8
NCHIP = NDEV // 2
DEPTH = 2
EPS = 1e-6
CHUNK = 128
NG = 8
SHORT_K = 3
CFM_K = 31
HALO = 32
N_MOD = 6
LANE = 128
VMEM_LIMIT = 56 * 1024 * 1024
ADAM_LR, ADAM_B1, ADAM_B2, ADAM_EPS, ADAM_WD, ADAM_STEP = 0.001, 0.9, 0.999, 1e-08, 0.01, 10
_G0 = math.sqrt(2.0 / math.pi)
_G1 = 0.044715
MESH = pl.DeviceIdType.MESH
ANY = pl.BlockSpec(memory_space=pl.ANY)


def _pcall(body, **kw):
    return pl.pallas_call(body, **kw)


def _params(sem=None):
    return pltpu.CompilerParams(dimension_semantics=sem, vmem_limit_bytes=VMEM_LIMIT)


def _sds(shape, dtype):
    return jax.ShapeDtypeStruct(tuple(shape), dtype)


def _mm_body(dims, nk, out_f32, blocks=1):
    def body(a_ref, b_ref, o_ref, *scr):
        k = pl.program_id(2)
        if blocks == 1:
            part = lax.dot_general(a_ref[...], b_ref[...], dims, preferred_element_type=F32)
        else:
            w = a_ref.shape[1] // blocks
            part = None
            for g in range(blocks):
                t = lax.dot_general(a_ref[:, g * w:(g + 1) * w], b_ref[g], dims, preferred_element_type=F32)
                part = t if part is None else part + t
        if nk == 1:
            o_ref[...] = part.reshape(o_ref.shape).astype(o_ref.dtype)
        elif out_f32:
            @pl.when(k == 0)
            def _():
                o_ref[...] = part.reshape(o_ref.shape)

            @pl.when(k > 0)
            def _():
                o_ref[...] += part.reshape(o_ref.shape)
        else:
            acc = scr[0]

            @pl.when(k == 0)
            def _():
                acc[...] = part

            @pl.when(k > 0)
            def _():
                acc[...] += part

            @pl.when(k == nk - 1)
            def _():
                o_ref[...] = acc[...].astype(o_ref.dtype)
    return body


def _after(body, n_in, deps):
    nd = len(deps)
    if nd == 0:
        return body

    def ordered(*refs):
        return body(*refs[:n_in], *refs[n_in + nd:])
    return ordered


def _mm_call(body, grid, in_specs, out_spec, out_shape, acc_shape, name, deps=()):
    scratch = [] if acc_shape is None else [pltpu.VMEM(acc_shape, F32)]
    return _pcall(_after(body, 2, deps), grid=grid, in_specs=in_specs + [ANY] * len(deps), out_specs=out_spec,
                  out_shape=out_shape, scratch_shapes=scratch, name=name,
                  compiler_params=_params(("parallel", "parallel", "arbitrary")))


def _mm_nn(a, b3, out_dtype, tm, tn, tk, name, w_outer=False, deps=()):
    M, K = a.shape
    G, _, Nb = b3.shape
    npb, nk = Nb // tn, K // tk
    out_f32 = out_dtype == F32
    body = _mm_body((((1,), (0,)), ((), ())), nk, out_f32)
    if w_outer:
        grid = (G * npb, M // tm, nk)
        ij = lambda p, q: (q, p)
    else:
        grid = (M // tm, G * npb, nk)
        ij = lambda p, q: (p, q)

    def a_map(p, q, k):
        i, j = ij(p, q)
        return (i, k)

    def b_map(p, q, k):
        i, j = ij(p, q)
        return (j // npb, k, j % npb)

    def o_map(p, q, k):
        return ij(p, q)

    def wrapped(a_ref, b_ref, o_ref, *scr):
        body(a_ref, b_ref, o_ref, *scr)

    return _mm_call(wrapped, grid, [pl.BlockSpec((tm, tk), a_map), pl.BlockSpec((None, tk, tn), b_map)],
                    pl.BlockSpec((tm, tn), o_map), _sds((M, G * Nb), out_dtype),
                    None if (nk == 1 or out_f32) else (tm, tn), name, deps)(a, b3, *deps)


def _mm_nt(a, b3, out_dtype, tm, tn, tk, name, deps=(), blocks_per_step=1):
    M, _ = a.shape
    G, Ko, Nb = b3.shape
    kpb = Nb // tk
    nk = G * kpb // blocks_per_step
    out_f32 = out_dtype == F32
    body = _mm_body((((1,), (1,)), ((), ())), nk, out_f32, blocks_per_step)

    def wrapped(a_ref, b_ref, o_ref, *scr):
        body(a_ref, b_ref, o_ref, *scr)

    if blocks_per_step > 1:
        assert tk == Nb and G % blocks_per_step == 0
        b_spec = pl.BlockSpec((blocks_per_step, tn, tk), lambda i, j, k: (k, j, 0))
    else:
        b_spec = pl.BlockSpec((None, tn, tk), lambda i, j, k: (k // kpb, j, k % kpb))
    return _mm_call(wrapped, (M // tm, Ko // tn, nk),
                    [pl.BlockSpec((tm, tk * blocks_per_step), lambda i, j, k: (i, k)), b_spec],
                    pl.BlockSpec((tm, tn), lambda i, j, k: (i, j)), _sds((M, Ko), out_dtype),
                    None if (nk == 1 or out_f32) else (tm, tn), name, deps)(a, b3, *deps)


def _mm_wgrad(at, b, G, tm, tn, tk, name, deps=()):
    M, T = at.shape
    Nb = b.shape[1] // G
    npb, nk = Nb // tn, T // tk
    body = _mm_body((((1,), (0,)), ((), ())), nk, False)

    def wrapped(a_ref, b_ref, o_ref, *scr):
        body(a_ref, b_ref, o_ref, *scr)

    a = at
    in_specs = [pl.BlockSpec((tm, tk), lambda i, j, k: (i, k)), pl.BlockSpec((tk, tn), lambda i, j, k: (k, j))]
    out_spec = pl.BlockSpec((None, tm, tn), lambda i, j, k: (j // npb, i, j % npb))
    return _mm_call(wrapped, (M // tm, G * npb, nk), in_specs, out_spec, _sds((G, M, Nb), BF16),
                    None if nk == 1 else (tm, tn), name, deps)(a, b, *deps)


def _mm3_nt(x3, ws, tm, name, deps=()):
    nb, S, K = x3.shape
    Ko = ws[0].shape[1]

    def body(x_ref, w0, w1, w2, o_ref):
        n = pl.program_id(0)
        for k, w in enumerate((w0, w1, w2)):
            @pl.when(n == k)
            def _(w=w):
                o_ref[...] = lax.dot_general(x_ref[...], w[...], (((1,), (1,)), ((), ())),
                                             preferred_element_type=F32).astype(BF16)

    wspec = pl.BlockSpec((None, Ko, K), lambda n, i: (0, 0, 0))
    return _pcall(_after(body, 4, deps), grid=(nb, S // tm),
                  in_specs=[pl.BlockSpec((None, tm, K), lambda n, i: (n, i, 0)), wspec, wspec, wspec] + [ANY] * len(deps),
                  out_specs=pl.BlockSpec((None, tm, Ko), lambda n, i: (n, i, 0)), out_shape=_sds((nb, S, Ko), BF16),
                  name=name, compiler_params=_params(("arbitrary", "parallel")))(x3, *ws, *deps)


def _mm3_wgrad(at3, b3, tn, name):
    nb, M, T = at3.shape
    N = b3.shape[2]

    def body(a_ref, b_ref, o_ref):
        o_ref[...] = jnp.dot(a_ref[...], b_ref[...], preferred_element_type=F32).astype(BF16)

    return _pcall(body, grid=(nb, N // tn),
                  in_specs=[pl.BlockSpec((None, M, T), lambda n, j: (n, 0, 0)), pl.BlockSpec((None, T, tn), lambda n, j: (n, 0, j))],
                  out_specs=pl.BlockSpec((None, M, tn), lambda n, j: (n, 0, j)), out_shape=_sds((nb, M, N), BF16),
                  name=name, compiler_params=_params(("arbitrary", "parallel")))(at3, b3)


def _rsum(v):
    return jnp.sum(v, axis=0, keepdims=True)


def _rmean(v):
    return jnp.mean(v, axis=-1, keepdims=True)


def _gelu(x):
    t = jnp.tanh(_G0 * (x + _G1 * (x * x * x)))
    return x * (0.5 * (1.0 + t)), t


def _dgelu(x, t):
    return 0.5 * (1.0 + t) + 0.5 * x * (1.0 - t * t) * (_G0 * (1.0 + 3.0 * _G1 * (x * x)))


def _sigmoid(x):
    return 0.5 * jnp.tanh(0.5 * x) + 0.5


def _fill_shifted(ext, rot):
    v = ext[...]
    n = v.shape[0]
    for b in range(1, 8):
        rot[b - 1] = pltpu.roll(v, n - b, 0)


def _rows_at(ext, rot, s, tm, cs=slice(None)):
    a, b = divmod(s, 8)
    return ext[8 * a:8 * a + tm, cs] if b == 0 else rot[b - 1, 8 * a:8 * a + tm, cs]


def _causal_conv(w_ref, taps, bias, ext, rot, offset, tm, out):
    D = out.shape[1]
    for cb in range(D // LANE):
        cs = slice(cb * LANE, (cb + 1) * LANE)
        acc = None
        for k, o in zip(taps, offset):
            term = w_ref[k:k + 1, cs] * _rows_at(ext, rot, o, tm, cs)
            acc = term if acc is None else acc + term
        out[:, cs] = acc if bias is None else acc + bias[:, cs]


def _rows(*vs):
    a = jnp.stack([v.astype(F32) for v in vs])
    return jnp.pad(a, ((0, 8 - len(vs)), (0, 0)))


def _row_spec(tm, D):
    return pl.BlockSpec((tm, D), lambda i: (i, 0))


def _const_spec(shape):
    nd = len(shape)
    return pl.BlockSpec(shape, lambda i: (0,) * nd)


def _norm_fwd(xp, f, vec, name, deps=()):
    S, D = xp.shape
    tm = min(512, S)
    has_f = f is not None

    def body(*refs):
        if has_f:
            xp_ref, f_ref, vec_ref, xo_ref, h_ref, ht_ref = refs
            x = xp_ref[...] + vec_ref[0:1, :] * f_ref[...]
            xo_ref[...] = x
        else:
            xp_ref, vec_ref, h_ref, ht_ref = refs
            x = xp_ref[...]
        r = lax.rsqrt(_rmean(x * x) + EPS)
        h = (x * r) * vec_ref[1:2, :]
        h = h * (1.0 + vec_ref[2:3, :]) + vec_ref[3:4, :]
        h_ref[...] = h.astype(BF16)
        ht_ref[...] = h.T.astype(BF16)

    rs = _row_spec(tm, D)
    ins = [xp, f, vec] if has_f else [xp, vec]
    in_specs = ([rs, rs] if has_f else [rs]) + [_const_spec((8, D))]
    out_shape = ([_sds((S, D), F32)] if has_f else []) + [_sds((S, D), BF16), _sds((D, S), BF16)]
    out_specs = [rs] * (len(out_shape) - 1) + [pl.BlockSpec((D, tm), lambda i: (0, i))]
    outs = _pcall(_after(body, len(ins), deps), grid=(S // tm,), in_specs=in_specs + [ANY] * len(deps),
                  out_specs=out_specs, out_shape=out_shape, name=name,
                  compiler_params=_params(("parallel",)))(*ins, *deps)
    return (outs[0], outs[1], outs[2]) if has_f else (xp, outs[0], outs[1])


def _mm_resid_norm(a, w3, xprev, vec, tm, name, deps=()):
    S, K = a.shape
    D = w3.shape[2]

    def body(a_ref, w_ref, xp_ref, vec_ref, p_ref, xo_ref, h_ref, ht_ref):
        p = jnp.dot(a_ref[...], w_ref[...], preferred_element_type=F32)
        p_ref[...] = p
        x = xp_ref[...] + vec_ref[0:1, :] * p
        xo_ref[...] = x
        r = lax.rsqrt(_rmean(x * x) + EPS)
        h = (x * r) * vec_ref[1:2, :]
        h = h * (1.0 + vec_ref[2:3, :]) + vec_ref[3:4, :]
        h_ref[...] = h.astype(BF16)
        ht_ref[...] = h.T.astype(BF16)

    rs = _row_spec(tm, D)
    return _pcall(_after(body, 4, deps), grid=(S // tm,),
                  in_specs=[_row_spec(tm, K), pl.BlockSpec((None, K, D), lambda i: (0, 0, 0)), rs, _const_spec((8, D))]
                  + [ANY] * len(deps),
                  out_specs=[rs, rs, rs, pl.BlockSpec((D, tm), lambda i: (0, i))],
                  out_shape=[_sds((S, D), F32), _sds((S, D), F32), _sds((S, D), BF16), _sds((D, S), BF16)], name=name,
                  compiler_params=_params(("parallel",)))(a, w3, xprev, vec, *deps)


def _mixer_fwd(z, wsh, sgu_ln, wtril, bias_full, cw, cvec, name, deps=()):
    S = z.shape[0]
    D = wsh.shape[1]
    tm = CHUNK

    def body(z_ref, wsh_ref, sln_ref, wt_ref, bias_ref, cw_ref, cv_ref, oa_ref, ob_ref, oc_ref, t_ref,
             conv_ref, pe, ge, gr, cbuf):
        i = pl.program_id(0)

        @pl.when(i == 0)
        def _():
            pe[0:HALO, :] = jnp.zeros((HALO, D), F32)
            ge[0:HALO, :] = jnp.zeros((HALO, D), F32)

        def col(n):
            return z_ref[:, n * D:(n + 1) * D].astype(F32)

        pe[HALO:HALO + tm, :] = col(1) * col(2)
        q = wsh_ref[0:1, :] * pe[HALO - 2:HALO - 2 + tm, :]
        q = q + wsh_ref[1:2, :] * pe[HALO - 1:HALO - 1 + tm, :]
        q = q + wsh_ref[2:3, :] * pe[HALO:HALO + tm, :]
        act_a = col(0) * q
        oa_ref[...] = act_a.astype(BF16)
        t_ref[0] = act_a.T.astype(BF16)
        gu, _ = _gelu(col(3))
        gv, _ = _gelu(col(4))
        d = gv - _rmean(gv)
        nrm = d * lax.rsqrt(_rmean(d * d) + EPS)
        vnb = (nrm * sln_ref[0:1, :] + sln_ref[1:2, :]).astype(BF16)
        for g in range(NG):
            cs = slice(g * LANE, (g + 1) * LANE)
            mixed = jnp.dot(wt_ref[g], vnb[:, cs], preferred_element_type=F32) + bias_ref[:, cs]
            act_b = gu[:, cs] * mixed
            ob_ref[:, cs] = act_b.astype(BF16)
            t_ref[1, cs, :] = act_b.T.astype(BF16)
        ge[HALO:HALO + tm, :] = col(5) * _sigmoid(col(6))
        _fill_shifted(ge, gr)
        o0 = HALO - (CFM_K - 1)
        _causal_conv(cw_ref, range(CFM_K), cv_ref[0:1, :], ge, gr, range(o0, o0 + CFM_K), tm, cbuf)
        conv = cbuf[...]
        conv_ref[...] = conv.astype(BF16)
        d = conv - _rmean(conv)
        ln = (d * lax.rsqrt(_rmean(d * d) + EPS)) * cv_ref[1:2, :] + cv_ref[2:3, :]
        act_c = ln * _sigmoid(ln)
        oc_ref[...] = act_c.astype(BF16)
        t_ref[2] = act_c.T.astype(BF16)
        pe[0:HALO, :] = pe[tm:tm + HALO, :]
        ge[0:HALO, :] = ge[tm:tm + HALO, :]

    rs = _row_spec(tm, D)
    outs = _pcall(
        _after(body, 7, deps), grid=(S // tm,),
        in_specs=[pl.BlockSpec((tm, 7 * D), lambda i: (i, 0)), _const_spec((8, D)), _const_spec((8, D)),
                  _const_spec((NG, CHUNK, CHUNK)), _const_spec((CHUNK, D)), _const_spec((HALO, D)), _const_spec((8, D))]
        + [ANY] * len(deps),
        out_specs=[rs, rs, rs, pl.BlockSpec((3, D, tm), lambda i: (0, 0, i)), rs],
        out_shape=[_sds((S, D), BF16)] * 3 + [_sds((3, D, S), BF16), _sds((S, D), BF16)],
        scratch_shapes=[pltpu.VMEM((HALO + tm, D), F32), pltpu.VMEM((HALO + tm, D), F32),
                        pltpu.VMEM((7, HALO + tm, D), F32), pltpu.VMEM((tm, D), F32)],
        name=name, compiler_params=_params(("arbitrary",)))(z, wsh, sgu_ln, wtril, bias_full, cw, cvec, *deps)
    return outs[:3], outs[3], outs[4]


def _branch_out(acts, ws, z, name):
    S, D = acts[0].shape
    tm = min(512, S)

    def body(a0, a1, a2, w0, w1, w2, g0, g1, g2, m_ref, mt_ref, y_ref):
        m = None
        for n, (a, w, g) in enumerate(((a0, w0, g0), (a1, w1, g1), (a2, w2, g2))):
            y = jnp.dot(a[...], w[...], preferred_element_type=F32)
            y_ref[n] = y.astype(BF16)
            t = _sigmoid(g[...].astype(F32)) * y
            m = t if m is None else m + t
        m_ref[...] = m.astype(BF16)
        mt_ref[...] = m.T.astype(BF16)

    rs = _row_spec(tm, D)
    gate_specs = [pl.BlockSpec((tm, D), functools.partial(lambda i, n: (i, 7 + n), n=n)) for n in range(3)]
    return _pcall(body, grid=(S // tm,),
                  in_specs=[rs, rs, rs] + [_const_spec((D, D))] * 3 + gate_specs,
                  out_specs=[rs, pl.BlockSpec((D, tm), lambda i: (0, i)), pl.BlockSpec((3, tm, D), lambda i: (0, i, 0))],
                  out_shape=[_sds((S, D), BF16), _sds((D, S), BF16), _sds((3, S, D), BF16)], name=name,
                  compiler_params=_params(("parallel",)))(*acts, *ws, z, z, z)


def _ffn_in_swiglu(h2, w3, tm, tn, name):
    S, D = h2.shape
    F = w3.shape[2] // 2
    nj = F // tn

    def body(a_ref, wg_ref, wu_ref, gu_ref, act_ref, actt_ref):
        a = a_ref[...]
        g = jnp.dot(a, wg_ref[...], preferred_element_type=F32)
        u = jnp.dot(a, wu_ref[...], preferred_element_type=F32)
        gu_ref[0] = g.astype(BF16)
        gu_ref[1] = u.astype(BF16)
        act = (g * _sigmoid(g)) * u
        act_ref[...] = act.astype(BF16)
        actt_ref[...] = act.T.astype(BF16)

    return _pcall(body, grid=(S // tm, nj),
                  in_specs=[pl.BlockSpec((tm, D), lambda i, j: (i, 0)), pl.BlockSpec((None, D, tn), lambda i, j: (0, 0, j)),
                            pl.BlockSpec((None, D, tn), lambda i, j: (0, 0, j + nj))],
                  out_specs=[pl.BlockSpec((2, tm, tn), lambda i, j: (0, i, j)), pl.BlockSpec((tm, tn), lambda i, j: (i, j)),
                             pl.BlockSpec((tn, tm), lambda i, j: (j, i))],
                  out_shape=[_sds((2, S, F), BF16), _sds((S, F), BF16), _sds((F, S), BF16)], name=name,
                  compiler_params=_params(("parallel", "parallel")))(h2, w3, w3)


def _swiglu_bwd(dact, gu, name):
    _, S, F = gu.shape
    F2 = 2 * F
    tm = min(256, S)

    def body(d_ref, g_ref, u_ref, o_ref):
        g = g_ref[...].astype(F32)
        sg = _sigmoid(g)
        d = d_ref[...].astype(F32)
        o_ref[:, 0:F] = (d * u_ref[...].astype(F32) * (sg * (1.0 + g * (1.0 - sg)))).astype(BF16)
        o_ref[:, F:2 * F] = (d * (g * sg)).astype(BF16)

    return _pcall(body, grid=(S // tm,),
                  in_specs=[pl.BlockSpec((tm, F), lambda i: (i, 0)), pl.BlockSpec((None, tm, F), lambda i: (0, i, 0)),
                            pl.BlockSpec((None, tm, F), lambda i: (1, i, 0))],
                  out_specs=pl.BlockSpec((tm, F2), lambda i: (i, 0)), out_shape=_sds((S, F2), BF16), name=name,
                  compiler_params=_params(("parallel",)))(dact, gu, gu)


def _final_bwd(x1, act, w3, tgt, vec, name):
    S, D = x1.shape
    K = act.shape[1]
    tm = min(512, S)

    def body(x_ref, a_ref, w_ref, t_ref, vec_ref, dx_ref, df_ref, sums_ref, loss_ref):
        @pl.when(pl.program_id(0) == 0)
        def _():
            sums_ref[...] = jnp.zeros_like(sums_ref)
            loss_ref[...] = jnp.zeros_like(loss_ref)

        gate, fg = vec_ref[0:1, :], vec_ref[1:2, :]
        fv = jnp.dot(a_ref[...], w_ref[...], preferred_element_type=F32)
        x = x_ref[...] + gate * fv
        r = lax.rsqrt(_rmean(x * x) + EPS)
        xn = x * r
        diff = xn * fg - t_ref[...]
        per_tok = _rmean(diff * diff)
        loss_ref[...] += 0.5 * jnp.sum(per_tok, axis=0, keepdims=True)
        dy = diff * (1.0 / D)
        sums_ref[0:1, :] += _rsum(dy * xn)
        dxn = dy * fg
        dx = r * (dxn - xn * _rmean(dxn * xn))
        sums_ref[1:2, :] += _rsum(dx * fv)
        dx_ref[...] = dx
        df_ref[...] = (dx * gate).astype(BF16)

    rs = _row_spec(tm, D)
    return _pcall(body, grid=(S // tm,),
                  in_specs=[rs, _row_spec(tm, K), pl.BlockSpec((None, K, D), lambda i: (0, 0, 0)), rs, _const_spec((8, D))],
                  out_specs=[rs, rs, _const_spec((8, D)), _const_spec((8, LANE))],
                  out_shape=[_sds((S, D), F32), _sds((S, D), BF16), _sds((8, D), F32), _sds((8, LANE), F32)],
                  name=name, compiler_params=_params(("arbitrary",)))(x1, act, w3, tgt, vec)


def _norm_bwd(xin, dh, dxup, vec, fprev, name, deps=()):
    S, D = xin.shape
    has_prev = fprev is not None
    fused = isinstance(dh, tuple)
    tm = min(512, S)
    n_dh = 2 if fused else 1
    G, Nb = (dh[1].shape[0], dh[1].shape[2]) if fused else (1, 0)
    bps = 1 if G == 1 else 2
    nk = G // bps

    def body(*refs):
        x_ref, dh_refs, (up_ref, vec_ref) = refs[0], refs[1:1 + n_dh], refs[1 + n_dh:3 + n_dh]
        rest = refs[3 + n_dh:]
        if has_prev:
            fp_ref, dx_ref, dp_ref, sums_ref = rest[:4]
        else:
            dx_ref, sums_ref = rest[:2]
        k = pl.program_id(1)

        @pl.when((pl.program_id(0) == 0) & (k == 0))
        def _():
            sums_ref[...] = jnp.zeros_like(sums_ref)

        def finish(dhv):
            g, scale = vec_ref[0:1, :], vec_ref[1:2, :]
            x = x_ref[...]
            r = lax.rsqrt(_rmean(x * x) + EPS)
            xn = x * r
            sums_ref[0:1, :] += _rsum(dhv)
            sums_ref[1:2, :] += _rsum(dhv * (xn * g))
            dm = dhv * (1.0 + scale)
            sums_ref[2:3, :] += _rsum(dm * xn)
            dxn = dm * g
            dx = up_ref[...] + r * (dxn - xn * _rmean(dxn * xn))
            dx_ref[...] = dx
            if has_prev:
                sums_ref[3:4, :] += _rsum(dx * fp_ref[...])
                dp_ref[...] = (dx * vec_ref[2:3, :]).astype(BF16)

        if not fused:
            finish(dh_refs[0][...])
        elif nk == 1:
            finish(lax.dot_general(dh_refs[0][...], dh_refs[1][...], (((1,), (1,)), ((), ())), preferred_element_type=F32))
        else:
            a_ref, b_ref, acc = dh_refs[0], dh_refs[1], rest[-1]
            part = None
            for j in range(bps):
                t = lax.dot_general(a_ref[:, j * Nb:(j + 1) * Nb], b_ref[j], (((1,), (1,)), ((), ())),
                                    preferred_element_type=F32)
                part = t if part is None else part + t

            @pl.when(k == 0)
            def _():
                acc[...] = part

            @pl.when(k > 0)
            def _():
                acc[...] += part

            @pl.when(k == nk - 1)
            def _():
                finish(acc[...])

    rs = pl.BlockSpec((tm, D), lambda i, k: (i, 0))
    vs = pl.BlockSpec((8, D), lambda i, k: (0, 0))
    if not fused:
        dh_ins, dh_specs = [dh], [rs]
    elif nk == 1:
        dh_ins, dh_specs = list(dh), [pl.BlockSpec((tm, Nb), lambda i, k: (i, 0)),
                                      pl.BlockSpec((None, D, Nb), lambda i, k: (0, 0, 0), pipeline_mode=pl.Buffered(1))]
    else:
        dh_ins, dh_specs = list(dh), [pl.BlockSpec((tm, bps * Nb), lambda i, k: (i, k)),
                                      pl.BlockSpec((bps, D, Nb), lambda i, k: (k, 0, 0))]
    ins = [xin, *dh_ins, dxup, vec] + ([fprev] if has_prev else [])
    in_specs = [rs, *dh_specs, rs, vs] + ([rs] if has_prev else [])
    out_shape = [_sds((S, D), F32)] + ([_sds((S, D), BF16)] if has_prev else []) + [_sds((8, D), F32)]
    out_specs = [rs] + ([rs] if has_prev else []) + [vs]
    outs = _pcall(_after(body, len(ins), deps), grid=(S // tm, nk), in_specs=in_specs + [ANY] * len(deps),
                  out_specs=out_specs, out_shape=out_shape, name=name,
                  scratch_shapes=[pltpu.VMEM((tm, D), F32)] if nk > 1 else [],
                  compiler_params=_params(("arbitrary", "arbitrary")))(*ins, *deps)
    return (outs[0], outs[1], outs[2]) if has_prev else (outs[0], None, outs[1])


def _gate_bwd(dmerged, z, ys, name, deps=()):
    S, D = dmerged.shape
    tm = min(512, S)
    ncol = z.shape[1] // D

    def body(dm_ref, g_ref, y_ref, dy_ref, dz_ref):
        sg = _sigmoid(g_ref[...].astype(F32))
        dm = dm_ref[...].astype(F32)
        dy_ref[...] = (dm * sg).astype(BF16)
        dz_ref[...] = (dm * y_ref[...].astype(F32) * (sg * (1.0 - sg))).astype(BF16)

    branch = pl.BlockSpec((None, tm, D), lambda i, n: (n, i, 0))
    return _pcall(_after(body, 3, deps), grid=(S // tm, 3),
                  in_specs=[pl.BlockSpec((tm, D), lambda i, n: (i, 0)), pl.BlockSpec((tm, D), lambda i, n: (i, 7 + n)),
                            branch] + [ANY] * len(deps),
                  out_specs=[branch, pl.BlockSpec((tm, D), lambda i, n: (i, 7 + n))],
                  out_shape=[_sds((3, S, D), BF16), _sds((S, ncol * D), BF16)], name=name,
                  compiler_params=_params(("parallel", "arbitrary")))(dmerged, z, ys, *deps)


def _mixer_bwd(z, dacts, conv, dz, wsh, sgu_ln, wtril, wtril_t, bias_full, cw, cvec, name):
    S = z.shape[0]
    D = wsh.shape[1]
    tm = CHUNK
    nt = S // tm
    hb = tm // HALO

    def body(zc, zp, da_ref, db_ref, dc_ref, conv_ref, wsh_ref, sln_ref, wt_ref, wtt_ref, bias_ref, cw_ref, cv_ref, _dz_in,
             dz_ref, vec_ref, dcw_ref, dws_ref, dbs_ref, pe, ge, dqe, dce, gr, dcr, cbuf, dcw8):
        i = pl.program_id(0)
        rb = nt - 1 - i

        @pl.when(i == 0)
        def _():
            vec_ref[...] = jnp.zeros_like(vec_ref)
            dcw8[...] = jnp.zeros_like(dcw8)
            dws_ref[...] = jnp.zeros_like(dws_ref)
            dbs_ref[...] = jnp.zeros_like(dbs_ref)
            dqe[tm:tm + HALO, :] = jnp.zeros((HALO, D), F32)
            dce[tm:tm + HALO, :] = jnp.zeros((HALO, D), F32)

        keep = (rb > 0).astype(F32)

        def col(n):
            return zc[:, n * D:(n + 1) * D].astype(F32)

        def pcol(n):
            return zp[:, n * D:(n + 1) * D].astype(F32)

        c_a, x_a = col(1), col(2)
        pe[0:HALO, :] = keep * (pcol(1) * pcol(2))
        pe[HALO:HALO + tm, :] = c_a * x_a
        q = wsh_ref[0:1, :] * pe[HALO - 2:HALO - 2 + tm, :]
        q = q + wsh_ref[1:2, :] * pe[HALO - 1:HALO - 1 + tm, :]
        q = q + wsh_ref[2:3, :] * pe[HALO:HALO + tm, :]
        dact = da_ref[...].astype(F32)
        dz_ref[:, 0:D] = (dact * q).astype(BF16)
        dq = dact * col(0)
        dqe[0:tm, :] = dq
        dp = wsh_ref[2:3, :] * dq + wsh_ref[1:2, :] * dqe[1:1 + tm, :] + wsh_ref[0:1, :] * dqe[2:2 + tm, :]
        dz_ref[:, D:2 * D] = (dp * x_a).astype(BF16)
        dz_ref[:, 2 * D:3 * D] = (dp * c_a).astype(BF16)
        for k in range(SHORT_K):
            o = HALO - (SHORT_K - 1) + k
            vec_ref[k:k + 1, :] += _rsum(dq * pe[o:o + tm, :])
        u, v = col(3), col(4)
        gu, tu = _gelu(u)
        gv, tv = _gelu(v)
        d = gv - _rmean(gv)
        rstd = lax.rsqrt(_rmean(d * d) + EPS)
        nrm = d * rstd
        vnb = (nrm * sln_ref[0:1, :] + sln_ref[1:2, :]).astype(BF16)
        dact = db_ref[...].astype(F32)
        dvn_parts, dgu_parts = [], []
        for g in range(NG):
            cs = slice(g * LANE, (g + 1) * LANE)
            vg = vnb[:, cs]
            mixed = jnp.dot(wt_ref[g], vg, preferred_element_type=F32) + bias_ref[:, cs]
            dgu_parts.append(dact[:, cs] * mixed)
            dmixed = dact[:, cs] * gu[:, cs]
            dmb = dmixed.astype(BF16)
            dws_ref[g] += lax.dot_general(dmb, vg, (((1,), (1,)), ((), ())), preferred_element_type=F32)
            dbs_ref[g] += jnp.broadcast_to(jnp.sum(dmixed, axis=1, keepdims=True), (CHUNK, LANE))
            dvn_parts.append(jnp.dot(wtt_ref[g], dmb, preferred_element_type=F32))
        dgu = jnp.concatenate(dgu_parts, axis=1)
        dvn = jnp.concatenate(dvn_parts, axis=1)
        dz_ref[:, 3 * D:4 * D] = (dgu * _dgelu(u, tu)).astype(BF16)
        vec_ref[3:4, :] += _rsum(dvn * nrm)
        vec_ref[4:5, :] += _rsum(dvn)
        dn = dvn * sln_ref[0:1, :]
        dgv = rstd * (dn - _rmean(dn) - nrm * _rmean(dn * nrm))
        dz_ref[:, 4 * D:5 * D] = (dgv * _dgelu(v, tv)).astype(BF16)
        a_c = col(5)
        sg = _sigmoid(col(6))
        ge[0:HALO, :] = keep * (pcol(5) * _sigmoid(pcol(6)))
        ge[HALO:HALO + tm, :] = a_c * sg
        _fill_shifted(ge, gr)
        o0 = HALO - (CFM_K - 1)
        conv = conv_ref[...].astype(F32)
        d = conv - _rmean(conv)
        rstd = lax.rsqrt(_rmean(d * d) + EPS)
        nrm = d * rstd
        ln = nrm * cv_ref[1:2, :] + cv_ref[2:3, :]
        sl = _sigmoid(ln)
        dln = dc_ref[...].astype(F32) * (sl * (1.0 + ln * (1.0 - sl)))
        vec_ref[6:7, :] += _rsum(dln * nrm)
        vec_ref[7:8, :] += _rsum(dln)
        dn = dln * cv_ref[1:2, :]
        dconv = rstd * (dn - _rmean(dn) - nrm * _rmean(dn * nrm))
        vec_ref[5:6, :] += _rsum(dconv)
        dce[0:tm, :] = dconv
        _fill_shifted(dce, dcr)
        _causal_conv(cw_ref, range(CFM_K), None, dce, dcr, [CFM_K - 1 - k for k in range(CFM_K)], tm, cbuf)
        dglu = cbuf[...]
        for cb in range(D // LANE):
            cs = slice(cb * LANE, (cb + 1) * LANE)
            dcv = dce[0:tm, cs]
            for k in range(CFM_K):
                prod = dcv * _rows_at(ge, gr, o0 + k, tm, cs)
                dcw8[k, :, cs] += jnp.sum(prod.reshape(tm // 8, 8, LANE), axis=0)

        @pl.when(i == nt - 1)
        def _():
            dcw_ref[...] = jnp.sum(dcw8[...], axis=1)
        dz_ref[:, 5 * D:6 * D] = (dglu * sg).astype(BF16)
        dz_ref[:, 6 * D:7 * D] = (dglu * a_c * (sg * (1.0 - sg))).astype(BF16)
        dqe[tm:tm + HALO, :] = dqe[0:HALO, :]
        dce[tm:tm + HALO, :] = dce[0:HALO, :]

    rev = lambda i: (nt - 1 - i, 0)
    rs = pl.BlockSpec((tm, D), rev)
    cur = pl.BlockSpec((tm, 7 * D), rev)
    prev = pl.BlockSpec((HALO, 7 * D), lambda i: (jnp.maximum((nt - 1 - i) * hb - 1, 0), 0))
    ext = pltpu.VMEM((HALO + tm, D), F32)
    outs = _pcall(
        body, grid=(nt,),
        in_specs=[cur, prev] + [pl.BlockSpec((None, tm, D), functools.partial(lambda i, n: (n, nt - 1 - i, 0), n=n))
                                for n in range(3)]
        + [rs, _const_spec((8, D)), _const_spec((8, D)), _const_spec((NG, CHUNK, CHUNK)),
                  _const_spec((NG, CHUNK, CHUNK)), _const_spec((CHUNK, D)), _const_spec((HALO, D)), _const_spec((8, D)),
                  ANY],
        out_specs=[cur, _const_spec((8, D)), _const_spec((HALO, D)), _const_spec((NG, CHUNK, CHUNK)),
                   _const_spec((NG, CHUNK, LANE))],
        out_shape=[_sds(dz.shape, BF16), _sds((8, D), F32), _sds((HALO, D), F32), _sds((NG, CHUNK, CHUNK), F32),
                   _sds((NG, CHUNK, LANE), F32)],
        scratch_shapes=[ext, ext, ext, ext, pltpu.VMEM((7, HALO + tm, D), F32), pltpu.VMEM((7, HALO + tm, D), F32),
                        pltpu.VMEM((tm, D), F32), pltpu.VMEM((HALO, 8, D), F32)],
        input_output_aliases={13: 0}, name=name,
        compiler_params=_params(("arbitrary",)))(z, z, dacts, dacts, dacts, conv, wsh, sgu_ln, wtril, wtril_t, bias_full, cw,
                                                 cvec, dz)
    return outs


def _ada_fwd(c_all, w_ada_loc, name):
    nb, D = c_all.shape
    L, _, nc = w_ada_loc.shape

    def body(c_ref, w_ref, o_ref, ca_ref):
        cv = c_ref[...]
        ca = cv * _sigmoid(cv)
        ca_ref[...] = ca
        o_ref[...] = jnp.dot(ca.astype(BF16), w_ref[...].astype(BF16), preferred_element_type=F32)

    return _pcall(body, grid=(L,),
                  in_specs=[_const_spec((nb, D)), pl.BlockSpec((None, D, nc), lambda l: (l, 0, 0))],
                  out_specs=[pl.BlockSpec((None, nb, nc), lambda l: (l, 0, 0)), _const_spec((nb, D))],
                  out_shape=[_sds((L, nb, nc), F32), _sds((nb, D), F32)], name=name,
                  compiler_params=_params(("arbitrary",)))(c_all, w_ada_loc)


def _adamw(w, g, m, v):
    m = ADAM_B1 * m + (1.0 - ADAM_B1) * g
    v = ADAM_B2 * v + (1.0 - ADAM_B2) * (g * g)
    m_hat = m / (1.0 - ADAM_B1 ** ADAM_STEP)
    v_hat = v / (1.0 - ADAM_B2 ** ADAM_STEP)
    delta = -ADAM_LR * (m_hat / (jnp.sqrt(v_hat) + ADAM_EPS) + ADAM_WD * w)
    return delta, m, v


def _tile_rows(R, C, align=8):
    cap = max(align, (1536 * 1024) // (4 * C))
    best = None
    for t in range(align, R + 1, align):
        if R % t == 0 and t <= cap:
            best = t
    return R if best is None else best


def _adam_ada(ct, dm, w, m, v, name):
    L, D, nc = w.shape
    nb = ct.shape[1]
    tr = _tile_rows(D, nc)

    def body(ct_ref, dm_ref, w_ref, m_ref, v_ref, g_ref, d_ref, mo_ref, vo_ref):
        g = ct_ref[:, 0:1] * dm_ref[0:1, :]
        for b in range(1, nb):
            g = g + ct_ref[:, b:b + 1] * dm_ref[b:b + 1, :]
        g_ref[...] = g
        d_ref[...], mo_ref[...], vo_ref[...] = _adamw(w_ref[...], g, m_ref[...], v_ref[...])

    ws = pl.BlockSpec((None, tr, nc), lambda l, r: (l, r, 0))
    return _pcall(body, grid=(L, D // tr),
                  in_specs=[pl.BlockSpec((tr, nb), lambda l, r: (r, 0)), pl.BlockSpec((None, nb, nc), lambda l, r: (l, 0, 0)),
                            ws, ws, ws],
                  out_specs=[ws] * 4, out_shape=[_sds(w.shape, F32)] * 4, name=name,
                  compiler_params=_params(("parallel", "parallel")))(ct, dm, w, m, v)


def _adam_small(parts, w, m, v, name, deps=(), single_rows=()):
    n, R, C = parts.shape
    tr = _tile_rows(R, C * n // 2)
    nl = len(single_rows[0]) if single_rows else 0

    def body(p_ref, w_ref, m_ref, v_ref, g_ref, d_ref, mo_ref, vo_ref, *single):
        g = p_ref[0]
        for j in range(1, n):
            g = g + p_ref[j]
        d, mo, vo = _adamw(w_ref[...], g, m_ref[...], v_ref[...])
        g_ref[...], d_ref[...], mo_ref[...], vo_ref[...] = g, d, mo, vo
        step = pl.program_id(0)
        for pi, rows in enumerate(single_rows):
            for l, row in enumerate(rows):
                @pl.when(step == row // tr)
                def _(pi=pi, l=l, off=row % tr):
                    for k, val in enumerate((g, d, mo, vo)):
                        single[4 * pi + k][l:l + 1, :] = val[off:off + 1, :]

    ws = pl.BlockSpec((tr, C), lambda r: (r, 0))
    one = pl.BlockSpec((nl, C), lambda r: (0, 0))
    outs = _pcall(_after(body, 4, deps), grid=(R // tr,),
                  in_specs=[pl.BlockSpec((n, tr, C), lambda r: (0, r, 0)), ws, ws, ws] + [ANY] * len(deps),
                  out_specs=[ws] * 4 + [one] * (4 * len(single_rows)),
                  out_shape=[_sds((R, C), F32)] * 4 + [_sds((nl, C), F32)] * (4 * len(single_rows)), name=name,
                  compiler_params=_params(("arbitrary",)))(parts, w, m, v, *deps)
    return outs[:4], outs[4:]


def _adam_plain(g, w, m, v, name):
    R, C = w.shape

    def body(g_ref, w_ref, m_ref, v_ref, d_ref, mo_ref, vo_ref):
        d_ref[...], mo_ref[...], vo_ref[...] = _adamw(w_ref[...], g_ref[...], m_ref[...], v_ref[...])

    ws = _const_spec((R, C))
    return _pcall(body, grid=(1,), in_specs=[ws] * 4, out_specs=[ws] * 3, out_shape=[_sds((R, C), F32)] * 3, name=name,
                  compiler_params=_params(("arbitrary",)))(g, w, m, v)


def _pair_sum(G, R1, my_c, name):
    n, R, C = G.shape
    half = n // 2
    tr = _tile_rows(R, C, align=16)

    def body(c_ref, g_ref, r_ref, o_ref):
        o_ref[...] = (g_ref[...].astype(F32) + r_ref[...].astype(F32)).astype(o_ref.dtype)

    blk = (None, tr, C)
    gs = pltpu.PrefetchScalarGridSpec(
        num_scalar_prefetch=1, grid=(half, R // tr),
        in_specs=[pl.BlockSpec(blk, lambda p, r, c: (2 * p + c[0], r, 0)), pl.BlockSpec(blk, lambda p, r, c: (p, r, 0))],
        out_specs=pl.BlockSpec(blk, lambda p, r, c: (p, r, 0)))
    return _pcall(body, grid_spec=gs, out_shape=_sds((half, R, C), G.dtype), name=name,
                  compiler_params=_params(("parallel", "parallel")))(my_c, G, R1)


def _adam_big(P, R2, my_chip, w, m, v, layer, prev, name, deps=()):
    _, R, C = P.shape
    nrecv = R2.shape[0]
    tr = _tile_rows(R, C, align=16)

    def body(p_sm, p_ref, r_ref, w_ref, m_ref, v_ref, *rest):
        g_ref, d_ref, mo_ref, vo_ref = rest[-4:]
        g = p_ref[...].astype(F32)
        for k in range(nrecv):
            g = g + r_ref[k].astype(F32)
        g_ref[...] = g
        d_ref[...], mo_ref[...], vo_ref[...] = _adamw(w_ref[...], g, m_ref[...], v_ref[...])

    ws = pl.BlockSpec((None, tr, C), lambda r, p: (layer, r, 0))
    held = [] if prev is None else list(prev)
    gs = pltpu.PrefetchScalarGridSpec(
        num_scalar_prefetch=1, grid=(R // tr,),
        in_specs=[pl.BlockSpec((None, tr, C), lambda r, p: (p[0], r, 0)),
                  pl.BlockSpec((nrecv, tr, C), lambda r, p: (0, r, 0)), ws, ws, ws] + [ANY] * (len(held) + len(deps)),
        out_specs=[ws] * 4)
    alias = {6 + i: i for i in range(len(held))}
    return _pcall(body, grid_spec=gs, out_shape=[_sds(w.shape, F32)] * 4, name=name, input_output_aliases=alias,
                  compiler_params=_params(("parallel",)))(my_chip, P, R2, w, m, v, *held, *deps)


def _place():
    return lax.axis_index("x"), lax.axis_index("y"), lax.axis_index("c")


def _all_gather(shards, name, deps=(), into=None):
    n = len(shards)
    bufs = [] if into is None else [b for b, _ in into]
    nb = len(bufs)

    def body(*refs):
        ins, outs = refs[:n], refs[n + nb:2 * n + nb]
        send_sems, recv_sems, local_sems = refs[2 * n + nb:]
        x, y, c = _place()
        me, sibling = (x, y, c), (x, y, 1 - c)
        chips = [(1 - x, y), (x, 1 - y), (1 - x, 1 - y)]

        def slot(a, px, py, pc):
            block = outs[a].at[4 * px + 2 * py + pc]
            return block if into is None else block.at[pl.ds(into[a][1], ins[a].shape[0])]

        def copy(a, k, block, to, src=None):
            return pltpu.make_async_remote_copy(
                src_ref=slot(a, *block) if src is None else src, dst_ref=slot(a, *block),
                send_sem=send_sems.at[7 * a + k], recv_sem=recv_sems.at[7 * a + k], device_id=to, device_id_type=MESH)

        mine = [pltpu.make_async_copy(ins[a], slot(a, *me), local_sems.at[a]) for a in range(n)]
        for cp in mine:
            cp.start()
        first = []
        for a in range(n):
            first.append(copy(a, 0, me, sibling, src=ins[a]))
            first += [copy(a, 1 + j, me, (*chip, c), src=ins[a]) for j, chip in enumerate(chips)]
        for cp in first:
            cp.start()
        passed = []
        for j, chip in enumerate(chips):
            for a in range(n):
                copy(a, 1 + j, (*chip, c), me).wait_recv()
                fwd = copy(a, 4 + j, (*chip, c), sibling)
                fwd.start()
                passed.append(fwd)
        for a in range(n):
            copy(a, 0, sibling, me).wait_recv()
        for j, chip in enumerate(chips):
            for a in range(n):
                copy(a, 4 + j, (*chip, 1 - c), me).wait_recv()
        for cp in first + passed:
            cp.wait_send()
        for cp in mine:
            cp.wait()

    out_shape = [_sds((NDEV,) + s.shape, s.dtype) for s in shards] if into is None else [_sds(b.shape, b.dtype) for b in bufs]
    outs = _pcall(_after(body, n + nb, deps), in_specs=[ANY] * (n + nb + len(deps)), out_specs=[ANY] * n,
                  out_shape=out_shape, input_output_aliases={n + a: a for a in range(nb)},
                  scratch_shapes=[pltpu.SemaphoreType.DMA((7 * n,)), pltpu.SemaphoreType.DMA((7 * n,)),
                                  pltpu.SemaphoreType.DMA((n,))], name=name)(*shards, *bufs, *deps)
    return list(outs)


HBM = pl.BlockSpec(memory_space=pltpu.HBM)
SEM = pl.BlockSpec(memory_space=pltpu.SEMAPHORE)


def _copies(plan, refs, send_sems, recv_sems):
    return [pltpu.make_async_remote_copy(src_ref=s, dst_ref=d, send_sem=send_sems.at[k], recv_sem=recv_sems.at[k],
                                         device_id=dev, device_id_type=MESH)
            for k, (s, d, dev) in enumerate(plan(refs, *_place()))]


def _xfer_start(bufs, ncopies, plan, name, deps=()):
    n = len(bufs)

    def body(*refs):
        for cp in _copies(plan, refs[:n], refs[n], refs[n + 1]):
            cp.start()
        token = refs[2 * n + 2]
        token[...] = jnp.zeros_like(token)

    outs = _pcall(
        _after(body, n, deps), name=name,
        out_shape=(pltpu.SemaphoreType.DMA((ncopies,)), pltpu.SemaphoreType.DMA((ncopies,)),
                   *[pltpu.HBM(b.shape, b.dtype) for b in bufs], _sds((8, LANE), F32)),
        in_specs=[HBM] * n + [ANY] * len(deps),
        out_specs=(SEM, SEM, *[HBM] * n, pl.BlockSpec(memory_space=pltpu.VMEM)),
        input_output_aliases={i: 2 + i for i in range(n)},
        compiler_params=pltpu.CompilerParams(has_side_effects=pltpu.SideEffectType.DATAFLOW_SIDE_EFFECTING),
    )(*[pltpu.with_memory_space_constraint(b, pltpu.HBM) for b in bufs], *deps)
    return (outs[0], outs[1]), list(outs[2:2 + n]), outs[2 + n]


def _xfer_wait(sems, bufs, plan, after, name):
    n = len(bufs)
    after = list(after) if isinstance(after, (list, tuple)) else [after]

    def body(*refs):
        for cp in _copies(plan, refs[:n], refs[n], refs[n + 1]):
            cp.wait_send()
            cp.wait_recv()

    outs = _pcall(
        body, name=name, out_shape=tuple(pltpu.HBM(b.shape, b.dtype) for b in bufs),
        in_specs=[HBM] * n + [SEM, SEM] + [ANY] * len(after), out_specs=tuple([HBM] * n),
        input_output_aliases={i: i for i in range(n)},
        compiler_params=pltpu.CompilerParams(has_side_effects=pltpu.SideEffectType.DATAFLOW_SIDE_EFFECTING),
    )(*bufs, *sems, *after)
    return list(outs)


def _chips_of(x, y):
    return [(1 - x, y), (x, 1 - y), (1 - x, 1 - y)]


def _landing(ref, dev_index, rows):
    block = ref.at[dev_index]
    return block if rows is None else block.at[pl.ds(rows[0], rows[1])]


def _gather_plan1(n, rows=None):
    def plan(refs, x, y, c):
        out = []
        for a in range(n):
            blk = _landing(refs[a], 4 * x + 2 * y + c, rows)
            out.append((blk, blk, (x, y, 1 - c)))
            out += [(blk, blk, (px, py, c)) for px, py in _chips_of(x, y)]
        return out
    return plan


def _gather_plan2(n, rows=None):
    def plan(refs, x, y, c):
        out = []
        for a in range(n):
            for px, py in _chips_of(x, y):
                blk = _landing(refs[a], 4 * px + 2 * py + c, rows)
                out.append((blk, blk, (x, y, 1 - c)))
        return out
    return plan


def _gather_start(shards, dev, name, deps=(), within=None):
    rows = None if within is None else (within[0], shards[0].shape[0])
    lands = []
    for s in shards:
        shape = (NDEV,) + s.shape if within is None else (NDEV, within[1]) + s.shape[1:]
        start = (dev,) + (0,) * s.ndim if within is None else (dev, within[0]) + (0,) * (s.ndim - 1)
        lands.append(lax.dynamic_update_slice(lax.empty(shape, s.dtype), s[None], start))
    n = len(shards)
    sems, lands, tok = _xfer_start(lands, 4 * n, _gather_plan1(n, rows), name + "_p1_start", deps)
    return dict(sems=sems, lands=lands, tok=tok, n=n, rows=rows)


def _gather_mid(st, after, name):
    n, rows = st["n"], st["rows"]
    lands = _xfer_wait(st["sems"], st["lands"], _gather_plan1(n, rows), after, name + "_p1_wait")
    sems, lands, tok = _xfer_start(lands, 3 * n, _gather_plan2(n, rows), name + "_p2_start")
    return dict(sems=sems, lands=lands, tok=tok, n=n, rows=rows)


def _gather_finish(st, after, name):
    return _xfer_wait(st["sems"], st["lands"], _gather_plan2(st["n"], st["rows"]), after, name + "_p2_wait")


def _scatter_plan1(n):
    def plan(refs, x, y, c):
        return [(refs[a].at[2 * p + 1 - c], refs[n + a].at[p], (x, y, 1 - c)) for a in range(n) for p in range(NCHIP)]
    return plan


def _scatter_plan2(n):
    def plan(refs, x, y, c):
        return [(refs[a].at[2 * px + py], refs[n + a].at[j], (px, py, c))
                for a in range(n) for j, (px, py) in enumerate(_chips_of(x, y))]
    return plan


def _scatter_start(Gs, name):
    n = len(Gs)
    R1s = [lax.empty((NCHIP,) + g.shape[1:], g.dtype) for g in Gs]
    sems, bufs, tok = _xfer_start(list(Gs) + R1s, NCHIP * n, _scatter_plan1(n), name + "_s1_start")
    return dict(sems=sems, bufs=bufs, tok=tok, n=n)


def _scatter_mid(st, after, my_c, name):
    n = st["n"]
    bufs = _xfer_wait(st["sems"], st["bufs"], _scatter_plan1(n), after, name + "_s1_wait")
    Ps = [_pair_sum(bufs[a], bufs[n + a], my_c, f"{name}_pair_sum{a}") for a in range(n)]
    R2s = [lax.empty((3,) + p.shape[1:], p.dtype) for p in Ps]
    sems, bufs, tok = _xfer_start(Ps + R2s, 3 * n, _scatter_plan2(n), name + "_s2_start")
    return dict(sems=sems, bufs=bufs, tok=tok, n=n)


def _scatter_finish(st, after, name):
    n = st["n"]
    bufs = _xfer_wait(st["sems"], st["bufs"], _scatter_plan2(n), after, name + "_s2_wait")
    return bufs[:n], bufs[n:]


SMALL_ROWS = {"norm1_g": (0, 1), "norm2_g": (1, 1), "sgu_ln_g": (2, 1), "sgu_ln_b": (3, 1), "cfm_conv_b": (4, 1),
              "cfm_ln_g": (5, 1), "cfm_ln_b": (6, 1), "b_sgu": (7, 1), "w_sgu": (8, 128), "b_ada": (136, N_MOD),
              "w_short": (142, SHORT_K), "cfm_conv_w": (145, CFM_K)}
ROWS_PER_LAYER = 176
FINAL_ROW = DEPTH * ROWS_PER_LAYER
PACK_ROWS = 360


def _pack(get, D, layers=tuple(range(DEPTH)), tail=True):
    parts = []
    for l in layers:
        for name, (_, nrows) in SMALL_ROWS.items():
            a = get(name, l)
            parts.append(jnp.zeros((nrows * D,), F32) if a is None else a.astype(F32).reshape(nrows * D))
    if tail:
        for name in ("final_g", "loss"):
            a = get(name, None)
            parts.append(jnp.zeros((D,), F32) if a is None else a.astype(F32).reshape(D))
        parts.append(jnp.zeros(((PACK_ROWS - FINAL_ROW - 2) * D,), F32))
    return jnp.concatenate(parts).reshape(-1, D)


def _unpack(pack, name, shape):
    D = pack.shape[1]
    r0, nrows = SMALL_ROWS[name]
    return jnp.stack([pack[l * ROWS_PER_LAYER + r0:l * ROWS_PER_LAYER + r0 + nrows] for l in range(DEPTH)]).reshape(shape)


def _mm_tiles(S):
    return min(512, S), min(1024, S), min(2048, S)


def kernel(x, c, w_ada, b_ada, norm1_g, w_in, w_short, w_a_out, sgu_ln_g, sgu_ln_b, w_sgu, b_sgu, w_b_out, cfm_conv_w, cfm_conv_b, cfm_ln_g, cfm_ln_b, w_c_out, w_o, norm2_g, w_ffn_in, w_ffn_out, final_g, loss_target, m_w_ada, m_b_ada, m_norm1_g, m_w_in, m_w_short, m_w_a_out, m_sgu_ln_g, m_sgu_ln_b, m_w_sgu, m_b_sgu, m_w_b_out, m_cfm_conv_w, m_cfm_conv_b, m_cfm_ln_g, m_cfm_ln_b, m_w_c_out, m_w_o, m_norm2_g, m_w_ffn_in, m_w_ffn_out, m_final_g, v_w_ada, v_b_ada, v_norm1_g, v_w_in, v_w_short, v_w_a_out, v_sgu_ln_g, v_sgu_ln_b, v_w_sgu, v_b_sgu, v_w_b_out, v_cfm_conv_w, v_cfm_conv_b, v_cfm_ln_g, v_cfm_ln_b, v_w_c_out, v_w_o, v_norm2_g, v_w_ffn_in, v_w_ffn_out, v_final_g):
    W = dict(w_ada=w_ada, b_ada=b_ada, norm1_g=norm1_g, w_in=w_in, w_short=w_short, w_a_out=w_a_out, sgu_ln_g=sgu_ln_g,
             sgu_ln_b=sgu_ln_b, w_sgu=w_sgu, b_sgu=b_sgu, w_b_out=w_b_out, cfm_conv_w=cfm_conv_w, cfm_conv_b=cfm_conv_b,
             cfm_ln_g=cfm_ln_g, cfm_ln_b=cfm_ln_b, w_c_out=w_c_out, w_o=w_o, norm2_g=norm2_g, w_ffn_in=w_ffn_in,
             w_ffn_out=w_ffn_out, final_g=final_g)
    Mo = dict(w_ada=m_w_ada, b_ada=m_b_ada, norm1_g=m_norm1_g, w_in=m_w_in, w_short=m_w_short, w_a_out=m_w_a_out,
              sgu_ln_g=m_sgu_ln_g, sgu_ln_b=m_sgu_ln_b, w_sgu=m_w_sgu, b_sgu=m_b_sgu, w_b_out=m_w_b_out,
              cfm_conv_w=m_cfm_conv_w, cfm_conv_b=m_cfm_conv_b, cfm_ln_g=m_cfm_ln_g, cfm_ln_b=m_cfm_ln_b,
              w_c_out=m_w_c_out, w_o=m_w_o, norm2_g=m_norm2_g, w_ffn_in=m_w_ffn_in, w_ffn_out=m_w_ffn_out,
              final_g=m_final_g)
    Vo = dict(w_ada=v_w_ada, b_ada=v_b_ada, norm1_g=v_norm1_g, w_in=v_w_in, w_short=v_w_short, w_a_out=v_w_a_out,
              sgu_ln_g=v_sgu_ln_g, sgu_ln_b=v_sgu_ln_b, w_sgu=v_w_sgu, b_sgu=v_b_sgu, w_b_out=v_w_b_out,
              cfm_conv_w=v_cfm_conv_w, cfm_conv_b=v_cfm_conv_b, cfm_ln_g=v_cfm_ln_g, cfm_ln_b=v_cfm_ln_b,
              w_c_out=v_w_c_out, w_o=v_w_o, norm2_g=v_norm2_g, w_ffn_in=v_w_ffn_in, w_ffn_out=v_w_ffn_out,
              final_g=v_final_g)
    order = ["w_ada", "b_ada", "norm1_g", "w_in", "w_short", "w_a_out", "sgu_ln_g", "sgu_ln_b", "w_sgu", "b_sgu",
             "w_b_out", "cfm_conv_w", "cfm_conv_b", "cfm_ln_g", "cfm_ln_b", "w_c_out", "w_o", "norm2_g", "w_ffn_in",
             "w_ffn_out", "final_g"]

    assert DEPTH == 2, "the weight-gather schedule below is written for two layers"
    S, D = x.shape[1], x.shape[2]
    F2 = w_ffn_in.shape[2] * NDEV
    FF = F2 // 2
    xi, yi, ci = _place()
    dev = 4 * xi + 2 * yi + ci
    my_c = jnp.reshape(ci, (1,)).astype(jnp.int32)
    my_chip = jnp.reshape(2 * xi + yi, (1,)).astype(jnp.int32)
    tm, tm_big, tm_huge = _mm_tiles(S)
    x0 = x.reshape(S, D)
    tgt = loss_target.reshape(S, D)

    def shards_of(l):
        return [w_in[l].astype(BF16), w_a_out[l].astype(BF16), w_b_out[l].astype(BF16), w_c_out[l].astype(BF16),
                w_o[l].astype(BF16), w_ffn_in[l].astype(BF16), w_ffn_out[l].astype(BF16)]

    c_all = _all_gather([jnp.pad(c, ((0, 7), (0, 0)))], "ag_c")[0][:, 0, :]
    modpart, c_act = _ada_fwd(c_all, w_ada, "ada_fwd")
    ncol = modpart.shape[2]
    mg = _all_gather([modpart.reshape(DEPTH * NDEV, ncol)], "ag_mod")[0].reshape(NDEV, DEPTH, NDEV, ncol)
    mine = lax.dynamic_index_in_dim(mg, dev, axis=2, keepdims=False)
    mod = (jnp.transpose(mine, (1, 0, 2)).reshape(DEPTH, N_MOD * D) + b_ada).reshape(DEPTH, N_MOD, D)

    ncs = w_short.shape[2]
    ag_in0 = _gather_start([w_in[0].astype(BF16), w_short.reshape(DEPTH * SHORT_K, ncs),
                            cfm_conv_w.reshape(DEPTH * CFM_K, ncs)], dev, "ag_w_in0", deps=(mod,))
    W, Mo, Vo = lax.optimization_barrier((ag_in0["tok"], (W, Mo, Vo)))[1]
    (norm1_g, norm2_g, w_in, w_a_out, w_b_out, w_c_out, w_o, w_ffn_in, w_ffn_out, sgu_ln_g, sgu_ln_b, w_sgu, b_sgu,
     cfm_conv_b, cfm_ln_g, cfm_ln_b, final_g) = [W[k] for k in (
         "norm1_g", "norm2_g", "w_in", "w_a_out", "w_b_out", "w_c_out", "w_o", "w_ffn_in", "w_ffn_out", "sgu_ln_g",
         "sgu_ln_b", "w_sgu", "b_sgu", "cfm_conv_b", "cfm_ln_g", "cfm_ln_b", "final_g")]
    m_w_ada, v_w_ada = Mo["w_ada"], Vo["w_ada"]
    xl0, h0, ht0 = _norm_fwd(x0, None, _rows(jnp.zeros((D,), F32), norm1_g[0], mod[0, 1], mod[0, 0]), "norm1_fwd0",
                             deps=(ag_in0["tok"],))
    ag_rest0 = _gather_start(shards_of(0)[1:], dev, "ag_rest0", deps=(h0,))

    tril = jnp.tril(jnp.ones((CHUNK, CHUNK), dtype=bool))

    def layer_consts(l):
        wt = jnp.where(tril[None], w_sgu[l], 0.0).astype(BF16)
        return dict(sgu_ln=_rows(sgu_ln_g[l], sgu_ln_b[l]), wtril=wt, wtril_t=jnp.swapaxes(wt, 1, 2),
                    bias_full=jnp.repeat(b_sgu[l].T, LANE, axis=1), cvec=_rows(cfm_conv_b[l], cfm_ln_g[l], cfm_ln_b[l]))

    def rest_of(g):
        return dict(w_a=g[0].reshape(1, D, D), w_b=g[1].reshape(1, D, D), w_c=g[2].reshape(1, D, D),
                    w_o=g[3].reshape(1, D, D), w_fi=jnp.transpose(g[4], (1, 0, 2)).reshape(1, D, F2),
                    w_fo=g[5].reshape(1, FF, D))

    sharded_small = ("w_short", "cfm_conv_w")

    def param_get(T):
        def get(name, l):
            if name == "final_g":
                return T[name]
            return None if name in sharded_small or name == "loss" else T[name][l]
        return get

    packs = [_pack(param_get(T), D) for T in (W, Mo, Vo)]
    ag_in0 = _gather_mid(ag_in0, [ag_rest0["tok"], *packs], "ag_w_in0")
    (w_sgu, b_sgu, sgu_ln_g, sgu_ln_b, cfm_conv_b, cfm_ln_g, cfm_ln_b), conv_wmv_in = lax.optimization_barrier(
        (ag_in0["tok"], ((w_sgu, b_sgu, sgu_ln_g, sgu_ln_b, cfm_conv_b, cfm_ln_g, cfm_ln_b),
                         [(T["w_short"], T["cfm_conv_w"]) for T in (W, Mo, Vo)])))[1]
    consts = [layer_consts(l) for l in range(DEPTH)]
    ncr = DEPTH * (SHORT_K + CFM_K)
    padr = (-ncr) % 8
    convw_wmv = [jnp.pad(jnp.concatenate([a.reshape(-1, ncs), b.reshape(-1, ncs)]), ((0, padr), (0, 0)))
                 for a, b in conv_wmv_in]
    g_in0 = _gather_finish(ag_in0, [*convw_wmv] + [a for cl in consts for a in cl.values()], "ag_w_in0")
    w_short_full = jnp.transpose(g_in0[1], (1, 0, 2)).reshape(DEPTH, SHORT_K, D)
    cfm_w_full = jnp.transpose(g_in0[2], (1, 0, 2)).reshape(DEPTH, CFM_K, D)
    for l in range(DEPTH):
        consts[l]["wsh"] = jnp.pad(w_short_full[l], ((0, 8 - SHORT_K), (0, 0)))
        consts[l]["cw"] = jnp.pad(cfm_w_full[l], ((0, HALO - CFM_K), (0, 0)))
    Wg = [dict(w_in=g_in0[0]), None]
    ag_l1 = None
    nin = w_in.shape[2]
    tn_in = nin if nin % 256 == 0 and nin <= 1280 else 256
    tn_fi = 512 if F2 % 512 == 0 else 256
    tn_dw = min(256, D)

    saved = []
    xcur, gprev, ffn_tail = x0, None, None
    for l in range(DEPTH):
        sh1, sc1, g1, sh2, sc2, g2 = [mod[l, k] for k in range(N_MOD)]
        cl = consts[l]
        if l == 0:
            xl, h, ht = xl0, h0, ht0
        else:
            vec1 = _rows(gprev, norm1_g[l], sc1, sh1)
            act_prev, w_fo_prev = ffn_tail
            ag_l1 = _gather_mid(ag_l1, act_prev, f"ag_w{l}")
            f_prev, xl, h, ht = _mm_resid_norm(act_prev, w_fo_prev, xcur, vec1, tm, f"mm_ffn_out_norm1_{l}",
                                               deps=(ag_l1["tok"],))
            saved[l - 1]["f"] = f_prev
            g = _gather_finish(ag_l1, h, f"ag_w{l}")
            Wg[l] = dict(w_in=g[0], **rest_of(g[1:]))
        wl = Wg[l]
        z = _mm_nn(h, wl["w_in"], BF16, tm_huge, tn_in, D, f"mm_in{l}", w_outer=True)
        mix_deps = ()
        if l == 0:
            ag_rest0 = _gather_mid(ag_rest0, z, "ag_rest0")
            mix_deps = (ag_rest0["tok"],)
            if DEPTH > 1:
                ag_l1 = _gather_start(shards_of(1), dev, "ag_w1")
                mix_deps += (ag_l1["tok"],)
        acts, acts_t, conv = _mixer_fwd(z, cl["wsh"], cl["sgu_ln"], cl["wtril"], cl["bias_full"], cl["cw"], cl["cvec"],
                                        f"mixer_fwd{l}", deps=mix_deps)
        if l == 0:
            wl.update(rest_of(_gather_finish(ag_rest0, acts[0], "ag_rest0")))
        merged, merged_t, ys = _branch_out(acts, [wl["w_a"][0], wl["w_b"][0], wl["w_c"][0]], z, f"branch_out{l}")
        o, x1, h2, h2t = _mm_resid_norm(merged, wl["w_o"], xl, _rows(g1, norm2_g[l], sc2, sh2), tm, f"mm_o_norm2_{l}")
        gu, act, act_t = _ffn_in_swiglu(h2, wl["w_fi"], tm_huge, 256, f"mm_ffn_in{l}")
        saved.append(dict(xl=xl, ht=ht, z=z, acts_t=acts_t, conv=conv, ys=ys, merged_t=merged_t, o=o, x1=x1, h2t=h2t, gu=gu,
                          act_t=act_t, f=None, consts=cl, mod=(sh1, sc1, g1, sh2, sc2, g2)))
        xcur, gprev, ffn_tail = x1, g2, (act, wl["w_fo"])

    last = saved[-1]
    dxup, dfb, fsums, loss_blk = _final_bwd(last["x1"], *ffn_tail, tgt, _rows(last["mod"][5], final_g), "final_bwd")
    loss_row = jnp.pad(loss_blk[0, 0:1], (0, D - 1))
    dgate2_next = fsums[1]
    small = [dict() for _ in range(DEPTH)]
    dmods = [None] * DEPTH
    nfi = w_ffn_in.shape[2]
    early_names, late_names = ["w_ffn_out", "w_ffn_in", "w_o"], ["w_a_out", "w_b_out", "w_c_out", "w_in"]
    results = {n: None for n in early_names + late_names}

    def adam_group(names, Ps, R2s, l, deps=()):
        for n, p, r2 in zip(names, Ps, R2s):
            results[n] = _adam_big(p, r2, my_chip, W[n], Mo[n], Vo[n], l, results[n], f"adam_{n}{l}", deps)

    deferred = []
    late_prev = None
    ag_s1, gathered1 = None, None
    tk_w = min(2048, S)
    tn_dw_in = tn_in // 2 if tn_in == 1280 else tn_in
    for l in reversed(range(DEPTH)):
        sv, wl, cl = saved[l], Wg[l], saved[l]["consts"]
        sh1, sc1, g1, sh2, sc2, g2 = sv["mod"]
        dact = _mm_nt(dfb, wl["w_fo"], BF16, tm_big, FF, D, f"mm_dact{l}",
                      deps=() if late_prev is None else (late_prev["tok"], ag_s1["tok"]))
        g_fo = _mm_wgrad(sv["act_t"], dfb, 1, FF // 2, D, tk_w, f"mm_dw_ffn_out{l}")
        dgu = _swiglu_bwd(dact, sv["gu"], f"swiglu_bwd{l}")
        g_fi = _mm_wgrad(sv["h2t"], dgu, 1, D, tn_fi, S, f"mm_dw_ffn_in{l}")
        if late_prev is not None:
            deferred.append((late_names, *_scatter_finish(late_prev, g_fi, f"rs_late{l + 1}"), l + 1))
            late_prev = None
        if ag_s1 is not None:
            ag_s1 = _gather_mid(ag_s1, g_fi, "ag_small1")
        dx1, dob, s2 = _norm_bwd(sv["x1"], (dgu, wl["w_fi"]), dxup, _rows(norm2_g[l], sc2, g1), sv["o"],
                                 f"mm_dh2_norm2_bwd{l}", deps=() if ag_s1 is None else (ag_s1["tok"],))
        dmerged = _mm_nt(dob, wl["w_o"], BF16, tm_big, D, D, f"mm_dmerged{l}")
        g_o = _mm_wgrad(sv["merged_t"], dob, 1, D, tn_dw, S, f"mm_dw_o{l}")
        early = _scatter_start([g_fo.reshape(NDEV, FF // NDEV, D),
                                jnp.transpose(g_fi.reshape(D, NDEV, nfi), (1, 0, 2)),
                                g_o.reshape(NDEV, D // NDEV, D)], f"rs_early{l}")
        dys, dz = _gate_bwd(dmerged, sv["z"], sv["ys"], f"gate_bwd{l}", deps=(early["tok"],))
        if ag_s1 is not None:
            gathered1 = _gather_finish(ag_s1, dys, "ag_small1")[0]
            ag_s1 = None
        early = _scatter_mid(early, dys, my_c, f"rs_early{l}")
        dacts = _mm3_nt(dys, [wl["w_a"], wl["w_b"], wl["w_c"]], tm_big, f"mm_dact_abc{l}", deps=(early["tok"],))
        g3 = _mm3_wgrad(sv["acts_t"], dys, tn_dw, f"mm_dw_abc{l}")
        g_abc = [g3[n] for n in range(3)]
        dz, mvec, dcw, dws, dbs = _mixer_bwd(sv["z"], dacts, sv["conv"], dz, cl["wsh"], cl["sgu_ln"], cl["wtril"],
                                             cl["wtril_t"], cl["bias_full"], cl["cw"], cl["cvec"], f"mixer_bwd{l}")
        g_in = _mm_wgrad(sv["ht"], dz, NDEV, D, tn_dw_in, S, f"mm_dw_in{l}")
        late = _scatter_start([g.reshape(NDEV, D // NDEV, D) for g in g_abc] + [g_in], f"rs_late{l}")
        if l > 0:
            pv = saved[l - 1]
            dxup, dfb, s1 = _norm_bwd(sv["xl"], (dz, wl["w_in"]), dx1, _rows(norm1_g[l], sc1, pv["mod"][5]), pv["f"],
                                      f"mm_dh_norm1_bwd{l}", deps=(late["tok"],))
        else:
            dxup, dfb, s1 = _norm_bwd(sv["xl"], (dz, wl["w_in"]), dx1, _rows(norm1_g[l], sc1), None,
                                      f"mm_dh_norm1_bwd{l}", deps=(late["tok"],))
        deferred.append((early_names, *_scatter_finish(early, dxup, f"rs_early{l}"), l))
        dmods[l] = jnp.stack([s1[0], s1[1], s2[3], s2[0], s2[1], dgate2_next])
        dgate2_next = s1[3]
        small[l] = dict(norm1_g=s1[2], norm2_g=s2[2], sgu_ln_g=mvec[3], sgu_ln_b=mvec[4], cfm_conv_b=mvec[5],
                        cfm_ln_g=mvec[6], cfm_ln_b=mvec[7], b_sgu=dbs[:, :, 0],
                        w_sgu=jnp.where(tril[None], dws, 0.0), b_ada=dmods[l], w_short=mvec[0:SHORT_K],
                        cfm_conv_w=dcw[0:CFM_K])
        small_get = lambda name, k: {"final_g": fsums[0], "loss": loss_row}.get(name) if k is None else small[k][name]
        if l > 0:
            late_prev = _scatter_mid(late, dxup, my_c, f"rs_late{l}")
            ag_s1 = _gather_start([_pack(small_get, D, layers=(l,), tail=True)], dev, "ag_small1", deps=(late_prev["tok"],),
                                  within=(l * ROWS_PER_LAYER, PACK_ROWS))
    grad_x = dxup.reshape(x.shape)

    gathered = _all_gather([_pack(small_get, D, layers=(0,), tail=False)], "ag_small0", deps=(dxup,),
                           into=[(gathered1, 0)])[0]
    late_prev = _scatter_mid(late, gathered, my_c, "rs_late0")
    one_row = [n for n in order if n in SMALL_ROWS and SMALL_ROWS[n][1] == 1 and W[n].ndim == 2]
    (sg, sd, sm, sv_), singles = _adam_small(
        gathered, *packs, name="adam_small", deps=(late_prev["tok"],),
        single_rows=[tuple(l * ROWS_PER_LAYER + SMALL_ROWS[n][0] for l in range(DEPTH)) for n in one_row])
    loss = sg[FINAL_ROW + 1, 0]
    out = {n: tuple(singles[4 * i:4 * i + 4]) for i, n in enumerate(one_row)}
    for name in order:
        if name in SMALL_ROWS and name not in sharded_small and name not in out:
            out[name] = tuple(_unpack(p, name, W[name].shape) for p in (sg, sd, sm, sv_))
    out["final_g"] = tuple(p[FINAL_ROW] for p in (sg, sd, sm, sv_))

    def my_cols(name):
        full = _unpack(sg, name, (DEPTH, SMALL_ROWS[name][1], D))
        return lax.dynamic_slice_in_dim(full, dev * ncs, ncs, axis=2)

    gcs = jnp.concatenate([my_cols("w_short").reshape(-1, ncs), my_cols("cfm_conv_w").reshape(-1, ncs)])
    cd, cm, cv = _adam_plain(jnp.pad(gcs, ((0, padr), (0, 0))), *convw_wmv, "adam_convw")
    nsh = DEPTH * SHORT_K
    out["w_short"] = tuple(a[0:nsh].reshape(w_short.shape) for a in (gcs, cd, cm, cv))
    out["cfm_conv_w"] = tuple(a[nsh:ncr].reshape(cfm_conv_w.shape) for a in (gcs, cd, cm, cv))

    dm_all = jnp.stack([gathered[:, l * ROWS_PER_LAYER + 136:l * ROWS_PER_LAYER + 136 + N_MOD, :].reshape(NDEV, N_MOD * D)
                        for l in range(DEPTH)])
    dm_mine = lax.dynamic_slice_in_dim(dm_all, dev * ncol, ncol, axis=2)
    out["w_ada"] = tuple(_adam_ada(jnp.transpose(c_act), dm_mine, w_ada, m_w_ada, v_w_ada, "adam_ada"))

    for names, Ps, R2s, l in deferred:
        adam_group(names, Ps, R2s, l, deps=(late_prev["tok"],))
    adam_group(late_names, *_scatter_finish(late_prev, results["w_o"][0], "rs_late0"), 0)
    for n in early_names + late_names:
        out[n] = tuple(results[n])

    grads = [out[n][0] for n in order]
    deltas = [out[n][1] for n in order]
    new_m = [out[n][2] for n in order]
    new_v = [out[n][3] for n in order]
    return (loss, grad_x, *grads, *deltas, *new_m, *new_v)
```

```python
import functools
import math

import jax
import jax.numpy as jnp
from jax import lax
from jax.experimental import pallas as pl
from jax.experimental.pallas import tpu as pltpu

F32, BF16 = jnp.float32, jnp.bfloat16
NDEV = 8
NCHIP = NDEV // 2
DEPTH = 2
EPS = 1e-6
CHUNK = 128
NG = 8
SHORT_K = 3
CFM_K = 31
HALO = 32
N_MOD = 6
LANE = 128
VMEM_LIMIT = 56 * 1024 * 1024
ADAM_LR, ADAM_B1, ADAM_B2, ADAM_EPS, ADAM_WD, ADAM_STEP = 0.001, 0.9, 0.999, 1e-08, 0.01, 10
_G0 = math.sqrt(2.0 / math.pi)
_G1 = 0.044715
MESH = pl.DeviceIdType.MESH
ANY = pl.BlockSpec(memory_space=pl.ANY)


def _pcall(body, **kw):
    return pl.pallas_call(body, **kw)


def _params(sem=None):
    return pltpu.CompilerParams(dimension_semantics=sem, vmem_limit_bytes=VMEM_LIMIT)


def _sds(shape, dtype):
    return jax.ShapeDtypeStruct(tuple(shape), dtype)


def _mm_body(dims, nk, out_f32, blocks=1):
    def body(a_ref, b_ref, o_ref, *scr):
        k = pl.program_id(2)
        if blocks == 1:
            part = lax.dot_general(a_ref[...], b_ref[...], dims, preferred_element_type=F32)
        else:
            w = a_ref.shape[1] // blocks
            part = None
            for g in range(blocks):
                t = lax.dot_general(a_ref[:, g * w:(g + 1) * w], b_ref[g], dims, preferred_element_type=F32)
                part = t if part is None else part + t
        if nk == 1:
            o_ref[...] = part.reshape(o_ref.shape).astype(o_ref.dtype)
        elif out_f32:
            @pl.when(k == 0)
            def _():
                o_ref[...] = part.reshape(o_ref.shape)

            @pl.when(k > 0)
            def _():
                o_ref[...] += part.reshape(o_ref.shape)
        else:
            acc = scr[0]

            @pl.when(k == 0)
            def _():
                acc[...] = part

            @pl.when(k > 0)
            def _():
                acc[...] += part

            @pl.when(k == nk - 1)
            def _():
                o_ref[...] = acc[...].astype(o_ref.dtype)
    return body


def _after(body, n_in, deps):
    nd = len(deps)
    if nd == 0:
        return body

    def ordered(*refs):
        return body(*refs[:n_in], *refs[n_in + nd:])
    return ordered


def _mm_call(body, grid, in_specs, out_spec, out_shape, acc_shape, name, deps=()):
    scratch = [] if acc_shape is None else [pltpu.VMEM(acc_shape, F32)]
    return _pcall(_after(body, 2, deps), grid=grid, in_specs=in_specs + [ANY] * len(deps), out_specs=out_spec,
                  out_shape=out_shape, scratch_shapes=scratch, name=name,
                  compiler_params=_params(("parallel", "parallel", "arbitrary")))


def _mm_nn(a, b3, out_dtype, tm, tn, tk, name, w_outer=False, deps=()):
    M, K = a.shape
    G, _, Nb = b3.shape
    npb, nk = Nb // tn, K // tk
    out_f32 = out_dtype == F32
    body = _mm_body((((1,), (0,)), ((), ())), nk, out_f32)
    if w_outer:
        grid = (G * npb, M // tm, nk)
        ij = lambda p, q: (q, p)
    else:
        grid = (M // tm, G * npb, nk)
        ij = lambda p, q: (p, q)

    def a_map(p, q, k):
        i, j = ij(p, q)
        return (i, k)

    def b_map(p, q, k):
        i, j = ij(p, q)
        return (j // npb, k, j % npb)

    def o_map(p, q, k):
        return ij(p, q)

    def wrapped(a_ref, b_ref, o_ref, *scr):
        body(a_ref, b_ref, o_ref, *scr)

    return _mm_call(wrapped, grid, [pl.BlockSpec((tm, tk), a_map), pl.BlockSpec((None, tk, tn), b_map)],
                    pl.BlockSpec((tm, tn), o_map), _sds((M, G * Nb), out_dtype),
                    None if (nk == 1 or out_f32) else (tm, tn), name, deps)(a, b3, *deps)


def _mm_nt(a, b3, out_dtype, tm, tn, tk, name, deps=(), blocks_per_step=1):
    M, _ = a.shape
    G, Ko, Nb = b3.shape
    kpb = Nb // tk
    nk = G * kpb // blocks_per_step
    out_f32 = out_dtype == F32
    body = _mm_body((((1,), (1,)), ((), ())), nk, out_f32, blocks_per_step)

    def wrapped(a_ref, b_ref, o_ref, *scr):
        body(a_ref, b_ref, o_ref, *scr)

    if blocks_per_step > 1:
        assert tk == Nb and G % blocks_per_step == 0
        b_spec = pl.BlockSpec((blocks_per_step, tn, tk), lambda i, j, k: (k, j, 0))
    else:
        b_spec = pl.BlockSpec((None, tn, tk), lambda i, j, k: (k // kpb, j, k % kpb))
    return _mm_call(wrapped, (M // tm, Ko // tn, nk),
                    [pl.BlockSpec((tm, tk * blocks_per_step), lambda i, j, k: (i, k)), b_spec],
                    pl.BlockSpec((tm, tn), lambda i, j, k: (i, j)), _sds((M, Ko), out_dtype),
                    None if (nk == 1 or out_f32) else (tm, tn), name, deps)(a, b3, *deps)


def _mm_wgrad(at, b, G, tm, tn, tk, name, deps=()):
    M, T = at.shape
    Nb = b.shape[1] // G
    npb, nk = Nb // tn, T // tk
    body = _mm_body((((1,), (0,)), ((), ())), nk, False)

    def wrapped(a_ref, b_ref, o_ref, *scr):
        body(a_ref, b_ref, o_ref, *scr)

    a = at
    in_specs = [pl.BlockSpec((tm, tk), lambda i, j, k: (i, k)), pl.BlockSpec((tk, tn), lambda i, j, k: (k, j))]
    out_spec = pl.BlockSpec((None, tm, tn), lambda i, j, k: (j // npb, i, j % npb))
    return _mm_call(wrapped, (M // tm, G * npb, nk), in_specs, out_spec, _sds((G, M, Nb), BF16),
                    None if nk == 1 else (tm, tn), name, deps)(a, b, *deps)


def _mm3_nt(x3, ws, tm, name, deps=()):
    nb, S, K = x3.shape
    Ko = ws[0].shape[1]

    def body(x_ref, w0, w1, w2, o_ref):
        n = pl.program_id(0)
        for k, w in enumerate((w0, w1, w2)):
            @pl.when(n == k)
            def _(w=w):
                o_ref[...] = lax.dot_general(x_ref[...], w[...], (((1,), (1,)), ((), ())),
                                             preferred_element_type=F32).astype(BF16)

    wspec = pl.BlockSpec((None, Ko, K), lambda n, i: (0, 0, 0))
    return _pcall(_after(body, 4, deps), grid=(nb, S // tm),
                  in_specs=[pl.BlockSpec((None, tm, K), lambda n, i: (n, i, 0)), wspec, wspec, wspec] + [ANY] * len(deps),
                  out_specs=pl.BlockSpec((None, tm, Ko), lambda n, i: (n, i, 0)), out_shape=_sds((nb, S, Ko), BF16),
                  name=name, compiler_params=_params(("arbitrary", "parallel")))(x3, *ws, *deps)


def _mm3_wgrad(at3, b3, tn, name):
    nb, M, T = at3.shape
    N = b3.shape[2]

    def body(a_ref, b_ref, o_ref):
        o_ref[...] = jnp.dot(a_ref[...], b_ref[...], preferred_element_type=F32).astype(BF16)

    return _pcall(body, grid=(nb, N // tn),
                  in_specs=[pl.BlockSpec((None, M, T), lambda n, j: (n, 0, 0)), pl.BlockSpec((None, T, tn), lambda n, j: (n, 0, j))],
                  out_specs=pl.BlockSpec((None, M, tn), lambda n, j: (n, 0, j)), out_shape=_sds((nb, M, N), BF16),
                  name=name, compiler_params=_params(("arbitrary", "parallel")))(at3, b3)


def _rsum(v):
    return jnp.sum(v, axis=0, keepdims=True)


def _rmean(v):
    return jnp.mean(v, axis=-1, keepdims=True)


def _gelu(x):
    t = jnp.tanh(_G0 * (x + _G1 * (x * x * x)))
    return x * (0.5 * (1.0 + t)), t


def _dgelu(x, t):
    return 0.5 * (1.0 + t) + 0.5 * x * (1.0 - t * t) * (_G0 * (1.0 + 3.0 * _G1 * (x * x)))


def _sigmoid(x):
    return 0.5 * jnp.tanh(0.5 * x) + 0.5


def _fill_shifted(ext, rot):
    v = ext[...]
    n = v.shape[0]
    for b in range(1, 8):
        rot[b - 1] = pltpu.roll(v, n - b, 0)


def _rows_at(ext, rot, s, tm, cs=slice(None)):
    a, b = divmod(s, 8)
    return ext[8 * a:8 * a + tm, cs] if b == 0 else rot[b - 1, 8 * a:8 * a + tm, cs]


def _causal_conv(w_ref, taps, bias, ext, rot, offset, tm, out):
    D = out.shape[1]
    for cb in range(D // LANE):
        cs = slice(cb * LANE, (cb + 1) * LANE)
        acc = None
        for k, o in zip(taps, offset):
            term = w_ref[k:k + 1, cs] * _rows_at(ext, rot, o, tm, cs)
            acc = term if acc is None else acc + term
        out[:, cs] = acc if bias is None else acc + bias[:, cs]


def _rows(*vs):
    a = jnp.stack([v.astype(F32) for v in vs])
    return jnp.pad(a, ((0, 8 - len(vs)), (0, 0)))


def _row_spec(tm, D):
    return pl.BlockSpec((tm, D), lambda i: (i, 0))


def _const_spec(shape):
    nd = len(shape)
    return pl.BlockSpec(shape, lambda i: (0,) * nd)


def _norm_fwd(xp, f, vec, name, deps=()):
    S, D = xp.shape
    tm = min(512, S)
    has_f = f is not None

    def body(*refs):
        if has_f:
            xp_ref, f_ref, vec_ref, xo_ref, h_ref, ht_ref = refs
            x = xp_ref[...] + vec_ref[0:1, :] * f_ref[...]
            xo_ref[...] = x
        else:
            xp_ref, vec_ref, h_ref, ht_ref = refs
            x = xp_ref[...]
        r = lax.rsqrt(_rmean(x * x) + EPS)
        h = (x * r) * vec_ref[1:2, :]
        h = h * (1.0 + vec_ref[2:3, :]) + vec_ref[3:4, :]
        h_ref[...] = h.astype(BF16)
        ht_ref[...] = h.T.astype(BF16)

    rs = _row_spec(tm, D)
    ins = [xp, f, vec] if has_f else [xp, vec]
    in_specs = ([rs, rs] if has_f else [rs]) + [_const_spec((8, D))]
    out_shape = ([_sds((S, D), F32)] if has_f else []) + [_sds((S, D), BF16), _sds((D, S), BF16)]
    out_specs = [rs] * (len(out_shape) - 1) + [pl.BlockSpec((D, tm), lambda i: (0, i))]
    outs = _pcall(_after(body, len(ins), deps), grid=(S // tm,), in_specs=in_specs + [ANY] * len(deps),
                  out_specs=out_specs, out_shape=out_shape, name=name,
                  compiler_params=_params(("parallel",)))(*ins, *deps)
    return (outs[0], outs[1], outs[2]) if has_f else (xp, outs[0], outs[1])


def _mm_resid_norm(a, w3, xprev, vec, tm, name, deps=()):
    S, K = a.shape
    D = w3.shape[2]

    def body(a_ref, w_ref, xp_ref, vec_ref, p_ref, xo_ref, h_ref, ht_ref):
        p = jnp.dot(a_ref[...], w_ref[...], preferred_element_type=F32)
        p_ref[...] = p
        x = xp_ref[...] + vec_ref[0:1, :] * p
        xo_ref[...] = x
        r = lax.rsqrt(_rmean(x * x) + EPS)
        h = (x * r) * vec_ref[1:2, :]
        h = h * (1.0 + vec_ref[2:3, :]) + vec_ref[3:4, :]
        h_ref[...] = h.astype(BF16)
        ht_ref[...] = h.T.astype(BF16)

    rs = _row_spec(tm, D)
    return _pcall(_after(body, 4, deps), grid=(S // tm,),
                  in_specs=[_row_spec(tm, K), pl.BlockSpec((None, K, D), lambda i: (0, 0, 0)), rs, _const_spec((8, D))]
                  + [ANY] * len(deps),
                  out_specs=[rs, rs, rs, pl.BlockSpec((D, tm), lambda i: (0, i))],
                  out_shape=[_sds((S, D), F32), _sds((S, D), F32), _sds((S, D), BF16), _sds((D, S), BF16)], name=name,
                  compiler_params=_params(("parallel",)))(a, w3, xprev, vec, *deps)


def _mixer_fwd(z, wsh, sgu_ln, wtril, bias_full, cw, cvec, name, deps=()):
    S = z.shape[0]
    D = wsh.shape[1]
    tm = CHUNK

    def body(z_ref, wsh_ref, sln_ref, wt_ref, bias_ref, cw_ref, cv_ref, oa_ref, ob_ref, oc_ref, t_ref,
             conv_ref, pe, ge, gr, cbuf):
        i = pl.program_id(0)

        @pl.when(i == 0)
        def _():
            pe[0:HALO, :] = jnp.zeros((HALO, D), F32)
            ge[0:HALO, :] = jnp.zeros((HALO, D), F32)

        def col(n):
            return z_ref[:, n * D:(n + 1) * D].astype(F32)

        pe[HALO:HALO + tm, :] = col(1) * col(2)
        q = wsh_ref[0:1, :] * pe[HALO - 2:HALO - 2 + tm, :]
        q = q + wsh_ref[1:2, :] * pe[HALO - 1:HALO - 1 + tm, :]
        q = q + wsh_ref[2:3, :] * pe[HALO:HALO + tm, :]
        act_a = col(0) * q
        oa_ref[...] = act_a.astype(BF16)
        t_ref[0] = act_a.T.astype(BF16)
        gu, _ = _gelu(col(3))
        gv, _ = _gelu(col(4))
        d = gv - _rmean(gv)
        nrm = d * lax.rsqrt(_rmean(d * d) + EPS)
        vnb = (nrm * sln_ref[0:1, :] + sln_ref[1:2, :]).astype(BF16)
        for g in range(NG):
            cs = slice(g * LANE, (g + 1) * LANE)
            mixed = jnp.dot(wt_ref[g], vnb[:, cs], preferred_element_type=F32) + bias_ref[:, cs]
            act_b = gu[:, cs] * mixed
            ob_ref[:, cs] = act_b.astype(BF16)
            t_ref[1, cs, :] = act_b.T.astype(BF16)
        ge[HALO:HALO + tm, :] = col(5) * _sigmoid(col(6))
        _fill_shifted(ge, gr)
        o0 = HALO - (CFM_K - 1)
        _causal_conv(cw_ref, range(CFM_K), cv_ref[0:1, :], ge, gr, range(o0, o0 + CFM_K), tm, cbuf)
        conv = cbuf[...]
        conv_ref[...] = conv.astype(BF16)
        d = conv - _rmean(conv)
        ln = (d * lax.rsqrt(_rmean(d * d) + EPS)) * cv_ref[1:2, :] + cv_ref[2:3, :]
        act_c = ln * _sigmoid(ln)
        oc_ref[...] = act_c.astype(BF16)
        t_ref[2] = act_c.T.astype(BF16)
        pe[0:HALO, :] = pe[tm:tm + HALO, :]
        ge[0:HALO, :] = ge[tm:tm + HALO, :]

    rs = _row_spec(tm, D)
    outs = _pcall(
        _after(body, 7, deps), grid=(S // tm,),
        in_specs=[pl.BlockSpec((tm, 7 * D), lambda i: (i, 0)), _const_spec((8, D)), _const_spec((8, D)),
                  _const_spec((NG, CHUNK, CHUNK)), _const_spec((CHUNK, D)), _const_spec((HALO, D)), _const_spec((8, D))]
        + [ANY] * len(deps),
        out_specs=[rs, rs, rs, pl.BlockSpec((3, D, tm), lambda i: (0, 0, i)), rs],
        out_shape=[_sds((S, D), BF16)] * 3 + [_sds((3, D, S), BF16), _sds((S, D), BF16)],
        scratch_shapes=[pltpu.VMEM((HALO + tm, D), F32), pltpu.VMEM((HALO + tm, D), F32),
                        pltpu.VMEM((7, HALO + tm, D), F32), pltpu.VMEM((tm, D), F32)],
        name=name, compiler_params=_params(("arbitrary",)))(z, wsh, sgu_ln, wtril, bias_full, cw, cvec, *deps)
    return outs[:3], outs[3], outs[4]


def _branch_out(acts, ws, z, name):
    S, D = acts[0].shape
    tm = min(512, S)

    def body(a0, a1, a2, w0, w1, w2, g0, g1, g2, m_ref, mt_ref, y_ref):
        m = None
        for n, (a, w, g) in enumerate(((a0, w0, g0), (a1, w1, g1), (a2, w2, g2))):
            y = jnp.dot(a[...], w[...], preferred_element_type=F32)
            y_ref[n] = y.astype(BF16)
            t = _sigmoid(g[...].astype(F32)) * y
            m = t if m is None else m + t
        m_ref[...] = m.astype(BF16)
        mt_ref[...] = m.T.astype(BF16)

    rs = _row_spec(tm, D)
    gate_specs = [pl.BlockSpec((tm, D), functools.partial(lambda i, n: (i, 7 + n), n=n)) for n in range(3)]
    return _pcall(body, grid=(S // tm,),
                  in_specs=[rs, rs, rs] + [_const_spec((D, D))] * 3 + gate_specs,
                  out_specs=[rs, pl.BlockSpec((D, tm), lambda i: (0, i)), pl.BlockSpec((3, tm, D), lambda i: (0, i, 0))],
                  out_shape=[_sds((S, D), BF16), _sds((D, S), BF16), _sds((3, S, D), BF16)], name=name,
                  compiler_params=_params(("parallel",)))(*acts, *ws, z, z, z)


def _ffn_in_swiglu(h2, w3, tm, tn, name):
    S, D = h2.shape
    F = w3.shape[2] // 2
    nj = F // tn

    def body(a_ref, wg_ref, wu_ref, gu_ref, act_ref, actt_ref):
        a = a_ref[...]
        g = jnp.dot(a, wg_ref[...], preferred_element_type=F32)
        u = jnp.dot(a, wu_ref[...], preferred_element_type=F32)
        gu_ref[0] = g.astype(BF16)
        gu_ref[1] = u.astype(BF16)
        act = (g * _sigmoid(g)) * u
        act_ref[...] = act.astype(BF16)
        actt_ref[...] = act.T.astype(BF16)

    return _pcall(body, grid=(S // tm, nj),
                  in_specs=[pl.BlockSpec((tm, D), lambda i, j: (i, 0)), pl.BlockSpec((None, D, tn), lambda i, j: (0, 0, j)),
                            pl.BlockSpec((None, D, tn), lambda i, j: (0, 0, j + nj))],
                  out_specs=[pl.BlockSpec((2, tm, tn), lambda i, j: (0, i, j)), pl.BlockSpec((tm, tn), lambda i, j: (i, j)),
                             pl.BlockSpec((tn, tm), lambda i, j: (j, i))],
                  out_shape=[_sds((2, S, F), BF16), _sds((S, F), BF16), _sds((F, S), BF16)], name=name,
                  compiler_params=_params(("parallel", "parallel")))(h2, w3, w3)


def _swiglu_bwd(df, w3, gu, name, deps=()):
    _, S, F = gu.shape
    F2 = 2 * F
    D = df.shape[1]
    tm = min(256, S)

    def body(df_ref, w_ref, g_ref, u_ref, o_ref):
        d = lax.dot_general(df_ref[...], w_ref[...], (((1,), (1,)), ((), ())), preferred_element_type=F32)
        g = g_ref[...].astype(F32)
        sg = _sigmoid(g)
        o_ref[:, 0:F] = (d * u_ref[...].astype(F32) * (sg * (1.0 + g * (1.0 - sg)))).astype(BF16)
        o_ref[:, F:2 * F] = (d * (g * sg)).astype(BF16)

    return _pcall(_after(body, 4, deps), grid=(S // tm,),
                  in_specs=[_row_spec(tm, D), pl.BlockSpec((None, F, D), lambda i: (0, 0, 0)),
                            pl.BlockSpec((None, tm, F), lambda i: (0, i, 0)), pl.BlockSpec((None, tm, F), lambda i: (1, i, 0))]
                  + [ANY] * len(deps),
                  out_specs=pl.BlockSpec((tm, F2), lambda i: (i, 0)), out_shape=_sds((S, F2), BF16), name=name,
                  compiler_params=_params(("parallel",)))(df, w3, gu, gu, *deps)


def _final_bwd(x1, act, w3, tgt, vec, name):
    S, D = x1.shape
    K = act.shape[1]
    tm = min(512, S)

    def body(x_ref, a_ref, w_ref, t_ref, vec_ref, dx_ref, df_ref, sums_ref, loss_ref):
        @pl.when(pl.program_id(0) == 0)
        def _():
            sums_ref[...] = jnp.zeros_like(sums_ref)
            loss_ref[...] = jnp.zeros_like(loss_ref)

        gate, fg = vec_ref[0:1, :], vec_ref[1:2, :]
        fv = jnp.dot(a_ref[...], w_ref[...], preferred_element_type=F32)
        x = x_ref[...] + gate * fv
        r = lax.rsqrt(_rmean(x * x) + EPS)
        xn = x * r
        diff = xn * fg - t_ref[...]
        per_tok = _rmean(diff * diff)
        loss_ref[...] += 0.5 * jnp.sum(per_tok, axis=0, keepdims=True)
        dy = diff * (1.0 / D)
        sums_ref[0:1, :] += _rsum(dy * xn)
        dxn = dy * fg
        dx = r * (dxn - xn * _rmean(dxn * xn))
        sums_ref[1:2, :] += _rsum(dx * fv)
        dx_ref[...] = dx
        df_ref[...] = (dx * gate).astype(BF16)

    rs = _row_spec(tm, D)
    return _pcall(body, grid=(S // tm,),
                  in_specs=[rs, _row_spec(tm, K), pl.BlockSpec((None, K, D), lambda i: (0, 0, 0)), rs, _const_spec((8, D))],
                  out_specs=[rs, rs, _const_spec((8, D)), _const_spec((8, LANE))],
                  out_shape=[_sds((S, D), F32), _sds((S, D), BF16), _sds((8, D), F32), _sds((8, LANE), F32)],
                  name=name, compiler_params=_params(("arbitrary",)))(x1, act, w3, tgt, vec)


def _norm_bwd(xin, dh, dxup, vec, fprev, name, deps=()):
    S, D = xin.shape
    has_prev = fprev is not None
    fused = isinstance(dh, tuple)
    tm = min(512, S)
    n_dh = 2 if fused else 1
    G, Nb = (dh[1].shape[0], dh[1].shape[2]) if fused else (1, 0)
    bps = 1 if G == 1 else 2
    nk = G // bps

    def body(*refs):
        x_ref, dh_refs, (up_ref, vec_ref) = refs[0], refs[1:1 + n_dh], refs[1 + n_dh:3 + n_dh]
        rest = refs[3 + n_dh:]
        if has_prev:
            fp_ref, dx_ref, dp_ref, sums_ref = rest[:4]
        else:
            dx_ref, sums_ref = rest[:2]
        k = pl.program_id(1)

        @pl.when((pl.program_id(0) == 0) & (k == 0))
        def _():
            sums_ref[...] = jnp.zeros_like(sums_ref)

        def finish(dhv):
            g, scale = vec_ref[0:1, :], vec_ref[1:2, :]
            x = x_ref[...]
            r = lax.rsqrt(_rmean(x * x) + EPS)
            xn = x * r
            sums_ref[0:1, :] += _rsum(dhv)
            sums_ref[1:2, :] += _rsum(dhv * (xn * g))
            dm = dhv * (1.0 + scale)
            sums_ref[2:3, :] += _rsum(dm * xn)
            dxn = dm * g
            dx = up_ref[...] + r * (dxn - xn * _rmean(dxn * xn))
            dx_ref[...] = dx
            if has_prev:
                sums_ref[3:4, :] += _rsum(dx * fp_ref[...])
                dp_ref[...] = (dx * vec_ref[2:3, :]).astype(BF16)

        if not fused:
            finish(dh_refs[0][...])
        elif nk == 1:
            finish(lax.dot_general(dh_refs[0][...], dh_refs[1][...], (((1,), (1,)), ((), ())), preferred_element_type=F32))
        else:
            a_ref, b_ref, acc = dh_refs[0], dh_refs[1], rest[-1]
            part = None
            for j in range(bps):
                t = lax.dot_general(a_ref[:, j * Nb:(j + 1) * Nb], b_ref[j], (((1,), (1,)), ((), ())),
                                    preferred_element_type=F32)
                part = t if part is None else part + t

            @pl.when(k == 0)
            def _():
                acc[...] = part

            @pl.when(k > 0)
            def _():
                acc[...] += part

            @pl.when(k == nk - 1)
            def _():
                finish(acc[...])

    rs = pl.BlockSpec((tm, D), lambda i, k: (i, 0))
    vs = pl.BlockSpec((8, D), lambda i, k: (0, 0))
    if not fused:
        dh_ins, dh_specs = [dh], [rs]
    elif nk == 1:
        dh_ins, dh_specs = list(dh), [pl.BlockSpec((tm, Nb), lambda i, k: (i, 0)),
                                      pl.BlockSpec((None, D, Nb), lambda i, k: (0, 0, 0), pipeline_mode=pl.Buffered(1))]
    else:
        dh_ins, dh_specs = list(dh), [pl.BlockSpec((tm, bps * Nb), lambda i, k: (i, k)),
                                      pl.BlockSpec((bps, D, Nb), lambda i, k: (k, 0, 0))]
    ins = [xin, *dh_ins, dxup, vec] + ([fprev] if has_prev else [])
    in_specs = [rs, *dh_specs, rs, vs] + ([rs] if has_prev else [])
    out_shape = [_sds((S, D), F32)] + ([_sds((S, D), BF16)] if has_prev else []) + [_sds((8, D), F32)]
    out_specs = [rs] + ([rs] if has_prev else []) + [vs]
    outs = _pcall(_after(body, len(ins), deps), grid=(S // tm, nk), in_specs=in_specs + [ANY] * len(deps),
                  out_specs=out_specs, out_shape=out_shape, name=name,
                  scratch_shapes=[pltpu.VMEM((tm, D), F32)] if nk > 1 else [],
                  compiler_params=_params(("arbitrary", "arbitrary")))(*ins, *deps)
    return (outs[0], outs[1], outs[2]) if has_prev else (outs[0], None, outs[1])


def _gate_bwd(dmerged, z, ys, name, deps=()):
    S, D = dmerged.shape
    tm = min(512, S)
    ncol = z.shape[1] // D

    def body(dm_ref, g_ref, y_ref, dy_ref, dz_ref):
        sg = _sigmoid(g_ref[...].astype(F32))
        dm = dm_ref[...].astype(F32)
        dy_ref[...] = (dm * sg).astype(BF16)
        dz_ref[...] = (dm * y_ref[...].astype(F32) * (sg * (1.0 - sg))).astype(BF16)

    branch = pl.BlockSpec((None, tm, D), lambda i, n: (n, i, 0))
    return _pcall(_after(body, 3, deps), grid=(S // tm, 3),
                  in_specs=[pl.BlockSpec((tm, D), lambda i, n: (i, 0)), pl.BlockSpec((tm, D), lambda i, n: (i, 7 + n)),
                            branch] + [ANY] * len(deps),
                  out_specs=[branch, pl.BlockSpec((tm, D), lambda i, n: (i, 7 + n))],
                  out_shape=[_sds((3, S, D), BF16), _sds((S, ncol * D), BF16)], name=name,
                  compiler_params=_params(("parallel", "arbitrary")))(dmerged, z, ys, *deps)


def _mixer_bwd(z, dacts, conv, dz, wsh, sgu_ln, wtril, wtril_t, bias_full, cw, cvec, name):
    S = z.shape[0]
    D = wsh.shape[1]
    tm = CHUNK
    nt = S // tm
    hb = tm // HALO

    def body(zc, zp, da_ref, db_ref, dc_ref, conv_ref, wsh_ref, sln_ref, wt_ref, wtt_ref, bias_ref, cw_ref, cv_ref, _dz_in,
             dz_ref, vec_ref, dcw_ref, dws_ref, dbs_ref, pe, ge, dqe, dce, gr, dcr, cbuf, dcw8):
        i = pl.program_id(0)
        rb = nt - 1 - i

        @pl.when(i == 0)
        def _():
            vec_ref[...] = jnp.zeros_like(vec_ref)
            dcw8[...] = jnp.zeros_like(dcw8)
            dws_ref[...] = jnp.zeros_like(dws_ref)
            dbs_ref[...] = jnp.zeros_like(dbs_ref)
            dqe[tm:tm + HALO, :] = jnp.zeros((HALO, D), F32)
            dce[tm:tm + HALO, :] = jnp.zeros((HALO, D), F32)

        keep = (rb > 0).astype(F32)

        def col(n):
            return zc[:, n * D:(n + 1) * D].astype(F32)

        def pcol(n):
            return zp[:, n * D:(n + 1) * D].astype(F32)

        c_a, x_a = col(1), col(2)
        pe[0:HALO, :] = keep * (pcol(1) * pcol(2))
        pe[HALO:HALO + tm, :] = c_a * x_a
        q = wsh_ref[0:1, :] * pe[HALO - 2:HALO - 2 + tm, :]
        q = q + wsh_ref[1:2, :] * pe[HALO - 1:HALO - 1 + tm, :]
        q = q + wsh_ref[2:3, :] * pe[HALO:HALO + tm, :]
        dact = da_ref[...].astype(F32)
        dz_ref[:, 0:D] = (dact * q).astype(BF16)
        dq = dact * col(0)
        dqe[0:tm, :] = dq
        dp = wsh_ref[2:3, :] * dq + wsh_ref[1:2, :] * dqe[1:1 + tm, :] + wsh_ref[0:1, :] * dqe[2:2 + tm, :]
        dz_ref[:, D:2 * D] = (dp * x_a).astype(BF16)
        dz_ref[:, 2 * D:3 * D] = (dp * c_a).astype(BF16)
        for k in range(SHORT_K):
            o = HALO - (SHORT_K - 1) + k
            vec_ref[k:k + 1, :] += _rsum(dq * pe[o:o + tm, :])
        u, v = col(3), col(4)
        gu, tu = _gelu(u)
        gv, tv = _gelu(v)
        d = gv - _rmean(gv)
        rstd = lax.rsqrt(_rmean(d * d) + EPS)
        nrm = d * rstd
        vnb = (nrm * sln_ref[0:1, :] + sln_ref[1:2, :]).astype(BF16)
        dact = db_ref[...].astype(F32)
        dvn_parts, dgu_parts = [], []
        for g in range(NG):
            cs = slice(g * LANE, (g + 1) * LANE)
            vg = vnb[:, cs]
            mixed = jnp.dot(wt_ref[g], vg, preferred_element_type=F32) + bias_ref[:, cs]
            dgu_parts.append(dact[:, cs] * mixed)
            dmixed = dact[:, cs] * gu[:, cs]
            dmb = dmixed.astype(BF16)
            dws_ref[g] += lax.dot_general(dmb, vg, (((1,), (1,)), ((), ())), preferred_element_type=F32)
            dbs_ref[g] += jnp.broadcast_to(jnp.sum(dmixed, axis=1, keepdims=True), (CHUNK, LANE))
            dvn_parts.append(jnp.dot(wtt_ref[g], dmb, preferred_element_type=F32))
        dgu = jnp.concatenate(dgu_parts, axis=1)
        dvn = jnp.concatenate(dvn_parts, axis=1)
        dz_ref[:, 3 * D:4 * D] = (dgu * _dgelu(u, tu)).astype(BF16)
        vec_ref[3:4, :] += _rsum(dvn * nrm)
        vec_ref[4:5, :] += _rsum(dvn)
        dn = dvn * sln_ref[0:1, :]
        dgv = rstd * (dn - _rmean(dn) - nrm * _rmean(dn * nrm))
        dz_ref[:, 4 * D:5 * D] = (dgv * _dgelu(v, tv)).astype(BF16)
        a_c = col(5)
        sg = _sigmoid(col(6))
        ge[0:HALO, :] = keep * (pcol(5) * _sigmoid(pcol(6)))
        ge[HALO:HALO + tm, :] = a_c * sg
        _fill_shifted(ge, gr)
        o0 = HALO - (CFM_K - 1)
        conv = conv_ref[...].astype(F32)
        d = conv - _rmean(conv)
        rstd = lax.rsqrt(_rmean(d * d) + EPS)
        nrm = d * rstd
        ln = nrm * cv_ref[1:2, :] + cv_ref[2:3, :]
        sl = _sigmoid(ln)
        dln = dc_ref[...].astype(F32) * (sl * (1.0 + ln * (1.0 - sl)))
        vec_ref[6:7, :] += _rsum(dln * nrm)
        vec_ref[7:8, :] += _rsum(dln)
        dn = dln * cv_ref[1:2, :]
        dconv = rstd * (dn - _rmean(dn) - nrm * _rmean(dn * nrm))
        vec_ref[5:6, :] += _rsum(dconv)
        dce[0:tm, :] = dconv
        _fill_shifted(dce, dcr)
        _causal_conv(cw_ref, range(CFM_K), None, dce, dcr, [CFM_K - 1 - k for k in range(CFM_K)], tm, cbuf)
        dglu = cbuf[...]
        for cb in range(D // LANE):
            cs = slice(cb * LANE, (cb + 1) * LANE)
            dcv = dce[0:tm, cs]
            for k in range(CFM_K):
                prod = dcv * _rows_at(ge, gr, o0 + k, tm, cs)
                dcw8[k, :, cs] += jnp.sum(prod.reshape(tm // 8, 8, LANE), axis=0)

        @pl.when(i == nt - 1)
        def _():
            dcw_ref[...] = jnp.sum(dcw8[...], axis=1)
        dz_ref[:, 5 * D:6 * D] = (dglu * sg).astype(BF16)
        dz_ref[:, 6 * D:7 * D] = (dglu * a_c * (sg * (1.0 - sg))).astype(BF16)
        dqe[tm:tm + HALO, :] = dqe[0:HALO, :]
        dce[tm:tm + HALO, :] = dce[0:HALO, :]

    rev = lambda i: (nt - 1 - i, 0)
    rs = pl.BlockSpec((tm, D), rev)
    cur = pl.BlockSpec((tm, 7 * D), rev)
    prev = pl.BlockSpec((HALO, 7 * D), lambda i: (jnp.maximum((nt - 1 - i) * hb - 1, 0), 0))
    ext = pltpu.VMEM((HALO + tm, D), F32)
    outs = _pcall(
        body, grid=(nt,),
        in_specs=[cur, prev] + [pl.BlockSpec((None, tm, D), functools.partial(lambda i, n: (n, nt - 1 - i, 0), n=n))
                                for n in range(3)]
        + [rs, _const_spec((8, D)), _const_spec((8, D)), _const_spec((NG, CHUNK, CHUNK)),
                  _const_spec((NG, CHUNK, CHUNK)), _const_spec((CHUNK, D)), _const_spec((HALO, D)), _const_spec((8, D)),
                  ANY],
        out_specs=[cur, _const_spec((8, D)), _const_spec((HALO, D)), _const_spec((NG, CHUNK, CHUNK)),
                   _const_spec((NG, CHUNK, LANE))],
        out_shape=[_sds(dz.shape, BF16), _sds((8, D), F32), _sds((HALO, D), F32), _sds((NG, CHUNK, CHUNK), F32),
                   _sds((NG, CHUNK, LANE), F32)],
        scratch_shapes=[ext, ext, ext, ext, pltpu.VMEM((7, HALO + tm, D), F32), pltpu.VMEM((7, HALO + tm, D), F32),
                        pltpu.VMEM((tm, D), F32), pltpu.VMEM((HALO, 8, D), F32)],
        input_output_aliases={13: 0}, name=name,
        compiler_params=_params(("arbitrary",)))(z, z, dacts, dacts, dacts, conv, wsh, sgu_ln, wtril, wtril_t, bias_full, cw,
                                                 cvec, dz)
    return outs


def _ada_fwd(c_all, w_ada_loc, name):
    nb, D = c_all.shape
    L, _, nc = w_ada_loc.shape

    def body(c_ref, w_ref, o_ref, ca_ref):
        cv = c_ref[...]
        ca = cv * _sigmoid(cv)
        ca_ref[...] = ca
        o_ref[...] = jnp.dot(ca.astype(BF16), w_ref[...].astype(BF16), preferred_element_type=F32)

    return _pcall(body, grid=(L,),
                  in_specs=[_const_spec((nb, D)), pl.BlockSpec((None, D, nc), lambda l: (l, 0, 0))],
                  out_specs=[pl.BlockSpec((None, nb, nc), lambda l: (l, 0, 0)), _const_spec((nb, D))],
                  out_shape=[_sds((L, nb, nc), F32), _sds((nb, D), F32)], name=name,
                  compiler_params=_params(("arbitrary",)))(c_all, w_ada_loc)


def _adamw(w, g, m, v):
    m = ADAM_B1 * m + (1.0 - ADAM_B1) * g
    v = ADAM_B2 * v + (1.0 - ADAM_B2) * (g * g)
    m_hat = m / (1.0 - ADAM_B1 ** ADAM_STEP)
    v_hat = v / (1.0 - ADAM_B2 ** ADAM_STEP)
    delta = -ADAM_LR * (m_hat / (jnp.sqrt(v_hat) + ADAM_EPS) + ADAM_WD * w)
    return delta, m, v


def _tile_rows(R, C, align=8):
    cap = max(align, (1536 * 1024) // (4 * C))
    best = None
    for t in range(align, R + 1, align):
        if R % t == 0 and t <= cap:
            best = t
    return R if best is None else best


def _adam_ada(ct, dm, w, m, v, name):
    L, D, nc = w.shape
    nb = ct.shape[1]
    tr = _tile_rows(D, nc)

    def body(ct_ref, dm_ref, w_ref, m_ref, v_ref, g_ref, d_ref, mo_ref, vo_ref):
        g = ct_ref[:, 0:1] * dm_ref[0:1, :]
        for b in range(1, nb):
            g = g + ct_ref[:, b:b + 1] * dm_ref[b:b + 1, :]
        g_ref[...] = g
        d_ref[...], mo_ref[...], vo_ref[...] = _adamw(w_ref[...], g, m_ref[...], v_ref[...])

    ws = pl.BlockSpec((None, tr, nc), lambda l, r: (l, r, 0))
    return _pcall(body, grid=(L, D // tr),
                  in_specs=[pl.BlockSpec((tr, nb), lambda l, r: (r, 0)), pl.BlockSpec((None, nb, nc), lambda l, r: (l, 0, 0)),
                            ws, ws, ws],
                  out_specs=[ws] * 4, out_shape=[_sds(w.shape, F32)] * 4, name=name,
                  compiler_params=_params(("parallel", "parallel")))(ct, dm, w, m, v)


def _adam_small(parts, w, m, v, name, deps=(), single_rows=()):
    n, R, C = parts.shape
    tr = _tile_rows(R, C * n // 2)
    nl = len(single_rows[0]) if single_rows else 0

    def body(p_ref, w_ref, m_ref, v_ref, g_ref, d_ref, mo_ref, vo_ref, *single):
        g = p_ref[0]
        for j in range(1, n):
            g = g + p_ref[j]
        d, mo, vo = _adamw(w_ref[...], g, m_ref[...], v_ref[...])
        g_ref[...], d_ref[...], mo_ref[...], vo_ref[...] = g, d, mo, vo
        step = pl.program_id(0)
        for pi, rows in enumerate(single_rows):
            for l, row in enumerate(rows):
                @pl.when(step == row // tr)
                def _(pi=pi, l=l, off=row % tr):
                    for k, val in enumerate((g, d, mo, vo)):
                        single[4 * pi + k][l:l + 1, :] = val[off:off + 1, :]

    ws = pl.BlockSpec((tr, C), lambda r: (r, 0))
    one = pl.BlockSpec((nl, C), lambda r: (0, 0))
    outs = _pcall(_after(body, 4, deps), grid=(R // tr,),
                  in_specs=[pl.BlockSpec((n, tr, C), lambda r: (0, r, 0)), ws, ws, ws] + [ANY] * len(deps),
                  out_specs=[ws] * 4 + [one] * (4 * len(single_rows)),
                  out_shape=[_sds((R, C), F32)] * 4 + [_sds((nl, C), F32)] * (4 * len(single_rows)), name=name,
                  compiler_params=_params(("arbitrary",)))(parts, w, m, v, *deps)
    return outs[:4], outs[4:]


def _adam_plain(g, w, m, v, name):
    R, C = w.shape

    def body(g_ref, w_ref, m_ref, v_ref, d_ref, mo_ref, vo_ref):
        d_ref[...], mo_ref[...], vo_ref[...] = _adamw(w_ref[...], g_ref[...], m_ref[...], v_ref[...])

    ws = _const_spec((R, C))
    return _pcall(body, grid=(1,), in_specs=[ws] * 4, out_specs=[ws] * 3, out_shape=[_sds((R, C), F32)] * 3, name=name,
                  compiler_params=_params(("arbitrary",)))(g, w, m, v)


def _pair_sum(G, R1, my_c, name):
    n, R, C = G.shape
    half = n // 2
    tr = _tile_rows(R, C, align=16)

    def body(c_ref, g_ref, r_ref, o_ref):
        o_ref[...] = (g_ref[...].astype(F32) + r_ref[...].astype(F32)).astype(o_ref.dtype)

    blk = (None, tr, C)
    gs = pltpu.PrefetchScalarGridSpec(
        num_scalar_prefetch=1, grid=(half, R // tr),
        in_specs=[pl.BlockSpec(blk, lambda p, r, c: (2 * p + c[0], r, 0)), pl.BlockSpec(blk, lambda p, r, c: (p, r, 0))],
        out_specs=pl.BlockSpec(blk, lambda p, r, c: (p, r, 0)))
    return _pcall(body, grid_spec=gs, out_shape=_sds((half, R, C), G.dtype), name=name,
                  compiler_params=_params(("parallel", "parallel")))(my_c, G, R1)


def _adam_big(P, R2, my_chip, w, m, v, layer, prev, name, deps=()):
    _, R, C = P.shape
    nrecv = R2.shape[0]
    tr = _tile_rows(R, C, align=16)

    def body(p_sm, p_ref, r_ref, w_ref, m_ref, v_ref, *rest):
        g_ref, d_ref, mo_ref, vo_ref = rest[-4:]
        g = p_ref[...].astype(F32)
        for k in range(nrecv):
            g = g + r_ref[k].astype(F32)
        g_ref[...] = g
        d_ref[...], mo_ref[...], vo_ref[...] = _adamw(w_ref[...], g, m_ref[...], v_ref[...])

    ws = pl.BlockSpec((None, tr, C), lambda r, p: (layer, r, 0))
    held = [] if prev is None else list(prev)
    gs = pltpu.PrefetchScalarGridSpec(
        num_scalar_prefetch=1, grid=(R // tr,),
        in_specs=[pl.BlockSpec((None, tr, C), lambda r, p: (p[0], r, 0)),
                  pl.BlockSpec((nrecv, tr, C), lambda r, p: (0, r, 0)), ws, ws, ws] + [ANY] * (len(held) + len(deps)),
        out_specs=[ws] * 4)
    alias = {6 + i: i for i in range(len(held))}
    return _pcall(body, grid_spec=gs, out_shape=[_sds(w.shape, F32)] * 4, name=name, input_output_aliases=alias,
                  compiler_params=_params(("parallel",)))(my_chip, P, R2, w, m, v, *held, *deps)


def _place():
    return lax.axis_index("x"), lax.axis_index("y"), lax.axis_index("c")


def _all_gather(shards, name, deps=(), into=None):
    n = len(shards)
    bufs = [] if into is None else [b for b, _ in into]
    nb = len(bufs)

    def body(*refs):
        ins, outs = refs[:n], refs[n + nb:2 * n + nb]
        send_sems, recv_sems, local_sems = refs[2 * n + nb:]
        x, y, c = _place()
        me, sibling = (x, y, c), (x, y, 1 - c)
        chips = [(1 - x, y), (x, 1 - y), (1 - x, 1 - y)]

        def slot(a, px, py, pc):
            block = outs[a].at[4 * px + 2 * py + pc]
            return block if into is None else block.at[pl.ds(into[a][1], ins[a].shape[0])]

        def copy(a, k, block, to, src=None):
            return pltpu.make_async_remote_copy(
                src_ref=slot(a, *block) if src is None else src, dst_ref=slot(a, *block),
                send_sem=send_sems.at[7 * a + k], recv_sem=recv_sems.at[7 * a + k], device_id=to, device_id_type=MESH)

        mine = [pltpu.make_async_copy(ins[a], slot(a, *me), local_sems.at[a]) for a in range(n)]
        for cp in mine:
            cp.start()
        first = []
        for a in range(n):
            first.append(copy(a, 0, me, sibling, src=ins[a]))
            first += [copy(a, 1 + j, me, (*chip, c), src=ins[a]) for j, chip in enumerate(chips)]
        for cp in first:
            cp.start()
        passed = []
        for j, chip in enumerate(chips):
            for a in range(n):
                copy(a, 1 + j, (*chip, c), me).wait_recv()
                fwd = copy(a, 4 + j, (*chip, c), sibling)
                fwd.start()
                passed.append(fwd)
        for a in range(n):
            copy(a, 0, sibling, me).wait_recv()
        for j, chip in enumerate(chips):
            for a in range(n):
                copy(a, 4 + j, (*chip, 1 - c), me).wait_recv()
        for cp in first + passed:
            cp.wait_send()
        for cp in mine:
            cp.wait()

    out_shape = [_sds((NDEV,) + s.shape, s.dtype) for s in shards] if into is None else [_sds(b.shape, b.dtype) for b in bufs]
    outs = _pcall(_after(body, n + nb, deps), in_specs=[ANY] * (n + nb + len(deps)), out_specs=[ANY] * n,
                  out_shape=out_shape, input_output_aliases={n + a: a for a in range(nb)},
                  scratch_shapes=[pltpu.SemaphoreType.DMA((7 * n,)), pltpu.SemaphoreType.DMA((7 * n,)),
                                  pltpu.SemaphoreType.DMA((n,))], name=name)(*shards, *bufs, *deps)
    return list(outs)


HBM = pl.BlockSpec(memory_space=pltpu.HBM)
SEM = pl.BlockSpec(memory_space=pltpu.SEMAPHORE)


def _copies(plan, refs, send_sems, recv_sems):
    return [pltpu.make_async_remote_copy(src_ref=s, dst_ref=d, send_sem=send_sems.at[k], recv_sem=recv_sems.at[k],
                                         device_id=dev, device_id_type=MESH)
            for k, (s, d, dev) in enumerate(plan(refs, *_place()))]


def _xfer_start(bufs, ncopies, plan, name, deps=()):
    n = len(bufs)

    def body(*refs):
        for cp in _copies(plan, refs[:n], refs[n], refs[n + 1]):
            cp.start()
        token = refs[2 * n + 2]
        token[...] = jnp.zeros_like(token)

    outs = _pcall(
        _after(body, n, deps), name=name,
        out_shape=(pltpu.SemaphoreType.DMA((ncopies,)), pltpu.SemaphoreType.DMA((ncopies,)),
                   *[pltpu.HBM(b.shape, b.dtype) for b in bufs], _sds((8, LANE), F32)),
        in_specs=[HBM] * n + [ANY] * len(deps),
        out_specs=(SEM, SEM, *[HBM] * n, pl.BlockSpec(memory_space=pltpu.VMEM)),
        input_output_aliases={i: 2 + i for i in range(n)},
        compiler_params=pltpu.CompilerParams(has_side_effects=pltpu.SideEffectType.DATAFLOW_SIDE_EFFECTING),
    )(*[pltpu.with_memory_space_constraint(b, pltpu.HBM) for b in bufs], *deps)
    return (outs[0], outs[1]), list(outs[2:2 + n]), outs[2 + n]


def _xfer_wait(sems, bufs, plan, after, name):
    n = len(bufs)
    after = list(after) if isinstance(after, (list, tuple)) else [after]

    def body(*refs):
        for cp in _copies(plan, refs[:n], refs[n], refs[n + 1]):
            cp.wait_send()
            cp.wait_recv()

    outs = _pcall(
        body, name=name, out_shape=tuple(pltpu.HBM(b.shape, b.dtype) for b in bufs),
        in_specs=[HBM] * n + [SEM, SEM] + [ANY] * len(after), out_specs=tuple([HBM] * n),
        input_output_aliases={i: i for i in range(n)},
        compiler_params=pltpu.CompilerParams(has_side_effects=pltpu.SideEffectType.DATAFLOW_SIDE_EFFECTING),
    )(*bufs, *sems, *after)
    return list(outs)


def _chips_of(x, y):
    return [(1 - x, y), (x, 1 - y), (1 - x, 1 - y)]


def _landing(ref, dev_index, rows):
    block = ref.at[dev_index]
    return block if rows is None else block.at[pl.ds(rows[0], rows[1])]


def _gather_plan1(n, rows=None):
    def plan(refs, x, y, c):
        out = []
        for a in range(n):
            blk = _landing(refs[a], 4 * x + 2 * y + c, rows)
            out.append((blk, blk, (x, y, 1 - c)))
            out += [(blk, blk, (px, py, c)) for px, py in _chips_of(x, y)]
        return out
    return plan


def _gather_plan2(n, rows=None):
    def plan(refs, x, y, c):
        out = []
        for a in range(n):
            for px, py in _chips_of(x, y):
                blk = _landing(refs[a], 4 * px + 2 * py + c, rows)
                out.append((blk, blk, (x, y, 1 - c)))
        return out
    return plan


def _gather_start(shards, dev, name, deps=(), within=None):
    rows = None if within is None else (within[0], shards[0].shape[0])
    lands = []
    for s in shards:
        shape = (NDEV,) + s.shape if within is None else (NDEV, within[1]) + s.shape[1:]
        start = (dev,) + (0,) * s.ndim if within is None else (dev, within[0]) + (0,) * (s.ndim - 1)
        lands.append(lax.dynamic_update_slice(lax.empty(shape, s.dtype), s[None], start))
    n = len(shards)
    sems, lands, tok = _xfer_start(lands, 4 * n, _gather_plan1(n, rows), name + "_p1_start", deps)
    return dict(sems=sems, lands=lands, tok=tok, n=n, rows=rows)


def _gather_mid(st, after, name):
    n, rows = st["n"], st["rows"]
    lands = _xfer_wait(st["sems"], st["lands"], _gather_plan1(n, rows), after, name + "_p1_wait")
    sems, lands, tok = _xfer_start(lands, 3 * n, _gather_plan2(n, rows), name + "_p2_start")
    return dict(sems=sems, lands=lands, tok=tok, n=n, rows=rows)


def _gather_finish(st, after, name):
    return _xfer_wait(st["sems"], st["lands"], _gather_plan2(st["n"], st["rows"]), after, name + "_p2_wait")


def _scatter_plan1(n):
    def plan(refs, x, y, c):
        return [(refs[a].at[2 * p + 1 - c], refs[n + a].at[p], (x, y, 1 - c)) for a in range(n) for p in range(NCHIP)]
    return plan


def _scatter_plan2(n):
    def plan(refs, x, y, c):
        return [(refs[a].at[2 * px + py], refs[n + a].at[j], (px, py, c))
                for a in range(n) for j, (px, py) in enumerate(_chips_of(x, y))]
    return plan


def _scatter_start(Gs, name):
    n = len(Gs)
    R1s = [lax.empty((NCHIP,) + g.shape[1:], g.dtype) for g in Gs]
    sems, bufs, tok = _xfer_start(list(Gs) + R1s, NCHIP * n, _scatter_plan1(n), name + "_s1_start")
    return dict(sems=sems, bufs=bufs, tok=tok, n=n)


def _scatter_mid(st, after, my_c, name):
    n = st["n"]
    bufs = _xfer_wait(st["sems"], st["bufs"], _scatter_plan1(n), after, name + "_s1_wait")
    Ps = [_pair_sum(bufs[a], bufs[n + a], my_c, f"{name}_pair_sum{a}") for a in range(n)]
    R2s = [lax.empty((3,) + p.shape[1:], p.dtype) for p in Ps]
    sems, bufs, tok = _xfer_start(Ps + R2s, 3 * n, _scatter_plan2(n), name + "_s2_start")
    return dict(sems=sems, bufs=bufs, tok=tok, n=n)


def _scatter_finish(st, after, name):
    n = st["n"]
    bufs = _xfer_wait(st["sems"], st["bufs"], _scatter_plan2(n), after, name + "_s2_wait")
    return bufs[:n], bufs[n:]


SMALL_ROWS = {"norm1_g": (0, 1), "norm2_g": (1, 1), "sgu_ln_g": (2, 1), "sgu_ln_b": (3, 1), "cfm_conv_b": (4, 1),
              "cfm_ln_g": (5, 1), "cfm_ln_b": (6, 1), "b_sgu": (7, 1), "w_sgu": (8, 128), "b_ada": (136, N_MOD),
              "w_short": (142, SHORT_K), "cfm_conv_w": (145, CFM_K)}
ROWS_PER_LAYER = 176
FINAL_ROW = DEPTH * ROWS_PER_LAYER
PACK_ROWS = 360


def _pack(get, D, layers=tuple(range(DEPTH)), tail=True):
    parts = []
    for l in layers:
        for name, (_, nrows) in SMALL_ROWS.items():
            a = get(name, l)
            parts.append(jnp.zeros((nrows * D,), F32) if a is None else a.astype(F32).reshape(nrows * D))
    if tail:
        for name in ("final_g", "loss"):
            a = get(name, None)
            parts.append(jnp.zeros((D,), F32) if a is None else a.astype(F32).reshape(D))
        parts.append(jnp.zeros(((PACK_ROWS - FINAL_ROW - 2) * D,), F32))
    return jnp.concatenate(parts).reshape(-1, D)


def _unpack(pack, name, shape):
    D = pack.shape[1]
    r0, nrows = SMALL_ROWS[name]
    return jnp.stack([pack[l * ROWS_PER_LAYER + r0:l * ROWS_PER_LAYER + r0 + nrows] for l in range(DEPTH)]).reshape(shape)


def _mm_tiles(S):
    return min(512, S), min(1024, S), min(2048, S)


def kernel(x, c, w_ada, b_ada, norm1_g, w_in, w_short, w_a_out, sgu_ln_g, sgu_ln_b, w_sgu, b_sgu, w_b_out, cfm_conv_w, cfm_conv_b, cfm_ln_g, cfm_ln_b, w_c_out, w_o, norm2_g, w_ffn_in, w_ffn_out, final_g, loss_target, m_w_ada, m_b_ada, m_norm1_g, m_w_in, m_w_short, m_w_a_out, m_sgu_ln_g, m_sgu_ln_b, m_w_sgu, m_b_sgu, m_w_b_out, m_cfm_conv_w, m_cfm_conv_b, m_cfm_ln_g, m_cfm_ln_b, m_w_c_out, m_w_o, m_norm2_g, m_w_ffn_in, m_w_ffn_out, m_final_g, v_w_ada, v_b_ada, v_norm1_g, v_w_in, v_w_short, v_w_a_out, v_sgu_ln_g, v_sgu_ln_b, v_w_sgu, v_b_sgu, v_w_b_out, v_cfm_conv_w, v_cfm_conv_b, v_cfm_ln_g, v_cfm_ln_b, v_w_c_out, v_w_o, v_norm2_g, v_w_ffn_in, v_w_ffn_out, v_final_g):
    W = dict(w_ada=w_ada, b_ada=b_ada, norm1_g=norm1_g, w_in=w_in, w_short=w_short, w_a_out=w_a_out, sgu_ln_g=sgu_ln_g,
             sgu_ln_b=sgu_ln_b, w_sgu=w_sgu, b_sgu=b_sgu, w_b_out=w_b_out, cfm_conv_w=cfm_conv_w, cfm_conv_b=cfm_conv_b,
             cfm_ln_g=cfm_ln_g, cfm_ln_b=cfm_ln_b, w_c_out=w_c_out, w_o=w_o, norm2_g=norm2_g, w_ffn_in=w_ffn_in,
             w_ffn_out=w_ffn_out, final_g=final_g)
    Mo = dict(w_ada=m_w_ada, b_ada=m_b_ada, norm1_g=m_norm1_g, w_in=m_w_in, w_short=m_w_short, w_a_out=m_w_a_out,
              sgu_ln_g=m_sgu_ln_g, sgu_ln_b=m_sgu_ln_b, w_sgu=m_w_sgu, b_sgu=m_b_sgu, w_b_out=m_w_b_out,
              cfm_conv_w=m_cfm_conv_w, cfm_conv_b=m_cfm_conv_b, cfm_ln_g=m_cfm_ln_g, cfm_ln_b=m_cfm_ln_b,
              w_c_out=m_w_c_out, w_o=m_w_o, norm2_g=m_norm2_g, w_ffn_in=m_w_ffn_in, w_ffn_out=m_w_ffn_out,
              final_g=m_final_g)
    Vo = dict(w_ada=v_w_ada, b_ada=v_b_ada, norm1_g=v_norm1_g, w_in=v_w_in, w_short=v_w_short, w_a_out=v_w_a_out,
              sgu_ln_g=v_sgu_ln_g, sgu_ln_b=v_sgu_ln_b, w_sgu=v_w_sgu, b_sgu=v_b_sgu, w_b_out=v_w_b_out,
              cfm_conv_w=v_cfm_conv_w, cfm_conv_b=v_cfm_conv_b, cfm_ln_g=v_cfm_ln_g, cfm_ln_b=v_cfm_ln_b,
              w_c_out=v_w_c_out, w_o=v_w_o, norm2_g=v_norm2_g, w_ffn_in=v_w_ffn_in, w_ffn_out=v_w_ffn_out,
              final_g=v_final_g)
    order = ["w_ada", "b_ada", "norm1_g", "w_in", "w_short", "w_a_out", "sgu_ln_g", "sgu_ln_b", "w_sgu", "b_sgu",
             "w_b_out", "cfm_conv_w", "cfm_conv_b", "cfm_ln_g", "cfm_ln_b", "w_c_out", "w_o", "norm2_g", "w_ffn_in",
             "w_ffn_out", "final_g"]

    assert DEPTH == 2, "the weight-gather schedule below is written for two layers"
    S, D = x.shape[1], x.shape[2]
    F2 = w_ffn_in.shape[2] * NDEV
    FF = F2 // 2
    xi, yi, ci = _place()
    dev = 4 * xi + 2 * yi + ci
    my_c = jnp.reshape(ci, (1,)).astype(jnp.int32)
    my_chip = jnp.reshape(2 * xi + yi, (1,)).astype(jnp.int32)
    tm, tm_big, tm_huge = _mm_tiles(S)
    x0 = x.reshape(S, D)
    tgt = loss_target.reshape(S, D)

    def shards_of(l):
        return [w_in[l].astype(BF16), w_a_out[l].astype(BF16), w_b_out[l].astype(BF16), w_c_out[l].astype(BF16),
                w_o[l].astype(BF16), w_ffn_in[l].astype(BF16), w_ffn_out[l].astype(BF16)]

    c_all = _all_gather([jnp.pad(c, ((0, 7), (0, 0)))], "ag_c")[0][:, 0, :]
    modpart, c_act = _ada_fwd(c_all, w_ada, "ada_fwd")
    ncol = modpart.shape[2]
    mg = _all_gather([modpart.reshape(DEPTH * NDEV, ncol)], "ag_mod")[0].reshape(NDEV, DEPTH, NDEV, ncol)
    mine = lax.dynamic_index_in_dim(mg, dev, axis=2, keepdims=False)
    mod = (jnp.transpose(mine, (1, 0, 2)).reshape(DEPTH, N_MOD * D) + b_ada).reshape(DEPTH, N_MOD, D)

    ncs = w_short.shape[2]
    ag_in0 = _gather_start([w_in[0].astype(BF16), w_short.reshape(DEPTH * SHORT_K, ncs),
                            cfm_conv_w.reshape(DEPTH * CFM_K, ncs)], dev, "ag_w_in0", deps=(mod,))
    W, Mo, Vo = lax.optimization_barrier((ag_in0["tok"], (W, Mo, Vo)))[1]
    (norm1_g, norm2_g, w_in, w_a_out, w_b_out, w_c_out, w_o, w_ffn_in, w_ffn_out, sgu_ln_g, sgu_ln_b, w_sgu, b_sgu,
     cfm_conv_b, cfm_ln_g, cfm_ln_b, final_g) = [W[k] for k in (
         "norm1_g", "norm2_g", "w_in", "w_a_out", "w_b_out", "w_c_out", "w_o", "w_ffn_in", "w_ffn_out", "sgu_ln_g",
         "sgu_ln_b", "w_sgu", "b_sgu", "cfm_conv_b", "cfm_ln_g", "cfm_ln_b", "final_g")]
    m_w_ada, v_w_ada = Mo["w_ada"], Vo["w_ada"]
    xl0, h0, ht0 = _norm_fwd(x0, None, _rows(jnp.zeros((D,), F32), norm1_g[0], mod[0, 1], mod[0, 0]), "norm1_fwd0",
                             deps=(ag_in0["tok"],))
    ag_rest0 = _gather_start(shards_of(0)[1:], dev, "ag_rest0", deps=(h0,))

    tril = jnp.tril(jnp.ones((CHUNK, CHUNK), dtype=bool))

    def layer_consts(l):
        wt = jnp.where(tril[None], w_sgu[l], 0.0).astype(BF16)
        return dict(sgu_ln=_rows(sgu_ln_g[l], sgu_ln_b[l]), wtril=wt, wtril_t=jnp.swapaxes(wt, 1, 2),
                    bias_full=jnp.repeat(b_sgu[l].T, LANE, axis=1), cvec=_rows(cfm_conv_b[l], cfm_ln_g[l], cfm_ln_b[l]))

    def rest_of(g):
        return dict(w_a=g[0].reshape(1, D, D), w_b=g[1].reshape(1, D, D), w_c=g[2].reshape(1, D, D),
                    w_o=g[3].reshape(1, D, D), w_fi=jnp.transpose(g[4], (1, 0, 2)).reshape(1, D, F2),
                    w_fo=g[5].reshape(1, FF, D))

    sharded_small = ("w_short", "cfm_conv_w")

    def param_get(T):
        def get(name, l):
            if name == "final_g":
                return T[name]
            return None if name in sharded_small or name == "loss" else T[name][l]
        return get

    packs = [_pack(param_get(T), D) for T in (W, Mo, Vo)]
    ag_in0 = _gather_mid(ag_in0, [ag_rest0["tok"], *packs], "ag_w_in0")
    (w_sgu, b_sgu, sgu_ln_g, sgu_ln_b, cfm_conv_b, cfm_ln_g, cfm_ln_b), conv_wmv_in = lax.optimization_barrier(
        (ag_in0["tok"], ((w_sgu, b_sgu, sgu_ln_g, sgu_ln_b, cfm_conv_b, cfm_ln_g, cfm_ln_b),
                         [(T["w_short"], T["cfm_conv_w"]) for T in (W, Mo, Vo)])))[1]
    consts = [layer_consts(l) for l in range(DEPTH)]
    ncr = DEPTH * (SHORT_K + CFM_K)
    padr = (-ncr) % 8
    convw_wmv = [jnp.pad(jnp.concatenate([a.reshape(-1, ncs), b.reshape(-1, ncs)]), ((0, padr), (0, 0)))
                 for a, b in conv_wmv_in]
    g_in0 = _gather_finish(ag_in0, [*convw_wmv] + [a for cl in consts for a in cl.values()], "ag_w_in0")
    w_short_full = jnp.transpose(g_in0[1], (1, 0, 2)).reshape(DEPTH, SHORT_K, D)
    cfm_w_full = jnp.transpose(g_in0[2], (1, 0, 2)).reshape(DEPTH, CFM_K, D)
    for l in range(DEPTH):
        consts[l]["wsh"] = jnp.pad(w_short_full[l], ((0, 8 - SHORT_K), (0, 0)))
        consts[l]["cw"] = jnp.pad(cfm_w_full[l], ((0, HALO - CFM_K), (0, 0)))
    Wg = [dict(w_in=g_in0[0]), None]
    ag_l1 = None
    nin = w_in.shape[2]
    tn_in = nin if nin % 256 == 0 and nin <= 1280 else 256
    tn_fi = 512 if F2 % 512 == 0 else 256
    tn_dw = min(256, D)

    saved = []
    xcur, gprev, ffn_tail = x0, None, None
    for l in range(DEPTH):
        sh1, sc1, g1, sh2, sc2, g2 = [mod[l, k] for k in range(N_MOD)]
        cl = consts[l]
        if l == 0:
            xl, h, ht = xl0, h0, ht0
        else:
            vec1 = _rows(gprev, norm1_g[l], sc1, sh1)
            act_prev, w_fo_prev = ffn_tail
            ag_l1 = _gather_mid(ag_l1, act_prev, f"ag_w{l}")
            f_prev, xl, h, ht = _mm_resid_norm(act_prev, w_fo_prev, xcur, vec1, tm, f"mm_ffn_out_norm1_{l}",
                                               deps=(ag_l1["tok"],))
            saved[l - 1]["f"] = f_prev
            g = _gather_finish(ag_l1, h, f"ag_w{l}")
            Wg[l] = dict(w_in=g[0], **rest_of(g[1:]))
        wl = Wg[l]
        z = _mm_nn(h, wl["w_in"], BF16, tm_huge, tn_in, D, f"mm_in{l}", w_outer=True)
        mix_deps = ()
        if l == 0:
            ag_rest0 = _gather_mid(ag_rest0, z, "ag_rest0")
            mix_deps = (ag_rest0["tok"],)
            if DEPTH > 1:
                ag_l1 = _gather_start(shards_of(1), dev, "ag_w1")
                mix_deps += (ag_l1["tok"],)
        acts, acts_t, conv = _mixer_fwd(z, cl["wsh"], cl["sgu_ln"], cl["wtril"], cl["bias_full"], cl["cw"], cl["cvec"],
                                        f"mixer_fwd{l}", deps=mix_deps)
        if l == 0:
            wl.update(rest_of(_gather_finish(ag_rest0, acts[0], "ag_rest0")))
        merged, merged_t, ys = _branch_out(acts, [wl["w_a"][0], wl["w_b"][0], wl["w_c"][0]], z, f"branch_out{l}")
        o, x1, h2, h2t = _mm_resid_norm(merged, wl["w_o"], xl, _rows(g1, norm2_g[l], sc2, sh2), tm, f"mm_o_norm2_{l}")
        gu, act, act_t = _ffn_in_swiglu(h2, wl["w_fi"], tm_huge, 256, f"mm_ffn_in{l}")
        saved.append(dict(xl=xl, ht=ht, z=z, acts_t=acts_t, conv=conv, ys=ys, merged_t=merged_t, o=o, x1=x1, h2t=h2t, gu=gu,
                          act_t=act_t, f=None, consts=cl, mod=(sh1, sc1, g1, sh2, sc2, g2)))
        xcur, gprev, ffn_tail = x1, g2, (act, wl["w_fo"])

    last = saved[-1]
    dxup, dfb, fsums, loss_blk = _final_bwd(last["x1"], *ffn_tail, tgt, _rows(last["mod"][5], final_g), "final_bwd")
    loss_row = jnp.pad(loss_blk[0, 0:1], (0, D - 1))
    dgate2_next = fsums[1]
    small = [dict() for _ in range(DEPTH)]
    dmods = [None] * DEPTH
    nfi = w_ffn_in.shape[2]
    early_names, late_names = ["w_ffn_out", "w_ffn_in", "w_o"], ["w_a_out", "w_b_out", "w_c_out", "w_in"]
    results = {n: None for n in early_names + late_names}

    def adam_group(names, Ps, R2s, l, deps=()):
        for n, p, r2 in zip(names, Ps, R2s):
            results[n] = _adam_big(p, r2, my_chip, W[n], Mo[n], Vo[n], l, results[n], f"adam_{n}{l}", deps)

    deferred = []
    late_prev = None
    ag_s1, gathered1 = None, None
    tk_w = min(2048, S)
    tn_dw_in = tn_in // 2 if tn_in == 1280 else tn_in
    for l in reversed(range(DEPTH)):
        sv, wl, cl = saved[l], Wg[l], saved[l]["consts"]
        sh1, sc1, g1, sh2, sc2, g2 = sv["mod"]
        dgu = _swiglu_bwd(dfb, wl["w_fo"], sv["gu"], f"mm_dact_swiglu_bwd{l}",
                          deps=() if late_prev is None else (late_prev["tok"], ag_s1["tok"]))
        g_fo = _mm_wgrad(sv["act_t"], dfb, 1, FF // 2, D, tk_w, f"mm_dw_ffn_out{l}")
        g_fi = _mm_wgrad(sv["h2t"], dgu, 1, D, tn_fi, S, f"mm_dw_ffn_in{l}")
        if late_prev is not None:
            deferred.append((late_names, *_scatter_finish(late_prev, g_fi, f"rs_late{l + 1}"), l + 1))
            late_prev = None
        if ag_s1 is not None:
            ag_s1 = _gather_mid(ag_s1, g_fi, "ag_small1")
        dx1, dob, s2 = _norm_bwd(sv["x1"], (dgu, wl["w_fi"]), dxup, _rows(norm2_g[l], sc2, g1), sv["o"],
                                 f"mm_dh2_norm2_bwd{l}", deps=() if ag_s1 is None else (ag_s1["tok"],))
        dmerged = _mm_nt(dob, wl["w_o"], BF16, tm_big, D, D, f"mm_dmerged{l}")
        g_o = _mm_wgrad(sv["merged_t"], dob, 1, D, tn_dw, S, f"mm_dw_o{l}")
        early = _scatter_start([g_fo.reshape(NDEV, FF // NDEV, D),
                                jnp.transpose(g_fi.reshape(D, NDEV, nfi), (1, 0, 2)),
                                g_o.reshape(NDEV, D // NDEV, D)], f"rs_early{l}")
        dys, dz = _gate_bwd(dmerged, sv["z"], sv["ys"], f"gate_bwd{l}", deps=(early["tok"],))
        if ag_s1 is not None:
            gathered1 = _gather_finish(ag_s1, dys, "ag_small1")[0]
            ag_s1 = None
        early = _scatter_mid(early, dys, my_c, f"rs_early{l}")
        dacts = _mm3_nt(dys, [wl["w_a"], wl["w_b"], wl["w_c"]], tm_big, f"mm_dact_abc{l}", deps=(early["tok"],))
        g3 = _mm3_wgrad(sv["acts_t"], dys, tn_dw, f"mm_dw_abc{l}")
        g_abc = [g3[n] for n in range(3)]
        dz, mvec, dcw, dws, dbs = _mixer_bwd(sv["z"], dacts, sv["conv"], dz, cl["wsh"], cl["sgu_ln"], cl["wtril"],
                                             cl["wtril_t"], cl["bias_full"], cl["cw"], cl["cvec"], f"mixer_bwd{l}")
        g_in = _mm_wgrad(sv["ht"], dz, NDEV, D, tn_dw_in, S, f"mm_dw_in{l}")
        late = _scatter_start([g.reshape(NDEV, D // NDEV, D) for g in g_abc] + [g_in], f"rs_late{l}")
        if l > 0:
            pv = saved[l - 1]
            dxup, dfb, s1 = _norm_bwd(sv["xl"], (dz, wl["w_in"]), dx1, _rows(norm1_g[l], sc1, pv["mod"][5]), pv["f"],
                                      f"mm_dh_norm1_bwd{l}", deps=(late["tok"],))
        else:
            dxup, dfb, s1 = _norm_bwd(sv["xl"], (dz, wl["w_in"]), dx1, _rows(norm1_g[l], sc1), None,
                                      f"mm_dh_norm1_bwd{l}", deps=(late["tok"],))
        deferred.append((early_names, *_scatter_finish(early, dxup, f"rs_early{l}"), l))
        dmods[l] = jnp.stack([s1[0], s1[1], s2[3], s2[0], s2[1], dgate2_next])
        dgate2_next = s1[3]
        small[l] = dict(norm1_g=s1[2], norm2_g=s2[2], sgu_ln_g=mvec[3], sgu_ln_b=mvec[4], cfm_conv_b=mvec[5],
                        cfm_ln_g=mvec[6], cfm_ln_b=mvec[7], b_sgu=dbs[:, :, 0],
                        w_sgu=jnp.where(tril[None], dws, 0.0), b_ada=dmods[l], w_short=mvec[0:SHORT_K],
                        cfm_conv_w=dcw[0:CFM_K])
        small_get = lambda name, k: {"final_g": fsums[0], "loss": loss_row}.get(name) if k is None else small[k][name]
        if l > 0:
            late_prev = _scatter_mid(late, dxup, my_c, f"rs_late{l}")
            ag_s1 = _gather_start([_pack(small_get, D, layers=(l,), tail=True)], dev, "ag_small1", deps=(late_prev["tok"],),
                                  within=(l * ROWS_PER_LAYER, PACK_ROWS))
    grad_x = dxup.reshape(x.shape)

    gathered = _all_gather([_pack(small_get, D, layers=(0,), tail=False)], "ag_small0", deps=(dxup,),
                           into=[(gathered1, 0)])[0]
    late_prev = _scatter_mid(late, gathered, my_c, "rs_late0")
    one_row = [n for n in order if n in SMALL_ROWS and SMALL_ROWS[n][1] == 1 and W[n].ndim == 2]
    (sg, sd, sm, sv_), singles = _adam_small(
        gathered, *packs, name="adam_small", deps=(late_prev["tok"],),
        single_rows=[tuple(l * ROWS_PER_LAYER + SMALL_ROWS[n][0] for l in range(DEPTH)) for n in one_row])
    loss = sg[FINAL_ROW + 1, 0]
    out = {n: tuple(singles[4 * i:4 * i + 4]) for i, n in enumerate(one_row)}
    for name in order:
        if name in SMALL_ROWS and name not in sharded_small and name not in out:
            out[name] = tuple(_unpack(p, name, W[name].shape) for p in (sg, sd, sm, sv_))
    out["final_g"] = tuple(p[FINAL_ROW] for p in (sg, sd, sm, sv_))

    def my_cols(name):
        full = _unpack(sg, name, (DEPTH, SMALL_ROWS[name][1], D))
        return lax.dynamic_slice_in_dim(full, dev * ncs, ncs, axis=2)

    gcs = jnp.concatenate([my_cols("w_short").reshape(-1, ncs), my_cols("cfm_conv_w").reshape(-1, ncs)])
    cd, cm, cv = _adam_plain(jnp.pad(gcs, ((0, padr), (0, 0))), *convw_wmv, "adam_convw")
    nsh = DEPTH * SHORT_K
    out["w_short"] = tuple(a[0:nsh].reshape(w_short.shape) for a in (gcs, cd, cm, cv))
    out["cfm_conv_w"] = tuple(a[nsh:ncr].reshape(cfm_conv_w.shape) for a in (gcs, cd, cm, cv))

    dm_all = jnp.stack([gathered[:, l * ROWS_PER_LAYER + 136:l * ROWS_PER_LAYER + 136 + N_MOD, :].reshape(NDEV, N_MOD * D)
                        for l in range(DEPTH)])
    dm_mine = lax.dynamic_slice_in_dim(dm_all, dev * ncol, ncol, axis=2)
    out["w_ada"] = tuple(_adam_ada(jnp.transpose(c_act), dm_mine, w_ada, m_w_ada, v_w_ada, "adam_ada"))

    for names, Ps, R2s, l in deferred:
        adam_group(names, Ps, R2s, l, deps=(late_prev["tok"],))
    adam_group(late_names, *_scatter_finish(late_prev, results["w_o"][0], "rs_late0"), 0)
    for n in early_names + late_names:
        out[n] = tuple(results[n])

    grads = [out[n][0] for n in order]
    deltas = [out[n][1] for n in order]
    new_m = [out[n][2] for n in order]
    new_v = [out[n][3] for n in order]
    return (loss, grad_x, *grads, *deltas, *new_m, *new_v)
```

```python
import functools
import math

import jax
import jax.numpy as jnp
from jax import lax
from jax.experimental import pallas as pl
from jax.experimental.pallas import tpu as pltpu

F32, BF16 = jnp.float32, jnp.bfloat16
NDEV = 8
NCHIP = NDEV // 2
DEPTH = 2
EPS = 1e-6
CHUNK = 128
NG = 8
SHORT_K = 3
CFM_K = 31
HALO = 32
N_MOD = 6
LANE = 128
VMEM_LIMIT = 56 * 1024 * 1024
ADAM_LR, ADAM_B1, ADAM_B2, ADAM_EPS, ADAM_WD, ADAM_STEP = 0.001, 0.9, 0.999, 1e-08, 0.01, 10
_G0 = math.sqrt(2.0 / math.pi)
_G1 = 0.044715
MESH = pl.DeviceIdType.MESH
ANY = pl.BlockSpec(memory_space=pl.ANY)


def _pcall(body, **kw):
    return pl.pallas_call(body, **kw)


def _params(sem=None):
    return pltpu.CompilerParams(dimension_semantics=sem, vmem_limit_bytes=VMEM_LIMIT)


def _sds(shape, dtype):
    return jax.ShapeDtypeStruct(tuple(shape), dtype)


def _mm_body(dims, nk, out_f32, blocks=1):
    def body(a_ref, b_ref, o_ref, *scr):
        k = pl.program_id(2)
        if blocks == 1:
            part = lax.dot_general(a_ref[...], b_ref[...], dims, preferred_element_type=F32)
        else:
            w = a_ref.shape[1] // blocks
            part = None
            for g in range(blocks):
                t = lax.dot_general(a_ref[:, g * w:(g + 1) * w], b_ref[g], dims, preferred_element_type=F32)
                part = t if part is None else part + t
        if nk == 1:
            o_ref[...] = part.reshape(o_ref.shape).astype(o_ref.dtype)
        elif out_f32:
            @pl.when(k == 0)
            def _():
                o_ref[...] = part.reshape(o_ref.shape)

            @pl.when(k > 0)
            def _():
                o_ref[...] += part.reshape(o_ref.shape)
        else:
            acc = scr[0]

            @pl.when(k == 0)
            def _():
                acc[...] = part

            @pl.when(k > 0)
            def _():
                acc[...] += part

            @pl.when(k == nk - 1)
            def _():
                o_ref[...] = acc[...].astype(o_ref.dtype)
    return body


def _after(body, n_in, deps):
    nd = len(deps)
    if nd == 0:
        return body

    def ordered(*refs):
        return body(*refs[:n_in], *refs[n_in + nd:])
    return ordered


def _mm_call(body, grid, in_specs, out_spec, out_shape, acc_shape, name, deps=()):
    scratch = [] if acc_shape is None else [pltpu.VMEM(acc_shape, F32)]
    return _pcall(_after(body, 2, deps), grid=grid, in_specs=in_specs + [ANY] * len(deps), out_specs=out_spec,
                  out_shape=out_shape, scratch_shapes=scratch, name=name,
                  compiler_params=_params(("parallel", "parallel", "arbitrary")))


def _mm_nn(a, b3, out_dtype, tm, tn, tk, name, w_outer=False, deps=()):
    M, K = a.shape
    G, _, Nb = b3.shape
    npb, nk = Nb // tn, K // tk
    out_f32 = out_dtype == F32
    body = _mm_body((((1,), (0,)), ((), ())), nk, out_f32)
    if w_outer:
        grid = (G * npb, M // tm, nk)
        ij = lambda p, q: (q, p)
    else:
        grid = (M // tm, G * npb, nk)
        ij = lambda p, q: (p, q)

    def a_map(p, q, k):
        i, j = ij(p, q)
        return (i, k)

    def b_map(p, q, k):
        i, j = ij(p, q)
        return (j // npb, k, j % npb)

    def o_map(p, q, k):
        return ij(p, q)

    def wrapped(a_ref, b_ref, o_ref, *scr):
        body(a_ref, b_ref, o_ref, *scr)

    return _mm_call(wrapped, grid, [pl.BlockSpec((tm, tk), a_map), pl.BlockSpec((None, tk, tn), b_map)],
                    pl.BlockSpec((tm, tn), o_map), _sds((M, G * Nb), out_dtype),
                    None if (nk == 1 or out_f32) else (tm, tn), name, deps)(a, b3, *deps)


def _mm_nt(a, b3, out_dtype, tm, tn, tk, name, deps=(), blocks_per_step=1):
    M, _ = a.shape
    G, Ko, Nb = b3.shape
    kpb = Nb // tk
    nk = G * kpb // blocks_per_step
    out_f32 = out_dtype == F32
    body = _mm_body((((1,), (1,)), ((), ())), nk, out_f32, blocks_per_step)

    def wrapped(a_ref, b_ref, o_ref, *scr):
        body(a_ref, b_ref, o_ref, *scr)

    if blocks_per_step > 1:
        assert tk == Nb and G % blocks_per_step == 0
        b_spec = pl.BlockSpec((blocks_per_step, tn, tk), lambda i, j, k: (k, j, 0))
    else:
        b_spec = pl.BlockSpec((None, tn, tk), lambda i, j, k: (k // kpb, j, k % kpb))
    return _mm_call(wrapped, (M // tm, Ko // tn, nk),
                    [pl.BlockSpec((tm, tk * blocks_per_step), lambda i, j, k: (i, k)), b_spec],
                    pl.BlockSpec((tm, tn), lambda i, j, k: (i, j)), _sds((M, Ko), out_dtype),
                    None if (nk == 1 or out_f32) else (tm, tn), name, deps)(a, b3, *deps)


def _mm_wgrad(at, b, G, tm, tn, tk, name, deps=()):
    M, T = at.shape
    Nb = b.shape[1] // G
    npb, nk = Nb // tn, T // tk
    body = _mm_body((((1,), (0,)), ((), ())), nk, False)

    def wrapped(a_ref, b_ref, o_ref, *scr):
        body(a_ref, b_ref, o_ref, *scr)

    a = at
    in_specs = [pl.BlockSpec((tm, tk), lambda i, j, k: (i, k)), pl.BlockSpec((tk, tn), lambda i, j, k: (k, j))]
    out_spec = pl.BlockSpec((None, tm, tn), lambda i, j, k: (j // npb, i, j % npb))
    return _mm_call(wrapped, (M // tm, G * npb, nk), in_specs, out_spec, _sds((G, M, Nb), BF16),
                    None if nk == 1 else (tm, tn), name, deps)(a, b, *deps)


def _mm3_nt(x3, ws, tm, name, deps=()):
    nb, S, K = x3.shape
    Ko = ws[0].shape[1]

    def body(x_ref, w0, w1, w2, o_ref):
        n = pl.program_id(0)
        for k, w in enumerate((w0, w1, w2)):
            @pl.when(n == k)
            def _(w=w):
                o_ref[...] = lax.dot_general(x_ref[...], w[...], (((1,), (1,)), ((), ())),
                                             preferred_element_type=F32).astype(BF16)

    wspec = pl.BlockSpec((None, Ko, K), lambda n, i: (0, 0, 0))
    return _pcall(_after(body, 4, deps), grid=(nb, S // tm),
                  in_specs=[pl.BlockSpec((None, tm, K), lambda n, i: (n, i, 0)), wspec, wspec, wspec] + [ANY] * len(deps),
                  out_specs=pl.BlockSpec((None, tm, Ko), lambda n, i: (n, i, 0)), out_shape=_sds((nb, S, Ko), BF16),
                  name=name, compiler_params=_params(("arbitrary", "parallel")))(x3, *ws, *deps)


def _mm3_wgrad(at3, b3, tn, name):
    nb, M, T = at3.shape
    N = b3.shape[2]

    def body(a_ref, b_ref, o_ref):
        o_ref[...] = jnp.dot(a_ref[...], b_ref[...], preferred_element_type=F32).astype(BF16)

    return _pcall(body, grid=(nb, N // tn),
                  in_specs=[pl.BlockSpec((None, M, T), lambda n, j: (n, 0, 0)), pl.BlockSpec((None, T, tn), lambda n, j: (n, 0, j))],
                  out_specs=pl.BlockSpec((None, M, tn), lambda n, j: (n, 0, j)), out_shape=_sds((nb, M, N), BF16),
                  name=name, compiler_params=_params(("arbitrary", "parallel")))(at3, b3)


def _rsum(v):
    return jnp.sum(v, axis=0, keepdims=True)


def _rmean(v):
    return jnp.mean(v, axis=-1, keepdims=True)


def _gelu(x):
    t = jnp.tanh(_G0 * (x + _G1 * (x * x * x)))
    return x * (0.5 * (1.0 + t)), t


def _dgelu(x, t):
    return 0.5 * (1.0 + t) + 0.5 * x * (1.0 - t * t) * (_G0 * (1.0 + 3.0 * _G1 * (x * x)))


def _sigmoid(x):
    return 0.5 * jnp.tanh(0.5 * x) + 0.5


def _fill_shifted(ext, rot):
    v = ext[...]
    n = v.shape[0]
    for b in range(1, 8):
        rot[b - 1] = pltpu.roll(v, n - b, 0)


def _rows_at(ext, rot, s, tm, cs=slice(None)):
    a, b = divmod(s, 8)
    return ext[8 * a:8 * a + tm, cs] if b == 0 else rot[b - 1, 8 * a:8 * a + tm, cs]


def _causal_conv(w_ref, taps, bias, ext, rot, offset, tm, out):
    D = out.shape[1]
    for cb in range(D // LANE):
        cs = slice(cb * LANE, (cb + 1) * LANE)
        acc = None
        for k, o in zip(taps, offset):
            term = w_ref[k:k + 1, cs] * _rows_at(ext, rot, o, tm, cs)
            acc = term if acc is None else acc + term
        out[:, cs] = acc if bias is None else acc + bias[:, cs]


def _rows(*vs):
    a = jnp.stack([v.astype(F32) for v in vs])
    return jnp.pad(a, ((0, 8 - len(vs)), (0, 0)))


def _row_spec(tm, D):
    return pl.BlockSpec((tm, D), lambda i: (i, 0))


def _const_spec(shape):
    nd = len(shape)
    return pl.BlockSpec(shape, lambda i: (0,) * nd)


def _norm_fwd(xp, f, vec, name, deps=()):
    S, D = xp.shape
    tm = min(512, S)
    has_f = f is not None

    def body(*refs):
        if has_f:
            xp_ref, f_ref, vec_ref, xo_ref, h_ref, ht_ref = refs
            x = xp_ref[...] + vec_ref[0:1, :] * f_ref[...]
            xo_ref[...] = x
        else:
            xp_ref, vec_ref, h_ref, ht_ref = refs
            x = xp_ref[...]
        r = lax.rsqrt(_rmean(x * x) + EPS)
        h = (x * r) * vec_ref[1:2, :]
        h = h * (1.0 + vec_ref[2:3, :]) + vec_ref[3:4, :]
        h_ref[...] = h.astype(BF16)
        ht_ref[...] = h.T.astype(BF16)

    rs = _row_spec(tm, D)
    ins = [xp, f, vec] if has_f else [xp, vec]
    in_specs = ([rs, rs] if has_f else [rs]) + [_const_spec((8, D))]
    out_shape = ([_sds((S, D), F32)] if has_f else []) + [_sds((S, D), BF16), _sds((D, S), BF16)]
    out_specs = [rs] * (len(out_shape) - 1) + [pl.BlockSpec((D, tm), lambda i: (0, i))]
    outs = _pcall(_after(body, len(ins), deps), grid=(S // tm,), in_specs=in_specs + [ANY] * len(deps),
                  out_specs=out_specs, out_shape=out_shape, name=name,
                  compiler_params=_params(("parallel",)))(*ins, *deps)
    return (outs[0], outs[1], outs[2]) if has_f else (xp, outs[0], outs[1])


def _mm_resid_norm(a, w3, xprev, vec, tm, name, deps=()):
    S, K = a.shape
    D = w3.shape[2]

    def body(a_ref, w_ref, xp_ref, vec_ref, p_ref, xo_ref, h_ref, ht_ref):
        p = jnp.dot(a_ref[...], w_ref[...], preferred_element_type=F32)
        p_ref[...] = p
        x = xp_ref[...] + vec_ref[0:1, :] * p
        xo_ref[...] = x
        r = lax.rsqrt(_rmean(x * x) + EPS)
        h = (x * r) * vec_ref[1:2, :]
        h = h * (1.0 + vec_ref[2:3, :]) + vec_ref[3:4, :]
        h_ref[...] = h.astype(BF16)
        ht_ref[...] = h.T.astype(BF16)

    rs = _row_spec(tm, D)
    return _pcall(_after(body, 4, deps), grid=(S // tm,),
                  in_specs=[_row_spec(tm, K), pl.BlockSpec((None, K, D), lambda i: (0, 0, 0)), rs, _const_spec((8, D))]
                  + [ANY] * len(deps),
                  out_specs=[rs, rs, rs, pl.BlockSpec((D, tm), lambda i: (0, i))],
                  out_shape=[_sds((S, D), F32), _sds((S, D), F32), _sds((S, D), BF16), _sds((D, S), BF16)], name=name,
                  compiler_params=_params(("parallel",)))(a, w3, xprev, vec, *deps)


def _mixer_fwd(z, wsh, sgu_ln, wtril, bias_full, cw, cvec, name, deps=()):
    S = z.shape[0]
    D = wsh.shape[1]
    tm = CHUNK

    def body(z_ref, wsh_ref, sln_ref, wt_ref, bias_ref, cw_ref, cv_ref, oa_ref, ob_ref, oc_ref, t_ref,
             conv_ref, pe, ge, gr, cbuf):
        i = pl.program_id(0)

        @pl.when(i == 0)
        def _():
            pe[0:HALO, :] = jnp.zeros((HALO, D), F32)
            ge[0:HALO, :] = jnp.zeros((HALO, D), F32)

        def col(n):
            return z_ref[:, n * D:(n + 1) * D].astype(F32)

        pe[HALO:HALO + tm, :] = col(1) * col(2)
        q = wsh_ref[0:1, :] * pe[HALO - 2:HALO - 2 + tm, :]
        q = q + wsh_ref[1:2, :] * pe[HALO - 1:HALO - 1 + tm, :]
        q = q + wsh_ref[2:3, :] * pe[HALO:HALO + tm, :]
        act_a = col(0) * q
        oa_ref[...] = act_a.astype(BF16)
        t_ref[0] = act_a.T.astype(BF16)
        gu, _ = _gelu(col(3))
        gv, _ = _gelu(col(4))
        d = gv - _rmean(gv)
        nrm = d * lax.rsqrt(_rmean(d * d) + EPS)
        vnb = (nrm * sln_ref[0:1, :] + sln_ref[1:2, :]).astype(BF16)
        for g in range(NG):
            cs = slice(g * LANE, (g + 1) * LANE)
            mixed = jnp.dot(wt_ref[g], vnb[:, cs], preferred_element_type=F32) + bias_ref[:, cs]
            act_b = gu[:, cs] * mixed
            ob_ref[:, cs] = act_b.astype(BF16)
            t_ref[1, cs, :] = act_b.T.astype(BF16)
        ge[HALO:HALO + tm, :] = col(5) * _sigmoid(col(6))
        _fill_shifted(ge, gr)
        o0 = HALO - (CFM_K - 1)
        _causal_conv(cw_ref, range(CFM_K), cv_ref[0:1, :], ge, gr, range(o0, o0 + CFM_K), tm, cbuf)
        conv = cbuf[...]
        conv_ref[...] = conv.astype(BF16)
        d = conv - _rmean(conv)
        ln = (d * lax.rsqrt(_rmean(d * d) + EPS)) * cv_ref[1:2, :] + cv_ref[2:3, :]
        act_c = ln * _sigmoid(ln)
        oc_ref[...] = act_c.astype(BF16)
        t_ref[2] = act_c.T.astype(BF16)
        pe[0:HALO, :] = pe[tm:tm + HALO, :]
        ge[0:HALO, :] = ge[tm:tm + HALO, :]

    rs = _row_spec(tm, D)
    outs = _pcall(
        _after(body, 7, deps), grid=(S // tm,),
        in_specs=[pl.BlockSpec((tm, 7 * D), lambda i: (i, 0)), _const_spec((8, D)), _const_spec((8, D)),
                  _const_spec((NG, CHUNK, CHUNK)), _const_spec((CHUNK, D)), _const_spec((HALO, D)), _const_spec((8, D))]
        + [ANY] * len(deps),
        out_specs=[rs, rs, rs, pl.BlockSpec((3, D, tm), lambda i: (0, 0, i)), rs],
        out_shape=[_sds((S, D), BF16)] * 3 + [_sds((3, D, S), BF16), _sds((S, D), BF16)],
        scratch_shapes=[pltpu.VMEM((HALO + tm, D), F32), pltpu.VMEM((HALO + tm, D), F32),
                        pltpu.VMEM((7, HALO + tm, D), F32), pltpu.VMEM((tm, D), F32)],
        name=name, compiler_params=_params(("arbitrary",)))(z, wsh, sgu_ln, wtril, bias_full, cw, cvec, *deps)
    return outs[:3], outs[3], outs[4]


def _branch_out(acts, ws, z, name):
    S, D = acts[0].shape
    tm = min(512, S)

    def body(a0, a1, a2, w0, w1, w2, g0, g1, g2, m_ref, mt_ref, y_ref):
        m = None
        for n, (a, w, g) in enumerate(((a0, w0, g0), (a1, w1, g1), (a2, w2, g2))):
            y = jnp.dot(a[...], w[...], preferred_element_type=F32)
            y_ref[n] = y.astype(BF16)
            t = _sigmoid(g[...].astype(F32)) * y
            m = t if m is None else m + t
        m_ref[...] = m.astype(BF16)
        mt_ref[...] = m.T.astype(BF16)

    rs = _row_spec(tm, D)
    gate_specs = [pl.BlockSpec((tm, D), functools.partial(lambda i, n: (i, 7 + n), n=n)) for n in range(3)]
    return _pcall(body, grid=(S // tm,),
                  in_specs=[rs, rs, rs] + [_const_spec((D, D))] * 3 + gate_specs,
                  out_specs=[rs, pl.BlockSpec((D, tm), lambda i: (0, i)), pl.BlockSpec((3, tm, D), lambda i: (0, i, 0))],
                  out_shape=[_sds((S, D), BF16), _sds((D, S), BF16), _sds((3, S, D), BF16)], name=name,
                  compiler_params=_params(("parallel",)))(*acts, *ws, z, z, z)


def _ffn_in_swiglu(h2, w3, tm, tn, name):
    S, D = h2.shape
    F = w3.shape[2] // 2
    nj = F // tn

    def body(a_ref, wg_ref, wu_ref, gu_ref, act_ref, actt_ref):
        a = a_ref[...]
        g = jnp.dot(a, wg_ref[...], preferred_element_type=F32)
        u = jnp.dot(a, wu_ref[...], preferred_element_type=F32)
        gu_ref[0] = g.astype(BF16)
        gu_ref[1] = u.astype(BF16)
        act = (g * _sigmoid(g)) * u
        act_ref[...] = act.astype(BF16)
        actt_ref[...] = act.T.astype(BF16)

    return _pcall(body, grid=(S // tm, nj),
                  in_specs=[pl.BlockSpec((tm, D), lambda i, j: (i, 0)), pl.BlockSpec((None, D, tn), lambda i, j: (0, 0, j)),
                            pl.BlockSpec((None, D, tn), lambda i, j: (0, 0, j + nj))],
                  out_specs=[pl.BlockSpec((2, tm, tn), lambda i, j: (0, i, j)), pl.BlockSpec((tm, tn), lambda i, j: (i, j)),
                             pl.BlockSpec((tn, tm), lambda i, j: (j, i))],
                  out_shape=[_sds((2, S, F), BF16), _sds((S, F), BF16), _sds((F, S), BF16)], name=name,
                  compiler_params=_params(("parallel", "parallel")))(h2, w3, w3)


def _swiglu_bwd(df, w3, gu, name, deps=()):
    _, S, F = gu.shape
    F2 = 2 * F
    D = df.shape[1]
    tm = min(256, S)

    def body(df_ref, w_ref, g_ref, u_ref, o_ref):
        d = lax.dot_general(df_ref[...], w_ref[...], (((1,), (1,)), ((), ())), preferred_element_type=F32)
        g = g_ref[...].astype(F32)
        sg = _sigmoid(g)
        o_ref[:, 0:F] = (d * u_ref[...].astype(F32) * (sg * (1.0 + g * (1.0 - sg)))).astype(BF16)
        o_ref[:, F:2 * F] = (d * (g * sg)).astype(BF16)

    return _pcall(_after(body, 4, deps), grid=(S // tm,),
                  in_specs=[_row_spec(tm, D), pl.BlockSpec((None, F, D), lambda i: (0, 0, 0)),
                            pl.BlockSpec((None, tm, F), lambda i: (0, i, 0)), pl.BlockSpec((None, tm, F), lambda i: (1, i, 0))]
                  + [ANY] * len(deps),
                  out_specs=pl.BlockSpec((tm, F2), lambda i: (i, 0)), out_shape=_sds((S, F2), BF16), name=name,
                  compiler_params=_params(("parallel",)))(df, w3, gu, gu, *deps)


def _final_bwd(x1, act, w3, tgt, vec, name):
    S, D = x1.shape
    K = act.shape[1]
    tm = min(512, S)

    def body(x_ref, a_ref, w_ref, t_ref, vec_ref, dx_ref, df_ref, sums_ref, loss_ref):
        @pl.when(pl.program_id(0) == 0)
        def _():
            sums_ref[...] = jnp.zeros_like(sums_ref)
            loss_ref[...] = jnp.zeros_like(loss_ref)

        gate, fg = vec_ref[0:1, :], vec_ref[1:2, :]
        fv = jnp.dot(a_ref[...], w_ref[...], preferred_element_type=F32)
        x = x_ref[...] + gate * fv
        r = lax.rsqrt(_rmean(x * x) + EPS)
        xn = x * r
        diff = xn * fg - t_ref[...]
        per_tok = _rmean(diff * diff)
        loss_ref[...] += 0.5 * jnp.sum(per_tok, axis=0, keepdims=True)
        dy = diff * (1.0 / D)
        sums_ref[0:1, :] += _rsum(dy * xn)
        dxn = dy * fg
        dx = r * (dxn - xn * _rmean(dxn * xn))
        sums_ref[1:2, :] += _rsum(dx * fv)
        dx_ref[...] = dx
        df_ref[...] = (dx * gate).astype(BF16)

    rs = _row_spec(tm, D)
    return _pcall(body, grid=(S // tm,),
                  in_specs=[rs, _row_spec(tm, K), pl.BlockSpec((None, K, D), lambda i: (0, 0, 0)), rs, _const_spec((8, D))],
                  out_specs=[rs, rs, _const_spec((8, D)), _const_spec((8, LANE))],
                  out_shape=[_sds((S, D), F32), _sds((S, D), BF16), _sds((8, D), F32), _sds((8, LANE), F32)],
                  name=name, compiler_params=_params(("arbitrary",)))(x1, act, w3, tgt, vec)


def _norm_bwd(xin, dh, dxup, vec, fprev, name, deps=()):
    S, D = xin.shape
    has_prev = fprev is not None
    fused = isinstance(dh, tuple)
    tm = min(512, S)
    n_dh = 2 if fused else 1
    G, Nb = (dh[1].shape[0], dh[1].shape[2]) if fused else (1, 0)
    bps = 1 if G == 1 else 2
    nk = G // bps

    def body(*refs):
        x_ref, dh_refs, (up_ref, vec_ref) = refs[0], refs[1:1 + n_dh], refs[1 + n_dh:3 + n_dh]
        rest = refs[3 + n_dh:]
        if has_prev:
            fp_ref, dx_ref, dp_ref, sums_ref = rest[:4]
        else:
            dx_ref, sums_ref = rest[:2]
        k = pl.program_id(1)

        @pl.when((pl.program_id(0) == 0) & (k == 0))
        def _():
            sums_ref[...] = jnp.zeros_like(sums_ref)

        def finish(dhv):
            g, scale = vec_ref[0:1, :], vec_ref[1:2, :]
            x = x_ref[...]
            r = lax.rsqrt(_rmean(x * x) + EPS)
            xn = x * r
            sums_ref[0:1, :] += _rsum(dhv)
            sums_ref[1:2, :] += _rsum(dhv * (xn * g))
            dm = dhv * (1.0 + scale)
            sums_ref[2:3, :] += _rsum(dm * xn)
            dxn = dm * g
            dx = up_ref[...] + r * (dxn - xn * _rmean(dxn * xn))
            dx_ref[...] = dx
            if has_prev:
                sums_ref[3:4, :] += _rsum(dx * fp_ref[...])
                dp_ref[...] = (dx * vec_ref[2:3, :]).astype(BF16)

        if not fused:
            finish(dh_refs[0][...])
        elif nk == 1:
            finish(lax.dot_general(dh_refs[0][...], dh_refs[1][...], (((1,), (1,)), ((), ())), preferred_element_type=F32))
        else:
            a_ref, b_ref, acc = dh_refs[0], dh_refs[1], rest[-1]
            part = None
            for j in range(bps):
                t = lax.dot_general(a_ref[:, j * Nb:(j + 1) * Nb], b_ref[j], (((1,), (1,)), ((), ())),
                                    preferred_element_type=F32)
                part = t if part is None else part + t

            @pl.when(k == 0)
            def _():
                acc[...] = part

            @pl.when(k > 0)
            def _():
                acc[...] += part

            @pl.when(k == nk - 1)
            def _():
                finish(acc[...])

    rs = pl.BlockSpec((tm, D), lambda i, k: (i, 0))
    vs = pl.BlockSpec((8, D), lambda i, k: (0, 0))
    if not fused:
        dh_ins, dh_specs = [dh], [rs]
    elif nk == 1:
        dh_ins, dh_specs = list(dh), [pl.BlockSpec((tm, Nb), lambda i, k: (i, 0)),
                                      pl.BlockSpec((None, D, Nb), lambda i, k: (0, 0, 0), pipeline_mode=pl.Buffered(1))]
    else:
        dh_ins, dh_specs = list(dh), [pl.BlockSpec((tm, bps * Nb), lambda i, k: (i, k)),
                                      pl.BlockSpec((bps, D, Nb), lambda i, k: (k, 0, 0))]
    ins = [xin, *dh_ins, dxup, vec] + ([fprev] if has_prev else [])
    in_specs = [rs, *dh_specs, rs, vs] + ([rs] if has_prev else [])
    out_shape = [_sds((S, D), F32)] + ([_sds((S, D), BF16)] if has_prev else []) + [_sds((8, D), F32)]
    out_specs = [rs] + ([rs] if has_prev else []) + [vs]
    outs = _pcall(_after(body, len(ins), deps), grid=(S // tm, nk), in_specs=in_specs + [ANY] * len(deps),
                  out_specs=out_specs, out_shape=out_shape, name=name,
                  scratch_shapes=[pltpu.VMEM((tm, D), F32)] if nk > 1 else [],
                  compiler_params=_params(("arbitrary", "arbitrary")))(*ins, *deps)
    return (outs[0], outs[1], outs[2]) if has_prev else (outs[0], None, outs[1])


def _gate_bwd(dmerged, z, ys, name, deps=()):
    S, D = dmerged.shape
    tm = min(512, S)
    ncol = z.shape[1] // D

    def body(dm_ref, g_ref, y_ref, dy_ref, dz_ref):
        sg = _sigmoid(g_ref[...].astype(F32))
        dm = dm_ref[...].astype(F32)
        dy_ref[...] = (dm * sg).astype(BF16)
        dz_ref[...] = (dm * y_ref[...].astype(F32) * (sg * (1.0 - sg))).astype(BF16)

    branch = pl.BlockSpec((None, tm, D), lambda i, n: (n, i, 0))
    return _pcall(_after(body, 3, deps), grid=(S // tm, 3),
                  in_specs=[pl.BlockSpec((tm, D), lambda i, n: (i, 0)), pl.BlockSpec((tm, D), lambda i, n: (i, 7 + n)),
                            branch] + [ANY] * len(deps),
                  out_specs=[branch, pl.BlockSpec((tm, D), lambda i, n: (i, 7 + n))],
                  out_shape=[_sds((3, S, D), BF16), _sds((S, ncol * D), BF16)], name=name,
                  compiler_params=_params(("parallel", "arbitrary")))(dmerged, z, ys, *deps)


def _mixer_bwd(z, dacts, conv, dz, wsh, sgu_ln, wtril, wtril_t, bias_full, cw, cvec, name):
    S = z.shape[0]
    D = wsh.shape[1]
    tm = CHUNK
    nt = S // tm
    hb = tm // HALO

    def body(zc, zp, da_ref, db_ref, dc_ref, conv_ref, wsh_ref, sln_ref, wt_ref, wtt_ref, bias_ref, cw_ref, cv_ref, _dz_in,
             dz_ref, vec_ref, dcw_ref, dws_ref, dbs_ref, pe, ge, dqe, dce, gr, dcr, cbuf, dcw8):
        i = pl.program_id(0)
        rb = nt - 1 - i

        @pl.when(i == 0)
        def _():
            vec_ref[...] = jnp.zeros_like(vec_ref)
            dcw8[...] = jnp.zeros_like(dcw8)
            dws_ref[...] = jnp.zeros_like(dws_ref)
            dbs_ref[...] = jnp.zeros_like(dbs_ref)
            dqe[tm:tm + HALO, :] = jnp.zeros((HALO, D), F32)
            dce[tm:tm + HALO, :] = jnp.zeros((HALO, D), F32)

        keep = (rb > 0).astype(F32)

        def col(n):
            return zc[:, n * D:(n + 1) * D].astype(F32)

        def pcol(n):
            return zp[:, n * D:(n + 1) * D].astype(F32)

        c_a, x_a = col(1), col(2)
        pe[0:HALO, :] = keep * (pcol(1) * pcol(2))
        pe[HALO:HALO + tm, :] = c_a * x_a
        q = wsh_ref[0:1, :] * pe[HALO - 2:HALO - 2 + tm, :]
        q = q + wsh_ref[1:2, :] * pe[HALO - 1:HALO - 1 + tm, :]
        q = q + wsh_ref[2:3, :] * pe[HALO:HALO + tm, :]
        dact = da_ref[...].astype(F32)
        dz_ref[:, 0:D] = (dact * q).astype(BF16)
        dq = dact * col(0)
        dqe[0:tm, :] = dq
        dp = wsh_ref[2:3, :] * dq + wsh_ref[1:2, :] * dqe[1:1 + tm, :] + wsh_ref[0:1, :] * dqe[2:2 + tm, :]
        dz_ref[:, D:2 * D] = (dp * x_a).astype(BF16)
        dz_ref[:, 2 * D:3 * D] = (dp * c_a).astype(BF16)
        for k in range(SHORT_K):
            o = HALO - (SHORT_K - 1) + k
            vec_ref[k:k + 1, :] += _rsum(dq * pe[o:o + tm, :])
        u, v = col(3), col(4)
        gu, tu = _gelu(u)
        gv, tv = _gelu(v)
        d = gv - _rmean(gv)
        rstd = lax.rsqrt(_rmean(d * d) + EPS)
        nrm = d * rstd
        vnb = (nrm * sln_ref[0:1, :] + sln_ref[1:2, :]).astype(BF16)
        dact = db_ref[...].astype(F32)
        dvn_parts, dgu_parts = [], []
        for g in range(NG):
            cs = slice(g * LANE, (g + 1) * LANE)
            vg = vnb[:, cs]
            mixed = jnp.dot(wt_ref[g], vg, preferred_element_type=F32) + bias_ref[:, cs]
            dgu_parts.append(dact[:, cs] * mixed)
            dmixed = dact[:, cs] * gu[:, cs]
            dmb = dmixed.astype(BF16)
            dws_ref[g] += lax.dot_general(dmb, vg, (((1,), (1,)), ((), ())), preferred_element_type=F32)
            dbs_ref[g] += jnp.broadcast_to(jnp.sum(dmixed, axis=1, keepdims=True), (CHUNK, LANE))
            dvn_parts.append(jnp.dot(wtt_ref[g], dmb, preferred_element_type=F32))
        dgu = jnp.concatenate(dgu_parts, axis=1)
        dvn = jnp.concatenate(dvn_parts, axis=1)
        dz_ref[:, 3 * D:4 * D] = (dgu * _dgelu(u, tu)).astype(BF16)
        vec_ref[3:4, :] += _rsum(dvn * nrm)
        vec_ref[4:5, :] += _rsum(dvn)
        dn = dvn * sln_ref[0:1, :]
        dgv = rstd * (dn - _rmean(dn) - nrm * _rmean(dn * nrm))
        dz_ref[:, 4 * D:5 * D] = (dgv * _dgelu(v, tv)).astype(BF16)
        a_c = col(5)
        sg = _sigmoid(col(6))
        ge[0:HALO, :] = keep * (pcol(5) * _sigmoid(pcol(6)))
        ge[HALO:HALO + tm, :] = a_c * sg
        _fill_shifted(ge, gr)
        o0 = HALO - (CFM_K - 1)
        conv = conv_ref[...].astype(F32)
        d = conv - _rmean(conv)
        rstd = lax.rsqrt(_rmean(d * d) + EPS)
        nrm = d * rstd
        ln = nrm * cv_ref[1:2, :] + cv_ref[2:3, :]
        sl = _sigmoid(ln)
        dln = dc_ref[...].astype(F32) * (sl * (1.0 + ln * (1.0 - sl)))
        vec_ref[6:7, :] += _rsum(dln * nrm)
        vec_ref[7:8, :] += _rsum(dln)
        dn = dln * cv_ref[1:2, :]
        dconv = rstd * (dn - _rmean(dn) - nrm * _rmean(dn * nrm))
        vec_ref[5:6, :] += _rsum(dconv)
        dce[0:tm, :] = dconv
        _fill_shifted(dce, dcr)
        _causal_conv(cw_ref, range(CFM_K), None, dce, dcr, [CFM_K - 1 - k for k in range(CFM_K)], tm, cbuf)
        dglu = cbuf[...]
        for cb in range(D // LANE):
            cs = slice(cb * LANE, (cb + 1) * LANE)
            dcv = dce[0:tm, cs]
            for k in range(CFM_K):
                prod = dcv * _rows_at(ge, gr, o0 + k, tm, cs)
                dcw8[k, :, cs] += jnp.sum(prod.reshape(tm // 8, 8, LANE), axis=0)

        @pl.when(i == nt - 1)
        def _():
            dcw_ref[...] = jnp.sum(dcw8[...], axis=1)
        dz_ref[:, 5 * D:6 * D] = (dglu * sg).astype(BF16)
        dz_ref[:, 6 * D:7 * D] = (dglu * a_c * (sg * (1.0 - sg))).astype(BF16)
        dqe[tm:tm + HALO, :] = dqe[0:HALO, :]
        dce[tm:tm + HALO, :] = dce[0:HALO, :]

    rev = lambda i: (nt - 1 - i, 0)
    rs = pl.BlockSpec((tm, D), rev)
    cur = pl.BlockSpec((tm, 7 * D), rev)
    prev = pl.BlockSpec((HALO, 7 * D), lambda i: (jnp.maximum((nt - 1 - i) * hb - 1, 0), 0))
    ext = pltpu.VMEM((HALO + tm, D), F32)
    outs = _pcall(
        body, grid=(nt,),
        in_specs=[cur, prev] + [pl.BlockSpec((None, tm, D), functools.partial(lambda i, n: (n, nt - 1 - i, 0), n=n))
                                for n in range(3)]
        + [rs, _const_spec((8, D)), _const_spec((8, D)), _const_spec((NG, CHUNK, CHUNK)),
                  _const_spec((NG, CHUNK, CHUNK)), _const_spec((CHUNK, D)), _const_spec((HALO, D)), _const_spec((8, D)),
                  ANY],
        out_specs=[cur, _const_spec((8, D)), _const_spec((HALO, D)), _const_spec((NG, CHUNK, CHUNK)),
                   _const_spec((NG, CHUNK, LANE))],
        out_shape=[_sds(dz.shape, BF16), _sds((8, D), F32), _sds((HALO, D), F32), _sds((NG, CHUNK, CHUNK), F32),
                   _sds((NG, CHUNK, LANE), F32)],
        scratch_shapes=[ext, ext, ext, ext, pltpu.VMEM((7, HALO + tm, D), F32), pltpu.VMEM((7, HALO + tm, D), F32),
                        pltpu.VMEM((tm, D), F32), pltpu.VMEM((HALO, 8, D), F32)],
        input_output_aliases={13: 0}, name=name,
        compiler_params=_params(("arbitrary",)))(z, z, dacts, dacts, dacts, conv, wsh, sgu_ln, wtril, wtril_t, bias_full, cw,
                                                 cvec, dz)
    return outs


def _ada_fwd(c_all, w_ada_loc, name):
    nb, D = c_all.shape
    L, _, nc = w_ada_loc.shape

    def body(c_ref, w_ref, o_ref, ca_ref):
        cv = c_ref[...]
        ca = cv * _sigmoid(cv)
        ca_ref[...] = ca
        o_ref[...] = jnp.dot(ca.astype(BF16), w_ref[...].astype(BF16), preferred_element_type=F32)

    return _pcall(body, grid=(L,),
                  in_specs=[_const_spec((nb, D)), pl.BlockSpec((None, D, nc), lambda l: (l, 0, 0))],
                  out_specs=[pl.BlockSpec((None, nb, nc), lambda l: (l, 0, 0)), _const_spec((nb, D))],
                  out_shape=[_sds((L, nb, nc), F32), _sds((nb, D), F32)], name=name,
                  compiler_params=_params(("arbitrary",)))(c_all, w_ada_loc)


def _adamw(w, g, m, v):
    m = ADAM_B1 * m + (1.0 - ADAM_B1) * g
    v = ADAM_B2 * v + (1.0 - ADAM_B2) * (g * g)
    m_hat = m / (1.0 - ADAM_B1 ** ADAM_STEP)
    v_hat = v / (1.0 - ADAM_B2 ** ADAM_STEP)
    delta = -ADAM_LR * (m_hat / (jnp.sqrt(v_hat) + ADAM_EPS) + ADAM_WD * w)
    return delta, m, v


def _tile_rows(R, C, align=8):
    cap = max(align, (1536 * 1024) // (4 * C))
    best = None
    for t in range(align, R + 1, align):
        if R % t == 0 and t <= cap:
            best = t
    return R if best is None else best


def _adam_ada(ct, dm, w, m, v, name):
    L, D, nc = w.shape
    nb = ct.shape[1]
    tr = _tile_rows(D, nc)

    def body(ct_ref, dm_ref, w_ref, m_ref, v_ref, g_ref, d_ref, mo_ref, vo_ref):
        g = ct_ref[:, 0:1] * dm_ref[0:1, :]
        for b in range(1, nb):
            g = g + ct_ref[:, b:b + 1] * dm_ref[b:b + 1, :]
        g_ref[...] = g
        d_ref[...], mo_ref[...], vo_ref[...] = _adamw(w_ref[...], g, m_ref[...], v_ref[...])

    ws = pl.BlockSpec((None, tr, nc), lambda l, r: (l, r, 0))
    return _pcall(body, grid=(L, D // tr),
                  in_specs=[pl.BlockSpec((tr, nb), lambda l, r: (r, 0)), pl.BlockSpec((None, nb, nc), lambda l, r: (l, 0, 0)),
                            ws, ws, ws],
                  out_specs=[ws] * 4, out_shape=[_sds(w.shape, F32)] * 4, name=name,
                  compiler_params=_params(("parallel", "parallel")))(ct, dm, w, m, v)


def _adam_small(parts, w, m, v, name, deps=(), single_rows=()):
    n, R, C = parts.shape
    tr = _tile_rows(R, C * n // 2)
    nl = len(single_rows[0]) if single_rows else 0

    def body(p_ref, w_ref, m_ref, v_ref, g_ref, d_ref, mo_ref, vo_ref, *single):
        g = p_ref[0]
        for j in range(1, n):
            g = g + p_ref[j]
        d, mo, vo = _adamw(w_ref[...], g, m_ref[...], v_ref[...])
        g_ref[...], d_ref[...], mo_ref[...], vo_ref[...] = g, d, mo, vo
        step = pl.program_id(0)
        for pi, rows in enumerate(single_rows):
            for l, row in enumerate(rows):
                @pl.when(step == row // tr)
                def _(pi=pi, l=l, off=row % tr):
                    for k, val in enumerate((g, d, mo, vo)):
                        single[4 * pi + k][l:l + 1, :] = val[off:off + 1, :]

    ws = pl.BlockSpec((tr, C), lambda r: (r, 0))
    one = pl.BlockSpec((nl, C), lambda r: (0, 0))
    outs = _pcall(_after(body, 4, deps), grid=(R // tr,),
                  in_specs=[pl.BlockSpec((n, tr, C), lambda r: (0, r, 0)), ws, ws, ws] + [ANY] * len(deps),
                  out_specs=[ws] * 4 + [one] * (4 * len(single_rows)),
                  out_shape=[_sds((R, C), F32)] * 4 + [_sds((nl, C), F32)] * (4 * len(single_rows)), name=name,
                  compiler_params=_params(("arbitrary",)))(parts, w, m, v, *deps)
    return outs[:4], outs[4:]


def _adam_plain(g, w, m, v, name):
    R, C = w.shape

    def body(g_ref, w_ref, m_ref, v_ref, d_ref, mo_ref, vo_ref):
        d_ref[...], mo_ref[...], vo_ref[...] = _adamw(w_ref[...], g_ref[...], m_ref[...], v_ref[...])

    ws = _const_spec((R, C))
    return _pcall(body, grid=(1,), in_specs=[ws] * 4, out_specs=[ws] * 3, out_shape=[_sds((R, C), F32)] * 3, name=name,
                  compiler_params=_params(("arbitrary",)))(g, w, m, v)


def _pair_sum(G, R1, my_c, name):
    n, R, C = G.shape
    half = n // 2
    tr = _tile_rows(R, C, align=16)

    def body(c_ref, g_ref, r_ref, o_ref):
        o_ref[...] = (g_ref[...].astype(F32) + r_ref[...].astype(F32)).astype(o_ref.dtype)

    blk = (None, tr, C)
    gs = pltpu.PrefetchScalarGridSpec(
        num_scalar_prefetch=1, grid=(half, R // tr),
        in_specs=[pl.BlockSpec(blk, lambda p, r, c: (2 * p + c[0], r, 0)), pl.BlockSpec(blk, lambda p, r, c: (p, r, 0))],
        out_specs=pl.BlockSpec(blk, lambda p, r, c: (p, r, 0)))
    return _pcall(body, grid_spec=gs, out_shape=_sds((half, R, C), G.dtype), name=name,
                  compiler_params=_params(("parallel", "parallel")))(my_c, G, R1)


def _adam_big(P, R2, my_chip, w, m, v, layer, prev, name, deps=()):
    _, R, C = P.shape
    nrecv = R2.shape[0]
    tr = _tile_rows(R, C, align=16)

    def body(p_sm, p_ref, r_ref, w_ref, m_ref, v_ref, *rest):
        g_ref, d_ref, mo_ref, vo_ref = rest[-4:]
        g = p_ref[...].astype(F32)
        for k in range(nrecv):
            g = g + r_ref[k].astype(F32)
        g_ref[...] = g
        d_ref[...], mo_ref[...], vo_ref[...] = _adamw(w_ref[...], g, m_ref[...], v_ref[...])

    ws = pl.BlockSpec((None, tr, C), lambda r, p: (layer, r, 0))
    held = [] if prev is None else list(prev)
    gs = pltpu.PrefetchScalarGridSpec(
        num_scalar_prefetch=1, grid=(R // tr,),
        in_specs=[pl.BlockSpec((None, tr, C), lambda r, p: (p[0], r, 0)),
                  pl.BlockSpec((nrecv, tr, C), lambda r, p: (0, r, 0)), ws, ws, ws] + [ANY] * (len(held) + len(deps)),
        out_specs=[ws] * 4)
    alias = {6 + i: i for i in range(len(held))}
    return _pcall(body, grid_spec=gs, out_shape=[_sds(w.shape, F32)] * 4, name=name, input_output_aliases=alias,
                  compiler_params=_params(("parallel",)))(my_chip, P, R2, w, m, v, *held, *deps)


def _place():
    return lax.axis_index("x"), lax.axis_index("y"), lax.axis_index("c")


def _all_gather(shards, name, deps=(), into=None):
    n = len(shards)
    bufs = [] if into is None else [b for b, _ in into]
    nb = len(bufs)

    def body(*refs):
        ins, outs = refs[:n], refs[n + nb:2 * n + nb]
        send_sems, recv_sems, local_sems = refs[2 * n + nb:]
        x, y, c = _place()
        me, sibling = (x, y, c), (x, y, 1 - c)
        chips = [(1 - x, y), (x, 1 - y), (1 - x, 1 - y)]

        def slot(a, px, py, pc):
            block = outs[a].at[4 * px + 2 * py + pc]
            return block if into is None else block.at[pl.ds(into[a][1], ins[a].shape[0])]

        def copy(a, k, block, to, src=None):
            return pltpu.make_async_remote_copy(
                src_ref=slot(a, *block) if src is None else src, dst_ref=slot(a, *block),
                send_sem=send_sems.at[7 * a + k], recv_sem=recv_sems.at[7 * a + k], device_id=to, device_id_type=MESH)

        mine = [pltpu.make_async_copy(ins[a], slot(a, *me), local_sems.at[a]) for a in range(n)]
        for cp in mine:
            cp.start()
        first = []
        for a in range(n):
            first.append(copy(a, 0, me, sibling, src=ins[a]))
            first += [copy(a, 1 + j, me, (*chip, c), src=ins[a]) for j, chip in enumerate(chips)]
        for cp in first:
            cp.start()
        passed = []
        for j, chip in enumerate(chips):
            for a in range(n):
                copy(a, 1 + j, (*chip, c), me).wait_recv()
                fwd = copy(a, 4 + j, (*chip, c), sibling)
                fwd.start()
                passed.append(fwd)
        for a in range(n):
            copy(a, 0, sibling, me).wait_recv()
        for j, chip in enumerate(chips):
            for a in range(n):
                copy(a, 4 + j, (*chip, 1 - c), me).wait_recv()
        for cp in first + passed:
            cp.wait_send()
        for cp in mine:
            cp.wait()

    out_shape = [_sds((NDEV,) + s.shape, s.dtype) for s in shards] if into is None else [_sds(b.shape, b.dtype) for b in bufs]
    outs = _pcall(_after(body, n + nb, deps), in_specs=[ANY] * (n + nb + len(deps)), out_specs=[ANY] * n,
                  out_shape=out_shape, input_output_aliases={n + a: a for a in range(nb)},
                  scratch_shapes=[pltpu.SemaphoreType.DMA((7 * n,)), pltpu.SemaphoreType.DMA((7 * n,)),
                                  pltpu.SemaphoreType.DMA((n,))], name=name)(*shards, *bufs, *deps)
    return list(outs)


HBM = pl.BlockSpec(memory_space=pltpu.HBM)
SEM = pl.BlockSpec(memory_space=pltpu.SEMAPHORE)


def _copies(plan, refs, send_sems, recv_sems):
    return [pltpu.make_async_remote_copy(src_ref=s, dst_ref=d, send_sem=send_sems.at[k], recv_sem=recv_sems.at[k],
                                         device_id=dev, device_id_type=MESH)
            for k, (s, d, dev) in enumerate(plan(refs, *_place()))]


def _xfer_start(bufs, ncopies, plan, name, deps=()):
    n = len(bufs)

    def body(*refs):
        for cp in _copies(plan, refs[:n], refs[n], refs[n + 1]):
            cp.start()
        token = refs[2 * n + 2]
        token[...] = jnp.zeros_like(token)

    outs = _pcall(
        _after(body, n, deps), name=name,
        out_shape=(pltpu.SemaphoreType.DMA((ncopies,)), pltpu.SemaphoreType.DMA((ncopies,)),
                   *[pltpu.HBM(b.shape, b.dtype) for b in bufs], _sds((8, LANE), F32)),
        in_specs=[HBM] * n + [ANY] * len(deps),
        out_specs=(SEM, SEM, *[HBM] * n, pl.BlockSpec(memory_space=pltpu.VMEM)),
        input_output_aliases={i: 2 + i for i in range(n)},
        compiler_params=pltpu.CompilerParams(has_side_effects=pltpu.SideEffectType.DATAFLOW_SIDE_EFFECTING),
    )(*[pltpu.with_memory_space_constraint(b, pltpu.HBM) for b in bufs], *deps)
    return (outs[0], outs[1]), list(outs[2:2 + n]), outs[2 + n]


def _xfer_wait(sems, bufs, plan, after, name):
    n = len(bufs)
    after = list(after) if isinstance(after, (list, tuple)) else [after]

    def body(*refs):
        for cp in _copies(plan, refs[:n], refs[n], refs[n + 1]):
            cp.wait_send()
            cp.wait_recv()

    outs = _pcall(
        body, name=name, out_shape=tuple(pltpu.HBM(b.shape, b.dtype) for b in bufs),
        in_specs=[HBM] * n + [SEM, SEM] + [ANY] * len(after), out_specs=tuple([HBM] * n),
        input_output_aliases={i: i for i in range(n)},
        compiler_params=pltpu.CompilerParams(has_side_effects=pltpu.SideEffectType.DATAFLOW_SIDE_EFFECTING),
    )(*bufs, *sems, *after)
    return list(outs)


def _chips_of(x, y):
    return [(1 - x, y), (x, 1 - y), (1 - x, 1 - y)]


def _landing(ref, dev_index, rows):
    block = ref.at[dev_index]
    return block if rows is None else block.at[pl.ds(rows[0], rows[1])]


def _gather_plan1(n, rows=None):
    def plan(refs, x, y, c):
        out = []
        for a in range(n):
            blk = _landing(refs[a], 4 * x + 2 * y + c, rows)
            out.append((blk, blk, (x, y, 1 - c)))
            out += [(blk, blk, (px, py, c)) for px, py in _chips_of(x, y)]
        return out
    return plan


def _gather_plan2(n, rows=None):
    def plan(refs, x, y, c):
        out = []
        for a in range(n):
            for px, py in _chips_of(x, y):
                blk = _landing(refs[a], 4 * px + 2 * py + c, rows)
                out.append((blk, blk, (x, y, 1 - c)))
        return out
    return plan


def _gather_start(shards, dev, name, deps=(), within=None):
    rows = None if within is None else (within[0], shards[0].shape[0])
    lands = []
    for s in shards:
        shape = (NDEV,) + s.shape if within is None else (NDEV, within[1]) + s.shape[1:]
        start = (dev,) + (0,) * s.ndim if within is None else (dev, within[0]) + (0,) * (s.ndim - 1)
        lands.append(lax.dynamic_update_slice(lax.empty(shape, s.dtype), s[None], start))
    n = len(shards)
    sems, lands, tok = _xfer_start(lands, 4 * n, _gather_plan1(n, rows), name + "_p1_start", deps)
    return dict(sems=sems, lands=lands, tok=tok, n=n, rows=rows)


def _gather_mid(st, after, name):
    n, rows = st["n"], st["rows"]
    lands = _xfer_wait(st["sems"], st["lands"], _gather_plan1(n, rows), after, name + "_p1_wait")
    sems, lands, tok = _xfer_start(lands, 3 * n, _gather_plan2(n, rows), name + "_p2_start")
    return dict(sems=sems, lands=lands, tok=tok, n=n, rows=rows)


def _gather_finish(st, after, name):
    return _xfer_wait(st["sems"], st["lands"], _gather_plan2(st["n"], st["rows"]), after, name + "_p2_wait")


def _scatter_plan1(n):
    def plan(refs, x, y, c):
        return [(refs[a].at[2 * p + 1 - c], refs[n + a].at[p], (x, y, 1 - c)) for a in range(n) for p in range(NCHIP)]
    return plan


def _scatter_plan2(n):
    def plan(refs, x, y, c):
        return [(refs[a].at[2 * px + py], refs[n + a].at[j], (px, py, c))
                for a in range(n) for j, (px, py) in enumerate(_chips_of(x, y))]
    return plan


def _scatter_start(Gs, name):
    n = len(Gs)
    R1s = [lax.empty((NCHIP,) + g.shape[1:], g.dtype) for g in Gs]
    sems, bufs, tok = _xfer_start(list(Gs) + R1s, NCHIP * n, _scatter_plan1(n), name + "_s1_start")
    return dict(sems=sems, bufs=bufs, tok=tok, n=n)


def _scatter_mid(st, after, my_c, name):
    n = st["n"]
    bufs = _xfer_wait(st["sems"], st["bufs"], _scatter_plan1(n), after, name + "_s1_wait")
    Ps = [_pair_sum(bufs[a], bufs[n + a], my_c, f"{name}_pair_sum{a}") for a in range(n)]
    R2s = [lax.empty((3,) + p.shape[1:], p.dtype) for p in Ps]
    sems, bufs, tok = _xfer_start(Ps + R2s, 3 * n, _scatter_plan2(n), name + "_s2_start")
    return dict(sems=sems, bufs=bufs, tok=tok, n=n)


def _scatter_finish(st, after, name):
    n = st["n"]
    bufs = _xfer_wait(st["sems"], st["bufs"], _scatter_plan2(n), after, name + "_s2_wait")
    return bufs[:n], bufs[n:]


SMALL_ROWS = {"norm1_g": (0, 1), "norm2_g": (1, 1), "sgu_ln_g": (2, 1), "sgu_ln_b": (3, 1), "cfm_conv_b": (4, 1),
              "cfm_ln_g": (5, 1), "cfm_ln_b": (6, 1), "b_sgu": (7, 1), "w_sgu": (8, 128), "b_ada": (136, N_MOD),
              "w_short": (142, SHORT_K), "cfm_conv_w": (145, CFM_K)}
ROWS_PER_LAYER = 176
FINAL_ROW = DEPTH * ROWS_PER_LAYER
PACK_ROWS = 360


def _pack(get, D, layers=tuple(range(DEPTH)), tail=True):
    parts = []
    for l in layers:
        for name, (_, nrows) in SMALL_ROWS.items():
            a = get(name, l)
            parts.append(jnp.zeros((nrows * D,), F32) if a is None else a.astype(F32).reshape(nrows * D))
    if tail:
        for name in ("final_g", "loss"):
            a = get(name, None)
            parts.append(jnp.zeros((D,), F32) if a is None else a.astype(F32).reshape(D))
        parts.append(jnp.zeros(((PACK_ROWS - FINAL_ROW - 2) * D,), F32))
    return jnp.concatenate(parts).reshape(-1, D)


def _unpack(pack, name, shape):
    D = pack.shape[1]
    r0, nrows = SMALL_ROWS[name]
    return jnp.stack([pack[l * ROWS_PER_LAYER + r0:l * ROWS_PER_LAYER + r0 + nrows] for l in range(DEPTH)]).reshape(shape)


def _mm_tiles(S):
    return min(512, S), min(1024, S), min(2048, S)


def kernel(x, c, w_ada, b_ada, norm1_g, w_in, w_short, w_a_out, sgu_ln_g, sgu_ln_b, w_sgu, b_sgu, w_b_out, cfm_conv_w, cfm_conv_b, cfm_ln_g, cfm_ln_b, w_c_out, w_o, norm2_g, w_ffn_in, w_ffn_out, final_g, loss_target, m_w_ada, m_b_ada, m_norm1_g, m_w_in, m_w_short, m_w_a_out, m_sgu_ln_g, m_sgu_ln_b, m_w_sgu, m_b_sgu, m_w_b_out, m_cfm_conv_w, m_cfm_conv_b, m_cfm_ln_g, m_cfm_ln_b, m_w_c_out, m_w_o, m_norm2_g, m_w_ffn_in, m_w_ffn_out, m_final_g, v_w_ada, v_b_ada, v_norm1_g, v_w_in, v_w_short, v_w_a_out, v_sgu_ln_g, v_sgu_ln_b, v_w_sgu, v_b_sgu, v_w_b_out, v_cfm_conv_w, v_cfm_conv_b, v_cfm_ln_g, v_cfm_ln_b, v_w_c_out, v_w_o, v_norm2_g, v_w_ffn_in, v_w_ffn_out, v_final_g):
    W = dict(w_ada=w_ada, b_ada=b_ada, norm1_g=norm1_g, w_in=w_in, w_short=w_short, w_a_out=w_a_out, sgu_ln_g=sgu_ln_g,
             sgu_ln_b=sgu_ln_b, w_sgu=w_sgu, b_sgu=b_sgu, w_b_out=w_b_out, cfm_conv_w=cfm_conv_w, cfm_conv_b=cfm_conv_b,
             cfm_ln_g=cfm_ln_g, cfm_ln_b=cfm_ln_b, w_c_out=w_c_out, w_o=w_o, norm2_g=norm2_g, w_ffn_in=w_ffn_in,
             w_ffn_out=w_ffn_out, final_g=final_g)
    Mo = dict(w_ada=m_w_ada, b_ada=m_b_ada, norm1_g=m_norm1_g, w_in=m_w_in, w_short=m_w_short, w_a_out=m_w_a_out,
              sgu_ln_g=m_sgu_ln_g, sgu_ln_b=m_sgu_ln_b, w_sgu=m_w_sgu, b_sgu=m_b_sgu, w_b_out=m_w_b_out,
              cfm_conv_w=m_cfm_conv_w, cfm_conv_b=m_cfm_conv_b, cfm_ln_g=m_cfm_ln_g, cfm_ln_b=m_cfm_ln_b,
              w_c_out=m_w_c_out, w_o=m_w_o, norm2_g=m_norm2_g, w_ffn_in=m_w_ffn_in, w_ffn_out=m_w_ffn_out,
              final_g=m_final_g)
    Vo = dict(w_ada=v_w_ada, b_ada=v_b_ada, norm1_g=v_norm1_g, w_in=v_w_in, w_short=v_w_short, w_a_out=v_w_a_out,
              sgu_ln_g=v_sgu_ln_g, sgu_ln_b=v_sgu_ln_b, w_sgu=v_w_sgu, b_sgu=v_b_sgu, w_b_out=v_w_b_out,
              cfm_conv_w=v_cfm_conv_w, cfm_conv_b=v_cfm_conv_b, cfm_ln_g=v_cfm_ln_g, cfm_ln_b=v_cfm_ln_b,
              w_c_out=v_w_c_out, w_o=v_w_o, norm2_g=v_norm2_g, w_ffn_in=v_w_ffn_in, w_ffn_out=v_w_ffn_out,
              final_g=v_final_g)
    order = ["w_ada", "b_ada", "norm1_g", "w_in", "w_short", "w_a_out", "sgu_ln_g", "sgu_ln_b", "w_sgu", "b_sgu",
             "w_b_out", "cfm_conv_w", "cfm_conv_b", "cfm_ln_g", "cfm_ln_b", "w_c_out", "w_o", "norm2_g", "w_ffn_in",
             "w_ffn_out", "final_g"]

    assert DEPTH == 2, "the weight-gather schedule below is written for two layers"
    S, D = x.shape[1], x.shape[2]
    F2 = w_ffn_in.shape[2] * NDEV
    FF = F2 // 2
    xi, yi, ci = _place()
    dev = 4 * xi + 2 * yi + ci
    my_c = jnp.reshape(ci, (1,)).astype(jnp.int32)
    my_chip = jnp.reshape(2 * xi + yi, (1,)).astype(jnp.int32)
    tm, tm_big, tm_huge = _mm_tiles(S)
    x0 = x.reshape(S, D)
    tgt = loss_target.reshape(S, D)

    def shards_of(l):
        return [w_in[l].astype(BF16), w_a_out[l].astype(BF16), w_b_out[l].astype(BF16), w_c_out[l].astype(BF16),
                w_o[l].astype(BF16), w_ffn_in[l].astype(BF16), w_ffn_out[l].astype(BF16)]

    c_all = _all_gather([jnp.pad(c, ((0, 7), (0, 0)))], "ag_c")[0][:, 0, :]
    modpart, c_act = _ada_fwd(c_all, w_ada, "ada_fwd")
    ncol = modpart.shape[2]
    mg = _all_gather([modpart.reshape(DEPTH * NDEV, ncol)], "ag_mod")[0].reshape(NDEV, DEPTH, NDEV, ncol)
    mine = lax.dynamic_index_in_dim(mg, dev, axis=2, keepdims=False)
    mod = (jnp.transpose(mine, (1, 0, 2)).reshape(DEPTH, N_MOD * D) + b_ada).reshape(DEPTH, N_MOD, D)

    ncs = w_short.shape[2]
    ag_in0 = _gather_start([w_in[0].astype(BF16), w_short.reshape(DEPTH * SHORT_K, ncs),
                            cfm_conv_w.reshape(DEPTH * CFM_K, ncs)], dev, "ag_w_in0", deps=(mod,))
    W, Mo, Vo = lax.optimization_barrier((ag_in0["tok"], (W, Mo, Vo)))[1]
    (norm1_g, norm2_g, w_in, w_a_out, w_b_out, w_c_out, w_o, w_ffn_in, w_ffn_out, sgu_ln_g, sgu_ln_b, w_sgu, b_sgu,
     cfm_conv_b, cfm_ln_g, cfm_ln_b, final_g) = [W[k] for k in (
         "norm1_g", "norm2_g", "w_in", "w_a_out", "w_b_out", "w_c_out", "w_o", "w_ffn_in", "w_ffn_out", "sgu_ln_g",
         "sgu_ln_b", "w_sgu", "b_sgu", "cfm_conv_b", "cfm_ln_g", "cfm_ln_b", "final_g")]
    m_w_ada, v_w_ada = Mo["w_ada"], Vo["w_ada"]
    xl0, h0, ht0 = _norm_fwd(x0, None, _rows(jnp.zeros((D,), F32), norm1_g[0], mod[0, 1], mod[0, 0]), "norm1_fwd0",
                             deps=(ag_in0["tok"],))
    ag_rest0 = _gather_start(shards_of(0)[1:5], dev, "ag_rest0", deps=(h0,))
    ag_ffn0 = _gather_start(shards_of(0)[5:], dev, "ag_ffn0", deps=(ag_rest0["tok"],))

    tril = jnp.tril(jnp.ones((CHUNK, CHUNK), dtype=bool))

    def layer_consts(l):
        wt = jnp.where(tril[None], w_sgu[l], 0.0).astype(BF16)
        return dict(sgu_ln=_rows(sgu_ln_g[l], sgu_ln_b[l]), wtril=wt, wtril_t=jnp.swapaxes(wt, 1, 2),
                    bias_full=jnp.repeat(b_sgu[l].T, LANE, axis=1), cvec=_rows(cfm_conv_b[l], cfm_ln_g[l], cfm_ln_b[l]))

    def rest_of(g):
        return dict(w_a=g[0].reshape(1, D, D), w_b=g[1].reshape(1, D, D), w_c=g[2].reshape(1, D, D),
                    w_o=g[3].reshape(1, D, D), w_fi=jnp.transpose(g[4], (1, 0, 2)).reshape(1, D, F2),
                    w_fo=g[5].reshape(1, FF, D))

    sharded_small = ("w_short", "cfm_conv_w")

    def param_get(T):
        def get(name, l):
            if name == "final_g":
                return T[name]
            return None if name in sharded_small or name == "loss" else T[name][l]
        return get

    packs = [_pack(param_get(T), D) for T in (W, Mo, Vo)]
    ag_in0 = _gather_mid(ag_in0, [ag_ffn0["tok"], *packs], "ag_w_in0")
    (w_sgu, b_sgu, sgu_ln_g, sgu_ln_b, cfm_conv_b, cfm_ln_g, cfm_ln_b), conv_wmv_in = lax.optimization_barrier(
        (ag_in0["tok"], ((w_sgu, b_sgu, sgu_ln_g, sgu_ln_b, cfm_conv_b, cfm_ln_g, cfm_ln_b),
                         [(T["w_short"], T["cfm_conv_w"]) for T in (W, Mo, Vo)])))[1]
    consts = [layer_consts(l) for l in range(DEPTH)]
    ncr = DEPTH * (SHORT_K + CFM_K)
    padr = (-ncr) % 8
    convw_wmv = [jnp.pad(jnp.concatenate([a.reshape(-1, ncs), b.reshape(-1, ncs)]), ((0, padr), (0, 0)))
                 for a, b in conv_wmv_in]
    g_in0 = _gather_finish(ag_in0, [*convw_wmv] + [a for cl in consts for a in cl.values()], "ag_w_in0")
    w_short_full = jnp.transpose(g_in0[1], (1, 0, 2)).reshape(DEPTH, SHORT_K, D)
    cfm_w_full = jnp.transpose(g_in0[2], (1, 0, 2)).reshape(DEPTH, CFM_K, D)
    for l in range(DEPTH):
        consts[l]["wsh"] = jnp.pad(w_short_full[l], ((0, 8 - SHORT_K), (0, 0)))
        consts[l]["cw"] = jnp.pad(cfm_w_full[l], ((0, HALO - CFM_K), (0, 0)))
    Wg = [dict(w_in=g_in0[0]), None]
    ag_l1 = None
    nin = w_in.shape[2]
    tn_in = nin if nin % 256 == 0 and nin <= 1280 else 256
    tn_fi = 512 if F2 % 512 == 0 else 256
    tn_dw = min(256, D)

    saved = []
    xcur, gprev, ffn_tail = x0, None, None
    for l in range(DEPTH):
        sh1, sc1, g1, sh2, sc2, g2 = [mod[l, k] for k in range(N_MOD)]
        cl = consts[l]
        if l == 0:
            xl, h, ht = xl0, h0, ht0
        else:
            vec1 = _rows(gprev, norm1_g[l], sc1, sh1)
            act_prev, w_fo_prev = ffn_tail
            ag_l1 = _gather_mid(ag_l1, act_prev, f"ag_w{l}")
            f_prev, xl, h, ht = _mm_resid_norm(act_prev, w_fo_prev, xcur, vec1, tm, f"mm_ffn_out_norm1_{l}",
                                               deps=(ag_l1["tok"],))
            saved[l - 1]["f"] = f_prev
            g = _gather_finish(ag_l1, h, f"ag_w{l}")
            Wg[l] = dict(w_in=g[0], **rest_of(g[1:]))
        wl = Wg[l]
        z = _mm_nn(h, wl["w_in"], BF16, tm_huge, tn_in, D, f"mm_in{l}", w_outer=True)
        mix_deps, o_deps = (), ()
        if l == 0:
            ag_rest0 = _gather_mid(ag_rest0, z, "ag_rest0")
            mix_deps = (ag_rest0["tok"],)
            if DEPTH > 1:
                ag_l1 = _gather_start(shards_of(1), dev, "ag_w1")
                mix_deps += (ag_l1["tok"],)
        acts, acts_t, conv = _mixer_fwd(z, cl["wsh"], cl["sgu_ln"], cl["wtril"], cl["bias_full"], cl["cw"], cl["cvec"],
                                        f"mixer_fwd{l}", deps=mix_deps)
        if l == 0:
            g = _gather_finish(ag_rest0, acts[0], "ag_rest0")
            wl.update(w_a=g[0].reshape(1, D, D), w_b=g[1].reshape(1, D, D), w_c=g[2].reshape(1, D, D),
                      w_o=g[3].reshape(1, D, D))
            ag_ffn0 = _gather_mid(ag_ffn0, acts[0], "ag_ffn0")
            o_deps = (ag_ffn0["tok"],)
        merged, merged_t, ys = _branch_out(acts, [wl["w_a"][0], wl["w_b"][0], wl["w_c"][0]], z, f"branch_out{l}")
        o, x1, h2, h2t = _mm_resid_norm(merged, wl["w_o"], xl, _rows(g1, norm2_g[l], sc2, sh2), tm, f"mm_o_norm2_{l}",
                                        deps=o_deps)
        if l == 0:
            g = _gather_finish(ag_ffn0, h2, "ag_ffn0")
            wl.update(w_fi=jnp.transpose(g[0], (1, 0, 2)).reshape(1, D, F2), w_fo=g[1].reshape(1, FF, D))
        gu, act, act_t = _ffn_in_swiglu(h2, wl["w_fi"], tm_huge, 256, f"mm_ffn_in{l}")
        saved.append(dict(xl=xl, ht=ht, z=z, acts_t=acts_t, conv=conv, ys=ys, merged_t=merged_t, o=o, x1=x1, h2t=h2t, gu=gu,
                          act_t=act_t, f=None, consts=cl, mod=(sh1, sc1, g1, sh2, sc2, g2)))
        xcur, gprev, ffn_tail = x1, g2, (act, wl["w_fo"])

    last = saved[-1]
    dxup, dfb, fsums, loss_blk = _final_bwd(last["x1"], *ffn_tail, tgt, _rows(last["mod"][5], final_g), "final_bwd")
    loss_row = jnp.pad(loss_blk[0, 0:1], (0, D - 1))
    dgate2_next = fsums[1]
    small = [dict() for _ in range(DEPTH)]
    dmods = [None] * DEPTH
    nfi = w_ffn_in.shape[2]
    early_names, late_names = ["w_ffn_out", "w_ffn_in", "w_o"], ["w_a_out", "w_b_out", "w_c_out", "w_in"]
    results = {n: None for n in early_names + late_names}

    def adam_group(names, Ps, R2s, l, deps=()):
        for n, p, r2 in zip(names, Ps, R2s):
            results[n] = _adam_big(p, r2, my_chip, W[n], Mo[n], Vo[n], l, results[n], f"adam_{n}{l}", deps)

    deferred = []
    late_prev = None
    ag_s1, gathered1 = None, None
    tk_w = min(2048, S)
    tn_dw_in = tn_in // 2 if tn_in == 1280 else tn_in
    for l in reversed(range(DEPTH)):
        sv, wl, cl = saved[l], Wg[l], saved[l]["consts"]
        sh1, sc1, g1, sh2, sc2, g2 = sv["mod"]
        dgu = _swiglu_bwd(dfb, wl["w_fo"], sv["gu"], f"mm_dact_swiglu_bwd{l}",
                          deps=() if late_prev is None else (late_prev["tok"], ag_s1["tok"]))
        g_fo = _mm_wgrad(sv["act_t"], dfb, 1, FF // 2, D, tk_w, f"mm_dw_ffn_out{l}")
        g_fi = _mm_wgrad(sv["h2t"], dgu, 1, D, tn_fi, S, f"mm_dw_ffn_in{l}")
        if late_prev is not None:
            deferred.append((late_names, *_scatter_finish(late_prev, g_fi, f"rs_late{l + 1}"), l + 1))
            late_prev = None
        dx1, dob, s2 = _norm_bwd(sv["x1"], (dgu, wl["w_fi"]), dxup, _rows(norm2_g[l], sc2, g1), sv["o"],
                                 f"mm_dh2_norm2_bwd{l}")
        dmerged = _mm_nt(dob, wl["w_o"], BF16, tm_big, D, D, f"mm_dmerged{l}")
        g_o = _mm_wgrad(sv["merged_t"], dob, 1, D, tn_dw, S, f"mm_dw_o{l}")
        early = _scatter_start([g_fo.reshape(NDEV, FF // NDEV, D),
                                jnp.transpose(g_fi.reshape(D, NDEV, nfi), (1, 0, 2)),
                                g_o.reshape(NDEV, D // NDEV, D)], f"rs_early{l}")
        dys, dz = _gate_bwd(dmerged, sv["z"], sv["ys"], f"gate_bwd{l}", deps=(early["tok"],))
        early = _scatter_mid(early, dys, my_c, f"rs_early{l}")
        abc_deps = (early["tok"],)
        if ag_s1 is not None:
            ag_s1 = _gather_mid(ag_s1, dys, "ag_small1")
            abc_deps += (ag_s1["tok"],)
        dacts = _mm3_nt(dys, [wl["w_a"], wl["w_b"], wl["w_c"]], tm_big, f"mm_dact_abc{l}", deps=abc_deps)
        g3 = _mm3_wgrad(sv["acts_t"], dys, tn_dw, f"mm_dw_abc{l}")
        g_abc = [g3[n] for n in range(3)]
        dz, mvec, dcw, dws, dbs = _mixer_bwd(sv["z"], dacts, sv["conv"], dz, cl["wsh"], cl["sgu_ln"], cl["wtril"],
                                             cl["wtril_t"], cl["bias_full"], cl["cw"], cl["cvec"], f"mixer_bwd{l}")
        if ag_s1 is not None:
            gathered1 = _gather_finish(ag_s1, dz, "ag_small1")[0]
            ag_s1 = None
        g_in = _mm_wgrad(sv["ht"], dz, NDEV, D, tn_dw_in, S, f"mm_dw_in{l}")
        late = _scatter_start([g.reshape(NDEV, D // NDEV, D) for g in g_abc] + [g_in], f"rs_late{l}")
        if l > 0:
            pv = saved[l - 1]
            dxup, dfb, s1 = _norm_bwd(sv["xl"], (dz, wl["w_in"]), dx1, _rows(norm1_g[l], sc1, pv["mod"][5]), pv["f"],
                                      f"mm_dh_norm1_bwd{l}", deps=(late["tok"],))
        else:
            dxup, dfb, s1 = _norm_bwd(sv["xl"], (dz, wl["w_in"]), dx1, _rows(norm1_g[l], sc1), None,
                                      f"mm_dh_norm1_bwd{l}", deps=(late["tok"],))
        deferred.append((early_names, *_scatter_finish(early, dxup, f"rs_early{l}"), l))
        dmods[l] = jnp.stack([s1[0], s1[1], s2[3], s2[0], s2[1], dgate2_next])
        dgate2_next = s1[3]
        small[l] = dict(norm1_g=s1[2], norm2_g=s2[2], sgu_ln_g=mvec[3], sgu_ln_b=mvec[4], cfm_conv_b=mvec[5],
                        cfm_ln_g=mvec[6], cfm_ln_b=mvec[7], b_sgu=dbs[:, :, 0],
                        w_sgu=jnp.where(tril[None], dws, 0.0), b_ada=dmods[l], w_short=mvec[0:SHORT_K],
                        cfm_conv_w=dcw[0:CFM_K])
        small_get = lambda name, k: {"final_g": fsums[0], "loss": loss_row}.get(name) if k is None else small[k][name]
        if l > 0:
            late_prev = _scatter_mid(late, dxup, my_c, f"rs_late{l}")
            ag_s1 = _gather_start([_pack(small_get, D, layers=(l,), tail=True)], dev, "ag_small1", deps=(late_prev["tok"],),
                                  within=(l * ROWS_PER_LAYER, PACK_ROWS))
    grad_x = dxup.reshape(x.shape)

    gathered = _all_gather([_pack(small_get, D, layers=(0,), tail=False)], "ag_small0", deps=(dxup,),
                           into=[(gathered1, 0)])[0]
    late_prev = _scatter_mid(late, gathered, my_c, "rs_late0")
    one_row = [n for n in order if n in SMALL_ROWS and SMALL_ROWS[n][1] == 1 and W[n].ndim == 2]
    (sg, sd, sm, sv_), singles = _adam_small(
        gathered, *packs, name="adam_small", deps=(late_prev["tok"],),
        single_rows=[tuple(l * ROWS_PER_LAYER + SMALL_ROWS[n][0] for l in range(DEPTH)) for n in one_row])
    loss = sg[FINAL_ROW + 1, 0]
    out = {n: tuple(singles[4 * i:4 * i + 4]) for i, n in enumerate(one_row)}
    for name in order:
        if name in SMALL_ROWS and name not in sharded_small and name not in out:
            out[name] = tuple(_unpack(p, name, W[name].shape) for p in (sg, sd, sm, sv_))
    out["final_g"] = tuple(p[FINAL_ROW] for p in (sg, sd, sm, sv_))

    def my_cols(name):
        full = _unpack(sg, name, (DEPTH, SMALL_ROWS[name][1], D))
        return lax.dynamic_slice_in_dim(full, dev * ncs, ncs, axis=2)

    gcs = jnp.concatenate([my_cols("w_short").reshape(-1, ncs), my_cols("cfm_conv_w").reshape(-1, ncs)])
    cd, cm, cv = _adam_plain(jnp.pad(gcs, ((0, padr), (0, 0))), *convw_wmv, "adam_convw")
    nsh = DEPTH * SHORT_K
    out["w_short"] = tuple(a[0:nsh].reshape(w_short.shape) for a in (gcs, cd, cm, cv))
    out["cfm_conv_w"] = tuple(a[nsh:ncr].reshape(cfm_conv_w.shape) for a in (gcs, cd, cm, cv))

    dm_all = jnp.stack([gathered[:, l * ROWS_PER_LAYER + 136:l * ROWS_PER_LAYER + 136 + N_MOD, :].reshape(NDEV, N_MOD * D)
                        for l in range(DEPTH)])
    dm_mine = lax.dynamic_slice_in_dim(dm_all, dev * ncol, ncol, axis=2)
    out["w_ada"] = tuple(_adam_ada(jnp.transpose(c_act), dm_mine, w_ada, m_w_ada, v_w_ada, "adam_ada"))

    for names, Ps, R2s, l in deferred:
        adam_group(names, Ps, R2s, l, deps=(late_prev["tok"],))
    adam_group(late_names, *_scatter_finish(late_prev, results["w_o"][0], "rs_late0"), 0)
    for n in early_names + late_names:
        out[n] = tuple(results[n])

    grads = [out[n][0] for n in order]
    deltas = [out[n][1] for n in order]
    new_m = [out[n][2] for n in order]
    new_v = [out[n][3] for n in order]
    return (loss, grad_x, *grads, *deltas, *new_m, *new_v)
```

```python
import functools
import math

import jax
import jax.numpy as jnp
from jax import lax
from jax.experimental import pallas as pl
from jax.experimental.pallas import tpu as pltpu

F32, BF16 = jnp.float32, jnp.bfloat16
NDEV = 8
NCHIP = NDEV // 2
DEPTH = 2
EPS = 1e-6
CHUNK = 128
NG = 8
SHORT_K = 3
CFM_K = 31
HALO = 32
N_MOD = 6
LANE = 128
VMEM_LIMIT = 56 * 1024 * 1024
ADAM_LR, ADAM_B1, ADAM_B2, ADAM_EPS, ADAM_WD, ADAM_STEP = 0.001, 0.9, 0.999, 1e-08, 0.01, 10
_G0 = math.sqrt(2.0 / math.pi)
_G1 = 0.044715
MESH = pl.DeviceIdType.MESH
ANY = pl.BlockSpec(memory_space=pl.ANY)


def _pcall(body, **kw):
    return pl.pallas_call(body, **kw)


def _params(sem=None):
    return pltpu.CompilerParams(dimension_semantics=sem, vmem_limit_bytes=VMEM_LIMIT)


def _sds(shape, dtype):
    return jax.ShapeDtypeStruct(tuple(shape), dtype)


def _mm_body(dims, nk, out_f32, blocks=1):
    def body(a_ref, b_ref, o_ref, *scr):
        k = pl.program_id(2)
        if blocks == 1:
            part = lax.dot_general(a_ref[...], b_ref[...], dims, preferred_element_type=F32)
        else:
            w = a_ref.shape[1] // blocks
            part = None
            for g in range(blocks):
                t = lax.dot_general(a_ref[:, g * w:(g + 1) * w], b_ref[g], dims, preferred_element_type=F32)
                part = t if part is None else part + t
        if nk == 1:
            o_ref[...] = part.reshape(o_ref.shape).astype(o_ref.dtype)
        elif out_f32:
            @pl.when(k == 0)
            def _():
                o_ref[...] = part.reshape(o_ref.shape)

            @pl.when(k > 0)
            def _():
                o_ref[...] += part.reshape(o_ref.shape)
        else:
            acc = scr[0]

            @pl.when(k == 0)
            def _():
                acc[...] = part

            @pl.when(k > 0)
            def _():
                acc[...] += part

            @pl.when(k == nk - 1)
            def _():
                o_ref[...] = acc[...].astype(o_ref.dtype)
    return body


def _after(body, n_in, deps):
    nd = len(deps)
    if nd == 0:
        return body

    def ordered(*refs):
        return body(*refs[:n_in], *refs[n_in + nd:])
    return ordered


def _mm_call(body, grid, in_specs, out_spec, out_shape, acc_shape, name, deps=()):
    scratch = [] if acc_shape is None else [pltpu.VMEM(acc_shape, F32)]
    return _pcall(_after(body, 2, deps), grid=grid, in_specs=in_specs + [ANY] * len(deps), out_specs=out_spec,
                  out_shape=out_shape, scratch_shapes=scratch, name=name,
                  compiler_params=_params(("parallel", "parallel", "arbitrary")))


def _mm_nn(a, b3, out_dtype, tm, tn, tk, name, w_outer=False, deps=()):
    M, K = a.shape
    G, _, Nb = b3.shape
    npb, nk = Nb // tn, K // tk
    out_f32 = out_dtype == F32
    body = _mm_body((((1,), (0,)), ((), ())), nk, out_f32)
    if w_outer:
        grid = (G * npb, M // tm, nk)
        ij = lambda p, q: (q, p)
    else:
        grid = (M // tm, G * npb, nk)
        ij = lambda p, q: (p, q)

    def a_map(p, q, k):
        i, j = ij(p, q)
        return (i, k)

    def b_map(p, q, k):
        i, j = ij(p, q)
        return (j // npb, k, j % npb)

    def o_map(p, q, k):
        return ij(p, q)

    def wrapped(a_ref, b_ref, o_ref, *scr):
        body(a_ref, b_ref, o_ref, *scr)

    return _mm_call(wrapped, grid, [pl.BlockSpec((tm, tk), a_map), pl.BlockSpec((None, tk, tn), b_map)],
                    pl.BlockSpec((tm, tn), o_map), _sds((M, G * Nb), out_dtype),
                    None if (nk == 1 or out_f32) else (tm, tn), name, deps)(a, b3, *deps)


def _mm_nt(a, b3, out_dtype, tm, tn, tk, name, deps=(), blocks_per_step=1):
    M, _ = a.shape
    G, Ko, Nb = b3.shape
    kpb = Nb // tk
    nk = G * kpb // blocks_per_step
    out_f32 = out_dtype == F32
    body = _mm_body((((1,), (1,)), ((), ())), nk, out_f32, blocks_per_step)

    def wrapped(a_ref, b_ref, o_ref, *scr):
        body(a_ref, b_ref, o_ref, *scr)

    if blocks_per_step > 1:
        assert tk == Nb and G % blocks_per_step == 0
        b_spec = pl.BlockSpec((blocks_per_step, tn, tk), lambda i, j, k: (k, j, 0))
    else:
        b_spec = pl.BlockSpec((None, tn, tk), lambda i, j, k: (k // kpb, j, k % kpb))
    return _mm_call(wrapped, (M // tm, Ko // tn, nk),
                    [pl.BlockSpec((tm, tk * blocks_per_step), lambda i, j, k: (i, k)), b_spec],
                    pl.BlockSpec((tm, tn), lambda i, j, k: (i, j)), _sds((M, Ko), out_dtype),
                    None if (nk == 1 or out_f32) else (tm, tn), name, deps)(a, b3, *deps)


def _mm_wgrad(at, b, G, tm, tn, tk, name, deps=()):
    M, T = at.shape
    Nb = b.shape[1] // G
    npb, nk = Nb // tn, T // tk
    body = _mm_body((((1,), (0,)), ((), ())), nk, False)

    def wrapped(a_ref, b_ref, o_ref, *scr):
        body(a_ref, b_ref, o_ref, *scr)

    a = at
    in_specs = [pl.BlockSpec((tm, tk), lambda i, j, k: (i, k)), pl.BlockSpec((tk, tn), lambda i, j, k: (k, j))]
    out_spec = pl.BlockSpec((None, tm, tn), lambda i, j, k: (j // npb, i, j % npb))
    return _mm_call(wrapped, (M // tm, G * npb, nk), in_specs, out_spec, _sds((G, M, Nb), BF16),
                    None if nk == 1 else (tm, tn), name, deps)(a, b, *deps)


def _mm3_nt(x3, ws, tm, name, deps=()):
    nb, S, K = x3.shape
    Ko = ws[0].shape[1]

    def body(x_ref, w0, w1, w2, o_ref):
        n = pl.program_id(0)
        for k, w in enumerate((w0, w1, w2)):
            @pl.when(n == k)
            def _(w=w):
                o_ref[...] = lax.dot_general(x_ref[...], w[...], (((1,), (1,)), ((), ())),
                                             preferred_element_type=F32).astype(BF16)

    wspec = pl.BlockSpec((None, Ko, K), lambda n, i: (0, 0, 0))
    return _pcall(_after(body, 4, deps), grid=(nb, S // tm),
                  in_specs=[pl.BlockSpec((None, tm, K), lambda n, i: (n, i, 0)), wspec, wspec, wspec] + [ANY] * len(deps),
                  out_specs=pl.BlockSpec((None, tm, Ko), lambda n, i: (n, i, 0)), out_shape=_sds((nb, S, Ko), BF16),
                  name=name, compiler_params=_params(("arbitrary", "parallel")))(x3, *ws, *deps)


def _mm3_wgrad(at3, b3, tn, name):
    nb, M, T = at3.shape
    N = b3.shape[2]

    def body(a_ref, b_ref, o_ref):
        o_ref[...] = jnp.dot(a_ref[...], b_ref[...], preferred_element_type=F32).astype(BF16)

    return _pcall(body, grid=(nb, N // tn),
                  in_specs=[pl.BlockSpec((None, M, T), lambda n, j: (n, 0, 0)), pl.BlockSpec((None, T, tn), lambda n, j: (n, 0, j))],
                  out_specs=pl.BlockSpec((None, M, tn), lambda n, j: (n, 0, j)), out_shape=_sds((nb, M, N), BF16),
                  name=name, compiler_params=_params(("arbitrary", "parallel")))(at3, b3)


def _rsum(v):
    return jnp.sum(v, axis=0, keepdims=True)


def _rmean(v):
    return jnp.mean(v, axis=-1, keepdims=True)


def _gelu(x):
    t = jnp.tanh(_G0 * (x + _G1 * (x * x * x)))
    return x * (0.5 * (1.0 + t)), t


def _dgelu(x, t):
    return 0.5 * (1.0 + t) + 0.5 * x * (1.0 - t * t) * (_G0 * (1.0 + 3.0 * _G1 * (x * x)))


def _sigmoid(x):
    return 0.5 * jnp.tanh(0.5 * x) + 0.5


def _fill_shifted(ext, rot):
    v = ext[...]
    n = v.shape[0]
    for b in range(1, 8):
        rot[b - 1] = pltpu.roll(v, n - b, 0)


def _rows_at(ext, rot, s, tm, cs=slice(None)):
    a, b = divmod(s, 8)
    return ext[8 * a:8 * a + tm, cs] if b == 0 else rot[b - 1, 8 * a:8 * a + tm, cs]


def _causal_conv(w_ref, taps, bias, ext, rot, offset, tm, out):
    D = out.shape[1]
    for cb in range(D // LANE):
        cs = slice(cb * LANE, (cb + 1) * LANE)
        acc = None
        for k, o in zip(taps, offset):
            term = w_ref[k:k + 1, cs] * _rows_at(ext, rot, o, tm, cs)
            acc = term if acc is None else acc + term
        out[:, cs] = acc if bias is None else acc + bias[:, cs]


def _rows(*vs):
    a = jnp.stack([v.astype(F32) for v in vs])
    return jnp.pad(a, ((0, 8 - len(vs)), (0, 0)))


def _row_spec(tm, D):
    return pl.BlockSpec((tm, D), lambda i: (i, 0))


def _const_spec(shape):
    nd = len(shape)
    return pl.BlockSpec(shape, lambda i: (0,) * nd)


def _norm_fwd(xp, f, vec, name, deps=()):
    S, D = xp.shape
    tm = min(512, S)
    has_f = f is not None

    def body(*refs):
        if has_f:
            xp_ref, f_ref, vec_ref, xo_ref, h_ref, ht_ref = refs
            x = xp_ref[...] + vec_ref[0:1, :] * f_ref[...]
            xo_ref[...] = x
        else:
            xp_ref, vec_ref, h_ref, ht_ref = refs
            x = xp_ref[...]
        r = lax.rsqrt(_rmean(x * x) + EPS)
        h = (x * r) * vec_ref[1:2, :]
        h = h * (1.0 + vec_ref[2:3, :]) + vec_ref[3:4, :]
        h_ref[...] = h.astype(BF16)
        ht_ref[...] = h.T.astype(BF16)

    rs = _row_spec(tm, D)
    ins = [xp, f, vec] if has_f else [xp, vec]
    in_specs = ([rs, rs] if has_f else [rs]) + [_const_spec((8, D))]
    out_shape = ([_sds((S, D), F32)] if has_f else []) + [_sds((S, D), BF16), _sds((D, S), BF16)]
    out_specs = [rs] * (len(out_shape) - 1) + [pl.BlockSpec((D, tm), lambda i: (0, i))]
    outs = _pcall(_after(body, len(ins), deps), grid=(S // tm,), in_specs=in_specs + [ANY] * len(deps),
                  out_specs=out_specs, out_shape=out_shape, name=name,
                  compiler_params=_params(("parallel",)))(*ins, *deps)
    return (outs[0], outs[1], outs[2]) if has_f else (xp, outs[0], outs[1])


def _mm_resid_norm(a, w3, xprev, vec, tm, name, deps=()):
    S, K = a.shape
    D = w3.shape[2]

    def body(a_ref, w_ref, xp_ref, vec_ref, p_ref, xo_ref, h_ref, ht_ref):
        p = jnp.dot(a_ref[...], w_ref[...], preferred_element_type=F32)
        p_ref[...] = p
        x = xp_ref[...] + vec_ref[0:1, :] * p
        xo_ref[...] = x
        r = lax.rsqrt(_rmean(x * x) + EPS)
        h = (x * r) * vec_ref[1:2, :]
        h = h * (1.0 + vec_ref[2:3, :]) + vec_ref[3:4, :]
        h_ref[...] = h.astype(BF16)
        ht_ref[...] = h.T.astype(BF16)

    rs = _row_spec(tm, D)
    return _pcall(_after(body, 4, deps), grid=(S // tm,),
                  in_specs=[_row_spec(tm, K), pl.BlockSpec((None, K, D), lambda i: (0, 0, 0)), rs, _const_spec((8, D))]
                  + [ANY] * len(deps),
                  out_specs=[rs, rs, rs, pl.BlockSpec((D, tm), lambda i: (0, i))],
                  out_shape=[_sds((S, D), F32), _sds((S, D), F32), _sds((S, D), BF16), _sds((D, S), BF16)], name=name,
                  compiler_params=_params(("parallel",)))(a, w3, xprev, vec, *deps)


def _mixer_fwd(z, wsh, sgu_ln, wtril, bias_full, cw, cvec, name, deps=()):
    S = z.shape[0]
    D = wsh.shape[1]
    tm = CHUNK

    def body(z_ref, wsh_ref, sln_ref, wt_ref, bias_ref, cw_ref, cv_ref, oa_ref, ob_ref, oc_ref, t_ref,
             conv_ref, pe, ge, gr, cbuf):
        i = pl.program_id(0)

        @pl.when(i == 0)
        def _():
            pe[0:HALO, :] = jnp.zeros((HALO, D), F32)
            ge[0:HALO, :] = jnp.zeros((HALO, D), F32)

        def col(n):
            return z_ref[:, n * D:(n + 1) * D].astype(F32)

        pe[HALO:HALO + tm, :] = col(1) * col(2)
        q = wsh_ref[0:1, :] * pe[HALO - 2:HALO - 2 + tm, :]
        q = q + wsh_ref[1:2, :] * pe[HALO - 1:HALO - 1 + tm, :]
        q = q + wsh_ref[2:3, :] * pe[HALO:HALO + tm, :]
        act_a = col(0) * q
        oa_ref[...] = act_a.astype(BF16)
        t_ref[0] = act_a.T.astype(BF16)
        gu, _ = _gelu(col(3))
        gv, _ = _gelu(col(4))
        d = gv - _rmean(gv)
        nrm = d * lax.rsqrt(_rmean(d * d) + EPS)
        vnb = (nrm * sln_ref[0:1, :] + sln_ref[1:2, :]).astype(BF16)
        for g in range(NG):
            cs = slice(g * LANE, (g + 1) * LANE)
            mixed = jnp.dot(wt_ref[g], vnb[:, cs], preferred_element_type=F32) + bias_ref[:, cs]
            act_b = gu[:, cs] * mixed
            ob_ref[:, cs] = act_b.astype(BF16)
            t_ref[1, cs, :] = act_b.T.astype(BF16)
        ge[HALO:HALO + tm, :] = col(5) * _sigmoid(col(6))
        _fill_shifted(ge, gr)
        o0 = HALO - (CFM_K - 1)
        _causal_conv(cw_ref, range(CFM_K), cv_ref[0:1, :], ge, gr, range(o0, o0 + CFM_K), tm, cbuf)
        conv = cbuf[...]
        conv_ref[...] = conv.astype(BF16)
        d = conv - _rmean(conv)
        ln = (d * lax.rsqrt(_rmean(d * d) + EPS)) * cv_ref[1:2, :] + cv_ref[2:3, :]
        act_c = ln * _sigmoid(ln)
        oc_ref[...] = act_c.astype(BF16)
        t_ref[2] = act_c.T.astype(BF16)
        pe[0:HALO, :] = pe[tm:tm + HALO, :]
        ge[0:HALO, :] = ge[tm:tm + HALO, :]

    rs = _row_spec(tm, D)
    outs = _pcall(
        _after(body, 7, deps), grid=(S // tm,),
        in_specs=[pl.BlockSpec((tm, 7 * D), lambda i: (i, 0)), _const_spec((8, D)), _const_spec((8, D)),
                  _const_spec((NG, CHUNK, CHUNK)), _const_spec((CHUNK, D)), _const_spec((HALO, D)), _const_spec((8, D))]
        + [ANY] * len(deps),
        out_specs=[rs, rs, rs, pl.BlockSpec((3, D, tm), lambda i: (0, 0, i)), rs],
        out_shape=[_sds((S, D), BF16)] * 3 + [_sds((3, D, S), BF16), _sds((S, D), BF16)],
        scratch_shapes=[pltpu.VMEM((HALO + tm, D), F32), pltpu.VMEM((HALO + tm, D), F32),
                        pltpu.VMEM((7, HALO + tm, D), F32), pltpu.VMEM((tm, D), F32)],
        name=name, compiler_params=_params(("arbitrary",)))(z, wsh, sgu_ln, wtril, bias_full, cw, cvec, *deps)
    return outs[:3], outs[3], outs[4]


def _branch_out(acts, ws, z, name):
    S, D = acts[0].shape
    tm = min(512, S)

    def body(a0, a1, a2, w0, w1, w2, g0, g1, g2, m_ref, mt_ref, y_ref):
        m = None
        for n, (a, w, g) in enumerate(((a0, w0, g0), (a1, w1, g1), (a2, w2, g2))):
            y = jnp.dot(a[...], w[...], preferred_element_type=F32)
            y_ref[n] = y.astype(BF16)
            t = _sigmoid(g[...].astype(F32)) * y
            m = t if m is None else m + t
        m_ref[...] = m.astype(BF16)
        mt_ref[...] = m.T.astype(BF16)

    rs = _row_spec(tm, D)
    gate_specs = [pl.BlockSpec((tm, D), functools.partial(lambda i, n: (i, 7 + n), n=n)) for n in range(3)]
    return _pcall(body, grid=(S // tm,),
                  in_specs=[rs, rs, rs] + [_const_spec((D, D))] * 3 + gate_specs,
                  out_specs=[rs, pl.BlockSpec((D, tm), lambda i: (0, i)), pl.BlockSpec((3, tm, D), lambda i: (0, i, 0))],
                  out_shape=[_sds((S, D), BF16), _sds((D, S), BF16), _sds((3, S, D), BF16)], name=name,
                  compiler_params=_params(("parallel",)))(*acts, *ws, z, z, z)


def _ffn_in_swiglu(h2, w3, tm, tn, name):
    S, D = h2.shape
    F = w3.shape[2] // 2
    nj = F // tn

    def body(a_ref, wg_ref, wu_ref, gu_ref, act_ref, actt_ref):
        a = a_ref[...]
        g = jnp.dot(a, wg_ref[...], preferred_element_type=F32)
        u = jnp.dot(a, wu_ref[...], preferred_element_type=F32)
        gu_ref[0] = g.astype(BF16)
        gu_ref[1] = u.astype(BF16)
        act = (g * _sigmoid(g)) * u
        act_ref[...] = act.astype(BF16)
        actt_ref[...] = act.T.astype(BF16)

    return _pcall(body, grid=(S // tm, nj),
                  in_specs=[pl.BlockSpec((tm, D), lambda i, j: (i, 0)), pl.BlockSpec((None, D, tn), lambda i, j: (0, 0, j)),
                            pl.BlockSpec((None, D, tn), lambda i, j: (0, 0, j + nj))],
                  out_specs=[pl.BlockSpec((2, tm, tn), lambda i, j: (0, i, j)), pl.BlockSpec((tm, tn), lambda i, j: (i, j)),
                             pl.BlockSpec((tn, tm), lambda i, j: (j, i))],
                  out_shape=[_sds((2, S, F), BF16), _sds((S, F), BF16), _sds((F, S), BF16)], name=name,
                  compiler_params=_params(("parallel", "parallel")))(h2, w3, w3)


def _swiglu_bwd(df, w3, gu, name, deps=()):
    _, S, F = gu.shape
    F2 = 2 * F
    D = df.shape[1]
    tm = min(256, S)

    def body(df_ref, w_ref, g_ref, u_ref, o_ref):
        d = lax.dot_general(df_ref[...], w_ref[...], (((1,), (1,)), ((), ())), preferred_element_type=F32)
        g = g_ref[...].astype(F32)
        sg = _sigmoid(g)
        o_ref[:, 0:F] = (d * u_ref[...].astype(F32) * (sg * (1.0 + g * (1.0 - sg)))).astype(BF16)
        o_ref[:, F:2 * F] = (d * (g * sg)).astype(BF16)

    return _pcall(_after(body, 4, deps), grid=(S // tm,),
                  in_specs=[_row_spec(tm, D), pl.BlockSpec((None, F, D), lambda i: (0, 0, 0)),
                            pl.BlockSpec((None, tm, F), lambda i: (0, i, 0)), pl.BlockSpec((None, tm, F), lambda i: (1, i, 0))]
                  + [ANY] * len(deps),
                  out_specs=pl.BlockSpec((tm, F2), lambda i: (i, 0)), out_shape=_sds((S, F2), BF16), name=name,
                  compiler_params=_params(("parallel",)))(df, w3, gu, gu, *deps)


def _final_bwd(x1, act, w3, tgt, vec, name):
    S, D = x1.shape
    K = act.shape[1]
    tm = min(512, S)

    def body(x_ref, a_ref, w_ref, t_ref, vec_ref, dx_ref, df_ref, sums_ref, loss_ref):
        @pl.when(pl.program_id(0) == 0)
        def _():
            sums_ref[...] = jnp.zeros_like(sums_ref)
            loss_ref[...] = jnp.zeros_like(loss_ref)

        gate, fg = vec_ref[0:1, :], vec_ref[1:2, :]
        fv = jnp.dot(a_ref[...], w_ref[...], preferred_element_type=F32)
        x = x_ref[...] + gate * fv
        r = lax.rsqrt(_rmean(x * x) + EPS)
        xn = x * r
        diff = xn * fg - t_ref[...]
        per_tok = _rmean(diff * diff)
        loss_ref[...] += 0.5 * jnp.sum(per_tok, axis=0, keepdims=True)
        dy = diff * (1.0 / D)
        sums_ref[0:1, :] += _rsum(dy * xn)
        dxn = dy * fg
        dx = r * (dxn - xn * _rmean(dxn * xn))
        sums_ref[1:2, :] += _rsum(dx * fv)
        dx_ref[...] = dx
        df_ref[...] = (dx * gate).astype(BF16)

    rs = _row_spec(tm, D)
    return _pcall(body, grid=(S // tm,),
                  in_specs=[rs, _row_spec(tm, K), pl.BlockSpec((None, K, D), lambda i: (0, 0, 0)), rs, _const_spec((8, D))],
                  out_specs=[rs, rs, _const_spec((8, D)), _const_spec((8, LANE))],
                  out_shape=[_sds((S, D), F32), _sds((S, D), BF16), _sds((8, D), F32), _sds((8, LANE), F32)],
                  name=name, compiler_params=_params(("arbitrary",)))(x1, act, w3, tgt, vec)


def _norm_bwd(xin, dh, dxup, vec, fprev, name, deps=()):
    S, D = xin.shape
    has_prev = fprev is not None
    fused = isinstance(dh, tuple)
    tm = min(512, S)
    n_dh = 2 if fused else 1
    G, Nb = (dh[1].shape[0], dh[1].shape[2]) if fused else (1, 0)
    bps = 1 if G == 1 else 2
    nk = G // bps

    def body(*refs):
        x_ref, dh_refs, (up_ref, vec_ref) = refs[0], refs[1:1 + n_dh], refs[1 + n_dh:3 + n_dh]
        rest = refs[3 + n_dh:]
        if has_prev:
            fp_ref, dx_ref, dp_ref, sums_ref = rest[:4]
        else:
            dx_ref, sums_ref = rest[:2]
        k = pl.program_id(1)

        @pl.when((pl.program_id(0) == 0) & (k == 0))
        def _():
            sums_ref[...] = jnp.zeros_like(sums_ref)

        def finish(dhv):
            g, scale = vec_ref[0:1, :], vec_ref[1:2, :]
            x = x_ref[...]
            r = lax.rsqrt(_rmean(x * x) + EPS)
            xn = x * r
            sums_ref[0:1, :] += _rsum(dhv)
            sums_ref[1:2, :] += _rsum(dhv * (xn * g))
            dm = dhv * (1.0 + scale)
            sums_ref[2:3, :] += _rsum(dm * xn)
            dxn = dm * g
            dx = up_ref[...] + r * (dxn - xn * _rmean(dxn * xn))
            dx_ref[...] = dx
            if has_prev:
                sums_ref[3:4, :] += _rsum(dx * fp_ref[...])
                dp_ref[...] = (dx * vec_ref[2:3, :]).astype(BF16)

        if not fused:
            finish(dh_refs[0][...])
        elif nk == 1:
            finish(lax.dot_general(dh_refs[0][...], dh_refs[1][...], (((1,), (1,)), ((), ())), preferred_element_type=F32))
        else:
            a_ref, b_ref, acc = dh_refs[0], dh_refs[1], rest[-1]
            part = None
            for j in range(bps):
                t = lax.dot_general(a_ref[:, j * Nb:(j + 1) * Nb], b_ref[j], (((1,), (1,)), ((), ())),
                                    preferred_element_type=F32)
                part = t if part is None else part + t

            @pl.when(k == 0)
            def _():
                acc[...] = part

            @pl.when(k > 0)
            def _():
                acc[...] += part

            @pl.when(k == nk - 1)
            def _():
                finish(acc[...])

    rs = pl.BlockSpec((tm, D), lambda i, k: (i, 0))
    vs = pl.BlockSpec((8, D), lambda i, k: (0, 0))
    if not fused:
        dh_ins, dh_specs = [dh], [rs]
    elif nk == 1:
        dh_ins, dh_specs = list(dh), [pl.BlockSpec((tm, Nb), lambda i, k: (i, 0)),
                                      pl.BlockSpec((None, D, Nb), lambda i, k: (0, 0, 0), pipeline_mode=pl.Buffered(1))]
    else:
        dh_ins, dh_specs = list(dh), [pl.BlockSpec((tm, bps * Nb), lambda i, k: (i, k)),
                                      pl.BlockSpec((bps, D, Nb), lambda i, k: (k, 0, 0))]
    ins = [xin, *dh_ins, dxup, vec] + ([fprev] if has_prev else [])
    in_specs = [rs, *dh_specs, rs, vs] + ([rs] if has_prev else [])
    out_shape = [_sds((S, D), F32)] + ([_sds((S, D), BF16)] if has_prev else []) + [_sds((8, D), F32)]
    out_specs = [rs] + ([rs] if has_prev else []) + [vs]
    outs = _pcall(_after(body, len(ins), deps), grid=(S // tm, nk), in_specs=in_specs + [ANY] * len(deps),
                  out_specs=out_specs, out_shape=out_shape, name=name,
                  scratch_shapes=[pltpu.VMEM((tm, D), F32)] if nk > 1 else [],
                  compiler_params=_params(("arbitrary", "arbitrary")))(*ins, *deps)
    return (outs[0], outs[1], outs[2]) if has_prev else (outs[0], None, outs[1])


def _gate_bwd(dmerged, z, ys, name, deps=()):
    S, D = dmerged.shape
    tm = min(512, S)
    ncol = z.shape[1] // D

    def body(dm_ref, g_ref, y_ref, dy_ref, dz_ref):
        sg = _sigmoid(g_ref[...].astype(F32))
        dm = dm_ref[...].astype(F32)
        dy_ref[...] = (dm * sg).astype(BF16)
        dz_ref[...] = (dm * y_ref[...].astype(F32) * (sg * (1.0 - sg))).astype(BF16)

    branch = pl.BlockSpec((None, tm, D), lambda i, n: (n, i, 0))
    return _pcall(_after(body, 3, deps), grid=(S // tm, 3),
                  in_specs=[pl.BlockSpec((tm, D), lambda i, n: (i, 0)), pl.BlockSpec((tm, D), lambda i, n: (i, 7 + n)),
                            branch] + [ANY] * len(deps),
                  out_specs=[branch, pl.BlockSpec((tm, D), lambda i, n: (i, 7 + n))],
                  out_shape=[_sds((3, S, D), BF16), _sds((S, ncol * D), BF16)], name=name,
                  compiler_params=_params(("parallel", "arbitrary")))(dmerged, z, ys, *deps)


def _mixer_bwd(z, dacts, conv, dz, wsh, sgu_ln, wtril, wtril_t, bias_full, cw, cvec, name):
    S = z.shape[0]
    D = wsh.shape[1]
    tm = CHUNK
    nt = S // tm
    hb = tm // HALO

    def body(zc, zp, da_ref, db_ref, dc_ref, conv_ref, wsh_ref, sln_ref, wt_ref, wtt_ref, bias_ref, cw_ref, cv_ref, _dz_in,
             dz_ref, vec_ref, dcw_ref, dws_ref, dbs_ref, pe, ge, dqe, dce, gr, dcr, cbuf, dcw8):
        i = pl.program_id(0)
        rb = nt - 1 - i

        @pl.when(i == 0)
        def _():
            vec_ref[...] = jnp.zeros_like(vec_ref)
            dcw8[...] = jnp.zeros_like(dcw8)
            dws_ref[...] = jnp.zeros_like(dws_ref)
            dbs_ref[...] = jnp.zeros_like(dbs_ref)
            dqe[tm:tm + HALO, :] = jnp.zeros((HALO, D), F32)
            dce[tm:tm + HALO, :] = jnp.zeros((HALO, D), F32)

        keep = (rb > 0).astype(F32)

        def col(n):
            return zc[:, n * D:(n + 1) * D].astype(F32)

        def pcol(n):
            return zp[:, n * D:(n + 1) * D].astype(F32)

        c_a, x_a = col(1), col(2)
        pe[0:HALO, :] = keep * (pcol(1) * pcol(2))
        pe[HALO:HALO + tm, :] = c_a * x_a
        q = wsh_ref[0:1, :] * pe[HALO - 2:HALO - 2 + tm, :]
        q = q + wsh_ref[1:2, :] * pe[HALO - 1:HALO - 1 + tm, :]
        q = q + wsh_ref[2:3, :] * pe[HALO:HALO + tm, :]
        dact = da_ref[...].astype(F32)
        dz_ref[:, 0:D] = (dact * q).astype(BF16)
        dq = dact * col(0)
        dqe[0:tm, :] = dq
        dp = wsh_ref[2:3, :] * dq + wsh_ref[1:2, :] * dqe[1:1 + tm, :] + wsh_ref[0:1, :] * dqe[2:2 + tm, :]
        dz_ref[:, D:2 * D] = (dp * x_a).astype(BF16)
        dz_ref[:, 2 * D:3 * D] = (dp * c_a).astype(BF16)
        for k in range(SHORT_K):
            o = HALO - (SHORT_K - 1) + k
            vec_ref[k:k + 1, :] += _rsum(dq * pe[o:o + tm, :])
        u, v = col(3), col(4)
        gu, tu = _gelu(u)
        gv, tv = _gelu(v)
        d = gv - _rmean(gv)
        rstd = lax.rsqrt(_rmean(d * d) + EPS)
        nrm = d * rstd
        vnb = (nrm * sln_ref[0:1, :] + sln_ref[1:2, :]).astype(BF16)
        dact = db_ref[...].astype(F32)
        dvn_parts, dgu_parts = [], []
        for g in range(NG):
            cs = slice(g * LANE, (g + 1) * LANE)
            vg = vnb[:, cs]
            mixed = jnp.dot(wt_ref[g], vg, preferred_element_type=F32) + bias_ref[:, cs]
            dgu_parts.append(dact[:, cs] * mixed)
            dmixed = dact[:, cs] * gu[:, cs]
            dmb = dmixed.astype(BF16)
            dws_ref[g] += lax.dot_general(dmb, vg, (((1,), (1,)), ((), ())), preferred_element_type=F32)
            dbs_ref[g] += jnp.broadcast_to(jnp.sum(dmixed, axis=1, keepdims=True), (CHUNK, LANE))
            dvn_parts.append(jnp.dot(wtt_ref[g], dmb, preferred_element_type=F32))
        dgu = jnp.concatenate(dgu_parts, axis=1)
        dvn = jnp.concatenate(dvn_parts, axis=1)
        dz_ref[:, 3 * D:4 * D] = (dgu * _dgelu(u, tu)).astype(BF16)
        vec_ref[3:4, :] += _rsum(dvn * nrm)
        vec_ref[4:5, :] += _rsum(dvn)
        dn = dvn * sln_ref[0:1, :]
        dgv = rstd * (dn - _rmean(dn) - nrm * _rmean(dn * nrm))
        dz_ref[:, 4 * D:5 * D] = (dgv * _dgelu(v, tv)).astype(BF16)
        a_c = col(5)
        sg = _sigmoid(col(6))
        ge[0:HALO, :] = keep * (pcol(5) * _sigmoid(pcol(6)))
        ge[HALO:HALO + tm, :] = a_c * sg
        _fill_shifted(ge, gr)
        o0 = HALO - (CFM_K - 1)
        conv = conv_ref[...].astype(F32)
        d = conv - _rmean(conv)
        rstd = lax.rsqrt(_rmean(d * d) + EPS)
        nrm = d * rstd
        ln = nrm * cv_ref[1:2, :] + cv_ref[2:3, :]
        sl = _sigmoid(ln)
        dln = dc_ref[...].astype(F32) * (sl * (1.0 + ln * (1.0 - sl)))
        vec_ref[6:7, :] += _rsum(dln * nrm)
        vec_ref[7:8, :] += _rsum(dln)
        dn = dln * cv_ref[1:2, :]
        dconv = rstd * (dn - _rmean(dn) - nrm * _rmean(dn * nrm))
        vec_ref[5:6, :] += _rsum(dconv)
        dce[0:tm, :] = dconv
        _fill_shifted(dce, dcr)
        _causal_conv(cw_ref, range(CFM_K), None, dce, dcr, [CFM_K - 1 - k for k in range(CFM_K)], tm, cbuf)
        dglu = cbuf[...]
        for cb in range(D // LANE):
            cs = slice(cb * LANE, (cb + 1) * LANE)
            dcv = dce[0:tm, cs]
            for k in range(CFM_K):
                prod = dcv * _rows_at(ge, gr, o0 + k, tm, cs)
                dcw8[k, :, cs] += jnp.sum(prod.reshape(tm // 8, 8, LANE), axis=0)

        @pl.when(i == nt - 1)
        def _():
            dcw_ref[...] = jnp.sum(dcw8[...], axis=1)
        dz_ref[:, 5 * D:6 * D] = (dglu * sg).astype(BF16)
        dz_ref[:, 6 * D:7 * D] = (dglu * a_c * (sg * (1.0 - sg))).astype(BF16)
        dqe[tm:tm + HALO, :] = dqe[0:HALO, :]
        dce[tm:tm + HALO, :] = dce[0:HALO, :]

    rev = lambda i: (nt - 1 - i, 0)
    rs = pl.BlockSpec((tm, D), rev)
    cur = pl.BlockSpec((tm, 7 * D), rev)
    prev = pl.BlockSpec((HALO, 7 * D), lambda i: (jnp.maximum((nt - 1 - i) * hb - 1, 0), 0))
    ext = pltpu.VMEM((HALO + tm, D), F32)
    outs = _pcall(
        body, grid=(nt,),
        in_specs=[cur, prev] + [pl.BlockSpec((None, tm, D), functools.partial(lambda i, n: (n, nt - 1 - i, 0), n=n))
                                for n in range(3)]
        + [rs, _const_spec((8, D)), _const_spec((8, D)), _const_spec((NG, CHUNK, CHUNK)),
                  _const_spec((NG, CHUNK, CHUNK)), _const_spec((CHUNK, D)), _const_spec((HALO, D)), _const_spec((8, D)),
                  ANY],
        out_specs=[cur, _const_spec((8, D)), _const_spec((HALO, D)), _const_spec((NG, CHUNK, CHUNK)),
                   _const_spec((NG, CHUNK, LANE))],
        out_shape=[_sds(dz.shape, BF16), _sds((8, D), F32), _sds((HALO, D), F32), _sds((NG, CHUNK, CHUNK), F32),
                   _sds((NG, CHUNK, LANE), F32)],
        scratch_shapes=[ext, ext, ext, ext, pltpu.VMEM((7, HALO + tm, D), F32), pltpu.VMEM((7, HALO + tm, D), F32),
                        pltpu.VMEM((tm, D), F32), pltpu.VMEM((HALO, 8, D), F32)],
        input_output_aliases={13: 0}, name=name,
        compiler_params=_params(("arbitrary",)))(z, z, dacts, dacts, dacts, conv, wsh, sgu_ln, wtril, wtril_t, bias_full, cw,
                                                 cvec, dz)
    return outs


def _ada_fwd(c_all, w_ada_loc, name):
    nb, D = c_all.shape
    L, _, nc = w_ada_loc.shape

    def body(c_ref, w_ref, o_ref, ca_ref):
        cv = c_ref[...]
        ca = cv * _sigmoid(cv)
        ca_ref[...] = ca
        o_ref[...] = jnp.dot(ca.astype(BF16), w_ref[...].astype(BF16), preferred_element_type=F32)

    return _pcall(body, grid=(L,),
                  in_specs=[_const_spec((nb, D)), pl.BlockSpec((None, D, nc), lambda l: (l, 0, 0))],
                  out_specs=[pl.BlockSpec((None, nb, nc), lambda l: (l, 0, 0)), _const_spec((nb, D))],
                  out_shape=[_sds((L, nb, nc), F32), _sds((nb, D), F32)], name=name,
                  compiler_params=_params(("arbitrary",)))(c_all, w_ada_loc)


def _adamw(w, g, m, v):
    m = ADAM_B1 * m + (1.0 - ADAM_B1) * g
    v = ADAM_B2 * v + (1.0 - ADAM_B2) * (g * g)
    m_hat = m / (1.0 - ADAM_B1 ** ADAM_STEP)
    v_hat = v / (1.0 - ADAM_B2 ** ADAM_STEP)
    delta = -ADAM_LR * (m_hat / (jnp.sqrt(v_hat) + ADAM_EPS) + ADAM_WD * w)
    return delta, m, v


def _tile_rows(R, C, align=8):
    cap = max(align, (1536 * 1024) // (4 * C))
    best = None
    for t in range(align, R + 1, align):
        if R % t == 0 and t <= cap:
            best = t
    return R if best is None else best


def _adam_ada(ct, dm, w, m, v, name):
    L, D, nc = w.shape
    nb = ct.shape[1]
    tr = _tile_rows(D, nc)

    def body(ct_ref, dm_ref, w_ref, m_ref, v_ref, g_ref, d_ref, mo_ref, vo_ref):
        g = ct_ref[:, 0:1] * dm_ref[0:1, :]
        for b in range(1, nb):
            g = g + ct_ref[:, b:b + 1] * dm_ref[b:b + 1, :]
        g_ref[...] = g
        d_ref[...], mo_ref[...], vo_ref[...] = _adamw(w_ref[...], g, m_ref[...], v_ref[...])

    ws = pl.BlockSpec((None, tr, nc), lambda l, r: (l, r, 0))
    return _pcall(body, grid=(L, D // tr),
                  in_specs=[pl.BlockSpec((tr, nb), lambda l, r: (r, 0)), pl.BlockSpec((None, nb, nc), lambda l, r: (l, 0, 0)),
                            ws, ws, ws],
                  out_specs=[ws] * 4, out_shape=[_sds(w.shape, F32)] * 4, name=name,
                  compiler_params=_params(("parallel", "parallel")))(ct, dm, w, m, v)


def _adam_small(parts, w, m, v, name, deps=(), single_rows=()):
    n, R, C = parts.shape
    tr = _tile_rows(R, C * n // 2)
    nl = len(single_rows[0]) if single_rows else 0

    def body(p_ref, w_ref, m_ref, v_ref, g_ref, d_ref, mo_ref, vo_ref, *single):
        g = p_ref[0]
        for j in range(1, n):
            g = g + p_ref[j]
        d, mo, vo = _adamw(w_ref[...], g, m_ref[...], v_ref[...])
        g_ref[...], d_ref[...], mo_ref[...], vo_ref[...] = g, d, mo, vo
        step = pl.program_id(0)
        for pi, rows in enumerate(single_rows):
            for l, row in enumerate(rows):
                @pl.when(step == row // tr)
                def _(pi=pi, l=l, off=row % tr):
                    for k, val in enumerate((g, d, mo, vo)):
                        single[4 * pi + k][l:l + 1, :] = val[off:off + 1, :]

    ws = pl.BlockSpec((tr, C), lambda r: (r, 0))
    one = pl.BlockSpec((nl, C), lambda r: (0, 0))
    outs = _pcall(_after(body, 4, deps), grid=(R // tr,),
                  in_specs=[pl.BlockSpec((n, tr, C), lambda r: (0, r, 0)), ws, ws, ws] + [ANY] * len(deps),
                  out_specs=[ws] * 4 + [one] * (4 * len(single_rows)),
                  out_shape=[_sds((R, C), F32)] * 4 + [_sds((nl, C), F32)] * (4 * len(single_rows)), name=name,
                  compiler_params=_params(("arbitrary",)))(parts, w, m, v, *deps)
    return outs[:4], outs[4:]


def _adam_plain(g, w, m, v, name):
    R, C = w.shape

    def body(g_ref, w_ref, m_ref, v_ref, d_ref, mo_ref, vo_ref):
        d_ref[...], mo_ref[...], vo_ref[...] = _adamw(w_ref[...], g_ref[...], m_ref[...], v_ref[...])

    ws = _const_spec((R, C))
    return _pcall(body, grid=(1,), in_specs=[ws] * 4, out_specs=[ws] * 3, out_shape=[_sds((R, C), F32)] * 3, name=name,
                  compiler_params=_params(("arbitrary",)))(g, w, m, v)


def _pair_sum(G, R1, my_c, name):
    n, R, C = G.shape
    half = n // 2
    tr = _tile_rows(R, C, align=16)

    def body(c_ref, g_ref, r_ref, o_ref):
        o_ref[...] = (g_ref[...].astype(F32) + r_ref[...].astype(F32)).astype(o_ref.dtype)

    blk = (None, tr, C)
    gs = pltpu.PrefetchScalarGridSpec(
        num_scalar_prefetch=1, grid=(half, R // tr),
        in_specs=[pl.BlockSpec(blk, lambda p, r, c: (2 * p + c[0], r, 0)), pl.BlockSpec(blk, lambda p, r, c: (p, r, 0))],
        out_specs=pl.BlockSpec(blk, lambda p, r, c: (p, r, 0)))
    return _pcall(body, grid_spec=gs, out_shape=_sds((half, R, C), G.dtype), name=name,
                  compiler_params=_params(("parallel", "parallel")))(my_c, G, R1)


def _adam_big(P, R2, my_chip, w, m, v, layer, prev, name, deps=()):
    _, R, C = P.shape
    nrecv = R2.shape[0]
    tr = _tile_rows(R, C, align=16)

    def body(p_sm, p_ref, r_ref, w_ref, m_ref, v_ref, *rest):
        g_ref, d_ref, mo_ref, vo_ref = rest[-4:]
        g = p_ref[...].astype(F32)
        for k in range(nrecv):
            g = g + r_ref[k].astype(F32)
        g_ref[...] = g
        d_ref[...], mo_ref[...], vo_ref[...] = _adamw(w_ref[...], g, m_ref[...], v_ref[...])

    ws = pl.BlockSpec((None, tr, C), lambda r, p: (layer, r, 0))
    held = [] if prev is None else list(prev)
    gs = pltpu.PrefetchScalarGridSpec(
        num_scalar_prefetch=1, grid=(R // tr,),
        in_specs=[pl.BlockSpec((None, tr, C), lambda r, p: (p[0], r, 0)),
                  pl.BlockSpec((nrecv, tr, C), lambda r, p: (0, r, 0)), ws, ws, ws] + [ANY] * (len(held) + len(deps)),
        out_specs=[ws] * 4)
    alias = {6 + i: i for i in range(len(held))}
    return _pcall(body, grid_spec=gs, out_shape=[_sds(w.shape, F32)] * 4, name=name, input_output_aliases=alias,
                  compiler_params=_params(("parallel",)))(my_chip, P, R2, w, m, v, *held, *deps)


def _place():
    return lax.axis_index("x"), lax.axis_index("y"), lax.axis_index("c")


def _all_gather(shards, name, deps=(), into=None):
    n = len(shards)
    bufs = [] if into is None else [b for b, _ in into]
    nb = len(bufs)

    def body(*refs):
        ins, outs = refs[:n], refs[n + nb:2 * n + nb]
        send_sems, recv_sems, local_sems = refs[2 * n + nb:]
        x, y, c = _place()
        me, sibling = (x, y, c), (x, y, 1 - c)
        chips = [(1 - x, y), (x, 1 - y), (1 - x, 1 - y)]

        def slot(a, px, py, pc):
            block = outs[a].at[4 * px + 2 * py + pc]
            return block if into is None else block.at[pl.ds(into[a][1], ins[a].shape[0])]

        def copy(a, k, block, to, src=None):
            return pltpu.make_async_remote_copy(
                src_ref=slot(a, *block) if src is None else src, dst_ref=slot(a, *block),
                send_sem=send_sems.at[7 * a + k], recv_sem=recv_sems.at[7 * a + k], device_id=to, device_id_type=MESH)

        mine = [pltpu.make_async_copy(ins[a], slot(a, *me), local_sems.at[a]) for a in range(n)]
        for cp in mine:
            cp.start()
        first = []
        for a in range(n):
            first.append(copy(a, 0, me, sibling, src=ins[a]))
            first += [copy(a, 1 + j, me, (*chip, c), src=ins[a]) for j, chip in enumerate(chips)]
        for cp in first:
            cp.start()
        passed = []
        for j, chip in enumerate(chips):
            for a in range(n):
                copy(a, 1 + j, (*chip, c), me).wait_recv()
                fwd = copy(a, 4 + j, (*chip, c), sibling)
                fwd.start()
                passed.append(fwd)
        for a in range(n):
            copy(a, 0, sibling, me).wait_recv()
        for j, chip in enumerate(chips):
            for a in range(n):
                copy(a, 4 + j, (*chip, 1 - c), me).wait_recv()
        for cp in first + passed:
            cp.wait_send()
        for cp in mine:
            cp.wait()

    out_shape = [_sds((NDEV,) + s.shape, s.dtype) for s in shards] if into is None else [_sds(b.shape, b.dtype) for b in bufs]
    outs = _pcall(_after(body, n + nb, deps), in_specs=[ANY] * (n + nb + len(deps)), out_specs=[ANY] * n,
                  out_shape=out_shape, input_output_aliases={n + a: a for a in range(nb)},
                  scratch_shapes=[pltpu.SemaphoreType.DMA((7 * n,)), pltpu.SemaphoreType.DMA((7 * n,)),
                                  pltpu.SemaphoreType.DMA((n,))], name=name)(*shards, *bufs, *deps)
    return list(outs)


HBM = pl.BlockSpec(memory_space=pltpu.HBM)
SEM = pl.BlockSpec(memory_space=pltpu.SEMAPHORE)


def _copies(plan, refs, send_sems, recv_sems):
    return [pltpu.make_async_remote_copy(src_ref=s, dst_ref=d, send_sem=send_sems.at[k], recv_sem=recv_sems.at[k],
                                         device_id=dev, device_id_type=MESH)
            for k, (s, d, dev) in enumerate(plan(refs, *_place()))]


def _xfer_start(bufs, ncopies, plan, name, deps=()):
    n = len(bufs)

    def body(*refs):
        for cp in _copies(plan, refs[:n], refs[n], refs[n + 1]):
            cp.start()
        token = refs[2 * n + 2]
        token[...] = jnp.zeros_like(token)

    outs = _pcall(
        _after(body, n, deps), name=name,
        out_shape=(pltpu.SemaphoreType.DMA((ncopies,)), pltpu.SemaphoreType.DMA((ncopies,)),
                   *[pltpu.HBM(b.shape, b.dtype) for b in bufs], _sds((8, LANE), F32)),
        in_specs=[HBM] * n + [ANY] * len(deps),
        out_specs=(SEM, SEM, *[HBM] * n, pl.BlockSpec(memory_space=pltpu.VMEM)),
        input_output_aliases={i: 2 + i for i in range(n)},
        compiler_params=pltpu.CompilerParams(has_side_effects=pltpu.SideEffectType.DATAFLOW_SIDE_EFFECTING),
    )(*[pltpu.with_memory_space_constraint(b, pltpu.HBM) for b in bufs], *deps)
    return (outs[0], outs[1]), list(outs[2:2 + n]), outs[2 + n]


def _xfer_wait(sems, bufs, plan, after, name):
    n = len(bufs)
    after = list(after) if isinstance(after, (list, tuple)) else [after]

    def body(*refs):
        for cp in _copies(plan, refs[:n], refs[n], refs[n + 1]):
            cp.wait_send()
            cp.wait_recv()

    outs = _pcall(
        body, name=name, out_shape=tuple(pltpu.HBM(b.shape, b.dtype) for b in bufs),
        in_specs=[HBM] * n + [SEM, SEM] + [ANY] * len(after), out_specs=tuple([HBM] * n),
        input_output_aliases={i: i for i in range(n)},
        compiler_params=pltpu.CompilerParams(has_side_effects=pltpu.SideEffectType.DATAFLOW_SIDE_EFFECTING),
    )(*bufs, *sems, *after)
    return list(outs)


def _chips_of(x, y):
    return [(1 - x, y), (x, 1 - y), (1 - x, 1 - y)]


def _landing(ref, dev_index, rows):
    block = ref.at[dev_index]
    return block if rows is None else block.at[pl.ds(rows[0], rows[1])]


def _gather_plan1(n, rows=None):
    def plan(refs, x, y, c):
        out = []
        for a in range(n):
            blk = _landing(refs[a], 4 * x + 2 * y + c, rows)
            out.append((blk, blk, (x, y, 1 - c)))
            out += [(blk, blk, (px, py, c)) for px, py in _chips_of(x, y)]
        return out
    return plan


def _gather_plan2(n, rows=None):
    def plan(refs, x, y, c):
        out = []
        for a in range(n):
            for px, py in _chips_of(x, y):
                blk = _landing(refs[a], 4 * px + 2 * py + c, rows)
                out.append((blk, blk, (x, y, 1 - c)))
        return out
    return plan


def _gather_start(shards, dev, name, deps=(), within=None):
    rows = None if within is None else (within[0], shards[0].shape[0])
    lands = []
    for s in shards:
        shape = (NDEV,) + s.shape if within is None else (NDEV, within[1]) + s.shape[1:]
        start = (dev,) + (0,) * s.ndim if within is None else (dev, within[0]) + (0,) * (s.ndim - 1)
        lands.append(lax.dynamic_update_slice(lax.empty(shape, s.dtype), s[None], start))
    n = len(shards)
    sems, lands, tok = _xfer_start(lands, 4 * n, _gather_plan1(n, rows), name + "_p1_start", deps)
    return dict(sems=sems, lands=lands, tok=tok, n=n, rows=rows)


def _gather_mid(st, after, name):
    n, rows = st["n"], st["rows"]
    lands = _xfer_wait(st["sems"], st["lands"], _gather_plan1(n, rows), after, name + "_p1_wait")
    sems, lands, tok = _xfer_start(lands, 3 * n, _gather_plan2(n, rows), name + "_p2_start")
    return dict(sems=sems, lands=lands, tok=tok, n=n, rows=rows)


def _gather_finish(st, after, name):
    return _xfer_wait(st["sems"], st["lands"], _gather_plan2(st["n"], st["rows"]), after, name + "_p2_wait")


def _scatter_plan1(n):
    def plan(refs, x, y, c):
        return [(refs[a].at[2 * p + 1 - c], refs[n + a].at[p], (x, y, 1 - c)) for a in range(n) for p in range(NCHIP)]
    return plan


def _scatter_plan2(n):
    def plan(refs, x, y, c):
        return [(refs[a].at[2 * px + py], refs[n + a].at[j], (px, py, c))
                for a in range(n) for j, (px, py) in enumerate(_chips_of(x, y))]
    return plan


def _scatter_start(Gs, name):
    n = len(Gs)
    R1s = [lax.empty((NCHIP,) + g.shape[1:], g.dtype) for g in Gs]
    sems, bufs, tok = _xfer_start(list(Gs) + R1s, NCHIP * n, _scatter_plan1(n), name + "_s1_start")
    return dict(sems=sems, bufs=bufs, tok=tok, n=n)


def _scatter_mid(st, after, my_c, name):
    n = st["n"]
    bufs = _xfer_wait(st["sems"], st["bufs"], _scatter_plan1(n), after, name + "_s1_wait")
    Ps = [_pair_sum(bufs[a], bufs[n + a], my_c, f"{name}_pair_sum{a}") for a in range(n)]
    R2s = [lax.empty((3,) + p.shape[1:], p.dtype) for p in Ps]
    sems, bufs, tok = _xfer_start(Ps + R2s, 3 * n, _scatter_plan2(n), name + "_s2_start")
    return dict(sems=sems, bufs=bufs, tok=tok, n=n)


def _scatter_finish(st, after, name):
    n = st["n"]
    bufs = _xfer_wait(st["sems"], st["bufs"], _scatter_plan2(n), after, name + "_s2_wait")
    return bufs[:n], bufs[n:]


SMALL_ROWS = {"norm1_g": (0, 1), "norm2_g": (1, 1), "sgu_ln_g": (2, 1), "sgu_ln_b": (3, 1), "cfm_conv_b": (4, 1),
              "cfm_ln_g": (5, 1), "cfm_ln_b": (6, 1), "b_sgu": (7, 1), "w_sgu": (8, 128), "b_ada": (136, N_MOD),
              "w_short": (142, SHORT_K), "cfm_conv_w": (145, CFM_K)}
ROWS_PER_LAYER = 176
FINAL_ROW = DEPTH * ROWS_PER_LAYER
PACK_ROWS = 360


def _pack(get, D, layers=tuple(range(DEPTH)), tail=True):
    parts = []
    for l in layers:
        for name, (_, nrows) in SMALL_ROWS.items():
            a = get(name, l)
            parts.append(jnp.zeros((nrows * D,), F32) if a is None else a.astype(F32).reshape(nrows * D))
    if tail:
        for name in ("final_g", "loss"):
            a = get(name, None)
            parts.append(jnp.zeros((D,), F32) if a is None else a.astype(F32).reshape(D))
        parts.append(jnp.zeros(((PACK_ROWS - FINAL_ROW - 2) * D,), F32))
    return jnp.concatenate(parts).reshape(-1, D)


def _unpack(pack, name, shape):
    D = pack.shape[1]
    r0, nrows = SMALL_ROWS[name]
    return jnp.stack([pack[l * ROWS_PER_LAYER + r0:l * ROWS_PER_LAYER + r0 + nrows] for l in range(DEPTH)]).reshape(shape)


def _mm_tiles(S):
    return min(512, S), min(1024, S), min(2048, S)


def kernel(x, c, w_ada, b_ada, norm1_g, w_in, w_short, w_a_out, sgu_ln_g, sgu_ln_b, w_sgu, b_sgu, w_b_out, cfm_conv_w, cfm_conv_b, cfm_ln_g, cfm_ln_b, w_c_out, w_o, norm2_g, w_ffn_in, w_ffn_out, final_g, loss_target, m_w_ada, m_b_ada, m_norm1_g, m_w_in, m_w_short, m_w_a_out, m_sgu_ln_g, m_sgu_ln_b, m_w_sgu, m_b_sgu, m_w_b_out, m_cfm_conv_w, m_cfm_conv_b, m_cfm_ln_g, m_cfm_ln_b, m_w_c_out, m_w_o, m_norm2_g, m_w_ffn_in, m_w_ffn_out, m_final_g, v_w_ada, v_b_ada, v_norm1_g, v_w_in, v_w_short, v_w_a_out, v_sgu_ln_g, v_sgu_ln_b, v_w_sgu, v_b_sgu, v_w_b_out, v_cfm_conv_w, v_cfm_conv_b, v_cfm_ln_g, v_cfm_ln_b, v_w_c_out, v_w_o, v_norm2_g, v_w_ffn_in, v_w_ffn_out, v_final_g):
    W = dict(w_ada=w_ada, b_ada=b_ada, norm1_g=norm1_g, w_in=w_in, w_short=w_short, w_a_out=w_a_out, sgu_ln_g=sgu_ln_g,
             sgu_ln_b=sgu_ln_b, w_sgu=w_sgu, b_sgu=b_sgu, w_b_out=w_b_out, cfm_conv_w=cfm_conv_w, cfm_conv_b=cfm_conv_b,
             cfm_ln_g=cfm_ln_g, cfm_ln_b=cfm_ln_b, w_c_out=w_c_out, w_o=w_o, norm2_g=norm2_g, w_ffn_in=w_ffn_in,
             w_ffn_out=w_ffn_out, final_g=final_g)
    Mo = dict(w_ada=m_w_ada, b_ada=m_b_ada, norm1_g=m_norm1_g, w_in=m_w_in, w_short=m_w_short, w_a_out=m_w_a_out,
              sgu_ln_g=m_sgu_ln_g, sgu_ln_b=m_sgu_ln_b, w_sgu=m_w_sgu, b_sgu=m_b_sgu, w_b_out=m_w_b_out,
              cfm_conv_w=m_cfm_conv_w, cfm_conv_b=m_cfm_conv_b, cfm_ln_g=m_cfm_ln_g, cfm_ln_b=m_cfm_ln_b,
              w_c_out=m_w_c_out, w_o=m_w_o, norm2_g=m_norm2_g, w_ffn_in=m_w_ffn_in, w_ffn_out=m_w_ffn_out,
              final_g=m_final_g)
    Vo = dict(w_ada=v_w_ada, b_ada=v_b_ada, norm1_g=v_norm1_g, w_in=v_w_in, w_short=v_w_short, w_a_out=v_w_a_out,
              sgu_ln_g=v_sgu_ln_g, sgu_ln_b=v_sgu_ln_b, w_sgu=v_w_sgu, b_sgu=v_b_sgu, w_b_out=v_w_b_out,
              cfm_conv_w=v_cfm_conv_w, cfm_conv_b=v_cfm_conv_b, cfm_ln_g=v_cfm_ln_g, cfm_ln_b=v_cfm_ln_b,
              w_c_out=v_w_c_out, w_o=v_w_o, norm2_g=v_norm2_g, w_ffn_in=v_w_ffn_in, w_ffn_out=v_w_ffn_out,
              final_g=v_final_g)
    order = ["w_ada", "b_ada", "norm1_g", "w_in", "w_short", "w_a_out", "sgu_ln_g", "sgu_ln_b", "w_sgu", "b_sgu",
             "w_b_out", "cfm_conv_w", "cfm_conv_b", "cfm_ln_g", "cfm_ln_b", "w_c_out", "w_o", "norm2_g", "w_ffn_in",
             "w_ffn_out", "final_g"]

    assert DEPTH == 2, "the weight-gather schedule below is written for two layers"
    S, D = x.shape[1], x.shape[2]
    F2 = w_ffn_in.shape[2] * NDEV
    FF = F2 // 2
    xi, yi, ci = _place()
    dev = 4 * xi + 2 * yi + ci
    my_c = jnp.reshape(ci, (1,)).astype(jnp.int32)
    my_chip = jnp.reshape(2 * xi + yi, (1,)).astype(jnp.int32)
    tm, tm_big, tm_huge = _mm_tiles(S)
    x0 = x.reshape(S, D)
    tgt = loss_target.reshape(S, D)

    def shards_of(l):
        return [w_in[l].astype(BF16), w_a_out[l].astype(BF16), w_b_out[l].astype(BF16), w_c_out[l].astype(BF16),
                w_o[l].astype(BF16), w_ffn_in[l].astype(BF16), w_ffn_out[l].astype(BF16)]

    c_all = _all_gather([jnp.pad(c, ((0, 7), (0, 0)))], "ag_c")[0][:, 0, :]
    modpart, c_act = _ada_fwd(c_all, w_ada, "ada_fwd")
    ncol = modpart.shape[2]
    mg = _all_gather([modpart.reshape(DEPTH * NDEV, ncol)], "ag_mod")[0].reshape(NDEV, DEPTH, NDEV, ncol)
    mine = lax.dynamic_index_in_dim(mg, dev, axis=2, keepdims=False)
    mod = (jnp.transpose(mine, (1, 0, 2)).reshape(DEPTH, N_MOD * D) + b_ada).reshape(DEPTH, N_MOD, D)

    ncs = w_short.shape[2]
    ag_in0 = _gather_start([w_in[0].astype(BF16), w_short.reshape(DEPTH * SHORT_K, ncs),
                            cfm_conv_w.reshape(DEPTH * CFM_K, ncs)], dev, "ag_w_in0", deps=(mod,))
    W, Mo, Vo = lax.optimization_barrier((ag_in0["tok"], (W, Mo, Vo)))[1]
    (norm1_g, norm2_g, w_in, w_a_out, w_b_out, w_c_out, w_o, w_ffn_in, w_ffn_out, sgu_ln_g, sgu_ln_b, w_sgu, b_sgu,
     cfm_conv_b, cfm_ln_g, cfm_ln_b, final_g) = [W[k] for k in (
         "norm1_g", "norm2_g", "w_in", "w_a_out", "w_b_out", "w_c_out", "w_o", "w_ffn_in", "w_ffn_out", "sgu_ln_g",
         "sgu_ln_b", "w_sgu", "b_sgu", "cfm_conv_b", "cfm_ln_g", "cfm_ln_b", "final_g")]
    m_w_ada, v_w_ada = Mo["w_ada"], Vo["w_ada"]
    xl0, h0, ht0 = _norm_fwd(x0, None, _rows(jnp.zeros((D,), F32), norm1_g[0], mod[0, 1], mod[0, 0]), "norm1_fwd0",
                             deps=(ag_in0["tok"],))
    ag_rest0 = _gather_start(shards_of(0)[1:5], dev, "ag_rest0", deps=(h0,))
    ag_ffn0 = _gather_start(shards_of(0)[5:], dev, "ag_ffn0", deps=(ag_rest0["tok"],))

    tril = jnp.tril(jnp.ones((CHUNK, CHUNK), dtype=bool))

    def layer_consts(l):
        wt = jnp.where(tril[None], w_sgu[l], 0.0).astype(BF16)
        return dict(sgu_ln=_rows(sgu_ln_g[l], sgu_ln_b[l]), wtril=wt, wtril_t=jnp.swapaxes(wt, 1, 2),
                    bias_full=jnp.repeat(b_sgu[l].T, LANE, axis=1), cvec=_rows(cfm_conv_b[l], cfm_ln_g[l], cfm_ln_b[l]))

    def rest_of(g):
        return dict(w_a=g[0].reshape(1, D, D), w_b=g[1].reshape(1, D, D), w_c=g[2].reshape(1, D, D),
                    w_o=g[3].reshape(1, D, D), w_fi=jnp.transpose(g[4], (1, 0, 2)).reshape(1, D, F2),
                    w_fo=g[5].reshape(1, FF, D))

    sharded_small = ("w_short", "cfm_conv_w")

    def param_get(T):
        def get(name, l):
            if name == "final_g":
                return T[name]
            return None if name in sharded_small or name == "loss" else T[name][l]
        return get

    packs = [_pack(param_get(T), D) for T in (W, Mo, Vo)]
    ag_in0 = _gather_mid(ag_in0, [ag_ffn0["tok"], *packs], "ag_w_in0")
    (w_sgu, b_sgu, sgu_ln_g, sgu_ln_b, cfm_conv_b, cfm_ln_g, cfm_ln_b), conv_wmv_in = lax.optimization_barrier(
        (ag_in0["tok"], ((w_sgu, b_sgu, sgu_ln_g, sgu_ln_b, cfm_conv_b, cfm_ln_g, cfm_ln_b),
                         [(T["w_short"], T["cfm_conv_w"]) for T in (W, Mo, Vo)])))[1]
    consts = [layer_consts(l) for l in range(DEPTH)]
    ncr = DEPTH * (SHORT_K + CFM_K)
    padr = (-ncr) % 8
    convw_wmv = [jnp.pad(jnp.concatenate([a.reshape(-1, ncs), b.reshape(-1, ncs)]), ((0, padr), (0, 0)))
                 for a, b in conv_wmv_in]
    g_in0 = _gather_finish(ag_in0, [*convw_wmv] + [a for cl in consts for a in cl.values()], "ag_w_in0")
    w_short_full = jnp.transpose(g_in0[1], (1, 0, 2)).reshape(DEPTH, SHORT_K, D)
    cfm_w_full = jnp.transpose(g_in0[2], (1, 0, 2)).reshape(DEPTH, CFM_K, D)
    for l in range(DEPTH):
        consts[l]["wsh"] = jnp.pad(w_short_full[l], ((0, 8 - SHORT_K), (0, 0)))
        consts[l]["cw"] = jnp.pad(cfm_w_full[l], ((0, HALO - CFM_K), (0, 0)))
    Wg = [dict(w_in=g_in0[0]), None]
    ag_l1 = None
    nin = w_in.shape[2]
    tn_in = nin if nin % 256 == 0 and nin <= 1280 else 256
    tn_fi = 512 if F2 % 512 == 0 else 256
    tn_dw = min(256, D)

    saved = []
    xcur, gprev, ffn_tail = x0, None, None
    for l in range(DEPTH):
        sh1, sc1, g1, sh2, sc2, g2 = [mod[l, k] for k in range(N_MOD)]
        cl = consts[l]
        if l == 0:
            xl, h, ht = xl0, h0, ht0
        else:
            vec1 = _rows(gprev, norm1_g[l], sc1, sh1)
            act_prev, w_fo_prev = ffn_tail
            ag_l1 = _gather_mid(ag_l1, act_prev, f"ag_w{l}")
            f_prev, xl, h, ht = _mm_resid_norm(act_prev, w_fo_prev, xcur, vec1, tm, f"mm_ffn_out_norm1_{l}",
                                               deps=(ag_l1["tok"],))
            saved[l - 1]["f"] = f_prev
            g = _gather_finish(ag_l1, h, f"ag_w{l}")
            Wg[l] = dict(w_in=g[0], **rest_of(g[1:]))
        wl = Wg[l]
        z = _mm_nn(h, wl["w_in"], BF16, tm_huge, tn_in, D, f"mm_in{l}", w_outer=True)
        mix_deps, o_deps = (), ()
        if l == 0:
            ag_rest0 = _gather_mid(ag_rest0, z, "ag_rest0")
            mix_deps = (ag_rest0["tok"],)
            if DEPTH > 1:
                ag_l1 = _gather_start(shards_of(1), dev, "ag_w1")
                mix_deps += (ag_l1["tok"],)
        acts, acts_t, conv = _mixer_fwd(z, cl["wsh"], cl["sgu_ln"], cl["wtril"], cl["bias_full"], cl["cw"], cl["cvec"],
                                        f"mixer_fwd{l}", deps=mix_deps)
        if l == 0:
            g = _gather_finish(ag_rest0, acts[0], "ag_rest0")
            wl.update(w_a=g[0].reshape(1, D, D), w_b=g[1].reshape(1, D, D), w_c=g[2].reshape(1, D, D),
                      w_o=g[3].reshape(1, D, D))
            ag_ffn0 = _gather_mid(ag_ffn0, acts[0], "ag_ffn0")
            o_deps = (ag_ffn0["tok"],)
        merged, merged_t, ys = _branch_out(acts, [wl["w_a"][0], wl["w_b"][0], wl["w_c"][0]], z, f"branch_out{l}")
        o, x1, h2, h2t = _mm_resid_norm(merged, wl["w_o"], xl, _rows(g1, norm2_g[l], sc2, sh2), tm, f"mm_o_norm2_{l}",
                                        deps=o_deps)
        if l == 0:
            g = _gather_finish(ag_ffn0, h2, "ag_ffn0")
            wl.update(w_fi=jnp.transpose(g[0], (1, 0, 2)).reshape(1, D, F2), w_fo=g[1].reshape(1, FF, D))
        gu, act, act_t = _ffn_in_swiglu(h2, wl["w_fi"], tm_huge, 256, f"mm_ffn_in{l}")
        saved.append(dict(xl=xl, ht=ht, z=z, acts_t=acts_t, conv=conv, ys=ys, merged_t=merged_t, o=o, x1=x1, h2t=h2t, gu=gu,
                          act_t=act_t, f=None, consts=cl, mod=(sh1, sc1, g1, sh2, sc2, g2)))
        xcur, gprev, ffn_tail = x1, g2, (act, wl["w_fo"])

    last = saved[-1]
    dxup, dfb, fsums, loss_blk = _final_bwd(last["x1"], *ffn_tail, tgt, _rows(last["mod"][5], final_g), "final_bwd")
    loss_row = jnp.pad(loss_blk[0, 0:1], (0, D - 1))
    dgate2_next = fsums[1]
    small = [dict() for _ in range(DEPTH)]
    dmods = [None] * DEPTH
    nfi = w_ffn_in.shape[2]
    early_names, late_names = ["w_ffn_out", "w_ffn_in", "w_o"], ["w_a_out", "w_b_out", "w_c_out", "w_in"]
    results = {n: None for n in early_names + late_names}

    def adam_group(names, Ps, R2s, l, deps=()):
        for n, p, r2 in zip(names, Ps, R2s):
            results[n] = _adam_big(p, r2, my_chip, W[n], Mo[n], Vo[n], l, results[n], f"adam_{n}{l}", deps)

    deferred = []
    late_prev = None
    ag_s1, gathered1 = None, None
    tk_w = min(2048, S)
    tn_dw_in = tn_in
    for l in reversed(range(DEPTH)):
        sv, wl, cl = saved[l], Wg[l], saved[l]["consts"]
        sh1, sc1, g1, sh2, sc2, g2 = sv["mod"]
        dgu = _swiglu_bwd(dfb, wl["w_fo"], sv["gu"], f"mm_dact_swiglu_bwd{l}",
                          deps=() if late_prev is None else (late_prev["tok"], ag_s1["tok"]))
        g_fo = _mm_wgrad(sv["act_t"], dfb, 1, FF // 2, D, tk_w, f"mm_dw_ffn_out{l}")
        g_fi = _mm_wgrad(sv["h2t"], dgu, 1, D, tn_fi, S, f"mm_dw_ffn_in{l}")
        if late_prev is not None:
            deferred.append((late_names, *_scatter_finish(late_prev, g_fi, f"rs_late{l + 1}"), l + 1))
            late_prev = None
        dx1, dob, s2 = _norm_bwd(sv["x1"], (dgu, wl["w_fi"]), dxup, _rows(norm2_g[l], sc2, g1), sv["o"],
                                 f"mm_dh2_norm2_bwd{l}")
        dmerged = _mm_nt(dob, wl["w_o"], BF16, tm_big, D, D, f"mm_dmerged{l}")
        g_o = _mm_wgrad(sv["merged_t"], dob, 1, D, tn_dw, S, f"mm_dw_o{l}")
        early = _scatter_start([g_fo.reshape(NDEV, FF // NDEV, D),
                                jnp.transpose(g_fi.reshape(D, NDEV, nfi), (1, 0, 2)),
                                g_o.reshape(NDEV, D // NDEV, D)], f"rs_early{l}")
        dys, dz = _gate_bwd(dmerged, sv["z"], sv["ys"], f"gate_bwd{l}", deps=(early["tok"],))
        early = _scatter_mid(early, dys, my_c, f"rs_early{l}")
        abc_deps = (early["tok"],)
        if ag_s1 is not None:
            ag_s1 = _gather_mid(ag_s1, dys, "ag_small1")
            abc_deps += (ag_s1["tok"],)
        dacts = _mm3_nt(dys, [wl["w_a"], wl["w_b"], wl["w_c"]], tm_big, f"mm_dact_abc{l}", deps=abc_deps)
        g3 = _mm3_wgrad(sv["acts_t"], dys, tn_dw, f"mm_dw_abc{l}")
        g_abc = [g3[n] for n in range(3)]
        dz, mvec, dcw, dws, dbs = _mixer_bwd(sv["z"], dacts, sv["conv"], dz, cl["wsh"], cl["sgu_ln"], cl["wtril"],
                                             cl["wtril_t"], cl["bias_full"], cl["cw"], cl["cvec"], f"mixer_bwd{l}")
        if ag_s1 is not None:
            gathered1 = _gather_finish(ag_s1, dz, "ag_small1")[0]
            ag_s1 = None
        g_in = _mm_wgrad(sv["ht"], dz, NDEV, D, tn_dw_in, S, f"mm_dw_in{l}")
        late = _scatter_start([g.reshape(NDEV, D // NDEV, D) for g in g_abc] + [g_in], f"rs_late{l}")
        if l > 0:
            pv = saved[l - 1]
            dxup, dfb, s1 = _norm_bwd(sv["xl"], (dz, wl["w_in"]), dx1, _rows(norm1_g[l], sc1, pv["mod"][5]), pv["f"],
                                      f"mm_dh_norm1_bwd{l}", deps=(late["tok"],))
        else:
            dxup, dfb, s1 = _norm_bwd(sv["xl"], (dz, wl["w_in"]), dx1, _rows(norm1_g[l], sc1), None,
                                      f"mm_dh_norm1_bwd{l}", deps=(late["tok"],))
        deferred.append((early_names, *_scatter_finish(early, dxup, f"rs_early{l}"), l))
        dmods[l] = jnp.stack([s1[0], s1[1], s2[3], s2[0], s2[1], dgate2_next])
        dgate2_next = s1[3]
        small[l] = dict(norm1_g=s1[2], norm2_g=s2[2], sgu_ln_g=mvec[3], sgu_ln_b=mvec[4], cfm_conv_b=mvec[5],
                        cfm_ln_g=mvec[6], cfm_ln_b=mvec[7], b_sgu=dbs[:, :, 0],
                        w_sgu=jnp.where(tril[None], dws, 0.0), b_ada=dmods[l], w_short=mvec[0:SHORT_K],
                        cfm_conv_w=dcw[0:CFM_K])
        small_get = lambda name, k: {"final_g": fsums[0], "loss": loss_row}.get(name) if k is None else small[k][name]
        if l > 0:
            late_prev = _scatter_mid(late, dxup, my_c, f"rs_late{l}")
            ag_s1 = _gather_start([_pack(small_get, D, layers=(l,), tail=True)], dev, "ag_small1", deps=(late_prev["tok"],),
                                  within=(l * ROWS_PER_LAYER, PACK_ROWS))
    grad_x = dxup.reshape(x.shape)

    gathered = _all_gather([_pack(small_get, D, layers=(0,), tail=False)], "ag_small0", deps=(dxup,),
                           into=[(gathered1, 0)])[0]
    late_prev = _scatter_mid(late, gathered, my_c, "rs_late0")
    one_row = [n for n in order if n in SMALL_ROWS and SMALL_ROWS[n][1] == 1 and W[n].ndim == 2]
    (sg, sd, sm, sv_), singles = _adam_small(
        gathered, *packs, name="adam_small", deps=(late_prev["tok"],),
        single_rows=[tuple(l * ROWS_PER_LAYER + SMALL_ROWS[n][0] for l in range(DEPTH)) for n in one_row])
    loss = sg[FINAL_ROW + 1, 0]
    out = {n: tuple(singles[4 * i:4 * i + 4]) for i, n in enumerate(one_row)}
    for name in order:
        if name in SMALL_ROWS and name not in sharded_small and name not in out:
            out[name] = tuple(_unpack(p, name, W[name].shape) for p in (sg, sd, sm, sv_))
    out["final_g"] = tuple(p[FINAL_ROW] for p in (sg, sd, sm, sv_))

    def my_cols(name):
        full = _unpack(sg, name, (DEPTH, SMALL_ROWS[name][1], D))
        return lax.dynamic_slice_in_dim(full, dev * ncs, ncs, axis=2)

    gcs = jnp.concatenate([my_cols("w_short").reshape(-1, ncs), my_cols("cfm_conv_w").reshape(-1, ncs)])
    cd, cm, cv = _adam_plain(jnp.pad(gcs, ((0, padr), (0, 0))), *convw_wmv, "adam_convw")
    nsh = DEPTH * SHORT_K
    out["w_short"] = tuple(a[0:nsh].reshape(w_short.shape) for a in (gcs, cd, cm, cv))
    out["cfm_conv_w"] = tuple(a[nsh:ncr].reshape(cfm_conv_w.shape) for a in (gcs, cd, cm, cv))

    dm_all = jnp.stack([gathered[:, l * ROWS_PER_LAYER + 136:l * ROWS_PER_LAYER + 136 + N_MOD, :].reshape(NDEV, N_MOD * D)
                        for l in range(DEPTH)])
    dm_mine = lax.dynamic_slice_in_dim(dm_all, dev * ncol, ncol, axis=2)
    out["w_ada"] = tuple(_adam_ada(jnp.transpose(c_act), dm_mine, w_ada, m_w_ada, v_w_ada, "adam_ada"))

    for names, Ps, R2s, l in deferred:
        adam_group(names, Ps, R2s, l, deps=(late_prev["tok"],))
    adam_group(late_names, *_scatter_finish(late_prev, results["w_o"][0], "rs_late0"), 0)
    for n in early_names + late_names:
        out[n] = tuple(results[n])

    grads = [out[n][0] for n in order]
    deltas = [out[n][1] for n in order]
    new_m = [out[n][2] for n in order]
    new_v = [out[n][3] for n in order]
    return (loss, grad_x, *grads, *deltas, *new_m, *new_v)
```

```python
import functools
import math

import jax
import jax.numpy as jnp
from jax import lax
from jax.experimental import pallas as pl
from jax.experimental.pallas import tpu as pltpu

F32, BF16 = jnp.float32, jnp.bfloat16
NDEV = 8
NCHIP = NDEV // 2
DEPTH = 2
EPS = 1e-6
CHUNK = 128
NG = 8
SHORT_K = 3
CFM_K = 31
HALO = 32
N_MOD = 6
LANE = 128
VMEM_LIMIT = 56 * 1024 * 1024
ADAM_LR, ADAM_B1, ADAM_B2, ADAM_EPS, ADAM_WD, ADAM_STEP = 0.001, 0.9, 0.999, 1e-08, 0.01, 10
_G0 = math.sqrt(2.0 / math.pi)
_G1 = 0.044715
MESH = pl.DeviceIdType.MESH
ANY = pl.BlockSpec(memory_space=pl.ANY)


def _pcall(body, **kw):
    return pl.pallas_call(body, **kw)


def _params(sem=None):
    return pltpu.CompilerParams(dimension_semantics=sem, vmem_limit_bytes=VMEM_LIMIT)


def _sds(shape, dtype):
    return jax.ShapeDtypeStruct(tuple(shape), dtype)


def _mm_body(dims, nk, out_f32, blocks=1):
    def body(a_ref, b_ref, o_ref, *scr):
        k = pl.program_id(2)
        if blocks == 1:
            part = lax.dot_general(a_ref[...], b_ref[...], dims, preferred_element_type=F32)
        else:
            w = a_ref.shape[1] // blocks
            part = None
            for g in range(blocks):
                t = lax.dot_general(a_ref[:, g * w:(g + 1) * w], b_ref[g], dims, preferred_element_type=F32)
                part = t if part is None else part + t
        if nk == 1:
            o_ref[...] = part.reshape(o_ref.shape).astype(o_ref.dtype)
        elif out_f32:
            @pl.when(k == 0)
            def _():
                o_ref[...] = part.reshape(o_ref.shape)

            @pl.when(k > 0)
            def _():
                o_ref[...] += part.reshape(o_ref.shape)
        else:
            acc = scr[0]

            @pl.when(k == 0)
            def _():
                acc[...] = part

            @pl.when(k > 0)
            def _():
                acc[...] += part

            @pl.when(k == nk - 1)
            def _():
                o_ref[...] = acc[...].astype(o_ref.dtype)
    return body


def _after(body, n_in, deps):
    nd = len(deps)
    if nd == 0:
        return body

    def ordered(*refs):
        return body(*refs[:n_in], *refs[n_in + nd:])
    return ordered


def _mm_call(body, grid, in_specs, out_spec, out_shape, acc_shape, name, deps=()):
    scratch = [] if acc_shape is None else [pltpu.VMEM(acc_shape, F32)]
    return _pcall(_after(body, 2, deps), grid=grid, in_specs=in_specs + [ANY] * len(deps), out_specs=out_spec,
                  out_shape=out_shape, scratch_shapes=scratch, name=name,
                  compiler_params=_params(("parallel", "parallel", "arbitrary")))


def _mm_nn(a, b3, out_dtype, tm, tn, tk, name, w_outer=False, deps=()):
    M, K = a.shape
    G, _, Nb = b3.shape
    npb, nk = Nb // tn, K // tk
    out_f32 = out_dtype == F32
    body = _mm_body((((1,), (0,)), ((), ())), nk, out_f32)
    if w_outer:
        grid = (G * npb, M // tm, nk)
        ij = lambda p, q: (q, p)
    else:
        grid = (M // tm, G * npb, nk)
        ij = lambda p, q: (p, q)

    def a_map(p, q, k):
        i, j = ij(p, q)
        return (i, k)

    def b_map(p, q, k):
        i, j = ij(p, q)
        return (j // npb, k, j % npb)

    def o_map(p, q, k):
        return ij(p, q)

    def wrapped(a_ref, b_ref, o_ref, *scr):
        body(a_ref, b_ref, o_ref, *scr)

    return _mm_call(wrapped, grid, [pl.BlockSpec((tm, tk), a_map), pl.BlockSpec((None, tk, tn), b_map)],
                    pl.BlockSpec((tm, tn), o_map), _sds((M, G * Nb), out_dtype),
                    None if (nk == 1 or out_f32) else (tm, tn), name, deps)(a, b3, *deps)


def _mm_nt(a, b3, out_dtype, tm, tn, tk, name, deps=(), blocks_per_step=1):
    M, _ = a.shape
    G, Ko, Nb = b3.shape
    kpb = Nb // tk
    nk = G * kpb // blocks_per_step
    out_f32 = out_dtype == F32
    body = _mm_body((((1,), (1,)), ((), ())), nk, out_f32, blocks_per_step)

    def wrapped(a_ref, b_ref, o_ref, *scr):
        body(a_ref, b_ref, o_ref, *scr)

    if blocks_per_step > 1:
        assert tk == Nb and G % blocks_per_step == 0
        b_spec = pl.BlockSpec((blocks_per_step, tn, tk), lambda i, j, k: (k, j, 0))
    else:
        b_spec = pl.BlockSpec((None, tn, tk), lambda i, j, k: (k // kpb, j, k % kpb))
    return _mm_call(wrapped, (M // tm, Ko // tn, nk),
                    [pl.BlockSpec((tm, tk * blocks_per_step), lambda i, j, k: (i, k)), b_spec],
                    pl.BlockSpec((tm, tn), lambda i, j, k: (i, j)), _sds((M, Ko), out_dtype),
                    None if (nk == 1 or out_f32) else (tm, tn), name, deps)(a, b3, *deps)


def _mm_wgrad(at, b, G, tm, tn, tk, name, deps=()):
    M, T = at.shape
    Nb = b.shape[1] // G
    npb, nk = Nb // tn, T // tk
    body = _mm_body((((1,), (0,)), ((), ())), nk, False)

    def wrapped(a_ref, b_ref, o_ref, *scr):
        body(a_ref, b_ref, o_ref, *scr)

    a = at
    in_specs = [pl.BlockSpec((tm, tk), lambda i, j, k: (i, k)), pl.BlockSpec((tk, tn), lambda i, j, k: (k, j))]
    out_spec = pl.BlockSpec((None, tm, tn), lambda i, j, k: (j // npb, i, j % npb))
    return _mm_call(wrapped, (M // tm, G * npb, nk), in_specs, out_spec, _sds((G, M, Nb), BF16),
                    None if nk == 1 else (tm, tn), name, deps)(a, b, *deps)


def _mm3_nt(x3, ws, tm, name, deps=()):
    nb, S, K = x3.shape
    Ko = ws[0].shape[1]

    def body(x_ref, w0, w1, w2, o_ref):
        n = pl.program_id(0)
        for k, w in enumerate((w0, w1, w2)):
            @pl.when(n == k)
            def _(w=w):
                o_ref[...] = lax.dot_general(x_ref[...], w[...], (((1,), (1,)), ((), ())),
                                             preferred_element_type=F32).astype(BF16)

    wspec = pl.BlockSpec((None, Ko, K), lambda n, i: (0, 0, 0))
    return _pcall(_after(body, 4, deps), grid=(nb, S // tm),
                  in_specs=[pl.BlockSpec((None, tm, K), lambda n, i: (n, i, 0)), wspec, wspec, wspec] + [ANY] * len(deps),
                  out_specs=pl.BlockSpec((None, tm, Ko), lambda n, i: (n, i, 0)), out_shape=_sds((nb, S, Ko), BF16),
                  name=name, compiler_params=_params(("arbitrary", "parallel")))(x3, *ws, *deps)


def _mm3_wgrad(at3, b3, tn, name):
    nb, M, T = at3.shape
    N = b3.shape[2]

    def body(a_ref, b_ref, o_ref):
        o_ref[...] = jnp.dot(a_ref[...], b_ref[...], preferred_element_type=F32).astype(BF16)

    return _pcall(body, grid=(nb, N // tn),
                  in_specs=[pl.BlockSpec((None, M, T), lambda n, j: (n, 0, 0)), pl.BlockSpec((None, T, tn), lambda n, j: (n, 0, j))],
                  out_specs=pl.BlockSpec((None, M, tn), lambda n, j: (n, 0, j)), out_shape=_sds((nb, M, N), BF16),
                  name=name, compiler_params=_params(("arbitrary", "parallel")))(at3, b3)


def _rsum(v):
    return jnp.sum(v, axis=0, keepdims=True)


def _rmean(v):
    return jnp.mean(v, axis=-1, keepdims=True)


def _gelu(x):
    t = jnp.tanh(_G0 * (x + _G1 * (x * x * x)))
    return x * (0.5 * (1.0 + t)), t


def _dgelu(x, t):
    return 0.5 * (1.0 + t) + 0.5 * x * (1.0 - t * t) * (_G0 * (1.0 + 3.0 * _G1 * (x * x)))


def _sigmoid(x):
    return 0.5 * jnp.tanh(0.5 * x) + 0.5


def _fill_shifted(ext, rot):
    v = ext[...]
    n = v.shape[0]
    for b in range(1, 8):
        rot[b - 1] = pltpu.roll(v, n - b, 0)


def _rows_at(ext, rot, s, tm, cs=slice(None)):
    a, b = divmod(s, 8)
    return ext[8 * a:8 * a + tm, cs] if b == 0 else rot[b - 1, 8 * a:8 * a + tm, cs]


def _causal_conv(w_ref, taps, bias, ext, rot, offset, tm, out):
    D = out.shape[1]
    for cb in range(D // LANE):
        cs = slice(cb * LANE, (cb + 1) * LANE)
        acc = None
        for k, o in zip(taps, offset):
            term = w_ref[k:k + 1, cs] * _rows_at(ext, rot, o, tm, cs)
            acc = term if acc is None else acc + term
        out[:, cs] = acc if bias is None else acc + bias[:, cs]


def _rows(*vs):
    a = jnp.stack([v.astype(F32) for v in vs])
    return jnp.pad(a, ((0, 8 - len(vs)), (0, 0)))


def _row_spec(tm, D):
    return pl.BlockSpec((tm, D), lambda i: (i, 0))


def _const_spec(shape):
    nd = len(shape)
    return pl.BlockSpec(shape, lambda i: (0,) * nd)


def _norm_fwd(xp, f, vec, name, deps=()):
    S, D = xp.shape
    tm = min(512, S)
    has_f = f is not None

    def body(*refs):
        if has_f:
            xp_ref, f_ref, vec_ref, xo_ref, h_ref, ht_ref = refs
            x = xp_ref[...] + vec_ref[0:1, :] * f_ref[...]
            xo_ref[...] = x
        else:
            xp_ref, vec_ref, h_ref, ht_ref = refs
            x = xp_ref[...]
        r = lax.rsqrt(_rmean(x * x) + EPS)
        h = (x * r) * vec_ref[1:2, :]
        h = h * (1.0 + vec_ref[2:3, :]) + vec_ref[3:4, :]
        h_ref[...] = h.astype(BF16)
        ht_ref[...] = h.T.astype(BF16)

    rs = _row_spec(tm, D)
    ins = [xp, f, vec] if has_f else [xp, vec]
    in_specs = ([rs, rs] if has_f else [rs]) + [_const_spec((8, D))]
    out_shape = ([_sds((S, D), F32)] if has_f else []) + [_sds((S, D), BF16), _sds((D, S), BF16)]
    out_specs = [rs] * (len(out_shape) - 1) + [pl.BlockSpec((D, tm), lambda i: (0, i))]
    outs = _pcall(_after(body, len(ins), deps), grid=(S // tm,), in_specs=in_specs + [ANY] * len(deps),
                  out_specs=out_specs, out_shape=out_shape, name=name,
                  compiler_params=_params(("parallel",)))(*ins, *deps)
    return (outs[0], outs[1], outs[2]) if has_f else (xp, outs[0], outs[1])


def _mm_resid_norm(a, w3, xprev, vec, tm, name, deps=()):
    S, K = a.shape
    D = w3.shape[2]

    def body(a_ref, w_ref, xp_ref, vec_ref, p_ref, xo_ref, h_ref, ht_ref):
        p = jnp.dot(a_ref[...], w_ref[...], preferred_element_type=F32)
        p_ref[...] = p
        x = xp_ref[...] + vec_ref[0:1, :] * p
        xo_ref[...] = x
        r = lax.rsqrt(_rmean(x * x) + EPS)
        h = (x * r) * vec_ref[1:2, :]
        h = h * (1.0 + vec_ref[2:3, :]) + vec_ref[3:4, :]
        h_ref[...] = h.astype(BF16)
        ht_ref[...] = h.T.astype(BF16)

    rs = _row_spec(tm, D)
    return _pcall(_after(body, 4, deps), grid=(S // tm,),
                  in_specs=[_row_spec(tm, K), pl.BlockSpec((None, K, D), lambda i: (0, 0, 0)), rs, _const_spec((8, D))]
                  + [ANY] * len(deps),
                  out_specs=[rs, rs, rs, pl.BlockSpec((D, tm), lambda i: (0, i))],
                  out_shape=[_sds((S, D), F32), _sds((S, D), F32), _sds((S, D), BF16), _sds((D, S), BF16)], name=name,
                  compiler_params=_params(("parallel",)))(a, w3, xprev, vec, *deps)


def _mixer_fwd(z, wsh, sgu_ln, wtril, bias_full, cw, cvec, name, deps=()):
    S = z.shape[0]
    D = wsh.shape[1]
    tm = CHUNK

    def body(z_ref, wsh_ref, sln_ref, wt_ref, bias_ref, cw_ref, cv_ref, oa_ref, ob_ref, oc_ref, t_ref,
             conv_ref, pe, ge, gr, cbuf):
        i = pl.program_id(0)

        @pl.when(i == 0)
        def _():
            pe[0:HALO, :] = jnp.zeros((HALO, D), F32)
            ge[0:HALO, :] = jnp.zeros((HALO, D), F32)

        def col(n):
            return z_ref[:, n * D:(n + 1) * D].astype(F32)

        pe[HALO:HALO + tm, :] = col(1) * col(2)
        q = wsh_ref[0:1, :] * pe[HALO - 2:HALO - 2 + tm, :]
        q = q + wsh_ref[1:2, :] * pe[HALO - 1:HALO - 1 + tm, :]
        q = q + wsh_ref[2:3, :] * pe[HALO:HALO + tm, :]
        act_a = col(0) * q
        oa_ref[...] = act_a.astype(BF16)
        t_ref[0] = act_a.T.astype(BF16)
        gu, _ = _gelu(col(3))
        gv, _ = _gelu(col(4))
        d = gv - _rmean(gv)
        nrm = d * lax.rsqrt(_rmean(d * d) + EPS)
        vnb = (nrm * sln_ref[0:1, :] + sln_ref[1:2, :]).astype(BF16)
        for g in range(NG):
            cs = slice(g * LANE, (g + 1) * LANE)
            mixed = jnp.dot(wt_ref[g], vnb[:, cs], preferred_element_type=F32) + bias_ref[:, cs]
            act_b = gu[:, cs] * mixed
            ob_ref[:, cs] = act_b.astype(BF16)
            t_ref[1, cs, :] = act_b.T.astype(BF16)
        ge[HALO:HALO + tm, :] = col(5) * _sigmoid(col(6))
        _fill_shifted(ge, gr)
        o0 = HALO - (CFM_K - 1)
        _causal_conv(cw_ref, range(CFM_K), cv_ref[0:1, :], ge, gr, range(o0, o0 + CFM_K), tm, cbuf)
        conv = cbuf[...]
        conv_ref[...] = conv.astype(BF16)
        d = conv - _rmean(conv)
        ln = (d * lax.rsqrt(_rmean(d * d) + EPS)) * cv_ref[1:2, :] + cv_ref[2:3, :]
        act_c = ln * _sigmoid(ln)
        oc_ref[...] = act_c.astype(BF16)
        t_ref[2] = act_c.T.astype(BF16)
        pe[0:HALO, :] = pe[tm:tm + HALO, :]
        ge[0:HALO, :] = ge[tm:tm + HALO, :]

    rs = _row_spec(tm, D)
    outs = _pcall(
        _after(body, 7, deps), grid=(S // tm,),
        in_specs=[pl.BlockSpec((tm, 7 * D), lambda i: (i, 0)), _const_spec((8, D)), _const_spec((8, D)),
                  _const_spec((NG, CHUNK, CHUNK)), _const_spec((CHUNK, D)), _const_spec((HALO, D)), _const_spec((8, D))]
        + [ANY] * len(deps),
        out_specs=[rs, rs, rs, pl.BlockSpec((3, D, tm), lambda i: (0, 0, i)), rs],
        out_shape=[_sds((S, D), BF16)] * 3 + [_sds((3, D, S), BF16), _sds((S, D), BF16)],
        scratch_shapes=[pltpu.VMEM((HALO + tm, D), F32), pltpu.VMEM((HALO + tm, D), F32),
                        pltpu.VMEM((7, HALO + tm, D), F32), pltpu.VMEM((tm, D), F32)],
        name=name, compiler_params=_params(("arbitrary",)))(z, wsh, sgu_ln, wtril, bias_full, cw, cvec, *deps)
    return outs[:3], outs[3], outs[4]


def _branch_out(acts, ws, z, name):
    S, D = acts[0].shape
    tm = min(512, S)

    def body(a0, a1, a2, w0, w1, w2, g0, g1, g2, m_ref, mt_ref, y_ref):
        m = None
        for n, (a, w, g) in enumerate(((a0, w0, g0), (a1, w1, g1), (a2, w2, g2))):
            y = jnp.dot(a[...], w[...], preferred_element_type=F32)
            y_ref[n] = y.astype(BF16)
            t = _sigmoid(g[...].astype(F32)) * y
            m = t if m is None else m + t
        m_ref[...] = m.astype(BF16)
        mt_ref[...] = m.T.astype(BF16)

    rs = _row_spec(tm, D)
    gate_specs = [pl.BlockSpec((tm, D), functools.partial(lambda i, n: (i, 7 + n), n=n)) for n in range(3)]
    return _pcall(body, grid=(S // tm,),
                  in_specs=[rs, rs, rs] + [_const_spec((D, D))] * 3 + gate_specs,
                  out_specs=[rs, pl.BlockSpec((D, tm), lambda i: (0, i)), pl.BlockSpec((3, tm, D), lambda i: (0, i, 0))],
                  out_shape=[_sds((S, D), BF16), _sds((D, S), BF16), _sds((3, S, D), BF16)], name=name,
                  compiler_params=_params(("parallel",)))(*acts, *ws, z, z, z)


def _ffn_in_swiglu(h2, w3, tm, tn, name):
    S, D = h2.shape
    F = w3.shape[2] // 2
    nj = F // tn

    def body(a_ref, wg_ref, wu_ref, gu_ref, act_ref, actt_ref):
        a = a_ref[...]
        g = jnp.dot(a, wg_ref[...], preferred_element_type=F32)
        u = jnp.dot(a, wu_ref[...], preferred_element_type=F32)
        gu_ref[0] = g.astype(BF16)
        gu_ref[1] = u.astype(BF16)
        act = (g * _sigmoid(g)) * u
        act_ref[...] = act.astype(BF16)
        actt_ref[...] = act.T.astype(BF16)

    return _pcall(body, grid=(S // tm, nj),
                  in_specs=[pl.BlockSpec((tm, D), lambda i, j: (i, 0)), pl.BlockSpec((None, D, tn), lambda i, j: (0, 0, j)),
                            pl.BlockSpec((None, D, tn), lambda i, j: (0, 0, j + nj))],
                  out_specs=[pl.BlockSpec((2, tm, tn), lambda i, j: (0, i, j)), pl.BlockSpec((tm, tn), lambda i, j: (i, j)),
                             pl.BlockSpec((tn, tm), lambda i, j: (j, i))],
                  out_shape=[_sds((2, S, F), BF16), _sds((S, F), BF16), _sds((F, S), BF16)], name=name,
                  compiler_params=_params(("parallel", "parallel")))(h2, w3, w3)


def _swiglu_bwd(df, w3, gu, name, deps=()):
    _, S, F = gu.shape
    F2 = 2 * F
    D = df.shape[1]
    tm = min(256, S)

    def body(df_ref, w_ref, g_ref, u_ref, o_ref):
        d = lax.dot_general(df_ref[...], w_ref[...], (((1,), (1,)), ((), ())), preferred_element_type=F32)
        g = g_ref[...].astype(F32)
        sg = _sigmoid(g)
        o_ref[:, 0:F] = (d * u_ref[...].astype(F32) * (sg * (1.0 + g * (1.0 - sg)))).astype(BF16)
        o_ref[:, F:2 * F] = (d * (g * sg)).astype(BF16)

    return _pcall(_after(body, 4, deps), grid=(S // tm,),
                  in_specs=[_row_spec(tm, D), pl.BlockSpec((None, F, D), lambda i: (0, 0, 0)),
                            pl.BlockSpec((None, tm, F), lambda i: (0, i, 0)), pl.BlockSpec((None, tm, F), lambda i: (1, i, 0))]
                  + [ANY] * len(deps),
                  out_specs=pl.BlockSpec((tm, F2), lambda i: (i, 0)), out_shape=_sds((S, F2), BF16), name=name,
                  compiler_params=_params(("parallel",)))(df, w3, gu, gu, *deps)


def _final_bwd(x1, act, w3, tgt, vec, name):
    S, D = x1.shape
    K = act.shape[1]
    tm = min(512, S)

    def body(x_ref, a_ref, w_ref, t_ref, vec_ref, dx_ref, df_ref, sums_ref, loss_ref):
        @pl.when(pl.program_id(0) == 0)
        def _():
            sums_ref[...] = jnp.zeros_like(sums_ref)
            loss_ref[...] = jnp.zeros_like(loss_ref)

        gate, fg = vec_ref[0:1, :], vec_ref[1:2, :]
        fv = jnp.dot(a_ref[...], w_ref[...], preferred_element_type=F32)
        x = x_ref[...] + gate * fv
        r = lax.rsqrt(_rmean(x * x) + EPS)
        xn = x * r
        diff = xn * fg - t_ref[...]
        per_tok = _rmean(diff * diff)
        loss_ref[...] += 0.5 * jnp.sum(per_tok, axis=0, keepdims=True)
        dy = diff * (1.0 / D)
        sums_ref[0:1, :] += _rsum(dy * xn)
        dxn = dy * fg
        dx = r * (dxn - xn * _rmean(dxn * xn))
        sums_ref[1:2, :] += _rsum(dx * fv)
        dx_ref[...] = dx
        df_ref[...] = (dx * gate).astype(BF16)

    rs = _row_spec(tm, D)
    return _pcall(body, grid=(S // tm,),
                  in_specs=[rs, _row_spec(tm, K), pl.BlockSpec((None, K, D), lambda i: (0, 0, 0)), rs, _const_spec((8, D))],
                  out_specs=[rs, rs, _const_spec((8, D)), _const_spec((8, LANE))],
                  out_shape=[_sds((S, D), F32), _sds((S, D), BF16), _sds((8, D), F32), _sds((8, LANE), F32)],
                  name=name, compiler_params=_params(("arbitrary",)))(x1, act, w3, tgt, vec)


def _norm_bwd(xin, dh, dxup, vec, fprev, name, deps=()):
    S, D = xin.shape
    has_prev = fprev is not None
    fused = isinstance(dh, tuple)
    tm = min(512, S)
    n_dh = 2 if fused else 1
    G, Nb = (dh[1].shape[0], dh[1].shape[2]) if fused else (1, 0)
    bps = 1 if G == 1 else 2
    nk = G // bps

    def body(*refs):
        x_ref, dh_refs, (up_ref, vec_ref) = refs[0], refs[1:1 + n_dh], refs[1 + n_dh:3 + n_dh]
        rest = refs[3 + n_dh:]
        if has_prev:
            fp_ref, dx_ref, dp_ref, sums_ref = rest[:4]
        else:
            dx_ref, sums_ref = rest[:2]
        k = pl.program_id(1)

        @pl.when((pl.program_id(0) == 0) & (k == 0))
        def _():
            sums_ref[...] = jnp.zeros_like(sums_ref)

        def finish(dhv):
            g, scale = vec_ref[0:1, :], vec_ref[1:2, :]
            x = x_ref[...]
            r = lax.rsqrt(_rmean(x * x) + EPS)
            xn = x * r
            sums_ref[0:1, :] += _rsum(dhv)
            sums_ref[1:2, :] += _rsum(dhv * (xn * g))
            dm = dhv * (1.0 + scale)
            sums_ref[2:3, :] += _rsum(dm * xn)
            dxn = dm * g
            dx = up_ref[...] + r * (dxn - xn * _rmean(dxn * xn))
            dx_ref[...] = dx
            if has_prev:
                sums_ref[3:4, :] += _rsum(dx * fp_ref[...])
                dp_ref[...] = (dx * vec_ref[2:3, :]).astype(BF16)

        if not fused:
            finish(dh_refs[0][...])
        elif nk == 1:
            finish(lax.dot_general(dh_refs[0][...], dh_refs[1][...], (((1,), (1,)), ((), ())), preferred_element_type=F32))
        else:
            a_ref, b_ref, acc = dh_refs[0], dh_refs[1], rest[-1]
            part = None
            for j in range(bps):
                t = lax.dot_general(a_ref[:, j * Nb:(j + 1) * Nb], b_ref[j], (((1,), (1,)), ((), ())),
                                    preferred_element_type=F32)
                part = t if part is None else part + t

            @pl.when(k == 0)
            def _():
                acc[...] = part

            @pl.when(k > 0)
            def _():
                acc[...] += part

            @pl.when(k == nk - 1)
            def _():
                finish(acc[...])

    rs = pl.BlockSpec((tm, D), lambda i, k: (i, 0))
    vs = pl.BlockSpec((8, D), lambda i, k: (0, 0))
    if not fused:
        dh_ins, dh_specs = [dh], [rs]
    elif nk == 1:
        dh_ins, dh_specs = list(dh), [pl.BlockSpec((tm, Nb), lambda i, k: (i, 0)),
                                      pl.BlockSpec((None, D, Nb), lambda i, k: (0, 0, 0), pipeline_mode=pl.Buffered(1))]
    else:
        dh_ins, dh_specs = list(dh), [pl.BlockSpec((tm, bps * Nb), lambda i, k: (i, k)),
                                      pl.BlockSpec((bps, D, Nb), lambda i, k: (k, 0, 0))]
    ins = [xin, *dh_ins, dxup, vec] + ([fprev] if has_prev else [])
    in_specs = [rs, *dh_specs, rs, vs] + ([rs] if has_prev else [])
    out_shape = [_sds((S, D), F32)] + ([_sds((S, D), BF16)] if has_prev else []) + [_sds((8, D), F32)]
    out_specs = [rs] + ([rs] if has_prev else []) + [vs]
    outs = _pcall(_after(body, len(ins), deps), grid=(S // tm, nk), in_specs=in_specs + [ANY] * len(deps),
                  out_specs=out_specs, out_shape=out_shape, name=name,
                  scratch_shapes=[pltpu.VMEM((tm, D), F32)] if nk > 1 else [],
                  compiler_params=_params(("arbitrary", "arbitrary")))(*ins, *deps)
    return (outs[0], outs[1], outs[2]) if has_prev else (outs[0], None, outs[1])


def _gate_bwd(dmerged, z, ys, name, deps=()):
    S, D = dmerged.shape
    tm = min(512, S)
    ncol = z.shape[1] // D

    def body(dm_ref, g_ref, y_ref, dy_ref, dz_ref):
        sg = _sigmoid(g_ref[...].astype(F32))
        dm = dm_ref[...].astype(F32)
        dy_ref[...] = (dm * sg).astype(BF16)
        dz_ref[...] = (dm * y_ref[...].astype(F32) * (sg * (1.0 - sg))).astype(BF16)

    branch = pl.BlockSpec((None, tm, D), lambda i, n: (n, i, 0))
    return _pcall(_after(body, 3, deps), grid=(S // tm, 3),
                  in_specs=[pl.BlockSpec((tm, D), lambda i, n: (i, 0)), pl.BlockSpec((tm, D), lambda i, n: (i, 7 + n)),
                            branch] + [ANY] * len(deps),
                  out_specs=[branch, pl.BlockSpec((tm, D), lambda i, n: (i, 7 + n))],
                  out_shape=[_sds((3, S, D), BF16), _sds((S, ncol * D), BF16)], name=name,
                  compiler_params=_params(("parallel", "arbitrary")))(dmerged, z, ys, *deps)


def _mixer_bwd(z, dacts, conv, dz, wsh, sgu_ln, wtril, wtril_t, bias_full, cw, cvec, name):
    S = z.shape[0]
    D = wsh.shape[1]
    tm = CHUNK
    nt = S // tm
    hb = tm // HALO

    def body(zc, zp, da_ref, db_ref, dc_ref, conv_ref, wsh_ref, sln_ref, wt_ref, wtt_ref, bias_ref, cw_ref, cv_ref, _dz_in,
             dz_ref, vec_ref, dcw_ref, dws_ref, dbs_ref, pe, ge, dqe, dce, gr, dcr, cbuf, dcw8):
        i = pl.program_id(0)
        rb = nt - 1 - i

        @pl.when(i == 0)
        def _():
            vec_ref[...] = jnp.zeros_like(vec_ref)
            dcw8[...] = jnp.zeros_like(dcw8)
            dws_ref[...] = jnp.zeros_like(dws_ref)
            dbs_ref[...] = jnp.zeros_like(dbs_ref)
            dqe[tm:tm + HALO, :] = jnp.zeros((HALO, D), F32)
            dce[tm:tm + HALO, :] = jnp.zeros((HALO, D), F32)

        keep = (rb > 0).astype(F32)

        def col(n):
            return zc[:, n * D:(n + 1) * D].astype(F32)

        def pcol(n):
            return zp[:, n * D:(n + 1) * D].astype(F32)

        c_a, x_a = col(1), col(2)
        pe[0:HALO, :] = keep * (pcol(1) * pcol(2))
        pe[HALO:HALO + tm, :] = c_a * x_a
        q = wsh_ref[0:1, :] * pe[HALO - 2:HALO - 2 + tm, :]
        q = q + wsh_ref[1:2, :] * pe[HALO - 1:HALO - 1 + tm, :]
        q = q + wsh_ref[2:3, :] * pe[HALO:HALO + tm, :]
        dact = da_ref[...].astype(F32)
        dz_ref[:, 0:D] = (dact * q).astype(BF16)
        dq = dact * col(0)
        dqe[0:tm, :] = dq
        dp = wsh_ref[2:3, :] * dq + wsh_ref[1:2, :] * dqe[1:1 + tm, :] + wsh_ref[0:1, :] * dqe[2:2 + tm, :]
        dz_ref[:, D:2 * D] = (dp * x_a).astype(BF16)
        dz_ref[:, 2 * D:3 * D] = (dp * c_a).astype(BF16)
        for k in range(SHORT_K):
            o = HALO - (SHORT_K - 1) + k
            vec_ref[k:k + 1, :] += _rsum(dq * pe[o:o + tm, :])
        u, v = col(3), col(4)
        gu, tu = _gelu(u)
        gv, tv = _gelu(v)
        d = gv - _rmean(gv)
        rstd = lax.rsqrt(_rmean(d * d) + EPS)
        nrm = d * rstd
        vnb = (nrm * sln_ref[0:1, :] + sln_ref[1:2, :]).astype(BF16)
        dact = db_ref[...].astype(F32)
        dvn_parts, dgu_parts = [], []
        for g in range(NG):
            cs = slice(g * LANE, (g + 1) * LANE)
            vg = vnb[:, cs]
            mixed = jnp.dot(wt_ref[g], vg, preferred_element_type=F32) + bias_ref[:, cs]
            dgu_parts.append(dact[:, cs] * mixed)
            dmixed = dact[:, cs] * gu[:, cs]
            dmb = dmixed.astype(BF16)
            dws_ref[g] += lax.dot_general(dmb, vg, (((1,), (1,)), ((), ())), preferred_element_type=F32)
            dbs_ref[g] += jnp.broadcast_to(jnp.sum(dmixed, axis=1, keepdims=True), (CHUNK, LANE))
            dvn_parts.append(jnp.dot(wtt_ref[g], dmb, preferred_element_type=F32))
        dgu = jnp.concatenate(dgu_parts, axis=1)
        dvn = jnp.concatenate(dvn_parts, axis=1)
        dz_ref[:, 3 * D:4 * D] = (dgu * _dgelu(u, tu)).astype(BF16)
        vec_ref[3:4, :] += _rsum(dvn * nrm)
        vec_ref[4:5, :] += _rsum(dvn)
        dn = dvn * sln_ref[0:1, :]
        dgv = rstd * (dn - _rmean(dn) - nrm * _rmean(dn * nrm))
        dz_ref[:, 4 * D:5 * D] = (dgv * _dgelu(v, tv)).astype(BF16)
        a_c = col(5)
        sg = _sigmoid(col(6))
        ge[0:HALO, :] = keep * (pcol(5) * _sigmoid(pcol(6)))
        ge[HALO:HALO + tm, :] = a_c * sg
        _fill_shifted(ge, gr)
        o0 = HALO - (CFM_K - 1)
        conv = conv_ref[...].astype(F32)
        d = conv - _rmean(conv)
        rstd = lax.rsqrt(_rmean(d * d) + EPS)
        nrm = d * rstd
        ln = nrm * cv_ref[1:2, :] + cv_ref[2:3, :]
        sl = _sigmoid(ln)
        dln = dc_ref[...].astype(F32) * (sl * (1.0 + ln * (1.0 - sl)))
        vec_ref[6:7, :] += _rsum(dln * nrm)
        vec_ref[7:8, :] += _rsum(dln)
        dn = dln * cv_ref[1:2, :]
        dconv = rstd * (dn - _rmean(dn) - nrm * _rmean(dn * nrm))
        vec_ref[5:6, :] += _rsum(dconv)
        dce[0:tm, :] = dconv
        _fill_shifted(dce, dcr)
        _causal_conv(cw_ref, range(CFM_K), None, dce, dcr, [CFM_K - 1 - k for k in range(CFM_K)], tm, cbuf)
        dglu = cbuf[...]
        for cb in range(D // LANE):
            cs = slice(cb * LANE, (cb + 1) * LANE)
            dcv = dce[0:tm, cs]
            for k in range(CFM_K):
                prod = dcv * _rows_at(ge, gr, o0 + k, tm, cs)
                dcw8[k, :, cs] += jnp.sum(prod.reshape(tm // 8, 8, LANE), axis=0)

        @pl.when(i == nt - 1)
        def _():
            dcw_ref[...] = jnp.sum(dcw8[...], axis=1)
        dz_ref[:, 5 * D:6 * D] = (dglu * sg).astype(BF16)
        dz_ref[:, 6 * D:7 * D] = (dglu * a_c * (sg * (1.0 - sg))).astype(BF16)
        dqe[tm:tm + HALO, :] = dqe[0:HALO, :]
        dce[tm:tm + HALO, :] = dce[0:HALO, :]

    rev = lambda i: (nt - 1 - i, 0)
    rs = pl.BlockSpec((tm, D), rev)
    cur = pl.BlockSpec((tm, 7 * D), rev)
    prev = pl.BlockSpec((HALO, 7 * D), lambda i: (jnp.maximum((nt - 1 - i) * hb - 1, 0), 0))
    ext = pltpu.VMEM((HALO + tm, D), F32)
    outs = _pcall(
        body, grid=(nt,),
        in_specs=[cur, prev] + [pl.BlockSpec((None, tm, D), functools.partial(lambda i, n: (n, nt - 1 - i, 0), n=n))
                                for n in range(3)]
        + [rs, _const_spec((8, D)), _const_spec((8, D)), _const_spec((NG, CHUNK, CHUNK)),
                  _const_spec((NG, CHUNK, CHUNK)), _const_spec((CHUNK, D)), _const_spec((HALO, D)), _const_spec((8, D)),
                  ANY],
        out_specs=[cur, _const_spec((8, D)), _const_spec((HALO, D)), _const_spec((NG, CHUNK, CHUNK)),
                   _const_spec((NG, CHUNK, LANE))],
        out_shape=[_sds(dz.shape, BF16), _sds((8, D), F32), _sds((HALO, D), F32), _sds((NG, CHUNK, CHUNK), F32),
                   _sds((NG, CHUNK, LANE), F32)],
        scratch_shapes=[ext, ext, ext, ext, pltpu.VMEM((7, HALO + tm, D), F32), pltpu.VMEM((7, HALO + tm, D), F32),
                        pltpu.VMEM((tm, D), F32), pltpu.VMEM((HALO, 8, D), F32)],
        input_output_aliases={13: 0}, name=name,
        compiler_params=_params(("arbitrary",)))(z, z, dacts, dacts, dacts, conv, wsh, sgu_ln, wtril, wtril_t, bias_full, cw,
                                                 cvec, dz)
    return outs


def _ada_fwd(c_all, w_ada_loc, name):
    nb, D = c_all.shape
    L, _, nc = w_ada_loc.shape

    def body(c_ref, w_ref, o_ref, ca_ref):
        cv = c_ref[...]
        ca = cv * _sigmoid(cv)
        ca_ref[...] = ca
        o_ref[...] = jnp.dot(ca.astype(BF16), w_ref[...].astype(BF16), preferred_element_type=F32)

    return _pcall(body, grid=(L,),
                  in_specs=[_const_spec((nb, D)), pl.BlockSpec((None, D, nc), lambda l: (l, 0, 0))],
                  out_specs=[pl.BlockSpec((None, nb, nc), lambda l: (l, 0, 0)), _const_spec((nb, D))],
                  out_shape=[_sds((L, nb, nc), F32), _sds((nb, D), F32)], name=name,
                  compiler_params=_params(("arbitrary",)))(c_all, w_ada_loc)


def _adamw(w, g, m, v):
    m = ADAM_B1 * m + (1.0 - ADAM_B1) * g
    v = ADAM_B2 * v + (1.0 - ADAM_B2) * (g * g)
    m_hat = m / (1.0 - ADAM_B1 ** ADAM_STEP)
    v_hat = v / (1.0 - ADAM_B2 ** ADAM_STEP)
    delta = -ADAM_LR * (m_hat / (jnp.sqrt(v_hat) + ADAM_EPS) + ADAM_WD * w)
    return delta, m, v


def _tile_rows(R, C, align=8):
    cap = max(align, (1536 * 1024) // (4 * C))
    best = None
    for t in range(align, R + 1, align):
        if R % t == 0 and t <= cap:
            best = t
    return R if best is None else best


def _adam_ada(ct, dm, w, m, v, name):
    L, D, nc = w.shape
    nb = ct.shape[1]
    tr = _tile_rows(D, nc)

    def body(ct_ref, dm_ref, w_ref, m_ref, v_ref, g_ref, d_ref, mo_ref, vo_ref):
        g = ct_ref[:, 0:1] * dm_ref[0:1, :]
        for b in range(1, nb):
            g = g + ct_ref[:, b:b + 1] * dm_ref[b:b + 1, :]
        g_ref[...] = g
        d_ref[...], mo_ref[...], vo_ref[...] = _adamw(w_ref[...], g, m_ref[...], v_ref[...])

    ws = pl.BlockSpec((None, tr, nc), lambda l, r: (l, r, 0))
    return _pcall(body, grid=(L, D // tr),
                  in_specs=[pl.BlockSpec((tr, nb), lambda l, r: (r, 0)), pl.BlockSpec((None, nb, nc), lambda l, r: (l, 0, 0)),
                            ws, ws, ws],
                  out_specs=[ws] * 4, out_shape=[_sds(w.shape, F32)] * 4, name=name,
                  compiler_params=_params(("parallel", "parallel")))(ct, dm, w, m, v)


def _adam_small(parts, w, m, v, name, deps=(), single_rows=()):
    n, R, C = parts.shape
    tr = _tile_rows(R, C * n // 2)
    nl = len(single_rows[0]) if single_rows else 0

    def body(p_ref, w_ref, m_ref, v_ref, g_ref, d_ref, mo_ref, vo_ref, *single):
        g = p_ref[0]
        for j in range(1, n):
            g = g + p_ref[j]
        d, mo, vo = _adamw(w_ref[...], g, m_ref[...], v_ref[...])
        g_ref[...], d_ref[...], mo_ref[...], vo_ref[...] = g, d, mo, vo
        step = pl.program_id(0)
        for pi, rows in enumerate(single_rows):
            for l, row in enumerate(rows):
                @pl.when(step == row // tr)
                def _(pi=pi, l=l, off=row % tr):
                    for k, val in enumerate((g, d, mo, vo)):
                        single[4 * pi + k][l:l + 1, :] = val[off:off + 1, :]

    ws = pl.BlockSpec((tr, C), lambda r: (r, 0))
    one = pl.BlockSpec((nl, C), lambda r: (0, 0))
    outs = _pcall(_after(body, 4, deps), grid=(R // tr,),
                  in_specs=[pl.BlockSpec((n, tr, C), lambda r: (0, r, 0)), ws, ws, ws] + [ANY] * len(deps),
                  out_specs=[ws] * 4 + [one] * (4 * len(single_rows)),
                  out_shape=[_sds((R, C), F32)] * 4 + [_sds((nl, C), F32)] * (4 * len(single_rows)), name=name,
                  compiler_params=_params(("arbitrary",)))(parts, w, m, v, *deps)
    return outs[:4], outs[4:]


def _adam_plain(g, w, m, v, name):
    R, C = w.shape

    def body(g_ref, w_ref, m_ref, v_ref, d_ref, mo_ref, vo_ref):
        d_ref[...], mo_ref[...], vo_ref[...] = _adamw(w_ref[...], g_ref[...], m_ref[...], v_ref[...])

    ws = _const_spec((R, C))
    return _pcall(body, grid=(1,), in_specs=[ws] * 4, out_specs=[ws] * 3, out_shape=[_sds((R, C), F32)] * 3, name=name,
                  compiler_params=_params(("arbitrary",)))(g, w, m, v)


def _pair_sum(G, R1, my_c, name):
    n, R, C = G.shape
    half = n // 2
    tr = _tile_rows(R, C, align=16)

    def body(c_ref, g_ref, r_ref, o_ref):
        o_ref[...] = (g_ref[...].astype(F32) + r_ref[...].astype(F32)).astype(o_ref.dtype)

    blk = (None, tr, C)
    gs = pltpu.PrefetchScalarGridSpec(
        num_scalar_prefetch=1, grid=(half, R // tr),
        in_specs=[pl.BlockSpec(blk, lambda p, r, c: (2 * p + c[0], r, 0)), pl.BlockSpec(blk, lambda p, r, c: (p, r, 0))],
        out_specs=pl.BlockSpec(blk, lambda p, r, c: (p, r, 0)))
    return _pcall(body, grid_spec=gs, out_shape=_sds((half, R, C), G.dtype), name=name,
                  compiler_params=_params(("parallel", "parallel")))(my_c, G, R1)


def _adam_big(P, R2, my_chip, w, m, v, layer, prev, name, deps=()):
    _, R, C = P.shape
    nrecv = R2.shape[0]
    tr = _tile_rows(R, C, align=16)

    def body(p_sm, p_ref, r_ref, w_ref, m_ref, v_ref, *rest):
        g_ref, d_ref, mo_ref, vo_ref = rest[-4:]
        g = p_ref[...].astype(F32)
        for k in range(nrecv):
            g = g + r_ref[k].astype(F32)
        g_ref[...] = g
        d_ref[...], mo_ref[...], vo_ref[...] = _adamw(w_ref[...], g, m_ref[...], v_ref[...])

    ws = pl.BlockSpec((None, tr, C), lambda r, p: (layer, r, 0))
    held = [] if prev is None else list(prev)
    gs = pltpu.PrefetchScalarGridSpec(
        num_scalar_prefetch=1, grid=(R // tr,),
        in_specs=[pl.BlockSpec((None, tr, C), lambda r, p: (p[0], r, 0)),
                  pl.BlockSpec((nrecv, tr, C), lambda r, p: (0, r, 0)), ws, ws, ws] + [ANY] * (len(held) + len(deps)),
        out_specs=[ws] * 4)
    alias = {6 + i: i for i in range(len(held))}
    return _pcall(body, grid_spec=gs, out_shape=[_sds(w.shape, F32)] * 4, name=name, input_output_aliases=alias,
                  compiler_params=_params(("parallel",)))(my_chip, P, R2, w, m, v, *held, *deps)


def _place():
    return lax.axis_index("x"), lax.axis_index("y"), lax.axis_index("c")


def _all_gather(shards, name, deps=(), into=None):
    n = len(shards)
    bufs = [] if into is None else [b for b, _ in into]
    nb = len(bufs)

    def body(*refs):
        ins, outs = refs[:n], refs[n + nb:2 * n + nb]
        send_sems, recv_sems, local_sems = refs[2 * n + nb:]
        x, y, c = _place()
        me, sibling = (x, y, c), (x, y, 1 - c)
        chips = [(1 - x, y), (x, 1 - y), (1 - x, 1 - y)]

        def slot(a, px, py, pc):
            block = outs[a].at[4 * px + 2 * py + pc]
            return block if into is None else block.at[pl.ds(into[a][1], ins[a].shape[0])]

        def copy(a, k, block, to, src=None):
            return pltpu.make_async_remote_copy(
                src_ref=slot(a, *block) if src is None else src, dst_ref=slot(a, *block),
                send_sem=send_sems.at[7 * a + k], recv_sem=recv_sems.at[7 * a + k], device_id=to, device_id_type=MESH)

        mine = [pltpu.make_async_copy(ins[a], slot(a, *me), local_sems.at[a]) for a in range(n)]
        for cp in mine:
            cp.start()
        first = []
        for a in range(n):
            first.append(copy(a, 0, me, sibling, src=ins[a]))
            first += [copy(a, 1 + j, me, (*chip, c), src=ins[a]) for j, chip in enumerate(chips)]
        for cp in first:
            cp.start()
        passed = []
        for j, chip in enumerate(chips):
            for a in range(n):
                copy(a, 1 + j, (*chip, c), me).wait_recv()
                fwd = copy(a, 4 + j, (*chip, c), sibling)
                fwd.start()
                passed.append(fwd)
        for a in range(n):
            copy(a, 0, sibling, me).wait_recv()
        for j, chip in enumerate(chips):
            for a in range(n):
                copy(a, 4 + j, (*chip, 1 - c), me).wait_recv()
        for cp in first + passed:
            cp.wait_send()
        for cp in mine:
            cp.wait()

    out_shape = [_sds((NDEV,) + s.shape, s.dtype) for s in shards] if into is None else [_sds(b.shape, b.dtype) for b in bufs]
    outs = _pcall(_after(body, n + nb, deps), in_specs=[ANY] * (n + nb + len(deps)), out_specs=[ANY] * n,
                  out_shape=out_shape, input_output_aliases={n + a: a for a in range(nb)},
                  scratch_shapes=[pltpu.SemaphoreType.DMA((7 * n,)), pltpu.SemaphoreType.DMA((7 * n,)),
                                  pltpu.SemaphoreType.DMA((n,))], name=name)(*shards, *bufs, *deps)
    return list(outs)


HBM = pl.BlockSpec(memory_space=pltpu.HBM)
SEM = pl.BlockSpec(memory_space=pltpu.SEMAPHORE)


def _copies(plan, refs, send_sems, recv_sems):
    return [pltpu.make_async_remote_copy(src_ref=s, dst_ref=d, send_sem=send_sems.at[k], recv_sem=recv_sems.at[k],
                                         device_id=dev, device_id_type=MESH)
            for k, (s, d, dev) in enumerate(plan(refs, *_place()))]


def _xfer_start(bufs, ncopies, plan, name, deps=()):
    n = len(bufs)

    def body(*refs):
        for cp in _copies(plan, refs[:n], refs[n], refs[n + 1]):
            cp.start()
        token = refs[2 * n + 2]
        token[...] = jnp.zeros_like(token)

    outs = _pcall(
        _after(body, n, deps), name=name,
        out_shape=(pltpu.SemaphoreType.DMA((ncopies,)), pltpu.SemaphoreType.DMA((ncopies,)),
                   *[pltpu.HBM(b.shape, b.dtype) for b in bufs], _sds((8, LANE), F32)),
        in_specs=[HBM] * n + [ANY] * len(deps),
        out_specs=(SEM, SEM, *[HBM] * n, pl.BlockSpec(memory_space=pltpu.VMEM)),
        input_output_aliases={i: 2 + i for i in range(n)},
        compiler_params=pltpu.CompilerParams(has_side_effects=pltpu.SideEffectType.DATAFLOW_SIDE_EFFECTING),
    )(*[pltpu.with_memory_space_constraint(b, pltpu.HBM) for b in bufs], *deps)
    return (outs[0], outs[1]), list(outs[2:2 + n]), outs[2 + n]


def _xfer_wait(sems, bufs, plan, after, name):
    n = len(bufs)
    after = list(after) if isinstance(after, (list, tuple)) else [after]

    def body(*refs):
        for cp in _copies(plan, refs[:n], refs[n], refs[n + 1]):
            cp.wait_send()
            cp.wait_recv()

    outs = _pcall(
        body, name=name, out_shape=tuple(pltpu.HBM(b.shape, b.dtype) for b in bufs),
        in_specs=[HBM] * n + [SEM, SEM] + [ANY] * len(after), out_specs=tuple([HBM] * n),
        input_output_aliases={i: i for i in range(n)},
        compiler_params=pltpu.CompilerParams(has_side_effects=pltpu.SideEffectType.DATAFLOW_SIDE_EFFECTING),
    )(*bufs, *sems, *after)
    return list(outs)


def _chips_of(x, y):
    return [(1 - x, y), (x, 1 - y), (1 - x, 1 - y)]


def _landing(ref, dev_index, rows):
    block = ref.at[dev_index]
    return block if rows is None else block.at[pl.ds(rows[0], rows[1])]


def _gather_plan1(n, rows=None):
    def plan(refs, x, y, c):
        out = []
        for a in range(n):
            blk = _landing(refs[a], 4 * x + 2 * y + c, rows)
            out.append((blk, blk, (x, y, 1 - c)))
            out += [(blk, blk, (px, py, c)) for px, py in _chips_of(x, y)]
        return out
    return plan


def _gather_plan2(n, rows=None):
    def plan(refs, x, y, c):
        out = []
        for a in range(n):
            for px, py in _chips_of(x, y):
                blk = _landing(refs[a], 4 * px + 2 * py + c, rows)
                out.append((blk, blk, (x, y, 1 - c)))
        return out
    return plan


def _gather_start(shards, dev, name, deps=(), within=None):
    rows = None if within is None else (within[0], shards[0].shape[0])
    lands = []
    for s in shards:
        shape = (NDEV,) + s.shape if within is None else (NDEV, within[1]) + s.shape[1:]
        start = (dev,) + (0,) * s.ndim if within is None else (dev, within[0]) + (0,) * (s.ndim - 1)
        lands.append(lax.dynamic_update_slice(lax.empty(shape, s.dtype), s[None], start))
    n = len(shards)
    sems, lands, tok = _xfer_start(lands, 4 * n, _gather_plan1(n, rows), name + "_p1_start", deps)
    return dict(sems=sems, lands=lands, tok=tok, n=n, rows=rows)


def _gather_mid(st, after, name):
    n, rows = st["n"], st["rows"]
    lands = _xfer_wait(st["sems"], st["lands"], _gather_plan1(n, rows), after, name + "_p1_wait")
    sems, lands, tok = _xfer_start(lands, 3 * n, _gather_plan2(n, rows), name + "_p2_start")
    return dict(sems=sems, lands=lands, tok=tok, n=n, rows=rows)


def _gather_finish(st, after, name):
    return _xfer_wait(st["sems"], st["lands"], _gather_plan2(st["n"], st["rows"]), after, name + "_p2_wait")


def _scatter_plan1(n):
    def plan(refs, x, y, c):
        return [(refs[a].at[2 * p + 1 - c], refs[n + a].at[p], (x, y, 1 - c)) for a in range(n) for p in range(NCHIP)]
    return plan


def _scatter_plan2(n):
    def plan(refs, x, y, c):
        return [(refs[a].at[2 * px + py], refs[n + a].at[j], (px, py, c))
                for a in range(n) for j, (px, py) in enumerate(_chips_of(x, y))]
    return plan


def _scatter_start(Gs, name):
    n = len(Gs)
    R1s = [lax.empty((NCHIP,) + g.shape[1:], g.dtype) for g in Gs]
    sems, bufs, tok = _xfer_start(list(Gs) + R1s, NCHIP * n, _scatter_plan1(n), name + "_s1_start")
    return dict(sems=sems, bufs=bufs, tok=tok, n=n)


def _scatter_mid(st, after, my_c, name):
    n = st["n"]
    bufs = _xfer_wait(st["sems"], st["bufs"], _scatter_plan1(n), after, name + "_s1_wait")
    Ps = [_pair_sum(bufs[a], bufs[n + a], my_c, f"{name}_pair_sum{a}") for a in range(n)]
    R2s = [lax.empty((3,) + p.shape[1:], p.dtype) for p in Ps]
    sems, bufs, tok = _xfer_start(Ps + R2s, 3 * n, _scatter_plan2(n), name + "_s2_start")
    return dict(sems=sems, bufs=bufs, tok=tok, n=n)


def _scatter_finish(st, after, name):
    n = st["n"]
    bufs = _xfer_wait(st["sems"], st["bufs"], _scatter_plan2(n), after, name + "_s2_wait")
    return bufs[:n], bufs[n:]


SMALL_ROWS = {"norm1_g": (0, 1), "norm2_g": (1, 1), "sgu_ln_g": (2, 1), "sgu_ln_b": (3, 1), "cfm_conv_b": (4, 1),
              "cfm_ln_g": (5, 1), "cfm_ln_b": (6, 1), "b_sgu": (7, 1), "w_sgu": (8, 128), "b_ada": (136, N_MOD),
              "w_short": (142, SHORT_K), "cfm_conv_w": (145, CFM_K)}
ROWS_PER_LAYER = 176
FINAL_ROW = DEPTH * ROWS_PER_LAYER
PACK_ROWS = 360


def _pack(get, D, layers=tuple(range(DEPTH)), tail=True):
    parts = []
    for l in layers:
        for name, (_, nrows) in SMALL_ROWS.items():
            a = get(name, l)
            parts.append(jnp.zeros((nrows * D,), F32) if a is None else a.astype(F32).reshape(nrows * D))
    if tail:
        for name in ("final_g", "loss"):
            a = get(name, None)
            parts.append(jnp.zeros((D,), F32) if a is None else a.astype(F32).reshape(D))
        parts.append(jnp.zeros(((PACK_ROWS - FINAL_ROW - 2) * D,), F32))
    return jnp.concatenate(parts).reshape(-1, D)


def _unpack(pack, name, shape):
    D = pack.shape[1]
    r0, nrows = SMALL_ROWS[name]
    return jnp.stack([pack[l * ROWS_PER_LAYER + r0:l * ROWS_PER_LAYER + r0 + nrows] for l in range(DEPTH)]).reshape(shape)


def _mm_tiles(S):
    return min(512, S), min(1024, S), min(2048, S)


def kernel(x, c, w_ada, b_ada, norm1_g, w_in, w_short, w_a_out, sgu_ln_g, sgu_ln_b, w_sgu, b_sgu, w_b_out, cfm_conv_w, cfm_conv_b, cfm_ln_g, cfm_ln_b, w_c_out, w_o, norm2_g, w_ffn_in, w_ffn_out, final_g, loss_target, m_w_ada, m_b_ada, m_norm1_g, m_w_in, m_w_short, m_w_a_out, m_sgu_ln_g, m_sgu_ln_b, m_w_sgu, m_b_sgu, m_w_b_out, m_cfm_conv_w, m_cfm_conv_b, m_cfm_ln_g, m_cfm_ln_b, m_w_c_out, m_w_o, m_norm2_g, m_w_ffn_in, m_w_ffn_out, m_final_g, v_w_ada, v_b_ada, v_norm1_g, v_w_in, v_w_short, v_w_a_out, v_sgu_ln_g, v_sgu_ln_b, v_w_sgu, v_b_sgu, v_w_b_out, v_cfm_conv_w, v_cfm_conv_b, v_cfm_ln_g, v_cfm_ln_b, v_w_c_out, v_w_o, v_norm2_g, v_w_ffn_in, v_w_ffn_out, v_final_g):
    W = dict(w_ada=w_ada, b_ada=b_ada, norm1_g=norm1_g, w_in=w_in, w_short=w_short, w_a_out=w_a_out, sgu_ln_g=sgu_ln_g,
             sgu_ln_b=sgu_ln_b, w_sgu=w_sgu, b_sgu=b_sgu, w_b_out=w_b_out, cfm_conv_w=cfm_conv_w, cfm_conv_b=cfm_conv_b,
             cfm_ln_g=cfm_ln_g, cfm_ln_b=cfm_ln_b, w_c_out=w_c_out, w_o=w_o, norm2_g=norm2_g, w_ffn_in=w_ffn_in,
             w_ffn_out=w_ffn_out, final_g=final_g)
    Mo = dict(w_ada=m_w_ada, b_ada=m_b_ada, norm1_g=m_norm1_g, w_in=m_w_in, w_short=m_w_short, w_a_out=m_w_a_out,
              sgu_ln_g=m_sgu_ln_g, sgu_ln_b=m_sgu_ln_b, w_sgu=m_w_sgu, b_sgu=m_b_sgu, w_b_out=m_w_b_out,
              cfm_conv_w=m_cfm_conv_w, cfm_conv_b=m_cfm_conv_b, cfm_ln_g=m_cfm_ln_g, cfm_ln_b=m_cfm_ln_b,
              w_c_out=m_w_c_out, w_o=m_w_o, norm2_g=m_norm2_g, w_ffn_in=m_w_ffn_in, w_ffn_out=m_w_ffn_out,
              final_g=m_final_g)
    Vo = dict(w_ada=v_w_ada, b_ada=v_b_ada, norm1_g=v_norm1_g, w_in=v_w_in, w_short=v_w_short, w_a_out=v_w_a_out,
              sgu_ln_g=v_sgu_ln_g, sgu_ln_b=v_sgu_ln_b, w_sgu=v_w_sgu, b_sgu=v_b_sgu, w_b_out=v_w_b_out,
              cfm_conv_w=v_cfm_conv_w, cfm_conv_b=v_cfm_conv_b, cfm_ln_g=v_cfm_ln_g, cfm_ln_b=v_cfm_ln_b,
              w_c_out=v_w_c_out, w_o=v_w_o, norm2_g=v_norm2_g, w_ffn_in=v_w_ffn_in, w_ffn_out=v_w_ffn_out,
              final_g=v_final_g)
    order = ["w_ada", "b_ada", "norm1_g", "w_in", "w_short", "w_a_out", "sgu_ln_g", "sgu_ln_b", "w_sgu", "b_sgu",
             "w_b_out", "cfm_conv_w", "cfm_conv_b", "cfm_ln_g", "cfm_ln_b", "w_c_out", "w_o", "norm2_g", "w_ffn_in",
             "w_ffn_out", "final_g"]

    assert DEPTH == 2, "the weight-gather schedule below is written for two layers"
    S, D = x.shape[1], x.shape[2]
    F2 = w_ffn_in.shape[2] * NDEV
    FF = F2 // 2
    xi, yi, ci = _place()
    dev = 4 * xi + 2 * yi + ci
    my_c = jnp.reshape(ci, (1,)).astype(jnp.int32)
    my_chip = jnp.reshape(2 * xi + yi, (1,)).astype(jnp.int32)
    tm, tm_big, tm_huge = _mm_tiles(S)
    x0 = x.reshape(S, D)
    tgt = loss_target.reshape(S, D)

    def shards_of(l):
        return [w_in[l].astype(BF16), w_a_out[l].astype(BF16), w_b_out[l].astype(BF16), w_c_out[l].astype(BF16),
                w_o[l].astype(BF16), w_ffn_in[l].astype(BF16), w_ffn_out[l].astype(BF16)]

    c_all = _all_gather([jnp.pad(c, ((0, 7), (0, 0)))], "ag_c")[0][:, 0, :]
    modpart, c_act = _ada_fwd(c_all, w_ada, "ada_fwd")
    ncol = modpart.shape[2]
    mg = _all_gather([modpart.reshape(DEPTH * NDEV, ncol)], "ag_mod")[0].reshape(NDEV, DEPTH, NDEV, ncol)
    mine = lax.dynamic_index_in_dim(mg, dev, axis=2, keepdims=False)
    mod = (jnp.transpose(mine, (1, 0, 2)).reshape(DEPTH, N_MOD * D) + b_ada).reshape(DEPTH, N_MOD, D)

    ncs = w_short.shape[2]
    ag_in0 = _gather_start([w_in[0].astype(BF16), w_short.reshape(DEPTH * SHORT_K, ncs),
                            cfm_conv_w.reshape(DEPTH * CFM_K, ncs)], dev, "ag_w_in0", deps=(mod,))
    W, Mo, Vo = lax.optimization_barrier((ag_in0["tok"], (W, Mo, Vo)))[1]
    (norm1_g, norm2_g, w_in, w_a_out, w_b_out, w_c_out, w_o, w_ffn_in, w_ffn_out, sgu_ln_g, sgu_ln_b, w_sgu, b_sgu,
     cfm_conv_b, cfm_ln_g, cfm_ln_b, final_g) = [W[k] for k in (
         "norm1_g", "norm2_g", "w_in", "w_a_out", "w_b_out", "w_c_out", "w_o", "w_ffn_in", "w_ffn_out", "sgu_ln_g",
         "sgu_ln_b", "w_sgu", "b_sgu", "cfm_conv_b", "cfm_ln_g", "cfm_ln_b", "final_g")]
    m_w_ada, v_w_ada = Mo["w_ada"], Vo["w_ada"]
    xl0, h0, ht0 = _norm_fwd(x0, None, _rows(jnp.zeros((D,), F32), norm1_g[0], mod[0, 1], mod[0, 0]), "norm1_fwd0",
                             deps=(ag_in0["tok"],))
    ag_rest0 = _gather_start(shards_of(0)[1:5], dev, "ag_rest0", deps=(h0,))
    ag_ffn0 = _gather_start(shards_of(0)[5:], dev, "ag_ffn0", deps=(ag_rest0["tok"],))

    tril = jnp.tril(jnp.ones((CHUNK, CHUNK), dtype=bool))

    def layer_consts(l):
        wt = jnp.where(tril[None], w_sgu[l], 0.0).astype(BF16)
        return dict(sgu_ln=_rows(sgu_ln_g[l], sgu_ln_b[l]), wtril=wt, wtril_t=jnp.swapaxes(wt, 1, 2),
                    bias_full=jnp.repeat(b_sgu[l].T, LANE, axis=1), cvec=_rows(cfm_conv_b[l], cfm_ln_g[l], cfm_ln_b[l]))

    def rest_of(g):
        return dict(w_a=g[0].reshape(1, D, D), w_b=g[1].reshape(1, D, D), w_c=g[2].reshape(1, D, D),
                    w_o=g[3].reshape(1, D, D), w_fi=jnp.transpose(g[4], (1, 0, 2)).reshape(1, D, F2),
                    w_fo=g[5].reshape(1, FF, D))

    sharded_small = ("w_short", "cfm_conv_w")

    def param_get(T):
        def get(name, l):
            if name == "final_g":
                return T[name]
            return None if name in sharded_small or name == "loss" else T[name][l]
        return get

    packs = [_pack(param_get(T), D) for T in (W, Mo, Vo)]
    ag_in0 = _gather_mid(ag_in0, [ag_ffn0["tok"], *packs], "ag_w_in0")
    (w_sgu, b_sgu, sgu_ln_g, sgu_ln_b, cfm_conv_b, cfm_ln_g, cfm_ln_b), conv_wmv_in = lax.optimization_barrier(
        (ag_in0["tok"], ((w_sgu, b_sgu, sgu_ln_g, sgu_ln_b, cfm_conv_b, cfm_ln_g, cfm_ln_b),
                         [(T["w_short"], T["cfm_conv_w"]) for T in (W, Mo, Vo)])))[1]
    consts = [layer_consts(l) for l in range(DEPTH)]
    ncr = DEPTH * (SHORT_K + CFM_K)
    padr = (-ncr) % 8
    convw_wmv = [jnp.pad(jnp.concatenate([a.reshape(-1, ncs), b.reshape(-1, ncs)]), ((0, padr), (0, 0)))
                 for a, b in conv_wmv_in]
    g_in0 = _gather_finish(ag_in0, [*convw_wmv] + [a for cl in consts for a in cl.values()], "ag_w_in0")
    w_short_full = jnp.transpose(g_in0[1], (1, 0, 2)).reshape(DEPTH, SHORT_K, D)
    cfm_w_full = jnp.transpose(g_in0[2], (1, 0, 2)).reshape(DEPTH, CFM_K, D)
    for l in range(DEPTH):
        consts[l]["wsh"] = jnp.pad(w_short_full[l], ((0, 8 - SHORT_K), (0, 0)))
        consts[l]["cw"] = jnp.pad(cfm_w_full[l], ((0, HALO - CFM_K), (0, 0)))
    Wg = [dict(w_in=g_in0[0]), None]
    ag_l1 = None
    nin = w_in.shape[2]
    tn_in = nin if nin % 256 == 0 and nin <= 1280 else 256
    tn_fi = 512 if F2 % 512 == 0 else 256
    tn_dw = min(256, D)

    saved = []
    xcur, gprev, ffn_tail = x0, None, None
    for l in range(DEPTH):
        sh1, sc1, g1, sh2, sc2, g2 = [mod[l, k] for k in range(N_MOD)]
        cl = consts[l]
        if l == 0:
            xl, h, ht = xl0, h0, ht0
        else:
            vec1 = _rows(gprev, norm1_g[l], sc1, sh1)
            act_prev, w_fo_prev = ffn_tail
            ag_l1 = _gather_mid(ag_l1, act_prev, f"ag_w{l}")
            f_prev, xl, h, ht = _mm_resid_norm(act_prev, w_fo_prev, xcur, vec1, tm, f"mm_ffn_out_norm1_{l}",
                                               deps=(ag_l1["tok"],))
            saved[l - 1]["f"] = f_prev
            g = _gather_finish(ag_l1, h, f"ag_w{l}")
            Wg[l] = dict(w_in=g[0], **rest_of(g[1:]))
        wl = Wg[l]
        z = _mm_nn(h, wl["w_in"], BF16, tm_huge, tn_in, D, f"mm_in{l}", w_outer=True)
        mix_deps, o_deps = (), ()
        if l == 0:
            ag_rest0 = _gather_mid(ag_rest0, z, "ag_rest0")
            mix_deps = (ag_rest0["tok"],)
            if DEPTH > 1:
                ag_l1 = _gather_start(shards_of(1), dev, "ag_w1")
                mix_deps += (ag_l1["tok"],)
        acts, acts_t, conv = _mixer_fwd(z, cl["wsh"], cl["sgu_ln"], cl["wtril"], cl["bias_full"], cl["cw"], cl["cvec"],
                                        f"mixer_fwd{l}", deps=mix_deps)
        if l == 0:
            g = _gather_finish(ag_rest0, acts[0], "ag_rest0")
            wl.update(w_a=g[0].reshape(1, D, D), w_b=g[1].reshape(1, D, D), w_c=g[2].reshape(1, D, D),
                      w_o=g[3].reshape(1, D, D))
            ag_ffn0 = _gather_mid(ag_ffn0, acts[0], "ag_ffn0")
            o_deps = (ag_ffn0["tok"],)
        merged, merged_t, ys = _branch_out(acts, [wl["w_a"][0], wl["w_b"][0], wl["w_c"][0]], z, f"branch_out{l}")
        o, x1, h2, h2t = _mm_resid_norm(merged, wl["w_o"], xl, _rows(g1, norm2_g[l], sc2, sh2), tm, f"mm_o_norm2_{l}",
                                        deps=o_deps)
        if l == 0:
            g = _gather_finish(ag_ffn0, h2, "ag_ffn0")
            wl.update(w_fi=jnp.transpose(g[0], (1, 0, 2)).reshape(1, D, F2), w_fo=g[1].reshape(1, FF, D))
        gu, act, act_t = _ffn_in_swiglu(h2, wl["w_fi"], tm_huge, 256, f"mm_ffn_in{l}")
        saved.append(dict(xl=xl, ht=ht, z=z, acts_t=acts_t, conv=conv, ys=ys, merged_t=merged_t, o=o, x1=x1, h2t=h2t, gu=gu,
                          act_t=act_t, f=None, consts=cl, mod=(sh1, sc1, g1, sh2, sc2, g2)))
        xcur, gprev, ffn_tail = x1, g2, (act, wl["w_fo"])

    last = saved[-1]
    dxup, dfb, fsums, loss_blk = _final_bwd(last["x1"], *ffn_tail, tgt, _rows(last["mod"][5], final_g), "final_bwd")
    loss_row = jnp.pad(loss_blk[0, 0:1], (0, D - 1))
    dgate2_next = fsums[1]
    small = [dict() for _ in range(DEPTH)]
    dmods = [None] * DEPTH
    nfi = w_ffn_in.shape[2]
    early_names, late_names = ["w_ffn_out", "w_ffn_in", "w_o"], ["w_a_out", "w_b_out", "w_c_out", "w_in"]
    results = {n: None for n in early_names + late_names}

    def adam_group(names, Ps, R2s, l, deps=()):
        for n, p, r2 in zip(names, Ps, R2s):
            results[n] = _adam_big(p, r2, my_chip, W[n], Mo[n], Vo[n], l, results[n], f"adam_{n}{l}", deps)

    deferred = []
    late_prev = None
    ag_s1, gathered1 = None, None
    tk_w = S
    tn_dw_in = tn_in
    for l in reversed(range(DEPTH)):
        sv, wl, cl = saved[l], Wg[l], saved[l]["consts"]
        sh1, sc1, g1, sh2, sc2, g2 = sv["mod"]
        dgu = _swiglu_bwd(dfb, wl["w_fo"], sv["gu"], f"mm_dact_swiglu_bwd{l}",
                          deps=() if late_prev is None else (late_prev["tok"], ag_s1["tok"]))
        g_fo = _mm_wgrad(sv["act_t"], dfb, 1, FF // 2, D, tk_w, f"mm_dw_ffn_out{l}")
        g_fi = _mm_wgrad(sv["h2t"], dgu, 1, D, tn_fi, S, f"mm_dw_ffn_in{l}")
        if late_prev is not None:
            deferred.append((late_names, *_scatter_finish(late_prev, g_fi, f"rs_late{l + 1}"), l + 1))
            late_prev = None
        dx1, dob, s2 = _norm_bwd(sv["x1"], (dgu, wl["w_fi"]), dxup, _rows(norm2_g[l], sc2, g1), sv["o"],
                                 f"mm_dh2_norm2_bwd{l}")
        dmerged = _mm_nt(dob, wl["w_o"], BF16, tm_big, D, D, f"mm_dmerged{l}")
        g_o = _mm_wgrad(sv["merged_t"], dob, 1, D, tn_dw, S, f"mm_dw_o{l}")
        early = _scatter_start([g_fo.reshape(NDEV, FF // NDEV, D),
                                jnp.transpose(g_fi.reshape(D, NDEV, nfi), (1, 0, 2)),
                                g_o.reshape(NDEV, D // NDEV, D)], f"rs_early{l}")
        dys, dz = _gate_bwd(dmerged, sv["z"], sv["ys"], f"gate_bwd{l}", deps=(early["tok"],))
        early = _scatter_mid(early, dys, my_c, f"rs_early{l}")
        abc_deps = (early["tok"],)
        if ag_s1 is not None:
            ag_s1 = _gather_mid(ag_s1, dys, "ag_small1")
            abc_deps += (ag_s1["tok"],)
        dacts = _mm3_nt(dys, [wl["w_a"], wl["w_b"], wl["w_c"]], tm_big, f"mm_dact_abc{l}", deps=abc_deps)
        g3 = _mm3_wgrad(sv["acts_t"], dys, tn_dw, f"mm_dw_abc{l}")
        g_abc = [g3[n] for n in range(3)]
        dz, mvec, dcw, dws, dbs = _mixer_bwd(sv["z"], dacts, sv["conv"], dz, cl["wsh"], cl["sgu_ln"], cl["wtril"],
                                             cl["wtril_t"], cl["bias_full"], cl["cw"], cl["cvec"], f"mixer_bwd{l}")
        if ag_s1 is not None:
            gathered1 = _gather_finish(ag_s1, dz, "ag_small1")[0]
            ag_s1 = None
        g_in = _mm_wgrad(sv["ht"], dz, NDEV, D, tn_dw_in, S, f"mm_dw_in{l}")
        late = _scatter_start([g.reshape(NDEV, D // NDEV, D) for g in g_abc] + [g_in], f"rs_late{l}")
        if l > 0:
            pv = saved[l - 1]
            dxup, dfb, s1 = _norm_bwd(sv["xl"], (dz, wl["w_in"]), dx1, _rows(norm1_g[l], sc1, pv["mod"][5]), pv["f"],
                                      f"mm_dh_norm1_bwd{l}", deps=(late["tok"],))
        else:
            dxup, dfb, s1 = _norm_bwd(sv["xl"], (dz, wl["w_in"]), dx1, _rows(norm1_g[l], sc1), None,
                                      f"mm_dh_norm1_bwd{l}", deps=(late["tok"],))
        deferred.append((early_names, *_scatter_finish(early, dxup, f"rs_early{l}"), l))
        dmods[l] = jnp.stack([s1[0], s1[1], s2[3], s2[0], s2[1], dgate2_next])
        dgate2_next = s1[3]
        small[l] = dict(norm1_g=s1[2], norm2_g=s2[2], sgu_ln_g=mvec[3], sgu_ln_b=mvec[4], cfm_conv_b=mvec[5],
                        cfm_ln_g=mvec[6], cfm_ln_b=mvec[7], b_sgu=dbs[:, :, 0],
                        w_sgu=jnp.where(tril[None], dws, 0.0), b_ada=dmods[l], w_short=mvec[0:SHORT_K],
                        cfm_conv_w=dcw[0:CFM_K])
        small_get = lambda name, k: {"final_g": fsums[0], "loss": loss_row}.get(name) if k is None else small[k][name]
        if l > 0:
            late_prev = _scatter_mid(late, dxup, my_c, f"rs_late{l}")
            ag_s1 = _gather_start([_pack(small_get, D, layers=(l,), tail=True)], dev, "ag_small1", deps=(late_prev["tok"],),
                                  within=(l * ROWS_PER_LAYER, PACK_ROWS))
    grad_x = dxup.reshape(x.shape)

    gathered = _all_gather([_pack(small_get, D, layers=(0,), tail=False)], "ag_small0", deps=(dxup,),
                           into=[(gathered1, 0)])[0]
    late_prev = _scatter_mid(late, gathered, my_c, "rs_late0")
    one_row = [n for n in order if n in SMALL_ROWS and SMALL_ROWS[n][1] == 1 and W[n].ndim == 2]
    (sg, sd, sm, sv_), singles = _adam_small(
        gathered, *packs, name="adam_small", deps=(late_prev["tok"],),
        single_rows=[tuple(l * ROWS_PER_LAYER + SMALL_ROWS[n][0] for l in range(DEPTH)) for n in one_row])
    loss = sg[FINAL_ROW + 1, 0]
    out = {n: tuple(singles[4 * i:4 * i + 4]) for i, n in enumerate(one_row)}
    for name in order:
        if name in SMALL_ROWS and name not in sharded_small and name not in out:
            out[name] = tuple(_unpack(p, name, W[name].shape) for p in (sg, sd, sm, sv_))
    out["final_g"] = tuple(p[FINAL_ROW] for p in (sg, sd, sm, sv_))

    def my_cols(name):
        full = _unpack(sg, name, (DEPTH, SMALL_ROWS[name][1], D))
        return lax.dynamic_slice_in_dim(full, dev * ncs, ncs, axis=2)

    gcs = jnp.concatenate([my_cols("w_short").reshape(-1, ncs), my_cols("cfm_conv_w").reshape(-1, ncs)])
    cd, cm, cv = _adam_plain(jnp.pad(gcs, ((0, padr), (0, 0))), *convw_wmv, "adam_convw")
    nsh = DEPTH * SHORT_K
    out["w_short"] = tuple(a[0:nsh].reshape(w_short.shape) for a in (gcs, cd, cm, cv))
    out["cfm_conv_w"] = tuple(a[nsh:ncr].reshape(cfm_conv_w.shape) for a in (gcs, cd, cm, cv))

    dm_all = jnp.stack([gathered[:, l * ROWS_PER_LAYER + 136:l * ROWS_PER_LAYER + 136 + N_MOD, :].reshape(NDEV, N_MOD * D)
                        for l in range(DEPTH)])
    dm_mine = lax.dynamic_slice_in_dim(dm_all, dev * ncol, ncol, axis=2)
    out["w_ada"] = tuple(_adam_ada(jnp.transpose(c_act), dm_mine, w_ada, m_w_ada, v_w_ada, "adam_ada"))

    for names, Ps, R2s, l in deferred:
        adam_group(names, Ps, R2s, l, deps=(late_prev["tok"],))
    adam_group(late_names, *_scatter_finish(late_prev, results["w_o"][0], "rs_late0"), 0)
    for n in early_names + late_names:
        out[n] = tuple(results[n])

    grads = [out[n][0] for n in order]
    deltas = [out[n][1] for n in order]
    new_m = [out[n][2] for n in order]
    new_v = [out[n][3] for n in order]
    return (loss, grad_x, *grads, *deltas, *new_m, *new_v)
```
